```python
import math
import jax, jax.numpy as jnp
from jax import lax
import numpy as np

D_MODEL = 1024
BATCH = 32
SEQ = 2048
DEPTH = 4

HEAD_DIM = 64
MLA_HEADS = 8
MLA_NOPE = 64
MLA_ROPE = 32
MLA_V = 64
MLA_Q_LORA = 384
MLA_KV_LORA = 256
ROPE_THETA = 10000.0
SWA_HEADS = 8
SWA_KV_HEADS = 2
SWA_WINDOW = 128
REL_BUCKETS = 32
REL_MAX_DIST = 128
FOX_HEADS = 16
D_FF = 4 * D_MODEL
D_PLE = 256
BLOCK_Q = 128
DN_ALPHA = (2 * DEPTH) ** 0.25
DN_BETA = (8 * DEPTH) ** -0.25
NORM_EPS = 1e-5
NEG_INF = -1e30
N_EVEN = (DEPTH + 1) // 2
N_ODD = DEPTH // 2
EVEN_SPLIT = (MLA_Q_LORA, MLA_KV_LORA, MLA_ROPE, SWA_HEADS * HEAD_DIM,
              SWA_KV_HEADS * HEAD_DIM, SWA_KV_HEADS * HEAD_DIM)
EVEN_IN = MLA_Q_LORA + MLA_KV_LORA + MLA_ROPE + (SWA_HEADS + 2 * SWA_KV_HEADS) * HEAD_DIM
EVEN_MIX = MLA_HEADS * MLA_V + SWA_HEADS * HEAD_DIM
ODD_SPLIT = (FOX_HEADS * HEAD_DIM, FOX_HEADS * HEAD_DIM, FOX_HEADS * HEAD_DIM, FOX_HEADS)
ODD_IN = 3 * FOX_HEADS * HEAD_DIM + FOX_HEADS
ODD_MIX = FOX_HEADS * HEAD_DIM

kernel_name = "hybrid_mla_swa_fox_deepnorm"


def _split(h, sizes):
    out, o = [], 0
    for n in sizes:
        out.append(h[..., o:o + n])
        o += n
    return out


def _layer_norm(x, g, b):
    xf = x.astype(jnp.float32)
    mu = jnp.mean(xf, -1, keepdims=True)
    var = jnp.mean(jnp.square(xf - mu), -1, keepdims=True)
    y = (xf - mu) * lax.rsqrt(var + NORM_EPS)
    return (y * g.astype(jnp.float32) + b.astype(jnp.float32)).astype(x.dtype)


def _rms_norm(x, g):
    xf = x.astype(jnp.float32)
    y = xf * lax.rsqrt(jnp.mean(jnp.square(xf), -1, keepdims=True) + NORM_EPS)
    return (y * g.astype(jnp.float32)).astype(x.dtype)


def _rope_tables(seq_len, dim):
    inv = 1.0 / (ROPE_THETA ** (jnp.arange(0, dim, 2, dtype=jnp.float32) / dim))
    ang = jnp.arange(seq_len, dtype=jnp.float32)[:, None] * inv[None, :]
    return jnp.cos(ang), jnp.sin(ang)


def _apply_rope(x, cos, sin):
    x1, x2 = jnp.split(x.astype(jnp.float32), 2, axis=-1)
    c = cos[:, None, :]
    s = sin[:, None, :]
    return jnp.concatenate([x1 * c - x2 * s, x2 * c + x1 * s], -1).astype(x.dtype)


def _t5_bucket(dist):
    exact = REL_BUCKETS // 2
    d = jnp.maximum(dist, 1).astype(jnp.float32)
    large = exact + (jnp.log(d / exact) / math.log(REL_MAX_DIST / exact)
                     * (REL_BUCKETS - exact)).astype(jnp.int32)
    large = jnp.minimum(large, REL_BUCKETS - 1)
    return jnp.where(dist < exact, dist, large)


def _mla_attend(q_nope, q_rope, k_nope, k_rope, v):
    B, S, H, _ = q_nope.shape
    nb = S // BLOCK_Q
    scale = (MLA_NOPE + MLA_ROPE) ** -0.5
    qn = q_nope.reshape(B, nb, BLOCK_Q, H, MLA_NOPE).transpose(1, 0, 2, 3, 4)
    qr = q_rope.reshape(B, nb, BLOCK_Q, H, MLA_ROPE).transpose(1, 0, 2, 3, 4)
    kpos = jnp.arange(S)

    def block(args):
        i, qn_b, qr_b = args
        s = (jnp.einsum('bqhd,bkhd->bhqk', qn_b, k_nope, preferred_element_type=jnp.float32)
             + jnp.einsum('bqhd,bkd->bhqk', qr_b, k_rope, preferred_element_type=jnp.float32)) * scale
        qpos = i * BLOCK_Q + jnp.arange(BLOCK_Q)
        s = jnp.where(kpos[None, :] <= qpos[:, None], s, NEG_INF)
        w = jax.nn.softmax(s, axis=-1).astype(v.dtype)
        return jnp.einsum('bhqk,bkhd->bqhd', w, v)

    out = lax.map(block, (jnp.arange(nb), qn, qr))
    return out.transpose(1, 0, 2, 3, 4).reshape(B, S, H * MLA_V)


def _swa_attend(q, k, v, sinks, rel_bias):
    B, S, H, d = q.shape
    KVH = k.shape[2]
    G = H // KVH
    nb = S // BLOCK_Q
    qb = q.reshape(B, nb, BLOCK_Q, KVH, G, d)

    def band(t):
        tb = t.reshape(B, nb, BLOCK_Q, KVH, d)
        prev = jnp.pad(tb, ((0, 0), (1, 0), (0, 0), (0, 0), (0, 0)))[:, :-1]
        return jnp.concatenate([prev, tb], axis=2)

    kb, vb = band(k), band(v)
    s = jnp.einsum('bnqkgd,bnskd->bnkgqs', qb, kb, preferred_element_type=jnp.float32) * (d ** -0.5)
    a = jnp.arange(BLOCK_Q)[:, None]
    col = jnp.arange(2 * BLOCK_Q)[None, :]
    dist = a + BLOCK_Q - col
    in_win = (dist >= 0) & (dist < SWA_WINDOW)
    pad = (jnp.arange(nb)[:, None, None] == 0) & (col < BLOCK_Q)[None]
    valid = in_win[None] & ~pad
    bias = rel_bias[_t5_bucket(jnp.maximum(dist, 0))].astype(jnp.float32)
    bias = bias.transpose(2, 0, 1).reshape(KVH, G, BLOCK_Q, 2 * BLOCK_Q)
    s = jnp.where(valid[None, :, None, None], s + bias, NEG_INF)
    sink = jnp.broadcast_to(sinks.astype(jnp.float32).reshape(1, 1, KVH, G, 1, 1), s.shape[:-1] + (1,))
    w = jax.nn.softmax(jnp.concatenate([s, sink], axis=-1), axis=-1)[..., :-1].astype(v.dtype)
    out = jnp.einsum('bnkgqs,bnskd->bnqkgd', w, vb)
    return out.reshape(B, S, H * d)


def _fox_attend(q, k, v, log_f):
    B, S, H, d = q.shape
    nb = S // BLOCK_Q
    c = jnp.cumsum(log_f, axis=1)
    cq = c.reshape(B, nb, BLOCK_Q, H).transpose(1, 0, 3, 2)
    ck = c.transpose(0, 2, 1)
    qb = q.reshape(B, nb, BLOCK_Q, H, d).transpose(1, 0, 2, 3, 4)
    kpos = jnp.arange(S)

    def block(args):
        i, q_b, cq_b = args
        s = jnp.einsum('bqhd,bkhd->bhqk', q_b, k, preferred_element_type=jnp.float32) * (d ** -0.5)
        s = s + cq_b[..., :, None] - ck[:, :, None, :]
        qpos = i * BLOCK_Q + jnp.arange(BLOCK_Q)
        s = jnp.where(kpos[None, :] <= qpos[:, None], s, NEG_INF)
        w = jax.nn.softmax(s, axis=-1).astype(v.dtype)
        return jnp.einsum('bhqk,bkhd->bqhd', w, v)

    out = lax.map(block, (jnp.arange(nb), qb, cq))
    return out.transpose(1, 0, 2, 3, 4).reshape(B, S, H * d)


def _even_mixer(x, w_in, q_norm, w_uq, kv_norm, w_ukv, sinks, w_out, rel_bias, cos, sin):
    B, S, _ = x.shape
    h = x @ w_in
    c_q, c_kv, k_rope, q_s, k_s, v_s = _split(h, EVEN_SPLIT)
    q = (_rms_norm(c_q, q_norm) @ w_uq).reshape(B, S, MLA_HEADS, MLA_NOPE + MLA_ROPE)
    q_nope = q[..., :MLA_NOPE]
    q_rope = _apply_rope(q[..., MLA_NOPE:], cos, sin)
    kv = (_rms_norm(c_kv, kv_norm) @ w_ukv).reshape(B, S, MLA_HEADS, MLA_NOPE + MLA_V)
    k_nope, v = kv[..., :MLA_NOPE], kv[..., MLA_NOPE:]
    k_rope = _apply_rope(k_rope[:, :, None, :], cos, sin)[:, :, 0]
    o_mla = _mla_attend(q_nope, q_rope, k_nope, k_rope, v)
    o_swa = _swa_attend(q_s.reshape(B, S, SWA_HEADS, HEAD_DIM),
                        k_s.reshape(B, S, SWA_KV_HEADS, HEAD_DIM),
                        v_s.reshape(B, S, SWA_KV_HEADS, HEAD_DIM), sinks, rel_bias)
    return jnp.concatenate([o_mla, o_swa], axis=-1) @ w_out


def _odd_mixer(x, w_in, b_f, w_out):
    B, S, _ = x.shape
    q, k, v, f = _split(x @ w_in, ODD_SPLIT)
    log_f = jax.nn.log_sigmoid((f + b_f).astype(jnp.float32))
    o = _fox_attend(q.reshape(B, S, FOX_HEADS, HEAD_DIM), k.reshape(B, S, FOX_HEADS, HEAD_DIM),
                    v.reshape(B, S, FOX_HEADS, HEAD_DIM), log_f)
    return o @ w_out


def _sq_relu_mlp(x, w_up, w_down):
    return jnp.square(jax.nn.relu(x @ w_up)) @ w_down


def _fwd_setup_inputs(seed: int = 0) -> dict:
    key = jax.random.key(seed)
    ks = jax.random.split(key, 24)
    nrm = jax.random.normal
    f32 = jnp.float32
    return {
        "x": nrm(ks[0], (BATCH, SEQ, D_MODEL), f32),
        "p": nrm(ks[1], (DEPTH, BATCH, SEQ, D_PLE), f32),
        "rel_bias": 0.5 * nrm(ks[2], (REL_BUCKETS, SWA_HEADS), f32),
        "ev_w_in": nrm(ks[3], (N_EVEN, D_MODEL, EVEN_IN), f32) * D_MODEL ** -0.5,
        "ev_q_norm": 1.0 + 0.02 * nrm(ks[4], (N_EVEN, MLA_Q_LORA), f32),
        "ev_w_uq": nrm(ks[5], (N_EVEN, MLA_Q_LORA, MLA_HEADS * (MLA_NOPE + MLA_ROPE)), f32) * MLA_Q_LORA ** -0.5,
        "ev_kv_norm": 1.0 + 0.02 * nrm(ks[6], (N_EVEN, MLA_KV_LORA), f32),
        "ev_w_ukv": nrm(ks[7], (N_EVEN, MLA_KV_LORA, MLA_HEADS * (MLA_NOPE + MLA_V)), f32) * MLA_KV_LORA ** -0.5,
        "ev_sinks": 0.5 * nrm(ks[8], (N_EVEN, SWA_HEADS), f32),
        "ev_w_out": nrm(ks[9], (N_EVEN, EVEN_MIX, D_MODEL), f32) * (EVEN_MIX ** -0.5 * DN_BETA),
        "od_w_in": nrm(ks[10], (N_ODD, D_MODEL, ODD_IN), f32) * D_MODEL ** -0.5,
        "od_b_f": jax.random.uniform(ks[11], (N_ODD, FOX_HEADS), f32, 1.0, 4.0),
        "od_w_out": nrm(ks[12], (N_ODD, ODD_MIX, D_MODEL), f32) * (ODD_MIX ** -0.5 * DN_BETA),
        "ln1_g": 1.0 + 0.02 * nrm(ks[13], (DEPTH, D_MODEL), f32),
        "ln1_b": 0.02 * nrm(ks[14], (DEPTH, D_MODEL), f32),
        "w_up": nrm(ks[15], (DEPTH, D_MODEL, D_FF), f32) * D_MODEL ** -0.5,
        "w_down": nrm(ks[16], (DEPTH, D_FF, D_MODEL), f32) * (D_FF ** -0.5 * DN_BETA),
        "ln2_g": 1.0 + 0.02 * nrm(ks[17], (DEPTH, D_MODEL), f32),
        "ln2_b": 0.02 * nrm(ks[18], (DEPTH, D_MODEL), f32),
        "ple_w_proj": nrm(ks[19], (DEPTH, D_PLE, D_MODEL), f32) * D_PLE ** -0.5,
        "ple_w_gate": nrm(ks[20], (DEPTH, D_MODEL, D_MODEL), f32) * D_MODEL ** -0.5,
        "ple_b_gate": 0.02 * nrm(ks[21], (DEPTH, D_MODEL), f32),
    }


def _fwd_reference(x, p, rel_bias, ev_w_in, ev_q_norm, ev_w_uq, ev_kv_norm, ev_w_ukv, ev_sinks, ev_w_out,
              od_w_in, od_b_f, od_w_out, ln1_g, ln1_b, w_up, w_down, ln2_g, ln2_b,
              ple_w_proj, ple_w_gate, ple_b_gate):
    S = x.shape[1]
    cos, sin = _rope_tables(S, MLA_ROPE)
    for i in range(DEPTH):
        j = i // 2
        if i % 2 == 0:
            m = _even_mixer(x, ev_w_in[j], ev_q_norm[j], ev_w_uq[j], ev_kv_norm[j], ev_w_ukv[j],
                            ev_sinks[j], ev_w_out[j], rel_bias, cos, sin)
        else:
            m = _odd_mixer(x, od_w_in[j], od_b_f[j], od_w_out[j])
        x = _layer_norm(DN_ALPHA * x + m, ln1_g[i], ln1_b[i])
        x = _layer_norm(DN_ALPHA * x + _sq_relu_mlp(x, w_up[i], w_down[i]), ln2_g[i], ln2_b[i])
        gate = jax.nn.sigmoid(x @ ple_w_gate[i] + ple_b_gate[i])
        x = x + gate * (p[i] @ ple_w_proj[i])
    return x


import jax as _jax
import jax.numpy as _jnp

TWIN_FORMAT = 'train_step'
FWD_PARAMS = ['x', 'p', 'rel_bias', 'ev_w_in', 'ev_q_norm', 'ev_w_uq', 'ev_kv_norm', 'ev_w_ukv', 'ev_sinks', 'ev_w_out', 'od_w_in', 'od_b_f', 'od_w_out', 'ln1_g', 'ln1_b', 'w_up', 'w_down', 'ln2_g', 'ln2_b', 'ple_w_proj', 'ple_w_gate', 'ple_b_gate']
TWIN_WEIGHTS = ['rel_bias', 'ev_w_in', 'ev_q_norm', 'ev_w_uq', 'ev_kv_norm', 'ev_w_ukv', 'ev_sinks', 'ev_w_out', 'od_w_in', 'od_b_f', 'od_w_out', 'ln1_g', 'ln1_b', 'w_up', 'w_down', 'ln2_g', 'ln2_b', 'ple_w_proj', 'ple_w_gate', 'ple_b_gate']
TWIN_DIFF_INPUT = 'x'
TWIN_INPUTS = ['x', 'p', 'rel_bias', 'ev_w_in', 'ev_q_norm', 'ev_w_uq', 'ev_kv_norm', 'ev_w_ukv', 'ev_sinks', 'ev_w_out', 'od_w_in', 'od_b_f', 'od_w_out', 'ln1_g', 'ln1_b', 'w_up', 'w_down', 'ln2_g', 'ln2_b', 'ple_w_proj', 'ple_w_gate', 'ple_b_gate', 'loss_target', 'm_rel_bias', 'm_ev_w_in', 'm_ev_q_norm', 'm_ev_w_uq', 'm_ev_kv_norm', 'm_ev_w_ukv', 'm_ev_sinks', 'm_ev_w_out', 'm_od_w_in', 'm_od_b_f', 'm_od_w_out', 'm_ln1_g', 'm_ln1_b', 'm_w_up', 'm_w_down', 'm_ln2_g', 'm_ln2_b', 'm_ple_w_proj', 'm_ple_w_gate', 'm_ple_b_gate', 'v_rel_bias', 'v_ev_w_in', 'v_ev_q_norm', 'v_ev_w_uq', 'v_ev_kv_norm', 'v_ev_w_ukv', 'v_ev_sinks', 'v_ev_w_out', 'v_od_w_in', 'v_od_b_f', 'v_od_w_out', 'v_ln1_g', 'v_ln1_b', 'v_w_up', 'v_w_down', 'v_ln2_g', 'v_ln2_b', 'v_ple_w_proj', 'v_ple_w_gate', 'v_ple_b_gate']
TWIN_OUTPUTS = ['loss', 'grad_x', 'grad_rel_bias', 'grad_ev_w_in', 'grad_ev_q_norm', 'grad_ev_w_uq', 'grad_ev_kv_norm', 'grad_ev_w_ukv', 'grad_ev_sinks', 'grad_ev_w_out', 'grad_od_w_in', 'grad_od_b_f', 'grad_od_w_out', 'grad_ln1_g', 'grad_ln1_b', 'grad_w_up', 'grad_w_down', 'grad_ln2_g', 'grad_ln2_b', 'grad_ple_w_proj', 'grad_ple_w_gate', 'grad_ple_b_gate', 'delta_rel_bias', 'delta_ev_w_in', 'delta_ev_q_norm', 'delta_ev_w_uq', 'delta_ev_kv_norm', 'delta_ev_w_ukv', 'delta_ev_sinks', 'delta_ev_w_out', 'delta_od_w_in', 'delta_od_b_f', 'delta_od_w_out', 'delta_ln1_g', 'delta_ln1_b', 'delta_w_up', 'delta_w_down', 'delta_ln2_g', 'delta_ln2_b', 'delta_ple_w_proj', 'delta_ple_w_gate', 'delta_ple_b_gate', 'new_m_rel_bias', 'new_m_ev_w_in', 'new_m_ev_q_norm', 'new_m_ev_w_uq', 'new_m_ev_kv_norm', 'new_m_ev_w_ukv', 'new_m_ev_sinks', 'new_m_ev_w_out', 'new_m_od_w_in', 'new_m_od_b_f', 'new_m_od_w_out', 'new_m_ln1_g', 'new_m_ln1_b', 'new_m_w_up', 'new_m_w_down', 'new_m_ln2_g', 'new_m_ln2_b', 'new_m_ple_w_proj', 'new_m_ple_w_gate', 'new_m_ple_b_gate', 'new_v_rel_bias', 'new_v_ev_w_in', 'new_v_ev_q_norm', 'new_v_ev_w_uq', 'new_v_ev_kv_norm', 'new_v_ev_w_ukv', 'new_v_ev_sinks', 'new_v_ev_w_out', 'new_v_od_w_in', 'new_v_od_b_f', 'new_v_od_w_out', 'new_v_ln1_g', 'new_v_ln1_b', 'new_v_w_up', 'new_v_w_down', 'new_v_ln2_g', 'new_v_ln2_b', 'new_v_ple_w_proj', 'new_v_ple_w_gate', 'new_v_ple_b_gate']
TWIN_LEAF_KINDS = {'loss': 'loss', 'grad_x': 'grad_x', 'grad_rel_bias': 'grad_w', 'grad_ev_w_in': 'grad_w', 'grad_ev_q_norm': 'grad_w', 'grad_ev_w_uq': 'grad_w', 'grad_ev_kv_norm': 'grad_w', 'grad_ev_w_ukv': 'grad_w', 'grad_ev_sinks': 'grad_w', 'grad_ev_w_out': 'grad_w', 'grad_od_w_in': 'grad_w', 'grad_od_b_f': 'grad_w', 'grad_od_w_out': 'grad_w', 'grad_ln1_g': 'grad_w', 'grad_ln1_b': 'grad_w', 'grad_w_up': 'grad_w', 'grad_w_down': 'grad_w', 'grad_ln2_g': 'grad_w', 'grad_ln2_b': 'grad_w', 'grad_ple_w_proj': 'grad_w', 'grad_ple_w_gate': 'grad_w', 'grad_ple_b_gate': 'grad_w', 'delta_rel_bias': 'delta_w', 'delta_ev_w_in': 'delta_w', 'delta_ev_q_norm': 'delta_w', 'delta_ev_w_uq': 'delta_w', 'delta_ev_kv_norm': 'delta_w', 'delta_ev_w_ukv': 'delta_w', 'delta_ev_sinks': 'delta_w', 'delta_ev_w_out': 'delta_w', 'delta_od_w_in': 'delta_w', 'delta_od_b_f': 'delta_w', 'delta_od_w_out': 'delta_w', 'delta_ln1_g': 'delta_w', 'delta_ln1_b': 'delta_w', 'delta_w_up': 'delta_w', 'delta_w_down': 'delta_w', 'delta_ln2_g': 'delta_w', 'delta_ln2_b': 'delta_w', 'delta_ple_w_proj': 'delta_w', 'delta_ple_w_gate': 'delta_w', 'delta_ple_b_gate': 'delta_w', 'new_m_rel_bias': 'new_m', 'new_m_ev_w_in': 'new_m', 'new_m_ev_q_norm': 'new_m', 'new_m_ev_w_uq': 'new_m', 'new_m_ev_kv_norm': 'new_m', 'new_m_ev_w_ukv': 'new_m', 'new_m_ev_sinks': 'new_m', 'new_m_ev_w_out': 'new_m', 'new_m_od_w_in': 'new_m', 'new_m_od_b_f': 'new_m', 'new_m_od_w_out': 'new_m', 'new_m_ln1_g': 'new_m', 'new_m_ln1_b': 'new_m', 'new_m_w_up': 'new_m', 'new_m_w_down': 'new_m', 'new_m_ln2_g': 'new_m', 'new_m_ln2_b': 'new_m', 'new_m_ple_w_proj': 'new_m', 'new_m_ple_w_gate': 'new_m', 'new_m_ple_b_gate': 'new_m', 'new_v_rel_bias': 'new_v', 'new_v_ev_w_in': 'new_v', 'new_v_ev_q_norm': 'new_v', 'new_v_ev_w_uq': 'new_v', 'new_v_ev_kv_norm': 'new_v', 'new_v_ev_w_ukv': 'new_v', 'new_v_ev_sinks': 'new_v', 'new_v_ev_w_out': 'new_v', 'new_v_od_w_in': 'new_v', 'new_v_od_b_f': 'new_v', 'new_v_od_w_out': 'new_v', 'new_v_ln1_g': 'new_v', 'new_v_ln1_b': 'new_v', 'new_v_w_up': 'new_v', 'new_v_w_down': 'new_v', 'new_v_ln2_g': 'new_v', 'new_v_ln2_b': 'new_v', 'new_v_ple_w_proj': 'new_v', 'new_v_ple_w_gate': 'new_v', 'new_v_ple_b_gate': 'new_v'}


def _forward(args):
    return _fwd_reference(*[args[k] for k in FWD_PARAMS])


def _output_shape():
    out = _jax.eval_shape(lambda: _forward(_fwd_setup_inputs(0)))
    return out.shape, out.dtype

N_MICROBATCH = 1
ADAM_LR = 0.001
ADAM_B1 = 0.9
ADAM_B2 = 0.999
ADAM_EPS = 1e-08
ADAM_WD = 0.01
ADAM_STEP = 10
PER_EXAMPLE_BATCH_AXIS = {'x': 0, 'p': 1, 'loss_target': 0}
SHARED_INPUTS = []
_WEIGHT_DTYPES = {'rel_bias': _jnp.float32, 'ev_w_in': _jnp.float32, 'ev_q_norm': _jnp.float32, 'ev_w_uq': _jnp.float32, 'ev_kv_norm': _jnp.float32, 'ev_w_ukv': _jnp.float32, 'ev_sinks': _jnp.float32, 'ev_w_out': _jnp.float32, 'od_w_in': _jnp.float32, 'od_b_f': _jnp.float32, 'od_w_out': _jnp.float32, 'ln1_g': _jnp.float32, 'ln1_b': _jnp.float32, 'w_up': _jnp.float32, 'w_down': _jnp.float32, 'ln2_g': _jnp.float32, 'ln2_b': _jnp.float32, 'ple_w_proj': _jnp.float32, 'ple_w_gate': _jnp.float32, 'ple_b_gate': _jnp.float32}
MOMENT_SCALE = {'rel_bias': 2.619536e-02, 'ev_w_in': 6.883095e-02, 'ev_q_norm': 1.287671e-02, 'ev_w_uq': 9.160329e-03, 'ev_kv_norm': 9.600138e-02, 'ev_w_ukv': 5.386883e-02, 'ev_sinks': 8.342936e-03, 'ev_w_out': 2.346599e-01, 'od_w_in': 7.007827e-02, 'od_b_f': 2.020394e-01, 'od_w_out': 3.244225e-01, 'ln1_g': 1.268419e+00, 'ln1_b': 5.091652e+00, 'w_up': 5.681328e-02, 'w_down': 1.236493e+00, 'ln2_g': 3.290317e+01, 'ln2_b': 8.478894e+00, 'ple_w_proj': 3.321291e-01, 'ple_w_gate': 5.420741e-01, 'ple_b_gate': 3.250671e+00}


def _to_microbatches(a, axis):
    t = _jnp.moveaxis(a, axis, 0)
    t = t.reshape((N_MICROBATCH, t.shape[0] // N_MICROBATCH) + t.shape[1:])
    return _jnp.moveaxis(t, 1, axis + 1)


def setup_inputs(seed: int = 0) -> dict:
    inp = _fwd_setup_inputs(seed)
    key = _jax.random.fold_in(_jax.random.key(seed), 7919)
    shape, _ = _output_shape()
    out = dict(inp)
    out["loss_target"] = _jax.random.normal(_jax.random.fold_in(key, 0), shape, _jnp.float32)
    for i, name in enumerate(TWIN_WEIGHTS):
        w = inp[name].astype(_jnp.float32)
        if MOMENT_SCALE is None:
            s = _jnp.sqrt(_jnp.mean(_jnp.square(w)) + 1e-30)
        else:
            s = MOMENT_SCALE[name]
        km, kv = _jax.random.split(_jax.random.fold_in(key, i + 1))
        out[name] = w
        out["m_" + name] = s * _jax.random.normal(km, w.shape, _jnp.float32)
        out["v_" + name] = (s * s) * _jax.random.uniform(kv, w.shape, _jnp.float32, 0.5, 1.5)
    if N_MICROBATCH > 1:
        for name, axis in PER_EXAMPLE_BATCH_AXIS.items():
            out[name] = _to_microbatches(out[name], axis)
    return {'x': out['x'], 'p': out['p'], 'rel_bias': out['rel_bias'], 'ev_w_in': out['ev_w_in'], 'ev_q_norm': out['ev_q_norm'], 'ev_w_uq': out['ev_w_uq'], 'ev_kv_norm': out['ev_kv_norm'], 'ev_w_ukv': out['ev_w_ukv'], 'ev_sinks': out['ev_sinks'], 'ev_w_out': out['ev_w_out'], 'od_w_in': out['od_w_in'], 'od_b_f': out['od_b_f'], 'od_w_out': out['od_w_out'], 'ln1_g': out['ln1_g'], 'ln1_b': out['ln1_b'], 'w_up': out['w_up'], 'w_down': out['w_down'], 'ln2_g': out['ln2_g'], 'ln2_b': out['ln2_b'], 'ple_w_proj': out['ple_w_proj'], 'ple_w_gate': out['ple_w_gate'], 'ple_b_gate': out['ple_b_gate'], 'loss_target': out['loss_target'], 'm_rel_bias': out['m_rel_bias'], 'm_ev_w_in': out['m_ev_w_in'], 'm_ev_q_norm': out['m_ev_q_norm'], 'm_ev_w_uq': out['m_ev_w_uq'], 'm_ev_kv_norm': out['m_ev_kv_norm'], 'm_ev_w_ukv': out['m_ev_w_ukv'], 'm_ev_sinks': out['m_ev_sinks'], 'm_ev_w_out': out['m_ev_w_out'], 'm_od_w_in': out['m_od_w_in'], 'm_od_b_f': out['m_od_b_f'], 'm_od_w_out': out['m_od_w_out'], 'm_ln1_g': out['m_ln1_g'], 'm_ln1_b': out['m_ln1_b'], 'm_w_up': out['m_w_up'], 'm_w_down': out['m_w_down'], 'm_ln2_g': out['m_ln2_g'], 'm_ln2_b': out['m_ln2_b'], 'm_ple_w_proj': out['m_ple_w_proj'], 'm_ple_w_gate': out['m_ple_w_gate'], 'm_ple_b_gate': out['m_ple_b_gate'], 'v_rel_bias': out['v_rel_bias'], 'v_ev_w_in': out['v_ev_w_in'], 'v_ev_q_norm': out['v_ev_q_norm'], 'v_ev_w_uq': out['v_ev_w_uq'], 'v_ev_kv_norm': out['v_ev_kv_norm'], 'v_ev_w_ukv': out['v_ev_w_ukv'], 'v_ev_sinks': out['v_ev_sinks'], 'v_ev_w_out': out['v_ev_w_out'], 'v_od_w_in': out['v_od_w_in'], 'v_od_b_f': out['v_od_b_f'], 'v_od_w_out': out['v_od_w_out'], 'v_ln1_g': out['v_ln1_g'], 'v_ln1_b': out['v_ln1_b'], 'v_w_up': out['v_w_up'], 'v_w_down': out['v_w_down'], 'v_ln2_g': out['v_ln2_g'], 'v_ln2_b': out['v_ln2_b'], 'v_ple_w_proj': out['v_ple_w_proj'], 'v_ple_w_gate': out['v_ple_w_gate'], 'v_ple_b_gate': out['v_ple_b_gate']}


def _loss(weights, diff, rest, loss_target):
    with _jax.named_scope("forward"):
        args = {**rest, TWIN_DIFF_INPUT: diff, **{k: w.astype(_WEIGHT_DTYPES[k]) for k, w in weights.items()}}
        y = _forward(args)
    with _jax.named_scope("loss_head"):
        err = _jnp.square(y.astype(_jnp.float32) - loss_target)
        return 0.5 * _jnp.sum(_jnp.mean(err, axis=-1)) if err.ndim else 0.5 * err


def _adamw(w, g, m, v):
    m = ADAM_B1 * m + (1.0 - ADAM_B1) * g
    v = ADAM_B2 * v + (1.0 - ADAM_B2) * _jnp.square(g)
    m_hat = m / (1.0 - ADAM_B1 ** ADAM_STEP)
    v_hat = v / (1.0 - ADAM_B2 ** ADAM_STEP)
    delta = -ADAM_LR * (m_hat / (_jnp.sqrt(v_hat) + ADAM_EPS) + ADAM_WD * w)
    return delta, m, v


def reference(x, p, rel_bias, ev_w_in, ev_q_norm, ev_w_uq, ev_kv_norm, ev_w_ukv, ev_sinks, ev_w_out, od_w_in, od_b_f, od_w_out, ln1_g, ln1_b, w_up, w_down, ln2_g, ln2_b, ple_w_proj, ple_w_gate, ple_b_gate, loss_target, m_rel_bias, m_ev_w_in, m_ev_q_norm, m_ev_w_uq, m_ev_kv_norm, m_ev_w_ukv, m_ev_sinks, m_ev_w_out, m_od_w_in, m_od_b_f, m_od_w_out, m_ln1_g, m_ln1_b, m_w_up, m_w_down, m_ln2_g, m_ln2_b, m_ple_w_proj, m_ple_w_gate, m_ple_b_gate, v_rel_bias, v_ev_w_in, v_ev_q_norm, v_ev_w_uq, v_ev_kv_norm, v_ev_w_ukv, v_ev_sinks, v_ev_w_out, v_od_w_in, v_od_b_f, v_od_w_out, v_ln1_g, v_ln1_b, v_w_up, v_w_down, v_ln2_g, v_ln2_b, v_ple_w_proj, v_ple_w_gate, v_ple_b_gate):
    given = dict(x=x, p=p, rel_bias=rel_bias, ev_w_in=ev_w_in, ev_q_norm=ev_q_norm, ev_w_uq=ev_w_uq, ev_kv_norm=ev_kv_norm, ev_w_ukv=ev_w_ukv, ev_sinks=ev_sinks, ev_w_out=ev_w_out, od_w_in=od_w_in, od_b_f=od_b_f, od_w_out=od_w_out, ln1_g=ln1_g, ln1_b=ln1_b, w_up=w_up, w_down=w_down, ln2_g=ln2_g, ln2_b=ln2_b, ple_w_proj=ple_w_proj, ple_w_gate=ple_w_gate, ple_b_gate=ple_b_gate, loss_target=loss_target, m_rel_bias=m_rel_bias, m_ev_w_in=m_ev_w_in, m_ev_q_norm=m_ev_q_norm, m_ev_w_uq=m_ev_w_uq, m_ev_kv_norm=m_ev_kv_norm, m_ev_w_ukv=m_ev_w_ukv, m_ev_sinks=m_ev_sinks, m_ev_w_out=m_ev_w_out, m_od_w_in=m_od_w_in, m_od_b_f=m_od_b_f, m_od_w_out=m_od_w_out, m_ln1_g=m_ln1_g, m_ln1_b=m_ln1_b, m_w_up=m_w_up, m_w_down=m_w_down, m_ln2_g=m_ln2_g, m_ln2_b=m_ln2_b, m_ple_w_proj=m_ple_w_proj, m_ple_w_gate=m_ple_w_gate, m_ple_b_gate=m_ple_b_gate, v_rel_bias=v_rel_bias, v_ev_w_in=v_ev_w_in, v_ev_q_norm=v_ev_q_norm, v_ev_w_uq=v_ev_w_uq, v_ev_kv_norm=v_ev_kv_norm, v_ev_w_ukv=v_ev_w_ukv, v_ev_sinks=v_ev_sinks, v_ev_w_out=v_ev_w_out, v_od_w_in=v_od_w_in, v_od_b_f=v_od_b_f, v_od_w_out=v_od_w_out, v_ln1_g=v_ln1_g, v_ln1_b=v_ln1_b, v_w_up=v_w_up, v_w_down=v_w_down, v_ln2_g=v_ln2_g, v_ln2_b=v_ln2_b, v_ple_w_proj=v_ple_w_proj, v_ple_w_gate=v_ple_w_gate, v_ple_b_gate=v_ple_b_gate)
    weights = {n: given[n] for n in TWIN_WEIGHTS}
    shared = {n: given[n] for n in SHARED_INPUTS}
    per_example = {n: given[n] for n in ['x', 'p']}
    grad_fn = _jax.value_and_grad(_loss, argnums=(0, 1))

    def one_microbatch(ex, loss_target):
        ex = dict(ex)
        diff = ex.pop(TWIN_DIFF_INPUT)
        return grad_fn(weights, diff, {**shared, **ex}, loss_target)

    if N_MICROBATCH == 1:
        loss, (grad_w, grad_x) = one_microbatch(per_example, given["loss_target"])
    else:
        def body(carry, xs):
            loss_sum, grad_sum = carry
            l_k, (gw_k, gx_k) = one_microbatch(xs[0], xs[1])
            with _jax.named_scope("update"):
                return (loss_sum + l_k, _jax.tree.map(_jnp.add, grad_sum, gw_k)), gx_k

        init = (_jnp.zeros((), _jnp.float32), _jax.tree.map(_jnp.zeros_like, weights))
        (loss, grad_w), grad_x = _jax.lax.scan(body, init, (per_example, given["loss_target"]))
    with _jax.named_scope("update"):
        delta_w, new_m, new_v = {}, {}, {}
        for n in TWIN_WEIGHTS:
            delta_w[n], new_m[n], new_v[n] = _adamw(weights[n], grad_w[n], given["m_" + n], given["v_" + n])
    return (loss, grad_x, *[grad_w[n] for n in TWIN_WEIGHTS], *[delta_w[n] for n in TWIN_WEIGHTS],
            *[new_m[n] for n in TWIN_WEIGHTS], *[new_v[n] for n in TWIN_WEIGHTS])
```

```python
import functools
import math

import jax
import jax.numpy as jnp
from jax import lax
from jax.experimental import pallas as pl
from jax.experimental.pallas import tpu as pltpu

F32, BF16 = jnp.float32, jnp.bfloat16

D_MODEL = 1024
DEPTH = 4
HEAD_DIM = 64
MLA_HEADS, MLA_NOPE, MLA_ROPE, MLA_V = 8, 64, 32, 64
MLA_Q_LORA, MLA_KV_LORA = 384, 256
MLA_QK = MLA_NOPE + MLA_ROPE
MLA_QK_PAD = 128
ROPE_THETA = 10000.0
SWA_HEADS, SWA_KV_HEADS, SWA_WINDOW = 8, 2, 128
SWA_GROUP = SWA_HEADS // SWA_KV_HEADS
REL_BUCKETS, REL_MAX_DIST = 32, 128
FOX_HEADS = 16
D_FF = 4 * D_MODEL
D_PLE = 256
BLOCK_Q = 128
DN_ALPHA = (2 * DEPTH) ** 0.25
NORM_EPS = 1e-5
NEG_INF = -1e30
EVEN_IN = 1440
EVEN_IN_PAD = 1536
ODD_QKV = 3 * FOX_HEADS * HEAD_DIM
ODD_IN = ODD_QKV + FOX_HEADS
LANES = 128

ADAM_LR, ADAM_B1, ADAM_B2, ADAM_EPS, ADAM_WD, ADAM_STEP = 0.001, 0.9, 0.999, 1e-08, 0.01, 10

N_DEV = 8
VMEM_LIMIT_BYTES = 48 * 1024 * 1024
ATT_TILE = 256

NN = (((1,), (0,)), ((), ()))
NT = (((1,), (1,)), ((), ()))
TN = (((0,), (0,)), ((), ()))

BIG = ['ev_w_in', 'ev_w_uq', 'ev_w_ukv', 'ev_w_out', 'od_w_in', 'od_w_out', 'w_up', 'w_down',
       'ple_w_proj', 'ple_w_gate']
BIG_AXIS = {'ev_w_in': 2, 'ev_w_uq': 2, 'ev_w_ukv': 2, 'ev_w_out': 1, 'od_w_in': 2, 'od_w_out': 1,
            'w_up': 2, 'w_down': 1, 'ple_w_proj': 2, 'ple_w_gate': 1}
SMALL = ['rel_bias', 'ev_q_norm', 'ev_kv_norm', 'ev_sinks', 'od_b_f', 'ln1_g', 'ln1_b', 'ln2_g', 'ln2_b',
         'ple_b_gate']
WEIGHTS = ['rel_bias', 'ev_w_in', 'ev_q_norm', 'ev_w_uq', 'ev_kv_norm', 'ev_w_ukv', 'ev_sinks', 'ev_w_out',
           'od_w_in', 'od_b_f', 'od_w_out', 'ln1_g', 'ln1_b', 'w_up', 'w_down', 'ln2_g', 'ln2_b',
           'ple_w_proj', 'ple_w_gate', 'ple_b_gate']


def _cparams(*sem):
    return pltpu.CompilerParams(dimension_semantics=sem, vmem_limit_bytes=VMEM_LIMIT_BYTES)


def _pick(n, cands):
    for c in cands:
        if n % c == 0:
            return c
    return n


def _mm(a, b, *, trans_b=False, extras=(), epilogue=None, out_dtypes=(F32,), name):
    M, K = a.shape
    N = b.shape[0] if trans_b else b.shape[1]
    tm = _pick(M, (512, 256, 128))
    tn = _pick(N, (512, 384, 256, 128))
    tk = K if K <= 2048 else _pick(K, (1024, 512))
    nk = K // tk
    n_ex, n_out = len(extras), len(out_dtypes)

    def body(*refs):
        a_ref, b_ref = refs[:2]
        ex = refs[2:2 + n_ex]
        outs = refs[2 + n_ex:2 + n_ex + n_out]
        part = lax.dot_general(a_ref[...].astype(BF16), b_ref[...].astype(BF16), NT if trans_b else NN,
                               preferred_element_type=F32)

        def finish(acc):
            res = epilogue(acc, *[e[...] for e in ex]) if epilogue is not None else (acc,)
            for o, r in zip(outs, res):
                o[...] = r.astype(o.dtype)

        if nk == 1:
            finish(part)
        else:
            acc_ref = refs[-1]
            k = pl.program_id(2)

            @pl.when(k == 0)
            def _():
                acc_ref[...] = part

            @pl.when(k > 0)
            def _():
                acc_ref[...] += part

            @pl.when(k == nk - 1)
            def _():
                finish(acc_ref[...])

    in_specs = [pl.BlockSpec((tm, tk), lambda i, j, k: (i, k)),
                pl.BlockSpec((tn, tk), lambda i, j, k: (j, k)) if trans_b
                else pl.BlockSpec((tk, tn), lambda i, j, k: (k, j))]
    for e in extras:
        if e.shape == (M, N):
            in_specs.append(pl.BlockSpec((tm, tn), lambda i, j, k: (i, j)))
        elif e.shape == (1, N):
            in_specs.append(pl.BlockSpec((1, tn), lambda i, j, k: (0, j)))
        else:
            raise ValueError(f"extra operand of shape {e.shape} for a ({M}, {N}) result")
    res = pl.pallas_call(
        body, name=name, grid=(M // tm, N // tn, nk), in_specs=in_specs,
        out_specs=[pl.BlockSpec((tm, tn), lambda i, j, k: (i, j)) for _ in out_dtypes],
        out_shape=[jax.ShapeDtypeStruct((M, N), d) for d in out_dtypes],
        scratch_shapes=[pltpu.VMEM((tm, tn), F32)] if nk > 1 else [],
        compiler_params=_cparams("parallel", "parallel", "arbitrary"),
    )(a, b, *extras)
    return res[0] if n_out == 1 else tuple(res)


def _mm_tn(a, b, *, name):
    T, K = a.shape
    N = b.shape[1]
    tka = _pick(K, (512, 384, 256, 128))
    tn = _pick(N, (512, 384, 256, 128))
    tt = _pick(T, (1024, 512, 256))

    def body(a_ref, b_ref, o_ref):
        part = lax.dot_general(a_ref[...].astype(BF16), b_ref[...].astype(BF16), TN, preferred_element_type=F32)
        t = pl.program_id(2)

        @pl.when(t == 0)
        def _():
            o_ref[...] = part

        @pl.when(t > 0)
        def _():
            o_ref[...] += part

    return pl.pallas_call(
        body, name=name, grid=(K // tka, N // tn, T // tt),
        in_specs=[pl.BlockSpec((tt, tka), lambda i, j, t: (t, i)), pl.BlockSpec((tt, tn), lambda i, j, t: (t, j))],
        out_specs=pl.BlockSpec((tka, tn), lambda i, j, t: (i, j)),
        out_shape=jax.ShapeDtypeStruct((K, N), F32),
        compiler_params=_cparams("parallel", "parallel", "arbitrary"),
    )(a, b)


ROW_TILE = 256


def _row_spec(cols, col_block=0):
    return pl.BlockSpec((ROW_TILE, cols), lambda i: (i, col_block))


def _tab_spec(cols, period):
    return pl.BlockSpec((ROW_TILE, cols), lambda i: (i % period, 0))


def _full_spec(shape):
    return pl.BlockSpec(shape, lambda i: (0,) * len(shape))


def _ln_fwd(x, m, g, b, *, name):
    T, D = x.shape

    def body(x_ref, m_ref, g_ref, b_ref, y_ref, yb_ref, xh_ref, r_ref):
        z = DN_ALPHA * x_ref[...] + m_ref[...]
        mu = jnp.mean(z, -1, keepdims=True)
        zc = z - mu
        r = lax.rsqrt(jnp.mean(zc * zc, -1, keepdims=True) + NORM_EPS)
        xh = zc * r
        y = xh * g_ref[...] + b_ref[...]
        y_ref[...] = y
        yb_ref[...] = y.astype(BF16)
        xh_ref[...] = xh
        r_ref[...] = r

    return pl.pallas_call(
        body, name=name, grid=(T // ROW_TILE,),
        in_specs=[_row_spec(D), _row_spec(D), _full_spec((1, D)), _full_spec((1, D))],
        out_specs=[_row_spec(D), _row_spec(D), _row_spec(D), _row_spec(1)],
        out_shape=[jax.ShapeDtypeStruct((T, D), F32), jax.ShapeDtypeStruct((T, D), BF16),
                   jax.ShapeDtypeStruct((T, D), F32), jax.ShapeDtypeStruct((T, 1), F32)],
        compiler_params=_cparams("parallel"),
    )(x, m, g, b)


def _ln_bwd(dy, xh, r, g, *, name):
    T, D = dy.shape

    def body(dy_ref, xh_ref, r_ref, g_ref, dz_ref, dzb_ref, dg_ref, db_ref):
        dyv, xhv = dy_ref[...], xh_ref[...]
        dyg = dyv * g_ref[...]
        c1 = jnp.mean(dyg, -1, keepdims=True)
        c2 = jnp.mean(dyg * xhv, -1, keepdims=True)
        dz = r_ref[...] * (dyg - c1 - xhv * c2)
        dz_ref[...] = dz
        dzb_ref[...] = dz.astype(BF16)

        @pl.when(pl.program_id(0) == 0)
        def _():
            dg_ref[...] = jnp.zeros_like(dg_ref)
            db_ref[...] = jnp.zeros_like(db_ref)

        dg_ref[...] += jnp.sum(dyv * xhv, 0, keepdims=True)
        db_ref[...] += jnp.sum(dyv, 0, keepdims=True)

    return pl.pallas_call(
        body, name=name, grid=(T // ROW_TILE,),
        in_specs=[_row_spec(D), _row_spec(D), _row_spec(1), _full_spec((1, D))],
        out_specs=[_row_spec(D), _row_spec(D), _full_spec((1, D)), _full_spec((1, D))],
        out_shape=[jax.ShapeDtypeStruct((T, D), F32), jax.ShapeDtypeStruct((T, D), BF16),
                   jax.ShapeDtypeStruct((1, D), F32), jax.ShapeDtypeStruct((1, D), F32)],
        compiler_params=_cparams("arbitrary"),
    )(dy, xh, r, g)


def _loss_grad(y, target, *, name):
    T, D = y.shape

    def body(y_ref, t_ref, dy_ref, sq_ref):
        err = y_ref[...] - t_ref[...]
        dy_ref[...] = err / D

        @pl.when(pl.program_id(0) == 0)
        def _():
            sq_ref[...] = jnp.zeros_like(sq_ref)

        sq_ref[...] += jnp.sum(err * err, 0, keepdims=True)

    return pl.pallas_call(
        body, name=name, grid=(T // ROW_TILE,),
        in_specs=[_row_spec(D), _row_spec(D)],
        out_specs=[_row_spec(D), _full_spec((1, D))],
        out_shape=[jax.ShapeDtypeStruct((T, D), F32), jax.ShapeDtypeStruct((1, D), F32)],
        compiler_params=_cparams("arbitrary"),
    )(y, target)


def _ple_bwd_elem(dx3, g, e, *, name):
    T, D = dx3.shape

    def body(dx_ref, g_ref, e_ref, de_ref, dz_ref, db_ref):
        dx, gv = dx_ref[...], g_ref[...]
        de_ref[...] = (dx * gv).astype(BF16)
        dz = dx * e_ref[...] * gv * (1.0 - gv)
        dz_ref[...] = dz.astype(BF16)

        @pl.when(pl.program_id(0) == 0)
        def _():
            db_ref[...] = jnp.zeros_like(db_ref)

        db_ref[...] += jnp.sum(dz, 0, keepdims=True)

    return pl.pallas_call(
        body, name=name, grid=(T // ROW_TILE,),
        in_specs=[_row_spec(D), _row_spec(D), _row_spec(D)],
        out_specs=[_row_spec(D), _row_spec(D), _full_spec((1, D))],
        out_shape=[jax.ShapeDtypeStruct((T, D), BF16), jax.ShapeDtypeStruct((T, D), BF16),
                   jax.ShapeDtypeStruct((1, D), F32)],
        compiler_params=_cparams("arbitrary"),
    )(dx3, g, e)


def _rope(x, tabs, seq, *, width, col_block, sign, out_dtype, name):
    T = x.shape[0]
    half = MLA_ROPE // 2

    def body(x_ref, a_ref, bm_ref, bp_ref, o_ref):
        xv = x_ref[...].astype(F32)
        rot = pltpu.roll(xv, width - half, 1) * bm_ref[...] + pltpu.roll(xv, half, 1) * bp_ref[...]
        o_ref[...] = (xv * a_ref[...] + sign * rot).astype(o_ref.dtype)

    return pl.pallas_call(
        body, name=name, grid=(T // ROW_TILE,),
        in_specs=[_row_spec(width, col_block)] + [_tab_spec(width, seq // ROW_TILE)] * 3,
        out_specs=_row_spec(width),
        out_shape=jax.ShapeDtypeStruct((T, width), out_dtype),
        compiler_params=_cparams("parallel"),
    )(x, *tabs)


def _even_norms(h, gq, gkv, *, name):
    T = h.shape[0]
    q0, q1, kv1 = 0, MLA_Q_LORA, MLA_Q_LORA + MLA_KV_LORA

    def body(h_ref, gq_ref, gkv_ref, cq_ref, ckv_ref, rq_ref, rkv_ref):
        cq = h_ref[:, q0:q1]
        rq = lax.rsqrt(jnp.mean(cq * cq, -1, keepdims=True) + NORM_EPS)
        cq_ref[...] = (cq * rq * gq_ref[...]).astype(BF16)
        rq_ref[...] = rq
        ckv = h_ref[:, q1:kv1]
        rkv = lax.rsqrt(jnp.mean(ckv * ckv, -1, keepdims=True) + NORM_EPS)
        ckv_ref[...] = (ckv * rkv * gkv_ref[...]).astype(BF16)
        rkv_ref[...] = rkv

    return pl.pallas_call(
        body, name=name, grid=(T // ROW_TILE,),
        in_specs=[_row_spec(EVEN_IN_PAD), _full_spec((1, MLA_Q_LORA)), _full_spec((1, MLA_KV_LORA))],
        out_specs=[_row_spec(MLA_Q_LORA), _row_spec(MLA_KV_LORA), _row_spec(1), _row_spec(1)],
        out_shape=[jax.ShapeDtypeStruct((T, MLA_Q_LORA), BF16), jax.ShapeDtypeStruct((T, MLA_KV_LORA), BF16),
                   jax.ShapeDtypeStruct((T, 1), F32), jax.ShapeDtypeStruct((T, 1), F32)],
        compiler_params=_cparams("parallel"),
    )(h, gq, gkv)


def _even_in_bwd(h, rq, rkv, gq, gkv, dcqn, dckvn, dqs, dks, dvs, dkr, *, name):
    T = h.shape[0]
    q1, kv1 = MLA_Q_LORA, MLA_Q_LORA + MLA_KV_LORA
    s1 = kv1 + SWA_HEADS * HEAD_DIM
    s2 = s1 + SWA_KV_HEADS * HEAD_DIM
    s3 = s2 + SWA_KV_HEADS * HEAD_DIM

    def rms_bwd(c, r, g, dy):
        xr = c * r
        dyg = dy * g
        return r * (dyg - xr * jnp.mean(dyg * xr, -1, keepdims=True)), jnp.sum(dy * xr, 0, keepdims=True)

    def body(h_ref, rq_ref, rkv_ref, gq_ref, gkv_ref, dcq_ref, dckv_ref, dqs_ref, dks_ref, dvs_ref, dkr_ref,
             dh_ref, dgq_ref, dgkv_ref):
        @pl.when(pl.program_id(0) == 0)
        def _():
            dgq_ref[...] = jnp.zeros_like(dgq_ref)
            dgkv_ref[...] = jnp.zeros_like(dgkv_ref)

        dcq, dgq = rms_bwd(h_ref[:, 0:q1], rq_ref[...], gq_ref[...], dcq_ref[...])
        dckv, dgkv = rms_bwd(h_ref[:, q1:kv1], rkv_ref[...], gkv_ref[...], dckv_ref[...])
        dgq_ref[...] += dgq
        dgkv_ref[...] += dgkv
        dh_ref[:, 0:q1] = dcq.astype(BF16)
        dh_ref[:, q1:kv1] = dckv.astype(BF16)
        dh_ref[:, kv1:s1] = dqs_ref[...]
        dh_ref[:, s1:s2] = dks_ref[...]
        dh_ref[:, s2:s3] = dvs_ref[...]
        dh_ref[:, s3:EVEN_IN_PAD] = dkr_ref[...]

    return pl.pallas_call(
        body, name=name, grid=(T // ROW_TILE,),
        in_specs=[_row_spec(EVEN_IN_PAD), _row_spec(1), _row_spec(1), _full_spec((1, MLA_Q_LORA)),
                  _full_spec((1, MLA_KV_LORA)), _row_spec(MLA_Q_LORA), _row_spec(MLA_KV_LORA),
                  _row_spec(SWA_HEADS * HEAD_DIM), _row_spec(SWA_KV_HEADS * HEAD_DIM),
                  _row_spec(SWA_KV_HEADS * HEAD_DIM), _row_spec(LANES)],
        out_specs=[_row_spec(EVEN_IN_PAD), _full_spec((1, MLA_Q_LORA)), _full_spec((1, MLA_KV_LORA))],
        out_shape=[jax.ShapeDtypeStruct((T, EVEN_IN_PAD), BF16), jax.ShapeDtypeStruct((1, MLA_Q_LORA), F32),
                   jax.ShapeDtypeStruct((1, MLA_KV_LORA), F32)],
        compiler_params=_cparams("arbitrary"),
    )(h, rq, rkv, gq, gkv, dcqn, dckvn, dqs, dks, dvs, dkr)


def _fox_decay_fwd(f3, bf, *, name):
    B, S, _ = f3.shape

    def body(f_ref, b_ref, csh_ref, chs_ref):
        x = f_ref[...] + b_ref[...]
        c = jnp.minimum(x, 0.0) - jnp.log1p(jnp.exp(-jnp.abs(x)))
        row = lax.broadcasted_iota(jnp.int32, (S, LANES), 0)
        k = 1
        while k < S:
            c = c + jnp.where(row >= k, pltpu.roll(c, k, 0), 0.0)
            k *= 2
        csh_ref[...] = c
        chs_ref[...] = c.T

    return pl.pallas_call(
        body, name=name, grid=(B,),
        in_specs=[pl.BlockSpec((None, S, LANES), lambda b: (b, 0, 0)), pl.BlockSpec((1, LANES), lambda b: (0, 0))],
        out_specs=[pl.BlockSpec((None, S, LANES), lambda b: (b, 0, 0)),
                   pl.BlockSpec((None, LANES, S), lambda b: (b, 0, 0))],
        out_shape=[jax.ShapeDtypeStruct((B, S, LANES), F32), jax.ShapeDtypeStruct((B, LANES, S), F32)],
        compiler_params=_cparams("parallel"),
    )(f3, bf)


def _fox_decay_bwd(dc_hs, f3, bf, *, name):
    B, S, _ = f3.shape

    def body(dc_ref, f_ref, b_ref, df_ref, db_ref):
        g = dc_ref[...].T
        row = lax.broadcasted_iota(jnp.int32, (S, LANES), 0)
        k = 1
        while k < S:
            g = g + jnp.where(row < S - k, pltpu.roll(g, S - k, 0), 0.0)
            k *= 2
        x = f_ref[...] + b_ref[...]
        df = g * (1.0 / (1.0 + jnp.exp(x)))
        df_ref[...] = df.astype(BF16)

        @pl.when(pl.program_id(0) == 0)
        def _():
            db_ref[...] = jnp.zeros_like(db_ref)

        db_ref[...] += jnp.sum(df, 0, keepdims=True)

    return pl.pallas_call(
        body, name=name, grid=(B,),
        in_specs=[pl.BlockSpec((None, LANES, S), lambda b: (b, 0, 0)),
                  pl.BlockSpec((None, S, LANES), lambda b: (b, 0, 0)), pl.BlockSpec((1, LANES), lambda b: (0, 0))],
        out_specs=[pl.BlockSpec((None, S, LANES), lambda b: (b, 0, 0)), pl.BlockSpec((1, LANES), lambda b: (0, 0))],
        out_shape=[jax.ShapeDtypeStruct((B, S, LANES), BF16), jax.ShapeDtypeStruct((1, LANES), F32)],
        compiler_params=_cparams("arbitrary"),
    )(dc_hs, f3, bf)


def _head_column(csh_block, h):
    lane = lax.broadcasted_iota(jnp.int32, csh_block.shape, 1)
    return jnp.sum(jnp.where(lane == h, csh_block, 0.0), axis=-1, keepdims=True)


def _causal_mask(s):
    r = lax.broadcasted_iota(jnp.int32, s.shape, 0)
    c = lax.broadcasted_iota(jnp.int32, s.shape, 1)
    return jnp.where(c <= r, s, NEG_INF)


def _flash_fwd(q, k, v, csh, crow, *, scale, name):
    B, H, S, d = q.shape
    dv = v.shape[-1]
    t = ATT_TILE
    nq = S // t
    decay = csh is not None

    def body(*refs):
        if decay:
            q_ref, k_ref, v_ref, csh_ref, crow_ref, o_ref, lse_ref, m_s, l_s, acc_s = refs
        else:
            q_ref, k_ref, v_ref, o_ref, lse_ref, m_s, l_s, acc_s = refs
        i = pl.program_id(2)
        qv = q_ref[...]
        cq = _head_column(csh_ref[...], pl.program_id(1)) if decay else None
        m_s[...] = jnp.full((t, 1), NEG_INF, F32)
        l_s[...] = jnp.zeros((t, 1), F32)
        acc_s[...] = jnp.zeros((t, dv), F32)

        def step(j, masked):
            rows = pl.ds(pl.multiple_of(j * t, t), t)
            s = lax.dot_general(qv, k_ref[rows, :], NT, preferred_element_type=F32) * scale
            if decay:
                s = s + cq - crow_ref[j]
            if masked:
                s = _causal_mask(s)
            m_prev = m_s[...]
            m_new = jnp.maximum(m_prev, jnp.max(s, -1, keepdims=True))
            alpha = jnp.exp(m_prev - m_new)
            p = jnp.exp(s - m_new)
            l_s[...] = alpha * l_s[...] + jnp.sum(p, -1, keepdims=True)
            acc_s[...] = alpha * acc_s[...] + lax.dot_general(p.astype(BF16), v_ref[rows, :], NN,
                                                              preferred_element_type=F32)
            m_s[...] = m_new

        def loop_body(j, carry):
            step(j, False)
            return carry

        lax.fori_loop(0, i, loop_body, 0)
        step(i, True)
        l = l_s[...]
        o_ref[...] = (acc_s[...] / l).astype(o_ref.dtype)
        lse_ref[...] = m_s[...] + jnp.log(l)

    in_specs = [pl.BlockSpec((None, None, t, d), lambda b, h, i: (b, h, i, 0)),
                pl.BlockSpec((None, None, S, d), lambda b, h, i: (b, h, 0, 0)),
                pl.BlockSpec((None, None, S, dv), lambda b, h, i: (b, h, 0, 0))]
    args = [q, k, v]
    if decay:
        in_specs += [pl.BlockSpec((None, t, LANES), lambda b, h, i: (b, i, 0)),
                     pl.BlockSpec((None, None, nq, 1, t), lambda b, h, i: (b, h, 0, 0, 0))]
        args += [csh, crow]
    return pl.pallas_call(
        body, name=name, grid=(B, H, nq), in_specs=in_specs,
        out_specs=[pl.BlockSpec((None, None, t, dv), lambda b, h, i: (b, h, i, 0)),
                   pl.BlockSpec((None, None, t, 1), lambda b, h, i: (b, h, i, 0))],
        out_shape=[jax.ShapeDtypeStruct((B, H, S, dv), BF16), jax.ShapeDtypeStruct((B, H, S, 1), F32)],
        scratch_shapes=[pltpu.VMEM((t, 1), F32), pltpu.VMEM((t, 1), F32), pltpu.VMEM((t, dv), F32)],
        compiler_params=_cparams("parallel", "parallel", "parallel"),
    )(*args)


def _flash_bwd(q, k, v, o, do, lse, csh, crow, *, scale, name):
    B, H, S, d = q.shape
    dv = v.shape[-1]
    t = ATT_TILE
    nq = S // t
    decay = csh is not None

    def body(*refs):
        if decay:
            (q_ref, k_ref, v_ref, o_ref, do_ref, lse_ref, csh_ref, crow_ref,
             dq_ref, dk_ref, dv_ref, dc_ref, dcq_ref, delta_s, cq_s) = refs
        else:
            q_ref, k_ref, v_ref, o_ref, do_ref, lse_ref, dq_ref, dk_ref, dv_ref, delta_s = refs
        head, j = pl.program_id(1), pl.program_id(2)

        @pl.when(j == 0)
        def _():
            dq_ref[...] = jnp.zeros_like(dq_ref)
            delta_s[...] = jnp.sum(do_ref[...].astype(F32) * o_ref[...].astype(F32), -1, keepdims=True)
            if decay:
                cq_s[...] = _head_column(csh_ref[...], head)
                dcq_ref[...] = jnp.zeros_like(dcq_ref)

        kb, vb = k_ref[...], v_ref[...]
        ck = crow_ref[j] if decay else None

        def step(i, masked, carry):
            dk_acc, dv_acc, dc_acc = carry
            rows = pl.ds(pl.multiple_of(i * t, t), t)
            qi, doi = q_ref[rows, :], do_ref[rows, :]
            s = lax.dot_general(qi, kb, NT, preferred_element_type=F32) * scale
            if decay:
                s = s + cq_s[rows, :] - ck
            if masked:
                s = _causal_mask(s)
            p = jnp.exp(s - lse_ref[rows, :])
            dv_acc = dv_acc + lax.dot_general(p.astype(BF16), doi, TN, preferred_element_type=F32)
            dp = lax.dot_general(doi, vb, NT, preferred_element_type=F32)
            ds = p * (dp - delta_s[rows, :])
            dss = (ds * scale).astype(BF16)
            dk_acc = dk_acc + lax.dot_general(dss, qi, TN, preferred_element_type=F32)
            dq_ref[rows, :] += lax.dot_general(dss, kb, NN, preferred_element_type=F32)
            if decay:
                dc_acc = dc_acc - jnp.sum(ds, 0, keepdims=True)
                dcq_ref[rows, :] += jnp.sum(ds, -1, keepdims=True)
            return dk_acc, dv_acc, dc_acc

        carry = step(j, True, (jnp.zeros((t, d), F32), jnp.zeros((t, dv), F32), jnp.zeros((1, t), F32)))
        dk_acc, dv_acc, dc_acc = lax.fori_loop(j + 1, nq, lambda i, c: step(i, False, c), carry)
        dk_ref[...] = dk_acc
        dv_ref[...] = dv_acc
        if decay:
            dc_ref[...] = dc_acc

    full = lambda w: pl.BlockSpec((None, None, S, w), lambda b, h, j: (b, h, 0, 0))
    blk = lambda w: pl.BlockSpec((None, None, t, w), lambda b, h, j: (b, h, j, 0))
    in_specs = [full(d), blk(d), blk(dv), full(dv), full(dv), full(1)]
    args = [q, k, v, o, do, lse]
    out_specs = [full(d), blk(d), blk(dv)]
    out_shape = [jax.ShapeDtypeStruct((B, H, S, d), F32), jax.ShapeDtypeStruct((B, H, S, d), F32),
                 jax.ShapeDtypeStruct((B, H, S, dv), F32)]
    scratch = [pltpu.VMEM((S, 1), F32)]
    if decay:
        in_specs += [pl.BlockSpec((None, S, LANES), lambda b, h, j: (b, 0, 0)),
                     pl.BlockSpec((None, None, nq, 1, t), lambda b, h, j: (b, h, 0, 0, 0))]
        args += [csh, crow]
        out_specs.append(pl.BlockSpec((None, None, None, 1, t), lambda b, h, j: (b, h, j, 0, 0)))
        out_shape.append(jax.ShapeDtypeStruct((B, H, nq, 1, t), F32))
        out_specs.append(full(1))
        out_shape.append(jax.ShapeDtypeStruct((B, H, S, 1), F32))
        scratch.append(pltpu.VMEM((S, 1), F32))
    return pl.pallas_call(
        body, name=name, grid=(B, H, nq), in_specs=in_specs, out_specs=out_specs, out_shape=out_shape,
        scratch_shapes=scratch, compiler_params=_cparams("parallel", "parallel", "arbitrary"),
    )(*args)


def _swa_logits(q_ref, kp_ref, ko_ref, bias_ref, n):
    G, Q = SWA_GROUP, BLOCK_Q
    qv = q_ref[...].reshape(G * Q, HEAD_DIM)
    kk = jnp.concatenate([kp_ref[...], ko_ref[...]], axis=0)
    s = lax.dot_general(qv, kk, NT, preferred_element_type=F32) * (HEAD_DIM ** -0.5)
    s3 = s.reshape(G, Q, 2 * Q) + bias_ref[...]
    a = lax.broadcasted_iota(jnp.int32, s3.shape, 1)
    col = lax.broadcasted_iota(jnp.int32, s3.shape, 2)
    dist = a + Q - col
    valid = (dist >= 0) & (dist < SWA_WINDOW) & ((col >= Q) | (n > 0))
    return qv, kk, jnp.where(valid, s3, NEG_INF)


def _swa_specs(B, S):
    G, Q = SWA_GROUP, BLOCK_Q
    qs = lambda w, im: pl.BlockSpec((None, G, Q, w), im)
    ks = lambda im: pl.BlockSpec((None, None, Q, HEAD_DIM), im)
    return G, Q, qs, ks


def _swa_fwd(q, k, v, bias, sinkcol, *, name):
    B, _, S, _ = q.shape
    G, Q, qs, ks = _swa_specs(B, S)
    nb = S // Q

    def body(q_ref, kp_ref, ko_ref, vp_ref, vo_ref, bias_ref, sink_ref, o_ref, lse_ref):
        _, _, s3 = _swa_logits(q_ref, kp_ref, ko_ref, bias_ref, pl.program_id(2))
        sink = sink_ref[...]
        m = jnp.maximum(jnp.max(s3, -1, keepdims=True), sink)
        p = jnp.exp(s3 - m)
        l = jnp.sum(p, -1, keepdims=True) + jnp.exp(sink - m)
        w = (p / l).astype(BF16).reshape(G * Q, 2 * Q)
        vv = jnp.concatenate([vp_ref[...], vo_ref[...]], axis=0)
        o = lax.dot_general(w, vv, NN, preferred_element_type=F32)
        o_ref[...] = o.reshape(G, Q, HEAD_DIM).astype(BF16)
        lse_ref[...] = m + jnp.log(l)

    own = lambda b, g, n: (b, g, n, 0)
    prev = lambda b, g, n: (b, g, jnp.maximum(n - 1, 0), 0)
    grp = lambda b, g, n: (g, 0, 0)
    return pl.pallas_call(
        body, name=name, grid=(B, SWA_KV_HEADS, nb),
        in_specs=[qs(HEAD_DIM, own), ks(prev), ks(own), ks(prev), ks(own),
                  pl.BlockSpec((G, Q, 2 * Q), grp), pl.BlockSpec((G, Q, 1), grp)],
        out_specs=[qs(HEAD_DIM, own), qs(1, own)],
        out_shape=[jax.ShapeDtypeStruct((B, SWA_HEADS, S, HEAD_DIM), BF16),
                   jax.ShapeDtypeStruct((B, SWA_HEADS, S, 1), F32)],
        compiler_params=_cparams("parallel", "parallel", "parallel"),
    )(q, k, k, v, v, bias, sinkcol)


def _swa_bwd(q, k, v, o, do, lse, bias, sinkcol, *, name):
    B, _, S, _ = q.shape
    G, Q, qs, ks = _swa_specs(B, S)
    nb = S // Q

    def body(q_ref, kp_ref, ko_ref, vp_ref, vo_ref, o_ref, do_ref, lse_ref, bias_ref, sink_ref,
             dq_ref, dko_ref, dkp_ref, dvo_ref, dvp_ref, dbias_ref, dsink_ref):
        @pl.when((pl.program_id(1) == 0) & (pl.program_id(2) == 0))
        def _():
            dbias_ref[...] = jnp.zeros_like(dbias_ref)
            dsink_ref[...] = jnp.zeros_like(dsink_ref)

        qv, kk, s3 = _swa_logits(q_ref, kp_ref, ko_ref, bias_ref, pl.program_id(2))
        lse = lse_ref[...]
        p = jnp.exp(s3 - lse)
        do3 = do_ref[...]
        do2 = do3.reshape(G * Q, HEAD_DIM)
        vv = jnp.concatenate([vp_ref[...], vo_ref[...]], axis=0)
        dp = lax.dot_general(do2, vv, NT, preferred_element_type=F32).reshape(G, Q, 2 * Q)
        delta = jnp.sum(do3.astype(F32) * o_ref[...].astype(F32), -1, keepdims=True)
        ds = p * (dp - delta)
        dbias_ref[...] += ds
        dsink_ref[...] -= jnp.exp(sink_ref[...] - lse) * delta
        dss = (ds * (HEAD_DIM ** -0.5)).astype(BF16).reshape(G * Q, 2 * Q)
        dq_ref[...] = lax.dot_general(dss, kk, NN, preferred_element_type=F32).reshape(G, Q, HEAD_DIM)
        dkk = lax.dot_general(dss, qv, TN, preferred_element_type=F32)
        dvv = lax.dot_general(p.astype(BF16).reshape(G * Q, 2 * Q), do2, TN, preferred_element_type=F32)
        dkp_ref[...] = dkk[:Q]
        dko_ref[...] = dkk[Q:]
        dvp_ref[...] = dvv[:Q]
        dvo_ref[...] = dvv[Q:]

    own = lambda g, b, n: (b, g, n, 0)
    prev = lambda g, b, n: (b, g, jnp.maximum(n - 1, 0), 0)
    grp = lambda g, b, n: (g, 0, 0)
    kv_shape = jax.ShapeDtypeStruct((B, SWA_KV_HEADS, S, HEAD_DIM), F32)
    return pl.pallas_call(
        body, name=name, grid=(SWA_KV_HEADS, B, nb),
        in_specs=[qs(HEAD_DIM, own), ks(prev), ks(own), ks(prev), ks(own), qs(HEAD_DIM, own), qs(HEAD_DIM, own),
                  qs(1, own), pl.BlockSpec((G, Q, 2 * Q), grp), pl.BlockSpec((G, Q, 1), grp)],
        out_specs=[qs(HEAD_DIM, own), ks(own), ks(own), ks(own), ks(own),
                   pl.BlockSpec((G, Q, 2 * Q), grp), pl.BlockSpec((G, Q, 1), grp)],
        out_shape=[jax.ShapeDtypeStruct((B, SWA_HEADS, S, HEAD_DIM), F32), kv_shape, kv_shape, kv_shape, kv_shape,
                   jax.ShapeDtypeStruct((SWA_HEADS, Q, 2 * Q), F32), jax.ShapeDtypeStruct((SWA_HEADS, Q, 1), F32)],
        compiler_params=_cparams("parallel", "arbitrary", "arbitrary"),
    )(q, k, k, v, v, o, do, lse, bias, sinkcol)


def _bias_bucket_sum(dbias, bucket, *, name):
    def body(d_ref, b_ref, o_ref):
        dbv, bk = d_ref[...], b_ref[...]
        lane = lax.broadcasted_iota(jnp.int32, (SWA_HEADS, LANES), 1)
        out = jnp.zeros((SWA_HEADS, LANES), F32)
        for b in range(REL_BUCKETS):
            part = jnp.sum(jnp.where(bk == b, dbv, 0.0), axis=1)
            tot = jnp.sum(part, axis=-1, keepdims=True)
            out = out + jnp.where(lane == b, tot, 0.0)
        o_ref[...] = out

    return pl.pallas_call(
        body, name=name, out_shape=jax.ShapeDtypeStruct((SWA_HEADS, LANES), F32),
        compiler_params=pltpu.CompilerParams(vmem_limit_bytes=VMEM_LIMIT_BYTES),
    )(dbias, bucket)


def _adamw(w, g, m, v, *, name):
    R, C = w.shape
    tr = R if R <= 1024 else _pick(R, (512, 256))

    def body(w_ref, g_ref, m_ref, v_ref, d_ref, nm_ref, nv_ref):
        gv = g_ref[...]
        m_new = ADAM_B1 * m_ref[...] + (1.0 - ADAM_B1) * gv
        v_new = ADAM_B2 * v_ref[...] + (1.0 - ADAM_B2) * jnp.square(gv)
        m_hat = m_new / (1.0 - ADAM_B1 ** ADAM_STEP)
        v_hat = v_new / (1.0 - ADAM_B2 ** ADAM_STEP)
        d_ref[...] = -ADAM_LR * (m_hat / (jnp.sqrt(v_hat) + ADAM_EPS) + ADAM_WD * w_ref[...])
        nm_ref[...] = m_new
        nv_ref[...] = v_new

    spec = pl.BlockSpec((tr, C), lambda i: (i, 0))
    return pl.pallas_call(
        body, name=name, grid=(R // tr,), in_specs=[spec] * 4, out_specs=[spec] * 3,
        out_shape=[jax.ShapeDtypeStruct((R, C), F32)] * 3, compiler_params=_cparams("parallel"),
    )(w, g, m, v)


MESH_ID = pl.DeviceIdType.MESH
HBM_SPEC = pl.BlockSpec(memory_space=pltpu.HBM)
VMEM_SPEC = pl.BlockSpec(memory_space=pltpu.VMEM)


def _mesh_place():
    x, y, c = lax.axis_index("x"), lax.axis_index("y"), lax.axis_index("c")
    return x, y, c, 4 * x + 2 * y + c


def _all_gather_hbm(block, *, name):
    R, W = block.shape

    def body(x_ref, out_ref, send_sems, recv_sems, local_sem):
        x, y, c, _ = _mesh_place()
        me, sibling = (x, y, c), (x, y, 1 - c)
        chips = [(1 - x, y), (x, 1 - y), (1 - x, 1 - y)]

        def slot(px, py, pc):
            return out_ref.at[4 * px + 2 * py + pc]

        def copy(k, blk, to, src=None):
            return pltpu.make_async_remote_copy(
                src_ref=slot(*blk) if src is None else src, dst_ref=slot(*blk),
                send_sem=send_sems.at[k], recv_sem=recv_sems.at[k], device_id=to, device_id_type=MESH_ID)

        mine = pltpu.make_async_copy(x_ref, slot(*me), local_sem)
        mine.start()
        first = [copy(0, me, sibling, src=x_ref)]
        first += [copy(1 + j, me, (*chip, c), src=x_ref) for j, chip in enumerate(chips)]
        for cp in first:
            cp.start()
        passed = [copy(4 + j, (*chip, c), sibling) for j, chip in enumerate(chips)]
        for j, chip in enumerate(chips):
            copy(1 + j, (*chip, c), me).wait_recv()
            passed[j].start()
        copy(0, sibling, me).wait_recv()
        for j, chip in enumerate(chips):
            copy(4 + j, (*chip, 1 - c), me).wait_recv()
        for cp in first + passed:
            cp.wait_send()
        mine.wait()

    return pl.pallas_call(
        body, name=name, out_shape=jax.ShapeDtypeStruct((N_DEV, R, W), block.dtype),
        in_specs=[HBM_SPEC], out_specs=HBM_SPEC,
        scratch_shapes=[pltpu.SemaphoreType.DMA((7,)), pltpu.SemaphoreType.DMA((7,)), pltpu.SemaphoreType.DMA],
    )(block)


def _peers(x, y, c):
    out = []
    for mask in range(1, N_DEV):
        dx, dy, dc = (mask >> 2) & 1, (mask >> 1) & 1, mask & 1
        px, py, pc = (1 - x if dx else x), (1 - y if dy else y), (1 - c if dc else c)
        out.append(((px, py, pc), 4 * px + 2 * py + pc))
    return out


def _all_to_all_hbm(parts, *, name):
    _, R, W = parts.shape

    def body(p_ref, out_ref, send_sems, recv_sems, local_sem):
        x, y, c, me = _mesh_place()
        mine = pltpu.make_async_copy(p_ref.at[me], out_ref.at[me], local_sem)
        mine.start()
        copies = []
        for k, (peer, peer_idx) in enumerate(_peers(x, y, c)):
            copies.append(pltpu.make_async_remote_copy(
                src_ref=p_ref.at[peer_idx], dst_ref=out_ref.at[me], send_sem=send_sems.at[k],
                recv_sem=recv_sems.at[k], device_id=peer, device_id_type=MESH_ID))
        for cp in copies:
            cp.start()
        for cp in copies:
            cp.wait_recv()
        for cp in copies:
            cp.wait_send()
        mine.wait()

    return pl.pallas_call(
        body, name=name, out_shape=jax.ShapeDtypeStruct(parts.shape, parts.dtype),
        in_specs=[HBM_SPEC], out_specs=HBM_SPEC,
        scratch_shapes=[pltpu.SemaphoreType.DMA((7,)), pltpu.SemaphoreType.DMA((7,)), pltpu.SemaphoreType.DMA],
    )(parts)


def _sum_slots(parts, *, name):
    n, R, W = parts.shape
    tr = _pick(R, (1024, 512, 256, 128, 64, 32, 16))

    def body(p_ref, o_ref):
        acc = p_ref[0].astype(F32)
        for j in range(1, n):
            acc = acc + p_ref[j].astype(F32)
        o_ref[...] = acc

    return pl.pallas_call(
        body, name=name, grid=(R // tr,),
        in_specs=[pl.BlockSpec((n, tr, W), lambda i: (0, i, 0))], out_specs=pl.BlockSpec((tr, W), lambda i: (i, 0)),
        out_shape=jax.ShapeDtypeStruct((R, W), F32), compiler_params=_cparams("parallel"),
    )(parts)


def _all_reduce_small(block, *, name):
    R, W = block.shape

    def body(x_ref, out_ref, buf, send_sems, recv_sems):
        x, y, c, me = _mesh_place()
        copies = []
        for k, (peer, _) in enumerate(_peers(x, y, c)):
            copies.append(pltpu.make_async_remote_copy(
                src_ref=x_ref, dst_ref=buf.at[me], send_sem=send_sems.at[k], recv_sem=recv_sems.at[k],
                device_id=peer, device_id_type=MESH_ID))
        for cp in copies:
            cp.start()
        buf[me] = x_ref[...]
        for cp in copies:
            cp.wait_recv()
        for cp in copies:
            cp.wait_send()
        acc = buf[0]
        for j in range(1, N_DEV):
            acc = acc + buf[j]
        out_ref[...] = acc

    return pl.pallas_call(
        body, name=name, out_shape=jax.ShapeDtypeStruct((R, W), F32),
        in_specs=[VMEM_SPEC], out_specs=VMEM_SPEC,
        scratch_shapes=[pltpu.VMEM((N_DEV, R, W), F32), pltpu.SemaphoreType.DMA((7,)), pltpu.SemaphoreType.DMA((7,))],
    )(block)


PACK_ROWS = 512


def _pad_flat(flat, width):
    n = flat.shape[-1]
    chunk = PACK_ROWS * width
    pad = (-n) % chunk
    if pad:
        flat = jnp.pad(flat, [(0, 0)] * (flat.ndim - 1) + [(0, pad)])
    return flat.reshape(flat.shape[:-1] + (-1, width))


def _pack_shards(shards):
    return _pad_flat(jnp.concatenate([shards[n].reshape(-1) for n in BIG]), LANES)


def _unpack_gathered(g, shard_shapes):
    flat = g.reshape(N_DEV, -1)
    out, off = {}, 0
    for n in BIG:
        shp = shard_shapes[n]
        sz = math.prod(shp)
        seg = flat[:, off:off + sz].reshape((N_DEV,) + shp)
        off += sz
        if BIG_AXIS[n] == 2:
            out[n] = seg.transpose(1, 2, 0, 3).reshape(shp[0], shp[1], N_DEV * shp[2])
        else:
            out[n] = seg.transpose(1, 0, 2, 3).reshape(shp[0], N_DEV * shp[1], shp[2])
    return out


def _pack_full_grads(grads, shard_shapes):
    segs = []
    for n in BIG:
        shp = shard_shapes[n]
        g = grads[n]
        if BIG_AXIS[n] == 2:
            s = g.reshape(shp[0], shp[1], N_DEV, shp[2]).transpose(2, 0, 1, 3)
        else:
            s = g.reshape(shp[0], N_DEV, shp[1], shp[2]).transpose(1, 0, 2, 3)
        segs.append(s.reshape(N_DEV, -1).astype(BF16))
    return _pad_flat(jnp.concatenate(segs, axis=1), LANES)


def _unpack_shards(block, shard_shapes):
    flat = block.reshape(-1)
    out, off = {}, 0
    for n in BIG:
        sz = math.prod(shard_shapes[n])
        out[n] = flat[off:off + sz].reshape(shard_shapes[n])
        off += sz
    return out


def _pack_small(vals):
    return _pad_flat(jnp.concatenate([vals[n].reshape(-1).astype(F32) for n in SMALL]), LANES)


def _unpack_small(block, shapes):
    flat = block.reshape(-1)
    out, off = {}, 0
    for n in SMALL:
        sz = math.prod(shapes[n])
        out[n] = flat[off:off + sz].reshape(shapes[n])
        off += sz
    return out


def _rope_tables(S):
    half = MLA_ROPE // 2
    inv = 1.0 / (ROPE_THETA ** (jnp.arange(0, MLA_ROPE, 2, dtype=F32) / MLA_ROPE))
    ang = jnp.arange(S, dtype=F32)[:, None] * inv[None, :]
    cos, sin = jnp.cos(ang), jnp.sin(ang)
    zeros = jnp.zeros((S, half), F32)
    a_r = jnp.concatenate([cos, cos], -1)
    bm_r = jnp.concatenate([-sin, zeros], -1)
    bp_r = jnp.concatenate([zeros, sin], -1)

    def q_tab(r, nope_val):
        head = jnp.concatenate([jnp.full((S, MLA_NOPE), nope_val, F32), r], -1)
        return jnp.tile(head, (1, MLA_HEADS))

    def k_tab(r):
        return jnp.concatenate([r, jnp.zeros((S, LANES - MLA_ROPE), F32)], -1)

    q_tabs = (q_tab(a_r, 1.0), q_tab(bm_r, 0.0), q_tab(bp_r, 0.0))
    k_tabs = (k_tab(a_r), k_tab(bm_r), k_tab(bp_r))
    return q_tabs, k_tabs


def _t5_bucket(dist):
    exact = REL_BUCKETS // 2
    d = jnp.maximum(dist, 1).astype(F32)
    large = exact + (jnp.log(d / exact) / math.log(REL_MAX_DIST / exact) * (REL_BUCKETS - exact)).astype(jnp.int32)
    large = jnp.minimum(large, REL_BUCKETS - 1)
    return jnp.where(dist < exact, dist, large)


def _swa_bucket_table():
    a = jnp.arange(BLOCK_Q)[:, None]
    col = jnp.arange(2 * BLOCK_Q)[None, :]
    return _t5_bucket(jnp.maximum(a + BLOCK_Q - col, 0)).astype(jnp.int32)


def _heads(t, B, S, H, d):
    return t.reshape(B, S, H, d).transpose(0, 2, 1, 3)


def _unheads(t):
    B, H, S, d = t.shape
    return t.transpose(0, 2, 1, 3).reshape(B * S, H * d)


def _even_weights(W, j):
    w = W['ev_w_in'][j]
    cut0 = MLA_Q_LORA + MLA_KV_LORA
    cut1 = cut0 + MLA_ROPE
    w_in = jnp.concatenate([w[:, :cut0], w[:, cut1:], w[:, cut0:cut1],
                            jnp.zeros((D_MODEL, EVEN_IN_PAD - EVEN_IN), w.dtype)], axis=1)
    return w_in, W['ev_w_uq'][j], W['ev_w_ukv'][j], W['ev_w_out'][j]


def _even_in_grad_unpad(dw):
    cut0 = MLA_Q_LORA + MLA_KV_LORA
    n_tail = EVEN_IN - cut0 - MLA_ROPE
    return jnp.concatenate([dw[:, :cut0], dw[:, cut0 + n_tail:cut0 + n_tail + MLA_ROPE], dw[:, cut0:cut0 + n_tail]],
                           axis=1)


def _even_fwd(xb, W, P, j, B, S, tabs, tag):
    q_tabs, k_tabs, bias, sinkcol = tabs
    w_in, w_uq, w_ukv, w_out = _even_weights(W, j)
    T = B * S
    h = _mm(xb, w_in, name=f"{tag}_in")
    cqn, ckvn, rq, rkv = _even_norms(h, P['ev_q_norm'][j][None], P['ev_kv_norm'][j][None], name=f"{tag}_norms")
    qf = _mm(cqn, w_uq, name=f"{tag}_uq")
    kv = _mm(ckvn, w_ukv, out_dtypes=(BF16,), name=f"{tag}_ukv")
    qr = _rope(qf, q_tabs, S, width=MLA_HEADS * MLA_QK, col_block=0, sign=1.0, out_dtype=BF16, name=f"{tag}_ropeq")
    kr = _rope(h, k_tabs, S, width=LANES, col_block=EVEN_IN_PAD // LANES - 1, sign=1.0, out_dtype=BF16,
               name=f"{tag}_ropek")
    q4 = qr.reshape(B, S, MLA_HEADS, MLA_QK)
    q4 = jnp.pad(q4, ((0, 0), (0, 0), (0, 0), (0, MLA_QK_PAD - MLA_QK))).transpose(0, 2, 1, 3)
    kv4 = kv.reshape(B, S, MLA_HEADS, MLA_NOPE + MLA_V)
    kr4 = jnp.broadcast_to(kr.reshape(B, S, 1, LANES)[..., :MLA_ROPE], (B, S, MLA_HEADS, MLA_ROPE))
    k4 = jnp.concatenate([kv4[..., :MLA_NOPE], kr4, jnp.zeros((B, S, MLA_HEADS, MLA_QK_PAD - MLA_QK), BF16)], -1)
    k4 = k4.transpose(0, 2, 1, 3)
    v4 = kv4[..., MLA_NOPE:].transpose(0, 2, 1, 3)
    o_mla, lse_mla = _flash_fwd(q4, k4, v4, None, None, scale=MLA_QK ** -0.5, name=f"{tag}_mla")
    c0 = MLA_Q_LORA + MLA_KV_LORA
    c1 = c0 + SWA_HEADS * HEAD_DIM
    c2 = c1 + SWA_KV_HEADS * HEAD_DIM
    c3 = c2 + SWA_KV_HEADS * HEAD_DIM
    qs = _heads(h[:, c0:c1].astype(BF16), B, S, SWA_HEADS, HEAD_DIM)
    ks = _heads(h[:, c1:c2].astype(BF16), B, S, SWA_KV_HEADS, HEAD_DIM)
    vs = _heads(h[:, c2:c3].astype(BF16), B, S, SWA_KV_HEADS, HEAD_DIM)
    o_swa, lse_swa = _swa_fwd(qs, ks, vs, bias, sinkcol, name=f"{tag}_swa")
    o_cat = jnp.concatenate([_unheads(o_mla), _unheads(o_swa)], axis=-1)
    m = _mm(o_cat, w_out, name=f"{tag}_out")
    res = dict(h=h, cqn=cqn, ckvn=ckvn, rq=rq, rkv=rkv, q4=q4, k4=k4, v4=v4, o_mla=o_mla, lse_mla=lse_mla,
               qs=qs, ks=ks, vs=vs, o_swa=o_swa, lse_swa=lse_swa, o_cat=o_cat)
    return m, res


def _even_bwd(dmb, dz1, xb, W, P, j, B, S, tabs, res, tag):
    q_tabs, k_tabs, bias, sinkcol = tabs
    w_in, w_uq, w_ukv, w_out = _even_weights(W, j)
    g = {}
    g['ev_w_out'] = _mm_tn(res['o_cat'], dmb, name=f"{tag}_dwout")
    do = _mm(dmb, w_out, trans_b=True, out_dtypes=(BF16,), name=f"{tag}_do")
    n_mla = MLA_HEADS * MLA_V
    do_mla = _heads(do[:, :n_mla], B, S, MLA_HEADS, MLA_V)
    do_swa = _heads(do[:, n_mla:], B, S, SWA_HEADS, HEAD_DIM)
    dq4, dk4, dv4 = _flash_bwd(res['q4'], res['k4'], res['v4'], res['o_mla'], do_mla, res['lse_mla'], None, None,
                               scale=MLA_QK ** -0.5, name=f"{tag}_mla_bwd")
    dqf = dq4[..., :MLA_QK].transpose(0, 2, 1, 3).reshape(B * S, MLA_HEADS * MLA_QK)
    dq_pre = _rope(dqf, q_tabs, S, width=MLA_HEADS * MLA_QK, col_block=0, sign=-1.0, out_dtype=BF16,
                   name=f"{tag}_ropeq_bwd")
    g['ev_w_uq'] = _mm_tn(res['cqn'], dq_pre, name=f"{tag}_dwuq")
    dcqn = _mm(dq_pre, w_uq, trans_b=True, name=f"{tag}_dcqn")
    dkv = jnp.concatenate([dk4[..., :MLA_NOPE], dv4], -1).transpose(0, 2, 1, 3)
    dkv = dkv.reshape(B * S, MLA_HEADS * (MLA_NOPE + MLA_V)).astype(BF16)
    g['ev_w_ukv'] = _mm_tn(res['ckvn'], dkv, name=f"{tag}_dwukv")
    dckvn = _mm(dkv, w_ukv, trans_b=True, name=f"{tag}_dckvn")
    dkr = jnp.sum(dk4[..., MLA_NOPE:MLA_QK], axis=1).reshape(B * S, MLA_ROPE)
    dkr = jnp.pad(dkr, ((0, 0), (0, LANES - MLA_ROPE)))
    dkr_pre = _rope(dkr, k_tabs, S, width=LANES, col_block=0, sign=-1.0, out_dtype=BF16, name=f"{tag}_ropek_bwd")
    dqs, dko, dkp, dvo, dvp, dbias, dsink = _swa_bwd(res['qs'], res['ks'], res['vs'], res['o_swa'], do_swa,
                                                     res['lse_swa'], bias, sinkcol, name=f"{tag}_swa_bwd")

    def fold(own, prev):
        shifted = jnp.concatenate([prev[:, :, BLOCK_Q:], jnp.zeros_like(prev[:, :, :BLOCK_Q])], axis=2)
        return _unheads(own + shifted).astype(BF16)

    dh, dgq, dgkv = _even_in_bwd(res['h'], res['rq'], res['rkv'], P['ev_q_norm'][j][None], P['ev_kv_norm'][j][None],
                                 dcqn, dckvn, _unheads(dqs).astype(BF16), fold(dko, dkp), fold(dvo, dvp), dkr_pre,
                                 name=f"{tag}_in_bwd")
    g['ev_w_in'] = _even_in_grad_unpad(_mm_tn(xb, dh, name=f"{tag}_dwin"))
    dx = _mm(dh, w_in, trans_b=True, extras=(dz1,), epilogue=lambda acc, r: (acc + DN_ALPHA * r,), name=f"{tag}_dx")
    small = dict(ev_q_norm=dgq[0], ev_kv_norm=dgkv[0], dbias=dbias, ev_sinks=jnp.sum(dsink, axis=(1, 2)))
    return dx, g, small


def _odd_fwd(xb, W, P, j, B, S, tag):
    w = W['od_w_in'][j]
    w_qkv = w[:, :ODD_QKV]
    w_f = jnp.pad(w[:, ODD_QKV:], ((0, 0), (0, LANES - FOX_HEADS)))
    bf = jnp.pad(P['od_b_f'][j], (0, LANES - FOX_HEADS))[None]
    qkv = _mm(xb, w_qkv, out_dtypes=(BF16,), name=f"{tag}_qkv")
    f = _mm(xb, w_f, name=f"{tag}_f").reshape(B, S, LANES)
    csh, chs = _fox_decay_fwd(f, bf, name=f"{tag}_decay")
    crow = chs[:, :FOX_HEADS].reshape(B, FOX_HEADS, S // ATT_TILE, 1, ATT_TILE)
    n = FOX_HEADS * HEAD_DIM
    q4, k4, v4 = (_heads(qkv[:, i * n:(i + 1) * n], B, S, FOX_HEADS, HEAD_DIM) for i in range(3))
    o4, lse = _flash_fwd(q4, k4, v4, csh, crow, scale=HEAD_DIM ** -0.5, name=f"{tag}_fox")
    o = _unheads(o4)
    m = _mm(o, W['od_w_out'][j], name=f"{tag}_out")
    res = dict(f=f, bf=bf, csh=csh, crow=crow, q4=q4, k4=k4, v4=v4, o4=o4, lse=lse, o=o, w_qkv=w_qkv, w_f=w_f)
    return m, res


def _odd_bwd(dmb, dz1, xb, W, P, j, B, S, res, tag):
    g = {}
    w_out = W['od_w_out'][j]
    g['od_w_out'] = _mm_tn(res['o'], dmb, name=f"{tag}_dwout")
    do = _mm(dmb, w_out, trans_b=True, out_dtypes=(BF16,), name=f"{tag}_do")
    do4 = _heads(do, B, S, FOX_HEADS, HEAD_DIM)
    dq4, dk4, dv4, dck, dcq = _flash_bwd(res['q4'], res['k4'], res['v4'], res['o4'], do4, res['lse'], res['csh'],
                                         res['crow'], scale=HEAD_DIM ** -0.5, name=f"{tag}_fox_bwd")
    dc = dck.reshape(B, FOX_HEADS, S) + dcq.reshape(B, FOX_HEADS, S)
    dc_hs = jnp.pad(dc, ((0, 0), (0, LANES - FOX_HEADS), (0, 0)))
    df, dbf = _fox_decay_bwd(dc_hs, res['f'], res['bf'], name=f"{tag}_decay_bwd")
    df = df.reshape(B * S, LANES)
    dqkv = jnp.concatenate([_unheads(dq4), _unheads(dk4), _unheads(dv4)], axis=-1).astype(BF16)
    dw_qkv = _mm_tn(xb, dqkv, name=f"{tag}_dwqkv")
    dw_f = _mm_tn(xb, df, name=f"{tag}_dwf")
    g['od_w_in'] = jnp.concatenate([dw_qkv, dw_f[:, :FOX_HEADS]], axis=1)
    dxf = _mm(df, res['w_f'], trans_b=True, extras=(dz1,), epilogue=lambda acc, r: (acc + DN_ALPHA * r,),
              name=f"{tag}_dxf")
    dx = _mm(dqkv, res['w_qkv'], trans_b=True, extras=(dxf,), epilogue=lambda acc, r: (acc + r,), name=f"{tag}_dx")
    small = dict(od_b_f=dbf[0, :FOX_HEADS])
    return dx, g, small


def _local_step(x, p, target, W, P):
    B, S, D = x.shape
    T = B * S
    q_tabs, k_tabs = _rope_tables(S)
    bucket = _swa_bucket_table()
    bias = P['rel_bias'][bucket].astype(F32).transpose(2, 0, 1)

    xc = x.reshape(T, D)
    xcb = xc.astype(BF16)
    saved = []
    for i in range(DEPTH):
        j = i // 2
        tag = f"l{i}"
        lay = dict(xb=xcb)
        if i % 2 == 0:
            sinkcol = jnp.broadcast_to(P['ev_sinks'][j][:, None, None], (SWA_HEADS, BLOCK_Q, 1)).astype(F32)
            lay['tabs'] = (q_tabs, k_tabs, bias, sinkcol)
            m, lay['mix'] = _even_fwd(xcb, W, P, j, B, S, lay['tabs'], tag)
        else:
            m, lay['mix'] = _odd_fwd(xcb, W, P, j, B, S, tag)
        x1, x1b, lay['xh1'], lay['r1'] = _ln_fwd(xc, m, P['ln1_g'][i][None], P['ln1_b'][i][None], name=f"{tag}_ln1")
        lay['x1b'] = x1b
        lay['u'], lay['a'] = _mm(x1b, W['w_up'][i], out_dtypes=(F32, BF16),
                                 epilogue=lambda acc: (acc, jnp.square(jnp.maximum(acc, 0.0))), name=f"{tag}_up")
        d = _mm(lay['a'], W['w_down'][i], name=f"{tag}_down")
        x2, x2b, lay['xh2'], lay['r2'] = _ln_fwd(x1, d, P['ln2_g'][i][None], P['ln2_b'][i][None], name=f"{tag}_ln2")
        lay['x2b'] = x2b
        lay['p'] = p[i].reshape(T, D_PLE)
        lay['e'] = _mm(lay['p'], W['ple_w_proj'][i], name=f"{tag}_ple_proj")

        def gate(acc, bg, e, x2v):
            gv = 1.0 / (1.0 + jnp.exp(-(acc + bg)))
            y = x2v + gv * e
            return y, y, gv

        xc, xcb, lay['g'] = _mm(x2b, W['ple_w_gate'][i], extras=(P['ple_b_gate'][i][None], lay['e'], x2),
                                epilogue=gate, out_dtypes=(F32, BF16, F32), name=f"{tag}_ple_gate")
        saved.append(lay)

    dy, sq = _loss_grad(xc, target.reshape(T, D), name="loss")

    G = {n: [None] * (DEPTH if n in ('w_up', 'w_down', 'ple_w_proj', 'ple_w_gate') else DEPTH // 2) for n in BIG}
    Gs = {n: [None] * DEPTH for n in ('ln1_g', 'ln1_b', 'ln2_g', 'ln2_b', 'ple_b_gate')}
    Gs.update({n: [None] * (DEPTH // 2) for n in ('ev_q_norm', 'ev_kv_norm', 'ev_sinks', 'od_b_f')})
    dbias_total = None
    for i in reversed(range(DEPTH)):
        j = i // 2
        tag = f"l{i}b"
        lay = saved[i]
        de, dzg, dbg = _ple_bwd_elem(dy, lay['g'], lay['e'], name=f"{tag}_ple_elem")
        Gs['ple_b_gate'][i] = dbg[0]
        G['ple_w_proj'][i] = _mm_tn(lay['p'], de, name=f"{tag}_dwproj")
        G['ple_w_gate'][i] = _mm_tn(lay['x2b'], dzg, name=f"{tag}_dwgate")
        dx2 = _mm(dzg, W['ple_w_gate'][i], trans_b=True, extras=(dy,), epilogue=lambda acc, r: (acc + r,),
                  name=f"{tag}_dx2")
        dz2, dz2b, dg2, db2 = _ln_bwd(dx2, lay['xh2'], lay['r2'], P['ln2_g'][i][None], name=f"{tag}_ln2")
        Gs['ln2_g'][i], Gs['ln2_b'][i] = dg2[0], db2[0]
        G['w_down'][i] = _mm_tn(lay['a'], dz2b, name=f"{tag}_dwdown")
        du = _mm(dz2b, W['w_down'][i], trans_b=True, extras=(lay['u'],), out_dtypes=(BF16,),
                 epilogue=lambda acc, u: (acc * (2.0 * jnp.maximum(u, 0.0)),), name=f"{tag}_du")
        G['w_up'][i] = _mm_tn(lay['x1b'], du, name=f"{tag}_dwup")
        dx1 = _mm(du, W['w_up'][i], trans_b=True, extras=(dz2,), epilogue=lambda acc, r: (acc + DN_ALPHA * r,),
                  name=f"{tag}_dx1")
        dz1, dz1b, dg1, db1 = _ln_bwd(dx1, lay['xh1'], lay['r1'], P['ln1_g'][i][None], name=f"{tag}_ln1")
        Gs['ln1_g'][i], Gs['ln1_b'][i] = dg1[0], db1[0]
        if i % 2 == 0:
            dy, g, small = _even_bwd(dz1b, dz1, lay['xb'], W, P, j, B, S, lay['tabs'], lay['mix'], tag)
            dbias_total = small['dbias'] if dbias_total is None else dbias_total + small['dbias']
            for n in ('ev_q_norm', 'ev_kv_norm', 'ev_sinks'):
                Gs[n][j] = small[n]
        else:
            dy, g, small = _odd_bwd(dz1b, dz1, lay['xb'], W, P, j, B, S, lay['mix'], tag)
            Gs['od_b_f'][j] = small['od_b_f']
        for n, val in g.items():
            G[n][j] = val

    grads_big = {n: jnp.stack(G[n]) for n in BIG}
    grads_small = {n: jnp.stack(v) for n, v in Gs.items()}
    drel = _bias_bucket_sum(dbias_total, bucket, name="rel_bias_grad")
    grads_small['rel_bias'] = drel[:, :REL_BUCKETS].T
    return sq, dy.reshape(B, S, D), grads_big, grads_small


def kernel(x, p, rel_bias, ev_w_in, ev_q_norm, ev_w_uq, ev_kv_norm, ev_w_ukv, ev_sinks, ev_w_out, od_w_in, od_b_f, od_w_out, ln1_g, ln1_b, w_up, w_down, ln2_g, ln2_b, ple_w_proj, ple_w_gate, ple_b_gate, loss_target, m_rel_bias, m_ev_w_in, m_ev_q_norm, m_ev_w_uq, m_ev_kv_norm, m_ev_w_ukv, m_ev_sinks, m_ev_w_out, m_od_w_in, m_od_b_f, m_od_w_out, m_ln1_g, m_ln1_b, m_w_up, m_w_down, m_ln2_g, m_ln2_b, m_ple_w_proj, m_ple_w_gate, m_ple_b_gate, v_rel_bias, v_ev_w_in, v_ev_q_norm, v_ev_w_uq, v_ev_kv_norm, v_ev_w_ukv, v_ev_sinks, v_ev_w_out, v_od_w_in, v_od_b_f, v_od_w_out, v_ln1_g, v_ln1_b, v_w_up, v_w_down, v_ln2_g, v_ln2_b, v_ple_w_proj, v_ple_w_gate, v_ple_b_gate):
    given = dict(locals())
    w = {n: given[n] for n in WEIGHTS}
    mom = {n: given["m_" + n] for n in WEIGHTS}
    var = {n: given["v_" + n] for n in WEIGHTS}
    shard_shapes = {n: w[n].shape for n in BIG}
    small_shapes = {n: w[n].shape for n in SMALL}

    gathered = _all_gather_hbm(_pack_shards({n: w[n].astype(BF16) for n in BIG}), name="gather_weights")
    W = _unpack_gathered(gathered, shard_shapes)
    P = {n: w[n] for n in SMALL}

    sq, grad_x, grads_big, grads_small = _local_step(x, p, loss_target, W, P)
    loss = lax.psum(0.5 * jnp.sum(sq) / D_MODEL, ("x", "y", "c"))

    received = _all_to_all_hbm(_pack_full_grads(grads_big, shard_shapes), name="exchange_grads")
    g_big = _unpack_shards(_sum_slots(received, name="sum_grads"), shard_shapes)
    g_small_packed = _all_reduce_small(_pack_small(grads_small), name="reduce_small_grads")
    g_small = _unpack_small(g_small_packed, small_shapes)

    grad, delta, new_m, new_v = {}, {}, {}, {}
    for n in BIG:
        shp = shard_shapes[n]
        two_d = (shp[0] * shp[1], shp[2])
        d, nm, nv = _adamw(w[n].reshape(two_d), g_big[n].reshape(two_d), mom[n].reshape(two_d),
                           var[n].reshape(two_d), name=f"adamw_{n}")
        grad[n], delta[n], new_m[n], new_v[n] = g_big[n], d.reshape(shp), nm.reshape(shp), nv.reshape(shp)
    d, nm, nv = _adamw(_pack_small(w), g_small_packed, _pack_small(mom), _pack_small(var), name="adamw_small")
    d, nm, nv = (_unpack_small(t, small_shapes) for t in (d, nm, nv))
    for n in SMALL:
        grad[n], delta[n], new_m[n], new_v[n] = g_small[n], d[n], nm[n], nv[n]

    return (loss, grad_x, *[grad[n] for n in WEIGHTS], *[delta[n] for n in WEIGHTS],
            *[new_m[n] for n in WEIGHTS], *[new_v[n] for n in WEIGHTS])
```

```python
import math

import jax
import jax.numpy as jnp
from jax import lax
from jax.experimental import pallas as pl
from jax.experimental.pallas import tpu as pltpu

F32, BF16 = jnp.float32, jnp.bfloat16

D_MODEL = 1024
DEPTH = 4
HEAD_DIM = 64
MLA_HEADS, MLA_NOPE, MLA_ROPE, MLA_V = 8, 64, 32, 64
MLA_Q_LORA, MLA_KV_LORA = 384, 256
MLA_QK = MLA_NOPE + MLA_ROPE
ROPE_THETA = 10000.0
SWA_HEADS, SWA_KV_HEADS, SWA_WINDOW = 8, 2, 128
REL_BUCKETS, REL_MAX_DIST = 32, 128
FOX_HEADS = 16
D_FF = 4 * D_MODEL
D_PLE = 256
BLOCK_Q = 128
DN_ALPHA = (2 * DEPTH) ** 0.25
NORM_EPS = 1e-5
NEG_INF = -1e30
EVEN_IN = 1440
ODD_QKV = 3 * FOX_HEADS * HEAD_DIM
LANES = 128

EV_QS = (0, 512)
EV_CQ = (512, 896)
EV_CKV = (896, 1152)
EV_KS = (1152, 1280)
EV_VS = (1280, 1408)
EV_KR = (1408, 1536)
EVEN_IN_PAD = 1536
KR_LANE0 = MLA_NOPE

ADAM_LR, ADAM_B1, ADAM_B2, ADAM_EPS, ADAM_WD, ADAM_STEP = 0.001, 0.9, 0.999, 1e-08, 0.01, 10

N_DEV = 8
VMEM_LIMIT_BYTES = 48 * 1024 * 1024
ATT_TILE = 256

NN = (((1,), (0,)), ((), ()))
NT = (((1,), (1,)), ((), ()))
TN = (((0,), (0,)), ((), ()))

BIG = ['ev_w_in', 'ev_w_uq', 'ev_w_ukv', 'ev_w_out', 'od_w_in', 'od_w_out', 'w_up', 'w_down',
       'ple_w_proj', 'ple_w_gate']
BIG_AXIS = {'ev_w_in': 2, 'ev_w_uq': 2, 'ev_w_ukv': 2, 'ev_w_out': 1, 'od_w_in': 2, 'od_w_out': 1,
            'w_up': 2, 'w_down': 1, 'ple_w_proj': 2, 'ple_w_gate': 1}
SMALL = ['rel_bias', 'ev_q_norm', 'ev_kv_norm', 'ev_sinks', 'od_b_f', 'ln1_g', 'ln1_b', 'ln2_g', 'ln2_b',
         'ple_b_gate']
WEIGHTS = ['rel_bias', 'ev_w_in', 'ev_q_norm', 'ev_w_uq', 'ev_kv_norm', 'ev_w_ukv', 'ev_sinks', 'ev_w_out',
           'od_w_in', 'od_b_f', 'od_w_out', 'ln1_g', 'ln1_b', 'w_up', 'w_down', 'ln2_g', 'ln2_b',
           'ple_w_proj', 'ple_w_gate', 'ple_b_gate']


def _cparams(*sem):
    return pltpu.CompilerParams(dimension_semantics=sem, vmem_limit_bytes=VMEM_LIMIT_BYTES)


def _pick(n, cands):
    for c in cands:
        if n % c == 0:
            return c
    return n


def _mm(a, b, *, trans_b=False, extras=(), epilogue=None, out_dtypes=(F32,), name):
    M, K = a.shape
    N = b.shape[0] if trans_b else b.shape[1]
    tm = _pick(M, (512, 256, 128))
    tn = _pick(N, (512, 384, 256, 128))
    tk = K if K <= 2048 else _pick(K, (1024, 512))
    nk = K // tk
    n_ex, n_out = len(extras), len(out_dtypes)

    def body(*refs):
        a_ref, b_ref = refs[:2]
        ex = refs[2:2 + n_ex]
        outs = refs[2 + n_ex:2 + n_ex + n_out]
        part = lax.dot_general(a_ref[...].astype(BF16), b_ref[...].astype(BF16), NT if trans_b else NN,
                               preferred_element_type=F32)

        def finish(acc):
            res = epilogue(acc, *[e[...] for e in ex]) if epilogue is not None else (acc,)
            for o, r in zip(outs, res):
                o[...] = r.astype(o.dtype)

        if nk == 1:
            finish(part)
        else:
            acc_ref = refs[-1]
            k = pl.program_id(2)

            @pl.when(k == 0)
            def _():
                acc_ref[...] = part

            @pl.when(k > 0)
            def _():
                acc_ref[...] += part

            @pl.when(k == nk - 1)
            def _():
                finish(acc_ref[...])

    in_specs = [pl.BlockSpec((tm, tk), lambda i, j, k: (i, k)),
                pl.BlockSpec((tn, tk), lambda i, j, k: (j, k)) if trans_b
                else pl.BlockSpec((tk, tn), lambda i, j, k: (k, j))]
    for e in extras:
        if e.shape == (M, N):
            in_specs.append(pl.BlockSpec((tm, tn), lambda i, j, k: (i, j)))
        elif e.shape == (1, N):
            in_specs.append(pl.BlockSpec((1, tn), lambda i, j, k: (0, j)))
        else:
            raise ValueError(f"extra operand of shape {e.shape} for a ({M}, {N}) result")
    res = pl.pallas_call(
        body, name=name, grid=(M // tm, N // tn, nk), in_specs=in_specs,
        out_specs=[pl.BlockSpec((tm, tn), lambda i, j, k: (i, j)) for _ in out_dtypes],
        out_shape=[jax.ShapeDtypeStruct((M, N), d) for d in out_dtypes],
        scratch_shapes=[pltpu.VMEM((tm, tn), F32)] if nk > 1 else [],
        compiler_params=_cparams("parallel", "parallel", "arbitrary"),
    )(a, b, *extras)
    return res[0] if n_out == 1 else tuple(res)


def _mm_tn(a, b, *, name):
    T, K = a.shape
    N = b.shape[1]
    tka = _pick(K, (512, 384, 256, 128))
    tn = _pick(N, (512, 384, 256, 128))
    tt = _pick(T, (1024, 512, 256))

    def body(a_ref, b_ref, o_ref):
        part = lax.dot_general(a_ref[...].astype(BF16), b_ref[...].astype(BF16), TN, preferred_element_type=F32)
        t = pl.program_id(2)

        @pl.when(t == 0)
        def _():
            o_ref[...] = part

        @pl.when(t > 0)
        def _():
            o_ref[...] += part

    return pl.pallas_call(
        body, name=name, grid=(K // tka, N // tn, T // tt),
        in_specs=[pl.BlockSpec((tt, tka), lambda i, j, t: (t, i)), pl.BlockSpec((tt, tn), lambda i, j, t: (t, j))],
        out_specs=pl.BlockSpec((tka, tn), lambda i, j, t: (i, j)),
        out_shape=jax.ShapeDtypeStruct((K, N), F32),
        compiler_params=_cparams("parallel", "parallel", "arbitrary"),
    )(a, b)


ROW_TILE = 256


def _row_spec(cols, col_block=0):
    return pl.BlockSpec((ROW_TILE, cols), lambda i: (i, col_block))


def _tab_spec(cols, period):
    return pl.BlockSpec((ROW_TILE, cols), lambda i: (i % period, 0))


def _full_spec(shape):
    return pl.BlockSpec(shape, lambda i: (0,) * len(shape))


def _ln_fwd(x, m, g, b, *, name):
    T, D = x.shape

    def body(x_ref, m_ref, g_ref, b_ref, y_ref, yb_ref, xh_ref, r_ref):
        z = DN_ALPHA * x_ref[...] + m_ref[...]
        mu = jnp.mean(z, -1, keepdims=True)
        zc = z - mu
        r = lax.rsqrt(jnp.mean(zc * zc, -1, keepdims=True) + NORM_EPS)
        xh = zc * r
        y = xh * g_ref[...] + b_ref[...]
        y_ref[...] = y
        yb_ref[...] = y.astype(BF16)
        xh_ref[...] = xh
        r_ref[...] = r

    return pl.pallas_call(
        body, name=name, grid=(T // ROW_TILE,),
        in_specs=[_row_spec(D), _row_spec(D), _full_spec((1, D)), _full_spec((1, D))],
        out_specs=[_row_spec(D), _row_spec(D), _row_spec(D), _row_spec(1)],
        out_shape=[jax.ShapeDtypeStruct((T, D), F32), jax.ShapeDtypeStruct((T, D), BF16),
                   jax.ShapeDtypeStruct((T, D), F32), jax.ShapeDtypeStruct((T, 1), F32)],
        compiler_params=_cparams("parallel"),
    )(x, m, g, b)


def _ln_bwd(dy, xh, r, g, *, name):
    T, D = dy.shape

    def body(dy_ref, xh_ref, r_ref, g_ref, dz_ref, dzb_ref, dg_ref, db_ref):
        dyv, xhv = dy_ref[...], xh_ref[...]
        dyg = dyv * g_ref[...]
        c1 = jnp.mean(dyg, -1, keepdims=True)
        c2 = jnp.mean(dyg * xhv, -1, keepdims=True)
        dz = r_ref[...] * (dyg - c1 - xhv * c2)
        dz_ref[...] = dz
        dzb_ref[...] = dz.astype(BF16)

        @pl.when(pl.program_id(0) == 0)
        def _():
            dg_ref[...] = jnp.zeros_like(dg_ref)
            db_ref[...] = jnp.zeros_like(db_ref)

        dg_ref[...] += jnp.sum(dyv * xhv, 0, keepdims=True)
        db_ref[...] += jnp.sum(dyv, 0, keepdims=True)

    return pl.pallas_call(
        body, name=name, grid=(T // ROW_TILE,),
        in_specs=[_row_spec(D), _row_spec(D), _row_spec(1), _full_spec((1, D))],
        out_specs=[_row_spec(D), _row_spec(D), _full_spec((1, D)), _full_spec((1, D))],
        out_shape=[jax.ShapeDtypeStruct((T, D), F32), jax.ShapeDtypeStruct((T, D), BF16),
                   jax.ShapeDtypeStruct((1, D), F32), jax.ShapeDtypeStruct((1, D), F32)],
        compiler_params=_cparams("arbitrary"),
    )(dy, xh, r, g)


def _loss_grad(y, target, *, name):
    T, D = y.shape

    def body(y_ref, t_ref, dy_ref, sq_ref):
        err = y_ref[...] - t_ref[...]
        dy_ref[...] = err / D

        @pl.when(pl.program_id(0) == 0)
        def _():
            sq_ref[...] = jnp.zeros_like(sq_ref)

        sq_ref[...] += jnp.sum(err * err, 0, keepdims=True)

    return pl.pallas_call(
        body, name=name, grid=(T // ROW_TILE,),
        in_specs=[_row_spec(D), _row_spec(D)],
        out_specs=[_row_spec(D), _full_spec((1, D))],
        out_shape=[jax.ShapeDtypeStruct((T, D), F32), jax.ShapeDtypeStruct((1, D), F32)],
        compiler_params=_cparams("arbitrary"),
    )(y, target)


def _ple_bwd_elem(dx3, g, e, *, name):
    T, D = dx3.shape

    def body(dx_ref, g_ref, e_ref, de_ref, dz_ref, db_ref):
        dx, gv = dx_ref[...], g_ref[...]
        de_ref[...] = (dx * gv).astype(BF16)
        dz = dx * e_ref[...] * gv * (1.0 - gv)
        dz_ref[...] = dz.astype(BF16)

        @pl.when(pl.program_id(0) == 0)
        def _():
            db_ref[...] = jnp.zeros_like(db_ref)

        db_ref[...] += jnp.sum(dz, 0, keepdims=True)

    return pl.pallas_call(
        body, name=name, grid=(T // ROW_TILE,),
        in_specs=[_row_spec(D), _row_spec(D), _row_spec(D)],
        out_specs=[_row_spec(D), _row_spec(D), _full_spec((1, D))],
        out_shape=[jax.ShapeDtypeStruct((T, D), BF16), jax.ShapeDtypeStruct((T, D), BF16),
                   jax.ShapeDtypeStruct((1, D), F32)],
        compiler_params=_cparams("arbitrary"),
    )(dx3, g, e)


def _rotate(xv, a, bm, bp, sign):
    half = MLA_ROPE // 2
    width = xv.shape[-1]
    return xv * a + sign * (pltpu.roll(xv, width - half, 1) * bm + pltpu.roll(xv, half, 1) * bp)


def _rope(x, tabs, seq, *, sign, name):
    T, width = x.shape

    def body(x_ref, a_ref, bm_ref, bp_ref, o_ref):
        o_ref[...] = _rotate(x_ref[...], a_ref[...], bm_ref[...], bp_ref[...], sign).astype(BF16)

    return pl.pallas_call(
        body, name=name, grid=(T // ROW_TILE,),
        in_specs=[_row_spec(width)] + [_tab_spec(width, seq // ROW_TILE)] * 3,
        out_specs=_row_spec(width),
        out_shape=jax.ShapeDtypeStruct((T, width), BF16),
        compiler_params=_cparams("parallel"),
    )(x, *tabs)


def _mla_keys(knp, h, k_tabs, seq, *, name):
    T = knp.shape[0]

    def body(k_ref, h_ref, a_ref, bm_ref, bp_ref, o_ref):
        kr = _rotate(h_ref[...], a_ref[...], bm_ref[...], bp_ref[...], 1.0)
        for hd in range(MLA_HEADS):
            cols = slice(hd * LANES, (hd + 1) * LANES)
            o_ref[:, cols] = (k_ref[:, cols].astype(F32) + kr).astype(BF16)

    return pl.pallas_call(
        body, name=name, grid=(T // ROW_TILE,),
        in_specs=[_row_spec(MLA_HEADS * LANES), _row_spec(LANES, EV_KR[0] // LANES)]
        + [_tab_spec(LANES, seq // ROW_TILE)] * 3,
        out_specs=_row_spec(MLA_HEADS * LANES),
        out_shape=jax.ShapeDtypeStruct((T, MLA_HEADS * LANES), BF16),
        compiler_params=_cparams("parallel"),
    )(knp, h, *k_tabs)


def _mla_rope_key_grad(dk, k_tabs, seq, *, name):
    T = dk.shape[0]

    def body(dk_ref, a_ref, bm_ref, bp_ref, o_ref):
        tot = dk_ref[:, 0:LANES]
        for hd in range(1, MLA_HEADS):
            tot = tot + dk_ref[:, hd * LANES:(hd + 1) * LANES]
        o_ref[...] = _rotate(tot, a_ref[...], bm_ref[...], bp_ref[...], -1.0).astype(BF16)

    return pl.pallas_call(
        body, name=name, grid=(T // ROW_TILE,),
        in_specs=[_row_spec(MLA_HEADS * LANES)] + [_tab_spec(LANES, seq // ROW_TILE)] * 3,
        out_specs=_row_spec(LANES),
        out_shape=jax.ShapeDtypeStruct((T, LANES), BF16),
        compiler_params=_cparams("parallel"),
    )(dk, *k_tabs)


def _even_norms(h, gq, gkv, *, name):
    T = h.shape[0]

    def body(h_ref, gq_ref, gkv_ref, cq_ref, ckv_ref, rq_ref, rkv_ref):
        cq = h_ref[:, EV_CQ[0]:EV_CQ[1]]
        rq = lax.rsqrt(jnp.mean(cq * cq, -1, keepdims=True) + NORM_EPS)
        cq_ref[...] = (cq * rq * gq_ref[...]).astype(BF16)
        rq_ref[...] = rq
        ckv = h_ref[:, EV_CKV[0]:EV_CKV[1]]
        rkv = lax.rsqrt(jnp.mean(ckv * ckv, -1, keepdims=True) + NORM_EPS)
        ckv_ref[...] = (ckv * rkv * gkv_ref[...]).astype(BF16)
        rkv_ref[...] = rkv

    return pl.pallas_call(
        body, name=name, grid=(T // ROW_TILE,),
        in_specs=[_row_spec(EVEN_IN_PAD), _full_spec((1, MLA_Q_LORA)), _full_spec((1, MLA_KV_LORA))],
        out_specs=[_row_spec(MLA_Q_LORA), _row_spec(MLA_KV_LORA), _row_spec(1), _row_spec(1)],
        out_shape=[jax.ShapeDtypeStruct((T, MLA_Q_LORA), BF16), jax.ShapeDtypeStruct((T, MLA_KV_LORA), BF16),
                   jax.ShapeDtypeStruct((T, 1), F32), jax.ShapeDtypeStruct((T, 1), F32)],
        compiler_params=_cparams("parallel"),
    )(h, gq, gkv)


def _even_in_bwd(h, rq, rkv, gq, gkv, dcqn, dckvn, dqs, dks, dvs, dkr, *, name):
    T = h.shape[0]

    def rms_bwd(c, r, g, dy):
        xr = c * r
        dyg = dy * g
        return r * (dyg - xr * jnp.mean(dyg * xr, -1, keepdims=True)), jnp.sum(dy * xr, 0, keepdims=True)

    def body(h_ref, rq_ref, rkv_ref, gq_ref, gkv_ref, dcq_ref, dckv_ref, dqs_ref, dks_ref, dvs_ref, dkr_ref,
             dh_ref, dgq_ref, dgkv_ref):
        @pl.when(pl.program_id(0) == 0)
        def _():
            dgq_ref[...] = jnp.zeros_like(dgq_ref)
            dgkv_ref[...] = jnp.zeros_like(dgkv_ref)

        dcq, dgq = rms_bwd(h_ref[:, EV_CQ[0]:EV_CQ[1]], rq_ref[...], gq_ref[...], dcq_ref[...])
        dckv, dgkv = rms_bwd(h_ref[:, EV_CKV[0]:EV_CKV[1]], rkv_ref[...], gkv_ref[...], dckv_ref[...])
        dgq_ref[...] += dgq
        dgkv_ref[...] += dgkv
        dh_ref[:, EV_QS[0]:EV_QS[1]] = dqs_ref[...]
        dh_ref[:, EV_CQ[0]:EV_CQ[1]] = dcq.astype(BF16)
        dh_ref[:, EV_CKV[0]:EV_CKV[1]] = dckv.astype(BF16)
        dh_ref[:, EV_KS[0]:EV_KS[1]] = dks_ref[...]
        dh_ref[:, EV_VS[0]:EV_VS[1]] = dvs_ref[...]
        dh_ref[:, EV_KR[0]:EV_KR[1]] = dkr_ref[...]

    return pl.pallas_call(
        body, name=name, grid=(T // ROW_TILE,),
        in_specs=[_row_spec(EVEN_IN_PAD), _row_spec(1), _row_spec(1), _full_spec((1, MLA_Q_LORA)),
                  _full_spec((1, MLA_KV_LORA)), _row_spec(MLA_Q_LORA), _row_spec(MLA_KV_LORA),
                  _row_spec(SWA_HEADS * HEAD_DIM), _row_spec(LANES), _row_spec(LANES), _row_spec(LANES)],
        out_specs=[_row_spec(EVEN_IN_PAD), _full_spec((1, MLA_Q_LORA)), _full_spec((1, MLA_KV_LORA))],
        out_shape=[jax.ShapeDtypeStruct((T, EVEN_IN_PAD), BF16), jax.ShapeDtypeStruct((1, MLA_Q_LORA), F32),
                   jax.ShapeDtypeStruct((1, MLA_KV_LORA), F32)],
        compiler_params=_cparams("arbitrary"),
    )(h, rq, rkv, gq, gkv, dcqn, dckvn, dqs, dks, dvs, dkr)


def _fox_decay_fwd(f3, bf, *, name):
    B, S, _ = f3.shape

    def body(f_ref, b_ref, csh_ref, chs_ref):
        x = f_ref[...] + b_ref[...]
        c = jnp.minimum(x, 0.0) - jnp.log1p(jnp.exp(-jnp.abs(x)))
        row = lax.broadcasted_iota(jnp.int32, (S, LANES), 0)
        k = 1
        while k < S:
            c = c + jnp.where(row >= k, pltpu.roll(c, k, 0), 0.0)
            k *= 2
        csh_ref[...] = c
        chs_ref[...] = c.T

    return pl.pallas_call(
        body, name=name, grid=(B,),
        in_specs=[pl.BlockSpec((None, S, LANES), lambda b: (b, 0, 0)), pl.BlockSpec((1, LANES), lambda b: (0, 0))],
        out_specs=[pl.BlockSpec((None, S, LANES), lambda b: (b, 0, 0)),
                   pl.BlockSpec((None, LANES, S), lambda b: (b, 0, 0))],
        out_shape=[jax.ShapeDtypeStruct((B, S, LANES), F32), jax.ShapeDtypeStruct((B, LANES, S), F32)],
        compiler_params=_cparams("parallel"),
    )(f3, bf)


def _fox_decay_bwd(dc_hs, f3, bf, *, name):
    B, S, _ = f3.shape

    def body(dc_ref, f_ref, b_ref, df_ref, db_ref):
        g = dc_ref[...].T
        row = lax.broadcasted_iota(jnp.int32, (S, LANES), 0)
        k = 1
        while k < S:
            g = g + jnp.where(row < S - k, pltpu.roll(g, S - k, 0), 0.0)
            k *= 2
        x = f_ref[...] + b_ref[...]
        df = g * (1.0 / (1.0 + jnp.exp(x)))
        df_ref[...] = df.astype(BF16)

        @pl.when(pl.program_id(0) == 0)
        def _():
            db_ref[...] = jnp.zeros_like(db_ref)

        db_ref[...] += jnp.sum(df, 0, keepdims=True)

    return pl.pallas_call(
        body, name=name, grid=(B,),
        in_specs=[pl.BlockSpec((None, LANES, S), lambda b: (b, 0, 0)),
                  pl.BlockSpec((None, S, LANES), lambda b: (b, 0, 0)), pl.BlockSpec((1, LANES), lambda b: (0, 0))],
        out_specs=[pl.BlockSpec((None, S, LANES), lambda b: (b, 0, 0)), pl.BlockSpec((1, LANES), lambda b: (0, 0))],
        out_shape=[jax.ShapeDtypeStruct((B, S, LANES), BF16), jax.ShapeDtypeStruct((1, LANES), F32)],
        compiler_params=_cparams("arbitrary"),
    )(dc_hs, f3, bf)


def _head_column(block, h):
    lane = lax.broadcasted_iota(jnp.int32, block.shape, 1)
    return jnp.sum(jnp.where(lane == h, block, 0.0), axis=-1, keepdims=True)


def _causal_mask(s):
    r = lax.broadcasted_iota(jnp.int32, s.shape, 0)
    c = lax.broadcasted_iota(jnp.int32, s.shape, 1)
    return jnp.where(c <= r, s, NEG_INF)


def _low_half(shape):
    return (lax.broadcasted_iota(jnp.int32, shape, 1) % LANES) < HEAD_DIM


def _widen(x, cols):
    return jnp.concatenate([x] * (cols // LANES), axis=1)


def _both_halves(x, lo):
    r = pltpu.roll(x, HEAD_DIM, 1)
    return jnp.where(lo, x, r), jnp.where(lo, r, x)


def _flash_fwd(qa, ka, va, *, q_blk0, k_blk0, v_blk0, W, n_pairs, B, S, scale, csh=None, crow=None, name):
    t = ATT_TILE
    nq = S // t
    decay = csh is not None
    split = W == LANES

    def body(*refs):
        if decay:
            q_ref, k_ref, v_ref, csh_ref, crow_ref, o_ref, lse_ref, m_s, acc_s = refs
        else:
            q_ref, k_ref, v_ref, o_ref, lse_ref, m_s, acc_s = refs
        g, i = pl.program_id(1), pl.program_id(2)
        lo = _low_half((t, LANES))
        qv = q_ref[...]
        if split:
            qh = [jnp.where(lo, qv, jnp.zeros_like(qv)), jnp.where(lo, jnp.zeros_like(qv), qv)]
        else:
            qh = [qv[:, :LANES], qv[:, LANES:]]
        if decay:
            cq = [jnp.broadcast_to(_head_column(csh_ref[...], 2 * g + hh), (t, LANES)) for hh in range(2)]
        m_s[...] = jnp.full(m_s.shape, NEG_INF, F32)
        acc_s[...] = jnp.zeros(acc_s.shape, F32)

        def step(j, masked):
            rows = pl.ds(pl.multiple_of(j * t, t), t)
            kb, vb = k_ref[rows, :], v_ref[rows, :]
            ones = jnp.ones_like(vb)
            vaug = [jnp.where(lo, vb, ones), jnp.where(lo, ones, vb)]
            for hh in range(2):
                kh = kb if split else kb[:, hh * LANES:(hh + 1) * LANES]
                s = lax.dot_general(qh[hh], kh, NT, preferred_element_type=F32) * scale
                if decay:
                    s = s + _widen(cq[hh], t) - crow_ref[hh, j]
                if masked:
                    s = _causal_mask(s)
                m_prev = m_s[hh]
                m_new = jnp.maximum(m_prev, jnp.max(s, -1, keepdims=True))
                p = jnp.exp(s - _widen(m_new, t))
                acc_s[hh] = jnp.exp(m_prev - m_new) * acc_s[hh] + lax.dot_general(
                    p.astype(BF16), vaug[hh], NN, preferred_element_type=F32)
                m_s[hh] = m_new

        def loop_body(j, carry):
            step(j, False)
            return carry

        lax.fori_loop(0, i, loop_body, 0)
        step(i, True)
        acc0, acc1 = acc_s[0], acc_s[1]
        _, l0 = _both_halves(acc0, lo)
        l1, _ = _both_halves(acc1, lo)
        o_ref[...] = jnp.where(lo, acc0 / l0, acc1 / l1).astype(BF16)
        lse_ref[...] = jnp.where(lo, m_s[0] + jnp.log(l0), m_s[1] + jnp.log(l1))

    in_specs = [pl.BlockSpec((t, W), lambda b, g, i: (b * nq + i, q_blk0 + g)),
                pl.BlockSpec((S, W), lambda b, g, i: (b, k_blk0 + g)),
                pl.BlockSpec((S, LANES), lambda b, g, i: (b, v_blk0 + g))]
    args = [qa, ka, va]
    if decay:
        in_specs += [pl.BlockSpec((None, t, LANES), lambda b, g, i: (b, i, 0)),
                     pl.BlockSpec((None, 2, nq, 1, t), lambda b, g, i: (b, g, 0, 0, 0))]
        args += [csh, crow]
    out_spec = pl.BlockSpec((t, LANES), lambda b, g, i: (b * nq + i, g))
    return pl.pallas_call(
        body, name=name, grid=(B, n_pairs, nq), in_specs=in_specs, out_specs=[out_spec, out_spec],
        out_shape=[jax.ShapeDtypeStruct((B * S, n_pairs * LANES), BF16),
                   jax.ShapeDtypeStruct((B * S, n_pairs * LANES), F32)],
        scratch_shapes=[pltpu.VMEM((2, t, LANES), F32), pltpu.VMEM((2, t, LANES), F32)],
        compiler_params=_cparams("parallel", "parallel", "parallel"),
    )(*args)


def _flash_bwd(qa, ka, va, oa, doa, lsea, *, q_blk0, k_blk0, v_blk0, do_blk0, W, n_pairs, B, S, scale, qk_dtype,
               csh=None, crow=None, name):
    t = ATT_TILE
    nq = S // t
    decay = csh is not None
    split = W == LANES

    def body(*refs):
        if decay:
            (q_ref, k_ref, v_ref, o_ref, do_ref, lse_ref, csh_ref, crow_ref, dq_ref, dk_ref, dv_ref, dck_ref, dcq_ref,
             dq_s, lse_s, delta_s, dk_s, dv_s, cq_s, dcq_s, dck_s) = refs
        else:
            (q_ref, k_ref, v_ref, o_ref, do_ref, lse_ref, dq_ref, dk_ref, dv_ref,
             dq_s, lse_s, delta_s, dk_s, dv_s) = refs
        g, j = pl.program_id(1), pl.program_id(2)
        lo = _low_half((t, LANES))

        @pl.when(j == 0)
        def _():
            lo_s = _low_half((S, LANES))
            dq_s[...] = jnp.zeros(dq_s.shape, F32)
            lse_s[0], lse_s[1] = _both_halves(lse_ref[...], lo_s)
            dd = do_ref[...].astype(F32) * o_ref[...].astype(F32)
            delta_s[0] = jnp.broadcast_to(jnp.sum(jnp.where(lo_s, dd, 0.0), -1, keepdims=True), (S, LANES))
            delta_s[1] = jnp.broadcast_to(jnp.sum(jnp.where(lo_s, 0.0, dd), -1, keepdims=True), (S, LANES))
            if decay:
                for hh in range(2):
                    cq_s[hh] = jnp.broadcast_to(_head_column(csh_ref[...], 2 * g + hh), (S, LANES))
                dcq_s[...] = jnp.zeros(dcq_s.shape, F32)

        kb, vb = k_ref[...], v_ref[...]
        zk, zv = jnp.zeros_like(kb), jnp.zeros_like(vb)
        kh = [jnp.where(lo, kb, zk), jnp.where(lo, zk, kb)] if split else [kb[:, :LANES], kb[:, LANES:]]
        vh = [jnp.where(lo, vb, zv), jnp.where(lo, zv, vb)]
        dk_s[...] = jnp.zeros(dk_s.shape, F32)
        dv_s[...] = jnp.zeros(dv_s.shape, F32)
        if decay:
            dck_s[...] = jnp.zeros(dck_s.shape, F32)

        def step(i, masked):
            rows = pl.ds(pl.multiple_of(i * t, t), t)
            qi, doi = q_ref[rows, :], do_ref[rows, :]
            for hh in range(2):
                qx = qi if split else qi[:, hh * LANES:(hh + 1) * LANES]
                s = lax.dot_general(qx, kh[hh], NT, preferred_element_type=F32) * scale
                if decay:
                    s = s + _widen(cq_s[hh, rows, :], t) - crow_ref[hh, j]
                if masked:
                    s = _causal_mask(s)
                p = jnp.exp(s - _widen(lse_s[hh, rows, :], t))
                dv_s[hh] += lax.dot_general(p.astype(BF16), doi, TN, preferred_element_type=F32)
                dp = lax.dot_general(doi, vh[hh], NT, preferred_element_type=F32)
                ds = p * (dp - _widen(delta_s[hh, rows, :], t))
                dss = (ds * scale).astype(BF16)
                dk_s[hh] += lax.dot_general(dss, qx, TN, preferred_element_type=F32)
                dqc = lax.dot_general(dss, kh[hh], NN, preferred_element_type=F32)
                if split:
                    dq_s[rows, :] += dqc
                else:
                    dq_s[rows, hh * LANES:(hh + 1) * LANES] += dqc
                if decay:
                    dck_s[hh] -= jnp.sum(ds, 0, keepdims=True)
                    part = ds[:, :LANES]
                    for c in range(1, t // LANES):
                        part = part + ds[:, c * LANES:(c + 1) * LANES]
                    dcq_s[hh, rows, :] += part

        def loop_body(i, carry):
            step(i, False)
            return carry

        step(j, True)
        lax.fori_loop(j + 1, nq, loop_body, 0)
        if split:
            dk_ref[...] = jnp.where(lo, dk_s[0], dk_s[1]).astype(dk_ref.dtype)
        else:
            dk_ref[:, :LANES] = dk_s[0].astype(dk_ref.dtype)
            dk_ref[:, LANES:] = dk_s[1].astype(dk_ref.dtype)
        dv_ref[...] = jnp.where(lo, dv_s[0], dv_s[1]).astype(BF16)
        if decay:
            dck_ref[...] = dck_s[...]

        @pl.when(j == nq - 1)
        def _():
            dq_ref[...] = dq_s[...].astype(dq_ref.dtype)
            if decay:
                for hh in range(2):
                    dcq_ref[hh] = jnp.sum(dcq_s[hh].T, 0, keepdims=True)

    full = lambda w, blk0: pl.BlockSpec((S, w), lambda b, g, j: (b, blk0 + g))
    blk = lambda w, blk0: pl.BlockSpec((t, w), lambda b, g, j: (b * nq + j, blk0 + g))
    in_specs = [full(W, q_blk0), blk(W, k_blk0), blk(LANES, v_blk0), full(LANES, 0), full(LANES, do_blk0),
                full(LANES, 0)]
    args = [qa, ka, va, oa, doa, lsea]
    T = B * S
    out_specs = [full(W, 0), blk(W, 0), blk(LANES, 0)]
    out_shape = [jax.ShapeDtypeStruct((T, n_pairs * W), qk_dtype), jax.ShapeDtypeStruct((T, n_pairs * W), qk_dtype),
                 jax.ShapeDtypeStruct((T, n_pairs * LANES), BF16)]
    scratch = [pltpu.VMEM((S, W), F32), pltpu.VMEM((2, S, LANES), F32), pltpu.VMEM((2, S, LANES), F32),
               pltpu.VMEM((2, t, LANES), F32), pltpu.VMEM((2, t, LANES), F32)]
    if decay:
        in_specs += [pl.BlockSpec((None, S, LANES), lambda b, g, j: (b, 0, 0)),
                     pl.BlockSpec((None, 2, nq, 1, t), lambda b, g, j: (b, g, 0, 0, 0))]
        args += [csh, crow]
        out_specs += [pl.BlockSpec((None, 2, None, 1, t), lambda b, g, j: (b, g, j, 0, 0)),
                      pl.BlockSpec((None, 2, 1, S), lambda b, g, j: (b, g, 0, 0))]
        out_shape += [jax.ShapeDtypeStruct((B, 2 * n_pairs, nq, 1, t), F32),
                      jax.ShapeDtypeStruct((B, 2 * n_pairs, 1, S), F32)]
        scratch += [pltpu.VMEM((2, S, LANES), F32), pltpu.VMEM((2, S, LANES), F32), pltpu.VMEM((2, 1, t), F32)]
    return pl.pallas_call(
        body, name=name, grid=(B, n_pairs, nq), in_specs=in_specs, out_specs=out_specs, out_shape=out_shape,
        scratch_shapes=scratch, compiler_params=_cparams("parallel", "parallel", "arbitrary"),
    )(*args)


def _swa_common(q_ref, kp_ref, ko_ref, vp_ref, vo_ref, n):
    Q = BLOCK_Q
    lo = _low_half((Q, LANES))
    lo2 = _low_half((2 * Q, LANES))
    kk = jnp.concatenate([kp_ref[...], ko_ref[...]], axis=0)
    vv = jnp.concatenate([vp_ref[...], vo_ref[...]], axis=0)
    kdup = [x.astype(BF16) for x in _both_halves(kk, lo2)]
    vdup = [x.astype(BF16) for x in _both_halves(vv, lo2)]
    a = lax.broadcasted_iota(jnp.int32, (Q, 2 * Q), 0)
    col = lax.broadcasted_iota(jnp.int32, (Q, 2 * Q), 1)
    dist = a + Q - col
    valid = (dist >= 0) & (dist < SWA_WINDOW) & ((col >= Q) | (n > 0))
    qv = q_ref[...]
    qm = []
    for a_head in range(SWA_HEADS):
        qp = qv[:, (a_head // 2) * LANES:(a_head // 2 + 1) * LANES]
        keep = lo if a_head % 2 == 0 else jnp.logical_not(lo)
        qm.append(jnp.where(keep, qp, 0.0).astype(BF16))
    return lo, lo2, kdup, vdup, valid, qm


def _swa_in_specs(nb):
    Q = BLOCK_Q
    own = lambda blk: (lambda b, n: (b * nb + n, blk))
    prev = lambda blk: (lambda b, n: (b * nb + jnp.maximum(n - 1, 0), blk))
    kb, vb = EV_KS[0] // LANES, EV_VS[0] // LANES
    return [pl.BlockSpec((Q, SWA_HEADS * HEAD_DIM), own(0)), pl.BlockSpec((Q, LANES), prev(kb)),
            pl.BlockSpec((Q, LANES), own(kb)), pl.BlockSpec((Q, LANES), prev(vb)), pl.BlockSpec((Q, LANES), own(vb))]


def _swa_fwd(h, bias, sinkcol, *, B, S, name):
    Q = BLOCK_Q
    nb = S // Q
    scale = HEAD_DIM ** -0.5

    def body(q_ref, kp_ref, ko_ref, vp_ref, vo_ref, bias_ref, sink_ref, o_ref, lse_ref):
        lo, _, kdup, vdup, valid, qm = _swa_common(q_ref, kp_ref, ko_ref, vp_ref, vo_ref, pl.program_id(1))
        lane = lax.broadcasted_iota(jnp.int32, (Q, LANES), 1)
        lse_blk = jnp.zeros((Q, LANES), F32)
        pairs = []
        for pr in range(SWA_HEADS // 2):
            pv = []
            for half in range(2):
                a = 2 * pr + half
                kvh = a // (SWA_HEADS // SWA_KV_HEADS)
                s = lax.dot_general(qm[a], kdup[kvh], NT, preferred_element_type=F32) * scale + bias_ref[a]
                s = jnp.where(valid, s, NEG_INF)
                sink = sink_ref[a]
                mx = jnp.maximum(jnp.max(s, -1, keepdims=True), sink)
                p = jnp.exp(s - mx)
                l = jnp.sum(p, -1, keepdims=True) + jnp.exp(sink - mx)
                pv.append(lax.dot_general((p / l).astype(BF16), vdup[kvh], NN, preferred_element_type=F32))
                lse_blk = jnp.where(lane == a, mx + jnp.log(l), lse_blk)
            pairs.append(jnp.where(lo, pv[0], pv[1]))
        o_ref[...] = jnp.concatenate(pairs, axis=1).astype(BF16)
        lse_ref[...] = lse_blk

    whole = lambda shape: pl.BlockSpec(shape, lambda b, n: (0,) * len(shape))
    return pl.pallas_call(
        body, name=name, grid=(B, nb),
        in_specs=_swa_in_specs(nb) + [whole((SWA_HEADS, Q, 2 * Q)), whole((SWA_HEADS, Q, 1))],
        out_specs=[pl.BlockSpec((Q, SWA_HEADS * HEAD_DIM), lambda b, n: (b * nb + n, 0)),
                   pl.BlockSpec((Q, LANES), lambda b, n: (b * nb + n, 0))],
        out_shape=[jax.ShapeDtypeStruct((B * S, SWA_HEADS * HEAD_DIM), BF16),
                   jax.ShapeDtypeStruct((B * S, LANES), F32)],
        compiler_params=_cparams("parallel", "parallel"),
    )(h, h, h, h, h, bias, sinkcol)


def _swa_bwd(h, o, do, lse, bias, sinkcol, *, do_blk0, B, S, name):
    Q = BLOCK_Q
    nb = S // Q
    scale = HEAD_DIM ** -0.5
    group = SWA_HEADS // SWA_KV_HEADS

    def body(q_ref, kp_ref, ko_ref, vp_ref, vo_ref, o_ref, do_ref, lse_ref, bias_ref, sink_ref,
             dq_ref, dko_ref, dkp_ref, dvo_ref, dvp_ref, dbias_ref, dsink_ref):
        @pl.when((pl.program_id(0) == 0) & (pl.program_id(1) == 0))
        def _():
            dbias_ref[...] = jnp.zeros_like(dbias_ref)
            dsink_ref[...] = jnp.zeros_like(dsink_ref)

        lo, lo2, kdup, vdup, valid, qm = _swa_common(q_ref, kp_ref, ko_ref, vp_ref, vo_ref, pl.program_id(1))
        lse_blk = lse_ref[...]
        dkk = [jnp.zeros((2 * Q, LANES), F32) for _ in range(SWA_KV_HEADS)]
        dvv = [jnp.zeros((2 * Q, LANES), F32) for _ in range(SWA_KV_HEADS)]
        dq_pairs = []
        for pr in range(SWA_HEADS // 2):
            cols = slice(pr * LANES, (pr + 1) * LANES)
            do_p, o_p = do_ref[:, cols], o_ref[:, cols]
            dq_half = []
            for half in range(2):
                a = 2 * pr + half
                kvh = a // group
                keep = lo if half == 0 else jnp.logical_not(lo)
                s = lax.dot_general(qm[a], kdup[kvh], NT, preferred_element_type=F32) * scale + bias_ref[a]
                s = jnp.where(valid, s, NEG_INF)
                lse_a = _head_column(lse_blk, a)
                p = jnp.exp(s - lse_a)
                doh = jnp.where(keep, do_p, jnp.zeros_like(do_p))
                delta = jnp.sum(doh.astype(F32) * o_p.astype(F32), -1, keepdims=True)
                dp = lax.dot_general(doh, vdup[kvh], NT, preferred_element_type=F32)
                ds = p * (dp - delta)
                dbias_ref[a] += ds
                dsink_ref[a] -= jnp.exp(sink_ref[a] - lse_a) * delta
                dss = (ds * scale).astype(BF16)
                dq_half.append(lax.dot_general(dss, kdup[kvh], NN, preferred_element_type=F32))
                dkk[kvh] = dkk[kvh] + lax.dot_general(dss, qm[a], TN, preferred_element_type=F32)
                dvv[kvh] = dvv[kvh] + lax.dot_general(p.astype(BF16), doh, TN, preferred_element_type=F32)
            dq_pairs.append(jnp.where(lo, dq_half[0], dq_half[1]))
        dq_ref[...] = jnp.concatenate(dq_pairs, axis=1).astype(BF16)
        fold = lambda x: x + pltpu.roll(x, HEAD_DIM, 1)
        dk_blk = jnp.where(lo2, fold(dkk[0]), fold(dkk[1]))
        dv_blk = jnp.where(lo2, fold(dvv[0]), fold(dvv[1]))
        dkp_ref[...] = dk_blk[:Q]
        dko_ref[...] = dk_blk[Q:]
        dvp_ref[...] = dv_blk[:Q]
        dvo_ref[...] = dv_blk[Q:]

    whole = lambda shape: pl.BlockSpec(shape, lambda b, n: (0,) * len(shape))
    wide = lambda blk: pl.BlockSpec((Q, SWA_HEADS * HEAD_DIM), lambda b, n: (b * nb + n, blk))
    narrow = pl.BlockSpec((Q, LANES), lambda b, n: (b * nb + n, 0))
    kv_shape = jax.ShapeDtypeStruct((B * S, LANES), F32)
    return pl.pallas_call(
        body, name=name, grid=(B, nb),
        in_specs=_swa_in_specs(nb) + [wide(0), wide(do_blk0), narrow, whole((SWA_HEADS, Q, 2 * Q)),
                                      whole((SWA_HEADS, Q, 1))],
        out_specs=[wide(0), narrow, narrow, narrow, narrow, whole((SWA_HEADS, Q, 2 * Q)), whole((SWA_HEADS, Q, 1))],
        out_shape=[jax.ShapeDtypeStruct((B * S, SWA_HEADS * HEAD_DIM), BF16), kv_shape, kv_shape, kv_shape, kv_shape,
                   jax.ShapeDtypeStruct((SWA_HEADS, Q, 2 * Q), F32), jax.ShapeDtypeStruct((SWA_HEADS, Q, 1), F32)],
        compiler_params=_cparams("arbitrary", "arbitrary"),
    )(h, h, h, h, h, o, do, lse, bias, sinkcol)


def _bias_bucket_sum(dbias, bucket, *, name):
    def body(d_ref, b_ref, o_ref):
        dbv, bk = d_ref[...], b_ref[...]
        lane = lax.broadcasted_iota(jnp.int32, (SWA_HEADS, LANES), 1)
        out = jnp.zeros((SWA_HEADS, LANES), F32)
        for b in range(REL_BUCKETS):
            part = jnp.sum(jnp.where(bk == b, dbv, 0.0), axis=1)
            tot = jnp.sum(part, axis=-1, keepdims=True)
            out = out + jnp.where(lane == b, tot, 0.0)
        o_ref[...] = out

    return pl.pallas_call(
        body, name=name, out_shape=jax.ShapeDtypeStruct((SWA_HEADS, LANES), F32),
        compiler_params=pltpu.CompilerParams(vmem_limit_bytes=VMEM_LIMIT_BYTES),
    )(dbias, bucket)


def _adamw_update(w, g, m, v):
    m_new = ADAM_B1 * m + (1.0 - ADAM_B1) * g
    v_new = ADAM_B2 * v + (1.0 - ADAM_B2) * jnp.square(g)
    m_hat = m_new / (1.0 - ADAM_B1 ** ADAM_STEP)
    v_hat = v_new / (1.0 - ADAM_B2 ** ADAM_STEP)
    return -ADAM_LR * (m_hat / (jnp.sqrt(v_hat) + ADAM_EPS) + ADAM_WD * w), m_new, v_new


def _adamw(w, g, m, v, *, name):
    def body(w_ref, g_ref, m_ref, v_ref, d_ref, nm_ref, nv_ref):
        d_ref[...], nm_ref[...], nv_ref[...] = _adamw_update(w_ref[...], g_ref[...], m_ref[...], v_ref[...])

    return pl.pallas_call(
        body, name=name, out_shape=[jax.ShapeDtypeStruct(w.shape, F32)] * 3,
        compiler_params=pltpu.CompilerParams(vmem_limit_bytes=VMEM_LIMIT_BYTES),
    )(w, g, m, v)


def _adamw_slots(w, parts, m, v, *, name):
    R, C = w.shape
    tr = R if R <= 512 else _pick(R, (256, 128))

    def body(w_ref, p_ref, m_ref, v_ref, g_ref, d_ref, nm_ref, nv_ref):
        g = p_ref[0].astype(F32)
        for j in range(1, N_DEV):
            g = g + p_ref[j].astype(F32)
        g_ref[...] = g
        d_ref[...], nm_ref[...], nv_ref[...] = _adamw_update(w_ref[...], g, m_ref[...], v_ref[...])

    spec = pl.BlockSpec((tr, C), lambda i: (i, 0))
    return pl.pallas_call(
        body, name=name, grid=(R // tr,),
        in_specs=[spec, pl.BlockSpec((N_DEV, tr, C), lambda i: (0, i, 0)), spec, spec], out_specs=[spec] * 4,
        out_shape=[jax.ShapeDtypeStruct((R, C), F32)] * 4, compiler_params=_cparams("parallel"),
    )(w, parts, m, v)


MESH_ID = pl.DeviceIdType.MESH
HBM_SPEC = pl.BlockSpec(memory_space=pltpu.HBM)
VMEM_SPEC = pl.BlockSpec(memory_space=pltpu.VMEM)


def _mesh_place():
    x, y, c = lax.axis_index("x"), lax.axis_index("y"), lax.axis_index("c")
    return x, y, c, 4 * x + 2 * y + c


def _all_gather_hbm(blocks, *, name):
    n = len(blocks)

    def body(*refs):
        x_refs, out_refs = refs[:n], refs[n:2 * n]
        send_sems, recv_sems, local_sems = refs[2 * n:]
        x, y, c, _ = _mesh_place()
        me, sibling = (x, y, c), (x, y, 1 - c)
        chips = [(1 - x, y), (x, 1 - y), (1 - x, 1 - y)]

        def copy(w, k, blk, to, src=None):
            px, py, pc = blk
            slot = out_refs[w].at[4 * px + 2 * py + pc]
            return pltpu.make_async_remote_copy(
                src_ref=slot if src is None else src, dst_ref=slot,
                send_sem=send_sems.at[w, k], recv_sem=recv_sems.at[w, k], device_id=to, device_id_type=MESH_ID)

        mine = [pltpu.make_async_copy(x_refs[w], out_refs[w].at[4 * x + 2 * y + c], local_sems.at[w])
                for w in range(n)]
        for cp in mine:
            cp.start()
        first = []
        for w in range(n):
            first.append(copy(w, 0, me, sibling, src=x_refs[w]))
            first += [copy(w, 1 + j, me, (*chip, c), src=x_refs[w]) for j, chip in enumerate(chips)]
        for cp in first:
            cp.start()
        passed = []
        for j, chip in enumerate(chips):
            for w in range(n):
                copy(w, 1 + j, (*chip, c), me).wait_recv()
                fwd = copy(w, 4 + j, (*chip, c), sibling)
                fwd.start()
                passed.append(fwd)
        for w in range(n):
            copy(w, 0, sibling, me).wait_recv()
            for j, chip in enumerate(chips):
                copy(w, 4 + j, (*chip, 1 - c), me).wait_recv()
        for cp in first + passed:
            cp.wait_send()
        for cp in mine:
            cp.wait()

    return pl.pallas_call(
        body, name=name, out_shape=[jax.ShapeDtypeStruct((N_DEV,) + b.shape, b.dtype) for b in blocks],
        in_specs=[HBM_SPEC] * n, out_specs=[HBM_SPEC] * n,
        scratch_shapes=[pltpu.SemaphoreType.DMA((n, 7)), pltpu.SemaphoreType.DMA((n, 7)),
                        pltpu.SemaphoreType.DMA((n,))],
    )(*blocks)


def _peers(x, y, c):
    out = []
    for mask in range(1, N_DEV):
        dx, dy, dc = (mask >> 2) & 1, (mask >> 1) & 1, mask & 1
        px, py, pc = (1 - x if dx else x), (1 - y if dy else y), (1 - c if dc else c)
        out.append(((px, py, pc), 4 * px + 2 * py + pc))
    return out


def _all_to_all_hbm(parts, *, name):
    n = len(parts)

    def body(*refs):
        p_refs, out_refs = refs[:n], refs[n:2 * n]
        send_sems, recv_sems, local_sems = refs[2 * n:]
        x, y, c, me = _mesh_place()
        mine = [pltpu.make_async_copy(p_refs[w].at[me], out_refs[w].at[me], local_sems.at[w]) for w in range(n)]
        for cp in mine:
            cp.start()
        copies = []
        for k, (peer, peer_idx) in enumerate(_peers(x, y, c)):
            for w in range(n):
                copies.append(pltpu.make_async_remote_copy(
                    src_ref=p_refs[w].at[peer_idx], dst_ref=out_refs[w].at[me], send_sem=send_sems.at[w, k],
                    recv_sem=recv_sems.at[w, k], device_id=peer, device_id_type=MESH_ID))
        for cp in copies:
            cp.start()
        for cp in copies:
            cp.wait_recv()
        for cp in copies:
            cp.wait_send()
        for cp in mine:
            cp.wait()

    return pl.pallas_call(
        body, name=name, out_shape=[jax.ShapeDtypeStruct(p.shape, p.dtype) for p in parts],
        in_specs=[HBM_SPEC] * n, out_specs=[HBM_SPEC] * n,
        scratch_shapes=[pltpu.SemaphoreType.DMA((n, 7)), pltpu.SemaphoreType.DMA((n, 7)),
                        pltpu.SemaphoreType.DMA((n,))],
    )(*parts)


def _all_reduce_small(block, *, name):
    R, W = block.shape

    def body(x_ref, out_ref, buf, send_sems, recv_sems):
        x, y, c, me = _mesh_place()
        copies = []
        for k, (peer, _) in enumerate(_peers(x, y, c)):
            copies.append(pltpu.make_async_remote_copy(
                src_ref=x_ref, dst_ref=buf.at[me], send_sem=send_sems.at[k], recv_sem=recv_sems.at[k],
                device_id=peer, device_id_type=MESH_ID))
        for cp in copies:
            cp.start()
        buf[me] = x_ref[...]
        for cp in copies:
            cp.wait_recv()
        for cp in copies:
            cp.wait_send()
        acc = buf[0]
        for j in range(1, N_DEV):
            acc = acc + buf[j]
        out_ref[...] = acc

    return pl.pallas_call(
        body, name=name, out_shape=jax.ShapeDtypeStruct((R, W), F32),
        in_specs=[VMEM_SPEC], out_specs=VMEM_SPEC,
        scratch_shapes=[pltpu.VMEM((N_DEV, R, W), F32), pltpu.SemaphoreType.DMA((7,)), pltpu.SemaphoreType.DMA((7,))],
    )(block)


def _assemble(name, g):
    if BIG_AXIS[name] == 2:
        return jnp.concatenate([g[j] for j in range(N_DEV)], axis=2)
    _, n0, a, b = g.shape
    return g.transpose(1, 0, 2, 3).reshape(n0, N_DEV * a, b)


def _split_for_devices(name, g, shard_shape):
    n0, a, b = shard_shape
    if BIG_AXIS[name] == 2:
        return jnp.stack([g[:, :, j * b:(j + 1) * b] for j in range(N_DEV)]).astype(BF16)
    return g.reshape(n0, N_DEV, a, b).transpose(1, 0, 2, 3).astype(BF16)


PACK_ROWS = 8


def _pack_small(vals):
    flat = jnp.concatenate([vals[n].reshape(-1).astype(F32) for n in SMALL])
    pad = (-flat.shape[0]) % (PACK_ROWS * LANES)
    return jnp.pad(flat, (0, pad)).reshape(-1, LANES)


def _unpack_small(block, shapes):
    flat = block.reshape(-1)
    out, off = {}, 0
    for n in SMALL:
        sz = math.prod(shapes[n])
        out[n] = flat[off:off + sz].reshape(shapes[n])
        off += sz
    return out


def _rope_tables(S):
    half = MLA_ROPE // 2
    inv = 1.0 / (ROPE_THETA ** (jnp.arange(0, MLA_ROPE, 2, dtype=F32) / MLA_ROPE))
    ang = jnp.arange(S, dtype=F32)[:, None] * inv[None, :]
    cos, sin = jnp.cos(ang), jnp.sin(ang)
    zeros = jnp.zeros((S, half), F32)
    tail = jnp.zeros((S, LANES - MLA_QK), F32)

    def block(rope_part, nope_val):
        return jnp.concatenate([jnp.full((S, MLA_NOPE), nope_val, F32), rope_part, tail], -1)

    a_r = jnp.concatenate([cos, cos], -1)
    bm_r = jnp.concatenate([-sin, zeros], -1)
    bp_r = jnp.concatenate([zeros, sin], -1)
    q_tabs = tuple(jnp.tile(block(r, v), (1, MLA_HEADS)) for r, v in ((a_r, 1.0), (bm_r, 0.0), (bp_r, 0.0)))
    k_tabs = tuple(block(r, 0.0) for r in (a_r, bm_r, bp_r))
    return q_tabs, k_tabs


def _t5_bucket(dist):
    exact = REL_BUCKETS // 2
    d = jnp.maximum(dist, 1).astype(F32)
    large = exact + (jnp.log(d / exact) / math.log(REL_MAX_DIST / exact) * (REL_BUCKETS - exact)).astype(jnp.int32)
    large = jnp.minimum(large, REL_BUCKETS - 1)
    return jnp.where(dist < exact, dist, large)


def _swa_bucket_table():
    a = jnp.arange(BLOCK_Q)[:, None]
    col = jnp.arange(2 * BLOCK_Q)[None, :]
    return _t5_bucket(jnp.maximum(a + BLOCK_Q - col, 0)).astype(jnp.int32)


def _even_weights(W, j):
    w = W['ev_w_in'][j]
    c_kv1 = MLA_Q_LORA + MLA_KV_LORA
    c_kr1 = c_kv1 + MLA_ROPE
    c_qs1 = c_kr1 + SWA_HEADS * HEAD_DIM
    zeros = lambda n: jnp.zeros((D_MODEL, n), w.dtype)
    w_in = jnp.concatenate([w[:, c_kr1:c_qs1], w[:, :c_kv1], w[:, c_qs1:], zeros(KR_LANE0), w[:, c_kv1:c_kr1],
                            zeros(LANES - KR_LANE0 - MLA_ROPE)], axis=1)
    uq = W['ev_w_uq'][j].reshape(MLA_Q_LORA, MLA_HEADS, MLA_QK)
    w_uq = jnp.pad(uq, ((0, 0), (0, 0), (0, LANES - MLA_QK))).reshape(MLA_Q_LORA, MLA_HEADS * LANES)
    ukv = W['ev_w_ukv'][j].reshape(MLA_KV_LORA, MLA_HEADS, MLA_NOPE + MLA_V)
    w_k = jnp.pad(ukv[..., :MLA_NOPE], ((0, 0), (0, 0), (0, LANES - MLA_NOPE))).reshape(MLA_KV_LORA, -1)
    w_v = ukv[..., MLA_NOPE:].reshape(MLA_KV_LORA, MLA_HEADS * MLA_V)
    return w_in, w_uq, w_k, w_v, W['ev_w_out'][j]


def _even_in_grad_unpad(dw):
    kr0 = EV_KR[0] + KR_LANE0
    return jnp.concatenate([dw[:, EV_CQ[0]:EV_CKV[1]], dw[:, kr0:kr0 + MLA_ROPE], dw[:, EV_QS[0]:EV_QS[1]],
                            dw[:, EV_KS[0]:EV_VS[1]]], axis=1)


def _even_fwd(xb, W, P, j, B, S, tabs, tag):
    q_tabs, k_tabs, bias, sinkcol = tabs
    w_in, w_uq, w_k, w_v, w_out = _even_weights(W, j)
    h = _mm(xb, w_in, name=f"{tag}_in")
    cqn, ckvn, rq, rkv = _even_norms(h, P['ev_q_norm'][j][None], P['ev_kv_norm'][j][None], name=f"{tag}_norms")
    q = _rope(_mm(cqn, w_uq, name=f"{tag}_uq"), q_tabs, S, sign=1.0, name=f"{tag}_ropeq")
    knp = _mm(ckvn, w_k, out_dtypes=(BF16,), name=f"{tag}_uk")
    v = _mm(ckvn, w_v, out_dtypes=(BF16,), name=f"{tag}_uv")
    k = _mla_keys(knp, h, k_tabs, S, name=f"{tag}_keys")
    o_mla, lse_mla = _flash_fwd(q, k, v, q_blk0=0, k_blk0=0, v_blk0=0, W=2 * LANES, n_pairs=MLA_HEADS // 2, B=B, S=S,
                                scale=MLA_QK ** -0.5, name=f"{tag}_mla")
    o_swa, lse_swa = _swa_fwd(h, bias, sinkcol, B=B, S=S, name=f"{tag}_swa")
    o_cat = jnp.concatenate([o_mla, o_swa], axis=-1)
    m = _mm(o_cat, w_out, name=f"{tag}_out")
    res = dict(h=h, cqn=cqn, ckvn=ckvn, rq=rq, rkv=rkv, q=q, k=k, v=v, o_mla=o_mla, lse_mla=lse_mla,
               o_swa=o_swa, lse_swa=lse_swa, o_cat=o_cat)
    return m, res


def _shift_prev(own, prev, B, S):
    prev = prev.reshape(B, S, LANES)
    shifted = jnp.concatenate([prev[:, BLOCK_Q:], jnp.zeros_like(prev[:, :BLOCK_Q])], axis=1)
    return (own + shifted.reshape(B * S, LANES)).astype(BF16)


def _even_bwd(dmb, dz1, xb, W, P, j, B, S, tabs, res, tag):
    q_tabs, k_tabs, bias, sinkcol = tabs
    w_in, w_uq, w_k, w_v, w_out = _even_weights(W, j)
    g = {}
    g['ev_w_out'] = _mm_tn(res['o_cat'], dmb, name=f"{tag}_dwout")
    do = _mm(dmb, w_out, trans_b=True, out_dtypes=(BF16,), name=f"{tag}_do")
    dq, dk, dv = _flash_bwd(res['q'], res['k'], res['v'], res['o_mla'], do, res['lse_mla'], q_blk0=0, k_blk0=0,
                            v_blk0=0, do_blk0=0, W=2 * LANES, n_pairs=MLA_HEADS // 2, B=B, S=S,
                            scale=MLA_QK ** -0.5, qk_dtype=F32, name=f"{tag}_mla_bwd")
    dq_pre = _rope(dq, q_tabs, S, sign=-1.0, name=f"{tag}_ropeq_bwd")
    dw_uq = _mm_tn(res['cqn'], dq_pre, name=f"{tag}_dwuq")
    g['ev_w_uq'] = dw_uq.reshape(MLA_Q_LORA, MLA_HEADS, LANES)[..., :MLA_QK].reshape(MLA_Q_LORA, MLA_HEADS * MLA_QK)
    dcqn = _mm(dq_pre, w_uq, trans_b=True, name=f"{tag}_dcqn")
    dw_k = _mm_tn(res['ckvn'], dk, name=f"{tag}_dwuk").reshape(MLA_KV_LORA, MLA_HEADS, LANES)[..., :MLA_NOPE]
    dw_v = _mm_tn(res['ckvn'], dv, name=f"{tag}_dwuv").reshape(MLA_KV_LORA, MLA_HEADS, MLA_V)
    g['ev_w_ukv'] = jnp.concatenate([dw_k, dw_v], axis=-1).reshape(MLA_KV_LORA, MLA_HEADS * (MLA_NOPE + MLA_V))
    dckvn_v = _mm(dv, w_v, trans_b=True, name=f"{tag}_dckvn_v")
    dckvn = _mm(dk, w_k, trans_b=True, extras=(dckvn_v,), epilogue=lambda acc, r: (acc + r,), name=f"{tag}_dckvn")
    dkr_pre = _mla_rope_key_grad(dk, k_tabs, S, name=f"{tag}_ropek_bwd")
    dqs, dko, dkp, dvo, dvp, dbias, dsink = _swa_bwd(res['h'], res['o_swa'], do, res['lse_swa'], bias, sinkcol,
                                                     do_blk0=1, B=B, S=S, name=f"{tag}_swa_bwd")
    dh, dgq, dgkv = _even_in_bwd(res['h'], res['rq'], res['rkv'], P['ev_q_norm'][j][None], P['ev_kv_norm'][j][None],
                                 dcqn, dckvn, dqs, _shift_prev(dko, dkp, B, S), _shift_prev(dvo, dvp, B, S), dkr_pre,
                                 name=f"{tag}_in_bwd")
    g['ev_w_in'] = _even_in_grad_unpad(_mm_tn(xb, dh, name=f"{tag}_dwin"))
    dx = _mm(dh, w_in, trans_b=True, extras=(dz1,), epilogue=lambda acc, r: (acc + DN_ALPHA * r,), name=f"{tag}_dx")
    small = dict(ev_q_norm=dgq[0], ev_kv_norm=dgkv[0], dbias=dbias, ev_sinks=jnp.sum(dsink, axis=(1, 2)))
    return dx, g, small


def _odd_fwd(xb, W, P, j, B, S, tag):
    w = W['od_w_in'][j]
    w_qkv = w[:, :ODD_QKV]
    w_f = jnp.pad(w[:, ODD_QKV:], ((0, 0), (0, LANES - FOX_HEADS)))
    bf = jnp.pad(P['od_b_f'][j], (0, LANES - FOX_HEADS))[None]
    qkv = _mm(xb, w_qkv, out_dtypes=(BF16,), name=f"{tag}_qkv")
    f = _mm(xb, w_f, name=f"{tag}_f").reshape(B, S, LANES)
    csh, chs = _fox_decay_fwd(f, bf, name=f"{tag}_decay")
    crow = chs[:, :FOX_HEADS].reshape(B, FOX_HEADS, S // ATT_TILE, 1, ATT_TILE)
    n_blk = FOX_HEADS * HEAD_DIM // LANES
    o, lse = _flash_fwd(qkv, qkv, qkv, q_blk0=0, k_blk0=n_blk, v_blk0=2 * n_blk, W=LANES, n_pairs=FOX_HEADS // 2,
                        B=B, S=S, scale=HEAD_DIM ** -0.5, csh=csh, crow=crow, name=f"{tag}_fox")
    m = _mm(o, W['od_w_out'][j], name=f"{tag}_out")
    res = dict(f=f, bf=bf, csh=csh, crow=crow, qkv=qkv, o=o, lse=lse, w_qkv=w_qkv, w_f=w_f)
    return m, res


def _odd_bwd(dmb, dz1, xb, W, P, j, B, S, res, tag):
    g = {}
    w_out = W['od_w_out'][j]
    g['od_w_out'] = _mm_tn(res['o'], dmb, name=f"{tag}_dwout")
    do = _mm(dmb, w_out, trans_b=True, out_dtypes=(BF16,), name=f"{tag}_do")
    qkv = res['qkv']
    n_blk = FOX_HEADS * HEAD_DIM // LANES
    dq, dk, dv, dck, dcq = _flash_bwd(qkv, qkv, qkv, res['o'], do, res['lse'], q_blk0=0, k_blk0=n_blk,
                                      v_blk0=2 * n_blk, do_blk0=0, W=LANES, n_pairs=FOX_HEADS // 2, B=B, S=S,
                                      scale=HEAD_DIM ** -0.5, qk_dtype=BF16, csh=res['csh'], crow=res['crow'],
                                      name=f"{tag}_fox_bwd")
    dc = dck.reshape(B, FOX_HEADS, S) + dcq.reshape(B, FOX_HEADS, S)
    dc_hs = jnp.pad(dc, ((0, 0), (0, LANES - FOX_HEADS), (0, 0)))
    df, dbf = _fox_decay_bwd(dc_hs, res['f'], res['bf'], name=f"{tag}_decay_bwd")
    df = df.reshape(B * S, LANES)
    dqkv = jnp.concatenate([dq, dk, dv], axis=-1)
    dw_qkv = _mm_tn(xb, dqkv, name=f"{tag}_dwqkv")
    dw_f = _mm_tn(xb, df, name=f"{tag}_dwf")
    g['od_w_in'] = jnp.concatenate([dw_qkv, dw_f[:, :FOX_HEADS]], axis=1)
    dxf = _mm(df, res['w_f'], trans_b=True, extras=(dz1,), epilogue=lambda acc, r: (acc + DN_ALPHA * r,),
              name=f"{tag}_dxf")
    dx = _mm(dqkv, res['w_qkv'], trans_b=True, extras=(dxf,), epilogue=lambda acc, r: (acc + r,), name=f"{tag}_dx")
    small = dict(od_b_f=dbf[0, :FOX_HEADS])
    return dx, g, small


def _local_step(x, p, target, W, P):
    B, S, D = x.shape
    T = B * S
    q_tabs, k_tabs = _rope_tables(S)
    bucket = _swa_bucket_table()
    bias = P['rel_bias'][bucket].astype(F32).transpose(2, 0, 1)

    xc = x.reshape(T, D)
    xcb = xc.astype(BF16)
    saved = []
    for i in range(DEPTH):
        j = i // 2
        tag = f"l{i}"
        lay = dict(xb=xcb)
        if i % 2 == 0:
            sinkcol = jnp.broadcast_to(P['ev_sinks'][j][:, None, None], (SWA_HEADS, BLOCK_Q, 1)).astype(F32)
            lay['tabs'] = (q_tabs, k_tabs, bias, sinkcol)
            m, lay['mix'] = _even_fwd(xcb, W, P, j, B, S, lay['tabs'], tag)
        else:
            m, lay['mix'] = _odd_fwd(xcb, W, P, j, B, S, tag)
        x1, x1b, lay['xh1'], lay['r1'] = _ln_fwd(xc, m, P['ln1_g'][i][None], P['ln1_b'][i][None], name=f"{tag}_ln1")
        lay['x1b'] = x1b
        lay['u'], lay['a'] = _mm(x1b, W['w_up'][i], out_dtypes=(F32, BF16),
                                 epilogue=lambda acc: (acc, jnp.square(jnp.maximum(acc, 0.0))), name=f"{tag}_up")
        d = _mm(lay['a'], W['w_down'][i], name=f"{tag}_down")
        x2, x2b, lay['xh2'], lay['r2'] = _ln_fwd(x1, d, P['ln2_g'][i][None], P['ln2_b'][i][None], name=f"{tag}_ln2")
        lay['x2b'] = x2b
        lay['p'] = p[i].reshape(T, D_PLE)
        lay['e'] = _mm(lay['p'], W['ple_w_proj'][i], name=f"{tag}_ple_proj")

        def gate(acc, bg, e, x2v):
            gv = 1.0 / (1.0 + jnp.exp(-(acc + bg)))
            y = x2v + gv * e
            return y, y, gv

        xc, xcb, lay['g'] = _mm(x2b, W['ple_w_gate'][i], extras=(P['ple_b_gate'][i][None], lay['e'], x2),
                                epilogue=gate, out_dtypes=(F32, BF16, F32), name=f"{tag}_ple_gate")
        saved.append(lay)

    dy, sq = _loss_grad(xc, target.reshape(T, D), name="loss")

    G = {n: [None] * (DEPTH if n in ('w_up', 'w_down', 'ple_w_proj', 'ple_w_gate') else DEPTH // 2) for n in BIG}
    Gs = {n: [None] * DEPTH for n in ('ln1_g', 'ln1_b', 'ln2_g', 'ln2_b', 'ple_b_gate')}
    Gs.update({n: [None] * (DEPTH // 2) for n in ('ev_q_norm', 'ev_kv_norm', 'ev_sinks', 'od_b_f')})
    dbias_total = None
    for i in reversed(range(DEPTH)):
        j = i // 2
        tag = f"l{i}b"
        lay = saved[i]
        de, dzg, dbg = _ple_bwd_elem(dy, lay['g'], lay['e'], name=f"{tag}_ple_elem")
        Gs['ple_b_gate'][i] = dbg[0]
        G['ple_w_proj'][i] = _mm_tn(lay['p'], de, name=f"{tag}_dwproj")
        G['ple_w_gate'][i] = _mm_tn(lay['x2b'], dzg, name=f"{tag}_dwgate")
        dx2 = _mm(dzg, W['ple_w_gate'][i], trans_b=True, extras=(dy,), epilogue=lambda acc, r: (acc + r,),
                  name=f"{tag}_dx2")
        dz2, dz2b, dg2, db2 = _ln_bwd(dx2, lay['xh2'], lay['r2'], P['ln2_g'][i][None], name=f"{tag}_ln2")
        Gs['ln2_g'][i], Gs['ln2_b'][i] = dg2[0], db2[0]
        G['w_down'][i] = _mm_tn(lay['a'], dz2b, name=f"{tag}_dwdown")
        du = _mm(dz2b, W['w_down'][i], trans_b=True, extras=(lay['u'],), out_dtypes=(BF16,),
                 epilogue=lambda acc, u: (acc * (2.0 * jnp.maximum(u, 0.0)),), name=f"{tag}_du")
        G['w_up'][i] = _mm_tn(lay['x1b'], du, name=f"{tag}_dwup")
        dx1 = _mm(du, W['w_up'][i], trans_b=True, extras=(dz2,), epilogue=lambda acc, r: (acc + DN_ALPHA * r,),
                  name=f"{tag}_dx1")
        dz1, dz1b, dg1, db1 = _ln_bwd(dx1, lay['xh1'], lay['r1'], P['ln1_g'][i][None], name=f"{tag}_ln1")
        Gs['ln1_g'][i], Gs['ln1_b'][i] = dg1[0], db1[0]
        if i % 2 == 0:
            dy, g, small = _even_bwd(dz1b, dz1, lay['xb'], W, P, j, B, S, lay['tabs'], lay['mix'], tag)
            dbias_total = small['dbias'] if dbias_total is None else dbias_total + small['dbias']
            for n in ('ev_q_norm', 'ev_kv_norm', 'ev_sinks'):
                Gs[n][j] = small[n]
        else:
            dy, g, small = _odd_bwd(dz1b, dz1, lay['xb'], W, P, j, B, S, lay['mix'], tag)
            Gs['od_b_f'][j] = small['od_b_f']
        for n, val in g.items():
            G[n][j] = val

    grads_big = {n: jnp.stack(G[n]) for n in BIG}
    grads_small = {n: jnp.stack(v) for n, v in Gs.items()}
    drel = _bias_bucket_sum(dbias_total, bucket, name="rel_bias_grad")
    grads_small['rel_bias'] = drel[:, :REL_BUCKETS].T
    return sq, dy.reshape(B, S, D), grads_big, grads_small


def kernel(x, p, rel_bias, ev_w_in, ev_q_norm, ev_w_uq, ev_kv_norm, ev_w_ukv, ev_sinks, ev_w_out, od_w_in, od_b_f, od_w_out, ln1_g, ln1_b, w_up, w_down, ln2_g, ln2_b, ple_w_proj, ple_w_gate, ple_b_gate, loss_target, m_rel_bias, m_ev_w_in, m_ev_q_norm, m_ev_w_uq, m_ev_kv_norm, m_ev_w_ukv, m_ev_sinks, m_ev_w_out, m_od_w_in, m_od_b_f, m_od_w_out, m_ln1_g, m_ln1_b, m_w_up, m_w_down, m_ln2_g, m_ln2_b, m_ple_w_proj, m_ple_w_gate, m_ple_b_gate, v_rel_bias, v_ev_w_in, v_ev_q_norm, v_ev_w_uq, v_ev_kv_norm, v_ev_w_ukv, v_ev_sinks, v_ev_w_out, v_od_w_in, v_od_b_f, v_od_w_out, v_ln1_g, v_ln1_b, v_w_up, v_w_down, v_ln2_g, v_ln2_b, v_ple_w_proj, v_ple_w_gate, v_ple_b_gate):
    given = dict(locals())
    w = {n: given[n] for n in WEIGHTS}
    mom = {n: given["m_" + n] for n in WEIGHTS}
    var = {n: given["v_" + n] for n in WEIGHTS}
    shard_shapes = {n: w[n].shape for n in BIG}
    small_shapes = {n: w[n].shape for n in SMALL}

    gathered = _all_gather_hbm([w[n].astype(BF16) for n in BIG], name="gather_weights")
    W = {n: _assemble(n, g) for n, g in zip(BIG, gathered)}
    P = {n: w[n] for n in SMALL}

    sq, grad_x, grads_big, grads_small = _local_step(x, p, loss_target, W, P)
    loss = lax.psum(0.5 * jnp.sum(sq) / D_MODEL, ("x", "y", "c"))

    received = _all_to_all_hbm([_split_for_devices(n, grads_big[n], shard_shapes[n]) for n in BIG],
                               name="exchange_grads")
    g_small_packed = _all_reduce_small(_pack_small(grads_small), name="reduce_small_grads")
    g_small = _unpack_small(g_small_packed, small_shapes)

    grad, delta, new_m, new_v = {}, {}, {}, {}
    for n, parts in zip(BIG, received):
        shp = shard_shapes[n]
        two_d = (shp[0] * shp[1], shp[2])
        outs = _adamw_slots(w[n].reshape(two_d), parts.reshape((N_DEV,) + two_d), mom[n].reshape(two_d),
                            var[n].reshape(two_d), name=f"adamw_{n}")
        grad[n], delta[n], new_m[n], new_v[n] = (t.reshape(shp) for t in outs)
    d, nm, nv = _adamw(_pack_small(w), g_small_packed, _pack_small(mom), _pack_small(var), name="adamw_small")
    d, nm, nv = (_unpack_small(t, small_shapes) for t in (d, nm, nv))
    for n in SMALL:
        grad[n], delta[n], new_m[n], new_v[n] = g_small[n], d[n], nm[n], nv[n]

    return (loss, grad_x, *[grad[n] for n in WEIGHTS], *[delta[n] for n in WEIGHTS],
            *[new_m[n] for n in WEIGHTS], *[new_v[n] for n in WEIGHTS])
```

```python
import math

import jax
import jax.numpy as jnp
from jax import lax
from jax.experimental import pallas as pl
from jax.experimental.pallas import tpu as pltpu

F32, BF16 = jnp.float32, jnp.bfloat16

D_MODEL = 1024
DEPTH = 4
HEAD_DIM = 64
MLA_HEADS, MLA_NOPE, MLA_ROPE, MLA_V = 8, 64, 32, 64
MLA_Q_LORA, MLA_KV_LORA = 384, 256
MLA_QK = MLA_NOPE + MLA_ROPE
ROPE_THETA = 10000.0
SWA_HEADS, SWA_KV_HEADS, SWA_WINDOW = 8, 2, 128
REL_BUCKETS, REL_MAX_DIST = 32, 128
FOX_HEADS = 16
D_FF = 4 * D_MODEL
D_PLE = 256
BLOCK_Q = 128
DN_ALPHA = (2 * DEPTH) ** 0.25
NORM_EPS = 1e-5
NEG_INF = -1e30
EVEN_IN = 1440
ODD_QKV = 3 * FOX_HEADS * HEAD_DIM
LANES = 128

EV_QS = (0, 512)
EV_CQ = (512, 896)
EV_CKV = (896, 1152)
EV_KS = (1152, 1280)
EV_VS = (1280, 1408)
EV_KR = (1408, 1536)
EVEN_IN_PAD = 1536
KR_LANE0 = MLA_NOPE

ADAM_LR, ADAM_B1, ADAM_B2, ADAM_EPS, ADAM_WD, ADAM_STEP = 0.001, 0.9, 0.999, 1e-08, 0.01, 10

N_DEV = 8
VMEM_LIMIT_BYTES = 48 * 1024 * 1024
ATT_TILE = 256
PAIRS_PER_STEP_FWD = 4
PAIRS_PER_STEP_BWD = 2

NN = (((1,), (0,)), ((), ()))
NT = (((1,), (1,)), ((), ()))
TN = (((0,), (0,)), ((), ()))

BIG = ['ev_w_in', 'ev_w_uq', 'ev_w_ukv', 'ev_w_out', 'od_w_in', 'od_w_out', 'w_up', 'w_down',
       'ple_w_proj', 'ple_w_gate']
BIG_AXIS = {'ev_w_in': 2, 'ev_w_uq': 2, 'ev_w_ukv': 2, 'ev_w_out': 1, 'od_w_in': 2, 'od_w_out': 1,
            'w_up': 2, 'w_down': 1, 'ple_w_proj': 2, 'ple_w_gate': 1}
SMALL = ['rel_bias', 'ev_q_norm', 'ev_kv_norm', 'ev_sinks', 'od_b_f', 'ln1_g', 'ln1_b', 'ln2_g', 'ln2_b',
         'ple_b_gate']
WEIGHTS = ['rel_bias', 'ev_w_in', 'ev_q_norm', 'ev_w_uq', 'ev_kv_norm', 'ev_w_ukv', 'ev_sinks', 'ev_w_out',
           'od_w_in', 'od_b_f', 'od_w_out', 'ln1_g', 'ln1_b', 'w_up', 'w_down', 'ln2_g', 'ln2_b',
           'ple_w_proj', 'ple_w_gate', 'ple_b_gate']


def _cparams(*sem):
    return pltpu.CompilerParams(dimension_semantics=sem, vmem_limit_bytes=VMEM_LIMIT_BYTES)


def _pick(n, cands):
    for c in cands:
        if n % c == 0:
            return c
    return n


MM_STEP_BYTES = 10 * 1024 * 1024
MM_OUT_BYTES = 8 * 1024 * 1024
MM_CHUNK = 512


def _mm(a, b, *, trans_b=False, extras=(), epilogue=None, out_dtypes=(F32,), name):
    M, K = a.shape
    N = b.shape[0] if trans_b else b.shape[1]
    n_ex, n_out = len(extras), len(out_dtypes)
    row_bytes = K * a.dtype.itemsize + N * (sum(jnp.dtype(d).itemsize for d in out_dtypes)
                                            + sum(e.dtype.itemsize for e in extras if e.shape[0] == M))
    tm = next((c for c in (1024, 512, 256) if M % c == 0 and c * row_bytes <= MM_STEP_BYTES), 128)
    nc = _pick(N, (MM_CHUNK, 384, 256, 128))

    def body(*refs):
        a_ref, b_ref = refs[:2]
        ex = refs[2:2 + n_ex]
        outs = refs[2 + n_ex:]
        av = a_ref[...].astype(BF16)
        for n0 in range(0, N, nc):
            cols = slice(n0, n0 + nc)
            bv = (b_ref[cols, :] if trans_b else b_ref[:, cols]).astype(BF16)
            acc = lax.dot_general(av, bv, NT if trans_b else NN, preferred_element_type=F32)
            res = epilogue(acc, *[e[:, cols] for e in ex]) if epilogue is not None else (acc,)
            for o, r in zip(outs, res):
                o[:, cols] = r.astype(o.dtype)

    in_specs = [pl.BlockSpec((tm, K), lambda i: (i, 0)), pl.BlockSpec(b.shape, lambda i: (0, 0))]
    for e in extras:
        if e.shape == (M, N):
            in_specs.append(pl.BlockSpec((tm, N), lambda i: (i, 0)))
        elif e.shape == (1, N):
            in_specs.append(pl.BlockSpec((1, N), lambda i: (0, 0)))
        else:
            raise ValueError(f"extra operand of shape {e.shape} for a ({M}, {N}) result")
    res = pl.pallas_call(
        body, name=name, grid=(M // tm,), in_specs=in_specs,
        out_specs=[pl.BlockSpec((tm, N), lambda i: (i, 0)) for _ in out_dtypes],
        out_shape=[jax.ShapeDtypeStruct((M, N), d) for d in out_dtypes],
        compiler_params=_cparams("parallel"),
    )(a, b, *extras)
    return res[0] if n_out == 1 else tuple(res)


def _mm_tn(a, b, *, name):
    T, K = a.shape
    N = b.shape[1]
    bk, bn = K, N
    while bk * bn * 4 > MM_OUT_BYTES:
        if bn >= bk and bn % (2 * LANES) == 0:
            bn //= 2
        else:
            bk //= 2
    tt = _pick(T, (1024, 512, 256))
    ck, cn = _pick(bk, (MM_CHUNK, 384, 256, 128)), _pick(bn, (MM_CHUNK, 384, 256, 128))

    def body(a_ref, b_ref, o_ref):
        t = pl.program_id(2)

        @pl.when(t == 0)
        def _():
            o_ref[...] = jnp.zeros_like(o_ref)

        for r0 in range(0, bk, ck):
            av = a_ref[:, r0:r0 + ck].astype(BF16)
            for c0 in range(0, bn, cn):
                o_ref[r0:r0 + ck, c0:c0 + cn] += lax.dot_general(
                    av, b_ref[:, c0:c0 + cn].astype(BF16), TN, preferred_element_type=F32)

    return pl.pallas_call(
        body, name=name, grid=(K // bk, N // bn, T // tt),
        in_specs=[pl.BlockSpec((tt, bk), lambda i, j, t: (t, i)), pl.BlockSpec((tt, bn), lambda i, j, t: (t, j))],
        out_specs=pl.BlockSpec((bk, bn), lambda i, j, t: (i, j)),
        out_shape=jax.ShapeDtypeStruct((K, N), F32),
        compiler_params=_cparams("parallel", "parallel", "arbitrary"),
    )(a, b)


ROW_TILE = 256


def _row_spec(cols, col_block=0):
    return pl.BlockSpec((ROW_TILE, cols), lambda i: (i, col_block))


def _tab_spec(cols, period):
    return pl.BlockSpec((ROW_TILE, cols), lambda i: (i % period, 0))


def _full_spec(shape):
    return pl.BlockSpec(shape, lambda i: (0,) * len(shape))


def _ln_fwd(x, m, g, b, *, name):
    T, D = x.shape

    def body(x_ref, m_ref, g_ref, b_ref, y_ref, yb_ref, xh_ref, r_ref):
        z = DN_ALPHA * x_ref[...] + m_ref[...]
        mu = jnp.mean(z, -1, keepdims=True)
        zc = z - mu
        r = lax.rsqrt(jnp.mean(zc * zc, -1, keepdims=True) + NORM_EPS)
        xh = zc * r
        y = xh * g_ref[...] + b_ref[...]
        y_ref[...] = y
        yb_ref[...] = y.astype(BF16)
        xh_ref[...] = xh
        r_ref[...] = r

    return pl.pallas_call(
        body, name=name, grid=(T // ROW_TILE,),
        in_specs=[_row_spec(D), _row_spec(D), _full_spec((1, D)), _full_spec((1, D))],
        out_specs=[_row_spec(D), _row_spec(D), _row_spec(D), _row_spec(1)],
        out_shape=[jax.ShapeDtypeStruct((T, D), F32), jax.ShapeDtypeStruct((T, D), BF16),
                   jax.ShapeDtypeStruct((T, D), F32), jax.ShapeDtypeStruct((T, 1), F32)],
        compiler_params=_cparams("parallel"),
    )(x, m, g, b)


def _ln_bwd(dy, xh, r, g, *, name):
    T, D = dy.shape

    def body(dy_ref, xh_ref, r_ref, g_ref, dz_ref, dzb_ref, dg_ref, db_ref):
        dyv, xhv = dy_ref[...], xh_ref[...]
        dyg = dyv * g_ref[...]
        c1 = jnp.mean(dyg, -1, keepdims=True)
        c2 = jnp.mean(dyg * xhv, -1, keepdims=True)
        dz = r_ref[...] * (dyg - c1 - xhv * c2)
        dz_ref[...] = dz
        dzb_ref[...] = dz.astype(BF16)

        @pl.when(pl.program_id(0) == 0)
        def _():
            dg_ref[...] = jnp.zeros_like(dg_ref)
            db_ref[...] = jnp.zeros_like(db_ref)

        dg_ref[...] += jnp.sum(dyv * xhv, 0, keepdims=True)
        db_ref[...] += jnp.sum(dyv, 0, keepdims=True)

    return pl.pallas_call(
        body, name=name, grid=(T // ROW_TILE,),
        in_specs=[_row_spec(D), _row_spec(D), _row_spec(1), _full_spec((1, D))],
        out_specs=[_row_spec(D), _row_spec(D), _full_spec((1, D)), _full_spec((1, D))],
        out_shape=[jax.ShapeDtypeStruct((T, D), F32), jax.ShapeDtypeStruct((T, D), BF16),
                   jax.ShapeDtypeStruct((1, D), F32), jax.ShapeDtypeStruct((1, D), F32)],
        compiler_params=_cparams("arbitrary"),
    )(dy, xh, r, g)


def _loss_grad(y, target, *, name):
    T, D = y.shape

    def body(y_ref, t_ref, dy_ref, sq_ref):
        err = y_ref[...] - t_ref[...]
        dy_ref[...] = err / D

        @pl.when(pl.program_id(0) == 0)
        def _():
            sq_ref[...] = jnp.zeros_like(sq_ref)

        sq_ref[...] += jnp.sum(err * err, 0, keepdims=True)

    return pl.pallas_call(
        body, name=name, grid=(T // ROW_TILE,),
        in_specs=[_row_spec(D), _row_spec(D)],
        out_specs=[_row_spec(D), _full_spec((1, D))],
        out_shape=[jax.ShapeDtypeStruct((T, D), F32), jax.ShapeDtypeStruct((1, D), F32)],
        compiler_params=_cparams("arbitrary"),
    )(y, target)


def _ple_bwd_elem(dx3, g, e, *, name):
    T, D = dx3.shape

    def body(dx_ref, g_ref, e_ref, de_ref, dz_ref, db_ref):
        dx, gv = dx_ref[...], g_ref[...]
        de_ref[...] = (dx * gv).astype(BF16)
        dz = dx * e_ref[...] * gv * (1.0 - gv)
        dz_ref[...] = dz.astype(BF16)

        @pl.when(pl.program_id(0) == 0)
        def _():
            db_ref[...] = jnp.zeros_like(db_ref)

        db_ref[...] += jnp.sum(dz, 0, keepdims=True)

    return pl.pallas_call(
        body, name=name, grid=(T // ROW_TILE,),
        in_specs=[_row_spec(D), _row_spec(D), _row_spec(D)],
        out_specs=[_row_spec(D), _row_spec(D), _full_spec((1, D))],
        out_shape=[jax.ShapeDtypeStruct((T, D), BF16), jax.ShapeDtypeStruct((T, D), BF16),
                   jax.ShapeDtypeStruct((1, D), F32)],
        compiler_params=_cparams("arbitrary"),
    )(dx3, g, e)


def _rotate(xv, a, bm, bp, sign):
    half = MLA_ROPE // 2
    width = xv.shape[-1]
    return xv * a + sign * (pltpu.roll(xv, width - half, 1) * bm + pltpu.roll(xv, half, 1) * bp)


def _rope(x, tabs, seq, *, sign, name):
    T, width = x.shape

    def body(x_ref, a_ref, bm_ref, bp_ref, o_ref):
        o_ref[...] = _rotate(x_ref[...], a_ref[...], bm_ref[...], bp_ref[...], sign).astype(BF16)

    return pl.pallas_call(
        body, name=name, grid=(T // ROW_TILE,),
        in_specs=[_row_spec(width)] + [_tab_spec(width, seq // ROW_TILE)] * 3,
        out_specs=_row_spec(width),
        out_shape=jax.ShapeDtypeStruct((T, width), BF16),
        compiler_params=_cparams("parallel"),
    )(x, *tabs)


def _mla_keys(knp, h, k_tabs, seq, *, name):
    T = knp.shape[0]

    def body(k_ref, h_ref, a_ref, bm_ref, bp_ref, o_ref):
        kr = _rotate(h_ref[...], a_ref[...], bm_ref[...], bp_ref[...], 1.0)
        for hd in range(MLA_HEADS):
            cols = slice(hd * LANES, (hd + 1) * LANES)
            o_ref[:, cols] = (k_ref[:, cols].astype(F32) + kr).astype(BF16)

    return pl.pallas_call(
        body, name=name, grid=(T // ROW_TILE,),
        in_specs=[_row_spec(MLA_HEADS * LANES), _row_spec(LANES, EV_KR[0] // LANES)]
        + [_tab_spec(LANES, seq // ROW_TILE)] * 3,
        out_specs=_row_spec(MLA_HEADS * LANES),
        out_shape=jax.ShapeDtypeStruct((T, MLA_HEADS * LANES), BF16),
        compiler_params=_cparams("parallel"),
    )(knp, h, *k_tabs)


def _mla_rope_key_grad(dk, k_tabs, seq, *, name):
    T = dk.shape[0]

    def body(dk_ref, a_ref, bm_ref, bp_ref, o_ref):
        tot = dk_ref[:, 0:LANES]
        for hd in range(1, MLA_HEADS):
            tot = tot + dk_ref[:, hd * LANES:(hd + 1) * LANES]
        o_ref[...] = _rotate(tot, a_ref[...], bm_ref[...], bp_ref[...], -1.0).astype(BF16)

    return pl.pallas_call(
        body, name=name, grid=(T // ROW_TILE,),
        in_specs=[_row_spec(MLA_HEADS * LANES)] + [_tab_spec(LANES, seq // ROW_TILE)] * 3,
        out_specs=_row_spec(LANES),
        out_shape=jax.ShapeDtypeStruct((T, LANES), BF16),
        compiler_params=_cparams("parallel"),
    )(dk, *k_tabs)


def _even_norms(h, gq, gkv, *, name):
    T = h.shape[0]

    def body(h_ref, gq_ref, gkv_ref, cq_ref, ckv_ref, rq_ref, rkv_ref):
        cq = h_ref[:, EV_CQ[0]:EV_CQ[1]]
        rq = lax.rsqrt(jnp.mean(cq * cq, -1, keepdims=True) + NORM_EPS)
        cq_ref[...] = (cq * rq * gq_ref[...]).astype(BF16)
        rq_ref[...] = rq
        ckv = h_ref[:, EV_CKV[0]:EV_CKV[1]]
        rkv = lax.rsqrt(jnp.mean(ckv * ckv, -1, keepdims=True) + NORM_EPS)
        ckv_ref[...] = (ckv * rkv * gkv_ref[...]).astype(BF16)
        rkv_ref[...] = rkv

    return pl.pallas_call(
        body, name=name, grid=(T // ROW_TILE,),
        in_specs=[_row_spec(EVEN_IN_PAD), _full_spec((1, MLA_Q_LORA)), _full_spec((1, MLA_KV_LORA))],
        out_specs=[_row_spec(MLA_Q_LORA), _row_spec(MLA_KV_LORA), _row_spec(1), _row_spec(1)],
        out_shape=[jax.ShapeDtypeStruct((T, MLA_Q_LORA), BF16), jax.ShapeDtypeStruct((T, MLA_KV_LORA), BF16),
                   jax.ShapeDtypeStruct((T, 1), F32), jax.ShapeDtypeStruct((T, 1), F32)],
        compiler_params=_cparams("parallel"),
    )(h, gq, gkv)


def _even_in_bwd(h, rq, rkv, gq, gkv, dcqn, dckvn, dqs, dks, dvs, dkr, *, name):
    T = h.shape[0]

    def rms_bwd(c, r, g, dy):
        xr = c * r
        dyg = dy * g
        return r * (dyg - xr * jnp.mean(dyg * xr, -1, keepdims=True)), jnp.sum(dy * xr, 0, keepdims=True)

    def body(h_ref, rq_ref, rkv_ref, gq_ref, gkv_ref, dcq_ref, dckv_ref, dqs_ref, dks_ref, dvs_ref, dkr_ref,
             dh_ref, dgq_ref, dgkv_ref):
        @pl.when(pl.program_id(0) == 0)
        def _():
            dgq_ref[...] = jnp.zeros_like(dgq_ref)
            dgkv_ref[...] = jnp.zeros_like(dgkv_ref)

        dcq, dgq = rms_bwd(h_ref[:, EV_CQ[0]:EV_CQ[1]], rq_ref[...], gq_ref[...], dcq_ref[...])
        dckv, dgkv = rms_bwd(h_ref[:, EV_CKV[0]:EV_CKV[1]], rkv_ref[...], gkv_ref[...], dckv_ref[...])
        dgq_ref[...] += dgq
        dgkv_ref[...] += dgkv
        dh_ref[:, EV_QS[0]:EV_QS[1]] = dqs_ref[...]
        dh_ref[:, EV_CQ[0]:EV_CQ[1]] = dcq.astype(BF16)
        dh_ref[:, EV_CKV[0]:EV_CKV[1]] = dckv.astype(BF16)
        dh_ref[:, EV_KS[0]:EV_KS[1]] = dks_ref[...]
        dh_ref[:, EV_VS[0]:EV_VS[1]] = dvs_ref[...]
        dh_ref[:, EV_KR[0]:EV_KR[1]] = dkr_ref[...]

    return pl.pallas_call(
        body, name=name, grid=(T // ROW_TILE,),
        in_specs=[_row_spec(EVEN_IN_PAD), _row_spec(1), _row_spec(1), _full_spec((1, MLA_Q_LORA)),
                  _full_spec((1, MLA_KV_LORA)), _row_spec(MLA_Q_LORA), _row_spec(MLA_KV_LORA),
                  _row_spec(SWA_HEADS * HEAD_DIM), _row_spec(LANES), _row_spec(LANES), _row_spec(LANES)],
        out_specs=[_row_spec(EVEN_IN_PAD), _full_spec((1, MLA_Q_LORA)), _full_spec((1, MLA_KV_LORA))],
        out_shape=[jax.ShapeDtypeStruct((T, EVEN_IN_PAD), BF16), jax.ShapeDtypeStruct((1, MLA_Q_LORA), F32),
                   jax.ShapeDtypeStruct((1, MLA_KV_LORA), F32)],
        compiler_params=_cparams("arbitrary"),
    )(h, rq, rkv, gq, gkv, dcqn, dckvn, dqs, dks, dvs, dkr)


def _fox_decay_fwd(f3, bf, *, name):
    B, S, _ = f3.shape

    def body(f_ref, b_ref, csh_ref, chs_ref):
        x = f_ref[...] + b_ref[...]
        c = jnp.minimum(x, 0.0) - jnp.log1p(jnp.exp(-jnp.abs(x)))
        row = lax.broadcasted_iota(jnp.int32, (S, LANES), 0)
        k = 1
        while k < S:
            c = c + jnp.where(row >= k, pltpu.roll(c, k, 0), 0.0)
            k *= 2
        csh_ref[...] = c
        chs_ref[...] = c.T

    return pl.pallas_call(
        body, name=name, grid=(B,),
        in_specs=[pl.BlockSpec((None, S, LANES), lambda b: (b, 0, 0)), pl.BlockSpec((1, LANES), lambda b: (0, 0))],
        out_specs=[pl.BlockSpec((None, S, LANES), lambda b: (b, 0, 0)),
                   pl.BlockSpec((None, LANES, S), lambda b: (b, 0, 0))],
        out_shape=[jax.ShapeDtypeStruct((B, S, LANES), F32), jax.ShapeDtypeStruct((B, LANES, S), F32)],
        compiler_params=_cparams("parallel"),
    )(f3, bf)


def _fox_decay_bwd(dc_hs, f3, bf, *, name):
    B, S, _ = f3.shape

    def body(dc_ref, f_ref, b_ref, df_ref, db_ref):
        g = dc_ref[...].T
        row = lax.broadcasted_iota(jnp.int32, (S, LANES), 0)
        k = 1
        while k < S:
            g = g + jnp.where(row < S - k, pltpu.roll(g, S - k, 0), 0.0)
            k *= 2
        x = f_ref[...] + b_ref[...]
        df = g * (1.0 / (1.0 + jnp.exp(x)))
        df_ref[...] = df.astype(BF16)

        @pl.when(pl.program_id(0) == 0)
        def _():
            db_ref[...] = jnp.zeros_like(db_ref)

        db_ref[...] += jnp.sum(df, 0, keepdims=True)

    return pl.pallas_call(
        body, name=name, grid=(B,),
        in_specs=[pl.BlockSpec((None, LANES, S), lambda b: (b, 0, 0)),
                  pl.BlockSpec((None, S, LANES), lambda b: (b, 0, 0)), pl.BlockSpec((1, LANES), lambda b: (0, 0))],
        out_specs=[pl.BlockSpec((None, S, LANES), lambda b: (b, 0, 0)), pl.BlockSpec((1, LANES), lambda b: (0, 0))],
        out_shape=[jax.ShapeDtypeStruct((B, S, LANES), BF16), jax.ShapeDtypeStruct((1, LANES), F32)],
        compiler_params=_cparams("arbitrary"),
    )(dc_hs, f3, bf)


def _head_column(block, h):
    lane = lax.broadcasted_iota(jnp.int32, block.shape, 1)
    return jnp.sum(jnp.where(lane == h, block, 0.0), axis=-1, keepdims=True)


def _causal_mask(s):
    r = lax.broadcasted_iota(jnp.int32, s.shape, 0)
    c = lax.broadcasted_iota(jnp.int32, s.shape, 1)
    return jnp.where(c <= r, s, NEG_INF)


def _low_half(shape):
    return (lax.broadcasted_iota(jnp.int32, shape, 1) % LANES) < HEAD_DIM


def _widen(x, cols):
    return jnp.concatenate([x] * (cols // LANES), axis=1)


def _both_halves(x, lo):
    r = pltpu.roll(x, HEAD_DIM, 1)
    return jnp.where(lo, x, r), jnp.where(lo, r, x)


def _flash_fwd(qa, ka, va, *, q_blk0, k_blk0, v_blk0, W, n_pairs, B, S, scale, csh=None, crow=None, name):
    t = ATT_TILE
    nq = S // t
    P = PAIRS_PER_STEP_FWD
    decay = csh is not None
    split = W == LANES
    assert n_pairs % P == 0 and q_blk0 % P == 0 and k_blk0 % P == 0 and v_blk0 % P == 0

    def body(*refs):
        if decay:
            q_ref, k_ref, v_ref, csh_ref, crow_ref, o_ref, lse_ref, m_s, acc_s = refs
        else:
            q_ref, k_ref, v_ref, o_ref, lse_ref, m_s, acc_s = refs
        g, i = pl.program_id(1), pl.program_id(2)
        lo = _low_half((t, LANES))
        qv = q_ref[...]
        qh = []
        for pr in range(P):
            qp = qv[:, pr * W:(pr + 1) * W]
            qh += [jnp.where(lo, qp, jnp.zeros_like(qp)), jnp.where(lo, jnp.zeros_like(qp), qp)] if split \
                else [qp[:, :LANES], qp[:, LANES:]]
        if decay:
            cq = [jnp.broadcast_to(_head_column(csh_ref[...], 2 * P * g + hd), (t, LANES)) for hd in range(2 * P)]
        m_s[...] = jnp.full(m_s.shape, NEG_INF, F32)
        acc_s[...] = jnp.zeros(acc_s.shape, F32)

        def step(j, masked):
            rows = pl.ds(pl.multiple_of(j * t, t), t)
            kb, vb = k_ref[rows, :], v_ref[rows, :]
            for pr in range(P):
                kp, vp = kb[:, pr * W:(pr + 1) * W], vb[:, pr * LANES:(pr + 1) * LANES]
                ones = jnp.ones_like(vp)
                vaug = [jnp.where(lo, vp, ones), jnp.where(lo, ones, vp)]
                for half in range(2):
                    hd = 2 * pr + half
                    kh = kp if split else kp[:, half * LANES:(half + 1) * LANES]
                    s = lax.dot_general(qh[hd], kh, NT, preferred_element_type=F32) * scale
                    if decay:
                        s = s + _widen(cq[hd], t) - crow_ref[hd, j]
                    if masked:
                        s = _causal_mask(s)
                    m_prev = m_s[hd]
                    m_new = jnp.maximum(m_prev, jnp.max(s, -1, keepdims=True))
                    p = jnp.exp(s - _widen(m_new, t))
                    acc_s[hd] = jnp.exp(m_prev - m_new) * acc_s[hd] + lax.dot_general(
                        p.astype(BF16), vaug[half], NN, preferred_element_type=F32)
                    m_s[hd] = m_new

        def loop_body(j, carry):
            step(j, False)
            return carry

        lax.fori_loop(0, i, loop_body, 0)
        step(i, True)
        for pr in range(P):
            acc0, acc1 = acc_s[2 * pr], acc_s[2 * pr + 1]
            _, l0 = _both_halves(acc0, lo)
            l1, _ = _both_halves(acc1, lo)
            cols = slice(pr * LANES, (pr + 1) * LANES)
            o_ref[:, cols] = jnp.where(lo, acc0 / l0, acc1 / l1).astype(BF16)
            lse_ref[:, cols] = jnp.where(lo, m_s[2 * pr] + jnp.log(l0), m_s[2 * pr + 1] + jnp.log(l1))

    in_specs = [pl.BlockSpec((t, P * W), lambda b, g, i: (b * nq + i, q_blk0 // P + g)),
                pl.BlockSpec((S, P * W), lambda b, g, i: (b, k_blk0 // P + g)),
                pl.BlockSpec((S, P * LANES), lambda b, g, i: (b, v_blk0 // P + g))]
    args = [qa, ka, va]
    if decay:
        in_specs += [pl.BlockSpec((None, t, LANES), lambda b, g, i: (b, i, 0)),
                     pl.BlockSpec((None, 2 * P, nq, 1, t), lambda b, g, i: (b, g, 0, 0, 0))]
        args += [csh, crow]
    out_spec = pl.BlockSpec((t, P * LANES), lambda b, g, i: (b * nq + i, g))
    return pl.pallas_call(
        body, name=name, grid=(B, n_pairs // P, nq), in_specs=in_specs, out_specs=[out_spec, out_spec],
        out_shape=[jax.ShapeDtypeStruct((B * S, n_pairs * LANES), BF16),
                   jax.ShapeDtypeStruct((B * S, n_pairs * LANES), F32)],
        scratch_shapes=[pltpu.VMEM((2 * P, t, LANES), F32), pltpu.VMEM((2 * P, t, LANES), F32)],
        compiler_params=_cparams("parallel", "parallel", "parallel"),
    )(*args)


def _flash_bwd(qa, ka, va, oa, doa, lsea, *, q_blk0, k_blk0, v_blk0, do_blk0, W, n_pairs, B, S, scale, qk_dtype,
               csh=None, crow=None, name):
    t = ATT_TILE
    nq = S // t
    P = PAIRS_PER_STEP_BWD
    decay = csh is not None
    split = W == LANES
    assert n_pairs % P == 0 and q_blk0 % P == 0 and k_blk0 % P == 0 and v_blk0 % P == 0 and do_blk0 % P == 0

    def body(*refs):
        if decay:
            (q_ref, k_ref, v_ref, o_ref, do_ref, lse_ref, csh_ref, crow_ref, dq_ref, dk_ref, dv_ref, dck_ref, dcq_ref,
             dq_s, lse_s, delta_s, dk_s, dv_s, cq_s, dcq_s, dck_s) = refs
        else:
            (q_ref, k_ref, v_ref, o_ref, do_ref, lse_ref, dq_ref, dk_ref, dv_ref,
             dq_s, lse_s, delta_s, dk_s, dv_s) = refs
        g, j = pl.program_id(1), pl.program_id(2)
        lo = _low_half((t, LANES))

        @pl.when(j == 0)
        def _():
            lo_s = _low_half((S, LANES))
            dq_s[...] = jnp.zeros(dq_s.shape, F32)
            for pr in range(P):
                cols = slice(pr * LANES, (pr + 1) * LANES)
                lse_s[2 * pr], lse_s[2 * pr + 1] = _both_halves(lse_ref[:, cols], lo_s)
                dd = do_ref[:, cols].astype(F32) * o_ref[:, cols].astype(F32)
                delta_s[2 * pr] = jnp.broadcast_to(jnp.sum(jnp.where(lo_s, dd, 0.0), -1, keepdims=True), (S, LANES))
                delta_s[2 * pr + 1] = jnp.broadcast_to(jnp.sum(jnp.where(lo_s, 0.0, dd), -1, keepdims=True),
                                                       (S, LANES))
            if decay:
                for hd in range(2 * P):
                    cq_s[hd] = jnp.broadcast_to(_head_column(csh_ref[...], 2 * P * g + hd), (S, LANES))
                dcq_s[...] = jnp.zeros(dcq_s.shape, F32)

        kb, vb = k_ref[...], v_ref[...]
        kh, vh = [], []
        for pr in range(P):
            kp, vp = kb[:, pr * W:(pr + 1) * W], vb[:, pr * LANES:(pr + 1) * LANES]
            zk, zv = jnp.zeros_like(kp), jnp.zeros_like(vp)
            kh += [jnp.where(lo, kp, zk), jnp.where(lo, zk, kp)] if split else [kp[:, :LANES], kp[:, LANES:]]
            vh += [jnp.where(lo, vp, zv), jnp.where(lo, zv, vp)]
        dk_s[...] = jnp.zeros(dk_s.shape, F32)
        dv_s[...] = jnp.zeros(dv_s.shape, F32)
        if decay:
            dck_s[...] = jnp.zeros(dck_s.shape, F32)

        def step(i, masked):
            rows = pl.ds(pl.multiple_of(i * t, t), t)
            qi, doi = q_ref[rows, :], do_ref[rows, :]
            for pr in range(P):
                qp, dop = qi[:, pr * W:(pr + 1) * W], doi[:, pr * LANES:(pr + 1) * LANES]
                for half in range(2):
                    hd = 2 * pr + half
                    qx = qp if split else qp[:, half * LANES:(half + 1) * LANES]
                    s = lax.dot_general(qx, kh[hd], NT, preferred_element_type=F32) * scale
                    if decay:
                        s = s + _widen(cq_s[hd, rows, :], t) - crow_ref[hd, j]
                    if masked:
                        s = _causal_mask(s)
                    p = jnp.exp(s - _widen(lse_s[hd, rows, :], t))
                    dv_s[hd] += lax.dot_general(p.astype(BF16), dop, TN, preferred_element_type=F32)
                    dp = lax.dot_general(dop, vh[hd], NT, preferred_element_type=F32)
                    ds = p * (dp - _widen(delta_s[hd, rows, :], t))
                    dss = (ds * scale).astype(BF16)
                    dk_s[hd] += lax.dot_general(dss, qx, TN, preferred_element_type=F32)
                    dqc = lax.dot_general(dss, kh[hd], NN, preferred_element_type=F32)
                    if split:
                        dq_s[rows, pr * W:(pr + 1) * W] += dqc
                    else:
                        dq_s[rows, hd * LANES:(hd + 1) * LANES] += dqc
                    if decay:
                        dck_s[hd] -= jnp.sum(ds, 0, keepdims=True)
                        part = ds[:, :LANES]
                        for c in range(1, t // LANES):
                            part = part + ds[:, c * LANES:(c + 1) * LANES]
                        dcq_s[hd, rows, :] += part

        def loop_body(i, carry):
            step(i, False)
            return carry

        step(j, True)
        lax.fori_loop(j + 1, nq, loop_body, 0)
        for pr in range(P):
            if split:
                dk_ref[:, pr * W:(pr + 1) * W] = jnp.where(lo, dk_s[2 * pr], dk_s[2 * pr + 1]).astype(dk_ref.dtype)
            else:
                for half in range(2):
                    hd = 2 * pr + half
                    dk_ref[:, hd * LANES:(hd + 1) * LANES] = dk_s[hd].astype(dk_ref.dtype)
            dv_ref[:, pr * LANES:(pr + 1) * LANES] = jnp.where(lo, dv_s[2 * pr], dv_s[2 * pr + 1]).astype(BF16)
        if decay:
            dck_ref[...] = dck_s[...]

        @pl.when(j == nq - 1)
        def _():
            dq_ref[...] = dq_s[...].astype(dq_ref.dtype)
            if decay:
                for hd in range(2 * P):
                    dcq_ref[hd] = jnp.sum(dcq_s[hd].T, 0, keepdims=True)

    full = lambda w, blk0: pl.BlockSpec((S, P * w), lambda b, g, j: (b, blk0 // P + g))
    blk = lambda w, blk0: pl.BlockSpec((t, P * w), lambda b, g, j: (b * nq + j, blk0 // P + g))
    in_specs = [full(W, q_blk0), blk(W, k_blk0), blk(LANES, v_blk0), full(LANES, 0), full(LANES, do_blk0),
                full(LANES, 0)]
    args = [qa, ka, va, oa, doa, lsea]
    T = B * S
    out_specs = [full(W, 0), blk(W, 0), blk(LANES, 0)]
    out_shape = [jax.ShapeDtypeStruct((T, n_pairs * W), qk_dtype), jax.ShapeDtypeStruct((T, n_pairs * W), qk_dtype),
                 jax.ShapeDtypeStruct((T, n_pairs * LANES), BF16)]
    per_head = lambda rows: pltpu.VMEM((2 * P, rows, LANES), F32)
    scratch = [pltpu.VMEM((S, P * W), F32), per_head(S), per_head(S), per_head(t), per_head(t)]
    if decay:
        in_specs += [pl.BlockSpec((None, S, LANES), lambda b, g, j: (b, 0, 0)),
                     pl.BlockSpec((None, 2 * P, nq, 1, t), lambda b, g, j: (b, g, 0, 0, 0))]
        args += [csh, crow]
        out_specs += [pl.BlockSpec((None, 2 * P, None, 1, t), lambda b, g, j: (b, g, j, 0, 0)),
                      pl.BlockSpec((None, 2 * P, 1, S), lambda b, g, j: (b, g, 0, 0))]
        out_shape += [jax.ShapeDtypeStruct((B, 2 * n_pairs, nq, 1, t), F32),
                      jax.ShapeDtypeStruct((B, 2 * n_pairs, 1, S), F32)]
        scratch += [per_head(S), per_head(S), pltpu.VMEM((2 * P, 1, t), F32)]
    return pl.pallas_call(
        body, name=name, grid=(B, n_pairs // P, nq), in_specs=in_specs, out_specs=out_specs, out_shape=out_shape,
        scratch_shapes=scratch, compiler_params=_cparams("parallel", "parallel", "arbitrary"),
    )(*args)


def _swa_common(q_ref, kp_ref, ko_ref, vp_ref, vo_ref, n):
    Q = BLOCK_Q
    lo = _low_half((Q, LANES))
    lo2 = _low_half((2 * Q, LANES))
    kk = jnp.concatenate([kp_ref[...], ko_ref[...]], axis=0)
    vv = jnp.concatenate([vp_ref[...], vo_ref[...]], axis=0)
    kdup = [x.astype(BF16) for x in _both_halves(kk, lo2)]
    vdup = [x.astype(BF16) for x in _both_halves(vv, lo2)]
    a = lax.broadcasted_iota(jnp.int32, (Q, 2 * Q), 0)
    col = lax.broadcasted_iota(jnp.int32, (Q, 2 * Q), 1)
    dist = a + Q - col
    valid = (dist >= 0) & (dist < SWA_WINDOW) & ((col >= Q) | (n > 0))
    qv = q_ref[...]
    qm = []
    for a_head in range(SWA_HEADS):
        qp = qv[:, (a_head // 2) * LANES:(a_head // 2 + 1) * LANES]
        keep = lo if a_head % 2 == 0 else jnp.logical_not(lo)
        qm.append(jnp.where(keep, qp, 0.0).astype(BF16))
    return lo, lo2, kdup, vdup, valid, qm


def _swa_in_specs(nb):
    Q = BLOCK_Q
    own = lambda blk: (lambda b, n: (b * nb + n, blk))
    prev = lambda blk: (lambda b, n: (b * nb + jnp.maximum(n - 1, 0), blk))
    kb, vb = EV_KS[0] // LANES, EV_VS[0] // LANES
    return [pl.BlockSpec((Q, SWA_HEADS * HEAD_DIM), own(0)), pl.BlockSpec((Q, LANES), prev(kb)),
            pl.BlockSpec((Q, LANES), own(kb)), pl.BlockSpec((Q, LANES), prev(vb)), pl.BlockSpec((Q, LANES), own(vb))]


def _swa_fwd(h, bias, sinkcol, *, B, S, name):
    Q = BLOCK_Q
    nb = S // Q
    scale = HEAD_DIM ** -0.5

    def body(q_ref, kp_ref, ko_ref, vp_ref, vo_ref, bias_ref, sink_ref, o_ref, lse_ref):
        lo, _, kdup, vdup, valid, qm = _swa_common(q_ref, kp_ref, ko_ref, vp_ref, vo_ref, pl.program_id(1))
        lane = lax.broadcasted_iota(jnp.int32, (Q, LANES), 1)
        lse_blk = jnp.zeros((Q, LANES), F32)
        pairs = []
        for pr in range(SWA_HEADS // 2):
            pv = []
            for half in range(2):
                a = 2 * pr + half
                kvh = a // (SWA_HEADS // SWA_KV_HEADS)
                s = lax.dot_general(qm[a], kdup[kvh], NT, preferred_element_type=F32) * scale + bias_ref[a]
                s = jnp.where(valid, s, NEG_INF)
                sink = sink_ref[a]
                mx = jnp.maximum(jnp.max(s, -1, keepdims=True), sink)
                p = jnp.exp(s - mx)
                l = jnp.sum(p, -1, keepdims=True) + jnp.exp(sink - mx)
                pv.append(lax.dot_general((p / l).astype(BF16), vdup[kvh], NN, preferred_element_type=F32))
                lse_blk = jnp.where(lane == a, mx + jnp.log(l), lse_blk)
            pairs.append(jnp.where(lo, pv[0], pv[1]))
        o_ref[...] = jnp.concatenate(pairs, axis=1).astype(BF16)
        lse_ref[...] = lse_blk

    whole = lambda shape: pl.BlockSpec(shape, lambda b, n: (0,) * len(shape))
    return pl.pallas_call(
        body, name=name, grid=(B, nb),
        in_specs=_swa_in_specs(nb) + [whole((SWA_HEADS, Q, 2 * Q)), whole((SWA_HEADS, Q, 1))],
        out_specs=[pl.BlockSpec((Q, SWA_HEADS * HEAD_DIM), lambda b, n: (b * nb + n, 0)),
                   pl.BlockSpec((Q, LANES), lambda b, n: (b * nb + n, 0))],
        out_shape=[jax.ShapeDtypeStruct((B * S, SWA_HEADS * HEAD_DIM), BF16),
                   jax.ShapeDtypeStruct((B * S, LANES), F32)],
        compiler_params=_cparams("parallel", "parallel"),
    )(h, h, h, h, h, bias, sinkcol)


def _swa_bwd(h, o, do, lse, bias, sinkcol, *, do_blk0, B, S, name):
    Q = BLOCK_Q
    nb = S // Q
    scale = HEAD_DIM ** -0.5
    group = SWA_HEADS // SWA_KV_HEADS

    def body(q_ref, kp_ref, ko_ref, vp_ref, vo_ref, o_ref, do_ref, lse_ref, bias_ref, sink_ref,
             dq_ref, dko_ref, dkp_ref, dvo_ref, dvp_ref, dbias_ref, dsink_ref):
        @pl.when((pl.program_id(0) == 0) & (pl.program_id(1) == 0))
        def _():
            dbias_ref[...] = jnp.zeros_like(dbias_ref)
            dsink_ref[...] = jnp.zeros_like(dsink_ref)

        lo, lo2, kdup, vdup, valid, qm = _swa_common(q_ref, kp_ref, ko_ref, vp_ref, vo_ref, pl.program_id(1))
        lse_blk = lse_ref[...]
        dkk = [jnp.zeros((2 * Q, LANES), F32) for _ in range(SWA_KV_HEADS)]
        dvv = [jnp.zeros((2 * Q, LANES), F32) for _ in range(SWA_KV_HEADS)]
        dq_pairs = []
        for pr in range(SWA_HEADS // 2):
            cols = slice(pr * LANES, (pr + 1) * LANES)
            do_p, o_p = do_ref[:, cols], o_ref[:, cols]
            dq_half = []
            for half in range(2):
                a = 2 * pr + half
                kvh = a // group
                keep = lo if half == 0 else jnp.logical_not(lo)
                s = lax.dot_general(qm[a], kdup[kvh], NT, preferred_element_type=F32) * scale + bias_ref[a]
                s = jnp.where(valid, s, NEG_INF)
                lse_a = _head_column(lse_blk, a)
                p = jnp.exp(s - lse_a)
                doh = jnp.where(keep, do_p, jnp.zeros_like(do_p))
                delta = jnp.sum(doh.astype(F32) * o_p.astype(F32), -1, keepdims=True)
                dp = lax.dot_general(doh, vdup[kvh], NT, preferred_element_type=F32)
                ds = p * (dp - delta)
                dbias_ref[a] += ds
                dsink_ref[a] -= jnp.exp(sink_ref[a] - lse_a) * delta
                dss = (ds * scale).astype(BF16)
                dq_half.append(lax.dot_general(dss, kdup[kvh], NN, preferred_element_type=F32))
                dkk[kvh] = dkk[kvh] + lax.dot_general(dss, qm[a], TN, preferred_element_type=F32)
                dvv[kvh] = dvv[kvh] + lax.dot_general(p.astype(BF16), doh, TN, preferred_element_type=F32)
            dq_pairs.append(jnp.where(lo, dq_half[0], dq_half[1]))
        dq_ref[...] = jnp.concatenate(dq_pairs, axis=1).astype(BF16)
        fold = lambda x: x + pltpu.roll(x, HEAD_DIM, 1)
        dk_blk = jnp.where(lo2, fold(dkk[0]), fold(dkk[1]))
        dv_blk = jnp.where(lo2, fold(dvv[0]), fold(dvv[1]))
        dkp_ref[...] = dk_blk[:Q]
        dko_ref[...] = dk_blk[Q:]
        dvp_ref[...] = dv_blk[:Q]
        dvo_ref[...] = dv_blk[Q:]

    whole = lambda shape: pl.BlockSpec(shape, lambda b, n: (0,) * len(shape))
    wide = lambda blk: pl.BlockSpec((Q, SWA_HEADS * HEAD_DIM), lambda b, n: (b * nb + n, blk))
    narrow = pl.BlockSpec((Q, LANES), lambda b, n: (b * nb + n, 0))
    kv_shape = jax.ShapeDtypeStruct((B * S, LANES), F32)
    return pl.pallas_call(
        body, name=name, grid=(B, nb),
        in_specs=_swa_in_specs(nb) + [wide(0), wide(do_blk0), narrow, whole((SWA_HEADS, Q, 2 * Q)),
                                      whole((SWA_HEADS, Q, 1))],
        out_specs=[wide(0), narrow, narrow, narrow, narrow, whole((SWA_HEADS, Q, 2 * Q)), whole((SWA_HEADS, Q, 1))],
        out_shape=[jax.ShapeDtypeStruct((B * S, SWA_HEADS * HEAD_DIM), BF16), kv_shape, kv_shape, kv_shape, kv_shape,
                   jax.ShapeDtypeStruct((SWA_HEADS, Q, 2 * Q), F32), jax.ShapeDtypeStruct((SWA_HEADS, Q, 1), F32)],
        compiler_params=_cparams("arbitrary", "arbitrary"),
    )(h, h, h, h, h, o, do, lse, bias, sinkcol)


def _bias_bucket_sum(dbias, bucket, *, name):
    def body(d_ref, b_ref, o_ref):
        dbv, bk = d_ref[...], b_ref[...]
        lane = lax.broadcasted_iota(jnp.int32, (SWA_HEADS, LANES), 1)
        out = jnp.zeros((SWA_HEADS, LANES), F32)
        for b in range(REL_BUCKETS):
            part = jnp.sum(jnp.where(bk == b, dbv, 0.0), axis=1)
            tot = jnp.sum(part, axis=-1, keepdims=True)
            out = out + jnp.where(lane == b, tot, 0.0)
        o_ref[...] = out

    return pl.pallas_call(
        body, name=name, out_shape=jax.ShapeDtypeStruct((SWA_HEADS, LANES), F32),
        compiler_params=pltpu.CompilerParams(vmem_limit_bytes=VMEM_LIMIT_BYTES),
    )(dbias, bucket)


def _adamw_update(w, g, m, v):
    m_new = ADAM_B1 * m + (1.0 - ADAM_B1) * g
    v_new = ADAM_B2 * v + (1.0 - ADAM_B2) * jnp.square(g)
    m_hat = m_new / (1.0 - ADAM_B1 ** ADAM_STEP)
    v_hat = v_new / (1.0 - ADAM_B2 ** ADAM_STEP)
    return -ADAM_LR * (m_hat / (jnp.sqrt(v_hat) + ADAM_EPS) + ADAM_WD * w), m_new, v_new


def _adamw(w, g, m, v, *, name):
    def body(w_ref, g_ref, m_ref, v_ref, d_ref, nm_ref, nv_ref):
        d_ref[...], nm_ref[...], nv_ref[...] = _adamw_update(w_ref[...], g_ref[...], m_ref[...], v_ref[...])

    return pl.pallas_call(
        body, name=name, out_shape=[jax.ShapeDtypeStruct(w.shape, F32)] * 3,
        compiler_params=pltpu.CompilerParams(vmem_limit_bytes=VMEM_LIMIT_BYTES),
    )(w, g, m, v)


def _adamw_slots(w, parts, m, v, *, name):
    R, C = w.shape
    tr = R if R <= 512 else _pick(R, (256, 128))

    def body(w_ref, p_ref, m_ref, v_ref, g_ref, d_ref, nm_ref, nv_ref):
        g = p_ref[0].astype(F32)
        for j in range(1, N_DEV):
            g = g + p_ref[j].astype(F32)
        g_ref[...] = g
        d_ref[...], nm_ref[...], nv_ref[...] = _adamw_update(w_ref[...], g, m_ref[...], v_ref[...])

    spec = pl.BlockSpec((tr, C), lambda i: (i, 0))
    return pl.pallas_call(
        body, name=name, grid=(R // tr,),
        in_specs=[spec, pl.BlockSpec((N_DEV, tr, C), lambda i: (0, i, 0)), spec, spec], out_specs=[spec] * 4,
        out_shape=[jax.ShapeDtypeStruct((R, C), F32)] * 4, compiler_params=_cparams("parallel"),
    )(w, parts, m, v)


MESH_ID = pl.DeviceIdType.MESH
HBM_SPEC = pl.BlockSpec(memory_space=pltpu.HBM)
VMEM_SPEC = pl.BlockSpec(memory_space=pltpu.VMEM)


def _mesh_place():
    x, y, c = lax.axis_index("x"), lax.axis_index("y"), lax.axis_index("c")
    return x, y, c, 4 * x + 2 * y + c


def _all_gather_hbm(blocks, *, name):
    n = len(blocks)

    def body(*refs):
        x_refs, out_refs = refs[:n], refs[n:2 * n]
        send_sems, recv_sems, local_sems = refs[2 * n:]
        x, y, c, _ = _mesh_place()
        me, sibling = (x, y, c), (x, y, 1 - c)
        chips = [(1 - x, y), (x, 1 - y), (1 - x, 1 - y)]

        def copy(w, k, blk, to, src=None):
            px, py, pc = blk
            slot = out_refs[w].at[4 * px + 2 * py + pc]
            return pltpu.make_async_remote_copy(
                src_ref=slot if src is None else src, dst_ref=slot,
                send_sem=send_sems.at[w, k], recv_sem=recv_sems.at[w, k], device_id=to, device_id_type=MESH_ID)

        mine = [pltpu.make_async_copy(x_refs[w], out_refs[w].at[4 * x + 2 * y + c], local_sems.at[w])
                for w in range(n)]
        for cp in mine:
            cp.start()
        first = []
        for w in range(n):
            first.append(copy(w, 0, me, sibling, src=x_refs[w]))
            first += [copy(w, 1 + j, me, (*chip, c), src=x_refs[w]) for j, chip in enumerate(chips)]
        for cp in first:
            cp.start()
        passed = []
        for j, chip in enumerate(chips):
            for w in range(n):
                copy(w, 1 + j, (*chip, c), me).wait_recv()
                fwd = copy(w, 4 + j, (*chip, c), sibling)
                fwd.start()
                passed.append(fwd)
        for w in range(n):
            copy(w, 0, sibling, me).wait_recv()
            for j, chip in enumerate(chips):
                copy(w, 4 + j, (*chip, 1 - c), me).wait_recv()
        for cp in first + passed:
            cp.wait_send()
        for cp in mine:
            cp.wait()

    return pl.pallas_call(
        body, name=name, out_shape=[jax.ShapeDtypeStruct((N_DEV,) + b.shape, b.dtype) for b in blocks],
        in_specs=[HBM_SPEC] * n, out_specs=[HBM_SPEC] * n,
        scratch_shapes=[pltpu.SemaphoreType.DMA((n, 7)), pltpu.SemaphoreType.DMA((n, 7)),
                        pltpu.SemaphoreType.DMA((n,))],
    )(*blocks)


def _peers(x, y, c):
    out = []
    for mask in range(1, N_DEV):
        dx, dy, dc = (mask >> 2) & 1, (mask >> 1) & 1, mask & 1
        px, py, pc = (1 - x if dx else x), (1 - y if dy else y), (1 - c if dc else c)
        out.append(((px, py, pc), 4 * px + 2 * py + pc))
    return out


def _all_to_all_hbm(parts, *, name):
    n = len(parts)

    def body(*refs):
        p_refs, out_refs = refs[:n], refs[n:2 * n]
        send_sems, recv_sems, local_sems = refs[2 * n:]
        x, y, c, me = _mesh_place()
        mine = [pltpu.make_async_copy(p_refs[w].at[me], out_refs[w].at[me], local_sems.at[w]) for w in range(n)]
        for cp in mine:
            cp.start()
        copies = []
        for k, (peer, peer_idx) in enumerate(_peers(x, y, c)):
            for w in range(n):
                copies.append(pltpu.make_async_remote_copy(
                    src_ref=p_refs[w].at[peer_idx], dst_ref=out_refs[w].at[me], send_sem=send_sems.at[w, k],
                    recv_sem=recv_sems.at[w, k], device_id=peer, device_id_type=MESH_ID))
        for cp in copies:
            cp.start()
        for cp in copies:
            cp.wait_recv()
        for cp in copies:
            cp.wait_send()
        for cp in mine:
            cp.wait()

    return pl.pallas_call(
        body, name=name, out_shape=[jax.ShapeDtypeStruct(p.shape, p.dtype) for p in parts],
        in_specs=[HBM_SPEC] * n, out_specs=[HBM_SPEC] * n,
        scratch_shapes=[pltpu.SemaphoreType.DMA((n, 7)), pltpu.SemaphoreType.DMA((n, 7)),
                        pltpu.SemaphoreType.DMA((n,))],
    )(*parts)


def _all_reduce_small(block, *, name):
    R, W = block.shape

    def body(x_ref, out_ref, buf, send_sems, recv_sems):
        x, y, c, me = _mesh_place()
        copies = []
        for k, (peer, _) in enumerate(_peers(x, y, c)):
            copies.append(pltpu.make_async_remote_copy(
                src_ref=x_ref, dst_ref=buf.at[me], send_sem=send_sems.at[k], recv_sem=recv_sems.at[k],
                device_id=peer, device_id_type=MESH_ID))
        for cp in copies:
            cp.start()
        buf[me] = x_ref[...]
        for cp in copies:
            cp.wait_recv()
        for cp in copies:
            cp.wait_send()
        acc = buf[0]
        for j in range(1, N_DEV):
            acc = acc + buf[j]
        out_ref[...] = acc

    return pl.pallas_call(
        body, name=name, out_shape=jax.ShapeDtypeStruct((R, W), F32),
        in_specs=[VMEM_SPEC], out_specs=VMEM_SPEC,
        scratch_shapes=[pltpu.VMEM((N_DEV, R, W), F32), pltpu.SemaphoreType.DMA((7,)), pltpu.SemaphoreType.DMA((7,))],
    )(block)


def _assemble(name, g):
    if BIG_AXIS[name] == 2:
        return jnp.concatenate([g[j] for j in range(N_DEV)], axis=2)
    _, n0, a, b = g.shape
    return g.transpose(1, 0, 2, 3).reshape(n0, N_DEV * a, b)


def _split_for_devices(name, g, shard_shape):
    n0, a, b = shard_shape
    if BIG_AXIS[name] == 2:
        return jnp.stack([g[:, :, j * b:(j + 1) * b] for j in range(N_DEV)]).astype(BF16)
    return g.reshape(n0, N_DEV, a, b).transpose(1, 0, 2, 3).astype(BF16)


PACK_ROWS = 8


def _pack_small(vals):
    flat = jnp.concatenate([vals[n].reshape(-1).astype(F32) for n in SMALL])
    pad = (-flat.shape[0]) % (PACK_ROWS * LANES)
    return jnp.pad(flat, (0, pad)).reshape(-1, LANES)


def _unpack_small(block, shapes):
    flat = block.reshape(-1)
    out, off = {}, 0
    for n in SMALL:
        sz = math.prod(shapes[n])
        out[n] = flat[off:off + sz].reshape(shapes[n])
        off += sz
    return out


def _rope_tables(S):
    half = MLA_ROPE // 2
    inv = 1.0 / (ROPE_THETA ** (jnp.arange(0, MLA_ROPE, 2, dtype=F32) / MLA_ROPE))
    ang = jnp.arange(S, dtype=F32)[:, None] * inv[None, :]
    cos, sin = jnp.cos(ang), jnp.sin(ang)
    zeros = jnp.zeros((S, half), F32)
    tail = jnp.zeros((S, LANES - MLA_QK), F32)

    def block(rope_part, nope_val):
        return jnp.concatenate([jnp.full((S, MLA_NOPE), nope_val, F32), rope_part, tail], -1)

    a_r = jnp.concatenate([cos, cos], -1)
    bm_r = jnp.concatenate([-sin, zeros], -1)
    bp_r = jnp.concatenate([zeros, sin], -1)
    q_tabs = tuple(jnp.tile(block(r, v), (1, MLA_HEADS)) for r, v in ((a_r, 1.0), (bm_r, 0.0), (bp_r, 0.0)))
    k_tabs = tuple(block(r, 0.0) for r in (a_r, bm_r, bp_r))
    return q_tabs, k_tabs


def _t5_bucket(dist):
    exact = REL_BUCKETS // 2
    d = jnp.maximum(dist, 1).astype(F32)
    large = exact + (jnp.log(d / exact) / math.log(REL_MAX_DIST / exact) * (REL_BUCKETS - exact)).astype(jnp.int32)
    large = jnp.minimum(large, REL_BUCKETS - 1)
    return jnp.where(dist < exact, dist, large)


def _swa_bucket_table():
    a = jnp.arange(BLOCK_Q)[:, None]
    col = jnp.arange(2 * BLOCK_Q)[None, :]
    return _t5_bucket(jnp.maximum(a + BLOCK_Q - col, 0)).astype(jnp.int32)


def _even_weights(W, j):
    w = W['ev_w_in'][j]
    c_kv1 = MLA_Q_LORA + MLA_KV_LORA
    c_kr1 = c_kv1 + MLA_ROPE
    c_qs1 = c_kr1 + SWA_HEADS * HEAD_DIM
    zeros = lambda n: jnp.zeros((D_MODEL, n), w.dtype)
    w_in = jnp.concatenate([w[:, c_kr1:c_qs1], w[:, :c_kv1], w[:, c_qs1:], zeros(KR_LANE0), w[:, c_kv1:c_kr1],
                            zeros(LANES - KR_LANE0 - MLA_ROPE)], axis=1)
    uq = W['ev_w_uq'][j].reshape(MLA_Q_LORA, MLA_HEADS, MLA_QK)
    w_uq = jnp.pad(uq, ((0, 0), (0, 0), (0, LANES - MLA_QK))).reshape(MLA_Q_LORA, MLA_HEADS * LANES)
    ukv = W['ev_w_ukv'][j].reshape(MLA_KV_LORA, MLA_HEADS, MLA_NOPE + MLA_V)
    w_k = jnp.pad(ukv[..., :MLA_NOPE], ((0, 0), (0, 0), (0, LANES - MLA_NOPE))).reshape(MLA_KV_LORA, -1)
    w_v = ukv[..., MLA_NOPE:].reshape(MLA_KV_LORA, MLA_HEADS * MLA_V)
    return w_in, w_uq, w_k, w_v, W['ev_w_out'][j]


def _even_in_grad_unpad(dw):
    kr0 = EV_KR[0] + KR_LANE0
    return jnp.concatenate([dw[:, EV_CQ[0]:EV_CKV[1]], dw[:, kr0:kr0 + MLA_ROPE], dw[:, EV_QS[0]:EV_QS[1]],
                            dw[:, EV_KS[0]:EV_VS[1]]], axis=1)


def _even_fwd(xb, W, P, j, B, S, tabs, tag):
    q_tabs, k_tabs, bias, sinkcol = tabs
    w_in, w_uq, w_k, w_v, w_out = _even_weights(W, j)
    h = _mm(xb, w_in, name=f"{tag}_in")
    cqn, ckvn, rq, rkv = _even_norms(h, P['ev_q_norm'][j][None], P['ev_kv_norm'][j][None], name=f"{tag}_norms")
    q = _rope(_mm(cqn, w_uq, name=f"{tag}_uq"), q_tabs, S, sign=1.0, name=f"{tag}_ropeq")
    knp = _mm(ckvn, w_k, out_dtypes=(BF16,), name=f"{tag}_uk")
    v = _mm(ckvn, w_v, out_dtypes=(BF16,), name=f"{tag}_uv")
    k = _mla_keys(knp, h, k_tabs, S, name=f"{tag}_keys")
    o_mla, lse_mla = _flash_fwd(q, k, v, q_blk0=0, k_blk0=0, v_blk0=0, W=2 * LANES, n_pairs=MLA_HEADS // 2, B=B, S=S,
                                scale=MLA_QK ** -0.5, name=f"{tag}_mla")
    o_swa, lse_swa = _swa_fwd(h, bias, sinkcol, B=B, S=S, name=f"{tag}_swa")
    o_cat = jnp.concatenate([o_mla, o_swa], axis=-1)
    m = _mm(o_cat, w_out, name=f"{tag}_out")
    res = dict(h=h, cqn=cqn, ckvn=ckvn, rq=rq, rkv=rkv, q=q, k=k, v=v, o_mla=o_mla, lse_mla=lse_mla,
               o_swa=o_swa, lse_swa=lse_swa, o_cat=o_cat)
    return m, res


def _shift_prev(own, prev, B, S):
    prev = prev.reshape(B, S, LANES)
    shifted = jnp.concatenate([prev[:, BLOCK_Q:], jnp.zeros_like(prev[:, :BLOCK_Q])], axis=1)
    return (own + shifted.reshape(B * S, LANES)).astype(BF16)


def _even_bwd(dmb, dz1, xb, W, P, j, B, S, tabs, res, tag):
    q_tabs, k_tabs, bias, sinkcol = tabs
    w_in, w_uq, w_k, w_v, w_out = _even_weights(W, j)
    g = {}
    g['ev_w_out'] = _mm_tn(res['o_cat'], dmb, name=f"{tag}_dwout")
    do = _mm(dmb, w_out, trans_b=True, out_dtypes=(BF16,), name=f"{tag}_do")
    dq, dk, dv = _flash_bwd(res['q'], res['k'], res['v'], res['o_mla'], do, res['lse_mla'], q_blk0=0, k_blk0=0,
                            v_blk0=0, do_blk0=0, W=2 * LANES, n_pairs=MLA_HEADS // 2, B=B, S=S,
                            scale=MLA_QK ** -0.5, qk_dtype=F32, name=f"{tag}_mla_bwd")
    dq_pre = _rope(dq, q_tabs, S, sign=-1.0, name=f"{tag}_ropeq_bwd")
    dw_uq = _mm_tn(res['cqn'], dq_pre, name=f"{tag}_dwuq")
    g['ev_w_uq'] = dw_uq.reshape(MLA_Q_LORA, MLA_HEADS, LANES)[..., :MLA_QK].reshape(MLA_Q_LORA, MLA_HEADS * MLA_QK)
    dcqn = _mm(dq_pre, w_uq, trans_b=True, name=f"{tag}_dcqn")
    dw_k = _mm_tn(res['ckvn'], dk, name=f"{tag}_dwuk").reshape(MLA_KV_LORA, MLA_HEADS, LANES)[..., :MLA_NOPE]
    dw_v = _mm_tn(res['ckvn'], dv, name=f"{tag}_dwuv").reshape(MLA_KV_LORA, MLA_HEADS, MLA_V)
    g['ev_w_ukv'] = jnp.concatenate([dw_k, dw_v], axis=-1).reshape(MLA_KV_LORA, MLA_HEADS * (MLA_NOPE + MLA_V))
    dckvn_v = _mm(dv, w_v, trans_b=True, name=f"{tag}_dckvn_v")
    dckvn = _mm(dk, w_k, trans_b=True, extras=(dckvn_v,), epilogue=lambda acc, r: (acc + r,), name=f"{tag}_dckvn")
    dkr_pre = _mla_rope_key_grad(dk, k_tabs, S, name=f"{tag}_ropek_bwd")
    dqs, dko, dkp, dvo, dvp, dbias, dsink = _swa_bwd(res['h'], res['o_swa'], do, res['lse_swa'], bias, sinkcol,
                                                     do_blk0=1, B=B, S=S, name=f"{tag}_swa_bwd")
    dh, dgq, dgkv = _even_in_bwd(res['h'], res['rq'], res['rkv'], P['ev_q_norm'][j][None], P['ev_kv_norm'][j][None],
                                 dcqn, dckvn, dqs, _shift_prev(dko, dkp, B, S), _shift_prev(dvo, dvp, B, S), dkr_pre,
                                 name=f"{tag}_in_bwd")
    g['ev_w_in'] = _even_in_grad_unpad(_mm_tn(xb, dh, name=f"{tag}_dwin"))
    dx = _mm(dh, w_in, trans_b=True, extras=(dz1,), epilogue=lambda acc, r: (acc + DN_ALPHA * r,), name=f"{tag}_dx")
    small = dict(ev_q_norm=dgq[0], ev_kv_norm=dgkv[0], dbias=dbias, ev_sinks=jnp.sum(dsink, axis=(1, 2)))
    return dx, g, small


def _odd_fwd(xb, W, P, j, B, S, tag):
    w = W['od_w_in'][j]
    w_qkv = w[:, :ODD_QKV]
    w_f = jnp.pad(w[:, ODD_QKV:], ((0, 0), (0, LANES - FOX_HEADS)))
    bf = jnp.pad(P['od_b_f'][j], (0, LANES - FOX_HEADS))[None]
    qkv = _mm(xb, w_qkv, out_dtypes=(BF16,), name=f"{tag}_qkv")
    f = _mm(xb, w_f, name=f"{tag}_f").reshape(B, S, LANES)
    csh, chs = _fox_decay_fwd(f, bf, name=f"{tag}_decay")
    crow = chs[:, :FOX_HEADS].reshape(B, FOX_HEADS, S // ATT_TILE, 1, ATT_TILE)
    n_blk = FOX_HEADS * HEAD_DIM // LANES
    o, lse = _flash_fwd(qkv, qkv, qkv, q_blk0=0, k_blk0=n_blk, v_blk0=2 * n_blk, W=LANES, n_pairs=FOX_HEADS // 2,
                        B=B, S=S, scale=HEAD_DIM ** -0.5, csh=csh, crow=crow, name=f"{tag}_fox")
    m = _mm(o, W['od_w_out'][j], name=f"{tag}_out")
    res = dict(f=f, bf=bf, csh=csh, crow=crow, qkv=qkv, o=o, lse=lse, w_qkv=w_qkv, w_f=w_f)
    return m, res


def _odd_bwd(dmb, dz1, xb, W, P, j, B, S, res, tag):
    g = {}
    w_out = W['od_w_out'][j]
    g['od_w_out'] = _mm_tn(res['o'], dmb, name=f"{tag}_dwout")
    do = _mm(dmb, w_out, trans_b=True, out_dtypes=(BF16,), name=f"{tag}_do")
    qkv = res['qkv']
    n_blk = FOX_HEADS * HEAD_DIM // LANES
    dq, dk, dv, dck, dcq = _flash_bwd(qkv, qkv, qkv, res['o'], do, res['lse'], q_blk0=0, k_blk0=n_blk,
                                      v_blk0=2 * n_blk, do_blk0=0, W=LANES, n_pairs=FOX_HEADS // 2, B=B, S=S,
                                      scale=HEAD_DIM ** -0.5, qk_dtype=BF16, csh=res['csh'], crow=res['crow'],
                                      name=f"{tag}_fox_bwd")
    dc = dck.reshape(B, FOX_HEADS, S) + dcq.reshape(B, FOX_HEADS, S)
    dc_hs = jnp.pad(dc, ((0, 0), (0, LANES - FOX_HEADS), (0, 0)))
    df, dbf = _fox_decay_bwd(dc_hs, res['f'], res['bf'], name=f"{tag}_decay_bwd")
    df = df.reshape(B * S, LANES)
    dqkv = jnp.concatenate([dq, dk, dv], axis=-1)
    dw_qkv = _mm_tn(xb, dqkv, name=f"{tag}_dwqkv")
    dw_f = _mm_tn(xb, df, name=f"{tag}_dwf")
    g['od_w_in'] = jnp.concatenate([dw_qkv, dw_f[:, :FOX_HEADS]], axis=1)
    dxf = _mm(df, res['w_f'], trans_b=True, extras=(dz1,), epilogue=lambda acc, r: (acc + DN_ALPHA * r,),
              name=f"{tag}_dxf")
    dx = _mm(dqkv, res['w_qkv'], trans_b=True, extras=(dxf,), epilogue=lambda acc, r: (acc + r,), name=f"{tag}_dx")
    small = dict(od_b_f=dbf[0, :FOX_HEADS])
    return dx, g, small


def _local_step(x, p, target, W, P):
    B, S, D = x.shape
    T = B * S
    q_tabs, k_tabs = _rope_tables(S)
    bucket = _swa_bucket_table()
    bias = P['rel_bias'][bucket].astype(F32).transpose(2, 0, 1)

    xc = x.reshape(T, D)
    xcb = xc.astype(BF16)
    saved = []
    for i in range(DEPTH):
        j = i // 2
        tag = f"l{i}"
        lay = dict(xb=xcb)
        if i % 2 == 0:
            sinkcol = jnp.broadcast_to(P['ev_sinks'][j][:, None, None], (SWA_HEADS, BLOCK_Q, 1)).astype(F32)
            lay['tabs'] = (q_tabs, k_tabs, bias, sinkcol)
            m, lay['mix'] = _even_fwd(xcb, W, P, j, B, S, lay['tabs'], tag)
        else:
            m, lay['mix'] = _odd_fwd(xcb, W, P, j, B, S, tag)
        x1, x1b, lay['xh1'], lay['r1'] = _ln_fwd(xc, m, P['ln1_g'][i][None], P['ln1_b'][i][None], name=f"{tag}_ln1")
        lay['x1b'] = x1b
        lay['u'], lay['a'] = _mm(x1b, W['w_up'][i], out_dtypes=(F32, BF16),
                                 epilogue=lambda acc: (acc, jnp.square(jnp.maximum(acc, 0.0))), name=f"{tag}_up")
        d = _mm(lay['a'], W['w_down'][i], name=f"{tag}_down")
        x2, x2b, lay['xh2'], lay['r2'] = _ln_fwd(x1, d, P['ln2_g'][i][None], P['ln2_b'][i][None], name=f"{tag}_ln2")
        lay['x2b'] = x2b
        lay['p'] = p[i].reshape(T, D_PLE)
        lay['e'] = _mm(lay['p'], W['ple_w_proj'][i], name=f"{tag}_ple_proj")

        def gate(acc, bg, e, x2v):
            gv = 1.0 / (1.0 + jnp.exp(-(acc + bg)))
            y = x2v + gv * e
            return y, y, gv

        xc, xcb, lay['g'] = _mm(x2b, W['ple_w_gate'][i], extras=(P['ple_b_gate'][i][None], lay['e'], x2),
                                epilogue=gate, out_dtypes=(F32, BF16, F32), name=f"{tag}_ple_gate")
        saved.append(lay)

    dy, sq = _loss_grad(xc, target.reshape(T, D), name="loss")

    G = {n: [None] * (DEPTH if n in ('w_up', 'w_down', 'ple_w_proj', 'ple_w_gate') else DEPTH // 2) for n in BIG}
    Gs = {n: [None] * DEPTH for n in ('ln1_g', 'ln1_b', 'ln2_g', 'ln2_b', 'ple_b_gate')}
    Gs.update({n: [None] * (DEPTH // 2) for n in ('ev_q_norm', 'ev_kv_norm', 'ev_sinks', 'od_b_f')})
    dbias_total = None
    for i in reversed(range(DEPTH)):
        j = i // 2
        tag = f"l{i}b"
        lay = saved[i]
        de, dzg, dbg = _ple_bwd_elem(dy, lay['g'], lay['e'], name=f"{tag}_ple_elem")
        Gs['ple_b_gate'][i] = dbg[0]
        G['ple_w_proj'][i] = _mm_tn(lay['p'], de, name=f"{tag}_dwproj")
        G['ple_w_gate'][i] = _mm_tn(lay['x2b'], dzg, name=f"{tag}_dwgate")
        dx2 = _mm(dzg, W['ple_w_gate'][i], trans_b=True, extras=(dy,), epilogue=lambda acc, r: (acc + r,),
                  name=f"{tag}_dx2")
        dz2, dz2b, dg2, db2 = _ln_bwd(dx2, lay['xh2'], lay['r2'], P['ln2_g'][i][None], name=f"{tag}_ln2")
        Gs['ln2_g'][i], Gs['ln2_b'][i] = dg2[0], db2[0]
        G['w_down'][i] = _mm_tn(lay['a'], dz2b, name=f"{tag}_dwdown")
        du = _mm(dz2b, W['w_down'][i], trans_b=True, extras=(lay['u'],), out_dtypes=(BF16,),
                 epilogue=lambda acc, u: (acc * (2.0 * jnp.maximum(u, 0.0)),), name=f"{tag}_du")
        G['w_up'][i] = _mm_tn(lay['x1b'], du, name=f"{tag}_dwup")
        dx1 = _mm(du, W['w_up'][i], trans_b=True, extras=(dz2,), epilogue=lambda acc, r: (acc + DN_ALPHA * r,),
                  name=f"{tag}_dx1")
        dz1, dz1b, dg1, db1 = _ln_bwd(dx1, lay['xh1'], lay['r1'], P['ln1_g'][i][None], name=f"{tag}_ln1")
        Gs['ln1_g'][i], Gs['ln1_b'][i] = dg1[0], db1[0]
        if i % 2 == 0:
            dy, g, small = _even_bwd(dz1b, dz1, lay['xb'], W, P, j, B, S, lay['tabs'], lay['mix'], tag)
            dbias_total = small['dbias'] if dbias_total is None else dbias_total + small['dbias']
            for n in ('ev_q_norm', 'ev_kv_norm', 'ev_sinks'):
                Gs[n][j] = small[n]
        else:
            dy, g, small = _odd_bwd(dz1b, dz1, lay['xb'], W, P, j, B, S, lay['mix'], tag)
            Gs['od_b_f'][j] = small['od_b_f']
        for n, val in g.items():
            G[n][j] = val

    grads_big = {n: jnp.stack(G[n]) for n in BIG}
    grads_small = {n: jnp.stack(v) for n, v in Gs.items()}
    drel = _bias_bucket_sum(dbias_total, bucket, name="rel_bias_grad")
    grads_small['rel_bias'] = drel[:, :REL_BUCKETS].T
    return sq, dy.reshape(B, S, D), grads_big, grads_small


def kernel(x, p, rel_bias, ev_w_in, ev_q_norm, ev_w_uq, ev_kv_norm, ev_w_ukv, ev_sinks, ev_w_out, od_w_in, od_b_f, od_w_out, ln1_g, ln1_b, w_up, w_down, ln2_g, ln2_b, ple_w_proj, ple_w_gate, ple_b_gate, loss_target, m_rel_bias, m_ev_w_in, m_ev_q_norm, m_ev_w_uq, m_ev_kv_norm, m_ev_w_ukv, m_ev_sinks, m_ev_w_out, m_od_w_in, m_od_b_f, m_od_w_out, m_ln1_g, m_ln1_b, m_w_up, m_w_down, m_ln2_g, m_ln2_b, m_ple_w_proj, m_ple_w_gate, m_ple_b_gate, v_rel_bias, v_ev_w_in, v_ev_q_norm, v_ev_w_uq, v_ev_kv_norm, v_ev_w_ukv, v_ev_sinks, v_ev_w_out, v_od_w_in, v_od_b_f, v_od_w_out, v_ln1_g, v_ln1_b, v_w_up, v_w_down, v_ln2_g, v_ln2_b, v_ple_w_proj, v_ple_w_gate, v_ple_b_gate):
    given = dict(locals())
    w = {n: given[n] for n in WEIGHTS}
    mom = {n: given["m_" + n] for n in WEIGHTS}
    var = {n: given["v_" + n] for n in WEIGHTS}
    shard_shapes = {n: w[n].shape for n in BIG}
    small_shapes = {n: w[n].shape for n in SMALL}

    gathered = _all_gather_hbm([w[n].astype(BF16) for n in BIG], name="gather_weights")
    W = {n: _assemble(n, g) for n, g in zip(BIG, gathered)}
    P = {n: w[n] for n in SMALL}

    sq, grad_x, grads_big, grads_small = _local_step(x, p, loss_target, W, P)
    loss = lax.psum(0.5 * jnp.sum(sq) / D_MODEL, ("x", "y", "c"))

    received = _all_to_all_hbm([_split_for_devices(n, grads_big[n], shard_shapes[n]) for n in BIG],
                               name="exchange_grads")
    g_small_packed = _all_reduce_small(_pack_small(grads_small), name="reduce_small_grads")
    g_small = _unpack_small(g_small_packed, small_shapes)

    grad, delta, new_m, new_v = {}, {}, {}, {}
    for n, parts in zip(BIG, received):
        shp = shard_shapes[n]
        two_d = (shp[0] * shp[1], shp[2])
        outs = _adamw_slots(w[n].reshape(two_d), parts.reshape((N_DEV,) + two_d), mom[n].reshape(two_d),
                            var[n].reshape(two_d), name=f"adamw_{n}")
        grad[n], delta[n], new_m[n], new_v[n] = (t.reshape(shp) for t in outs)
    d, nm, nv = _adamw(_pack_small(w), g_small_packed, _pack_small(mom), _pack_small(var), name="adamw_small")
    d, nm, nv = (_unpack_small(t, small_shapes) for t in (d, nm, nv))
    for n in SMALL:
        grad[n], delta[n], new_m[n], new_v[n] = g_small[n], d[n], nm[n], nv[n]

    return (loss, grad_x, *[grad[n] for n in WEIGHTS], *[delta[n] for n in WEIGHTS],
            *[new_m[n] for n in WEIGHTS], *[new_v[n] for n in WEIGHTS])
```

```python
import math

import jax
import jax.numpy as jnp
from jax import lax
from jax.experimental import pallas as pl
from jax.experimental.pallas import tpu as pltpu

F32, BF16 = jnp.float32, jnp.bfloat16

D_MODEL = 1024
DEPTH = 4
HEAD_DIM = 64
MLA_HEADS, MLA_NOPE, MLA_ROPE, MLA_V = 8, 64, 32, 64
MLA_Q_LORA, MLA_KV_LORA = 384, 256
MLA_QK = MLA_NOPE + MLA_ROPE
ROPE_THETA = 10000.0
SWA_HEADS, SWA_KV_HEADS, SWA_WINDOW = 8, 2, 128
REL_BUCKETS, REL_MAX_DIST = 32, 128
FOX_HEADS = 16
D_FF = 4 * D_MODEL
D_PLE = 256
BLOCK_Q = 128
DN_ALPHA = (2 * DEPTH) ** 0.25
NORM_EPS = 1e-5
NEG_INF = -1e30
EVEN_IN = 1440
ODD_QKV = 3 * FOX_HEADS * HEAD_DIM
LANES = 128

EV_QS = (0, 512)
EV_CQ = (512, 896)
EV_CKV = (896, 1152)
EV_KS = (1152, 1280)
EV_VS = (1280, 1408)
EV_KR = (1408, 1536)
EVEN_IN_PAD = 1536
KR_LANE0 = MLA_NOPE

ADAM_LR, ADAM_B1, ADAM_B2, ADAM_EPS, ADAM_WD, ADAM_STEP = 0.001, 0.9, 0.999, 1e-08, 0.01, 10

N_DEV = 8
VMEM_LIMIT_BYTES = 48 * 1024 * 1024
ATT_TILE = 256
PAIRS_PER_STEP_FWD = 4
PAIRS_PER_STEP_BWD = 2

NN = (((1,), (0,)), ((), ()))
NT = (((1,), (1,)), ((), ()))
TN = (((0,), (0,)), ((), ()))

BIG = ['ev_w_in', 'ev_w_uq', 'ev_w_ukv', 'ev_w_out', 'od_w_in', 'od_w_out', 'w_up', 'w_down',
       'ple_w_proj', 'ple_w_gate']
BIG_AXIS = {'ev_w_in': 2, 'ev_w_uq': 2, 'ev_w_ukv': 2, 'ev_w_out': 1, 'od_w_in': 2, 'od_w_out': 1,
            'w_up': 2, 'w_down': 1, 'ple_w_proj': 2, 'ple_w_gate': 1}
SMALL = ['rel_bias', 'ev_q_norm', 'ev_kv_norm', 'ev_sinks', 'od_b_f', 'ln1_g', 'ln1_b', 'ln2_g', 'ln2_b',
         'ple_b_gate']
WEIGHTS = ['rel_bias', 'ev_w_in', 'ev_q_norm', 'ev_w_uq', 'ev_kv_norm', 'ev_w_ukv', 'ev_sinks', 'ev_w_out',
           'od_w_in', 'od_b_f', 'od_w_out', 'ln1_g', 'ln1_b', 'w_up', 'w_down', 'ln2_g', 'ln2_b',
           'ple_w_proj', 'ple_w_gate', 'ple_b_gate']


def _cparams(*sem):
    return pltpu.CompilerParams(dimension_semantics=sem, vmem_limit_bytes=VMEM_LIMIT_BYTES)


def _pick(n, cands):
    for c in cands:
        if n % c == 0:
            return c
    return n


MM_STEP_BYTES = 10 * 1024 * 1024
MM_OUT_BYTES = 8 * 1024 * 1024
MM_CHUNK = 512


def _mm(a, b, *, trans_b=False, extras=(), epilogue=None, out_dtypes=(F32,), name):
    M, K = a.shape
    N = b.shape[0] if trans_b else b.shape[1]
    n_ex, n_out = len(extras), len(out_dtypes)
    row_bytes = K * a.dtype.itemsize + N * (sum(jnp.dtype(d).itemsize for d in out_dtypes)
                                            + sum(e.dtype.itemsize for e in extras if e.shape[0] == M))
    tm = next((c for c in (1024, 512, 256) if M % c == 0 and c * row_bytes <= MM_STEP_BYTES), 128)
    nc = _pick(N, (MM_CHUNK, 384, 256, 128))

    def body(*refs):
        a_ref, b_ref = refs[:2]
        ex = refs[2:2 + n_ex]
        outs = refs[2 + n_ex:]
        av = a_ref[...].astype(BF16)
        for n0 in range(0, N, nc):
            cols = slice(n0, n0 + nc)
            bv = (b_ref[cols, :] if trans_b else b_ref[:, cols]).astype(BF16)
            acc = lax.dot_general(av, bv, NT if trans_b else NN, preferred_element_type=F32)
            res = epilogue(acc, *[e[:, cols] for e in ex]) if epilogue is not None else (acc,)
            for o, r in zip(outs, res):
                o[:, cols] = r.astype(o.dtype)

    in_specs = [pl.BlockSpec((tm, K), lambda i: (i, 0)), pl.BlockSpec(b.shape, lambda i: (0, 0))]
    for e in extras:
        if e.shape == (M, N):
            in_specs.append(pl.BlockSpec((tm, N), lambda i: (i, 0)))
        elif e.shape == (1, N):
            in_specs.append(pl.BlockSpec((1, N), lambda i: (0, 0)))
        else:
            raise ValueError(f"extra operand of shape {e.shape} for a ({M}, {N}) result")
    res = pl.pallas_call(
        body, name=name, grid=(M // tm,), in_specs=in_specs,
        out_specs=[pl.BlockSpec((tm, N), lambda i: (i, 0)) for _ in out_dtypes],
        out_shape=[jax.ShapeDtypeStruct((M, N), d) for d in out_dtypes],
        compiler_params=_cparams("parallel"),
    )(a, b, *extras)
    return res[0] if n_out == 1 else tuple(res)


def _mm_tn(a, b, *, name):
    T, K = a.shape
    N = b.shape[1]
    bk, bn = K, N
    while bk * bn * 4 > MM_OUT_BYTES:
        if bn >= bk and bn % (2 * LANES) == 0:
            bn //= 2
        else:
            bk //= 2
    tt = _pick(T, (1024, 512, 256))
    ck, cn = _pick(bk, (MM_CHUNK, 384, 256, 128)), _pick(bn, (MM_CHUNK, 384, 256, 128))

    def body(a_ref, b_ref, o_ref):
        t = pl.program_id(2)

        @pl.when(t == 0)
        def _():
            o_ref[...] = jnp.zeros_like(o_ref)

        for r0 in range(0, bk, ck):
            av = a_ref[:, r0:r0 + ck].astype(BF16)
            for c0 in range(0, bn, cn):
                o_ref[r0:r0 + ck, c0:c0 + cn] += lax.dot_general(
                    av, b_ref[:, c0:c0 + cn].astype(BF16), TN, preferred_element_type=F32)

    return pl.pallas_call(
        body, name=name, grid=(K // bk, N // bn, T // tt),
        in_specs=[pl.BlockSpec((tt, bk), lambda i, j, t: (t, i)), pl.BlockSpec((tt, bn), lambda i, j, t: (t, j))],
        out_specs=pl.BlockSpec((bk, bn), lambda i, j, t: (i, j)),
        out_shape=jax.ShapeDtypeStruct((K, N), F32),
        compiler_params=_cparams("parallel", "parallel", "arbitrary"),
    )(a, b)


ROW_TILE = 256


def _row_spec(cols, col_block=0):
    return pl.BlockSpec((ROW_TILE, cols), lambda i: (i, col_block))


def _tab_spec(cols, period):
    return pl.BlockSpec((ROW_TILE, cols), lambda i: (i % period, 0))


def _full_spec(shape):
    return pl.BlockSpec(shape, lambda i: (0,) * len(shape))


def _ln_fwd(x, m, g, b, *, name):
    T, D = x.shape

    def body(x_ref, m_ref, g_ref, b_ref, y_ref, yb_ref, xh_ref, r_ref):
        z = DN_ALPHA * x_ref[...] + m_ref[...]
        mu = jnp.mean(z, -1, keepdims=True)
        zc = z - mu
        r = lax.rsqrt(jnp.mean(zc * zc, -1, keepdims=True) + NORM_EPS)
        xh = zc * r
        y = xh * g_ref[...] + b_ref[...]
        y_ref[...] = y
        yb_ref[...] = y.astype(BF16)
        xh_ref[...] = xh
        r_ref[...] = r

    return pl.pallas_call(
        body, name=name, grid=(T // ROW_TILE,),
        in_specs=[_row_spec(D), _row_spec(D), _full_spec((1, D)), _full_spec((1, D))],
        out_specs=[_row_spec(D), _row_spec(D), _row_spec(D), _row_spec(1)],
        out_shape=[jax.ShapeDtypeStruct((T, D), F32), jax.ShapeDtypeStruct((T, D), BF16),
                   jax.ShapeDtypeStruct((T, D), F32), jax.ShapeDtypeStruct((T, 1), F32)],
        compiler_params=_cparams("parallel"),
    )(x, m, g, b)


def _ln_bwd(dy, xh, r, g, *, name):
    T, D = dy.shape

    def body(dy_ref, xh_ref, r_ref, g_ref, dz_ref, dzb_ref, dg_ref, db_ref):
        dyv, xhv = dy_ref[...], xh_ref[...]
        dyg = dyv * g_ref[...]
        c1 = jnp.mean(dyg, -1, keepdims=True)
        c2 = jnp.mean(dyg * xhv, -1, keepdims=True)
        dz = r_ref[...] * (dyg - c1 - xhv * c2)
        dz_ref[...] = dz
        dzb_ref[...] = dz.astype(BF16)

        @pl.when(pl.program_id(0) == 0)
        def _():
            dg_ref[...] = jnp.zeros_like(dg_ref)
            db_ref[...] = jnp.zeros_like(db_ref)

        dg_ref[...] += jnp.sum(dyv * xhv, 0, keepdims=True)
        db_ref[...] += jnp.sum(dyv, 0, keepdims=True)

    return pl.pallas_call(
        body, name=name, grid=(T // ROW_TILE,),
        in_specs=[_row_spec(D), _row_spec(D), _row_spec(1), _full_spec((1, D))],
        out_specs=[_row_spec(D), _row_spec(D), _full_spec((1, D)), _full_spec((1, D))],
        out_shape=[jax.ShapeDtypeStruct((T, D), F32), jax.ShapeDtypeStruct((T, D), BF16),
                   jax.ShapeDtypeStruct((1, D), F32), jax.ShapeDtypeStruct((1, D), F32)],
        compiler_params=_cparams("arbitrary"),
    )(dy, xh, r, g)


def _loss_grad(y, target, *, name):
    T, D = y.shape

    def body(y_ref, t_ref, dy_ref, sq_ref):
        err = y_ref[...] - t_ref[...]
        dy_ref[...] = err / D

        @pl.when(pl.program_id(0) == 0)
        def _():
            sq_ref[...] = jnp.zeros_like(sq_ref)

        sq_ref[...] += jnp.sum(err * err, 0, keepdims=True)

    return pl.pallas_call(
        body, name=name, grid=(T // ROW_TILE,),
        in_specs=[_row_spec(D), _row_spec(D)],
        out_specs=[_row_spec(D), _full_spec((1, D))],
        out_shape=[jax.ShapeDtypeStruct((T, D), F32), jax.ShapeDtypeStruct((1, D), F32)],
        compiler_params=_cparams("arbitrary"),
    )(y, target)


def _ple_bwd_elem(dx3, g, e, *, name):
    T, D = dx3.shape

    def body(dx_ref, g_ref, e_ref, de_ref, dz_ref, db_ref):
        dx, gv = dx_ref[...], g_ref[...]
        de_ref[...] = (dx * gv).astype(BF16)
        dz = dx * e_ref[...] * gv * (1.0 - gv)
        dz_ref[...] = dz.astype(BF16)

        @pl.when(pl.program_id(0) == 0)
        def _():
            db_ref[...] = jnp.zeros_like(db_ref)

        db_ref[...] += jnp.sum(dz, 0, keepdims=True)

    return pl.pallas_call(
        body, name=name, grid=(T // ROW_TILE,),
        in_specs=[_row_spec(D), _row_spec(D), _row_spec(D)],
        out_specs=[_row_spec(D), _row_spec(D), _full_spec((1, D))],
        out_shape=[jax.ShapeDtypeStruct((T, D), BF16), jax.ShapeDtypeStruct((T, D), BF16),
                   jax.ShapeDtypeStruct((1, D), F32)],
        compiler_params=_cparams("arbitrary"),
    )(dx3, g, e)


def _rotate(xv, a, bm, bp, sign):
    half = MLA_ROPE // 2
    width = xv.shape[-1]
    return xv * a + sign * (pltpu.roll(xv, width - half, 1) * bm + pltpu.roll(xv, half, 1) * bp)


def _rope(x, tabs, seq, *, sign, name):
    T, width = x.shape

    def body(x_ref, a_ref, bm_ref, bp_ref, o_ref):
        o_ref[...] = _rotate(x_ref[...], a_ref[...], bm_ref[...], bp_ref[...], sign).astype(BF16)

    return pl.pallas_call(
        body, name=name, grid=(T // ROW_TILE,),
        in_specs=[_row_spec(width)] + [_tab_spec(width, seq // ROW_TILE)] * 3,
        out_specs=_row_spec(width),
        out_shape=jax.ShapeDtypeStruct((T, width), BF16),
        compiler_params=_cparams("parallel"),
    )(x, *tabs)


def _mla_keys(knp, h, k_tabs, seq, *, name):
    T = knp.shape[0]

    def body(k_ref, h_ref, a_ref, bm_ref, bp_ref, o_ref):
        kr = _rotate(h_ref[...], a_ref[...], bm_ref[...], bp_ref[...], 1.0)
        for hd in range(MLA_HEADS):
            cols = slice(hd * LANES, (hd + 1) * LANES)
            o_ref[:, cols] = (k_ref[:, cols].astype(F32) + kr).astype(BF16)

    return pl.pallas_call(
        body, name=name, grid=(T // ROW_TILE,),
        in_specs=[_row_spec(MLA_HEADS * LANES), _row_spec(LANES, EV_KR[0] // LANES)]
        + [_tab_spec(LANES, seq // ROW_TILE)] * 3,
        out_specs=_row_spec(MLA_HEADS * LANES),
        out_shape=jax.ShapeDtypeStruct((T, MLA_HEADS * LANES), BF16),
        compiler_params=_cparams("parallel"),
    )(knp, h, *k_tabs)


def _mla_rope_key_grad(dk, k_tabs, seq, *, name):
    T = dk.shape[0]

    def body(dk_ref, a_ref, bm_ref, bp_ref, o_ref):
        tot = dk_ref[:, 0:LANES]
        for hd in range(1, MLA_HEADS):
            tot = tot + dk_ref[:, hd * LANES:(hd + 1) * LANES]
        o_ref[...] = _rotate(tot, a_ref[...], bm_ref[...], bp_ref[...], -1.0).astype(BF16)

    return pl.pallas_call(
        body, name=name, grid=(T // ROW_TILE,),
        in_specs=[_row_spec(MLA_HEADS * LANES)] + [_tab_spec(LANES, seq // ROW_TILE)] * 3,
        out_specs=_row_spec(LANES),
        out_shape=jax.ShapeDtypeStruct((T, LANES), BF16),
        compiler_params=_cparams("parallel"),
    )(dk, *k_tabs)


def _even_norms(h, gq, gkv, *, name):
    T = h.shape[0]

    def body(h_ref, gq_ref, gkv_ref, cq_ref, ckv_ref, rq_ref, rkv_ref):
        cq = h_ref[:, EV_CQ[0]:EV_CQ[1]]
        rq = lax.rsqrt(jnp.mean(cq * cq, -1, keepdims=True) + NORM_EPS)
        cq_ref[...] = (cq * rq * gq_ref[...]).astype(BF16)
        rq_ref[...] = rq
        ckv = h_ref[:, EV_CKV[0]:EV_CKV[1]]
        rkv = lax.rsqrt(jnp.mean(ckv * ckv, -1, keepdims=True) + NORM_EPS)
        ckv_ref[...] = (ckv * rkv * gkv_ref[...]).astype(BF16)
        rkv_ref[...] = rkv

    return pl.pallas_call(
        body, name=name, grid=(T // ROW_TILE,),
        in_specs=[_row_spec(EVEN_IN_PAD), _full_spec((1, MLA_Q_LORA)), _full_spec((1, MLA_KV_LORA))],
        out_specs=[_row_spec(MLA_Q_LORA), _row_spec(MLA_KV_LORA), _row_spec(1), _row_spec(1)],
        out_shape=[jax.ShapeDtypeStruct((T, MLA_Q_LORA), BF16), jax.ShapeDtypeStruct((T, MLA_KV_LORA), BF16),
                   jax.ShapeDtypeStruct((T, 1), F32), jax.ShapeDtypeStruct((T, 1), F32)],
        compiler_params=_cparams("parallel"),
    )(h, gq, gkv)


def _even_in_bwd(h, rq, rkv, gq, gkv, dcqn, dckvn, dqs, dks, dvs, dkr, *, name):
    T = h.shape[0]

    def rms_bwd(c, r, g, dy):
        xr = c * r
        dyg = dy * g
        return r * (dyg - xr * jnp.mean(dyg * xr, -1, keepdims=True)), jnp.sum(dy * xr, 0, keepdims=True)

    def body(h_ref, rq_ref, rkv_ref, gq_ref, gkv_ref, dcq_ref, dckv_ref, dqs_ref, dks_ref, dvs_ref, dkr_ref,
             dh_ref, dgq_ref, dgkv_ref):
        @pl.when(pl.program_id(0) == 0)
        def _():
            dgq_ref[...] = jnp.zeros_like(dgq_ref)
            dgkv_ref[...] = jnp.zeros_like(dgkv_ref)

        dcq, dgq = rms_bwd(h_ref[:, EV_CQ[0]:EV_CQ[1]], rq_ref[...], gq_ref[...], dcq_ref[...])
        dckv, dgkv = rms_bwd(h_ref[:, EV_CKV[0]:EV_CKV[1]], rkv_ref[...], gkv_ref[...], dckv_ref[...])
        dgq_ref[...] += dgq
        dgkv_ref[...] += dgkv
        dh_ref[:, EV_QS[0]:EV_QS[1]] = dqs_ref[...]
        dh_ref[:, EV_CQ[0]:EV_CQ[1]] = dcq.astype(BF16)
        dh_ref[:, EV_CKV[0]:EV_CKV[1]] = dckv.astype(BF16)
        dh_ref[:, EV_KS[0]:EV_KS[1]] = dks_ref[...]
        dh_ref[:, EV_VS[0]:EV_VS[1]] = dvs_ref[...]
        dh_ref[:, EV_KR[0]:EV_KR[1]] = dkr_ref[...]

    return pl.pallas_call(
        body, name=name, grid=(T // ROW_TILE,),
        in_specs=[_row_spec(EVEN_IN_PAD), _row_spec(1), _row_spec(1), _full_spec((1, MLA_Q_LORA)),
                  _full_spec((1, MLA_KV_LORA)), _row_spec(MLA_Q_LORA), _row_spec(MLA_KV_LORA),
                  _row_spec(SWA_HEADS * HEAD_DIM), _row_spec(LANES), _row_spec(LANES), _row_spec(LANES)],
        out_specs=[_row_spec(EVEN_IN_PAD), _full_spec((1, MLA_Q_LORA)), _full_spec((1, MLA_KV_LORA))],
        out_shape=[jax.ShapeDtypeStruct((T, EVEN_IN_PAD), BF16), jax.ShapeDtypeStruct((1, MLA_Q_LORA), F32),
                   jax.ShapeDtypeStruct((1, MLA_KV_LORA), F32)],
        compiler_params=_cparams("arbitrary"),
    )(h, rq, rkv, gq, gkv, dcqn, dckvn, dqs, dks, dvs, dkr)


def _fox_decay_fwd(f3, bf, *, name):
    B, S, _ = f3.shape

    def body(f_ref, b_ref, csh_ref, chs_ref):
        x = f_ref[...] + b_ref[...]
        c = jnp.minimum(x, 0.0) - jnp.log1p(jnp.exp(-jnp.abs(x)))
        row = lax.broadcasted_iota(jnp.int32, (S, LANES), 0)
        k = 1
        while k < S:
            c = c + jnp.where(row >= k, pltpu.roll(c, k, 0), 0.0)
            k *= 2
        csh_ref[...] = c
        chs_ref[...] = c.T

    return pl.pallas_call(
        body, name=name, grid=(B,),
        in_specs=[pl.BlockSpec((None, S, LANES), lambda b: (b, 0, 0)), pl.BlockSpec((1, LANES), lambda b: (0, 0))],
        out_specs=[pl.BlockSpec((None, S, LANES), lambda b: (b, 0, 0)),
                   pl.BlockSpec((None, LANES, S), lambda b: (b, 0, 0))],
        out_shape=[jax.ShapeDtypeStruct((B, S, LANES), F32), jax.ShapeDtypeStruct((B, LANES, S), F32)],
        compiler_params=_cparams("parallel"),
    )(f3, bf)


def _fox_decay_bwd(dc_hs, f3, bf, *, name):
    B, S, _ = f3.shape

    def body(dc_ref, f_ref, b_ref, df_ref, db_ref):
        g = dc_ref[...].T
        row = lax.broadcasted_iota(jnp.int32, (S, LANES), 0)
        k = 1
        while k < S:
            g = g + jnp.where(row < S - k, pltpu.roll(g, S - k, 0), 0.0)
            k *= 2
        x = f_ref[...] + b_ref[...]
        df = g * (1.0 / (1.0 + jnp.exp(x)))
        df_ref[...] = df.astype(BF16)

        @pl.when(pl.program_id(0) == 0)
        def _():
            db_ref[...] = jnp.zeros_like(db_ref)

        db_ref[...] += jnp.sum(df, 0, keepdims=True)

    return pl.pallas_call(
        body, name=name, grid=(B,),
        in_specs=[pl.BlockSpec((None, LANES, S), lambda b: (b, 0, 0)),
                  pl.BlockSpec((None, S, LANES), lambda b: (b, 0, 0)), pl.BlockSpec((1, LANES), lambda b: (0, 0))],
        out_specs=[pl.BlockSpec((None, S, LANES), lambda b: (b, 0, 0)), pl.BlockSpec((1, LANES), lambda b: (0, 0))],
        out_shape=[jax.ShapeDtypeStruct((B, S, LANES), BF16), jax.ShapeDtypeStruct((1, LANES), F32)],
        compiler_params=_cparams("arbitrary"),
    )(dc_hs, f3, bf)


def _head_column(block, h):
    lane = lax.broadcasted_iota(jnp.int32, block.shape, 1)
    return jnp.sum(jnp.where(lane == h, block, 0.0), axis=-1, keepdims=True)


def _causal_mask(s):
    r = lax.broadcasted_iota(jnp.int32, s.shape, 0)
    c = lax.broadcasted_iota(jnp.int32, s.shape, 1)
    return jnp.where(c <= r, s, NEG_INF)


def _low_half(shape):
    return (lax.broadcasted_iota(jnp.int32, shape, 1) % LANES) < HEAD_DIM


def _widen(x, cols):
    return jnp.concatenate([x] * (cols // LANES), axis=1)


def _both_halves(x, lo):
    r = pltpu.roll(x, HEAD_DIM, 1)
    return jnp.where(lo, x, r), jnp.where(lo, r, x)


MESH_ID = pl.DeviceIdType.MESH
HBM_SPEC = pl.BlockSpec(memory_space=pltpu.HBM)
VMEM_SPEC = pl.BlockSpec(memory_space=pltpu.VMEM)


def _mesh_place():
    x, y, c = lax.axis_index("x"), lax.axis_index("y"), lax.axis_index("c")
    return x, y, c, 4 * x + 2 * y + c


def _peers(x, y, c):
    out = []
    for mask in range(1, N_DEV):
        dx, dy, dc = (mask >> 2) & 1, (mask >> 1) & 1, mask & 1
        px, py, pc = (1 - x if dx else x), (1 - y if dy else y), (1 - c if dc else c)
        out.append(((px, py, pc), 4 * px + 2 * py + pc))
    return out


def _comm_out_shapes(comm):
    return [jax.ShapeDtypeStruct(((N_DEV,) + a.shape) if kind == "gather" else a.shape, a.dtype) for kind, a in comm]


def _comm_scratch(comm):
    n = len(comm)
    return [pltpu.SemaphoreType.DMA((n, 7)), pltpu.SemaphoreType.DMA((n, 7)), pltpu.SemaphoreType.DMA((n,))]


def _comm_copies(kinds, in_refs, out_refs, sems, place):
    send_sems, recv_sems, local_sems = sems
    x, y, c, me = place
    local, remote = [], []
    for w, kind in enumerate(kinds):
        mine = in_refs[w] if kind == "gather" else in_refs[w].at[me]
        local.append(pltpu.make_async_copy(mine, out_refs[w].at[me], local_sems.at[w]))
        for k, (peer, peer_idx) in enumerate(_peers(x, y, c)):
            remote.append(pltpu.make_async_remote_copy(
                src_ref=in_refs[w] if kind == "gather" else in_refs[w].at[peer_idx], dst_ref=out_refs[w].at[me],
                send_sem=send_sems.at[w, k], recv_sem=recv_sems.at[w, k], device_id=peer, device_id_type=MESH_ID))
    return local, remote


def _comm_start(kinds, in_refs, out_refs, sems, place):
    local, remote = _comm_copies(kinds, in_refs, out_refs, sems, place)
    for cp in local + remote:
        cp.start()


def _comm_wait(kinds, in_refs, out_refs, sems, place):
    local, remote = _comm_copies(kinds, in_refs, out_refs, sems, place)
    for cp in remote:
        cp.wait_recv()
    for cp in remote:
        cp.wait_send()
    for cp in local:
        cp.wait()


def _exchange(comm, *, name):
    n = len(comm)
    kinds = [k for k, _ in comm]

    def body(*refs):
        place = _mesh_place()
        _comm_start(kinds, refs[:n], refs[n:2 * n], refs[2 * n:], place)
        _comm_wait(kinds, refs[:n], refs[n:2 * n], refs[2 * n:], place)

    return pl.pallas_call(
        body, name=name, out_shape=_comm_out_shapes(comm), in_specs=[HBM_SPEC] * n, out_specs=[HBM_SPEC] * n,
        scratch_shapes=_comm_scratch(comm),
    )(*[a for _, a in comm])


def _flash_fwd(qa, ka, va, *, q_blk0, k_blk0, v_blk0, W, n_pairs, B, S, scale, csh=None, crow=None, comm=(), name):
    t = ATT_TILE
    nq = S // t
    P = PAIRS_PER_STEP_FWD
    decay = csh is not None
    split = W == LANES
    assert n_pairs % P == 0 and q_blk0 % P == 0 and k_blk0 % P == 0 and v_blk0 % P == 0
    n_c, kinds = len(comm), [k for k, _ in comm]
    n_in = 5 if decay else 3
    n_steps = (B, n_pairs // P, nq)

    def body(*refs):
        c_in, c_out = refs[n_in:n_in + n_c], refs[n_in + n_c + 2:n_in + 2 * n_c + 2]
        sems = refs[n_in + 2 * n_c + 4:]
        refs = refs[:n_in] + refs[n_in + n_c:n_in + n_c + 2] + refs[n_in + 2 * n_c + 2:n_in + 2 * n_c + 4]
        if decay:
            q_ref, k_ref, v_ref, csh_ref, crow_ref, o_ref, lse_ref, m_s, acc_s = refs
        else:
            q_ref, k_ref, v_ref, o_ref, lse_ref, m_s, acc_s = refs
        g, i = pl.program_id(1), pl.program_id(2)
        if n_c:
            place = _mesh_place()
            ids = [pl.program_id(ax) for ax in range(3)]

            @pl.when((ids[0] == 0) & (ids[1] == 0) & (ids[2] == 0))
            def _():
                _comm_start(kinds, c_in, c_out, sems, place)

        lo = _low_half((t, LANES))
        qv = q_ref[...]
        qh = []
        for pr in range(P):
            qp = qv[:, pr * W:(pr + 1) * W]
            qh += [jnp.where(lo, qp, jnp.zeros_like(qp)), jnp.where(lo, jnp.zeros_like(qp), qp)] if split \
                else [qp[:, :LANES], qp[:, LANES:]]
        if decay:
            cq = [jnp.broadcast_to(_head_column(csh_ref[...], 2 * P * g + hd), (t, LANES)) for hd in range(2 * P)]
        m_s[...] = jnp.full(m_s.shape, NEG_INF, F32)
        acc_s[...] = jnp.zeros(acc_s.shape, F32)

        def step(j, masked):
            rows = pl.ds(pl.multiple_of(j * t, t), t)
            kb, vb = k_ref[rows, :], v_ref[rows, :]
            for pr in range(P):
                kp, vp = kb[:, pr * W:(pr + 1) * W], vb[:, pr * LANES:(pr + 1) * LANES]
                ones = jnp.ones_like(vp)
                vaug = [jnp.where(lo, vp, ones), jnp.where(lo, ones, vp)]
                for half in range(2):
                    hd = 2 * pr + half
                    kh = kp if split else kp[:, half * LANES:(half + 1) * LANES]
                    s = lax.dot_general(qh[hd], kh, NT, preferred_element_type=F32) * scale
                    if decay:
                        s = s + _widen(cq[hd], t) - crow_ref[hd, j]
                    if masked:
                        s = _causal_mask(s)
                    m_prev = m_s[hd]
                    m_new = jnp.maximum(m_prev, jnp.max(s, -1, keepdims=True))
                    p = jnp.exp(s - _widen(m_new, t))
                    acc_s[hd] = jnp.exp(m_prev - m_new) * acc_s[hd] + lax.dot_general(
                        p.astype(BF16), vaug[half], NN, preferred_element_type=F32)
                    m_s[hd] = m_new

        def loop_body(j, carry):
            step(j, False)
            return carry

        lax.fori_loop(0, i, loop_body, 0)
        step(i, True)
        for pr in range(P):
            acc0, acc1 = acc_s[2 * pr], acc_s[2 * pr + 1]
            _, l0 = _both_halves(acc0, lo)
            l1, _ = _both_halves(acc1, lo)
            cols = slice(pr * LANES, (pr + 1) * LANES)
            o_ref[:, cols] = jnp.where(lo, acc0 / l0, acc1 / l1).astype(BF16)
            lse_ref[:, cols] = jnp.where(lo, m_s[2 * pr] + jnp.log(l0), m_s[2 * pr + 1] + jnp.log(l1))
        if n_c:
            @pl.when((ids[0] == n_steps[0] - 1) & (ids[1] == n_steps[1] - 1) & (ids[2] == n_steps[2] - 1))
            def _():
                _comm_wait(kinds, c_in, c_out, sems, place)

    in_specs = [pl.BlockSpec((t, P * W), lambda b, g, i: (b * nq + i, q_blk0 // P + g)),
                pl.BlockSpec((S, P * W), lambda b, g, i: (b, k_blk0 // P + g)),
                pl.BlockSpec((S, P * LANES), lambda b, g, i: (b, v_blk0 // P + g))]
    args = [qa, ka, va]
    if decay:
        in_specs += [pl.BlockSpec((None, t, LANES), lambda b, g, i: (b, i, 0)),
                     pl.BlockSpec((None, 2 * P, nq, 1, t), lambda b, g, i: (b, g, 0, 0, 0))]
        args += [csh, crow]
    out_spec = pl.BlockSpec((t, P * LANES), lambda b, g, i: (b * nq + i, g))
    res = pl.pallas_call(
        body, name=name, grid=n_steps, in_specs=in_specs + [HBM_SPEC] * n_c,
        out_specs=[out_spec, out_spec] + [HBM_SPEC] * n_c,
        out_shape=[jax.ShapeDtypeStruct((B * S, n_pairs * LANES), BF16),
                   jax.ShapeDtypeStruct((B * S, n_pairs * LANES), F32)] + _comm_out_shapes(comm),
        scratch_shapes=[pltpu.VMEM((2 * P, t, LANES), F32), pltpu.VMEM((2 * P, t, LANES), F32)]
        + (_comm_scratch(comm) if n_c else []),
        compiler_params=_cparams(*(("arbitrary",) * 3 if n_c else ("parallel",) * 3)),
    )(*args, *[a for _, a in comm])
    return res[0], res[1], list(res[2:])


def _flash_bwd(qa, ka, va, oa, doa, lsea, *, q_blk0, k_blk0, v_blk0, do_blk0, W, n_pairs, B, S, scale, qk_dtype,
               csh=None, crow=None, comm=(), name):
    t = ATT_TILE
    nq = S // t
    P = PAIRS_PER_STEP_BWD
    decay = csh is not None
    split = W == LANES
    assert n_pairs % P == 0 and q_blk0 % P == 0 and k_blk0 % P == 0 and v_blk0 % P == 0 and do_blk0 % P == 0
    n_c, kinds = len(comm), [k for k, _ in comm]
    n_in, n_out, n_scr = (8, 5, 8) if decay else (6, 3, 5)
    n_steps = (B, n_pairs // P, nq)

    def body(*refs):
        c_in = refs[n_in:n_in + n_c]
        c_out = refs[n_in + n_c + n_out:n_in + 2 * n_c + n_out]
        sems = refs[n_in + 2 * n_c + n_out + n_scr:]
        refs = (refs[:n_in] + refs[n_in + n_c:n_in + n_c + n_out]
                + refs[n_in + 2 * n_c + n_out:n_in + 2 * n_c + n_out + n_scr])
        if n_c:
            place = _mesh_place()
            ids = [pl.program_id(ax) for ax in range(3)]

            @pl.when((ids[0] == 0) & (ids[1] == 0) & (ids[2] == 0))
            def _():
                _comm_start(kinds, c_in, c_out, sems, place)

        if decay:
            (q_ref, k_ref, v_ref, o_ref, do_ref, lse_ref, csh_ref, crow_ref, dq_ref, dk_ref, dv_ref, dck_ref, dcq_ref,
             dq_s, lse_s, delta_s, dk_s, dv_s, cq_s, dcq_s, dck_s) = refs
        else:
            (q_ref, k_ref, v_ref, o_ref, do_ref, lse_ref, dq_ref, dk_ref, dv_ref,
             dq_s, lse_s, delta_s, dk_s, dv_s) = refs
        g, j = pl.program_id(1), pl.program_id(2)
        lo = _low_half((t, LANES))

        @pl.when(j == 0)
        def _():
            lo_s = _low_half((S, LANES))
            dq_s[...] = jnp.zeros(dq_s.shape, F32)
            for pr in range(P):
                cols = slice(pr * LANES, (pr + 1) * LANES)
                lse_s[2 * pr], lse_s[2 * pr + 1] = _both_halves(lse_ref[:, cols], lo_s)
                dd = do_ref[:, cols].astype(F32) * o_ref[:, cols].astype(F32)
                delta_s[2 * pr] = jnp.broadcast_to(jnp.sum(jnp.where(lo_s, dd, 0.0), -1, keepdims=True), (S, LANES))
                delta_s[2 * pr + 1] = jnp.broadcast_to(jnp.sum(jnp.where(lo_s, 0.0, dd), -1, keepdims=True),
                                                       (S, LANES))
            if decay:
                for hd in range(2 * P):
                    cq_s[hd] = jnp.broadcast_to(_head_column(csh_ref[...], 2 * P * g + hd), (S, LANES))
                dcq_s[...] = jnp.zeros(dcq_s.shape, F32)

        kb, vb = k_ref[...], v_ref[...]
        kh, vh = [], []
        for pr in range(P):
            kp, vp = kb[:, pr * W:(pr + 1) * W], vb[:, pr * LANES:(pr + 1) * LANES]
            zk, zv = jnp.zeros_like(kp), jnp.zeros_like(vp)
            kh += [jnp.where(lo, kp, zk), jnp.where(lo, zk, kp)] if split else [kp[:, :LANES], kp[:, LANES:]]
            vh += [jnp.where(lo, vp, zv), jnp.where(lo, zv, vp)]
        dk_s[...] = jnp.zeros(dk_s.shape, F32)
        dv_s[...] = jnp.zeros(dv_s.shape, F32)
        if decay:
            dck_s[...] = jnp.zeros(dck_s.shape, F32)

        def step(i, masked):
            rows = pl.ds(pl.multiple_of(i * t, t), t)
            qi, doi = q_ref[rows, :], do_ref[rows, :]
            for pr in range(P):
                qp, dop = qi[:, pr * W:(pr + 1) * W], doi[:, pr * LANES:(pr + 1) * LANES]
                for half in range(2):
                    hd = 2 * pr + half
                    qx = qp if split else qp[:, half * LANES:(half + 1) * LANES]
                    s = lax.dot_general(qx, kh[hd], NT, preferred_element_type=F32) * scale
                    if decay:
                        s = s + _widen(cq_s[hd, rows, :], t) - crow_ref[hd, j]
                    if masked:
                        s = _causal_mask(s)
                    p = jnp.exp(s - _widen(lse_s[hd, rows, :], t))
                    dv_s[hd] += lax.dot_general(p.astype(BF16), dop, TN, preferred_element_type=F32)
                    dp = lax.dot_general(dop, vh[hd], NT, preferred_element_type=F32)
                    ds = p * (dp - _widen(delta_s[hd, rows, :], t))
                    dss = (ds * scale).astype(BF16)
                    dk_s[hd] += lax.dot_general(dss, qx, TN, preferred_element_type=F32)
                    dqc = lax.dot_general(dss, kh[hd], NN, preferred_element_type=F32)
                    if split:
                        dq_s[rows, pr * W:(pr + 1) * W] += dqc
                    else:
                        dq_s[rows, hd * LANES:(hd + 1) * LANES] += dqc
                    if decay:
                        dck_s[hd] -= jnp.sum(ds, 0, keepdims=True)
                        part = ds[:, :LANES]
                        for c in range(1, t // LANES):
                            part = part + ds[:, c * LANES:(c + 1) * LANES]
                        dcq_s[hd, rows, :] += part

        def loop_body(i, carry):
            step(i, False)
            return carry

        step(j, True)
        lax.fori_loop(j + 1, nq, loop_body, 0)
        for pr in range(P):
            if split:
                dk_ref[:, pr * W:(pr + 1) * W] = jnp.where(lo, dk_s[2 * pr], dk_s[2 * pr + 1]).astype(dk_ref.dtype)
            else:
                for half in range(2):
                    hd = 2 * pr + half
                    dk_ref[:, hd * LANES:(hd + 1) * LANES] = dk_s[hd].astype(dk_ref.dtype)
            dv_ref[:, pr * LANES:(pr + 1) * LANES] = jnp.where(lo, dv_s[2 * pr], dv_s[2 * pr + 1]).astype(BF16)
        if decay:
            dck_ref[...] = dck_s[...]

        @pl.when(j == nq - 1)
        def _():
            dq_ref[...] = dq_s[...].astype(dq_ref.dtype)
            if decay:
                for hd in range(2 * P):
                    dcq_ref[hd] = jnp.sum(dcq_s[hd].T, 0, keepdims=True)

        if n_c:
            @pl.when((ids[0] == n_steps[0] - 1) & (ids[1] == n_steps[1] - 1) & (ids[2] == n_steps[2] - 1))
            def _():
                _comm_wait(kinds, c_in, c_out, sems, place)

    full = lambda w, blk0: pl.BlockSpec((S, P * w), lambda b, g, j: (b, blk0 // P + g))
    blk = lambda w, blk0: pl.BlockSpec((t, P * w), lambda b, g, j: (b * nq + j, blk0 // P + g))
    in_specs = [full(W, q_blk0), blk(W, k_blk0), blk(LANES, v_blk0), full(LANES, 0), full(LANES, do_blk0),
                full(LANES, 0)]
    args = [qa, ka, va, oa, doa, lsea]
    T = B * S
    out_specs = [full(W, 0), blk(W, 0), blk(LANES, 0)]
    out_shape = [jax.ShapeDtypeStruct((T, n_pairs * W), qk_dtype), jax.ShapeDtypeStruct((T, n_pairs * W), qk_dtype),
                 jax.ShapeDtypeStruct((T, n_pairs * LANES), BF16)]
    per_head = lambda rows: pltpu.VMEM((2 * P, rows, LANES), F32)
    scratch = [pltpu.VMEM((S, P * W), F32), per_head(S), per_head(S), per_head(t), per_head(t)]
    if decay:
        in_specs += [pl.BlockSpec((None, S, LANES), lambda b, g, j: (b, 0, 0)),
                     pl.BlockSpec((None, 2 * P, nq, 1, t), lambda b, g, j: (b, g, 0, 0, 0))]
        args += [csh, crow]
        out_specs += [pl.BlockSpec((None, 2 * P, None, 1, t), lambda b, g, j: (b, g, j, 0, 0)),
                      pl.BlockSpec((None, 2 * P, 1, S), lambda b, g, j: (b, g, 0, 0))]
        out_shape += [jax.ShapeDtypeStruct((B, 2 * n_pairs, nq, 1, t), F32),
                      jax.ShapeDtypeStruct((B, 2 * n_pairs, 1, S), F32)]
        scratch += [per_head(S), per_head(S), pltpu.VMEM((2 * P, 1, t), F32)]
    res = pl.pallas_call(
        body, name=name, grid=n_steps, in_specs=in_specs + [HBM_SPEC] * n_c,
        out_specs=out_specs + [HBM_SPEC] * n_c, out_shape=out_shape + _comm_out_shapes(comm),
        scratch_shapes=scratch + (_comm_scratch(comm) if n_c else []),
        compiler_params=_cparams(*(("arbitrary",) * 3 if n_c else ("parallel", "parallel", "arbitrary"))),
    )(*args, *[a for _, a in comm])
    return tuple(res[:n_out]) + (list(res[n_out:]),)


def _swa_common(q_ref, kp_ref, ko_ref, vp_ref, vo_ref, n):
    Q = BLOCK_Q
    lo = _low_half((Q, LANES))
    lo2 = _low_half((2 * Q, LANES))
    kk = jnp.concatenate([kp_ref[...], ko_ref[...]], axis=0)
    vv = jnp.concatenate([vp_ref[...], vo_ref[...]], axis=0)
    kdup = [x.astype(BF16) for x in _both_halves(kk, lo2)]
    vdup = [x.astype(BF16) for x in _both_halves(vv, lo2)]
    a = lax.broadcasted_iota(jnp.int32, (Q, 2 * Q), 0)
    col = lax.broadcasted_iota(jnp.int32, (Q, 2 * Q), 1)
    dist = a + Q - col
    valid = (dist >= 0) & (dist < SWA_WINDOW) & ((col >= Q) | (n > 0))
    qv = q_ref[...]
    qm = []
    for a_head in range(SWA_HEADS):
        qp = qv[:, (a_head // 2) * LANES:(a_head // 2 + 1) * LANES]
        keep = lo if a_head % 2 == 0 else jnp.logical_not(lo)
        qm.append(jnp.where(keep, qp, 0.0).astype(BF16))
    return lo, lo2, kdup, vdup, valid, qm


def _swa_in_specs(nb):
    Q = BLOCK_Q
    own = lambda blk: (lambda b, n: (b * nb + n, blk))
    prev = lambda blk: (lambda b, n: (b * nb + jnp.maximum(n - 1, 0), blk))
    kb, vb = EV_KS[0] // LANES, EV_VS[0] // LANES
    return [pl.BlockSpec((Q, SWA_HEADS * HEAD_DIM), own(0)), pl.BlockSpec((Q, LANES), prev(kb)),
            pl.BlockSpec((Q, LANES), own(kb)), pl.BlockSpec((Q, LANES), prev(vb)), pl.BlockSpec((Q, LANES), own(vb))]


def _swa_fwd(h, bias, sinkcol, *, B, S, name):
    Q = BLOCK_Q
    nb = S // Q
    scale = HEAD_DIM ** -0.5

    def body(q_ref, kp_ref, ko_ref, vp_ref, vo_ref, bias_ref, sink_ref, o_ref, lse_ref):
        lo, _, kdup, vdup, valid, qm = _swa_common(q_ref, kp_ref, ko_ref, vp_ref, vo_ref, pl.program_id(1))
        lane = lax.broadcasted_iota(jnp.int32, (Q, LANES), 1)
        lse_blk = jnp.zeros((Q, LANES), F32)
        pairs = []
        for pr in range(SWA_HEADS // 2):
            pv = []
            for half in range(2):
                a = 2 * pr + half
                kvh = a // (SWA_HEADS // SWA_KV_HEADS)
                s = lax.dot_general(qm[a], kdup[kvh], NT, preferred_element_type=F32) * scale + bias_ref[a]
                s = jnp.where(valid, s, NEG_INF)
                sink = sink_ref[a]
                mx = jnp.maximum(jnp.max(s, -1, keepdims=True), sink)
                p = jnp.exp(s - mx)
                l = jnp.sum(p, -1, keepdims=True) + jnp.exp(sink - mx)
                pv.append(lax.dot_general((p / l).astype(BF16), vdup[kvh], NN, preferred_element_type=F32))
                lse_blk = jnp.where(lane == a, mx + jnp.log(l), lse_blk)
            pairs.append(jnp.where(lo, pv[0], pv[1]))
        o_ref[...] = jnp.concatenate(pairs, axis=1).astype(BF16)
        lse_ref[...] = lse_blk

    whole = lambda shape: pl.BlockSpec(shape, lambda b, n: (0,) * len(shape))
    return pl.pallas_call(
        body, name=name, grid=(B, nb),
        in_specs=_swa_in_specs(nb) + [whole((SWA_HEADS, Q, 2 * Q)), whole((SWA_HEADS, Q, 1))],
        out_specs=[pl.BlockSpec((Q, SWA_HEADS * HEAD_DIM), lambda b, n: (b * nb + n, 0)),
                   pl.BlockSpec((Q, LANES), lambda b, n: (b * nb + n, 0))],
        out_shape=[jax.ShapeDtypeStruct((B * S, SWA_HEADS * HEAD_DIM), BF16),
                   jax.ShapeDtypeStruct((B * S, LANES), F32)],
        compiler_params=_cparams("parallel", "parallel"),
    )(h, h, h, h, h, bias, sinkcol)


def _swa_bwd(h, o, do, lse, bias, sinkcol, *, do_blk0, B, S, name):
    Q = BLOCK_Q
    nb = S // Q
    scale = HEAD_DIM ** -0.5
    group = SWA_HEADS // SWA_KV_HEADS

    def body(q_ref, kp_ref, ko_ref, vp_ref, vo_ref, o_ref, do_ref, lse_ref, bias_ref, sink_ref,
             dq_ref, dko_ref, dkp_ref, dvo_ref, dvp_ref, dbias_ref, dsink_ref):
        @pl.when((pl.program_id(0) == 0) & (pl.program_id(1) == 0))
        def _():
            dbias_ref[...] = jnp.zeros_like(dbias_ref)
            dsink_ref[...] = jnp.zeros_like(dsink_ref)

        lo, lo2, kdup, vdup, valid, qm = _swa_common(q_ref, kp_ref, ko_ref, vp_ref, vo_ref, pl.program_id(1))
        lse_blk = lse_ref[...]
        dkk = [jnp.zeros((2 * Q, LANES), F32) for _ in range(SWA_KV_HEADS)]
        dvv = [jnp.zeros((2 * Q, LANES), F32) for _ in range(SWA_KV_HEADS)]
        dq_pairs = []
        for pr in range(SWA_HEADS // 2):
            cols = slice(pr * LANES, (pr + 1) * LANES)
            do_p, o_p = do_ref[:, cols], o_ref[:, cols]
            dq_half = []
            for half in range(2):
                a = 2 * pr + half
                kvh = a // group
                keep = lo if half == 0 else jnp.logical_not(lo)
                s = lax.dot_general(qm[a], kdup[kvh], NT, preferred_element_type=F32) * scale + bias_ref[a]
                s = jnp.where(valid, s, NEG_INF)
                lse_a = _head_column(lse_blk, a)
                p = jnp.exp(s - lse_a)
                doh = jnp.where(keep, do_p, jnp.zeros_like(do_p))
                delta = jnp.sum(doh.astype(F32) * o_p.astype(F32), -1, keepdims=True)
                dp = lax.dot_general(doh, vdup[kvh], NT, preferred_element_type=F32)
                ds = p * (dp - delta)
                dbias_ref[a] += ds
                dsink_ref[a] -= jnp.exp(sink_ref[a] - lse_a) * delta
                dss = (ds * scale).astype(BF16)
                dq_half.append(lax.dot_general(dss, kdup[kvh], NN, preferred_element_type=F32))
                dkk[kvh] = dkk[kvh] + lax.dot_general(dss, qm[a], TN, preferred_element_type=F32)
                dvv[kvh] = dvv[kvh] + lax.dot_general(p.astype(BF16), doh, TN, preferred_element_type=F32)
            dq_pairs.append(jnp.where(lo, dq_half[0], dq_half[1]))
        dq_ref[...] = jnp.concatenate(dq_pairs, axis=1).astype(BF16)
        fold = lambda x: x + pltpu.roll(x, HEAD_DIM, 1)
        dk_blk = jnp.where(lo2, fold(dkk[0]), fold(dkk[1]))
        dv_blk = jnp.where(lo2, fold(dvv[0]), fold(dvv[1]))
        dkp_ref[...] = dk_blk[:Q]
        dko_ref[...] = dk_blk[Q:]
        dvp_ref[...] = dv_blk[:Q]
        dvo_ref[...] = dv_blk[Q:]

    whole = lambda shape: pl.BlockSpec(shape, lambda b, n: (0,) * len(shape))
    wide = lambda blk: pl.BlockSpec((Q, SWA_HEADS * HEAD_DIM), lambda b, n: (b * nb + n, blk))
    narrow = pl.BlockSpec((Q, LANES), lambda b, n: (b * nb + n, 0))
    kv_shape = jax.ShapeDtypeStruct((B * S, LANES), F32)
    return pl.pallas_call(
        body, name=name, grid=(B, nb),
        in_specs=_swa_in_specs(nb) + [wide(0), wide(do_blk0), narrow, whole((SWA_HEADS, Q, 2 * Q)),
                                      whole((SWA_HEADS, Q, 1))],
        out_specs=[wide(0), narrow, narrow, narrow, narrow, whole((SWA_HEADS, Q, 2 * Q)), whole((SWA_HEADS, Q, 1))],
        out_shape=[jax.ShapeDtypeStruct((B * S, SWA_HEADS * HEAD_DIM), BF16), kv_shape, kv_shape, kv_shape, kv_shape,
                   jax.ShapeDtypeStruct((SWA_HEADS, Q, 2 * Q), F32), jax.ShapeDtypeStruct((SWA_HEADS, Q, 1), F32)],
        compiler_params=_cparams("arbitrary", "arbitrary"),
    )(h, h, h, h, h, o, do, lse, bias, sinkcol)


def _bias_bucket_sum(dbias, bucket, *, name):
    def body(d_ref, b_ref, o_ref):
        dbv, bk = d_ref[...], b_ref[...]
        lane = lax.broadcasted_iota(jnp.int32, (SWA_HEADS, LANES), 1)
        out = jnp.zeros((SWA_HEADS, LANES), F32)
        for b in range(REL_BUCKETS):
            part = jnp.sum(jnp.where(bk == b, dbv, 0.0), axis=1)
            tot = jnp.sum(part, axis=-1, keepdims=True)
            out = out + jnp.where(lane == b, tot, 0.0)
        o_ref[...] = out

    return pl.pallas_call(
        body, name=name, out_shape=jax.ShapeDtypeStruct((SWA_HEADS, LANES), F32),
        compiler_params=pltpu.CompilerParams(vmem_limit_bytes=VMEM_LIMIT_BYTES),
    )(dbias, bucket)


def _adamw_update(w, g, m, v):
    m_new = ADAM_B1 * m + (1.0 - ADAM_B1) * g
    v_new = ADAM_B2 * v + (1.0 - ADAM_B2) * jnp.square(g)
    m_hat = m_new / (1.0 - ADAM_B1 ** ADAM_STEP)
    v_hat = v_new / (1.0 - ADAM_B2 ** ADAM_STEP)
    return -ADAM_LR * (m_hat / (jnp.sqrt(v_hat) + ADAM_EPS) + ADAM_WD * w), m_new, v_new


def _adamw(w, g, m, v, *, name):
    def body(w_ref, g_ref, m_ref, v_ref, d_ref, nm_ref, nv_ref):
        d_ref[...], nm_ref[...], nv_ref[...] = _adamw_update(w_ref[...], g_ref[...], m_ref[...], v_ref[...])

    return pl.pallas_call(
        body, name=name, out_shape=[jax.ShapeDtypeStruct(w.shape, F32)] * 3,
        compiler_params=pltpu.CompilerParams(vmem_limit_bytes=VMEM_LIMIT_BYTES),
    )(w, g, m, v)


def _adamw_slots(w, parts, m, v, *, name):
    R, C = w.shape
    tr = R if R <= 512 else _pick(R, (256, 128))

    def body(w_ref, p_ref, m_ref, v_ref, g_ref, d_ref, nm_ref, nv_ref):
        g = p_ref[0].astype(F32)
        for j in range(1, N_DEV):
            g = g + p_ref[j].astype(F32)
        g_ref[...] = g
        d_ref[...], nm_ref[...], nv_ref[...] = _adamw_update(w_ref[...], g, m_ref[...], v_ref[...])

    spec = pl.BlockSpec((tr, C), lambda i: (i, 0))
    return pl.pallas_call(
        body, name=name, grid=(R // tr,),
        in_specs=[spec, pl.BlockSpec((N_DEV, tr, C), lambda i: (0, i, 0)), spec, spec], out_specs=[spec] * 4,
        out_shape=[jax.ShapeDtypeStruct((R, C), F32)] * 4, compiler_params=_cparams("parallel"),
    )(w, parts, m, v)


def _all_gather_hbm(blocks, *, name):
    n = len(blocks)

    def body(*refs):
        x_refs, out_refs = refs[:n], refs[n:2 * n]
        send_sems, recv_sems, local_sems = refs[2 * n:]
        x, y, c, _ = _mesh_place()
        me, sibling = (x, y, c), (x, y, 1 - c)
        chips = [(1 - x, y), (x, 1 - y), (1 - x, 1 - y)]

        def copy(w, k, blk, to, src=None):
            px, py, pc = blk
            slot = out_refs[w].at[4 * px + 2 * py + pc]
            return pltpu.make_async_remote_copy(
                src_ref=slot if src is None else src, dst_ref=slot,
                send_sem=send_sems.at[w, k], recv_sem=recv_sems.at[w, k], device_id=to, device_id_type=MESH_ID)

        mine = [pltpu.make_async_copy(x_refs[w], out_refs[w].at[4 * x + 2 * y + c], local_sems.at[w])
                for w in range(n)]
        for cp in mine:
            cp.start()
        first = []
        for w in range(n):
            first.append(copy(w, 0, me, sibling, src=x_refs[w]))
            first += [copy(w, 1 + j, me, (*chip, c), src=x_refs[w]) for j, chip in enumerate(chips)]
        for cp in first:
            cp.start()
        passed = []
        for j, chip in enumerate(chips):
            for w in range(n):
                copy(w, 1 + j, (*chip, c), me).wait_recv()
                fwd = copy(w, 4 + j, (*chip, c), sibling)
                fwd.start()
                passed.append(fwd)
        for w in range(n):
            copy(w, 0, sibling, me).wait_recv()
            for j, chip in enumerate(chips):
                copy(w, 4 + j, (*chip, 1 - c), me).wait_recv()
        for cp in first + passed:
            cp.wait_send()
        for cp in mine:
            cp.wait()

    return pl.pallas_call(
        body, name=name, out_shape=[jax.ShapeDtypeStruct((N_DEV,) + b.shape, b.dtype) for b in blocks],
        in_specs=[HBM_SPEC] * n, out_specs=[HBM_SPEC] * n,
        scratch_shapes=[pltpu.SemaphoreType.DMA((n, 7)), pltpu.SemaphoreType.DMA((n, 7)),
                        pltpu.SemaphoreType.DMA((n,))],
    )(*blocks)


def _all_reduce_small(block, *, name):
    R, W = block.shape

    def body(x_ref, out_ref, buf, send_sems, recv_sems):
        x, y, c, me = _mesh_place()
        copies = []
        for k, (peer, _) in enumerate(_peers(x, y, c)):
            copies.append(pltpu.make_async_remote_copy(
                src_ref=x_ref, dst_ref=buf.at[me], send_sem=send_sems.at[k], recv_sem=recv_sems.at[k],
                device_id=peer, device_id_type=MESH_ID))
        for cp in copies:
            cp.start()
        buf[me] = x_ref[...]
        for cp in copies:
            cp.wait_recv()
        for cp in copies:
            cp.wait_send()
        acc = buf[0]
        for j in range(1, N_DEV):
            acc = acc + buf[j]
        out_ref[...] = acc

    return pl.pallas_call(
        body, name=name, out_shape=jax.ShapeDtypeStruct((R, W), F32),
        in_specs=[VMEM_SPEC], out_specs=VMEM_SPEC,
        scratch_shapes=[pltpu.VMEM((N_DEV, R, W), F32), pltpu.SemaphoreType.DMA((7,)), pltpu.SemaphoreType.DMA((7,))],
    )(block)


def _assemble(name, g):
    if BIG_AXIS[name] == 2:
        return jnp.concatenate([g[j] for j in range(N_DEV)], axis=1)
    return g.reshape(N_DEV * g.shape[1], g.shape[2])


def _split_for_devices(name, g):
    if BIG_AXIS[name] == 2:
        b = g.shape[1] // N_DEV
        return jnp.stack([g[:, j * b:(j + 1) * b] for j in range(N_DEV)]).astype(BF16)
    return g.reshape(N_DEV, g.shape[0] // N_DEV, g.shape[1]).astype(BF16)


def _layer_weight_keys(i):
    j = i // 2
    mixer = [('ev_w_in', j), ('ev_w_uq', j), ('ev_w_ukv', j), ('ev_w_out', j)] if i % 2 == 0 \
        else [('od_w_in', j), ('od_w_out', j)]
    return mixer + [('w_up', i), ('w_down', i), ('ple_w_proj', i), ('ple_w_gate', i)]


class _MeshExchange:
    def __init__(self, shards):
        self.shards = shards
        self.weights = {}
        self.pending = []
        self.in_flight = []
        self.received = {}

    def layer_weights(self, i):
        keys = _layer_weight_keys(i)
        if i == 0:
            got = _all_gather_hbm([self.shards[k] for k in keys], name="gather_l0")
            self.weights[0] = {k[0]: _assemble(k[0], g) for k, g in zip(keys, got)}
        return self.weights[i]

    def fwd_items(self, i):
        if i + 1 >= DEPTH:
            return []
        return [("gather", self.shards[k]) for k in _layer_weight_keys(i + 1)]

    def fwd_done(self, i, outs):
        if outs:
            keys = _layer_weight_keys(i + 1)
            self.weights[i + 1] = {k[0]: _assemble(k[0], g) for k, g in zip(keys, outs)}

    def push_grads(self, grads):
        self.pending += [(k, _split_for_devices(k[0], g)) for k, g in grads.items()]

    def bwd_items(self):
        self.in_flight, self.pending = self.pending, []
        return [("scatter", parts) for _, parts in self.in_flight]

    def bwd_done(self, outs):
        for (k, _), got in zip(self.in_flight, outs):
            self.received[k] = got
        self.in_flight = []

    def finish(self):
        if self.pending:
            outs = _exchange(self.bwd_items(), name="exchange_rest")
            self.bwd_done(outs)
        return self.received


PACK_ROWS = 8


def _pack_small(vals):
    flat = jnp.concatenate([vals[n].reshape(-1).astype(F32) for n in SMALL])
    pad = (-flat.shape[0]) % (PACK_ROWS * LANES)
    return jnp.pad(flat, (0, pad)).reshape(-1, LANES)


def _unpack_small(block, shapes):
    flat = block.reshape(-1)
    out, off = {}, 0
    for n in SMALL:
        sz = math.prod(shapes[n])
        out[n] = flat[off:off + sz].reshape(shapes[n])
        off += sz
    return out


def _rope_tables(S):
    half = MLA_ROPE // 2
    inv = 1.0 / (ROPE_THETA ** (jnp.arange(0, MLA_ROPE, 2, dtype=F32) / MLA_ROPE))
    ang = jnp.arange(S, dtype=F32)[:, None] * inv[None, :]
    cos, sin = jnp.cos(ang), jnp.sin(ang)
    zeros = jnp.zeros((S, half), F32)
    tail = jnp.zeros((S, LANES - MLA_QK), F32)

    def block(rope_part, nope_val):
        return jnp.concatenate([jnp.full((S, MLA_NOPE), nope_val, F32), rope_part, tail], -1)

    a_r = jnp.concatenate([cos, cos], -1)
    bm_r = jnp.concatenate([-sin, zeros], -1)
    bp_r = jnp.concatenate([zeros, sin], -1)
    q_tabs = tuple(jnp.tile(block(r, v), (1, MLA_HEADS)) for r, v in ((a_r, 1.0), (bm_r, 0.0), (bp_r, 0.0)))
    k_tabs = tuple(block(r, 0.0) for r in (a_r, bm_r, bp_r))
    return q_tabs, k_tabs


def _t5_bucket(dist):
    exact = REL_BUCKETS // 2
    d = jnp.maximum(dist, 1).astype(F32)
    large = exact + (jnp.log(d / exact) / math.log(REL_MAX_DIST / exact) * (REL_BUCKETS - exact)).astype(jnp.int32)
    large = jnp.minimum(large, REL_BUCKETS - 1)
    return jnp.where(dist < exact, dist, large)


def _swa_bucket_table():
    a = jnp.arange(BLOCK_Q)[:, None]
    col = jnp.arange(2 * BLOCK_Q)[None, :]
    return _t5_bucket(jnp.maximum(a + BLOCK_Q - col, 0)).astype(jnp.int32)


def _even_weights(W):
    w = W['ev_w_in']
    c_kv1 = MLA_Q_LORA + MLA_KV_LORA
    c_kr1 = c_kv1 + MLA_ROPE
    c_qs1 = c_kr1 + SWA_HEADS * HEAD_DIM
    zeros = lambda n: jnp.zeros((D_MODEL, n), w.dtype)
    w_in = jnp.concatenate([w[:, c_kr1:c_qs1], w[:, :c_kv1], w[:, c_qs1:], zeros(KR_LANE0), w[:, c_kv1:c_kr1],
                            zeros(LANES - KR_LANE0 - MLA_ROPE)], axis=1)
    uq = W['ev_w_uq'].reshape(MLA_Q_LORA, MLA_HEADS, MLA_QK)
    w_uq = jnp.pad(uq, ((0, 0), (0, 0), (0, LANES - MLA_QK))).reshape(MLA_Q_LORA, MLA_HEADS * LANES)
    ukv = W['ev_w_ukv'].reshape(MLA_KV_LORA, MLA_HEADS, MLA_NOPE + MLA_V)
    w_k = jnp.pad(ukv[..., :MLA_NOPE], ((0, 0), (0, 0), (0, LANES - MLA_NOPE))).reshape(MLA_KV_LORA, -1)
    w_v = ukv[..., MLA_NOPE:].reshape(MLA_KV_LORA, MLA_HEADS * MLA_V)
    return w_in, w_uq, w_k, w_v, W['ev_w_out']


def _even_in_grad_unpad(dw):
    kr0 = EV_KR[0] + KR_LANE0
    return jnp.concatenate([dw[:, EV_CQ[0]:EV_CKV[1]], dw[:, kr0:kr0 + MLA_ROPE], dw[:, EV_QS[0]:EV_QS[1]],
                            dw[:, EV_KS[0]:EV_VS[1]]], axis=1)


def _even_fwd(xb, W, P, i, B, S, tabs, xchg, tag):
    j = i // 2
    q_tabs, k_tabs, bias, sinkcol = tabs
    w_in, w_uq, w_k, w_v, w_out = _even_weights(W)
    h = _mm(xb, w_in, name=f"{tag}_in")
    cqn, ckvn, rq, rkv = _even_norms(h, P['ev_q_norm'][j][None], P['ev_kv_norm'][j][None], name=f"{tag}_norms")
    q = _rope(_mm(cqn, w_uq, name=f"{tag}_uq"), q_tabs, S, sign=1.0, name=f"{tag}_ropeq")
    knp = _mm(ckvn, w_k, out_dtypes=(BF16,), name=f"{tag}_uk")
    v = _mm(ckvn, w_v, out_dtypes=(BF16,), name=f"{tag}_uv")
    k = _mla_keys(knp, h, k_tabs, S, name=f"{tag}_keys")
    o_mla, lse_mla, got = _flash_fwd(q, k, v, q_blk0=0, k_blk0=0, v_blk0=0, W=2 * LANES, n_pairs=MLA_HEADS // 2,
                                     B=B, S=S, scale=MLA_QK ** -0.5, comm=xchg.fwd_items(i), name=f"{tag}_mla")
    xchg.fwd_done(i, got)
    o_swa, lse_swa = _swa_fwd(h, bias, sinkcol, B=B, S=S, name=f"{tag}_swa")
    o_cat = jnp.concatenate([o_mla, o_swa], axis=-1)
    m = _mm(o_cat, w_out, name=f"{tag}_out")
    res = dict(h=h, cqn=cqn, ckvn=ckvn, rq=rq, rkv=rkv, q=q, k=k, v=v, o_mla=o_mla, lse_mla=lse_mla,
               o_swa=o_swa, lse_swa=lse_swa, o_cat=o_cat)
    return m, res


def _shift_prev(own, prev, B, S):
    prev = prev.reshape(B, S, LANES)
    shifted = jnp.concatenate([prev[:, BLOCK_Q:], jnp.zeros_like(prev[:, :BLOCK_Q])], axis=1)
    return (own + shifted.reshape(B * S, LANES)).astype(BF16)


def _even_bwd(dmb, dz1, xb, W, P, j, B, S, tabs, res, xchg, tag):
    q_tabs, k_tabs, bias, sinkcol = tabs
    w_in, w_uq, w_k, w_v, w_out = _even_weights(W)
    g = {}
    g['ev_w_out'] = _mm_tn(res['o_cat'], dmb, name=f"{tag}_dwout")
    do = _mm(dmb, w_out, trans_b=True, out_dtypes=(BF16,), name=f"{tag}_do")
    dq, dk, dv, got = _flash_bwd(res['q'], res['k'], res['v'], res['o_mla'], do, res['lse_mla'], q_blk0=0, k_blk0=0,
                                 v_blk0=0, do_blk0=0, W=2 * LANES, n_pairs=MLA_HEADS // 2, B=B, S=S,
                                 scale=MLA_QK ** -0.5, qk_dtype=F32, comm=xchg.bwd_items(), name=f"{tag}_mla_bwd")
    xchg.bwd_done(got)
    dq_pre = _rope(dq, q_tabs, S, sign=-1.0, name=f"{tag}_ropeq_bwd")
    dw_uq = _mm_tn(res['cqn'], dq_pre, name=f"{tag}_dwuq")
    g['ev_w_uq'] = dw_uq.reshape(MLA_Q_LORA, MLA_HEADS, LANES)[..., :MLA_QK].reshape(MLA_Q_LORA, MLA_HEADS * MLA_QK)
    dcqn = _mm(dq_pre, w_uq, trans_b=True, name=f"{tag}_dcqn")
    dw_k = _mm_tn(res['ckvn'], dk, name=f"{tag}_dwuk").reshape(MLA_KV_LORA, MLA_HEADS, LANES)[..., :MLA_NOPE]
    dw_v = _mm_tn(res['ckvn'], dv, name=f"{tag}_dwuv").reshape(MLA_KV_LORA, MLA_HEADS, MLA_V)
    g['ev_w_ukv'] = jnp.concatenate([dw_k, dw_v], axis=-1).reshape(MLA_KV_LORA, MLA_HEADS * (MLA_NOPE + MLA_V))
    dckvn_v = _mm(dv, w_v, trans_b=True, name=f"{tag}_dckvn_v")
    dckvn = _mm(dk, w_k, trans_b=True, extras=(dckvn_v,), epilogue=lambda acc, r: (acc + r,), name=f"{tag}_dckvn")
    dkr_pre = _mla_rope_key_grad(dk, k_tabs, S, name=f"{tag}_ropek_bwd")
    dqs, dko, dkp, dvo, dvp, dbias, dsink = _swa_bwd(res['h'], res['o_swa'], do, res['lse_swa'], bias, sinkcol,
                                                     do_blk0=1, B=B, S=S, name=f"{tag}_swa_bwd")
    dh, dgq, dgkv = _even_in_bwd(res['h'], res['rq'], res['rkv'], P['ev_q_norm'][j][None], P['ev_kv_norm'][j][None],
                                 dcqn, dckvn, dqs, _shift_prev(dko, dkp, B, S), _shift_prev(dvo, dvp, B, S), dkr_pre,
                                 name=f"{tag}_in_bwd")
    g['ev_w_in'] = _even_in_grad_unpad(_mm_tn(xb, dh, name=f"{tag}_dwin"))
    dx = _mm(dh, w_in, trans_b=True, extras=(dz1,), epilogue=lambda acc, r: (acc + DN_ALPHA * r,), name=f"{tag}_dx")
    small = dict(ev_q_norm=dgq[0], ev_kv_norm=dgkv[0], dbias=dbias, ev_sinks=jnp.sum(dsink, axis=(1, 2)))
    return dx, g, small


def _odd_fwd(xb, W, P, i, B, S, xchg, tag):
    j = i // 2
    w = W['od_w_in']
    w_qkv = w[:, :ODD_QKV]
    w_f = jnp.pad(w[:, ODD_QKV:], ((0, 0), (0, LANES - FOX_HEADS)))
    bf = jnp.pad(P['od_b_f'][j], (0, LANES - FOX_HEADS))[None]
    qkv = _mm(xb, w_qkv, out_dtypes=(BF16,), name=f"{tag}_qkv")
    f = _mm(xb, w_f, name=f"{tag}_f").reshape(B, S, LANES)
    csh, chs = _fox_decay_fwd(f, bf, name=f"{tag}_decay")
    crow = chs[:, :FOX_HEADS].reshape(B, FOX_HEADS, S // ATT_TILE, 1, ATT_TILE)
    n_blk = FOX_HEADS * HEAD_DIM // LANES
    o, lse, got = _flash_fwd(qkv, qkv, qkv, q_blk0=0, k_blk0=n_blk, v_blk0=2 * n_blk, W=LANES,
                             n_pairs=FOX_HEADS // 2, B=B, S=S, scale=HEAD_DIM ** -0.5, csh=csh, crow=crow,
                             comm=xchg.fwd_items(i), name=f"{tag}_fox")
    xchg.fwd_done(i, got)
    m = _mm(o, W['od_w_out'], name=f"{tag}_out")
    res = dict(f=f, bf=bf, csh=csh, crow=crow, qkv=qkv, o=o, lse=lse, w_qkv=w_qkv, w_f=w_f)
    return m, res


def _odd_bwd(dmb, dz1, xb, W, P, j, B, S, res, xchg, tag):
    g = {}
    w_out = W['od_w_out']
    g['od_w_out'] = _mm_tn(res['o'], dmb, name=f"{tag}_dwout")
    do = _mm(dmb, w_out, trans_b=True, out_dtypes=(BF16,), name=f"{tag}_do")
    qkv = res['qkv']
    n_blk = FOX_HEADS * HEAD_DIM // LANES
    dq, dk, dv, dck, dcq, got = _flash_bwd(qkv, qkv, qkv, res['o'], do, res['lse'], q_blk0=0, k_blk0=n_blk,
                                           v_blk0=2 * n_blk, do_blk0=0, W=LANES, n_pairs=FOX_HEADS // 2, B=B, S=S,
                                           scale=HEAD_DIM ** -0.5, qk_dtype=BF16, csh=res['csh'], crow=res['crow'],
                                           comm=xchg.bwd_items(), name=f"{tag}_fox_bwd")
    xchg.bwd_done(got)
    dc = dck.reshape(B, FOX_HEADS, S) + dcq.reshape(B, FOX_HEADS, S)
    dc_hs = jnp.pad(dc, ((0, 0), (0, LANES - FOX_HEADS), (0, 0)))
    df, dbf = _fox_decay_bwd(dc_hs, res['f'], res['bf'], name=f"{tag}_decay_bwd")
    df = df.reshape(B * S, LANES)
    dqkv = jnp.concatenate([dq, dk, dv], axis=-1)
    dw_qkv = _mm_tn(xb, dqkv, name=f"{tag}_dwqkv")
    dw_f = _mm_tn(xb, df, name=f"{tag}_dwf")
    g['od_w_in'] = jnp.concatenate([dw_qkv, dw_f[:, :FOX_HEADS]], axis=1)
    dxf = _mm(df, res['w_f'], trans_b=True, extras=(dz1,), epilogue=lambda acc, r: (acc + DN_ALPHA * r,),
              name=f"{tag}_dxf")
    dx = _mm(dqkv, res['w_qkv'], trans_b=True, extras=(dxf,), epilogue=lambda acc, r: (acc + r,), name=f"{tag}_dx")
    small = dict(od_b_f=dbf[0, :FOX_HEADS])
    return dx, g, small


def _local_step(x, p, target, P, xchg):
    B, S, D = x.shape
    T = B * S
    q_tabs, k_tabs = _rope_tables(S)
    bucket = _swa_bucket_table()
    bias = P['rel_bias'][bucket].astype(F32).transpose(2, 0, 1)

    xc = x.reshape(T, D)
    xcb = xc.astype(BF16)
    saved = []
    for i in range(DEPTH):
        j = i // 2
        tag = f"l{i}"
        W = xchg.layer_weights(i)
        lay = dict(xb=xcb, W=W)
        if i % 2 == 0:
            sinkcol = jnp.broadcast_to(P['ev_sinks'][j][:, None, None], (SWA_HEADS, BLOCK_Q, 1)).astype(F32)
            lay['tabs'] = (q_tabs, k_tabs, bias, sinkcol)
            m, lay['mix'] = _even_fwd(xcb, W, P, i, B, S, lay['tabs'], xchg, tag)
        else:
            m, lay['mix'] = _odd_fwd(xcb, W, P, i, B, S, xchg, tag)
        x1, x1b, lay['xh1'], lay['r1'] = _ln_fwd(xc, m, P['ln1_g'][i][None], P['ln1_b'][i][None], name=f"{tag}_ln1")
        lay['x1b'] = x1b
        lay['u'], lay['a'] = _mm(x1b, W['w_up'], out_dtypes=(F32, BF16),
                                 epilogue=lambda acc: (acc, jnp.square(jnp.maximum(acc, 0.0))), name=f"{tag}_up")
        d = _mm(lay['a'], W['w_down'], name=f"{tag}_down")
        x2, x2b, lay['xh2'], lay['r2'] = _ln_fwd(x1, d, P['ln2_g'][i][None], P['ln2_b'][i][None], name=f"{tag}_ln2")
        lay['x2b'] = x2b
        lay['p'] = p[i].reshape(T, D_PLE)
        lay['e'] = _mm(lay['p'], W['ple_w_proj'], name=f"{tag}_ple_proj")

        def gate(acc, bg, e, x2v):
            gv = 1.0 / (1.0 + jnp.exp(-(acc + bg)))
            y = x2v + gv * e
            return y, y, gv

        xc, xcb, lay['g'] = _mm(x2b, W['ple_w_gate'], extras=(P['ple_b_gate'][i][None], lay['e'], x2),
                                epilogue=gate, out_dtypes=(F32, BF16, F32), name=f"{tag}_ple_gate")
        saved.append(lay)

    dy, sq = _loss_grad(xc, target.reshape(T, D), name="loss")

    Gs = {n: [None] * DEPTH for n in ('ln1_g', 'ln1_b', 'ln2_g', 'ln2_b', 'ple_b_gate')}
    Gs.update({n: [None] * (DEPTH // 2) for n in ('ev_q_norm', 'ev_kv_norm', 'ev_sinks', 'od_b_f')})
    dbias_total = None
    for i in reversed(range(DEPTH)):
        j = i // 2
        tag = f"l{i}b"
        lay = saved[i]
        W = lay['W']
        de, dzg, dbg = _ple_bwd_elem(dy, lay['g'], lay['e'], name=f"{tag}_ple_elem")
        Gs['ple_b_gate'][i] = dbg[0]
        g_mlp = {('ple_w_proj', i): _mm_tn(lay['p'], de, name=f"{tag}_dwproj"),
                 ('ple_w_gate', i): _mm_tn(lay['x2b'], dzg, name=f"{tag}_dwgate")}
        dx2 = _mm(dzg, W['ple_w_gate'], trans_b=True, extras=(dy,), epilogue=lambda acc, r: (acc + r,),
                  name=f"{tag}_dx2")
        dz2, dz2b, dg2, db2 = _ln_bwd(dx2, lay['xh2'], lay['r2'], P['ln2_g'][i][None], name=f"{tag}_ln2")
        Gs['ln2_g'][i], Gs['ln2_b'][i] = dg2[0], db2[0]
        g_mlp[('w_down', i)] = _mm_tn(lay['a'], dz2b, name=f"{tag}_dwdown")
        du = _mm(dz2b, W['w_down'], trans_b=True, extras=(lay['u'],), out_dtypes=(BF16,),
                 epilogue=lambda acc, u: (acc * (2.0 * jnp.maximum(u, 0.0)),), name=f"{tag}_du")
        g_mlp[('w_up', i)] = _mm_tn(lay['x1b'], du, name=f"{tag}_dwup")
        xchg.push_grads(g_mlp)
        dx1 = _mm(du, W['w_up'], trans_b=True, extras=(dz2,), epilogue=lambda acc, r: (acc + DN_ALPHA * r,),
                  name=f"{tag}_dx1")
        dz1, dz1b, dg1, db1 = _ln_bwd(dx1, lay['xh1'], lay['r1'], P['ln1_g'][i][None], name=f"{tag}_ln1")
        Gs['ln1_g'][i], Gs['ln1_b'][i] = dg1[0], db1[0]
        if i % 2 == 0:
            dy, g, small = _even_bwd(dz1b, dz1, lay['xb'], W, P, j, B, S, lay['tabs'], lay['mix'], xchg, tag)
            dbias_total = small['dbias'] if dbias_total is None else dbias_total + small['dbias']
            for n in ('ev_q_norm', 'ev_kv_norm', 'ev_sinks'):
                Gs[n][j] = small[n]
        else:
            dy, g, small = _odd_bwd(dz1b, dz1, lay['xb'], W, P, j, B, S, lay['mix'], xchg, tag)
            Gs['od_b_f'][j] = small['od_b_f']
        xchg.push_grads({(n, j): val for n, val in g.items()})

    grads_small = {n: jnp.stack(v) for n, v in Gs.items()}
    drel = _bias_bucket_sum(dbias_total, bucket, name="rel_bias_grad")
    grads_small['rel_bias'] = drel[:, :REL_BUCKETS].T
    return sq, dy.reshape(B, S, D), grads_small


def kernel(x, p, rel_bias, ev_w_in, ev_q_norm, ev_w_uq, ev_kv_norm, ev_w_ukv, ev_sinks, ev_w_out, od_w_in, od_b_f, od_w_out, ln1_g, ln1_b, w_up, w_down, ln2_g, ln2_b, ple_w_proj, ple_w_gate, ple_b_gate, loss_target, m_rel_bias, m_ev_w_in, m_ev_q_norm, m_ev_w_uq, m_ev_kv_norm, m_ev_w_ukv, m_ev_sinks, m_ev_w_out, m_od_w_in, m_od_b_f, m_od_w_out, m_ln1_g, m_ln1_b, m_w_up, m_w_down, m_ln2_g, m_ln2_b, m_ple_w_proj, m_ple_w_gate, m_ple_b_gate, v_rel_bias, v_ev_w_in, v_ev_q_norm, v_ev_w_uq, v_ev_kv_norm, v_ev_w_ukv, v_ev_sinks, v_ev_w_out, v_od_w_in, v_od_b_f, v_od_w_out, v_ln1_g, v_ln1_b, v_w_up, v_w_down, v_ln2_g, v_ln2_b, v_ple_w_proj, v_ple_w_gate, v_ple_b_gate):
    given = dict(locals())
    w = {n: given[n] for n in WEIGHTS}
    mom = {n: given["m_" + n] for n in WEIGHTS}
    var = {n: given["v_" + n] for n in WEIGHTS}
    small_shapes = {n: w[n].shape for n in SMALL}

    shards = {(n, idx): w[n][idx].astype(BF16) for n in BIG for idx in range(w[n].shape[0])}
    xchg = _MeshExchange(shards)
    P = {n: w[n] for n in SMALL}

    sq, grad_x, grads_small = _local_step(x, p, loss_target, P, xchg)
    loss = lax.psum(0.5 * jnp.sum(sq) / D_MODEL, ("x", "y", "c"))

    received = xchg.finish()
    g_small_packed = _all_reduce_small(_pack_small(grads_small), name="reduce_small_grads")
    g_small = _unpack_small(g_small_packed, small_shapes)

    grad, delta, new_m, new_v = {}, {}, {}, {}
    for n in BIG:
        per_layer = [_adamw_slots(w[n][idx], received[(n, idx)], mom[n][idx], var[n][idx], name=f"adamw_{n}{idx}")
                     for idx in range(w[n].shape[0])]
        grad[n], delta[n], new_m[n], new_v[n] = (jnp.stack(t) for t in zip(*per_layer))
    d, nm, nv = _adamw(_pack_small(w), g_small_packed, _pack_small(mom), _pack_small(var), name="adamw_small")
    d, nm, nv = (_unpack_small(t, small_shapes) for t in (d, nm, nv))
    for n in SMALL:
        grad[n], delta[n], new_m[n], new_v[n] = g_small[n], d[n], nm[n], nv[n]

    return (loss, grad_x, *[grad[n] for n in WEIGHTS], *[delta[n] for n in WEIGHTS],
            *[new_m[n] for n in WEIGHTS], *[new_v[n] for n in WEIGHTS])
```

```python
import math

import jax
import jax.numpy as jnp
from jax import lax
from jax.experimental import pallas as pl
from jax.experimental.pallas import tpu as pltpu

F32, BF16 = jnp.float32, jnp.bfloat16

D_MODEL = 1024
DEPTH = 4
HEAD_DIM = 64
MLA_HEADS, MLA_NOPE, MLA_ROPE, MLA_V = 8, 64, 32, 64
MLA_Q_LORA, MLA_KV_LORA = 384, 256
MLA_QK = MLA_NOPE + MLA_ROPE
ROPE_THETA = 10000.0
SWA_HEADS, SWA_KV_HEADS, SWA_WINDOW = 8, 2, 128
REL_BUCKETS, REL_MAX_DIST = 32, 128
FOX_HEADS = 16
D_FF = 4 * D_MODEL
D_PLE = 256
BLOCK_Q = 128
DN_ALPHA = (2 * DEPTH) ** 0.25
NORM_EPS = 1e-5
NEG_INF = -1e30
EVEN_IN = 1440
ODD_QKV = 3 * FOX_HEADS * HEAD_DIM
LANES = 128

EV_QS = (0, 512)
EV_CQ = (512, 896)
EV_CKV = (896, 1152)
EV_KS = (1152, 1280)
EV_VS = (1280, 1408)
EV_KR = (1408, 1536)
EVEN_IN_PAD = 1536
KR_LANE0 = MLA_NOPE

ADAM_LR, ADAM_B1, ADAM_B2, ADAM_EPS, ADAM_WD, ADAM_STEP = 0.001, 0.9, 0.999, 1e-08, 0.01, 10

N_DEV = 8
VMEM_LIMIT_BYTES = 48 * 1024 * 1024
ATT_TILE = 512
ATT_TILE_BWD = 512
PAIRS_PER_STEP_FWD = 4
PAIRS_PER_STEP_BWD = 2

NN = (((1,), (0,)), ((), ()))
NT = (((1,), (1,)), ((), ()))
TN = (((0,), (0,)), ((), ()))

BIG = ['ev_w_in', 'ev_w_uq', 'ev_w_ukv', 'ev_w_out', 'od_w_in', 'od_w_out', 'w_up', 'w_down',
       'ple_w_proj', 'ple_w_gate']
BIG_AXIS = {'ev_w_in': 2, 'ev_w_uq': 2, 'ev_w_ukv': 2, 'ev_w_out': 1, 'od_w_in': 2, 'od_w_out': 1,
            'w_up': 2, 'w_down': 1, 'ple_w_proj': 2, 'ple_w_gate': 1}
SMALL = ['rel_bias', 'ev_q_norm', 'ev_kv_norm', 'ev_sinks', 'od_b_f', 'ln1_g', 'ln1_b', 'ln2_g', 'ln2_b',
         'ple_b_gate']
WEIGHTS = ['rel_bias', 'ev_w_in', 'ev_q_norm', 'ev_w_uq', 'ev_kv_norm', 'ev_w_ukv', 'ev_sinks', 'ev_w_out',
           'od_w_in', 'od_b_f', 'od_w_out', 'ln1_g', 'ln1_b', 'w_up', 'w_down', 'ln2_g', 'ln2_b',
           'ple_w_proj', 'ple_w_gate', 'ple_b_gate']


def _cparams(*sem):
    return pltpu.CompilerParams(dimension_semantics=sem, vmem_limit_bytes=VMEM_LIMIT_BYTES)


def _pick(n, cands):
    for c in cands:
        if n % c == 0:
            return c
    return n


MM_STEP_BYTES = 10 * 1024 * 1024
MM_OUT_BYTES = 8 * 1024 * 1024
MM_CHUNK = 512


def _mm(a, b, *, trans_b=False, extras=(), epilogue=None, out_dtypes=(F32,), name):
    M, K = a.shape
    N = b.shape[0] if trans_b else b.shape[1]
    n_ex, n_out = len(extras), len(out_dtypes)
    row_bytes = K * a.dtype.itemsize + N * (sum(jnp.dtype(d).itemsize for d in out_dtypes)
                                            + sum(e.dtype.itemsize for e in extras if e.shape[0] == M))
    tm = next((c for c in (1024, 512, 256) if M % c == 0 and c * row_bytes <= MM_STEP_BYTES), 128)
    nc = _pick(N, (MM_CHUNK, 384, 256, 128))

    def body(*refs):
        a_ref, b_ref = refs[:2]
        ex = refs[2:2 + n_ex]
        outs = refs[2 + n_ex:]
        av = a_ref[...].astype(BF16)
        for n0 in range(0, N, nc):
            cols = slice(n0, n0 + nc)
            bv = (b_ref[cols, :] if trans_b else b_ref[:, cols]).astype(BF16)
            acc = lax.dot_general(av, bv, NT if trans_b else NN, preferred_element_type=F32)
            res = epilogue(acc, *[e[:, cols] for e in ex]) if epilogue is not None else (acc,)
            for o, r in zip(outs, res):
                o[:, cols] = r.astype(o.dtype)

    in_specs = [pl.BlockSpec((tm, K), lambda i: (i, 0)), pl.BlockSpec(b.shape, lambda i: (0, 0))]
    for e in extras:
        if e.shape == (M, N):
            in_specs.append(pl.BlockSpec((tm, N), lambda i: (i, 0)))
        elif e.shape == (1, N):
            in_specs.append(pl.BlockSpec((1, N), lambda i: (0, 0)))
        else:
            raise ValueError(f"extra operand of shape {e.shape} for a ({M}, {N}) result")
    res = pl.pallas_call(
        body, name=name, grid=(M // tm,), in_specs=in_specs,
        out_specs=[pl.BlockSpec((tm, N), lambda i: (i, 0)) for _ in out_dtypes],
        out_shape=[jax.ShapeDtypeStruct((M, N), d) for d in out_dtypes],
        compiler_params=_cparams("parallel"),
    )(a, b, *extras)
    return res[0] if n_out == 1 else tuple(res)


def _mm_tn(a, b, *, name):
    T, K = a.shape
    N = b.shape[1]
    bk, bn = K, N
    while bk * bn * 4 > MM_OUT_BYTES:
        if bn >= bk and bn % (2 * LANES) == 0:
            bn //= 2
        else:
            bk //= 2
    tt = _pick(T, (1024, 512, 256))
    ck, cn = _pick(bk, (MM_CHUNK, 384, 256, 128)), _pick(bn, (MM_CHUNK, 384, 256, 128))

    def body(a_ref, b_ref, o_ref):
        t = pl.program_id(2)

        @pl.when(t == 0)
        def _():
            o_ref[...] = jnp.zeros_like(o_ref)

        for r0 in range(0, bk, ck):
            av = a_ref[:, r0:r0 + ck].astype(BF16)
            for c0 in range(0, bn, cn):
                o_ref[r0:r0 + ck, c0:c0 + cn] += lax.dot_general(
                    av, b_ref[:, c0:c0 + cn].astype(BF16), TN, preferred_element_type=F32)

    return pl.pallas_call(
        body, name=name, grid=(K // bk, N // bn, T // tt),
        in_specs=[pl.BlockSpec((tt, bk), lambda i, j, t: (t, i)), pl.BlockSpec((tt, bn), lambda i, j, t: (t, j))],
        out_specs=pl.BlockSpec((bk, bn), lambda i, j, t: (i, j)),
        out_shape=jax.ShapeDtypeStruct((K, N), F32),
        compiler_params=_cparams("parallel", "parallel", "arbitrary"),
    )(a, b)


ROW_TILE = 256


def _row_spec(cols, col_block=0):
    return pl.BlockSpec((ROW_TILE, cols), lambda i: (i, col_block))


def _tab_spec(cols, period):
    return pl.BlockSpec((ROW_TILE, cols), lambda i: (i % period, 0))


def _full_spec(shape):
    return pl.BlockSpec(shape, lambda i: (0,) * len(shape))


def _ln_fwd(x, m, g, b, *, name):
    T, D = x.shape

    def body(x_ref, m_ref, g_ref, b_ref, y_ref, yb_ref, xh_ref, r_ref):
        z = DN_ALPHA * x_ref[...] + m_ref[...]
        mu = jnp.mean(z, -1, keepdims=True)
        zc = z - mu
        r = lax.rsqrt(jnp.mean(zc * zc, -1, keepdims=True) + NORM_EPS)
        xh = zc * r
        y = xh * g_ref[...] + b_ref[...]
        y_ref[...] = y
        yb_ref[...] = y.astype(BF16)
        xh_ref[...] = xh
        r_ref[...] = r

    return pl.pallas_call(
        body, name=name, grid=(T // ROW_TILE,),
        in_specs=[_row_spec(D), _row_spec(D), _full_spec((1, D)), _full_spec((1, D))],
        out_specs=[_row_spec(D), _row_spec(D), _row_spec(D), _row_spec(1)],
        out_shape=[jax.ShapeDtypeStruct((T, D), F32), jax.ShapeDtypeStruct((T, D), BF16),
                   jax.ShapeDtypeStruct((T, D), F32), jax.ShapeDtypeStruct((T, 1), F32)],
        compiler_params=_cparams("parallel"),
    )(x, m, g, b)


def _ln_bwd(dy, xh, r, g, *, name):
    T, D = dy.shape

    def body(dy_ref, xh_ref, r_ref, g_ref, dz_ref, dzb_ref, dg_ref, db_ref):
        dyv, xhv = dy_ref[...], xh_ref[...]
        dyg = dyv * g_ref[...]
        c1 = jnp.mean(dyg, -1, keepdims=True)
        c2 = jnp.mean(dyg * xhv, -1, keepdims=True)
        dz = r_ref[...] * (dyg - c1 - xhv * c2)
        dz_ref[...] = dz
        dzb_ref[...] = dz.astype(BF16)

        @pl.when(pl.program_id(0) == 0)
        def _():
            dg_ref[...] = jnp.zeros_like(dg_ref)
            db_ref[...] = jnp.zeros_like(db_ref)

        dg_ref[...] += jnp.sum(dyv * xhv, 0, keepdims=True)
        db_ref[...] += jnp.sum(dyv, 0, keepdims=True)

    return pl.pallas_call(
        body, name=name, grid=(T // ROW_TILE,),
        in_specs=[_row_spec(D), _row_spec(D), _row_spec(1), _full_spec((1, D))],
        out_specs=[_row_spec(D), _row_spec(D), _full_spec((1, D)), _full_spec((1, D))],
        out_shape=[jax.ShapeDtypeStruct((T, D), F32), jax.ShapeDtypeStruct((T, D), BF16),
                   jax.ShapeDtypeStruct((1, D), F32), jax.ShapeDtypeStruct((1, D), F32)],
        compiler_params=_cparams("arbitrary"),
    )(dy, xh, r, g)


def _loss_grad(y, target, *, name):
    T, D = y.shape

    def body(y_ref, t_ref, dy_ref, sq_ref):
        err = y_ref[...] - t_ref[...]
        dy_ref[...] = err / D

        @pl.when(pl.program_id(0) == 0)
        def _():
            sq_ref[...] = jnp.zeros_like(sq_ref)

        sq_ref[...] += jnp.sum(err * err, 0, keepdims=True)

    return pl.pallas_call(
        body, name=name, grid=(T // ROW_TILE,),
        in_specs=[_row_spec(D), _row_spec(D)],
        out_specs=[_row_spec(D), _full_spec((1, D))],
        out_shape=[jax.ShapeDtypeStruct((T, D), F32), jax.ShapeDtypeStruct((1, D), F32)],
        compiler_params=_cparams("arbitrary"),
    )(y, target)


def _ple_bwd_elem(dx3, g, e, *, name):
    T, D = dx3.shape

    def body(dx_ref, g_ref, e_ref, de_ref, dz_ref, db_ref):
        dx, gv = dx_ref[...], g_ref[...]
        de_ref[...] = (dx * gv).astype(BF16)
        dz = dx * e_ref[...] * gv * (1.0 - gv)
        dz_ref[...] = dz.astype(BF16)

        @pl.when(pl.program_id(0) == 0)
        def _():
            db_ref[...] = jnp.zeros_like(db_ref)

        db_ref[...] += jnp.sum(dz, 0, keepdims=True)

    return pl.pallas_call(
        body, name=name, grid=(T // ROW_TILE,),
        in_specs=[_row_spec(D), _row_spec(D), _row_spec(D)],
        out_specs=[_row_spec(D), _row_spec(D), _full_spec((1, D))],
        out_shape=[jax.ShapeDtypeStruct((T, D), BF16), jax.ShapeDtypeStruct((T, D), BF16),
                   jax.ShapeDtypeStruct((1, D), F32)],
        compiler_params=_cparams("arbitrary"),
    )(dx3, g, e)


def _rotate(xv, a, bm, bp, sign):
    half = MLA_ROPE // 2
    width = xv.shape[-1]
    return xv * a + sign * (pltpu.roll(xv, width - half, 1) * bm + pltpu.roll(xv, half, 1) * bp)


def _rope(x, tabs, seq, *, sign, name):
    T, width = x.shape

    def body(x_ref, a_ref, bm_ref, bp_ref, o_ref):
        o_ref[...] = _rotate(x_ref[...], a_ref[...], bm_ref[...], bp_ref[...], sign).astype(BF16)

    return pl.pallas_call(
        body, name=name, grid=(T // ROW_TILE,),
        in_specs=[_row_spec(width)] + [_tab_spec(width, seq // ROW_TILE)] * 3,
        out_specs=_row_spec(width),
        out_shape=jax.ShapeDtypeStruct((T, width), BF16),
        compiler_params=_cparams("parallel"),
    )(x, *tabs)


def _mla_keys(knp, h, k_tabs, seq, *, name):
    T = knp.shape[0]

    def body(k_ref, h_ref, a_ref, bm_ref, bp_ref, o_ref):
        kr = _rotate(h_ref[...], a_ref[...], bm_ref[...], bp_ref[...], 1.0)
        for hd in range(MLA_HEADS):
            cols = slice(hd * LANES, (hd + 1) * LANES)
            o_ref[:, cols] = (k_ref[:, cols].astype(F32) + kr).astype(BF16)

    return pl.pallas_call(
        body, name=name, grid=(T // ROW_TILE,),
        in_specs=[_row_spec(MLA_HEADS * LANES), _row_spec(LANES, EV_KR[0] // LANES)]
        + [_tab_spec(LANES, seq // ROW_TILE)] * 3,
        out_specs=_row_spec(MLA_HEADS * LANES),
        out_shape=jax.ShapeDtypeStruct((T, MLA_HEADS * LANES), BF16),
        compiler_params=_cparams("parallel"),
    )(knp, h, *k_tabs)


def _mla_rope_key_grad(dk, k_tabs, seq, *, name):
    T = dk.shape[0]

    def body(dk_ref, a_ref, bm_ref, bp_ref, o_ref):
        tot = dk_ref[:, 0:LANES]
        for hd in range(1, MLA_HEADS):
            tot = tot + dk_ref[:, hd * LANES:(hd + 1) * LANES]
        o_ref[...] = _rotate(tot, a_ref[...], bm_ref[...], bp_ref[...], -1.0).astype(BF16)

    return pl.pallas_call(
        body, name=name, grid=(T // ROW_TILE,),
        in_specs=[_row_spec(MLA_HEADS * LANES)] + [_tab_spec(LANES, seq // ROW_TILE)] * 3,
        out_specs=_row_spec(LANES),
        out_shape=jax.ShapeDtypeStruct((T, LANES), BF16),
        compiler_params=_cparams("parallel"),
    )(dk, *k_tabs)


def _even_norms(h, gq, gkv, *, name):
    T = h.shape[0]

    def body(h_ref, gq_ref, gkv_ref, cq_ref, ckv_ref, rq_ref, rkv_ref):
        cq = h_ref[:, EV_CQ[0]:EV_CQ[1]]
        rq = lax.rsqrt(jnp.mean(cq * cq, -1, keepdims=True) + NORM_EPS)
        cq_ref[...] = (cq * rq * gq_ref[...]).astype(BF16)
        rq_ref[...] = rq
        ckv = h_ref[:, EV_CKV[0]:EV_CKV[1]]
        rkv = lax.rsqrt(jnp.mean(ckv * ckv, -1, keepdims=True) + NORM_EPS)
        ckv_ref[...] = (ckv * rkv * gkv_ref[...]).astype(BF16)
        rkv_ref[...] = rkv

    return pl.pallas_call(
        body, name=name, grid=(T // ROW_TILE,),
        in_specs=[_row_spec(EVEN_IN_PAD), _full_spec((1, MLA_Q_LORA)), _full_spec((1, MLA_KV_LORA))],
        out_specs=[_row_spec(MLA_Q_LORA), _row_spec(MLA_KV_LORA), _row_spec(1), _row_spec(1)],
        out_shape=[jax.ShapeDtypeStruct((T, MLA_Q_LORA), BF16), jax.ShapeDtypeStruct((T, MLA_KV_LORA), BF16),
                   jax.ShapeDtypeStruct((T, 1), F32), jax.ShapeDtypeStruct((T, 1), F32)],
        compiler_params=_cparams("parallel"),
    )(h, gq, gkv)


def _even_in_bwd(h, rq, rkv, gq, gkv, dcqn, dckvn, dqs, dks, dvs, dkr, *, name):
    T = h.shape[0]

    def rms_bwd(c, r, g, dy):
        xr = c * r
        dyg = dy * g
        return r * (dyg - xr * jnp.mean(dyg * xr, -1, keepdims=True)), jnp.sum(dy * xr, 0, keepdims=True)

    def body(h_ref, rq_ref, rkv_ref, gq_ref, gkv_ref, dcq_ref, dckv_ref, dqs_ref, dks_ref, dvs_ref, dkr_ref,
             dh_ref, dgq_ref, dgkv_ref):
        @pl.when(pl.program_id(0) == 0)
        def _():
            dgq_ref[...] = jnp.zeros_like(dgq_ref)
            dgkv_ref[...] = jnp.zeros_like(dgkv_ref)

        dcq, dgq = rms_bwd(h_ref[:, EV_CQ[0]:EV_CQ[1]], rq_ref[...], gq_ref[...], dcq_ref[...])
        dckv, dgkv = rms_bwd(h_ref[:, EV_CKV[0]:EV_CKV[1]], rkv_ref[...], gkv_ref[...], dckv_ref[...])
        dgq_ref[...] += dgq
        dgkv_ref[...] += dgkv
        dh_ref[:, EV_QS[0]:EV_QS[1]] = dqs_ref[...]
        dh_ref[:, EV_CQ[0]:EV_CQ[1]] = dcq.astype(BF16)
        dh_ref[:, EV_CKV[0]:EV_CKV[1]] = dckv.astype(BF16)
        dh_ref[:, EV_KS[0]:EV_KS[1]] = dks_ref[...]
        dh_ref[:, EV_VS[0]:EV_VS[1]] = dvs_ref[...]
        dh_ref[:, EV_KR[0]:EV_KR[1]] = dkr_ref[...]

    return pl.pallas_call(
        body, name=name, grid=(T // ROW_TILE,),
        in_specs=[_row_spec(EVEN_IN_PAD), _row_spec(1), _row_spec(1), _full_spec((1, MLA_Q_LORA)),
                  _full_spec((1, MLA_KV_LORA)), _row_spec(MLA_Q_LORA), _row_spec(MLA_KV_LORA),
                  _row_spec(SWA_HEADS * HEAD_DIM), _row_spec(LANES), _row_spec(LANES), _row_spec(LANES)],
        out_specs=[_row_spec(EVEN_IN_PAD), _full_spec((1, MLA_Q_LORA)), _full_spec((1, MLA_KV_LORA))],
        out_shape=[jax.ShapeDtypeStruct((T, EVEN_IN_PAD), BF16), jax.ShapeDtypeStruct((1, MLA_Q_LORA), F32),
                   jax.ShapeDtypeStruct((1, MLA_KV_LORA), F32)],
        compiler_params=_cparams("arbitrary"),
    )(h, rq, rkv, gq, gkv, dcqn, dckvn, dqs, dks, dvs, dkr)


def _fox_decay_fwd(f3, bf, *, name):
    B, S, _ = f3.shape

    def body(f_ref, b_ref, csh_ref, chs_ref):
        x = f_ref[...] + b_ref[...]
        c = jnp.minimum(x, 0.0) - jnp.log1p(jnp.exp(-jnp.abs(x)))
        row = lax.broadcasted_iota(jnp.int32, (S, LANES), 0)
        k = 1
        while k < S:
            c = c + jnp.where(row >= k, pltpu.roll(c, k, 0), 0.0)
            k *= 2
        csh_ref[...] = c
        chs_ref[...] = c.T

    return pl.pallas_call(
        body, name=name, grid=(B,),
        in_specs=[pl.BlockSpec((None, S, LANES), lambda b: (b, 0, 0)), pl.BlockSpec((1, LANES), lambda b: (0, 0))],
        out_specs=[pl.BlockSpec((None, S, LANES), lambda b: (b, 0, 0)),
                   pl.BlockSpec((None, LANES, S), lambda b: (b, 0, 0))],
        out_shape=[jax.ShapeDtypeStruct((B, S, LANES), F32), jax.ShapeDtypeStruct((B, LANES, S), F32)],
        compiler_params=_cparams("parallel"),
    )(f3, bf)


def _fox_decay_bwd(dc_hs, f3, bf, *, name):
    B, S, _ = f3.shape

    def body(dc_ref, f_ref, b_ref, df_ref, db_ref):
        g = dc_ref[...].T
        row = lax.broadcasted_iota(jnp.int32, (S, LANES), 0)
        k = 1
        while k < S:
            g = g + jnp.where(row < S - k, pltpu.roll(g, S - k, 0), 0.0)
            k *= 2
        x = f_ref[...] + b_ref[...]
        df = g * (1.0 / (1.0 + jnp.exp(x)))
        df_ref[...] = df.astype(BF16)

        @pl.when(pl.program_id(0) == 0)
        def _():
            db_ref[...] = jnp.zeros_like(db_ref)

        db_ref[...] += jnp.sum(df, 0, keepdims=True)

    return pl.pallas_call(
        body, name=name, grid=(B,),
        in_specs=[pl.BlockSpec((None, LANES, S), lambda b: (b, 0, 0)),
                  pl.BlockSpec((None, S, LANES), lambda b: (b, 0, 0)), pl.BlockSpec((1, LANES), lambda b: (0, 0))],
        out_specs=[pl.BlockSpec((None, S, LANES), lambda b: (b, 0, 0)), pl.BlockSpec((1, LANES), lambda b: (0, 0))],
        out_shape=[jax.ShapeDtypeStruct((B, S, LANES), BF16), jax.ShapeDtypeStruct((1, LANES), F32)],
        compiler_params=_cparams("arbitrary"),
    )(dc_hs, f3, bf)


def _head_column(block, h):
    lane = lax.broadcasted_iota(jnp.int32, block.shape, 1)
    return jnp.sum(jnp.where(lane == h, block, 0.0), axis=-1, keepdims=True)


def _causal_mask(s):
    r = lax.broadcasted_iota(jnp.int32, s.shape, 0)
    c = lax.broadcasted_iota(jnp.int32, s.shape, 1)
    return jnp.where(c <= r, s, NEG_INF)


def _low_half(shape):
    return (lax.broadcasted_iota(jnp.int32, shape, 1) % LANES) < HEAD_DIM


def _widen(x, cols):
    return jnp.concatenate([x] * (cols // LANES), axis=1)


def _both_halves(x, lo):
    r = pltpu.roll(x, HEAD_DIM, 1)
    return jnp.where(lo, x, r), jnp.where(lo, r, x)


MESH_ID = pl.DeviceIdType.MESH
HBM_SPEC = pl.BlockSpec(memory_space=pltpu.HBM)
VMEM_SPEC = pl.BlockSpec(memory_space=pltpu.VMEM)


def _mesh_place():
    x, y, c = lax.axis_index("x"), lax.axis_index("y"), lax.axis_index("c")
    return x, y, c, 4 * x + 2 * y + c


def _peers(x, y, c):
    out = []
    for mask in range(1, N_DEV):
        dx, dy, dc = (mask >> 2) & 1, (mask >> 1) & 1, mask & 1
        px, py, pc = (1 - x if dx else x), (1 - y if dy else y), (1 - c if dc else c)
        out.append(((px, py, pc), 4 * px + 2 * py + pc))
    return out


def _comm_out_shapes(comm):
    return [jax.ShapeDtypeStruct(((N_DEV,) + a.shape) if kind == "gather" else a.shape, a.dtype) for kind, a in comm]


def _comm_scratch(comm):
    n = len(comm)
    return [pltpu.SemaphoreType.DMA((n, 7)), pltpu.SemaphoreType.DMA((n, 7)), pltpu.SemaphoreType.DMA((n,))]


def _comm_copies(kinds, in_refs, out_refs, sems, place):
    send_sems, recv_sems, local_sems = sems
    x, y, c, me = place
    local, remote = [], []
    for w, kind in enumerate(kinds):
        mine = in_refs[w] if kind == "gather" else in_refs[w].at[me]
        local.append(pltpu.make_async_copy(mine, out_refs[w].at[me], local_sems.at[w]))
        for k, (peer, peer_idx) in enumerate(_peers(x, y, c)):
            remote.append(pltpu.make_async_remote_copy(
                src_ref=in_refs[w] if kind == "gather" else in_refs[w].at[peer_idx], dst_ref=out_refs[w].at[me],
                send_sem=send_sems.at[w, k], recv_sem=recv_sems.at[w, k], device_id=peer, device_id_type=MESH_ID))
    return local, remote


def _comm_start(kinds, in_refs, out_refs, sems, place):
    local, remote = _comm_copies(kinds, in_refs, out_refs, sems, place)
    for cp in local + remote:
        cp.start()


def _comm_wait(kinds, in_refs, out_refs, sems, place):
    local, remote = _comm_copies(kinds, in_refs, out_refs, sems, place)
    for cp in remote:
        cp.wait_recv()
    for cp in remote:
        cp.wait_send()
    for cp in local:
        cp.wait()


def _exchange(comm, *, name):
    n = len(comm)
    kinds = [k for k, _ in comm]

    def body(*refs):
        place = _mesh_place()
        _comm_start(kinds, refs[:n], refs[n:2 * n], refs[2 * n:], place)
        _comm_wait(kinds, refs[:n], refs[n:2 * n], refs[2 * n:], place)

    return pl.pallas_call(
        body, name=name, out_shape=_comm_out_shapes(comm), in_specs=[HBM_SPEC] * n, out_specs=[HBM_SPEC] * n,
        scratch_shapes=_comm_scratch(comm),
    )(*[a for _, a in comm])


def _flash_fwd(qa, ka, va, *, q_blk0, k_blk0, v_blk0, W, n_pairs, B, S, scale, csh=None, crow=None, comm=(), name):
    t = ATT_TILE
    nq = S // t
    P = PAIRS_PER_STEP_FWD
    decay = csh is not None
    split = W == LANES
    assert n_pairs % P == 0 and q_blk0 % P == 0 and k_blk0 % P == 0 and v_blk0 % P == 0
    n_c, kinds = len(comm), [k for k, _ in comm]
    n_in = 5 if decay else 3
    n_steps = (B, n_pairs // P, nq)

    def body(*refs):
        c_in, c_out = refs[n_in:n_in + n_c], refs[n_in + n_c + 2:n_in + 2 * n_c + 2]
        sems = refs[n_in + 2 * n_c + 4:]
        refs = refs[:n_in] + refs[n_in + n_c:n_in + n_c + 2] + refs[n_in + 2 * n_c + 2:n_in + 2 * n_c + 4]
        if decay:
            q_ref, k_ref, v_ref, csh_ref, crow_ref, o_ref, lse_ref, m_s, acc_s = refs
        else:
            q_ref, k_ref, v_ref, o_ref, lse_ref, m_s, acc_s = refs
        g, i = pl.program_id(1), pl.program_id(2)
        if n_c:
            place = _mesh_place()
            ids = [pl.program_id(ax) for ax in range(3)]

            @pl.when((ids[0] == 0) & (ids[1] == 0) & (ids[2] == 0))
            def _():
                _comm_start(kinds, c_in, c_out, sems, place)

        lo = _low_half((t, LANES))
        qv = q_ref[...]
        qh = []
        for pr in range(P):
            qp = qv[:, pr * W:(pr + 1) * W]
            qh += [jnp.where(lo, qp, jnp.zeros_like(qp)), jnp.where(lo, jnp.zeros_like(qp), qp)] if split \
                else [qp[:, :LANES], qp[:, LANES:]]
        if decay:
            cq = [jnp.broadcast_to(_head_column(csh_ref[...], 2 * P * g + hd), (t, LANES)) for hd in range(2 * P)]
        m_s[...] = jnp.full(m_s.shape, NEG_INF, F32)
        acc_s[...] = jnp.zeros(acc_s.shape, F32)

        def step(j, masked):
            rows = pl.ds(pl.multiple_of(j * t, t), t)
            kb, vb = k_ref[rows, :], v_ref[rows, :]
            for pr in range(P):
                kp, vp = kb[:, pr * W:(pr + 1) * W], vb[:, pr * LANES:(pr + 1) * LANES]
                ones = jnp.ones_like(vp)
                vaug = [jnp.where(lo, vp, ones), jnp.where(lo, ones, vp)]
                for half in range(2):
                    hd = 2 * pr + half
                    kh = kp if split else kp[:, half * LANES:(half + 1) * LANES]
                    s = lax.dot_general(qh[hd], kh, NT, preferred_element_type=F32) * scale
                    if decay:
                        s = s + _widen(cq[hd], t) - crow_ref[hd, j]
                    if masked:
                        s = _causal_mask(s)
                    m_prev = m_s[hd]
                    m_new = jnp.maximum(m_prev, jnp.max(s, -1, keepdims=True))
                    p = jnp.exp(s - _widen(m_new, t))
                    acc_s[hd] = jnp.exp(m_prev - m_new) * acc_s[hd] + lax.dot_general(
                        p.astype(BF16), vaug[half], NN, preferred_element_type=F32)
                    m_s[hd] = m_new

        def loop_body(j, carry):
            step(j, False)
            return carry

        lax.fori_loop(0, i, loop_body, 0)
        step(i, True)
        for pr in range(P):
            acc0, acc1 = acc_s[2 * pr], acc_s[2 * pr + 1]
            _, l0 = _both_halves(acc0, lo)
            l1, _ = _both_halves(acc1, lo)
            cols = slice(pr * LANES, (pr + 1) * LANES)
            o_ref[:, cols] = jnp.where(lo, acc0 / l0, acc1 / l1).astype(BF16)
            lse_ref[:, cols] = jnp.where(lo, m_s[2 * pr] + jnp.log(l0), m_s[2 * pr + 1] + jnp.log(l1))
        if n_c:
            @pl.when((ids[0] == n_steps[0] - 1) & (ids[1] == n_steps[1] - 1) & (ids[2] == n_steps[2] - 1))
            def _():
                _comm_wait(kinds, c_in, c_out, sems, place)

    in_specs = [pl.BlockSpec((t, P * W), lambda b, g, i: (b * nq + i, q_blk0 // P + g)),
                pl.BlockSpec((S, P * W), lambda b, g, i: (b, k_blk0 // P + g)),
                pl.BlockSpec((S, P * LANES), lambda b, g, i: (b, v_blk0 // P + g))]
    args = [qa, ka, va]
    if decay:
        in_specs += [pl.BlockSpec((None, t, LANES), lambda b, g, i: (b, i, 0)),
                     pl.BlockSpec((None, 2 * P, nq, 1, t), lambda b, g, i: (b, g, 0, 0, 0))]
        args += [csh, crow]
    out_spec = pl.BlockSpec((t, P * LANES), lambda b, g, i: (b * nq + i, g))
    res = pl.pallas_call(
        body, name=name, grid=n_steps, in_specs=in_specs + [HBM_SPEC] * n_c,
        out_specs=[out_spec, out_spec] + [HBM_SPEC] * n_c,
        out_shape=[jax.ShapeDtypeStruct((B * S, n_pairs * LANES), BF16),
                   jax.ShapeDtypeStruct((B * S, n_pairs * LANES), F32)] + _comm_out_shapes(comm),
        scratch_shapes=[pltpu.VMEM((2 * P, t, LANES), F32), pltpu.VMEM((2 * P, t, LANES), F32)]
        + (_comm_scratch(comm) if n_c else []),
        compiler_params=_cparams(*(("arbitrary",) * 3 if n_c else ("parallel",) * 3)),
    )(*args, *[a for _, a in comm])
    return res[0], res[1], list(res[2:])


def _flash_bwd(qa, ka, va, oa, doa, lsea, *, q_blk0, k_blk0, v_blk0, do_blk0, W, n_pairs, B, S, scale, qk_dtype,
               csh=None, crow=None, comm=(), name):
    t = ATT_TILE_BWD
    nq = S // t
    P = PAIRS_PER_STEP_BWD
    decay = csh is not None
    if decay:
        crow = crow.reshape(B, 2 * n_pairs, nq, 1, t)
    split = W == LANES
    assert n_pairs % P == 0 and q_blk0 % P == 0 and k_blk0 % P == 0 and v_blk0 % P == 0 and do_blk0 % P == 0
    n_c, kinds = len(comm), [k for k, _ in comm]
    n_in, n_out, n_scr = (8, 5, 8) if decay else (6, 3, 5)
    n_steps = (B, n_pairs // P, nq)

    def body(*refs):
        c_in = refs[n_in:n_in + n_c]
        c_out = refs[n_in + n_c + n_out:n_in + 2 * n_c + n_out]
        sems = refs[n_in + 2 * n_c + n_out + n_scr:]
        refs = (refs[:n_in] + refs[n_in + n_c:n_in + n_c + n_out]
                + refs[n_in + 2 * n_c + n_out:n_in + 2 * n_c + n_out + n_scr])
        if n_c:
            place = _mesh_place()
            ids = [pl.program_id(ax) for ax in range(3)]

            @pl.when((ids[0] == 0) & (ids[1] == 0) & (ids[2] == 0))
            def _():
                _comm_start(kinds, c_in, c_out, sems, place)

        if decay:
            (q_ref, k_ref, v_ref, o_ref, do_ref, lse_ref, csh_ref, crow_ref, dq_ref, dk_ref, dv_ref, dck_ref, dcq_ref,
             dq_s, lse_s, delta_s, dk_s, dv_s, cq_s, dcq_s, dck_s) = refs
        else:
            (q_ref, k_ref, v_ref, o_ref, do_ref, lse_ref, dq_ref, dk_ref, dv_ref,
             dq_s, lse_s, delta_s, dk_s, dv_s) = refs
        g, j = pl.program_id(1), pl.program_id(2)
        lo = _low_half((t, LANES))

        @pl.when(j == 0)
        def _():
            lo_s = _low_half((S, LANES))
            dq_s[...] = jnp.zeros(dq_s.shape, F32)
            for pr in range(P):
                cols = slice(pr * LANES, (pr + 1) * LANES)
                lse_s[2 * pr], lse_s[2 * pr + 1] = _both_halves(lse_ref[:, cols], lo_s)
                dd = do_ref[:, cols].astype(F32) * o_ref[:, cols].astype(F32)
                delta_s[2 * pr] = jnp.broadcast_to(jnp.sum(jnp.where(lo_s, dd, 0.0), -1, keepdims=True), (S, LANES))
                delta_s[2 * pr + 1] = jnp.broadcast_to(jnp.sum(jnp.where(lo_s, 0.0, dd), -1, keepdims=True),
                                                       (S, LANES))
            if decay:
                for hd in range(2 * P):
                    cq_s[hd] = jnp.broadcast_to(_head_column(csh_ref[...], 2 * P * g + hd), (S, LANES))
                dcq_s[...] = jnp.zeros(dcq_s.shape, F32)

        kb, vb = k_ref[...], v_ref[...]
        kh, vh = [], []
        for pr in range(P):
            kp, vp = kb[:, pr * W:(pr + 1) * W], vb[:, pr * LANES:(pr + 1) * LANES]
            zk, zv = jnp.zeros_like(kp), jnp.zeros_like(vp)
            kh += [jnp.where(lo, kp, zk), jnp.where(lo, zk, kp)] if split else [kp[:, :LANES], kp[:, LANES:]]
            vh += [jnp.where(lo, vp, zv), jnp.where(lo, zv, vp)]
        dk_s[...] = jnp.zeros(dk_s.shape, F32)
        dv_s[...] = jnp.zeros(dv_s.shape, F32)
        if decay:
            dck_s[...] = jnp.zeros(dck_s.shape, F32)

        def step(i, masked):
            rows = pl.ds(pl.multiple_of(i * t, t), t)
            qi, doi = q_ref[rows, :], do_ref[rows, :]
            for pr in range(P):
                qp, dop = qi[:, pr * W:(pr + 1) * W], doi[:, pr * LANES:(pr + 1) * LANES]
                for half in range(2):
                    hd = 2 * pr + half
                    qx = qp if split else qp[:, half * LANES:(half + 1) * LANES]
                    s = lax.dot_general(qx, kh[hd], NT, preferred_element_type=F32) * scale
                    if decay:
                        s = s + _widen(cq_s[hd, rows, :], t) - crow_ref[hd, j]
                    if masked:
                        s = _causal_mask(s)
                    p = jnp.exp(s - _widen(lse_s[hd, rows, :], t))
                    dv_s[hd] += lax.dot_general(p.astype(BF16), dop, TN, preferred_element_type=F32)
                    dp = lax.dot_general(dop, vh[hd], NT, preferred_element_type=F32)
                    ds = p * (dp - _widen(delta_s[hd, rows, :], t))
                    dss = (ds * scale).astype(BF16)
                    dk_s[hd] += lax.dot_general(dss, qx, TN, preferred_element_type=F32)
                    dqc = lax.dot_general(dss, kh[hd], NN, preferred_element_type=F32)
                    if split:
                        dq_s[rows, pr * W:(pr + 1) * W] += dqc
                    else:
                        dq_s[rows, hd * LANES:(hd + 1) * LANES] += dqc
                    if decay:
                        dck_s[hd] -= jnp.sum(ds, 0, keepdims=True)
                        part = ds[:, :LANES]
                        for c in range(1, t // LANES):
                            part = part + ds[:, c * LANES:(c + 1) * LANES]
                        dcq_s[hd, rows, :] += part

        def loop_body(i, carry):
            step(i, False)
            return carry

        step(j, True)
        lax.fori_loop(j + 1, nq, loop_body, 0)
        for pr in range(P):
            if split:
                dk_ref[:, pr * W:(pr + 1) * W] = jnp.where(lo, dk_s[2 * pr], dk_s[2 * pr + 1]).astype(dk_ref.dtype)
            else:
                for half in range(2):
                    hd = 2 * pr + half
                    dk_ref[:, hd * LANES:(hd + 1) * LANES] = dk_s[hd].astype(dk_ref.dtype)
            dv_ref[:, pr * LANES:(pr + 1) * LANES] = jnp.where(lo, dv_s[2 * pr], dv_s[2 * pr + 1]).astype(BF16)
        if decay:
            dck_ref[...] = dck_s[...]

        @pl.when(j == nq - 1)
        def _():
            dq_ref[...] = dq_s[...].astype(dq_ref.dtype)
            if decay:
                for hd in range(2 * P):
                    dcq_ref[hd] = jnp.sum(dcq_s[hd].T, 0, keepdims=True)

        if n_c:
            @pl.when((ids[0] == n_steps[0] - 1) & (ids[1] == n_steps[1] - 1) & (ids[2] == n_steps[2] - 1))
            def _():
                _comm_wait(kinds, c_in, c_out, sems, place)

    full = lambda w, blk0: pl.BlockSpec((S, P * w), lambda b, g, j: (b, blk0 // P + g))
    blk = lambda w, blk0: pl.BlockSpec((t, P * w), lambda b, g, j: (b * nq + j, blk0 // P + g))
    in_specs = [full(W, q_blk0), blk(W, k_blk0), blk(LANES, v_blk0), full(LANES, 0), full(LANES, do_blk0),
                full(LANES, 0)]
    args = [qa, ka, va, oa, doa, lsea]
    T = B * S
    out_specs = [full(W, 0), blk(W, 0), blk(LANES, 0)]
    out_shape = [jax.ShapeDtypeStruct((T, n_pairs * W), qk_dtype), jax.ShapeDtypeStruct((T, n_pairs * W), qk_dtype),
                 jax.ShapeDtypeStruct((T, n_pairs * LANES), BF16)]
    per_head = lambda rows: pltpu.VMEM((2 * P, rows, LANES), F32)
    scratch = [pltpu.VMEM((S, P * W), F32), per_head(S), per_head(S), per_head(t), per_head(t)]
    if decay:
        in_specs += [pl.BlockSpec((None, S, LANES), lambda b, g, j: (b, 0, 0)),
                     pl.BlockSpec((None, 2 * P, nq, 1, t), lambda b, g, j: (b, g, 0, 0, 0))]
        args += [csh, crow]
        out_specs += [pl.BlockSpec((None, 2 * P, None, 1, t), lambda b, g, j: (b, g, j, 0, 0)),
                      pl.BlockSpec((None, 2 * P, 1, S), lambda b, g, j: (b, g, 0, 0))]
        out_shape += [jax.ShapeDtypeStruct((B, 2 * n_pairs, nq, 1, t), F32),
                      jax.ShapeDtypeStruct((B, 2 * n_pairs, 1, S), F32)]
        scratch += [per_head(S), per_head(S), pltpu.VMEM((2 * P, 1, t), F32)]
    res = pl.pallas_call(
        body, name=name, grid=n_steps, in_specs=in_specs + [HBM_SPEC] * n_c,
        out_specs=out_specs + [HBM_SPEC] * n_c, out_shape=out_shape + _comm_out_shapes(comm),
        scratch_shapes=scratch + (_comm_scratch(comm) if n_c else []),
        compiler_params=_cparams(*(("arbitrary",) * 3 if n_c else ("parallel", "parallel", "arbitrary"))),
    )(*args, *[a for _, a in comm])
    return tuple(res[:n_out]) + (list(res[n_out:]),)


def _swa_common(q_ref, kp_ref, ko_ref, vp_ref, vo_ref, n):
    Q = BLOCK_Q
    lo = _low_half((Q, LANES))
    lo2 = _low_half((2 * Q, LANES))
    kk = jnp.concatenate([kp_ref[...], ko_ref[...]], axis=0)
    vv = jnp.concatenate([vp_ref[...], vo_ref[...]], axis=0)
    kdup = [x.astype(BF16) for x in _both_halves(kk, lo2)]
    vdup = [x.astype(BF16) for x in _both_halves(vv, lo2)]
    a = lax.broadcasted_iota(jnp.int32, (Q, 2 * Q), 0)
    col = lax.broadcasted_iota(jnp.int32, (Q, 2 * Q), 1)
    dist = a + Q - col
    valid = (dist >= 0) & (dist < SWA_WINDOW) & ((col >= Q) | (n > 0))
    qv = q_ref[...]
    qm = []
    for a_head in range(SWA_HEADS):
        qp = qv[:, (a_head // 2) * LANES:(a_head // 2 + 1) * LANES]
        keep = lo if a_head % 2 == 0 else jnp.logical_not(lo)
        qm.append(jnp.where(keep, qp, 0.0).astype(BF16))
    return lo, lo2, kdup, vdup, valid, qm


def _swa_in_specs(nb):
    Q = BLOCK_Q
    own = lambda blk: (lambda b, n: (b * nb + n, blk))
    prev = lambda blk: (lambda b, n: (b * nb + jnp.maximum(n - 1, 0), blk))
    kb, vb = EV_KS[0] // LANES, EV_VS[0] // LANES
    return [pl.BlockSpec((Q, SWA_HEADS * HEAD_DIM), own(0)), pl.BlockSpec((Q, LANES), prev(kb)),
            pl.BlockSpec((Q, LANES), own(kb)), pl.BlockSpec((Q, LANES), prev(vb)), pl.BlockSpec((Q, LANES), own(vb))]


def _swa_fwd(h, bias, sinkcol, *, B, S, name):
    Q = BLOCK_Q
    nb = S // Q
    scale = HEAD_DIM ** -0.5

    def body(q_ref, kp_ref, ko_ref, vp_ref, vo_ref, bias_ref, sink_ref, o_ref, lse_ref):
        lo, _, kdup, vdup, valid, qm = _swa_common(q_ref, kp_ref, ko_ref, vp_ref, vo_ref, pl.program_id(1))
        lane = lax.broadcasted_iota(jnp.int32, (Q, LANES), 1)
        lse_blk = jnp.zeros((Q, LANES), F32)
        pairs = []
        for pr in range(SWA_HEADS // 2):
            pv = []
            for half in range(2):
                a = 2 * pr + half
                kvh = a // (SWA_HEADS // SWA_KV_HEADS)
                s = lax.dot_general(qm[a], kdup[kvh], NT, preferred_element_type=F32) * scale + bias_ref[a]
                s = jnp.where(valid, s, NEG_INF)
                sink = sink_ref[a]
                mx = jnp.maximum(jnp.max(s, -1, keepdims=True), sink)
                p = jnp.exp(s - mx)
                l = jnp.sum(p, -1, keepdims=True) + jnp.exp(sink - mx)
                pv.append(lax.dot_general((p / l).astype(BF16), vdup[kvh], NN, preferred_element_type=F32))
                lse_blk = jnp.where(lane == a, mx + jnp.log(l), lse_blk)
            pairs.append(jnp.where(lo, pv[0], pv[1]))
        o_ref[...] = jnp.concatenate(pairs, axis=1).astype(BF16)
        lse_ref[...] = lse_blk

    whole = lambda shape: pl.BlockSpec(shape, lambda b, n: (0,) * len(shape))
    return pl.pallas_call(
        body, name=name, grid=(B, nb),
        in_specs=_swa_in_specs(nb) + [whole((SWA_HEADS, Q, 2 * Q)), whole((SWA_HEADS, Q, 1))],
        out_specs=[pl.BlockSpec((Q, SWA_HEADS * HEAD_DIM), lambda b, n: (b * nb + n, 0)),
                   pl.BlockSpec((Q, LANES), lambda b, n: (b * nb + n, 0))],
        out_shape=[jax.ShapeDtypeStruct((B * S, SWA_HEADS * HEAD_DIM), BF16),
                   jax.ShapeDtypeStruct((B * S, LANES), F32)],
        compiler_params=_cparams("parallel", "parallel"),
    )(h, h, h, h, h, bias, sinkcol)


def _swa_bwd(h, o, do, lse, bias, sinkcol, *, do_blk0, B, S, name):
    Q = BLOCK_Q
    nb = S // Q
    scale = HEAD_DIM ** -0.5
    group = SWA_HEADS // SWA_KV_HEADS

    def body(q_ref, kp_ref, ko_ref, vp_ref, vo_ref, o_ref, do_ref, lse_ref, bias_ref, sink_ref,
             dq_ref, dko_ref, dkp_ref, dvo_ref, dvp_ref, dbias_ref, dsink_ref):
        @pl.when((pl.program_id(0) == 0) & (pl.program_id(1) == 0))
        def _():
            dbias_ref[...] = jnp.zeros_like(dbias_ref)
            dsink_ref[...] = jnp.zeros_like(dsink_ref)

        lo, lo2, kdup, vdup, valid, qm = _swa_common(q_ref, kp_ref, ko_ref, vp_ref, vo_ref, pl.program_id(1))
        lse_blk = lse_ref[...]
        dkk = [jnp.zeros((2 * Q, LANES), F32) for _ in range(SWA_KV_HEADS)]
        dvv = [jnp.zeros((2 * Q, LANES), F32) for _ in range(SWA_KV_HEADS)]
        dq_pairs = []
        for pr in range(SWA_HEADS // 2):
            cols = slice(pr * LANES, (pr + 1) * LANES)
            do_p, o_p = do_ref[:, cols], o_ref[:, cols]
            dq_half = []
            for half in range(2):
                a = 2 * pr + half
                kvh = a // group
                keep = lo if half == 0 else jnp.logical_not(lo)
                s = lax.dot_general(qm[a], kdup[kvh], NT, preferred_element_type=F32) * scale + bias_ref[a]
                s = jnp.where(valid, s, NEG_INF)
                lse_a = _head_column(lse_blk, a)
                p = jnp.exp(s - lse_a)
                doh = jnp.where(keep, do_p, jnp.zeros_like(do_p))
                delta = jnp.sum(doh.astype(F32) * o_p.astype(F32), -1, keepdims=True)
                dp = lax.dot_general(doh, vdup[kvh], NT, preferred_element_type=F32)
                ds = p * (dp - delta)
                dbias_ref[a] += ds
                dsink_ref[a] -= jnp.exp(sink_ref[a] - lse_a) * delta
                dss = (ds * scale).astype(BF16)
                dq_half.append(lax.dot_general(dss, kdup[kvh], NN, preferred_element_type=F32))
                dkk[kvh] = dkk[kvh] + lax.dot_general(dss, qm[a], TN, preferred_element_type=F32)
                dvv[kvh] = dvv[kvh] + lax.dot_general(p.astype(BF16), doh, TN, preferred_element_type=F32)
            dq_pairs.append(jnp.where(lo, dq_half[0], dq_half[1]))
        dq_ref[...] = jnp.concatenate(dq_pairs, axis=1).astype(BF16)
        fold = lambda x: x + pltpu.roll(x, HEAD_DIM, 1)
        dk_blk = jnp.where(lo2, fold(dkk[0]), fold(dkk[1]))
        dv_blk = jnp.where(lo2, fold(dvv[0]), fold(dvv[1]))
        dkp_ref[...] = dk_blk[:Q]
        dko_ref[...] = dk_blk[Q:]
        dvp_ref[...] = dv_blk[:Q]
        dvo_ref[...] = dv_blk[Q:]

    whole = lambda shape: pl.BlockSpec(shape, lambda b, n: (0,) * len(shape))
    wide = lambda blk: pl.BlockSpec((Q, SWA_HEADS * HEAD_DIM), lambda b, n: (b * nb + n, blk))
    narrow = pl.BlockSpec((Q, LANES), lambda b, n: (b * nb + n, 0))
    kv_shape = jax.ShapeDtypeStruct((B * S, LANES), F32)
    return pl.pallas_call(
        body, name=name, grid=(B, nb),
        in_specs=_swa_in_specs(nb) + [wide(0), wide(do_blk0), narrow, whole((SWA_HEADS, Q, 2 * Q)),
                                      whole((SWA_HEADS, Q, 1))],
        out_specs=[wide(0), narrow, narrow, narrow, narrow, whole((SWA_HEADS, Q, 2 * Q)), whole((SWA_HEADS, Q, 1))],
        out_shape=[jax.ShapeDtypeStruct((B * S, SWA_HEADS * HEAD_DIM), BF16), kv_shape, kv_shape, kv_shape, kv_shape,
                   jax.ShapeDtypeStruct((SWA_HEADS, Q, 2 * Q), F32), jax.ShapeDtypeStruct((SWA_HEADS, Q, 1), F32)],
        compiler_params=_cparams("arbitrary", "arbitrary"),
    )(h, h, h, h, h, o, do, lse, bias, sinkcol)


def _bias_bucket_sum(dbias, bucket, *, name):
    def body(d_ref, b_ref, o_ref):
        dbv, bk = d_ref[...], b_ref[...]
        lane = lax.broadcasted_iota(jnp.int32, (SWA_HEADS, LANES), 1)
        out = jnp.zeros((SWA_HEADS, LANES), F32)
        for b in range(REL_BUCKETS):
            part = jnp.sum(jnp.where(bk == b, dbv, 0.0), axis=1)
            tot = jnp.sum(part, axis=-1, keepdims=True)
            out = out + jnp.where(lane == b, tot, 0.0)
        o_ref[...] = out

    return pl.pallas_call(
        body, name=name, out_shape=jax.ShapeDtypeStruct((SWA_HEADS, LANES), F32),
        compiler_params=pltpu.CompilerParams(vmem_limit_bytes=VMEM_LIMIT_BYTES),
    )(dbias, bucket)


def _adamw_update(w, g, m, v):
    m_new = ADAM_B1 * m + (1.0 - ADAM_B1) * g
    v_new = ADAM_B2 * v + (1.0 - ADAM_B2) * jnp.square(g)
    m_hat = m_new / (1.0 - ADAM_B1 ** ADAM_STEP)
    v_hat = v_new / (1.0 - ADAM_B2 ** ADAM_STEP)
    return -ADAM_LR * (m_hat / (jnp.sqrt(v_hat) + ADAM_EPS) + ADAM_WD * w), m_new, v_new


def _adamw(w, g, m, v, *, name):
    def body(w_ref, g_ref, m_ref, v_ref, d_ref, nm_ref, nv_ref):
        d_ref[...], nm_ref[...], nv_ref[...] = _adamw_update(w_ref[...], g_ref[...], m_ref[...], v_ref[...])

    return pl.pallas_call(
        body, name=name, out_shape=[jax.ShapeDtypeStruct(w.shape, F32)] * 3,
        compiler_params=pltpu.CompilerParams(vmem_limit_bytes=VMEM_LIMIT_BYTES),
    )(w, g, m, v)


def _adamw_slots(w, parts, m, v, *, name):
    R, C = w.shape
    tr = R if R <= 512 else _pick(R, (256, 128))

    def body(w_ref, p_ref, m_ref, v_ref, g_ref, d_ref, nm_ref, nv_ref):
        g = p_ref[0].astype(F32)
        for j in range(1, N_DEV):
            g = g + p_ref[j].astype(F32)
        g_ref[...] = g
        d_ref[...], nm_ref[...], nv_ref[...] = _adamw_update(w_ref[...], g, m_ref[...], v_ref[...])

    spec = pl.BlockSpec((tr, C), lambda i: (i, 0))
    return pl.pallas_call(
        body, name=name, grid=(R // tr,),
        in_specs=[spec, pl.BlockSpec((N_DEV, tr, C), lambda i: (0, i, 0)), spec, spec], out_specs=[spec] * 4,
        out_shape=[jax.ShapeDtypeStruct((R, C), F32)] * 4, compiler_params=_cparams("parallel"),
    )(w, parts, m, v)


def _all_gather_hbm(blocks, *, name):
    n = len(blocks)

    def body(*refs):
        x_refs, out_refs = refs[:n], refs[n:2 * n]
        send_sems, recv_sems, local_sems = refs[2 * n:]
        x, y, c, _ = _mesh_place()
        me, sibling = (x, y, c), (x, y, 1 - c)
        chips = [(1 - x, y), (x, 1 - y), (1 - x, 1 - y)]

        def copy(w, k, blk, to, src=None):
            px, py, pc = blk
            slot = out_refs[w].at[4 * px + 2 * py + pc]
            return pltpu.make_async_remote_copy(
                src_ref=slot if src is None else src, dst_ref=slot,
                send_sem=send_sems.at[w, k], recv_sem=recv_sems.at[w, k], device_id=to, device_id_type=MESH_ID)

        mine = [pltpu.make_async_copy(x_refs[w], out_refs[w].at[4 * x + 2 * y + c], local_sems.at[w])
                for w in range(n)]
        for cp in mine:
            cp.start()
        first = []
        for w in range(n):
            first.append(copy(w, 0, me, sibling, src=x_refs[w]))
            first += [copy(w, 1 + j, me, (*chip, c), src=x_refs[w]) for j, chip in enumerate(chips)]
        for cp in first:
            cp.start()
        passed = []
        for j, chip in enumerate(chips):
            for w in range(n):
                copy(w, 1 + j, (*chip, c), me).wait_recv()
                fwd = copy(w, 4 + j, (*chip, c), sibling)
                fwd.start()
                passed.append(fwd)
        for w in range(n):
            copy(w, 0, sibling, me).wait_recv()
            for j, chip in enumerate(chips):
                copy(w, 4 + j, (*chip, 1 - c), me).wait_recv()
        for cp in first + passed:
            cp.wait_send()
        for cp in mine:
            cp.wait()

    return pl.pallas_call(
        body, name=name, out_shape=[jax.ShapeDtypeStruct((N_DEV,) + b.shape, b.dtype) for b in blocks],
        in_specs=[HBM_SPEC] * n, out_specs=[HBM_SPEC] * n,
        scratch_shapes=[pltpu.SemaphoreType.DMA((n, 7)), pltpu.SemaphoreType.DMA((n, 7)),
                        pltpu.SemaphoreType.DMA((n,))],
    )(*blocks)


def _all_reduce_small(block, *, name):
    R, W = block.shape

    def body(x_ref, out_ref, buf, send_sems, recv_sems):
        x, y, c, me = _mesh_place()
        copies = []
        for k, (peer, _) in enumerate(_peers(x, y, c)):
            copies.append(pltpu.make_async_remote_copy(
                src_ref=x_ref, dst_ref=buf.at[me], send_sem=send_sems.at[k], recv_sem=recv_sems.at[k],
                device_id=peer, device_id_type=MESH_ID))
        for cp in copies:
            cp.start()
        buf[me] = x_ref[...]
        for cp in copies:
            cp.wait_recv()
        for cp in copies:
            cp.wait_send()
        acc = buf[0]
        for j in range(1, N_DEV):
            acc = acc + buf[j]
        out_ref[...] = acc

    return pl.pallas_call(
        body, name=name, out_shape=jax.ShapeDtypeStruct((R, W), F32),
        in_specs=[VMEM_SPEC], out_specs=VMEM_SPEC,
        scratch_shapes=[pltpu.VMEM((N_DEV, R, W), F32), pltpu.SemaphoreType.DMA((7,)), pltpu.SemaphoreType.DMA((7,))],
    )(block)


def _assemble(name, g):
    if BIG_AXIS[name] == 2:
        return jnp.concatenate([g[j] for j in range(N_DEV)], axis=1)
    return g.reshape(N_DEV * g.shape[1], g.shape[2])


def _split_for_devices(name, g):
    if BIG_AXIS[name] == 2:
        b = g.shape[1] // N_DEV
        return jnp.stack([g[:, j * b:(j + 1) * b] for j in range(N_DEV)]).astype(BF16)
    return g.reshape(N_DEV, g.shape[0] // N_DEV, g.shape[1]).astype(BF16)


def _layer_weight_keys(i):
    j = i // 2
    mixer = [('ev_w_in', j), ('ev_w_uq', j), ('ev_w_ukv', j), ('ev_w_out', j)] if i % 2 == 0 \
        else [('od_w_in', j), ('od_w_out', j)]
    return mixer + [('w_up', i), ('w_down', i), ('ple_w_proj', i), ('ple_w_gate', i)]


class _MeshExchange:
    def __init__(self, shards):
        self.shards = shards
        self.weights = {}
        self.pending = []
        self.in_flight = []
        self.received = {}

    def layer_weights(self, i):
        keys = _layer_weight_keys(i)
        if i == 0:
            got = _all_gather_hbm([self.shards[k] for k in keys], name="gather_l0")
            self.weights[0] = {k[0]: _assemble(k[0], g) for k, g in zip(keys, got)}
        return self.weights[i]

    def fwd_items(self, i):
        if i + 1 >= DEPTH:
            return []
        return [("gather", self.shards[k]) for k in _layer_weight_keys(i + 1)]

    def fwd_done(self, i, outs):
        if outs:
            keys = _layer_weight_keys(i + 1)
            self.weights[i + 1] = {k[0]: _assemble(k[0], g) for k, g in zip(keys, outs)}

    def push_grads(self, grads):
        self.pending += [(k, _split_for_devices(k[0], g)) for k, g in grads.items()]

    def bwd_items(self):
        self.in_flight, self.pending = self.pending, []
        return [("scatter", parts) for _, parts in self.in_flight]

    def bwd_done(self, outs):
        for (k, _), got in zip(self.in_flight, outs):
            self.received[k] = got
        self.in_flight = []

    def finish(self):
        if self.pending:
            outs = _exchange(self.bwd_items(), name="exchange_rest")
            self.bwd_done(outs)
        return self.received


PACK_ROWS = 8


def _pack_small(vals):
    flat = jnp.concatenate([vals[n].reshape(-1).astype(F32) for n in SMALL])
    pad = (-flat.shape[0]) % (PACK_ROWS * LANES)
    return jnp.pad(flat, (0, pad)).reshape(-1, LANES)


def _unpack_small(block, shapes):
    flat = block.reshape(-1)
    out, off = {}, 0
    for n in SMALL:
        sz = math.prod(shapes[n])
        out[n] = flat[off:off + sz].reshape(shapes[n])
        off += sz
    return out


def _rope_tables(S):
    half = MLA_ROPE // 2
    inv = 1.0 / (ROPE_THETA ** (jnp.arange(0, MLA_ROPE, 2, dtype=F32) / MLA_ROPE))
    ang = jnp.arange(S, dtype=F32)[:, None] * inv[None, :]
    cos, sin = jnp.cos(ang), jnp.sin(ang)
    zeros = jnp.zeros((S, half), F32)
    tail = jnp.zeros((S, LANES - MLA_QK), F32)

    def block(rope_part, nope_val):
        return jnp.concatenate([jnp.full((S, MLA_NOPE), nope_val, F32), rope_part, tail], -1)

    a_r = jnp.concatenate([cos, cos], -1)
    bm_r = jnp.concatenate([-sin, zeros], -1)
    bp_r = jnp.concatenate([zeros, sin], -1)
    q_tabs = tuple(jnp.tile(block(r, v), (1, MLA_HEADS)) for r, v in ((a_r, 1.0), (bm_r, 0.0), (bp_r, 0.0)))
    k_tabs = tuple(block(r, 0.0) for r in (a_r, bm_r, bp_r))
    return q_tabs, k_tabs


def _t5_bucket(dist):
    exact = REL_BUCKETS // 2
    d = jnp.maximum(dist, 1).astype(F32)
    large = exact + (jnp.log(d / exact) / math.log(REL_MAX_DIST / exact) * (REL_BUCKETS - exact)).astype(jnp.int32)
    large = jnp.minimum(large, REL_BUCKETS - 1)
    return jnp.where(dist < exact, dist, large)


def _swa_bucket_table():
    a = jnp.arange(BLOCK_Q)[:, None]
    col = jnp.arange(2 * BLOCK_Q)[None, :]
    return _t5_bucket(jnp.maximum(a + BLOCK_Q - col, 0)).astype(jnp.int32)


def _even_weights(W):
    w = W['ev_w_in']
    c_kv1 = MLA_Q_LORA + MLA_KV_LORA
    c_kr1 = c_kv1 + MLA_ROPE
    c_qs1 = c_kr1 + SWA_HEADS * HEAD_DIM
    zeros = lambda n: jnp.zeros((D_MODEL, n), w.dtype)
    w_in = jnp.concatenate([w[:, c_kr1:c_qs1], w[:, :c_kv1], w[:, c_qs1:], zeros(KR_LANE0), w[:, c_kv1:c_kr1],
                            zeros(LANES - KR_LANE0 - MLA_ROPE)], axis=1)
    uq = W['ev_w_uq'].reshape(MLA_Q_LORA, MLA_HEADS, MLA_QK)
    w_uq = jnp.pad(uq, ((0, 0), (0, 0), (0, LANES - MLA_QK))).reshape(MLA_Q_LORA, MLA_HEADS * LANES)
    ukv = W['ev_w_ukv'].reshape(MLA_KV_LORA, MLA_HEADS, MLA_NOPE + MLA_V)
    w_k = jnp.pad(ukv[..., :MLA_NOPE], ((0, 0), (0, 0), (0, LANES - MLA_NOPE))).reshape(MLA_KV_LORA, -1)
    w_v = ukv[..., MLA_NOPE:].reshape(MLA_KV_LORA, MLA_HEADS * MLA_V)
    return w_in, w_uq, w_k, w_v, W['ev_w_out']


def _even_in_grad_unpad(dw):
    kr0 = EV_KR[0] + KR_LANE0
    return jnp.concatenate([dw[:, EV_CQ[0]:EV_CKV[1]], dw[:, kr0:kr0 + MLA_ROPE], dw[:, EV_QS[0]:EV_QS[1]],
                            dw[:, EV_KS[0]:EV_VS[1]]], axis=1)


def _even_fwd(xb, W, P, i, B, S, tabs, xchg, tag):
    j = i // 2
    q_tabs, k_tabs, bias, sinkcol = tabs
    w_in, w_uq, w_k, w_v, w_out = _even_weights(W)
    h = _mm(xb, w_in, name=f"{tag}_in")
    cqn, ckvn, rq, rkv = _even_norms(h, P['ev_q_norm'][j][None], P['ev_kv_norm'][j][None], name=f"{tag}_norms")
    q = _rope(_mm(cqn, w_uq, name=f"{tag}_uq"), q_tabs, S, sign=1.0, name=f"{tag}_ropeq")
    knp = _mm(ckvn, w_k, out_dtypes=(BF16,), name=f"{tag}_uk")
    v = _mm(ckvn, w_v, out_dtypes=(BF16,), name=f"{tag}_uv")
    k = _mla_keys(knp, h, k_tabs, S, name=f"{tag}_keys")
    o_mla, lse_mla, got = _flash_fwd(q, k, v, q_blk0=0, k_blk0=0, v_blk0=0, W=2 * LANES, n_pairs=MLA_HEADS // 2,
                                     B=B, S=S, scale=MLA_QK ** -0.5, comm=xchg.fwd_items(i), name=f"{tag}_mla")
    xchg.fwd_done(i, got)
    o_swa, lse_swa = _swa_fwd(h, bias, sinkcol, B=B, S=S, name=f"{tag}_swa")
    o_cat = jnp.concatenate([o_mla, o_swa], axis=-1)
    m = _mm(o_cat, w_out, name=f"{tag}_out")
    res = dict(h=h, cqn=cqn, ckvn=ckvn, rq=rq, rkv=rkv, q=q, k=k, v=v, o_mla=o_mla, lse_mla=lse_mla,
               o_swa=o_swa, lse_swa=lse_swa, o_cat=o_cat)
    return m, res


def _shift_prev(own, prev, B, S):
    prev = prev.reshape(B, S, LANES)
    shifted = jnp.concatenate([prev[:, BLOCK_Q:], jnp.zeros_like(prev[:, :BLOCK_Q])], axis=1)
    return (own + shifted.reshape(B * S, LANES)).astype(BF16)


def _even_bwd(dmb, dz1, xb, W, P, j, B, S, tabs, res, xchg, tag):
    q_tabs, k_tabs, bias, sinkcol = tabs
    w_in, w_uq, w_k, w_v, w_out = _even_weights(W)
    g = {}
    g['ev_w_out'] = _mm_tn(res['o_cat'], dmb, name=f"{tag}_dwout")
    do = _mm(dmb, w_out, trans_b=True, out_dtypes=(BF16,), name=f"{tag}_do")
    dq, dk, dv, got = _flash_bwd(res['q'], res['k'], res['v'], res['o_mla'], do, res['lse_mla'], q_blk0=0, k_blk0=0,
                                 v_blk0=0, do_blk0=0, W=2 * LANES, n_pairs=MLA_HEADS // 2, B=B, S=S,
                                 scale=MLA_QK ** -0.5, qk_dtype=F32, comm=xchg.bwd_items(), name=f"{tag}_mla_bwd")
    xchg.bwd_done(got)
    dq_pre = _rope(dq, q_tabs, S, sign=-1.0, name=f"{tag}_ropeq_bwd")
    dw_uq = _mm_tn(res['cqn'], dq_pre, name=f"{tag}_dwuq")
    g['ev_w_uq'] = dw_uq.reshape(MLA_Q_LORA, MLA_HEADS, LANES)[..., :MLA_QK].reshape(MLA_Q_LORA, MLA_HEADS * MLA_QK)
    dcqn = _mm(dq_pre, w_uq, trans_b=True, name=f"{tag}_dcqn")
    dw_k = _mm_tn(res['ckvn'], dk, name=f"{tag}_dwuk").reshape(MLA_KV_LORA, MLA_HEADS, LANES)[..., :MLA_NOPE]
    dw_v = _mm_tn(res['ckvn'], dv, name=f"{tag}_dwuv").reshape(MLA_KV_LORA, MLA_HEADS, MLA_V)
    g['ev_w_ukv'] = jnp.concatenate([dw_k, dw_v], axis=-1).reshape(MLA_KV_LORA, MLA_HEADS * (MLA_NOPE + MLA_V))
    dckvn_v = _mm(dv, w_v, trans_b=True, name=f"{tag}_dckvn_v")
    dckvn = _mm(dk, w_k, trans_b=True, extras=(dckvn_v,), epilogue=lambda acc, r: (acc + r,), name=f"{tag}_dckvn")
    dkr_pre = _mla_rope_key_grad(dk, k_tabs, S, name=f"{tag}_ropek_bwd")
    dqs, dko, dkp, dvo, dvp, dbias, dsink = _swa_bwd(res['h'], res['o_swa'], do, res['lse_swa'], bias, sinkcol,
                                                     do_blk0=1, B=B, S=S, name=f"{tag}_swa_bwd")
    dh, dgq, dgkv = _even_in_bwd(res['h'], res['rq'], res['rkv'], P['ev_q_norm'][j][None], P['ev_kv_norm'][j][None],
                                 dcqn, dckvn, dqs, _shift_prev(dko, dkp, B, S), _shift_prev(dvo, dvp, B, S), dkr_pre,
                                 name=f"{tag}_in_bwd")
    g['ev_w_in'] = _even_in_grad_unpad(_mm_tn(xb, dh, name=f"{tag}_dwin"))
    dx = _mm(dh, w_in, trans_b=True, extras=(dz1,), epilogue=lambda acc, r: (acc + DN_ALPHA * r,), name=f"{tag}_dx")
    small = dict(ev_q_norm=dgq[0], ev_kv_norm=dgkv[0], dbias=dbias, ev_sinks=jnp.sum(dsink, axis=(1, 2)))
    return dx, g, small


def _odd_fwd(xb, W, P, i, B, S, xchg, tag):
    j = i // 2
    w = W['od_w_in']
    w_qkv = w[:, :ODD_QKV]
    w_f = jnp.pad(w[:, ODD_QKV:], ((0, 0), (0, LANES - FOX_HEADS)))
    bf = jnp.pad(P['od_b_f'][j], (0, LANES - FOX_HEADS))[None]
    qkv = _mm(xb, w_qkv, out_dtypes=(BF16,), name=f"{tag}_qkv")
    f = _mm(xb, w_f, name=f"{tag}_f").reshape(B, S, LANES)
    csh, chs = _fox_decay_fwd(f, bf, name=f"{tag}_decay")
    crow = chs[:, :FOX_HEADS].reshape(B, FOX_HEADS, S // ATT_TILE, 1, ATT_TILE)
    n_blk = FOX_HEADS * HEAD_DIM // LANES
    o, lse, got = _flash_fwd(qkv, qkv, qkv, q_blk0=0, k_blk0=n_blk, v_blk0=2 * n_blk, W=LANES,
                             n_pairs=FOX_HEADS // 2, B=B, S=S, scale=HEAD_DIM ** -0.5, csh=csh, crow=crow,
                             comm=xchg.fwd_items(i), name=f"{tag}_fox")
    xchg.fwd_done(i, got)
    m = _mm(o, W['od_w_out'], name=f"{tag}_out")
    res = dict(f=f, bf=bf, csh=csh, crow=crow, qkv=qkv, o=o, lse=lse, w_qkv=w_qkv, w_f=w_f)
    return m, res


def _odd_bwd(dmb, dz1, xb, W, P, j, B, S, res, xchg, tag):
    g = {}
    w_out = W['od_w_out']
    g['od_w_out'] = _mm_tn(res['o'], dmb, name=f"{tag}_dwout")
    do = _mm(dmb, w_out, trans_b=True, out_dtypes=(BF16,), name=f"{tag}_do")
    qkv = res['qkv']
    n_blk = FOX_HEADS * HEAD_DIM // LANES
    dq, dk, dv, dck, dcq, got = _flash_bwd(qkv, qkv, qkv, res['o'], do, res['lse'], q_blk0=0, k_blk0=n_blk,
                                           v_blk0=2 * n_blk, do_blk0=0, W=LANES, n_pairs=FOX_HEADS // 2, B=B, S=S,
                                           scale=HEAD_DIM ** -0.5, qk_dtype=BF16, csh=res['csh'], crow=res['crow'],
                                           comm=xchg.bwd_items(), name=f"{tag}_fox_bwd")
    xchg.bwd_done(got)
    dc = dck.reshape(B, FOX_HEADS, S) + dcq.reshape(B, FOX_HEADS, S)
    dc_hs = jnp.pad(dc, ((0, 0), (0, LANES - FOX_HEADS), (0, 0)))
    df, dbf = _fox_decay_bwd(dc_hs, res['f'], res['bf'], name=f"{tag}_decay_bwd")
    df = df.reshape(B * S, LANES)
    dqkv = jnp.concatenate([dq, dk, dv], axis=-1)
    dw_qkv = _mm_tn(xb, dqkv, name=f"{tag}_dwqkv")
    dw_f = _mm_tn(xb, df, name=f"{tag}_dwf")
    g['od_w_in'] = jnp.concatenate([dw_qkv, dw_f[:, :FOX_HEADS]], axis=1)
    dxf = _mm(df, res['w_f'], trans_b=True, extras=(dz1,), epilogue=lambda acc, r: (acc + DN_ALPHA * r,),
              name=f"{tag}_dxf")
    dx = _mm(dqkv, res['w_qkv'], trans_b=True, extras=(dxf,), epilogue=lambda acc, r: (acc + r,), name=f"{tag}_dx")
    small = dict(od_b_f=dbf[0, :FOX_HEADS])
    return dx, g, small


def _local_step(x, p, target, P, xchg):
    B, S, D = x.shape
    T = B * S
    q_tabs, k_tabs = _rope_tables(S)
    bucket = _swa_bucket_table()
    bias = P['rel_bias'][bucket].astype(F32).transpose(2, 0, 1)

    xc = x.reshape(T, D)
    xcb = xc.astype(BF16)
    saved = []
    for i in range(DEPTH):
        j = i // 2
        tag = f"l{i}"
        W = xchg.layer_weights(i)
        lay = dict(xb=xcb, W=W)
        if i % 2 == 0:
            sinkcol = jnp.broadcast_to(P['ev_sinks'][j][:, None, None], (SWA_HEADS, BLOCK_Q, 1)).astype(F32)
            lay['tabs'] = (q_tabs, k_tabs, bias, sinkcol)
            m, lay['mix'] = _even_fwd(xcb, W, P, i, B, S, lay['tabs'], xchg, tag)
        else:
            m, lay['mix'] = _odd_fwd(xcb, W, P, i, B, S, xchg, tag)
        x1, x1b, lay['xh1'], lay['r1'] = _ln_fwd(xc, m, P['ln1_g'][i][None], P['ln1_b'][i][None], name=f"{tag}_ln1")
        lay['x1b'] = x1b
        lay['u'], lay['a'] = _mm(x1b, W['w_up'], out_dtypes=(F32, BF16),
                                 epilogue=lambda acc: (acc, jnp.square(jnp.maximum(acc, 0.0))), name=f"{tag}_up")
        d = _mm(lay['a'], W['w_down'], name=f"{tag}_down")
        x2, x2b, lay['xh2'], lay['r2'] = _ln_fwd(x1, d, P['ln2_g'][i][None], P['ln2_b'][i][None], name=f"{tag}_ln2")
        lay['x2b'] = x2b
        lay['p'] = p[i].reshape(T, D_PLE)
        lay['e'] = _mm(lay['p'], W['ple_w_proj'], name=f"{tag}_ple_proj")

        def gate(acc, bg, e, x2v):
            gv = 1.0 / (1.0 + jnp.exp(-(acc + bg)))
            y = x2v + gv * e
            return y, y, gv

        xc, xcb, lay['g'] = _mm(x2b, W['ple_w_gate'], extras=(P['ple_b_gate'][i][None], lay['e'], x2),
                                epilogue=gate, out_dtypes=(F32, BF16, F32), name=f"{tag}_ple_gate")
        saved.append(lay)

    dy, sq = _loss_grad(xc, target.reshape(T, D), name="loss")

    Gs = {n: [None] * DEPTH for n in ('ln1_g', 'ln1_b', 'ln2_g', 'ln2_b', 'ple_b_gate')}
    Gs.update({n: [None] * (DEPTH // 2) for n in ('ev_q_norm', 'ev_kv_norm', 'ev_sinks', 'od_b_f')})
    dbias_total = None
    for i in reversed(range(DEPTH)):
        j = i // 2
        tag = f"l{i}b"
        lay = saved[i]
        W = lay['W']
        de, dzg, dbg = _ple_bwd_elem(dy, lay['g'], lay['e'], name=f"{tag}_ple_elem")
        Gs['ple_b_gate'][i] = dbg[0]
        g_mlp = {('ple_w_proj', i): _mm_tn(lay['p'], de, name=f"{tag}_dwproj"),
                 ('ple_w_gate', i): _mm_tn(lay['x2b'], dzg, name=f"{tag}_dwgate")}
        dx2 = _mm(dzg, W['ple_w_gate'], trans_b=True, extras=(dy,), epilogue=lambda acc, r: (acc + r,),
                  name=f"{tag}_dx2")
        dz2, dz2b, dg2, db2 = _ln_bwd(dx2, lay['xh2'], lay['r2'], P['ln2_g'][i][None], name=f"{tag}_ln2")
        Gs['ln2_g'][i], Gs['ln2_b'][i] = dg2[0], db2[0]
        g_mlp[('w_down', i)] = _mm_tn(lay['a'], dz2b, name=f"{tag}_dwdown")
        du = _mm(dz2b, W['w_down'], trans_b=True, extras=(lay['u'],), out_dtypes=(BF16,),
                 epilogue=lambda acc, u: (acc * (2.0 * jnp.maximum(u, 0.0)),), name=f"{tag}_du")
        g_mlp[('w_up', i)] = _mm_tn(lay['x1b'], du, name=f"{tag}_dwup")
        xchg.push_grads(g_mlp)
        dx1 = _mm(du, W['w_up'], trans_b=True, extras=(dz2,), epilogue=lambda acc, r: (acc + DN_ALPHA * r,),
                  name=f"{tag}_dx1")
        dz1, dz1b, dg1, db1 = _ln_bwd(dx1, lay['xh1'], lay['r1'], P['ln1_g'][i][None], name=f"{tag}_ln1")
        Gs['ln1_g'][i], Gs['ln1_b'][i] = dg1[0], db1[0]
        if i % 2 == 0:
            dy, g, small = _even_bwd(dz1b, dz1, lay['xb'], W, P, j, B, S, lay['tabs'], lay['mix'], xchg, tag)
            dbias_total = small['dbias'] if dbias_total is None else dbias_total + small['dbias']
            for n in ('ev_q_norm', 'ev_kv_norm', 'ev_sinks'):
                Gs[n][j] = small[n]
        else:
            dy, g, small = _odd_bwd(dz1b, dz1, lay['xb'], W, P, j, B, S, lay['mix'], xchg, tag)
            Gs['od_b_f'][j] = small['od_b_f']
        xchg.push_grads({(n, j): val for n, val in g.items()})

    grads_small = {n: jnp.stack(v) for n, v in Gs.items()}
    drel = _bias_bucket_sum(dbias_total, bucket, name="rel_bias_grad")
    grads_small['rel_bias'] = drel[:, :REL_BUCKETS].T
    return sq, dy.reshape(B, S, D), grads_small


def kernel(x, p, rel_bias, ev_w_in, ev_q_norm, ev_w_uq, ev_kv_norm, ev_w_ukv, ev_sinks, ev_w_out, od_w_in, od_b_f, od_w_out, ln1_g, ln1_b, w_up, w_down, ln2_g, ln2_b, ple_w_proj, ple_w_gate, ple_b_gate, loss_target, m_rel_bias, m_ev_w_in, m_ev_q_norm, m_ev_w_uq, m_ev_kv_norm, m_ev_w_ukv, m_ev_sinks, m_ev_w_out, m_od_w_in, m_od_b_f, m_od_w_out, m_ln1_g, m_ln1_b, m_w_up, m_w_down, m_ln2_g, m_ln2_b, m_ple_w_proj, m_ple_w_gate, m_ple_b_gate, v_rel_bias, v_ev_w_in, v_ev_q_norm, v_ev_w_uq, v_ev_kv_norm, v_ev_w_ukv, v_ev_sinks, v_ev_w_out, v_od_w_in, v_od_b_f, v_od_w_out, v_ln1_g, v_ln1_b, v_w_up, v_w_down, v_ln2_g, v_ln2_b, v_ple_w_proj, v_ple_w_gate, v_ple_b_gate):
    given = dict(locals())
    w = {n: given[n] for n in WEIGHTS}
    mom = {n: given["m_" + n] for n in WEIGHTS}
    var = {n: given["v_" + n] for n in WEIGHTS}
    small_shapes = {n: w[n].shape for n in SMALL}

    shards = {(n, idx): w[n][idx].astype(BF16) for n in BIG for idx in range(w[n].shape[0])}
    xchg = _MeshExchange(shards)
    P = {n: w[n] for n in SMALL}

    sq, grad_x, grads_small = _local_step(x, p, loss_target, P, xchg)
    loss = lax.psum(0.5 * jnp.sum(sq) / D_MODEL, ("x", "y", "c"))

    received = xchg.finish()
    g_small_packed = _all_reduce_small(_pack_small(grads_small), name="reduce_small_grads")
    g_small = _unpack_small(g_small_packed, small_shapes)

    grad, delta, new_m, new_v = {}, {}, {}, {}
    for n in BIG:
        per_layer = [_adamw_slots(w[n][idx], received[(n, idx)], mom[n][idx], var[n][idx], name=f"adamw_{n}{idx}")
                     for idx in range(w[n].shape[0])]
        grad[n], delta[n], new_m[n], new_v[n] = (jnp.stack(t) for t in zip(*per_layer))
    d, nm, nv = _adamw(_pack_small(w), g_small_packed, _pack_small(mom), _pack_small(var), name="adamw_small")
    d, nm, nv = (_unpack_small(t, small_shapes) for t in (d, nm, nv))
    for n in SMALL:
        grad[n], delta[n], new_m[n], new_v[n] = g_small[n], d[n], nm[n], nv[n]

    return (loss, grad_x, *[grad[n] for n in WEIGHTS], *[delta[n] for n in WEIGHTS],
            *[new_m[n] for n in WEIGHTS], *[new_v[n] for n in WEIGHTS])
```

```python
import math

import jax
import jax.numpy as jnp
from jax import lax
from jax.experimental import pallas as pl
from jax.experimental.pallas import tpu as pltpu

F32, BF16 = jnp.float32, jnp.bfloat16

D_MODEL = 1024
DEPTH = 4
HEAD_DIM = 64
MLA_HEADS, MLA_NOPE, MLA_ROPE, MLA_V = 8, 64, 32, 64
MLA_Q_LORA, MLA_KV_LORA = 384, 256
MLA_QK = MLA_NOPE + MLA_ROPE
ROPE_THETA = 10000.0
SWA_HEADS, SWA_KV_HEADS, SWA_WINDOW = 8, 2, 128
REL_BUCKETS, REL_MAX_DIST = 32, 128
FOX_HEADS = 16
D_FF = 4 * D_MODEL
D_PLE = 256
BLOCK_Q = 128
DN_ALPHA = (2 * DEPTH) ** 0.25
NORM_EPS = 1e-5
NEG_INF = -1e30
EVEN_IN = 1440
ODD_QKV = 3 * FOX_HEADS * HEAD_DIM
LANES = 128

EV_QS = (0, 512)
EV_CQ = (512, 896)
EV_CKV = (896, 1152)
EV_KS = (1152, 1280)
EV_VS = (1280, 1408)
EV_KR = (1408, 1536)
EVEN_IN_PAD = 1536
KR_LANE0 = MLA_NOPE

ADAM_LR, ADAM_B1, ADAM_B2, ADAM_EPS, ADAM_WD, ADAM_STEP = 0.001, 0.9, 0.999, 1e-08, 0.01, 10

N_DEV = 8
VMEM_LIMIT_BYTES = 48 * 1024 * 1024
ATT_TILE = 512
ATT_TILE_BWD = 512
PAIRS_PER_STEP_FWD = 4
PAIRS_PER_STEP_BWD = 2

NN = (((1,), (0,)), ((), ()))
NT = (((1,), (1,)), ((), ()))
TN = (((0,), (0,)), ((), ()))

BIG = ['ev_w_in', 'ev_w_uq', 'ev_w_ukv', 'ev_w_out', 'od_w_in', 'od_w_out', 'w_up', 'w_down',
       'ple_w_proj', 'ple_w_gate']
BIG_AXIS = {'ev_w_in': 2, 'ev_w_uq': 2, 'ev_w_ukv': 2, 'ev_w_out': 1, 'od_w_in': 2, 'od_w_out': 1,
            'w_up': 2, 'w_down': 1, 'ple_w_proj': 2, 'ple_w_gate': 1}
SMALL = ['rel_bias', 'ev_q_norm', 'ev_kv_norm', 'ev_sinks', 'od_b_f', 'ln1_g', 'ln1_b', 'ln2_g', 'ln2_b',
         'ple_b_gate']
WEIGHTS = ['rel_bias', 'ev_w_in', 'ev_q_norm', 'ev_w_uq', 'ev_kv_norm', 'ev_w_ukv', 'ev_sinks', 'ev_w_out',
           'od_w_in', 'od_b_f', 'od_w_out', 'ln1_g', 'ln1_b', 'w_up', 'w_down', 'ln2_g', 'ln2_b',
           'ple_w_proj', 'ple_w_gate', 'ple_b_gate']


def _cparams(*sem):
    return pltpu.CompilerParams(dimension_semantics=sem, vmem_limit_bytes=VMEM_LIMIT_BYTES)


def _pick(n, cands):
    for c in cands:
        if n % c == 0:
            return c
    return n


MM_STEP_BYTES = 10 * 1024 * 1024
MM_OUT_BYTES = 8 * 1024 * 1024
MM_CHUNK = 512


def _mm(a, b, *, trans_b=False, extras=(), epilogue=None, out_dtypes=(F32,), name):
    M, K = a.shape
    N = b.shape[0] if trans_b else b.shape[1]
    n_ex, n_out = len(extras), len(out_dtypes)
    row_bytes = K * a.dtype.itemsize + N * (sum(jnp.dtype(d).itemsize for d in out_dtypes)
                                            + sum(e.dtype.itemsize for e in extras if e.shape[0] == M))
    tm = next((c for c in (1024, 512, 256) if M % c == 0 and c * row_bytes <= MM_STEP_BYTES), 128)
    nc = _pick(N, (MM_CHUNK, 384, 256, 128))

    def body(*refs):
        a_ref, b_ref = refs[:2]
        ex = refs[2:2 + n_ex]
        outs = refs[2 + n_ex:]
        av = a_ref[...].astype(BF16)
        for n0 in range(0, N, nc):
            cols = slice(n0, n0 + nc)
            bv = (b_ref[cols, :] if trans_b else b_ref[:, cols]).astype(BF16)
            acc = lax.dot_general(av, bv, NT if trans_b else NN, preferred_element_type=F32)
            res = epilogue(acc, *[e[:, cols] for e in ex]) if epilogue is not None else (acc,)
            for o, r in zip(outs, res):
                o[:, cols] = r.astype(o.dtype)

    in_specs = [pl.BlockSpec((tm, K), lambda i: (i, 0)), pl.BlockSpec(b.shape, lambda i: (0, 0))]
    for e in extras:
        if e.shape == (M, N):
            in_specs.append(pl.BlockSpec((tm, N), lambda i: (i, 0)))
        elif e.shape == (1, N):
            in_specs.append(pl.BlockSpec((1, N), lambda i: (0, 0)))
        else:
            raise ValueError(f"extra operand of shape {e.shape} for a ({M}, {N}) result")
    res = pl.pallas_call(
        body, name=name, grid=(M // tm,), in_specs=in_specs,
        out_specs=[pl.BlockSpec((tm, N), lambda i: (i, 0)) for _ in out_dtypes],
        out_shape=[jax.ShapeDtypeStruct((M, N), d) for d in out_dtypes],
        compiler_params=_cparams("parallel"),
    )(a, b, *extras)
    return res[0] if n_out == 1 else tuple(res)


def _mm_tn(a, b, *, name):
    T, K = a.shape
    N = b.shape[1]
    bk, bn = K, N
    while bk * bn * 4 > MM_OUT_BYTES:
        if bn >= bk and bn % (2 * LANES) == 0:
            bn //= 2
        else:
            bk //= 2
    tt = _pick(T, (1024, 512, 256))
    ck, cn = _pick(bk, (MM_CHUNK, 384, 256, 128)), _pick(bn, (MM_CHUNK, 384, 256, 128))

    def body(a_ref, b_ref, o_ref):
        t = pl.program_id(2)

        @pl.when(t == 0)
        def _():
            o_ref[...] = jnp.zeros_like(o_ref)

        for r0 in range(0, bk, ck):
            av = a_ref[:, r0:r0 + ck].astype(BF16)
            for c0 in range(0, bn, cn):
                o_ref[r0:r0 + ck, c0:c0 + cn] += lax.dot_general(
                    av, b_ref[:, c0:c0 + cn].astype(BF16), TN, preferred_element_type=F32)

    return pl.pallas_call(
        body, name=name, grid=(K // bk, N // bn, T // tt),
        in_specs=[pl.BlockSpec((tt, bk), lambda i, j, t: (t, i)), pl.BlockSpec((tt, bn), lambda i, j, t: (t, j))],
        out_specs=pl.BlockSpec((bk, bn), lambda i, j, t: (i, j)),
        out_shape=jax.ShapeDtypeStruct((K, N), F32),
        compiler_params=_cparams("parallel", "parallel", "arbitrary"),
    )(a, b)


ROW_TILE = 256


def _row_spec(cols, col_block=0):
    return pl.BlockSpec((ROW_TILE, cols), lambda i: (i, col_block))


def _tab_spec(cols, period):
    return pl.BlockSpec((ROW_TILE, cols), lambda i: (i % period, 0))


def _full_spec(shape):
    return pl.BlockSpec(shape, lambda i: (0,) * len(shape))


def _ln_fwd(x, m, g, b, *, name):
    T, D = x.shape

    def body(x_ref, m_ref, g_ref, b_ref, y_ref, yb_ref, xh_ref, r_ref):
        z = DN_ALPHA * x_ref[...] + m_ref[...]
        mu = jnp.mean(z, -1, keepdims=True)
        zc = z - mu
        r = lax.rsqrt(jnp.mean(zc * zc, -1, keepdims=True) + NORM_EPS)
        xh = zc * r
        y = xh * g_ref[...] + b_ref[...]
        y_ref[...] = y
        yb_ref[...] = y.astype(BF16)
        xh_ref[...] = xh
        r_ref[...] = jnp.broadcast_to(r, r_ref.shape)

    return pl.pallas_call(
        body, name=name, grid=(T // ROW_TILE,),
        in_specs=[_row_spec(D), _row_spec(D), _full_spec((1, D)), _full_spec((1, D))],
        out_specs=[_row_spec(D), _row_spec(D), _row_spec(D), _row_spec(LANES)],
        out_shape=[jax.ShapeDtypeStruct((T, D), F32), jax.ShapeDtypeStruct((T, D), BF16),
                   jax.ShapeDtypeStruct((T, D), F32), jax.ShapeDtypeStruct((T, LANES), F32)],
        compiler_params=_cparams("parallel"),
    )(x, m, g, b)


def _ln_bwd(dy, xh, r, g, *, name):
    T, D = dy.shape

    def body(dy_ref, xh_ref, r_ref, g_ref, dz_ref, dzb_ref, dg_ref, db_ref):
        dyv, xhv = dy_ref[...], xh_ref[...]
        dyg = dyv * g_ref[...]
        c1 = jnp.mean(dyg, -1, keepdims=True)
        c2 = jnp.mean(dyg * xhv, -1, keepdims=True)
        dz = _widen(r_ref[...], D) * (dyg - c1 - xhv * c2)
        dz_ref[...] = dz
        dzb_ref[...] = dz.astype(BF16)

        @pl.when(pl.program_id(0) == 0)
        def _():
            dg_ref[...] = jnp.zeros_like(dg_ref)
            db_ref[...] = jnp.zeros_like(db_ref)

        dg_ref[...] += jnp.sum(dyv * xhv, 0, keepdims=True)
        db_ref[...] += jnp.sum(dyv, 0, keepdims=True)

    return pl.pallas_call(
        body, name=name, grid=(T // ROW_TILE,),
        in_specs=[_row_spec(D), _row_spec(D), _row_spec(LANES), _full_spec((1, D))],
        out_specs=[_row_spec(D), _row_spec(D), _full_spec((1, D)), _full_spec((1, D))],
        out_shape=[jax.ShapeDtypeStruct((T, D), F32), jax.ShapeDtypeStruct((T, D), BF16),
                   jax.ShapeDtypeStruct((1, D), F32), jax.ShapeDtypeStruct((1, D), F32)],
        compiler_params=_cparams("arbitrary"),
    )(dy, xh, r, g)


def _loss_grad(y, target, *, name):
    T, D = y.shape

    def body(y_ref, t_ref, dy_ref, sq_ref):
        err = y_ref[...] - t_ref[...]
        dy_ref[...] = err / D

        @pl.when(pl.program_id(0) == 0)
        def _():
            sq_ref[...] = jnp.zeros_like(sq_ref)

        sq_ref[...] += jnp.sum(err * err, 0, keepdims=True)

    return pl.pallas_call(
        body, name=name, grid=(T // ROW_TILE,),
        in_specs=[_row_spec(D), _row_spec(D)],
        out_specs=[_row_spec(D), _full_spec((1, D))],
        out_shape=[jax.ShapeDtypeStruct((T, D), F32), jax.ShapeDtypeStruct((1, D), F32)],
        compiler_params=_cparams("arbitrary"),
    )(y, target)


def _ple_bwd_elem(dx3, g, e, *, name):
    T, D = dx3.shape

    def body(dx_ref, g_ref, e_ref, de_ref, dz_ref, db_ref):
        dx, gv = dx_ref[...], g_ref[...]
        de_ref[...] = (dx * gv).astype(BF16)
        dz = dx * e_ref[...] * gv * (1.0 - gv)
        dz_ref[...] = dz.astype(BF16)

        @pl.when(pl.program_id(0) == 0)
        def _():
            db_ref[...] = jnp.zeros_like(db_ref)

        db_ref[...] += jnp.sum(dz, 0, keepdims=True)

    return pl.pallas_call(
        body, name=name, grid=(T // ROW_TILE,),
        in_specs=[_row_spec(D), _row_spec(D), _row_spec(D)],
        out_specs=[_row_spec(D), _row_spec(D), _full_spec((1, D))],
        out_shape=[jax.ShapeDtypeStruct((T, D), BF16), jax.ShapeDtypeStruct((T, D), BF16),
                   jax.ShapeDtypeStruct((1, D), F32)],
        compiler_params=_cparams("arbitrary"),
    )(dx3, g, e)


def _rotate(xv, a, bm, bp, sign):
    half = MLA_ROPE // 2
    width = xv.shape[-1]
    a, bm, bp = (_widen(t, width) for t in (a, bm, bp))
    return xv * a + sign * (pltpu.roll(xv, width - half, 1) * bm + pltpu.roll(xv, half, 1) * bp)


def _rope(x, tabs, seq, *, sign, name):
    T, width = x.shape

    def body(x_ref, a_ref, bm_ref, bp_ref, o_ref):
        o_ref[...] = _rotate(x_ref[...], a_ref[...], bm_ref[...], bp_ref[...], sign).astype(BF16)

    return pl.pallas_call(
        body, name=name, grid=(T // ROW_TILE,),
        in_specs=[_row_spec(width)] + [_tab_spec(LANES, seq // ROW_TILE)] * 3,
        out_specs=_row_spec(width),
        out_shape=jax.ShapeDtypeStruct((T, width), BF16),
        compiler_params=_cparams("parallel"),
    )(x, *tabs)


def _mla_keys(knp, h, k_tabs, seq, *, name):
    T = knp.shape[0]

    def body(k_ref, h_ref, a_ref, bm_ref, bp_ref, o_ref):
        kr = _rotate(h_ref[...], a_ref[...], bm_ref[...], bp_ref[...], 1.0)
        for hd in range(MLA_HEADS):
            cols = slice(hd * LANES, (hd + 1) * LANES)
            o_ref[:, cols] = (k_ref[:, cols].astype(F32) + kr).astype(BF16)

    return pl.pallas_call(
        body, name=name, grid=(T // ROW_TILE,),
        in_specs=[_row_spec(MLA_HEADS * LANES), _row_spec(LANES, EV_KR[0] // LANES)]
        + [_tab_spec(LANES, seq // ROW_TILE)] * 3,
        out_specs=_row_spec(MLA_HEADS * LANES),
        out_shape=jax.ShapeDtypeStruct((T, MLA_HEADS * LANES), BF16),
        compiler_params=_cparams("parallel"),
    )(knp, h, *k_tabs)


def _mla_rope_key_grad(dk, k_tabs, seq, *, name):
    T = dk.shape[0]

    def body(dk_ref, a_ref, bm_ref, bp_ref, o_ref):
        tot = dk_ref[:, 0:LANES]
        for hd in range(1, MLA_HEADS):
            tot = tot + dk_ref[:, hd * LANES:(hd + 1) * LANES]
        o_ref[...] = _rotate(tot, a_ref[...], bm_ref[...], bp_ref[...], -1.0).astype(BF16)

    return pl.pallas_call(
        body, name=name, grid=(T // ROW_TILE,),
        in_specs=[_row_spec(MLA_HEADS * LANES)] + [_tab_spec(LANES, seq // ROW_TILE)] * 3,
        out_specs=_row_spec(LANES),
        out_shape=jax.ShapeDtypeStruct((T, LANES), BF16),
        compiler_params=_cparams("parallel"),
    )(dk, *k_tabs)


def _even_norms(h, gq, gkv, *, name):
    T = h.shape[0]

    def body(h_ref, gq_ref, gkv_ref, cq_ref, ckv_ref, rq_ref, rkv_ref):
        cq = h_ref[:, EV_CQ[0]:EV_CQ[1]]
        rq = lax.rsqrt(jnp.mean(cq * cq, -1, keepdims=True) + NORM_EPS)
        cq_ref[...] = (cq * rq * gq_ref[...]).astype(BF16)
        rq_ref[...] = jnp.broadcast_to(rq, rq_ref.shape)
        ckv = h_ref[:, EV_CKV[0]:EV_CKV[1]]
        rkv = lax.rsqrt(jnp.mean(ckv * ckv, -1, keepdims=True) + NORM_EPS)
        ckv_ref[...] = (ckv * rkv * gkv_ref[...]).astype(BF16)
        rkv_ref[...] = jnp.broadcast_to(rkv, rkv_ref.shape)

    return pl.pallas_call(
        body, name=name, grid=(T // ROW_TILE,),
        in_specs=[_row_spec(EVEN_IN_PAD), _full_spec((1, MLA_Q_LORA)), _full_spec((1, MLA_KV_LORA))],
        out_specs=[_row_spec(MLA_Q_LORA), _row_spec(MLA_KV_LORA), _row_spec(LANES), _row_spec(LANES)],
        out_shape=[jax.ShapeDtypeStruct((T, MLA_Q_LORA), BF16), jax.ShapeDtypeStruct((T, MLA_KV_LORA), BF16),
                   jax.ShapeDtypeStruct((T, LANES), F32), jax.ShapeDtypeStruct((T, LANES), F32)],
        compiler_params=_cparams("parallel"),
    )(h, gq, gkv)


def _even_in_bwd(h, rq, rkv, gq, gkv, dcqn, dckvn, dqs, dks, dvs, dkr, *, name):
    T = h.shape[0]

    def rms_bwd(c, r, g, dy):
        r = _widen(r, c.shape[-1])
        xr = c * r
        dyg = dy * g
        return r * (dyg - xr * jnp.mean(dyg * xr, -1, keepdims=True)), jnp.sum(dy * xr, 0, keepdims=True)

    def body(h_ref, rq_ref, rkv_ref, gq_ref, gkv_ref, dcq_ref, dckv_ref, dqs_ref, dks_ref, dvs_ref, dkr_ref,
             dh_ref, dgq_ref, dgkv_ref):
        @pl.when(pl.program_id(0) == 0)
        def _():
            dgq_ref[...] = jnp.zeros_like(dgq_ref)
            dgkv_ref[...] = jnp.zeros_like(dgkv_ref)

        dcq, dgq = rms_bwd(h_ref[:, EV_CQ[0]:EV_CQ[1]], rq_ref[...], gq_ref[...], dcq_ref[...])
        dckv, dgkv = rms_bwd(h_ref[:, EV_CKV[0]:EV_CKV[1]], rkv_ref[...], gkv_ref[...], dckv_ref[...])
        dgq_ref[...] += dgq
        dgkv_ref[...] += dgkv
        dh_ref[:, EV_QS[0]:EV_QS[1]] = dqs_ref[...]
        dh_ref[:, EV_CQ[0]:EV_CQ[1]] = dcq.astype(BF16)
        dh_ref[:, EV_CKV[0]:EV_CKV[1]] = dckv.astype(BF16)
        dh_ref[:, EV_KS[0]:EV_KS[1]] = dks_ref[...]
        dh_ref[:, EV_VS[0]:EV_VS[1]] = dvs_ref[...]
        dh_ref[:, EV_KR[0]:EV_KR[1]] = dkr_ref[...]

    return pl.pallas_call(
        body, name=name, grid=(T // ROW_TILE,),
        in_specs=[_row_spec(EVEN_IN_PAD), _row_spec(LANES), _row_spec(LANES), _full_spec((1, MLA_Q_LORA)),
                  _full_spec((1, MLA_KV_LORA)), _row_spec(MLA_Q_LORA), _row_spec(MLA_KV_LORA),
                  _row_spec(SWA_HEADS * HEAD_DIM), _row_spec(LANES), _row_spec(LANES), _row_spec(LANES)],
        out_specs=[_row_spec(EVEN_IN_PAD), _full_spec((1, MLA_Q_LORA)), _full_spec((1, MLA_KV_LORA))],
        out_shape=[jax.ShapeDtypeStruct((T, EVEN_IN_PAD), BF16), jax.ShapeDtypeStruct((1, MLA_Q_LORA), F32),
                   jax.ShapeDtypeStruct((1, MLA_KV_LORA), F32)],
        compiler_params=_cparams("arbitrary"),
    )(h, rq, rkv, gq, gkv, dcqn, dckvn, dqs, dks, dvs, dkr)


def _fox_decay_fwd(f3, bf, *, name):
    B, S, _ = f3.shape

    def body(f_ref, b_ref, csh_ref, chs_ref):
        x = f_ref[...] + b_ref[...]
        c = jnp.minimum(x, 0.0) - jnp.log1p(jnp.exp(-jnp.abs(x)))
        row = lax.broadcasted_iota(jnp.int32, (S, LANES), 0)
        k = 1
        while k < S:
            c = c + jnp.where(row >= k, pltpu.roll(c, k, 0), 0.0)
            k *= 2
        csh_ref[...] = c
        chs_ref[...] = c.T

    return pl.pallas_call(
        body, name=name, grid=(B,),
        in_specs=[pl.BlockSpec((None, S, LANES), lambda b: (b, 0, 0)), pl.BlockSpec((1, LANES), lambda b: (0, 0))],
        out_specs=[pl.BlockSpec((None, S, LANES), lambda b: (b, 0, 0)),
                   pl.BlockSpec((None, LANES, S), lambda b: (b, 0, 0))],
        out_shape=[jax.ShapeDtypeStruct((B, S, LANES), F32), jax.ShapeDtypeStruct((B, LANES, S), F32)],
        compiler_params=_cparams("parallel"),
    )(f3, bf)


def _fox_decay_bwd(dc_hs, f3, bf, *, name):
    B, S, _ = f3.shape

    def body(dc_ref, f_ref, b_ref, df_ref, db_ref):
        g = dc_ref[...].T
        row = lax.broadcasted_iota(jnp.int32, (S, LANES), 0)
        k = 1
        while k < S:
            g = g + jnp.where(row < S - k, pltpu.roll(g, S - k, 0), 0.0)
            k *= 2
        x = f_ref[...] + b_ref[...]
        df = g * (1.0 / (1.0 + jnp.exp(x)))
        df_ref[...] = df.astype(BF16)

        @pl.when(pl.program_id(0) == 0)
        def _():
            db_ref[...] = jnp.zeros_like(db_ref)

        db_ref[...] += jnp.sum(df, 0, keepdims=True)

    return pl.pallas_call(
        body, name=name, grid=(B,),
        in_specs=[pl.BlockSpec((None, LANES, S), lambda b: (b, 0, 0)),
                  pl.BlockSpec((None, S, LANES), lambda b: (b, 0, 0)), pl.BlockSpec((1, LANES), lambda b: (0, 0))],
        out_specs=[pl.BlockSpec((None, S, LANES), lambda b: (b, 0, 0)), pl.BlockSpec((1, LANES), lambda b: (0, 0))],
        out_shape=[jax.ShapeDtypeStruct((B, S, LANES), BF16), jax.ShapeDtypeStruct((1, LANES), F32)],
        compiler_params=_cparams("arbitrary"),
    )(dc_hs, f3, bf)


def _head_column(block, h):
    lane = lax.broadcasted_iota(jnp.int32, block.shape, 1)
    return jnp.sum(jnp.where(lane == h, block, 0.0), axis=-1, keepdims=True)


def _causal_mask(s):
    r = lax.broadcasted_iota(jnp.int32, s.shape, 0)
    c = lax.broadcasted_iota(jnp.int32, s.shape, 1)
    return jnp.where(c <= r, s, NEG_INF)


def _low_half(shape):
    return (lax.broadcasted_iota(jnp.int32, shape, 1) % LANES) < HEAD_DIM


def _widen(x, cols):
    return jnp.concatenate([x] * (cols // LANES), axis=1)


def _both_halves(x, lo):
    r = pltpu.roll(x, HEAD_DIM, 1)
    return jnp.where(lo, x, r), jnp.where(lo, r, x)


MESH_ID = pl.DeviceIdType.MESH
HBM_SPEC = pl.BlockSpec(memory_space=pltpu.HBM)
VMEM_SPEC = pl.BlockSpec(memory_space=pltpu.VMEM)


def _mesh_place():
    x, y, c = lax.axis_index("x"), lax.axis_index("y"), lax.axis_index("c")
    return x, y, c, 4 * x + 2 * y + c


def _peers(x, y, c):
    out = []
    for mask in range(1, N_DEV):
        dx, dy, dc = (mask >> 2) & 1, (mask >> 1) & 1, mask & 1
        px, py, pc = (1 - x if dx else x), (1 - y if dy else y), (1 - c if dc else c)
        out.append(((px, py, pc), 4 * px + 2 * py + pc))
    return out


def _comm_out_shapes(comm):
    return [jax.ShapeDtypeStruct(((N_DEV,) + a.shape) if kind == "gather" else a.shape, a.dtype) for kind, a in comm]


def _comm_scratch(comm):
    n = len(comm)
    return [pltpu.SemaphoreType.DMA((n, 7)), pltpu.SemaphoreType.DMA((n, 7)), pltpu.SemaphoreType.DMA((n,))]


def _comm_copies(kinds, in_refs, out_refs, sems, place):
    send_sems, recv_sems, local_sems = sems
    x, y, c, me = place
    local, remote = [], []
    for w, kind in enumerate(kinds):
        mine = in_refs[w] if kind == "gather" else in_refs[w].at[me]
        local.append(pltpu.make_async_copy(mine, out_refs[w].at[me], local_sems.at[w]))
        for k, (peer, peer_idx) in enumerate(_peers(x, y, c)):
            remote.append(pltpu.make_async_remote_copy(
                src_ref=in_refs[w] if kind == "gather" else in_refs[w].at[peer_idx], dst_ref=out_refs[w].at[me],
                send_sem=send_sems.at[w, k], recv_sem=recv_sems.at[w, k], device_id=peer, device_id_type=MESH_ID))
    return local, remote


def _comm_start(kinds, in_refs, out_refs, sems, place):
    local, remote = _comm_copies(kinds, in_refs, out_refs, sems, place)
    for cp in local + remote:
        cp.start()


def _comm_wait(kinds, in_refs, out_refs, sems, place):
    local, remote = _comm_copies(kinds, in_refs, out_refs, sems, place)
    for cp in remote:
        cp.wait_recv()
    for cp in remote:
        cp.wait_send()
    for cp in local:
        cp.wait()


def _exchange(comm, *, name):
    n = len(comm)
    kinds = [k for k, _ in comm]

    def body(*refs):
        place = _mesh_place()
        _comm_start(kinds, refs[:n], refs[n:2 * n], refs[2 * n:], place)
        _comm_wait(kinds, refs[:n], refs[n:2 * n], refs[2 * n:], place)

    return pl.pallas_call(
        body, name=name, out_shape=_comm_out_shapes(comm), in_specs=[HBM_SPEC] * n, out_specs=[HBM_SPEC] * n,
        scratch_shapes=_comm_scratch(comm),
    )(*[a for _, a in comm])


def _flash_fwd(qa, ka, va, *, q_blk0, k_blk0, v_blk0, W, n_pairs, B, S, scale, csh=None, crow=None, comm=(), name):
    t = ATT_TILE
    nq = S // t
    P = PAIRS_PER_STEP_FWD
    decay = csh is not None
    split = W == LANES
    assert n_pairs % P == 0 and q_blk0 % P == 0 and k_blk0 % P == 0 and v_blk0 % P == 0
    n_c, kinds = len(comm), [k for k, _ in comm]
    n_in = 5 if decay else 3
    n_steps = (B, n_pairs // P, nq)

    def body(*refs):
        c_in, c_out = refs[n_in:n_in + n_c], refs[n_in + n_c + 2:n_in + 2 * n_c + 2]
        sems = refs[n_in + 2 * n_c + 4:]
        refs = refs[:n_in] + refs[n_in + n_c:n_in + n_c + 2] + refs[n_in + 2 * n_c + 2:n_in + 2 * n_c + 4]
        if decay:
            q_ref, k_ref, v_ref, csh_ref, crow_ref, o_ref, lse_ref, m_s, acc_s = refs
        else:
            q_ref, k_ref, v_ref, o_ref, lse_ref, m_s, acc_s = refs
        g, i = pl.program_id(1), pl.program_id(2)
        if n_c:
            place = _mesh_place()
            ids = [pl.program_id(ax) for ax in range(3)]

            @pl.when((ids[0] == 0) & (ids[1] == 0) & (ids[2] == 0))
            def _():
                _comm_start(kinds, c_in, c_out, sems, place)

        lo = _low_half((t, LANES))
        qv = q_ref[...]
        qh = []
        for pr in range(P):
            qp = qv[:, pr * W:(pr + 1) * W]
            qh += [jnp.where(lo, qp, jnp.zeros_like(qp)), jnp.where(lo, jnp.zeros_like(qp), qp)] if split \
                else [qp[:, :LANES], qp[:, LANES:]]
        if decay:
            cq = [jnp.broadcast_to(_head_column(csh_ref[...], 2 * P * g + hd), (t, LANES)) for hd in range(2 * P)]
        m_s[...] = jnp.full(m_s.shape, NEG_INF, F32)
        acc_s[...] = jnp.zeros(acc_s.shape, F32)

        def step(j, masked):
            rows = pl.ds(pl.multiple_of(j * t, t), t)
            kb, vb = k_ref[rows, :], v_ref[rows, :]
            for pr in range(P):
                kp, vp = kb[:, pr * W:(pr + 1) * W], vb[:, pr * LANES:(pr + 1) * LANES]
                ones = jnp.ones_like(vp)
                vaug = [jnp.where(lo, vp, ones), jnp.where(lo, ones, vp)]
                for half in range(2):
                    hd = 2 * pr + half
                    kh = kp if split else kp[:, half * LANES:(half + 1) * LANES]
                    s = lax.dot_general(qh[hd], kh, NT, preferred_element_type=F32) * scale
                    if decay:
                        s = s + _widen(cq[hd], t) - crow_ref[hd, j]
                    if masked:
                        s = _causal_mask(s)
                    m_prev = m_s[hd]
                    m_new = jnp.maximum(m_prev, jnp.max(s, -1, keepdims=True))
                    p = jnp.exp(s - _widen(m_new, t))
                    acc_s[hd] = jnp.exp(m_prev - m_new) * acc_s[hd] + lax.dot_general(
                        p.astype(BF16), vaug[half], NN, preferred_element_type=F32)
                    m_s[hd] = m_new

        def loop_body(j, carry):
            step(j, False)
            return carry

        lax.fori_loop(0, i, loop_body, 0)
        step(i, True)
        for pr in range(P):
            acc0, acc1 = acc_s[2 * pr], acc_s[2 * pr + 1]
            _, l0 = _both_halves(acc0, lo)
            l1, _ = _both_halves(acc1, lo)
            cols = slice(pr * LANES, (pr + 1) * LANES)
            o_ref[:, cols] = jnp.where(lo, acc0 / l0, acc1 / l1).astype(BF16)
            lse_ref[:, cols] = jnp.where(lo, m_s[2 * pr] + jnp.log(l0), m_s[2 * pr + 1] + jnp.log(l1))
        if n_c:
            @pl.when((ids[0] == n_steps[0] - 1) & (ids[1] == n_steps[1] - 1) & (ids[2] == n_steps[2] - 1))
            def _():
                _comm_wait(kinds, c_in, c_out, sems, place)

    in_specs = [pl.BlockSpec((t, P * W), lambda b, g, i: (b * nq + i, q_blk0 // P + g)),
                pl.BlockSpec((S, P * W), lambda b, g, i: (b, k_blk0 // P + g)),
                pl.BlockSpec((S, P * LANES), lambda b, g, i: (b, v_blk0 // P + g))]
    args = [qa, ka, va]
    if decay:
        in_specs += [pl.BlockSpec((None, t, LANES), lambda b, g, i: (b, i, 0)),
                     pl.BlockSpec((None, 2 * P, nq, 1, t), lambda b, g, i: (b, g, 0, 0, 0))]
        args += [csh, crow]
    out_spec = pl.BlockSpec((t, P * LANES), lambda b, g, i: (b * nq + i, g))
    res = pl.pallas_call(
        body, name=name, grid=n_steps, in_specs=in_specs + [HBM_SPEC] * n_c,
        out_specs=[out_spec, out_spec] + [HBM_SPEC] * n_c,
        out_shape=[jax.ShapeDtypeStruct((B * S, n_pairs * LANES), BF16),
                   jax.ShapeDtypeStruct((B * S, n_pairs * LANES), F32)] + _comm_out_shapes(comm),
        scratch_shapes=[pltpu.VMEM((2 * P, t, LANES), F32), pltpu.VMEM((2 * P, t, LANES), F32)]
        + (_comm_scratch(comm) if n_c else []),
        compiler_params=_cparams(*(("arbitrary",) * 3 if n_c else ("parallel",) * 3)),
    )(*args, *[a for _, a in comm])
    return res[0], res[1], list(res[2:])


def _flash_bwd(qa, ka, va, oa, doa, lsea, *, q_blk0, k_blk0, v_blk0, do_blk0, W, n_pairs, B, S, scale, qk_dtype,
               csh=None, crow=None, comm=(), name):
    t = ATT_TILE_BWD
    nq = S // t
    P = PAIRS_PER_STEP_BWD
    decay = csh is not None
    if decay:
        crow = crow.reshape(B, 2 * n_pairs, nq, 1, t)
    split = W == LANES
    assert n_pairs % P == 0 and q_blk0 % P == 0 and k_blk0 % P == 0 and v_blk0 % P == 0 and do_blk0 % P == 0
    n_c, kinds = len(comm), [k for k, _ in comm]
    n_in, n_out, n_scr = (8, 5, 8) if decay else (6, 3, 5)
    n_steps = (B, n_pairs // P, nq)

    def body(*refs):
        c_in = refs[n_in:n_in + n_c]
        c_out = refs[n_in + n_c + n_out:n_in + 2 * n_c + n_out]
        sems = refs[n_in + 2 * n_c + n_out + n_scr:]
        refs = (refs[:n_in] + refs[n_in + n_c:n_in + n_c + n_out]
                + refs[n_in + 2 * n_c + n_out:n_in + 2 * n_c + n_out + n_scr])
        if n_c:
            place = _mesh_place()
            ids = [pl.program_id(ax) for ax in range(3)]

            @pl.when((ids[0] == 0) & (ids[1] == 0) & (ids[2] == 0))
            def _():
                _comm_start(kinds, c_in, c_out, sems, place)

        if decay:
            (q_ref, k_ref, v_ref, o_ref, do_ref, lse_ref, csh_ref, crow_ref, dq_ref, dk_ref, dv_ref, dck_ref, dcq_ref,
             dq_s, lse_s, delta_s, dk_s, dv_s, cq_s, dcq_s, dck_s) = refs
        else:
            (q_ref, k_ref, v_ref, o_ref, do_ref, lse_ref, dq_ref, dk_ref, dv_ref,
             dq_s, lse_s, delta_s, dk_s, dv_s) = refs
        g, j = pl.program_id(1), pl.program_id(2)
        lo = _low_half((t, LANES))

        @pl.when(j == 0)
        def _():
            lo_s = _low_half((S, LANES))
            dq_s[...] = jnp.zeros(dq_s.shape, F32)
            for pr in range(P):
                cols = slice(pr * LANES, (pr + 1) * LANES)
                lse_s[2 * pr], lse_s[2 * pr + 1] = _both_halves(lse_ref[:, cols], lo_s)
                dd = do_ref[:, cols].astype(F32) * o_ref[:, cols].astype(F32)
                delta_s[2 * pr] = jnp.broadcast_to(jnp.sum(jnp.where(lo_s, dd, 0.0), -1, keepdims=True), (S, LANES))
                delta_s[2 * pr + 1] = jnp.broadcast_to(jnp.sum(jnp.where(lo_s, 0.0, dd), -1, keepdims=True),
                                                       (S, LANES))
            if decay:
                for hd in range(2 * P):
                    cq_s[hd] = jnp.broadcast_to(_head_column(csh_ref[...], 2 * P * g + hd), (S, LANES))
                dcq_s[...] = jnp.zeros(dcq_s.shape, F32)

        kb, vb = k_ref[...], v_ref[...]
        kh, vh = [], []
        for pr in range(P):
            kp, vp = kb[:, pr * W:(pr + 1) * W], vb[:, pr * LANES:(pr + 1) * LANES]
            zk, zv = jnp.zeros_like(kp), jnp.zeros_like(vp)
            kh += [jnp.where(lo, kp, zk), jnp.where(lo, zk, kp)] if split else [kp[:, :LANES], kp[:, LANES:]]
            vh += [jnp.where(lo, vp, zv), jnp.where(lo, zv, vp)]
        dk_s[...] = jnp.zeros(dk_s.shape, F32)
        dv_s[...] = jnp.zeros(dv_s.shape, F32)
        if decay:
            dck_s[...] = jnp.zeros(dck_s.shape, F32)

        def step(i, masked):
            rows = pl.ds(pl.multiple_of(i * t, t), t)
            qi, doi = q_ref[rows, :], do_ref[rows, :]
            for pr in range(P):
                qp, dop = qi[:, pr * W:(pr + 1) * W], doi[:, pr * LANES:(pr + 1) * LANES]
                for half in range(2):
                    hd = 2 * pr + half
                    qx = qp if split else qp[:, half * LANES:(half + 1) * LANES]
                    s = lax.dot_general(qx, kh[hd], NT, preferred_element_type=F32) * scale
                    if decay:
                        s = s + _widen(cq_s[hd, rows, :], t) - crow_ref[hd, j]
                    if masked:
                        s = _causal_mask(s)
                    p = jnp.exp(s - _widen(lse_s[hd, rows, :], t))
                    dv_s[hd] += lax.dot_general(p.astype(BF16), dop, TN, preferred_element_type=F32)
                    dp = lax.dot_general(dop, vh[hd], NT, preferred_element_type=F32)
                    ds = p * (dp - _widen(delta_s[hd, rows, :], t))
                    dss = (ds * scale).astype(BF16)
                    dk_s[hd] += lax.dot_general(dss, qx, TN, preferred_element_type=F32)
                    dqc = lax.dot_general(dss, kh[hd], NN, preferred_element_type=F32)
                    if split:
                        dq_s[rows, pr * W:(pr + 1) * W] += dqc
                    else:
                        dq_s[rows, hd * LANES:(hd + 1) * LANES] += dqc
                    if decay:
                        dck_s[hd] -= jnp.sum(ds, 0, keepdims=True)
                        part = ds[:, :LANES]
                        for c in range(1, t // LANES):
                            part = part + ds[:, c * LANES:(c + 1) * LANES]
                        dcq_s[hd, rows, :] += part

        def loop_body(i, carry):
            step(i, False)
            return carry

        step(j, True)
        lax.fori_loop(j + 1, nq, loop_body, 0)
        for pr in range(P):
            if split:
                dk_ref[:, pr * W:(pr + 1) * W] = jnp.where(lo, dk_s[2 * pr], dk_s[2 * pr + 1]).astype(dk_ref.dtype)
            else:
                for half in range(2):
                    hd = 2 * pr + half
                    dk_ref[:, hd * LANES:(hd + 1) * LANES] = dk_s[hd].astype(dk_ref.dtype)
            dv_ref[:, pr * LANES:(pr + 1) * LANES] = jnp.where(lo, dv_s[2 * pr], dv_s[2 * pr + 1]).astype(BF16)
        if decay:
            dck_ref[...] = dck_s[...]

        @pl.when(j == nq - 1)
        def _():
            dq_ref[...] = dq_s[...].astype(dq_ref.dtype)
            if decay:
                for hd in range(2 * P):
                    dcq_ref[hd] = jnp.sum(dcq_s[hd].T, 0, keepdims=True)

        if n_c:
            @pl.when((ids[0] == n_steps[0] - 1) & (ids[1] == n_steps[1] - 1) & (ids[2] == n_steps[2] - 1))
            def _():
                _comm_wait(kinds, c_in, c_out, sems, place)

    full = lambda w, blk0: pl.BlockSpec((S, P * w), lambda b, g, j: (b, blk0 // P + g))
    blk = lambda w, blk0: pl.BlockSpec((t, P * w), lambda b, g, j: (b * nq + j, blk0 // P + g))
    in_specs = [full(W, q_blk0), blk(W, k_blk0), blk(LANES, v_blk0), full(LANES, 0), full(LANES, do_blk0),
                full(LANES, 0)]
    args = [qa, ka, va, oa, doa, lsea]
    T = B * S
    out_specs = [full(W, 0), blk(W, 0), blk(LANES, 0)]
    out_shape = [jax.ShapeDtypeStruct((T, n_pairs * W), qk_dtype), jax.ShapeDtypeStruct((T, n_pairs * W), qk_dtype),
                 jax.ShapeDtypeStruct((T, n_pairs * LANES), BF16)]
    per_head = lambda rows: pltpu.VMEM((2 * P, rows, LANES), F32)
    scratch = [pltpu.VMEM((S, P * W), F32), per_head(S), per_head(S), per_head(t), per_head(t)]
    if decay:
        in_specs += [pl.BlockSpec((None, S, LANES), lambda b, g, j: (b, 0, 0)),
                     pl.BlockSpec((None, 2 * P, nq, 1, t), lambda b, g, j: (b, g, 0, 0, 0))]
        args += [csh, crow]
        out_specs += [pl.BlockSpec((None, 2 * P, None, 1, t), lambda b, g, j: (b, g, j, 0, 0)),
                      pl.BlockSpec((None, 2 * P, 1, S), lambda b, g, j: (b, g, 0, 0))]
        out_shape += [jax.ShapeDtypeStruct((B, 2 * n_pairs, nq, 1, t), F32),
                      jax.ShapeDtypeStruct((B, 2 * n_pairs, 1, S), F32)]
        scratch += [per_head(S), per_head(S), pltpu.VMEM((2 * P, 1, t), F32)]
    res = pl.pallas_call(
        body, name=name, grid=n_steps, in_specs=in_specs + [HBM_SPEC] * n_c,
        out_specs=out_specs + [HBM_SPEC] * n_c, out_shape=out_shape + _comm_out_shapes(comm),
        scratch_shapes=scratch + (_comm_scratch(comm) if n_c else []),
        compiler_params=_cparams(*(("arbitrary",) * 3 if n_c else ("parallel", "parallel", "arbitrary"))),
    )(*args, *[a for _, a in comm])
    return tuple(res[:n_out]) + (list(res[n_out:]),)


def _swa_common(q_ref, kp_ref, ko_ref, vp_ref, vo_ref, n):
    Q = BLOCK_Q
    lo = _low_half((Q, LANES))
    lo2 = _low_half((2 * Q, LANES))
    kk = jnp.concatenate([kp_ref[...], ko_ref[...]], axis=0)
    vv = jnp.concatenate([vp_ref[...], vo_ref[...]], axis=0)
    kdup = [x.astype(BF16) for x in _both_halves(kk, lo2)]
    vdup = [x.astype(BF16) for x in _both_halves(vv, lo2)]
    a = lax.broadcasted_iota(jnp.int32, (Q, 2 * Q), 0)
    col = lax.broadcasted_iota(jnp.int32, (Q, 2 * Q), 1)
    dist = a + Q - col
    valid = (dist >= 0) & (dist < SWA_WINDOW) & ((col >= Q) | (n > 0))
    qv = q_ref[...]
    qm = []
    for a_head in range(SWA_HEADS):
        qp = qv[:, (a_head // 2) * LANES:(a_head // 2 + 1) * LANES]
        keep = lo if a_head % 2 == 0 else jnp.logical_not(lo)
        qm.append(jnp.where(keep, qp, 0.0).astype(BF16))
    return lo, lo2, kdup, vdup, valid, qm


def _swa_in_specs(nb):
    Q = BLOCK_Q
    own = lambda blk: (lambda b, n: (b * nb + n, blk))
    prev = lambda blk: (lambda b, n: (b * nb + jnp.maximum(n - 1, 0), blk))
    kb, vb = EV_KS[0] // LANES, EV_VS[0] // LANES
    return [pl.BlockSpec((Q, SWA_HEADS * HEAD_DIM), own(0)), pl.BlockSpec((Q, LANES), prev(kb)),
            pl.BlockSpec((Q, LANES), own(kb)), pl.BlockSpec((Q, LANES), prev(vb)), pl.BlockSpec((Q, LANES), own(vb))]


def _swa_fwd(h, bias, sinkcol, *, B, S, comm=(), name):
    Q = BLOCK_Q
    nb = S // Q
    scale = HEAD_DIM ** -0.5
    n_c, kinds = len(comm), [k for k, _ in comm]

    def body(*refs):
        c_in, c_out, sems = refs[7:7 + n_c], refs[9 + n_c:9 + 2 * n_c], refs[9 + 2 * n_c:]
        q_ref, kp_ref, ko_ref, vp_ref, vo_ref, bias_ref, sink_ref = refs[:7]
        o_ref, lse_ref = refs[7 + n_c:9 + n_c]
        if n_c:
            place = _mesh_place()
            ids = [pl.program_id(0), pl.program_id(1)]

            @pl.when((ids[0] == 0) & (ids[1] == 0))
            def _():
                _comm_start(kinds, c_in, c_out, sems, place)

        lo, _, kdup, vdup, valid, qm = _swa_common(q_ref, kp_ref, ko_ref, vp_ref, vo_ref, pl.program_id(1))
        lane = lax.broadcasted_iota(jnp.int32, (Q, LANES), 1)
        lse_blk = jnp.zeros((Q, LANES), F32)
        pairs = []
        for pr in range(SWA_HEADS // 2):
            pv = []
            for half in range(2):
                a = 2 * pr + half
                kvh = a // (SWA_HEADS // SWA_KV_HEADS)
                s = lax.dot_general(qm[a], kdup[kvh], NT, preferred_element_type=F32) * scale + bias_ref[a]
                s = jnp.where(valid, s, NEG_INF)
                sink = sink_ref[a]
                mx = jnp.maximum(jnp.max(s, -1, keepdims=True), sink)
                p = jnp.exp(s - mx)
                l = jnp.sum(p, -1, keepdims=True) + jnp.exp(sink - mx)
                pv.append(lax.dot_general((p / l).astype(BF16), vdup[kvh], NN, preferred_element_type=F32))
                lse_blk = jnp.where(lane == a, mx + jnp.log(l), lse_blk)
            pairs.append(jnp.where(lo, pv[0], pv[1]))
        o_ref[...] = jnp.concatenate(pairs, axis=1).astype(BF16)
        lse_ref[...] = lse_blk
        if n_c:
            @pl.when((ids[0] == B - 1) & (ids[1] == nb - 1))
            def _():
                _comm_wait(kinds, c_in, c_out, sems, place)

    whole = lambda shape: pl.BlockSpec(shape, lambda b, n: (0,) * len(shape))
    res = pl.pallas_call(
        body, name=name, grid=(B, nb),
        in_specs=_swa_in_specs(nb) + [whole((SWA_HEADS, Q, 2 * Q)), whole((SWA_HEADS, Q, 1))] + [HBM_SPEC] * n_c,
        out_specs=[pl.BlockSpec((Q, SWA_HEADS * HEAD_DIM), lambda b, n: (b * nb + n, 0)),
                   pl.BlockSpec((Q, LANES), lambda b, n: (b * nb + n, 0))] + [HBM_SPEC] * n_c,
        out_shape=[jax.ShapeDtypeStruct((B * S, SWA_HEADS * HEAD_DIM), BF16),
                   jax.ShapeDtypeStruct((B * S, LANES), F32)] + _comm_out_shapes(comm),
        scratch_shapes=_comm_scratch(comm) if n_c else [],
        compiler_params=_cparams(*(("arbitrary",) * 2 if n_c else ("parallel",) * 2)),
    )(h, h, h, h, h, bias, sinkcol, *[a for _, a in comm])
    return res[0], res[1], list(res[2:])


def _swa_bwd(h, o, do, lse, bias, sinkcol, *, do_blk0, B, S, name):
    Q = BLOCK_Q
    nb = S // Q
    scale = HEAD_DIM ** -0.5
    group = SWA_HEADS // SWA_KV_HEADS

    def body(q_ref, kp_ref, ko_ref, vp_ref, vo_ref, o_ref, do_ref, lse_ref, bias_ref, sink_ref,
             dq_ref, dko_ref, dkp_ref, dvo_ref, dvp_ref, dbias_ref, dsink_ref):
        @pl.when((pl.program_id(0) == 0) & (pl.program_id(1) == 0))
        def _():
            dbias_ref[...] = jnp.zeros_like(dbias_ref)
            dsink_ref[...] = jnp.zeros_like(dsink_ref)

        lo, lo2, kdup, vdup, valid, qm = _swa_common(q_ref, kp_ref, ko_ref, vp_ref, vo_ref, pl.program_id(1))
        lse_blk = lse_ref[...]
        dkk = [jnp.zeros((2 * Q, LANES), F32) for _ in range(SWA_KV_HEADS)]
        dvv = [jnp.zeros((2 * Q, LANES), F32) for _ in range(SWA_KV_HEADS)]
        dq_pairs = []
        for pr in range(SWA_HEADS // 2):
            cols = slice(pr * LANES, (pr + 1) * LANES)
            do_p, o_p = do_ref[:, cols], o_ref[:, cols]
            dq_half = []
            for half in range(2):
                a = 2 * pr + half
                kvh = a // group
                keep = lo if half == 0 else jnp.logical_not(lo)
                s = lax.dot_general(qm[a], kdup[kvh], NT, preferred_element_type=F32) * scale + bias_ref[a]
                s = jnp.where(valid, s, NEG_INF)
                lse_a = _head_column(lse_blk, a)
                p = jnp.exp(s - lse_a)
                doh = jnp.where(keep, do_p, jnp.zeros_like(do_p))
                delta = jnp.sum(doh.astype(F32) * o_p.astype(F32), -1, keepdims=True)
                dp = lax.dot_general(doh, vdup[kvh], NT, preferred_element_type=F32)
                ds = p * (dp - delta)
                dbias_ref[a] += ds
                dsink_ref[a] -= jnp.exp(sink_ref[a] - lse_a) * delta
                dss = (ds * scale).astype(BF16)
                dq_half.append(lax.dot_general(dss, kdup[kvh], NN, preferred_element_type=F32))
                dkk[kvh] = dkk[kvh] + lax.dot_general(dss, qm[a], TN, preferred_element_type=F32)
                dvv[kvh] = dvv[kvh] + lax.dot_general(p.astype(BF16), doh, TN, preferred_element_type=F32)
            dq_pairs.append(jnp.where(lo, dq_half[0], dq_half[1]))
        dq_ref[...] = jnp.concatenate(dq_pairs, axis=1).astype(BF16)
        fold = lambda x: x + pltpu.roll(x, HEAD_DIM, 1)
        dk_blk = jnp.where(lo2, fold(dkk[0]), fold(dkk[1]))
        dv_blk = jnp.where(lo2, fold(dvv[0]), fold(dvv[1]))
        dkp_ref[...] = dk_blk[:Q]
        dko_ref[...] = dk_blk[Q:]
        dvp_ref[...] = dv_blk[:Q]
        dvo_ref[...] = dv_blk[Q:]

    whole = lambda shape: pl.BlockSpec(shape, lambda b, n: (0,) * len(shape))
    wide = lambda blk: pl.BlockSpec((Q, SWA_HEADS * HEAD_DIM), lambda b, n: (b * nb + n, blk))
    narrow = pl.BlockSpec((Q, LANES), lambda b, n: (b * nb + n, 0))
    kv_shape = jax.ShapeDtypeStruct((B * S, LANES), F32)
    return pl.pallas_call(
        body, name=name, grid=(B, nb),
        in_specs=_swa_in_specs(nb) + [wide(0), wide(do_blk0), narrow, whole((SWA_HEADS, Q, 2 * Q)),
                                      whole((SWA_HEADS, Q, 1))],
        out_specs=[wide(0), narrow, narrow, narrow, narrow, whole((SWA_HEADS, Q, 2 * Q)), whole((SWA_HEADS, Q, 1))],
        out_shape=[jax.ShapeDtypeStruct((B * S, SWA_HEADS * HEAD_DIM), BF16), kv_shape, kv_shape, kv_shape, kv_shape,
                   jax.ShapeDtypeStruct((SWA_HEADS, Q, 2 * Q), F32), jax.ShapeDtypeStruct((SWA_HEADS, Q, 1), F32)],
        compiler_params=_cparams("arbitrary", "arbitrary"),
    )(h, h, h, h, h, o, do, lse, bias, sinkcol)


def _bias_bucket_sum(dbias, bucket, *, name):
    def body(d_ref, b_ref, o_ref):
        dbv, bk = d_ref[...], b_ref[...]
        lane = lax.broadcasted_iota(jnp.int32, (SWA_HEADS, LANES), 1)
        out = jnp.zeros((SWA_HEADS, LANES), F32)
        for b in range(REL_BUCKETS):
            part = jnp.sum(jnp.where(bk == b, dbv, 0.0), axis=1)
            tot = jnp.sum(part, axis=-1, keepdims=True)
            out = out + jnp.where(lane == b, tot, 0.0)
        o_ref[...] = out

    return pl.pallas_call(
        body, name=name, out_shape=jax.ShapeDtypeStruct((SWA_HEADS, LANES), F32),
        compiler_params=pltpu.CompilerParams(vmem_limit_bytes=VMEM_LIMIT_BYTES),
    )(dbias, bucket)


def _adamw_update(w, g, m, v):
    m_new = ADAM_B1 * m + (1.0 - ADAM_B1) * g
    v_new = ADAM_B2 * v + (1.0 - ADAM_B2) * jnp.square(g)
    m_hat = m_new / (1.0 - ADAM_B1 ** ADAM_STEP)
    v_hat = v_new / (1.0 - ADAM_B2 ** ADAM_STEP)
    return -ADAM_LR * (m_hat / (jnp.sqrt(v_hat) + ADAM_EPS) + ADAM_WD * w), m_new, v_new


def _adamw(w, g, m, v, *, name):
    def body(w_ref, g_ref, m_ref, v_ref, d_ref, nm_ref, nv_ref):
        d_ref[...], nm_ref[...], nv_ref[...] = _adamw_update(w_ref[...], g_ref[...], m_ref[...], v_ref[...])

    return pl.pallas_call(
        body, name=name, out_shape=[jax.ShapeDtypeStruct(w.shape, F32)] * 3,
        compiler_params=pltpu.CompilerParams(vmem_limit_bytes=VMEM_LIMIT_BYTES),
    )(w, g, m, v)


def _adamw_slots(w, parts, m, v, *, name):
    R, C = w.shape
    tr = R if R <= 512 else _pick(R, (256, 128))

    def body(w_ref, p_ref, m_ref, v_ref, g_ref, d_ref, nm_ref, nv_ref):
        g = p_ref[0].astype(F32)
        for j in range(1, N_DEV):
            g = g + p_ref[j].astype(F32)
        g_ref[...] = g
        d_ref[...], nm_ref[...], nv_ref[...] = _adamw_update(w_ref[...], g, m_ref[...], v_ref[...])

    spec = pl.BlockSpec((tr, C), lambda i: (i, 0))
    return pl.pallas_call(
        body, name=name, grid=(R // tr,),
        in_specs=[spec, pl.BlockSpec((N_DEV, tr, C), lambda i: (0, i, 0)), spec, spec], out_specs=[spec] * 4,
        out_shape=[jax.ShapeDtypeStruct((R, C), F32)] * 4, compiler_params=_cparams("parallel"),
    )(w, parts, m, v)


def _all_gather_hbm(blocks, *, name):
    n = len(blocks)

    def body(*refs):
        x_refs, out_refs = refs[:n], refs[n:2 * n]
        send_sems, recv_sems, local_sems = refs[2 * n:]
        x, y, c, _ = _mesh_place()
        me, sibling = (x, y, c), (x, y, 1 - c)
        chips = [(1 - x, y), (x, 1 - y), (1 - x, 1 - y)]

        def copy(w, k, blk, to, src=None):
            px, py, pc = blk
            slot = out_refs[w].at[4 * px + 2 * py + pc]
            return pltpu.make_async_remote_copy(
                src_ref=slot if src is None else src, dst_ref=slot,
                send_sem=send_sems.at[w, k], recv_sem=recv_sems.at[w, k], device_id=to, device_id_type=MESH_ID)

        mine = [pltpu.make_async_copy(x_refs[w], out_refs[w].at[4 * x + 2 * y + c], local_sems.at[w])
                for w in range(n)]
        for cp in mine:
            cp.start()
        first = []
        for w in range(n):
            first.append(copy(w, 0, me, sibling, src=x_refs[w]))
            first += [copy(w, 1 + j, me, (*chip, c), src=x_refs[w]) for j, chip in enumerate(chips)]
        for cp in first:
            cp.start()
        passed = []
        for j, chip in enumerate(chips):
            for w in range(n):
                copy(w, 1 + j, (*chip, c), me).wait_recv()
                fwd = copy(w, 4 + j, (*chip, c), sibling)
                fwd.start()
                passed.append(fwd)
        for w in range(n):
            copy(w, 0, sibling, me).wait_recv()
            for j, chip in enumerate(chips):
                copy(w, 4 + j, (*chip, 1 - c), me).wait_recv()
        for cp in first + passed:
            cp.wait_send()
        for cp in mine:
            cp.wait()

    return pl.pallas_call(
        body, name=name, out_shape=[jax.ShapeDtypeStruct((N_DEV,) + b.shape, b.dtype) for b in blocks],
        in_specs=[HBM_SPEC] * n, out_specs=[HBM_SPEC] * n,
        scratch_shapes=[pltpu.SemaphoreType.DMA((n, 7)), pltpu.SemaphoreType.DMA((n, 7)),
                        pltpu.SemaphoreType.DMA((n,))],
    )(*blocks)


def _all_reduce_small(block, *, name):
    R, W = block.shape

    def body(x_ref, out_ref, buf, send_sems, recv_sems):
        x, y, c, me = _mesh_place()
        copies = []
        for k, (peer, _) in enumerate(_peers(x, y, c)):
            copies.append(pltpu.make_async_remote_copy(
                src_ref=x_ref, dst_ref=buf.at[me], send_sem=send_sems.at[k], recv_sem=recv_sems.at[k],
                device_id=peer, device_id_type=MESH_ID))
        for cp in copies:
            cp.start()
        buf[me] = x_ref[...]
        for cp in copies:
            cp.wait_recv()
        for cp in copies:
            cp.wait_send()
        acc = buf[0]
        for j in range(1, N_DEV):
            acc = acc + buf[j]
        out_ref[...] = acc

    return pl.pallas_call(
        body, name=name, out_shape=jax.ShapeDtypeStruct((R, W), F32),
        in_specs=[VMEM_SPEC], out_specs=VMEM_SPEC,
        scratch_shapes=[pltpu.VMEM((N_DEV, R, W), F32), pltpu.SemaphoreType.DMA((7,)), pltpu.SemaphoreType.DMA((7,))],
    )(block)


def _assemble(name, g):
    if BIG_AXIS[name] == 2:
        return jnp.concatenate([g[j] for j in range(N_DEV)], axis=1)
    return g.reshape(N_DEV * g.shape[1], g.shape[2])


def _split_for_devices(name, g):
    if BIG_AXIS[name] == 2:
        b = g.shape[1] // N_DEV
        return jnp.stack([g[:, j * b:(j + 1) * b] for j in range(N_DEV)]).astype(BF16)
    return g.reshape(N_DEV, g.shape[0] // N_DEV, g.shape[1]).astype(BF16)


def _layer_weight_keys(i):
    j = i // 2
    mixer = [('ev_w_in', j), ('ev_w_uq', j), ('ev_w_ukv', j), ('ev_w_out', j)] if i % 2 == 0 \
        else [('od_w_in', j), ('od_w_out', j)]
    return mixer + [('w_up', i), ('w_down', i), ('ple_w_proj', i), ('ple_w_gate', i)]


class _MeshExchange:
    def __init__(self, shards):
        self.shards = shards
        self.weights = {}
        self.pending = []
        self.in_flight = []
        self.received = {}

    def layer_weights(self, i):
        keys = _layer_weight_keys(i)
        if i == 0:
            got = _all_gather_hbm([self.shards[k] for k in keys], name="gather_l0")
            self.weights[0] = {k[0]: _assemble(k[0], g) for k, g in zip(keys, got)}
        return self.weights[i]

    def _fwd_keys(self, i, part):
        if i + 1 >= DEPTH:
            return []
        keys = _layer_weight_keys(i + 1)
        if i % 2 == 1:
            return keys if part == 0 else []
        return keys[:2] if part == 0 else keys[2:]

    def fwd_items(self, i, part):
        return [("gather", self.shards[k]) for k in self._fwd_keys(i, part)]

    def fwd_done(self, i, part, outs):
        got = {k[0]: _assemble(k[0], g) for k, g in zip(self._fwd_keys(i, part), outs)}
        self.weights.setdefault(i + 1, {}).update(got)

    def push_grads(self, grads):
        self.pending += [(k, _split_for_devices(k[0], g)) for k, g in grads.items()]

    def bwd_items(self):
        self.in_flight, self.pending = self.pending, []
        return [("scatter", parts) for _, parts in self.in_flight]

    def bwd_done(self, outs):
        for (k, _), got in zip(self.in_flight, outs):
            self.received[k] = got
        self.in_flight = []

    def finish(self):
        if self.pending:
            outs = _exchange(self.bwd_items(), name="exchange_rest")
            self.bwd_done(outs)
        return self.received


PACK_ROWS = 8


def _pack_small(vals):
    flat = jnp.concatenate([vals[n].reshape(-1).astype(F32) for n in SMALL])
    pad = (-flat.shape[0]) % (PACK_ROWS * LANES)
    return jnp.pad(flat, (0, pad)).reshape(-1, LANES)


def _unpack_small(block, shapes):
    flat = block.reshape(-1)
    out, off = {}, 0
    for n in SMALL:
        sz = math.prod(shapes[n])
        out[n] = flat[off:off + sz].reshape(shapes[n])
        off += sz
    return out


def _rope_tables(S):
    half = MLA_ROPE // 2
    inv = 1.0 / (ROPE_THETA ** (jnp.arange(0, MLA_ROPE, 2, dtype=F32) / MLA_ROPE))
    ang = jnp.arange(S, dtype=F32)[:, None] * inv[None, :]
    cos, sin = jnp.cos(ang), jnp.sin(ang)
    zeros = jnp.zeros((S, half), F32)
    tail = jnp.zeros((S, LANES - MLA_QK), F32)

    def block(rope_part, nope_val):
        return jnp.concatenate([jnp.full((S, MLA_NOPE), nope_val, F32), rope_part, tail], -1)

    a_r = jnp.concatenate([cos, cos], -1)
    bm_r = jnp.concatenate([-sin, zeros], -1)
    bp_r = jnp.concatenate([zeros, sin], -1)
    q_tabs = tuple(block(r, v) for r, v in ((a_r, 1.0), (bm_r, 0.0), (bp_r, 0.0)))
    k_tabs = tuple(block(r, 0.0) for r in (a_r, bm_r, bp_r))
    return q_tabs, k_tabs


def _t5_bucket(dist):
    exact = REL_BUCKETS // 2
    d = jnp.maximum(dist, 1).astype(F32)
    large = exact + (jnp.log(d / exact) / math.log(REL_MAX_DIST / exact) * (REL_BUCKETS - exact)).astype(jnp.int32)
    large = jnp.minimum(large, REL_BUCKETS - 1)
    return jnp.where(dist < exact, dist, large)


def _swa_bucket_table():
    a = jnp.arange(BLOCK_Q)[:, None]
    col = jnp.arange(2 * BLOCK_Q)[None, :]
    return _t5_bucket(jnp.maximum(a + BLOCK_Q - col, 0)).astype(jnp.int32)


def _even_weights(W):
    w = W['ev_w_in']
    c_kv1 = MLA_Q_LORA + MLA_KV_LORA
    c_kr1 = c_kv1 + MLA_ROPE
    c_qs1 = c_kr1 + SWA_HEADS * HEAD_DIM
    zeros = lambda n: jnp.zeros((D_MODEL, n), w.dtype)
    w_in = jnp.concatenate([w[:, c_kr1:c_qs1], w[:, :c_kv1], w[:, c_qs1:], zeros(KR_LANE0), w[:, c_kv1:c_kr1],
                            zeros(LANES - KR_LANE0 - MLA_ROPE)], axis=1)
    uq = W['ev_w_uq'].reshape(MLA_Q_LORA, MLA_HEADS, MLA_QK)
    w_uq = jnp.pad(uq, ((0, 0), (0, 0), (0, LANES - MLA_QK))).reshape(MLA_Q_LORA, MLA_HEADS * LANES)
    ukv = W['ev_w_ukv'].reshape(MLA_KV_LORA, MLA_HEADS, MLA_NOPE + MLA_V)
    w_k = jnp.pad(ukv[..., :MLA_NOPE], ((0, 0), (0, 0), (0, LANES - MLA_NOPE))).reshape(MLA_KV_LORA, -1)
    w_v = ukv[..., MLA_NOPE:].reshape(MLA_KV_LORA, MLA_HEADS * MLA_V)
    return w_in, w_uq, w_k, w_v, W['ev_w_out']


def _even_in_grad_unpad(dw):
    kr0 = EV_KR[0] + KR_LANE0
    return jnp.concatenate([dw[:, EV_CQ[0]:EV_CKV[1]], dw[:, kr0:kr0 + MLA_ROPE], dw[:, EV_QS[0]:EV_QS[1]],
                            dw[:, EV_KS[0]:EV_VS[1]]], axis=1)


def _even_fwd(xb, W, P, i, B, S, tabs, xchg, tag):
    j = i // 2
    q_tabs, k_tabs, bias, sinkcol = tabs
    w_in, w_uq, w_k, w_v, w_out = _even_weights(W)
    h = _mm(xb, w_in, name=f"{tag}_in")
    cqn, ckvn, rq, rkv = _even_norms(h, P['ev_q_norm'][j][None], P['ev_kv_norm'][j][None], name=f"{tag}_norms")
    q = _rope(_mm(cqn, w_uq, name=f"{tag}_uq"), q_tabs, S, sign=1.0, name=f"{tag}_ropeq")
    knp = _mm(ckvn, w_k, out_dtypes=(BF16,), name=f"{tag}_uk")
    v = _mm(ckvn, w_v, out_dtypes=(BF16,), name=f"{tag}_uv")
    k = _mla_keys(knp, h, k_tabs, S, name=f"{tag}_keys")
    o_mla, lse_mla, got = _flash_fwd(q, k, v, q_blk0=0, k_blk0=0, v_blk0=0, W=2 * LANES, n_pairs=MLA_HEADS // 2,
                                     B=B, S=S, scale=MLA_QK ** -0.5, comm=xchg.fwd_items(i, 0), name=f"{tag}_mla")
    xchg.fwd_done(i, 0, got)
    o_swa, lse_swa, got = _swa_fwd(h, bias, sinkcol, B=B, S=S, comm=xchg.fwd_items(i, 1), name=f"{tag}_swa")
    xchg.fwd_done(i, 1, got)
    o_cat = jnp.concatenate([o_mla, o_swa], axis=-1)
    m = _mm(o_cat, w_out, name=f"{tag}_out")
    res = dict(h=h, cqn=cqn, ckvn=ckvn, rq=rq, rkv=rkv, q=q, k=k, v=v, o_mla=o_mla, lse_mla=lse_mla,
               o_swa=o_swa, lse_swa=lse_swa, o_cat=o_cat)
    return m, res


def _shift_prev(own, prev, B, S):
    prev = prev.reshape(B, S, LANES)
    shifted = jnp.concatenate([prev[:, BLOCK_Q:], jnp.zeros_like(prev[:, :BLOCK_Q])], axis=1)
    return (own + shifted.reshape(B * S, LANES)).astype(BF16)


def _even_bwd(dmb, dz1, xb, W, P, j, B, S, tabs, res, xchg, tag):
    q_tabs, k_tabs, bias, sinkcol = tabs
    w_in, w_uq, w_k, w_v, w_out = _even_weights(W)
    g = {}
    g['ev_w_out'] = _mm_tn(res['o_cat'], dmb, name=f"{tag}_dwout")
    do = _mm(dmb, w_out, trans_b=True, out_dtypes=(BF16,), name=f"{tag}_do")
    dq, dk, dv, got = _flash_bwd(res['q'], res['k'], res['v'], res['o_mla'], do, res['lse_mla'], q_blk0=0, k_blk0=0,
                                 v_blk0=0, do_blk0=0, W=2 * LANES, n_pairs=MLA_HEADS // 2, B=B, S=S,
                                 scale=MLA_QK ** -0.5, qk_dtype=F32, comm=xchg.bwd_items(), name=f"{tag}_mla_bwd")
    xchg.bwd_done(got)
    dq_pre = _rope(dq, q_tabs, S, sign=-1.0, name=f"{tag}_ropeq_bwd")
    dw_uq = _mm_tn(res['cqn'], dq_pre, name=f"{tag}_dwuq")
    g['ev_w_uq'] = dw_uq.reshape(MLA_Q_LORA, MLA_HEADS, LANES)[..., :MLA_QK].reshape(MLA_Q_LORA, MLA_HEADS * MLA_QK)
    dcqn = _mm(dq_pre, w_uq, trans_b=True, name=f"{tag}_dcqn")
    dw_k = _mm_tn(res['ckvn'], dk, name=f"{tag}_dwuk").reshape(MLA_KV_LORA, MLA_HEADS, LANES)[..., :MLA_NOPE]
    dw_v = _mm_tn(res['ckvn'], dv, name=f"{tag}_dwuv").reshape(MLA_KV_LORA, MLA_HEADS, MLA_V)
    g['ev_w_ukv'] = jnp.concatenate([dw_k, dw_v], axis=-1).reshape(MLA_KV_LORA, MLA_HEADS * (MLA_NOPE + MLA_V))
    dckvn_v = _mm(dv, w_v, trans_b=True, name=f"{tag}_dckvn_v")
    dckvn = _mm(dk, w_k, trans_b=True, extras=(dckvn_v,), epilogue=lambda acc, r: (acc + r,), name=f"{tag}_dckvn")
    dkr_pre = _mla_rope_key_grad(dk, k_tabs, S, name=f"{tag}_ropek_bwd")
    dqs, dko, dkp, dvo, dvp, dbias, dsink = _swa_bwd(res['h'], res['o_swa'], do, res['lse_swa'], bias, sinkcol,
                                                     do_blk0=1, B=B, S=S, name=f"{tag}_swa_bwd")
    dh, dgq, dgkv = _even_in_bwd(res['h'], res['rq'], res['rkv'], P['ev_q_norm'][j][None], P['ev_kv_norm'][j][None],
                                 dcqn, dckvn, dqs, _shift_prev(dko, dkp, B, S), _shift_prev(dvo, dvp, B, S), dkr_pre,
                                 name=f"{tag}_in_bwd")
    g['ev_w_in'] = _even_in_grad_unpad(_mm_tn(xb, dh, name=f"{tag}_dwin"))
    dx = _mm(dh, w_in, trans_b=True, extras=(dz1,), epilogue=lambda acc, r: (acc + DN_ALPHA * r,), name=f"{tag}_dx")
    small = dict(ev_q_norm=dgq[0], ev_kv_norm=dgkv[0], dbias=dbias, ev_sinks=jnp.sum(dsink, axis=(1, 2)))
    return dx, g, small


def _odd_fwd(xb, W, P, i, B, S, xchg, tag):
    j = i // 2
    w = W['od_w_in']
    w_qkv = w[:, :ODD_QKV]
    w_f = jnp.pad(w[:, ODD_QKV:], ((0, 0), (0, LANES - FOX_HEADS)))
    bf = jnp.pad(P['od_b_f'][j], (0, LANES - FOX_HEADS))[None]
    qkv = _mm(xb, w_qkv, out_dtypes=(BF16,), name=f"{tag}_qkv")
    f = _mm(xb, w_f, name=f"{tag}_f").reshape(B, S, LANES)
    csh, chs = _fox_decay_fwd(f, bf, name=f"{tag}_decay")
    crow = chs[:, :FOX_HEADS].reshape(B, FOX_HEADS, S // ATT_TILE, 1, ATT_TILE)
    n_blk = FOX_HEADS * HEAD_DIM // LANES
    o, lse, got = _flash_fwd(qkv, qkv, qkv, q_blk0=0, k_blk0=n_blk, v_blk0=2 * n_blk, W=LANES,
                             n_pairs=FOX_HEADS // 2, B=B, S=S, scale=HEAD_DIM ** -0.5, csh=csh, crow=crow,
                             comm=xchg.fwd_items(i, 0), name=f"{tag}_fox")
    xchg.fwd_done(i, 0, got)
    m = _mm(o, W['od_w_out'], name=f"{tag}_out")
    res = dict(f=f, bf=bf, csh=csh, crow=crow, qkv=qkv, o=o, lse=lse, w_qkv=w_qkv, w_f=w_f)
    return m, res


def _odd_bwd(dmb, dz1, xb, W, P, j, B, S, res, xchg, tag):
    g = {}
    w_out = W['od_w_out']
    g['od_w_out'] = _mm_tn(res['o'], dmb, name=f"{tag}_dwout")
    do = _mm(dmb, w_out, trans_b=True, out_dtypes=(BF16,), name=f"{tag}_do")
    qkv = res['qkv']
    n_blk = FOX_HEADS * HEAD_DIM // LANES
    dq, dk, dv, dck, dcq, got = _flash_bwd(qkv, qkv, qkv, res['o'], do, res['lse'], q_blk0=0, k_blk0=n_blk,
                                           v_blk0=2 * n_blk, do_blk0=0, W=LANES, n_pairs=FOX_HEADS // 2, B=B, S=S,
                                           scale=HEAD_DIM ** -0.5, qk_dtype=BF16, csh=res['csh'], crow=res['crow'],
                                           comm=xchg.bwd_items(), name=f"{tag}_fox_bwd")
    xchg.bwd_done(got)
    dc = dck.reshape(B, FOX_HEADS, S) + dcq.reshape(B, FOX_HEADS, S)
    dc_hs = jnp.pad(dc, ((0, 0), (0, LANES - FOX_HEADS), (0, 0)))
    df, dbf = _fox_decay_bwd(dc_hs, res['f'], res['bf'], name=f"{tag}_decay_bwd")
    df = df.reshape(B * S, LANES)
    dqkv = jnp.concatenate([dq, dk, dv], axis=-1)
    dw_qkv = _mm_tn(xb, dqkv, name=f"{tag}_dwqkv")
    dw_f = _mm_tn(xb, df, name=f"{tag}_dwf")
    g['od_w_in'] = jnp.concatenate([dw_qkv, dw_f[:, :FOX_HEADS]], axis=1)
    dxf = _mm(df, res['w_f'], trans_b=True, extras=(dz1,), epilogue=lambda acc, r: (acc + DN_ALPHA * r,),
              name=f"{tag}_dxf")
    dx = _mm(dqkv, res['w_qkv'], trans_b=True, extras=(dxf,), epilogue=lambda acc, r: (acc + r,), name=f"{tag}_dx")
    small = dict(od_b_f=dbf[0, :FOX_HEADS])
    return dx, g, small


def _local_step(x, p, target, P, xchg):
    B, S, D = x.shape
    T = B * S
    q_tabs, k_tabs = _rope_tables(S)
    bucket = _swa_bucket_table()
    in_bucket = (bucket[..., None] == jnp.arange(REL_BUCKETS)).astype(F32)
    bias = jnp.einsum('acb,bh->hac', in_bucket, P['rel_bias'], precision=lax.Precision.HIGHEST)

    xc = x.reshape(T, D)
    xcb = xc.astype(BF16)
    saved = []
    for i in range(DEPTH):
        j = i // 2
        tag = f"l{i}"
        W = xchg.layer_weights(i)
        lay = dict(xb=xcb, W=W)
        if i % 2 == 0:
            sinkcol = jnp.broadcast_to(P['ev_sinks'][j][:, None, None], (SWA_HEADS, BLOCK_Q, 1)).astype(F32)
            lay['tabs'] = (q_tabs, k_tabs, bias, sinkcol)
            m, lay['mix'] = _even_fwd(xcb, W, P, i, B, S, lay['tabs'], xchg, tag)
        else:
            m, lay['mix'] = _odd_fwd(xcb, W, P, i, B, S, xchg, tag)
        x1, x1b, lay['xh1'], lay['r1'] = _ln_fwd(xc, m, P['ln1_g'][i][None], P['ln1_b'][i][None], name=f"{tag}_ln1")
        lay['x1b'] = x1b
        lay['u'], lay['a'] = _mm(x1b, W['w_up'], out_dtypes=(F32, BF16),
                                 epilogue=lambda acc: (acc, jnp.square(jnp.maximum(acc, 0.0))), name=f"{tag}_up")
        d = _mm(lay['a'], W['w_down'], name=f"{tag}_down")
        x2, x2b, lay['xh2'], lay['r2'] = _ln_fwd(x1, d, P['ln2_g'][i][None], P['ln2_b'][i][None], name=f"{tag}_ln2")
        lay['x2b'] = x2b
        lay['p'] = p[i].reshape(T, D_PLE)
        lay['e'] = _mm(lay['p'], W['ple_w_proj'], name=f"{tag}_ple_proj")

        def gate(acc, bg, e, x2v):
            gv = 1.0 / (1.0 + jnp.exp(-(acc + bg)))
            y = x2v + gv * e
            return y, y, gv

        xc, xcb, lay['g'] = _mm(x2b, W['ple_w_gate'], extras=(P['ple_b_gate'][i][None], lay['e'], x2),
                                epilogue=gate, out_dtypes=(F32, BF16, F32), name=f"{tag}_ple_gate")
        saved.append(lay)

    dy, sq = _loss_grad(xc, target.reshape(T, D), name="loss")

    Gs = {n: [None] * DEPTH for n in ('ln1_g', 'ln1_b', 'ln2_g', 'ln2_b', 'ple_b_gate')}
    Gs.update({n: [None] * (DEPTH // 2) for n in ('ev_q_norm', 'ev_kv_norm', 'ev_sinks', 'od_b_f')})
    dbias_total = None
    for i in reversed(range(DEPTH)):
        j = i // 2
        tag = f"l{i}b"
        lay = saved[i]
        W = lay['W']
        de, dzg, dbg = _ple_bwd_elem(dy, lay['g'], lay['e'], name=f"{tag}_ple_elem")
        Gs['ple_b_gate'][i] = dbg[0]
        g_mlp = {('ple_w_proj', i): _mm_tn(lay['p'], de, name=f"{tag}_dwproj"),
                 ('ple_w_gate', i): _mm_tn(lay['x2b'], dzg, name=f"{tag}_dwgate")}
        dx2 = _mm(dzg, W['ple_w_gate'], trans_b=True, extras=(dy,), epilogue=lambda acc, r: (acc + r,),
                  name=f"{tag}_dx2")
        dz2, dz2b, dg2, db2 = _ln_bwd(dx2, lay['xh2'], lay['r2'], P['ln2_g'][i][None], name=f"{tag}_ln2")
        Gs['ln2_g'][i], Gs['ln2_b'][i] = dg2[0], db2[0]
        g_mlp[('w_down', i)] = _mm_tn(lay['a'], dz2b, name=f"{tag}_dwdown")
        du = _mm(dz2b, W['w_down'], trans_b=True, extras=(lay['u'],), out_dtypes=(BF16,),
                 epilogue=lambda acc, u: (acc * (2.0 * jnp.maximum(u, 0.0)),), name=f"{tag}_du")
        g_mlp[('w_up', i)] = _mm_tn(lay['x1b'], du, name=f"{tag}_dwup")
        xchg.push_grads(g_mlp)
        dx1 = _mm(du, W['w_up'], trans_b=True, extras=(dz2,), epilogue=lambda acc, r: (acc + DN_ALPHA * r,),
                  name=f"{tag}_dx1")
        dz1, dz1b, dg1, db1 = _ln_bwd(dx1, lay['xh1'], lay['r1'], P['ln1_g'][i][None], name=f"{tag}_ln1")
        Gs['ln1_g'][i], Gs['ln1_b'][i] = dg1[0], db1[0]
        if i % 2 == 0:
            dy, g, small = _even_bwd(dz1b, dz1, lay['xb'], W, P, j, B, S, lay['tabs'], lay['mix'], xchg, tag)
            dbias_total = small['dbias'] if dbias_total is None else dbias_total + small['dbias']
            for n in ('ev_q_norm', 'ev_kv_norm', 'ev_sinks'):
                Gs[n][j] = small[n]
        else:
            dy, g, small = _odd_bwd(dz1b, dz1, lay['xb'], W, P, j, B, S, lay['mix'], xchg, tag)
            Gs['od_b_f'][j] = small['od_b_f']
        xchg.push_grads({(n, j): val for n, val in g.items()})

    grads_small = {n: jnp.stack(v) for n, v in Gs.items()}
    drel = _bias_bucket_sum(dbias_total, bucket, name="rel_bias_grad")
    grads_small['rel_bias'] = drel[:, :REL_BUCKETS].T
    return sq, dy.reshape(B, S, D), grads_small


def kernel(x, p, rel_bias, ev_w_in, ev_q_norm, ev_w_uq, ev_kv_norm, ev_w_ukv, ev_sinks, ev_w_out, od_w_in, od_b_f, od_w_out, ln1_g, ln1_b, w_up, w_down, ln2_g, ln2_b, ple_w_proj, ple_w_gate, ple_b_gate, loss_target, m_rel_bias, m_ev_w_in, m_ev_q_norm, m_ev_w_uq, m_ev_kv_norm, m_ev_w_ukv, m_ev_sinks, m_ev_w_out, m_od_w_in, m_od_b_f, m_od_w_out, m_ln1_g, m_ln1_b, m_w_up, m_w_down, m_ln2_g, m_ln2_b, m_ple_w_proj, m_ple_w_gate, m_ple_b_gate, v_rel_bias, v_ev_w_in, v_ev_q_norm, v_ev_w_uq, v_ev_kv_norm, v_ev_w_ukv, v_ev_sinks, v_ev_w_out, v_od_w_in, v_od_b_f, v_od_w_out, v_ln1_g, v_ln1_b, v_w_up, v_w_down, v_ln2_g, v_ln2_b, v_ple_w_proj, v_ple_w_gate, v_ple_b_gate):
    given = dict(locals())
    w = {n: given[n] for n in WEIGHTS}
    mom = {n: given["m_" + n] for n in WEIGHTS}
    var = {n: given["v_" + n] for n in WEIGHTS}
    small_shapes = {n: w[n].shape for n in SMALL}

    shards = {(n, idx): w[n][idx].astype(BF16) for n in BIG for idx in range(w[n].shape[0])}
    xchg = _MeshExchange(shards)
    P = {n: w[n] for n in SMALL}

    sq, grad_x, grads_small = _local_step(x, p, loss_target, P, xchg)
    loss = lax.psum(0.5 * jnp.sum(sq) / D_MODEL, ("x", "y", "c"))

    received = xchg.finish()
    g_small_packed = _all_reduce_small(_pack_small(grads_small), name="reduce_small_grads")
    g_small = _unpack_small(g_small_packed, small_shapes)

    grad, delta, new_m, new_v = {}, {}, {}, {}
    for n in BIG:
        per_layer = [_adamw_slots(w[n][idx], received[(n, idx)], mom[n][idx], var[n][idx], name=f"adamw_{n}{idx}")
                     for idx in range(w[n].shape[0])]
        grad[n], delta[n], new_m[n], new_v[n] = (jnp.stack(t) for t in zip(*per_layer))
    d, nm, nv = _adamw(_pack_small(w), g_small_packed, _pack_small(mom), _pack_small(var), name="adamw_small")
    d, nm, nv = (_unpack_small(t, small_shapes) for t in (d, nm, nv))
    for n in SMALL:
        grad[n], delta[n], new_m[n], new_v[n] = g_small[n], d[n], nm[n], nv[n]

    return (loss, grad_x, *[grad[n] for n in WEIGHTS], *[delta[n] for n in WEIGHTS],
            *[new_m[n] for n in WEIGHTS], *[new_v[n] for n in WEIGHTS])
```

```python
import math

import jax
import jax.numpy as jnp
from jax import lax
from jax.experimental import pallas as pl
from jax.experimental.pallas import tpu as pltpu

F32, BF16 = jnp.float32, jnp.bfloat16

D_MODEL = 1024
DEPTH = 4
HEAD_DIM = 64
MLA_HEADS, MLA_NOPE, MLA_ROPE, MLA_V = 8, 64, 32, 64
MLA_Q_LORA, MLA_KV_LORA = 384, 256
MLA_QK = MLA_NOPE + MLA_ROPE
ROPE_THETA = 10000.0
SWA_HEADS, SWA_KV_HEADS, SWA_WINDOW = 8, 2, 128
SWA_GROUP = SWA_HEADS // SWA_KV_HEADS
REL_BUCKETS, REL_MAX_DIST = 32, 128
FOX_HEADS = 16
D_FF = 4 * D_MODEL
D_PLE = 256
BLOCK_Q = 128
DN_ALPHA = (2 * DEPTH) ** 0.25
NORM_EPS = 1e-5
NEG_INF = -1e30
EVEN_IN = 1440
ODD_QKV = 3 * FOX_HEADS * HEAD_DIM
LANES = 128

EV_QS = (0, 512)
EV_CQ = (512, 896)
EV_CKV = (896, 1152)
EV_KS = (1152, 1280)
EV_VS = (1280, 1408)
EV_KR = (1408, 1536)
EVEN_IN_PAD = 1536
KR_LANE0 = MLA_NOPE

ADAM_LR, ADAM_B1, ADAM_B2, ADAM_EPS, ADAM_WD, ADAM_STEP = 0.001, 0.9, 0.999, 1e-08, 0.01, 10

N_DEV = 8
VMEM_LIMIT_BYTES = 48 * 1024 * 1024
ATT_TILE = 512
ATT_TILE_BWD = 512
PAIRS_PER_STEP_FWD = 4
PAIRS_PER_STEP_BWD = 2

NN = (((1,), (0,)), ((), ()))
NT = (((1,), (1,)), ((), ()))
TN = (((0,), (0,)), ((), ()))

BIG = ['ev_w_in', 'ev_w_uq', 'ev_w_ukv', 'ev_w_out', 'od_w_in', 'od_w_out', 'w_up', 'w_down',
       'ple_w_proj', 'ple_w_gate']
BIG_AXIS = {'ev_w_in': 2, 'ev_w_uq': 2, 'ev_w_ukv': 2, 'ev_w_out': 1, 'od_w_in': 2, 'od_w_out': 1,
            'w_up': 2, 'w_down': 1, 'ple_w_proj': 2, 'ple_w_gate': 1}
SMALL = ['rel_bias', 'ev_q_norm', 'ev_kv_norm', 'ev_sinks', 'od_b_f', 'ln1_g', 'ln1_b', 'ln2_g', 'ln2_b',
         'ple_b_gate']
WEIGHTS = ['rel_bias', 'ev_w_in', 'ev_q_norm', 'ev_w_uq', 'ev_kv_norm', 'ev_w_ukv', 'ev_sinks', 'ev_w_out',
           'od_w_in', 'od_b_f', 'od_w_out', 'ln1_g', 'ln1_b', 'w_up', 'w_down', 'ln2_g', 'ln2_b',
           'ple_w_proj', 'ple_w_gate', 'ple_b_gate']


def _cparams(*sem):
    return pltpu.CompilerParams(dimension_semantics=sem, vmem_limit_bytes=VMEM_LIMIT_BYTES)


def _pick(n, cands):
    for c in cands:
        if n % c == 0:
            return c
    return n


MM_STEP_BYTES = 10 * 1024 * 1024
MM_OUT_BYTES = 8 * 1024 * 1024
MM_CHUNK = 512


def _mm(a, b, *, trans_b=False, extras=(), epilogue=None, row_epilogue=None, out_dtypes=(F32,), out_widths=None,
        name):
    M, K = a.shape
    N = b.shape[0] if trans_b else b.shape[1]
    n_ex, n_out = len(extras), len(out_dtypes)
    out_widths = (N,) * n_out if out_widths is None else out_widths
    row_bytes = K * a.dtype.itemsize + (sum(w * jnp.dtype(d).itemsize for w, d in zip(out_widths, out_dtypes))
                                        + N * sum(e.dtype.itemsize for e in extras if e.shape[0] == M)
                                        + (4 * N if row_epilogue is not None else 0))
    tm = next((c for c in (1024, 512, 256) if M % c == 0 and c * row_bytes <= MM_STEP_BYTES), 128)
    nc = _pick(N, (MM_CHUNK, 384, 256, 128))

    def body(*refs):
        a_ref, b_ref = refs[:2]
        ex = refs[2:2 + n_ex]
        outs = refs[2 + n_ex:2 + n_ex + n_out]
        av = a_ref[...].astype(BF16)
        for n0 in range(0, N, nc):
            cols = slice(n0, n0 + nc)
            bv = (b_ref[cols, :] if trans_b else b_ref[:, cols]).astype(BF16)
            acc = lax.dot_general(av, bv, NT if trans_b else NN, preferred_element_type=F32)
            if row_epilogue is not None:
                refs[-1][:, cols] = acc
                continue
            res = epilogue(acc, *[e[:, cols] for e in ex]) if epilogue is not None else (acc,)
            for o, r in zip(outs, res):
                o[:, cols] = r.astype(o.dtype)
        if row_epilogue is not None:
            for o, r in zip(outs, row_epilogue(refs[-1][...], *[e[...] for e in ex])):
                o[...] = r.astype(o.dtype)

    in_specs = [pl.BlockSpec((tm, K), lambda i: (i, 0)), pl.BlockSpec(b.shape, lambda i: (0, 0))]
    for e in extras:
        if e.shape == (M, N):
            in_specs.append(pl.BlockSpec((tm, N), lambda i: (i, 0)))
        elif e.shape == (1, N):
            in_specs.append(pl.BlockSpec((1, N), lambda i: (0, 0)))
        else:
            raise ValueError(f"extra operand of shape {e.shape} for a ({M}, {N}) result")
    res = pl.pallas_call(
        body, name=name, grid=(M // tm,), in_specs=in_specs,
        out_specs=[pl.BlockSpec((tm, w), lambda i: (i, 0)) for w in out_widths],
        out_shape=[jax.ShapeDtypeStruct((M, w), d) for w, d in zip(out_widths, out_dtypes)],
        scratch_shapes=[pltpu.VMEM((tm, N), F32)] if row_epilogue is not None else [],
        compiler_params=_cparams("parallel"),
    )(a, b, *extras)
    return res[0] if n_out == 1 else tuple(res)


def _mm_tn(a, b, *, name):
    T, K = a.shape
    N = b.shape[1]
    bk, bn = K, N
    while bk * bn * 4 > MM_OUT_BYTES:
        if bn >= bk and bn % (2 * LANES) == 0:
            bn //= 2
        else:
            bk //= 2
    tt = _pick(T, (1024, 512, 256))
    ck, cn = _pick(bk, (MM_CHUNK, 384, 256, 128)), _pick(bn, (MM_CHUNK, 384, 256, 128))

    def body(a_ref, b_ref, o_ref):
        t = pl.program_id(2)

        @pl.when(t == 0)
        def _():
            o_ref[...] = jnp.zeros_like(o_ref)

        for r0 in range(0, bk, ck):
            av = a_ref[:, r0:r0 + ck].astype(BF16)
            for c0 in range(0, bn, cn):
                o_ref[r0:r0 + ck, c0:c0 + cn] += lax.dot_general(
                    av, b_ref[:, c0:c0 + cn].astype(BF16), TN, preferred_element_type=F32)

    return pl.pallas_call(
        body, name=name, grid=(K // bk, N // bn, T // tt),
        in_specs=[pl.BlockSpec((tt, bk), lambda i, j, t: (t, i)), pl.BlockSpec((tt, bn), lambda i, j, t: (t, j))],
        out_specs=pl.BlockSpec((bk, bn), lambda i, j, t: (i, j)),
        out_shape=jax.ShapeDtypeStruct((K, N), F32),
        compiler_params=_cparams("parallel", "parallel", "arbitrary"),
    )(a, b)


ROW_TILE = 256


def _row_spec(cols, col_block=0):
    return pl.BlockSpec((ROW_TILE, cols), lambda i: (i, col_block))


def _tab_spec(cols, period):
    return pl.BlockSpec((ROW_TILE, cols), lambda i: (i % period, 0))


def _full_spec(shape):
    return pl.BlockSpec(shape, lambda i: (0,) * len(shape))


def _mm_ln(a, w, x, g, b, *, name):
    def ln_rows(m, xv, gv, bv):
        z = DN_ALPHA * xv + m
        mu = jnp.mean(z, -1, keepdims=True)
        zc = z - mu
        r = lax.rsqrt(jnp.mean(zc * zc, -1, keepdims=True) + NORM_EPS)
        xh = zc * r
        y = xh * gv + bv
        return y, y, xh, jnp.broadcast_to(r, (r.shape[0], LANES))

    D = w.shape[1]
    return _mm(a, w, extras=(x, g, b), row_epilogue=ln_rows, out_dtypes=(F32, BF16, F32, F32),
               out_widths=(D, D, D, LANES), name=name)


def _ln_bwd(dy, xh, r, g, *, name):
    T, D = dy.shape

    def body(dy_ref, xh_ref, r_ref, g_ref, dz_ref, dzb_ref, dg_ref, db_ref):
        dyv, xhv = dy_ref[...], xh_ref[...]
        dyg = dyv * g_ref[...]
        c1 = jnp.mean(dyg, -1, keepdims=True)
        c2 = jnp.mean(dyg * xhv, -1, keepdims=True)
        dz = _widen(r_ref[...], D) * (dyg - c1 - xhv * c2)
        dz_ref[...] = dz
        dzb_ref[...] = dz.astype(BF16)

        @pl.when(pl.program_id(0) == 0)
        def _():
            dg_ref[...] = jnp.zeros_like(dg_ref)
            db_ref[...] = jnp.zeros_like(db_ref)

        dg_ref[...] += jnp.sum(dyv * xhv, 0, keepdims=True)
        db_ref[...] += jnp.sum(dyv, 0, keepdims=True)

    return pl.pallas_call(
        body, name=name, grid=(T // ROW_TILE,),
        in_specs=[_row_spec(D), _row_spec(D), _row_spec(LANES), _full_spec((1, D))],
        out_specs=[_row_spec(D), _row_spec(D), _full_spec((1, D)), _full_spec((1, D))],
        out_shape=[jax.ShapeDtypeStruct((T, D), F32), jax.ShapeDtypeStruct((T, D), BF16),
                   jax.ShapeDtypeStruct((1, D), F32), jax.ShapeDtypeStruct((1, D), F32)],
        compiler_params=_cparams("arbitrary"),
    )(dy, xh, r, g)


def _loss_grad(y, target, *, name):
    T, D = y.shape

    def body(y_ref, t_ref, dy_ref, sq_ref):
        err = y_ref[...] - t_ref[...]
        dy_ref[...] = err / D

        @pl.when(pl.program_id(0) == 0)
        def _():
            sq_ref[...] = jnp.zeros_like(sq_ref)

        sq_ref[...] += jnp.sum(err * err, 0, keepdims=True)

    return pl.pallas_call(
        body, name=name, grid=(T // ROW_TILE,),
        in_specs=[_row_spec(D), _row_spec(D)],
        out_specs=[_row_spec(D), _full_spec((1, D))],
        out_shape=[jax.ShapeDtypeStruct((T, D), F32), jax.ShapeDtypeStruct((1, D), F32)],
        compiler_params=_cparams("arbitrary"),
    )(y, target)


def _ple_bwd_elem(dx3, g, e, *, name):
    T, D = dx3.shape

    def body(dx_ref, g_ref, e_ref, de_ref, dz_ref, db_ref):
        dx, gv = dx_ref[...], g_ref[...]
        de_ref[...] = (dx * gv).astype(BF16)
        dz = dx * e_ref[...] * gv * (1.0 - gv)
        dz_ref[...] = dz.astype(BF16)

        @pl.when(pl.program_id(0) == 0)
        def _():
            db_ref[...] = jnp.zeros_like(db_ref)

        db_ref[...] += jnp.sum(dz, 0, keepdims=True)

    return pl.pallas_call(
        body, name=name, grid=(T // ROW_TILE,),
        in_specs=[_row_spec(D), _row_spec(D), _row_spec(D)],
        out_specs=[_row_spec(D), _row_spec(D), _full_spec((1, D))],
        out_shape=[jax.ShapeDtypeStruct((T, D), BF16), jax.ShapeDtypeStruct((T, D), BF16),
                   jax.ShapeDtypeStruct((1, D), F32)],
        compiler_params=_cparams("arbitrary"),
    )(dx3, g, e)


def _rotate(xv, a, bm, bp, sign):
    half = MLA_ROPE // 2
    width = xv.shape[-1]
    a, bm, bp = (_widen(t, width) for t in (a, bm, bp))
    return xv * a + sign * (pltpu.roll(xv, width - half, 1) * bm + pltpu.roll(xv, half, 1) * bp)


def _rope(x, tabs, seq, *, sign, name):
    T, width = x.shape

    def body(x_ref, a_ref, bm_ref, bp_ref, o_ref):
        o_ref[...] = _rotate(x_ref[...], a_ref[...], bm_ref[...], bp_ref[...], sign).astype(BF16)

    return pl.pallas_call(
        body, name=name, grid=(T // ROW_TILE,),
        in_specs=[_row_spec(width)] + [_tab_spec(LANES, seq // ROW_TILE)] * 3,
        out_specs=_row_spec(width),
        out_shape=jax.ShapeDtypeStruct((T, width), BF16),
        compiler_params=_cparams("parallel"),
    )(x, *tabs)


def _mla_keys(knp, h, k_tabs, seq, *, name):
    T = knp.shape[0]

    def body(k_ref, h_ref, a_ref, bm_ref, bp_ref, o_ref):
        kr = _rotate(h_ref[...], a_ref[...], bm_ref[...], bp_ref[...], 1.0)
        for hd in range(MLA_HEADS):
            cols = slice(hd * LANES, (hd + 1) * LANES)
            o_ref[:, cols] = (k_ref[:, cols].astype(F32) + kr).astype(BF16)

    return pl.pallas_call(
        body, name=name, grid=(T // ROW_TILE,),
        in_specs=[_row_spec(MLA_HEADS * LANES), _row_spec(LANES, EV_KR[0] // LANES)]
        + [_tab_spec(LANES, seq // ROW_TILE)] * 3,
        out_specs=_row_spec(MLA_HEADS * LANES),
        out_shape=jax.ShapeDtypeStruct((T, MLA_HEADS * LANES), BF16),
        compiler_params=_cparams("parallel"),
    )(knp, h, *k_tabs)


def _mla_rope_key_grad(dk, k_tabs, seq, *, name):
    T = dk.shape[0]

    def body(dk_ref, a_ref, bm_ref, bp_ref, o_ref):
        tot = dk_ref[:, 0:LANES]
        for hd in range(1, MLA_HEADS):
            tot = tot + dk_ref[:, hd * LANES:(hd + 1) * LANES]
        o_ref[...] = _rotate(tot, a_ref[...], bm_ref[...], bp_ref[...], -1.0).astype(BF16)

    return pl.pallas_call(
        body, name=name, grid=(T // ROW_TILE,),
        in_specs=[_row_spec(MLA_HEADS * LANES)] + [_tab_spec(LANES, seq // ROW_TILE)] * 3,
        out_specs=_row_spec(LANES),
        out_shape=jax.ShapeDtypeStruct((T, LANES), BF16),
        compiler_params=_cparams("parallel"),
    )(dk, *k_tabs)


def _even_norms(h, gq, gkv, *, name):
    T = h.shape[0]

    def body(h_ref, gq_ref, gkv_ref, cq_ref, ckv_ref, rq_ref, rkv_ref):
        cq = h_ref[:, EV_CQ[0]:EV_CQ[1]]
        rq = lax.rsqrt(jnp.mean(cq * cq, -1, keepdims=True) + NORM_EPS)
        cq_ref[...] = (cq * rq * gq_ref[...]).astype(BF16)
        rq_ref[...] = jnp.broadcast_to(rq, rq_ref.shape)
        ckv = h_ref[:, EV_CKV[0]:EV_CKV[1]]
        rkv = lax.rsqrt(jnp.mean(ckv * ckv, -1, keepdims=True) + NORM_EPS)
        ckv_ref[...] = (ckv * rkv * gkv_ref[...]).astype(BF16)
        rkv_ref[...] = jnp.broadcast_to(rkv, rkv_ref.shape)

    return pl.pallas_call(
        body, name=name, grid=(T // ROW_TILE,),
        in_specs=[_row_spec(EVEN_IN_PAD), _full_spec((1, MLA_Q_LORA)), _full_spec((1, MLA_KV_LORA))],
        out_specs=[_row_spec(MLA_Q_LORA), _row_spec(MLA_KV_LORA), _row_spec(LANES), _row_spec(LANES)],
        out_shape=[jax.ShapeDtypeStruct((T, MLA_Q_LORA), BF16), jax.ShapeDtypeStruct((T, MLA_KV_LORA), BF16),
                   jax.ShapeDtypeStruct((T, LANES), F32), jax.ShapeDtypeStruct((T, LANES), F32)],
        compiler_params=_cparams("parallel"),
    )(h, gq, gkv)


def _even_in_bwd(h, rq, rkv, gq, gkv, dcqn, dckvn, dqs, dks, dvs, dkr, *, name):
    T = h.shape[0]

    def rms_bwd(c, r, g, dy):
        r = _widen(r, c.shape[-1])
        xr = c * r
        dyg = dy * g
        return r * (dyg - xr * jnp.mean(dyg * xr, -1, keepdims=True)), jnp.sum(dy * xr, 0, keepdims=True)

    def body(h_ref, rq_ref, rkv_ref, gq_ref, gkv_ref, dcq_ref, dckv_ref, dqs_ref, dks_ref, dvs_ref, dkr_ref,
             dh_ref, dgq_ref, dgkv_ref):
        @pl.when(pl.program_id(0) == 0)
        def _():
            dgq_ref[...] = jnp.zeros_like(dgq_ref)
            dgkv_ref[...] = jnp.zeros_like(dgkv_ref)

        dcq, dgq = rms_bwd(h_ref[:, EV_CQ[0]:EV_CQ[1]], rq_ref[...], gq_ref[...], dcq_ref[...])
        dckv, dgkv = rms_bwd(h_ref[:, EV_CKV[0]:EV_CKV[1]], rkv_ref[...], gkv_ref[...], dckv_ref[...])
        dgq_ref[...] += dgq
        dgkv_ref[...] += dgkv
        dh_ref[:, EV_QS[0]:EV_QS[1]] = dqs_ref[...]
        dh_ref[:, EV_CQ[0]:EV_CQ[1]] = dcq.astype(BF16)
        dh_ref[:, EV_CKV[0]:EV_CKV[1]] = dckv.astype(BF16)
        dh_ref[:, EV_KS[0]:EV_KS[1]] = dks_ref[...]
        dh_ref[:, EV_VS[0]:EV_VS[1]] = dvs_ref[...]
        dh_ref[:, EV_KR[0]:EV_KR[1]] = dkr_ref[...]

    return pl.pallas_call(
        body, name=name, grid=(T // ROW_TILE,),
        in_specs=[_row_spec(EVEN_IN_PAD), _row_spec(LANES), _row_spec(LANES), _full_spec((1, MLA_Q_LORA)),
                  _full_spec((1, MLA_KV_LORA)), _row_spec(MLA_Q_LORA), _row_spec(MLA_KV_LORA),
                  _row_spec(SWA_HEADS * HEAD_DIM), _row_spec(LANES), _row_spec(LANES), _row_spec(LANES)],
        out_specs=[_row_spec(EVEN_IN_PAD), _full_spec((1, MLA_Q_LORA)), _full_spec((1, MLA_KV_LORA))],
        out_shape=[jax.ShapeDtypeStruct((T, EVEN_IN_PAD), BF16), jax.ShapeDtypeStruct((1, MLA_Q_LORA), F32),
                   jax.ShapeDtypeStruct((1, MLA_KV_LORA), F32)],
        compiler_params=_cparams("arbitrary"),
    )(h, rq, rkv, gq, gkv, dcqn, dckvn, dqs, dks, dvs, dkr)


def _fox_decay_fwd(f3, bf, *, name):
    B, S, _ = f3.shape

    def body(f_ref, b_ref, csh_ref, chs_ref):
        x = f_ref[...] + b_ref[...]
        c = jnp.minimum(x, 0.0) - jnp.log1p(jnp.exp(-jnp.abs(x)))
        row = lax.broadcasted_iota(jnp.int32, (S, LANES), 0)
        k = 1
        while k < S:
            c = c + jnp.where(row >= k, pltpu.roll(c, k, 0), 0.0)
            k *= 2
        csh_ref[...] = c
        chs_ref[...] = c.T

    return pl.pallas_call(
        body, name=name, grid=(B,),
        in_specs=[pl.BlockSpec((None, S, LANES), lambda b: (b, 0, 0)), pl.BlockSpec((1, LANES), lambda b: (0, 0))],
        out_specs=[pl.BlockSpec((None, S, LANES), lambda b: (b, 0, 0)),
                   pl.BlockSpec((None, LANES, S), lambda b: (b, 0, 0))],
        out_shape=[jax.ShapeDtypeStruct((B, S, LANES), F32), jax.ShapeDtypeStruct((B, LANES, S), F32)],
        compiler_params=_cparams("parallel"),
    )(f3, bf)


def _fox_decay_bwd(dc_hs, f3, bf, *, name):
    B, S, _ = f3.shape

    def body(dc_ref, f_ref, b_ref, df_ref, db_ref):
        g = dc_ref[...].T
        row = lax.broadcasted_iota(jnp.int32, (S, LANES), 0)
        k = 1
        while k < S:
            g = g + jnp.where(row < S - k, pltpu.roll(g, S - k, 0), 0.0)
            k *= 2
        x = f_ref[...] + b_ref[...]
        df = g * (1.0 / (1.0 + jnp.exp(x)))
        df_ref[...] = df.astype(BF16)

        @pl.when(pl.program_id(0) == 0)
        def _():
            db_ref[...] = jnp.zeros_like(db_ref)

        db_ref[...] += jnp.sum(df, 0, keepdims=True)

    return pl.pallas_call(
        body, name=name, grid=(B,),
        in_specs=[pl.BlockSpec((None, LANES, S), lambda b: (b, 0, 0)),
                  pl.BlockSpec((None, S, LANES), lambda b: (b, 0, 0)), pl.BlockSpec((1, LANES), lambda b: (0, 0))],
        out_specs=[pl.BlockSpec((None, S, LANES), lambda b: (b, 0, 0)), pl.BlockSpec((1, LANES), lambda b: (0, 0))],
        out_shape=[jax.ShapeDtypeStruct((B, S, LANES), BF16), jax.ShapeDtypeStruct((1, LANES), F32)],
        compiler_params=_cparams("arbitrary"),
    )(dc_hs, f3, bf)


def _head_column(block, h):
    lane = lax.broadcasted_iota(jnp.int32, block.shape, 1)
    return jnp.sum(jnp.where(lane == h, block, 0.0), axis=-1, keepdims=True)


def _causal_mask(s):
    r = lax.broadcasted_iota(jnp.int32, s.shape, 0)
    c = lax.broadcasted_iota(jnp.int32, s.shape, 1)
    return jnp.where(c <= r, s, NEG_INF)


def _low_half(shape):
    return (lax.broadcasted_iota(jnp.int32, shape, 1) % LANES) < HEAD_DIM


def _widen(x, cols):
    return jnp.concatenate([x] * (cols // LANES), axis=1)


def _both_halves(x, lo):
    r = pltpu.roll(x, HEAD_DIM, 1)
    return jnp.where(lo, x, r), jnp.where(lo, r, x)


MESH_ID = pl.DeviceIdType.MESH
HBM_SPEC = pl.BlockSpec(memory_space=pltpu.HBM)
VMEM_SPEC = pl.BlockSpec(memory_space=pltpu.VMEM)


def _mesh_place():
    x, y, c = lax.axis_index("x"), lax.axis_index("y"), lax.axis_index("c")
    return x, y, c, 4 * x + 2 * y + c


def _peers(x, y, c):
    out = []
    for mask in range(1, N_DEV):
        dx, dy, dc = (mask >> 2) & 1, (mask >> 1) & 1, mask & 1
        px, py, pc = (1 - x if dx else x), (1 - y if dy else y), (1 - c if dc else c)
        out.append(((px, py, pc), 4 * px + 2 * py + pc))
    return out


def _comm_out_shapes(comm):
    return [jax.ShapeDtypeStruct(((N_DEV,) + a.shape) if kind == "gather" else a.shape, a.dtype) for kind, a in comm]


def _comm_scratch(comm):
    n = len(comm)
    return [pltpu.SemaphoreType.DMA((n, 7)), pltpu.SemaphoreType.DMA((n, 7)), pltpu.SemaphoreType.DMA((n,))]


def _comm_copies(kinds, in_refs, out_refs, sems, place):
    send_sems, recv_sems, local_sems = sems
    x, y, c, me = place
    local, remote = [], []
    for w, kind in enumerate(kinds):
        mine = in_refs[w] if kind == "gather" else in_refs[w].at[me]
        local.append(pltpu.make_async_copy(mine, out_refs[w].at[me], local_sems.at[w]))
        for k, (peer, peer_idx) in enumerate(_peers(x, y, c)):
            remote.append(pltpu.make_async_remote_copy(
                src_ref=in_refs[w] if kind == "gather" else in_refs[w].at[peer_idx], dst_ref=out_refs[w].at[me],
                send_sem=send_sems.at[w, k], recv_sem=recv_sems.at[w, k], device_id=peer, device_id_type=MESH_ID))
    return local, remote


def _comm_start(kinds, in_refs, out_refs, sems, place):
    local, remote = _comm_copies(kinds, in_refs, out_refs, sems, place)
    for cp in local + remote:
        cp.start()


def _comm_wait(kinds, in_refs, out_refs, sems, place):
    local, remote = _comm_copies(kinds, in_refs, out_refs, sems, place)
    for cp in remote:
        cp.wait_recv()
    for cp in remote:
        cp.wait_send()
    for cp in local:
        cp.wait()


def _exchange(comm, *, name):
    n = len(comm)
    kinds = [k for k, _ in comm]

    def body(*refs):
        place = _mesh_place()
        _comm_start(kinds, refs[:n], refs[n:2 * n], refs[2 * n:], place)
        _comm_wait(kinds, refs[:n], refs[n:2 * n], refs[2 * n:], place)

    return pl.pallas_call(
        body, name=name, out_shape=_comm_out_shapes(comm), in_specs=[HBM_SPEC] * n, out_specs=[HBM_SPEC] * n,
        scratch_shapes=_comm_scratch(comm),
    )(*[a for _, a in comm])


def _flash_fwd(qa, ka, va, *, q_blk0, k_blk0, v_blk0, W, n_pairs, B, S, scale, csh=None, crow=None, comm=(), name):
    t = ATT_TILE
    nq = S // t
    P = PAIRS_PER_STEP_FWD
    decay = csh is not None
    split = W == LANES
    assert n_pairs % P == 0 and q_blk0 % P == 0 and k_blk0 % P == 0 and v_blk0 % P == 0
    n_c, kinds = len(comm), [k for k, _ in comm]
    n_in = 5 if decay else 3
    n_steps = (B, n_pairs // P, nq)

    def body(*refs):
        c_in, c_out = refs[n_in:n_in + n_c], refs[n_in + n_c + 2:n_in + 2 * n_c + 2]
        sems = refs[n_in + 2 * n_c + 4:]
        refs = refs[:n_in] + refs[n_in + n_c:n_in + n_c + 2] + refs[n_in + 2 * n_c + 2:n_in + 2 * n_c + 4]
        if decay:
            q_ref, k_ref, v_ref, csh_ref, crow_ref, o_ref, lse_ref, m_s, acc_s = refs
        else:
            q_ref, k_ref, v_ref, o_ref, lse_ref, m_s, acc_s = refs
        g, i = pl.program_id(1), pl.program_id(2)
        if n_c:
            place = _mesh_place()
            ids = [pl.program_id(ax) for ax in range(3)]

            @pl.when((ids[0] == 0) & (ids[1] == 0) & (ids[2] == 0))
            def _():
                _comm_start(kinds, c_in, c_out, sems, place)

        lo = _low_half((t, LANES))
        qv = q_ref[...]
        qh = []
        for pr in range(P):
            qp = qv[:, pr * W:(pr + 1) * W]
            qh += [jnp.where(lo, qp, jnp.zeros_like(qp)), jnp.where(lo, jnp.zeros_like(qp), qp)] if split \
                else [qp[:, :LANES], qp[:, LANES:]]
        if decay:
            cq = [jnp.broadcast_to(_head_column(csh_ref[...], 2 * P * g + hd), (t, LANES)) for hd in range(2 * P)]
        m_s[...] = jnp.full(m_s.shape, NEG_INF, F32)
        acc_s[...] = jnp.zeros(acc_s.shape, F32)

        def step(j, masked):
            rows = pl.ds(pl.multiple_of(j * t, t), t)
            kb, vb = k_ref[rows, :], v_ref[rows, :]
            for pr in range(P):
                kp, vp = kb[:, pr * W:(pr + 1) * W], vb[:, pr * LANES:(pr + 1) * LANES]
                ones = jnp.ones_like(vp)
                vaug = [jnp.where(lo, vp, ones), jnp.where(lo, ones, vp)]
                for half in range(2):
                    hd = 2 * pr + half
                    kh = kp if split else kp[:, half * LANES:(half + 1) * LANES]
                    s = lax.dot_general(qh[hd], kh, NT, preferred_element_type=F32) * scale
                    if decay:
                        s = s + _widen(cq[hd], t) - crow_ref[hd, j]
                    if masked:
                        s = _causal_mask(s)
                    m_prev = m_s[hd]
                    m_new = jnp.maximum(m_prev, jnp.max(s, -1, keepdims=True))
                    p = jnp.exp(s - _widen(m_new, t))
                    acc_s[hd] = jnp.exp(m_prev - m_new) * acc_s[hd] + lax.dot_general(
                        p.astype(BF16), vaug[half], NN, preferred_element_type=F32)
                    m_s[hd] = m_new

        def loop_body(j, carry):
            step(j, False)
            return carry

        lax.fori_loop(0, i, loop_body, 0)
        step(i, True)
        for pr in range(P):
            acc0, acc1 = acc_s[2 * pr], acc_s[2 * pr + 1]
            _, l0 = _both_halves(acc0, lo)
            l1, _ = _both_halves(acc1, lo)
            cols = slice(pr * LANES, (pr + 1) * LANES)
            o_ref[:, cols] = jnp.where(lo, acc0 / l0, acc1 / l1).astype(BF16)
            lse_ref[:, cols] = jnp.where(lo, m_s[2 * pr] + jnp.log(l0), m_s[2 * pr + 1] + jnp.log(l1))
        if n_c:
            @pl.when((ids[0] == n_steps[0] - 1) & (ids[1] == n_steps[1] - 1) & (ids[2] == n_steps[2] - 1))
            def _():
                _comm_wait(kinds, c_in, c_out, sems, place)

    in_specs = [pl.BlockSpec((t, P * W), lambda b, g, i: (b * nq + i, q_blk0 // P + g)),
                pl.BlockSpec((S, P * W), lambda b, g, i: (b, k_blk0 // P + g)),
                pl.BlockSpec((S, P * LANES), lambda b, g, i: (b, v_blk0 // P + g))]
    args = [qa, ka, va]
    if decay:
        in_specs += [pl.BlockSpec((None, t, LANES), lambda b, g, i: (b, i, 0)),
                     pl.BlockSpec((None, 2 * P, nq, 1, t), lambda b, g, i: (b, g, 0, 0, 0))]
        args += [csh, crow]
    out_spec = pl.BlockSpec((t, P * LANES), lambda b, g, i: (b * nq + i, g))
    res = pl.pallas_call(
        body, name=name, grid=n_steps, in_specs=in_specs + [HBM_SPEC] * n_c,
        out_specs=[out_spec, out_spec] + [HBM_SPEC] * n_c,
        out_shape=[jax.ShapeDtypeStruct((B * S, n_pairs * LANES), BF16),
                   jax.ShapeDtypeStruct((B * S, n_pairs * LANES), F32)] + _comm_out_shapes(comm),
        scratch_shapes=[pltpu.VMEM((2 * P, t, LANES), F32), pltpu.VMEM((2 * P, t, LANES), F32)]
        + (_comm_scratch(comm) if n_c else []),
        compiler_params=_cparams(*(("arbitrary",) * 3 if n_c else ("parallel",) * 3)),
    )(*args, *[a for _, a in comm])
    return res[0], res[1], list(res[2:])


def _flash_bwd(qa, ka, va, oa, doa, lsea, *, q_blk0, k_blk0, v_blk0, do_blk0, W, n_pairs, B, S, scale, qk_dtype,
               csh=None, crow=None, comm=(), name):
    t = ATT_TILE_BWD
    nq = S // t
    P = PAIRS_PER_STEP_BWD
    decay = csh is not None
    if decay:
        crow = crow.reshape(B, 2 * n_pairs, nq, 1, t)
    split = W == LANES
    assert n_pairs % P == 0 and q_blk0 % P == 0 and k_blk0 % P == 0 and v_blk0 % P == 0 and do_blk0 % P == 0
    n_c, kinds = len(comm), [k for k, _ in comm]
    n_in, n_out, n_scr = (8, 5, 8) if decay else (6, 3, 5)
    n_steps = (B, n_pairs // P, nq)

    def body(*refs):
        c_in = refs[n_in:n_in + n_c]
        c_out = refs[n_in + n_c + n_out:n_in + 2 * n_c + n_out]
        sems = refs[n_in + 2 * n_c + n_out + n_scr:]
        refs = (refs[:n_in] + refs[n_in + n_c:n_in + n_c + n_out]
                + refs[n_in + 2 * n_c + n_out:n_in + 2 * n_c + n_out + n_scr])
        if n_c:
            place = _mesh_place()
            ids = [pl.program_id(ax) for ax in range(3)]

            @pl.when((ids[0] == 0) & (ids[1] == 0) & (ids[2] == 0))
            def _():
                _comm_start(kinds, c_in, c_out, sems, place)

        if decay:
            (q_ref, k_ref, v_ref, o_ref, do_ref, lse_ref, csh_ref, crow_ref, dq_ref, dk_ref, dv_ref, dck_ref, dcq_ref,
             dq_s, lse_s, delta_s, dk_s, dv_s, cq_s, dcq_s, dck_s) = refs
        else:
            (q_ref, k_ref, v_ref, o_ref, do_ref, lse_ref, dq_ref, dk_ref, dv_ref,
             dq_s, lse_s, delta_s, dk_s, dv_s) = refs
        g, j = pl.program_id(1), pl.program_id(2)
        lo = _low_half((t, LANES))

        @pl.when(j == 0)
        def _():
            lo_s = _low_half((S, LANES))
            dq_s[...] = jnp.zeros(dq_s.shape, F32)
            for pr in range(P):
                cols = slice(pr * LANES, (pr + 1) * LANES)
                lse_s[2 * pr], lse_s[2 * pr + 1] = _both_halves(lse_ref[:, cols], lo_s)
                dd = do_ref[:, cols].astype(F32) * o_ref[:, cols].astype(F32)
                delta_s[2 * pr] = jnp.broadcast_to(jnp.sum(jnp.where(lo_s, dd, 0.0), -1, keepdims=True), (S, LANES))
                delta_s[2 * pr + 1] = jnp.broadcast_to(jnp.sum(jnp.where(lo_s, 0.0, dd), -1, keepdims=True),
                                                       (S, LANES))
            if decay:
                for hd in range(2 * P):
                    cq_s[hd] = jnp.broadcast_to(_head_column(csh_ref[...], 2 * P * g + hd), (S, LANES))
                dcq_s[...] = jnp.zeros(dcq_s.shape, F32)

        kb, vb = k_ref[...], v_ref[...]
        kh, vh = [], []
        for pr in range(P):
            kp, vp = kb[:, pr * W:(pr + 1) * W], vb[:, pr * LANES:(pr + 1) * LANES]
            zk, zv = jnp.zeros_like(kp), jnp.zeros_like(vp)
            kh += [jnp.where(lo, kp, zk), jnp.where(lo, zk, kp)] if split else [kp[:, :LANES], kp[:, LANES:]]
            vh += [jnp.where(lo, vp, zv), jnp.where(lo, zv, vp)]
        dk_s[...] = jnp.zeros(dk_s.shape, F32)
        dv_s[...] = jnp.zeros(dv_s.shape, F32)
        if decay:
            dck_s[...] = jnp.zeros(dck_s.shape, F32)

        def step(i, masked):
            rows = pl.ds(pl.multiple_of(i * t, t), t)
            qi, doi = q_ref[rows, :], do_ref[rows, :]
            for pr in range(P):
                qp, dop = qi[:, pr * W:(pr + 1) * W], doi[:, pr * LANES:(pr + 1) * LANES]
                for half in range(2):
                    hd = 2 * pr + half
                    qx = qp if split else qp[:, half * LANES:(half + 1) * LANES]
                    s = lax.dot_general(qx, kh[hd], NT, preferred_element_type=F32) * scale
                    if decay:
                        s = s + _widen(cq_s[hd, rows, :], t) - crow_ref[hd, j]
                    if masked:
                        s = _causal_mask(s)
                    p = jnp.exp(s - _widen(lse_s[hd, rows, :], t))
                    dv_s[hd] += lax.dot_general(p.astype(BF16), dop, TN, preferred_element_type=F32)
                    dp = lax.dot_general(dop, vh[hd], NT, preferred_element_type=F32)
                    ds = p * (dp - _widen(delta_s[hd, rows, :], t))
                    dss = (ds * scale).astype(BF16)
                    dk_s[hd] += lax.dot_general(dss, qx, TN, preferred_element_type=F32)
                    dqc = lax.dot_general(dss, kh[hd], NN, preferred_element_type=F32)
                    if split:
                        dq_s[rows, pr * W:(pr + 1) * W] += dqc
                    else:
                        dq_s[rows, hd * LANES:(hd + 1) * LANES] += dqc
                    if decay:
                        dck_s[hd] -= jnp.sum(ds, 0, keepdims=True)
                        part = ds[:, :LANES]
                        for c in range(1, t // LANES):
                            part = part + ds[:, c * LANES:(c + 1) * LANES]
                        dcq_s[hd, rows, :] += part

        def loop_body(i, carry):
            step(i, False)
            return carry

        step(j, True)
        lax.fori_loop(j + 1, nq, loop_body, 0)
        for pr in range(P):
            if split:
                dk_ref[:, pr * W:(pr + 1) * W] = jnp.where(lo, dk_s[2 * pr], dk_s[2 * pr + 1]).astype(dk_ref.dtype)
            else:
                for half in range(2):
                    hd = 2 * pr + half
                    dk_ref[:, hd * LANES:(hd + 1) * LANES] = dk_s[hd].astype(dk_ref.dtype)
            dv_ref[:, pr * LANES:(pr + 1) * LANES] = jnp.where(lo, dv_s[2 * pr], dv_s[2 * pr + 1]).astype(BF16)
        if decay:
            dck_ref[...] = dck_s[...]

        @pl.when(j == nq - 1)
        def _():
            dq_ref[...] = dq_s[...].astype(dq_ref.dtype)
            if decay:
                for hd in range(2 * P):
                    dcq_ref[hd] = jnp.sum(dcq_s[hd].T, 0, keepdims=True)

        if n_c:
            @pl.when((ids[0] == n_steps[0] - 1) & (ids[1] == n_steps[1] - 1) & (ids[2] == n_steps[2] - 1))
            def _():
                _comm_wait(kinds, c_in, c_out, sems, place)

    full = lambda w, blk0: pl.BlockSpec((S, P * w), lambda b, g, j: (b, blk0 // P + g))
    blk = lambda w, blk0: pl.BlockSpec((t, P * w), lambda b, g, j: (b * nq + j, blk0 // P + g))
    in_specs = [full(W, q_blk0), blk(W, k_blk0), blk(LANES, v_blk0), full(LANES, 0), full(LANES, do_blk0),
                full(LANES, 0)]
    args = [qa, ka, va, oa, doa, lsea]
    T = B * S
    out_specs = [full(W, 0), blk(W, 0), blk(LANES, 0)]
    out_shape = [jax.ShapeDtypeStruct((T, n_pairs * W), qk_dtype), jax.ShapeDtypeStruct((T, n_pairs * W), qk_dtype),
                 jax.ShapeDtypeStruct((T, n_pairs * LANES), BF16)]
    per_head = lambda rows: pltpu.VMEM((2 * P, rows, LANES), F32)
    scratch = [pltpu.VMEM((S, P * W), F32), per_head(S), per_head(S), per_head(t), per_head(t)]
    if decay:
        in_specs += [pl.BlockSpec((None, S, LANES), lambda b, g, j: (b, 0, 0)),
                     pl.BlockSpec((None, 2 * P, nq, 1, t), lambda b, g, j: (b, g, 0, 0, 0))]
        args += [csh, crow]
        out_specs += [pl.BlockSpec((None, 2 * P, None, 1, t), lambda b, g, j: (b, g, j, 0, 0)),
                      pl.BlockSpec((None, 2 * P, 1, S), lambda b, g, j: (b, g, 0, 0))]
        out_shape += [jax.ShapeDtypeStruct((B, 2 * n_pairs, nq, 1, t), F32),
                      jax.ShapeDtypeStruct((B, 2 * n_pairs, 1, S), F32)]
        scratch += [per_head(S), per_head(S), pltpu.VMEM((2 * P, 1, t), F32)]
    res = pl.pallas_call(
        body, name=name, grid=n_steps, in_specs=in_specs + [HBM_SPEC] * n_c,
        out_specs=out_specs + [HBM_SPEC] * n_c, out_shape=out_shape + _comm_out_shapes(comm),
        scratch_shapes=scratch + (_comm_scratch(comm) if n_c else []),
        compiler_params=_cparams(*(("arbitrary",) * 3 if n_c else ("parallel", "parallel", "arbitrary"))),
    )(*args, *[a for _, a in comm])
    return tuple(res[:n_out]) + (list(res[n_out:]),)


def _swa_common(q_ref, kp_ref, ko_ref, vp_ref, vo_ref, n):
    Q = BLOCK_Q
    lo = _low_half((Q, LANES))
    lo2 = _low_half((2 * Q, LANES))
    kk = jnp.concatenate([kp_ref[...], ko_ref[...]], axis=0)
    vv = jnp.concatenate([vp_ref[...], vo_ref[...]], axis=0)
    kdup = [x.astype(BF16) for x in _both_halves(kk, lo2)]
    vdup = [x.astype(BF16) for x in _both_halves(vv, lo2)]
    a = lax.broadcasted_iota(jnp.int32, (SWA_GROUP * Q, 2 * Q), 0) % Q
    col = lax.broadcasted_iota(jnp.int32, (SWA_GROUP * Q, 2 * Q), 1)
    dist = a + Q - col
    valid = (dist >= 0) & (dist < SWA_WINDOW) & ((col >= Q) | (n > 0))
    qv = q_ref[...]
    qm = []
    for a_head in range(SWA_HEADS):
        qp = qv[:, (a_head // 2) * LANES:(a_head // 2 + 1) * LANES]
        keep = lo if a_head % 2 == 0 else jnp.logical_not(lo)
        qm.append(jnp.where(keep, qp, 0.0).astype(BF16))
    qs = [jnp.concatenate(qm[g * SWA_GROUP:(g + 1) * SWA_GROUP], axis=0) for g in range(SWA_KV_HEADS)]
    return lo, lo2, kdup, vdup, valid, qs


def _swa_group_logits(g, qs, kdup, valid, bias_ref):
    heads = slice(g * SWA_GROUP, (g + 1) * SWA_GROUP)
    s = lax.dot_general(qs[g], kdup[g], NT, preferred_element_type=F32) * (HEAD_DIM ** -0.5)
    s = s + bias_ref[heads].reshape(SWA_GROUP * BLOCK_Q, 2 * BLOCK_Q)
    return heads, jnp.where(valid, s, NEG_INF)


def _pair_halves(x, lo):
    Q = BLOCK_Q
    return [jnp.where(lo, x[2 * pr * Q:(2 * pr + 1) * Q], x[(2 * pr + 1) * Q:(2 * pr + 2) * Q])
            for pr in range(SWA_GROUP // 2)]


def _swa_in_specs(nb):
    Q = BLOCK_Q
    own = lambda blk: (lambda b, n: (b * nb + n, blk))
    prev = lambda blk: (lambda b, n: (b * nb + jnp.maximum(n - 1, 0), blk))
    kb, vb = EV_KS[0] // LANES, EV_VS[0] // LANES
    return [pl.BlockSpec((Q, SWA_HEADS * HEAD_DIM), own(0)), pl.BlockSpec((Q, LANES), prev(kb)),
            pl.BlockSpec((Q, LANES), own(kb)), pl.BlockSpec((Q, LANES), prev(vb)), pl.BlockSpec((Q, LANES), own(vb))]


def _swa_fwd(h, bias, sinkcol, *, B, S, comm=(), name):
    Q = BLOCK_Q
    nb = S // Q
    n_c, kinds = len(comm), [k for k, _ in comm]

    def body(*refs):
        c_in, c_out, sems = refs[7:7 + n_c], refs[9 + n_c:9 + 2 * n_c], refs[9 + 2 * n_c:]
        q_ref, kp_ref, ko_ref, vp_ref, vo_ref, bias_ref, sink_ref = refs[:7]
        o_ref, lse_ref = refs[7 + n_c:9 + n_c]
        if n_c:
            place = _mesh_place()
            ids = [pl.program_id(0), pl.program_id(1)]

            @pl.when((ids[0] == 0) & (ids[1] == 0))
            def _():
                _comm_start(kinds, c_in, c_out, sems, place)

        lo, _, kdup, vdup, valid, qs = _swa_common(q_ref, kp_ref, ko_ref, vp_ref, vo_ref, pl.program_id(1))
        lane = lax.broadcasted_iota(jnp.int32, (Q, LANES), 1)
        lse_blk = jnp.zeros((Q, LANES), F32)
        pairs = []
        for g in range(SWA_KV_HEADS):
            heads, s = _swa_group_logits(g, qs, kdup, valid, bias_ref)
            sink = sink_ref[heads].reshape(SWA_GROUP * Q, 1)
            mx = jnp.maximum(jnp.max(s, -1, keepdims=True), sink)
            p = jnp.exp(s - mx)
            l = jnp.sum(p, -1, keepdims=True) + jnp.exp(sink - mx)
            pv = lax.dot_general((p / l).astype(BF16), vdup[g], NN, preferred_element_type=F32)
            lse_g = mx + jnp.log(l)
            for i in range(SWA_GROUP):
                lse_blk = jnp.where(lane == g * SWA_GROUP + i, lse_g[i * Q:(i + 1) * Q], lse_blk)
            pairs += _pair_halves(pv, lo)
        o_ref[...] = jnp.concatenate(pairs, axis=1).astype(BF16)
        lse_ref[...] = lse_blk
        if n_c:
            @pl.when((ids[0] == B - 1) & (ids[1] == nb - 1))
            def _():
                _comm_wait(kinds, c_in, c_out, sems, place)

    whole = lambda shape: pl.BlockSpec(shape, lambda b, n: (0,) * len(shape))
    res = pl.pallas_call(
        body, name=name, grid=(B, nb),
        in_specs=_swa_in_specs(nb) + [whole((SWA_HEADS, Q, 2 * Q)), whole((SWA_HEADS, Q, 1))] + [HBM_SPEC] * n_c,
        out_specs=[pl.BlockSpec((Q, SWA_HEADS * HEAD_DIM), lambda b, n: (b * nb + n, 0)),
                   pl.BlockSpec((Q, LANES), lambda b, n: (b * nb + n, 0))] + [HBM_SPEC] * n_c,
        out_shape=[jax.ShapeDtypeStruct((B * S, SWA_HEADS * HEAD_DIM), BF16),
                   jax.ShapeDtypeStruct((B * S, LANES), F32)] + _comm_out_shapes(comm),
        scratch_shapes=_comm_scratch(comm) if n_c else [],
        compiler_params=_cparams(*(("arbitrary",) * 2 if n_c else ("parallel",) * 2)),
    )(h, h, h, h, h, bias, sinkcol, *[a for _, a in comm])
    return res[0], res[1], list(res[2:])


def _swa_bwd(h, o, do, lse, bias, sinkcol, *, do_blk0, B, S, name):
    Q = BLOCK_Q
    nb = S // Q
    scale = HEAD_DIM ** -0.5

    def body(q_ref, kp_ref, ko_ref, vp_ref, vo_ref, o_ref, do_ref, lse_ref, bias_ref, sink_ref,
             dq_ref, dko_ref, dkp_ref, dvo_ref, dvp_ref, dbias_ref, dsink_ref):
        @pl.when((pl.program_id(0) == 0) & (pl.program_id(1) == 0))
        def _():
            dbias_ref[...] = jnp.zeros_like(dbias_ref)
            dsink_ref[...] = jnp.zeros_like(dsink_ref)

        lo, lo2, kdup, vdup, valid, qs = _swa_common(q_ref, kp_ref, ko_ref, vp_ref, vo_ref, pl.program_id(1))
        lse_blk = lse_ref[...]
        dkk, dvv, dq_pairs = [], [], []
        for g in range(SWA_KV_HEADS):
            heads, s = _swa_group_logits(g, qs, kdup, valid, bias_ref)
            lse_g = jnp.concatenate([_head_column(lse_blk, g * SWA_GROUP + i) for i in range(SWA_GROUP)], axis=0)
            p = jnp.exp(s - lse_g)
            do_g, o_g = [], []
            for i in range(SWA_GROUP):
                cols = slice((g * SWA_GROUP + i) // 2 * LANES, ((g * SWA_GROUP + i) // 2 + 1) * LANES)
                do_p = do_ref[:, cols]
                do_g.append(jnp.where(lo if i % 2 == 0 else jnp.logical_not(lo), do_p, jnp.zeros_like(do_p)))
                o_g.append(o_ref[:, cols])
            doh, oh = jnp.concatenate(do_g, axis=0), jnp.concatenate(o_g, axis=0)
            delta = jnp.sum(doh.astype(F32) * oh.astype(F32), -1, keepdims=True)
            dp = lax.dot_general(doh, vdup[g], NT, preferred_element_type=F32)
            ds = p * (dp - delta)
            dbias_ref[heads] += ds.reshape(SWA_GROUP, Q, 2 * Q)
            dsink_ref[heads] -= (jnp.exp(sink_ref[heads].reshape(SWA_GROUP * Q, 1) - lse_g)
                                 * delta).reshape(SWA_GROUP, Q, 1)
            dss = (ds * scale).astype(BF16)
            dq_pairs += _pair_halves(lax.dot_general(dss, kdup[g], NN, preferred_element_type=F32), lo)
            dkk.append(lax.dot_general(dss, qs[g], TN, preferred_element_type=F32))
            dvv.append(lax.dot_general(p.astype(BF16), doh, TN, preferred_element_type=F32))
        dq_ref[...] = jnp.concatenate(dq_pairs, axis=1).astype(BF16)
        fold = lambda x: x + pltpu.roll(x, HEAD_DIM, 1)
        dk_blk = jnp.where(lo2, fold(dkk[0]), fold(dkk[1]))
        dv_blk = jnp.where(lo2, fold(dvv[0]), fold(dvv[1]))
        dkp_ref[...] = dk_blk[:Q]
        dko_ref[...] = dk_blk[Q:]
        dvp_ref[...] = dv_blk[:Q]
        dvo_ref[...] = dv_blk[Q:]

    whole = lambda shape: pl.BlockSpec(shape, lambda b, n: (0,) * len(shape))
    wide = lambda blk: pl.BlockSpec((Q, SWA_HEADS * HEAD_DIM), lambda b, n: (b * nb + n, blk))
    narrow = pl.BlockSpec((Q, LANES), lambda b, n: (b * nb + n, 0))
    kv_shape = jax.ShapeDtypeStruct((B * S, LANES), F32)
    return pl.pallas_call(
        body, name=name, grid=(B, nb),
        in_specs=_swa_in_specs(nb) + [wide(0), wide(do_blk0), narrow, whole((SWA_HEADS, Q, 2 * Q)),
                                      whole((SWA_HEADS, Q, 1))],
        out_specs=[wide(0), narrow, narrow, narrow, narrow, whole((SWA_HEADS, Q, 2 * Q)), whole((SWA_HEADS, Q, 1))],
        out_shape=[jax.ShapeDtypeStruct((B * S, SWA_HEADS * HEAD_DIM), BF16), kv_shape, kv_shape, kv_shape, kv_shape,
                   jax.ShapeDtypeStruct((SWA_HEADS, Q, 2 * Q), F32), jax.ShapeDtypeStruct((SWA_HEADS, Q, 1), F32)],
        compiler_params=_cparams("arbitrary", "arbitrary"),
    )(h, h, h, h, h, o, do, lse, bias, sinkcol)


def _bias_bucket_sum(dbias, bucket, *, name):
    def body(d_ref, b_ref, o_ref):
        dbv, bk = d_ref[...], b_ref[...]
        lane = lax.broadcasted_iota(jnp.int32, (SWA_HEADS, LANES), 1)
        out = jnp.zeros((SWA_HEADS, LANES), F32)
        for b in range(REL_BUCKETS):
            part = jnp.sum(jnp.where(bk == b, dbv, 0.0), axis=1)
            tot = jnp.sum(part, axis=-1, keepdims=True)
            out = out + jnp.where(lane == b, tot, 0.0)
        o_ref[...] = out

    return pl.pallas_call(
        body, name=name, out_shape=jax.ShapeDtypeStruct((SWA_HEADS, LANES), F32),
        compiler_params=pltpu.CompilerParams(vmem_limit_bytes=VMEM_LIMIT_BYTES),
    )(dbias, bucket)


def _adamw_update(w, g, m, v):
    m_new = ADAM_B1 * m + (1.0 - ADAM_B1) * g
    v_new = ADAM_B2 * v + (1.0 - ADAM_B2) * jnp.square(g)
    m_hat = m_new / (1.0 - ADAM_B1 ** ADAM_STEP)
    v_hat = v_new / (1.0 - ADAM_B2 ** ADAM_STEP)
    return -ADAM_LR * (m_hat / (jnp.sqrt(v_hat) + ADAM_EPS) + ADAM_WD * w), m_new, v_new


def _adamw(w, g, m, v, *, name):
    def body(w_ref, g_ref, m_ref, v_ref, d_ref, nm_ref, nv_ref):
        d_ref[...], nm_ref[...], nv_ref[...] = _adamw_update(w_ref[...], g_ref[...], m_ref[...], v_ref[...])

    return pl.pallas_call(
        body, name=name, out_shape=[jax.ShapeDtypeStruct(w.shape, F32)] * 3,
        compiler_params=pltpu.CompilerParams(vmem_limit_bytes=VMEM_LIMIT_BYTES),
    )(w, g, m, v)


def _adamw_slots(w, parts, m, v, *, name):
    R, C = w.shape
    tr = R if R <= 512 else _pick(R, (256, 128))

    def body(w_ref, p_ref, m_ref, v_ref, g_ref, d_ref, nm_ref, nv_ref):
        g = p_ref[0].astype(F32)
        for j in range(1, N_DEV):
            g = g + p_ref[j].astype(F32)
        g_ref[...] = g
        d_ref[...], nm_ref[...], nv_ref[...] = _adamw_update(w_ref[...], g, m_ref[...], v_ref[...])

    spec = pl.BlockSpec((tr, C), lambda i: (i, 0))
    return pl.pallas_call(
        body, name=name, grid=(R // tr,),
        in_specs=[spec, pl.BlockSpec((N_DEV, tr, C), lambda i: (0, i, 0)), spec, spec], out_specs=[spec] * 4,
        out_shape=[jax.ShapeDtypeStruct((R, C), F32)] * 4, compiler_params=_cparams("parallel"),
    )(w, parts, m, v)


def _all_gather_hbm(blocks, *, name):
    n = len(blocks)

    def body(*refs):
        x_refs, out_refs = refs[:n], refs[n:2 * n]
        send_sems, recv_sems, local_sems = refs[2 * n:]
        x, y, c, _ = _mesh_place()
        me, sibling = (x, y, c), (x, y, 1 - c)
        chips = [(1 - x, y), (x, 1 - y), (1 - x, 1 - y)]

        def copy(w, k, blk, to, src=None):
            px, py, pc = blk
            slot = out_refs[w].at[4 * px + 2 * py + pc]
            return pltpu.make_async_remote_copy(
                src_ref=slot if src is None else src, dst_ref=slot,
                send_sem=send_sems.at[w, k], recv_sem=recv_sems.at[w, k], device_id=to, device_id_type=MESH_ID)

        mine = [pltpu.make_async_copy(x_refs[w], out_refs[w].at[4 * x + 2 * y + c], local_sems.at[w])
                for w in range(n)]
        for cp in mine:
            cp.start()
        first = []
        for w in range(n):
            first.append(copy(w, 0, me, sibling, src=x_refs[w]))
            first += [copy(w, 1 + j, me, (*chip, c), src=x_refs[w]) for j, chip in enumerate(chips)]
        for cp in first:
            cp.start()
        passed = []
        for j, chip in enumerate(chips):
            for w in range(n):
                copy(w, 1 + j, (*chip, c), me).wait_recv()
                fwd = copy(w, 4 + j, (*chip, c), sibling)
                fwd.start()
                passed.append(fwd)
        for w in range(n):
            copy(w, 0, sibling, me).wait_recv()
            for j, chip in enumerate(chips):
                copy(w, 4 + j, (*chip, 1 - c), me).wait_recv()
        for cp in first + passed:
            cp.wait_send()
        for cp in mine:
            cp.wait()

    return pl.pallas_call(
        body, name=name, out_shape=[jax.ShapeDtypeStruct((N_DEV,) + b.shape, b.dtype) for b in blocks],
        in_specs=[HBM_SPEC] * n, out_specs=[HBM_SPEC] * n,
        scratch_shapes=[pltpu.SemaphoreType.DMA((n, 7)), pltpu.SemaphoreType.DMA((n, 7)),
                        pltpu.SemaphoreType.DMA((n,))],
    )(*blocks)


def _all_reduce_small(block, *, name):
    R, W = block.shape

    def body(x_ref, out_ref, buf, send_sems, recv_sems):
        x, y, c, me = _mesh_place()
        copies = []
        for k, (peer, _) in enumerate(_peers(x, y, c)):
            copies.append(pltpu.make_async_remote_copy(
                src_ref=x_ref, dst_ref=buf.at[me], send_sem=send_sems.at[k], recv_sem=recv_sems.at[k],
                device_id=peer, device_id_type=MESH_ID))
        for cp in copies:
            cp.start()
        buf[me] = x_ref[...]
        for cp in copies:
            cp.wait_recv()
        for cp in copies:
            cp.wait_send()
        acc = buf[0]
        for j in range(1, N_DEV):
            acc = acc + buf[j]
        out_ref[...] = acc

    return pl.pallas_call(
        body, name=name, out_shape=jax.ShapeDtypeStruct((R, W), F32),
        in_specs=[VMEM_SPEC], out_specs=VMEM_SPEC,
        scratch_shapes=[pltpu.VMEM((N_DEV, R, W), F32), pltpu.SemaphoreType.DMA((7,)), pltpu.SemaphoreType.DMA((7,))],
    )(block)


def _assemble(name, g):
    if BIG_AXIS[name] == 2:
        return jnp.concatenate([g[j] for j in range(N_DEV)], axis=1)
    return g.reshape(N_DEV * g.shape[1], g.shape[2])


def _split_for_devices(name, g):
    if BIG_AXIS[name] == 2:
        b = g.shape[1] // N_DEV
        return jnp.stack([g[:, j * b:(j + 1) * b] for j in range(N_DEV)]).astype(BF16)
    return g.reshape(N_DEV, g.shape[0] // N_DEV, g.shape[1]).astype(BF16)


def _layer_weight_keys(i):
    j = i // 2
    mixer = [('ev_w_in', j), ('ev_w_uq', j), ('ev_w_ukv', j), ('ev_w_out', j)] if i % 2 == 0 \
        else [('od_w_in', j), ('od_w_out', j)]
    return mixer + [('w_up', i), ('w_down', i), ('ple_w_proj', i), ('ple_w_gate', i)]


class _MeshExchange:
    def __init__(self, shards):
        self.shards = shards
        self.weights = {}
        self.pending = []
        self.in_flight = []
        self.received = {}

    def layer_weights(self, i):
        keys = _layer_weight_keys(i)
        if i == 0:
            got = _all_gather_hbm([self.shards[k] for k in keys], name="gather_l0")
            self.weights[0] = {k[0]: _assemble(k[0], g) for k, g in zip(keys, got)}
        return self.weights[i]

    def _fwd_keys(self, i, part):
        if i + 1 >= DEPTH:
            return []
        keys = _layer_weight_keys(i + 1)
        if i % 2 == 1:
            return keys if part == 0 else []
        return keys[:2] if part == 0 else keys[2:]

    def fwd_items(self, i, part):
        return [("gather", self.shards[k]) for k in self._fwd_keys(i, part)]

    def fwd_done(self, i, part, outs):
        got = {k[0]: _assemble(k[0], g) for k, g in zip(self._fwd_keys(i, part), outs)}
        self.weights.setdefault(i + 1, {}).update(got)

    def push_grads(self, grads):
        self.pending += [(k, _split_for_devices(k[0], g)) for k, g in grads.items()]

    def bwd_items(self):
        self.in_flight, self.pending = self.pending, []
        return [("scatter", parts) for _, parts in self.in_flight]

    def bwd_done(self, outs):
        for (k, _), got in zip(self.in_flight, outs):
            self.received[k] = got
        self.in_flight = []

    def finish(self):
        if self.pending:
            outs = _exchange(self.bwd_items(), name="exchange_rest")
            self.bwd_done(outs)
        return self.received


PACK_ROWS = 8


def _pack_small(vals):
    flat = jnp.concatenate([vals[n].reshape(-1).astype(F32) for n in SMALL])
    pad = (-flat.shape[0]) % (PACK_ROWS * LANES)
    return jnp.pad(flat, (0, pad)).reshape(-1, LANES)


def _unpack_small(block, shapes):
    flat = block.reshape(-1)
    out, off = {}, 0
    for n in SMALL:
        sz = math.prod(shapes[n])
        out[n] = flat[off:off + sz].reshape(shapes[n])
        off += sz
    return out


def _rope_tables(S):
    half = MLA_ROPE // 2
    inv = 1.0 / (ROPE_THETA ** (jnp.arange(0, MLA_ROPE, 2, dtype=F32) / MLA_ROPE))
    ang = jnp.arange(S, dtype=F32)[:, None] * inv[None, :]
    cos, sin = jnp.cos(ang), jnp.sin(ang)
    zeros = jnp.zeros((S, half), F32)
    tail = jnp.zeros((S, LANES - MLA_QK), F32)

    def block(rope_part, nope_val):
        return jnp.concatenate([jnp.full((S, MLA_NOPE), nope_val, F32), rope_part, tail], -1)

    a_r = jnp.concatenate([cos, cos], -1)
    bm_r = jnp.concatenate([-sin, zeros], -1)
    bp_r = jnp.concatenate([zeros, sin], -1)
    q_tabs = tuple(block(r, v) for r, v in ((a_r, 1.0), (bm_r, 0.0), (bp_r, 0.0)))
    k_tabs = tuple(block(r, 0.0) for r in (a_r, bm_r, bp_r))
    return q_tabs, k_tabs


def _t5_bucket(dist):
    exact = REL_BUCKETS // 2
    d = jnp.maximum(dist, 1).astype(F32)
    large = exact + (jnp.log(d / exact) / math.log(REL_MAX_DIST / exact) * (REL_BUCKETS - exact)).astype(jnp.int32)
    large = jnp.minimum(large, REL_BUCKETS - 1)
    return jnp.where(dist < exact, dist, large)


def _swa_bucket_table():
    a = jnp.arange(BLOCK_Q)[:, None]
    col = jnp.arange(2 * BLOCK_Q)[None, :]
    return _t5_bucket(jnp.maximum(a + BLOCK_Q - col, 0)).astype(jnp.int32)


def _even_weights(W):
    w = W['ev_w_in']
    c_kv1 = MLA_Q_LORA + MLA_KV_LORA
    c_kr1 = c_kv1 + MLA_ROPE
    c_qs1 = c_kr1 + SWA_HEADS * HEAD_DIM
    zeros = lambda n: jnp.zeros((D_MODEL, n), w.dtype)
    w_in = jnp.concatenate([w[:, c_kr1:c_qs1], w[:, :c_kv1], w[:, c_qs1:], zeros(KR_LANE0), w[:, c_kv1:c_kr1],
                            zeros(LANES - KR_LANE0 - MLA_ROPE)], axis=1)
    uq = W['ev_w_uq'].reshape(MLA_Q_LORA, MLA_HEADS, MLA_QK)
    w_uq = jnp.pad(uq, ((0, 0), (0, 0), (0, LANES - MLA_QK))).reshape(MLA_Q_LORA, MLA_HEADS * LANES)
    ukv = W['ev_w_ukv'].reshape(MLA_KV_LORA, MLA_HEADS, MLA_NOPE + MLA_V)
    w_k = jnp.pad(ukv[..., :MLA_NOPE], ((0, 0), (0, 0), (0, LANES - MLA_NOPE))).reshape(MLA_KV_LORA, -1)
    w_v = ukv[..., MLA_NOPE:].reshape(MLA_KV_LORA, MLA_HEADS * MLA_V)
    return w_in, w_uq, w_k, w_v, W['ev_w_out']


def _even_in_grad_unpad(dw):
    kr0 = EV_KR[0] + KR_LANE0
    return jnp.concatenate([dw[:, EV_CQ[0]:EV_CKV[1]], dw[:, kr0:kr0 + MLA_ROPE], dw[:, EV_QS[0]:EV_QS[1]],
                            dw[:, EV_KS[0]:EV_VS[1]]], axis=1)


def _even_fwd(xb, W, P, i, B, S, tabs, xchg, tag):
    j = i // 2
    q_tabs, k_tabs, bias, sinkcol = tabs
    w_in, w_uq, w_k, w_v, w_out = _even_weights(W)
    h = _mm(xb, w_in, name=f"{tag}_in")
    cqn, ckvn, rq, rkv = _even_norms(h, P['ev_q_norm'][j][None], P['ev_kv_norm'][j][None], name=f"{tag}_norms")
    q = _rope(_mm(cqn, w_uq, name=f"{tag}_uq"), q_tabs, S, sign=1.0, name=f"{tag}_ropeq")
    knp = _mm(ckvn, w_k, out_dtypes=(BF16,), name=f"{tag}_uk")
    v = _mm(ckvn, w_v, out_dtypes=(BF16,), name=f"{tag}_uv")
    k = _mla_keys(knp, h, k_tabs, S, name=f"{tag}_keys")
    o_mla, lse_mla, got = _flash_fwd(q, k, v, q_blk0=0, k_blk0=0, v_blk0=0, W=2 * LANES, n_pairs=MLA_HEADS // 2,
                                     B=B, S=S, scale=MLA_QK ** -0.5, comm=xchg.fwd_items(i, 0), name=f"{tag}_mla")
    xchg.fwd_done(i, 0, got)
    o_swa, lse_swa, got = _swa_fwd(h, bias, sinkcol, B=B, S=S, comm=xchg.fwd_items(i, 1), name=f"{tag}_swa")
    xchg.fwd_done(i, 1, got)
    o_cat = jnp.concatenate([o_mla, o_swa], axis=-1)
    res = dict(h=h, cqn=cqn, ckvn=ckvn, rq=rq, rkv=rkv, q=q, k=k, v=v, o_mla=o_mla, lse_mla=lse_mla,
               o_swa=o_swa, lse_swa=lse_swa, o_cat=o_cat)
    return (o_cat, w_out), res


def _shift_prev(own, prev, B, S):
    prev = prev.reshape(B, S, LANES)
    shifted = jnp.concatenate([prev[:, BLOCK_Q:], jnp.zeros_like(prev[:, :BLOCK_Q])], axis=1)
    return (own + shifted.reshape(B * S, LANES)).astype(BF16)


def _even_bwd(dmb, dz1, xb, W, P, j, B, S, tabs, res, xchg, tag):
    q_tabs, k_tabs, bias, sinkcol = tabs
    w_in, w_uq, w_k, w_v, w_out = _even_weights(W)
    g = {}
    g['ev_w_out'] = _mm_tn(res['o_cat'], dmb, name=f"{tag}_dwout")
    do = _mm(dmb, w_out, trans_b=True, out_dtypes=(BF16,), name=f"{tag}_do")
    dq, dk, dv, got = _flash_bwd(res['q'], res['k'], res['v'], res['o_mla'], do, res['lse_mla'], q_blk0=0, k_blk0=0,
                                 v_blk0=0, do_blk0=0, W=2 * LANES, n_pairs=MLA_HEADS // 2, B=B, S=S,
                                 scale=MLA_QK ** -0.5, qk_dtype=F32, comm=xchg.bwd_items(), name=f"{tag}_mla_bwd")
    xchg.bwd_done(got)
    dq_pre = _rope(dq, q_tabs, S, sign=-1.0, name=f"{tag}_ropeq_bwd")
    dw_uq = _mm_tn(res['cqn'], dq_pre, name=f"{tag}_dwuq")
    g['ev_w_uq'] = dw_uq.reshape(MLA_Q_LORA, MLA_HEADS, LANES)[..., :MLA_QK].reshape(MLA_Q_LORA, MLA_HEADS * MLA_QK)
    dcqn = _mm(dq_pre, w_uq, trans_b=True, name=f"{tag}_dcqn")
    dw_k = _mm_tn(res['ckvn'], dk, name=f"{tag}_dwuk").reshape(MLA_KV_LORA, MLA_HEADS, LANES)[..., :MLA_NOPE]
    dw_v = _mm_tn(res['ckvn'], dv, name=f"{tag}_dwuv").reshape(MLA_KV_LORA, MLA_HEADS, MLA_V)
    g['ev_w_ukv'] = jnp.concatenate([dw_k, dw_v], axis=-1).reshape(MLA_KV_LORA, MLA_HEADS * (MLA_NOPE + MLA_V))
    dckvn_v = _mm(dv, w_v, trans_b=True, name=f"{tag}_dckvn_v")
    dckvn = _mm(dk, w_k, trans_b=True, extras=(dckvn_v,), epilogue=lambda acc, r: (acc + r,), name=f"{tag}_dckvn")
    dkr_pre = _mla_rope_key_grad(dk, k_tabs, S, name=f"{tag}_ropek_bwd")
    dqs, dko, dkp, dvo, dvp, dbias, dsink = _swa_bwd(res['h'], res['o_swa'], do, res['lse_swa'], bias, sinkcol,
                                                     do_blk0=1, B=B, S=S, name=f"{tag}_swa_bwd")
    dh, dgq, dgkv = _even_in_bwd(res['h'], res['rq'], res['rkv'], P['ev_q_norm'][j][None], P['ev_kv_norm'][j][None],
                                 dcqn, dckvn, dqs, _shift_prev(dko, dkp, B, S), _shift_prev(dvo, dvp, B, S), dkr_pre,
                                 name=f"{tag}_in_bwd")
    g['ev_w_in'] = _even_in_grad_unpad(_mm_tn(xb, dh, name=f"{tag}_dwin"))
    dx = _mm(dh, w_in, trans_b=True, extras=(dz1,), epilogue=lambda acc, r: (acc + DN_ALPHA * r,), name=f"{tag}_dx")
    small = dict(ev_q_norm=dgq[0], ev_kv_norm=dgkv[0], dbias=dbias, ev_sinks=jnp.sum(dsink, axis=(1, 2)))
    return dx, g, small


def _odd_fwd(xb, W, P, i, B, S, xchg, tag):
    j = i // 2
    w = W['od_w_in']
    w_qkv = w[:, :ODD_QKV]
    w_f = jnp.pad(w[:, ODD_QKV:], ((0, 0), (0, LANES - FOX_HEADS)))
    bf = jnp.pad(P['od_b_f'][j], (0, LANES - FOX_HEADS))[None]
    qkv = _mm(xb, w_qkv, out_dtypes=(BF16,), name=f"{tag}_qkv")
    f = _mm(xb, w_f, name=f"{tag}_f").reshape(B, S, LANES)
    csh, chs = _fox_decay_fwd(f, bf, name=f"{tag}_decay")
    crow = chs[:, :FOX_HEADS].reshape(B, FOX_HEADS, S // ATT_TILE, 1, ATT_TILE)
    n_blk = FOX_HEADS * HEAD_DIM // LANES
    o, lse, got = _flash_fwd(qkv, qkv, qkv, q_blk0=0, k_blk0=n_blk, v_blk0=2 * n_blk, W=LANES,
                             n_pairs=FOX_HEADS // 2, B=B, S=S, scale=HEAD_DIM ** -0.5, csh=csh, crow=crow,
                             comm=xchg.fwd_items(i, 0), name=f"{tag}_fox")
    xchg.fwd_done(i, 0, got)
    res = dict(f=f, bf=bf, csh=csh, crow=crow, qkv=qkv, o=o, lse=lse, w_qkv=w_qkv, w_f=w_f)
    return (o, W['od_w_out']), res


def _odd_bwd(dmb, dz1, xb, W, P, j, B, S, res, xchg, tag):
    g = {}
    w_out = W['od_w_out']
    g['od_w_out'] = _mm_tn(res['o'], dmb, name=f"{tag}_dwout")
    do = _mm(dmb, w_out, trans_b=True, out_dtypes=(BF16,), name=f"{tag}_do")
    qkv = res['qkv']
    n_blk = FOX_HEADS * HEAD_DIM // LANES
    dq, dk, dv, dck, dcq, got = _flash_bwd(qkv, qkv, qkv, res['o'], do, res['lse'], q_blk0=0, k_blk0=n_blk,
                                           v_blk0=2 * n_blk, do_blk0=0, W=LANES, n_pairs=FOX_HEADS // 2, B=B, S=S,
                                           scale=HEAD_DIM ** -0.5, qk_dtype=BF16, csh=res['csh'], crow=res['crow'],
                                           comm=xchg.bwd_items(), name=f"{tag}_fox_bwd")
    xchg.bwd_done(got)
    dc = dck.reshape(B, FOX_HEADS, S) + dcq.reshape(B, FOX_HEADS, S)
    dc_hs = jnp.pad(dc, ((0, 0), (0, LANES - FOX_HEADS), (0, 0)))
    df, dbf = _fox_decay_bwd(dc_hs, res['f'], res['bf'], name=f"{tag}_decay_bwd")
    df = df.reshape(B * S, LANES)
    dqkv = jnp.concatenate([dq, dk, dv], axis=-1)
    dw_qkv = _mm_tn(xb, dqkv, name=f"{tag}_dwqkv")
    dw_f = _mm_tn(xb, df, name=f"{tag}_dwf")
    g['od_w_in'] = jnp.concatenate([dw_qkv, dw_f[:, :FOX_HEADS]], axis=1)
    dxf = _mm(df, res['w_f'], trans_b=True, extras=(dz1,), epilogue=lambda acc, r: (acc + DN_ALPHA * r,),
              name=f"{tag}_dxf")
    dx = _mm(dqkv, res['w_qkv'], trans_b=True, extras=(dxf,), epilogue=lambda acc, r: (acc + r,), name=f"{tag}_dx")
    small = dict(od_b_f=dbf[0, :FOX_HEADS])
    return dx, g, small


def _local_step(x, p, target, P, xchg):
    B, S, D = x.shape
    T = B * S
    q_tabs, k_tabs = _rope_tables(S)
    bucket = _swa_bucket_table()
    in_bucket = (bucket[..., None] == jnp.arange(REL_BUCKETS)).astype(F32)
    bias = jnp.einsum('acb,bh->hac', in_bucket, P['rel_bias'], precision=lax.Precision.HIGHEST)

    xc = x.reshape(T, D)
    xcb = xc.astype(BF16)
    saved = []
    for i in range(DEPTH):
        j = i // 2
        tag = f"l{i}"
        W = xchg.layer_weights(i)
        lay = dict(xb=xcb, W=W)
        if i % 2 == 0:
            sinkcol = jnp.broadcast_to(P['ev_sinks'][j][:, None, None], (SWA_HEADS, BLOCK_Q, 1)).astype(F32)
            lay['tabs'] = (q_tabs, k_tabs, bias, sinkcol)
            (o, w_out), lay['mix'] = _even_fwd(xcb, W, P, i, B, S, lay['tabs'], xchg, tag)
        else:
            (o, w_out), lay['mix'] = _odd_fwd(xcb, W, P, i, B, S, xchg, tag)
        x1, x1b, lay['xh1'], lay['r1'] = _mm_ln(o, w_out, xc, P['ln1_g'][i][None], P['ln1_b'][i][None],
                                                name=f"{tag}_out_ln1")
        lay['x1b'] = x1b
        lay['u'], lay['a'] = _mm(x1b, W['w_up'], out_dtypes=(F32, BF16),
                                 epilogue=lambda acc: (acc, jnp.square(jnp.maximum(acc, 0.0))), name=f"{tag}_up")
        x2, x2b, lay['xh2'], lay['r2'] = _mm_ln(lay['a'], W['w_down'], x1, P['ln2_g'][i][None], P['ln2_b'][i][None],
                                                name=f"{tag}_down_ln2")
        lay['x2b'] = x2b
        lay['p'] = p[i].reshape(T, D_PLE)
        lay['e'] = _mm(lay['p'], W['ple_w_proj'], name=f"{tag}_ple_proj")

        def gate(acc, bg, e, x2v):
            gv = 1.0 / (1.0 + jnp.exp(-(acc + bg)))
            y = x2v + gv * e
            return y, y, gv

        xc, xcb, lay['g'] = _mm(x2b, W['ple_w_gate'], extras=(P['ple_b_gate'][i][None], lay['e'], x2),
                                epilogue=gate, out_dtypes=(F32, BF16, F32), name=f"{tag}_ple_gate")
        saved.append(lay)

    dy, sq = _loss_grad(xc, target.reshape(T, D), name="loss")

    Gs = {n: [None] * DEPTH for n in ('ln1_g', 'ln1_b', 'ln2_g', 'ln2_b', 'ple_b_gate')}
    Gs.update({n: [None] * (DEPTH // 2) for n in ('ev_q_norm', 'ev_kv_norm', 'ev_sinks', 'od_b_f')})
    dbias_total = None
    for i in reversed(range(DEPTH)):
        j = i // 2
        tag = f"l{i}b"
        lay = saved[i]
        W = lay['W']
        de, dzg, dbg = _ple_bwd_elem(dy, lay['g'], lay['e'], name=f"{tag}_ple_elem")
        Gs['ple_b_gate'][i] = dbg[0]
        g_mlp = {('ple_w_proj', i): _mm_tn(lay['p'], de, name=f"{tag}_dwproj"),
                 ('ple_w_gate', i): _mm_tn(lay['x2b'], dzg, name=f"{tag}_dwgate")}
        dx2 = _mm(dzg, W['ple_w_gate'], trans_b=True, extras=(dy,), epilogue=lambda acc, r: (acc + r,),
                  name=f"{tag}_dx2")
        dz2, dz2b, dg2, db2 = _ln_bwd(dx2, lay['xh2'], lay['r2'], P['ln2_g'][i][None], name=f"{tag}_ln2")
        Gs['ln2_g'][i], Gs['ln2_b'][i] = dg2[0], db2[0]
        g_mlp[('w_down', i)] = _mm_tn(lay['a'], dz2b, name=f"{tag}_dwdown")
        du = _mm(dz2b, W['w_down'], trans_b=True, extras=(lay['u'],), out_dtypes=(BF16,),
                 epilogue=lambda acc, u: (acc * (2.0 * jnp.maximum(u, 0.0)),), name=f"{tag}_du")
        g_mlp[('w_up', i)] = _mm_tn(lay['x1b'], du, name=f"{tag}_dwup")
        xchg.push_grads(g_mlp)
        dx1 = _mm(du, W['w_up'], trans_b=True, extras=(dz2,), epilogue=lambda acc, r: (acc + DN_ALPHA * r,),
                  name=f"{tag}_dx1")
        dz1, dz1b, dg1, db1 = _ln_bwd(dx1, lay['xh1'], lay['r1'], P['ln1_g'][i][None], name=f"{tag}_ln1")
        Gs['ln1_g'][i], Gs['ln1_b'][i] = dg1[0], db1[0]
        if i % 2 == 0:
            dy, g, small = _even_bwd(dz1b, dz1, lay['xb'], W, P, j, B, S, lay['tabs'], lay['mix'], xchg, tag)
            dbias_total = small['dbias'] if dbias_total is None else dbias_total + small['dbias']
            for n in ('ev_q_norm', 'ev_kv_norm', 'ev_sinks'):
                Gs[n][j] = small[n]
        else:
            dy, g, small = _odd_bwd(dz1b, dz1, lay['xb'], W, P, j, B, S, lay['mix'], xchg, tag)
            Gs['od_b_f'][j] = small['od_b_f']
        xchg.push_grads({(n, j): val for n, val in g.items()})

    grads_small = {n: jnp.stack(v) for n, v in Gs.items()}
    drel = _bias_bucket_sum(dbias_total, bucket, name="rel_bias_grad")
    grads_small['rel_bias'] = drel[:, :REL_BUCKETS].T
    return sq, dy.reshape(B, S, D), grads_small


def kernel(x, p, rel_bias, ev_w_in, ev_q_norm, ev_w_uq, ev_kv_norm, ev_w_ukv, ev_sinks, ev_w_out, od_w_in, od_b_f, od_w_out, ln1_g, ln1_b, w_up, w_down, ln2_g, ln2_b, ple_w_proj, ple_w_gate, ple_b_gate, loss_target, m_rel_bias, m_ev_w_in, m_ev_q_norm, m_ev_w_uq, m_ev_kv_norm, m_ev_w_ukv, m_ev_sinks, m_ev_w_out, m_od_w_in, m_od_b_f, m_od_w_out, m_ln1_g, m_ln1_b, m_w_up, m_w_down, m_ln2_g, m_ln2_b, m_ple_w_proj, m_ple_w_gate, m_ple_b_gate, v_rel_bias, v_ev_w_in, v_ev_q_norm, v_ev_w_uq, v_ev_kv_norm, v_ev_w_ukv, v_ev_sinks, v_ev_w_out, v_od_w_in, v_od_b_f, v_od_w_out, v_ln1_g, v_ln1_b, v_w_up, v_w_down, v_ln2_g, v_ln2_b, v_ple_w_proj, v_ple_w_gate, v_ple_b_gate):
    given = dict(locals())
    w = {n: given[n] for n in WEIGHTS}
    mom = {n: given["m_" + n] for n in WEIGHTS}
    var = {n: given["v_" + n] for n in WEIGHTS}
    small_shapes = {n: w[n].shape for n in SMALL}

    shards = {(n, idx): w[n][idx].astype(BF16) for n in BIG for idx in range(w[n].shape[0])}
    xchg = _MeshExchange(shards)
    P = {n: w[n] for n in SMALL}

    sq, grad_x, grads_small = _local_step(x, p, loss_target, P, xchg)
    loss = lax.psum(0.5 * jnp.sum(sq) / D_MODEL, ("x", "y", "c"))

    received = xchg.finish()
    g_small_packed = _all_reduce_small(_pack_small(grads_small), name="reduce_small_grads")
    g_small = _unpack_small(g_small_packed, small_shapes)

    grad, delta, new_m, new_v = {}, {}, {}, {}
    for n in BIG:
        per_layer = [_adamw_slots(w[n][idx], received[(n, idx)], mom[n][idx], var[n][idx], name=f"adamw_{n}{idx}")
                     for idx in range(w[n].shape[0])]
        grad[n], delta[n], new_m[n], new_v[n] = (jnp.stack(t) for t in zip(*per_layer))
    d, nm, nv = _adamw(_pack_small(w), g_small_packed, _pack_small(mom), _pack_small(var), name="adamw_small")
    d, nm, nv = (_unpack_small(t, small_shapes) for t in (d, nm, nv))
    for n in SMALL:
        grad[n], delta[n], new_m[n], new_v[n] = g_small[n], d[n], nm[n], nv[n]

    return (loss, grad_x, *[grad[n] for n in WEIGHTS], *[delta[n] for n in WEIGHTS],
            *[new_m[n] for n in WEIGHTS], *[new_v[n] for n in WEIGHTS])
```

```python
import math

import jax
import jax.numpy as jnp
from jax import lax
from jax.experimental import pallas as pl
from jax.experimental.pallas import tpu as pltpu

F32, BF16 = jnp.float32, jnp.bfloat16

D_MODEL = 1024
DEPTH = 4
HEAD_DIM = 64
MLA_HEADS, MLA_NOPE, MLA_ROPE, MLA_V = 8, 64, 32, 64
MLA_Q_LORA, MLA_KV_LORA = 384, 256
MLA_QK = MLA_NOPE + MLA_ROPE
ROPE_THETA = 10000.0
SWA_HEADS, SWA_KV_HEADS, SWA_WINDOW = 8, 2, 128
SWA_GROUP = SWA_HEADS // SWA_KV_HEADS
REL_BUCKETS, REL_MAX_DIST = 32, 128
FOX_HEADS = 16
D_FF = 4 * D_MODEL
D_PLE = 256
BLOCK_Q = 128
DN_ALPHA = (2 * DEPTH) ** 0.25
NORM_EPS = 1e-5
NEG_INF = -1e30
EVEN_IN = 1440
ODD_QKV = 3 * FOX_HEADS * HEAD_DIM
LANES = 128

EV_QS = (0, 512)
EV_CQ = (512, 896)
EV_CKV = (896, 1152)
EV_KS = (1152, 1280)
EV_VS = (1280, 1408)
EV_KR = (1408, 1536)
EVEN_IN_PAD = 1536
KR_LANE0 = MLA_NOPE

ADAM_LR, ADAM_B1, ADAM_B2, ADAM_EPS, ADAM_WD, ADAM_STEP = 0.001, 0.9, 0.999, 1e-08, 0.01, 10

N_DEV = 8
VMEM_LIMIT_BYTES = 48 * 1024 * 1024
ATT_TILE = 512
ATT_TILE_BWD = 512
PAIRS_PER_STEP_FWD = 4
PAIRS_PER_STEP_BWD = 2

NN = (((1,), (0,)), ((), ()))
NT = (((1,), (1,)), ((), ()))
TN = (((0,), (0,)), ((), ()))

BIG = ['ev_w_in', 'ev_w_uq', 'ev_w_ukv', 'ev_w_out', 'od_w_in', 'od_w_out', 'w_up', 'w_down',
       'ple_w_proj', 'ple_w_gate']
BIG_AXIS = {'ev_w_in': 2, 'ev_w_uq': 2, 'ev_w_ukv': 2, 'ev_w_out': 1, 'od_w_in': 2, 'od_w_out': 1,
            'w_up': 2, 'w_down': 1, 'ple_w_proj': 2, 'ple_w_gate': 1}
SMALL = ['rel_bias', 'ev_q_norm', 'ev_kv_norm', 'ev_sinks', 'od_b_f', 'ln1_g', 'ln1_b', 'ln2_g', 'ln2_b',
         'ple_b_gate']
WEIGHTS = ['rel_bias', 'ev_w_in', 'ev_q_norm', 'ev_w_uq', 'ev_kv_norm', 'ev_w_ukv', 'ev_sinks', 'ev_w_out',
           'od_w_in', 'od_b_f', 'od_w_out', 'ln1_g', 'ln1_b', 'w_up', 'w_down', 'ln2_g', 'ln2_b',
           'ple_w_proj', 'ple_w_gate', 'ple_b_gate']


def _cparams(*sem):
    return pltpu.CompilerParams(dimension_semantics=sem, vmem_limit_bytes=VMEM_LIMIT_BYTES)


def _pick(n, cands):
    for c in cands:
        if n % c == 0:
            return c
    return n


MM_STEP_BYTES = 10 * 1024 * 1024
MM_OUT_BYTES = 8 * 1024 * 1024
MM_CHUNK = 512


def _mm(a, b, *, trans_b=False, extras=(), epilogue=None, row_epilogue=None, out_dtypes=(F32,), out_widths=None,
        n_sums=0, name):
    M, K = a.shape
    N = b.shape[0] if trans_b else b.shape[1]
    n_ex, n_out = len(extras), len(out_dtypes)
    n_rows_out = n_out - n_sums
    out_widths = (N,) * n_out if out_widths is None else out_widths
    row_bytes = K * a.dtype.itemsize + (sum(w * jnp.dtype(d).itemsize
                                            for w, d in zip(out_widths[:n_rows_out], out_dtypes))
                                        + sum(e.shape[1] * e.dtype.itemsize for e in extras if e.shape[0] == M)
                                        + (4 * N if row_epilogue is not None else 0))
    tm = next((c for c in (1024, 512, 256) if M % c == 0 and c * row_bytes <= MM_STEP_BYTES), 128)
    nc = _pick(N, (MM_CHUNK, 384, 256, 128))

    def body(*refs):
        a_ref, b_ref = refs[:2]
        ex = refs[2:2 + n_ex]
        outs = refs[2 + n_ex:2 + n_ex + n_out]
        av = a_ref[...].astype(BF16)
        for n0 in range(0, N, nc):
            cols = slice(n0, n0 + nc)
            bv = (b_ref[cols, :] if trans_b else b_ref[:, cols]).astype(BF16)
            acc = lax.dot_general(av, bv, NT if trans_b else NN, preferred_element_type=F32)
            if row_epilogue is not None:
                refs[-1][:, cols] = acc
                continue
            res = epilogue(acc, *[e[:, cols] for e in ex]) if epilogue is not None else (acc,)
            for o, r in zip(outs, res):
                o[:, cols] = r.astype(o.dtype)
        if row_epilogue is not None:
            res = row_epilogue(refs[-1][...], *[e[...] for e in ex])
            for o, r in zip(outs[:n_rows_out], res):
                o[...] = r.astype(o.dtype)
            if n_sums:
                @pl.when(pl.program_id(0) == 0)
                def _():
                    for o in outs[n_rows_out:]:
                        o[...] = jnp.zeros_like(o)

                for o, r in zip(outs[n_rows_out:], res[n_rows_out:]):
                    o[...] += r

    in_specs = [pl.BlockSpec((tm, K), lambda i: (i, 0)), pl.BlockSpec(b.shape, lambda i: (0, 0))]
    for e in extras:
        if e.shape[0] == M:
            in_specs.append(pl.BlockSpec((tm, e.shape[1]), lambda i: (i, 0)))
        elif e.shape == (1, N):
            in_specs.append(pl.BlockSpec((1, N), lambda i: (0, 0)))
        else:
            raise ValueError(f"extra operand of shape {e.shape} for a ({M}, {N}) result")
    res = pl.pallas_call(
        body, name=name, grid=(M // tm,), in_specs=in_specs,
        out_specs=[pl.BlockSpec((tm, w), lambda i: (i, 0)) for w in out_widths[:n_rows_out]]
        + [pl.BlockSpec((1, w), lambda i: (0, 0)) for w in out_widths[n_rows_out:]],
        out_shape=[jax.ShapeDtypeStruct((M, w), d) for w, d in zip(out_widths[:n_rows_out], out_dtypes)]
        + [jax.ShapeDtypeStruct((1, w), d) for w, d in zip(out_widths[n_rows_out:], out_dtypes[n_rows_out:])],
        scratch_shapes=[pltpu.VMEM((tm, N), F32)] if row_epilogue is not None else [],
        compiler_params=_cparams("arbitrary" if n_sums else "parallel"),
    )(a, b, *extras)
    return res[0] if n_out == 1 else tuple(res)


def _mm_tn(a, b, *, name):
    T, K = a.shape
    N = b.shape[1]
    bk, bn = K, N
    while bk * bn * 4 > MM_OUT_BYTES:
        if bn >= bk and bn % (2 * LANES) == 0:
            bn //= 2
        else:
            bk //= 2
    tt = _pick(T, (1024, 512, 256))
    ck, cn = _pick(bk, (MM_CHUNK, 384, 256, 128)), _pick(bn, (MM_CHUNK, 384, 256, 128))

    def body(a_ref, b_ref, o_ref):
        t = pl.program_id(2)

        @pl.when(t == 0)
        def _():
            o_ref[...] = jnp.zeros_like(o_ref)

        for r0 in range(0, bk, ck):
            av = a_ref[:, r0:r0 + ck].astype(BF16)
            for c0 in range(0, bn, cn):
                o_ref[r0:r0 + ck, c0:c0 + cn] += lax.dot_general(
                    av, b_ref[:, c0:c0 + cn].astype(BF16), TN, preferred_element_type=F32)

    return pl.pallas_call(
        body, name=name, grid=(K // bk, N // bn, T // tt),
        in_specs=[pl.BlockSpec((tt, bk), lambda i, j, t: (t, i)), pl.BlockSpec((tt, bn), lambda i, j, t: (t, j))],
        out_specs=pl.BlockSpec((bk, bn), lambda i, j, t: (i, j)),
        out_shape=jax.ShapeDtypeStruct((K, N), F32),
        compiler_params=_cparams("parallel", "parallel", "arbitrary"),
    )(a, b)


ROW_TILE = 256


def _row_spec(cols, col_block=0):
    return pl.BlockSpec((ROW_TILE, cols), lambda i: (i, col_block))


def _tab_spec(cols, period):
    return pl.BlockSpec((ROW_TILE, cols), lambda i: (i % period, 0))


def _full_spec(shape):
    return pl.BlockSpec(shape, lambda i: (0,) * len(shape))


def _mm_ln(a, w, x, g, b, *, name):
    def ln_rows(m, xv, gv, bv):
        z = DN_ALPHA * xv + m
        mu = jnp.mean(z, -1, keepdims=True)
        zc = z - mu
        r = lax.rsqrt(jnp.mean(zc * zc, -1, keepdims=True) + NORM_EPS)
        xh = zc * r
        y = xh * gv + bv
        return y, y, xh, jnp.broadcast_to(r, (r.shape[0], LANES))

    D = w.shape[1]
    return _mm(a, w, extras=(x, g, b), row_epilogue=ln_rows, out_dtypes=(F32, BF16, F32, F32),
               out_widths=(D, D, D, LANES), name=name)


def _mm_ln_bwd(a, w, resid, resid_scale, xh, r, g, *, name):
    def ln_bwd_rows(acc, rv, xhv, rstd, gv):
        dyv = acc + resid_scale * rv
        dyg = dyv * gv
        c1 = jnp.mean(dyg, -1, keepdims=True)
        c2 = jnp.mean(dyg * xhv, -1, keepdims=True)
        dz = _widen(rstd, dyv.shape[-1]) * (dyg - c1 - xhv * c2)
        return dz, dz, jnp.sum(dyv * xhv, 0, keepdims=True), jnp.sum(dyv, 0, keepdims=True)

    D = w.shape[0]
    return _mm(a, w, trans_b=True, extras=(resid, xh, r, g), row_epilogue=ln_bwd_rows,
               out_dtypes=(F32, BF16, F32, F32), out_widths=(D, D, D, D), n_sums=2, name=name)


def _loss_grad(y, target, *, name):
    T, D = y.shape

    def body(y_ref, t_ref, dy_ref, sq_ref):
        err = y_ref[...] - t_ref[...]
        dy_ref[...] = err / D

        @pl.when(pl.program_id(0) == 0)
        def _():
            sq_ref[...] = jnp.zeros_like(sq_ref)

        sq_ref[...] += jnp.sum(err * err, 0, keepdims=True)

    return pl.pallas_call(
        body, name=name, grid=(T // ROW_TILE,),
        in_specs=[_row_spec(D), _row_spec(D)],
        out_specs=[_row_spec(D), _full_spec((1, D))],
        out_shape=[jax.ShapeDtypeStruct((T, D), F32), jax.ShapeDtypeStruct((1, D), F32)],
        compiler_params=_cparams("arbitrary"),
    )(y, target)


def _ple_bwd_elem(dx3, g, e, *, name):
    T, D = dx3.shape

    def body(dx_ref, g_ref, e_ref, de_ref, dz_ref, db_ref):
        dx, gv = dx_ref[...], g_ref[...]
        de_ref[...] = (dx * gv).astype(BF16)
        dz = dx * e_ref[...] * gv * (1.0 - gv)
        dz_ref[...] = dz.astype(BF16)

        @pl.when(pl.program_id(0) == 0)
        def _():
            db_ref[...] = jnp.zeros_like(db_ref)

        db_ref[...] += jnp.sum(dz, 0, keepdims=True)

    return pl.pallas_call(
        body, name=name, grid=(T // ROW_TILE,),
        in_specs=[_row_spec(D), _row_spec(D), _row_spec(D)],
        out_specs=[_row_spec(D), _row_spec(D), _full_spec((1, D))],
        out_shape=[jax.ShapeDtypeStruct((T, D), BF16), jax.ShapeDtypeStruct((T, D), BF16),
                   jax.ShapeDtypeStruct((1, D), F32)],
        compiler_params=_cparams("arbitrary"),
    )(dx3, g, e)


def _rotate(xv, a, bm, bp, sign):
    half = MLA_ROPE // 2
    width = xv.shape[-1]
    a, bm, bp = (_widen(t, width) for t in (a, bm, bp))
    return xv * a + sign * (pltpu.roll(xv, width - half, 1) * bm + pltpu.roll(xv, half, 1) * bp)


def _rope(x, tabs, seq, *, sign, name):
    T, width = x.shape

    def body(x_ref, a_ref, bm_ref, bp_ref, o_ref):
        o_ref[...] = _rotate(x_ref[...], a_ref[...], bm_ref[...], bp_ref[...], sign).astype(BF16)

    return pl.pallas_call(
        body, name=name, grid=(T // ROW_TILE,),
        in_specs=[_row_spec(width)] + [_tab_spec(LANES, seq // ROW_TILE)] * 3,
        out_specs=_row_spec(width),
        out_shape=jax.ShapeDtypeStruct((T, width), BF16),
        compiler_params=_cparams("parallel"),
    )(x, *tabs)


def _mla_keys(knp, h, k_tabs, seq, *, name):
    T = knp.shape[0]

    def body(k_ref, h_ref, a_ref, bm_ref, bp_ref, o_ref):
        kr = _rotate(h_ref[...], a_ref[...], bm_ref[...], bp_ref[...], 1.0)
        for hd in range(MLA_HEADS):
            cols = slice(hd * LANES, (hd + 1) * LANES)
            o_ref[:, cols] = (k_ref[:, cols].astype(F32) + kr).astype(BF16)

    return pl.pallas_call(
        body, name=name, grid=(T // ROW_TILE,),
        in_specs=[_row_spec(MLA_HEADS * LANES), _row_spec(LANES, EV_KR[0] // LANES)]
        + [_tab_spec(LANES, seq // ROW_TILE)] * 3,
        out_specs=_row_spec(MLA_HEADS * LANES),
        out_shape=jax.ShapeDtypeStruct((T, MLA_HEADS * LANES), BF16),
        compiler_params=_cparams("parallel"),
    )(knp, h, *k_tabs)


def _mla_rope_key_grad(dk, k_tabs, seq, *, name):
    T = dk.shape[0]

    def body(dk_ref, a_ref, bm_ref, bp_ref, o_ref):
        tot = dk_ref[:, 0:LANES]
        for hd in range(1, MLA_HEADS):
            tot = tot + dk_ref[:, hd * LANES:(hd + 1) * LANES]
        o_ref[...] = _rotate(tot, a_ref[...], bm_ref[...], bp_ref[...], -1.0).astype(BF16)

    return pl.pallas_call(
        body, name=name, grid=(T // ROW_TILE,),
        in_specs=[_row_spec(MLA_HEADS * LANES)] + [_tab_spec(LANES, seq // ROW_TILE)] * 3,
        out_specs=_row_spec(LANES),
        out_shape=jax.ShapeDtypeStruct((T, LANES), BF16),
        compiler_params=_cparams("parallel"),
    )(dk, *k_tabs)


def _even_norms(h, gq, gkv, *, name):
    T = h.shape[0]

    def body(h_ref, gq_ref, gkv_ref, cq_ref, ckv_ref, rq_ref, rkv_ref):
        cq = h_ref[:, EV_CQ[0]:EV_CQ[1]]
        rq = lax.rsqrt(jnp.mean(cq * cq, -1, keepdims=True) + NORM_EPS)
        cq_ref[...] = (cq * rq * gq_ref[...]).astype(BF16)
        rq_ref[...] = jnp.broadcast_to(rq, rq_ref.shape)
        ckv = h_ref[:, EV_CKV[0]:EV_CKV[1]]
        rkv = lax.rsqrt(jnp.mean(ckv * ckv, -1, keepdims=True) + NORM_EPS)
        ckv_ref[...] = (ckv * rkv * gkv_ref[...]).astype(BF16)
        rkv_ref[...] = jnp.broadcast_to(rkv, rkv_ref.shape)

    return pl.pallas_call(
        body, name=name, grid=(T // ROW_TILE,),
        in_specs=[_row_spec(EVEN_IN_PAD), _full_spec((1, MLA_Q_LORA)), _full_spec((1, MLA_KV_LORA))],
        out_specs=[_row_spec(MLA_Q_LORA), _row_spec(MLA_KV_LORA), _row_spec(LANES), _row_spec(LANES)],
        out_shape=[jax.ShapeDtypeStruct((T, MLA_Q_LORA), BF16), jax.ShapeDtypeStruct((T, MLA_KV_LORA), BF16),
                   jax.ShapeDtypeStruct((T, LANES), F32), jax.ShapeDtypeStruct((T, LANES), F32)],
        compiler_params=_cparams("parallel"),
    )(h, gq, gkv)


def _even_in_bwd(h, rq, rkv, gq, gkv, dcqn, dckvn, dqs, dks, dvs, dkr, *, name):
    T = h.shape[0]

    def rms_bwd(c, r, g, dy):
        r = _widen(r, c.shape[-1])
        xr = c * r
        dyg = dy * g
        return r * (dyg - xr * jnp.mean(dyg * xr, -1, keepdims=True)), jnp.sum(dy * xr, 0, keepdims=True)

    def body(h_ref, rq_ref, rkv_ref, gq_ref, gkv_ref, dcq_ref, dckv_ref, dqs_ref, dks_ref, dvs_ref, dkr_ref,
             dh_ref, dgq_ref, dgkv_ref):
        @pl.when(pl.program_id(0) == 0)
        def _():
            dgq_ref[...] = jnp.zeros_like(dgq_ref)
            dgkv_ref[...] = jnp.zeros_like(dgkv_ref)

        dcq, dgq = rms_bwd(h_ref[:, EV_CQ[0]:EV_CQ[1]], rq_ref[...], gq_ref[...], dcq_ref[...])
        dckv, dgkv = rms_bwd(h_ref[:, EV_CKV[0]:EV_CKV[1]], rkv_ref[...], gkv_ref[...], dckv_ref[...])
        dgq_ref[...] += dgq
        dgkv_ref[...] += dgkv
        dh_ref[:, EV_QS[0]:EV_QS[1]] = dqs_ref[...]
        dh_ref[:, EV_CQ[0]:EV_CQ[1]] = dcq.astype(BF16)
        dh_ref[:, EV_CKV[0]:EV_CKV[1]] = dckv.astype(BF16)
        dh_ref[:, EV_KS[0]:EV_KS[1]] = dks_ref[...]
        dh_ref[:, EV_VS[0]:EV_VS[1]] = dvs_ref[...]
        dh_ref[:, EV_KR[0]:EV_KR[1]] = dkr_ref[...]

    return pl.pallas_call(
        body, name=name, grid=(T // ROW_TILE,),
        in_specs=[_row_spec(EVEN_IN_PAD), _row_spec(LANES), _row_spec(LANES), _full_spec((1, MLA_Q_LORA)),
                  _full_spec((1, MLA_KV_LORA)), _row_spec(MLA_Q_LORA), _row_spec(MLA_KV_LORA),
                  _row_spec(SWA_HEADS * HEAD_DIM), _row_spec(LANES), _row_spec(LANES), _row_spec(LANES)],
        out_specs=[_row_spec(EVEN_IN_PAD), _full_spec((1, MLA_Q_LORA)), _full_spec((1, MLA_KV_LORA))],
        out_shape=[jax.ShapeDtypeStruct((T, EVEN_IN_PAD), BF16), jax.ShapeDtypeStruct((1, MLA_Q_LORA), F32),
                   jax.ShapeDtypeStruct((1, MLA_KV_LORA), F32)],
        compiler_params=_cparams("arbitrary"),
    )(h, rq, rkv, gq, gkv, dcqn, dckvn, dqs, dks, dvs, dkr)


def _fox_decay_fwd(f3, bf, *, name):
    B, S, _ = f3.shape

    def body(f_ref, b_ref, csh_ref, chs_ref):
        x = f_ref[...] + b_ref[...]
        c = jnp.minimum(x, 0.0) - jnp.log1p(jnp.exp(-jnp.abs(x)))
        row = lax.broadcasted_iota(jnp.int32, (S, LANES), 0)
        k = 1
        while k < S:
            c = c + jnp.where(row >= k, pltpu.roll(c, k, 0), 0.0)
            k *= 2
        csh_ref[...] = c
        chs_ref[...] = c.T

    return pl.pallas_call(
        body, name=name, grid=(B,),
        in_specs=[pl.BlockSpec((None, S, LANES), lambda b: (b, 0, 0)), pl.BlockSpec((1, LANES), lambda b: (0, 0))],
        out_specs=[pl.BlockSpec((None, S, LANES), lambda b: (b, 0, 0)),
                   pl.BlockSpec((None, LANES, S), lambda b: (b, 0, 0))],
        out_shape=[jax.ShapeDtypeStruct((B, S, LANES), F32), jax.ShapeDtypeStruct((B, LANES, S), F32)],
        compiler_params=_cparams("parallel"),
    )(f3, bf)


def _fox_decay_bwd(dc_hs, f3, bf, *, name):
    B, S, _ = f3.shape

    def body(dc_ref, f_ref, b_ref, df_ref, db_ref):
        g = dc_ref[...].T
        row = lax.broadcasted_iota(jnp.int32, (S, LANES), 0)
        k = 1
        while k < S:
            g = g + jnp.where(row < S - k, pltpu.roll(g, S - k, 0), 0.0)
            k *= 2
        x = f_ref[...] + b_ref[...]
        df = g * (1.0 / (1.0 + jnp.exp(x)))
        df_ref[...] = df.astype(BF16)

        @pl.when(pl.program_id(0) == 0)
        def _():
            db_ref[...] = jnp.zeros_like(db_ref)

        db_ref[...] += jnp.sum(df, 0, keepdims=True)

    return pl.pallas_call(
        body, name=name, grid=(B,),
        in_specs=[pl.BlockSpec((None, LANES, S), lambda b: (b, 0, 0)),
                  pl.BlockSpec((None, S, LANES), lambda b: (b, 0, 0)), pl.BlockSpec((1, LANES), lambda b: (0, 0))],
        out_specs=[pl.BlockSpec((None, S, LANES), lambda b: (b, 0, 0)), pl.BlockSpec((1, LANES), lambda b: (0, 0))],
        out_shape=[jax.ShapeDtypeStruct((B, S, LANES), BF16), jax.ShapeDtypeStruct((1, LANES), F32)],
        compiler_params=_cparams("arbitrary"),
    )(dc_hs, f3, bf)


def _head_column(block, h):
    lane = lax.broadcasted_iota(jnp.int32, block.shape, 1)
    return jnp.sum(jnp.where(lane == h, block, 0.0), axis=-1, keepdims=True)


def _causal_mask(s):
    r = lax.broadcasted_iota(jnp.int32, s.shape, 0)
    c = lax.broadcasted_iota(jnp.int32, s.shape, 1)
    return jnp.where(c <= r, s, NEG_INF)


def _low_half(shape):
    return (lax.broadcasted_iota(jnp.int32, shape, 1) % LANES) < HEAD_DIM


def _widen(x, cols):
    return jnp.concatenate([x] * (cols // LANES), axis=1)


def _both_halves(x, lo):
    r = pltpu.roll(x, HEAD_DIM, 1)
    return jnp.where(lo, x, r), jnp.where(lo, r, x)


MESH_ID = pl.DeviceIdType.MESH
HBM_SPEC = pl.BlockSpec(memory_space=pltpu.HBM)
VMEM_SPEC = pl.BlockSpec(memory_space=pltpu.VMEM)


def _mesh_place():
    x, y, c = lax.axis_index("x"), lax.axis_index("y"), lax.axis_index("c")
    return x, y, c, 4 * x + 2 * y + c


def _peers(x, y, c):
    out = []
    for mask in range(1, N_DEV):
        dx, dy, dc = (mask >> 2) & 1, (mask >> 1) & 1, mask & 1
        px, py, pc = (1 - x if dx else x), (1 - y if dy else y), (1 - c if dc else c)
        out.append(((px, py, pc), 4 * px + 2 * py + pc))
    return out


def _comm_out_shapes(comm):
    return [jax.ShapeDtypeStruct(((N_DEV,) + a.shape) if kind == "gather" else a.shape, a.dtype) for kind, a in comm]


def _comm_scratch(comm):
    n = len(comm)
    return [pltpu.SemaphoreType.DMA((n, 7)), pltpu.SemaphoreType.DMA((n, 7)), pltpu.SemaphoreType.DMA((n,))]


def _comm_copies(kinds, in_refs, out_refs, sems, place):
    send_sems, recv_sems, local_sems = sems
    x, y, c, me = place
    local, remote = [], []
    for w, kind in enumerate(kinds):
        mine = in_refs[w] if kind == "gather" else in_refs[w].at[me]
        local.append(pltpu.make_async_copy(mine, out_refs[w].at[me], local_sems.at[w]))
        for k, (peer, peer_idx) in enumerate(_peers(x, y, c)):
            remote.append(pltpu.make_async_remote_copy(
                src_ref=in_refs[w] if kind == "gather" else in_refs[w].at[peer_idx], dst_ref=out_refs[w].at[me],
                send_sem=send_sems.at[w, k], recv_sem=recv_sems.at[w, k], device_id=peer, device_id_type=MESH_ID))
    return local, remote


def _comm_start(kinds, in_refs, out_refs, sems, place):
    local, remote = _comm_copies(kinds, in_refs, out_refs, sems, place)
    for cp in local + remote:
        cp.start()


def _comm_wait(kinds, in_refs, out_refs, sems, place):
    local, remote = _comm_copies(kinds, in_refs, out_refs, sems, place)
    for cp in remote:
        cp.wait_recv()
    for cp in remote:
        cp.wait_send()
    for cp in local:
        cp.wait()


def _exchange(comm, *, name):
    n = len(comm)
    kinds = [k for k, _ in comm]

    def body(*refs):
        place = _mesh_place()
        _comm_start(kinds, refs[:n], refs[n:2 * n], refs[2 * n:], place)
        _comm_wait(kinds, refs[:n], refs[n:2 * n], refs[2 * n:], place)

    return pl.pallas_call(
        body, name=name, out_shape=_comm_out_shapes(comm), in_specs=[HBM_SPEC] * n, out_specs=[HBM_SPEC] * n,
        scratch_shapes=_comm_scratch(comm),
    )(*[a for _, a in comm])


def _flash_fwd(qa, ka, va, *, q_blk0, k_blk0, v_blk0, W, n_pairs, B, S, scale, csh=None, crow=None, comm=(), name):
    t = ATT_TILE
    nq = S // t
    P = PAIRS_PER_STEP_FWD
    decay = csh is not None
    split = W == LANES
    assert n_pairs % P == 0 and q_blk0 % P == 0 and k_blk0 % P == 0 and v_blk0 % P == 0
    n_c, kinds = len(comm), [k for k, _ in comm]
    n_in = 5 if decay else 3
    n_steps = (B, n_pairs // P, nq)

    def body(*refs):
        c_in, c_out = refs[n_in:n_in + n_c], refs[n_in + n_c + 2:n_in + 2 * n_c + 2]
        sems = refs[n_in + 2 * n_c + 4:]
        refs = refs[:n_in] + refs[n_in + n_c:n_in + n_c + 2] + refs[n_in + 2 * n_c + 2:n_in + 2 * n_c + 4]
        if decay:
            q_ref, k_ref, v_ref, csh_ref, crow_ref, o_ref, lse_ref, m_s, acc_s = refs
        else:
            q_ref, k_ref, v_ref, o_ref, lse_ref, m_s, acc_s = refs
        g, i = pl.program_id(1), pl.program_id(2)
        if n_c:
            place = _mesh_place()
            ids = [pl.program_id(ax) for ax in range(3)]

            @pl.when((ids[0] == 0) & (ids[1] == 0) & (ids[2] == 0))
            def _():
                _comm_start(kinds, c_in, c_out, sems, place)

        lo = _low_half((t, LANES))
        qv = q_ref[...]
        qh = []
        for pr in range(P):
            qp = qv[:, pr * W:(pr + 1) * W]
            qh += [jnp.where(lo, qp, jnp.zeros_like(qp)), jnp.where(lo, jnp.zeros_like(qp), qp)] if split \
                else [qp[:, :LANES], qp[:, LANES:]]
        if decay:
            cq = [jnp.broadcast_to(_head_column(csh_ref[...], 2 * P * g + hd), (t, LANES)) for hd in range(2 * P)]
        m_s[...] = jnp.full(m_s.shape, NEG_INF, F32)
        acc_s[...] = jnp.zeros(acc_s.shape, F32)

        def step(j, masked):
            rows = pl.ds(pl.multiple_of(j * t, t), t)
            kb, vb = k_ref[rows, :], v_ref[rows, :]
            for pr in range(P):
                kp, vp = kb[:, pr * W:(pr + 1) * W], vb[:, pr * LANES:(pr + 1) * LANES]
                ones = jnp.ones_like(vp)
                vaug = [jnp.where(lo, vp, ones), jnp.where(lo, ones, vp)]
                for half in range(2):
                    hd = 2 * pr + half
                    kh = kp if split else kp[:, half * LANES:(half + 1) * LANES]
                    s = lax.dot_general(qh[hd], kh, NT, preferred_element_type=F32) * scale
                    if decay:
                        s = s + _widen(cq[hd], t) - crow_ref[hd, j]
                    if masked:
                        s = _causal_mask(s)
                    m_prev = m_s[hd]
                    m_new = jnp.maximum(m_prev, jnp.max(s, -1, keepdims=True))
                    p = jnp.exp(s - _widen(m_new, t))
                    acc_s[hd] = jnp.exp(m_prev - m_new) * acc_s[hd] + lax.dot_general(
                        p.astype(BF16), vaug[half], NN, preferred_element_type=F32)
                    m_s[hd] = m_new

        def loop_body(j, carry):
            step(j, False)
            return carry

        lax.fori_loop(0, i, loop_body, 0)
        step(i, True)
        for pr in range(P):
            acc0, acc1 = acc_s[2 * pr], acc_s[2 * pr + 1]
            _, l0 = _both_halves(acc0, lo)
            l1, _ = _both_halves(acc1, lo)
            cols = slice(pr * LANES, (pr + 1) * LANES)
            o_ref[:, cols] = jnp.where(lo, acc0 / l0, acc1 / l1).astype(BF16)
            lse_ref[:, cols] = jnp.where(lo, m_s[2 * pr] + jnp.log(l0), m_s[2 * pr + 1] + jnp.log(l1))
        if n_c:
            @pl.when((ids[0] == n_steps[0] - 1) & (ids[1] == n_steps[1] - 1) & (ids[2] == n_steps[2] - 1))
            def _():
                _comm_wait(kinds, c_in, c_out, sems, place)

    in_specs = [pl.BlockSpec((t, P * W), lambda b, g, i: (b * nq + i, q_blk0 // P + g)),
                pl.BlockSpec((S, P * W), lambda b, g, i: (b, k_blk0 // P + g)),
                pl.BlockSpec((S, P * LANES), lambda b, g, i: (b, v_blk0 // P + g))]
    args = [qa, ka, va]
    if decay:
        in_specs += [pl.BlockSpec((None, t, LANES), lambda b, g, i: (b, i, 0)),
                     pl.BlockSpec((None, 2 * P, nq, 1, t), lambda b, g, i: (b, g, 0, 0, 0))]
        args += [csh, crow]
    out_spec = pl.BlockSpec((t, P * LANES), lambda b, g, i: (b * nq + i, g))
    res = pl.pallas_call(
        body, name=name, grid=n_steps, in_specs=in_specs + [HBM_SPEC] * n_c,
        out_specs=[out_spec, out_spec] + [HBM_SPEC] * n_c,
        out_shape=[jax.ShapeDtypeStruct((B * S, n_pairs * LANES), BF16),
                   jax.ShapeDtypeStruct((B * S, n_pairs * LANES), F32)] + _comm_out_shapes(comm),
        scratch_shapes=[pltpu.VMEM((2 * P, t, LANES), F32), pltpu.VMEM((2 * P, t, LANES), F32)]
        + (_comm_scratch(comm) if n_c else []),
        compiler_params=_cparams(*(("arbitrary",) * 3 if n_c else ("parallel",) * 3)),
    )(*args, *[a for _, a in comm])
    return res[0], res[1], list(res[2:])


def _flash_bwd(qa, ka, va, oa, doa, lsea, *, q_blk0, k_blk0, v_blk0, do_blk0, W, n_pairs, B, S, scale, qk_dtype,
               csh=None, crow=None, comm=(), name):
    t = ATT_TILE_BWD
    nq = S // t
    P = PAIRS_PER_STEP_BWD
    decay = csh is not None
    if decay:
        crow = crow.reshape(B, 2 * n_pairs, nq, 1, t)
    split = W == LANES
    assert n_pairs % P == 0 and q_blk0 % P == 0 and k_blk0 % P == 0 and v_blk0 % P == 0 and do_blk0 % P == 0
    n_c, kinds = len(comm), [k for k, _ in comm]
    n_in, n_out, n_scr = (8, 5, 8) if decay else (6, 3, 5)
    n_steps = (B, n_pairs // P, nq)

    def body(*refs):
        c_in = refs[n_in:n_in + n_c]
        c_out = refs[n_in + n_c + n_out:n_in + 2 * n_c + n_out]
        sems = refs[n_in + 2 * n_c + n_out + n_scr:]
        refs = (refs[:n_in] + refs[n_in + n_c:n_in + n_c + n_out]
                + refs[n_in + 2 * n_c + n_out:n_in + 2 * n_c + n_out + n_scr])
        if n_c:
            place = _mesh_place()
            ids = [pl.program_id(ax) for ax in range(3)]

            @pl.when((ids[0] == 0) & (ids[1] == 0) & (ids[2] == 0))
            def _():
                _comm_start(kinds, c_in, c_out, sems, place)

        if decay:
            (q_ref, k_ref, v_ref, o_ref, do_ref, lse_ref, csh_ref, crow_ref, dq_ref, dk_ref, dv_ref, dck_ref, dcq_ref,
             dq_s, lse_s, delta_s, dk_s, dv_s, cq_s, dcq_s, dck_s) = refs
        else:
            (q_ref, k_ref, v_ref, o_ref, do_ref, lse_ref, dq_ref, dk_ref, dv_ref,
             dq_s, lse_s, delta_s, dk_s, dv_s) = refs
        g, j = pl.program_id(1), pl.program_id(2)
        lo = _low_half((t, LANES))

        @pl.when(j == 0)
        def _():
            lo_s = _low_half((S, LANES))
            dq_s[...] = jnp.zeros(dq_s.shape, F32)
            for pr in range(P):
                cols = slice(pr * LANES, (pr + 1) * LANES)
                lse_s[2 * pr], lse_s[2 * pr + 1] = _both_halves(lse_ref[:, cols], lo_s)
                dd = do_ref[:, cols].astype(F32) * o_ref[:, cols].astype(F32)
                delta_s[2 * pr] = jnp.broadcast_to(jnp.sum(jnp.where(lo_s, dd, 0.0), -1, keepdims=True), (S, LANES))
                delta_s[2 * pr + 1] = jnp.broadcast_to(jnp.sum(jnp.where(lo_s, 0.0, dd), -1, keepdims=True),
                                                       (S, LANES))
            if decay:
                for hd in range(2 * P):
                    cq_s[hd] = jnp.broadcast_to(_head_column(csh_ref[...], 2 * P * g + hd), (S, LANES))
                dcq_s[...] = jnp.zeros(dcq_s.shape, F32)

        kb, vb = k_ref[...], v_ref[...]
        kh, vh = [], []
        for pr in range(P):
            kp, vp = kb[:, pr * W:(pr + 1) * W], vb[:, pr * LANES:(pr + 1) * LANES]
            zk, zv = jnp.zeros_like(kp), jnp.zeros_like(vp)
            kh += [jnp.where(lo, kp, zk), jnp.where(lo, zk, kp)] if split else [kp[:, :LANES], kp[:, LANES:]]
            vh += [jnp.where(lo, vp, zv), jnp.where(lo, zv, vp)]
        dk_s[...] = jnp.zeros(dk_s.shape, F32)
        dv_s[...] = jnp.zeros(dv_s.shape, F32)
        if decay:
            dck_s[...] = jnp.zeros(dck_s.shape, F32)

        def step(i, masked):
            rows = pl.ds(pl.multiple_of(i * t, t), t)
            qi, doi = q_ref[rows, :], do_ref[rows, :]
            for pr in range(P):
                qp, dop = qi[:, pr * W:(pr + 1) * W], doi[:, pr * LANES:(pr + 1) * LANES]
                for half in range(2):
                    hd = 2 * pr + half
                    qx = qp if split else qp[:, half * LANES:(half + 1) * LANES]
                    s = lax.dot_general(qx, kh[hd], NT, preferred_element_type=F32) * scale
                    if decay:
                        s = s + _widen(cq_s[hd, rows, :], t) - crow_ref[hd, j]
                    if masked:
                        s = _causal_mask(s)
                    p = jnp.exp(s - _widen(lse_s[hd, rows, :], t))
                    dv_s[hd] += lax.dot_general(p.astype(BF16), dop, TN, preferred_element_type=F32)
                    dp = lax.dot_general(dop, vh[hd], NT, preferred_element_type=F32)
                    ds = p * (dp - _widen(delta_s[hd, rows, :], t))
                    dss = (ds * scale).astype(BF16)
                    dk_s[hd] += lax.dot_general(dss, qx, TN, preferred_element_type=F32)
                    dqc = lax.dot_general(dss, kh[hd], NN, preferred_element_type=F32)
                    if split:
                        dq_s[rows, pr * W:(pr + 1) * W] += dqc
                    else:
                        dq_s[rows, hd * LANES:(hd + 1) * LANES] += dqc
                    if decay:
                        dck_s[hd] -= jnp.sum(ds, 0, keepdims=True)
                        part = ds[:, :LANES]
                        for c in range(1, t // LANES):
                            part = part + ds[:, c * LANES:(c + 1) * LANES]
                        dcq_s[hd, rows, :] += part

        def loop_body(i, carry):
            step(i, False)
            return carry

        step(j, True)
        lax.fori_loop(j + 1, nq, loop_body, 0)
        for pr in range(P):
            if split:
                dk_ref[:, pr * W:(pr + 1) * W] = jnp.where(lo, dk_s[2 * pr], dk_s[2 * pr + 1]).astype(dk_ref.dtype)
            else:
                for half in range(2):
                    hd = 2 * pr + half
                    dk_ref[:, hd * LANES:(hd + 1) * LANES] = dk_s[hd].astype(dk_ref.dtype)
            dv_ref[:, pr * LANES:(pr + 1) * LANES] = jnp.where(lo, dv_s[2 * pr], dv_s[2 * pr + 1]).astype(BF16)
        if decay:
            dck_ref[...] = dck_s[...]

        @pl.when(j == nq - 1)
        def _():
            dq_ref[...] = dq_s[...].astype(dq_ref.dtype)
            if decay:
                for hd in range(2 * P):
                    dcq_ref[hd] = jnp.sum(dcq_s[hd].T, 0, keepdims=True)

        if n_c:
            @pl.when((ids[0] == n_steps[0] - 1) & (ids[1] == n_steps[1] - 1) & (ids[2] == n_steps[2] - 1))
            def _():
                _comm_wait(kinds, c_in, c_out, sems, place)

    full = lambda w, blk0: pl.BlockSpec((S, P * w), lambda b, g, j: (b, blk0 // P + g))
    blk = lambda w, blk0: pl.BlockSpec((t, P * w), lambda b, g, j: (b * nq + j, blk0 // P + g))
    in_specs = [full(W, q_blk0), blk(W, k_blk0), blk(LANES, v_blk0), full(LANES, 0), full(LANES, do_blk0),
                full(LANES, 0)]
    args = [qa, ka, va, oa, doa, lsea]
    T = B * S
    out_specs = [full(W, 0), blk(W, 0), blk(LANES, 0)]
    out_shape = [jax.ShapeDtypeStruct((T, n_pairs * W), qk_dtype), jax.ShapeDtypeStruct((T, n_pairs * W), qk_dtype),
                 jax.ShapeDtypeStruct((T, n_pairs * LANES), BF16)]
    per_head = lambda rows: pltpu.VMEM((2 * P, rows, LANES), F32)
    scratch = [pltpu.VMEM((S, P * W), F32), per_head(S), per_head(S), per_head(t), per_head(t)]
    if decay:
        in_specs += [pl.BlockSpec((None, S, LANES), lambda b, g, j: (b, 0, 0)),
                     pl.BlockSpec((None, 2 * P, nq, 1, t), lambda b, g, j: (b, g, 0, 0, 0))]
        args += [csh, crow]
        out_specs += [pl.BlockSpec((None, 2 * P, None, 1, t), lambda b, g, j: (b, g, j, 0, 0)),
                      pl.BlockSpec((None, 2 * P, 1, S), lambda b, g, j: (b, g, 0, 0))]
        out_shape += [jax.ShapeDtypeStruct((B, 2 * n_pairs, nq, 1, t), F32),
                      jax.ShapeDtypeStruct((B, 2 * n_pairs, 1, S), F32)]
        scratch += [per_head(S), per_head(S), pltpu.VMEM((2 * P, 1, t), F32)]
    res = pl.pallas_call(
        body, name=name, grid=n_steps, in_specs=in_specs + [HBM_SPEC] * n_c,
        out_specs=out_specs + [HBM_SPEC] * n_c, out_shape=out_shape + _comm_out_shapes(comm),
        scratch_shapes=scratch + (_comm_scratch(comm) if n_c else []),
        compiler_params=_cparams(*(("arbitrary",) * 3 if n_c else ("parallel", "parallel", "arbitrary"))),
    )(*args, *[a for _, a in comm])
    return tuple(res[:n_out]) + (list(res[n_out:]),)


def _swa_common(q_ref, kp_ref, ko_ref, vp_ref, vo_ref, n):
    Q = BLOCK_Q
    lo = _low_half((Q, LANES))
    lo2 = _low_half((2 * Q, LANES))
    kk = jnp.concatenate([kp_ref[...], ko_ref[...]], axis=0)
    vv = jnp.concatenate([vp_ref[...], vo_ref[...]], axis=0)
    kdup = [x.astype(BF16) for x in _both_halves(kk, lo2)]
    vdup = [x.astype(BF16) for x in _both_halves(vv, lo2)]
    a = lax.broadcasted_iota(jnp.int32, (SWA_GROUP * Q, 2 * Q), 0) % Q
    col = lax.broadcasted_iota(jnp.int32, (SWA_GROUP * Q, 2 * Q), 1)
    dist = a + Q - col
    valid = (dist >= 0) & (dist < SWA_WINDOW) & ((col >= Q) | (n > 0))
    qv = q_ref[...]
    qm = []
    for a_head in range(SWA_HEADS):
        qp = qv[:, (a_head // 2) * LANES:(a_head // 2 + 1) * LANES]
        keep = lo if a_head % 2 == 0 else jnp.logical_not(lo)
        qm.append(jnp.where(keep, qp, 0.0).astype(BF16))
    qs = [jnp.concatenate(qm[g * SWA_GROUP:(g + 1) * SWA_GROUP], axis=0) for g in range(SWA_KV_HEADS)]
    return lo, lo2, kdup, vdup, valid, qs


def _swa_group_logits(g, qs, kdup, valid, bias_ref):
    heads = slice(g * SWA_GROUP, (g + 1) * SWA_GROUP)
    s = lax.dot_general(qs[g], kdup[g], NT, preferred_element_type=F32) * (HEAD_DIM ** -0.5)
    s = s + bias_ref[heads].reshape(SWA_GROUP * BLOCK_Q, 2 * BLOCK_Q)
    return heads, jnp.where(valid, s, NEG_INF)


def _pair_halves(x, lo):
    Q = BLOCK_Q
    return [jnp.where(lo, x[2 * pr * Q:(2 * pr + 1) * Q], x[(2 * pr + 1) * Q:(2 * pr + 2) * Q])
            for pr in range(SWA_GROUP // 2)]


def _swa_in_specs(nb):
    Q = BLOCK_Q
    own = lambda blk: (lambda b, n: (b * nb + n, blk))
    prev = lambda blk: (lambda b, n: (b * nb + jnp.maximum(n - 1, 0), blk))
    kb, vb = EV_KS[0] // LANES, EV_VS[0] // LANES
    return [pl.BlockSpec((Q, SWA_HEADS * HEAD_DIM), own(0)), pl.BlockSpec((Q, LANES), prev(kb)),
            pl.BlockSpec((Q, LANES), own(kb)), pl.BlockSpec((Q, LANES), prev(vb)), pl.BlockSpec((Q, LANES), own(vb))]


def _swa_fwd(h, bias, sinkcol, *, B, S, comm=(), name):
    Q = BLOCK_Q
    nb = S // Q
    n_c, kinds = len(comm), [k for k, _ in comm]

    def body(*refs):
        c_in, c_out, sems = refs[7:7 + n_c], refs[9 + n_c:9 + 2 * n_c], refs[9 + 2 * n_c:]
        q_ref, kp_ref, ko_ref, vp_ref, vo_ref, bias_ref, sink_ref = refs[:7]
        o_ref, lse_ref = refs[7 + n_c:9 + n_c]
        if n_c:
            place = _mesh_place()
            ids = [pl.program_id(0), pl.program_id(1)]

            @pl.when((ids[0] == 0) & (ids[1] == 0))
            def _():
                _comm_start(kinds, c_in, c_out, sems, place)

        lo, lo2, kdup, vdup, valid, qs = _swa_common(q_ref, kp_ref, ko_ref, vp_ref, vo_ref, pl.program_id(1))
        lane = lax.broadcasted_iota(jnp.int32, (Q, LANES), 1)
        lse_blk = jnp.zeros((Q, LANES), F32)
        pairs = []
        lo4 = _low_half((SWA_GROUP * Q, LANES))
        for g in range(SWA_KV_HEADS):
            heads, s = _swa_group_logits(g, qs, kdup, valid, bias_ref)
            sink = jnp.broadcast_to(sink_ref[heads].reshape(SWA_GROUP * Q, 1), (SWA_GROUP * Q, LANES))
            m = jnp.maximum(jnp.max(s, -1, keepdims=True), sink)
            p = jnp.exp(s - _widen(m, 2 * Q))
            vaug = jnp.where(lo2, vdup[g], jnp.ones_like(vdup[g]))
            pv = lax.dot_general(p.astype(BF16), vaug, NN, preferred_element_type=F32)
            rolled = pltpu.roll(pv, HEAD_DIM, 1)
            l = jnp.where(lo4, rolled, pv) + jnp.exp(sink - m)
            out = pv / l
            lse_g = m + jnp.log(l)
            for i in range(SWA_GROUP):
                lse_blk = jnp.where(lane == g * SWA_GROUP + i, lse_g[i * Q:(i + 1) * Q], lse_blk)
            shifted = pltpu.roll(out, HEAD_DIM, 1)
            pairs += [jnp.where(lo, out[2 * pr * Q:(2 * pr + 1) * Q], shifted[(2 * pr + 1) * Q:(2 * pr + 2) * Q])
                      for pr in range(SWA_GROUP // 2)]
        o_ref[...] = jnp.concatenate(pairs, axis=1).astype(BF16)
        lse_ref[...] = lse_blk
        if n_c:
            @pl.when((ids[0] == B - 1) & (ids[1] == nb - 1))
            def _():
                _comm_wait(kinds, c_in, c_out, sems, place)

    whole = lambda shape: pl.BlockSpec(shape, lambda b, n: (0,) * len(shape))
    res = pl.pallas_call(
        body, name=name, grid=(B, nb),
        in_specs=_swa_in_specs(nb) + [whole((SWA_HEADS, Q, 2 * Q)), whole((SWA_HEADS, Q, 1))] + [HBM_SPEC] * n_c,
        out_specs=[pl.BlockSpec((Q, SWA_HEADS * HEAD_DIM), lambda b, n: (b * nb + n, 0)),
                   pl.BlockSpec((Q, LANES), lambda b, n: (b * nb + n, 0))] + [HBM_SPEC] * n_c,
        out_shape=[jax.ShapeDtypeStruct((B * S, SWA_HEADS * HEAD_DIM), BF16),
                   jax.ShapeDtypeStruct((B * S, LANES), F32)] + _comm_out_shapes(comm),
        scratch_shapes=_comm_scratch(comm) if n_c else [],
        compiler_params=_cparams(*(("arbitrary",) * 2 if n_c else ("parallel",) * 2)),
    )(h, h, h, h, h, bias, sinkcol, *[a for _, a in comm])
    return res[0], res[1], list(res[2:])


def _swa_bwd(h, o, do, lse, bias, sinkcol, *, do_blk0, B, S, name):
    Q = BLOCK_Q
    nb = S // Q
    scale = HEAD_DIM ** -0.5

    def body(q_ref, kp_ref, ko_ref, vp_ref, vo_ref, o_ref, do_ref, lse_ref, bias_ref, sink_ref,
             dq_ref, dko_ref, dkp_ref, dvo_ref, dvp_ref, dbias_ref, dsink_ref):
        @pl.when((pl.program_id(0) == 0) & (pl.program_id(1) == 0))
        def _():
            dbias_ref[...] = jnp.zeros_like(dbias_ref)
            dsink_ref[...] = jnp.zeros_like(dsink_ref)

        lo, lo2, kdup, vdup, valid, qs = _swa_common(q_ref, kp_ref, ko_ref, vp_ref, vo_ref, pl.program_id(1))
        lse_blk = lse_ref[...]
        dkk, dvv, dq_pairs = [], [], []
        for g in range(SWA_KV_HEADS):
            heads, s = _swa_group_logits(g, qs, kdup, valid, bias_ref)
            lse_g = jnp.concatenate([_head_column(lse_blk, g * SWA_GROUP + i) for i in range(SWA_GROUP)], axis=0)
            p = jnp.exp(s - lse_g)
            do_g, o_g = [], []
            for i in range(SWA_GROUP):
                cols = slice((g * SWA_GROUP + i) // 2 * LANES, ((g * SWA_GROUP + i) // 2 + 1) * LANES)
                do_p = do_ref[:, cols]
                do_g.append(jnp.where(lo if i % 2 == 0 else jnp.logical_not(lo), do_p, jnp.zeros_like(do_p)))
                o_g.append(o_ref[:, cols])
            doh, oh = jnp.concatenate(do_g, axis=0), jnp.concatenate(o_g, axis=0)
            delta = jnp.sum(doh.astype(F32) * oh.astype(F32), -1, keepdims=True)
            dp = lax.dot_general(doh, vdup[g], NT, preferred_element_type=F32)
            ds = p * (dp - delta)
            dbias_ref[heads] += ds.reshape(SWA_GROUP, Q, 2 * Q)
            dsink_ref[heads] -= (jnp.exp(sink_ref[heads].reshape(SWA_GROUP * Q, 1) - lse_g)
                                 * delta).reshape(SWA_GROUP, Q, 1)
            dss = (ds * scale).astype(BF16)
            dq_pairs += _pair_halves(lax.dot_general(dss, kdup[g], NN, preferred_element_type=F32), lo)
            dkk.append(lax.dot_general(dss, qs[g], TN, preferred_element_type=F32))
            dvv.append(lax.dot_general(p.astype(BF16), doh, TN, preferred_element_type=F32))
        dq_ref[...] = jnp.concatenate(dq_pairs, axis=1).astype(BF16)
        fold = lambda x: x + pltpu.roll(x, HEAD_DIM, 1)
        dk_blk = jnp.where(lo2, fold(dkk[0]), fold(dkk[1]))
        dv_blk = jnp.where(lo2, fold(dvv[0]), fold(dvv[1]))
        dkp_ref[...] = dk_blk[:Q]
        dko_ref[...] = dk_blk[Q:]
        dvp_ref[...] = dv_blk[:Q]
        dvo_ref[...] = dv_blk[Q:]

    whole = lambda shape: pl.BlockSpec(shape, lambda b, n: (0,) * len(shape))
    wide = lambda blk: pl.BlockSpec((Q, SWA_HEADS * HEAD_DIM), lambda b, n: (b * nb + n, blk))
    narrow = pl.BlockSpec((Q, LANES), lambda b, n: (b * nb + n, 0))
    kv_shape = jax.ShapeDtypeStruct((B * S, LANES), F32)
    return pl.pallas_call(
        body, name=name, grid=(B, nb),
        in_specs=_swa_in_specs(nb) + [wide(0), wide(do_blk0), narrow, whole((SWA_HEADS, Q, 2 * Q)),
                                      whole((SWA_HEADS, Q, 1))],
        out_specs=[wide(0), narrow, narrow, narrow, narrow, whole((SWA_HEADS, Q, 2 * Q)), whole((SWA_HEADS, Q, 1))],
        out_shape=[jax.ShapeDtypeStruct((B * S, SWA_HEADS * HEAD_DIM), BF16), kv_shape, kv_shape, kv_shape, kv_shape,
                   jax.ShapeDtypeStruct((SWA_HEADS, Q, 2 * Q), F32), jax.ShapeDtypeStruct((SWA_HEADS, Q, 1), F32)],
        compiler_params=_cparams("arbitrary", "arbitrary"),
    )(h, h, h, h, h, o, do, lse, bias, sinkcol)


def _bias_bucket_sum(dbias, bucket, *, name):
    def body(d_ref, b_ref, o_ref):
        dbv, bk = d_ref[...], b_ref[...]
        lane = lax.broadcasted_iota(jnp.int32, (SWA_HEADS, LANES), 1)
        out = jnp.zeros((SWA_HEADS, LANES), F32)
        for b in range(REL_BUCKETS):
            part = jnp.sum(jnp.where(bk == b, dbv, 0.0), axis=1)
            tot = jnp.sum(part, axis=-1, keepdims=True)
            out = out + jnp.where(lane == b, tot, 0.0)
        o_ref[...] = out

    return pl.pallas_call(
        body, name=name, out_shape=jax.ShapeDtypeStruct((SWA_HEADS, LANES), F32),
        compiler_params=pltpu.CompilerParams(vmem_limit_bytes=VMEM_LIMIT_BYTES),
    )(dbias, bucket)


def _adamw_update(w, g, m, v):
    m_new = ADAM_B1 * m + (1.0 - ADAM_B1) * g
    v_new = ADAM_B2 * v + (1.0 - ADAM_B2) * jnp.square(g)
    m_hat = m_new / (1.0 - ADAM_B1 ** ADAM_STEP)
    v_hat = v_new / (1.0 - ADAM_B2 ** ADAM_STEP)
    return -ADAM_LR * (m_hat / (jnp.sqrt(v_hat) + ADAM_EPS) + ADAM_WD * w), m_new, v_new


def _adamw(w, g, m, v, *, name):
    def body(w_ref, g_ref, m_ref, v_ref, d_ref, nm_ref, nv_ref):
        d_ref[...], nm_ref[...], nv_ref[...] = _adamw_update(w_ref[...], g_ref[...], m_ref[...], v_ref[...])

    return pl.pallas_call(
        body, name=name, out_shape=[jax.ShapeDtypeStruct(w.shape, F32)] * 3,
        compiler_params=pltpu.CompilerParams(vmem_limit_bytes=VMEM_LIMIT_BYTES),
    )(w, g, m, v)


def _adamw_slots(w, parts, m, v, *, name):
    R, C = w.shape
    tr = R if R <= 512 else _pick(R, (256, 128))

    def body(w_ref, p_ref, m_ref, v_ref, g_ref, d_ref, nm_ref, nv_ref):
        g = p_ref[0].astype(F32)
        for j in range(1, N_DEV):
            g = g + p_ref[j].astype(F32)
        g_ref[...] = g
        d_ref[...], nm_ref[...], nv_ref[...] = _adamw_update(w_ref[...], g, m_ref[...], v_ref[...])

    spec = pl.BlockSpec((tr, C), lambda i: (i, 0))
    return pl.pallas_call(
        body, name=name, grid=(R // tr,),
        in_specs=[spec, pl.BlockSpec((N_DEV, tr, C), lambda i: (0, i, 0)), spec, spec], out_specs=[spec] * 4,
        out_shape=[jax.ShapeDtypeStruct((R, C), F32)] * 4, compiler_params=_cparams("parallel"),
    )(w, parts, m, v)


def _all_gather_hbm(blocks, *, name):
    n = len(blocks)

    def body(*refs):
        x_refs, out_refs = refs[:n], refs[n:2 * n]
        send_sems, recv_sems, local_sems = refs[2 * n:]
        x, y, c, _ = _mesh_place()
        me, sibling = (x, y, c), (x, y, 1 - c)
        chips = [(1 - x, y), (x, 1 - y), (1 - x, 1 - y)]

        def copy(w, k, blk, to, src=None):
            px, py, pc = blk
            slot = out_refs[w].at[4 * px + 2 * py + pc]
            return pltpu.make_async_remote_copy(
                src_ref=slot if src is None else src, dst_ref=slot,
                send_sem=send_sems.at[w, k], recv_sem=recv_sems.at[w, k], device_id=to, device_id_type=MESH_ID)

        mine = [pltpu.make_async_copy(x_refs[w], out_refs[w].at[4 * x + 2 * y + c], local_sems.at[w])
                for w in range(n)]
        for cp in mine:
            cp.start()
        first = []
        for w in range(n):
            first.append(copy(w, 0, me, sibling, src=x_refs[w]))
            first += [copy(w, 1 + j, me, (*chip, c), src=x_refs[w]) for j, chip in enumerate(chips)]
        for cp in first:
            cp.start()
        passed = []
        for j, chip in enumerate(chips):
            for w in range(n):
                copy(w, 1 + j, (*chip, c), me).wait_recv()
                fwd = copy(w, 4 + j, (*chip, c), sibling)
                fwd.start()
                passed.append(fwd)
        for w in range(n):
            copy(w, 0, sibling, me).wait_recv()
            for j, chip in enumerate(chips):
                copy(w, 4 + j, (*chip, 1 - c), me).wait_recv()
        for cp in first + passed:
            cp.wait_send()
        for cp in mine:
            cp.wait()

    return pl.pallas_call(
        body, name=name, out_shape=[jax.ShapeDtypeStruct((N_DEV,) + b.shape, b.dtype) for b in blocks],
        in_specs=[HBM_SPEC] * n, out_specs=[HBM_SPEC] * n,
        scratch_shapes=[pltpu.SemaphoreType.DMA((n, 7)), pltpu.SemaphoreType.DMA((n, 7)),
                        pltpu.SemaphoreType.DMA((n,))],
    )(*blocks)


def _all_reduce_small(block, *, name):
    R, W = block.shape

    def body(x_ref, out_ref, buf, send_sems, recv_sems):
        x, y, c, me = _mesh_place()
        copies = []
        for k, (peer, _) in enumerate(_peers(x, y, c)):
            copies.append(pltpu.make_async_remote_copy(
                src_ref=x_ref, dst_ref=buf.at[me], send_sem=send_sems.at[k], recv_sem=recv_sems.at[k],
                device_id=peer, device_id_type=MESH_ID))
        for cp in copies:
            cp.start()
        buf[me] = x_ref[...]
        for cp in copies:
            cp.wait_recv()
        for cp in copies:
            cp.wait_send()
        acc = buf[0]
        for j in range(1, N_DEV):
            acc = acc + buf[j]
        out_ref[...] = acc

    return pl.pallas_call(
        body, name=name, out_shape=jax.ShapeDtypeStruct((R, W), F32),
        in_specs=[VMEM_SPEC], out_specs=VMEM_SPEC,
        scratch_shapes=[pltpu.VMEM((N_DEV, R, W), F32), pltpu.SemaphoreType.DMA((7,)), pltpu.SemaphoreType.DMA((7,))],
    )(block)


def _assemble(name, g):
    if BIG_AXIS[name] == 2:
        return jnp.concatenate([g[j] for j in range(N_DEV)], axis=1)
    return g.reshape(N_DEV * g.shape[1], g.shape[2])


def _split_for_devices(name, g):
    if BIG_AXIS[name] == 2:
        b = g.shape[1] // N_DEV
        return jnp.stack([g[:, j * b:(j + 1) * b] for j in range(N_DEV)]).astype(BF16)
    return g.reshape(N_DEV, g.shape[0] // N_DEV, g.shape[1]).astype(BF16)


def _layer_weight_keys(i):
    j = i // 2
    mixer = [('ev_w_in', j), ('ev_w_uq', j), ('ev_w_ukv', j), ('ev_w_out', j)] if i % 2 == 0 \
        else [('od_w_in', j), ('od_w_out', j)]
    return mixer + [('w_up', i), ('w_down', i), ('ple_w_proj', i), ('ple_w_gate', i)]


class _MeshExchange:
    def __init__(self, shards):
        self.shards = shards
        self.weights = {}
        self.pending = []
        self.in_flight = []
        self.received = {}

    def layer_weights(self, i):
        keys = _layer_weight_keys(i)
        if i == 0:
            got = _all_gather_hbm([self.shards[k] for k in keys], name="gather_l0")
            self.weights[0] = {k[0]: _assemble(k[0], g) for k, g in zip(keys, got)}
        return self.weights[i]

    def _fwd_keys(self, i, part):
        if i + 1 >= DEPTH:
            return []
        keys = _layer_weight_keys(i + 1)
        if i % 2 == 1:
            return keys if part == 0 else []
        return keys[:2] if part == 0 else keys[2:]

    def fwd_items(self, i, part):
        return [("gather", self.shards[k]) for k in self._fwd_keys(i, part)]

    def fwd_done(self, i, part, outs):
        got = {k[0]: _assemble(k[0], g) for k, g in zip(self._fwd_keys(i, part), outs)}
        self.weights.setdefault(i + 1, {}).update(got)

    def push_grads(self, grads):
        self.pending += [(k, _split_for_devices(k[0], g)) for k, g in grads.items()]

    def bwd_items(self):
        self.in_flight, self.pending = self.pending, []
        return [("scatter", parts) for _, parts in self.in_flight]

    def bwd_done(self, outs):
        for (k, _), got in zip(self.in_flight, outs):
            self.received[k] = got
        self.in_flight = []

    def finish(self):
        if self.pending:
            outs = _exchange(self.bwd_items(), name="exchange_rest")
            self.bwd_done(outs)
        return self.received


PACK_ROWS = 8


def _pack_small(vals):
    flat = jnp.concatenate([vals[n].reshape(-1).astype(F32) for n in SMALL])
    pad = (-flat.shape[0]) % (PACK_ROWS * LANES)
    return jnp.pad(flat, (0, pad)).reshape(-1, LANES)


def _unpack_small(block, shapes):
    flat = block.reshape(-1)
    out, off = {}, 0
    for n in SMALL:
        sz = math.prod(shapes[n])
        out[n] = flat[off:off + sz].reshape(shapes[n])
        off += sz
    return out


def _rope_tables(S):
    half = MLA_ROPE // 2
    inv = 1.0 / (ROPE_THETA ** (jnp.arange(0, MLA_ROPE, 2, dtype=F32) / MLA_ROPE))
    ang = jnp.arange(S, dtype=F32)[:, None] * inv[None, :]
    cos, sin = jnp.cos(ang), jnp.sin(ang)
    zeros = jnp.zeros((S, half), F32)
    tail = jnp.zeros((S, LANES - MLA_QK), F32)

    def block(rope_part, nope_val):
        return jnp.concatenate([jnp.full((S, MLA_NOPE), nope_val, F32), rope_part, tail], -1)

    a_r = jnp.concatenate([cos, cos], -1)
    bm_r = jnp.concatenate([-sin, zeros], -1)
    bp_r = jnp.concatenate([zeros, sin], -1)
    q_tabs = tuple(block(r, v) for r, v in ((a_r, 1.0), (bm_r, 0.0), (bp_r, 0.0)))
    k_tabs = tuple(block(r, 0.0) for r in (a_r, bm_r, bp_r))
    return q_tabs, k_tabs


def _t5_bucket(dist):
    exact = REL_BUCKETS // 2
    d = jnp.maximum(dist, 1).astype(F32)
    large = exact + (jnp.log(d / exact) / math.log(REL_MAX_DIST / exact) * (REL_BUCKETS - exact)).astype(jnp.int32)
    large = jnp.minimum(large, REL_BUCKETS - 1)
    return jnp.where(dist < exact, dist, large)


def _swa_bucket_table():
    a = jnp.arange(BLOCK_Q)[:, None]
    col = jnp.arange(2 * BLOCK_Q)[None, :]
    return _t5_bucket(jnp.maximum(a + BLOCK_Q - col, 0)).astype(jnp.int32)


def _even_weights(W):
    w = W['ev_w_in']
    c_kv1 = MLA_Q_LORA + MLA_KV_LORA
    c_kr1 = c_kv1 + MLA_ROPE
    c_qs1 = c_kr1 + SWA_HEADS * HEAD_DIM
    zeros = lambda n: jnp.zeros((D_MODEL, n), w.dtype)
    w_in = jnp.concatenate([w[:, c_kr1:c_qs1], w[:, :c_kv1], w[:, c_qs1:], zeros(KR_LANE0), w[:, c_kv1:c_kr1],
                            zeros(LANES - KR_LANE0 - MLA_ROPE)], axis=1)
    uq = W['ev_w_uq'].reshape(MLA_Q_LORA, MLA_HEADS, MLA_QK)
    w_uq = jnp.pad(uq, ((0, 0), (0, 0), (0, LANES - MLA_QK))).reshape(MLA_Q_LORA, MLA_HEADS * LANES)
    ukv = W['ev_w_ukv'].reshape(MLA_KV_LORA, MLA_HEADS, MLA_NOPE + MLA_V)
    w_k = jnp.pad(ukv[..., :MLA_NOPE], ((0, 0), (0, 0), (0, LANES - MLA_NOPE))).reshape(MLA_KV_LORA, -1)
    w_v = ukv[..., MLA_NOPE:].reshape(MLA_KV_LORA, MLA_HEADS * MLA_V)
    return w_in, w_uq, w_k, w_v, W['ev_w_out']


def _even_in_grad_unpad(dw):
    kr0 = EV_KR[0] + KR_LANE0
    return jnp.concatenate([dw[:, EV_CQ[0]:EV_CKV[1]], dw[:, kr0:kr0 + MLA_ROPE], dw[:, EV_QS[0]:EV_QS[1]],
                            dw[:, EV_KS[0]:EV_VS[1]]], axis=1)


def _even_fwd(xb, W, P, i, B, S, tabs, xchg, tag):
    j = i // 2
    q_tabs, k_tabs, bias, sinkcol = tabs
    w_in, w_uq, w_k, w_v, w_out = _even_weights(W)
    h = _mm(xb, w_in, name=f"{tag}_in")
    cqn, ckvn, rq, rkv = _even_norms(h, P['ev_q_norm'][j][None], P['ev_kv_norm'][j][None], name=f"{tag}_norms")
    q = _rope(_mm(cqn, w_uq, name=f"{tag}_uq"), q_tabs, S, sign=1.0, name=f"{tag}_ropeq")
    knp = _mm(ckvn, w_k, out_dtypes=(BF16,), name=f"{tag}_uk")
    v = _mm(ckvn, w_v, out_dtypes=(BF16,), name=f"{tag}_uv")
    k = _mla_keys(knp, h, k_tabs, S, name=f"{tag}_keys")
    o_mla, lse_mla, got = _flash_fwd(q, k, v, q_blk0=0, k_blk0=0, v_blk0=0, W=2 * LANES, n_pairs=MLA_HEADS // 2,
                                     B=B, S=S, scale=MLA_QK ** -0.5, comm=xchg.fwd_items(i, 0), name=f"{tag}_mla")
    xchg.fwd_done(i, 0, got)
    o_swa, lse_swa, got = _swa_fwd(h, bias, sinkcol, B=B, S=S, comm=xchg.fwd_items(i, 1), name=f"{tag}_swa")
    xchg.fwd_done(i, 1, got)
    o_cat = jnp.concatenate([o_mla, o_swa], axis=-1)
    res = dict(h=h, cqn=cqn, ckvn=ckvn, rq=rq, rkv=rkv, q=q, k=k, v=v, o_mla=o_mla, lse_mla=lse_mla,
               o_swa=o_swa, lse_swa=lse_swa, o_cat=o_cat)
    return (o_cat, w_out), res


def _shift_prev(own, prev, B, S):
    prev = prev.reshape(B, S, LANES)
    shifted = jnp.concatenate([prev[:, BLOCK_Q:], jnp.zeros_like(prev[:, :BLOCK_Q])], axis=1)
    return (own + shifted.reshape(B * S, LANES)).astype(BF16)


def _even_bwd(dmb, dz1, xb, W, P, j, B, S, tabs, res, xchg, tag):
    q_tabs, k_tabs, bias, sinkcol = tabs
    w_in, w_uq, w_k, w_v, w_out = _even_weights(W)
    g = {}
    g['ev_w_out'] = _mm_tn(res['o_cat'], dmb, name=f"{tag}_dwout")
    do = _mm(dmb, w_out, trans_b=True, out_dtypes=(BF16,), name=f"{tag}_do")
    dq, dk, dv, got = _flash_bwd(res['q'], res['k'], res['v'], res['o_mla'], do, res['lse_mla'], q_blk0=0, k_blk0=0,
                                 v_blk0=0, do_blk0=0, W=2 * LANES, n_pairs=MLA_HEADS // 2, B=B, S=S,
                                 scale=MLA_QK ** -0.5, qk_dtype=F32, comm=xchg.bwd_items(), name=f"{tag}_mla_bwd")
    xchg.bwd_done(got)
    dq_pre = _rope(dq, q_tabs, S, sign=-1.0, name=f"{tag}_ropeq_bwd")
    dw_uq = _mm_tn(res['cqn'], dq_pre, name=f"{tag}_dwuq")
    g['ev_w_uq'] = dw_uq.reshape(MLA_Q_LORA, MLA_HEADS, LANES)[..., :MLA_QK].reshape(MLA_Q_LORA, MLA_HEADS * MLA_QK)
    dcqn = _mm(dq_pre, w_uq, trans_b=True, name=f"{tag}_dcqn")
    dw_k = _mm_tn(res['ckvn'], dk, name=f"{tag}_dwuk").reshape(MLA_KV_LORA, MLA_HEADS, LANES)[..., :MLA_NOPE]
    dw_v = _mm_tn(res['ckvn'], dv, name=f"{tag}_dwuv").reshape(MLA_KV_LORA, MLA_HEADS, MLA_V)
    g['ev_w_ukv'] = jnp.concatenate([dw_k, dw_v], axis=-1).reshape(MLA_KV_LORA, MLA_HEADS * (MLA_NOPE + MLA_V))
    dckvn_v = _mm(dv, w_v, trans_b=True, name=f"{tag}_dckvn_v")
    dckvn = _mm(dk, w_k, trans_b=True, extras=(dckvn_v,), epilogue=lambda acc, r: (acc + r,), name=f"{tag}_dckvn")
    dkr_pre = _mla_rope_key_grad(dk, k_tabs, S, name=f"{tag}_ropek_bwd")
    dqs, dko, dkp, dvo, dvp, dbias, dsink = _swa_bwd(res['h'], res['o_swa'], do, res['lse_swa'], bias, sinkcol,
                                                     do_blk0=1, B=B, S=S, name=f"{tag}_swa_bwd")
    dh, dgq, dgkv = _even_in_bwd(res['h'], res['rq'], res['rkv'], P['ev_q_norm'][j][None], P['ev_kv_norm'][j][None],
                                 dcqn, dckvn, dqs, _shift_prev(dko, dkp, B, S), _shift_prev(dvo, dvp, B, S), dkr_pre,
                                 name=f"{tag}_in_bwd")
    g['ev_w_in'] = _even_in_grad_unpad(_mm_tn(xb, dh, name=f"{tag}_dwin"))
    dx = _mm(dh, w_in, trans_b=True, extras=(dz1,), epilogue=lambda acc, r: (acc + DN_ALPHA * r,), name=f"{tag}_dx")
    small = dict(ev_q_norm=dgq[0], ev_kv_norm=dgkv[0], dbias=dbias, ev_sinks=jnp.sum(dsink, axis=(1, 2)))
    return dx, g, small


def _odd_fwd(xb, W, P, i, B, S, xchg, tag):
    j = i // 2
    w = W['od_w_in']
    w_qkv = w[:, :ODD_QKV]
    w_f = jnp.pad(w[:, ODD_QKV:], ((0, 0), (0, LANES - FOX_HEADS)))
    bf = jnp.pad(P['od_b_f'][j], (0, LANES - FOX_HEADS))[None]
    qkv = _mm(xb, w_qkv, out_dtypes=(BF16,), name=f"{tag}_qkv")
    f = _mm(xb, w_f, name=f"{tag}_f").reshape(B, S, LANES)
    csh, chs = _fox_decay_fwd(f, bf, name=f"{tag}_decay")
    crow = chs[:, :FOX_HEADS].reshape(B, FOX_HEADS, S // ATT_TILE, 1, ATT_TILE)
    n_blk = FOX_HEADS * HEAD_DIM // LANES
    o, lse, got = _flash_fwd(qkv, qkv, qkv, q_blk0=0, k_blk0=n_blk, v_blk0=2 * n_blk, W=LANES,
                             n_pairs=FOX_HEADS // 2, B=B, S=S, scale=HEAD_DIM ** -0.5, csh=csh, crow=crow,
                             comm=xchg.fwd_items(i, 0), name=f"{tag}_fox")
    xchg.fwd_done(i, 0, got)
    res = dict(f=f, bf=bf, csh=csh, crow=crow, qkv=qkv, o=o, lse=lse, w_qkv=w_qkv, w_f=w_f)
    return (o, W['od_w_out']), res


def _odd_bwd(dmb, dz1, xb, W, P, j, B, S, res, xchg, tag):
    g = {}
    w_out = W['od_w_out']
    g['od_w_out'] = _mm_tn(res['o'], dmb, name=f"{tag}_dwout")
    do = _mm(dmb, w_out, trans_b=True, out_dtypes=(BF16,), name=f"{tag}_do")
    qkv = res['qkv']
    n_blk = FOX_HEADS * HEAD_DIM // LANES
    dq, dk, dv, dck, dcq, got = _flash_bwd(qkv, qkv, qkv, res['o'], do, res['lse'], q_blk0=0, k_blk0=n_blk,
                                           v_blk0=2 * n_blk, do_blk0=0, W=LANES, n_pairs=FOX_HEADS // 2, B=B, S=S,
                                           scale=HEAD_DIM ** -0.5, qk_dtype=BF16, csh=res['csh'], crow=res['crow'],
                                           comm=xchg.bwd_items(), name=f"{tag}_fox_bwd")
    xchg.bwd_done(got)
    dc = dck.reshape(B, FOX_HEADS, S) + dcq.reshape(B, FOX_HEADS, S)
    dc_hs = jnp.pad(dc, ((0, 0), (0, LANES - FOX_HEADS), (0, 0)))
    df, dbf = _fox_decay_bwd(dc_hs, res['f'], res['bf'], name=f"{tag}_decay_bwd")
    df = df.reshape(B * S, LANES)
    dqkv = jnp.concatenate([dq, dk, dv], axis=-1)
    dw_qkv = _mm_tn(xb, dqkv, name=f"{tag}_dwqkv")
    dw_f = _mm_tn(xb, df, name=f"{tag}_dwf")
    g['od_w_in'] = jnp.concatenate([dw_qkv, dw_f[:, :FOX_HEADS]], axis=1)
    dxf = _mm(df, res['w_f'], trans_b=True, extras=(dz1,), epilogue=lambda acc, r: (acc + DN_ALPHA * r,),
              name=f"{tag}_dxf")
    dx = _mm(dqkv, res['w_qkv'], trans_b=True, extras=(dxf,), epilogue=lambda acc, r: (acc + r,), name=f"{tag}_dx")
    small = dict(od_b_f=dbf[0, :FOX_HEADS])
    return dx, g, small


def _local_step(x, p, target, P, xchg):
    B, S, D = x.shape
    T = B * S
    q_tabs, k_tabs = _rope_tables(S)
    bucket = _swa_bucket_table()
    in_bucket = (bucket[..., None] == jnp.arange(REL_BUCKETS)).astype(F32)
    bias = jnp.einsum('acb,bh->hac', in_bucket, P['rel_bias'], precision=lax.Precision.HIGHEST)

    xc = x.reshape(T, D)
    xcb = xc.astype(BF16)
    saved = []
    for i in range(DEPTH):
        j = i // 2
        tag = f"l{i}"
        W = xchg.layer_weights(i)
        lay = dict(xb=xcb, W=W)
        if i % 2 == 0:
            sinkcol = jnp.broadcast_to(P['ev_sinks'][j][:, None, None], (SWA_HEADS, BLOCK_Q, 1)).astype(F32)
            lay['tabs'] = (q_tabs, k_tabs, bias, sinkcol)
            (o, w_out), lay['mix'] = _even_fwd(xcb, W, P, i, B, S, lay['tabs'], xchg, tag)
        else:
            (o, w_out), lay['mix'] = _odd_fwd(xcb, W, P, i, B, S, xchg, tag)
        x1, x1b, lay['xh1'], lay['r1'] = _mm_ln(o, w_out, xc, P['ln1_g'][i][None], P['ln1_b'][i][None],
                                                name=f"{tag}_out_ln1")
        lay['x1b'] = x1b
        lay['u'], lay['a'] = _mm(x1b, W['w_up'], out_dtypes=(F32, BF16),
                                 epilogue=lambda acc: (acc, jnp.square(jnp.maximum(acc, 0.0))), name=f"{tag}_up")
        x2, x2b, lay['xh2'], lay['r2'] = _mm_ln(lay['a'], W['w_down'], x1, P['ln2_g'][i][None], P['ln2_b'][i][None],
                                                name=f"{tag}_down_ln2")
        lay['x2b'] = x2b
        lay['p'] = p[i].reshape(T, D_PLE)
        lay['e'] = _mm(lay['p'], W['ple_w_proj'], name=f"{tag}_ple_proj")

        def gate(acc, bg, e, x2v):
            gv = 1.0 / (1.0 + jnp.exp(-(acc + bg)))
            y = x2v + gv * e
            return y, y, gv

        xc, xcb, lay['g'] = _mm(x2b, W['ple_w_gate'], extras=(P['ple_b_gate'][i][None], lay['e'], x2),
                                epilogue=gate, out_dtypes=(F32, BF16, F32), name=f"{tag}_ple_gate")
        saved.append(lay)

    dy, sq = _loss_grad(xc, target.reshape(T, D), name="loss")

    Gs = {n: [None] * DEPTH for n in ('ln1_g', 'ln1_b', 'ln2_g', 'ln2_b', 'ple_b_gate')}
    Gs.update({n: [None] * (DEPTH // 2) for n in ('ev_q_norm', 'ev_kv_norm', 'ev_sinks', 'od_b_f')})
    dbias_total = None
    for i in reversed(range(DEPTH)):
        j = i // 2
        tag = f"l{i}b"
        lay = saved[i]
        W = lay['W']
        de, dzg, dbg = _ple_bwd_elem(dy, lay['g'], lay['e'], name=f"{tag}_ple_elem")
        Gs['ple_b_gate'][i] = dbg[0]
        g_mlp = {('ple_w_proj', i): _mm_tn(lay['p'], de, name=f"{tag}_dwproj"),
                 ('ple_w_gate', i): _mm_tn(lay['x2b'], dzg, name=f"{tag}_dwgate")}
        dz2, dz2b, dg2, db2 = _mm_ln_bwd(dzg, W['ple_w_gate'], dy, 1.0, lay['xh2'], lay['r2'], P['ln2_g'][i][None],
                                         name=f"{tag}_dx2_ln2")
        Gs['ln2_g'][i], Gs['ln2_b'][i] = dg2[0], db2[0]
        g_mlp[('w_down', i)] = _mm_tn(lay['a'], dz2b, name=f"{tag}_dwdown")
        du = _mm(dz2b, W['w_down'], trans_b=True, extras=(lay['u'],), out_dtypes=(BF16,),
                 epilogue=lambda acc, u: (acc * (2.0 * jnp.maximum(u, 0.0)),), name=f"{tag}_du")
        g_mlp[('w_up', i)] = _mm_tn(lay['x1b'], du, name=f"{tag}_dwup")
        xchg.push_grads(g_mlp)
        dz1, dz1b, dg1, db1 = _mm_ln_bwd(du, W['w_up'], dz2, DN_ALPHA, lay['xh1'], lay['r1'], P['ln1_g'][i][None],
                                         name=f"{tag}_dx1_ln1")
        Gs['ln1_g'][i], Gs['ln1_b'][i] = dg1[0], db1[0]
        if i % 2 == 0:
            dy, g, small = _even_bwd(dz1b, dz1, lay['xb'], W, P, j, B, S, lay['tabs'], lay['mix'], xchg, tag)
            dbias_total = small['dbias'] if dbias_total is None else dbias_total + small['dbias']
            for n in ('ev_q_norm', 'ev_kv_norm', 'ev_sinks'):
                Gs[n][j] = small[n]
        else:
            dy, g, small = _odd_bwd(dz1b, dz1, lay['xb'], W, P, j, B, S, lay['mix'], xchg, tag)
            Gs['od_b_f'][j] = small['od_b_f']
        xchg.push_grads({(n, j): val for n, val in g.items()})

    grads_small = {n: jnp.stack(v) for n, v in Gs.items()}
    drel = _bias_bucket_sum(dbias_total, bucket, name="rel_bias_grad")
    grads_small['rel_bias'] = drel[:, :REL_BUCKETS].T
    return sq, dy.reshape(B, S, D), grads_small


def kernel(x, p, rel_bias, ev_w_in, ev_q_norm, ev_w_uq, ev_kv_norm, ev_w_ukv, ev_sinks, ev_w_out, od_w_in, od_b_f, od_w_out, ln1_g, ln1_b, w_up, w_down, ln2_g, ln2_b, ple_w_proj, ple_w_gate, ple_b_gate, loss_target, m_rel_bias, m_ev_w_in, m_ev_q_norm, m_ev_w_uq, m_ev_kv_norm, m_ev_w_ukv, m_ev_sinks, m_ev_w_out, m_od_w_in, m_od_b_f, m_od_w_out, m_ln1_g, m_ln1_b, m_w_up, m_w_down, m_ln2_g, m_ln2_b, m_ple_w_proj, m_ple_w_gate, m_ple_b_gate, v_rel_bias, v_ev_w_in, v_ev_q_norm, v_ev_w_uq, v_ev_kv_norm, v_ev_w_ukv, v_ev_sinks, v_ev_w_out, v_od_w_in, v_od_b_f, v_od_w_out, v_ln1_g, v_ln1_b, v_w_up, v_w_down, v_ln2_g, v_ln2_b, v_ple_w_proj, v_ple_w_gate, v_ple_b_gate):
    given = dict(locals())
    w = {n: given[n] for n in WEIGHTS}
    mom = {n: given["m_" + n] for n in WEIGHTS}
    var = {n: given["v_" + n] for n in WEIGHTS}
    small_shapes = {n: w[n].shape for n in SMALL}

    shards = {(n, idx): w[n][idx].astype(BF16) for n in BIG for idx in range(w[n].shape[0])}
    xchg = _MeshExchange(shards)
    P = {n: w[n] for n in SMALL}

    sq, grad_x, grads_small = _local_step(x, p, loss_target, P, xchg)
    loss = lax.psum(0.5 * jnp.sum(sq) / D_MODEL, ("x", "y", "c"))

    received = xchg.finish()
    g_small_packed = _all_reduce_small(_pack_small(grads_small), name="reduce_small_grads")
    g_small = _unpack_small(g_small_packed, small_shapes)

    grad, delta, new_m, new_v = {}, {}, {}, {}
    for n in BIG:
        per_layer = [_adamw_slots(w[n][idx], received[(n, idx)], mom[n][idx], var[n][idx], name=f"adamw_{n}{idx}")
                     for idx in range(w[n].shape[0])]
        grad[n], delta[n], new_m[n], new_v[n] = (jnp.stack(t) for t in zip(*per_layer))
    d, nm, nv = _adamw(_pack_small(w), g_small_packed, _pack_small(mom), _pack_small(var), name="adamw_small")
    d, nm, nv = (_unpack_small(t, small_shapes) for t in (d, nm, nv))
    for n in SMALL:
        grad[n], delta[n], new_m[n], new_v[n] = g_small[n], d[n], nm[n], nv[n]

    return (loss, grad_x, *[grad[n] for n in WEIGHTS], *[delta[n] for n in WEIGHTS],
            *[new_m[n] for n in WEIGHTS], *[new_v[n] for n in WEIGHTS])
```

```python
import math

import jax
import jax.numpy as jnp
from jax import lax
from jax.experimental import pallas as pl
from jax.experimental.pallas import tpu as pltpu

F32, BF16 = jnp.float32, jnp.bfloat16

D_MODEL = 1024
DEPTH = 4
HEAD_DIM = 64
MLA_HEADS, MLA_NOPE, MLA_ROPE, MLA_V = 8, 64, 32, 64
MLA_Q_LORA, MLA_KV_LORA = 384, 256
MLA_QK = MLA_NOPE + MLA_ROPE
ROPE_THETA = 10000.0
SWA_HEADS, SWA_KV_HEADS, SWA_WINDOW = 8, 2, 128
SWA_GROUP = SWA_HEADS // SWA_KV_HEADS
REL_BUCKETS, REL_MAX_DIST = 32, 128
FOX_HEADS = 16
D_FF = 4 * D_MODEL
D_PLE = 256
BLOCK_Q = 128
DN_ALPHA = (2 * DEPTH) ** 0.25
NORM_EPS = 1e-5
NEG_INF = -1e30
EVEN_IN = 1440
ODD_QKV = 3 * FOX_HEADS * HEAD_DIM
LANES = 128

EV_QS = (0, 512)
EV_CQ = (512, 896)
EV_CKV = (896, 1152)
EV_KS = (1152, 1280)
EV_VS = (1280, 1408)
EV_KR = (1408, 1536)
EVEN_IN_PAD = 1536
KR_LANE0 = MLA_NOPE

ADAM_LR, ADAM_B1, ADAM_B2, ADAM_EPS, ADAM_WD, ADAM_STEP = 0.001, 0.9, 0.999, 1e-08, 0.01, 10

N_DEV = 8
VMEM_LIMIT_BYTES = 48 * 1024 * 1024
ATT_TILE = 512
ATT_TILE_BWD = 512
PAIRS_PER_STEP_FWD = 4
PAIRS_PER_STEP_BWD = 2

NN = (((1,), (0,)), ((), ()))
NT = (((1,), (1,)), ((), ()))
TN = (((0,), (0,)), ((), ()))

BIG = ['ev_w_in', 'ev_w_uq', 'ev_w_ukv', 'ev_w_out', 'od_w_in', 'od_w_out', 'w_up', 'w_down',
       'ple_w_proj', 'ple_w_gate']
BIG_AXIS = {'ev_w_in': 2, 'ev_w_uq': 2, 'ev_w_ukv': 2, 'ev_w_out': 1, 'od_w_in': 2, 'od_w_out': 1,
            'w_up': 2, 'w_down': 1, 'ple_w_proj': 2, 'ple_w_gate': 1}
SMALL = ['rel_bias', 'ev_q_norm', 'ev_kv_norm', 'ev_sinks', 'od_b_f', 'ln1_g', 'ln1_b', 'ln2_g', 'ln2_b',
         'ple_b_gate']
WEIGHTS = ['rel_bias', 'ev_w_in', 'ev_q_norm', 'ev_w_uq', 'ev_kv_norm', 'ev_w_ukv', 'ev_sinks', 'ev_w_out',
           'od_w_in', 'od_b_f', 'od_w_out', 'ln1_g', 'ln1_b', 'w_up', 'w_down', 'ln2_g', 'ln2_b',
           'ple_w_proj', 'ple_w_gate', 'ple_b_gate']


def _cparams(*sem):
    return pltpu.CompilerParams(dimension_semantics=sem, vmem_limit_bytes=VMEM_LIMIT_BYTES)


def _pick(n, cands):
    for c in cands:
        if n % c == 0:
            return c
    return n


MM_STEP_BYTES = 10 * 1024 * 1024
MM_OUT_BYTES = 8 * 1024 * 1024
MM_CHUNK = 512


def _mm(a, b, *, trans_b=False, extras=(), epilogue=None, row_epilogue=None, out_dtypes=(F32,), out_widths=None,
        n_sums=0, comm=(), name):
    M, K = a.shape
    N = b.shape[0] if trans_b else b.shape[1]
    n_ex, n_out = len(extras), len(out_dtypes)
    n_rows_out = n_out - n_sums
    out_widths = (N,) * n_out if out_widths is None else out_widths
    row_bytes = K * a.dtype.itemsize + (sum(w * jnp.dtype(d).itemsize
                                            for w, d in zip(out_widths[:n_rows_out], out_dtypes))
                                        + sum(e.shape[1] * e.dtype.itemsize for e in extras if e.shape[0] == M)
                                        + (4 * N if row_epilogue is not None else 0))
    tm = next((c for c in (1024, 512, 256) if M % c == 0 and c * row_bytes <= MM_STEP_BYTES), 128)
    nc = _pick(N, (MM_CHUNK, 384, 256, 128))
    n_c, kinds = len(comm), [k for k, _ in comm]
    n_scr = 1 if row_epilogue is not None else 0

    def body(*refs):
        c_in = refs[2 + n_ex:2 + n_ex + n_c]
        c_out = refs[2 + n_ex + n_c + n_out:2 + n_ex + 2 * n_c + n_out]
        sems = refs[2 + n_ex + 2 * n_c + n_out + n_scr:]
        refs = refs[:2 + n_ex] + refs[2 + n_ex + n_c:2 + n_ex + n_c + n_out] \
            + refs[2 + n_ex + 2 * n_c + n_out:2 + n_ex + 2 * n_c + n_out + n_scr]
        if n_c:
            place = _mesh_place()
            step = pl.program_id(0)

            @pl.when(step == 0)
            def _():
                _comm_start(kinds, c_in, c_out, sems, place)

        a_ref, b_ref = refs[:2]
        ex = refs[2:2 + n_ex]
        outs = refs[2 + n_ex:2 + n_ex + n_out]
        av = a_ref[...].astype(BF16)
        for n0 in range(0, N, nc):
            cols = slice(n0, n0 + nc)
            bv = (b_ref[cols, :] if trans_b else b_ref[:, cols]).astype(BF16)
            acc = lax.dot_general(av, bv, NT if trans_b else NN, preferred_element_type=F32)
            if row_epilogue is not None:
                refs[-1][:, cols] = acc
                continue
            res = epilogue(acc, *[e[:, cols] for e in ex]) if epilogue is not None else (acc,)
            for o, r in zip(outs, res):
                o[:, cols] = r.astype(o.dtype)
        if row_epilogue is not None:
            res = row_epilogue(refs[-1][...], *[e[...] for e in ex])
            for o, r in zip(outs[:n_rows_out], res):
                o[...] = r.astype(o.dtype)
            if n_sums:
                @pl.when(pl.program_id(0) == 0)
                def _():
                    for o in outs[n_rows_out:]:
                        o[...] = jnp.zeros_like(o)

                for o, r in zip(outs[n_rows_out:], res[n_rows_out:]):
                    o[...] += r
        if n_c:
            @pl.when(step == M // tm - 1)
            def _():
                _comm_wait(kinds, c_in, c_out, sems, place)

    in_specs = [pl.BlockSpec((tm, K), lambda i: (i, 0)), pl.BlockSpec(b.shape, lambda i: (0, 0))]
    for e in extras:
        if e.shape[0] == M:
            in_specs.append(pl.BlockSpec((tm, e.shape[1]), lambda i: (i, 0)))
        elif e.shape == (1, N):
            in_specs.append(pl.BlockSpec((1, N), lambda i: (0, 0)))
        else:
            raise ValueError(f"extra operand of shape {e.shape} for a ({M}, {N}) result")
    res = pl.pallas_call(
        body, name=name, grid=(M // tm,), in_specs=in_specs + [HBM_SPEC] * n_c,
        out_specs=[pl.BlockSpec((tm, w), lambda i: (i, 0)) for w in out_widths[:n_rows_out]]
        + [pl.BlockSpec((1, w), lambda i: (0, 0)) for w in out_widths[n_rows_out:]] + [HBM_SPEC] * n_c,
        out_shape=[jax.ShapeDtypeStruct((M, w), d) for w, d in zip(out_widths[:n_rows_out], out_dtypes)]
        + [jax.ShapeDtypeStruct((1, w), d) for w, d in zip(out_widths[n_rows_out:], out_dtypes[n_rows_out:])]
        + _comm_out_shapes(comm),
        scratch_shapes=([pltpu.VMEM((tm, N), F32)] if row_epilogue is not None else [])
        + (_comm_scratch(comm) if n_c else []),
        compiler_params=_cparams("arbitrary" if n_sums or n_c else "parallel"),
    )(a, b, *extras, *[c for _, c in comm])
    main = res[0] if n_out == 1 else tuple(res[:n_out])
    return (main, list(res[n_out:])) if n_c else main


def _mm_tn(a, b, *, name):
    T, K = a.shape
    N = b.shape[1]
    bk, bn = K, N
    while bk * bn * 4 > MM_OUT_BYTES:
        if bn >= bk and bn % (2 * LANES) == 0:
            bn //= 2
        else:
            bk //= 2
    tt = _pick(T, (1024, 512, 256))
    ck, cn = _pick(bk, (MM_CHUNK, 384, 256, 128)), _pick(bn, (MM_CHUNK, 384, 256, 128))

    def body(a_ref, b_ref, o_ref):
        t = pl.program_id(2)

        @pl.when(t == 0)
        def _():
            o_ref[...] = jnp.zeros_like(o_ref)

        for r0 in range(0, bk, ck):
            av = a_ref[:, r0:r0 + ck].astype(BF16)
            for c0 in range(0, bn, cn):
                o_ref[r0:r0 + ck, c0:c0 + cn] += lax.dot_general(
                    av, b_ref[:, c0:c0 + cn].astype(BF16), TN, preferred_element_type=F32)

    return pl.pallas_call(
        body, name=name, grid=(K // bk, N // bn, T // tt),
        in_specs=[pl.BlockSpec((tt, bk), lambda i, j, t: (t, i)), pl.BlockSpec((tt, bn), lambda i, j, t: (t, j))],
        out_specs=pl.BlockSpec((bk, bn), lambda i, j, t: (i, j)),
        out_shape=jax.ShapeDtypeStruct((K, N), F32),
        compiler_params=_cparams("parallel", "parallel", "arbitrary"),
    )(a, b)


ROW_TILE = 256


def _row_spec(cols, col_block=0):
    return pl.BlockSpec((ROW_TILE, cols), lambda i: (i, col_block))


def _tab_spec(cols, period):
    return pl.BlockSpec((ROW_TILE, cols), lambda i: (i % period, 0))


def _full_spec(shape):
    return pl.BlockSpec(shape, lambda i: (0,) * len(shape))


def _mm_ln(a, w, x, g, b, *, comm=(), name):
    def ln_rows(m, xv, gv, bv):
        z = DN_ALPHA * xv + m
        mu = jnp.mean(z, -1, keepdims=True)
        zc = z - mu
        r = lax.rsqrt(jnp.mean(zc * zc, -1, keepdims=True) + NORM_EPS)
        xh = zc * r
        y = xh * gv + bv
        return y, y, xh, jnp.broadcast_to(r, (r.shape[0], LANES))

    D = w.shape[1]
    return _mm(a, w, extras=(x, g, b), row_epilogue=ln_rows, out_dtypes=(F32, BF16, F32, F32),
               out_widths=(D, D, D, LANES), comm=comm, name=name)


def _mm_ln_bwd(a, w, resid, resid_scale, xh, r, g, *, name):
    def ln_bwd_rows(acc, rv, xhv, rstd, gv):
        dyv = acc + resid_scale * rv
        dyg = dyv * gv
        c1 = jnp.mean(dyg, -1, keepdims=True)
        c2 = jnp.mean(dyg * xhv, -1, keepdims=True)
        dz = _widen(rstd, dyv.shape[-1]) * (dyg - c1 - xhv * c2)
        return dz, dz, jnp.sum(dyv * xhv, 0, keepdims=True), jnp.sum(dyv, 0, keepdims=True)

    D = w.shape[0]
    return _mm(a, w, trans_b=True, extras=(resid, xh, r, g), row_epilogue=ln_bwd_rows,
               out_dtypes=(F32, BF16, F32, F32), out_widths=(D, D, D, D), n_sums=2, name=name)


def _loss_grad(y, target, *, name):
    T, D = y.shape

    def body(y_ref, t_ref, dy_ref, sq_ref):
        err = y_ref[...] - t_ref[...]
        dy_ref[...] = err / D

        @pl.when(pl.program_id(0) == 0)
        def _():
            sq_ref[...] = jnp.zeros_like(sq_ref)

        sq_ref[...] += jnp.sum(err * err, 0, keepdims=True)

    return pl.pallas_call(
        body, name=name, grid=(T // ROW_TILE,),
        in_specs=[_row_spec(D), _row_spec(D)],
        out_specs=[_row_spec(D), _full_spec((1, D))],
        out_shape=[jax.ShapeDtypeStruct((T, D), F32), jax.ShapeDtypeStruct((1, D), F32)],
        compiler_params=_cparams("arbitrary"),
    )(y, target)


def _ple_bwd_elem(dx3, g, e, *, name):
    T, D = dx3.shape

    def body(dx_ref, g_ref, e_ref, de_ref, dz_ref, db_ref):
        dx, gv = dx_ref[...], g_ref[...]
        de_ref[...] = (dx * gv).astype(BF16)
        dz = dx * e_ref[...] * gv * (1.0 - gv)
        dz_ref[...] = dz.astype(BF16)

        @pl.when(pl.program_id(0) == 0)
        def _():
            db_ref[...] = jnp.zeros_like(db_ref)

        db_ref[...] += jnp.sum(dz, 0, keepdims=True)

    return pl.pallas_call(
        body, name=name, grid=(T // ROW_TILE,),
        in_specs=[_row_spec(D), _row_spec(D), _row_spec(D)],
        out_specs=[_row_spec(D), _row_spec(D), _full_spec((1, D))],
        out_shape=[jax.ShapeDtypeStruct((T, D), BF16), jax.ShapeDtypeStruct((T, D), BF16),
                   jax.ShapeDtypeStruct((1, D), F32)],
        compiler_params=_cparams("arbitrary"),
    )(dx3, g, e)


def _rotate(xv, a, bm, bp, sign):
    half = MLA_ROPE // 2
    width = xv.shape[-1]
    a, bm, bp = (_widen(t, width) for t in (a, bm, bp))
    return xv * a + sign * (pltpu.roll(xv, width - half, 1) * bm + pltpu.roll(xv, half, 1) * bp)


def _rope(x, tabs, seq, *, sign, name):
    T, width = x.shape

    def body(x_ref, a_ref, bm_ref, bp_ref, o_ref):
        o_ref[...] = _rotate(x_ref[...], a_ref[...], bm_ref[...], bp_ref[...], sign).astype(BF16)

    return pl.pallas_call(
        body, name=name, grid=(T // ROW_TILE,),
        in_specs=[_row_spec(width)] + [_tab_spec(LANES, seq // ROW_TILE)] * 3,
        out_specs=_row_spec(width),
        out_shape=jax.ShapeDtypeStruct((T, width), BF16),
        compiler_params=_cparams("parallel"),
    )(x, *tabs)


def _mla_keys(knp, h, k_tabs, seq, *, name):
    T = knp.shape[0]

    def body(k_ref, h_ref, a_ref, bm_ref, bp_ref, o_ref):
        kr = _rotate(h_ref[...], a_ref[...], bm_ref[...], bp_ref[...], 1.0)
        for hd in range(MLA_HEADS):
            cols = slice(hd * LANES, (hd + 1) * LANES)
            o_ref[:, cols] = (k_ref[:, cols].astype(F32) + kr).astype(BF16)

    return pl.pallas_call(
        body, name=name, grid=(T // ROW_TILE,),
        in_specs=[_row_spec(MLA_HEADS * LANES), _row_spec(LANES, EV_KR[0] // LANES)]
        + [_tab_spec(LANES, seq // ROW_TILE)] * 3,
        out_specs=_row_spec(MLA_HEADS * LANES),
        out_shape=jax.ShapeDtypeStruct((T, MLA_HEADS * LANES), BF16),
        compiler_params=_cparams("parallel"),
    )(knp, h, *k_tabs)


def _mla_rope_key_grad(dk, k_tabs, seq, *, name):
    T = dk.shape[0]

    def body(dk_ref, a_ref, bm_ref, bp_ref, o_ref):
        tot = dk_ref[:, 0:LANES]
        for hd in range(1, MLA_HEADS):
            tot = tot + dk_ref[:, hd * LANES:(hd + 1) * LANES]
        o_ref[...] = _rotate(tot, a_ref[...], bm_ref[...], bp_ref[...], -1.0).astype(BF16)

    return pl.pallas_call(
        body, name=name, grid=(T // ROW_TILE,),
        in_specs=[_row_spec(MLA_HEADS * LANES)] + [_tab_spec(LANES, seq // ROW_TILE)] * 3,
        out_specs=_row_spec(LANES),
        out_shape=jax.ShapeDtypeStruct((T, LANES), BF16),
        compiler_params=_cparams("parallel"),
    )(dk, *k_tabs)


def _even_norms(h, gq, gkv, *, name):
    T = h.shape[0]

    def body(h_ref, gq_ref, gkv_ref, cq_ref, ckv_ref, rq_ref, rkv_ref):
        cq = h_ref[:, EV_CQ[0]:EV_CQ[1]]
        rq = lax.rsqrt(jnp.mean(cq * cq, -1, keepdims=True) + NORM_EPS)
        cq_ref[...] = (cq * rq * gq_ref[...]).astype(BF16)
        rq_ref[...] = jnp.broadcast_to(rq, rq_ref.shape)
        ckv = h_ref[:, EV_CKV[0]:EV_CKV[1]]
        rkv = lax.rsqrt(jnp.mean(ckv * ckv, -1, keepdims=True) + NORM_EPS)
        ckv_ref[...] = (ckv * rkv * gkv_ref[...]).astype(BF16)
        rkv_ref[...] = jnp.broadcast_to(rkv, rkv_ref.shape)

    return pl.pallas_call(
        body, name=name, grid=(T // ROW_TILE,),
        in_specs=[_row_spec(EVEN_IN_PAD), _full_spec((1, MLA_Q_LORA)), _full_spec((1, MLA_KV_LORA))],
        out_specs=[_row_spec(MLA_Q_LORA), _row_spec(MLA_KV_LORA), _row_spec(LANES), _row_spec(LANES)],
        out_shape=[jax.ShapeDtypeStruct((T, MLA_Q_LORA), BF16), jax.ShapeDtypeStruct((T, MLA_KV_LORA), BF16),
                   jax.ShapeDtypeStruct((T, LANES), F32), jax.ShapeDtypeStruct((T, LANES), F32)],
        compiler_params=_cparams("parallel"),
    )(h, gq, gkv)


def _even_in_bwd(h, rq, rkv, gq, gkv, dcqn, dckvn, dqs, dks, dvs, dkr, *, name):
    T = h.shape[0]

    def rms_bwd(c, r, g, dy):
        r = _widen(r, c.shape[-1])
        xr = c * r
        dyg = dy * g
        return r * (dyg - xr * jnp.mean(dyg * xr, -1, keepdims=True)), jnp.sum(dy * xr, 0, keepdims=True)

    def body(h_ref, rq_ref, rkv_ref, gq_ref, gkv_ref, dcq_ref, dckv_ref, dqs_ref, dks_ref, dvs_ref, dkr_ref,
             dh_ref, dgq_ref, dgkv_ref):
        @pl.when(pl.program_id(0) == 0)
        def _():
            dgq_ref[...] = jnp.zeros_like(dgq_ref)
            dgkv_ref[...] = jnp.zeros_like(dgkv_ref)

        dcq, dgq = rms_bwd(h_ref[:, EV_CQ[0]:EV_CQ[1]], rq_ref[...], gq_ref[...], dcq_ref[...])
        dckv, dgkv = rms_bwd(h_ref[:, EV_CKV[0]:EV_CKV[1]], rkv_ref[...], gkv_ref[...], dckv_ref[...])
        dgq_ref[...] += dgq
        dgkv_ref[...] += dgkv
        dh_ref[:, EV_QS[0]:EV_QS[1]] = dqs_ref[...]
        dh_ref[:, EV_CQ[0]:EV_CQ[1]] = dcq.astype(BF16)
        dh_ref[:, EV_CKV[0]:EV_CKV[1]] = dckv.astype(BF16)
        dh_ref[:, EV_KS[0]:EV_KS[1]] = dks_ref[...]
        dh_ref[:, EV_VS[0]:EV_VS[1]] = dvs_ref[...]
        dh_ref[:, EV_KR[0]:EV_KR[1]] = dkr_ref[...]

    return pl.pallas_call(
        body, name=name, grid=(T // ROW_TILE,),
        in_specs=[_row_spec(EVEN_IN_PAD), _row_spec(LANES), _row_spec(LANES), _full_spec((1, MLA_Q_LORA)),
                  _full_spec((1, MLA_KV_LORA)), _row_spec(MLA_Q_LORA), _row_spec(MLA_KV_LORA),
                  _row_spec(SWA_HEADS * HEAD_DIM), _row_spec(LANES), _row_spec(LANES), _row_spec(LANES)],
        out_specs=[_row_spec(EVEN_IN_PAD), _full_spec((1, MLA_Q_LORA)), _full_spec((1, MLA_KV_LORA))],
        out_shape=[jax.ShapeDtypeStruct((T, EVEN_IN_PAD), BF16), jax.ShapeDtypeStruct((1, MLA_Q_LORA), F32),
                   jax.ShapeDtypeStruct((1, MLA_KV_LORA), F32)],
        compiler_params=_cparams("arbitrary"),
    )(h, rq, rkv, gq, gkv, dcqn, dckvn, dqs, dks, dvs, dkr)


def _fox_decay_fwd(f3, bf, *, name):
    B, S, _ = f3.shape

    def body(f_ref, b_ref, csh_ref, chs_ref):
        x = f_ref[...] + b_ref[...]
        c = jnp.minimum(x, 0.0) - jnp.log1p(jnp.exp(-jnp.abs(x)))
        row = lax.broadcasted_iota(jnp.int32, (S, LANES), 0)
        k = 1
        while k < S:
            c = c + jnp.where(row >= k, pltpu.roll(c, k, 0), 0.0)
            k *= 2
        csh_ref[...] = c
        chs_ref[...] = c.T

    return pl.pallas_call(
        body, name=name, grid=(B,),
        in_specs=[pl.BlockSpec((None, S, LANES), lambda b: (b, 0, 0)), pl.BlockSpec((1, LANES), lambda b: (0, 0))],
        out_specs=[pl.BlockSpec((None, S, LANES), lambda b: (b, 0, 0)),
                   pl.BlockSpec((None, LANES, S), lambda b: (b, 0, 0))],
        out_shape=[jax.ShapeDtypeStruct((B, S, LANES), F32), jax.ShapeDtypeStruct((B, LANES, S), F32)],
        compiler_params=_cparams("parallel"),
    )(f3, bf)


def _fox_decay_bwd(dc_hs, f3, bf, *, name):
    B, S, _ = f3.shape

    def body(dc_ref, f_ref, b_ref, df_ref, db_ref):
        g = dc_ref[...].T
        row = lax.broadcasted_iota(jnp.int32, (S, LANES), 0)
        k = 1
        while k < S:
            g = g + jnp.where(row < S - k, pltpu.roll(g, S - k, 0), 0.0)
            k *= 2
        x = f_ref[...] + b_ref[...]
        df = g * (1.0 / (1.0 + jnp.exp(x)))
        df_ref[...] = df.astype(BF16)

        @pl.when(pl.program_id(0) == 0)
        def _():
            db_ref[...] = jnp.zeros_like(db_ref)

        db_ref[...] += jnp.sum(df, 0, keepdims=True)

    return pl.pallas_call(
        body, name=name, grid=(B,),
        in_specs=[pl.BlockSpec((None, LANES, S), lambda b: (b, 0, 0)),
                  pl.BlockSpec((None, S, LANES), lambda b: (b, 0, 0)), pl.BlockSpec((1, LANES), lambda b: (0, 0))],
        out_specs=[pl.BlockSpec((None, S, LANES), lambda b: (b, 0, 0)), pl.BlockSpec((1, LANES), lambda b: (0, 0))],
        out_shape=[jax.ShapeDtypeStruct((B, S, LANES), BF16), jax.ShapeDtypeStruct((1, LANES), F32)],
        compiler_params=_cparams("arbitrary"),
    )(dc_hs, f3, bf)


def _head_column(block, h):
    lane = lax.broadcasted_iota(jnp.int32, block.shape, 1)
    return jnp.sum(jnp.where(lane == h, block, 0.0), axis=-1, keepdims=True)


def _causal_mask(s):
    r = lax.broadcasted_iota(jnp.int32, s.shape, 0)
    c = lax.broadcasted_iota(jnp.int32, s.shape, 1)
    return jnp.where(c <= r, s, NEG_INF)


def _low_half(shape):
    return (lax.broadcasted_iota(jnp.int32, shape, 1) % LANES) < HEAD_DIM


def _widen(x, cols):
    return jnp.concatenate([x] * (cols // LANES), axis=1)


def _both_halves(x, lo):
    r = pltpu.roll(x, HEAD_DIM, 1)
    return jnp.where(lo, x, r), jnp.where(lo, r, x)


MESH_ID = pl.DeviceIdType.MESH
HBM_SPEC = pl.BlockSpec(memory_space=pltpu.HBM)
VMEM_SPEC = pl.BlockSpec(memory_space=pltpu.VMEM)


def _mesh_place():
    x, y, c = lax.axis_index("x"), lax.axis_index("y"), lax.axis_index("c")
    return x, y, c, 4 * x + 2 * y + c


def _peers(x, y, c):
    out = []
    for mask in range(1, N_DEV):
        dx, dy, dc = (mask >> 2) & 1, (mask >> 1) & 1, mask & 1
        px, py, pc = (1 - x if dx else x), (1 - y if dy else y), (1 - c if dc else c)
        out.append(((px, py, pc), 4 * px + 2 * py + pc))
    return out


def _comm_out_shapes(comm):
    return [jax.ShapeDtypeStruct(((N_DEV,) + a.shape) if kind == "gather" else a.shape, a.dtype) for kind, a in comm]


def _comm_scratch(comm):
    n = len(comm)
    return [pltpu.SemaphoreType.DMA((n, 7)), pltpu.SemaphoreType.DMA((n, 7)), pltpu.SemaphoreType.DMA((n,))]


def _comm_copies(kinds, in_refs, out_refs, sems, place):
    send_sems, recv_sems, local_sems = sems
    x, y, c, me = place
    local, remote = [], []
    for w, kind in enumerate(kinds):
        mine = in_refs[w] if kind == "gather" else in_refs[w].at[me]
        local.append(pltpu.make_async_copy(mine, out_refs[w].at[me], local_sems.at[w]))
        for k, (peer, peer_idx) in enumerate(_peers(x, y, c)):
            remote.append(pltpu.make_async_remote_copy(
                src_ref=in_refs[w] if kind == "gather" else in_refs[w].at[peer_idx], dst_ref=out_refs[w].at[me],
                send_sem=send_sems.at[w, k], recv_sem=recv_sems.at[w, k], device_id=peer, device_id_type=MESH_ID))
    return local, remote


def _comm_start(kinds, in_refs, out_refs, sems, place):
    local, remote = _comm_copies(kinds, in_refs, out_refs, sems, place)
    for cp in local + remote:
        cp.start()


def _comm_wait(kinds, in_refs, out_refs, sems, place):
    local, remote = _comm_copies(kinds, in_refs, out_refs, sems, place)
    for cp in remote:
        cp.wait_recv()
    for cp in remote:
        cp.wait_send()
    for cp in local:
        cp.wait()


def _exchange(comm, *, name):
    n = len(comm)
    kinds = [k for k, _ in comm]

    def body(*refs):
        place = _mesh_place()
        _comm_start(kinds, refs[:n], refs[n:2 * n], refs[2 * n:], place)
        _comm_wait(kinds, refs[:n], refs[n:2 * n], refs[2 * n:], place)

    return pl.pallas_call(
        body, name=name, out_shape=_comm_out_shapes(comm), in_specs=[HBM_SPEC] * n, out_specs=[HBM_SPEC] * n,
        scratch_shapes=_comm_scratch(comm),
    )(*[a for _, a in comm])


def _flash_fwd(qa, ka, va, *, q_blk0, k_blk0, v_blk0, W, n_pairs, B, S, scale, csh=None, crow=None, comm=(), name):
    t = ATT_TILE
    nq = S // t
    P = PAIRS_PER_STEP_FWD
    decay = csh is not None
    split = W == LANES
    assert n_pairs % P == 0 and q_blk0 % P == 0 and k_blk0 % P == 0 and v_blk0 % P == 0
    n_c, kinds = len(comm), [k for k, _ in comm]
    n_in = 5 if decay else 3
    n_steps = (B, n_pairs // P, nq)

    def body(*refs):
        c_in, c_out = refs[n_in:n_in + n_c], refs[n_in + n_c + 2:n_in + 2 * n_c + 2]
        sems = refs[n_in + 2 * n_c + 4:]
        refs = refs[:n_in] + refs[n_in + n_c:n_in + n_c + 2] + refs[n_in + 2 * n_c + 2:n_in + 2 * n_c + 4]
        if decay:
            q_ref, k_ref, v_ref, csh_ref, crow_ref, o_ref, lse_ref, m_s, acc_s = refs
        else:
            q_ref, k_ref, v_ref, o_ref, lse_ref, m_s, acc_s = refs
        g, i = pl.program_id(1), pl.program_id(2)
        if n_c:
            place = _mesh_place()
            ids = [pl.program_id(ax) for ax in range(3)]

            @pl.when((ids[0] == 0) & (ids[1] == 0) & (ids[2] == 0))
            def _():
                _comm_start(kinds, c_in, c_out, sems, place)

        lo = _low_half((t, LANES))
        qv = q_ref[...]
        qh = []
        for pr in range(P):
            qp = qv[:, pr * W:(pr + 1) * W]
            qh += [jnp.where(lo, qp, jnp.zeros_like(qp)), jnp.where(lo, jnp.zeros_like(qp), qp)] if split \
                else [qp[:, :LANES], qp[:, LANES:]]
        if decay:
            cq = [jnp.broadcast_to(_head_column(csh_ref[...], 2 * P * g + hd), (t, LANES)) for hd in range(2 * P)]
        m_s[...] = jnp.full(m_s.shape, NEG_INF, F32)
        acc_s[...] = jnp.zeros(acc_s.shape, F32)

        def step(j, masked):
            rows = pl.ds(pl.multiple_of(j * t, t), t)
            kb, vb = k_ref[rows, :], v_ref[rows, :]
            for pr in range(P):
                kp, vp = kb[:, pr * W:(pr + 1) * W], vb[:, pr * LANES:(pr + 1) * LANES]
                ones = jnp.ones_like(vp)
                vaug = [jnp.where(lo, vp, ones), jnp.where(lo, ones, vp)]
                for half in range(2):
                    hd = 2 * pr + half
                    kh = kp if split else kp[:, half * LANES:(half + 1) * LANES]
                    s = lax.dot_general(qh[hd], kh, NT, preferred_element_type=F32) * scale
                    if decay:
                        s = s + _widen(cq[hd], t) - crow_ref[hd, j]
                    if masked:
                        s = _causal_mask(s)
                    m_prev = m_s[hd]
                    m_new = jnp.maximum(m_prev, jnp.max(s, -1, keepdims=True))
                    p = jnp.exp(s - _widen(m_new, t))
                    acc_s[hd] = jnp.exp(m_prev - m_new) * acc_s[hd] + lax.dot_general(
                        p.astype(BF16), vaug[half], NN, preferred_element_type=F32)
                    m_s[hd] = m_new

        def loop_body(j, carry):
            step(j, False)
            return carry

        lax.fori_loop(0, i, loop_body, 0)
        step(i, True)
        for pr in range(P):
            acc0, acc1 = acc_s[2 * pr], acc_s[2 * pr + 1]
            _, l0 = _both_halves(acc0, lo)
            l1, _ = _both_halves(acc1, lo)
            cols = slice(pr * LANES, (pr + 1) * LANES)
            o_ref[:, cols] = jnp.where(lo, acc0 / l0, acc1 / l1).astype(BF16)
            lse_ref[:, cols] = jnp.where(lo, m_s[2 * pr] + jnp.log(l0), m_s[2 * pr + 1] + jnp.log(l1))
        if n_c:
            @pl.when((ids[0] == n_steps[0] - 1) & (ids[1] == n_steps[1] - 1) & (ids[2] == n_steps[2] - 1))
            def _():
                _comm_wait(kinds, c_in, c_out, sems, place)

    in_specs = [pl.BlockSpec((t, P * W), lambda b, g, i: (b * nq + i, q_blk0 // P + g)),
                pl.BlockSpec((S, P * W), lambda b, g, i: (b, k_blk0 // P + g)),
                pl.BlockSpec((S, P * LANES), lambda b, g, i: (b, v_blk0 // P + g))]
    args = [qa, ka, va]
    if decay:
        in_specs += [pl.BlockSpec((None, t, LANES), lambda b, g, i: (b, i, 0)),
                     pl.BlockSpec((None, 2 * P, nq, 1, t), lambda b, g, i: (b, g, 0, 0, 0))]
        args += [csh, crow]
    out_spec = pl.BlockSpec((t, P * LANES), lambda b, g, i: (b * nq + i, g))
    res = pl.pallas_call(
        body, name=name, grid=n_steps, in_specs=in_specs + [HBM_SPEC] * n_c,
        out_specs=[out_spec, out_spec] + [HBM_SPEC] * n_c,
        out_shape=[jax.ShapeDtypeStruct((B * S, n_pairs * LANES), BF16),
                   jax.ShapeDtypeStruct((B * S, n_pairs * LANES), F32)] + _comm_out_shapes(comm),
        scratch_shapes=[pltpu.VMEM((2 * P, t, LANES), F32), pltpu.VMEM((2 * P, t, LANES), F32)]
        + (_comm_scratch(comm) if n_c else []),
        compiler_params=_cparams(*(("arbitrary",) * 3 if n_c else ("parallel",) * 3)),
    )(*args, *[a for _, a in comm])
    return res[0], res[1], list(res[2:])


def _flash_bwd(qa, ka, va, oa, doa, lsea, *, q_blk0, k_blk0, v_blk0, do_blk0, W, n_pairs, B, S, scale, qk_dtype,
               csh=None, crow=None, comm=(), name):
    t = ATT_TILE_BWD
    nq = S // t
    P = PAIRS_PER_STEP_BWD
    decay = csh is not None
    if decay:
        crow = crow.reshape(B, 2 * n_pairs, nq, 1, t)
    split = W == LANES
    assert n_pairs % P == 0 and q_blk0 % P == 0 and k_blk0 % P == 0 and v_blk0 % P == 0 and do_blk0 % P == 0
    n_c, kinds = len(comm), [k for k, _ in comm]
    n_in, n_out, n_scr = (8, 5, 8) if decay else (6, 3, 5)
    n_steps = (B, n_pairs // P, nq)

    def body(*refs):
        c_in = refs[n_in:n_in + n_c]
        c_out = refs[n_in + n_c + n_out:n_in + 2 * n_c + n_out]
        sems = refs[n_in + 2 * n_c + n_out + n_scr:]
        refs = (refs[:n_in] + refs[n_in + n_c:n_in + n_c + n_out]
                + refs[n_in + 2 * n_c + n_out:n_in + 2 * n_c + n_out + n_scr])
        if n_c:
            place = _mesh_place()
            ids = [pl.program_id(ax) for ax in range(3)]

            @pl.when((ids[0] == 0) & (ids[1] == 0) & (ids[2] == 0))
            def _():
                _comm_start(kinds, c_in, c_out, sems, place)

        if decay:
            (q_ref, k_ref, v_ref, o_ref, do_ref, lse_ref, csh_ref, crow_ref, dq_ref, dk_ref, dv_ref, dck_ref, dcq_ref,
             dq_s, lse_s, delta_s, dk_s, dv_s, cq_s, dcq_s, dck_s) = refs
        else:
            (q_ref, k_ref, v_ref, o_ref, do_ref, lse_ref, dq_ref, dk_ref, dv_ref,
             dq_s, lse_s, delta_s, dk_s, dv_s) = refs
        g, j = pl.program_id(1), pl.program_id(2)
        lo = _low_half((t, LANES))

        @pl.when(j == 0)
        def _():
            lo_s = _low_half((S, LANES))
            dq_s[...] = jnp.zeros(dq_s.shape, F32)
            for pr in range(P):
                cols = slice(pr * LANES, (pr + 1) * LANES)
                lse_s[2 * pr], lse_s[2 * pr + 1] = _both_halves(lse_ref[:, cols], lo_s)
                dd = do_ref[:, cols].astype(F32) * o_ref[:, cols].astype(F32)
                delta_s[2 * pr] = jnp.broadcast_to(jnp.sum(jnp.where(lo_s, dd, 0.0), -1, keepdims=True), (S, LANES))
                delta_s[2 * pr + 1] = jnp.broadcast_to(jnp.sum(jnp.where(lo_s, 0.0, dd), -1, keepdims=True),
                                                       (S, LANES))
            if decay:
                for hd in range(2 * P):
                    cq_s[hd] = jnp.broadcast_to(_head_column(csh_ref[...], 2 * P * g + hd), (S, LANES))
                dcq_s[...] = jnp.zeros(dcq_s.shape, F32)

        kb, vb = k_ref[...], v_ref[...]
        kh, vh = [], []
        for pr in range(P):
            kp, vp = kb[:, pr * W:(pr + 1) * W], vb[:, pr * LANES:(pr + 1) * LANES]
            zk, zv = jnp.zeros_like(kp), jnp.zeros_like(vp)
            kh += [jnp.where(lo, kp, zk), jnp.where(lo, zk, kp)] if split else [kp[:, :LANES], kp[:, LANES:]]
            vh += [jnp.where(lo, vp, zv), jnp.where(lo, zv, vp)]
        dk_s[...] = jnp.zeros(dk_s.shape, F32)
        dv_s[...] = jnp.zeros(dv_s.shape, F32)
        if decay:
            dck_s[...] = jnp.zeros(dck_s.shape, F32)

        def step(i, masked):
            rows = pl.ds(pl.multiple_of(i * t, t), t)
            qi, doi = q_ref[rows, :], do_ref[rows, :]
            for pr in range(P):
                qp, dop = qi[:, pr * W:(pr + 1) * W], doi[:, pr * LANES:(pr + 1) * LANES]
                for half in range(2):
                    hd = 2 * pr + half
                    qx = qp if split else qp[:, half * LANES:(half + 1) * LANES]
                    s = lax.dot_general(qx, kh[hd], NT, preferred_element_type=F32) * scale
                    if decay:
                        s = s + _widen(cq_s[hd, rows, :], t) - crow_ref[hd, j]
                    if masked:
                        s = _causal_mask(s)
                    p = jnp.exp(s - _widen(lse_s[hd, rows, :], t))
                    dv_s[hd] += lax.dot_general(p.astype(BF16), dop, TN, preferred_element_type=F32)
                    dp = lax.dot_general(dop, vh[hd], NT, preferred_element_type=F32)
                    ds = p * (dp - _widen(delta_s[hd, rows, :], t))
                    dss = (ds * scale).astype(BF16)
                    dk_s[hd] += lax.dot_general(dss, qx, TN, preferred_element_type=F32)
                    dqc = lax.dot_general(dss, kh[hd], NN, preferred_element_type=F32)
                    if split:
                        dq_s[rows, pr * W:(pr + 1) * W] += dqc
                    else:
                        dq_s[rows, hd * LANES:(hd + 1) * LANES] += dqc
                    if decay:
                        dck_s[hd] -= jnp.sum(ds, 0, keepdims=True)
                        part = ds[:, :LANES]
                        for c in range(1, t // LANES):
                            part = part + ds[:, c * LANES:(c + 1) * LANES]
                        dcq_s[hd, rows, :] += part

        def loop_body(i, carry):
            step(i, False)
            return carry

        step(j, True)
        lax.fori_loop(j + 1, nq, loop_body, 0)
        for pr in range(P):
            if split:
                dk_ref[:, pr * W:(pr + 1) * W] = jnp.where(lo, dk_s[2 * pr], dk_s[2 * pr + 1]).astype(dk_ref.dtype)
            else:
                for half in range(2):
                    hd = 2 * pr + half
                    dk_ref[:, hd * LANES:(hd + 1) * LANES] = dk_s[hd].astype(dk_ref.dtype)
            dv_ref[:, pr * LANES:(pr + 1) * LANES] = jnp.where(lo, dv_s[2 * pr], dv_s[2 * pr + 1]).astype(BF16)
        if decay:
            dck_ref[...] = dck_s[...]

        @pl.when(j == nq - 1)
        def _():
            dq_ref[...] = dq_s[...].astype(dq_ref.dtype)
            if decay:
                for hd in range(2 * P):
                    dcq_ref[hd] = jnp.sum(dcq_s[hd].T, 0, keepdims=True)

        if n_c:
            @pl.when((ids[0] == n_steps[0] - 1) & (ids[1] == n_steps[1] - 1) & (ids[2] == n_steps[2] - 1))
            def _():
                _comm_wait(kinds, c_in, c_out, sems, place)

    full = lambda w, blk0: pl.BlockSpec((S, P * w), lambda b, g, j: (b, blk0 // P + g))
    blk = lambda w, blk0: pl.BlockSpec((t, P * w), lambda b, g, j: (b * nq + j, blk0 // P + g))
    in_specs = [full(W, q_blk0), blk(W, k_blk0), blk(LANES, v_blk0), full(LANES, 0), full(LANES, do_blk0),
                full(LANES, 0)]
    args = [qa, ka, va, oa, doa, lsea]
    T = B * S
    out_specs = [full(W, 0), blk(W, 0), blk(LANES, 0)]
    out_shape = [jax.ShapeDtypeStruct((T, n_pairs * W), qk_dtype), jax.ShapeDtypeStruct((T, n_pairs * W), qk_dtype),
                 jax.ShapeDtypeStruct((T, n_pairs * LANES), BF16)]
    per_head = lambda rows: pltpu.VMEM((2 * P, rows, LANES), F32)
    scratch = [pltpu.VMEM((S, P * W), F32), per_head(S), per_head(S), per_head(t), per_head(t)]
    if decay:
        in_specs += [pl.BlockSpec((None, S, LANES), lambda b, g, j: (b, 0, 0)),
                     pl.BlockSpec((None, 2 * P, nq, 1, t), lambda b, g, j: (b, g, 0, 0, 0))]
        args += [csh, crow]
        out_specs += [pl.BlockSpec((None, 2 * P, None, 1, t), lambda b, g, j: (b, g, j, 0, 0)),
                      pl.BlockSpec((None, 2 * P, 1, S), lambda b, g, j: (b, g, 0, 0))]
        out_shape += [jax.ShapeDtypeStruct((B, 2 * n_pairs, nq, 1, t), F32),
                      jax.ShapeDtypeStruct((B, 2 * n_pairs, 1, S), F32)]
        scratch += [per_head(S), per_head(S), pltpu.VMEM((2 * P, 1, t), F32)]
    res = pl.pallas_call(
        body, name=name, grid=n_steps, in_specs=in_specs + [HBM_SPEC] * n_c,
        out_specs=out_specs + [HBM_SPEC] * n_c, out_shape=out_shape + _comm_out_shapes(comm),
        scratch_shapes=scratch + (_comm_scratch(comm) if n_c else []),
        compiler_params=_cparams(*(("arbitrary",) * 3 if n_c else ("parallel", "parallel", "arbitrary"))),
    )(*args, *[a for _, a in comm])
    return tuple(res[:n_out]) + (list(res[n_out:]),)


def _swa_common(q_ref, kp_ref, ko_ref, vp_ref, vo_ref, n):
    Q = BLOCK_Q
    lo = _low_half((Q, LANES))
    lo2 = _low_half((2 * Q, LANES))
    kk = jnp.concatenate([kp_ref[...], ko_ref[...]], axis=0)
    vv = jnp.concatenate([vp_ref[...], vo_ref[...]], axis=0)
    kdup = [x.astype(BF16) for x in _both_halves(kk, lo2)]
    vdup = [x.astype(BF16) for x in _both_halves(vv, lo2)]
    a = lax.broadcasted_iota(jnp.int32, (SWA_GROUP * Q, 2 * Q), 0) % Q
    col = lax.broadcasted_iota(jnp.int32, (SWA_GROUP * Q, 2 * Q), 1)
    dist = a + Q - col
    valid = (dist >= 0) & (dist < SWA_WINDOW) & ((col >= Q) | (n > 0))
    qv = q_ref[...]
    qm = []
    for a_head in range(SWA_HEADS):
        qp = qv[:, (a_head // 2) * LANES:(a_head // 2 + 1) * LANES]
        keep = lo if a_head % 2 == 0 else jnp.logical_not(lo)
        qm.append(jnp.where(keep, qp, 0.0).astype(BF16))
    qs = [jnp.concatenate(qm[g * SWA_GROUP:(g + 1) * SWA_GROUP], axis=0) for g in range(SWA_KV_HEADS)]
    return lo, lo2, kdup, vdup, valid, qs


def _swa_group_logits(g, qs, kdup, valid, bias_ref):
    heads = slice(g * SWA_GROUP, (g + 1) * SWA_GROUP)
    s = lax.dot_general(qs[g], kdup[g], NT, preferred_element_type=F32) * (HEAD_DIM ** -0.5)
    s = s + bias_ref[heads].reshape(SWA_GROUP * BLOCK_Q, 2 * BLOCK_Q)
    return heads, jnp.where(valid, s, NEG_INF)


def _pair_halves(x, lo):
    Q = BLOCK_Q
    return [jnp.where(lo, x[2 * pr * Q:(2 * pr + 1) * Q], x[(2 * pr + 1) * Q:(2 * pr + 2) * Q])
            for pr in range(SWA_GROUP // 2)]


def _swa_in_specs(nb):
    Q = BLOCK_Q
    own = lambda blk: (lambda b, n: (b * nb + n, blk))
    prev = lambda blk: (lambda b, n: (b * nb + jnp.maximum(n - 1, 0), blk))
    kb, vb = EV_KS[0] // LANES, EV_VS[0] // LANES
    return [pl.BlockSpec((Q, SWA_HEADS * HEAD_DIM), own(0)), pl.BlockSpec((Q, LANES), prev(kb)),
            pl.BlockSpec((Q, LANES), own(kb)), pl.BlockSpec((Q, LANES), prev(vb)), pl.BlockSpec((Q, LANES), own(vb))]


def _swa_fwd(h, bias, sinkcol, *, B, S, comm=(), name):
    Q = BLOCK_Q
    nb = S // Q
    n_c, kinds = len(comm), [k for k, _ in comm]

    def body(*refs):
        c_in, c_out, sems = refs[7:7 + n_c], refs[9 + n_c:9 + 2 * n_c], refs[9 + 2 * n_c:]
        q_ref, kp_ref, ko_ref, vp_ref, vo_ref, bias_ref, sink_ref = refs[:7]
        o_ref, lse_ref = refs[7 + n_c:9 + n_c]
        if n_c:
            place = _mesh_place()
            ids = [pl.program_id(0), pl.program_id(1)]

            @pl.when((ids[0] == 0) & (ids[1] == 0))
            def _():
                _comm_start(kinds, c_in, c_out, sems, place)

        lo, lo2, kdup, vdup, valid, qs = _swa_common(q_ref, kp_ref, ko_ref, vp_ref, vo_ref, pl.program_id(1))
        lane = lax.broadcasted_iota(jnp.int32, (Q, LANES), 1)
        lse_blk = jnp.zeros((Q, LANES), F32)
        pairs = []
        lo4 = _low_half((SWA_GROUP * Q, LANES))
        for g in range(SWA_KV_HEADS):
            heads, s = _swa_group_logits(g, qs, kdup, valid, bias_ref)
            sink = jnp.broadcast_to(sink_ref[heads].reshape(SWA_GROUP * Q, 1), (SWA_GROUP * Q, LANES))
            m = jnp.maximum(jnp.max(s, -1, keepdims=True), sink)
            p = jnp.exp(s - _widen(m, 2 * Q))
            vaug = jnp.where(lo2, vdup[g], jnp.ones_like(vdup[g]))
            pv = lax.dot_general(p.astype(BF16), vaug, NN, preferred_element_type=F32)
            rolled = pltpu.roll(pv, HEAD_DIM, 1)
            l = jnp.where(lo4, rolled, pv) + jnp.exp(sink - m)
            out = pv / l
            lse_g = m + jnp.log(l)
            for i in range(SWA_GROUP):
                lse_blk = jnp.where(lane == g * SWA_GROUP + i, lse_g[i * Q:(i + 1) * Q], lse_blk)
            shifted = pltpu.roll(out, HEAD_DIM, 1)
            pairs += [jnp.where(lo, out[2 * pr * Q:(2 * pr + 1) * Q], shifted[(2 * pr + 1) * Q:(2 * pr + 2) * Q])
                      for pr in range(SWA_GROUP // 2)]
        o_ref[...] = jnp.concatenate(pairs, axis=1).astype(BF16)
        lse_ref[...] = lse_blk
        if n_c:
            @pl.when((ids[0] == B - 1) & (ids[1] == nb - 1))
            def _():
                _comm_wait(kinds, c_in, c_out, sems, place)

    whole = lambda shape: pl.BlockSpec(shape, lambda b, n: (0,) * len(shape))
    res = pl.pallas_call(
        body, name=name, grid=(B, nb),
        in_specs=_swa_in_specs(nb) + [whole((SWA_HEADS, Q, 2 * Q)), whole((SWA_HEADS, Q, 1))] + [HBM_SPEC] * n_c,
        out_specs=[pl.BlockSpec((Q, SWA_HEADS * HEAD_DIM), lambda b, n: (b * nb + n, 0)),
                   pl.BlockSpec((Q, LANES), lambda b, n: (b * nb + n, 0))] + [HBM_SPEC] * n_c,
        out_shape=[jax.ShapeDtypeStruct((B * S, SWA_HEADS * HEAD_DIM), BF16),
                   jax.ShapeDtypeStruct((B * S, LANES), F32)] + _comm_out_shapes(comm),
        scratch_shapes=_comm_scratch(comm) if n_c else [],
        compiler_params=_cparams(*(("arbitrary",) * 2 if n_c else ("parallel",) * 2)),
    )(h, h, h, h, h, bias, sinkcol, *[a for _, a in comm])
    return res[0], res[1], list(res[2:])


def _swa_bwd(h, o, do, lse, bias, sinkcol, *, do_blk0, B, S, name):
    Q = BLOCK_Q
    nb = S // Q
    scale = HEAD_DIM ** -0.5

    def body(q_ref, kp_ref, ko_ref, vp_ref, vo_ref, o_ref, do_ref, lse_ref, bias_ref, sink_ref,
             dq_ref, dko_ref, dkp_ref, dvo_ref, dvp_ref, dbias_ref, dsink_ref):
        @pl.when((pl.program_id(0) == 0) & (pl.program_id(1) == 0))
        def _():
            dbias_ref[...] = jnp.zeros_like(dbias_ref)
            dsink_ref[...] = jnp.zeros_like(dsink_ref)

        lo, lo2, kdup, vdup, valid, qs = _swa_common(q_ref, kp_ref, ko_ref, vp_ref, vo_ref, pl.program_id(1))
        lse_blk = lse_ref[...]
        dkk, dvv, dq_pairs = [], [], []
        for g in range(SWA_KV_HEADS):
            heads, s = _swa_group_logits(g, qs, kdup, valid, bias_ref)
            lse_g = jnp.concatenate([_head_column(lse_blk, g * SWA_GROUP + i) for i in range(SWA_GROUP)], axis=0)
            p = jnp.exp(s - lse_g)
            do_g, o_g = [], []
            for i in range(SWA_GROUP):
                cols = slice((g * SWA_GROUP + i) // 2 * LANES, ((g * SWA_GROUP + i) // 2 + 1) * LANES)
                do_p = do_ref[:, cols]
                do_g.append(jnp.where(lo if i % 2 == 0 else jnp.logical_not(lo), do_p, jnp.zeros_like(do_p)))
                o_g.append(o_ref[:, cols])
            doh, oh = jnp.concatenate(do_g, axis=0), jnp.concatenate(o_g, axis=0)
            delta = jnp.sum(doh.astype(F32) * oh.astype(F32), -1, keepdims=True)
            dp = lax.dot_general(doh, vdup[g], NT, preferred_element_type=F32)
            ds = p * (dp - delta)
            dbias_ref[heads] += ds.reshape(SWA_GROUP, Q, 2 * Q)
            dsink_ref[heads] -= (jnp.exp(sink_ref[heads].reshape(SWA_GROUP * Q, 1) - lse_g)
                                 * delta).reshape(SWA_GROUP, Q, 1)
            dss = (ds * scale).astype(BF16)
            dq_pairs += _pair_halves(lax.dot_general(dss, kdup[g], NN, preferred_element_type=F32), lo)
            dkk.append(lax.dot_general(dss, qs[g], TN, preferred_element_type=F32))
            dvv.append(lax.dot_general(p.astype(BF16), doh, TN, preferred_element_type=F32))
        dq_ref[...] = jnp.concatenate(dq_pairs, axis=1).astype(BF16)
        fold = lambda x: x + pltpu.roll(x, HEAD_DIM, 1)
        dk_blk = jnp.where(lo2, fold(dkk[0]), fold(dkk[1]))
        dv_blk = jnp.where(lo2, fold(dvv[0]), fold(dvv[1]))
        dkp_ref[...] = dk_blk[:Q]
        dko_ref[...] = dk_blk[Q:]
        dvp_ref[...] = dv_blk[:Q]
        dvo_ref[...] = dv_blk[Q:]

    whole = lambda shape: pl.BlockSpec(shape, lambda b, n: (0,) * len(shape))
    wide = lambda blk: pl.BlockSpec((Q, SWA_HEADS * HEAD_DIM), lambda b, n: (b * nb + n, blk))
    narrow = pl.BlockSpec((Q, LANES), lambda b, n: (b * nb + n, 0))
    kv_shape = jax.ShapeDtypeStruct((B * S, LANES), F32)
    return pl.pallas_call(
        body, name=name, grid=(B, nb),
        in_specs=_swa_in_specs(nb) + [wide(0), wide(do_blk0), narrow, whole((SWA_HEADS, Q, 2 * Q)),
                                      whole((SWA_HEADS, Q, 1))],
        out_specs=[wide(0), narrow, narrow, narrow, narrow, whole((SWA_HEADS, Q, 2 * Q)), whole((SWA_HEADS, Q, 1))],
        out_shape=[jax.ShapeDtypeStruct((B * S, SWA_HEADS * HEAD_DIM), BF16), kv_shape, kv_shape, kv_shape, kv_shape,
                   jax.ShapeDtypeStruct((SWA_HEADS, Q, 2 * Q), F32), jax.ShapeDtypeStruct((SWA_HEADS, Q, 1), F32)],
        compiler_params=_cparams("arbitrary", "arbitrary"),
    )(h, h, h, h, h, o, do, lse, bias, sinkcol)


def _bias_bucket_sum(dbias, bucket, *, name):
    def body(d_ref, b_ref, o_ref):
        dbv, bk = d_ref[...], b_ref[...]
        lane = lax.broadcasted_iota(jnp.int32, (SWA_HEADS, LANES), 1)
        out = jnp.zeros((SWA_HEADS, LANES), F32)
        for b in range(REL_BUCKETS):
            part = jnp.sum(jnp.where(bk == b, dbv, 0.0), axis=1)
            tot = jnp.sum(part, axis=-1, keepdims=True)
            out = out + jnp.where(lane == b, tot, 0.0)
        o_ref[...] = out

    return pl.pallas_call(
        body, name=name, out_shape=jax.ShapeDtypeStruct((SWA_HEADS, LANES), F32),
        compiler_params=pltpu.CompilerParams(vmem_limit_bytes=VMEM_LIMIT_BYTES),
    )(dbias, bucket)


def _adamw_update(w, g, m, v):
    m_new = ADAM_B1 * m + (1.0 - ADAM_B1) * g
    v_new = ADAM_B2 * v + (1.0 - ADAM_B2) * jnp.square(g)
    m_hat = m_new / (1.0 - ADAM_B1 ** ADAM_STEP)
    v_hat = v_new / (1.0 - ADAM_B2 ** ADAM_STEP)
    return -ADAM_LR * (m_hat / (jnp.sqrt(v_hat) + ADAM_EPS) + ADAM_WD * w), m_new, v_new


def _adamw(w, g, m, v, *, name):
    def body(w_ref, g_ref, m_ref, v_ref, d_ref, nm_ref, nv_ref):
        d_ref[...], nm_ref[...], nv_ref[...] = _adamw_update(w_ref[...], g_ref[...], m_ref[...], v_ref[...])

    return pl.pallas_call(
        body, name=name, out_shape=[jax.ShapeDtypeStruct(w.shape, F32)] * 3,
        compiler_params=pltpu.CompilerParams(vmem_limit_bytes=VMEM_LIMIT_BYTES),
    )(w, g, m, v)


def _adamw_slots(w, parts, m, v, *, name):
    R, C = w.shape
    tr = R if R <= 512 else _pick(R, (256, 128))

    def body(w_ref, p_ref, m_ref, v_ref, g_ref, d_ref, nm_ref, nv_ref):
        g = p_ref[0].astype(F32)
        for j in range(1, N_DEV):
            g = g + p_ref[j].astype(F32)
        g_ref[...] = g
        d_ref[...], nm_ref[...], nv_ref[...] = _adamw_update(w_ref[...], g, m_ref[...], v_ref[...])

    spec = pl.BlockSpec((tr, C), lambda i: (i, 0))
    return pl.pallas_call(
        body, name=name, grid=(R // tr,),
        in_specs=[spec, pl.BlockSpec((N_DEV, tr, C), lambda i: (0, i, 0)), spec, spec], out_specs=[spec] * 4,
        out_shape=[jax.ShapeDtypeStruct((R, C), F32)] * 4, compiler_params=_cparams("parallel"),
    )(w, parts, m, v)


def _all_gather_hbm(blocks, *, name):
    n = len(blocks)

    def body(*refs):
        x_refs, out_refs = refs[:n], refs[n:2 * n]
        send_sems, recv_sems, local_sems = refs[2 * n:]
        x, y, c, _ = _mesh_place()
        me, sibling = (x, y, c), (x, y, 1 - c)
        chips = [(1 - x, y), (x, 1 - y), (1 - x, 1 - y)]

        def copy(w, k, blk, to, src=None):
            px, py, pc = blk
            slot = out_refs[w].at[4 * px + 2 * py + pc]
            return pltpu.make_async_remote_copy(
                src_ref=slot if src is None else src, dst_ref=slot,
                send_sem=send_sems.at[w, k], recv_sem=recv_sems.at[w, k], device_id=to, device_id_type=MESH_ID)

        mine = [pltpu.make_async_copy(x_refs[w], out_refs[w].at[4 * x + 2 * y + c], local_sems.at[w])
                for w in range(n)]
        for cp in mine:
            cp.start()
        first = []
        for w in range(n):
            first.append(copy(w, 0, me, sibling, src=x_refs[w]))
            first += [copy(w, 1 + j, me, (*chip, c), src=x_refs[w]) for j, chip in enumerate(chips)]
        for cp in first:
            cp.start()
        passed = []
        for j, chip in enumerate(chips):
            for w in range(n):
                copy(w, 1 + j, (*chip, c), me).wait_recv()
                fwd = copy(w, 4 + j, (*chip, c), sibling)
                fwd.start()
                passed.append(fwd)
        for w in range(n):
            copy(w, 0, sibling, me).wait_recv()
            for j, chip in enumerate(chips):
                copy(w, 4 + j, (*chip, 1 - c), me).wait_recv()
        for cp in first + passed:
            cp.wait_send()
        for cp in mine:
            cp.wait()

    return pl.pallas_call(
        body, name=name, out_shape=[jax.ShapeDtypeStruct((N_DEV,) + b.shape, b.dtype) for b in blocks],
        in_specs=[HBM_SPEC] * n, out_specs=[HBM_SPEC] * n,
        scratch_shapes=[pltpu.SemaphoreType.DMA((n, 7)), pltpu.SemaphoreType.DMA((n, 7)),
                        pltpu.SemaphoreType.DMA((n,))],
    )(*blocks)


def _all_reduce_small(block, *, name):
    R, W = block.shape

    def body(x_ref, out_ref, buf, send_sems, recv_sems):
        x, y, c, me = _mesh_place()
        copies = []
        for k, (peer, _) in enumerate(_peers(x, y, c)):
            copies.append(pltpu.make_async_remote_copy(
                src_ref=x_ref, dst_ref=buf.at[me], send_sem=send_sems.at[k], recv_sem=recv_sems.at[k],
                device_id=peer, device_id_type=MESH_ID))
        for cp in copies:
            cp.start()
        buf[me] = x_ref[...]
        for cp in copies:
            cp.wait_recv()
        for cp in copies:
            cp.wait_send()
        acc = buf[0]
        for j in range(1, N_DEV):
            acc = acc + buf[j]
        out_ref[...] = acc

    return pl.pallas_call(
        body, name=name, out_shape=jax.ShapeDtypeStruct((R, W), F32),
        in_specs=[VMEM_SPEC], out_specs=VMEM_SPEC,
        scratch_shapes=[pltpu.VMEM((N_DEV, R, W), F32), pltpu.SemaphoreType.DMA((7,)), pltpu.SemaphoreType.DMA((7,))],
    )(block)


def _assemble(name, g):
    if BIG_AXIS[name] == 2:
        return jnp.concatenate([g[j] for j in range(N_DEV)], axis=1)
    return g.reshape(N_DEV * g.shape[1], g.shape[2])


def _split_for_devices(name, g):
    if BIG_AXIS[name] == 2:
        b = g.shape[1] // N_DEV
        return jnp.stack([g[:, j * b:(j + 1) * b] for j in range(N_DEV)]).astype(BF16)
    return g.reshape(N_DEV, g.shape[0] // N_DEV, g.shape[1]).astype(BF16)


def _layer_weight_keys(i):
    j = i // 2
    mixer = [('ev_w_in', j), ('ev_w_uq', j), ('ev_w_ukv', j), ('ev_w_out', j)] if i % 2 == 0 \
        else [('od_w_in', j), ('od_w_out', j)]
    return mixer + [('w_up', i), ('w_down', i), ('ple_w_proj', i), ('ple_w_gate', i)]


def _weight_layer(key):
    name, idx = key
    return 2 * idx if name.startswith('ev_') else 2 * idx + 1 if name.startswith('od_') else idx


FIRST_GATHER = [('ev_w_in', 0), ('ev_w_uq', 0), ('ev_w_ukv', 0), ('ev_w_out', 0)]
FWD_CARRIERS = {
    'l0_mla': [('w_up', 0), ('ple_w_proj', 0), ('ple_w_gate', 0)],
    'l0_swa': [('w_down', 0)],
    'l0_out_ln1': [('od_w_out', 0)],
    'l0_up': [('od_w_in', 0)],
    'l0_down_ln2': [('w_up', 1)],
    'l0_ple_gate': [('ple_w_proj', 1), ('ple_w_gate', 1)],
    'l1_fox': [('w_down', 1), ('ev_w_in', 1), ('ev_w_uq', 1), ('ev_w_ukv', 1), ('ev_w_out', 1), ('w_up', 2)],
    'l1_up': [('w_down', 2)],
    'l1_down_ln2': [('ple_w_proj', 2), ('ple_w_gate', 2)],
    'l2_mla': [('od_w_in', 1), ('od_w_out', 1)],
    'l2_swa': [('w_up', 3)],
    'l2_up': [('w_down', 3)],
    'l2_down_ln2': [('ple_w_proj', 3), ('ple_w_gate', 3)],
}


class _MeshExchange:
    def __init__(self, shards):
        self.shards = shards
        self.weights = {i: {} for i in range(DEPTH)}
        self.pending = []
        self.in_flight = []
        self.received = {}
        got = _all_gather_hbm([self.shards[k] for k in FIRST_GATHER], name="gather_first")
        self._landed(FIRST_GATHER, got)

    def _landed(self, keys, gathered):
        for k, g in zip(keys, gathered):
            self.weights[_weight_layer(k)][k[0]] = _assemble(k[0], g)

    def layer_weights(self, i):
        return self.weights[i]

    def carry(self, kernel_name):
        return [("gather", self.shards[k]) for k in FWD_CARRIERS.get(kernel_name, [])]

    def carried(self, kernel_name, outs):
        self._landed(FWD_CARRIERS.get(kernel_name, []), outs)

    def push_grads(self, grads):
        self.pending += [(k, _split_for_devices(k[0], g)) for k, g in grads.items()]

    def bwd_items(self):
        self.in_flight, self.pending = self.pending, []
        return [("scatter", parts) for _, parts in self.in_flight]

    def bwd_done(self, outs):
        for (k, _), got in zip(self.in_flight, outs):
            self.received[k] = got
        self.in_flight = []

    def finish(self):
        if self.pending:
            outs = _exchange(self.bwd_items(), name="scatter_rest")
            self.bwd_done(outs)
        return self.received


PACK_ROWS = 8


def _pack_small(vals):
    flat = jnp.concatenate([vals[n].reshape(-1).astype(F32) for n in SMALL])
    pad = (-flat.shape[0]) % (PACK_ROWS * LANES)
    return jnp.pad(flat, (0, pad)).reshape(-1, LANES)


def _unpack_small(block, shapes):
    flat = block.reshape(-1)
    out, off = {}, 0
    for n in SMALL:
        sz = math.prod(shapes[n])
        out[n] = flat[off:off + sz].reshape(shapes[n])
        off += sz
    return out


def _rope_tables(S):
    half = MLA_ROPE // 2
    inv = 1.0 / (ROPE_THETA ** (jnp.arange(0, MLA_ROPE, 2, dtype=F32) / MLA_ROPE))
    ang = jnp.arange(S, dtype=F32)[:, None] * inv[None, :]
    cos, sin = jnp.cos(ang), jnp.sin(ang)
    zeros = jnp.zeros((S, half), F32)
    tail = jnp.zeros((S, LANES - MLA_QK), F32)

    def block(rope_part, nope_val):
        return jnp.concatenate([jnp.full((S, MLA_NOPE), nope_val, F32), rope_part, tail], -1)

    a_r = jnp.concatenate([cos, cos], -1)
    bm_r = jnp.concatenate([-sin, zeros], -1)
    bp_r = jnp.concatenate([zeros, sin], -1)
    q_tabs = tuple(block(r, v) for r, v in ((a_r, 1.0), (bm_r, 0.0), (bp_r, 0.0)))
    k_tabs = tuple(block(r, 0.0) for r in (a_r, bm_r, bp_r))
    return q_tabs, k_tabs


def _t5_bucket(dist):
    exact = REL_BUCKETS // 2
    d = jnp.maximum(dist, 1).astype(F32)
    large = exact + (jnp.log(d / exact) / math.log(REL_MAX_DIST / exact) * (REL_BUCKETS - exact)).astype(jnp.int32)
    large = jnp.minimum(large, REL_BUCKETS - 1)
    return jnp.where(dist < exact, dist, large)


def _swa_bucket_table():
    a = jnp.arange(BLOCK_Q)[:, None]
    col = jnp.arange(2 * BLOCK_Q)[None, :]
    return _t5_bucket(jnp.maximum(a + BLOCK_Q - col, 0)).astype(jnp.int32)


def _even_weights(W):
    w = W['ev_w_in']
    c_kv1 = MLA_Q_LORA + MLA_KV_LORA
    c_kr1 = c_kv1 + MLA_ROPE
    c_qs1 = c_kr1 + SWA_HEADS * HEAD_DIM
    zeros = lambda n: jnp.zeros((D_MODEL, n), w.dtype)
    w_in = jnp.concatenate([w[:, c_kr1:c_qs1], w[:, :c_kv1], w[:, c_qs1:], zeros(KR_LANE0), w[:, c_kv1:c_kr1],
                            zeros(LANES - KR_LANE0 - MLA_ROPE)], axis=1)
    uq = W['ev_w_uq'].reshape(MLA_Q_LORA, MLA_HEADS, MLA_QK)
    w_uq = jnp.pad(uq, ((0, 0), (0, 0), (0, LANES - MLA_QK))).reshape(MLA_Q_LORA, MLA_HEADS * LANES)
    ukv = W['ev_w_ukv'].reshape(MLA_KV_LORA, MLA_HEADS, MLA_NOPE + MLA_V)
    w_k = jnp.pad(ukv[..., :MLA_NOPE], ((0, 0), (0, 0), (0, LANES - MLA_NOPE))).reshape(MLA_KV_LORA, -1)
    w_v = ukv[..., MLA_NOPE:].reshape(MLA_KV_LORA, MLA_HEADS * MLA_V)
    return w_in, w_uq, w_k, w_v, W['ev_w_out']


def _even_in_grad_unpad(dw):
    kr0 = EV_KR[0] + KR_LANE0
    return jnp.concatenate([dw[:, EV_CQ[0]:EV_CKV[1]], dw[:, kr0:kr0 + MLA_ROPE], dw[:, EV_QS[0]:EV_QS[1]],
                            dw[:, EV_KS[0]:EV_VS[1]]], axis=1)


def _even_fwd(xb, W, P, i, B, S, tabs, xchg, tag):
    j = i // 2
    q_tabs, k_tabs, bias, sinkcol = tabs
    w_in, w_uq, w_k, w_v, w_out = _even_weights(W)
    h = _mm(xb, w_in, name=f"{tag}_in")
    cqn, ckvn, rq, rkv = _even_norms(h, P['ev_q_norm'][j][None], P['ev_kv_norm'][j][None], name=f"{tag}_norms")
    q = _rope(_mm(cqn, w_uq, name=f"{tag}_uq"), q_tabs, S, sign=1.0, name=f"{tag}_ropeq")
    knp = _mm(ckvn, w_k, out_dtypes=(BF16,), name=f"{tag}_uk")
    v = _mm(ckvn, w_v, out_dtypes=(BF16,), name=f"{tag}_uv")
    k = _mla_keys(knp, h, k_tabs, S, name=f"{tag}_keys")
    o_mla, lse_mla, got = _flash_fwd(q, k, v, q_blk0=0, k_blk0=0, v_blk0=0, W=2 * LANES, n_pairs=MLA_HEADS // 2,
                                     B=B, S=S, scale=MLA_QK ** -0.5, comm=xchg.carry(f"{tag}_mla"), name=f"{tag}_mla")
    xchg.carried(f"{tag}_mla", got)
    o_swa, lse_swa, got = _swa_fwd(h, bias, sinkcol, B=B, S=S, comm=xchg.carry(f"{tag}_swa"), name=f"{tag}_swa")
    xchg.carried(f"{tag}_swa", got)
    o_cat = jnp.concatenate([o_mla, o_swa], axis=-1)
    res = dict(h=h, cqn=cqn, ckvn=ckvn, rq=rq, rkv=rkv, q=q, k=k, v=v, o_mla=o_mla, lse_mla=lse_mla,
               o_swa=o_swa, lse_swa=lse_swa, o_cat=o_cat)
    return (o_cat, w_out), res


def _shift_prev(own, prev, B, S):
    prev = prev.reshape(B, S, LANES)
    shifted = jnp.concatenate([prev[:, BLOCK_Q:], jnp.zeros_like(prev[:, :BLOCK_Q])], axis=1)
    return (own + shifted.reshape(B * S, LANES)).astype(BF16)


def _even_bwd(dmb, dz1, xb, W, P, j, B, S, tabs, res, xchg, tag):
    q_tabs, k_tabs, bias, sinkcol = tabs
    w_in, w_uq, w_k, w_v, w_out = _even_weights(W)
    g = {}
    g['ev_w_out'] = _mm_tn(res['o_cat'], dmb, name=f"{tag}_dwout")
    do = _mm(dmb, w_out, trans_b=True, out_dtypes=(BF16,), name=f"{tag}_do")
    dq, dk, dv, got = _flash_bwd(res['q'], res['k'], res['v'], res['o_mla'], do, res['lse_mla'], q_blk0=0, k_blk0=0,
                                 v_blk0=0, do_blk0=0, W=2 * LANES, n_pairs=MLA_HEADS // 2, B=B, S=S,
                                 scale=MLA_QK ** -0.5, qk_dtype=F32, comm=xchg.bwd_items(), name=f"{tag}_mla_bwd")
    xchg.bwd_done(got)
    dq_pre = _rope(dq, q_tabs, S, sign=-1.0, name=f"{tag}_ropeq_bwd")
    dw_uq = _mm_tn(res['cqn'], dq_pre, name=f"{tag}_dwuq")
    g['ev_w_uq'] = dw_uq.reshape(MLA_Q_LORA, MLA_HEADS, LANES)[..., :MLA_QK].reshape(MLA_Q_LORA, MLA_HEADS * MLA_QK)
    dcqn = _mm(dq_pre, w_uq, trans_b=True, name=f"{tag}_dcqn")
    dw_k = _mm_tn(res['ckvn'], dk, name=f"{tag}_dwuk").reshape(MLA_KV_LORA, MLA_HEADS, LANES)[..., :MLA_NOPE]
    dw_v = _mm_tn(res['ckvn'], dv, name=f"{tag}_dwuv").reshape(MLA_KV_LORA, MLA_HEADS, MLA_V)
    g['ev_w_ukv'] = jnp.concatenate([dw_k, dw_v], axis=-1).reshape(MLA_KV_LORA, MLA_HEADS * (MLA_NOPE + MLA_V))
    dckvn_v = _mm(dv, w_v, trans_b=True, name=f"{tag}_dckvn_v")
    dckvn = _mm(dk, w_k, trans_b=True, extras=(dckvn_v,), epilogue=lambda acc, r: (acc + r,), name=f"{tag}_dckvn")
    dkr_pre = _mla_rope_key_grad(dk, k_tabs, S, name=f"{tag}_ropek_bwd")
    dqs, dko, dkp, dvo, dvp, dbias, dsink = _swa_bwd(res['h'], res['o_swa'], do, res['lse_swa'], bias, sinkcol,
                                                     do_blk0=1, B=B, S=S, name=f"{tag}_swa_bwd")
    dh, dgq, dgkv = _even_in_bwd(res['h'], res['rq'], res['rkv'], P['ev_q_norm'][j][None], P['ev_kv_norm'][j][None],
                                 dcqn, dckvn, dqs, _shift_prev(dko, dkp, B, S), _shift_prev(dvo, dvp, B, S), dkr_pre,
                                 name=f"{tag}_in_bwd")
    g['ev_w_in'] = _even_in_grad_unpad(_mm_tn(xb, dh, name=f"{tag}_dwin"))
    xchg.push_grads({(n, j): val for n, val in g.items()})
    dx = _scattering(xchg, _mm, dh, w_in, trans_b=True, extras=(dz1,), epilogue=lambda acc, r: (acc + DN_ALPHA * r,),
                     name=f"{tag}_dx")
    small = dict(ev_q_norm=dgq[0], ev_kv_norm=dgkv[0], dbias=dbias, ev_sinks=jnp.sum(dsink, axis=(1, 2)))
    return dx, small


def _odd_fwd(xb, W, P, i, B, S, xchg, tag):
    j = i // 2
    w = W['od_w_in']
    w_qkv = w[:, :ODD_QKV]
    w_f = jnp.pad(w[:, ODD_QKV:], ((0, 0), (0, LANES - FOX_HEADS)))
    bf = jnp.pad(P['od_b_f'][j], (0, LANES - FOX_HEADS))[None]
    qkv = _mm(xb, w_qkv, out_dtypes=(BF16,), name=f"{tag}_qkv")
    f = _mm(xb, w_f, name=f"{tag}_f").reshape(B, S, LANES)
    csh, chs = _fox_decay_fwd(f, bf, name=f"{tag}_decay")
    crow = chs[:, :FOX_HEADS].reshape(B, FOX_HEADS, S // ATT_TILE, 1, ATT_TILE)
    n_blk = FOX_HEADS * HEAD_DIM // LANES
    o, lse, got = _flash_fwd(qkv, qkv, qkv, q_blk0=0, k_blk0=n_blk, v_blk0=2 * n_blk, W=LANES,
                             n_pairs=FOX_HEADS // 2, B=B, S=S, scale=HEAD_DIM ** -0.5, csh=csh, crow=crow,
                             comm=xchg.carry(f"{tag}_fox"), name=f"{tag}_fox")
    xchg.carried(f"{tag}_fox", got)
    res = dict(f=f, bf=bf, csh=csh, crow=crow, qkv=qkv, o=o, lse=lse, w_qkv=w_qkv, w_f=w_f)
    return (o, W['od_w_out']), res


def _odd_bwd(dmb, dz1, xb, W, P, j, B, S, res, xchg, tag):
    g = {}
    w_out = W['od_w_out']
    g['od_w_out'] = _mm_tn(res['o'], dmb, name=f"{tag}_dwout")
    do = _mm(dmb, w_out, trans_b=True, out_dtypes=(BF16,), name=f"{tag}_do")
    qkv = res['qkv']
    n_blk = FOX_HEADS * HEAD_DIM // LANES
    dq, dk, dv, dck, dcq, got = _flash_bwd(qkv, qkv, qkv, res['o'], do, res['lse'], q_blk0=0, k_blk0=n_blk,
                                           v_blk0=2 * n_blk, do_blk0=0, W=LANES, n_pairs=FOX_HEADS // 2, B=B, S=S,
                                           scale=HEAD_DIM ** -0.5, qk_dtype=BF16, csh=res['csh'], crow=res['crow'],
                                           comm=xchg.bwd_items(), name=f"{tag}_fox_bwd")
    xchg.bwd_done(got)
    dc = dck.reshape(B, FOX_HEADS, S) + dcq.reshape(B, FOX_HEADS, S)
    dc_hs = jnp.pad(dc, ((0, 0), (0, LANES - FOX_HEADS), (0, 0)))
    df, dbf = _fox_decay_bwd(dc_hs, res['f'], res['bf'], name=f"{tag}_decay_bwd")
    df = df.reshape(B * S, LANES)
    dqkv = jnp.concatenate([dq, dk, dv], axis=-1)
    dw_qkv = _mm_tn(xb, dqkv, name=f"{tag}_dwqkv")
    dw_f = _mm_tn(xb, df, name=f"{tag}_dwf")
    g['od_w_in'] = jnp.concatenate([dw_qkv, dw_f[:, :FOX_HEADS]], axis=1)
    dxf = _mm(df, res['w_f'], trans_b=True, extras=(dz1,), epilogue=lambda acc, r: (acc + DN_ALPHA * r,),
              name=f"{tag}_dxf")
    xchg.push_grads({(n, j): val for n, val in g.items()})
    dx = _scattering(xchg, _mm, dqkv, res['w_qkv'], trans_b=True, extras=(dxf,), epilogue=lambda acc, r: (acc + r,),
                     name=f"{tag}_dx")
    small = dict(od_b_f=dbf[0, :FOX_HEADS])
    return dx, small


def _carrying(xchg, name, call, *args, **kwargs):
    comm = xchg.carry(name)
    out = call(*args, comm=comm, name=name, **kwargs)
    if comm:
        out, got = out
        xchg.carried(name, got)
    return out


def _scattering(xchg, call, *args, **kwargs):
    comm = xchg.bwd_items()
    out = call(*args, comm=comm, **kwargs)
    if comm:
        out, got = out
        xchg.bwd_done(got)
    return out


def _local_step(x, p, target, P, xchg):
    B, S, D = x.shape
    T = B * S
    q_tabs, k_tabs = _rope_tables(S)
    bucket = _swa_bucket_table()
    in_bucket = (bucket[..., None] == jnp.arange(REL_BUCKETS)).astype(F32)
    bias = jnp.einsum('acb,bh->hac', in_bucket, P['rel_bias'], precision=lax.Precision.HIGHEST)

    xc = x.reshape(T, D)
    xcb = xc.astype(BF16)
    saved = []
    for i in range(DEPTH):
        j = i // 2
        tag = f"l{i}"
        W = xchg.layer_weights(i)
        lay = dict(xb=xcb, W=W)
        if i % 2 == 0:
            sinkcol = jnp.broadcast_to(P['ev_sinks'][j][:, None, None], (SWA_HEADS, BLOCK_Q, 1)).astype(F32)
            lay['tabs'] = (q_tabs, k_tabs, bias, sinkcol)
            (o, w_out), lay['mix'] = _even_fwd(xcb, W, P, i, B, S, lay['tabs'], xchg, tag)
        else:
            (o, w_out), lay['mix'] = _odd_fwd(xcb, W, P, i, B, S, xchg, tag)
        x1, x1b, lay['xh1'], lay['r1'] = _carrying(xchg, f"{tag}_out_ln1", _mm_ln, o, w_out, xc,
                                                   P['ln1_g'][i][None], P['ln1_b'][i][None])
        lay['x1b'] = x1b
        lay['u'], lay['a'] = _carrying(xchg, f"{tag}_up", _mm, x1b, W['w_up'], out_dtypes=(F32, BF16),
                                       epilogue=lambda acc: (acc, jnp.square(jnp.maximum(acc, 0.0))))
        x2, x2b, lay['xh2'], lay['r2'] = _carrying(xchg, f"{tag}_down_ln2", _mm_ln, lay['a'], W['w_down'], x1,
                                                   P['ln2_g'][i][None], P['ln2_b'][i][None])
        lay['x2b'] = x2b
        lay['p'] = p[i].reshape(T, D_PLE)
        lay['e'] = _mm(lay['p'], W['ple_w_proj'], name=f"{tag}_ple_proj")

        def gate(acc, bg, e, x2v):
            gv = 1.0 / (1.0 + jnp.exp(-(acc + bg)))
            y = x2v + gv * e
            return y, y, gv

        xc, xcb, lay['g'] = _carrying(xchg, f"{tag}_ple_gate", _mm, x2b, W['ple_w_gate'],
                                      extras=(P['ple_b_gate'][i][None], lay['e'], x2), epilogue=gate,
                                      out_dtypes=(F32, BF16, F32))
        saved.append(lay)

    dy, sq = _loss_grad(xc, target.reshape(T, D), name="loss")

    Gs = {n: [None] * DEPTH for n in ('ln1_g', 'ln1_b', 'ln2_g', 'ln2_b', 'ple_b_gate')}
    Gs.update({n: [None] * (DEPTH // 2) for n in ('ev_q_norm', 'ev_kv_norm', 'ev_sinks', 'od_b_f')})
    dbias_total = None
    for i in reversed(range(DEPTH)):
        j = i // 2
        tag = f"l{i}b"
        lay = saved[i]
        W = lay['W']
        de, dzg, dbg = _ple_bwd_elem(dy, lay['g'], lay['e'], name=f"{tag}_ple_elem")
        Gs['ple_b_gate'][i] = dbg[0]
        g_mlp = {('ple_w_proj', i): _mm_tn(lay['p'], de, name=f"{tag}_dwproj"),
                 ('ple_w_gate', i): _mm_tn(lay['x2b'], dzg, name=f"{tag}_dwgate")}
        dz2, dz2b, dg2, db2 = _mm_ln_bwd(dzg, W['ple_w_gate'], dy, 1.0, lay['xh2'], lay['r2'], P['ln2_g'][i][None],
                                         name=f"{tag}_dx2_ln2")
        Gs['ln2_g'][i], Gs['ln2_b'][i] = dg2[0], db2[0]
        g_mlp[('w_down', i)] = _mm_tn(lay['a'], dz2b, name=f"{tag}_dwdown")
        du = _mm(dz2b, W['w_down'], trans_b=True, extras=(lay['u'],), out_dtypes=(BF16,),
                 epilogue=lambda acc, u: (acc * (2.0 * jnp.maximum(u, 0.0)),), name=f"{tag}_du")
        g_mlp[('w_up', i)] = _mm_tn(lay['x1b'], du, name=f"{tag}_dwup")
        xchg.push_grads(g_mlp)
        dz1, dz1b, dg1, db1 = _mm_ln_bwd(du, W['w_up'], dz2, DN_ALPHA, lay['xh1'], lay['r1'], P['ln1_g'][i][None],
                                         name=f"{tag}_dx1_ln1")
        Gs['ln1_g'][i], Gs['ln1_b'][i] = dg1[0], db1[0]
        if i % 2 == 0:
            dy, small = _even_bwd(dz1b, dz1, lay['xb'], W, P, j, B, S, lay['tabs'], lay['mix'], xchg, tag)
            dbias_total = small['dbias'] if dbias_total is None else dbias_total + small['dbias']
            for n in ('ev_q_norm', 'ev_kv_norm', 'ev_sinks'):
                Gs[n][j] = small[n]
        else:
            dy, small = _odd_bwd(dz1b, dz1, lay['xb'], W, P, j, B, S, lay['mix'], xchg, tag)
            Gs['od_b_f'][j] = small['od_b_f']

    grads_small = {n: jnp.stack(v) for n, v in Gs.items()}
    drel = _bias_bucket_sum(dbias_total, bucket, name="rel_bias_grad")
    grads_small['rel_bias'] = drel[:, :REL_BUCKETS].T
    return sq, dy.reshape(B, S, D), grads_small


def kernel(x, p, rel_bias, ev_w_in, ev_q_norm, ev_w_uq, ev_kv_norm, ev_w_ukv, ev_sinks, ev_w_out, od_w_in, od_b_f, od_w_out, ln1_g, ln1_b, w_up, w_down, ln2_g, ln2_b, ple_w_proj, ple_w_gate, ple_b_gate, loss_target, m_rel_bias, m_ev_w_in, m_ev_q_norm, m_ev_w_uq, m_ev_kv_norm, m_ev_w_ukv, m_ev_sinks, m_ev_w_out, m_od_w_in, m_od_b_f, m_od_w_out, m_ln1_g, m_ln1_b, m_w_up, m_w_down, m_ln2_g, m_ln2_b, m_ple_w_proj, m_ple_w_gate, m_ple_b_gate, v_rel_bias, v_ev_w_in, v_ev_q_norm, v_ev_w_uq, v_ev_kv_norm, v_ev_w_ukv, v_ev_sinks, v_ev_w_out, v_od_w_in, v_od_b_f, v_od_w_out, v_ln1_g, v_ln1_b, v_w_up, v_w_down, v_ln2_g, v_ln2_b, v_ple_w_proj, v_ple_w_gate, v_ple_b_gate):
    given = dict(locals())
    w = {n: given[n] for n in WEIGHTS}
    mom = {n: given["m_" + n] for n in WEIGHTS}
    var = {n: given["v_" + n] for n in WEIGHTS}
    small_shapes = {n: w[n].shape for n in SMALL}

    shards = {(n, idx): w[n][idx].astype(BF16) for n in BIG for idx in range(w[n].shape[0])}
    xchg = _MeshExchange(shards)
    P = {n: w[n] for n in SMALL}

    sq, grad_x, grads_small = _local_step(x, p, loss_target, P, xchg)
    loss = lax.psum(0.5 * jnp.sum(sq) / D_MODEL, ("x", "y", "c"))

    received = xchg.finish()
    g_small_packed = _all_reduce_small(_pack_small(grads_small), name="reduce_small_grads")
    g_small = _unpack_small(g_small_packed, small_shapes)

    grad, delta, new_m, new_v = {}, {}, {}, {}
    for n in BIG:
        per_layer = [_adamw_slots(w[n][idx], received[(n, idx)], mom[n][idx], var[n][idx], name=f"adamw_{n}{idx}")
                     for idx in range(w[n].shape[0])]
        grad[n], delta[n], new_m[n], new_v[n] = (jnp.stack(t) for t in zip(*per_layer))
    d, nm, nv = _adamw(_pack_small(w), g_small_packed, _pack_small(mom), _pack_small(var), name="adamw_small")
    d, nm, nv = (_unpack_small(t, small_shapes) for t in (d, nm, nv))
    for n in SMALL:
        grad[n], delta[n], new_m[n], new_v[n] = g_small[n], d[n], nm[n], nv[n]

    return (loss, grad_x, *[grad[n] for n in WEIGHTS], *[delta[n] for n in WEIGHTS],
            *[new_m[n] for n in WEIGHTS], *[new_v[n] for n in WEIGHTS])
```

```python
import math

import jax
import jax.numpy as jnp
from jax import lax
from jax.experimental import pallas as pl
from jax.experimental.pallas import tpu as pltpu

F32, BF16 = jnp.float32, jnp.bfloat16

D_MODEL = 1024
DEPTH = 4
HEAD_DIM = 64
MLA_HEADS, MLA_NOPE, MLA_ROPE, MLA_V = 8, 64, 32, 64
MLA_Q_LORA, MLA_KV_LORA = 384, 256
MLA_QK = MLA_NOPE + MLA_ROPE
ROPE_THETA = 10000.0
SWA_HEADS, SWA_KV_HEADS, SWA_WINDOW = 8, 2, 128
SWA_GROUP = SWA_HEADS // SWA_KV_HEADS
REL_BUCKETS, REL_MAX_DIST = 32, 128
FOX_HEADS = 16
D_FF = 4 * D_MODEL
D_PLE = 256
BLOCK_Q = 128
DN_ALPHA = (2 * DEPTH) ** 0.25
NORM_EPS = 1e-5
NEG_INF = -1e30
EVEN_IN = 1440
ODD_QKV = 3 * FOX_HEADS * HEAD_DIM
LANES = 128

EV_QS = (0, 512)
EV_CQ = (512, 896)
EV_CKV = (896, 1152)
EV_KS = (1152, 1280)
EV_VS = (1280, 1408)
EV_KR = (1408, 1536)
EVEN_IN_PAD = 1536
KR_LANE0 = MLA_NOPE

ADAM_LR, ADAM_B1, ADAM_B2, ADAM_EPS, ADAM_WD, ADAM_STEP = 0.001, 0.9, 0.999, 1e-08, 0.01, 10

N_DEV = 8
VMEM_LIMIT_BYTES = 48 * 1024 * 1024
ATT_TILE = 512
ATT_TILE_BWD = 512
PAIRS_PER_STEP_FWD = 4
PAIRS_PER_STEP_BWD = 2

NN = (((1,), (0,)), ((), ()))
NT = (((1,), (1,)), ((), ()))
TN = (((0,), (0,)), ((), ()))

BIG = ['ev_w_in', 'ev_w_uq', 'ev_w_ukv', 'ev_w_out', 'od_w_in', 'od_w_out', 'w_up', 'w_down',
       'ple_w_proj', 'ple_w_gate']
BIG_AXIS = {'ev_w_in': 2, 'ev_w_uq': 2, 'ev_w_ukv': 2, 'ev_w_out': 1, 'od_w_in': 2, 'od_w_out': 1,
            'w_up': 2, 'w_down': 1, 'ple_w_proj': 2, 'ple_w_gate': 1}
SMALL = ['rel_bias', 'ev_q_norm', 'ev_kv_norm', 'ev_sinks', 'od_b_f', 'ln1_g', 'ln1_b', 'ln2_g', 'ln2_b',
         'ple_b_gate']
WEIGHTS = ['rel_bias', 'ev_w_in', 'ev_q_norm', 'ev_w_uq', 'ev_kv_norm', 'ev_w_ukv', 'ev_sinks', 'ev_w_out',
           'od_w_in', 'od_b_f', 'od_w_out', 'ln1_g', 'ln1_b', 'w_up', 'w_down', 'ln2_g', 'ln2_b',
           'ple_w_proj', 'ple_w_gate', 'ple_b_gate']


def _cparams(*sem):
    return pltpu.CompilerParams(dimension_semantics=sem, vmem_limit_bytes=VMEM_LIMIT_BYTES)


def _pick(n, cands):
    for c in cands:
        if n % c == 0:
            return c
    return n


MM_STEP_BYTES = 10 * 1024 * 1024
MM_OUT_BYTES = 8 * 1024 * 1024
MM_CHUNK = 512


def _mm(a, b, *, trans_b=False, extras=(), epilogue=None, row_epilogue=None, out_dtypes=(F32,), out_widths=None,
        n_sums=0, comm=(), name):
    M, K = a.shape
    N = b.shape[0] if trans_b else b.shape[1]
    n_ex, n_out = len(extras), len(out_dtypes)
    n_rows_out = n_out - n_sums
    out_widths = (N,) * n_out if out_widths is None else out_widths
    row_bytes = K * a.dtype.itemsize + (sum(w * jnp.dtype(d).itemsize
                                            for w, d in zip(out_widths[:n_rows_out], out_dtypes))
                                        + sum(e.shape[1] * e.dtype.itemsize for e in extras if e.shape[0] == M)
                                        + (4 * N if row_epilogue is not None else 0))
    tm = next((c for c in (1024, 512, 256) if M % c == 0 and c * row_bytes <= MM_STEP_BYTES), 128)
    nc = _pick(N, (MM_CHUNK, 384, 256, 128))
    n_c, kinds = len(comm), [k for k, _ in comm]
    n_scr = 1 if row_epilogue is not None else 0

    def body(*refs):
        c_in = refs[2 + n_ex:2 + n_ex + n_c]
        c_out = refs[2 + n_ex + n_c + n_out:2 + n_ex + 2 * n_c + n_out]
        sems = refs[2 + n_ex + 2 * n_c + n_out + n_scr:]
        refs = refs[:2 + n_ex] + refs[2 + n_ex + n_c:2 + n_ex + n_c + n_out] \
            + refs[2 + n_ex + 2 * n_c + n_out:2 + n_ex + 2 * n_c + n_out + n_scr]
        if n_c:
            place = _mesh_place()
            step = pl.program_id(0)

            @pl.when(step == 0)
            def _():
                _comm_start(kinds, c_in, c_out, sems, place)

        a_ref, b_ref = refs[:2]
        ex = refs[2:2 + n_ex]
        outs = refs[2 + n_ex:2 + n_ex + n_out]
        av = a_ref[...].astype(BF16)
        for n0 in range(0, N, nc):
            cols = slice(n0, n0 + nc)
            bv = (b_ref[cols, :] if trans_b else b_ref[:, cols]).astype(BF16)
            acc = lax.dot_general(av, bv, NT if trans_b else NN, preferred_element_type=F32)
            if row_epilogue is not None:
                refs[-1][:, cols] = acc
                continue
            res = epilogue(acc, *[e[:, cols] for e in ex]) if epilogue is not None else (acc,)
            for o, r in zip(outs, res):
                o[:, cols] = r.astype(o.dtype)
        if row_epilogue is not None:
            res = row_epilogue(refs[-1][...], *[e[...] for e in ex])
            for o, r in zip(outs[:n_rows_out], res):
                o[...] = r.astype(o.dtype)
            if n_sums:
                @pl.when(pl.program_id(0) == 0)
                def _():
                    for o in outs[n_rows_out:]:
                        o[...] = jnp.zeros_like(o)

                for o, r in zip(outs[n_rows_out:], res[n_rows_out:]):
                    o[...] += r
        if n_c:
            @pl.when(step == M // tm - 1)
            def _():
                _comm_wait(kinds, c_in, c_out, sems, place)

    in_specs = [pl.BlockSpec((tm, K), lambda i: (i, 0)), pl.BlockSpec(b.shape, lambda i: (0, 0))]
    for e in extras:
        if e.shape[0] == M:
            in_specs.append(pl.BlockSpec((tm, e.shape[1]), lambda i: (i, 0)))
        elif e.shape == (1, N):
            in_specs.append(pl.BlockSpec((1, N), lambda i: (0, 0)))
        else:
            raise ValueError(f"extra operand of shape {e.shape} for a ({M}, {N}) result")
    res = pl.pallas_call(
        body, name=name, grid=(M // tm,), in_specs=in_specs + [HBM_SPEC] * n_c,
        out_specs=[pl.BlockSpec((tm, w), lambda i: (i, 0)) for w in out_widths[:n_rows_out]]
        + [pl.BlockSpec((1, w), lambda i: (0, 0)) for w in out_widths[n_rows_out:]] + [HBM_SPEC] * n_c,
        out_shape=[jax.ShapeDtypeStruct((M, w), d) for w, d in zip(out_widths[:n_rows_out], out_dtypes)]
        + [jax.ShapeDtypeStruct((1, w), d) for w, d in zip(out_widths[n_rows_out:], out_dtypes[n_rows_out:])]
        + _comm_out_shapes(comm),
        scratch_shapes=([pltpu.VMEM((tm, N), F32)] if row_epilogue is not None else [])
        + (_comm_scratch(comm) if n_c else []),
        compiler_params=_cparams("arbitrary" if n_sums or n_c else "parallel"),
    )(a, b, *extras, *[c for _, c in comm])
    main = res[0] if n_out == 1 else tuple(res[:n_out])
    return (main, list(res[n_out:])) if n_c else main


def _mm_tn(a, b, *, name):
    T, K = a.shape
    N = b.shape[1]
    bk, bn = K, N
    while bk * bn * 4 > MM_OUT_BYTES:
        if bn >= bk and bn % (2 * LANES) == 0:
            bn //= 2
        else:
            bk //= 2
    tt = _pick(T, (1024, 512, 256))
    ck, cn = _pick(bk, (MM_CHUNK, 384, 256, 128)), _pick(bn, (MM_CHUNK, 384, 256, 128))

    def body(a_ref, b_ref, o_ref):
        t = pl.program_id(2)

        @pl.when(t == 0)
        def _():
            o_ref[...] = jnp.zeros_like(o_ref)

        for r0 in range(0, bk, ck):
            av = a_ref[:, r0:r0 + ck].astype(BF16)
            for c0 in range(0, bn, cn):
                o_ref[r0:r0 + ck, c0:c0 + cn] += lax.dot_general(
                    av, b_ref[:, c0:c0 + cn].astype(BF16), TN, preferred_element_type=F32)

    return pl.pallas_call(
        body, name=name, grid=(K // bk, N // bn, T // tt),
        in_specs=[pl.BlockSpec((tt, bk), lambda i, j, t: (t, i)), pl.BlockSpec((tt, bn), lambda i, j, t: (t, j))],
        out_specs=pl.BlockSpec((bk, bn), lambda i, j, t: (i, j)),
        out_shape=jax.ShapeDtypeStruct((K, N), F32),
        compiler_params=_cparams("parallel", "parallel", "arbitrary"),
    )(a, b)


ROW_TILE = 256


def _row_spec(cols, col_block=0):
    return pl.BlockSpec((ROW_TILE, cols), lambda i: (i, col_block))


def _tab_spec(cols, period):
    return pl.BlockSpec((ROW_TILE, cols), lambda i: (i % period, 0))


def _full_spec(shape):
    return pl.BlockSpec(shape, lambda i: (0,) * len(shape))


def _mm_ln(a, w, x, g, b, *, comm=(), name):
    def ln_rows(m, xv, gv, bv):
        z = DN_ALPHA * xv + m
        mu = jnp.mean(z, -1, keepdims=True)
        zc = z - mu
        r = lax.rsqrt(jnp.mean(zc * zc, -1, keepdims=True) + NORM_EPS)
        xh = zc * r
        y = xh * gv + bv
        return y, y, xh, jnp.broadcast_to(r, (r.shape[0], LANES))

    D = w.shape[1]
    return _mm(a, w, extras=(x, g, b), row_epilogue=ln_rows, out_dtypes=(F32, BF16, F32, F32),
               out_widths=(D, D, D, LANES), comm=comm, name=name)


def _mm_ln_bwd(a, w, resid, resid_scale, xh, r, g, *, name):
    def ln_bwd_rows(acc, rv, xhv, rstd, gv):
        dyv = acc + resid_scale * rv
        dyg = dyv * gv
        c1 = jnp.mean(dyg, -1, keepdims=True)
        c2 = jnp.mean(dyg * xhv, -1, keepdims=True)
        dz = _widen(rstd, dyv.shape[-1]) * (dyg - c1 - xhv * c2)
        return dz, dz, jnp.sum(dyv * xhv, 0, keepdims=True), jnp.sum(dyv, 0, keepdims=True)

    D = w.shape[0]
    return _mm(a, w, trans_b=True, extras=(resid, xh, r, g), row_epilogue=ln_bwd_rows,
               out_dtypes=(F32, BF16, F32, F32), out_widths=(D, D, D, D), n_sums=2, name=name)


def _loss_grad(y, target, *, name):
    T, D = y.shape

    def body(y_ref, t_ref, dy_ref, sq_ref):
        err = y_ref[...] - t_ref[...]
        dy_ref[...] = err / D

        @pl.when(pl.program_id(0) == 0)
        def _():
            sq_ref[...] = jnp.zeros_like(sq_ref)

        sq_ref[...] += jnp.sum(err * err, 0, keepdims=True)

    return pl.pallas_call(
        body, name=name, grid=(T // ROW_TILE,),
        in_specs=[_row_spec(D), _row_spec(D)],
        out_specs=[_row_spec(D), _full_spec((1, D))],
        out_shape=[jax.ShapeDtypeStruct((T, D), F32), jax.ShapeDtypeStruct((1, D), F32)],
        compiler_params=_cparams("arbitrary"),
    )(y, target)


def _ple_bwd_elem(dx3, g, e, *, name):
    T, D = dx3.shape

    def body(dx_ref, g_ref, e_ref, de_ref, dz_ref, db_ref):
        dx, gv = dx_ref[...], g_ref[...]
        de_ref[...] = (dx * gv).astype(BF16)
        dz = dx * e_ref[...] * gv * (1.0 - gv)
        dz_ref[...] = dz.astype(BF16)

        @pl.when(pl.program_id(0) == 0)
        def _():
            db_ref[...] = jnp.zeros_like(db_ref)

        db_ref[...] += jnp.sum(dz, 0, keepdims=True)

    return pl.pallas_call(
        body, name=name, grid=(T // ROW_TILE,),
        in_specs=[_row_spec(D), _row_spec(D), _row_spec(D)],
        out_specs=[_row_spec(D), _row_spec(D), _full_spec((1, D))],
        out_shape=[jax.ShapeDtypeStruct((T, D), BF16), jax.ShapeDtypeStruct((T, D), BF16),
                   jax.ShapeDtypeStruct((1, D), F32)],
        compiler_params=_cparams("arbitrary"),
    )(dx3, g, e)


def _rotate(xv, a, bm, bp, sign):
    half = MLA_ROPE // 2
    width = xv.shape[-1]
    a, bm, bp = (_widen(t, width) for t in (a, bm, bp))
    return xv * a + sign * (pltpu.roll(xv, width - half, 1) * bm + pltpu.roll(xv, half, 1) * bp)


def _rope(x, tabs, seq, *, sign, name):
    T, width = x.shape

    def body(x_ref, a_ref, bm_ref, bp_ref, o_ref):
        o_ref[...] = _rotate(x_ref[...], a_ref[...], bm_ref[...], bp_ref[...], sign).astype(BF16)

    return pl.pallas_call(
        body, name=name, grid=(T // ROW_TILE,),
        in_specs=[_row_spec(width)] + [_tab_spec(LANES, seq // ROW_TILE)] * 3,
        out_specs=_row_spec(width),
        out_shape=jax.ShapeDtypeStruct((T, width), BF16),
        compiler_params=_cparams("parallel"),
    )(x, *tabs)


def _mla_keys(knp, h, k_tabs, seq, *, name):
    T = knp.shape[0]

    def body(k_ref, h_ref, a_ref, bm_ref, bp_ref, o_ref):
        kr = _rotate(h_ref[...], a_ref[...], bm_ref[...], bp_ref[...], 1.0)
        for hd in range(MLA_HEADS):
            cols = slice(hd * LANES, (hd + 1) * LANES)
            o_ref[:, cols] = (k_ref[:, cols].astype(F32) + kr).astype(BF16)

    return pl.pallas_call(
        body, name=name, grid=(T // ROW_TILE,),
        in_specs=[_row_spec(MLA_HEADS * LANES), _row_spec(LANES, EV_KR[0] // LANES)]
        + [_tab_spec(LANES, seq // ROW_TILE)] * 3,
        out_specs=_row_spec(MLA_HEADS * LANES),
        out_shape=jax.ShapeDtypeStruct((T, MLA_HEADS * LANES), BF16),
        compiler_params=_cparams("parallel"),
    )(knp, h, *k_tabs)


def _mla_rope_key_grad(dk, k_tabs, seq, *, name):
    T = dk.shape[0]

    def body(dk_ref, a_ref, bm_ref, bp_ref, o_ref):
        tot = dk_ref[:, 0:LANES]
        for hd in range(1, MLA_HEADS):
            tot = tot + dk_ref[:, hd * LANES:(hd + 1) * LANES]
        o_ref[...] = _rotate(tot, a_ref[...], bm_ref[...], bp_ref[...], -1.0).astype(BF16)

    return pl.pallas_call(
        body, name=name, grid=(T // ROW_TILE,),
        in_specs=[_row_spec(MLA_HEADS * LANES)] + [_tab_spec(LANES, seq // ROW_TILE)] * 3,
        out_specs=_row_spec(LANES),
        out_shape=jax.ShapeDtypeStruct((T, LANES), BF16),
        compiler_params=_cparams("parallel"),
    )(dk, *k_tabs)


def _even_norms(h, gq, gkv, *, name):
    T = h.shape[0]

    def body(h_ref, gq_ref, gkv_ref, cq_ref, ckv_ref, rq_ref, rkv_ref):
        cq = h_ref[:, EV_CQ[0]:EV_CQ[1]]
        rq = lax.rsqrt(jnp.mean(cq * cq, -1, keepdims=True) + NORM_EPS)
        cq_ref[...] = (cq * rq * gq_ref[...]).astype(BF16)
        rq_ref[...] = jnp.broadcast_to(rq, rq_ref.shape)
        ckv = h_ref[:, EV_CKV[0]:EV_CKV[1]]
        rkv = lax.rsqrt(jnp.mean(ckv * ckv, -1, keepdims=True) + NORM_EPS)
        ckv_ref[...] = (ckv * rkv * gkv_ref[...]).astype(BF16)
        rkv_ref[...] = jnp.broadcast_to(rkv, rkv_ref.shape)

    return pl.pallas_call(
        body, name=name, grid=(T // ROW_TILE,),
        in_specs=[_row_spec(EVEN_IN_PAD), _full_spec((1, MLA_Q_LORA)), _full_spec((1, MLA_KV_LORA))],
        out_specs=[_row_spec(MLA_Q_LORA), _row_spec(MLA_KV_LORA), _row_spec(LANES), _row_spec(LANES)],
        out_shape=[jax.ShapeDtypeStruct((T, MLA_Q_LORA), BF16), jax.ShapeDtypeStruct((T, MLA_KV_LORA), BF16),
                   jax.ShapeDtypeStruct((T, LANES), F32), jax.ShapeDtypeStruct((T, LANES), F32)],
        compiler_params=_cparams("parallel"),
    )(h, gq, gkv)


def _even_in_bwd(h, rq, rkv, gq, gkv, dcqn, dckvn, dqs, dks, dvs, dkr, *, name):
    T = h.shape[0]

    def rms_bwd(c, r, g, dy):
        r = _widen(r, c.shape[-1])
        xr = c * r
        dyg = dy * g
        return r * (dyg - xr * jnp.mean(dyg * xr, -1, keepdims=True)), jnp.sum(dy * xr, 0, keepdims=True)

    def body(h_ref, rq_ref, rkv_ref, gq_ref, gkv_ref, dcq_ref, dckv_ref, dqs_ref, dks_ref, dvs_ref, dkr_ref,
             dh_ref, dgq_ref, dgkv_ref):
        @pl.when(pl.program_id(0) == 0)
        def _():
            dgq_ref[...] = jnp.zeros_like(dgq_ref)
            dgkv_ref[...] = jnp.zeros_like(dgkv_ref)

        dcq, dgq = rms_bwd(h_ref[:, EV_CQ[0]:EV_CQ[1]], rq_ref[...], gq_ref[...], dcq_ref[...])
        dckv, dgkv = rms_bwd(h_ref[:, EV_CKV[0]:EV_CKV[1]], rkv_ref[...], gkv_ref[...], dckv_ref[...])
        dgq_ref[...] += dgq
        dgkv_ref[...] += dgkv
        dh_ref[:, EV_QS[0]:EV_QS[1]] = dqs_ref[...]
        dh_ref[:, EV_CQ[0]:EV_CQ[1]] = dcq.astype(BF16)
        dh_ref[:, EV_CKV[0]:EV_CKV[1]] = dckv.astype(BF16)
        dh_ref[:, EV_KS[0]:EV_KS[1]] = dks_ref[...]
        dh_ref[:, EV_VS[0]:EV_VS[1]] = dvs_ref[...]
        dh_ref[:, EV_KR[0]:EV_KR[1]] = dkr_ref[...]

    return pl.pallas_call(
        body, name=name, grid=(T // ROW_TILE,),
        in_specs=[_row_spec(EVEN_IN_PAD), _row_spec(LANES), _row_spec(LANES), _full_spec((1, MLA_Q_LORA)),
                  _full_spec((1, MLA_KV_LORA)), _row_spec(MLA_Q_LORA), _row_spec(MLA_KV_LORA),
                  _row_spec(SWA_HEADS * HEAD_DIM), _row_spec(LANES), _row_spec(LANES), _row_spec(LANES)],
        out_specs=[_row_spec(EVEN_IN_PAD), _full_spec((1, MLA_Q_LORA)), _full_spec((1, MLA_KV_LORA))],
        out_shape=[jax.ShapeDtypeStruct((T, EVEN_IN_PAD), BF16), jax.ShapeDtypeStruct((1, MLA_Q_LORA), F32),
                   jax.ShapeDtypeStruct((1, MLA_KV_LORA), F32)],
        compiler_params=_cparams("arbitrary"),
    )(h, rq, rkv, gq, gkv, dcqn, dckvn, dqs, dks, dvs, dkr)


def _fox_decay_fwd(f3, bf, *, name):
    B, S, _ = f3.shape

    def body(f_ref, b_ref, csh_ref, chs_ref):
        x = f_ref[...] + b_ref[...]
        c = jnp.minimum(x, 0.0) - jnp.log1p(jnp.exp(-jnp.abs(x)))
        row = lax.broadcasted_iota(jnp.int32, (S, LANES), 0)
        k = 1
        while k < S:
            c = c + jnp.where(row >= k, pltpu.roll(c, k, 0), 0.0)
            k *= 2
        csh_ref[...] = c
        chs_ref[...] = c.T

    return pl.pallas_call(
        body, name=name, grid=(B,),
        in_specs=[pl.BlockSpec((None, S, LANES), lambda b: (b, 0, 0)), pl.BlockSpec((1, LANES), lambda b: (0, 0))],
        out_specs=[pl.BlockSpec((None, S, LANES), lambda b: (b, 0, 0)),
                   pl.BlockSpec((None, LANES, S), lambda b: (b, 0, 0))],
        out_shape=[jax.ShapeDtypeStruct((B, S, LANES), F32), jax.ShapeDtypeStruct((B, LANES, S), F32)],
        compiler_params=_cparams("parallel"),
    )(f3, bf)


def _fox_decay_bwd(dc_hs, f3, bf, *, name):
    B, S, _ = f3.shape

    def body(dc_ref, f_ref, b_ref, df_ref, db_ref):
        g = dc_ref[...].T
        row = lax.broadcasted_iota(jnp.int32, (S, LANES), 0)
        k = 1
        while k < S:
            g = g + jnp.where(row < S - k, pltpu.roll(g, S - k, 0), 0.0)
            k *= 2
        x = f_ref[...] + b_ref[...]
        df = g * (1.0 / (1.0 + jnp.exp(x)))
        df_ref[...] = df.astype(BF16)

        @pl.when(pl.program_id(0) == 0)
        def _():
            db_ref[...] = jnp.zeros_like(db_ref)

        db_ref[...] += jnp.sum(df, 0, keepdims=True)

    return pl.pallas_call(
        body, name=name, grid=(B,),
        in_specs=[pl.BlockSpec((None, LANES, S), lambda b: (b, 0, 0)),
                  pl.BlockSpec((None, S, LANES), lambda b: (b, 0, 0)), pl.BlockSpec((1, LANES), lambda b: (0, 0))],
        out_specs=[pl.BlockSpec((None, S, LANES), lambda b: (b, 0, 0)), pl.BlockSpec((1, LANES), lambda b: (0, 0))],
        out_shape=[jax.ShapeDtypeStruct((B, S, LANES), BF16), jax.ShapeDtypeStruct((1, LANES), F32)],
        compiler_params=_cparams("arbitrary"),
    )(dc_hs, f3, bf)


def _head_column(block, h):
    lane = lax.broadcasted_iota(jnp.int32, block.shape, 1)
    return jnp.sum(jnp.where(lane == h, block, 0.0), axis=-1, keepdims=True)


def _causal_mask(s):
    r = lax.broadcasted_iota(jnp.int32, s.shape, 0)
    c = lax.broadcasted_iota(jnp.int32, s.shape, 1)
    return jnp.where(c <= r, s, NEG_INF)


def _low_half(shape):
    return (lax.broadcasted_iota(jnp.int32, shape, 1) % LANES) < HEAD_DIM


def _widen(x, cols):
    return jnp.concatenate([x] * (cols // LANES), axis=1)


def _both_halves(x, lo):
    r = pltpu.roll(x, HEAD_DIM, 1)
    return jnp.where(lo, x, r), jnp.where(lo, r, x)


MESH_ID = pl.DeviceIdType.MESH
HBM_SPEC = pl.BlockSpec(memory_space=pltpu.HBM)
VMEM_SPEC = pl.BlockSpec(memory_space=pltpu.VMEM)


def _mesh_place():
    x, y, c = lax.axis_index("x"), lax.axis_index("y"), lax.axis_index("c")
    return x, y, c, 4 * x + 2 * y + c


def _peers(x, y, c):
    out = []
    for mask in range(1, N_DEV):
        dx, dy, dc = (mask >> 2) & 1, (mask >> 1) & 1, mask & 1
        px, py, pc = (1 - x if dx else x), (1 - y if dy else y), (1 - c if dc else c)
        out.append(((px, py, pc), 4 * px + 2 * py + pc))
    return out


def _comm_out_shapes(comm):
    return [jax.ShapeDtypeStruct(((N_DEV,) + a.shape) if kind == "gather" else a.shape, a.dtype) for kind, a in comm]


def _comm_scratch(comm):
    n = len(comm)
    return [pltpu.SemaphoreType.DMA((n, 7)), pltpu.SemaphoreType.DMA((n, 7)), pltpu.SemaphoreType.DMA((n,))]


def _comm_copies(kinds, in_refs, out_refs, sems, place):
    send_sems, recv_sems, local_sems = sems
    x, y, c, me = place
    local, remote = [], []
    for w, kind in enumerate(kinds):
        mine = in_refs[w] if kind == "gather" else in_refs[w].at[me]
        local.append(pltpu.make_async_copy(mine, out_refs[w].at[me], local_sems.at[w]))
        for k, (peer, peer_idx) in enumerate(_peers(x, y, c)):
            remote.append(pltpu.make_async_remote_copy(
                src_ref=in_refs[w] if kind == "gather" else in_refs[w].at[peer_idx], dst_ref=out_refs[w].at[me],
                send_sem=send_sems.at[w, k], recv_sem=recv_sems.at[w, k], device_id=peer, device_id_type=MESH_ID))
    return local, remote


def _comm_start(kinds, in_refs, out_refs, sems, place):
    local, remote = _comm_copies(kinds, in_refs, out_refs, sems, place)
    for cp in local + remote:
        cp.start()


def _comm_wait(kinds, in_refs, out_refs, sems, place):
    local, remote = _comm_copies(kinds, in_refs, out_refs, sems, place)
    for cp in remote:
        cp.wait_recv()
    for cp in remote:
        cp.wait_send()
    for cp in local:
        cp.wait()


def _exchange(comm, *, name):
    n = len(comm)
    kinds = [k for k, _ in comm]

    def body(*refs):
        place = _mesh_place()
        _comm_start(kinds, refs[:n], refs[n:2 * n], refs[2 * n:], place)
        _comm_wait(kinds, refs[:n], refs[n:2 * n], refs[2 * n:], place)

    return pl.pallas_call(
        body, name=name, out_shape=_comm_out_shapes(comm), in_specs=[HBM_SPEC] * n, out_specs=[HBM_SPEC] * n,
        scratch_shapes=_comm_scratch(comm),
    )(*[a for _, a in comm])


def _flash_fwd(qa, ka, va, *, q_blk0, k_blk0, v_blk0, W, n_pairs, B, S, scale, csh=None, crow=None, comm=(), name):
    t = ATT_TILE
    nq = S // t
    P = PAIRS_PER_STEP_FWD
    decay = csh is not None
    split = W == LANES
    assert n_pairs % P == 0 and q_blk0 % P == 0 and k_blk0 % P == 0 and v_blk0 % P == 0
    n_c, kinds = len(comm), [k for k, _ in comm]
    n_in = 5 if decay else 3
    fold_scale = math.log2(scale).is_integer()
    n_steps = (B, n_pairs // P, nq)

    def body(*refs):
        c_in, c_out = refs[n_in:n_in + n_c], refs[n_in + n_c + 2:n_in + 2 * n_c + 2]
        sems = refs[n_in + 2 * n_c + 4:]
        refs = refs[:n_in] + refs[n_in + n_c:n_in + n_c + 2] + refs[n_in + 2 * n_c + 2:n_in + 2 * n_c + 4]
        if decay:
            q_ref, k_ref, v_ref, csh_ref, crow_ref, o_ref, lse_ref, m_s, acc_s = refs
        else:
            q_ref, k_ref, v_ref, o_ref, lse_ref, m_s, acc_s = refs
        g, i = pl.program_id(1), pl.program_id(2)
        if n_c:
            place = _mesh_place()
            ids = [pl.program_id(ax) for ax in range(3)]

            @pl.when((ids[0] == 0) & (ids[1] == 0) & (ids[2] == 0))
            def _():
                _comm_start(kinds, c_in, c_out, sems, place)

        lo = _low_half((t, LANES))
        qv = q_ref[...]
        qh = []
        for pr in range(P):
            qp = qv[:, pr * W:(pr + 1) * W]
            qh += [jnp.where(lo, qp, jnp.zeros_like(qp)), jnp.where(lo, jnp.zeros_like(qp), qp)] if split \
                else [qp[:, :LANES], qp[:, LANES:]]
        if fold_scale:
            qh = [x * scale for x in qh]
        if decay:
            cq = [jnp.broadcast_to(_head_column(csh_ref[...], 2 * P * g + hd), (t, LANES)) for hd in range(2 * P)]
        m_s[...] = jnp.full(m_s.shape, NEG_INF, F32)
        acc_s[...] = jnp.zeros(acc_s.shape, F32)

        def step(j, masked):
            rows = pl.ds(pl.multiple_of(j * t, t), t)
            kb, vb = k_ref[rows, :], v_ref[rows, :]
            for pr in range(P):
                kp, vp = kb[:, pr * W:(pr + 1) * W], vb[:, pr * LANES:(pr + 1) * LANES]
                ones = jnp.ones_like(vp)
                vaug = [jnp.where(lo, vp, ones), jnp.where(lo, ones, vp)]
                for half in range(2):
                    hd = 2 * pr + half
                    kh = kp if split else kp[:, half * LANES:(half + 1) * LANES]
                    s = lax.dot_general(qh[hd], kh, NT, preferred_element_type=F32)
                    if not fold_scale:
                        s = s * scale
                    if decay:
                        s = s + _widen(cq[hd], t) - crow_ref[hd, j]
                    if masked:
                        s = _causal_mask(s)
                    m_prev = m_s[hd]
                    m_new = jnp.maximum(m_prev, jnp.max(s, -1, keepdims=True))
                    p = jnp.exp(s - _widen(m_new, t))
                    acc_s[hd] = jnp.exp(m_prev - m_new) * acc_s[hd] + lax.dot_general(
                        p.astype(BF16), vaug[half], NN, preferred_element_type=F32)
                    m_s[hd] = m_new

        def loop_body(j, carry):
            step(j, False)
            return carry

        lax.fori_loop(0, i, loop_body, 0)
        step(i, True)
        for pr in range(P):
            acc0, acc1 = acc_s[2 * pr], acc_s[2 * pr + 1]
            _, l0 = _both_halves(acc0, lo)
            l1, _ = _both_halves(acc1, lo)
            cols = slice(pr * LANES, (pr + 1) * LANES)
            o_ref[:, cols] = jnp.where(lo, acc0 / l0, acc1 / l1).astype(BF16)
            lse_ref[:, cols] = jnp.where(lo, m_s[2 * pr] + jnp.log(l0), m_s[2 * pr + 1] + jnp.log(l1))
        if n_c:
            @pl.when((ids[0] == n_steps[0] - 1) & (ids[1] == n_steps[1] - 1) & (ids[2] == n_steps[2] - 1))
            def _():
                _comm_wait(kinds, c_in, c_out, sems, place)

    in_specs = [pl.BlockSpec((t, P * W), lambda b, g, i: (b * nq + i, q_blk0 // P + g)),
                pl.BlockSpec((S, P * W), lambda b, g, i: (b, k_blk0 // P + g)),
                pl.BlockSpec((S, P * LANES), lambda b, g, i: (b, v_blk0 // P + g))]
    args = [qa, ka, va]
    if decay:
        in_specs += [pl.BlockSpec((None, t, LANES), lambda b, g, i: (b, i, 0)),
                     pl.BlockSpec((None, 2 * P, nq, 1, t), lambda b, g, i: (b, g, 0, 0, 0))]
        args += [csh, crow]
    out_spec = pl.BlockSpec((t, P * LANES), lambda b, g, i: (b * nq + i, g))
    res = pl.pallas_call(
        body, name=name, grid=n_steps, in_specs=in_specs + [HBM_SPEC] * n_c,
        out_specs=[out_spec, out_spec] + [HBM_SPEC] * n_c,
        out_shape=[jax.ShapeDtypeStruct((B * S, n_pairs * LANES), BF16),
                   jax.ShapeDtypeStruct((B * S, n_pairs * LANES), F32)] + _comm_out_shapes(comm),
        scratch_shapes=[pltpu.VMEM((2 * P, t, LANES), F32), pltpu.VMEM((2 * P, t, LANES), F32)]
        + (_comm_scratch(comm) if n_c else []),
        compiler_params=_cparams(*(("arbitrary",) * 3 if n_c else ("parallel",) * 3)),
    )(*args, *[a for _, a in comm])
    return res[0], res[1], list(res[2:])


def _flash_bwd(qa, ka, va, oa, doa, lsea, *, q_blk0, k_blk0, v_blk0, do_blk0, W, n_pairs, B, S, scale, qk_dtype,
               csh=None, crow=None, comm=(), name):
    t = ATT_TILE_BWD
    nq = S // t
    P = PAIRS_PER_STEP_BWD
    decay = csh is not None
    if decay:
        crow = crow.reshape(B, 2 * n_pairs, nq, 1, t)
    split = W == LANES
    assert n_pairs % P == 0 and q_blk0 % P == 0 and k_blk0 % P == 0 and v_blk0 % P == 0 and do_blk0 % P == 0
    n_c, kinds = len(comm), [k for k, _ in comm]
    n_in, n_out, n_scr = (8, 5, 8) if decay else (6, 3, 5)
    n_steps = (B, n_pairs // P, nq)

    def body(*refs):
        c_in = refs[n_in:n_in + n_c]
        c_out = refs[n_in + n_c + n_out:n_in + 2 * n_c + n_out]
        sems = refs[n_in + 2 * n_c + n_out + n_scr:]
        refs = (refs[:n_in] + refs[n_in + n_c:n_in + n_c + n_out]
                + refs[n_in + 2 * n_c + n_out:n_in + 2 * n_c + n_out + n_scr])
        if n_c:
            place = _mesh_place()
            ids = [pl.program_id(ax) for ax in range(3)]

            @pl.when((ids[0] == 0) & (ids[1] == 0) & (ids[2] == 0))
            def _():
                _comm_start(kinds, c_in, c_out, sems, place)

        if decay:
            (q_ref, k_ref, v_ref, o_ref, do_ref, lse_ref, csh_ref, crow_ref, dq_ref, dk_ref, dv_ref, dck_ref, dcq_ref,
             dq_s, lse_s, delta_s, dk_s, dv_s, cq_s, dcq_s, dck_s) = refs
        else:
            (q_ref, k_ref, v_ref, o_ref, do_ref, lse_ref, dq_ref, dk_ref, dv_ref,
             dq_s, lse_s, delta_s, dk_s, dv_s) = refs
        g, j = pl.program_id(1), pl.program_id(2)
        lo = _low_half((t, LANES))

        @pl.when(j == 0)
        def _():
            lo_s = _low_half((S, LANES))
            dq_s[...] = jnp.zeros(dq_s.shape, F32)
            for pr in range(P):
                cols = slice(pr * LANES, (pr + 1) * LANES)
                lse_s[2 * pr], lse_s[2 * pr + 1] = _both_halves(lse_ref[:, cols], lo_s)
                dd = do_ref[:, cols].astype(F32) * o_ref[:, cols].astype(F32)
                delta_s[2 * pr] = jnp.broadcast_to(jnp.sum(jnp.where(lo_s, dd, 0.0), -1, keepdims=True), (S, LANES))
                delta_s[2 * pr + 1] = jnp.broadcast_to(jnp.sum(jnp.where(lo_s, 0.0, dd), -1, keepdims=True),
                                                       (S, LANES))
            if decay:
                for hd in range(2 * P):
                    cq_s[hd] = jnp.broadcast_to(_head_column(csh_ref[...], 2 * P * g + hd), (S, LANES))
                dcq_s[...] = jnp.zeros(dcq_s.shape, F32)

        kb, vb = k_ref[...], v_ref[...]
        kh, vh = [], []
        for pr in range(P):
            kp, vp = kb[:, pr * W:(pr + 1) * W], vb[:, pr * LANES:(pr + 1) * LANES]
            zk, zv = jnp.zeros_like(kp), jnp.zeros_like(vp)
            kh += [jnp.where(lo, kp, zk), jnp.where(lo, zk, kp)] if split else [kp[:, :LANES], kp[:, LANES:]]
            vh += [jnp.where(lo, vp, zv), jnp.where(lo, zv, vp)]
        dk_s[...] = jnp.zeros(dk_s.shape, F32)
        dv_s[...] = jnp.zeros(dv_s.shape, F32)
        if decay:
            dck_s[...] = jnp.zeros(dck_s.shape, F32)

        def step(i, masked):
            rows = pl.ds(pl.multiple_of(i * t, t), t)
            qi, doi = q_ref[rows, :], do_ref[rows, :]
            for pr in range(P):
                qp, dop = qi[:, pr * W:(pr + 1) * W], doi[:, pr * LANES:(pr + 1) * LANES]
                for half in range(2):
                    hd = 2 * pr + half
                    qx = qp if split else qp[:, half * LANES:(half + 1) * LANES]
                    s = lax.dot_general(qx, kh[hd], NT, preferred_element_type=F32) * scale
                    if decay:
                        s = s + _widen(cq_s[hd, rows, :], t) - crow_ref[hd, j]
                    if masked:
                        s = _causal_mask(s)
                    p = jnp.exp(s - _widen(lse_s[hd, rows, :], t))
                    dv_s[hd] += lax.dot_general(p.astype(BF16), dop, TN, preferred_element_type=F32)
                    dp = lax.dot_general(dop, vh[hd], NT, preferred_element_type=F32)
                    ds = p * (dp - _widen(delta_s[hd, rows, :], t))
                    dss = (ds * scale).astype(BF16)
                    dk_s[hd] += lax.dot_general(dss, qx, TN, preferred_element_type=F32)
                    dqc = lax.dot_general(dss, kh[hd], NN, preferred_element_type=F32)
                    if split:
                        dq_s[rows, pr * W:(pr + 1) * W] += dqc
                    else:
                        dq_s[rows, hd * LANES:(hd + 1) * LANES] += dqc
                    if decay:
                        dck_s[hd] -= jnp.sum(ds, 0, keepdims=True)
                        part = ds[:, :LANES]
                        for c in range(1, t // LANES):
                            part = part + ds[:, c * LANES:(c + 1) * LANES]
                        dcq_s[hd, rows, :] += part

        def loop_body(i, carry):
            step(i, False)
            return carry

        step(j, True)
        lax.fori_loop(j + 1, nq, loop_body, 0)
        for pr in range(P):
            if split:
                dk_ref[:, pr * W:(pr + 1) * W] = jnp.where(lo, dk_s[2 * pr], dk_s[2 * pr + 1]).astype(dk_ref.dtype)
            else:
                for half in range(2):
                    hd = 2 * pr + half
                    dk_ref[:, hd * LANES:(hd + 1) * LANES] = dk_s[hd].astype(dk_ref.dtype)
            dv_ref[:, pr * LANES:(pr + 1) * LANES] = jnp.where(lo, dv_s[2 * pr], dv_s[2 * pr + 1]).astype(BF16)
        if decay:
            dck_ref[...] = dck_s[...]

        @pl.when(j == nq - 1)
        def _():
            dq_ref[...] = dq_s[...].astype(dq_ref.dtype)
            if decay:
                for hd in range(2 * P):
                    dcq_ref[hd] = jnp.sum(dcq_s[hd].T, 0, keepdims=True)

        if n_c:
            @pl.when((ids[0] == n_steps[0] - 1) & (ids[1] == n_steps[1] - 1) & (ids[2] == n_steps[2] - 1))
            def _():
                _comm_wait(kinds, c_in, c_out, sems, place)

    full = lambda w, blk0: pl.BlockSpec((S, P * w), lambda b, g, j: (b, blk0 // P + g))
    blk = lambda w, blk0: pl.BlockSpec((t, P * w), lambda b, g, j: (b * nq + j, blk0 // P + g))
    in_specs = [full(W, q_blk0), blk(W, k_blk0), blk(LANES, v_blk0), full(LANES, 0), full(LANES, do_blk0),
                full(LANES, 0)]
    args = [qa, ka, va, oa, doa, lsea]
    T = B * S
    out_specs = [full(W, 0), blk(W, 0), blk(LANES, 0)]
    out_shape = [jax.ShapeDtypeStruct((T, n_pairs * W), qk_dtype), jax.ShapeDtypeStruct((T, n_pairs * W), qk_dtype),
                 jax.ShapeDtypeStruct((T, n_pairs * LANES), BF16)]
    per_head = lambda rows: pltpu.VMEM((2 * P, rows, LANES), F32)
    scratch = [pltpu.VMEM((S, P * W), F32), per_head(S), per_head(S), per_head(t), per_head(t)]
    if decay:
        in_specs += [pl.BlockSpec((None, S, LANES), lambda b, g, j: (b, 0, 0)),
                     pl.BlockSpec((None, 2 * P, nq, 1, t), lambda b, g, j: (b, g, 0, 0, 0))]
        args += [csh, crow]
        out_specs += [pl.BlockSpec((None, 2 * P, None, 1, t), lambda b, g, j: (b, g, j, 0, 0)),
                      pl.BlockSpec((None, 2 * P, 1, S), lambda b, g, j: (b, g, 0, 0))]
        out_shape += [jax.ShapeDtypeStruct((B, 2 * n_pairs, nq, 1, t), F32),
                      jax.ShapeDtypeStruct((B, 2 * n_pairs, 1, S), F32)]
        scratch += [per_head(S), per_head(S), pltpu.VMEM((2 * P, 1, t), F32)]
    res = pl.pallas_call(
        body, name=name, grid=n_steps, in_specs=in_specs + [HBM_SPEC] * n_c,
        out_specs=out_specs + [HBM_SPEC] * n_c, out_shape=out_shape + _comm_out_shapes(comm),
        scratch_shapes=scratch + (_comm_scratch(comm) if n_c else []),
        compiler_params=_cparams(*(("arbitrary",) * 3 if n_c else ("parallel", "parallel", "arbitrary"))),
    )(*args, *[a for _, a in comm])
    return tuple(res[:n_out]) + (list(res[n_out:]),)


def _swa_common(q_ref, kp_ref, ko_ref, vp_ref, vo_ref, n):
    Q = BLOCK_Q
    lo = _low_half((Q, LANES))
    lo2 = _low_half((2 * Q, LANES))
    kk = jnp.concatenate([kp_ref[...], ko_ref[...]], axis=0)
    vv = jnp.concatenate([vp_ref[...], vo_ref[...]], axis=0)
    kdup = [x.astype(BF16) for x in _both_halves(kk, lo2)]
    vdup = [x.astype(BF16) for x in _both_halves(vv, lo2)]
    a = lax.broadcasted_iota(jnp.int32, (SWA_GROUP * Q, 2 * Q), 0) % Q
    col = lax.broadcasted_iota(jnp.int32, (SWA_GROUP * Q, 2 * Q), 1)
    dist = a + Q - col
    valid = (dist >= 0) & (dist < SWA_WINDOW) & ((col >= Q) | (n > 0))
    qv = q_ref[...]
    qm = []
    for a_head in range(SWA_HEADS):
        qp = qv[:, (a_head // 2) * LANES:(a_head // 2 + 1) * LANES]
        keep = lo if a_head % 2 == 0 else jnp.logical_not(lo)
        qm.append(jnp.where(keep, qp, 0.0).astype(BF16))
    qs = [jnp.concatenate(qm[g * SWA_GROUP:(g + 1) * SWA_GROUP], axis=0) for g in range(SWA_KV_HEADS)]
    return lo, lo2, kdup, vdup, valid, qs


def _swa_group_logits(g, qs, kdup, valid, bias_ref):
    heads = slice(g * SWA_GROUP, (g + 1) * SWA_GROUP)
    s = lax.dot_general(qs[g], kdup[g], NT, preferred_element_type=F32) * (HEAD_DIM ** -0.5)
    s = s + bias_ref[heads].reshape(SWA_GROUP * BLOCK_Q, 2 * BLOCK_Q)
    return heads, jnp.where(valid, s, NEG_INF)


def _pair_halves(x, lo):
    Q = BLOCK_Q
    return [jnp.where(lo, x[2 * pr * Q:(2 * pr + 1) * Q], x[(2 * pr + 1) * Q:(2 * pr + 2) * Q])
            for pr in range(SWA_GROUP // 2)]


def _swa_in_specs(nb):
    Q = BLOCK_Q
    own = lambda blk: (lambda b, n: (b * nb + n, blk))
    prev = lambda blk: (lambda b, n: (b * nb + jnp.maximum(n - 1, 0), blk))
    kb, vb = EV_KS[0] // LANES, EV_VS[0] // LANES
    return [pl.BlockSpec((Q, SWA_HEADS * HEAD_DIM), own(0)), pl.BlockSpec((Q, LANES), prev(kb)),
            pl.BlockSpec((Q, LANES), own(kb)), pl.BlockSpec((Q, LANES), prev(vb)), pl.BlockSpec((Q, LANES), own(vb))]


def _swa_fwd(h, bias, sinkcol, *, B, S, comm=(), name):
    Q = BLOCK_Q
    nb = S // Q
    n_c, kinds = len(comm), [k for k, _ in comm]

    def body(*refs):
        c_in, c_out, sems = refs[7:7 + n_c], refs[9 + n_c:9 + 2 * n_c], refs[9 + 2 * n_c:]
        q_ref, kp_ref, ko_ref, vp_ref, vo_ref, bias_ref, sink_ref = refs[:7]
        o_ref, lse_ref = refs[7 + n_c:9 + n_c]
        if n_c:
            place = _mesh_place()
            ids = [pl.program_id(0), pl.program_id(1)]

            @pl.when((ids[0] == 0) & (ids[1] == 0))
            def _():
                _comm_start(kinds, c_in, c_out, sems, place)

        lo, lo2, kdup, vdup, valid, qs = _swa_common(q_ref, kp_ref, ko_ref, vp_ref, vo_ref, pl.program_id(1))
        lane = lax.broadcasted_iota(jnp.int32, (Q, LANES), 1)
        lse_blk = jnp.zeros((Q, LANES), F32)
        pairs = []
        lo4 = _low_half((SWA_GROUP * Q, LANES))
        for g in range(SWA_KV_HEADS):
            heads, s = _swa_group_logits(g, qs, kdup, valid, bias_ref)
            sink = jnp.broadcast_to(sink_ref[heads].reshape(SWA_GROUP * Q, 1), (SWA_GROUP * Q, LANES))
            m = jnp.maximum(jnp.max(s, -1, keepdims=True), sink)
            p = jnp.exp(s - _widen(m, 2 * Q))
            vaug = jnp.where(lo2, vdup[g], jnp.ones_like(vdup[g]))
            pv = lax.dot_general(p.astype(BF16), vaug, NN, preferred_element_type=F32)
            rolled = pltpu.roll(pv, HEAD_DIM, 1)
            l = jnp.where(lo4, rolled, pv) + jnp.exp(sink - m)
            out = pv / l
            lse_g = m + jnp.log(l)
            for i in range(SWA_GROUP):
                lse_blk = jnp.where(lane == g * SWA_GROUP + i, lse_g[i * Q:(i + 1) * Q], lse_blk)
            shifted = pltpu.roll(out, HEAD_DIM, 1)
            pairs += [jnp.where(lo, out[2 * pr * Q:(2 * pr + 1) * Q], shifted[(2 * pr + 1) * Q:(2 * pr + 2) * Q])
                      for pr in range(SWA_GROUP // 2)]
        o_ref[...] = jnp.concatenate(pairs, axis=1).astype(BF16)
        lse_ref[...] = lse_blk
        if n_c:
            @pl.when((ids[0] == B - 1) & (ids[1] == nb - 1))
            def _():
                _comm_wait(kinds, c_in, c_out, sems, place)

    whole = lambda shape: pl.BlockSpec(shape, lambda b, n: (0,) * len(shape))
    res = pl.pallas_call(
        body, name=name, grid=(B, nb),
        in_specs=_swa_in_specs(nb) + [whole((SWA_HEADS, Q, 2 * Q)), whole((SWA_HEADS, Q, 1))] + [HBM_SPEC] * n_c,
        out_specs=[pl.BlockSpec((Q, SWA_HEADS * HEAD_DIM), lambda b, n: (b * nb + n, 0)),
                   pl.BlockSpec((Q, LANES), lambda b, n: (b * nb + n, 0))] + [HBM_SPEC] * n_c,
        out_shape=[jax.ShapeDtypeStruct((B * S, SWA_HEADS * HEAD_DIM), BF16),
                   jax.ShapeDtypeStruct((B * S, LANES), F32)] + _comm_out_shapes(comm),
        scratch_shapes=_comm_scratch(comm) if n_c else [],
        compiler_params=_cparams(*(("arbitrary",) * 2 if n_c else ("parallel",) * 2)),
    )(h, h, h, h, h, bias, sinkcol, *[a for _, a in comm])
    return res[0], res[1], list(res[2:])


def _swa_bwd(h, o, do, lse, bias, sinkcol, *, do_blk0, B, S, name):
    Q = BLOCK_Q
    nb = S // Q
    scale = HEAD_DIM ** -0.5

    def body(q_ref, kp_ref, ko_ref, vp_ref, vo_ref, o_ref, do_ref, lse_ref, bias_ref, sink_ref,
             dq_ref, dko_ref, dkp_ref, dvo_ref, dvp_ref, dbias_ref, dsink_ref):
        @pl.when((pl.program_id(0) == 0) & (pl.program_id(1) == 0))
        def _():
            dbias_ref[...] = jnp.zeros_like(dbias_ref)
            dsink_ref[...] = jnp.zeros_like(dsink_ref)

        lo, lo2, kdup, vdup, valid, qs = _swa_common(q_ref, kp_ref, ko_ref, vp_ref, vo_ref, pl.program_id(1))
        lse_blk = lse_ref[...]
        dkk, dvv, dq_pairs = [], [], []
        for g in range(SWA_KV_HEADS):
            heads, s = _swa_group_logits(g, qs, kdup, valid, bias_ref)
            lse_g = jnp.concatenate([_head_column(lse_blk, g * SWA_GROUP + i) for i in range(SWA_GROUP)], axis=0)
            p = jnp.exp(s - lse_g)
            do_g, o_g = [], []
            for i in range(SWA_GROUP):
                cols = slice((g * SWA_GROUP + i) // 2 * LANES, ((g * SWA_GROUP + i) // 2 + 1) * LANES)
                do_p = do_ref[:, cols]
                do_g.append(jnp.where(lo if i % 2 == 0 else jnp.logical_not(lo), do_p, jnp.zeros_like(do_p)))
                o_g.append(o_ref[:, cols])
            doh, oh = jnp.concatenate(do_g, axis=0), jnp.concatenate(o_g, axis=0)
            delta = jnp.sum(doh.astype(F32) * oh.astype(F32), -1, keepdims=True)
            dp = lax.dot_general(doh, vdup[g], NT, preferred_element_type=F32)
            ds = p * (dp - delta)
            dbias_ref[heads] += ds.reshape(SWA_GROUP, Q, 2 * Q)
            dsink_ref[heads] -= (jnp.exp(sink_ref[heads].reshape(SWA_GROUP * Q, 1) - lse_g)
                                 * delta).reshape(SWA_GROUP, Q, 1)
            dss = (ds * scale).astype(BF16)
            dq_pairs += _pair_halves(lax.dot_general(dss, kdup[g], NN, preferred_element_type=F32), lo)
            dkk.append(lax.dot_general(dss, qs[g], TN, preferred_element_type=F32))
            dvv.append(lax.dot_general(p.astype(BF16), doh, TN, preferred_element_type=F32))
        dq_ref[...] = jnp.concatenate(dq_pairs, axis=1).astype(BF16)
        fold = lambda x: x + pltpu.roll(x, HEAD_DIM, 1)
        dk_blk = jnp.where(lo2, fold(dkk[0]), fold(dkk[1]))
        dv_blk = jnp.where(lo2, fold(dvv[0]), fold(dvv[1]))
        dkp_ref[...] = dk_blk[:Q]
        dko_ref[...] = dk_blk[Q:]
        dvp_ref[...] = dv_blk[:Q]
        dvo_ref[...] = dv_blk[Q:]

    whole = lambda shape: pl.BlockSpec(shape, lambda b, n: (0,) * len(shape))
    wide = lambda blk: pl.BlockSpec((Q, SWA_HEADS * HEAD_DIM), lambda b, n: (b * nb + n, blk))
    narrow = pl.BlockSpec((Q, LANES), lambda b, n: (b * nb + n, 0))
    kv_shape = jax.ShapeDtypeStruct((B * S, LANES), F32)
    return pl.pallas_call(
        body, name=name, grid=(B, nb),
        in_specs=_swa_in_specs(nb) + [wide(0), wide(do_blk0), narrow, whole((SWA_HEADS, Q, 2 * Q)),
                                      whole((SWA_HEADS, Q, 1))],
        out_specs=[wide(0), narrow, narrow, narrow, narrow, whole((SWA_HEADS, Q, 2 * Q)), whole((SWA_HEADS, Q, 1))],
        out_shape=[jax.ShapeDtypeStruct((B * S, SWA_HEADS * HEAD_DIM), BF16), kv_shape, kv_shape, kv_shape, kv_shape,
                   jax.ShapeDtypeStruct((SWA_HEADS, Q, 2 * Q), F32), jax.ShapeDtypeStruct((SWA_HEADS, Q, 1), F32)],
        compiler_params=_cparams("arbitrary", "arbitrary"),
    )(h, h, h, h, h, o, do, lse, bias, sinkcol)


def _bias_bucket_sum(dbias, bucket, *, name):
    def body(d_ref, b_ref, o_ref):
        dbv, bk = d_ref[...], b_ref[...]
        lane = lax.broadcasted_iota(jnp.int32, (SWA_HEADS, LANES), 1)
        out = jnp.zeros((SWA_HEADS, LANES), F32)
        for b in range(REL_BUCKETS):
            part = jnp.sum(jnp.where(bk == b, dbv, 0.0), axis=1)
            tot = jnp.sum(part, axis=-1, keepdims=True)
            out = out + jnp.where(lane == b, tot, 0.0)
        o_ref[...] = out

    return pl.pallas_call(
        body, name=name, out_shape=jax.ShapeDtypeStruct((SWA_HEADS, LANES), F32),
        compiler_params=pltpu.CompilerParams(vmem_limit_bytes=VMEM_LIMIT_BYTES),
    )(dbias, bucket)


def _adamw_update(w, g, m, v):
    m_new = ADAM_B1 * m + (1.0 - ADAM_B1) * g
    v_new = ADAM_B2 * v + (1.0 - ADAM_B2) * jnp.square(g)
    m_hat = m_new / (1.0 - ADAM_B1 ** ADAM_STEP)
    v_hat = v_new / (1.0 - ADAM_B2 ** ADAM_STEP)
    return -ADAM_LR * (m_hat / (jnp.sqrt(v_hat) + ADAM_EPS) + ADAM_WD * w), m_new, v_new


def _adamw(w, g, m, v, *, name):
    def body(w_ref, g_ref, m_ref, v_ref, d_ref, nm_ref, nv_ref):
        d_ref[...], nm_ref[...], nv_ref[...] = _adamw_update(w_ref[...], g_ref[...], m_ref[...], v_ref[...])

    return pl.pallas_call(
        body, name=name, out_shape=[jax.ShapeDtypeStruct(w.shape, F32)] * 3,
        compiler_params=pltpu.CompilerParams(vmem_limit_bytes=VMEM_LIMIT_BYTES),
    )(w, g, m, v)


def _adamw_slots(w, parts, m, v, *, name):
    R, C = w.shape
    tr = R if R <= 512 else _pick(R, (256, 128))

    def body(w_ref, p_ref, m_ref, v_ref, g_ref, d_ref, nm_ref, nv_ref):
        g = p_ref[0].astype(F32)
        for j in range(1, N_DEV):
            g = g + p_ref[j].astype(F32)
        g_ref[...] = g
        d_ref[...], nm_ref[...], nv_ref[...] = _adamw_update(w_ref[...], g, m_ref[...], v_ref[...])

    spec = pl.BlockSpec((tr, C), lambda i: (i, 0))
    return pl.pallas_call(
        body, name=name, grid=(R // tr,),
        in_specs=[spec, pl.BlockSpec((N_DEV, tr, C), lambda i: (0, i, 0)), spec, spec], out_specs=[spec] * 4,
        out_shape=[jax.ShapeDtypeStruct((R, C), F32)] * 4, compiler_params=_cparams("parallel"),
    )(w, parts, m, v)


def _all_gather_hbm(blocks, *, name):
    n = len(blocks)

    def body(*refs):
        x_refs, out_refs = refs[:n], refs[n:2 * n]
        send_sems, recv_sems, local_sems = refs[2 * n:]
        x, y, c, _ = _mesh_place()
        me, sibling = (x, y, c), (x, y, 1 - c)
        chips = [(1 - x, y), (x, 1 - y), (1 - x, 1 - y)]

        def copy(w, k, blk, to, src=None):
            px, py, pc = blk
            slot = out_refs[w].at[4 * px + 2 * py + pc]
            return pltpu.make_async_remote_copy(
                src_ref=slot if src is None else src, dst_ref=slot,
                send_sem=send_sems.at[w, k], recv_sem=recv_sems.at[w, k], device_id=to, device_id_type=MESH_ID)

        mine = [pltpu.make_async_copy(x_refs[w], out_refs[w].at[4 * x + 2 * y + c], local_sems.at[w])
                for w in range(n)]
        for cp in mine:
            cp.start()
        first = []
        for w in range(n):
            first.append(copy(w, 0, me, sibling, src=x_refs[w]))
            first += [copy(w, 1 + j, me, (*chip, c), src=x_refs[w]) for j, chip in enumerate(chips)]
        for cp in first:
            cp.start()
        passed = []
        for j, chip in enumerate(chips):
            for w in range(n):
                copy(w, 1 + j, (*chip, c), me).wait_recv()
                fwd = copy(w, 4 + j, (*chip, c), sibling)
                fwd.start()
                passed.append(fwd)
        for w in range(n):
            copy(w, 0, sibling, me).wait_recv()
            for j, chip in enumerate(chips):
                copy(w, 4 + j, (*chip, 1 - c), me).wait_recv()
        for cp in first + passed:
            cp.wait_send()
        for cp in mine:
            cp.wait()

    return pl.pallas_call(
        body, name=name, out_shape=[jax.ShapeDtypeStruct((N_DEV,) + b.shape, b.dtype) for b in blocks],
        in_specs=[HBM_SPEC] * n, out_specs=[HBM_SPEC] * n,
        scratch_shapes=[pltpu.SemaphoreType.DMA((n, 7)), pltpu.SemaphoreType.DMA((n, 7)),
                        pltpu.SemaphoreType.DMA((n,))],
    )(*blocks)


def _all_reduce_small(block, *, name):
    R, W = block.shape

    def body(x_ref, out_ref, buf, send_sems, recv_sems):
        x, y, c, me = _mesh_place()
        copies = []
        for k, (peer, _) in enumerate(_peers(x, y, c)):
            copies.append(pltpu.make_async_remote_copy(
                src_ref=x_ref, dst_ref=buf.at[me], send_sem=send_sems.at[k], recv_sem=recv_sems.at[k],
                device_id=peer, device_id_type=MESH_ID))
        for cp in copies:
            cp.start()
        buf[me] = x_ref[...]
        for cp in copies:
            cp.wait_recv()
        for cp in copies:
            cp.wait_send()
        acc = buf[0]
        for j in range(1, N_DEV):
            acc = acc + buf[j]
        out_ref[...] = acc

    return pl.pallas_call(
        body, name=name, out_shape=jax.ShapeDtypeStruct((R, W), F32),
        in_specs=[VMEM_SPEC], out_specs=VMEM_SPEC,
        scratch_shapes=[pltpu.VMEM((N_DEV, R, W), F32), pltpu.SemaphoreType.DMA((7,)), pltpu.SemaphoreType.DMA((7,))],
    )(block)


def _assemble(name, g):
    if BIG_AXIS[name] == 2:
        return jnp.concatenate([g[j] for j in range(N_DEV)], axis=1)
    return g.reshape(N_DEV * g.shape[1], g.shape[2])


def _split_for_devices(name, g):
    if BIG_AXIS[name] == 2:
        b = g.shape[1] // N_DEV
        return jnp.stack([g[:, j * b:(j + 1) * b] for j in range(N_DEV)]).astype(BF16)
    return g.reshape(N_DEV, g.shape[0] // N_DEV, g.shape[1]).astype(BF16)


def _layer_weight_keys(i):
    j = i // 2
    mixer = [('ev_w_in', j), ('ev_w_uq', j), ('ev_w_ukv', j), ('ev_w_out', j)] if i % 2 == 0 \
        else [('od_w_in', j), ('od_w_out', j)]
    return mixer + [('w_up', i), ('w_down', i), ('ple_w_proj', i), ('ple_w_gate', i)]


def _weight_layer(key):
    name, idx = key
    return 2 * idx if name.startswith('ev_') else 2 * idx + 1 if name.startswith('od_') else idx


FIRST_GATHER = [('ev_w_in', 0), ('ev_w_uq', 0), ('ev_w_ukv', 0), ('ev_w_out', 0)]
FWD_CARRIERS = {
    'l0_mla': [('w_up', 0), ('ple_w_proj', 0), ('ple_w_gate', 0)],
    'l0_swa': [('w_down', 0)],
    'l0_out_ln1': [('od_w_out', 0)],
    'l0_up': [('od_w_in', 0)],
    'l0_down_ln2': [('w_up', 1)],
    'l0_ple_gate': [('ple_w_proj', 1), ('ple_w_gate', 1)],
    'l1_fox': [('w_down', 1), ('ev_w_in', 1), ('ev_w_uq', 1), ('ev_w_ukv', 1), ('ev_w_out', 1), ('w_up', 2)],
    'l1_up': [('w_down', 2)],
    'l1_down_ln2': [('ple_w_proj', 2), ('ple_w_gate', 2)],
    'l2_mla': [('od_w_in', 1), ('od_w_out', 1)],
    'l2_swa': [('w_up', 3)],
    'l2_up': [('w_down', 3)],
    'l2_down_ln2': [('ple_w_proj', 3), ('ple_w_gate', 3)],
}


class _MeshExchange:
    def __init__(self, shards):
        self.shards = shards
        self.weights = {i: {} for i in range(DEPTH)}
        self.pending = []
        self.in_flight = []
        self.received = {}
        got = _all_gather_hbm([self.shards[k] for k in FIRST_GATHER], name="gather_first")
        self._landed(FIRST_GATHER, got)

    def _landed(self, keys, gathered):
        for k, g in zip(keys, gathered):
            self.weights[_weight_layer(k)][k[0]] = _assemble(k[0], g)

    def layer_weights(self, i):
        return self.weights[i]

    def carry(self, kernel_name):
        return [("gather", self.shards[k]) for k in FWD_CARRIERS.get(kernel_name, [])]

    def carried(self, kernel_name, outs):
        self._landed(FWD_CARRIERS.get(kernel_name, []), outs)

    def push_grads(self, grads):
        self.pending += [(k, _split_for_devices(k[0], g)) for k, g in grads.items()]

    def bwd_items(self):
        self.in_flight, self.pending = self.pending, []
        return [("scatter", parts) for _, parts in self.in_flight]

    def bwd_done(self, outs):
        for (k, _), got in zip(self.in_flight, outs):
            self.received[k] = got
        self.in_flight = []

    def finish(self):
        if self.pending:
            outs = _exchange(self.bwd_items(), name="scatter_rest")
            self.bwd_done(outs)
        return self.received


PACK_ROWS = 8


def _pack_small(vals):
    flat = jnp.concatenate([vals[n].reshape(-1).astype(F32) for n in SMALL])
    pad = (-flat.shape[0]) % (PACK_ROWS * LANES)
    return jnp.pad(flat, (0, pad)).reshape(-1, LANES)


def _unpack_small(block, shapes):
    flat = block.reshape(-1)
    out, off = {}, 0
    for n in SMALL:
        sz = math.prod(shapes[n])
        out[n] = flat[off:off + sz].reshape(shapes[n])
        off += sz
    return out


def _rope_tables(S):
    half = MLA_ROPE // 2
    inv = 1.0 / (ROPE_THETA ** (jnp.arange(0, MLA_ROPE, 2, dtype=F32) / MLA_ROPE))
    ang = jnp.arange(S, dtype=F32)[:, None] * inv[None, :]
    cos, sin = jnp.cos(ang), jnp.sin(ang)
    zeros = jnp.zeros((S, half), F32)
    tail = jnp.zeros((S, LANES - MLA_QK), F32)

    def block(rope_part, nope_val):
        return jnp.concatenate([jnp.full((S, MLA_NOPE), nope_val, F32), rope_part, tail], -1)

    a_r = jnp.concatenate([cos, cos], -1)
    bm_r = jnp.concatenate([-sin, zeros], -1)
    bp_r = jnp.concatenate([zeros, sin], -1)
    q_tabs = tuple(block(r, v) for r, v in ((a_r, 1.0), (bm_r, 0.0), (bp_r, 0.0)))
    k_tabs = tuple(block(r, 0.0) for r in (a_r, bm_r, bp_r))
    return q_tabs, k_tabs


def _t5_bucket(dist):
    exact = REL_BUCKETS // 2
    d = jnp.maximum(dist, 1).astype(F32)
    large = exact + (jnp.log(d / exact) / math.log(REL_MAX_DIST / exact) * (REL_BUCKETS - exact)).astype(jnp.int32)
    large = jnp.minimum(large, REL_BUCKETS - 1)
    return jnp.where(dist < exact, dist, large)


def _swa_bucket_table():
    a = jnp.arange(BLOCK_Q)[:, None]
    col = jnp.arange(2 * BLOCK_Q)[None, :]
    return _t5_bucket(jnp.maximum(a + BLOCK_Q - col, 0)).astype(jnp.int32)


def _even_weights(W):
    w = W['ev_w_in']
    c_kv1 = MLA_Q_LORA + MLA_KV_LORA
    c_kr1 = c_kv1 + MLA_ROPE
    c_qs1 = c_kr1 + SWA_HEADS * HEAD_DIM
    zeros = lambda n: jnp.zeros((D_MODEL, n), w.dtype)
    w_in = jnp.concatenate([w[:, c_kr1:c_qs1], w[:, :c_kv1], w[:, c_qs1:], zeros(KR_LANE0), w[:, c_kv1:c_kr1],
                            zeros(LANES - KR_LANE0 - MLA_ROPE)], axis=1)
    uq = W['ev_w_uq'].reshape(MLA_Q_LORA, MLA_HEADS, MLA_QK)
    w_uq = jnp.pad(uq, ((0, 0), (0, 0), (0, LANES - MLA_QK))).reshape(MLA_Q_LORA, MLA_HEADS * LANES)
    ukv = W['ev_w_ukv'].reshape(MLA_KV_LORA, MLA_HEADS, MLA_NOPE + MLA_V)
    w_k = jnp.pad(ukv[..., :MLA_NOPE], ((0, 0), (0, 0), (0, LANES - MLA_NOPE))).reshape(MLA_KV_LORA, -1)
    w_v = ukv[..., MLA_NOPE:].reshape(MLA_KV_LORA, MLA_HEADS * MLA_V)
    return w_in, w_uq, w_k, w_v, W['ev_w_out']


def _even_in_grad_unpad(dw):
    kr0 = EV_KR[0] + KR_LANE0
    return jnp.concatenate([dw[:, EV_CQ[0]:EV_CKV[1]], dw[:, kr0:kr0 + MLA_ROPE], dw[:, EV_QS[0]:EV_QS[1]],
                            dw[:, EV_KS[0]:EV_VS[1]]], axis=1)


def _even_fwd(xb, W, P, i, B, S, tabs, xchg, tag):
    j = i // 2
    q_tabs, k_tabs, bias, sinkcol = tabs
    w_in, w_uq, w_k, w_v, w_out = _even_weights(W)
    h = _mm(xb, w_in, name=f"{tag}_in")
    cqn, ckvn, rq, rkv = _even_norms(h, P['ev_q_norm'][j][None], P['ev_kv_norm'][j][None], name=f"{tag}_norms")
    q = _rope(_mm(cqn, w_uq, name=f"{tag}_uq"), q_tabs, S, sign=1.0, name=f"{tag}_ropeq")
    knp = _mm(ckvn, w_k, out_dtypes=(BF16,), name=f"{tag}_uk")
    v = _mm(ckvn, w_v, out_dtypes=(BF16,), name=f"{tag}_uv")
    k = _mla_keys(knp, h, k_tabs, S, name=f"{tag}_keys")
    o_mla, lse_mla, got = _flash_fwd(q, k, v, q_blk0=0, k_blk0=0, v_blk0=0, W=2 * LANES, n_pairs=MLA_HEADS // 2,
                                     B=B, S=S, scale=MLA_QK ** -0.5, comm=xchg.carry(f"{tag}_mla"), name=f"{tag}_mla")
    xchg.carried(f"{tag}_mla", got)
    o_swa, lse_swa, got = _swa_fwd(h, bias, sinkcol, B=B, S=S, comm=xchg.carry(f"{tag}_swa"), name=f"{tag}_swa")
    xchg.carried(f"{tag}_swa", got)
    o_cat = jnp.concatenate([o_mla, o_swa], axis=-1)
    res = dict(h=h, cqn=cqn, ckvn=ckvn, rq=rq, rkv=rkv, q=q, k=k, v=v, o_mla=o_mla, lse_mla=lse_mla,
               o_swa=o_swa, lse_swa=lse_swa, o_cat=o_cat)
    return (o_cat, w_out), res


def _shift_prev(own, prev, B, S):
    prev = prev.reshape(B, S, LANES)
    shifted = jnp.concatenate([prev[:, BLOCK_Q:], jnp.zeros_like(prev[:, :BLOCK_Q])], axis=1)
    return (own + shifted.reshape(B * S, LANES)).astype(BF16)


def _even_bwd(dmb, dz1, xb, W, P, j, B, S, tabs, res, xchg, tag):
    q_tabs, k_tabs, bias, sinkcol = tabs
    w_in, w_uq, w_k, w_v, w_out = _even_weights(W)
    g = {}
    g['ev_w_out'] = _mm_tn(res['o_cat'], dmb, name=f"{tag}_dwout")
    do = _mm(dmb, w_out, trans_b=True, out_dtypes=(BF16,), name=f"{tag}_do")
    dq, dk, dv, got = _flash_bwd(res['q'], res['k'], res['v'], res['o_mla'], do, res['lse_mla'], q_blk0=0, k_blk0=0,
                                 v_blk0=0, do_blk0=0, W=2 * LANES, n_pairs=MLA_HEADS // 2, B=B, S=S,
                                 scale=MLA_QK ** -0.5, qk_dtype=F32, comm=xchg.bwd_items(), name=f"{tag}_mla_bwd")
    xchg.bwd_done(got)
    dq_pre = _rope(dq, q_tabs, S, sign=-1.0, name=f"{tag}_ropeq_bwd")
    dw_uq = _mm_tn(res['cqn'], dq_pre, name=f"{tag}_dwuq")
    g['ev_w_uq'] = dw_uq.reshape(MLA_Q_LORA, MLA_HEADS, LANES)[..., :MLA_QK].reshape(MLA_Q_LORA, MLA_HEADS * MLA_QK)
    dcqn = _mm(dq_pre, w_uq, trans_b=True, name=f"{tag}_dcqn")
    dw_k = _mm_tn(res['ckvn'], dk, name=f"{tag}_dwuk").reshape(MLA_KV_LORA, MLA_HEADS, LANES)[..., :MLA_NOPE]
    dw_v = _mm_tn(res['ckvn'], dv, name=f"{tag}_dwuv").reshape(MLA_KV_LORA, MLA_HEADS, MLA_V)
    g['ev_w_ukv'] = jnp.concatenate([dw_k, dw_v], axis=-1).reshape(MLA_KV_LORA, MLA_HEADS * (MLA_NOPE + MLA_V))
    dckvn_v = _mm(dv, w_v, trans_b=True, name=f"{tag}_dckvn_v")
    dckvn = _mm(dk, w_k, trans_b=True, extras=(dckvn_v,), epilogue=lambda acc, r: (acc + r,), name=f"{tag}_dckvn")
    dkr_pre = _mla_rope_key_grad(dk, k_tabs, S, name=f"{tag}_ropek_bwd")
    dqs, dko, dkp, dvo, dvp, dbias, dsink = _swa_bwd(res['h'], res['o_swa'], do, res['lse_swa'], bias, sinkcol,
                                                     do_blk0=1, B=B, S=S, name=f"{tag}_swa_bwd")
    dh, dgq, dgkv = _even_in_bwd(res['h'], res['rq'], res['rkv'], P['ev_q_norm'][j][None], P['ev_kv_norm'][j][None],
                                 dcqn, dckvn, dqs, _shift_prev(dko, dkp, B, S), _shift_prev(dvo, dvp, B, S), dkr_pre,
                                 name=f"{tag}_in_bwd")
    g['ev_w_in'] = _even_in_grad_unpad(_mm_tn(xb, dh, name=f"{tag}_dwin"))
    xchg.push_grads({(n, j): val for n, val in g.items()})
    dx_kwargs = dict(trans_b=True, extras=(dz1,), epilogue=lambda acc, r: (acc + DN_ALPHA * r,), name=f"{tag}_dx")
    dx = _scattering(xchg, _mm, dh, w_in, **dx_kwargs) if j == 0 else _mm(dh, w_in, **dx_kwargs)
    small = dict(ev_q_norm=dgq[0], ev_kv_norm=dgkv[0], dbias=dbias, ev_sinks=jnp.sum(dsink, axis=(1, 2)))
    return dx, small


def _odd_fwd(xb, W, P, i, B, S, xchg, tag):
    j = i // 2
    w = W['od_w_in']
    w_qkv = w[:, :ODD_QKV]
    w_f = jnp.pad(w[:, ODD_QKV:], ((0, 0), (0, LANES - FOX_HEADS)))
    bf = jnp.pad(P['od_b_f'][j], (0, LANES - FOX_HEADS))[None]
    qkv = _mm(xb, w_qkv, out_dtypes=(BF16,), name=f"{tag}_qkv")
    f = _mm(xb, w_f, name=f"{tag}_f").reshape(B, S, LANES)
    csh, chs = _fox_decay_fwd(f, bf, name=f"{tag}_decay")
    crow = chs[:, :FOX_HEADS].reshape(B, FOX_HEADS, S // ATT_TILE, 1, ATT_TILE)
    n_blk = FOX_HEADS * HEAD_DIM // LANES
    o, lse, got = _flash_fwd(qkv, qkv, qkv, q_blk0=0, k_blk0=n_blk, v_blk0=2 * n_blk, W=LANES,
                             n_pairs=FOX_HEADS // 2, B=B, S=S, scale=HEAD_DIM ** -0.5, csh=csh, crow=crow,
                             comm=xchg.carry(f"{tag}_fox"), name=f"{tag}_fox")
    xchg.carried(f"{tag}_fox", got)
    res = dict(f=f, bf=bf, csh=csh, crow=crow, qkv=qkv, o=o, lse=lse, w_qkv=w_qkv, w_f=w_f)
    return (o, W['od_w_out']), res


def _odd_bwd(dmb, dz1, xb, W, P, j, B, S, res, xchg, tag):
    g = {}
    w_out = W['od_w_out']
    g['od_w_out'] = _mm_tn(res['o'], dmb, name=f"{tag}_dwout")
    do = _mm(dmb, w_out, trans_b=True, out_dtypes=(BF16,), name=f"{tag}_do")
    qkv = res['qkv']
    n_blk = FOX_HEADS * HEAD_DIM // LANES
    dq, dk, dv, dck, dcq, got = _flash_bwd(qkv, qkv, qkv, res['o'], do, res['lse'], q_blk0=0, k_blk0=n_blk,
                                           v_blk0=2 * n_blk, do_blk0=0, W=LANES, n_pairs=FOX_HEADS // 2, B=B, S=S,
                                           scale=HEAD_DIM ** -0.5, qk_dtype=BF16, csh=res['csh'], crow=res['crow'],
                                           comm=xchg.bwd_items(), name=f"{tag}_fox_bwd")
    xchg.bwd_done(got)
    dc = dck.reshape(B, FOX_HEADS, S) + dcq.reshape(B, FOX_HEADS, S)
    dc_hs = jnp.pad(dc, ((0, 0), (0, LANES - FOX_HEADS), (0, 0)))
    df, dbf = _fox_decay_bwd(dc_hs, res['f'], res['bf'], name=f"{tag}_decay_bwd")
    df = df.reshape(B * S, LANES)
    dqkv = jnp.concatenate([dq, dk, dv], axis=-1)
    dw_qkv = _mm_tn(xb, dqkv, name=f"{tag}_dwqkv")
    dw_f = _mm_tn(xb, df, name=f"{tag}_dwf")
    g['od_w_in'] = jnp.concatenate([dw_qkv, dw_f[:, :FOX_HEADS]], axis=1)
    dxf = _mm(df, res['w_f'], trans_b=True, extras=(dz1,), epilogue=lambda acc, r: (acc + DN_ALPHA * r,),
              name=f"{tag}_dxf")
    xchg.push_grads({(n, j): val for n, val in g.items()})
    dx = _mm(dqkv, res['w_qkv'], trans_b=True, extras=(dxf,), epilogue=lambda acc, r: (acc + r,), name=f"{tag}_dx")
    small = dict(od_b_f=dbf[0, :FOX_HEADS])
    return dx, small


def _carrying(xchg, name, call, *args, **kwargs):
    comm = xchg.carry(name)
    out = call(*args, comm=comm, name=name, **kwargs)
    if comm:
        out, got = out
        xchg.carried(name, got)
    return out


def _scattering(xchg, call, *args, **kwargs):
    comm = xchg.bwd_items()
    out = call(*args, comm=comm, **kwargs)
    if comm:
        out, got = out
        xchg.bwd_done(got)
    return out


def _local_step(x, p, target, P, xchg):
    B, S, D = x.shape
    T = B * S
    q_tabs, k_tabs = _rope_tables(S)
    bucket = _swa_bucket_table()
    in_bucket = (bucket[..., None] == jnp.arange(REL_BUCKETS)).astype(F32)
    bias = jnp.einsum('acb,bh->hac', in_bucket, P['rel_bias'], precision=lax.Precision.HIGHEST)

    xc = x.reshape(T, D)
    xcb = xc.astype(BF16)
    saved = []
    for i in range(DEPTH):
        j = i // 2
        tag = f"l{i}"
        W = xchg.layer_weights(i)
        lay = dict(xb=xcb, W=W)
        if i % 2 == 0:
            sinkcol = jnp.broadcast_to(P['ev_sinks'][j][:, None, None], (SWA_HEADS, BLOCK_Q, 1)).astype(F32)
            lay['tabs'] = (q_tabs, k_tabs, bias, sinkcol)
            (o, w_out), lay['mix'] = _even_fwd(xcb, W, P, i, B, S, lay['tabs'], xchg, tag)
        else:
            (o, w_out), lay['mix'] = _odd_fwd(xcb, W, P, i, B, S, xchg, tag)
        x1, x1b, lay['xh1'], lay['r1'] = _carrying(xchg, f"{tag}_out_ln1", _mm_ln, o, w_out, xc,
                                                   P['ln1_g'][i][None], P['ln1_b'][i][None])
        lay['x1b'] = x1b
        lay['u'], lay['a'] = _carrying(xchg, f"{tag}_up", _mm, x1b, W['w_up'], out_dtypes=(F32, BF16),
                                       epilogue=lambda acc: (acc, jnp.square(jnp.maximum(acc, 0.0))))
        x2, x2b, lay['xh2'], lay['r2'] = _carrying(xchg, f"{tag}_down_ln2", _mm_ln, lay['a'], W['w_down'], x1,
                                                   P['ln2_g'][i][None], P['ln2_b'][i][None])
        lay['x2b'] = x2b
        lay['p'] = p[i].reshape(T, D_PLE)
        lay['e'] = _mm(lay['p'], W['ple_w_proj'], name=f"{tag}_ple_proj")

        def gate(acc, bg, e, x2v):
            gv = 1.0 / (1.0 + jnp.exp(-(acc + bg)))
            y = x2v + gv * e
            return y, y, gv

        xc, xcb, lay['g'] = _carrying(xchg, f"{tag}_ple_gate", _mm, x2b, W['ple_w_gate'],
                                      extras=(P['ple_b_gate'][i][None], lay['e'], x2), epilogue=gate,
                                      out_dtypes=(F32, BF16, F32))
        saved.append(lay)

    dy, sq = _loss_grad(xc, target.reshape(T, D), name="loss")

    Gs = {n: [None] * DEPTH for n in ('ln1_g', 'ln1_b', 'ln2_g', 'ln2_b', 'ple_b_gate')}
    Gs.update({n: [None] * (DEPTH // 2) for n in ('ev_q_norm', 'ev_kv_norm', 'ev_sinks', 'od_b_f')})
    dbias_total = None
    for i in reversed(range(DEPTH)):
        j = i // 2
        tag = f"l{i}b"
        lay = saved[i]
        W = lay['W']
        de, dzg, dbg = _ple_bwd_elem(dy, lay['g'], lay['e'], name=f"{tag}_ple_elem")
        Gs['ple_b_gate'][i] = dbg[0]
        g_mlp = {('ple_w_proj', i): _mm_tn(lay['p'], de, name=f"{tag}_dwproj"),
                 ('ple_w_gate', i): _mm_tn(lay['x2b'], dzg, name=f"{tag}_dwgate")}
        dz2, dz2b, dg2, db2 = _mm_ln_bwd(dzg, W['ple_w_gate'], dy, 1.0, lay['xh2'], lay['r2'], P['ln2_g'][i][None],
                                         name=f"{tag}_dx2_ln2")
        Gs['ln2_g'][i], Gs['ln2_b'][i] = dg2[0], db2[0]
        g_mlp[('w_down', i)] = _mm_tn(lay['a'], dz2b, name=f"{tag}_dwdown")
        du = _mm(dz2b, W['w_down'], trans_b=True, extras=(lay['u'],), out_dtypes=(BF16,),
                 epilogue=lambda acc, u: (acc * (2.0 * jnp.maximum(u, 0.0)),), name=f"{tag}_du")
        g_mlp[('w_up', i)] = _mm_tn(lay['x1b'], du, name=f"{tag}_dwup")
        xchg.push_grads(g_mlp)
        dz1, dz1b, dg1, db1 = _mm_ln_bwd(du, W['w_up'], dz2, DN_ALPHA, lay['xh1'], lay['r1'], P['ln1_g'][i][None],
                                         name=f"{tag}_dx1_ln1")
        Gs['ln1_g'][i], Gs['ln1_b'][i] = dg1[0], db1[0]
        if i % 2 == 0:
            dy, small = _even_bwd(dz1b, dz1, lay['xb'], W, P, j, B, S, lay['tabs'], lay['mix'], xchg, tag)
            dbias_total = small['dbias'] if dbias_total is None else dbias_total + small['dbias']
            for n in ('ev_q_norm', 'ev_kv_norm', 'ev_sinks'):
                Gs[n][j] = small[n]
        else:
            dy, small = _odd_bwd(dz1b, dz1, lay['xb'], W, P, j, B, S, lay['mix'], xchg, tag)
            Gs['od_b_f'][j] = small['od_b_f']

    grads_small = {n: jnp.stack(v) for n, v in Gs.items()}
    drel = _bias_bucket_sum(dbias_total, bucket, name="rel_bias_grad")
    grads_small['rel_bias'] = drel[:, :REL_BUCKETS].T
    return sq, dy.reshape(B, S, D), grads_small


def kernel(x, p, rel_bias, ev_w_in, ev_q_norm, ev_w_uq, ev_kv_norm, ev_w_ukv, ev_sinks, ev_w_out, od_w_in, od_b_f, od_w_out, ln1_g, ln1_b, w_up, w_down, ln2_g, ln2_b, ple_w_proj, ple_w_gate, ple_b_gate, loss_target, m_rel_bias, m_ev_w_in, m_ev_q_norm, m_ev_w_uq, m_ev_kv_norm, m_ev_w_ukv, m_ev_sinks, m_ev_w_out, m_od_w_in, m_od_b_f, m_od_w_out, m_ln1_g, m_ln1_b, m_w_up, m_w_down, m_ln2_g, m_ln2_b, m_ple_w_proj, m_ple_w_gate, m_ple_b_gate, v_rel_bias, v_ev_w_in, v_ev_q_norm, v_ev_w_uq, v_ev_kv_norm, v_ev_w_ukv, v_ev_sinks, v_ev_w_out, v_od_w_in, v_od_b_f, v_od_w_out, v_ln1_g, v_ln1_b, v_w_up, v_w_down, v_ln2_g, v_ln2_b, v_ple_w_proj, v_ple_w_gate, v_ple_b_gate):
    given = dict(locals())
    w = {n: given[n] for n in WEIGHTS}
    mom = {n: given["m_" + n] for n in WEIGHTS}
    var = {n: given["v_" + n] for n in WEIGHTS}
    small_shapes = {n: w[n].shape for n in SMALL}

    shards = {(n, idx): w[n][idx].astype(BF16) for n in BIG for idx in range(w[n].shape[0])}
    xchg = _MeshExchange(shards)
    P = {n: w[n] for n in SMALL}

    sq, grad_x, grads_small = _local_step(x, p, loss_target, P, xchg)
    loss = lax.psum(0.5 * jnp.sum(sq) / D_MODEL, ("x", "y", "c"))

    received = xchg.finish()
    g_small_packed = _all_reduce_small(_pack_small(grads_small), name="reduce_small_grads")
    g_small = _unpack_small(g_small_packed, small_shapes)

    grad, delta, new_m, new_v = {}, {}, {}, {}
    for n in BIG:
        per_layer = [_adamw_slots(w[n][idx], received[(n, idx)], mom[n][idx], var[n][idx], name=f"adamw_{n}{idx}")
                     for idx in range(w[n].shape[0])]
        grad[n], delta[n], new_m[n], new_v[n] = (jnp.stack(t) for t in zip(*per_layer))
    d, nm, nv = _adamw(_pack_small(w), g_small_packed, _pack_small(mom), _pack_small(var), name="adamw_small")
    d, nm, nv = (_unpack_small(t, small_shapes) for t in (d, nm, nv))
    for n in SMALL:
        grad[n], delta[n], new_m[n], new_v[n] = g_small[n], d[n], nm[n], nv[n]

    return (loss, grad_x, *[grad[n] for n in WEIGHTS], *[delta[n] for n in WEIGHTS],
            *[new_m[n] for n in WEIGHTS], *[new_v[n] for n in WEIGHTS])
```

```python
import math

import jax
import jax.numpy as jnp
from jax import lax
from jax.experimental import pallas as pl
from jax.experimental.pallas import tpu as pltpu

F32, BF16 = jnp.float32, jnp.bfloat16

D_MODEL = 1024
DEPTH = 4
HEAD_DIM = 64
MLA_HEADS, MLA_NOPE, MLA_ROPE, MLA_V = 8, 64, 32, 64
MLA_Q_LORA, MLA_KV_LORA = 384, 256
MLA_QK = MLA_NOPE + MLA_ROPE
ROPE_THETA = 10000.0
SWA_HEADS, SWA_KV_HEADS, SWA_WINDOW = 8, 2, 128
SWA_GROUP = SWA_HEADS // SWA_KV_HEADS
REL_BUCKETS, REL_MAX_DIST = 32, 128
FOX_HEADS = 16
D_FF = 4 * D_MODEL
D_PLE = 256
BLOCK_Q = 128
DN_ALPHA = (2 * DEPTH) ** 0.25
NORM_EPS = 1e-5
NEG_INF = -1e30
EVEN_IN = 1440
ODD_QKV = 3 * FOX_HEADS * HEAD_DIM
LANES = 128

EV_QS = (0, 512)
EV_CQ = (512, 896)
EV_CKV = (896, 1152)
EV_KS = (1152, 1280)
EV_VS = (1280, 1408)
EV_KR = (1408, 1536)
EVEN_IN_PAD = 1536
KR_LANE0 = MLA_NOPE

ADAM_LR, ADAM_B1, ADAM_B2, ADAM_EPS, ADAM_WD, ADAM_STEP = 0.001, 0.9, 0.999, 1e-08, 0.01, 10

N_DEV = 8
VMEM_LIMIT_BYTES = 48 * 1024 * 1024
ATT_TILE = 512
ATT_TILE_BWD = 512
PAIRS_PER_STEP_FWD = 4
PAIRS_PER_STEP_BWD = 2

NN = (((1,), (0,)), ((), ()))
NT = (((1,), (1,)), ((), ()))
TN = (((0,), (0,)), ((), ()))

BIG = ['ev_w_in', 'ev_w_uq', 'ev_w_ukv', 'ev_w_out', 'od_w_in', 'od_w_out', 'w_up', 'w_down',
       'ple_w_proj', 'ple_w_gate']
BIG_AXIS = {'ev_w_in': 2, 'ev_w_uq': 2, 'ev_w_ukv': 2, 'ev_w_out': 1, 'od_w_in': 2, 'od_w_out': 1,
            'w_up': 2, 'w_down': 1, 'ple_w_proj': 2, 'ple_w_gate': 1}
SMALL = ['rel_bias', 'ev_q_norm', 'ev_kv_norm', 'ev_sinks', 'od_b_f', 'ln1_g', 'ln1_b', 'ln2_g', 'ln2_b',
         'ple_b_gate']
WEIGHTS = ['rel_bias', 'ev_w_in', 'ev_q_norm', 'ev_w_uq', 'ev_kv_norm', 'ev_w_ukv', 'ev_sinks', 'ev_w_out',
           'od_w_in', 'od_b_f', 'od_w_out', 'ln1_g', 'ln1_b', 'w_up', 'w_down', 'ln2_g', 'ln2_b',
           'ple_w_proj', 'ple_w_gate', 'ple_b_gate']


def _cparams(*sem):
    return pltpu.CompilerParams(dimension_semantics=sem, vmem_limit_bytes=VMEM_LIMIT_BYTES)


def _pick(n, cands):
    for c in cands:
        if n % c == 0:
            return c
    return n


MM_STEP_BYTES = 10 * 1024 * 1024
MM_OUT_BYTES = 8 * 1024 * 1024
MM_CHUNK = 512


def _mm(a, b, *, trans_b=False, extras=(), epilogue=None, row_epilogue=None, out_dtypes=(F32,), out_widths=None,
        n_sums=0, comm=(), name):
    M, K = a.shape
    N = b.shape[0] if trans_b else b.shape[1]
    n_ex, n_out = len(extras), len(out_dtypes)
    n_rows_out = n_out - n_sums
    out_widths = (N,) * n_out if out_widths is None else out_widths
    row_bytes = K * a.dtype.itemsize + (sum(w * jnp.dtype(d).itemsize
                                            for w, d in zip(out_widths[:n_rows_out], out_dtypes))
                                        + sum(e.shape[1] * e.dtype.itemsize for e in extras if e.shape[0] == M)
                                        + (4 * N if row_epilogue is not None else 0))
    tm = next((c for c in (1024, 512, 256) if M % c == 0 and c * row_bytes <= MM_STEP_BYTES), 128)
    nc = _pick(N, (MM_CHUNK, 384, 256, 128))
    n_c, kinds = len(comm), [k for k, _ in comm]
    n_scr = 1 if row_epilogue is not None else 0

    def body(*refs):
        c_in = refs[2 + n_ex:2 + n_ex + n_c]
        c_out = refs[2 + n_ex + n_c + n_out:2 + n_ex + 2 * n_c + n_out]
        sems = refs[2 + n_ex + 2 * n_c + n_out + n_scr:]
        refs = refs[:2 + n_ex] + refs[2 + n_ex + n_c:2 + n_ex + n_c + n_out] \
            + refs[2 + n_ex + 2 * n_c + n_out:2 + n_ex + 2 * n_c + n_out + n_scr]
        if n_c:
            place = _mesh_place()
            step = pl.program_id(0)

            @pl.when(step == 0)
            def _():
                _comm_start(kinds, c_in, c_out, sems, place)

        a_ref, b_ref = refs[:2]
        ex = refs[2:2 + n_ex]
        outs = refs[2 + n_ex:2 + n_ex + n_out]
        av = a_ref[...].astype(BF16)
        for n0 in range(0, N, nc):
            cols = slice(n0, n0 + nc)
            bv = (b_ref[cols, :] if trans_b else b_ref[:, cols]).astype(BF16)
            acc = lax.dot_general(av, bv, NT if trans_b else NN, preferred_element_type=F32)
            if row_epilogue is not None:
                refs[-1][:, cols] = acc
                continue
            res = epilogue(acc, *[e[:, cols] for e in ex]) if epilogue is not None else (acc,)
            for o, r in zip(outs, res):
                o[:, cols] = r.astype(o.dtype)
        if row_epilogue is not None:
            res = row_epilogue(refs[-1][...], *[e[...] for e in ex])
            for o, r in zip(outs[:n_rows_out], res):
                o[...] = r.astype(o.dtype)
            if n_sums:
                @pl.when(pl.program_id(0) == 0)
                def _():
                    for o in outs[n_rows_out:]:
                        o[...] = jnp.zeros_like(o)

                for o, r in zip(outs[n_rows_out:], res[n_rows_out:]):
                    o[...] += r
        if n_c:
            @pl.when(step == M // tm - 1)
            def _():
                _comm_wait(kinds, c_in, c_out, sems, place)

    in_specs = [pl.BlockSpec((tm, K), lambda i: (i, 0)), pl.BlockSpec(b.shape, lambda i: (0, 0))]
    for e in extras:
        if e.shape[0] == M:
            in_specs.append(pl.BlockSpec((tm, e.shape[1]), lambda i: (i, 0)))
        elif e.shape == (1, N):
            in_specs.append(pl.BlockSpec((1, N), lambda i: (0, 0)))
        else:
            raise ValueError(f"extra operand of shape {e.shape} for a ({M}, {N}) result")
    res = pl.pallas_call(
        body, name=name, grid=(M // tm,), in_specs=in_specs + [HBM_SPEC] * n_c,
        out_specs=[pl.BlockSpec((tm, w), lambda i: (i, 0)) for w in out_widths[:n_rows_out]]
        + [pl.BlockSpec((1, w), lambda i: (0, 0)) for w in out_widths[n_rows_out:]] + [HBM_SPEC] * n_c,
        out_shape=[jax.ShapeDtypeStruct((M, w), d) for w, d in zip(out_widths[:n_rows_out], out_dtypes)]
        + [jax.ShapeDtypeStruct((1, w), d) for w, d in zip(out_widths[n_rows_out:], out_dtypes[n_rows_out:])]
        + _comm_out_shapes(comm),
        scratch_shapes=([pltpu.VMEM((tm, N), F32)] if row_epilogue is not None else [])
        + (_comm_scratch(comm) if n_c else []),
        compiler_params=_cparams("arbitrary" if n_sums or n_c else "parallel"),
    )(a, b, *extras, *[c for _, c in comm])
    main = res[0] if n_out == 1 else tuple(res[:n_out])
    return (main, list(res[n_out:])) if n_c else main


def _mm_tn(a, b, *, name):
    T, K = a.shape
    N = b.shape[1]
    bk, bn = K, N
    while bk * bn * 4 > MM_OUT_BYTES:
        if bn >= bk and bn % (2 * LANES) == 0:
            bn //= 2
        else:
            bk //= 2
    tt = _pick(T, (1024, 512, 256))
    ck, cn = _pick(bk, (MM_CHUNK, 384, 256, 128)), _pick(bn, (MM_CHUNK, 384, 256, 128))

    def body(a_ref, b_ref, o_ref):
        t = pl.program_id(2)

        @pl.when(t == 0)
        def _():
            o_ref[...] = jnp.zeros_like(o_ref)

        for r0 in range(0, bk, ck):
            av = a_ref[:, r0:r0 + ck].astype(BF16)
            for c0 in range(0, bn, cn):
                o_ref[r0:r0 + ck, c0:c0 + cn] += lax.dot_general(
                    av, b_ref[:, c0:c0 + cn].astype(BF16), TN, preferred_element_type=F32)

    return pl.pallas_call(
        body, name=name, grid=(K // bk, N // bn, T // tt),
        in_specs=[pl.BlockSpec((tt, bk), lambda i, j, t: (t, i)), pl.BlockSpec((tt, bn), lambda i, j, t: (t, j))],
        out_specs=pl.BlockSpec((bk, bn), lambda i, j, t: (i, j)),
        out_shape=jax.ShapeDtypeStruct((K, N), F32),
        compiler_params=_cparams("parallel", "parallel", "arbitrary"),
    )(a, b)


ROW_TILE = 256


def _row_spec(cols, col_block=0):
    return pl.BlockSpec((ROW_TILE, cols), lambda i: (i, col_block))


def _tab_spec(cols, period):
    return pl.BlockSpec((ROW_TILE, cols), lambda i: (i % period, 0))


def _full_spec(shape):
    return pl.BlockSpec(shape, lambda i: (0,) * len(shape))


def _mm_ln(a, w, x, g, b, *, comm=(), name):
    def ln_rows(m, xv, gv, bv):
        z = DN_ALPHA * xv + m
        mu = jnp.mean(z, -1, keepdims=True)
        zc = z - mu
        r = lax.rsqrt(jnp.mean(zc * zc, -1, keepdims=True) + NORM_EPS)
        xh = zc * r
        y = xh * gv + bv
        return y, y, xh, jnp.broadcast_to(r, (r.shape[0], LANES))

    D = w.shape[1]
    return _mm(a, w, extras=(x, g, b), row_epilogue=ln_rows, out_dtypes=(F32, BF16, F32, F32),
               out_widths=(D, D, D, LANES), comm=comm, name=name)


def _mm_ln_bwd(a, w, resid, resid_scale, xh, r, g, *, name):
    def ln_bwd_rows(acc, rv, xhv, rstd, gv):
        dyv = acc + resid_scale * rv
        dyg = dyv * gv
        c1 = jnp.mean(dyg, -1, keepdims=True)
        c2 = jnp.mean(dyg * xhv, -1, keepdims=True)
        dz = _widen(rstd, dyv.shape[-1]) * (dyg - c1 - xhv * c2)
        return dz, dz, jnp.sum(dyv * xhv, 0, keepdims=True), jnp.sum(dyv, 0, keepdims=True)

    D = w.shape[0]
    return _mm(a, w, trans_b=True, extras=(resid, xh, r, g), row_epilogue=ln_bwd_rows,
               out_dtypes=(F32, BF16, F32, F32), out_widths=(D, D, D, D), n_sums=2, name=name)


def _loss_grad(y, target, *, name):
    T, D = y.shape

    def body(y_ref, t_ref, dy_ref, sq_ref):
        err = y_ref[...] - t_ref[...]
        dy_ref[...] = err / D

        @pl.when(pl.program_id(0) == 0)
        def _():
            sq_ref[...] = jnp.zeros_like(sq_ref)

        sq_ref[...] += jnp.sum(err * err, 0, keepdims=True)

    return pl.pallas_call(
        body, name=name, grid=(T // ROW_TILE,),
        in_specs=[_row_spec(D), _row_spec(D)],
        out_specs=[_row_spec(D), _full_spec((1, D))],
        out_shape=[jax.ShapeDtypeStruct((T, D), F32), jax.ShapeDtypeStruct((1, D), F32)],
        compiler_params=_cparams("arbitrary"),
    )(y, target)


def _ple_bwd_elem(dx3, g, e, *, name):
    T, D = dx3.shape

    def body(dx_ref, g_ref, e_ref, de_ref, dz_ref, db_ref):
        dx, gv = dx_ref[...], g_ref[...]
        de_ref[...] = (dx * gv).astype(BF16)
        dz = dx * e_ref[...] * gv * (1.0 - gv)
        dz_ref[...] = dz.astype(BF16)

        @pl.when(pl.program_id(0) == 0)
        def _():
            db_ref[...] = jnp.zeros_like(db_ref)

        db_ref[...] += jnp.sum(dz, 0, keepdims=True)

    return pl.pallas_call(
        body, name=name, grid=(T // ROW_TILE,),
        in_specs=[_row_spec(D), _row_spec(D), _row_spec(D)],
        out_specs=[_row_spec(D), _row_spec(D), _full_spec((1, D))],
        out_shape=[jax.ShapeDtypeStruct((T, D), BF16), jax.ShapeDtypeStruct((T, D), BF16),
                   jax.ShapeDtypeStruct((1, D), F32)],
        compiler_params=_cparams("arbitrary"),
    )(dx3, g, e)


def _rotate(xv, a, bm, bp, sign):
    half = MLA_ROPE // 2
    width = xv.shape[-1]
    a, bm, bp = (_widen(t, width) for t in (a, bm, bp))
    return xv * a + sign * (pltpu.roll(xv, width - half, 1) * bm + pltpu.roll(xv, half, 1) * bp)


def _rope(x, tabs, seq, *, sign, name):
    T, width = x.shape

    def body(x_ref, a_ref, bm_ref, bp_ref, o_ref):
        o_ref[...] = _rotate(x_ref[...], a_ref[...], bm_ref[...], bp_ref[...], sign).astype(BF16)

    return pl.pallas_call(
        body, name=name, grid=(T // ROW_TILE,),
        in_specs=[_row_spec(width)] + [_tab_spec(LANES, seq // ROW_TILE)] * 3,
        out_specs=_row_spec(width),
        out_shape=jax.ShapeDtypeStruct((T, width), BF16),
        compiler_params=_cparams("parallel"),
    )(x, *tabs)


def _mla_keys(knp, h, k_tabs, seq, *, name):
    T = knp.shape[0]

    def body(k_ref, h_ref, a_ref, bm_ref, bp_ref, o_ref):
        kr = _rotate(h_ref[...], a_ref[...], bm_ref[...], bp_ref[...], 1.0)
        for hd in range(MLA_HEADS):
            cols = slice(hd * LANES, (hd + 1) * LANES)
            o_ref[:, cols] = (k_ref[:, cols].astype(F32) + kr).astype(BF16)

    return pl.pallas_call(
        body, name=name, grid=(T // ROW_TILE,),
        in_specs=[_row_spec(MLA_HEADS * LANES), _row_spec(LANES, EV_KR[0] // LANES)]
        + [_tab_spec(LANES, seq // ROW_TILE)] * 3,
        out_specs=_row_spec(MLA_HEADS * LANES),
        out_shape=jax.ShapeDtypeStruct((T, MLA_HEADS * LANES), BF16),
        compiler_params=_cparams("parallel"),
    )(knp, h, *k_tabs)


def _mla_rope_key_grad(dk, k_tabs, seq, *, name):
    T = dk.shape[0]

    def body(dk_ref, a_ref, bm_ref, bp_ref, o_ref):
        tot = dk_ref[:, 0:LANES]
        for hd in range(1, MLA_HEADS):
            tot = tot + dk_ref[:, hd * LANES:(hd + 1) * LANES]
        o_ref[...] = _rotate(tot, a_ref[...], bm_ref[...], bp_ref[...], -1.0).astype(BF16)

    return pl.pallas_call(
        body, name=name, grid=(T // ROW_TILE,),
        in_specs=[_row_spec(MLA_HEADS * LANES)] + [_tab_spec(LANES, seq // ROW_TILE)] * 3,
        out_specs=_row_spec(LANES),
        out_shape=jax.ShapeDtypeStruct((T, LANES), BF16),
        compiler_params=_cparams("parallel"),
    )(dk, *k_tabs)


def _even_norms(h, gq, gkv, *, name):
    T = h.shape[0]

    def body(h_ref, gq_ref, gkv_ref, cq_ref, ckv_ref, rq_ref, rkv_ref):
        cq = h_ref[:, EV_CQ[0]:EV_CQ[1]]
        rq = lax.rsqrt(jnp.mean(cq * cq, -1, keepdims=True) + NORM_EPS)
        cq_ref[...] = (cq * rq * gq_ref[...]).astype(BF16)
        rq_ref[...] = jnp.broadcast_to(rq, rq_ref.shape)
        ckv = h_ref[:, EV_CKV[0]:EV_CKV[1]]
        rkv = lax.rsqrt(jnp.mean(ckv * ckv, -1, keepdims=True) + NORM_EPS)
        ckv_ref[...] = (ckv * rkv * gkv_ref[...]).astype(BF16)
        rkv_ref[...] = jnp.broadcast_to(rkv, rkv_ref.shape)

    return pl.pallas_call(
        body, name=name, grid=(T // ROW_TILE,),
        in_specs=[_row_spec(EVEN_IN_PAD), _full_spec((1, MLA_Q_LORA)), _full_spec((1, MLA_KV_LORA))],
        out_specs=[_row_spec(MLA_Q_LORA), _row_spec(MLA_KV_LORA), _row_spec(LANES), _row_spec(LANES)],
        out_shape=[jax.ShapeDtypeStruct((T, MLA_Q_LORA), BF16), jax.ShapeDtypeStruct((T, MLA_KV_LORA), BF16),
                   jax.ShapeDtypeStruct((T, LANES), F32), jax.ShapeDtypeStruct((T, LANES), F32)],
        compiler_params=_cparams("parallel"),
    )(h, gq, gkv)


def _even_in_bwd(h, rq, rkv, gq, gkv, dcqn, dckvn, dqs, dks, dvs, dkr, *, name):
    T = h.shape[0]

    def rms_bwd(c, r, g, dy):
        r = _widen(r, c.shape[-1])
        xr = c * r
        dyg = dy * g
        return r * (dyg - xr * jnp.mean(dyg * xr, -1, keepdims=True)), jnp.sum(dy * xr, 0, keepdims=True)

    def body(h_ref, rq_ref, rkv_ref, gq_ref, gkv_ref, dcq_ref, dckv_ref, dqs_ref, dks_ref, dvs_ref, dkr_ref,
             dh_ref, dgq_ref, dgkv_ref):
        @pl.when(pl.program_id(0) == 0)
        def _():
            dgq_ref[...] = jnp.zeros_like(dgq_ref)
            dgkv_ref[...] = jnp.zeros_like(dgkv_ref)

        dcq, dgq = rms_bwd(h_ref[:, EV_CQ[0]:EV_CQ[1]], rq_ref[...], gq_ref[...], dcq_ref[...])
        dckv, dgkv = rms_bwd(h_ref[:, EV_CKV[0]:EV_CKV[1]], rkv_ref[...], gkv_ref[...], dckv_ref[...])
        dgq_ref[...] += dgq
        dgkv_ref[...] += dgkv
        dh_ref[:, EV_QS[0]:EV_QS[1]] = dqs_ref[...]
        dh_ref[:, EV_CQ[0]:EV_CQ[1]] = dcq.astype(BF16)
        dh_ref[:, EV_CKV[0]:EV_CKV[1]] = dckv.astype(BF16)
        dh_ref[:, EV_KS[0]:EV_KS[1]] = dks_ref[...]
        dh_ref[:, EV_VS[0]:EV_VS[1]] = dvs_ref[...]
        dh_ref[:, EV_KR[0]:EV_KR[1]] = dkr_ref[...]

    return pl.pallas_call(
        body, name=name, grid=(T // ROW_TILE,),
        in_specs=[_row_spec(EVEN_IN_PAD), _row_spec(LANES), _row_spec(LANES), _full_spec((1, MLA_Q_LORA)),
                  _full_spec((1, MLA_KV_LORA)), _row_spec(MLA_Q_LORA), _row_spec(MLA_KV_LORA),
                  _row_spec(SWA_HEADS * HEAD_DIM), _row_spec(LANES), _row_spec(LANES), _row_spec(LANES)],
        out_specs=[_row_spec(EVEN_IN_PAD), _full_spec((1, MLA_Q_LORA)), _full_spec((1, MLA_KV_LORA))],
        out_shape=[jax.ShapeDtypeStruct((T, EVEN_IN_PAD), BF16), jax.ShapeDtypeStruct((1, MLA_Q_LORA), F32),
                   jax.ShapeDtypeStruct((1, MLA_KV_LORA), F32)],
        compiler_params=_cparams("arbitrary"),
    )(h, rq, rkv, gq, gkv, dcqn, dckvn, dqs, dks, dvs, dkr)


def _fox_decay_fwd(f3, bf, *, name):
    B, S, _ = f3.shape

    def body(f_ref, b_ref, csh_ref, chs_ref):
        x = f_ref[...] + b_ref[...]
        c = jnp.minimum(x, 0.0) - jnp.log1p(jnp.exp(-jnp.abs(x)))
        row = lax.broadcasted_iota(jnp.int32, (S, LANES), 0)
        k = 1
        while k < S:
            c = c + jnp.where(row >= k, pltpu.roll(c, k, 0), 0.0)
            k *= 2
        csh_ref[...] = c
        chs_ref[...] = c.T

    return pl.pallas_call(
        body, name=name, grid=(B,),
        in_specs=[pl.BlockSpec((None, S, LANES), lambda b: (b, 0, 0)), pl.BlockSpec((1, LANES), lambda b: (0, 0))],
        out_specs=[pl.BlockSpec((None, S, LANES), lambda b: (b, 0, 0)),
                   pl.BlockSpec((None, LANES, S), lambda b: (b, 0, 0))],
        out_shape=[jax.ShapeDtypeStruct((B, S, LANES), F32), jax.ShapeDtypeStruct((B, LANES, S), F32)],
        compiler_params=_cparams("parallel"),
    )(f3, bf)


def _fox_decay_bwd(dc_hs, f3, bf, *, name):
    B, S, _ = f3.shape

    def body(dc_ref, f_ref, b_ref, df_ref, db_ref):
        g = dc_ref[...].T
        row = lax.broadcasted_iota(jnp.int32, (S, LANES), 0)
        k = 1
        while k < S:
            g = g + jnp.where(row < S - k, pltpu.roll(g, S - k, 0), 0.0)
            k *= 2
        x = f_ref[...] + b_ref[...]
        df = g * (1.0 / (1.0 + jnp.exp(x)))
        df_ref[...] = df.astype(BF16)

        @pl.when(pl.program_id(0) == 0)
        def _():
            db_ref[...] = jnp.zeros_like(db_ref)

        db_ref[...] += jnp.sum(df, 0, keepdims=True)

    return pl.pallas_call(
        body, name=name, grid=(B,),
        in_specs=[pl.BlockSpec((None, LANES, S), lambda b: (b, 0, 0)),
                  pl.BlockSpec((None, S, LANES), lambda b: (b, 0, 0)), pl.BlockSpec((1, LANES), lambda b: (0, 0))],
        out_specs=[pl.BlockSpec((None, S, LANES), lambda b: (b, 0, 0)), pl.BlockSpec((1, LANES), lambda b: (0, 0))],
        out_shape=[jax.ShapeDtypeStruct((B, S, LANES), BF16), jax.ShapeDtypeStruct((1, LANES), F32)],
        compiler_params=_cparams("arbitrary"),
    )(dc_hs, f3, bf)


def _head_column(block, h):
    lane = lax.broadcasted_iota(jnp.int32, block.shape, 1)
    return jnp.sum(jnp.where(lane == h, block, 0.0), axis=-1, keepdims=True)


def _causal_mask(s):
    r = lax.broadcasted_iota(jnp.int32, s.shape, 0)
    c = lax.broadcasted_iota(jnp.int32, s.shape, 1)
    return jnp.where(c <= r, s, NEG_INF)


def _low_half(shape):
    return (lax.broadcasted_iota(jnp.int32, shape, 1) % LANES) < HEAD_DIM


def _widen(x, cols):
    return jnp.concatenate([x] * (cols // LANES), axis=1)


def _both_halves(x, lo):
    r = pltpu.roll(x, HEAD_DIM, 1)
    return jnp.where(lo, x, r), jnp.where(lo, r, x)


MESH_ID = pl.DeviceIdType.MESH
HBM_SPEC = pl.BlockSpec(memory_space=pltpu.HBM)
VMEM_SPEC = pl.BlockSpec(memory_space=pltpu.VMEM)


def _mesh_place():
    x, y, c = lax.axis_index("x"), lax.axis_index("y"), lax.axis_index("c")
    return x, y, c, 4 * x + 2 * y + c


def _peers(x, y, c):
    out = []
    for mask in range(1, N_DEV):
        dx, dy, dc = (mask >> 2) & 1, (mask >> 1) & 1, mask & 1
        px, py, pc = (1 - x if dx else x), (1 - y if dy else y), (1 - c if dc else c)
        out.append(((px, py, pc), 4 * px + 2 * py + pc))
    return out


def _comm_out_shapes(comm):
    return [jax.ShapeDtypeStruct(a.shape if kind == "scatter" else (N_DEV,) + a.shape[1:], a.dtype) for kind, a in comm]


def _comm_scratch(comm):
    n = len(comm)
    return [pltpu.SemaphoreType.DMA((n, 7)), pltpu.SemaphoreType.DMA((n, 7)), pltpu.SemaphoreType.DMA((n,))]


def _comm_copies(kinds, in_refs, out_refs, sems, place):
    send_sems, recv_sems, local_sems = sems
    x, y, c, me = place
    local, remote = [], []
    for w, kind in enumerate(kinds):
        mine = in_refs[w].at[me] if kind == "scatter" else in_refs[w].at[kind[1]]
        local.append(pltpu.make_async_copy(mine, out_refs[w].at[me], local_sems.at[w]))
        for k, (peer, peer_idx) in enumerate(_peers(x, y, c)):
            remote.append(pltpu.make_async_remote_copy(
                src_ref=in_refs[w].at[peer_idx] if kind == "scatter" else mine, dst_ref=out_refs[w].at[me],
                send_sem=send_sems.at[w, k], recv_sem=recv_sems.at[w, k], device_id=peer, device_id_type=MESH_ID))
    return local, remote


def _comm_start(kinds, in_refs, out_refs, sems, place):
    local, remote = _comm_copies(kinds, in_refs, out_refs, sems, place)
    for cp in local + remote:
        cp.start()


def _comm_wait(kinds, in_refs, out_refs, sems, place):
    local, remote = _comm_copies(kinds, in_refs, out_refs, sems, place)
    for cp in remote:
        cp.wait_recv()
    for cp in remote:
        cp.wait_send()
    for cp in local:
        cp.wait()


def _exchange(comm, *, name):
    n = len(comm)
    kinds = [k for k, _ in comm]

    def body(*refs):
        place = _mesh_place()
        _comm_start(kinds, refs[:n], refs[n:2 * n], refs[2 * n:], place)
        _comm_wait(kinds, refs[:n], refs[n:2 * n], refs[2 * n:], place)

    return pl.pallas_call(
        body, name=name, out_shape=_comm_out_shapes(comm), in_specs=[HBM_SPEC] * n, out_specs=[HBM_SPEC] * n,
        scratch_shapes=_comm_scratch(comm),
    )(*[a for _, a in comm])


def _flash_fwd(qa, ka, va, *, q_blk0, k_blk0, v_blk0, W, n_pairs, B, S, scale, csh=None, crow=None, comm=(), name):
    t = ATT_TILE
    nq = S // t
    P = PAIRS_PER_STEP_FWD
    decay = csh is not None
    split = W == LANES
    assert n_pairs % P == 0 and q_blk0 % P == 0 and k_blk0 % P == 0 and v_blk0 % P == 0
    n_c, kinds = len(comm), [k for k, _ in comm]
    n_in = 5 if decay else 3
    fold_scale = math.log2(scale).is_integer()
    n_steps = (B, n_pairs // P, nq)

    def body(*refs):
        c_in, c_out = refs[n_in:n_in + n_c], refs[n_in + n_c + 2:n_in + 2 * n_c + 2]
        sems = refs[n_in + 2 * n_c + 4:]
        refs = refs[:n_in] + refs[n_in + n_c:n_in + n_c + 2] + refs[n_in + 2 * n_c + 2:n_in + 2 * n_c + 4]
        if decay:
            q_ref, k_ref, v_ref, csh_ref, crow_ref, o_ref, lse_ref, m_s, acc_s = refs
        else:
            q_ref, k_ref, v_ref, o_ref, lse_ref, m_s, acc_s = refs
        g, i = pl.program_id(1), pl.program_id(2)
        if n_c:
            place = _mesh_place()
            ids = [pl.program_id(ax) for ax in range(3)]

            @pl.when((ids[0] == 0) & (ids[1] == 0) & (ids[2] == 0))
            def _():
                _comm_start(kinds, c_in, c_out, sems, place)

        lo = _low_half((t, LANES))
        qv = q_ref[...]
        qh = []
        for pr in range(P):
            qp = qv[:, pr * W:(pr + 1) * W]
            qh += [jnp.where(lo, qp, jnp.zeros_like(qp)), jnp.where(lo, jnp.zeros_like(qp), qp)] if split \
                else [qp[:, :LANES], qp[:, LANES:]]
        if fold_scale:
            qh = [x * scale for x in qh]
        if decay:
            cq = [jnp.broadcast_to(_head_column(csh_ref[...], 2 * P * g + hd), (t, LANES)) for hd in range(2 * P)]
        m_s[...] = jnp.full(m_s.shape, NEG_INF, F32)
        acc_s[...] = jnp.zeros(acc_s.shape, F32)

        def step(j, masked):
            rows = pl.ds(pl.multiple_of(j * t, t), t)
            kb, vb = k_ref[rows, :], v_ref[rows, :]
            for pr in range(P):
                kp, vp = kb[:, pr * W:(pr + 1) * W], vb[:, pr * LANES:(pr + 1) * LANES]
                ones = jnp.ones_like(vp)
                vaug = [jnp.where(lo, vp, ones), jnp.where(lo, ones, vp)]
                for half in range(2):
                    hd = 2 * pr + half
                    kh = kp if split else kp[:, half * LANES:(half + 1) * LANES]
                    s = lax.dot_general(qh[hd], kh, NT, preferred_element_type=F32)
                    if not fold_scale:
                        s = s * scale
                    if decay:
                        s = s + _widen(cq[hd], t) - crow_ref[hd, j]
                    if masked:
                        s = _causal_mask(s)
                    m_prev = m_s[hd]
                    m_new = jnp.maximum(m_prev, jnp.max(s, -1, keepdims=True))
                    p = jnp.exp(s - _widen(m_new, t))
                    acc_s[hd] = jnp.exp(m_prev - m_new) * acc_s[hd] + lax.dot_general(
                        p.astype(BF16), vaug[half], NN, preferred_element_type=F32)
                    m_s[hd] = m_new

        def loop_body(j, carry):
            step(j, False)
            return carry

        lax.fori_loop(0, i, loop_body, 0)
        step(i, True)
        for pr in range(P):
            acc0, acc1 = acc_s[2 * pr], acc_s[2 * pr + 1]
            _, l0 = _both_halves(acc0, lo)
            l1, _ = _both_halves(acc1, lo)
            cols = slice(pr * LANES, (pr + 1) * LANES)
            o_ref[:, cols] = jnp.where(lo, acc0 / l0, acc1 / l1).astype(BF16)
            lse_ref[:, cols] = jnp.where(lo, m_s[2 * pr] + jnp.log(l0), m_s[2 * pr + 1] + jnp.log(l1))
        if n_c:
            @pl.when((ids[0] == n_steps[0] - 1) & (ids[1] == n_steps[1] - 1) & (ids[2] == n_steps[2] - 1))
            def _():
                _comm_wait(kinds, c_in, c_out, sems, place)

    in_specs = [pl.BlockSpec((t, P * W), lambda b, g, i: (b * nq + i, q_blk0 // P + g)),
                pl.BlockSpec((S, P * W), lambda b, g, i: (b, k_blk0 // P + g)),
                pl.BlockSpec((S, P * LANES), lambda b, g, i: (b, v_blk0 // P + g))]
    args = [qa, ka, va]
    if decay:
        in_specs += [pl.BlockSpec((None, t, LANES), lambda b, g, i: (b, i, 0)),
                     pl.BlockSpec((None, 2 * P, nq, 1, t), lambda b, g, i: (b, g, 0, 0, 0))]
        args += [csh, crow]
    out_spec = pl.BlockSpec((t, P * LANES), lambda b, g, i: (b * nq + i, g))
    res = pl.pallas_call(
        body, name=name, grid=n_steps, in_specs=in_specs + [HBM_SPEC] * n_c,
        out_specs=[out_spec, out_spec] + [HBM_SPEC] * n_c,
        out_shape=[jax.ShapeDtypeStruct((B * S, n_pairs * LANES), BF16),
                   jax.ShapeDtypeStruct((B * S, n_pairs * LANES), F32)] + _comm_out_shapes(comm),
        scratch_shapes=[pltpu.VMEM((2 * P, t, LANES), F32), pltpu.VMEM((2 * P, t, LANES), F32)]
        + (_comm_scratch(comm) if n_c else []),
        compiler_params=_cparams(*(("arbitrary",) * 3 if n_c else ("parallel",) * 3)),
    )(*args, *[a for _, a in comm])
    return res[0], res[1], list(res[2:])


def _flash_bwd(qa, ka, va, oa, doa, lsea, *, q_blk0, k_blk0, v_blk0, do_blk0, W, n_pairs, B, S, scale, qk_dtype,
               csh=None, crow=None, comm=(), name):
    t = ATT_TILE_BWD
    nq = S // t
    P = PAIRS_PER_STEP_BWD
    decay = csh is not None
    if decay:
        crow = crow.reshape(B, 2 * n_pairs, nq, 1, t)
    split = W == LANES
    assert n_pairs % P == 0 and q_blk0 % P == 0 and k_blk0 % P == 0 and v_blk0 % P == 0 and do_blk0 % P == 0
    n_c, kinds = len(comm), [k for k, _ in comm]
    n_in, n_out, n_scr = (8, 5, 8) if decay else (6, 3, 5)
    n_steps = (B, n_pairs // P, nq)

    def body(*refs):
        c_in = refs[n_in:n_in + n_c]
        c_out = refs[n_in + n_c + n_out:n_in + 2 * n_c + n_out]
        sems = refs[n_in + 2 * n_c + n_out + n_scr:]
        refs = (refs[:n_in] + refs[n_in + n_c:n_in + n_c + n_out]
                + refs[n_in + 2 * n_c + n_out:n_in + 2 * n_c + n_out + n_scr])
        if n_c:
            place = _mesh_place()
            ids = [pl.program_id(ax) for ax in range(3)]

            @pl.when((ids[0] == 0) & (ids[1] == 0) & (ids[2] == 0))
            def _():
                _comm_start(kinds, c_in, c_out, sems, place)

        if decay:
            (q_ref, k_ref, v_ref, o_ref, do_ref, lse_ref, csh_ref, crow_ref, dq_ref, dk_ref, dv_ref, dck_ref, dcq_ref,
             dq_s, lse_s, delta_s, dk_s, dv_s, cq_s, dcq_s, dck_s) = refs
        else:
            (q_ref, k_ref, v_ref, o_ref, do_ref, lse_ref, dq_ref, dk_ref, dv_ref,
             dq_s, lse_s, delta_s, dk_s, dv_s) = refs
        g, j = pl.program_id(1), pl.program_id(2)
        lo = _low_half((t, LANES))

        @pl.when(j == 0)
        def _():
            lo_s = _low_half((S, LANES))
            dq_s[...] = jnp.zeros(dq_s.shape, F32)
            for pr in range(P):
                cols = slice(pr * LANES, (pr + 1) * LANES)
                lse_s[2 * pr], lse_s[2 * pr + 1] = _both_halves(lse_ref[:, cols], lo_s)
                dd = do_ref[:, cols].astype(F32) * o_ref[:, cols].astype(F32)
                delta_s[2 * pr] = jnp.broadcast_to(jnp.sum(jnp.where(lo_s, dd, 0.0), -1, keepdims=True), (S, LANES))
                delta_s[2 * pr + 1] = jnp.broadcast_to(jnp.sum(jnp.where(lo_s, 0.0, dd), -1, keepdims=True),
                                                       (S, LANES))
            if decay:
                for hd in range(2 * P):
                    cq_s[hd] = jnp.broadcast_to(_head_column(csh_ref[...], 2 * P * g + hd), (S, LANES))
                dcq_s[...] = jnp.zeros(dcq_s.shape, F32)

        kb, vb = k_ref[...], v_ref[...]
        kh, vh = [], []
        for pr in range(P):
            kp, vp = kb[:, pr * W:(pr + 1) * W], vb[:, pr * LANES:(pr + 1) * LANES]
            zk, zv = jnp.zeros_like(kp), jnp.zeros_like(vp)
            kh += [jnp.where(lo, kp, zk), jnp.where(lo, zk, kp)] if split else [kp[:, :LANES], kp[:, LANES:]]
            vh += [jnp.where(lo, vp, zv), jnp.where(lo, zv, vp)]
        dk_s[...] = jnp.zeros(dk_s.shape, F32)
        dv_s[...] = jnp.zeros(dv_s.shape, F32)
        if decay:
            dck_s[...] = jnp.zeros(dck_s.shape, F32)

        def step(i, masked):
            rows = pl.ds(pl.multiple_of(i * t, t), t)
            qi, doi = q_ref[rows, :], do_ref[rows, :]
            for pr in range(P):
                qp, dop = qi[:, pr * W:(pr + 1) * W], doi[:, pr * LANES:(pr + 1) * LANES]
                for half in range(2):
                    hd = 2 * pr + half
                    qx = qp if split else qp[:, half * LANES:(half + 1) * LANES]
                    s = lax.dot_general(qx, kh[hd], NT, preferred_element_type=F32) * scale
                    if decay:
                        s = s + _widen(cq_s[hd, rows, :], t) - crow_ref[hd, j]
                    if masked:
                        s = _causal_mask(s)
                    p = jnp.exp(s - _widen(lse_s[hd, rows, :], t))
                    dv_s[hd] += lax.dot_general(p.astype(BF16), dop, TN, preferred_element_type=F32)
                    dp = lax.dot_general(dop, vh[hd], NT, preferred_element_type=F32)
                    ds = p * (dp - _widen(delta_s[hd, rows, :], t))
                    dss = (ds * scale).astype(BF16)
                    dk_s[hd] += lax.dot_general(dss, qx, TN, preferred_element_type=F32)
                    dqc = lax.dot_general(dss, kh[hd], NN, preferred_element_type=F32)
                    if split:
                        dq_s[rows, pr * W:(pr + 1) * W] += dqc
                    else:
                        dq_s[rows, hd * LANES:(hd + 1) * LANES] += dqc
                    if decay:
                        dck_s[hd] -= jnp.sum(ds, 0, keepdims=True)
                        part = ds[:, :LANES]
                        for c in range(1, t // LANES):
                            part = part + ds[:, c * LANES:(c + 1) * LANES]
                        dcq_s[hd, rows, :] += part

        def loop_body(i, carry):
            step(i, False)
            return carry

        step(j, True)
        lax.fori_loop(j + 1, nq, loop_body, 0)
        for pr in range(P):
            if split:
                dk_ref[:, pr * W:(pr + 1) * W] = jnp.where(lo, dk_s[2 * pr], dk_s[2 * pr + 1]).astype(dk_ref.dtype)
            else:
                for half in range(2):
                    hd = 2 * pr + half
                    dk_ref[:, hd * LANES:(hd + 1) * LANES] = dk_s[hd].astype(dk_ref.dtype)
            dv_ref[:, pr * LANES:(pr + 1) * LANES] = jnp.where(lo, dv_s[2 * pr], dv_s[2 * pr + 1]).astype(BF16)
        if decay:
            dck_ref[...] = dck_s[...]

        @pl.when(j == nq - 1)
        def _():
            dq_ref[...] = dq_s[...].astype(dq_ref.dtype)
            if decay:
                for hd in range(2 * P):
                    dcq_ref[hd] = jnp.sum(dcq_s[hd].T, 0, keepdims=True)

        if n_c:
            @pl.when((ids[0] == n_steps[0] - 1) & (ids[1] == n_steps[1] - 1) & (ids[2] == n_steps[2] - 1))
            def _():
                _comm_wait(kinds, c_in, c_out, sems, place)

    full = lambda w, blk0: pl.BlockSpec((S, P * w), lambda b, g, j: (b, blk0 // P + g))
    blk = lambda w, blk0: pl.BlockSpec((t, P * w), lambda b, g, j: (b * nq + j, blk0 // P + g))
    in_specs = [full(W, q_blk0), blk(W, k_blk0), blk(LANES, v_blk0), full(LANES, 0), full(LANES, do_blk0),
                full(LANES, 0)]
    args = [qa, ka, va, oa, doa, lsea]
    T = B * S
    out_specs = [full(W, 0), blk(W, 0), blk(LANES, 0)]
    out_shape = [jax.ShapeDtypeStruct((T, n_pairs * W), qk_dtype), jax.ShapeDtypeStruct((T, n_pairs * W), qk_dtype),
                 jax.ShapeDtypeStruct((T, n_pairs * LANES), BF16)]
    per_head = lambda rows: pltpu.VMEM((2 * P, rows, LANES), F32)
    scratch = [pltpu.VMEM((S, P * W), F32), per_head(S), per_head(S), per_head(t), per_head(t)]
    if decay:
        in_specs += [pl.BlockSpec((None, S, LANES), lambda b, g, j: (b, 0, 0)),
                     pl.BlockSpec((None, 2 * P, nq, 1, t), lambda b, g, j: (b, g, 0, 0, 0))]
        args += [csh, crow]
        out_specs += [pl.BlockSpec((None, 2 * P, None, 1, t), lambda b, g, j: (b, g, j, 0, 0)),
                      pl.BlockSpec((None, 2 * P, 1, S), lambda b, g, j: (b, g, 0, 0))]
        out_shape += [jax.ShapeDtypeStruct((B, 2 * n_pairs, nq, 1, t), F32),
                      jax.ShapeDtypeStruct((B, 2 * n_pairs, 1, S), F32)]
        scratch += [per_head(S), per_head(S), pltpu.VMEM((2 * P, 1, t), F32)]
    res = pl.pallas_call(
        body, name=name, grid=n_steps, in_specs=in_specs + [HBM_SPEC] * n_c,
        out_specs=out_specs + [HBM_SPEC] * n_c, out_shape=out_shape + _comm_out_shapes(comm),
        scratch_shapes=scratch + (_comm_scratch(comm) if n_c else []),
        compiler_params=_cparams(*(("arbitrary",) * 3 if n_c else ("parallel", "parallel", "arbitrary"))),
    )(*args, *[a for _, a in comm])
    return tuple(res[:n_out]) + (list(res[n_out:]),)


def _swa_common(q_ref, kp_ref, ko_ref, vp_ref, vo_ref, n):
    Q = BLOCK_Q
    lo = _low_half((Q, LANES))
    lo2 = _low_half((2 * Q, LANES))
    kk = jnp.concatenate([kp_ref[...], ko_ref[...]], axis=0)
    vv = jnp.concatenate([vp_ref[...], vo_ref[...]], axis=0)
    kdup = [x.astype(BF16) for x in _both_halves(kk, lo2)]
    vdup = [x.astype(BF16) for x in _both_halves(vv, lo2)]
    a = lax.broadcasted_iota(jnp.int32, (SWA_GROUP * Q, 2 * Q), 0) % Q
    col = lax.broadcasted_iota(jnp.int32, (SWA_GROUP * Q, 2 * Q), 1)
    dist = a + Q - col
    valid = (dist >= 0) & (dist < SWA_WINDOW) & ((col >= Q) | (n > 0))
    qv = q_ref[...]
    qm = []
    for a_head in range(SWA_HEADS):
        qp = qv[:, (a_head // 2) * LANES:(a_head // 2 + 1) * LANES]
        keep = lo if a_head % 2 == 0 else jnp.logical_not(lo)
        qm.append(jnp.where(keep, qp, 0.0).astype(BF16))
    qs = [jnp.concatenate(qm[g * SWA_GROUP:(g + 1) * SWA_GROUP], axis=0) for g in range(SWA_KV_HEADS)]
    return lo, lo2, kdup, vdup, valid, qs


def _swa_group_logits(g, qs, kdup, valid, bias_ref):
    heads = slice(g * SWA_GROUP, (g + 1) * SWA_GROUP)
    s = lax.dot_general(qs[g], kdup[g], NT, preferred_element_type=F32) * (HEAD_DIM ** -0.5)
    s = s + bias_ref[heads].reshape(SWA_GROUP * BLOCK_Q, 2 * BLOCK_Q)
    return heads, jnp.where(valid, s, NEG_INF)


def _pair_halves(x, lo):
    Q = BLOCK_Q
    return [jnp.where(lo, x[2 * pr * Q:(2 * pr + 1) * Q], x[(2 * pr + 1) * Q:(2 * pr + 2) * Q])
            for pr in range(SWA_GROUP // 2)]


def _swa_in_specs(nb):
    Q = BLOCK_Q
    own = lambda blk: (lambda b, n: (b * nb + n, blk))
    prev = lambda blk: (lambda b, n: (b * nb + jnp.maximum(n - 1, 0), blk))
    kb, vb = EV_KS[0] // LANES, EV_VS[0] // LANES
    return [pl.BlockSpec((Q, SWA_HEADS * HEAD_DIM), own(0)), pl.BlockSpec((Q, LANES), prev(kb)),
            pl.BlockSpec((Q, LANES), own(kb)), pl.BlockSpec((Q, LANES), prev(vb)), pl.BlockSpec((Q, LANES), own(vb))]


def _swa_fwd(h, bias, sinkcol, *, B, S, comm=(), name):
    Q = BLOCK_Q
    nb = S // Q
    n_c, kinds = len(comm), [k for k, _ in comm]

    def body(*refs):
        c_in, c_out, sems = refs[7:7 + n_c], refs[9 + n_c:9 + 2 * n_c], refs[9 + 2 * n_c:]
        q_ref, kp_ref, ko_ref, vp_ref, vo_ref, bias_ref, sink_ref = refs[:7]
        o_ref, lse_ref = refs[7 + n_c:9 + n_c]
        if n_c:
            place = _mesh_place()
            ids = [pl.program_id(0), pl.program_id(1)]

            @pl.when((ids[0] == 0) & (ids[1] == 0))
            def _():
                _comm_start(kinds, c_in, c_out, sems, place)

        lo, lo2, kdup, vdup, valid, qs = _swa_common(q_ref, kp_ref, ko_ref, vp_ref, vo_ref, pl.program_id(1))
        lane = lax.broadcasted_iota(jnp.int32, (Q, LANES), 1)
        lse_blk = jnp.zeros((Q, LANES), F32)
        pairs = []
        lo4 = _low_half((SWA_GROUP * Q, LANES))
        for g in range(SWA_KV_HEADS):
            heads, s = _swa_group_logits(g, qs, kdup, valid, bias_ref)
            sink = jnp.broadcast_to(sink_ref[heads].reshape(SWA_GROUP * Q, 1), (SWA_GROUP * Q, LANES))
            m = jnp.maximum(jnp.max(s, -1, keepdims=True), sink)
            p = jnp.exp(s - _widen(m, 2 * Q))
            vaug = jnp.where(lo2, vdup[g], jnp.ones_like(vdup[g]))
            pv = lax.dot_general(p.astype(BF16), vaug, NN, preferred_element_type=F32)
            rolled = pltpu.roll(pv, HEAD_DIM, 1)
            l = jnp.where(lo4, rolled, pv) + jnp.exp(sink - m)
            out = pv / l
            lse_g = m + jnp.log(l)
            for i in range(SWA_GROUP):
                lse_blk = jnp.where(lane == g * SWA_GROUP + i, lse_g[i * Q:(i + 1) * Q], lse_blk)
            shifted = pltpu.roll(out, HEAD_DIM, 1)
            pairs += [jnp.where(lo, out[2 * pr * Q:(2 * pr + 1) * Q], shifted[(2 * pr + 1) * Q:(2 * pr + 2) * Q])
                      for pr in range(SWA_GROUP // 2)]
        o_ref[...] = jnp.concatenate(pairs, axis=1).astype(BF16)
        lse_ref[...] = lse_blk
        if n_c:
            @pl.when((ids[0] == B - 1) & (ids[1] == nb - 1))
            def _():
                _comm_wait(kinds, c_in, c_out, sems, place)

    whole = lambda shape: pl.BlockSpec(shape, lambda b, n: (0,) * len(shape))
    res = pl.pallas_call(
        body, name=name, grid=(B, nb),
        in_specs=_swa_in_specs(nb) + [whole((SWA_HEADS, Q, 2 * Q)), whole((SWA_HEADS, Q, 1))] + [HBM_SPEC] * n_c,
        out_specs=[pl.BlockSpec((Q, SWA_HEADS * HEAD_DIM), lambda b, n: (b * nb + n, 0)),
                   pl.BlockSpec((Q, LANES), lambda b, n: (b * nb + n, 0))] + [HBM_SPEC] * n_c,
        out_shape=[jax.ShapeDtypeStruct((B * S, SWA_HEADS * HEAD_DIM), BF16),
                   jax.ShapeDtypeStruct((B * S, LANES), F32)] + _comm_out_shapes(comm),
        scratch_shapes=_comm_scratch(comm) if n_c else [],
        compiler_params=_cparams(*(("arbitrary",) * 2 if n_c else ("parallel",) * 2)),
    )(h, h, h, h, h, bias, sinkcol, *[a for _, a in comm])
    return res[0], res[1], list(res[2:])


def _swa_bwd(h, o, do, lse, bias, sinkcol, *, do_blk0, B, S, name):
    Q = BLOCK_Q
    nb = S // Q
    scale = HEAD_DIM ** -0.5

    def body(q_ref, kp_ref, ko_ref, vp_ref, vo_ref, o_ref, do_ref, lse_ref, bias_ref, sink_ref,
             dq_ref, dko_ref, dkp_ref, dvo_ref, dvp_ref, dbias_ref, dsink_ref):
        @pl.when((pl.program_id(0) == 0) & (pl.program_id(1) == 0))
        def _():
            dbias_ref[...] = jnp.zeros_like(dbias_ref)
            dsink_ref[...] = jnp.zeros_like(dsink_ref)

        lo, lo2, kdup, vdup, valid, qs = _swa_common(q_ref, kp_ref, ko_ref, vp_ref, vo_ref, pl.program_id(1))
        lse_blk = lse_ref[...]
        dkk, dvv, dq_pairs = [], [], []
        for g in range(SWA_KV_HEADS):
            heads, s = _swa_group_logits(g, qs, kdup, valid, bias_ref)
            lse_g = jnp.concatenate([_head_column(lse_blk, g * SWA_GROUP + i) for i in range(SWA_GROUP)], axis=0)
            p = jnp.exp(s - lse_g)
            do_g, o_g = [], []
            for i in range(SWA_GROUP):
                cols = slice((g * SWA_GROUP + i) // 2 * LANES, ((g * SWA_GROUP + i) // 2 + 1) * LANES)
                do_p = do_ref[:, cols]
                do_g.append(jnp.where(lo if i % 2 == 0 else jnp.logical_not(lo), do_p, jnp.zeros_like(do_p)))
                o_g.append(o_ref[:, cols])
            doh, oh = jnp.concatenate(do_g, axis=0), jnp.concatenate(o_g, axis=0)
            delta = jnp.sum(doh.astype(F32) * oh.astype(F32), -1, keepdims=True)
            dp = lax.dot_general(doh, vdup[g], NT, preferred_element_type=F32)
            ds = p * (dp - delta)
            dbias_ref[heads] += ds.reshape(SWA_GROUP, Q, 2 * Q)
            dsink_ref[heads] -= (jnp.exp(sink_ref[heads].reshape(SWA_GROUP * Q, 1) - lse_g)
                                 * delta).reshape(SWA_GROUP, Q, 1)
            dss = (ds * scale).astype(BF16)
            dq_pairs += _pair_halves(lax.dot_general(dss, kdup[g], NN, preferred_element_type=F32), lo)
            dkk.append(lax.dot_general(dss, qs[g], TN, preferred_element_type=F32))
            dvv.append(lax.dot_general(p.astype(BF16), doh, TN, preferred_element_type=F32))
        dq_ref[...] = jnp.concatenate(dq_pairs, axis=1).astype(BF16)
        fold = lambda x: x + pltpu.roll(x, HEAD_DIM, 1)
        dk_blk = jnp.where(lo2, fold(dkk[0]), fold(dkk[1]))
        dv_blk = jnp.where(lo2, fold(dvv[0]), fold(dvv[1]))
        dkp_ref[...] = dk_blk[:Q]
        dko_ref[...] = dk_blk[Q:]
        dvp_ref[...] = dv_blk[:Q]
        dvo_ref[...] = dv_blk[Q:]

    whole = lambda shape: pl.BlockSpec(shape, lambda b, n: (0,) * len(shape))
    wide = lambda blk: pl.BlockSpec((Q, SWA_HEADS * HEAD_DIM), lambda b, n: (b * nb + n, blk))
    narrow = pl.BlockSpec((Q, LANES), lambda b, n: (b * nb + n, 0))
    kv_shape = jax.ShapeDtypeStruct((B * S, LANES), F32)
    return pl.pallas_call(
        body, name=name, grid=(B, nb),
        in_specs=_swa_in_specs(nb) + [wide(0), wide(do_blk0), narrow, whole((SWA_HEADS, Q, 2 * Q)),
                                      whole((SWA_HEADS, Q, 1))],
        out_specs=[wide(0), narrow, narrow, narrow, narrow, whole((SWA_HEADS, Q, 2 * Q)), whole((SWA_HEADS, Q, 1))],
        out_shape=[jax.ShapeDtypeStruct((B * S, SWA_HEADS * HEAD_DIM), BF16), kv_shape, kv_shape, kv_shape, kv_shape,
                   jax.ShapeDtypeStruct((SWA_HEADS, Q, 2 * Q), F32), jax.ShapeDtypeStruct((SWA_HEADS, Q, 1), F32)],
        compiler_params=_cparams("arbitrary", "arbitrary"),
    )(h, h, h, h, h, o, do, lse, bias, sinkcol)


def _bias_bucket_sum(dbias, bucket, *, name):
    def body(d_ref, b_ref, o_ref):
        dbv, bk = d_ref[...], b_ref[...]
        lane = lax.broadcasted_iota(jnp.int32, (SWA_HEADS, LANES), 1)
        out = jnp.zeros((SWA_HEADS, LANES), F32)
        for b in range(REL_BUCKETS):
            part = jnp.sum(jnp.where(bk == b, dbv, 0.0), axis=1)
            tot = jnp.sum(part, axis=-1, keepdims=True)
            out = out + jnp.where(lane == b, tot, 0.0)
        o_ref[...] = out

    return pl.pallas_call(
        body, name=name, out_shape=jax.ShapeDtypeStruct((SWA_HEADS, LANES), F32),
        compiler_params=pltpu.CompilerParams(vmem_limit_bytes=VMEM_LIMIT_BYTES),
    )(dbias, bucket)


def _adamw_update(w, g, m, v):
    m_new = ADAM_B1 * m + (1.0 - ADAM_B1) * g
    v_new = ADAM_B2 * v + (1.0 - ADAM_B2) * jnp.square(g)
    m_hat = m_new / (1.0 - ADAM_B1 ** ADAM_STEP)
    v_hat = v_new / (1.0 - ADAM_B2 ** ADAM_STEP)
    return -ADAM_LR * (m_hat / (jnp.sqrt(v_hat) + ADAM_EPS) + ADAM_WD * w), m_new, v_new


def _adamw(w, g, m, v, *, name):
    def body(w_ref, g_ref, m_ref, v_ref, d_ref, nm_ref, nv_ref):
        d_ref[...], nm_ref[...], nv_ref[...] = _adamw_update(w_ref[...], g_ref[...], m_ref[...], v_ref[...])

    return pl.pallas_call(
        body, name=name, out_shape=[jax.ShapeDtypeStruct(w.shape, F32)] * 3,
        compiler_params=pltpu.CompilerParams(vmem_limit_bytes=VMEM_LIMIT_BYTES),
    )(w, g, m, v)


ADAMW_PARTS_BYTES = 8 * 1024 * 1024


def _adamw_slots(w, parts, m, v, *, name):
    n0, R, C = w.shape
    tr = next((c for c in (512, 256, 128, 64, 32, 16) if R % c == 0 and 4 * n0 * N_DEV * c * C <= ADAMW_PARTS_BYTES), 8)

    def body(*refs):
        w_ref, p_refs, (m_ref, v_ref, g_ref, d_ref, nm_ref, nv_ref) = refs[0], refs[1:1 + n0], refs[1 + n0:]
        layer = pl.program_id(0)
        for l in range(n0):
            @pl.when(layer == l)
            def _(p_ref=p_refs[l]):
                g = p_ref[0].astype(F32)
                for j in range(1, N_DEV):
                    g = g + p_ref[j].astype(F32)
                g_ref[...] = g
                d_ref[...], nm_ref[...], nv_ref[...] = _adamw_update(w_ref[...], g, m_ref[...], v_ref[...])

    spec = pl.BlockSpec((None, tr, C), lambda l, i: (l, i, 0))
    part_spec = lambda own: pl.BlockSpec((N_DEV, tr, C), lambda l, i: (0, jnp.where(l == own, i, 0), 0))
    return pl.pallas_call(
        body, name=name, grid=(n0, R // tr),
        in_specs=[spec] + [part_spec(l) for l in range(n0)] + [spec, spec], out_specs=[spec] * 4,
        out_shape=[jax.ShapeDtypeStruct((n0, R, C), F32)] * 4, compiler_params=_cparams("arbitrary", "arbitrary"),
    )(w, *parts, m, v)


def _all_gather_hbm(blocks, *, name):
    n = len(blocks)

    def body(*refs):
        x_refs, out_refs = refs[:n], refs[n:2 * n]
        send_sems, recv_sems, local_sems = refs[2 * n:]
        x, y, c, _ = _mesh_place()
        me, sibling = (x, y, c), (x, y, 1 - c)
        chips = [(1 - x, y), (x, 1 - y), (1 - x, 1 - y)]

        def copy(w, k, blk, to, src=None):
            px, py, pc = blk
            slot = out_refs[w].at[4 * px + 2 * py + pc]
            return pltpu.make_async_remote_copy(
                src_ref=slot if src is None else src, dst_ref=slot,
                send_sem=send_sems.at[w, k], recv_sem=recv_sems.at[w, k], device_id=to, device_id_type=MESH_ID)

        mine = [pltpu.make_async_copy(x_refs[w], out_refs[w].at[4 * x + 2 * y + c], local_sems.at[w])
                for w in range(n)]
        for cp in mine:
            cp.start()
        first = []
        for w in range(n):
            first.append(copy(w, 0, me, sibling, src=x_refs[w]))
            first += [copy(w, 1 + j, me, (*chip, c), src=x_refs[w]) for j, chip in enumerate(chips)]
        for cp in first:
            cp.start()
        passed = []
        for j, chip in enumerate(chips):
            for w in range(n):
                copy(w, 1 + j, (*chip, c), me).wait_recv()
                fwd = copy(w, 4 + j, (*chip, c), sibling)
                fwd.start()
                passed.append(fwd)
        for w in range(n):
            copy(w, 0, sibling, me).wait_recv()
            for j, chip in enumerate(chips):
                copy(w, 4 + j, (*chip, 1 - c), me).wait_recv()
        for cp in first + passed:
            cp.wait_send()
        for cp in mine:
            cp.wait()

    return pl.pallas_call(
        body, name=name, out_shape=[jax.ShapeDtypeStruct((N_DEV,) + b.shape, b.dtype) for b in blocks],
        in_specs=[HBM_SPEC] * n, out_specs=[HBM_SPEC] * n,
        scratch_shapes=[pltpu.SemaphoreType.DMA((n, 7)), pltpu.SemaphoreType.DMA((n, 7)),
                        pltpu.SemaphoreType.DMA((n,))],
    )(*blocks)


def _all_reduce_small(block, *, name):
    R, W = block.shape

    def body(x_ref, out_ref, buf, send_sems, recv_sems):
        x, y, c, me = _mesh_place()
        copies = []
        for k, (peer, _) in enumerate(_peers(x, y, c)):
            copies.append(pltpu.make_async_remote_copy(
                src_ref=x_ref, dst_ref=buf.at[me], send_sem=send_sems.at[k], recv_sem=recv_sems.at[k],
                device_id=peer, device_id_type=MESH_ID))
        for cp in copies:
            cp.start()
        buf[me] = x_ref[...]
        for cp in copies:
            cp.wait_recv()
        for cp in copies:
            cp.wait_send()
        acc = buf[0]
        for j in range(1, N_DEV):
            acc = acc + buf[j]
        out_ref[...] = acc

    return pl.pallas_call(
        body, name=name, out_shape=jax.ShapeDtypeStruct((R, W), F32),
        in_specs=[VMEM_SPEC], out_specs=VMEM_SPEC,
        scratch_shapes=[pltpu.VMEM((N_DEV, R, W), F32), pltpu.SemaphoreType.DMA((7,)), pltpu.SemaphoreType.DMA((7,))],
    )(block)


def _assemble(name, g):
    if BIG_AXIS[name] == 2:
        return jnp.concatenate([g[j] for j in range(N_DEV)], axis=1)
    return g.reshape(N_DEV * g.shape[1], g.shape[2])


def _split_for_devices(name, g):
    if BIG_AXIS[name] == 2:
        b = g.shape[1] // N_DEV
        return jnp.stack([g[:, j * b:(j + 1) * b] for j in range(N_DEV)]).astype(BF16)
    return g.reshape(N_DEV, g.shape[0] // N_DEV, g.shape[1]).astype(BF16)


def _layer_weight_keys(i):
    j = i // 2
    mixer = [('ev_w_in', j), ('ev_w_uq', j), ('ev_w_ukv', j), ('ev_w_out', j)] if i % 2 == 0 \
        else [('od_w_in', j), ('od_w_out', j)]
    return mixer + [('w_up', i), ('w_down', i), ('ple_w_proj', i), ('ple_w_gate', i)]


def _weight_layer(key):
    name, idx = key
    return 2 * idx if name.startswith('ev_') else 2 * idx + 1 if name.startswith('od_') else idx


FIRST_GATHER = [('ev_w_in', 0), ('ev_w_uq', 0), ('ev_w_ukv', 0), ('ev_w_out', 0)]
FWD_CARRIERS = {
    'l0_mla': [('w_up', 0), ('ple_w_proj', 0), ('ple_w_gate', 0)],
    'l0_swa': [('w_down', 0)],
    'l0_out_ln1': [('od_w_out', 0)],
    'l0_up': [('od_w_in', 0)],
    'l0_down_ln2': [('w_up', 1)],
    'l0_ple_gate': [('ple_w_proj', 1), ('ple_w_gate', 1)],
    'l1_fox': [('w_down', 1), ('ev_w_in', 1), ('ev_w_uq', 1), ('ev_w_ukv', 1), ('ev_w_out', 1), ('w_up', 2)],
    'l1_up': [('w_down', 2)],
    'l1_down_ln2': [('ple_w_proj', 2), ('ple_w_gate', 2)],
    'l2_mla': [('od_w_in', 1), ('od_w_out', 1)],
    'l2_swa': [('w_up', 3)],
    'l2_up': [('w_down', 3)],
    'l2_down_ln2': [('ple_w_proj', 3), ('ple_w_gate', 3)],
}


class _MeshExchange:
    def __init__(self, shards):
        self.shards = shards
        self.weights = {i: {} for i in range(DEPTH)}
        self.pending = []
        self.in_flight = []
        self.received = {}
        got = _all_gather_hbm([self.shards[n][idx] for n, idx in FIRST_GATHER], name="gather_first")
        self._landed(FIRST_GATHER, got)

    def _landed(self, keys, gathered):
        for k, g in zip(keys, gathered):
            self.weights[_weight_layer(k)][k[0]] = _assemble(k[0], g)

    def layer_weights(self, i):
        return self.weights[i]

    def carry(self, kernel_name):
        return [(("gather", idx), self.shards[n]) for n, idx in FWD_CARRIERS.get(kernel_name, [])]

    def carried(self, kernel_name, outs):
        self._landed(FWD_CARRIERS.get(kernel_name, []), outs)

    def push_grads(self, grads):
        self.pending += [(k, _split_for_devices(k[0], g)) for k, g in grads.items()]

    def bwd_items(self):
        self.in_flight, self.pending = self.pending, []
        return [("scatter", parts) for _, parts in self.in_flight]

    def bwd_done(self, outs):
        for (k, _), got in zip(self.in_flight, outs):
            self.received[k] = got
        self.in_flight = []

    def finish(self):
        if self.pending:
            outs = _exchange(self.bwd_items(), name="scatter_rest")
            self.bwd_done(outs)
        return self.received


PACK_ROWS = 8


def _pack_small(vals):
    flat = jnp.concatenate([vals[n].reshape(-1).astype(F32) for n in SMALL])
    pad = (-flat.shape[0]) % (PACK_ROWS * LANES)
    return jnp.pad(flat, (0, pad)).reshape(-1, LANES)


def _unpack_small(block, shapes):
    flat = block.reshape(-1)
    out, off = {}, 0
    for n in SMALL:
        sz = math.prod(shapes[n])
        out[n] = flat[off:off + sz].reshape(shapes[n])
        off += sz
    return out


def _rope_tables(S):
    half = MLA_ROPE // 2
    inv = 1.0 / (ROPE_THETA ** (jnp.arange(0, MLA_ROPE, 2, dtype=F32) / MLA_ROPE))
    ang = jnp.arange(S, dtype=F32)[:, None] * inv[None, :]
    cos, sin = jnp.cos(ang), jnp.sin(ang)
    zeros = jnp.zeros((S, half), F32)
    tail = jnp.zeros((S, LANES - MLA_QK), F32)

    def block(rope_part, nope_val):
        return jnp.concatenate([jnp.full((S, MLA_NOPE), nope_val, F32), rope_part, tail], -1)

    a_r = jnp.concatenate([cos, cos], -1)
    bm_r = jnp.concatenate([-sin, zeros], -1)
    bp_r = jnp.concatenate([zeros, sin], -1)
    q_tabs = tuple(block(r, v) for r, v in ((a_r, 1.0), (bm_r, 0.0), (bp_r, 0.0)))
    k_tabs = tuple(block(r, 0.0) for r in (a_r, bm_r, bp_r))
    return q_tabs, k_tabs


def _t5_bucket(dist):
    exact = REL_BUCKETS // 2
    d = jnp.maximum(dist, 1).astype(F32)
    large = exact + (jnp.log(d / exact) / math.log(REL_MAX_DIST / exact) * (REL_BUCKETS - exact)).astype(jnp.int32)
    large = jnp.minimum(large, REL_BUCKETS - 1)
    return jnp.where(dist < exact, dist, large)


def _swa_bucket_table():
    a = jnp.arange(BLOCK_Q)[:, None]
    col = jnp.arange(2 * BLOCK_Q)[None, :]
    return _t5_bucket(jnp.maximum(a + BLOCK_Q - col, 0)).astype(jnp.int32)


def _even_weights(W):
    w = W['ev_w_in']
    c_kv1 = MLA_Q_LORA + MLA_KV_LORA
    c_kr1 = c_kv1 + MLA_ROPE
    c_qs1 = c_kr1 + SWA_HEADS * HEAD_DIM
    zeros = lambda n: jnp.zeros((D_MODEL, n), w.dtype)
    w_in = jnp.concatenate([w[:, c_kr1:c_qs1], w[:, :c_kv1], w[:, c_qs1:], zeros(KR_LANE0), w[:, c_kv1:c_kr1],
                            zeros(LANES - KR_LANE0 - MLA_ROPE)], axis=1)
    uq = W['ev_w_uq'].reshape(MLA_Q_LORA, MLA_HEADS, MLA_QK)
    w_uq = jnp.pad(uq, ((0, 0), (0, 0), (0, LANES - MLA_QK))).reshape(MLA_Q_LORA, MLA_HEADS * LANES)
    ukv = W['ev_w_ukv'].reshape(MLA_KV_LORA, MLA_HEADS, MLA_NOPE + MLA_V)
    w_k = jnp.pad(ukv[..., :MLA_NOPE], ((0, 0), (0, 0), (0, LANES - MLA_NOPE))).reshape(MLA_KV_LORA, -1)
    w_v = ukv[..., MLA_NOPE:].reshape(MLA_KV_LORA, MLA_HEADS * MLA_V)
    return w_in, w_uq, w_k, w_v, W['ev_w_out']


def _even_in_grad_unpad(dw):
    kr0 = EV_KR[0] + KR_LANE0
    return jnp.concatenate([dw[:, EV_CQ[0]:EV_CKV[1]], dw[:, kr0:kr0 + MLA_ROPE], dw[:, EV_QS[0]:EV_QS[1]],
                            dw[:, EV_KS[0]:EV_VS[1]]], axis=1)


def _even_fwd(xb, W, P, i, B, S, tabs, xchg, tag):
    j = i // 2
    q_tabs, k_tabs, bias, sinkcol = tabs
    w_in, w_uq, w_k, w_v, w_out = _even_weights(W)
    h = _mm(xb, w_in, name=f"{tag}_in")
    cqn, ckvn, rq, rkv = _even_norms(h, P['ev_q_norm'][j][None], P['ev_kv_norm'][j][None], name=f"{tag}_norms")
    q = _rope(_mm(cqn, w_uq, name=f"{tag}_uq"), q_tabs, S, sign=1.0, name=f"{tag}_ropeq")
    knp = _mm(ckvn, w_k, out_dtypes=(BF16,), name=f"{tag}_uk")
    v = _mm(ckvn, w_v, out_dtypes=(BF16,), name=f"{tag}_uv")
    k = _mla_keys(knp, h, k_tabs, S, name=f"{tag}_keys")
    o_mla, lse_mla, got = _flash_fwd(q, k, v, q_blk0=0, k_blk0=0, v_blk0=0, W=2 * LANES, n_pairs=MLA_HEADS // 2,
                                     B=B, S=S, scale=MLA_QK ** -0.5, comm=xchg.carry(f"{tag}_mla"), name=f"{tag}_mla")
    xchg.carried(f"{tag}_mla", got)
    o_swa, lse_swa, got = _swa_fwd(h, bias, sinkcol, B=B, S=S, comm=xchg.carry(f"{tag}_swa"), name=f"{tag}_swa")
    xchg.carried(f"{tag}_swa", got)
    o_cat = jnp.concatenate([o_mla, o_swa], axis=-1)
    res = dict(h=h, cqn=cqn, ckvn=ckvn, rq=rq, rkv=rkv, q=q, k=k, v=v, o_mla=o_mla, lse_mla=lse_mla,
               o_swa=o_swa, lse_swa=lse_swa, o_cat=o_cat)
    return (o_cat, w_out), res


def _shift_prev(own, prev, B, S):
    prev = prev.reshape(B, S, LANES)
    shifted = jnp.concatenate([prev[:, BLOCK_Q:], jnp.zeros_like(prev[:, :BLOCK_Q])], axis=1)
    return (own + shifted.reshape(B * S, LANES)).astype(BF16)


def _even_bwd(dmb, dz1, xb, W, P, j, B, S, tabs, res, xchg, tag):
    q_tabs, k_tabs, bias, sinkcol = tabs
    w_in, w_uq, w_k, w_v, w_out = _even_weights(W)
    g = {}
    g['ev_w_out'] = _mm_tn(res['o_cat'], dmb, name=f"{tag}_dwout")
    do = _mm(dmb, w_out, trans_b=True, out_dtypes=(BF16,), name=f"{tag}_do")
    dq, dk, dv, got = _flash_bwd(res['q'], res['k'], res['v'], res['o_mla'], do, res['lse_mla'], q_blk0=0, k_blk0=0,
                                 v_blk0=0, do_blk0=0, W=2 * LANES, n_pairs=MLA_HEADS // 2, B=B, S=S,
                                 scale=MLA_QK ** -0.5, qk_dtype=F32, comm=xchg.bwd_items(), name=f"{tag}_mla_bwd")
    xchg.bwd_done(got)
    dq_pre = _rope(dq, q_tabs, S, sign=-1.0, name=f"{tag}_ropeq_bwd")
    dw_uq = _mm_tn(res['cqn'], dq_pre, name=f"{tag}_dwuq")
    g['ev_w_uq'] = dw_uq.reshape(MLA_Q_LORA, MLA_HEADS, LANES)[..., :MLA_QK].reshape(MLA_Q_LORA, MLA_HEADS * MLA_QK)
    dcqn = _mm(dq_pre, w_uq, trans_b=True, name=f"{tag}_dcqn")
    dw_k = _mm_tn(res['ckvn'], dk, name=f"{tag}_dwuk").reshape(MLA_KV_LORA, MLA_HEADS, LANES)[..., :MLA_NOPE]
    dw_v = _mm_tn(res['ckvn'], dv, name=f"{tag}_dwuv").reshape(MLA_KV_LORA, MLA_HEADS, MLA_V)
    g['ev_w_ukv'] = jnp.concatenate([dw_k, dw_v], axis=-1).reshape(MLA_KV_LORA, MLA_HEADS * (MLA_NOPE + MLA_V))
    dckvn_v = _mm(dv, w_v, trans_b=True, name=f"{tag}_dckvn_v")
    dckvn = _mm(dk, w_k, trans_b=True, extras=(dckvn_v,), epilogue=lambda acc, r: (acc + r,), name=f"{tag}_dckvn")
    dkr_pre = _mla_rope_key_grad(dk, k_tabs, S, name=f"{tag}_ropek_bwd")
    dqs, dko, dkp, dvo, dvp, dbias, dsink = _swa_bwd(res['h'], res['o_swa'], do, res['lse_swa'], bias, sinkcol,
                                                     do_blk0=1, B=B, S=S, name=f"{tag}_swa_bwd")
    dh, dgq, dgkv = _even_in_bwd(res['h'], res['rq'], res['rkv'], P['ev_q_norm'][j][None], P['ev_kv_norm'][j][None],
                                 dcqn, dckvn, dqs, _shift_prev(dko, dkp, B, S), _shift_prev(dvo, dvp, B, S), dkr_pre,
                                 name=f"{tag}_in_bwd")
    g['ev_w_in'] = _even_in_grad_unpad(_mm_tn(xb, dh, name=f"{tag}_dwin"))
    xchg.push_grads({(n, j): val for n, val in g.items()})
    dx_kwargs = dict(trans_b=True, extras=(dz1,), epilogue=lambda acc, r: (acc + DN_ALPHA * r,), name=f"{tag}_dx")
    dx = _scattering(xchg, _mm, dh, w_in, **dx_kwargs) if j == 0 else _mm(dh, w_in, **dx_kwargs)
    small = dict(ev_q_norm=dgq[0], ev_kv_norm=dgkv[0], dbias=dbias, ev_sinks=jnp.sum(dsink, axis=(1, 2)))
    return dx, small


def _odd_fwd(xb, W, P, i, B, S, xchg, tag):
    j = i // 2
    w = W['od_w_in']
    w_qkv = w[:, :ODD_QKV]
    w_f = jnp.pad(w[:, ODD_QKV:], ((0, 0), (0, LANES - FOX_HEADS)))
    bf = jnp.pad(P['od_b_f'][j], (0, LANES - FOX_HEADS))[None]
    qkv = _mm(xb, w_qkv, out_dtypes=(BF16,), name=f"{tag}_qkv")
    f = _mm(xb, w_f, name=f"{tag}_f").reshape(B, S, LANES)
    csh, chs = _fox_decay_fwd(f, bf, name=f"{tag}_decay")
    crow = chs[:, :FOX_HEADS].reshape(B, FOX_HEADS, S // ATT_TILE, 1, ATT_TILE)
    n_blk = FOX_HEADS * HEAD_DIM // LANES
    o, lse, got = _flash_fwd(qkv, qkv, qkv, q_blk0=0, k_blk0=n_blk, v_blk0=2 * n_blk, W=LANES,
                             n_pairs=FOX_HEADS // 2, B=B, S=S, scale=HEAD_DIM ** -0.5, csh=csh, crow=crow,
                             comm=xchg.carry(f"{tag}_fox"), name=f"{tag}_fox")
    xchg.carried(f"{tag}_fox", got)
    res = dict(f=f, bf=bf, csh=csh, crow=crow, qkv=qkv, o=o, lse=lse, w_qkv=w_qkv, w_f=w_f)
    return (o, W['od_w_out']), res


def _odd_bwd(dmb, dz1, xb, W, P, j, B, S, res, xchg, tag):
    g = {}
    w_out = W['od_w_out']
    g['od_w_out'] = _mm_tn(res['o'], dmb, name=f"{tag}_dwout")
    do = _mm(dmb, w_out, trans_b=True, out_dtypes=(BF16,), name=f"{tag}_do")
    qkv = res['qkv']
    n_blk = FOX_HEADS * HEAD_DIM // LANES
    dq, dk, dv, dck, dcq, got = _flash_bwd(qkv, qkv, qkv, res['o'], do, res['lse'], q_blk0=0, k_blk0=n_blk,
                                           v_blk0=2 * n_blk, do_blk0=0, W=LANES, n_pairs=FOX_HEADS // 2, B=B, S=S,
                                           scale=HEAD_DIM ** -0.5, qk_dtype=BF16, csh=res['csh'], crow=res['crow'],
                                           comm=xchg.bwd_items(), name=f"{tag}_fox_bwd")
    xchg.bwd_done(got)
    dc = dck.reshape(B, FOX_HEADS, S) + dcq.reshape(B, FOX_HEADS, S)
    dc_hs = jnp.pad(dc, ((0, 0), (0, LANES - FOX_HEADS), (0, 0)))
    df, dbf = _fox_decay_bwd(dc_hs, res['f'], res['bf'], name=f"{tag}_decay_bwd")
    df = df.reshape(B * S, LANES)
    dqkv = jnp.concatenate([dq, dk, dv], axis=-1)
    dw_qkv = _mm_tn(xb, dqkv, name=f"{tag}_dwqkv")
    dw_f = _mm_tn(xb, df, name=f"{tag}_dwf")
    g['od_w_in'] = jnp.concatenate([dw_qkv, dw_f[:, :FOX_HEADS]], axis=1)
    dxf = _mm(df, res['w_f'], trans_b=True, extras=(dz1,), epilogue=lambda acc, r: (acc + DN_ALPHA * r,),
              name=f"{tag}_dxf")
    xchg.push_grads({(n, j): val for n, val in g.items()})
    dx = _mm(dqkv, res['w_qkv'], trans_b=True, extras=(dxf,), epilogue=lambda acc, r: (acc + r,), name=f"{tag}_dx")
    small = dict(od_b_f=dbf[0, :FOX_HEADS])
    return dx, small


def _carrying(xchg, name, call, *args, **kwargs):
    comm = xchg.carry(name)
    out = call(*args, comm=comm, name=name, **kwargs)
    if comm:
        out, got = out
        xchg.carried(name, got)
    return out


def _scattering(xchg, call, *args, **kwargs):
    comm = xchg.bwd_items()
    out = call(*args, comm=comm, **kwargs)
    if comm:
        out, got = out
        xchg.bwd_done(got)
    return out


def _local_step(x, p, target, P, xchg):
    B, S, D = x.shape
    T = B * S
    q_tabs, k_tabs = _rope_tables(S)
    bucket = _swa_bucket_table()
    in_bucket = (bucket[..., None] == jnp.arange(REL_BUCKETS)).astype(F32)
    bias = jnp.einsum('acb,bh->hac', in_bucket, P['rel_bias'], precision=lax.Precision.HIGHEST)

    xc = x.reshape(T, D)
    xcb = xc.astype(BF16)
    saved = []
    for i in range(DEPTH):
        j = i // 2
        tag = f"l{i}"
        W = xchg.layer_weights(i)
        lay = dict(xb=xcb, W=W)
        if i % 2 == 0:
            sinkcol = jnp.broadcast_to(P['ev_sinks'][j][:, None, None], (SWA_HEADS, BLOCK_Q, 1)).astype(F32)
            lay['tabs'] = (q_tabs, k_tabs, bias, sinkcol)
            (o, w_out), lay['mix'] = _even_fwd(xcb, W, P, i, B, S, lay['tabs'], xchg, tag)
        else:
            (o, w_out), lay['mix'] = _odd_fwd(xcb, W, P, i, B, S, xchg, tag)
        x1, x1b, lay['xh1'], lay['r1'] = _carrying(xchg, f"{tag}_out_ln1", _mm_ln, o, w_out, xc,
                                                   P['ln1_g'][i][None], P['ln1_b'][i][None])
        lay['x1b'] = x1b
        lay['u'], lay['a'] = _carrying(xchg, f"{tag}_up", _mm, x1b, W['w_up'], out_dtypes=(F32, BF16),
                                       epilogue=lambda acc: (acc, jnp.square(jnp.maximum(acc, 0.0))))
        x2, x2b, lay['xh2'], lay['r2'] = _carrying(xchg, f"{tag}_down_ln2", _mm_ln, lay['a'], W['w_down'], x1,
                                                   P['ln2_g'][i][None], P['ln2_b'][i][None])
        lay['x2b'] = x2b
        lay['p'] = p[i].reshape(T, D_PLE)
        lay['e'] = _mm(lay['p'], W['ple_w_proj'], name=f"{tag}_ple_proj")

        def gate(acc, bg, e, x2v):
            gv = 1.0 / (1.0 + jnp.exp(-(acc + bg)))
            y = x2v + gv * e
            return y, y, gv

        xc, xcb, lay['g'] = _carrying(xchg, f"{tag}_ple_gate", _mm, x2b, W['ple_w_gate'],
                                      extras=(P['ple_b_gate'][i][None], lay['e'], x2), epilogue=gate,
                                      out_dtypes=(F32, BF16, F32))
        saved.append(lay)

    dy, sq = _loss_grad(xc, target.reshape(T, D), name="loss")

    Gs = {n: [None] * DEPTH for n in ('ln1_g', 'ln1_b', 'ln2_g', 'ln2_b', 'ple_b_gate')}
    Gs.update({n: [None] * (DEPTH // 2) for n in ('ev_q_norm', 'ev_kv_norm', 'ev_sinks', 'od_b_f')})
    dbias_total = None
    for i in reversed(range(DEPTH)):
        j = i // 2
        tag = f"l{i}b"
        lay = saved[i]
        W = lay['W']
        de, dzg, dbg = _ple_bwd_elem(dy, lay['g'], lay['e'], name=f"{tag}_ple_elem")
        Gs['ple_b_gate'][i] = dbg[0]
        g_mlp = {('ple_w_proj', i): _mm_tn(lay['p'], de, name=f"{tag}_dwproj"),
                 ('ple_w_gate', i): _mm_tn(lay['x2b'], dzg, name=f"{tag}_dwgate")}
        dz2, dz2b, dg2, db2 = _mm_ln_bwd(dzg, W['ple_w_gate'], dy, 1.0, lay['xh2'], lay['r2'], P['ln2_g'][i][None],
                                         name=f"{tag}_dx2_ln2")
        Gs['ln2_g'][i], Gs['ln2_b'][i] = dg2[0], db2[0]
        g_mlp[('w_down', i)] = _mm_tn(lay['a'], dz2b, name=f"{tag}_dwdown")
        du = _mm(dz2b, W['w_down'], trans_b=True, extras=(lay['u'],), out_dtypes=(BF16,),
                 epilogue=lambda acc, u: (acc * (2.0 * jnp.maximum(u, 0.0)),), name=f"{tag}_du")
        g_mlp[('w_up', i)] = _mm_tn(lay['x1b'], du, name=f"{tag}_dwup")
        xchg.push_grads(g_mlp)
        dz1, dz1b, dg1, db1 = _mm_ln_bwd(du, W['w_up'], dz2, DN_ALPHA, lay['xh1'], lay['r1'], P['ln1_g'][i][None],
                                         name=f"{tag}_dx1_ln1")
        Gs['ln1_g'][i], Gs['ln1_b'][i] = dg1[0], db1[0]
        if i % 2 == 0:
            dy, small = _even_bwd(dz1b, dz1, lay['xb'], W, P, j, B, S, lay['tabs'], lay['mix'], xchg, tag)
            dbias_total = small['dbias'] if dbias_total is None else dbias_total + small['dbias']
            for n in ('ev_q_norm', 'ev_kv_norm', 'ev_sinks'):
                Gs[n][j] = small[n]
        else:
            dy, small = _odd_bwd(dz1b, dz1, lay['xb'], W, P, j, B, S, lay['mix'], xchg, tag)
            Gs['od_b_f'][j] = small['od_b_f']

    grads_small = {n: jnp.stack(v) for n, v in Gs.items()}
    drel = _bias_bucket_sum(dbias_total, bucket, name="rel_bias_grad")
    grads_small['rel_bias'] = drel[:, :REL_BUCKETS].T
    return sq, dy.reshape(B, S, D), grads_small


def kernel(x, p, rel_bias, ev_w_in, ev_q_norm, ev_w_uq, ev_kv_norm, ev_w_ukv, ev_sinks, ev_w_out, od_w_in, od_b_f, od_w_out, ln1_g, ln1_b, w_up, w_down, ln2_g, ln2_b, ple_w_proj, ple_w_gate, ple_b_gate, loss_target, m_rel_bias, m_ev_w_in, m_ev_q_norm, m_ev_w_uq, m_ev_kv_norm, m_ev_w_ukv, m_ev_sinks, m_ev_w_out, m_od_w_in, m_od_b_f, m_od_w_out, m_ln1_g, m_ln1_b, m_w_up, m_w_down, m_ln2_g, m_ln2_b, m_ple_w_proj, m_ple_w_gate, m_ple_b_gate, v_rel_bias, v_ev_w_in, v_ev_q_norm, v_ev_w_uq, v_ev_kv_norm, v_ev_w_ukv, v_ev_sinks, v_ev_w_out, v_od_w_in, v_od_b_f, v_od_w_out, v_ln1_g, v_ln1_b, v_w_up, v_w_down, v_ln2_g, v_ln2_b, v_ple_w_proj, v_ple_w_gate, v_ple_b_gate):
    given = dict(locals())
    w = {n: given[n] for n in WEIGHTS}
    mom = {n: given["m_" + n] for n in WEIGHTS}
    var = {n: given["v_" + n] for n in WEIGHTS}
    small_shapes = {n: w[n].shape for n in SMALL}

    xchg = _MeshExchange({n: w[n].astype(BF16) for n in BIG})
    P = {n: w[n] for n in SMALL}

    sq, grad_x, grads_small = _local_step(x, p, loss_target, P, xchg)
    loss = lax.psum(0.5 * jnp.sum(sq) / D_MODEL, ("x", "y", "c"))

    received = xchg.finish()
    g_small_packed = _all_reduce_small(_pack_small(grads_small), name="reduce_small_grads")
    g_small = _unpack_small(g_small_packed, small_shapes)

    grad, delta, new_m, new_v = {}, {}, {}, {}
    for n in BIG:
        parts = [received[(n, idx)] for idx in range(w[n].shape[0])]
        grad[n], delta[n], new_m[n], new_v[n] = _adamw_slots(w[n], parts, mom[n], var[n], name=f"adamw_{n}")
    d, nm, nv = _adamw(_pack_small(w), g_small_packed, _pack_small(mom), _pack_small(var), name="adamw_small")
    d, nm, nv = (_unpack_small(t, small_shapes) for t in (d, nm, nv))
    for n in SMALL:
        grad[n], delta[n], new_m[n], new_v[n] = g_small[n], d[n], nm[n], nv[n]

    return (loss, grad_x, *[grad[n] for n in WEIGHTS], *[delta[n] for n in WEIGHTS],
            *[new_m[n] for n in WEIGHTS], *[new_v[n] for n in WEIGHTS])
```

```python
import math

import jax
import jax.numpy as jnp
from jax import lax
from jax.experimental import pallas as pl
from jax.experimental.pallas import tpu as pltpu

F32, BF16 = jnp.float32, jnp.bfloat16

D_MODEL = 1024
DEPTH = 4
HEAD_DIM = 64
MLA_HEADS, MLA_NOPE, MLA_ROPE, MLA_V = 8, 64, 32, 64
MLA_Q_LORA, MLA_KV_LORA = 384, 256
MLA_QK = MLA_NOPE + MLA_ROPE
ROPE_THETA = 10000.0
SWA_HEADS, SWA_KV_HEADS, SWA_WINDOW = 8, 2, 128
SWA_GROUP = SWA_HEADS // SWA_KV_HEADS
REL_BUCKETS, REL_MAX_DIST = 32, 128
FOX_HEADS = 16
D_FF = 4 * D_MODEL
D_PLE = 256
BLOCK_Q = 128
DN_ALPHA = (2 * DEPTH) ** 0.25
NORM_EPS = 1e-5
NEG_INF = -1e30
EVEN_IN = 1440
ODD_QKV = 3 * FOX_HEADS * HEAD_DIM
LANES = 128

EV_QS = (0, 512)
EV_CQ = (512, 896)
EV_CKV = (896, 1152)
EV_KS = (1152, 1280)
EV_VS = (1280, 1408)
EV_KR = (1408, 1536)
EVEN_IN_PAD = 1536
KR_LANE0 = MLA_NOPE

ADAM_LR, ADAM_B1, ADAM_B2, ADAM_EPS, ADAM_WD, ADAM_STEP = 0.001, 0.9, 0.999, 1e-08, 0.01, 10

N_DEV = 8
VMEM_LIMIT_BYTES = 48 * 1024 * 1024
ATT_TILE = 512
ATT_TILE_BWD = 512
PAIRS_PER_STEP_FWD = 4
PAIRS_PER_STEP_BWD = 2

NN = (((1,), (0,)), ((), ()))
NT = (((1,), (1,)), ((), ()))
TN = (((0,), (0,)), ((), ()))

BIG = ['ev_w_in', 'ev_w_uq', 'ev_w_ukv', 'ev_w_out', 'od_w_in', 'od_w_out', 'w_up', 'w_down',
       'ple_w_proj', 'ple_w_gate']
BIG_AXIS = {'ev_w_in': 2, 'ev_w_uq': 2, 'ev_w_ukv': 2, 'ev_w_out': 1, 'od_w_in': 2, 'od_w_out': 1,
            'w_up': 2, 'w_down': 1, 'ple_w_proj': 2, 'ple_w_gate': 1}
SMALL = ['rel_bias', 'ev_q_norm', 'ev_kv_norm', 'ev_sinks', 'od_b_f', 'ln1_g', 'ln1_b', 'ln2_g', 'ln2_b',
         'ple_b_gate']
WEIGHTS = ['rel_bias', 'ev_w_in', 'ev_q_norm', 'ev_w_uq', 'ev_kv_norm', 'ev_w_ukv', 'ev_sinks', 'ev_w_out',
           'od_w_in', 'od_b_f', 'od_w_out', 'ln1_g', 'ln1_b', 'w_up', 'w_down', 'ln2_g', 'ln2_b',
           'ple_w_proj', 'ple_w_gate', 'ple_b_gate']


def _cparams(*sem):
    return pltpu.CompilerParams(dimension_semantics=sem, vmem_limit_bytes=VMEM_LIMIT_BYTES)


def _pick(n, cands):
    for c in cands:
        if n % c == 0:
            return c
    return n


MM_STEP_BYTES = 10 * 1024 * 1024
MM_OUT_BYTES = 8 * 1024 * 1024
MM_CHUNK = 512


def _mm(a, b, *, trans_b=False, extras=(), epilogue=None, row_epilogue=None, out_dtypes=(F32,), out_widths=None,
        n_sums=0, comm=(), name):
    M, K = a.shape
    N = b.shape[0] if trans_b else b.shape[1]
    n_ex, n_out = len(extras), len(out_dtypes)
    n_rows_out = n_out - n_sums
    out_widths = (N,) * n_out if out_widths is None else out_widths
    row_bytes = K * a.dtype.itemsize + (sum(w * jnp.dtype(d).itemsize
                                            for w, d in zip(out_widths[:n_rows_out], out_dtypes))
                                        + sum(e.shape[1] * e.dtype.itemsize for e in extras if e.shape[0] == M)
                                        + (4 * N if row_epilogue is not None else 0))
    tm = next((c for c in (1024, 512, 256) if M % c == 0 and c * row_bytes <= MM_STEP_BYTES), 128)
    nc = _pick(N, (MM_CHUNK, 384, 256, 128))
    n_c, kinds = len(comm), [k for k, _ in comm]
    n_scr = 1 if row_epilogue is not None else 0

    def body(*refs):
        c_in = refs[2 + n_ex:2 + n_ex + n_c]
        c_out = refs[2 + n_ex + n_c + n_out:2 + n_ex + 2 * n_c + n_out]
        sems = refs[2 + n_ex + 2 * n_c + n_out + n_scr:]
        refs = refs[:2 + n_ex] + refs[2 + n_ex + n_c:2 + n_ex + n_c + n_out] \
            + refs[2 + n_ex + 2 * n_c + n_out:2 + n_ex + 2 * n_c + n_out + n_scr]
        if n_c:
            place = _mesh_place()
            step = pl.program_id(0)

            @pl.when(step == 0)
            def _():
                _comm_start(kinds, c_in, c_out, sems, place)

        a_ref, b_ref = refs[:2]
        ex = refs[2:2 + n_ex]
        outs = refs[2 + n_ex:2 + n_ex + n_out]
        av = a_ref[...].astype(BF16)
        for n0 in range(0, N, nc):
            cols = slice(n0, n0 + nc)
            bv = (b_ref[cols, :] if trans_b else b_ref[:, cols]).astype(BF16)
            acc = lax.dot_general(av, bv, NT if trans_b else NN, preferred_element_type=F32)
            if row_epilogue is not None:
                refs[-1][:, cols] = acc
                continue
            res = epilogue(acc, *[e[:, cols] for e in ex]) if epilogue is not None else (acc,)
            for o, r in zip(outs, res):
                o[:, cols] = r.astype(o.dtype)
        if row_epilogue is not None:
            res = row_epilogue(refs[-1][...], *[e[...] for e in ex])
            for o, r in zip(outs[:n_rows_out], res):
                o[...] = r.astype(o.dtype)
            if n_sums:
                @pl.when(pl.program_id(0) == 0)
                def _():
                    for o in outs[n_rows_out:]:
                        o[...] = jnp.zeros_like(o)

                for o, r in zip(outs[n_rows_out:], res[n_rows_out:]):
                    o[...] += r
        if n_c:
            @pl.when(step == M // tm - 1)
            def _():
                _comm_wait(kinds, c_in, c_out, sems, place)

    in_specs = [pl.BlockSpec((tm, K), lambda i: (i, 0)), pl.BlockSpec(b.shape, lambda i: (0, 0))]
    for e in extras:
        if e.shape[0] == M:
            in_specs.append(pl.BlockSpec((tm, e.shape[1]), lambda i: (i, 0)))
        elif e.shape == (1, N):
            in_specs.append(pl.BlockSpec((1, N), lambda i: (0, 0)))
        else:
            raise ValueError(f"extra operand of shape {e.shape} for a ({M}, {N}) result")
    res = pl.pallas_call(
        body, name=name, grid=(M // tm,), in_specs=in_specs + [HBM_SPEC] * n_c,
        out_specs=[pl.BlockSpec((tm, w), lambda i: (i, 0)) for w in out_widths[:n_rows_out]]
        + [pl.BlockSpec((1, w), lambda i: (0, 0)) for w in out_widths[n_rows_out:]] + [HBM_SPEC] * n_c,
        out_shape=[jax.ShapeDtypeStruct((M, w), d) for w, d in zip(out_widths[:n_rows_out], out_dtypes)]
        + [jax.ShapeDtypeStruct((1, w), d) for w, d in zip(out_widths[n_rows_out:], out_dtypes[n_rows_out:])]
        + _comm_out_shapes(comm),
        scratch_shapes=([pltpu.VMEM((tm, N), F32)] if row_epilogue is not None else [])
        + (_comm_scratch(comm) if n_c else []),
        compiler_params=_cparams("arbitrary" if n_sums or n_c else "parallel"),
    )(a, b, *extras, *[c for _, c in comm])
    main = res[0] if n_out == 1 else tuple(res[:n_out])
    return (main, list(res[n_out:])) if n_c else main


def _mm_tn(a, b, *, name):
    T, K = a.shape
    N = b.shape[1]
    bk, bn = K, N
    while bk * bn * 4 > MM_OUT_BYTES:
        if bn >= bk and bn % (2 * LANES) == 0:
            bn //= 2
        else:
            bk //= 2
    tt = _pick(T, (1024, 512, 256))
    ck, cn = _pick(bk, (MM_CHUNK, 384, 256, 128)), _pick(bn, (MM_CHUNK, 384, 256, 128))

    def body(a_ref, b_ref, o_ref):
        t = pl.program_id(2)

        @pl.when(t == 0)
        def _():
            o_ref[...] = jnp.zeros_like(o_ref)

        for r0 in range(0, bk, ck):
            av = a_ref[:, r0:r0 + ck].astype(BF16)
            for c0 in range(0, bn, cn):
                o_ref[r0:r0 + ck, c0:c0 + cn] += lax.dot_general(
                    av, b_ref[:, c0:c0 + cn].astype(BF16), TN, preferred_element_type=F32)

    return pl.pallas_call(
        body, name=name, grid=(K // bk, N // bn, T // tt),
        in_specs=[pl.BlockSpec((tt, bk), lambda i, j, t: (t, i)), pl.BlockSpec((tt, bn), lambda i, j, t: (t, j))],
        out_specs=pl.BlockSpec((bk, bn), lambda i, j, t: (i, j)),
        out_shape=jax.ShapeDtypeStruct((K, N), F32),
        compiler_params=_cparams("parallel", "parallel", "arbitrary"),
    )(a, b)


ROW_TILE = 256


def _row_spec(cols, col_block=0):
    return pl.BlockSpec((ROW_TILE, cols), lambda i: (i, col_block))


def _tab_spec(cols, period):
    return pl.BlockSpec((ROW_TILE, cols), lambda i: (i % period, 0))


def _full_spec(shape):
    return pl.BlockSpec(shape, lambda i: (0,) * len(shape))


def _mm_ln(a, w, x, g, b, *, comm=(), name):
    def ln_rows(m, xv, gv, bv):
        z = DN_ALPHA * xv + m
        mu = jnp.mean(z, -1, keepdims=True)
        zc = z - mu
        r = lax.rsqrt(jnp.mean(zc * zc, -1, keepdims=True) + NORM_EPS)
        xh = zc * r
        y = xh * gv + bv
        return y, y, xh, jnp.broadcast_to(r, (r.shape[0], LANES))

    D = w.shape[1]
    return _mm(a, w, extras=(x, g, b), row_epilogue=ln_rows, out_dtypes=(F32, BF16, F32, F32),
               out_widths=(D, D, D, LANES), comm=comm, name=name)


def _mm_ln_bwd(a, w, resid, resid_scale, xh, r, g, *, name):
    def ln_bwd_rows(acc, rv, xhv, rstd, gv):
        dyv = acc + resid_scale * rv
        dyg = dyv * gv
        c1 = jnp.mean(dyg, -1, keepdims=True)
        c2 = jnp.mean(dyg * xhv, -1, keepdims=True)
        dz = _widen(rstd, dyv.shape[-1]) * (dyg - c1 - xhv * c2)
        return dz, dz, jnp.sum(dyv * xhv, 0, keepdims=True), jnp.sum(dyv, 0, keepdims=True)

    D = w.shape[0]
    return _mm(a, w, trans_b=True, extras=(resid, xh, r, g), row_epilogue=ln_bwd_rows,
               out_dtypes=(F32, BF16, F32, F32), out_widths=(D, D, D, D), n_sums=2, name=name)


def _loss_grad(y, target, *, name):
    T, D = y.shape

    def body(y_ref, t_ref, dy_ref, sq_ref):
        err = y_ref[...] - t_ref[...]
        dy_ref[...] = err / D

        @pl.when(pl.program_id(0) == 0)
        def _():
            sq_ref[...] = jnp.zeros_like(sq_ref)

        sq_ref[...] += jnp.sum(err * err, 0, keepdims=True)

    return pl.pallas_call(
        body, name=name, grid=(T // ROW_TILE,),
        in_specs=[_row_spec(D), _row_spec(D)],
        out_specs=[_row_spec(D), _full_spec((1, D))],
        out_shape=[jax.ShapeDtypeStruct((T, D), F32), jax.ShapeDtypeStruct((1, D), F32)],
        compiler_params=_cparams("arbitrary"),
    )(y, target)


def _ple_bwd_elem(dx3, g, e, *, name):
    T, D = dx3.shape

    def body(dx_ref, g_ref, e_ref, de_ref, dz_ref, db_ref):
        dx, gv = dx_ref[...], g_ref[...]
        de_ref[...] = (dx * gv).astype(BF16)
        dz = dx * e_ref[...] * gv * (1.0 - gv)
        dz_ref[...] = dz.astype(BF16)

        @pl.when(pl.program_id(0) == 0)
        def _():
            db_ref[...] = jnp.zeros_like(db_ref)

        db_ref[...] += jnp.sum(dz, 0, keepdims=True)

    return pl.pallas_call(
        body, name=name, grid=(T // ROW_TILE,),
        in_specs=[_row_spec(D), _row_spec(D), _row_spec(D)],
        out_specs=[_row_spec(D), _row_spec(D), _full_spec((1, D))],
        out_shape=[jax.ShapeDtypeStruct((T, D), BF16), jax.ShapeDtypeStruct((T, D), BF16),
                   jax.ShapeDtypeStruct((1, D), F32)],
        compiler_params=_cparams("arbitrary"),
    )(dx3, g, e)


def _rotate(xv, a, bm, bp, sign):
    half = MLA_ROPE // 2
    width = xv.shape[-1]
    a, bm, bp = (_widen(t, width) for t in (a, bm, bp))
    return xv * a + sign * (pltpu.roll(xv, width - half, 1) * bm + pltpu.roll(xv, half, 1) * bp)


def _rope(x, tabs, seq, *, sign, name):
    T, width = x.shape

    def body(x_ref, a_ref, bm_ref, bp_ref, o_ref):
        o_ref[...] = _rotate(x_ref[...], a_ref[...], bm_ref[...], bp_ref[...], sign).astype(BF16)

    return pl.pallas_call(
        body, name=name, grid=(T // ROW_TILE,),
        in_specs=[_row_spec(width)] + [_tab_spec(LANES, seq // ROW_TILE)] * 3,
        out_specs=_row_spec(width),
        out_shape=jax.ShapeDtypeStruct((T, width), BF16),
        compiler_params=_cparams("parallel"),
    )(x, *tabs)


def _mla_keys(knp, h, k_tabs, seq, *, name):
    T = knp.shape[0]

    def body(k_ref, h_ref, a_ref, bm_ref, bp_ref, o_ref):
        kr = _rotate(h_ref[...], a_ref[...], bm_ref[...], bp_ref[...], 1.0)
        for hd in range(MLA_HEADS):
            cols = slice(hd * LANES, (hd + 1) * LANES)
            o_ref[:, cols] = (k_ref[:, cols].astype(F32) + kr).astype(BF16)

    return pl.pallas_call(
        body, name=name, grid=(T // ROW_TILE,),
        in_specs=[_row_spec(MLA_HEADS * LANES), _row_spec(LANES, EV_KR[0] // LANES)]
        + [_tab_spec(LANES, seq // ROW_TILE)] * 3,
        out_specs=_row_spec(MLA_HEADS * LANES),
        out_shape=jax.ShapeDtypeStruct((T, MLA_HEADS * LANES), BF16),
        compiler_params=_cparams("parallel"),
    )(knp, h, *k_tabs)


def _mla_rope_key_grad(dk, k_tabs, seq, *, name):
    T = dk.shape[0]

    def body(dk_ref, a_ref, bm_ref, bp_ref, o_ref):
        tot = dk_ref[:, 0:LANES]
        for hd in range(1, MLA_HEADS):
            tot = tot + dk_ref[:, hd * LANES:(hd + 1) * LANES]
        o_ref[...] = _rotate(tot, a_ref[...], bm_ref[...], bp_ref[...], -1.0).astype(BF16)

    return pl.pallas_call(
        body, name=name, grid=(T // ROW_TILE,),
        in_specs=[_row_spec(MLA_HEADS * LANES)] + [_tab_spec(LANES, seq // ROW_TILE)] * 3,
        out_specs=_row_spec(LANES),
        out_shape=jax.ShapeDtypeStruct((T, LANES), BF16),
        compiler_params=_cparams("parallel"),
    )(dk, *k_tabs)


def _even_norms(h, gq, gkv, *, name):
    T = h.shape[0]

    def body(h_ref, gq_ref, gkv_ref, cq_ref, ckv_ref, rq_ref, rkv_ref):
        cq = h_ref[:, EV_CQ[0]:EV_CQ[1]]
        rq = lax.rsqrt(jnp.mean(cq * cq, -1, keepdims=True) + NORM_EPS)
        cq_ref[...] = (cq * rq * gq_ref[...]).astype(BF16)
        rq_ref[...] = jnp.broadcast_to(rq, rq_ref.shape)
        ckv = h_ref[:, EV_CKV[0]:EV_CKV[1]]
        rkv = lax.rsqrt(jnp.mean(ckv * ckv, -1, keepdims=True) + NORM_EPS)
        ckv_ref[...] = (ckv * rkv * gkv_ref[...]).astype(BF16)
        rkv_ref[...] = jnp.broadcast_to(rkv, rkv_ref.shape)

    return pl.pallas_call(
        body, name=name, grid=(T // ROW_TILE,),
        in_specs=[_row_spec(EVEN_IN_PAD), _full_spec((1, MLA_Q_LORA)), _full_spec((1, MLA_KV_LORA))],
        out_specs=[_row_spec(MLA_Q_LORA), _row_spec(MLA_KV_LORA), _row_spec(LANES), _row_spec(LANES)],
        out_shape=[jax.ShapeDtypeStruct((T, MLA_Q_LORA), BF16), jax.ShapeDtypeStruct((T, MLA_KV_LORA), BF16),
                   jax.ShapeDtypeStruct((T, LANES), F32), jax.ShapeDtypeStruct((T, LANES), F32)],
        compiler_params=_cparams("parallel"),
    )(h, gq, gkv)


def _even_in_bwd(h, rq, rkv, gq, gkv, dcqn, dckvn, dqs, dks, dvs, dkr, *, name):
    T = h.shape[0]

    def rms_bwd(c, r, g, dy):
        r = _widen(r, c.shape[-1])
        xr = c * r
        dyg = dy * g
        return r * (dyg - xr * jnp.mean(dyg * xr, -1, keepdims=True)), jnp.sum(dy * xr, 0, keepdims=True)

    def body(h_ref, rq_ref, rkv_ref, gq_ref, gkv_ref, dcq_ref, dckv_ref, dqs_ref, dks_ref, dvs_ref, dkr_ref,
             dh_ref, dgq_ref, dgkv_ref):
        @pl.when(pl.program_id(0) == 0)
        def _():
            dgq_ref[...] = jnp.zeros_like(dgq_ref)
            dgkv_ref[...] = jnp.zeros_like(dgkv_ref)

        dcq, dgq = rms_bwd(h_ref[:, EV_CQ[0]:EV_CQ[1]], rq_ref[...], gq_ref[...], dcq_ref[...])
        dckv, dgkv = rms_bwd(h_ref[:, EV_CKV[0]:EV_CKV[1]], rkv_ref[...], gkv_ref[...], dckv_ref[...])
        dgq_ref[...] += dgq
        dgkv_ref[...] += dgkv
        dh_ref[:, EV_QS[0]:EV_QS[1]] = dqs_ref[...]
        dh_ref[:, EV_CQ[0]:EV_CQ[1]] = dcq.astype(BF16)
        dh_ref[:, EV_CKV[0]:EV_CKV[1]] = dckv.astype(BF16)
        dh_ref[:, EV_KS[0]:EV_KS[1]] = dks_ref[...]
        dh_ref[:, EV_VS[0]:EV_VS[1]] = dvs_ref[...]
        dh_ref[:, EV_KR[0]:EV_KR[1]] = dkr_ref[...]

    return pl.pallas_call(
        body, name=name, grid=(T // ROW_TILE,),
        in_specs=[_row_spec(EVEN_IN_PAD), _row_spec(LANES), _row_spec(LANES), _full_spec((1, MLA_Q_LORA)),
                  _full_spec((1, MLA_KV_LORA)), _row_spec(MLA_Q_LORA), _row_spec(MLA_KV_LORA),
                  _row_spec(SWA_HEADS * HEAD_DIM), _row_spec(LANES), _row_spec(LANES), _row_spec(LANES)],
        out_specs=[_row_spec(EVEN_IN_PAD), _full_spec((1, MLA_Q_LORA)), _full_spec((1, MLA_KV_LORA))],
        out_shape=[jax.ShapeDtypeStruct((T, EVEN_IN_PAD), BF16), jax.ShapeDtypeStruct((1, MLA_Q_LORA), F32),
                   jax.ShapeDtypeStruct((1, MLA_KV_LORA), F32)],
        compiler_params=_cparams("arbitrary"),
    )(h, rq, rkv, gq, gkv, dcqn, dckvn, dqs, dks, dvs, dkr)


def _fox_decay_fwd(f3, bf, *, name):
    B, S, _ = f3.shape

    def body(f_ref, b_ref, csh_ref, chs_ref):
        x = f_ref[...] + b_ref[...]
        c = jnp.minimum(x, 0.0) - jnp.log1p(jnp.exp(-jnp.abs(x)))
        row = lax.broadcasted_iota(jnp.int32, (S, LANES), 0)
        k = 1
        while k < S:
            c = c + jnp.where(row >= k, pltpu.roll(c, k, 0), 0.0)
            k *= 2
        csh_ref[...] = c
        chs_ref[...] = c.T

    return pl.pallas_call(
        body, name=name, grid=(B,),
        in_specs=[pl.BlockSpec((None, S, LANES), lambda b: (b, 0, 0)), pl.BlockSpec((1, LANES), lambda b: (0, 0))],
        out_specs=[pl.BlockSpec((None, S, LANES), lambda b: (b, 0, 0)),
                   pl.BlockSpec((None, LANES, S), lambda b: (b, 0, 0))],
        out_shape=[jax.ShapeDtypeStruct((B, S, LANES), F32), jax.ShapeDtypeStruct((B, LANES, S), F32)],
        compiler_params=_cparams("parallel"),
    )(f3, bf)


def _fox_decay_bwd(dc_hs, f3, bf, *, name):
    B, S, _ = f3.shape

    def body(dc_ref, f_ref, b_ref, df_ref, db_ref):
        g = dc_ref[...].T
        row = lax.broadcasted_iota(jnp.int32, (S, LANES), 0)
        k = 1
        while k < S:
            g = g + jnp.where(row < S - k, pltpu.roll(g, S - k, 0), 0.0)
            k *= 2
        x = f_ref[...] + b_ref[...]
        df = g * (1.0 / (1.0 + jnp.exp(x)))
        df_ref[...] = df.astype(BF16)

        @pl.when(pl.program_id(0) == 0)
        def _():
            db_ref[...] = jnp.zeros_like(db_ref)

        db_ref[...] += jnp.sum(df, 0, keepdims=True)

    return pl.pallas_call(
        body, name=name, grid=(B,),
        in_specs=[pl.BlockSpec((None, LANES, S), lambda b: (b, 0, 0)),
                  pl.BlockSpec((None, S, LANES), lambda b: (b, 0, 0)), pl.BlockSpec((1, LANES), lambda b: (0, 0))],
        out_specs=[pl.BlockSpec((None, S, LANES), lambda b: (b, 0, 0)), pl.BlockSpec((1, LANES), lambda b: (0, 0))],
        out_shape=[jax.ShapeDtypeStruct((B, S, LANES), BF16), jax.ShapeDtypeStruct((1, LANES), F32)],
        compiler_params=_cparams("arbitrary"),
    )(dc_hs, f3, bf)


def _head_column(block, h):
    lane = lax.broadcasted_iota(jnp.int32, block.shape, 1)
    return jnp.sum(jnp.where(lane == h, block, 0.0), axis=-1, keepdims=True)


def _causal_mask(s):
    r = lax.broadcasted_iota(jnp.int32, s.shape, 0)
    c = lax.broadcasted_iota(jnp.int32, s.shape, 1)
    return jnp.where(c <= r, s, NEG_INF)


def _low_half(shape):
    return (lax.broadcasted_iota(jnp.int32, shape, 1) % LANES) < HEAD_DIM


def _widen(x, cols):
    return jnp.concatenate([x] * (cols // LANES), axis=1)


def _both_halves(x, lo):
    r = pltpu.roll(x, HEAD_DIM, 1)
    return jnp.where(lo, x, r), jnp.where(lo, r, x)


MESH_ID = pl.DeviceIdType.MESH
HBM_SPEC = pl.BlockSpec(memory_space=pltpu.HBM)
VMEM_SPEC = pl.BlockSpec(memory_space=pltpu.VMEM)


def _mesh_place():
    x, y, c = lax.axis_index("x"), lax.axis_index("y"), lax.axis_index("c")
    return x, y, c, 4 * x + 2 * y + c


def _peers(x, y, c):
    out = []
    for mask in range(1, N_DEV):
        dx, dy, dc = (mask >> 2) & 1, (mask >> 1) & 1, mask & 1
        px, py, pc = (1 - x if dx else x), (1 - y if dy else y), (1 - c if dc else c)
        out.append(((px, py, pc), 4 * px + 2 * py + pc))
    return out


def _comm_out_shapes(comm):
    return [jax.ShapeDtypeStruct(a.shape if kind == "scatter" else (N_DEV,) + a.shape[1:], a.dtype) for kind, a in comm]


def _comm_scratch(comm):
    n = len(comm)
    return [pltpu.SemaphoreType.DMA((n, 7)), pltpu.SemaphoreType.DMA((n, 7)), pltpu.SemaphoreType.DMA((n,))]


def _comm_copies(kinds, in_refs, out_refs, sems, place):
    send_sems, recv_sems, local_sems = sems
    x, y, c, me = place
    local, remote = [], []
    for w, kind in enumerate(kinds):
        mine = in_refs[w].at[me] if kind == "scatter" else in_refs[w].at[kind[1]]
        local.append(pltpu.make_async_copy(mine, out_refs[w].at[me], local_sems.at[w]))
        for k, (peer, peer_idx) in enumerate(_peers(x, y, c)):
            remote.append(pltpu.make_async_remote_copy(
                src_ref=in_refs[w].at[peer_idx] if kind == "scatter" else mine, dst_ref=out_refs[w].at[me],
                send_sem=send_sems.at[w, k], recv_sem=recv_sems.at[w, k], device_id=peer, device_id_type=MESH_ID))
    return local, remote


def _comm_start(kinds, in_refs, out_refs, sems, place):
    local, remote = _comm_copies(kinds, in_refs, out_refs, sems, place)
    for cp in local + remote:
        cp.start()


def _comm_wait(kinds, in_refs, out_refs, sems, place):
    local, remote = _comm_copies(kinds, in_refs, out_refs, sems, place)
    for cp in remote:
        cp.wait_recv()
    for cp in remote:
        cp.wait_send()
    for cp in local:
        cp.wait()


def _exchange(comm, *, name):
    n = len(comm)
    kinds = [k for k, _ in comm]

    def body(*refs):
        place = _mesh_place()
        _comm_start(kinds, refs[:n], refs[n:2 * n], refs[2 * n:], place)
        _comm_wait(kinds, refs[:n], refs[n:2 * n], refs[2 * n:], place)

    return pl.pallas_call(
        body, name=name, out_shape=_comm_out_shapes(comm), in_specs=[HBM_SPEC] * n, out_specs=[HBM_SPEC] * n,
        scratch_shapes=_comm_scratch(comm),
    )(*[a for _, a in comm])


def _flash_fwd(qa, ka, va, *, q_blk0, k_blk0, v_blk0, W, n_pairs, B, S, scale, csh=None, crow=None, comm=(), name):
    t = ATT_TILE
    nq = S // t
    P = PAIRS_PER_STEP_FWD
    decay = csh is not None
    split = W == LANES
    assert n_pairs % P == 0 and q_blk0 % P == 0 and k_blk0 % P == 0 and v_blk0 % P == 0
    n_c, kinds = len(comm), [k for k, _ in comm]
    n_in = 5 if decay else 3
    fold_scale = math.log2(scale).is_integer()
    n_steps = (B, n_pairs // P, nq)

    def body(*refs):
        c_in, c_out = refs[n_in:n_in + n_c], refs[n_in + n_c + 2:n_in + 2 * n_c + 2]
        sems = refs[n_in + 2 * n_c + 4:]
        refs = refs[:n_in] + refs[n_in + n_c:n_in + n_c + 2] + refs[n_in + 2 * n_c + 2:n_in + 2 * n_c + 4]
        if decay:
            q_ref, k_ref, v_ref, csh_ref, crow_ref, o_ref, lse_ref, m_s, acc_s = refs
        else:
            q_ref, k_ref, v_ref, o_ref, lse_ref, m_s, acc_s = refs
        g, i = pl.program_id(1), pl.program_id(2)
        if n_c:
            place = _mesh_place()
            ids = [pl.program_id(ax) for ax in range(3)]

            @pl.when((ids[0] == 0) & (ids[1] == 0) & (ids[2] == 0))
            def _():
                _comm_start(kinds, c_in, c_out, sems, place)

        lo = _low_half((t, LANES))
        qv = q_ref[...]
        qh = []
        for pr in range(P):
            qp = qv[:, pr * W:(pr + 1) * W]
            qh += [jnp.where(lo, qp, jnp.zeros_like(qp)), jnp.where(lo, jnp.zeros_like(qp), qp)] if split \
                else [qp[:, :LANES], qp[:, LANES:]]
        if fold_scale:
            qh = [x * scale for x in qh]
        if decay:
            cq = [jnp.broadcast_to(_head_column(csh_ref[...], 2 * P * g + hd), (t, LANES)) for hd in range(2 * P)]
        m_s[...] = jnp.full(m_s.shape, NEG_INF, F32)
        acc_s[...] = jnp.zeros(acc_s.shape, F32)

        def step(j, masked):
            rows = pl.ds(pl.multiple_of(j * t, t), t)
            kb, vb = k_ref[rows, :], v_ref[rows, :]
            for pr in range(P):
                kp, vp = kb[:, pr * W:(pr + 1) * W], vb[:, pr * LANES:(pr + 1) * LANES]
                ones = jnp.ones_like(vp)
                vaug = [jnp.where(lo, vp, ones), jnp.where(lo, ones, vp)]
                for half in range(2):
                    hd = 2 * pr + half
                    kh = kp if split else kp[:, half * LANES:(half + 1) * LANES]
                    s = lax.dot_general(qh[hd], kh, NT, preferred_element_type=F32)
                    if not fold_scale:
                        s = s * scale
                    if decay:
                        s = s + _widen(cq[hd], t) - crow_ref[hd, j]
                    if masked:
                        s = _causal_mask(s)
                    m_prev = m_s[hd]
                    m_new = jnp.maximum(m_prev, jnp.max(s, -1, keepdims=True))
                    p = jnp.exp(s - _widen(m_new, t))
                    acc_s[hd] = jnp.exp(m_prev - m_new) * acc_s[hd] + lax.dot_general(
                        p.astype(BF16), vaug[half], NN, preferred_element_type=F32)
                    m_s[hd] = m_new

        def loop_body(j, carry):
            step(j, False)
            return carry

        lax.fori_loop(0, i, loop_body, 0)
        step(i, True)
        for pr in range(P):
            acc0, acc1 = acc_s[2 * pr], acc_s[2 * pr + 1]
            _, l0 = _both_halves(acc0, lo)
            l1, _ = _both_halves(acc1, lo)
            cols = slice(pr * LANES, (pr + 1) * LANES)
            o_ref[:, cols] = jnp.where(lo, acc0 / l0, acc1 / l1).astype(BF16)
            lse_ref[:, cols] = jnp.where(lo, m_s[2 * pr] + jnp.log(l0), m_s[2 * pr + 1] + jnp.log(l1))
        if n_c:
            @pl.when((ids[0] == n_steps[0] - 1) & (ids[1] == n_steps[1] - 1) & (ids[2] == n_steps[2] - 1))
            def _():
                _comm_wait(kinds, c_in, c_out, sems, place)

    in_specs = [pl.BlockSpec((t, P * W), lambda b, g, i: (b * nq + i, q_blk0 // P + g)),
                pl.BlockSpec((S, P * W), lambda b, g, i: (b, k_blk0 // P + g)),
                pl.BlockSpec((S, P * LANES), lambda b, g, i: (b, v_blk0 // P + g))]
    args = [qa, ka, va]
    if decay:
        in_specs += [pl.BlockSpec((None, t, LANES), lambda b, g, i: (b, i, 0)),
                     pl.BlockSpec((None, 2 * P, nq, 1, t), lambda b, g, i: (b, g, 0, 0, 0))]
        args += [csh, crow]
    out_spec = pl.BlockSpec((t, P * LANES), lambda b, g, i: (b * nq + i, g))
    res = pl.pallas_call(
        body, name=name, grid=n_steps, in_specs=in_specs + [HBM_SPEC] * n_c,
        out_specs=[out_spec, out_spec] + [HBM_SPEC] * n_c,
        out_shape=[jax.ShapeDtypeStruct((B * S, n_pairs * LANES), BF16),
                   jax.ShapeDtypeStruct((B * S, n_pairs * LANES), F32)] + _comm_out_shapes(comm),
        scratch_shapes=[pltpu.VMEM((2 * P, t, LANES), F32), pltpu.VMEM((2 * P, t, LANES), F32)]
        + (_comm_scratch(comm) if n_c else []),
        compiler_params=_cparams(*(("arbitrary",) * 3 if n_c else ("parallel",) * 3)),
    )(*args, *[a for _, a in comm])
    return res[0], res[1], list(res[2:])


def _flash_bwd(qa, ka, va, oa, doa, lsea, *, q_blk0, k_blk0, v_blk0, do_blk0, W, n_pairs, B, S, scale, qk_dtype,
               csh=None, crow=None, comm=(), name):
    t = ATT_TILE_BWD
    nq = S // t
    P = PAIRS_PER_STEP_BWD
    decay = csh is not None
    if decay:
        crow = crow.reshape(B, 2 * n_pairs, nq, 1, t)
    split = W == LANES
    assert n_pairs % P == 0 and q_blk0 % P == 0 and k_blk0 % P == 0 and v_blk0 % P == 0 and do_blk0 % P == 0
    n_c, kinds = len(comm), [k for k, _ in comm]
    n_in, n_out, n_scr = (8, 5, 8) if decay else (6, 3, 5)
    n_steps = (B, n_pairs // P, nq)

    def body(*refs):
        c_in = refs[n_in:n_in + n_c]
        c_out = refs[n_in + n_c + n_out:n_in + 2 * n_c + n_out]
        sems = refs[n_in + 2 * n_c + n_out + n_scr:]
        refs = (refs[:n_in] + refs[n_in + n_c:n_in + n_c + n_out]
                + refs[n_in + 2 * n_c + n_out:n_in + 2 * n_c + n_out + n_scr])
        if n_c:
            place = _mesh_place()
            ids = [pl.program_id(ax) for ax in range(3)]

            @pl.when((ids[0] == 0) & (ids[1] == 0) & (ids[2] == 0))
            def _():
                _comm_start(kinds, c_in, c_out, sems, place)

        if decay:
            (q_ref, k_ref, v_ref, o_ref, do_ref, lse_ref, csh_ref, crow_ref, dq_ref, dk_ref, dv_ref, dck_ref, dcq_ref,
             dq_s, lse_s, delta_s, dk_s, dv_s, cq_s, dcq_s, dck_s) = refs
        else:
            (q_ref, k_ref, v_ref, o_ref, do_ref, lse_ref, dq_ref, dk_ref, dv_ref,
             dq_s, lse_s, delta_s, dk_s, dv_s) = refs
        g, j = pl.program_id(1), pl.program_id(2)
        lo = _low_half((t, LANES))

        @pl.when(j == 0)
        def _():
            lo_s = _low_half((S, LANES))
            dq_s[...] = jnp.zeros(dq_s.shape, F32)
            for pr in range(P):
                cols = slice(pr * LANES, (pr + 1) * LANES)
                lse_s[2 * pr], lse_s[2 * pr + 1] = _both_halves(lse_ref[:, cols], lo_s)
                dd = do_ref[:, cols].astype(F32) * o_ref[:, cols].astype(F32)
                delta_s[2 * pr] = jnp.broadcast_to(jnp.sum(jnp.where(lo_s, dd, 0.0), -1, keepdims=True), (S, LANES))
                delta_s[2 * pr + 1] = jnp.broadcast_to(jnp.sum(jnp.where(lo_s, 0.0, dd), -1, keepdims=True),
                                                       (S, LANES))
            if decay:
                for hd in range(2 * P):
                    cq_s[hd] = jnp.broadcast_to(_head_column(csh_ref[...], 2 * P * g + hd), (S, LANES))
                dcq_s[...] = jnp.zeros(dcq_s.shape, F32)

        kb, vb = k_ref[...], v_ref[...]
        kh, vh = [], []
        for pr in range(P):
            kp, vp = kb[:, pr * W:(pr + 1) * W], vb[:, pr * LANES:(pr + 1) * LANES]
            zk, zv = jnp.zeros_like(kp), jnp.zeros_like(vp)
            kh += [jnp.where(lo, kp, zk), jnp.where(lo, zk, kp)] if split else [kp[:, :LANES], kp[:, LANES:]]
            vh += [jnp.where(lo, vp, zv), jnp.where(lo, zv, vp)]
        dk_s[...] = jnp.zeros(dk_s.shape, F32)
        dv_s[...] = jnp.zeros(dv_s.shape, F32)
        if decay:
            dck_s[...] = jnp.zeros(dck_s.shape, F32)

        def step(i, masked):
            rows = pl.ds(pl.multiple_of(i * t, t), t)
            qi, doi = q_ref[rows, :], do_ref[rows, :]
            for pr in range(P):
                qp, dop = qi[:, pr * W:(pr + 1) * W], doi[:, pr * LANES:(pr + 1) * LANES]
                for half in range(2):
                    hd = 2 * pr + half
                    qx = qp if split else qp[:, half * LANES:(half + 1) * LANES]
                    s = lax.dot_general(qx, kh[hd], NT, preferred_element_type=F32) * scale
                    if decay:
                        s = s + _widen(cq_s[hd, rows, :], t) - crow_ref[hd, j]
                    if masked:
                        s = _causal_mask(s)
                    p = jnp.exp(s - _widen(lse_s[hd, rows, :], t))
                    dv_s[hd] += lax.dot_general(p.astype(BF16), dop, TN, preferred_element_type=F32)
                    dp = lax.dot_general(dop, vh[hd], NT, preferred_element_type=F32)
                    ds = p * (dp - _widen(delta_s[hd, rows, :], t))
                    dss = (ds * scale).astype(BF16)
                    dk_s[hd] += lax.dot_general(dss, qx, TN, preferred_element_type=F32)
                    dqc = lax.dot_general(dss, kh[hd], NN, preferred_element_type=F32)
                    if split:
                        dq_s[rows, pr * W:(pr + 1) * W] += dqc
                    else:
                        dq_s[rows, hd * LANES:(hd + 1) * LANES] += dqc
                    if decay:
                        dck_s[hd] -= jnp.sum(ds, 0, keepdims=True)
                        part = ds[:, :LANES]
                        for c in range(1, t // LANES):
                            part = part + ds[:, c * LANES:(c + 1) * LANES]
                        dcq_s[hd, rows, :] += part

        def loop_body(i, carry):
            step(i, False)
            return carry

        step(j, True)
        lax.fori_loop(j + 1, nq, loop_body, 0)
        for pr in range(P):
            if split:
                dk_ref[:, pr * W:(pr + 1) * W] = jnp.where(lo, dk_s[2 * pr], dk_s[2 * pr + 1]).astype(dk_ref.dtype)
            else:
                for half in range(2):
                    hd = 2 * pr + half
                    dk_ref[:, hd * LANES:(hd + 1) * LANES] = dk_s[hd].astype(dk_ref.dtype)
            dv_ref[:, pr * LANES:(pr + 1) * LANES] = jnp.where(lo, dv_s[2 * pr], dv_s[2 * pr + 1]).astype(BF16)
        if decay:
            dck_ref[...] = dck_s[...]

        @pl.when(j == nq - 1)
        def _():
            dq_ref[...] = dq_s[...].astype(dq_ref.dtype)
            if decay:
                for hd in range(2 * P):
                    dcq_ref[hd] = jnp.sum(dcq_s[hd].T, 0, keepdims=True)

        if n_c:
            @pl.when((ids[0] == n_steps[0] - 1) & (ids[1] == n_steps[1] - 1) & (ids[2] == n_steps[2] - 1))
            def _():
                _comm_wait(kinds, c_in, c_out, sems, place)

    full = lambda w, blk0: pl.BlockSpec((S, P * w), lambda b, g, j: (b, blk0 // P + g))
    blk = lambda w, blk0: pl.BlockSpec((t, P * w), lambda b, g, j: (b * nq + j, blk0 // P + g))
    in_specs = [full(W, q_blk0), blk(W, k_blk0), blk(LANES, v_blk0), full(LANES, 0), full(LANES, do_blk0),
                full(LANES, 0)]
    args = [qa, ka, va, oa, doa, lsea]
    T = B * S
    out_specs = [full(W, 0), blk(W, 0), blk(LANES, 0)]
    out_shape = [jax.ShapeDtypeStruct((T, n_pairs * W), qk_dtype), jax.ShapeDtypeStruct((T, n_pairs * W), qk_dtype),
                 jax.ShapeDtypeStruct((T, n_pairs * LANES), BF16)]
    per_head = lambda rows: pltpu.VMEM((2 * P, rows, LANES), F32)
    scratch = [pltpu.VMEM((S, P * W), F32), per_head(S), per_head(S), per_head(t), per_head(t)]
    if decay:
        in_specs += [pl.BlockSpec((None, S, LANES), lambda b, g, j: (b, 0, 0)),
                     pl.BlockSpec((None, 2 * P, nq, 1, t), lambda b, g, j: (b, g, 0, 0, 0))]
        args += [csh, crow]
        out_specs += [pl.BlockSpec((None, 2 * P, None, 1, t), lambda b, g, j: (b, g, j, 0, 0)),
                      pl.BlockSpec((None, 2 * P, 1, S), lambda b, g, j: (b, g, 0, 0))]
        out_shape += [jax.ShapeDtypeStruct((B, 2 * n_pairs, nq, 1, t), F32),
                      jax.ShapeDtypeStruct((B, 2 * n_pairs, 1, S), F32)]
        scratch += [per_head(S), per_head(S), pltpu.VMEM((2 * P, 1, t), F32)]
    res = pl.pallas_call(
        body, name=name, grid=n_steps, in_specs=in_specs + [HBM_SPEC] * n_c,
        out_specs=out_specs + [HBM_SPEC] * n_c, out_shape=out_shape + _comm_out_shapes(comm),
        scratch_shapes=scratch + (_comm_scratch(comm) if n_c else []),
        compiler_params=_cparams(*(("arbitrary",) * 3 if n_c else ("parallel", "parallel", "arbitrary"))),
    )(*args, *[a for _, a in comm])
    return tuple(res[:n_out]) + (list(res[n_out:]),)


def _swa_common(q_ref, kp_ref, ko_ref, vp_ref, vo_ref, n):
    Q = BLOCK_Q
    lo = _low_half((Q, LANES))
    lo2 = _low_half((2 * Q, LANES))
    kk = jnp.concatenate([kp_ref[...], ko_ref[...]], axis=0)
    vv = jnp.concatenate([vp_ref[...], vo_ref[...]], axis=0)
    kdup = [x.astype(BF16) for x in _both_halves(kk, lo2)]
    vdup = [x.astype(BF16) for x in _both_halves(vv, lo2)]
    a = lax.broadcasted_iota(jnp.int32, (SWA_GROUP * Q, 2 * Q), 0) % Q
    col = lax.broadcasted_iota(jnp.int32, (SWA_GROUP * Q, 2 * Q), 1)
    dist = a + Q - col
    valid = (dist >= 0) & (dist < SWA_WINDOW) & ((col >= Q) | (n > 0))
    qv = q_ref[...]
    qm = []
    for a_head in range(SWA_HEADS):
        qp = qv[:, (a_head // 2) * LANES:(a_head // 2 + 1) * LANES]
        keep = lo if a_head % 2 == 0 else jnp.logical_not(lo)
        qm.append(jnp.where(keep, qp, 0.0).astype(BF16))
    qs = [jnp.concatenate(qm[g * SWA_GROUP:(g + 1) * SWA_GROUP], axis=0) for g in range(SWA_KV_HEADS)]
    return lo, lo2, kdup, vdup, valid, qs


def _swa_group_logits(g, qs, kdup, valid, bias_ref):
    heads = slice(g * SWA_GROUP, (g + 1) * SWA_GROUP)
    s = lax.dot_general(qs[g], kdup[g], NT, preferred_element_type=F32) * (HEAD_DIM ** -0.5)
    s = s + bias_ref[heads].reshape(SWA_GROUP * BLOCK_Q, 2 * BLOCK_Q)
    return heads, jnp.where(valid, s, NEG_INF)


def _pair_halves(x, lo):
    Q = BLOCK_Q
    return [jnp.where(lo, x[2 * pr * Q:(2 * pr + 1) * Q], x[(2 * pr + 1) * Q:(2 * pr + 2) * Q])
            for pr in range(SWA_GROUP // 2)]


def _swa_in_specs(nb):
    Q = BLOCK_Q
    own = lambda blk: (lambda b, n: (b * nb + n, blk))
    prev = lambda blk: (lambda b, n: (b * nb + jnp.maximum(n - 1, 0), blk))
    kb, vb = EV_KS[0] // LANES, EV_VS[0] // LANES
    return [pl.BlockSpec((Q, SWA_HEADS * HEAD_DIM), own(0)), pl.BlockSpec((Q, LANES), prev(kb)),
            pl.BlockSpec((Q, LANES), own(kb)), pl.BlockSpec((Q, LANES), prev(vb)), pl.BlockSpec((Q, LANES), own(vb))]


def _swa_fwd(h, bias, sinkcol, *, B, S, comm=(), name):
    Q = BLOCK_Q
    nb = S // Q
    n_c, kinds = len(comm), [k for k, _ in comm]

    def body(*refs):
        c_in, c_out, sems = refs[7:7 + n_c], refs[9 + n_c:9 + 2 * n_c], refs[9 + 2 * n_c:]
        q_ref, kp_ref, ko_ref, vp_ref, vo_ref, bias_ref, sink_ref = refs[:7]
        o_ref, lse_ref = refs[7 + n_c:9 + n_c]
        if n_c:
            place = _mesh_place()
            ids = [pl.program_id(0), pl.program_id(1)]

            @pl.when((ids[0] == 0) & (ids[1] == 0))
            def _():
                _comm_start(kinds, c_in, c_out, sems, place)

        lo, lo2, kdup, vdup, valid, qs = _swa_common(q_ref, kp_ref, ko_ref, vp_ref, vo_ref, pl.program_id(1))
        pairs = []
        lo4 = _low_half((SWA_GROUP * Q, LANES))
        for g in range(SWA_KV_HEADS):
            heads, s = _swa_group_logits(g, qs, kdup, valid, bias_ref)
            sink = jnp.broadcast_to(sink_ref[heads].reshape(SWA_GROUP * Q, 1), (SWA_GROUP * Q, LANES))
            m = jnp.maximum(jnp.max(s, -1, keepdims=True), sink)
            p = jnp.exp(s - _widen(m, 2 * Q))
            vaug = jnp.where(lo2, vdup[g], jnp.ones_like(vdup[g]))
            pv = lax.dot_general(p.astype(BF16), vaug, NN, preferred_element_type=F32)
            rolled = pltpu.roll(pv, HEAD_DIM, 1)
            l = jnp.where(lo4, rolled, pv) + jnp.exp(sink - m)
            out = pv / l
            lse_g = m + jnp.log(l)
            for i in range(SWA_GROUP):
                a = g * SWA_GROUP + i
                lse_ref[:, a * LANES:(a + 1) * LANES] = lse_g[i * Q:(i + 1) * Q]
            shifted = pltpu.roll(out, HEAD_DIM, 1)
            pairs += [jnp.where(lo, out[2 * pr * Q:(2 * pr + 1) * Q], shifted[(2 * pr + 1) * Q:(2 * pr + 2) * Q])
                      for pr in range(SWA_GROUP // 2)]
        o_ref[...] = jnp.concatenate(pairs, axis=1).astype(BF16)
        if n_c:
            @pl.when((ids[0] == B - 1) & (ids[1] == nb - 1))
            def _():
                _comm_wait(kinds, c_in, c_out, sems, place)

    whole = lambda shape: pl.BlockSpec(shape, lambda b, n: (0,) * len(shape))
    res = pl.pallas_call(
        body, name=name, grid=(B, nb),
        in_specs=_swa_in_specs(nb) + [whole((SWA_HEADS, Q, 2 * Q)), whole((SWA_HEADS, Q, 1))] + [HBM_SPEC] * n_c,
        out_specs=[pl.BlockSpec((Q, SWA_HEADS * HEAD_DIM), lambda b, n: (b * nb + n, 0)),
                   pl.BlockSpec((Q, SWA_HEADS * LANES), lambda b, n: (b * nb + n, 0))] + [HBM_SPEC] * n_c,
        out_shape=[jax.ShapeDtypeStruct((B * S, SWA_HEADS * HEAD_DIM), BF16),
                   jax.ShapeDtypeStruct((B * S, SWA_HEADS * LANES), F32)] + _comm_out_shapes(comm),
        scratch_shapes=_comm_scratch(comm) if n_c else [],
        compiler_params=_cparams(*(("arbitrary",) * 2 if n_c else ("parallel",) * 2)),
    )(h, h, h, h, h, bias, sinkcol, *[a for _, a in comm])
    return res[0], res[1], list(res[2:])


def _swa_bwd(h, o, do, lse, bias, sinkcol, *, do_blk0, B, S, name):
    Q = BLOCK_Q
    nb = S // Q
    scale = HEAD_DIM ** -0.5

    def body(q_ref, kp_ref, ko_ref, vp_ref, vo_ref, o_ref, do_ref, lse_ref, bias_ref, sink_ref,
             dq_ref, dko_ref, dkp_ref, dvo_ref, dvp_ref, dbias_ref, dsink_ref):
        @pl.when((pl.program_id(0) == 0) & (pl.program_id(1) == 0))
        def _():
            dbias_ref[...] = jnp.zeros_like(dbias_ref)
            dsink_ref[...] = jnp.zeros_like(dsink_ref)

        lo, lo2, kdup, vdup, valid, qs = _swa_common(q_ref, kp_ref, ko_ref, vp_ref, vo_ref, pl.program_id(1))
        dkk, dvv, dq_pairs = [], [], []
        for g in range(SWA_KV_HEADS):
            heads, s = _swa_group_logits(g, qs, kdup, valid, bias_ref)
            lse_g = jnp.concatenate([lse_ref[:, a * LANES:(a + 1) * LANES]
                                     for a in range(g * SWA_GROUP, (g + 1) * SWA_GROUP)], axis=0)
            p = jnp.exp(s - _widen(lse_g, 2 * Q))
            do_g, o_g = [], []
            for i in range(SWA_GROUP):
                cols = slice((g * SWA_GROUP + i) // 2 * LANES, ((g * SWA_GROUP + i) // 2 + 1) * LANES)
                do_p = do_ref[:, cols]
                do_g.append(jnp.where(lo if i % 2 == 0 else jnp.logical_not(lo), do_p, jnp.zeros_like(do_p)))
                o_g.append(o_ref[:, cols])
            doh, oh = jnp.concatenate(do_g, axis=0), jnp.concatenate(o_g, axis=0)
            delta = jnp.sum(doh.astype(F32) * oh.astype(F32), -1, keepdims=True)
            dp = lax.dot_general(doh, vdup[g], NT, preferred_element_type=F32)
            ds = p * (dp - delta)
            dbias_ref[heads] += ds.reshape(SWA_GROUP, Q, 2 * Q)
            dsink_ref[heads] -= (jnp.exp(sink_ref[heads].reshape(SWA_GROUP * Q, 1) - lse_g[:, :1])
                                 * delta).reshape(SWA_GROUP, Q, 1)
            dss = (ds * scale).astype(BF16)
            dq_pairs += _pair_halves(lax.dot_general(dss, kdup[g], NN, preferred_element_type=F32), lo)
            dkk.append(lax.dot_general(dss, qs[g], TN, preferred_element_type=F32))
            dvv.append(lax.dot_general(p.astype(BF16), doh, TN, preferred_element_type=F32))
        dq_ref[...] = jnp.concatenate(dq_pairs, axis=1).astype(BF16)
        fold = lambda x: x + pltpu.roll(x, HEAD_DIM, 1)
        dk_blk = jnp.where(lo2, fold(dkk[0]), fold(dkk[1]))
        dv_blk = jnp.where(lo2, fold(dvv[0]), fold(dvv[1]))
        dkp_ref[...] = dk_blk[:Q]
        dko_ref[...] = dk_blk[Q:]
        dvp_ref[...] = dv_blk[:Q]
        dvo_ref[...] = dv_blk[Q:]

    whole = lambda shape: pl.BlockSpec(shape, lambda b, n: (0,) * len(shape))
    wide = lambda blk: pl.BlockSpec((Q, SWA_HEADS * HEAD_DIM), lambda b, n: (b * nb + n, blk))
    narrow = pl.BlockSpec((Q, LANES), lambda b, n: (b * nb + n, 0))
    kv_shape = jax.ShapeDtypeStruct((B * S, LANES), F32)
    return pl.pallas_call(
        body, name=name, grid=(B, nb),
        in_specs=_swa_in_specs(nb) + [wide(0), wide(do_blk0),
                                      pl.BlockSpec((Q, SWA_HEADS * LANES), lambda b, n: (b * nb + n, 0)),
                                      whole((SWA_HEADS, Q, 2 * Q)),
                                      whole((SWA_HEADS, Q, 1))],
        out_specs=[wide(0), narrow, narrow, narrow, narrow, whole((SWA_HEADS, Q, 2 * Q)), whole((SWA_HEADS, Q, 1))],
        out_shape=[jax.ShapeDtypeStruct((B * S, SWA_HEADS * HEAD_DIM), BF16), kv_shape, kv_shape, kv_shape, kv_shape,
                   jax.ShapeDtypeStruct((SWA_HEADS, Q, 2 * Q), F32), jax.ShapeDtypeStruct((SWA_HEADS, Q, 1), F32)],
        compiler_params=_cparams("arbitrary", "arbitrary"),
    )(h, h, h, h, h, o, do, lse, bias, sinkcol)


def _bias_bucket_sum(dbias, bucket, *, name):
    def body(d_ref, b_ref, o_ref):
        dbv, bk = d_ref[...], b_ref[...]
        lane = lax.broadcasted_iota(jnp.int32, (SWA_HEADS, LANES), 1)
        out = jnp.zeros((SWA_HEADS, LANES), F32)
        for b in range(REL_BUCKETS):
            part = jnp.sum(jnp.where(bk == b, dbv, 0.0), axis=1)
            tot = jnp.sum(part, axis=-1, keepdims=True)
            out = out + jnp.where(lane == b, tot, 0.0)
        o_ref[...] = out

    return pl.pallas_call(
        body, name=name, out_shape=jax.ShapeDtypeStruct((SWA_HEADS, LANES), F32),
        compiler_params=pltpu.CompilerParams(vmem_limit_bytes=VMEM_LIMIT_BYTES),
    )(dbias, bucket)


def _adamw_update(w, g, m, v):
    m_new = ADAM_B1 * m + (1.0 - ADAM_B1) * g
    v_new = ADAM_B2 * v + (1.0 - ADAM_B2) * jnp.square(g)
    m_hat = m_new / (1.0 - ADAM_B1 ** ADAM_STEP)
    v_hat = v_new / (1.0 - ADAM_B2 ** ADAM_STEP)
    return -ADAM_LR * (m_hat / (jnp.sqrt(v_hat) + ADAM_EPS) + ADAM_WD * w), m_new, v_new


def _adamw(w, g, m, v, *, name):
    def body(w_ref, g_ref, m_ref, v_ref, d_ref, nm_ref, nv_ref):
        d_ref[...], nm_ref[...], nv_ref[...] = _adamw_update(w_ref[...], g_ref[...], m_ref[...], v_ref[...])

    return pl.pallas_call(
        body, name=name, out_shape=[jax.ShapeDtypeStruct(w.shape, F32)] * 3,
        compiler_params=pltpu.CompilerParams(vmem_limit_bytes=VMEM_LIMIT_BYTES),
    )(w, g, m, v)


ADAMW_PARTS_BYTES = 8 * 1024 * 1024


def _adamw_slots(w, parts, m, v, *, name):
    n0, R, C = w.shape
    tr = next((c for c in (512, 256, 128, 64, 32, 16, 8) if R % c == 0 and 4 * n0 * N_DEV * c * C <= ADAMW_PARTS_BYTES), R)

    def body(*refs):
        w_ref, p_refs, (m_ref, v_ref, g_ref, d_ref, nm_ref, nv_ref) = refs[0], refs[1:1 + n0], refs[1 + n0:]
        layer = pl.program_id(0)
        for l in range(n0):
            @pl.when(layer == l)
            def _(p_ref=p_refs[l]):
                g = p_ref[0].astype(F32)
                for j in range(1, N_DEV):
                    g = g + p_ref[j].astype(F32)
                g_ref[...] = g
                d_ref[...], nm_ref[...], nv_ref[...] = _adamw_update(w_ref[...], g, m_ref[...], v_ref[...])

    spec = pl.BlockSpec((None, tr, C), lambda l, i: (l, i, 0))
    part_spec = lambda own: pl.BlockSpec((N_DEV, tr, C), lambda l, i: (0, jnp.where(l == own, i, 0), 0))
    return pl.pallas_call(
        body, name=name, grid=(n0, R // tr),
        in_specs=[spec] + [part_spec(l) for l in range(n0)] + [spec, spec], out_specs=[spec] * 4,
        out_shape=[jax.ShapeDtypeStruct((n0, R, C), F32)] * 4, compiler_params=_cparams("arbitrary", "arbitrary"),
    )(w, *parts, m, v)


def _all_gather_hbm(blocks, *, name):
    n = len(blocks)

    def body(*refs):
        x_refs, out_refs = refs[:n], refs[n:2 * n]
        send_sems, recv_sems, local_sems = refs[2 * n:]
        x, y, c, _ = _mesh_place()
        me, sibling = (x, y, c), (x, y, 1 - c)
        chips = [(1 - x, y), (x, 1 - y), (1 - x, 1 - y)]

        def copy(w, k, blk, to, src=None):
            px, py, pc = blk
            slot = out_refs[w].at[4 * px + 2 * py + pc]
            return pltpu.make_async_remote_copy(
                src_ref=slot if src is None else src, dst_ref=slot,
                send_sem=send_sems.at[w, k], recv_sem=recv_sems.at[w, k], device_id=to, device_id_type=MESH_ID)

        mine = [pltpu.make_async_copy(x_refs[w], out_refs[w].at[4 * x + 2 * y + c], local_sems.at[w])
                for w in range(n)]
        for cp in mine:
            cp.start()
        first = []
        for w in range(n):
            first.append(copy(w, 0, me, sibling, src=x_refs[w]))
            first += [copy(w, 1 + j, me, (*chip, c), src=x_refs[w]) for j, chip in enumerate(chips)]
        for cp in first:
            cp.start()
        passed = []
        for j, chip in enumerate(chips):
            for w in range(n):
                copy(w, 1 + j, (*chip, c), me).wait_recv()
                fwd = copy(w, 4 + j, (*chip, c), sibling)
                fwd.start()
                passed.append(fwd)
        for w in range(n):
            copy(w, 0, sibling, me).wait_recv()
            for j, chip in enumerate(chips):
                copy(w, 4 + j, (*chip, 1 - c), me).wait_recv()
        for cp in first + passed:
            cp.wait_send()
        for cp in mine:
            cp.wait()

    return pl.pallas_call(
        body, name=name, out_shape=[jax.ShapeDtypeStruct((N_DEV,) + b.shape, b.dtype) for b in blocks],
        in_specs=[HBM_SPEC] * n, out_specs=[HBM_SPEC] * n,
        scratch_shapes=[pltpu.SemaphoreType.DMA((n, 7)), pltpu.SemaphoreType.DMA((n, 7)),
                        pltpu.SemaphoreType.DMA((n,))],
    )(*blocks)


def _all_reduce_small(block, *, name):
    R, W = block.shape

    def body(x_ref, out_ref, buf, send_sems, recv_sems):
        x, y, c, me = _mesh_place()
        copies = []
        for k, (peer, _) in enumerate(_peers(x, y, c)):
            copies.append(pltpu.make_async_remote_copy(
                src_ref=x_ref, dst_ref=buf.at[me], send_sem=send_sems.at[k], recv_sem=recv_sems.at[k],
                device_id=peer, device_id_type=MESH_ID))
        for cp in copies:
            cp.start()
        buf[me] = x_ref[...]
        for cp in copies:
            cp.wait_recv()
        for cp in copies:
            cp.wait_send()
        acc = buf[0]
        for j in range(1, N_DEV):
            acc = acc + buf[j]
        out_ref[...] = acc

    return pl.pallas_call(
        body, name=name, out_shape=jax.ShapeDtypeStruct((R, W), F32),
        in_specs=[VMEM_SPEC], out_specs=VMEM_SPEC,
        scratch_shapes=[pltpu.VMEM((N_DEV, R, W), F32), pltpu.SemaphoreType.DMA((7,)), pltpu.SemaphoreType.DMA((7,))],
    )(block)


def _assemble(name, g):
    if BIG_AXIS[name] == 2 and g.shape[2] % LANES == 0:
        return g.transpose(1, 0, 2).reshape(g.shape[1], N_DEV * g.shape[2])
    if BIG_AXIS[name] == 2:
        return jnp.concatenate([g[j] for j in range(N_DEV)], axis=1)
    return g.reshape(N_DEV * g.shape[1], g.shape[2])


def _split_for_devices(name, g):
    if BIG_AXIS[name] == 2:
        b = g.shape[1] // N_DEV
        if b % LANES == 0:
            return g.astype(BF16).reshape(g.shape[0], N_DEV, b).transpose(1, 0, 2)
        return jnp.stack([g[:, j * b:(j + 1) * b] for j in range(N_DEV)]).astype(BF16)
    return g.reshape(N_DEV, g.shape[0] // N_DEV, g.shape[1]).astype(BF16)


def _layer_weight_keys(i):
    j = i // 2
    mixer = [('ev_w_in', j), ('ev_w_uq', j), ('ev_w_ukv', j), ('ev_w_out', j)] if i % 2 == 0 \
        else [('od_w_in', j), ('od_w_out', j)]
    return mixer + [('w_up', i), ('w_down', i), ('ple_w_proj', i), ('ple_w_gate', i)]


def _weight_layer(key):
    name, idx = key
    return 2 * idx if name.startswith('ev_') else 2 * idx + 1 if name.startswith('od_') else idx


FIRST_GATHER = [('ev_w_in', 0), ('ev_w_uq', 0), ('ev_w_ukv', 0), ('ev_w_out', 0)]
FWD_CARRIERS = {
    'l0_mla': [('w_up', 0), ('ple_w_proj', 0), ('ple_w_gate', 0)],
    'l0_swa': [('w_down', 0)],
    'l0_out_ln1': [('od_w_out', 0)],
    'l0_up': [('od_w_in', 0)],
    'l0_down_ln2': [('w_up', 1)],
    'l0_ple_gate': [('ple_w_proj', 1), ('ple_w_gate', 1)],
    'l1_fox': [('w_down', 1), ('ev_w_in', 1), ('ev_w_uq', 1), ('ev_w_ukv', 1), ('ev_w_out', 1), ('w_up', 2)],
    'l1_up': [('w_down', 2)],
    'l1_down_ln2': [('ple_w_proj', 2), ('ple_w_gate', 2)],
    'l2_mla': [('od_w_in', 1), ('od_w_out', 1)],
    'l2_swa': [('w_up', 3)],
    'l2_up': [('w_down', 3)],
    'l2_down_ln2': [('ple_w_proj', 3), ('ple_w_gate', 3)],
}


class _MeshExchange:
    def __init__(self, shards):
        self.shards = shards
        self.weights = {i: {} for i in range(DEPTH)}
        self.pending = []
        self.in_flight = []
        self.received = {}
        got = _all_gather_hbm([self.shards[n][idx] for n, idx in FIRST_GATHER], name="gather_first")
        self._landed(FIRST_GATHER, got)

    def _landed(self, keys, gathered):
        for k, g in zip(keys, gathered):
            self.weights[_weight_layer(k)][k[0]] = _assemble(k[0], g)

    def layer_weights(self, i):
        return self.weights[i]

    def carry(self, kernel_name):
        return [(("gather", idx), self.shards[n]) for n, idx in FWD_CARRIERS.get(kernel_name, [])]

    def carried(self, kernel_name, outs):
        self._landed(FWD_CARRIERS.get(kernel_name, []), outs)

    def push_grads(self, grads):
        self.pending += [(k, _split_for_devices(k[0], g)) for k, g in grads.items()]

    def bwd_items(self):
        self.in_flight, self.pending = self.pending, []
        return [("scatter", parts) for _, parts in self.in_flight]

    def bwd_done(self, outs):
        for (k, _), got in zip(self.in_flight, outs):
            self.received[k] = got
        self.in_flight = []

    def finish(self):
        if self.pending:
            outs = _exchange(self.bwd_items(), name="scatter_rest")
            self.bwd_done(outs)
        return self.received


PACK_ROWS = 8


def _pack_small(vals):
    flat = jnp.concatenate([vals[n].reshape(-1).astype(F32) for n in SMALL])
    pad = (-flat.shape[0]) % (PACK_ROWS * LANES)
    return jnp.pad(flat, (0, pad)).reshape(-1, LANES)


def _unpack_small(block, shapes):
    flat = block.reshape(-1)
    out, off = {}, 0
    for n in SMALL:
        sz = math.prod(shapes[n])
        out[n] = flat[off:off + sz].reshape(shapes[n])
        off += sz
    return out


def _rope_tables(S):
    half = MLA_ROPE // 2
    inv = 1.0 / (ROPE_THETA ** (jnp.arange(0, MLA_ROPE, 2, dtype=F32) / MLA_ROPE))
    ang = jnp.arange(S, dtype=F32)[:, None] * inv[None, :]
    cos, sin = jnp.cos(ang), jnp.sin(ang)
    zeros = jnp.zeros((S, half), F32)
    tail = jnp.zeros((S, LANES - MLA_QK), F32)

    def block(rope_part, nope_val):
        return jnp.concatenate([jnp.full((S, MLA_NOPE), nope_val, F32), rope_part, tail], -1)

    a_r = jnp.concatenate([cos, cos], -1)
    bm_r = jnp.concatenate([-sin, zeros], -1)
    bp_r = jnp.concatenate([zeros, sin], -1)
    q_tabs = tuple(block(r, v) for r, v in ((a_r, 1.0), (bm_r, 0.0), (bp_r, 0.0)))
    k_tabs = tuple(block(r, 0.0) for r in (a_r, bm_r, bp_r))
    return q_tabs, k_tabs


def _t5_bucket(dist):
    exact = REL_BUCKETS // 2
    d = jnp.maximum(dist, 1).astype(F32)
    large = exact + (jnp.log(d / exact) / math.log(REL_MAX_DIST / exact) * (REL_BUCKETS - exact)).astype(jnp.int32)
    large = jnp.minimum(large, REL_BUCKETS - 1)
    return jnp.where(dist < exact, dist, large)


def _swa_bucket_table():
    a = jnp.arange(BLOCK_Q)[:, None]
    col = jnp.arange(2 * BLOCK_Q)[None, :]
    return _t5_bucket(jnp.maximum(a + BLOCK_Q - col, 0)).astype(jnp.int32)


def _even_weights(W):
    w = W['ev_w_in']
    c_kv1 = MLA_Q_LORA + MLA_KV_LORA
    c_kr1 = c_kv1 + MLA_ROPE
    c_qs1 = c_kr1 + SWA_HEADS * HEAD_DIM
    zeros = lambda n: jnp.zeros((D_MODEL, n), w.dtype)
    w_in = jnp.concatenate([w[:, c_kr1:c_qs1], w[:, :c_kv1], w[:, c_qs1:], zeros(KR_LANE0), w[:, c_kv1:c_kr1],
                            zeros(LANES - KR_LANE0 - MLA_ROPE)], axis=1)
    uq = W['ev_w_uq'].reshape(MLA_Q_LORA, MLA_HEADS, MLA_QK)
    w_uq = jnp.pad(uq, ((0, 0), (0, 0), (0, LANES - MLA_QK))).reshape(MLA_Q_LORA, MLA_HEADS * LANES)
    ukv = W['ev_w_ukv'].reshape(MLA_KV_LORA, MLA_HEADS, MLA_NOPE + MLA_V)
    w_k = jnp.pad(ukv[..., :MLA_NOPE], ((0, 0), (0, 0), (0, LANES - MLA_NOPE))).reshape(MLA_KV_LORA, -1)
    w_v = ukv[..., MLA_NOPE:].reshape(MLA_KV_LORA, MLA_HEADS * MLA_V)
    return w_in, w_uq, w_k, w_v, W['ev_w_out']


def _even_in_grad_unpad(dw):
    kr0 = EV_KR[0] + KR_LANE0
    return jnp.concatenate([dw[:, EV_CQ[0]:EV_CKV[1]], dw[:, kr0:kr0 + MLA_ROPE], dw[:, EV_QS[0]:EV_QS[1]],
                            dw[:, EV_KS[0]:EV_VS[1]]], axis=1)


def _even_fwd(xb, W, P, i, B, S, tabs, xchg, tag):
    j = i // 2
    q_tabs, k_tabs, bias, sinkcol = tabs
    w_in, w_uq, w_k, w_v, w_out = _even_weights(W)
    h = _mm(xb, w_in, name=f"{tag}_in")
    cqn, ckvn, rq, rkv = _even_norms(h, P['ev_q_norm'][j][None], P['ev_kv_norm'][j][None], name=f"{tag}_norms")
    q = _rope(_mm(cqn, w_uq, name=f"{tag}_uq"), q_tabs, S, sign=1.0, name=f"{tag}_ropeq")
    knp = _mm(ckvn, w_k, out_dtypes=(BF16,), name=f"{tag}_uk")
    v = _mm(ckvn, w_v, out_dtypes=(BF16,), name=f"{tag}_uv")
    k = _mla_keys(knp, h, k_tabs, S, name=f"{tag}_keys")
    o_mla, lse_mla, got = _flash_fwd(q, k, v, q_blk0=0, k_blk0=0, v_blk0=0, W=2 * LANES, n_pairs=MLA_HEADS // 2,
                                     B=B, S=S, scale=MLA_QK ** -0.5, comm=xchg.carry(f"{tag}_mla"), name=f"{tag}_mla")
    xchg.carried(f"{tag}_mla", got)
    o_swa, lse_swa, got = _swa_fwd(h, bias, sinkcol, B=B, S=S, comm=xchg.carry(f"{tag}_swa"), name=f"{tag}_swa")
    xchg.carried(f"{tag}_swa", got)
    o_cat = jnp.concatenate([o_mla, o_swa], axis=-1)
    res = dict(h=h, cqn=cqn, ckvn=ckvn, rq=rq, rkv=rkv, q=q, k=k, v=v, o_mla=o_mla, lse_mla=lse_mla,
               o_swa=o_swa, lse_swa=lse_swa, o_cat=o_cat)
    return (o_cat, w_out), res


def _shift_prev(own, prev, B, S):
    prev = prev.reshape(B, S, LANES)
    shifted = jnp.concatenate([prev[:, BLOCK_Q:], jnp.zeros_like(prev[:, :BLOCK_Q])], axis=1)
    return (own + shifted.reshape(B * S, LANES)).astype(BF16)


def _even_bwd(dmb, dz1, xb, W, P, j, B, S, tabs, res, xchg, tag):
    q_tabs, k_tabs, bias, sinkcol = tabs
    w_in, w_uq, w_k, w_v, w_out = _even_weights(W)
    g = {}
    g['ev_w_out'] = _mm_tn(res['o_cat'], dmb, name=f"{tag}_dwout")
    do = _mm(dmb, w_out, trans_b=True, out_dtypes=(BF16,), name=f"{tag}_do")
    dq, dk, dv, got = _flash_bwd(res['q'], res['k'], res['v'], res['o_mla'], do, res['lse_mla'], q_blk0=0, k_blk0=0,
                                 v_blk0=0, do_blk0=0, W=2 * LANES, n_pairs=MLA_HEADS // 2, B=B, S=S,
                                 scale=MLA_QK ** -0.5, qk_dtype=F32, comm=xchg.bwd_items(), name=f"{tag}_mla_bwd")
    xchg.bwd_done(got)
    dq_pre = _rope(dq, q_tabs, S, sign=-1.0, name=f"{tag}_ropeq_bwd")
    dw_uq = _mm_tn(res['cqn'], dq_pre, name=f"{tag}_dwuq")
    g['ev_w_uq'] = dw_uq.reshape(MLA_Q_LORA, MLA_HEADS, LANES)[..., :MLA_QK].reshape(MLA_Q_LORA, MLA_HEADS * MLA_QK)
    dcqn = _mm(dq_pre, w_uq, trans_b=True, name=f"{tag}_dcqn")
    dw_k = _mm_tn(res['ckvn'], dk, name=f"{tag}_dwuk").reshape(MLA_KV_LORA, MLA_HEADS, LANES)[..., :MLA_NOPE]
    dw_v = _mm_tn(res['ckvn'], dv, name=f"{tag}_dwuv").reshape(MLA_KV_LORA, MLA_HEADS, MLA_V)
    g['ev_w_ukv'] = jnp.concatenate([dw_k, dw_v], axis=-1).reshape(MLA_KV_LORA, MLA_HEADS * (MLA_NOPE + MLA_V))
    dckvn_v = _mm(dv, w_v, trans_b=True, name=f"{tag}_dckvn_v")
    dckvn = _mm(dk, w_k, trans_b=True, extras=(dckvn_v,), epilogue=lambda acc, r: (acc + r,), name=f"{tag}_dckvn")
    dkr_pre = _mla_rope_key_grad(dk, k_tabs, S, name=f"{tag}_ropek_bwd")
    dqs, dko, dkp, dvo, dvp, dbias, dsink = _swa_bwd(res['h'], res['o_swa'], do, res['lse_swa'], bias, sinkcol,
                                                     do_blk0=1, B=B, S=S, name=f"{tag}_swa_bwd")
    dh, dgq, dgkv = _even_in_bwd(res['h'], res['rq'], res['rkv'], P['ev_q_norm'][j][None], P['ev_kv_norm'][j][None],
                                 dcqn, dckvn, dqs, _shift_prev(dko, dkp, B, S), _shift_prev(dvo, dvp, B, S), dkr_pre,
                                 name=f"{tag}_in_bwd")
    g['ev_w_in'] = _even_in_grad_unpad(_mm_tn(xb, dh, name=f"{tag}_dwin"))
    xchg.push_grads({(n, j): val for n, val in g.items()})
    dx_kwargs = dict(trans_b=True, extras=(dz1,), epilogue=lambda acc, r: (acc + DN_ALPHA * r,), name=f"{tag}_dx")
    dx = _scattering(xchg, _mm, dh, w_in, **dx_kwargs) if j == 0 else _mm(dh, w_in, **dx_kwargs)
    small = dict(ev_q_norm=dgq[0], ev_kv_norm=dgkv[0], dbias=dbias, ev_sinks=jnp.sum(dsink, axis=(1, 2)))
    return dx, small


def _odd_fwd(xb, W, P, i, B, S, xchg, tag):
    j = i // 2
    w = W['od_w_in']
    w_qkv = w[:, :ODD_QKV]
    w_f = jnp.pad(w[:, ODD_QKV:], ((0, 0), (0, LANES - FOX_HEADS)))
    bf = jnp.pad(P['od_b_f'][j], (0, LANES - FOX_HEADS))[None]
    qkv = _mm(xb, w_qkv, out_dtypes=(BF16,), name=f"{tag}_qkv")
    f = _mm(xb, w_f, name=f"{tag}_f").reshape(B, S, LANES)
    csh, chs = _fox_decay_fwd(f, bf, name=f"{tag}_decay")
    crow = chs[:, :FOX_HEADS].reshape(B, FOX_HEADS, S // ATT_TILE, 1, ATT_TILE)
    n_blk = FOX_HEADS * HEAD_DIM // LANES
    o, lse, got = _flash_fwd(qkv, qkv, qkv, q_blk0=0, k_blk0=n_blk, v_blk0=2 * n_blk, W=LANES,
                             n_pairs=FOX_HEADS // 2, B=B, S=S, scale=HEAD_DIM ** -0.5, csh=csh, crow=crow,
                             comm=xchg.carry(f"{tag}_fox"), name=f"{tag}_fox")
    xchg.carried(f"{tag}_fox", got)
    res = dict(f=f, bf=bf, csh=csh, crow=crow, qkv=qkv, o=o, lse=lse, w_qkv=w_qkv, w_f=w_f)
    return (o, W['od_w_out']), res


def _odd_bwd(dmb, dz1, xb, W, P, j, B, S, res, xchg, tag):
    g = {}
    w_out = W['od_w_out']
    g['od_w_out'] = _mm_tn(res['o'], dmb, name=f"{tag}_dwout")
    do = _mm(dmb, w_out, trans_b=True, out_dtypes=(BF16,), name=f"{tag}_do")
    qkv = res['qkv']
    n_blk = FOX_HEADS * HEAD_DIM // LANES
    dq, dk, dv, dck, dcq, got = _flash_bwd(qkv, qkv, qkv, res['o'], do, res['lse'], q_blk0=0, k_blk0=n_blk,
                                           v_blk0=2 * n_blk, do_blk0=0, W=LANES, n_pairs=FOX_HEADS // 2, B=B, S=S,
                                           scale=HEAD_DIM ** -0.5, qk_dtype=BF16, csh=res['csh'], crow=res['crow'],
                                           comm=xchg.bwd_items(), name=f"{tag}_fox_bwd")
    xchg.bwd_done(got)
    dc = dck.reshape(B, FOX_HEADS, S) + dcq.reshape(B, FOX_HEADS, S)
    dc_hs = jnp.pad(dc, ((0, 0), (0, LANES - FOX_HEADS), (0, 0)))
    df, dbf = _fox_decay_bwd(dc_hs, res['f'], res['bf'], name=f"{tag}_decay_bwd")
    df = df.reshape(B * S, LANES)
    dqkv = jnp.concatenate([dq, dk, dv], axis=-1)
    dw_qkv = _mm_tn(xb, dqkv, name=f"{tag}_dwqkv")
    dw_f = _mm_tn(xb, df, name=f"{tag}_dwf")
    g['od_w_in'] = jnp.concatenate([dw_qkv, dw_f[:, :FOX_HEADS]], axis=1)
    dxf = _mm(df, res['w_f'], trans_b=True, extras=(dz1,), epilogue=lambda acc, r: (acc + DN_ALPHA * r,),
              name=f"{tag}_dxf")
    xchg.push_grads({(n, j): val for n, val in g.items()})
    dx = _mm(dqkv, res['w_qkv'], trans_b=True, extras=(dxf,), epilogue=lambda acc, r: (acc + r,), name=f"{tag}_dx")
    small = dict(od_b_f=dbf[0, :FOX_HEADS])
    return dx, small


def _carrying(xchg, name, call, *args, **kwargs):
    comm = xchg.carry(name)
    out = call(*args, comm=comm, name=name, **kwargs)
    if comm:
        out, got = out
        xchg.carried(name, got)
    return out


def _scattering(xchg, call, *args, **kwargs):
    comm = xchg.bwd_items()
    out = call(*args, comm=comm, **kwargs)
    if comm:
        out, got = out
        xchg.bwd_done(got)
    return out


def _local_step(x, p, target, P, xchg):
    B, S, D = x.shape
    T = B * S
    q_tabs, k_tabs = _rope_tables(S)
    bucket = _swa_bucket_table()
    in_bucket = (bucket[..., None] == jnp.arange(REL_BUCKETS)).astype(F32)
    bias = jnp.einsum('acb,bh->hac', in_bucket, P['rel_bias'], precision=lax.Precision.HIGHEST)

    xc = x.reshape(T, D)
    xcb = xc.astype(BF16)
    saved = []
    for i in range(DEPTH):
        j = i // 2
        tag = f"l{i}"
        W = xchg.layer_weights(i)
        lay = dict(xb=xcb, W=W)
        if i % 2 == 0:
            sinkcol = jnp.broadcast_to(P['ev_sinks'][j][:, None, None], (SWA_HEADS, BLOCK_Q, 1)).astype(F32)
            lay['tabs'] = (q_tabs, k_tabs, bias, sinkcol)
            (o, w_out), lay['mix'] = _even_fwd(xcb, W, P, i, B, S, lay['tabs'], xchg, tag)
        else:
            (o, w_out), lay['mix'] = _odd_fwd(xcb, W, P, i, B, S, xchg, tag)
        x1, x1b, lay['xh1'], lay['r1'] = _carrying(xchg, f"{tag}_out_ln1", _mm_ln, o, w_out, xc,
                                                   P['ln1_g'][i][None], P['ln1_b'][i][None])
        lay['x1b'] = x1b
        lay['u'], lay['a'] = _carrying(xchg, f"{tag}_up", _mm, x1b, W['w_up'], out_dtypes=(F32, BF16),
                                       epilogue=lambda acc: (acc, jnp.square(jnp.maximum(acc, 0.0))))
        x2, x2b, lay['xh2'], lay['r2'] = _carrying(xchg, f"{tag}_down_ln2", _mm_ln, lay['a'], W['w_down'], x1,
                                                   P['ln2_g'][i][None], P['ln2_b'][i][None])
        lay['x2b'] = x2b
        lay['p'] = p[i].reshape(T, D_PLE)
        lay['e'] = _mm(lay['p'], W['ple_w_proj'], name=f"{tag}_ple_proj")

        def gate(acc, bg, e, x2v):
            gv = 1.0 / (1.0 + jnp.exp(-(acc + bg)))
            y = x2v + gv * e
            return y, y, gv

        xc, xcb, lay['g'] = _carrying(xchg, f"{tag}_ple_gate", _mm, x2b, W['ple_w_gate'],
                                      extras=(P['ple_b_gate'][i][None], lay['e'], x2), epilogue=gate,
                                      out_dtypes=(F32, BF16, F32))
        saved.append(lay)

    dy, sq = _loss_grad(xc, target.reshape(T, D), name="loss")

    Gs = {n: [None] * DEPTH for n in ('ln1_g', 'ln1_b', 'ln2_g', 'ln2_b', 'ple_b_gate')}
    Gs.update({n: [None] * (DEPTH // 2) for n in ('ev_q_norm', 'ev_kv_norm', 'ev_sinks', 'od_b_f')})
    dbias_total = None
    for i in reversed(range(DEPTH)):
        j = i // 2
        tag = f"l{i}b"
        lay = saved[i]
        W = lay['W']
        de, dzg, dbg = _ple_bwd_elem(dy, lay['g'], lay['e'], name=f"{tag}_ple_elem")
        Gs['ple_b_gate'][i] = dbg[0]
        g_mlp = {('ple_w_proj', i): _mm_tn(lay['p'], de, name=f"{tag}_dwproj"),
                 ('ple_w_gate', i): _mm_tn(lay['x2b'], dzg, name=f"{tag}_dwgate")}
        dz2, dz2b, dg2, db2 = _mm_ln_bwd(dzg, W['ple_w_gate'], dy, 1.0, lay['xh2'], lay['r2'], P['ln2_g'][i][None],
                                         name=f"{tag}_dx2_ln2")
        Gs['ln2_g'][i], Gs['ln2_b'][i] = dg2[0], db2[0]
        g_mlp[('w_down', i)] = _mm_tn(lay['a'], dz2b, name=f"{tag}_dwdown")
        du = _mm(dz2b, W['w_down'], trans_b=True, extras=(lay['u'],), out_dtypes=(BF16,),
                 epilogue=lambda acc, u: (acc * (2.0 * jnp.maximum(u, 0.0)),), name=f"{tag}_du")
        g_mlp[('w_up', i)] = _mm_tn(lay['x1b'], du, name=f"{tag}_dwup")
        xchg.push_grads(g_mlp)
        dz1, dz1b, dg1, db1 = _mm_ln_bwd(du, W['w_up'], dz2, DN_ALPHA, lay['xh1'], lay['r1'], P['ln1_g'][i][None],
                                         name=f"{tag}_dx1_ln1")
        Gs['ln1_g'][i], Gs['ln1_b'][i] = dg1[0], db1[0]
        if i % 2 == 0:
            dy, small = _even_bwd(dz1b, dz1, lay['xb'], W, P, j, B, S, lay['tabs'], lay['mix'], xchg, tag)
            dbias_total = small['dbias'] if dbias_total is None else dbias_total + small['dbias']
            for n in ('ev_q_norm', 'ev_kv_norm', 'ev_sinks'):
                Gs[n][j] = small[n]
        else:
            dy, small = _odd_bwd(dz1b, dz1, lay['xb'], W, P, j, B, S, lay['mix'], xchg, tag)
            Gs['od_b_f'][j] = small['od_b_f']

    grads_small = {n: jnp.stack(v) for n, v in Gs.items()}
    drel = _bias_bucket_sum(dbias_total, bucket, name="rel_bias_grad")
    grads_small['rel_bias'] = drel[:, :REL_BUCKETS].T
    return sq, dy.reshape(B, S, D), grads_small


def kernel(x, p, rel_bias, ev_w_in, ev_q_norm, ev_w_uq, ev_kv_norm, ev_w_ukv, ev_sinks, ev_w_out, od_w_in, od_b_f, od_w_out, ln1_g, ln1_b, w_up, w_down, ln2_g, ln2_b, ple_w_proj, ple_w_gate, ple_b_gate, loss_target, m_rel_bias, m_ev_w_in, m_ev_q_norm, m_ev_w_uq, m_ev_kv_norm, m_ev_w_ukv, m_ev_sinks, m_ev_w_out, m_od_w_in, m_od_b_f, m_od_w_out, m_ln1_g, m_ln1_b, m_w_up, m_w_down, m_ln2_g, m_ln2_b, m_ple_w_proj, m_ple_w_gate, m_ple_b_gate, v_rel_bias, v_ev_w_in, v_ev_q_norm, v_ev_w_uq, v_ev_kv_norm, v_ev_w_ukv, v_ev_sinks, v_ev_w_out, v_od_w_in, v_od_b_f, v_od_w_out, v_ln1_g, v_ln1_b, v_w_up, v_w_down, v_ln2_g, v_ln2_b, v_ple_w_proj, v_ple_w_gate, v_ple_b_gate):
    given = dict(locals())
    w = {n: given[n] for n in WEIGHTS}
    mom = {n: given["m_" + n] for n in WEIGHTS}
    var = {n: given["v_" + n] for n in WEIGHTS}
    small_shapes = {n: w[n].shape for n in SMALL}

    xchg = _MeshExchange({n: w[n].astype(BF16) for n in BIG})
    P = {n: w[n] for n in SMALL}

    sq, grad_x, grads_small = _local_step(x, p, loss_target, P, xchg)
    loss = lax.psum(0.5 * jnp.sum(sq) / D_MODEL, ("x", "y", "c"))

    received = xchg.finish()
    g_small_packed = _all_reduce_small(_pack_small(grads_small), name="reduce_small_grads")
    g_small = _unpack_small(g_small_packed, small_shapes)

    grad, delta, new_m, new_v = {}, {}, {}, {}
    for n in BIG:
        parts = [received[(n, idx)] for idx in range(w[n].shape[0])]
        grad[n], delta[n], new_m[n], new_v[n] = _adamw_slots(w[n], parts, mom[n], var[n], name=f"adamw_{n}")
    d, nm, nv = _adamw(_pack_small(w), g_small_packed, _pack_small(mom), _pack_small(var), name="adamw_small")
    d, nm, nv = (_unpack_small(t, small_shapes) for t in (d, nm, nv))
    for n in SMALL:
        grad[n], delta[n], new_m[n], new_v[n] = g_small[n], d[n], nm[n], nv[n]

    return (loss, grad_x, *[grad[n] for n in WEIGHTS], *[delta[n] for n in WEIGHTS],
            *[new_m[n] for n in WEIGHTS], *[new_v[n] for n in WEIGHTS])
```

```python
import math

import jax
import jax.numpy as jnp
from jax import lax
from jax.experimental import pallas as pl
from jax.experimental.pallas import tpu as pltpu

F32, BF16 = jnp.float32, jnp.bfloat16

D_MODEL = 1024
DEPTH = 4
HEAD_DIM = 64
MLA_HEADS, MLA_NOPE, MLA_ROPE, MLA_V = 8, 64, 32, 64
MLA_Q_LORA, MLA_KV_LORA = 384, 256
MLA_QK = MLA_NOPE + MLA_ROPE
ROPE_THETA = 10000.0
SWA_HEADS, SWA_KV_HEADS, SWA_WINDOW = 8, 2, 128
SWA_GROUP = SWA_HEADS // SWA_KV_HEADS
REL_BUCKETS, REL_MAX_DIST = 32, 128
FOX_HEADS = 16
D_FF = 4 * D_MODEL
D_PLE = 256
BLOCK_Q = 128
DN_ALPHA = (2 * DEPTH) ** 0.25
NORM_EPS = 1e-5
NEG_INF = -1e30
EVEN_IN = 1440
ODD_QKV = 3 * FOX_HEADS * HEAD_DIM
LANES = 128

EV_QS = (0, 512)
EV_CQ = (512, 896)
EV_CKV = (896, 1152)
EV_KS = (1152, 1280)
EV_VS = (1280, 1408)
EV_KR = (1408, 1536)
EVEN_IN_PAD = 1536
KR_LANE0 = MLA_NOPE

ADAM_LR, ADAM_B1, ADAM_B2, ADAM_EPS, ADAM_WD, ADAM_STEP = 0.001, 0.9, 0.999, 1e-08, 0.01, 10

N_DEV = 8
VMEM_LIMIT_BYTES = 48 * 1024 * 1024
ATT_TILE = 512
ATT_TILE_BWD = 512
PAIRS_PER_STEP_FWD = 4
PAIRS_PER_STEP_BWD = 2

NN = (((1,), (0,)), ((), ()))
NT = (((1,), (1,)), ((), ()))
TN = (((0,), (0,)), ((), ()))

BIG = ['ev_w_in', 'ev_w_uq', 'ev_w_ukv', 'ev_w_out', 'od_w_in', 'od_w_out', 'w_up', 'w_down',
       'ple_w_proj', 'ple_w_gate']
BIG_AXIS = {'ev_w_in': 2, 'ev_w_uq': 2, 'ev_w_ukv': 2, 'ev_w_out': 1, 'od_w_in': 2, 'od_w_out': 1,
            'w_up': 2, 'w_down': 1, 'ple_w_proj': 2, 'ple_w_gate': 1}
SMALL = ['rel_bias', 'ev_q_norm', 'ev_kv_norm', 'ev_sinks', 'od_b_f', 'ln1_g', 'ln1_b', 'ln2_g', 'ln2_b',
         'ple_b_gate']
WEIGHTS = ['rel_bias', 'ev_w_in', 'ev_q_norm', 'ev_w_uq', 'ev_kv_norm', 'ev_w_ukv', 'ev_sinks', 'ev_w_out',
           'od_w_in', 'od_b_f', 'od_w_out', 'ln1_g', 'ln1_b', 'w_up', 'w_down', 'ln2_g', 'ln2_b',
           'ple_w_proj', 'ple_w_gate', 'ple_b_gate']


def _cparams(*sem):
    return pltpu.CompilerParams(dimension_semantics=sem, vmem_limit_bytes=VMEM_LIMIT_BYTES)


def _pick(n, cands):
    for c in cands:
        if n % c == 0:
            return c
    return n


MM_STEP_BYTES = 10 * 1024 * 1024
MM_OUT_BYTES = 8 * 1024 * 1024
MM_CHUNK = 512


def _mm(a, b, *, trans_b=False, extras=(), epilogue=None, row_epilogue=None, out_dtypes=(F32,), out_widths=None,
        n_sums=0, comm=(), name):
    M, K = a.shape
    N = b.shape[0] if trans_b else b.shape[1]
    n_ex, n_out = len(extras), len(out_dtypes)
    n_rows_out = n_out - n_sums
    out_widths = (N,) * n_out if out_widths is None else out_widths
    row_bytes = K * a.dtype.itemsize + (sum(w * jnp.dtype(d).itemsize
                                            for w, d in zip(out_widths[:n_rows_out], out_dtypes))
                                        + sum(e.shape[1] * e.dtype.itemsize for e in extras if e.shape[0] == M)
                                        + (4 * N if row_epilogue is not None else 0))
    tm = next((c for c in (1024, 512, 256) if M % c == 0 and c * row_bytes <= MM_STEP_BYTES), 128)
    nc = _pick(N, (MM_CHUNK, 384, 256, 128))
    n_c, kinds = len(comm), [k for k, _ in comm]
    n_scr = 1 if row_epilogue is not None else 0

    def body(*refs):
        c_in = refs[2 + n_ex:2 + n_ex + n_c]
        c_out = refs[2 + n_ex + n_c + n_out:2 + n_ex + 2 * n_c + n_out]
        sems = refs[2 + n_ex + 2 * n_c + n_out + n_scr:]
        refs = refs[:2 + n_ex] + refs[2 + n_ex + n_c:2 + n_ex + n_c + n_out] \
            + refs[2 + n_ex + 2 * n_c + n_out:2 + n_ex + 2 * n_c + n_out + n_scr]
        if n_c:
            place = _mesh_place()
            step = pl.program_id(0)

            @pl.when(step == 0)
            def _():
                _comm_start(kinds, c_in, c_out, sems, place)

        a_ref, b_ref = refs[:2]
        ex = refs[2:2 + n_ex]
        outs = refs[2 + n_ex:2 + n_ex + n_out]
        av = a_ref[...].astype(BF16)
        for n0 in range(0, N, nc):
            cols = slice(n0, n0 + nc)
            bv = (b_ref[cols, :] if trans_b else b_ref[:, cols]).astype(BF16)
            acc = lax.dot_general(av, bv, NT if trans_b else NN, preferred_element_type=F32)
            if row_epilogue is not None:
                refs[-1][:, cols] = acc
                continue
            res = epilogue(acc, *[e[:, cols] for e in ex]) if epilogue is not None else (acc,)
            for o, r in zip(outs, res):
                o[:, cols] = r.astype(o.dtype)
        if row_epilogue is not None:
            res = row_epilogue(refs[-1][...], *[e[...] for e in ex])
            for o, r in zip(outs[:n_rows_out], res):
                o[...] = r.astype(o.dtype)
            if n_sums:
                @pl.when(pl.program_id(0) == 0)
                def _():
                    for o in outs[n_rows_out:]:
                        o[...] = jnp.zeros_like(o)

                for o, r in zip(outs[n_rows_out:], res[n_rows_out:]):
                    o[...] += r
        if n_c:
            @pl.when(step == M // tm - 1)
            def _():
                _comm_wait(kinds, c_in, c_out, sems, place)

    in_specs = [pl.BlockSpec((tm, K), lambda i: (i, 0)), pl.BlockSpec(b.shape, lambda i: (0, 0))]
    for e in extras:
        if e.shape[0] == M:
            in_specs.append(pl.BlockSpec((tm, e.shape[1]), lambda i: (i, 0)))
        elif e.shape == (1, N):
            in_specs.append(pl.BlockSpec((1, N), lambda i: (0, 0)))
        else:
            raise ValueError(f"extra operand of shape {e.shape} for a ({M}, {N}) result")
    res = pl.pallas_call(
        body, name=name, grid=(M // tm,), in_specs=in_specs + [HBM_SPEC] * n_c,
        out_specs=[pl.BlockSpec((tm, w), lambda i: (i, 0)) for w in out_widths[:n_rows_out]]
        + [pl.BlockSpec((1, w), lambda i: (0, 0)) for w in out_widths[n_rows_out:]] + [HBM_SPEC] * n_c,
        out_shape=[jax.ShapeDtypeStruct((M, w), d) for w, d in zip(out_widths[:n_rows_out], out_dtypes)]
        + [jax.ShapeDtypeStruct((1, w), d) for w, d in zip(out_widths[n_rows_out:], out_dtypes[n_rows_out:])]
        + _comm_out_shapes(comm),
        scratch_shapes=([pltpu.VMEM((tm, N), F32)] if row_epilogue is not None else [])
        + (_comm_scratch(comm) if n_c else []),
        compiler_params=_cparams("arbitrary" if n_sums or n_c else "parallel"),
    )(a, b, *extras, *[c for _, c in comm])
    main = res[0] if n_out == 1 else tuple(res[:n_out])
    return (main, list(res[n_out:])) if n_c else main


def _mm_tn(a, b, *, name):
    T, K = a.shape
    N = b.shape[1]
    bk, bn = K, N
    while bk * bn * 4 > MM_OUT_BYTES:
        if bn >= bk and bn % (2 * LANES) == 0:
            bn //= 2
        else:
            bk //= 2
    tt = _pick(T, (1024, 512, 256))
    ck, cn = _pick(bk, (MM_CHUNK, 384, 256, 128)), _pick(bn, (MM_CHUNK, 384, 256, 128))

    def body(a_ref, b_ref, o_ref):
        t = pl.program_id(2)

        @pl.when(t == 0)
        def _():
            o_ref[...] = jnp.zeros_like(o_ref)

        for r0 in range(0, bk, ck):
            av = a_ref[:, r0:r0 + ck].astype(BF16)
            for c0 in range(0, bn, cn):
                o_ref[r0:r0 + ck, c0:c0 + cn] += lax.dot_general(
                    av, b_ref[:, c0:c0 + cn].astype(BF16), TN, preferred_element_type=F32)

    return pl.pallas_call(
        body, name=name, grid=(K // bk, N // bn, T // tt),
        in_specs=[pl.BlockSpec((tt, bk), lambda i, j, t: (t, i)), pl.BlockSpec((tt, bn), lambda i, j, t: (t, j))],
        out_specs=pl.BlockSpec((bk, bn), lambda i, j, t: (i, j)),
        out_shape=jax.ShapeDtypeStruct((K, N), F32),
        compiler_params=_cparams("parallel", "parallel", "arbitrary"),
    )(a, b)


ROW_TILE = 256


def _row_spec(cols, col_block=0):
    return pl.BlockSpec((ROW_TILE, cols), lambda i: (i, col_block))


def _tab_spec(cols, period):
    return pl.BlockSpec((ROW_TILE, cols), lambda i: (i % period, 0))


def _full_spec(shape):
    return pl.BlockSpec(shape, lambda i: (0,) * len(shape))


def _mm_ln(a, w, x, g, b, *, comm=(), name):
    def ln_rows(m, xv, gv, bv):
        z = DN_ALPHA * xv + m
        mu = jnp.mean(z, -1, keepdims=True)
        zc = z - mu
        r = lax.rsqrt(jnp.mean(zc * zc, -1, keepdims=True) + NORM_EPS)
        xh = zc * r
        y = xh * gv + bv
        return y, y, xh, jnp.broadcast_to(r, (r.shape[0], LANES))

    D = w.shape[1]
    return _mm(a, w, extras=(x, g, b), row_epilogue=ln_rows, out_dtypes=(F32, BF16, F32, F32),
               out_widths=(D, D, D, LANES), comm=comm, name=name)


def _mm_ln_bwd(a, w, resid, resid_scale, xh, r, g, *, name):
    def ln_bwd_rows(acc, rv, xhv, rstd, gv):
        dyv = acc + resid_scale * rv
        dyg = dyv * gv
        c1 = jnp.mean(dyg, -1, keepdims=True)
        c2 = jnp.mean(dyg * xhv, -1, keepdims=True)
        dz = _widen(rstd, dyv.shape[-1]) * (dyg - c1 - xhv * c2)
        return dz, dz, jnp.sum(dyv * xhv, 0, keepdims=True), jnp.sum(dyv, 0, keepdims=True)

    D = w.shape[0]
    return _mm(a, w, trans_b=True, extras=(resid, xh, r, g), row_epilogue=ln_bwd_rows,
               out_dtypes=(F32, BF16, F32, F32), out_widths=(D, D, D, D), n_sums=2, name=name)


def _loss_grad(y, target, *, name):
    T, D = y.shape

    def body(y_ref, t_ref, dy_ref, sq_ref):
        err = y_ref[...] - t_ref[...]
        dy_ref[...] = err / D

        @pl.when(pl.program_id(0) == 0)
        def _():
            sq_ref[...] = jnp.zeros_like(sq_ref)

        sq_ref[...] += jnp.sum(err * err, 0, keepdims=True)

    return pl.pallas_call(
        body, name=name, grid=(T // ROW_TILE,),
        in_specs=[_row_spec(D), _row_spec(D)],
        out_specs=[_row_spec(D), _full_spec((1, D))],
        out_shape=[jax.ShapeDtypeStruct((T, D), F32), jax.ShapeDtypeStruct((1, D), F32)],
        compiler_params=_cparams("arbitrary"),
    )(y, target)


def _ple_bwd_elem(dx3, g, e, *, name):
    T, D = dx3.shape

    def body(dx_ref, g_ref, e_ref, de_ref, dz_ref, db_ref):
        dx, gv = dx_ref[...], g_ref[...]
        de_ref[...] = (dx * gv).astype(BF16)
        dz = dx * e_ref[...] * gv * (1.0 - gv)
        dz_ref[...] = dz.astype(BF16)

        @pl.when(pl.program_id(0) == 0)
        def _():
            db_ref[...] = jnp.zeros_like(db_ref)

        db_ref[...] += jnp.sum(dz, 0, keepdims=True)

    return pl.pallas_call(
        body, name=name, grid=(T // ROW_TILE,),
        in_specs=[_row_spec(D), _row_spec(D), _row_spec(D)],
        out_specs=[_row_spec(D), _row_spec(D), _full_spec((1, D))],
        out_shape=[jax.ShapeDtypeStruct((T, D), BF16), jax.ShapeDtypeStruct((T, D), BF16),
                   jax.ShapeDtypeStruct((1, D), F32)],
        compiler_params=_cparams("arbitrary"),
    )(dx3, g, e)


def _rotate(xv, a, bm, bp, sign):
    half = MLA_ROPE // 2
    width = xv.shape[-1]
    a, bm, bp = (_widen(t, width) for t in (a, bm, bp))
    return xv * a + sign * (pltpu.roll(xv, width - half, 1) * bm + pltpu.roll(xv, half, 1) * bp)


def _rope(x, tabs, seq, *, sign, name):
    T, width = x.shape

    def body(x_ref, a_ref, bm_ref, bp_ref, o_ref):
        o_ref[...] = _rotate(x_ref[...], a_ref[...], bm_ref[...], bp_ref[...], sign).astype(BF16)

    return pl.pallas_call(
        body, name=name, grid=(T // ROW_TILE,),
        in_specs=[_row_spec(width)] + [_tab_spec(LANES, seq // ROW_TILE)] * 3,
        out_specs=_row_spec(width),
        out_shape=jax.ShapeDtypeStruct((T, width), BF16),
        compiler_params=_cparams("parallel"),
    )(x, *tabs)


def _mla_keys(knp, h, k_tabs, seq, *, name):
    T = knp.shape[0]

    def body(k_ref, h_ref, a_ref, bm_ref, bp_ref, o_ref):
        kr = _rotate(h_ref[...], a_ref[...], bm_ref[...], bp_ref[...], 1.0)
        for hd in range(MLA_HEADS):
            cols = slice(hd * LANES, (hd + 1) * LANES)
            o_ref[:, cols] = (k_ref[:, cols].astype(F32) + kr).astype(BF16)

    return pl.pallas_call(
        body, name=name, grid=(T // ROW_TILE,),
        in_specs=[_row_spec(MLA_HEADS * LANES), _row_spec(LANES, EV_KR[0] // LANES)]
        + [_tab_spec(LANES, seq // ROW_TILE)] * 3,
        out_specs=_row_spec(MLA_HEADS * LANES),
        out_shape=jax.ShapeDtypeStruct((T, MLA_HEADS * LANES), BF16),
        compiler_params=_cparams("parallel"),
    )(knp, h, *k_tabs)


def _mla_rope_key_grad(dk, k_tabs, seq, *, name):
    T = dk.shape[0]

    def body(dk_ref, a_ref, bm_ref, bp_ref, o_ref):
        tot = dk_ref[:, 0:LANES]
        for hd in range(1, MLA_HEADS):
            tot = tot + dk_ref[:, hd * LANES:(hd + 1) * LANES]
        o_ref[...] = _rotate(tot, a_ref[...], bm_ref[...], bp_ref[...], -1.0).astype(BF16)

    return pl.pallas_call(
        body, name=name, grid=(T // ROW_TILE,),
        in_specs=[_row_spec(MLA_HEADS * LANES)] + [_tab_spec(LANES, seq // ROW_TILE)] * 3,
        out_specs=_row_spec(LANES),
        out_shape=jax.ShapeDtypeStruct((T, LANES), BF16),
        compiler_params=_cparams("parallel"),
    )(dk, *k_tabs)


def _even_norms(h, gq, gkv, *, name):
    T = h.shape[0]

    def body(h_ref, gq_ref, gkv_ref, cq_ref, ckv_ref, rq_ref, rkv_ref):
        cq = h_ref[:, EV_CQ[0]:EV_CQ[1]]
        rq = lax.rsqrt(jnp.mean(cq * cq, -1, keepdims=True) + NORM_EPS)
        cq_ref[...] = (cq * rq * gq_ref[...]).astype(BF16)
        rq_ref[...] = jnp.broadcast_to(rq, rq_ref.shape)
        ckv = h_ref[:, EV_CKV[0]:EV_CKV[1]]
        rkv = lax.rsqrt(jnp.mean(ckv * ckv, -1, keepdims=True) + NORM_EPS)
        ckv_ref[...] = (ckv * rkv * gkv_ref[...]).astype(BF16)
        rkv_ref[...] = jnp.broadcast_to(rkv, rkv_ref.shape)

    return pl.pallas_call(
        body, name=name, grid=(T // ROW_TILE,),
        in_specs=[_row_spec(EVEN_IN_PAD), _full_spec((1, MLA_Q_LORA)), _full_spec((1, MLA_KV_LORA))],
        out_specs=[_row_spec(MLA_Q_LORA), _row_spec(MLA_KV_LORA), _row_spec(LANES), _row_spec(LANES)],
        out_shape=[jax.ShapeDtypeStruct((T, MLA_Q_LORA), BF16), jax.ShapeDtypeStruct((T, MLA_KV_LORA), BF16),
                   jax.ShapeDtypeStruct((T, LANES), F32), jax.ShapeDtypeStruct((T, LANES), F32)],
        compiler_params=_cparams("parallel"),
    )(h, gq, gkv)


def _even_in_bwd(h, rq, rkv, gq, gkv, dcqn, dckvn, dqs, dks, dvs, dkr, *, name):
    T = h.shape[0]

    def rms_bwd(c, r, g, dy):
        r = _widen(r, c.shape[-1])
        xr = c * r
        dyg = dy * g
        return r * (dyg - xr * jnp.mean(dyg * xr, -1, keepdims=True)), jnp.sum(dy * xr, 0, keepdims=True)

    def body(h_ref, rq_ref, rkv_ref, gq_ref, gkv_ref, dcq_ref, dckv_ref, dqs_ref, dks_ref, dvs_ref, dkr_ref,
             dh_ref, dgq_ref, dgkv_ref):
        @pl.when(pl.program_id(0) == 0)
        def _():
            dgq_ref[...] = jnp.zeros_like(dgq_ref)
            dgkv_ref[...] = jnp.zeros_like(dgkv_ref)

        dcq, dgq = rms_bwd(h_ref[:, EV_CQ[0]:EV_CQ[1]], rq_ref[...], gq_ref[...], dcq_ref[...])
        dckv, dgkv = rms_bwd(h_ref[:, EV_CKV[0]:EV_CKV[1]], rkv_ref[...], gkv_ref[...], dckv_ref[...])
        dgq_ref[...] += dgq
        dgkv_ref[...] += dgkv
        dh_ref[:, EV_QS[0]:EV_QS[1]] = dqs_ref[...]
        dh_ref[:, EV_CQ[0]:EV_CQ[1]] = dcq.astype(BF16)
        dh_ref[:, EV_CKV[0]:EV_CKV[1]] = dckv.astype(BF16)
        dh_ref[:, EV_KS[0]:EV_KS[1]] = dks_ref[...]
        dh_ref[:, EV_VS[0]:EV_VS[1]] = dvs_ref[...]
        dh_ref[:, EV_KR[0]:EV_KR[1]] = dkr_ref[...]

    return pl.pallas_call(
        body, name=name, grid=(T // ROW_TILE,),
        in_specs=[_row_spec(EVEN_IN_PAD), _row_spec(LANES), _row_spec(LANES), _full_spec((1, MLA_Q_LORA)),
                  _full_spec((1, MLA_KV_LORA)), _row_spec(MLA_Q_LORA), _row_spec(MLA_KV_LORA),
                  _row_spec(SWA_HEADS * HEAD_DIM), _row_spec(LANES), _row_spec(LANES), _row_spec(LANES)],
        out_specs=[_row_spec(EVEN_IN_PAD), _full_spec((1, MLA_Q_LORA)), _full_spec((1, MLA_KV_LORA))],
        out_shape=[jax.ShapeDtypeStruct((T, EVEN_IN_PAD), BF16), jax.ShapeDtypeStruct((1, MLA_Q_LORA), F32),
                   jax.ShapeDtypeStruct((1, MLA_KV_LORA), F32)],
        compiler_params=_cparams("arbitrary"),
    )(h, rq, rkv, gq, gkv, dcqn, dckvn, dqs, dks, dvs, dkr)


def _fox_decay_fwd(f3, bf, *, name):
    B, S, _ = f3.shape

    def body(f_ref, b_ref, csh_ref, chs_ref):
        x = f_ref[...] + b_ref[...]
        c = jnp.minimum(x, 0.0) - jnp.log1p(jnp.exp(-jnp.abs(x)))
        row = lax.broadcasted_iota(jnp.int32, (S, LANES), 0)
        k = 1
        while k < S:
            c = c + jnp.where(row >= k, pltpu.roll(c, k, 0), 0.0)
            k *= 2
        csh_ref[...] = c
        chs_ref[...] = c.T

    return pl.pallas_call(
        body, name=name, grid=(B,),
        in_specs=[pl.BlockSpec((None, S, LANES), lambda b: (b, 0, 0)), pl.BlockSpec((1, LANES), lambda b: (0, 0))],
        out_specs=[pl.BlockSpec((None, S, LANES), lambda b: (b, 0, 0)),
                   pl.BlockSpec((None, LANES, S), lambda b: (b, 0, 0))],
        out_shape=[jax.ShapeDtypeStruct((B, S, LANES), F32), jax.ShapeDtypeStruct((B, LANES, S), F32)],
        compiler_params=_cparams("parallel"),
    )(f3, bf)


def _fox_decay_bwd(dc_hs, f3, bf, *, name):
    B, S, _ = f3.shape

    def body(dc_ref, f_ref, b_ref, df_ref, db_ref):
        g = dc_ref[...].T
        row = lax.broadcasted_iota(jnp.int32, (S, LANES), 0)
        k = 1
        while k < S:
            g = g + jnp.where(row < S - k, pltpu.roll(g, S - k, 0), 0.0)
            k *= 2
        x = f_ref[...] + b_ref[...]
        df = g * (1.0 / (1.0 + jnp.exp(x)))
        df_ref[...] = df.astype(BF16)

        @pl.when(pl.program_id(0) == 0)
        def _():
            db_ref[...] = jnp.zeros_like(db_ref)

        db_ref[...] += jnp.sum(df, 0, keepdims=True)

    return pl.pallas_call(
        body, name=name, grid=(B,),
        in_specs=[pl.BlockSpec((None, LANES, S), lambda b: (b, 0, 0)),
                  pl.BlockSpec((None, S, LANES), lambda b: (b, 0, 0)), pl.BlockSpec((1, LANES), lambda b: (0, 0))],
        out_specs=[pl.BlockSpec((None, S, LANES), lambda b: (b, 0, 0)), pl.BlockSpec((1, LANES), lambda b: (0, 0))],
        out_shape=[jax.ShapeDtypeStruct((B, S, LANES), BF16), jax.ShapeDtypeStruct((1, LANES), F32)],
        compiler_params=_cparams("arbitrary"),
    )(dc_hs, f3, bf)


def _head_column(block, h):
    lane = lax.broadcasted_iota(jnp.int32, block.shape, 1)
    return jnp.sum(jnp.where(lane == h, block, 0.0), axis=-1, keepdims=True)


def _causal_mask(s):
    r = lax.broadcasted_iota(jnp.int32, s.shape, 0)
    c = lax.broadcasted_iota(jnp.int32, s.shape, 1)
    return jnp.where(c <= r, s, NEG_INF)


def _low_half(shape):
    return (lax.broadcasted_iota(jnp.int32, shape, 1) % LANES) < HEAD_DIM


def _widen(x, cols):
    return jnp.concatenate([x] * (cols // LANES), axis=1)


def _both_halves(x, lo):
    r = pltpu.roll(x, HEAD_DIM, 1)
    return jnp.where(lo, x, r), jnp.where(lo, r, x)


MESH_ID = pl.DeviceIdType.MESH
HBM_SPEC = pl.BlockSpec(memory_space=pltpu.HBM)
VMEM_SPEC = pl.BlockSpec(memory_space=pltpu.VMEM)


def _mesh_place():
    x, y, c = lax.axis_index("x"), lax.axis_index("y"), lax.axis_index("c")
    return x, y, c, 4 * x + 2 * y + c


def _peers(x, y, c):
    out = []
    for mask in range(1, N_DEV):
        dx, dy, dc = (mask >> 2) & 1, (mask >> 1) & 1, mask & 1
        px, py, pc = (1 - x if dx else x), (1 - y if dy else y), (1 - c if dc else c)
        out.append(((px, py, pc), 4 * px + 2 * py + pc))
    return out


def _comm_out_shapes(comm):
    return [jax.ShapeDtypeStruct(a.shape if kind == "scatter" else (N_DEV,) + a.shape[1:], a.dtype) for kind, a in comm]


def _comm_scratch(comm):
    n = len(comm)
    return [pltpu.SemaphoreType.DMA((n, 7)), pltpu.SemaphoreType.DMA((n, 7)), pltpu.SemaphoreType.DMA((n,))]


def _comm_copies(kinds, in_refs, out_refs, sems, place):
    send_sems, recv_sems, local_sems = sems
    x, y, c, me = place
    local, remote = [], []
    for w, kind in enumerate(kinds):
        mine = in_refs[w].at[me] if kind == "scatter" else in_refs[w].at[kind[1]]
        local.append(pltpu.make_async_copy(mine, out_refs[w].at[me], local_sems.at[w]))
        for k, (peer, peer_idx) in enumerate(_peers(x, y, c)):
            remote.append(pltpu.make_async_remote_copy(
                src_ref=in_refs[w].at[peer_idx] if kind == "scatter" else mine, dst_ref=out_refs[w].at[me],
                send_sem=send_sems.at[w, k], recv_sem=recv_sems.at[w, k], device_id=peer, device_id_type=MESH_ID))
    return local, remote


def _comm_start(kinds, in_refs, out_refs, sems, place):
    local, remote = _comm_copies(kinds, in_refs, out_refs, sems, place)
    for cp in local + remote:
        cp.start()


def _comm_wait(kinds, in_refs, out_refs, sems, place):
    local, remote = _comm_copies(kinds, in_refs, out_refs, sems, place)
    for cp in remote:
        cp.wait_recv()
    for cp in remote:
        cp.wait_send()
    for cp in local:
        cp.wait()


def _exchange(comm, *, name):
    n = len(comm)
    kinds = [k for k, _ in comm]

    def body(*refs):
        place = _mesh_place()
        _comm_start(kinds, refs[:n], refs[n:2 * n], refs[2 * n:], place)
        _comm_wait(kinds, refs[:n], refs[n:2 * n], refs[2 * n:], place)

    return pl.pallas_call(
        body, name=name, out_shape=_comm_out_shapes(comm), in_specs=[HBM_SPEC] * n, out_specs=[HBM_SPEC] * n,
        scratch_shapes=_comm_scratch(comm),
    )(*[a for _, a in comm])


def _flash_fwd(qa, ka, va, *, q_blk0, k_blk0, v_blk0, W, n_pairs, B, S, scale, csh=None, crow=None, comm=(), name):
    t = ATT_TILE
    nq = S // t
    P = PAIRS_PER_STEP_FWD
    decay = csh is not None
    split = W == LANES
    assert n_pairs % P == 0 and q_blk0 % P == 0 and k_blk0 % P == 0 and v_blk0 % P == 0
    n_c, kinds = len(comm), [k for k, _ in comm]
    n_in = 5 if decay else 3
    fold_scale = math.log2(scale).is_integer()
    n_steps = (B, n_pairs // P, nq)

    def body(*refs):
        c_in, c_out = refs[n_in:n_in + n_c], refs[n_in + n_c + 2:n_in + 2 * n_c + 2]
        sems = refs[n_in + 2 * n_c + 4:]
        refs = refs[:n_in] + refs[n_in + n_c:n_in + n_c + 2] + refs[n_in + 2 * n_c + 2:n_in + 2 * n_c + 4]
        if decay:
            q_ref, k_ref, v_ref, csh_ref, crow_ref, o_ref, lse_ref, m_s, acc_s = refs
        else:
            q_ref, k_ref, v_ref, o_ref, lse_ref, m_s, acc_s = refs
        g, i = pl.program_id(1), pl.program_id(2)
        if n_c:
            place = _mesh_place()
            ids = [pl.program_id(ax) for ax in range(3)]

            @pl.when((ids[0] == 0) & (ids[1] == 0) & (ids[2] == 0))
            def _():
                _comm_start(kinds, c_in, c_out, sems, place)

        lo = _low_half((t, LANES))
        qv = q_ref[...]
        qh = []
        for pr in range(P):
            qp = qv[:, pr * W:(pr + 1) * W]
            qh += [jnp.where(lo, qp, jnp.zeros_like(qp)), jnp.where(lo, jnp.zeros_like(qp), qp)] if split \
                else [qp[:, :LANES], qp[:, LANES:]]
        if fold_scale:
            qh = [x * scale for x in qh]
        if decay:
            cq = [jnp.broadcast_to(_head_column(csh_ref[...], 2 * P * g + hd), (t, LANES)) for hd in range(2 * P)]
        m_s[...] = jnp.full(m_s.shape, NEG_INF, F32)
        acc_s[...] = jnp.zeros(acc_s.shape, F32)

        def step(j, masked):
            rows = pl.ds(pl.multiple_of(j * t, t), t)
            kb, vb = k_ref[rows, :], v_ref[rows, :]
            for pr in range(P):
                kp, vp = kb[:, pr * W:(pr + 1) * W], vb[:, pr * LANES:(pr + 1) * LANES]
                ones = jnp.ones_like(vp)
                vaug = [jnp.where(lo, vp, ones), jnp.where(lo, ones, vp)]
                for half in range(2):
                    hd = 2 * pr + half
                    kh = kp if split else kp[:, half * LANES:(half + 1) * LANES]
                    s = lax.dot_general(qh[hd], kh, NT, preferred_element_type=F32)
                    if not fold_scale:
                        s = s * scale
                    if decay:
                        s = s + _widen(cq[hd], t) - crow_ref[hd, j]
                    if masked:
                        s = _causal_mask(s)
                    m_prev = m_s[hd]
                    m_new = jnp.maximum(m_prev, jnp.max(s, -1, keepdims=True))
                    p = jnp.exp(s - _widen(m_new, t))
                    acc_s[hd] = jnp.exp(m_prev - m_new) * acc_s[hd] + lax.dot_general(
                        p.astype(BF16), vaug[half], NN, preferred_element_type=F32)
                    m_s[hd] = m_new

        def loop_body(j, carry):
            step(j, False)
            return carry

        lax.fori_loop(0, i, loop_body, 0)
        step(i, True)
        for pr in range(P):
            acc0, acc1 = acc_s[2 * pr], acc_s[2 * pr + 1]
            _, l0 = _both_halves(acc0, lo)
            l1, _ = _both_halves(acc1, lo)
            cols = slice(pr * LANES, (pr + 1) * LANES)
            o_ref[:, cols] = jnp.where(lo, acc0 / l0, acc1 / l1).astype(BF16)
            lse_ref[:, cols] = jnp.where(lo, m_s[2 * pr] + jnp.log(l0), m_s[2 * pr + 1] + jnp.log(l1))
        if n_c:
            @pl.when((ids[0] == n_steps[0] - 1) & (ids[1] == n_steps[1] - 1) & (ids[2] == n_steps[2] - 1))
            def _():
                _comm_wait(kinds, c_in, c_out, sems, place)

    in_specs = [pl.BlockSpec((t, P * W), lambda b, g, i: (b * nq + i, q_blk0 // P + g)),
                pl.BlockSpec((S, P * W), lambda b, g, i: (b, k_blk0 // P + g)),
                pl.BlockSpec((S, P * LANES), lambda b, g, i: (b, v_blk0 // P + g))]
    args = [qa, ka, va]
    if decay:
        in_specs += [pl.BlockSpec((None, t, LANES), lambda b, g, i: (b, i, 0)),
                     pl.BlockSpec((None, 2 * P, nq, 1, t), lambda b, g, i: (b, g, 0, 0, 0))]
        args += [csh, crow]
    out_spec = pl.BlockSpec((t, P * LANES), lambda b, g, i: (b * nq + i, g))
    res = pl.pallas_call(
        body, name=name, grid=n_steps, in_specs=in_specs + [HBM_SPEC] * n_c,
        out_specs=[out_spec, out_spec] + [HBM_SPEC] * n_c,
        out_shape=[jax.ShapeDtypeStruct((B * S, n_pairs * LANES), BF16),
                   jax.ShapeDtypeStruct((B * S, n_pairs * LANES), F32)] + _comm_out_shapes(comm),
        scratch_shapes=[pltpu.VMEM((2 * P, t, LANES), F32), pltpu.VMEM((2 * P, t, LANES), F32)]
        + (_comm_scratch(comm) if n_c else []),
        compiler_params=_cparams(*(("arbitrary",) * 3 if n_c else ("parallel",) * 3)),
    )(*args, *[a for _, a in comm])
    return res[0], res[1], list(res[2:])


def _flash_bwd(qa, ka, va, oa, doa, lsea, *, q_blk0, k_blk0, v_blk0, do_blk0, W, n_pairs, B, S, scale, qk_dtype,
               csh=None, crow=None, comm=(), name):
    t = ATT_TILE_BWD
    nq = S // t
    P = PAIRS_PER_STEP_BWD
    decay = csh is not None
    if decay:
        crow = crow.reshape(B, 2 * n_pairs, nq, 1, t)
    split = W == LANES
    assert n_pairs % P == 0 and q_blk0 % P == 0 and k_blk0 % P == 0 and v_blk0 % P == 0 and do_blk0 % P == 0
    n_c, kinds = len(comm), [k for k, _ in comm]
    n_in, n_out, n_scr = (8, 5, 8) if decay else (6, 3, 5)
    n_steps = (B, n_pairs // P, nq)

    def body(*refs):
        c_in = refs[n_in:n_in + n_c]
        c_out = refs[n_in + n_c + n_out:n_in + 2 * n_c + n_out]
        sems = refs[n_in + 2 * n_c + n_out + n_scr:]
        refs = (refs[:n_in] + refs[n_in + n_c:n_in + n_c + n_out]
                + refs[n_in + 2 * n_c + n_out:n_in + 2 * n_c + n_out + n_scr])
        if n_c:
            place = _mesh_place()
            ids = [pl.program_id(ax) for ax in range(3)]

            @pl.when((ids[0] == 0) & (ids[1] == 0) & (ids[2] == 0))
            def _():
                _comm_start(kinds, c_in, c_out, sems, place)

        if decay:
            (q_ref, k_ref, v_ref, o_ref, do_ref, lse_ref, csh_ref, crow_ref, dq_ref, dk_ref, dv_ref, dck_ref, dcq_ref,
             dq_s, lse_s, delta_s, dk_s, dv_s, cq_s, dcq_s, dck_s) = refs
        else:
            (q_ref, k_ref, v_ref, o_ref, do_ref, lse_ref, dq_ref, dk_ref, dv_ref,
             dq_s, lse_s, delta_s, dk_s, dv_s) = refs
        g, j = pl.program_id(1), pl.program_id(2)
        lo = _low_half((t, LANES))

        @pl.when(j == 0)
        def _():
            lo_s = _low_half((S, LANES))
            dq_s[...] = jnp.zeros(dq_s.shape, F32)
            for pr in range(P):
                cols = slice(pr * LANES, (pr + 1) * LANES)
                lse_s[2 * pr], lse_s[2 * pr + 1] = _both_halves(lse_ref[:, cols], lo_s)
                dd = do_ref[:, cols].astype(F32) * o_ref[:, cols].astype(F32)
                delta_s[2 * pr] = jnp.broadcast_to(jnp.sum(jnp.where(lo_s, dd, 0.0), -1, keepdims=True), (S, LANES))
                delta_s[2 * pr + 1] = jnp.broadcast_to(jnp.sum(jnp.where(lo_s, 0.0, dd), -1, keepdims=True),
                                                       (S, LANES))
            if decay:
                for hd in range(2 * P):
                    cq_s[hd] = jnp.broadcast_to(_head_column(csh_ref[...], 2 * P * g + hd), (S, LANES))
                dcq_s[...] = jnp.zeros(dcq_s.shape, F32)

        kb, vb = k_ref[...], v_ref[...]
        kh, vh = [], []
        for pr in range(P):
            kp, vp = kb[:, pr * W:(pr + 1) * W], vb[:, pr * LANES:(pr + 1) * LANES]
            zk, zv = jnp.zeros_like(kp), jnp.zeros_like(vp)
            kh += [jnp.where(lo, kp, zk), jnp.where(lo, zk, kp)] if split else [kp[:, :LANES], kp[:, LANES:]]
            vh += [jnp.where(lo, vp, zv), jnp.where(lo, zv, vp)]
        dk_s[...] = jnp.zeros(dk_s.shape, F32)
        dv_s[...] = jnp.zeros(dv_s.shape, F32)
        if decay:
            dck_s[...] = jnp.zeros(dck_s.shape, F32)

        def step(i, masked):
            rows = pl.ds(pl.multiple_of(i * t, t), t)
            qi, doi = q_ref[rows, :], do_ref[rows, :]
            for pr in range(P):
                qp, dop = qi[:, pr * W:(pr + 1) * W], doi[:, pr * LANES:(pr + 1) * LANES]
                for half in range(2):
                    hd = 2 * pr + half
                    qx = qp if split else qp[:, half * LANES:(half + 1) * LANES]
                    s = lax.dot_general(qx, kh[hd], NT, preferred_element_type=F32) * scale
                    if decay:
                        s = s + _widen(cq_s[hd, rows, :], t) - crow_ref[hd, j]
                    if masked:
                        s = _causal_mask(s)
                    p = jnp.exp(s - _widen(lse_s[hd, rows, :], t))
                    dv_s[hd] += lax.dot_general(p.astype(BF16), dop, TN, preferred_element_type=F32)
                    dp = lax.dot_general(dop, vh[hd], NT, preferred_element_type=F32)
                    ds = p * (dp - _widen(delta_s[hd, rows, :], t))
                    dss = (ds * scale).astype(BF16)
                    dk_s[hd] += lax.dot_general(dss, qx, TN, preferred_element_type=F32)
                    dqc = lax.dot_general(dss, kh[hd], NN, preferred_element_type=F32)
                    if split:
                        dq_s[rows, pr * W:(pr + 1) * W] += dqc
                    else:
                        dq_s[rows, hd * LANES:(hd + 1) * LANES] += dqc
                    if decay:
                        dck_s[hd] -= jnp.sum(ds, 0, keepdims=True)
                        part = ds[:, :LANES]
                        for c in range(1, t // LANES):
                            part = part + ds[:, c * LANES:(c + 1) * LANES]
                        dcq_s[hd, rows, :] += part

        def loop_body(i, carry):
            step(i, False)
            return carry

        step(j, True)
        lax.fori_loop(j + 1, nq, loop_body, 0)
        for pr in range(P):
            if split:
                dk_ref[:, pr * W:(pr + 1) * W] = jnp.where(lo, dk_s[2 * pr], dk_s[2 * pr + 1]).astype(dk_ref.dtype)
            else:
                for half in range(2):
                    hd = 2 * pr + half
                    dk_ref[:, hd * LANES:(hd + 1) * LANES] = dk_s[hd].astype(dk_ref.dtype)
            dv_ref[:, pr * LANES:(pr + 1) * LANES] = jnp.where(lo, dv_s[2 * pr], dv_s[2 * pr + 1]).astype(BF16)
        if decay:
            dck_ref[...] = dck_s[...]

        @pl.when(j == nq - 1)
        def _():
            dq_ref[...] = dq_s[...].astype(dq_ref.dtype)
            if decay:
                for hd in range(2 * P):
                    dcq_ref[hd] = jnp.sum(dcq_s[hd].T, 0, keepdims=True)

        if n_c:
            @pl.when((ids[0] == n_steps[0] - 1) & (ids[1] == n_steps[1] - 1) & (ids[2] == n_steps[2] - 1))
            def _():
                _comm_wait(kinds, c_in, c_out, sems, place)

    full = lambda w, blk0: pl.BlockSpec((S, P * w), lambda b, g, j: (b, blk0 // P + g))
    blk = lambda w, blk0: pl.BlockSpec((t, P * w), lambda b, g, j: (b * nq + j, blk0 // P + g))
    in_specs = [full(W, q_blk0), blk(W, k_blk0), blk(LANES, v_blk0), full(LANES, 0), full(LANES, do_blk0),
                full(LANES, 0)]
    args = [qa, ka, va, oa, doa, lsea]
    T = B * S
    out_specs = [full(W, 0), blk(W, 0), blk(LANES, 0)]
    out_shape = [jax.ShapeDtypeStruct((T, n_pairs * W), qk_dtype), jax.ShapeDtypeStruct((T, n_pairs * W), qk_dtype),
                 jax.ShapeDtypeStruct((T, n_pairs * LANES), BF16)]
    per_head = lambda rows: pltpu.VMEM((2 * P, rows, LANES), F32)
    scratch = [pltpu.VMEM((S, P * W), F32), per_head(S), per_head(S), per_head(t), per_head(t)]
    if decay:
        in_specs += [pl.BlockSpec((None, S, LANES), lambda b, g, j: (b, 0, 0)),
                     pl.BlockSpec((None, 2 * P, nq, 1, t), lambda b, g, j: (b, g, 0, 0, 0))]
        args += [csh, crow]
        out_specs += [pl.BlockSpec((None, 2 * P, None, 1, t), lambda b, g, j: (b, g, j, 0, 0)),
                      pl.BlockSpec((None, 2 * P, 1, S), lambda b, g, j: (b, g, 0, 0))]
        out_shape += [jax.ShapeDtypeStruct((B, 2 * n_pairs, nq, 1, t), F32),
                      jax.ShapeDtypeStruct((B, 2 * n_pairs, 1, S), F32)]
        scratch += [per_head(S), per_head(S), pltpu.VMEM((2 * P, 1, t), F32)]
    res = pl.pallas_call(
        body, name=name, grid=n_steps, in_specs=in_specs + [HBM_SPEC] * n_c,
        out_specs=out_specs + [HBM_SPEC] * n_c, out_shape=out_shape + _comm_out_shapes(comm),
        scratch_shapes=scratch + (_comm_scratch(comm) if n_c else []),
        compiler_params=_cparams(*(("arbitrary",) * 3 if n_c else ("parallel", "parallel", "arbitrary"))),
    )(*args, *[a for _, a in comm])
    return tuple(res[:n_out]) + (list(res[n_out:]),)


def _swa_common(q_ref, kp_ref, ko_ref, vp_ref, vo_ref, n):
    Q = BLOCK_Q
    lo = _low_half((Q, LANES))
    lo2 = _low_half((2 * Q, LANES))
    kk = jnp.concatenate([kp_ref[...], ko_ref[...]], axis=0)
    vv = jnp.concatenate([vp_ref[...], vo_ref[...]], axis=0)
    kdup = [x.astype(BF16) for x in _both_halves(kk, lo2)]
    vdup = [x.astype(BF16) for x in _both_halves(vv, lo2)]
    a = lax.broadcasted_iota(jnp.int32, (SWA_GROUP * Q, 2 * Q), 0) % Q
    col = lax.broadcasted_iota(jnp.int32, (SWA_GROUP * Q, 2 * Q), 1)
    dist = a + Q - col
    valid = (dist >= 0) & (dist < SWA_WINDOW) & ((col >= Q) | (n > 0))
    qv = q_ref[...]
    qm = []
    for a_head in range(SWA_HEADS):
        qp = qv[:, (a_head // 2) * LANES:(a_head // 2 + 1) * LANES]
        keep = lo if a_head % 2 == 0 else jnp.logical_not(lo)
        qm.append(jnp.where(keep, qp, 0.0).astype(BF16))
    qs = [jnp.concatenate(qm[g * SWA_GROUP:(g + 1) * SWA_GROUP], axis=0) for g in range(SWA_KV_HEADS)]
    return lo, lo2, kdup, vdup, valid, qs


def _swa_group_logits(g, qs, kdup, valid, bias_ref):
    heads = slice(g * SWA_GROUP, (g + 1) * SWA_GROUP)
    s = lax.dot_general(qs[g], kdup[g], NT, preferred_element_type=F32) * (HEAD_DIM ** -0.5)
    s = s + bias_ref[heads].reshape(SWA_GROUP * BLOCK_Q, 2 * BLOCK_Q)
    return heads, jnp.where(valid, s, NEG_INF)


def _pair_halves(x, lo):
    Q = BLOCK_Q
    return [jnp.where(lo, x[2 * pr * Q:(2 * pr + 1) * Q], x[(2 * pr + 1) * Q:(2 * pr + 2) * Q])
            for pr in range(SWA_GROUP // 2)]


def _swa_in_specs(nb):
    Q = BLOCK_Q
    own = lambda blk: (lambda b, n: (b * nb + n, blk))
    prev = lambda blk: (lambda b, n: (b * nb + jnp.maximum(n - 1, 0), blk))
    kb, vb = EV_KS[0] // LANES, EV_VS[0] // LANES
    return [pl.BlockSpec((Q, SWA_HEADS * HEAD_DIM), own(0)), pl.BlockSpec((Q, LANES), prev(kb)),
            pl.BlockSpec((Q, LANES), own(kb)), pl.BlockSpec((Q, LANES), prev(vb)), pl.BlockSpec((Q, LANES), own(vb))]


def _swa_fwd(h, bias, sinkcol, *, B, S, comm=(), name):
    Q = BLOCK_Q
    nb = S // Q
    n_c, kinds = len(comm), [k for k, _ in comm]

    def body(*refs):
        c_in, c_out, sems = refs[7:7 + n_c], refs[9 + n_c:9 + 2 * n_c], refs[9 + 2 * n_c:]
        q_ref, kp_ref, ko_ref, vp_ref, vo_ref, bias_ref, sink_ref = refs[:7]
        o_ref, lse_ref = refs[7 + n_c:9 + n_c]
        if n_c:
            place = _mesh_place()
            ids = [pl.program_id(0), pl.program_id(1)]

            @pl.when((ids[0] == 0) & (ids[1] == 0))
            def _():
                _comm_start(kinds, c_in, c_out, sems, place)

        lo, lo2, kdup, vdup, valid, qs = _swa_common(q_ref, kp_ref, ko_ref, vp_ref, vo_ref, pl.program_id(1))
        pairs = []
        lo4 = _low_half((SWA_GROUP * Q, LANES))
        for g in range(SWA_KV_HEADS):
            heads, s = _swa_group_logits(g, qs, kdup, valid, bias_ref)
            sink = jnp.broadcast_to(sink_ref[heads].reshape(SWA_GROUP * Q, 1), (SWA_GROUP * Q, LANES))
            m = jnp.maximum(jnp.max(s, -1, keepdims=True), sink)
            p = jnp.exp(s - _widen(m, 2 * Q))
            vaug = jnp.where(lo2, vdup[g], jnp.ones_like(vdup[g]))
            pv = lax.dot_general(p.astype(BF16), vaug, NN, preferred_element_type=F32)
            rolled = pltpu.roll(pv, HEAD_DIM, 1)
            l = jnp.where(lo4, rolled, pv) + jnp.exp(sink - m)
            out = pv / l
            lse_g = m + jnp.log(l)
            for i in range(SWA_GROUP):
                a = g * SWA_GROUP + i
                lse_ref[:, a * LANES:(a + 1) * LANES] = lse_g[i * Q:(i + 1) * Q]
            shifted = pltpu.roll(out, HEAD_DIM, 1)
            pairs += [jnp.where(lo, out[2 * pr * Q:(2 * pr + 1) * Q], shifted[(2 * pr + 1) * Q:(2 * pr + 2) * Q])
                      for pr in range(SWA_GROUP // 2)]
        o_ref[...] = jnp.concatenate(pairs, axis=1).astype(BF16)
        if n_c:
            @pl.when((ids[0] == B - 1) & (ids[1] == nb - 1))
            def _():
                _comm_wait(kinds, c_in, c_out, sems, place)

    whole = lambda shape: pl.BlockSpec(shape, lambda b, n: (0,) * len(shape))
    res = pl.pallas_call(
        body, name=name, grid=(B, nb),
        in_specs=_swa_in_specs(nb) + [whole((SWA_HEADS, Q, 2 * Q)), whole((SWA_HEADS, Q, 1))] + [HBM_SPEC] * n_c,
        out_specs=[pl.BlockSpec((Q, SWA_HEADS * HEAD_DIM), lambda b, n: (b * nb + n, 0)),
                   pl.BlockSpec((Q, SWA_HEADS * LANES), lambda b, n: (b * nb + n, 0))] + [HBM_SPEC] * n_c,
        out_shape=[jax.ShapeDtypeStruct((B * S, SWA_HEADS * HEAD_DIM), BF16),
                   jax.ShapeDtypeStruct((B * S, SWA_HEADS * LANES), F32)] + _comm_out_shapes(comm),
        scratch_shapes=_comm_scratch(comm) if n_c else [],
        compiler_params=_cparams(*(("arbitrary",) * 2 if n_c else ("parallel",) * 2)),
    )(h, h, h, h, h, bias, sinkcol, *[a for _, a in comm])
    return res[0], res[1], list(res[2:])


def _swa_bwd(h, o, do, lse, bias, sinkcol, *, do_blk0, B, S, name):
    Q = BLOCK_Q
    nb = S // Q
    scale = HEAD_DIM ** -0.5

    def body(q_ref, kp_ref, ko_ref, vp_ref, vo_ref, o_ref, do_ref, lse_ref, bias_ref, sink_ref,
             dq_ref, dko_ref, dkp_ref, dvo_ref, dvp_ref, dbias_ref, dsink_ref):
        @pl.when((pl.program_id(0) == 0) & (pl.program_id(1) == 0))
        def _():
            dbias_ref[...] = jnp.zeros_like(dbias_ref)
            dsink_ref[...] = jnp.zeros_like(dsink_ref)

        lo, lo2, kdup, vdup, valid, qs = _swa_common(q_ref, kp_ref, ko_ref, vp_ref, vo_ref, pl.program_id(1))
        dkk, dvv, dq_pairs = [], [], []
        for g in range(SWA_KV_HEADS):
            heads, s = _swa_group_logits(g, qs, kdup, valid, bias_ref)
            lse_g = jnp.concatenate([lse_ref[:, a * LANES:(a + 1) * LANES]
                                     for a in range(g * SWA_GROUP, (g + 1) * SWA_GROUP)], axis=0)
            p = jnp.exp(s - _widen(lse_g, 2 * Q))
            do_g, o_g = [], []
            for i in range(SWA_GROUP):
                cols = slice((g * SWA_GROUP + i) // 2 * LANES, ((g * SWA_GROUP + i) // 2 + 1) * LANES)
                do_p = do_ref[:, cols]
                do_g.append(jnp.where(lo if i % 2 == 0 else jnp.logical_not(lo), do_p, jnp.zeros_like(do_p)))
                o_g.append(o_ref[:, cols])
            doh, oh = jnp.concatenate(do_g, axis=0), jnp.concatenate(o_g, axis=0)
            delta = jnp.sum(doh.astype(F32) * oh.astype(F32), -1, keepdims=True)
            dp = lax.dot_general(doh, vdup[g], NT, preferred_element_type=F32)
            ds = p * (dp - delta)
            dbias_ref[heads] += ds.reshape(SWA_GROUP, Q, 2 * Q)
            dsink_ref[heads] -= (jnp.exp(sink_ref[heads].reshape(SWA_GROUP * Q, 1) - lse_g[:, :1])
                                 * delta).reshape(SWA_GROUP, Q, 1)
            dss = (ds * scale).astype(BF16)
            dq_pairs += _pair_halves(lax.dot_general(dss, kdup[g], NN, preferred_element_type=F32), lo)
            dkk.append(lax.dot_general(dss, qs[g], TN, preferred_element_type=F32))
            dvv.append(lax.dot_general(p.astype(BF16), doh, TN, preferred_element_type=F32))
        dq_ref[...] = jnp.concatenate(dq_pairs, axis=1).astype(BF16)
        fold = lambda x: x + pltpu.roll(x, HEAD_DIM, 1)
        dk_blk = jnp.where(lo2, fold(dkk[0]), fold(dkk[1]))
        dv_blk = jnp.where(lo2, fold(dvv[0]), fold(dvv[1]))
        dkp_ref[...] = dk_blk[:Q]
        dko_ref[...] = dk_blk[Q:]
        dvp_ref[...] = dv_blk[:Q]
        dvo_ref[...] = dv_blk[Q:]

    whole = lambda shape: pl.BlockSpec(shape, lambda b, n: (0,) * len(shape))
    wide = lambda blk: pl.BlockSpec((Q, SWA_HEADS * HEAD_DIM), lambda b, n: (b * nb + n, blk))
    narrow = pl.BlockSpec((Q, LANES), lambda b, n: (b * nb + n, 0))
    kv_shape = jax.ShapeDtypeStruct((B * S, LANES), F32)
    return pl.pallas_call(
        body, name=name, grid=(B, nb),
        in_specs=_swa_in_specs(nb) + [wide(0), wide(do_blk0),
                                      pl.BlockSpec((Q, SWA_HEADS * LANES), lambda b, n: (b * nb + n, 0)),
                                      whole((SWA_HEADS, Q, 2 * Q)),
                                      whole((SWA_HEADS, Q, 1))],
        out_specs=[wide(0), narrow, narrow, narrow, narrow, whole((SWA_HEADS, Q, 2 * Q)), whole((SWA_HEADS, Q, 1))],
        out_shape=[jax.ShapeDtypeStruct((B * S, SWA_HEADS * HEAD_DIM), BF16), kv_shape, kv_shape, kv_shape, kv_shape,
                   jax.ShapeDtypeStruct((SWA_HEADS, Q, 2 * Q), F32), jax.ShapeDtypeStruct((SWA_HEADS, Q, 1), F32)],
        compiler_params=_cparams("arbitrary", "arbitrary"),
    )(h, h, h, h, h, o, do, lse, bias, sinkcol)


def _bias_bucket_sum(dbias, bucket, *, name):
    def body(d_ref, b_ref, o_ref):
        dbv, bk = d_ref[...], b_ref[...]
        lane = lax.broadcasted_iota(jnp.int32, (SWA_HEADS, LANES), 1)
        out = jnp.zeros((SWA_HEADS, LANES), F32)
        for b in range(REL_BUCKETS):
            part = jnp.sum(jnp.where(bk == b, dbv, 0.0), axis=1)
            tot = jnp.sum(part, axis=-1, keepdims=True)
            out = out + jnp.where(lane == b, tot, 0.0)
        o_ref[...] = out

    return pl.pallas_call(
        body, name=name, out_shape=jax.ShapeDtypeStruct((SWA_HEADS, LANES), F32),
        compiler_params=pltpu.CompilerParams(vmem_limit_bytes=VMEM_LIMIT_BYTES),
    )(dbias, bucket)


def _adamw_update(w, g, m, v):
    m_new = ADAM_B1 * m + (1.0 - ADAM_B1) * g
    v_new = ADAM_B2 * v + (1.0 - ADAM_B2) * jnp.square(g)
    m_hat = m_new / (1.0 - ADAM_B1 ** ADAM_STEP)
    v_hat = v_new / (1.0 - ADAM_B2 ** ADAM_STEP)
    return -ADAM_LR * (m_hat / (jnp.sqrt(v_hat) + ADAM_EPS) + ADAM_WD * w), m_new, v_new


def _adamw(w, g, m, v, *, name):
    def body(w_ref, g_ref, m_ref, v_ref, d_ref, nm_ref, nv_ref):
        d_ref[...], nm_ref[...], nv_ref[...] = _adamw_update(w_ref[...], g_ref[...], m_ref[...], v_ref[...])

    return pl.pallas_call(
        body, name=name, out_shape=[jax.ShapeDtypeStruct(w.shape, F32)] * 3,
        compiler_params=pltpu.CompilerParams(vmem_limit_bytes=VMEM_LIMIT_BYTES),
    )(w, g, m, v)


ADAMW_PARTS_BYTES = 8 * 1024 * 1024


def _adamw_slots(w, parts, m, v, *, name):
    n0, R, C = w.shape
    tr = next((c for c in (512, 256, 128, 64, 32, 16, 8) if R % c == 0 and 4 * n0 * N_DEV * c * C <= ADAMW_PARTS_BYTES), R)

    def body(*refs):
        w_ref, p_refs, (m_ref, v_ref, g_ref, d_ref, nm_ref, nv_ref) = refs[0], refs[1:1 + n0], refs[1 + n0:]
        layer = pl.program_id(0)
        for l in range(n0):
            @pl.when(layer == l)
            def _(p_ref=p_refs[l]):
                g = p_ref[0].astype(F32)
                for j in range(1, N_DEV):
                    g = g + p_ref[j].astype(F32)
                g_ref[...] = g
                d_ref[...], nm_ref[...], nv_ref[...] = _adamw_update(w_ref[...], g, m_ref[...], v_ref[...])

    spec = pl.BlockSpec((None, tr, C), lambda l, i: (l, i, 0))
    part_spec = lambda own: pl.BlockSpec((N_DEV, tr, C), lambda l, i: (0, jnp.where(l == own, i, 0), 0))
    return pl.pallas_call(
        body, name=name, grid=(n0, R // tr),
        in_specs=[spec] + [part_spec(l) for l in range(n0)] + [spec, spec], out_specs=[spec] * 4,
        out_shape=[jax.ShapeDtypeStruct((n0, R, C), F32)] * 4, compiler_params=_cparams("arbitrary", "arbitrary"),
    )(w, *parts, m, v)


def _all_gather_hbm(blocks, *, name):
    n = len(blocks)

    def body(*refs):
        x_refs, out_refs = refs[:n], refs[n:2 * n]
        send_sems, recv_sems, local_sems = refs[2 * n:]
        x, y, c, _ = _mesh_place()
        me, sibling = (x, y, c), (x, y, 1 - c)
        chips = [(1 - x, y), (x, 1 - y), (1 - x, 1 - y)]

        def copy(w, k, blk, to, src=None):
            px, py, pc = blk
            slot = out_refs[w].at[4 * px + 2 * py + pc]
            return pltpu.make_async_remote_copy(
                src_ref=slot if src is None else src, dst_ref=slot,
                send_sem=send_sems.at[w, k], recv_sem=recv_sems.at[w, k], device_id=to, device_id_type=MESH_ID)

        mine = [pltpu.make_async_copy(x_refs[w], out_refs[w].at[4 * x + 2 * y + c], local_sems.at[w])
                for w in range(n)]
        for cp in mine:
            cp.start()
        first = []
        for w in range(n):
            first.append(copy(w, 0, me, sibling, src=x_refs[w]))
            first += [copy(w, 1 + j, me, (*chip, c), src=x_refs[w]) for j, chip in enumerate(chips)]
        for cp in first:
            cp.start()
        passed = []
        for j, chip in enumerate(chips):
            for w in range(n):
                copy(w, 1 + j, (*chip, c), me).wait_recv()
                fwd = copy(w, 4 + j, (*chip, c), sibling)
                fwd.start()
                passed.append(fwd)
        for w in range(n):
            copy(w, 0, sibling, me).wait_recv()
            for j, chip in enumerate(chips):
                copy(w, 4 + j, (*chip, 1 - c), me).wait_recv()
        for cp in first + passed:
            cp.wait_send()
        for cp in mine:
            cp.wait()

    return pl.pallas_call(
        body, name=name, out_shape=[jax.ShapeDtypeStruct((N_DEV,) + b.shape, b.dtype) for b in blocks],
        in_specs=[HBM_SPEC] * n, out_specs=[HBM_SPEC] * n,
        scratch_shapes=[pltpu.SemaphoreType.DMA((n, 7)), pltpu.SemaphoreType.DMA((n, 7)),
                        pltpu.SemaphoreType.DMA((n,))],
    )(*blocks)


def _all_reduce_small(block, *, name):
    R, W = block.shape

    def body(x_ref, out_ref, buf, send_sems, recv_sems):
        x, y, c, me = _mesh_place()
        copies = []
        for k, (peer, _) in enumerate(_peers(x, y, c)):
            copies.append(pltpu.make_async_remote_copy(
                src_ref=x_ref, dst_ref=buf.at[me], send_sem=send_sems.at[k], recv_sem=recv_sems.at[k],
                device_id=peer, device_id_type=MESH_ID))
        for cp in copies:
            cp.start()
        buf[me] = x_ref[...]
        for cp in copies:
            cp.wait_recv()
        for cp in copies:
            cp.wait_send()
        acc = buf[0]
        for j in range(1, N_DEV):
            acc = acc + buf[j]
        out_ref[...] = acc

    return pl.pallas_call(
        body, name=name, out_shape=jax.ShapeDtypeStruct((R, W), F32),
        in_specs=[VMEM_SPEC], out_specs=VMEM_SPEC,
        scratch_shapes=[pltpu.VMEM((N_DEV, R, W), F32), pltpu.SemaphoreType.DMA((7,)), pltpu.SemaphoreType.DMA((7,))],
    )(block)


def _assemble(name, g):
    if BIG_AXIS[name] == 2:
        return jnp.concatenate([g[j] for j in range(N_DEV)], axis=1)
    return g.reshape(N_DEV * g.shape[1], g.shape[2])


def _split_for_devices(name, g):
    if BIG_AXIS[name] == 2:
        b = g.shape[1] // N_DEV
        return jnp.stack([g[:, j * b:(j + 1) * b] for j in range(N_DEV)]).astype(BF16)
    return g.reshape(N_DEV, g.shape[0] // N_DEV, g.shape[1]).astype(BF16)


def _layer_weight_keys(i):
    j = i // 2
    mixer = [('ev_w_in', j), ('ev_w_uq', j), ('ev_w_ukv', j), ('ev_w_out', j)] if i % 2 == 0 \
        else [('od_w_in', j), ('od_w_out', j)]
    return mixer + [('w_up', i), ('w_down', i), ('ple_w_proj', i), ('ple_w_gate', i)]


def _weight_layer(key):
    name, idx = key
    return 2 * idx if name.startswith('ev_') else 2 * idx + 1 if name.startswith('od_') else idx


FIRST_GATHER = [('ev_w_in', 0), ('ev_w_uq', 0), ('ev_w_ukv', 0), ('ev_w_out', 0)]
FWD_CARRIERS = {
    'l0_mla': [('w_up', 0), ('ple_w_proj', 0), ('ple_w_gate', 0)],
    'l0_swa': [('w_down', 0)],
    'l0_out_ln1': [('od_w_out', 0)],
    'l0_up': [('od_w_in', 0)],
    'l0_down_ln2': [('w_up', 1)],
    'l0_ple_gate': [('ple_w_proj', 1), ('ple_w_gate', 1)],
    'l1_fox': [('w_down', 1), ('ev_w_in', 1), ('ev_w_uq', 1), ('ev_w_ukv', 1), ('ev_w_out', 1), ('w_up', 2)],
    'l1_up': [('w_down', 2)],
    'l1_down_ln2': [('ple_w_proj', 2), ('ple_w_gate', 2)],
    'l2_mla': [('od_w_in', 1), ('od_w_out', 1)],
    'l2_swa': [('w_up', 3)],
    'l2_up': [('w_down', 3)],
    'l2_down_ln2': [('ple_w_proj', 3), ('ple_w_gate', 3)],
}


class _MeshExchange:
    def __init__(self, shards):
        self.shards = shards
        self.weights = {i: {} for i in range(DEPTH)}
        self.pending = []
        self.in_flight = []
        self.received = {}
        got = _all_gather_hbm([self.shards[n][idx] for n, idx in FIRST_GATHER], name="gather_first")
        self._landed(FIRST_GATHER, got)

    def _landed(self, keys, gathered):
        for k, g in zip(keys, gathered):
            self.weights[_weight_layer(k)][k[0]] = _assemble(k[0], g)

    def layer_weights(self, i):
        return self.weights[i]

    def carry(self, kernel_name):
        return [(("gather", idx), self.shards[n]) for n, idx in FWD_CARRIERS.get(kernel_name, [])]

    def carried(self, kernel_name, outs):
        self._landed(FWD_CARRIERS.get(kernel_name, []), outs)

    def push_grads(self, grads):
        self.pending += [(k, _split_for_devices(k[0], g)) for k, g in grads.items()]

    def bwd_items(self):
        self.in_flight, self.pending = self.pending, []
        return [("scatter", parts) for _, parts in self.in_flight]

    def bwd_done(self, outs):
        for (k, _), got in zip(self.in_flight, outs):
            self.received[k] = got
        self.in_flight = []

    def finish(self):
        if self.pending:
            outs = _exchange(self.bwd_items(), name="scatter_rest")
            self.bwd_done(outs)
        return self.received


PACK_ROWS = 8


def _pack_small(vals):
    flat = jnp.concatenate([vals[n].reshape(-1).astype(F32) for n in SMALL])
    pad = (-flat.shape[0]) % (PACK_ROWS * LANES)
    return jnp.pad(flat, (0, pad)).reshape(-1, LANES)


def _unpack_small(block, shapes):
    flat = block.reshape(-1)
    out, off = {}, 0
    for n in SMALL:
        sz = math.prod(shapes[n])
        out[n] = flat[off:off + sz].reshape(shapes[n])
        off += sz
    return out


def _rope_tables(S):
    half = MLA_ROPE // 2
    inv = 1.0 / (ROPE_THETA ** (jnp.arange(0, MLA_ROPE, 2, dtype=F32) / MLA_ROPE))
    ang = jnp.arange(S, dtype=F32)[:, None] * inv[None, :]
    cos, sin = jnp.cos(ang), jnp.sin(ang)
    zeros = jnp.zeros((S, half), F32)
    tail = jnp.zeros((S, LANES - MLA_QK), F32)

    def block(rope_part, nope_val):
        return jnp.concatenate([jnp.full((S, MLA_NOPE), nope_val, F32), rope_part, tail], -1)

    a_r = jnp.concatenate([cos, cos], -1)
    bm_r = jnp.concatenate([-sin, zeros], -1)
    bp_r = jnp.concatenate([zeros, sin], -1)
    q_tabs = tuple(block(r, v) for r, v in ((a_r, 1.0), (bm_r, 0.0), (bp_r, 0.0)))
    k_tabs = tuple(block(r, 0.0) for r in (a_r, bm_r, bp_r))
    return q_tabs, k_tabs


def _t5_bucket(dist):
    exact = REL_BUCKETS // 2
    d = jnp.maximum(dist, 1).astype(F32)
    large = exact + (jnp.log(d / exact) / math.log(REL_MAX_DIST / exact) * (REL_BUCKETS - exact)).astype(jnp.int32)
    large = jnp.minimum(large, REL_BUCKETS - 1)
    return jnp.where(dist < exact, dist, large)


def _swa_bucket_table():
    a = jnp.arange(BLOCK_Q)[:, None]
    col = jnp.arange(2 * BLOCK_Q)[None, :]
    return _t5_bucket(jnp.maximum(a + BLOCK_Q - col, 0)).astype(jnp.int32)


def _even_weights(W):
    w = W['ev_w_in']
    c_kv1 = MLA_Q_LORA + MLA_KV_LORA
    c_kr1 = c_kv1 + MLA_ROPE
    c_qs1 = c_kr1 + SWA_HEADS * HEAD_DIM
    zeros = lambda n: jnp.zeros((D_MODEL, n), w.dtype)
    w_in = jnp.concatenate([w[:, c_kr1:c_qs1], w[:, :c_kv1], w[:, c_qs1:], zeros(KR_LANE0), w[:, c_kv1:c_kr1],
                            zeros(LANES - KR_LANE0 - MLA_ROPE)], axis=1)
    uq = W['ev_w_uq'].reshape(MLA_Q_LORA, MLA_HEADS, MLA_QK)
    w_uq = jnp.pad(uq, ((0, 0), (0, 0), (0, LANES - MLA_QK))).reshape(MLA_Q_LORA, MLA_HEADS * LANES)
    ukv = W['ev_w_ukv'].reshape(MLA_KV_LORA, MLA_HEADS, MLA_NOPE + MLA_V)
    w_k = jnp.pad(ukv[..., :MLA_NOPE], ((0, 0), (0, 0), (0, LANES - MLA_NOPE))).reshape(MLA_KV_LORA, -1)
    w_v = ukv[..., MLA_NOPE:].reshape(MLA_KV_LORA, MLA_HEADS * MLA_V)
    return w_in, w_uq, w_k, w_v, W['ev_w_out']


def _even_in_grad_unpad(dw):
    kr0 = EV_KR[0] + KR_LANE0
    return jnp.concatenate([dw[:, EV_CQ[0]:EV_CKV[1]], dw[:, kr0:kr0 + MLA_ROPE], dw[:, EV_QS[0]:EV_QS[1]],
                            dw[:, EV_KS[0]:EV_VS[1]]], axis=1)


def _even_fwd(xb, W, P, i, B, S, tabs, xchg, tag):
    j = i // 2
    q_tabs, k_tabs, bias, sinkcol = tabs
    w_in, w_uq, w_k, w_v, w_out = _even_weights(W)
    h = _mm(xb, w_in, name=f"{tag}_in")
    cqn, ckvn, rq, rkv = _even_norms(h, P['ev_q_norm'][j][None], P['ev_kv_norm'][j][None], name=f"{tag}_norms")
    q = _rope(_mm(cqn, w_uq, name=f"{tag}_uq"), q_tabs, S, sign=1.0, name=f"{tag}_ropeq")
    knp = _mm(ckvn, w_k, out_dtypes=(BF16,), name=f"{tag}_uk")
    v = _mm(ckvn, w_v, out_dtypes=(BF16,), name=f"{tag}_uv")
    k = _mla_keys(knp, h, k_tabs, S, name=f"{tag}_keys")
    o_mla, lse_mla, got = _flash_fwd(q, k, v, q_blk0=0, k_blk0=0, v_blk0=0, W=2 * LANES, n_pairs=MLA_HEADS // 2,
                                     B=B, S=S, scale=MLA_QK ** -0.5, comm=xchg.carry(f"{tag}_mla"), name=f"{tag}_mla")
    xchg.carried(f"{tag}_mla", got)
    o_swa, lse_swa, got = _swa_fwd(h, bias, sinkcol, B=B, S=S, comm=xchg.carry(f"{tag}_swa"), name=f"{tag}_swa")
    xchg.carried(f"{tag}_swa", got)
    o_cat = jnp.concatenate([o_mla, o_swa], axis=-1)
    res = dict(h=h, cqn=cqn, ckvn=ckvn, rq=rq, rkv=rkv, q=q, k=k, v=v, o_mla=o_mla, lse_mla=lse_mla,
               o_swa=o_swa, lse_swa=lse_swa, o_cat=o_cat)
    return (o_cat, w_out), res


def _shift_prev(own, prev, B, S):
    prev = prev.reshape(B, S, LANES)
    shifted = jnp.concatenate([prev[:, BLOCK_Q:], jnp.zeros_like(prev[:, :BLOCK_Q])], axis=1)
    return (own + shifted.reshape(B * S, LANES)).astype(BF16)


def _even_bwd(dmb, dz1, xb, W, P, j, B, S, tabs, res, xchg, tag):
    q_tabs, k_tabs, bias, sinkcol = tabs
    w_in, w_uq, w_k, w_v, w_out = _even_weights(W)
    g = {}
    g['ev_w_out'] = _mm_tn(res['o_cat'], dmb, name=f"{tag}_dwout")
    do = _mm(dmb, w_out, trans_b=True, out_dtypes=(BF16,), name=f"{tag}_do")
    dq, dk, dv, got = _flash_bwd(res['q'], res['k'], res['v'], res['o_mla'], do, res['lse_mla'], q_blk0=0, k_blk0=0,
                                 v_blk0=0, do_blk0=0, W=2 * LANES, n_pairs=MLA_HEADS // 2, B=B, S=S,
                                 scale=MLA_QK ** -0.5, qk_dtype=F32, comm=xchg.bwd_items(), name=f"{tag}_mla_bwd")
    xchg.bwd_done(got)
    dq_pre = _rope(dq, q_tabs, S, sign=-1.0, name=f"{tag}_ropeq_bwd")
    dw_uq = _mm_tn(res['cqn'], dq_pre, name=f"{tag}_dwuq")
    g['ev_w_uq'] = dw_uq.reshape(MLA_Q_LORA, MLA_HEADS, LANES)[..., :MLA_QK].reshape(MLA_Q_LORA, MLA_HEADS * MLA_QK)
    dcqn = _mm(dq_pre, w_uq, trans_b=True, name=f"{tag}_dcqn")
    dw_k = _mm_tn(res['ckvn'], dk, name=f"{tag}_dwuk").reshape(MLA_KV_LORA, MLA_HEADS, LANES)[..., :MLA_NOPE]
    dw_v = _mm_tn(res['ckvn'], dv, name=f"{tag}_dwuv").reshape(MLA_KV_LORA, MLA_HEADS, MLA_V)
    g['ev_w_ukv'] = jnp.concatenate([dw_k, dw_v], axis=-1).reshape(MLA_KV_LORA, MLA_HEADS * (MLA_NOPE + MLA_V))
    dckvn_v = _mm(dv, w_v, trans_b=True, name=f"{tag}_dckvn_v")
    dckvn = _mm(dk, w_k, trans_b=True, extras=(dckvn_v,), epilogue=lambda acc, r: (acc + r,), name=f"{tag}_dckvn")
    dkr_pre = _mla_rope_key_grad(dk, k_tabs, S, name=f"{tag}_ropek_bwd")
    dqs, dko, dkp, dvo, dvp, dbias, dsink = _swa_bwd(res['h'], res['o_swa'], do, res['lse_swa'], bias, sinkcol,
                                                     do_blk0=1, B=B, S=S, name=f"{tag}_swa_bwd")
    dh, dgq, dgkv = _even_in_bwd(res['h'], res['rq'], res['rkv'], P['ev_q_norm'][j][None], P['ev_kv_norm'][j][None],
                                 dcqn, dckvn, dqs, _shift_prev(dko, dkp, B, S), _shift_prev(dvo, dvp, B, S), dkr_pre,
                                 name=f"{tag}_in_bwd")
    g['ev_w_in'] = _even_in_grad_unpad(_mm_tn(xb, dh, name=f"{tag}_dwin"))
    xchg.push_grads({(n, j): val for n, val in g.items()})
    dx_kwargs = dict(trans_b=True, extras=(dz1,), epilogue=lambda acc, r: (acc + DN_ALPHA * r,), name=f"{tag}_dx")
    dx = _scattering(xchg, _mm, dh, w_in, **dx_kwargs) if j == 0 else _mm(dh, w_in, **dx_kwargs)
    small = dict(ev_q_norm=dgq[0], ev_kv_norm=dgkv[0], dbias=dbias, ev_sinks=jnp.sum(dsink, axis=(1, 2)))
    return dx, small


def _odd_fwd(xb, W, P, i, B, S, xchg, tag):
    j = i // 2
    w = W['od_w_in']
    w_qkv = w[:, :ODD_QKV]
    w_f = jnp.pad(w[:, ODD_QKV:], ((0, 0), (0, LANES - FOX_HEADS)))
    bf = jnp.pad(P['od_b_f'][j], (0, LANES - FOX_HEADS))[None]
    qkv = _mm(xb, w_qkv, out_dtypes=(BF16,), name=f"{tag}_qkv")
    f = _mm(xb, w_f, name=f"{tag}_f").reshape(B, S, LANES)
    csh, chs = _fox_decay_fwd(f, bf, name=f"{tag}_decay")
    crow = chs[:, :FOX_HEADS].reshape(B, FOX_HEADS, S // ATT_TILE, 1, ATT_TILE)
    n_blk = FOX_HEADS * HEAD_DIM // LANES
    o, lse, got = _flash_fwd(qkv, qkv, qkv, q_blk0=0, k_blk0=n_blk, v_blk0=2 * n_blk, W=LANES,
                             n_pairs=FOX_HEADS // 2, B=B, S=S, scale=HEAD_DIM ** -0.5, csh=csh, crow=crow,
                             comm=xchg.carry(f"{tag}_fox"), name=f"{tag}_fox")
    xchg.carried(f"{tag}_fox", got)
    res = dict(f=f, bf=bf, csh=csh, crow=crow, qkv=qkv, o=o, lse=lse, w_qkv=w_qkv, w_f=w_f)
    return (o, W['od_w_out']), res


def _odd_bwd(dmb, dz1, xb, W, P, j, B, S, res, xchg, tag):
    g = {}
    w_out = W['od_w_out']
    g['od_w_out'] = _mm_tn(res['o'], dmb, name=f"{tag}_dwout")
    do = _mm(dmb, w_out, trans_b=True, out_dtypes=(BF16,), name=f"{tag}_do")
    qkv = res['qkv']
    n_blk = FOX_HEADS * HEAD_DIM // LANES
    dq, dk, dv, dck, dcq, got = _flash_bwd(qkv, qkv, qkv, res['o'], do, res['lse'], q_blk0=0, k_blk0=n_blk,
                                           v_blk0=2 * n_blk, do_blk0=0, W=LANES, n_pairs=FOX_HEADS // 2, B=B, S=S,
                                           scale=HEAD_DIM ** -0.5, qk_dtype=BF16, csh=res['csh'], crow=res['crow'],
                                           comm=xchg.bwd_items(), name=f"{tag}_fox_bwd")
    xchg.bwd_done(got)
    dc = dck.reshape(B, FOX_HEADS, S) + dcq.reshape(B, FOX_HEADS, S)
    dc_hs = jnp.pad(dc, ((0, 0), (0, LANES - FOX_HEADS), (0, 0)))
    df, dbf = _fox_decay_bwd(dc_hs, res['f'], res['bf'], name=f"{tag}_decay_bwd")
    df = df.reshape(B * S, LANES)
    dqkv = jnp.concatenate([dq, dk, dv], axis=-1)
    dw_qkv = _mm_tn(xb, dqkv, name=f"{tag}_dwqkv")
    dw_f = _mm_tn(xb, df, name=f"{tag}_dwf")
    g['od_w_in'] = jnp.concatenate([dw_qkv, dw_f[:, :FOX_HEADS]], axis=1)
    dxf = _mm(df, res['w_f'], trans_b=True, extras=(dz1,), epilogue=lambda acc, r: (acc + DN_ALPHA * r,),
              name=f"{tag}_dxf")
    xchg.push_grads({(n, j): val for n, val in g.items()})
    dx = _mm(dqkv, res['w_qkv'], trans_b=True, extras=(dxf,), epilogue=lambda acc, r: (acc + r,), name=f"{tag}_dx")
    small = dict(od_b_f=dbf[0, :FOX_HEADS])
    return dx, small


def _carrying(xchg, name, call, *args, **kwargs):
    comm = xchg.carry(name)
    out = call(*args, comm=comm, name=name, **kwargs)
    if comm:
        out, got = out
        xchg.carried(name, got)
    return out


def _scattering(xchg, call, *args, **kwargs):
    comm = xchg.bwd_items()
    out = call(*args, comm=comm, **kwargs)
    if comm:
        out, got = out
        xchg.bwd_done(got)
    return out


def _local_step(x, p, target, P, xchg):
    B, S, D = x.shape
    T = B * S
    q_tabs, k_tabs = _rope_tables(S)
    bucket = _swa_bucket_table()
    in_bucket = (bucket[..., None] == jnp.arange(REL_BUCKETS)).astype(F32)
    bias = jnp.einsum('acb,bh->hac', in_bucket, P['rel_bias'], precision=lax.Precision.HIGHEST)

    xc = x.reshape(T, D)
    xcb = xc.astype(BF16)
    saved = []
    for i in range(DEPTH):
        j = i // 2
        tag = f"l{i}"
        W = xchg.layer_weights(i)
        lay = dict(xb=xcb, W=W)
        if i % 2 == 0:
            sinkcol = jnp.broadcast_to(P['ev_sinks'][j][:, None, None], (SWA_HEADS, BLOCK_Q, 1)).astype(F32)
            lay['tabs'] = (q_tabs, k_tabs, bias, sinkcol)
            (o, w_out), lay['mix'] = _even_fwd(xcb, W, P, i, B, S, lay['tabs'], xchg, tag)
        else:
            (o, w_out), lay['mix'] = _odd_fwd(xcb, W, P, i, B, S, xchg, tag)
        x1, x1b, lay['xh1'], lay['r1'] = _carrying(xchg, f"{tag}_out_ln1", _mm_ln, o, w_out, xc,
                                                   P['ln1_g'][i][None], P['ln1_b'][i][None])
        lay['x1b'] = x1b
        lay['u'], lay['a'] = _carrying(xchg, f"{tag}_up", _mm, x1b, W['w_up'], out_dtypes=(F32, BF16),
                                       epilogue=lambda acc: (acc, jnp.square(jnp.maximum(acc, 0.0))))
        x2, x2b, lay['xh2'], lay['r2'] = _carrying(xchg, f"{tag}_down_ln2", _mm_ln, lay['a'], W['w_down'], x1,
                                                   P['ln2_g'][i][None], P['ln2_b'][i][None])
        lay['x2b'] = x2b
        lay['p'] = p[i].reshape(T, D_PLE)
        lay['e'] = _mm(lay['p'], W['ple_w_proj'], name=f"{tag}_ple_proj")

        def gate(acc, bg, e, x2v):
            gv = 1.0 / (1.0 + jnp.exp(-(acc + bg)))
            y = x2v + gv * e
            return y, y, gv

        xc, xcb, lay['g'] = _carrying(xchg, f"{tag}_ple_gate", _mm, x2b, W['ple_w_gate'],
                                      extras=(P['ple_b_gate'][i][None], lay['e'], x2), epilogue=gate,
                                      out_dtypes=(F32, BF16, F32))
        saved.append(lay)

    dy, sq = _loss_grad(xc, target.reshape(T, D), name="loss")

    Gs = {n: [None] * DEPTH for n in ('ln1_g', 'ln1_b', 'ln2_g', 'ln2_b', 'ple_b_gate')}
    Gs.update({n: [None] * (DEPTH // 2) for n in ('ev_q_norm', 'ev_kv_norm', 'ev_sinks', 'od_b_f')})
    dbias_total = None
    for i in reversed(range(DEPTH)):
        j = i // 2
        tag = f"l{i}b"
        lay = saved[i]
        W = lay['W']
        de, dzg, dbg = _ple_bwd_elem(dy, lay['g'], lay['e'], name=f"{tag}_ple_elem")
        Gs['ple_b_gate'][i] = dbg[0]
        g_mlp = {('ple_w_proj', i): _mm_tn(lay['p'], de, name=f"{tag}_dwproj"),
                 ('ple_w_gate', i): _mm_tn(lay['x2b'], dzg, name=f"{tag}_dwgate")}
        dz2, dz2b, dg2, db2 = _mm_ln_bwd(dzg, W['ple_w_gate'], dy, 1.0, lay['xh2'], lay['r2'], P['ln2_g'][i][None],
                                         name=f"{tag}_dx2_ln2")
        Gs['ln2_g'][i], Gs['ln2_b'][i] = dg2[0], db2[0]
        g_mlp[('w_down', i)] = _mm_tn(lay['a'], dz2b, name=f"{tag}_dwdown")
        du = _mm(dz2b, W['w_down'], trans_b=True, extras=(lay['u'],), out_dtypes=(BF16,),
                 epilogue=lambda acc, u: (acc * (2.0 * jnp.maximum(u, 0.0)),), name=f"{tag}_du")
        g_mlp[('w_up', i)] = _mm_tn(lay['x1b'], du, name=f"{tag}_dwup")
        xchg.push_grads(g_mlp)
        dz1, dz1b, dg1, db1 = _mm_ln_bwd(du, W['w_up'], dz2, DN_ALPHA, lay['xh1'], lay['r1'], P['ln1_g'][i][None],
                                         name=f"{tag}_dx1_ln1")
        Gs['ln1_g'][i], Gs['ln1_b'][i] = dg1[0], db1[0]
        if i % 2 == 0:
            dy, small = _even_bwd(dz1b, dz1, lay['xb'], W, P, j, B, S, lay['tabs'], lay['mix'], xchg, tag)
            dbias_total = small['dbias'] if dbias_total is None else dbias_total + small['dbias']
            for n in ('ev_q_norm', 'ev_kv_norm', 'ev_sinks'):
                Gs[n][j] = small[n]
        else:
            dy, small = _odd_bwd(dz1b, dz1, lay['xb'], W, P, j, B, S, lay['mix'], xchg, tag)
            Gs['od_b_f'][j] = small['od_b_f']

    grads_small = {n: jnp.stack(v) for n, v in Gs.items()}
    drel = _bias_bucket_sum(dbias_total, bucket, name="rel_bias_grad")
    grads_small['rel_bias'] = drel[:, :REL_BUCKETS].T
    return sq, dy.reshape(B, S, D), grads_small


def kernel(x, p, rel_bias, ev_w_in, ev_q_norm, ev_w_uq, ev_kv_norm, ev_w_ukv, ev_sinks, ev_w_out, od_w_in, od_b_f, od_w_out, ln1_g, ln1_b, w_up, w_down, ln2_g, ln2_b, ple_w_proj, ple_w_gate, ple_b_gate, loss_target, m_rel_bias, m_ev_w_in, m_ev_q_norm, m_ev_w_uq, m_ev_kv_norm, m_ev_w_ukv, m_ev_sinks, m_ev_w_out, m_od_w_in, m_od_b_f, m_od_w_out, m_ln1_g, m_ln1_b, m_w_up, m_w_down, m_ln2_g, m_ln2_b, m_ple_w_proj, m_ple_w_gate, m_ple_b_gate, v_rel_bias, v_ev_w_in, v_ev_q_norm, v_ev_w_uq, v_ev_kv_norm, v_ev_w_ukv, v_ev_sinks, v_ev_w_out, v_od_w_in, v_od_b_f, v_od_w_out, v_ln1_g, v_ln1_b, v_w_up, v_w_down, v_ln2_g, v_ln2_b, v_ple_w_proj, v_ple_w_gate, v_ple_b_gate):
    given = dict(locals())
    w = {n: given[n] for n in WEIGHTS}
    mom = {n: given["m_" + n] for n in WEIGHTS}
    var = {n: given["v_" + n] for n in WEIGHTS}
    small_shapes = {n: w[n].shape for n in SMALL}

    xchg = _MeshExchange({n: w[n].astype(BF16) for n in BIG})
    P = {n: w[n] for n in SMALL}

    sq, grad_x, grads_small = _local_step(x, p, loss_target, P, xchg)
    loss = lax.psum(0.5 * jnp.sum(sq) / D_MODEL, ("x", "y", "c"))

    received = xchg.finish()
    g_small_packed = _all_reduce_small(_pack_small(grads_small), name="reduce_small_grads")
    g_small = _unpack_small(g_small_packed, small_shapes)

    grad, delta, new_m, new_v = {}, {}, {}, {}
    for n in BIG:
        parts = [received[(n, idx)] for idx in range(w[n].shape[0])]
        grad[n], delta[n], new_m[n], new_v[n] = _adamw_slots(w[n], parts, mom[n], var[n], name=f"adamw_{n}")
    d, nm, nv = _adamw(_pack_small(w), g_small_packed, _pack_small(mom), _pack_small(var), name="adamw_small")
    d, nm, nv = (_unpack_small(t, small_shapes) for t in (d, nm, nv))
    for n in SMALL:
        grad[n], delta[n], new_m[n], new_v[n] = g_small[n], d[n], nm[n], nv[n]

    return (loss, grad_x, *[grad[n] for n in WEIGHTS], *[delta[n] for n in WEIGHTS],
            *[new_m[n] for n in WEIGHTS], *[new_v[n] for n in WEIGHTS])
```

```python
import math

import jax
import jax.numpy as jnp
from jax import lax
from jax.experimental import pallas as pl
from jax.experimental.pallas import tpu as pltpu

F32, BF16 = jnp.float32, jnp.bfloat16

D_MODEL = 1024
DEPTH = 4
HEAD_DIM = 64
MLA_HEADS, MLA_NOPE, MLA_ROPE, MLA_V = 8, 64, 32, 64
MLA_Q_LORA, MLA_KV_LORA = 384, 256
MLA_QK = MLA_NOPE + MLA_ROPE
ROPE_THETA = 10000.0
SWA_HEADS, SWA_KV_HEADS, SWA_WINDOW = 8, 2, 128
SWA_GROUP = SWA_HEADS // SWA_KV_HEADS
REL_BUCKETS, REL_MAX_DIST = 32, 128
FOX_HEADS = 16
D_FF = 4 * D_MODEL
D_PLE = 256
BLOCK_Q = 128
DN_ALPHA = (2 * DEPTH) ** 0.25
NORM_EPS = 1e-5
NEG_INF = -1e30
EVEN_IN = 1440
ODD_QKV = 3 * FOX_HEADS * HEAD_DIM
LANES = 128

EV_QS = (0, 512)
EV_CQ = (512, 896)
EV_CKV = (896, 1152)
EV_KS = (1152, 1280)
EV_VS = (1280, 1408)
EV_KR = (1408, 1536)
EVEN_IN_PAD = 1536
KR_LANE0 = MLA_NOPE

ADAM_LR, ADAM_B1, ADAM_B2, ADAM_EPS, ADAM_WD, ADAM_STEP = 0.001, 0.9, 0.999, 1e-08, 0.01, 10

N_DEV = 8
VMEM_LIMIT_BYTES = 48 * 1024 * 1024
ATT_TILE = 512
ATT_TILE_BWD = 512
PAIRS_PER_STEP_FWD = 4
PAIRS_PER_STEP_BWD = 2

NN = (((1,), (0,)), ((), ()))
NT = (((1,), (1,)), ((), ()))
TN = (((0,), (0,)), ((), ()))

BIG = ['ev_w_in', 'ev_w_uq', 'ev_w_ukv', 'ev_w_out', 'od_w_in', 'od_w_out', 'w_up', 'w_down',
       'ple_w_proj', 'ple_w_gate']
BIG_AXIS = {'ev_w_in': 2, 'ev_w_uq': 2, 'ev_w_ukv': 2, 'ev_w_out': 1, 'od_w_in': 2, 'od_w_out': 1,
            'w_up': 2, 'w_down': 1, 'ple_w_proj': 2, 'ple_w_gate': 1}
SMALL = ['rel_bias', 'ev_q_norm', 'ev_kv_norm', 'ev_sinks', 'od_b_f', 'ln1_g', 'ln1_b', 'ln2_g', 'ln2_b',
         'ple_b_gate']
WEIGHTS = ['rel_bias', 'ev_w_in', 'ev_q_norm', 'ev_w_uq', 'ev_kv_norm', 'ev_w_ukv', 'ev_sinks', 'ev_w_out',
           'od_w_in', 'od_b_f', 'od_w_out', 'ln1_g', 'ln1_b', 'w_up', 'w_down', 'ln2_g', 'ln2_b',
           'ple_w_proj', 'ple_w_gate', 'ple_b_gate']


def _cparams(*sem):
    return pltpu.CompilerParams(dimension_semantics=sem, vmem_limit_bytes=VMEM_LIMIT_BYTES)


def _pick(n, cands):
    for c in cands:
        if n % c == 0:
            return c
    return n


MM_STEP_BYTES = 10 * 1024 * 1024
MM_OUT_BYTES = 8 * 1024 * 1024
MM_CHUNK = 512


def _mm(a, b, *, trans_b=False, extras=(), epilogue=None, row_epilogue=None, out_dtypes=(F32,), out_widths=None,
        n_sums=0, comm=(), name):
    M, K = a.shape
    N = b.shape[0] if trans_b else b.shape[1]
    n_ex, n_out = len(extras), len(out_dtypes)
    n_rows_out = n_out - n_sums
    out_widths = (N,) * n_out if out_widths is None else out_widths
    row_bytes = K * a.dtype.itemsize + (sum(w * jnp.dtype(d).itemsize
                                            for w, d in zip(out_widths[:n_rows_out], out_dtypes))
                                        + sum(e.shape[1] * e.dtype.itemsize for e in extras if e.shape[0] == M)
                                        + (4 * N if row_epilogue is not None else 0))
    tm = next((c for c in (1024, 512, 256) if M % c == 0 and c * row_bytes <= MM_STEP_BYTES), 128)
    nc = _pick(N, (MM_CHUNK, 384, 256, 128))
    n_c, kinds = len(comm), [k for k, _ in comm]
    n_scr = 1 if row_epilogue is not None else 0

    def body(*refs):
        c_in = refs[2 + n_ex:2 + n_ex + n_c]
        c_out = refs[2 + n_ex + n_c + n_out:2 + n_ex + 2 * n_c + n_out]
        sems = refs[2 + n_ex + 2 * n_c + n_out + n_scr:]
        refs = refs[:2 + n_ex] + refs[2 + n_ex + n_c:2 + n_ex + n_c + n_out] \
            + refs[2 + n_ex + 2 * n_c + n_out:2 + n_ex + 2 * n_c + n_out + n_scr]
        if n_c:
            place = _mesh_place()
            step = pl.program_id(0)

            @pl.when(step == 0)
            def _():
                _comm_start(kinds, c_in, c_out, sems, place)

        a_ref, b_ref = refs[:2]
        ex = refs[2:2 + n_ex]
        outs = refs[2 + n_ex:2 + n_ex + n_out]
        av = a_ref[...].astype(BF16)
        for n0 in range(0, N, nc):
            cols = slice(n0, n0 + nc)
            bv = (b_ref[cols, :] if trans_b else b_ref[:, cols]).astype(BF16)
            acc = lax.dot_general(av, bv, NT if trans_b else NN, preferred_element_type=F32)
            if row_epilogue is not None:
                refs[-1][:, cols] = acc
                continue
            res = epilogue(acc, *[e[:, cols] for e in ex]) if epilogue is not None else (acc,)
            for o, r in zip(outs, res):
                o[:, cols] = r.astype(o.dtype)
        if row_epilogue is not None:
            res = row_epilogue(refs[-1][...], *[e[...] for e in ex])
            for o, r in zip(outs[:n_rows_out], res):
                o[...] = r.astype(o.dtype)
            if n_sums:
                @pl.when(pl.program_id(0) == 0)
                def _():
                    for o in outs[n_rows_out:]:
                        o[...] = jnp.zeros_like(o)

                for o, r in zip(outs[n_rows_out:], res[n_rows_out:]):
                    o[...] += r
        if n_c:
            @pl.when(step == M // tm - 1)
            def _():
                _comm_wait(kinds, c_in, c_out, sems, place)

    in_specs = [pl.BlockSpec((tm, K), lambda i: (i, 0)), pl.BlockSpec(b.shape, lambda i: (0, 0))]
    for e in extras:
        if e.shape[0] == M:
            in_specs.append(pl.BlockSpec((tm, e.shape[1]), lambda i: (i, 0)))
        elif e.shape == (1, N):
            in_specs.append(pl.BlockSpec((1, N), lambda i: (0, 0)))
        else:
            raise ValueError(f"extra operand of shape {e.shape} for a ({M}, {N}) result")
    res = pl.pallas_call(
        body, name=name, grid=(M // tm,), in_specs=in_specs + [HBM_SPEC] * n_c,
        out_specs=[pl.BlockSpec((tm, w), lambda i: (i, 0)) for w in out_widths[:n_rows_out]]
        + [pl.BlockSpec((1, w), lambda i: (0, 0)) for w in out_widths[n_rows_out:]] + [HBM_SPEC] * n_c,
        out_shape=[jax.ShapeDtypeStruct((M, w), d) for w, d in zip(out_widths[:n_rows_out], out_dtypes)]
        + [jax.ShapeDtypeStruct((1, w), d) for w, d in zip(out_widths[n_rows_out:], out_dtypes[n_rows_out:])]
        + _comm_out_shapes(comm),
        scratch_shapes=([pltpu.VMEM((tm, N), F32)] if row_epilogue is not None else [])
        + (_comm_scratch(comm) if n_c else []),
        compiler_params=_cparams("arbitrary" if n_sums or n_c else "parallel"),
    )(a, b, *extras, *[c for _, c in comm])
    main = res[0] if n_out == 1 else tuple(res[:n_out])
    return (main, list(res[n_out:])) if n_c else main


def _mm_tn(a, b, *, name):
    T, K = a.shape
    N = b.shape[1]
    bk, bn = K, N
    while bk * bn * 4 > MM_OUT_BYTES:
        if bn >= bk and bn % (2 * LANES) == 0:
            bn //= 2
        else:
            bk //= 2
    tt = _pick(T, (1024, 512, 256))
    ck, cn = _pick(bk, (MM_CHUNK, 384, 256, 128)), _pick(bn, (MM_CHUNK, 384, 256, 128))

    def body(a_ref, b_ref, o_ref, acc_ref):
        t = pl.program_id(2)

        @pl.when(t == 0)
        def _():
            acc_ref[...] = jnp.zeros_like(acc_ref)

        for r0 in range(0, bk, ck):
            av = a_ref[:, r0:r0 + ck].astype(BF16)
            for c0 in range(0, bn, cn):
                acc_ref[r0:r0 + ck, c0:c0 + cn] += lax.dot_general(
                    av, b_ref[:, c0:c0 + cn].astype(BF16), TN, preferred_element_type=F32)

        @pl.when(t == T // tt - 1)
        def _():
            o_ref[...] = acc_ref[...].astype(o_ref.dtype)

    return pl.pallas_call(
        body, name=name, grid=(K // bk, N // bn, T // tt),
        in_specs=[pl.BlockSpec((tt, bk), lambda i, j, t: (t, i)), pl.BlockSpec((tt, bn), lambda i, j, t: (t, j))],
        out_specs=pl.BlockSpec((bk, bn), lambda i, j, t: (i, j)),
        out_shape=jax.ShapeDtypeStruct((K, N), BF16), scratch_shapes=[pltpu.VMEM((bk, bn), F32)],
        compiler_params=_cparams("parallel", "parallel", "arbitrary"),
    )(a, b)


ROW_TILE = 256


def _row_spec(cols, col_block=0):
    return pl.BlockSpec((ROW_TILE, cols), lambda i: (i, col_block))


def _tab_spec(cols, period):
    return pl.BlockSpec((ROW_TILE, cols), lambda i: (i % period, 0))


def _full_spec(shape):
    return pl.BlockSpec(shape, lambda i: (0,) * len(shape))


def _mm_ln(a, w, x, g, b, *, comm=(), name):
    def ln_rows(m, xv, gv, bv):
        z = DN_ALPHA * xv + m
        mu = jnp.mean(z, -1, keepdims=True)
        zc = z - mu
        r = lax.rsqrt(jnp.mean(zc * zc, -1, keepdims=True) + NORM_EPS)
        xh = zc * r
        y = xh * gv + bv
        return y, y, xh, jnp.broadcast_to(r, (r.shape[0], LANES))

    D = w.shape[1]
    return _mm(a, w, extras=(x, g, b), row_epilogue=ln_rows, out_dtypes=(F32, BF16, F32, F32),
               out_widths=(D, D, D, LANES), comm=comm, name=name)


def _mm_ln_bwd(a, w, resid, resid_scale, xh, r, g, *, name):
    def ln_bwd_rows(acc, rv, xhv, rstd, gv):
        dyv = acc + resid_scale * rv
        dyg = dyv * gv
        c1 = jnp.mean(dyg, -1, keepdims=True)
        c2 = jnp.mean(dyg * xhv, -1, keepdims=True)
        dz = _widen(rstd, dyv.shape[-1]) * (dyg - c1 - xhv * c2)
        return dz, dz, jnp.sum(dyv * xhv, 0, keepdims=True), jnp.sum(dyv, 0, keepdims=True)

    D = w.shape[0]
    return _mm(a, w, trans_b=True, extras=(resid, xh, r, g), row_epilogue=ln_bwd_rows,
               out_dtypes=(F32, BF16, F32, F32), out_widths=(D, D, D, D), n_sums=2, name=name)


def _loss_grad(y, target, *, name):
    T, D = y.shape

    def body(y_ref, t_ref, dy_ref, sq_ref):
        err = y_ref[...] - t_ref[...]
        dy_ref[...] = err / D

        @pl.when(pl.program_id(0) == 0)
        def _():
            sq_ref[...] = jnp.zeros_like(sq_ref)

        sq_ref[...] += jnp.sum(err * err, 0, keepdims=True)

    return pl.pallas_call(
        body, name=name, grid=(T // ROW_TILE,),
        in_specs=[_row_spec(D), _row_spec(D)],
        out_specs=[_row_spec(D), _full_spec((1, D))],
        out_shape=[jax.ShapeDtypeStruct((T, D), F32), jax.ShapeDtypeStruct((1, D), F32)],
        compiler_params=_cparams("arbitrary"),
    )(y, target)


def _ple_bwd_elem(dx3, g, e, *, name):
    T, D = dx3.shape

    def body(dx_ref, g_ref, e_ref, de_ref, dz_ref, db_ref):
        dx, gv = dx_ref[...], g_ref[...]
        de_ref[...] = (dx * gv).astype(BF16)
        dz = dx * e_ref[...] * gv * (1.0 - gv)
        dz_ref[...] = dz.astype(BF16)

        @pl.when(pl.program_id(0) == 0)
        def _():
            db_ref[...] = jnp.zeros_like(db_ref)

        db_ref[...] += jnp.sum(dz, 0, keepdims=True)

    return pl.pallas_call(
        body, name=name, grid=(T // ROW_TILE,),
        in_specs=[_row_spec(D), _row_spec(D), _row_spec(D)],
        out_specs=[_row_spec(D), _row_spec(D), _full_spec((1, D))],
        out_shape=[jax.ShapeDtypeStruct((T, D), BF16), jax.ShapeDtypeStruct((T, D), BF16),
                   jax.ShapeDtypeStruct((1, D), F32)],
        compiler_params=_cparams("arbitrary"),
    )(dx3, g, e)


def _rotate(xv, a, bm, bp, sign):
    half = MLA_ROPE // 2
    width = xv.shape[-1]
    a, bm, bp = (_widen(t, width) for t in (a, bm, bp))
    return xv * a + sign * (pltpu.roll(xv, width - half, 1) * bm + pltpu.roll(xv, half, 1) * bp)


def _rope(x, tabs, seq, *, sign, name):
    T, width = x.shape

    def body(x_ref, a_ref, bm_ref, bp_ref, o_ref):
        o_ref[...] = _rotate(x_ref[...], a_ref[...], bm_ref[...], bp_ref[...], sign).astype(BF16)

    return pl.pallas_call(
        body, name=name, grid=(T // ROW_TILE,),
        in_specs=[_row_spec(width)] + [_tab_spec(LANES, seq // ROW_TILE)] * 3,
        out_specs=_row_spec(width),
        out_shape=jax.ShapeDtypeStruct((T, width), BF16),
        compiler_params=_cparams("parallel"),
    )(x, *tabs)


def _mla_keys(knp, h, k_tabs, seq, *, name):
    T = knp.shape[0]

    def body(k_ref, h_ref, a_ref, bm_ref, bp_ref, o_ref):
        kr = _rotate(h_ref[...], a_ref[...], bm_ref[...], bp_ref[...], 1.0)
        for hd in range(MLA_HEADS):
            cols = slice(hd * LANES, (hd + 1) * LANES)
            o_ref[:, cols] = (k_ref[:, cols].astype(F32) + kr).astype(BF16)

    return pl.pallas_call(
        body, name=name, grid=(T // ROW_TILE,),
        in_specs=[_row_spec(MLA_HEADS * LANES), _row_spec(LANES, EV_KR[0] // LANES)]
        + [_tab_spec(LANES, seq // ROW_TILE)] * 3,
        out_specs=_row_spec(MLA_HEADS * LANES),
        out_shape=jax.ShapeDtypeStruct((T, MLA_HEADS * LANES), BF16),
        compiler_params=_cparams("parallel"),
    )(knp, h, *k_tabs)


def _mla_rope_key_grad(dk, k_tabs, seq, *, name):
    T = dk.shape[0]

    def body(dk_ref, a_ref, bm_ref, bp_ref, o_ref):
        tot = dk_ref[:, 0:LANES]
        for hd in range(1, MLA_HEADS):
            tot = tot + dk_ref[:, hd * LANES:(hd + 1) * LANES]
        o_ref[...] = _rotate(tot, a_ref[...], bm_ref[...], bp_ref[...], -1.0).astype(BF16)

    return pl.pallas_call(
        body, name=name, grid=(T // ROW_TILE,),
        in_specs=[_row_spec(MLA_HEADS * LANES)] + [_tab_spec(LANES, seq // ROW_TILE)] * 3,
        out_specs=_row_spec(LANES),
        out_shape=jax.ShapeDtypeStruct((T, LANES), BF16),
        compiler_params=_cparams("parallel"),
    )(dk, *k_tabs)


def _even_norms(h, gq, gkv, *, name):
    T = h.shape[0]

    def body(h_ref, gq_ref, gkv_ref, cq_ref, ckv_ref, rq_ref, rkv_ref):
        cq = h_ref[:, EV_CQ[0]:EV_CQ[1]]
        rq = lax.rsqrt(jnp.mean(cq * cq, -1, keepdims=True) + NORM_EPS)
        cq_ref[...] = (cq * rq * gq_ref[...]).astype(BF16)
        rq_ref[...] = jnp.broadcast_to(rq, rq_ref.shape)
        ckv = h_ref[:, EV_CKV[0]:EV_CKV[1]]
        rkv = lax.rsqrt(jnp.mean(ckv * ckv, -1, keepdims=True) + NORM_EPS)
        ckv_ref[...] = (ckv * rkv * gkv_ref[...]).astype(BF16)
        rkv_ref[...] = jnp.broadcast_to(rkv, rkv_ref.shape)

    return pl.pallas_call(
        body, name=name, grid=(T // ROW_TILE,),
        in_specs=[_row_spec(EVEN_IN_PAD), _full_spec((1, MLA_Q_LORA)), _full_spec((1, MLA_KV_LORA))],
        out_specs=[_row_spec(MLA_Q_LORA), _row_spec(MLA_KV_LORA), _row_spec(LANES), _row_spec(LANES)],
        out_shape=[jax.ShapeDtypeStruct((T, MLA_Q_LORA), BF16), jax.ShapeDtypeStruct((T, MLA_KV_LORA), BF16),
                   jax.ShapeDtypeStruct((T, LANES), F32), jax.ShapeDtypeStruct((T, LANES), F32)],
        compiler_params=_cparams("parallel"),
    )(h, gq, gkv)


def _even_in_bwd(h, rq, rkv, gq, gkv, dcqn, dckvn, dqs, dks, dvs, dkr, *, name):
    T = h.shape[0]

    def rms_bwd(c, r, g, dy):
        r = _widen(r, c.shape[-1])
        xr = c * r
        dyg = dy * g
        return r * (dyg - xr * jnp.mean(dyg * xr, -1, keepdims=True)), jnp.sum(dy * xr, 0, keepdims=True)

    def body(h_ref, rq_ref, rkv_ref, gq_ref, gkv_ref, dcq_ref, dckv_ref, dqs_ref, dks_ref, dvs_ref, dkr_ref,
             dh_ref, dgq_ref, dgkv_ref):
        @pl.when(pl.program_id(0) == 0)
        def _():
            dgq_ref[...] = jnp.zeros_like(dgq_ref)
            dgkv_ref[...] = jnp.zeros_like(dgkv_ref)

        dcq, dgq = rms_bwd(h_ref[:, EV_CQ[0]:EV_CQ[1]], rq_ref[...], gq_ref[...], dcq_ref[...])
        dckv, dgkv = rms_bwd(h_ref[:, EV_CKV[0]:EV_CKV[1]], rkv_ref[...], gkv_ref[...], dckv_ref[...])
        dgq_ref[...] += dgq
        dgkv_ref[...] += dgkv
        dh_ref[:, EV_QS[0]:EV_QS[1]] = dqs_ref[...]
        dh_ref[:, EV_CQ[0]:EV_CQ[1]] = dcq.astype(BF16)
        dh_ref[:, EV_CKV[0]:EV_CKV[1]] = dckv.astype(BF16)
        dh_ref[:, EV_KS[0]:EV_KS[1]] = dks_ref[...]
        dh_ref[:, EV_VS[0]:EV_VS[1]] = dvs_ref[...]
        dh_ref[:, EV_KR[0]:EV_KR[1]] = dkr_ref[...]

    return pl.pallas_call(
        body, name=name, grid=(T // ROW_TILE,),
        in_specs=[_row_spec(EVEN_IN_PAD), _row_spec(LANES), _row_spec(LANES), _full_spec((1, MLA_Q_LORA)),
                  _full_spec((1, MLA_KV_LORA)), _row_spec(MLA_Q_LORA), _row_spec(MLA_KV_LORA),
                  _row_spec(SWA_HEADS * HEAD_DIM), _row_spec(LANES), _row_spec(LANES), _row_spec(LANES)],
        out_specs=[_row_spec(EVEN_IN_PAD), _full_spec((1, MLA_Q_LORA)), _full_spec((1, MLA_KV_LORA))],
        out_shape=[jax.ShapeDtypeStruct((T, EVEN_IN_PAD), BF16), jax.ShapeDtypeStruct((1, MLA_Q_LORA), F32),
                   jax.ShapeDtypeStruct((1, MLA_KV_LORA), F32)],
        compiler_params=_cparams("arbitrary"),
    )(h, rq, rkv, gq, gkv, dcqn, dckvn, dqs, dks, dvs, dkr)


def _fox_decay_fwd(f3, bf, *, name):
    B, S, _ = f3.shape

    def body(f_ref, b_ref, csh_ref, chs_ref):
        x = f_ref[...] + b_ref[...]
        c = jnp.minimum(x, 0.0) - jnp.log1p(jnp.exp(-jnp.abs(x)))
        row = lax.broadcasted_iota(jnp.int32, (S, LANES), 0)
        k = 1
        while k < S:
            c = c + jnp.where(row >= k, pltpu.roll(c, k, 0), 0.0)
            k *= 2
        csh_ref[...] = c
        chs_ref[...] = c.T

    return pl.pallas_call(
        body, name=name, grid=(B,),
        in_specs=[pl.BlockSpec((None, S, LANES), lambda b: (b, 0, 0)), pl.BlockSpec((1, LANES), lambda b: (0, 0))],
        out_specs=[pl.BlockSpec((None, S, LANES), lambda b: (b, 0, 0)),
                   pl.BlockSpec((None, LANES, S), lambda b: (b, 0, 0))],
        out_shape=[jax.ShapeDtypeStruct((B, S, LANES), F32), jax.ShapeDtypeStruct((B, LANES, S), F32)],
        compiler_params=_cparams("parallel"),
    )(f3, bf)


def _fox_decay_bwd(dc_hs, f3, bf, *, name):
    B, S, _ = f3.shape

    def body(dc_ref, f_ref, b_ref, df_ref, db_ref):
        g = dc_ref[...].T
        row = lax.broadcasted_iota(jnp.int32, (S, LANES), 0)
        k = 1
        while k < S:
            g = g + jnp.where(row < S - k, pltpu.roll(g, S - k, 0), 0.0)
            k *= 2
        x = f_ref[...] + b_ref[...]
        df = g * (1.0 / (1.0 + jnp.exp(x)))
        df_ref[...] = df.astype(BF16)

        @pl.when(pl.program_id(0) == 0)
        def _():
            db_ref[...] = jnp.zeros_like(db_ref)

        db_ref[...] += jnp.sum(df, 0, keepdims=True)

    return pl.pallas_call(
        body, name=name, grid=(B,),
        in_specs=[pl.BlockSpec((None, LANES, S), lambda b: (b, 0, 0)),
                  pl.BlockSpec((None, S, LANES), lambda b: (b, 0, 0)), pl.BlockSpec((1, LANES), lambda b: (0, 0))],
        out_specs=[pl.BlockSpec((None, S, LANES), lambda b: (b, 0, 0)), pl.BlockSpec((1, LANES), lambda b: (0, 0))],
        out_shape=[jax.ShapeDtypeStruct((B, S, LANES), BF16), jax.ShapeDtypeStruct((1, LANES), F32)],
        compiler_params=_cparams("arbitrary"),
    )(dc_hs, f3, bf)


def _head_column(block, h):
    lane = lax.broadcasted_iota(jnp.int32, block.shape, 1)
    return jnp.sum(jnp.where(lane == h, block, 0.0), axis=-1, keepdims=True)


def _causal_mask(s):
    r = lax.broadcasted_iota(jnp.int32, s.shape, 0)
    c = lax.broadcasted_iota(jnp.int32, s.shape, 1)
    return jnp.where(c <= r, s, NEG_INF)


def _low_half(shape):
    return (lax.broadcasted_iota(jnp.int32, shape, 1) % LANES) < HEAD_DIM


def _widen(x, cols):
    return jnp.concatenate([x] * (cols // LANES), axis=1)


def _both_halves(x, lo):
    r = pltpu.roll(x, HEAD_DIM, 1)
    return jnp.where(lo, x, r), jnp.where(lo, r, x)


MESH_ID = pl.DeviceIdType.MESH
HBM_SPEC = pl.BlockSpec(memory_space=pltpu.HBM)
VMEM_SPEC = pl.BlockSpec(memory_space=pltpu.VMEM)


def _mesh_place():
    x, y, c = lax.axis_index("x"), lax.axis_index("y"), lax.axis_index("c")
    return x, y, c, 4 * x + 2 * y + c


def _peers(x, y, c):
    out = []
    for mask in range(1, N_DEV):
        dx, dy, dc = (mask >> 2) & 1, (mask >> 1) & 1, mask & 1
        px, py, pc = (1 - x if dx else x), (1 - y if dy else y), (1 - c if dc else c)
        out.append(((px, py, pc), 4 * px + 2 * py + pc))
    return out


def _comm_out_shapes(comm):
    return [jax.ShapeDtypeStruct(a.shape if kind == "scatter" else (N_DEV,) + a.shape[1:], a.dtype) for kind, a in comm]


def _comm_scratch(comm):
    n = len(comm)
    return [pltpu.SemaphoreType.DMA((n, 7)), pltpu.SemaphoreType.DMA((n, 7)), pltpu.SemaphoreType.DMA((n,))]


def _comm_copies(kinds, in_refs, out_refs, sems, place):
    send_sems, recv_sems, local_sems = sems
    x, y, c, me = place
    local, remote = [], []
    for w, kind in enumerate(kinds):
        mine = in_refs[w].at[me] if kind == "scatter" else in_refs[w].at[kind[1]]
        local.append(pltpu.make_async_copy(mine, out_refs[w].at[me], local_sems.at[w]))
        for k, (peer, peer_idx) in enumerate(_peers(x, y, c)):
            remote.append(pltpu.make_async_remote_copy(
                src_ref=in_refs[w].at[peer_idx] if kind == "scatter" else mine, dst_ref=out_refs[w].at[me],
                send_sem=send_sems.at[w, k], recv_sem=recv_sems.at[w, k], device_id=peer, device_id_type=MESH_ID))
    return local, remote


def _comm_start(kinds, in_refs, out_refs, sems, place):
    local, remote = _comm_copies(kinds, in_refs, out_refs, sems, place)
    for cp in local + remote:
        cp.start()


def _comm_wait(kinds, in_refs, out_refs, sems, place):
    local, remote = _comm_copies(kinds, in_refs, out_refs, sems, place)
    for cp in remote:
        cp.wait_recv()
    for cp in remote:
        cp.wait_send()
    for cp in local:
        cp.wait()


def _exchange(comm, *, name):
    n = len(comm)
    kinds = [k for k, _ in comm]

    def body(*refs):
        place = _mesh_place()
        _comm_start(kinds, refs[:n], refs[n:2 * n], refs[2 * n:], place)
        _comm_wait(kinds, refs[:n], refs[n:2 * n], refs[2 * n:], place)

    return pl.pallas_call(
        body, name=name, out_shape=_comm_out_shapes(comm), in_specs=[HBM_SPEC] * n, out_specs=[HBM_SPEC] * n,
        scratch_shapes=_comm_scratch(comm),
    )(*[a for _, a in comm])


def _flash_fwd(qa, ka, va, *, q_blk0, k_blk0, v_blk0, W, n_pairs, B, S, scale, csh=None, crow=None, comm=(), name):
    t = ATT_TILE
    nq = S // t
    P = PAIRS_PER_STEP_FWD
    decay = csh is not None
    split = W == LANES
    assert n_pairs % P == 0 and q_blk0 % P == 0 and k_blk0 % P == 0 and v_blk0 % P == 0
    n_c, kinds = len(comm), [k for k, _ in comm]
    n_in = 5 if decay else 3
    fold_scale = math.log2(scale).is_integer()
    n_steps = (B, n_pairs // P, nq)

    def body(*refs):
        c_in, c_out = refs[n_in:n_in + n_c], refs[n_in + n_c + 2:n_in + 2 * n_c + 2]
        sems = refs[n_in + 2 * n_c + 4:]
        refs = refs[:n_in] + refs[n_in + n_c:n_in + n_c + 2] + refs[n_in + 2 * n_c + 2:n_in + 2 * n_c + 4]
        if decay:
            q_ref, k_ref, v_ref, csh_ref, crow_ref, o_ref, lse_ref, m_s, acc_s = refs
        else:
            q_ref, k_ref, v_ref, o_ref, lse_ref, m_s, acc_s = refs
        g, i = pl.program_id(1), pl.program_id(2)
        if n_c:
            place = _mesh_place()
            ids = [pl.program_id(ax) for ax in range(3)]

            @pl.when((ids[0] == 0) & (ids[1] == 0) & (ids[2] == 0))
            def _():
                _comm_start(kinds, c_in, c_out, sems, place)

        lo = _low_half((t, LANES))
        qv = q_ref[...]
        qh = []
        for pr in range(P):
            qp = qv[:, pr * W:(pr + 1) * W]
            qh += [jnp.where(lo, qp, jnp.zeros_like(qp)), jnp.where(lo, jnp.zeros_like(qp), qp)] if split \
                else [qp[:, :LANES], qp[:, LANES:]]
        if fold_scale:
            qh = [x * scale for x in qh]
        if decay:
            cq = [jnp.broadcast_to(_head_column(csh_ref[...], 2 * P * g + hd), (t, LANES)) for hd in range(2 * P)]
        m_s[...] = jnp.full(m_s.shape, NEG_INF, F32)
        acc_s[...] = jnp.zeros(acc_s.shape, F32)

        def step(j, masked):
            rows = pl.ds(pl.multiple_of(j * t, t), t)
            kb, vb = k_ref[rows, :], v_ref[rows, :]
            for pr in range(P):
                kp, vp = kb[:, pr * W:(pr + 1) * W], vb[:, pr * LANES:(pr + 1) * LANES]
                ones = jnp.ones_like(vp)
                vaug = [jnp.where(lo, vp, ones), jnp.where(lo, ones, vp)]
                for half in range(2):
                    hd = 2 * pr + half
                    kh = kp if split else kp[:, half * LANES:(half + 1) * LANES]
                    s = lax.dot_general(qh[hd], kh, NT, preferred_element_type=F32)
                    if not fold_scale:
                        s = s * scale
                    if decay:
                        s = s + _widen(cq[hd], t) - crow_ref[hd, j]
                    if masked:
                        s = _causal_mask(s)
                    m_prev = m_s[hd]
                    m_new = jnp.maximum(m_prev, jnp.max(s, -1, keepdims=True))
                    p = jnp.exp(s - _widen(m_new, t))
                    acc_s[hd] = jnp.exp(m_prev - m_new) * acc_s[hd] + lax.dot_general(
                        p.astype(BF16), vaug[half], NN, preferred_element_type=F32)
                    m_s[hd] = m_new

        def loop_body(j, carry):
            step(j, False)
            return carry

        lax.fori_loop(0, i, loop_body, 0)
        step(i, True)
        for pr in range(P):
            acc0, acc1 = acc_s[2 * pr], acc_s[2 * pr + 1]
            _, l0 = _both_halves(acc0, lo)
            l1, _ = _both_halves(acc1, lo)
            cols = slice(pr * LANES, (pr + 1) * LANES)
            o_ref[:, cols] = jnp.where(lo, acc0 / l0, acc1 / l1).astype(BF16)
            lse_ref[:, cols] = jnp.where(lo, m_s[2 * pr] + jnp.log(l0), m_s[2 * pr + 1] + jnp.log(l1))
        if n_c:
            @pl.when((ids[0] == n_steps[0] - 1) & (ids[1] == n_steps[1] - 1) & (ids[2] == n_steps[2] - 1))
            def _():
                _comm_wait(kinds, c_in, c_out, sems, place)

    in_specs = [pl.BlockSpec((t, P * W), lambda b, g, i: (b * nq + i, q_blk0 // P + g)),
                pl.BlockSpec((S, P * W), lambda b, g, i: (b, k_blk0 // P + g)),
                pl.BlockSpec((S, P * LANES), lambda b, g, i: (b, v_blk0 // P + g))]
    args = [qa, ka, va]
    if decay:
        in_specs += [pl.BlockSpec((None, t, LANES), lambda b, g, i: (b, i, 0)),
                     pl.BlockSpec((None, 2 * P, nq, 1, t), lambda b, g, i: (b, g, 0, 0, 0))]
        args += [csh, crow]
    out_spec = pl.BlockSpec((t, P * LANES), lambda b, g, i: (b * nq + i, g))
    res = pl.pallas_call(
        body, name=name, grid=n_steps, in_specs=in_specs + [HBM_SPEC] * n_c,
        out_specs=[out_spec, out_spec] + [HBM_SPEC] * n_c,
        out_shape=[jax.ShapeDtypeStruct((B * S, n_pairs * LANES), BF16),
                   jax.ShapeDtypeStruct((B * S, n_pairs * LANES), F32)] + _comm_out_shapes(comm),
        scratch_shapes=[pltpu.VMEM((2 * P, t, LANES), F32), pltpu.VMEM((2 * P, t, LANES), F32)]
        + (_comm_scratch(comm) if n_c else []),
        compiler_params=_cparams(*(("arbitrary",) * 3 if n_c else ("parallel",) * 3)),
    )(*args, *[a for _, a in comm])
    return res[0], res[1], list(res[2:])


def _flash_bwd(qa, ka, va, oa, doa, lsea, *, q_blk0, k_blk0, v_blk0, do_blk0, W, n_pairs, B, S, scale, qk_dtype,
               csh=None, crow=None, comm=(), name):
    t = ATT_TILE_BWD
    nq = S // t
    P = PAIRS_PER_STEP_BWD
    decay = csh is not None
    if decay:
        crow = crow.reshape(B, 2 * n_pairs, nq, 1, t)
    split = W == LANES
    assert n_pairs % P == 0 and q_blk0 % P == 0 and k_blk0 % P == 0 and v_blk0 % P == 0 and do_blk0 % P == 0
    n_c, kinds = len(comm), [k for k, _ in comm]
    n_in, n_out, n_scr = (8, 5, 8) if decay else (6, 3, 5)
    n_steps = (B, n_pairs // P, nq)

    def body(*refs):
        c_in = refs[n_in:n_in + n_c]
        c_out = refs[n_in + n_c + n_out:n_in + 2 * n_c + n_out]
        sems = refs[n_in + 2 * n_c + n_out + n_scr:]
        refs = (refs[:n_in] + refs[n_in + n_c:n_in + n_c + n_out]
                + refs[n_in + 2 * n_c + n_out:n_in + 2 * n_c + n_out + n_scr])
        if n_c:
            place = _mesh_place()
            ids = [pl.program_id(ax) for ax in range(3)]

            @pl.when((ids[0] == 0) & (ids[1] == 0) & (ids[2] == 0))
            def _():
                _comm_start(kinds, c_in, c_out, sems, place)

        if decay:
            (q_ref, k_ref, v_ref, o_ref, do_ref, lse_ref, csh_ref, crow_ref, dq_ref, dk_ref, dv_ref, dck_ref, dcq_ref,
             dq_s, lse_s, delta_s, dk_s, dv_s, cq_s, dcq_s, dck_s) = refs
        else:
            (q_ref, k_ref, v_ref, o_ref, do_ref, lse_ref, dq_ref, dk_ref, dv_ref,
             dq_s, lse_s, delta_s, dk_s, dv_s) = refs
        g, j = pl.program_id(1), pl.program_id(2)
        lo = _low_half((t, LANES))

        @pl.when(j == 0)
        def _():
            lo_s = _low_half((S, LANES))
            dq_s[...] = jnp.zeros(dq_s.shape, F32)
            for pr in range(P):
                cols = slice(pr * LANES, (pr + 1) * LANES)
                lse_s[2 * pr], lse_s[2 * pr + 1] = _both_halves(lse_ref[:, cols], lo_s)
                dd = do_ref[:, cols].astype(F32) * o_ref[:, cols].astype(F32)
                delta_s[2 * pr] = jnp.broadcast_to(jnp.sum(jnp.where(lo_s, dd, 0.0), -1, keepdims=True), (S, LANES))
                delta_s[2 * pr + 1] = jnp.broadcast_to(jnp.sum(jnp.where(lo_s, 0.0, dd), -1, keepdims=True),
                                                       (S, LANES))
            if decay:
                for hd in range(2 * P):
                    cq_s[hd] = jnp.broadcast_to(_head_column(csh_ref[...], 2 * P * g + hd), (S, LANES))
                dcq_s[...] = jnp.zeros(dcq_s.shape, F32)

        kb, vb = k_ref[...], v_ref[...]
        kh, vh = [], []
        for pr in range(P):
            kp, vp = kb[:, pr * W:(pr + 1) * W], vb[:, pr * LANES:(pr + 1) * LANES]
            zk, zv = jnp.zeros_like(kp), jnp.zeros_like(vp)
            kh += [jnp.where(lo, kp, zk), jnp.where(lo, zk, kp)] if split else [kp[:, :LANES], kp[:, LANES:]]
            vh += [jnp.where(lo, vp, zv), jnp.where(lo, zv, vp)]
        dk_s[...] = jnp.zeros(dk_s.shape, F32)
        dv_s[...] = jnp.zeros(dv_s.shape, F32)
        if decay:
            dck_s[...] = jnp.zeros(dck_s.shape, F32)

        def step(i, masked):
            rows = pl.ds(pl.multiple_of(i * t, t), t)
            qi, doi = q_ref[rows, :], do_ref[rows, :]
            for pr in range(P):
                qp, dop = qi[:, pr * W:(pr + 1) * W], doi[:, pr * LANES:(pr + 1) * LANES]
                for half in range(2):
                    hd = 2 * pr + half
                    qx = qp if split else qp[:, half * LANES:(half + 1) * LANES]
                    s = lax.dot_general(qx, kh[hd], NT, preferred_element_type=F32) * scale
                    if decay:
                        s = s + _widen(cq_s[hd, rows, :], t) - crow_ref[hd, j]
                    if masked:
                        s = _causal_mask(s)
                    p = jnp.exp(s - _widen(lse_s[hd, rows, :], t))
                    dv_s[hd] += lax.dot_general(p.astype(BF16), dop, TN, preferred_element_type=F32)
                    dp = lax.dot_general(dop, vh[hd], NT, preferred_element_type=F32)
                    ds = p * (dp - _widen(delta_s[hd, rows, :], t))
                    dss = (ds * scale).astype(BF16)
                    dk_s[hd] += lax.dot_general(dss, qx, TN, preferred_element_type=F32)
                    dqc = lax.dot_general(dss, kh[hd], NN, preferred_element_type=F32)
                    if split:
                        dq_s[rows, pr * W:(pr + 1) * W] += dqc
                    else:
                        dq_s[rows, hd * LANES:(hd + 1) * LANES] += dqc
                    if decay:
                        dck_s[hd] -= jnp.sum(ds, 0, keepdims=True)
                        part = ds[:, :LANES]
                        for c in range(1, t // LANES):
                            part = part + ds[:, c * LANES:(c + 1) * LANES]
                        dcq_s[hd, rows, :] += part

        def loop_body(i, carry):
            step(i, False)
            return carry

        step(j, True)
        lax.fori_loop(j + 1, nq, loop_body, 0)
        for pr in range(P):
            if split:
                dk_ref[:, pr * W:(pr + 1) * W] = jnp.where(lo, dk_s[2 * pr], dk_s[2 * pr + 1]).astype(dk_ref.dtype)
            else:
                for half in range(2):
                    hd = 2 * pr + half
                    dk_ref[:, hd * LANES:(hd + 1) * LANES] = dk_s[hd].astype(dk_ref.dtype)
            dv_ref[:, pr * LANES:(pr + 1) * LANES] = jnp.where(lo, dv_s[2 * pr], dv_s[2 * pr + 1]).astype(BF16)
        if decay:
            dck_ref[...] = dck_s[...]

        @pl.when(j == nq - 1)
        def _():
            dq_ref[...] = dq_s[...].astype(dq_ref.dtype)
            if decay:
                for hd in range(2 * P):
                    dcq_ref[hd] = jnp.sum(dcq_s[hd].T, 0, keepdims=True)

        if n_c:
            @pl.when((ids[0] == n_steps[0] - 1) & (ids[1] == n_steps[1] - 1) & (ids[2] == n_steps[2] - 1))
            def _():
                _comm_wait(kinds, c_in, c_out, sems, place)

    full = lambda w, blk0: pl.BlockSpec((S, P * w), lambda b, g, j: (b, blk0 // P + g))
    blk = lambda w, blk0: pl.BlockSpec((t, P * w), lambda b, g, j: (b * nq + j, blk0 // P + g))
    in_specs = [full(W, q_blk0), blk(W, k_blk0), blk(LANES, v_blk0), full(LANES, 0), full(LANES, do_blk0),
                full(LANES, 0)]
    args = [qa, ka, va, oa, doa, lsea]
    T = B * S
    out_specs = [full(W, 0), blk(W, 0), blk(LANES, 0)]
    out_shape = [jax.ShapeDtypeStruct((T, n_pairs * W), qk_dtype), jax.ShapeDtypeStruct((T, n_pairs * W), qk_dtype),
                 jax.ShapeDtypeStruct((T, n_pairs * LANES), BF16)]
    per_head = lambda rows: pltpu.VMEM((2 * P, rows, LANES), F32)
    scratch = [pltpu.VMEM((S, P * W), F32), per_head(S), per_head(S), per_head(t), per_head(t)]
    if decay:
        in_specs += [pl.BlockSpec((None, S, LANES), lambda b, g, j: (b, 0, 0)),
                     pl.BlockSpec((None, 2 * P, nq, 1, t), lambda b, g, j: (b, g, 0, 0, 0))]
        args += [csh, crow]
        out_specs += [pl.BlockSpec((None, 2 * P, None, 1, t), lambda b, g, j: (b, g, j, 0, 0)),
                      pl.BlockSpec((None, 2 * P, 1, S), lambda b, g, j: (b, g, 0, 0))]
        out_shape += [jax.ShapeDtypeStruct((B, 2 * n_pairs, nq, 1, t), F32),
                      jax.ShapeDtypeStruct((B, 2 * n_pairs, 1, S), F32)]
        scratch += [per_head(S), per_head(S), pltpu.VMEM((2 * P, 1, t), F32)]
    res = pl.pallas_call(
        body, name=name, grid=n_steps, in_specs=in_specs + [HBM_SPEC] * n_c,
        out_specs=out_specs + [HBM_SPEC] * n_c, out_shape=out_shape + _comm_out_shapes(comm),
        scratch_shapes=scratch + (_comm_scratch(comm) if n_c else []),
        compiler_params=_cparams(*(("arbitrary",) * 3 if n_c else ("parallel", "parallel", "arbitrary"))),
    )(*args, *[a for _, a in comm])
    return tuple(res[:n_out]) + (list(res[n_out:]),)


def _swa_common(q_ref, kp_ref, ko_ref, vp_ref, vo_ref, n):
    Q = BLOCK_Q
    lo = _low_half((Q, LANES))
    lo2 = _low_half((2 * Q, LANES))
    kk = jnp.concatenate([kp_ref[...], ko_ref[...]], axis=0)
    vv = jnp.concatenate([vp_ref[...], vo_ref[...]], axis=0)
    kdup = [x.astype(BF16) for x in _both_halves(kk, lo2)]
    vdup = [x.astype(BF16) for x in _both_halves(vv, lo2)]
    a = lax.broadcasted_iota(jnp.int32, (SWA_GROUP * Q, 2 * Q), 0) % Q
    col = lax.broadcasted_iota(jnp.int32, (SWA_GROUP * Q, 2 * Q), 1)
    dist = a + Q - col
    valid = (dist >= 0) & (dist < SWA_WINDOW) & ((col >= Q) | (n > 0))
    qv = q_ref[...]
    qm = []
    for a_head in range(SWA_HEADS):
        qp = qv[:, (a_head // 2) * LANES:(a_head // 2 + 1) * LANES]
        keep = lo if a_head % 2 == 0 else jnp.logical_not(lo)
        qm.append(jnp.where(keep, qp, 0.0).astype(BF16))
    qs = [jnp.concatenate(qm[g * SWA_GROUP:(g + 1) * SWA_GROUP], axis=0) for g in range(SWA_KV_HEADS)]
    return lo, lo2, kdup, vdup, valid, qs


def _swa_group_logits(g, qs, kdup, valid, bias_ref):
    heads = slice(g * SWA_GROUP, (g + 1) * SWA_GROUP)
    s = lax.dot_general(qs[g], kdup[g], NT, preferred_element_type=F32) * (HEAD_DIM ** -0.5)
    s = s + bias_ref[heads].reshape(SWA_GROUP * BLOCK_Q, 2 * BLOCK_Q)
    return heads, jnp.where(valid, s, NEG_INF)


def _pair_halves(x, lo):
    Q = BLOCK_Q
    return [jnp.where(lo, x[2 * pr * Q:(2 * pr + 1) * Q], x[(2 * pr + 1) * Q:(2 * pr + 2) * Q])
            for pr in range(SWA_GROUP // 2)]


def _swa_in_specs(nb):
    Q = BLOCK_Q
    own = lambda blk: (lambda b, n: (b * nb + n, blk))
    prev = lambda blk: (lambda b, n: (b * nb + jnp.maximum(n - 1, 0), blk))
    kb, vb = EV_KS[0] // LANES, EV_VS[0] // LANES
    return [pl.BlockSpec((Q, SWA_HEADS * HEAD_DIM), own(0)), pl.BlockSpec((Q, LANES), prev(kb)),
            pl.BlockSpec((Q, LANES), own(kb)), pl.BlockSpec((Q, LANES), prev(vb)), pl.BlockSpec((Q, LANES), own(vb))]


def _swa_fwd(h, bias, sinkcol, *, B, S, comm=(), name):
    Q = BLOCK_Q
    nb = S // Q
    n_c, kinds = len(comm), [k for k, _ in comm]

    def body(*refs):
        c_in, c_out, sems = refs[7:7 + n_c], refs[9 + n_c:9 + 2 * n_c], refs[9 + 2 * n_c:]
        q_ref, kp_ref, ko_ref, vp_ref, vo_ref, bias_ref, sink_ref = refs[:7]
        o_ref, lse_ref = refs[7 + n_c:9 + n_c]
        if n_c:
            place = _mesh_place()
            ids = [pl.program_id(0), pl.program_id(1)]

            @pl.when((ids[0] == 0) & (ids[1] == 0))
            def _():
                _comm_start(kinds, c_in, c_out, sems, place)

        lo, lo2, kdup, vdup, valid, qs = _swa_common(q_ref, kp_ref, ko_ref, vp_ref, vo_ref, pl.program_id(1))
        pairs = []
        lo4 = _low_half((SWA_GROUP * Q, LANES))
        for g in range(SWA_KV_HEADS):
            heads, s = _swa_group_logits(g, qs, kdup, valid, bias_ref)
            sink = jnp.broadcast_to(sink_ref[heads].reshape(SWA_GROUP * Q, 1), (SWA_GROUP * Q, LANES))
            m = jnp.maximum(jnp.max(s, -1, keepdims=True), sink)
            p = jnp.exp(s - _widen(m, 2 * Q))
            vaug = jnp.where(lo2, vdup[g], jnp.ones_like(vdup[g]))
            pv = lax.dot_general(p.astype(BF16), vaug, NN, preferred_element_type=F32)
            rolled = pltpu.roll(pv, HEAD_DIM, 1)
            l = jnp.where(lo4, rolled, pv) + jnp.exp(sink - m)
            out = pv / l
            lse_g = m + jnp.log(l)
            for i in range(SWA_GROUP):
                a = g * SWA_GROUP + i
                lse_ref[:, a * LANES:(a + 1) * LANES] = lse_g[i * Q:(i + 1) * Q]
            shifted = pltpu.roll(out, HEAD_DIM, 1)
            pairs += [jnp.where(lo, out[2 * pr * Q:(2 * pr + 1) * Q], shifted[(2 * pr + 1) * Q:(2 * pr + 2) * Q])
                      for pr in range(SWA_GROUP // 2)]
        o_ref[...] = jnp.concatenate(pairs, axis=1).astype(BF16)
        if n_c:
            @pl.when((ids[0] == B - 1) & (ids[1] == nb - 1))
            def _():
                _comm_wait(kinds, c_in, c_out, sems, place)

    whole = lambda shape: pl.BlockSpec(shape, lambda b, n: (0,) * len(shape))
    res = pl.pallas_call(
        body, name=name, grid=(B, nb),
        in_specs=_swa_in_specs(nb) + [whole((SWA_HEADS, Q, 2 * Q)), whole((SWA_HEADS, Q, 1))] + [HBM_SPEC] * n_c,
        out_specs=[pl.BlockSpec((Q, SWA_HEADS * HEAD_DIM), lambda b, n: (b * nb + n, 0)),
                   pl.BlockSpec((Q, SWA_HEADS * LANES), lambda b, n: (b * nb + n, 0))] + [HBM_SPEC] * n_c,
        out_shape=[jax.ShapeDtypeStruct((B * S, SWA_HEADS * HEAD_DIM), BF16),
                   jax.ShapeDtypeStruct((B * S, SWA_HEADS * LANES), F32)] + _comm_out_shapes(comm),
        scratch_shapes=_comm_scratch(comm) if n_c else [],
        compiler_params=_cparams(*(("arbitrary",) * 2 if n_c else ("parallel",) * 2)),
    )(h, h, h, h, h, bias, sinkcol, *[a for _, a in comm])
    return res[0], res[1], list(res[2:])


def _swa_bwd(h, o, do, lse, bias, sinkcol, *, do_blk0, B, S, comm=(), name):
    Q = BLOCK_Q
    nb = S // Q
    scale = HEAD_DIM ** -0.5
    n_c, kinds = len(comm), [k for k, _ in comm]

    def body(*refs):
        c_in, c_out, sems = refs[10:10 + n_c], refs[17 + n_c:17 + 2 * n_c], refs[17 + 2 * n_c:]
        q_ref, kp_ref, ko_ref, vp_ref, vo_ref, o_ref, do_ref, lse_ref, bias_ref, sink_ref = refs[:10]
        dq_ref, dko_ref, dkp_ref, dvo_ref, dvp_ref, dbias_ref, dsink_ref = refs[10 + n_c:17 + n_c]
        ids = [pl.program_id(0), pl.program_id(1)]
        if n_c:
            place = _mesh_place()

        @pl.when((ids[0] == 0) & (ids[1] == 0))
        def _():
            dbias_ref[...] = jnp.zeros_like(dbias_ref)
            dsink_ref[...] = jnp.zeros_like(dsink_ref)
            if n_c:
                _comm_start(kinds, c_in, c_out, sems, place)

        lo, lo2, kdup, vdup, valid, qs = _swa_common(q_ref, kp_ref, ko_ref, vp_ref, vo_ref, pl.program_id(1))
        dkk, dvv, dq_pairs = [], [], []
        for g in range(SWA_KV_HEADS):
            heads, s = _swa_group_logits(g, qs, kdup, valid, bias_ref)
            lse_g = jnp.concatenate([lse_ref[:, a * LANES:(a + 1) * LANES]
                                     for a in range(g * SWA_GROUP, (g + 1) * SWA_GROUP)], axis=0)
            p = jnp.exp(s - _widen(lse_g, 2 * Q))
            do_g, o_g = [], []
            for i in range(SWA_GROUP):
                cols = slice((g * SWA_GROUP + i) // 2 * LANES, ((g * SWA_GROUP + i) // 2 + 1) * LANES)
                do_p = do_ref[:, cols]
                do_g.append(jnp.where(lo if i % 2 == 0 else jnp.logical_not(lo), do_p, jnp.zeros_like(do_p)))
                o_g.append(o_ref[:, cols])
            doh, oh = jnp.concatenate(do_g, axis=0), jnp.concatenate(o_g, axis=0)
            delta = jnp.sum(doh.astype(F32) * oh.astype(F32), -1, keepdims=True)
            dp = lax.dot_general(doh, vdup[g], NT, preferred_element_type=F32)
            ds = p * (dp - delta)
            dbias_ref[heads] += ds.reshape(SWA_GROUP, Q, 2 * Q)
            dsink_ref[heads] -= (jnp.exp(sink_ref[heads].reshape(SWA_GROUP * Q, 1) - lse_g[:, :1])
                                 * delta).reshape(SWA_GROUP, Q, 1)
            dss = (ds * scale).astype(BF16)
            dq_pairs += _pair_halves(lax.dot_general(dss, kdup[g], NN, preferred_element_type=F32), lo)
            dkk.append(lax.dot_general(dss, qs[g], TN, preferred_element_type=F32))
            dvv.append(lax.dot_general(p.astype(BF16), doh, TN, preferred_element_type=F32))
        dq_ref[...] = jnp.concatenate(dq_pairs, axis=1).astype(BF16)
        fold = lambda x: x + pltpu.roll(x, HEAD_DIM, 1)
        dk_blk = jnp.where(lo2, fold(dkk[0]), fold(dkk[1]))
        dv_blk = jnp.where(lo2, fold(dvv[0]), fold(dvv[1]))
        dkp_ref[...] = dk_blk[:Q]
        dko_ref[...] = dk_blk[Q:]
        dvp_ref[...] = dv_blk[:Q]
        dvo_ref[...] = dv_blk[Q:]
        if n_c:
            @pl.when((ids[0] == B - 1) & (ids[1] == nb - 1))
            def _():
                _comm_wait(kinds, c_in, c_out, sems, place)

    whole = lambda shape: pl.BlockSpec(shape, lambda b, n: (0,) * len(shape))
    wide = lambda blk: pl.BlockSpec((Q, SWA_HEADS * HEAD_DIM), lambda b, n: (b * nb + n, blk))
    narrow = pl.BlockSpec((Q, LANES), lambda b, n: (b * nb + n, 0))
    kv_shape = jax.ShapeDtypeStruct((B * S, LANES), F32)
    res = pl.pallas_call(
        body, name=name, grid=(B, nb),
        in_specs=_swa_in_specs(nb) + [wide(0), wide(do_blk0),
                                      pl.BlockSpec((Q, SWA_HEADS * LANES), lambda b, n: (b * nb + n, 0)),
                                      whole((SWA_HEADS, Q, 2 * Q)),
                                      whole((SWA_HEADS, Q, 1))] + [HBM_SPEC] * n_c,
        out_specs=[wide(0), narrow, narrow, narrow, narrow, whole((SWA_HEADS, Q, 2 * Q)), whole((SWA_HEADS, Q, 1))]
        + [HBM_SPEC] * n_c,
        out_shape=[jax.ShapeDtypeStruct((B * S, SWA_HEADS * HEAD_DIM), BF16), kv_shape, kv_shape, kv_shape, kv_shape,
                   jax.ShapeDtypeStruct((SWA_HEADS, Q, 2 * Q), F32), jax.ShapeDtypeStruct((SWA_HEADS, Q, 1), F32)]
        + _comm_out_shapes(comm),
        scratch_shapes=_comm_scratch(comm) if n_c else [],
        compiler_params=_cparams("arbitrary", "arbitrary"),
    )(h, h, h, h, h, o, do, lse, bias, sinkcol, *[a for _, a in comm])
    return tuple(res[:7]) + (list(res[7:]),)


def _bias_bucket_sum(dbias, bucket, *, name):
    def body(d_ref, b_ref, o_ref):
        dbv, bk = d_ref[...], b_ref[...]
        lane = lax.broadcasted_iota(jnp.int32, (SWA_HEADS, LANES), 1)
        out = jnp.zeros((SWA_HEADS, LANES), F32)
        for b in range(REL_BUCKETS):
            part = jnp.sum(jnp.where(bk == b, dbv, 0.0), axis=1)
            tot = jnp.sum(part, axis=-1, keepdims=True)
            out = out + jnp.where(lane == b, tot, 0.0)
        o_ref[...] = out

    return pl.pallas_call(
        body, name=name, out_shape=jax.ShapeDtypeStruct((SWA_HEADS, LANES), F32),
        compiler_params=pltpu.CompilerParams(vmem_limit_bytes=VMEM_LIMIT_BYTES),
    )(dbias, bucket)


def _adamw_update(w, g, m, v):
    m_new = ADAM_B1 * m + (1.0 - ADAM_B1) * g
    v_new = ADAM_B2 * v + (1.0 - ADAM_B2) * jnp.square(g)
    m_hat = m_new / (1.0 - ADAM_B1 ** ADAM_STEP)
    v_hat = v_new / (1.0 - ADAM_B2 ** ADAM_STEP)
    return -ADAM_LR * (m_hat / (jnp.sqrt(v_hat) + ADAM_EPS) + ADAM_WD * w), m_new, v_new


def _adamw(w, g, m, v, *, name):
    def body(w_ref, g_ref, m_ref, v_ref, d_ref, nm_ref, nv_ref):
        d_ref[...], nm_ref[...], nv_ref[...] = _adamw_update(w_ref[...], g_ref[...], m_ref[...], v_ref[...])

    return pl.pallas_call(
        body, name=name, out_shape=[jax.ShapeDtypeStruct(w.shape, F32)] * 3,
        compiler_params=pltpu.CompilerParams(vmem_limit_bytes=VMEM_LIMIT_BYTES),
    )(w, g, m, v)


ADAMW_PARTS_BYTES = 8 * 1024 * 1024


def _adamw_slots(w, parts, m, v, *, name):
    n0, R, C = w.shape
    tr = next((c for c in (512, 256, 128, 64, 32, 16, 8) if R % c == 0 and 4 * n0 * N_DEV * c * C <= ADAMW_PARTS_BYTES), R)

    def body(*refs):
        w_ref, p_refs, (m_ref, v_ref, g_ref, d_ref, nm_ref, nv_ref) = refs[0], refs[1:1 + n0], refs[1 + n0:]
        layer = pl.program_id(0)
        for l in range(n0):
            @pl.when(layer == l)
            def _(p_ref=p_refs[l]):
                g = p_ref[0].astype(F32)
                for j in range(1, N_DEV):
                    g = g + p_ref[j].astype(F32)
                g_ref[...] = g
                d_ref[...], nm_ref[...], nv_ref[...] = _adamw_update(w_ref[...], g, m_ref[...], v_ref[...])

    spec = pl.BlockSpec((None, tr, C), lambda l, i: (l, i, 0))
    part_spec = lambda own: pl.BlockSpec((N_DEV, tr, C), lambda l, i: (0, jnp.where(l == own, i, 0), 0))
    return pl.pallas_call(
        body, name=name, grid=(n0, R // tr),
        in_specs=[spec] + [part_spec(l) for l in range(n0)] + [spec, spec], out_specs=[spec] * 4,
        out_shape=[jax.ShapeDtypeStruct((n0, R, C), F32)] * 4, compiler_params=_cparams("arbitrary", "arbitrary"),
    )(w, *parts, m, v)


def _all_gather_hbm(blocks, *, name):
    n = len(blocks)

    def body(*refs):
        x_refs, out_refs = refs[:n], refs[n:2 * n]
        send_sems, recv_sems, local_sems = refs[2 * n:]
        x, y, c, _ = _mesh_place()
        me, sibling = (x, y, c), (x, y, 1 - c)
        chips = [(1 - x, y), (x, 1 - y), (1 - x, 1 - y)]

        def copy(w, k, blk, to, src=None):
            px, py, pc = blk
            slot = out_refs[w].at[4 * px + 2 * py + pc]
            return pltpu.make_async_remote_copy(
                src_ref=slot if src is None else src, dst_ref=slot,
                send_sem=send_sems.at[w, k], recv_sem=recv_sems.at[w, k], device_id=to, device_id_type=MESH_ID)

        mine = [pltpu.make_async_copy(x_refs[w], out_refs[w].at[4 * x + 2 * y + c], local_sems.at[w])
                for w in range(n)]
        for cp in mine:
            cp.start()
        first = []
        for w in range(n):
            first.append(copy(w, 0, me, sibling, src=x_refs[w]))
            first += [copy(w, 1 + j, me, (*chip, c), src=x_refs[w]) for j, chip in enumerate(chips)]
        for cp in first:
            cp.start()
        passed = []
        for j, chip in enumerate(chips):
            for w in range(n):
                copy(w, 1 + j, (*chip, c), me).wait_recv()
                fwd = copy(w, 4 + j, (*chip, c), sibling)
                fwd.start()
                passed.append(fwd)
        for w in range(n):
            copy(w, 0, sibling, me).wait_recv()
            for j, chip in enumerate(chips):
                copy(w, 4 + j, (*chip, 1 - c), me).wait_recv()
        for cp in first + passed:
            cp.wait_send()
        for cp in mine:
            cp.wait()

    return pl.pallas_call(
        body, name=name, out_shape=[jax.ShapeDtypeStruct((N_DEV,) + b.shape, b.dtype) for b in blocks],
        in_specs=[HBM_SPEC] * n, out_specs=[HBM_SPEC] * n,
        scratch_shapes=[pltpu.SemaphoreType.DMA((n, 7)), pltpu.SemaphoreType.DMA((n, 7)),
                        pltpu.SemaphoreType.DMA((n,))],
    )(*blocks)


def _all_reduce_small(block, *, name):
    R, W = block.shape

    def body(x_ref, out_ref, buf, send_sems, recv_sems):
        x, y, c, me = _mesh_place()
        copies = []
        for k, (peer, _) in enumerate(_peers(x, y, c)):
            copies.append(pltpu.make_async_remote_copy(
                src_ref=x_ref, dst_ref=buf.at[me], send_sem=send_sems.at[k], recv_sem=recv_sems.at[k],
                device_id=peer, device_id_type=MESH_ID))
        for cp in copies:
            cp.start()
        buf[me] = x_ref[...]
        for cp in copies:
            cp.wait_recv()
        for cp in copies:
            cp.wait_send()
        acc = buf[0]
        for j in range(1, N_DEV):
            acc = acc + buf[j]
        out_ref[...] = acc

    return pl.pallas_call(
        body, name=name, out_shape=jax.ShapeDtypeStruct((R, W), F32),
        in_specs=[VMEM_SPEC], out_specs=VMEM_SPEC,
        scratch_shapes=[pltpu.VMEM((N_DEV, R, W), F32), pltpu.SemaphoreType.DMA((7,)), pltpu.SemaphoreType.DMA((7,))],
    )(block)


def _assemble(name, g):
    if BIG_AXIS[name] == 2:
        return jnp.concatenate([g[j] for j in range(N_DEV)], axis=1)
    return g.reshape(N_DEV * g.shape[1], g.shape[2])


def _split_for_devices(name, g):
    if BIG_AXIS[name] == 2:
        b = g.shape[1] // N_DEV
        return jnp.stack([g[:, j * b:(j + 1) * b] for j in range(N_DEV)]).astype(BF16)
    return g.reshape(N_DEV, g.shape[0] // N_DEV, g.shape[1]).astype(BF16)


def _layer_weight_keys(i):
    j = i // 2
    mixer = [('ev_w_in', j), ('ev_w_uq', j), ('ev_w_ukv', j), ('ev_w_out', j)] if i % 2 == 0 \
        else [('od_w_in', j), ('od_w_out', j)]
    return mixer + [('w_up', i), ('w_down', i), ('ple_w_proj', i), ('ple_w_gate', i)]


def _weight_layer(key):
    name, idx = key
    return 2 * idx if name.startswith('ev_') else 2 * idx + 1 if name.startswith('od_') else idx


FIRST_GATHER = [('ev_w_in', 0), ('ev_w_uq', 0), ('ev_w_ukv', 0), ('ev_w_out', 0)]
FWD_CARRIERS = {
    'l0_mla': [('w_up', 0), ('ple_w_proj', 0), ('ple_w_gate', 0)],
    'l0_swa': [('w_down', 0)],
    'l0_out_ln1': [('od_w_out', 0)],
    'l0_up': [('od_w_in', 0)],
    'l0_down_ln2': [('w_up', 1)],
    'l0_ple_gate': [('ple_w_proj', 1), ('ple_w_gate', 1)],
    'l1_fox': [('w_down', 1), ('ev_w_in', 1), ('ev_w_uq', 1), ('ev_w_ukv', 1), ('ev_w_out', 1), ('w_up', 2)],
    'l1_up': [('w_down', 2)],
    'l1_down_ln2': [('ple_w_proj', 2), ('ple_w_gate', 2)],
    'l2_mla': [('od_w_in', 1), ('od_w_out', 1)],
    'l2_swa': [('w_up', 3)],
    'l2_up': [('w_down', 3)],
    'l2_down_ln2': [('ple_w_proj', 3), ('ple_w_gate', 3)],
}


class _MeshExchange:
    def __init__(self, shards):
        self.shards = shards
        self.weights = {i: {} for i in range(DEPTH)}
        self.pending = []
        self.in_flight = []
        self.received = {}
        got = _all_gather_hbm([self.shards[n][idx] for n, idx in FIRST_GATHER], name="gather_first")
        self._landed(FIRST_GATHER, got)

    def _landed(self, keys, gathered):
        for k, g in zip(keys, gathered):
            self.weights[_weight_layer(k)][k[0]] = _assemble(k[0], g)

    def layer_weights(self, i):
        return self.weights[i]

    def carry(self, kernel_name):
        return [(("gather", idx), self.shards[n]) for n, idx in FWD_CARRIERS.get(kernel_name, [])]

    def carried(self, kernel_name, outs):
        self._landed(FWD_CARRIERS.get(kernel_name, []), outs)

    def push_grads(self, grads):
        self.pending += [(k, _split_for_devices(k[0], g)) for k, g in grads.items()]

    def bwd_items(self):
        self.in_flight, self.pending = self.pending, []
        return [("scatter", parts) for _, parts in self.in_flight]

    def bwd_done(self, outs):
        for (k, _), got in zip(self.in_flight, outs):
            self.received[k] = got
        self.in_flight = []

    def finish(self):
        if self.pending:
            outs = _exchange(self.bwd_items(), name="scatter_rest")
            self.bwd_done(outs)
        return self.received


PACK_ROWS = 8


def _pack_small(vals):
    flat = jnp.concatenate([vals[n].reshape(-1).astype(F32) for n in SMALL])
    pad = (-flat.shape[0]) % (PACK_ROWS * LANES)
    return jnp.pad(flat, (0, pad)).reshape(-1, LANES)


def _unpack_small(block, shapes):
    flat = block.reshape(-1)
    out, off = {}, 0
    for n in SMALL:
        sz = math.prod(shapes[n])
        out[n] = flat[off:off + sz].reshape(shapes[n])
        off += sz
    return out


def _rope_tables(S):
    half = MLA_ROPE // 2
    inv = 1.0 / (ROPE_THETA ** (jnp.arange(0, MLA_ROPE, 2, dtype=F32) / MLA_ROPE))
    ang = jnp.arange(S, dtype=F32)[:, None] * inv[None, :]
    cos, sin = jnp.cos(ang), jnp.sin(ang)
    zeros = jnp.zeros((S, half), F32)
    tail = jnp.zeros((S, LANES - MLA_QK), F32)

    def block(rope_part, nope_val):
        return jnp.concatenate([jnp.full((S, MLA_NOPE), nope_val, F32), rope_part, tail], -1)

    a_r = jnp.concatenate([cos, cos], -1)
    bm_r = jnp.concatenate([-sin, zeros], -1)
    bp_r = jnp.concatenate([zeros, sin], -1)
    q_tabs = tuple(block(r, v) for r, v in ((a_r, 1.0), (bm_r, 0.0), (bp_r, 0.0)))
    k_tabs = tuple(block(r, 0.0) for r in (a_r, bm_r, bp_r))
    return q_tabs, k_tabs


def _t5_bucket(dist):
    exact = REL_BUCKETS // 2
    d = jnp.maximum(dist, 1).astype(F32)
    large = exact + (jnp.log(d / exact) / math.log(REL_MAX_DIST / exact) * (REL_BUCKETS - exact)).astype(jnp.int32)
    large = jnp.minimum(large, REL_BUCKETS - 1)
    return jnp.where(dist < exact, dist, large)


def _swa_bucket_table():
    a = jnp.arange(BLOCK_Q)[:, None]
    col = jnp.arange(2 * BLOCK_Q)[None, :]
    return _t5_bucket(jnp.maximum(a + BLOCK_Q - col, 0)).astype(jnp.int32)


def _even_weights(W):
    w = W['ev_w_in']
    c_kv1 = MLA_Q_LORA + MLA_KV_LORA
    c_kr1 = c_kv1 + MLA_ROPE
    c_qs1 = c_kr1 + SWA_HEADS * HEAD_DIM
    zeros = lambda n: jnp.zeros((D_MODEL, n), w.dtype)
    w_in = jnp.concatenate([w[:, c_kr1:c_qs1], w[:, :c_kv1], w[:, c_qs1:], zeros(KR_LANE0), w[:, c_kv1:c_kr1],
                            zeros(LANES - KR_LANE0 - MLA_ROPE)], axis=1)
    uq = W['ev_w_uq'].reshape(MLA_Q_LORA, MLA_HEADS, MLA_QK)
    w_uq = jnp.pad(uq, ((0, 0), (0, 0), (0, LANES - MLA_QK))).reshape(MLA_Q_LORA, MLA_HEADS * LANES)
    ukv = W['ev_w_ukv'].reshape(MLA_KV_LORA, MLA_HEADS, MLA_NOPE + MLA_V)
    w_k = jnp.pad(ukv[..., :MLA_NOPE], ((0, 0), (0, 0), (0, LANES - MLA_NOPE))).reshape(MLA_KV_LORA, -1)
    w_v = ukv[..., MLA_NOPE:].reshape(MLA_KV_LORA, MLA_HEADS * MLA_V)
    return w_in, w_uq, w_k, w_v, W['ev_w_out']


def _even_in_grad_unpad(dw):
    kr0 = EV_KR[0] + KR_LANE0
    return jnp.concatenate([dw[:, EV_CQ[0]:EV_CKV[1]], dw[:, kr0:kr0 + MLA_ROPE], dw[:, EV_QS[0]:EV_QS[1]],
                            dw[:, EV_KS[0]:EV_VS[1]]], axis=1)


def _even_fwd(xb, W, P, i, B, S, tabs, xchg, tag):
    j = i // 2
    q_tabs, k_tabs, bias, sinkcol = tabs
    w_in, w_uq, w_k, w_v, w_out = _even_weights(W)
    h = _mm(xb, w_in, name=f"{tag}_in")
    cqn, ckvn, rq, rkv = _even_norms(h, P['ev_q_norm'][j][None], P['ev_kv_norm'][j][None], name=f"{tag}_norms")
    q = _rope(_mm(cqn, w_uq, name=f"{tag}_uq"), q_tabs, S, sign=1.0, name=f"{tag}_ropeq")
    knp = _mm(ckvn, w_k, out_dtypes=(BF16,), name=f"{tag}_uk")
    v = _mm(ckvn, w_v, out_dtypes=(BF16,), name=f"{tag}_uv")
    k = _mla_keys(knp, h, k_tabs, S, name=f"{tag}_keys")
    o_mla, lse_mla, got = _flash_fwd(q, k, v, q_blk0=0, k_blk0=0, v_blk0=0, W=2 * LANES, n_pairs=MLA_HEADS // 2,
                                     B=B, S=S, scale=MLA_QK ** -0.5, comm=xchg.carry(f"{tag}_mla"), name=f"{tag}_mla")
    xchg.carried(f"{tag}_mla", got)
    o_swa, lse_swa, got = _swa_fwd(h, bias, sinkcol, B=B, S=S, comm=xchg.carry(f"{tag}_swa"), name=f"{tag}_swa")
    xchg.carried(f"{tag}_swa", got)
    o_cat = jnp.concatenate([o_mla, o_swa], axis=-1)
    res = dict(h=h, cqn=cqn, ckvn=ckvn, rq=rq, rkv=rkv, q=q, k=k, v=v, o_mla=o_mla, lse_mla=lse_mla,
               o_swa=o_swa, lse_swa=lse_swa, o_cat=o_cat)
    return (o_cat, w_out), res


def _shift_prev(own, prev, B, S):
    prev = prev.reshape(B, S, LANES)
    shifted = jnp.concatenate([prev[:, BLOCK_Q:], jnp.zeros_like(prev[:, :BLOCK_Q])], axis=1)
    return (own + shifted.reshape(B * S, LANES)).astype(BF16)


def _even_bwd(dmb, dz1, xb, W, P, j, B, S, tabs, res, xchg, tag):
    q_tabs, k_tabs, bias, sinkcol = tabs
    w_in, w_uq, w_k, w_v, w_out = _even_weights(W)
    g = {}
    g['ev_w_out'] = _mm_tn(res['o_cat'], dmb, name=f"{tag}_dwout")
    do = _mm(dmb, w_out, trans_b=True, out_dtypes=(BF16,), name=f"{tag}_do")
    dq, dk, dv, got = _flash_bwd(res['q'], res['k'], res['v'], res['o_mla'], do, res['lse_mla'], q_blk0=0, k_blk0=0,
                                 v_blk0=0, do_blk0=0, W=2 * LANES, n_pairs=MLA_HEADS // 2, B=B, S=S,
                                 scale=MLA_QK ** -0.5, qk_dtype=F32, comm=xchg.bwd_items(), name=f"{tag}_mla_bwd")
    xchg.bwd_done(got)
    dq_pre = _rope(dq, q_tabs, S, sign=-1.0, name=f"{tag}_ropeq_bwd")
    dw_uq = _mm_tn(res['cqn'], dq_pre, name=f"{tag}_dwuq")
    g['ev_w_uq'] = dw_uq.reshape(MLA_Q_LORA, MLA_HEADS, LANES)[..., :MLA_QK].reshape(MLA_Q_LORA, MLA_HEADS * MLA_QK)
    dcqn = _mm(dq_pre, w_uq, trans_b=True, name=f"{tag}_dcqn")
    dw_k = _mm_tn(res['ckvn'], dk, name=f"{tag}_dwuk").reshape(MLA_KV_LORA, MLA_HEADS, LANES)[..., :MLA_NOPE]
    dw_v = _mm_tn(res['ckvn'], dv, name=f"{tag}_dwuv").reshape(MLA_KV_LORA, MLA_HEADS, MLA_V)
    g['ev_w_ukv'] = jnp.concatenate([dw_k, dw_v], axis=-1).reshape(MLA_KV_LORA, MLA_HEADS * (MLA_NOPE + MLA_V))
    dckvn_v = _mm(dv, w_v, trans_b=True, name=f"{tag}_dckvn_v")
    dckvn = _mm(dk, w_k, trans_b=True, extras=(dckvn_v,), epilogue=lambda acc, r: (acc + r,), name=f"{tag}_dckvn")
    dkr_pre = _mla_rope_key_grad(dk, k_tabs, S, name=f"{tag}_ropek_bwd")
    xchg.push_grads({(n, j): g.pop(n) for n in list(g)})
    dqs, dko, dkp, dvo, dvp, dbias, dsink, got = _swa_bwd(res['h'], res['o_swa'], do, res['lse_swa'], bias, sinkcol,
                                                          do_blk0=1, B=B, S=S, comm=xchg.bwd_items(),
                                                          name=f"{tag}_swa_bwd")
    xchg.bwd_done(got)
    dh, dgq, dgkv = _even_in_bwd(res['h'], res['rq'], res['rkv'], P['ev_q_norm'][j][None], P['ev_kv_norm'][j][None],
                                 dcqn, dckvn, dqs, _shift_prev(dko, dkp, B, S), _shift_prev(dvo, dvp, B, S), dkr_pre,
                                 name=f"{tag}_in_bwd")
    g['ev_w_in'] = _even_in_grad_unpad(_mm_tn(xb, dh, name=f"{tag}_dwin"))
    xchg.push_grads({(n, j): val for n, val in g.items()})
    dx_kwargs = dict(trans_b=True, extras=(dz1,), epilogue=lambda acc, r: (acc + DN_ALPHA * r,), name=f"{tag}_dx")
    dx = _scattering(xchg, _mm, dh, w_in, **dx_kwargs) if j == 0 else _mm(dh, w_in, **dx_kwargs)
    small = dict(ev_q_norm=dgq[0], ev_kv_norm=dgkv[0], dbias=dbias, ev_sinks=jnp.sum(dsink, axis=(1, 2)))
    return dx, small


def _odd_fwd(xb, W, P, i, B, S, xchg, tag):
    j = i // 2
    w = W['od_w_in']
    w_qkv = w[:, :ODD_QKV]
    w_f = jnp.pad(w[:, ODD_QKV:], ((0, 0), (0, LANES - FOX_HEADS)))
    bf = jnp.pad(P['od_b_f'][j], (0, LANES - FOX_HEADS))[None]
    qkv = _mm(xb, w_qkv, out_dtypes=(BF16,), name=f"{tag}_qkv")
    f = _mm(xb, w_f, name=f"{tag}_f").reshape(B, S, LANES)
    csh, chs = _fox_decay_fwd(f, bf, name=f"{tag}_decay")
    crow = chs[:, :FOX_HEADS].reshape(B, FOX_HEADS, S // ATT_TILE, 1, ATT_TILE)
    n_blk = FOX_HEADS * HEAD_DIM // LANES
    o, lse, got = _flash_fwd(qkv, qkv, qkv, q_blk0=0, k_blk0=n_blk, v_blk0=2 * n_blk, W=LANES,
                             n_pairs=FOX_HEADS // 2, B=B, S=S, scale=HEAD_DIM ** -0.5, csh=csh, crow=crow,
                             comm=xchg.carry(f"{tag}_fox"), name=f"{tag}_fox")
    xchg.carried(f"{tag}_fox", got)
    res = dict(f=f, bf=bf, csh=csh, crow=crow, qkv=qkv, o=o, lse=lse, w_qkv=w_qkv, w_f=w_f)
    return (o, W['od_w_out']), res


def _odd_bwd(dmb, dz1, xb, W, P, j, B, S, res, xchg, tag):
    g = {}
    w_out = W['od_w_out']
    g['od_w_out'] = _mm_tn(res['o'], dmb, name=f"{tag}_dwout")
    do = _mm(dmb, w_out, trans_b=True, out_dtypes=(BF16,), name=f"{tag}_do")
    qkv = res['qkv']
    n_blk = FOX_HEADS * HEAD_DIM // LANES
    dq, dk, dv, dck, dcq, got = _flash_bwd(qkv, qkv, qkv, res['o'], do, res['lse'], q_blk0=0, k_blk0=n_blk,
                                           v_blk0=2 * n_blk, do_blk0=0, W=LANES, n_pairs=FOX_HEADS // 2, B=B, S=S,
                                           scale=HEAD_DIM ** -0.5, qk_dtype=BF16, csh=res['csh'], crow=res['crow'],
                                           comm=xchg.bwd_items(), name=f"{tag}_fox_bwd")
    xchg.bwd_done(got)
    dc = dck.reshape(B, FOX_HEADS, S) + dcq.reshape(B, FOX_HEADS, S)
    dc_hs = jnp.pad(dc, ((0, 0), (0, LANES - FOX_HEADS), (0, 0)))
    df, dbf = _fox_decay_bwd(dc_hs, res['f'], res['bf'], name=f"{tag}_decay_bwd")
    df = df.reshape(B * S, LANES)
    dqkv = jnp.concatenate([dq, dk, dv], axis=-1)
    dw_qkv = _mm_tn(xb, dqkv, name=f"{tag}_dwqkv")
    dw_f = _mm_tn(xb, df, name=f"{tag}_dwf")
    g['od_w_in'] = jnp.concatenate([dw_qkv, dw_f[:, :FOX_HEADS]], axis=1)
    dxf = _mm(df, res['w_f'], trans_b=True, extras=(dz1,), epilogue=lambda acc, r: (acc + DN_ALPHA * r,),
              name=f"{tag}_dxf")
    xchg.push_grads({(n, j): val for n, val in g.items()})
    dx = _mm(dqkv, res['w_qkv'], trans_b=True, extras=(dxf,), epilogue=lambda acc, r: (acc + r,), name=f"{tag}_dx")
    small = dict(od_b_f=dbf[0, :FOX_HEADS])
    return dx, small


def _carrying(xchg, name, call, *args, **kwargs):
    comm = xchg.carry(name)
    out = call(*args, comm=comm, name=name, **kwargs)
    if comm:
        out, got = out
        xchg.carried(name, got)
    return out


def _scattering(xchg, call, *args, **kwargs):
    comm = xchg.bwd_items()
    out = call(*args, comm=comm, **kwargs)
    if comm:
        out, got = out
        xchg.bwd_done(got)
    return out


def _local_step(x, p, target, P, xchg):
    B, S, D = x.shape
    T = B * S
    q_tabs, k_tabs = _rope_tables(S)
    bucket = _swa_bucket_table()
    in_bucket = (bucket[..., None] == jnp.arange(REL_BUCKETS)).astype(F32)
    bias = jnp.einsum('acb,bh->hac', in_bucket, P['rel_bias'], precision=lax.Precision.HIGHEST)

    xc = x.reshape(T, D)
    xcb = xc.astype(BF16)
    saved = []
    for i in range(DEPTH):
        j = i // 2
        tag = f"l{i}"
        W = xchg.layer_weights(i)
        lay = dict(xb=xcb, W=W)
        if i % 2 == 0:
            sinkcol = jnp.broadcast_to(P['ev_sinks'][j][:, None, None], (SWA_HEADS, BLOCK_Q, 1)).astype(F32)
            lay['tabs'] = (q_tabs, k_tabs, bias, sinkcol)
            (o, w_out), lay['mix'] = _even_fwd(xcb, W, P, i, B, S, lay['tabs'], xchg, tag)
        else:
            (o, w_out), lay['mix'] = _odd_fwd(xcb, W, P, i, B, S, xchg, tag)
        x1, x1b, lay['xh1'], lay['r1'] = _carrying(xchg, f"{tag}_out_ln1", _mm_ln, o, w_out, xc,
                                                   P['ln1_g'][i][None], P['ln1_b'][i][None])
        lay['x1b'] = x1b
        lay['u'], lay['a'] = _carrying(xchg, f"{tag}_up", _mm, x1b, W['w_up'], out_dtypes=(F32, BF16),
                                       epilogue=lambda acc: (acc, jnp.square(jnp.maximum(acc, 0.0))))
        x2, x2b, lay['xh2'], lay['r2'] = _carrying(xchg, f"{tag}_down_ln2", _mm_ln, lay['a'], W['w_down'], x1,
                                                   P['ln2_g'][i][None], P['ln2_b'][i][None])
        lay['x2b'] = x2b
        lay['p'] = p[i].reshape(T, D_PLE)
        lay['e'] = _mm(lay['p'], W['ple_w_proj'], name=f"{tag}_ple_proj")

        def gate(acc, bg, e, x2v):
            gv = 1.0 / (1.0 + jnp.exp(-(acc + bg)))
            y = x2v + gv * e
            return y, y, gv

        xc, xcb, lay['g'] = _carrying(xchg, f"{tag}_ple_gate", _mm, x2b, W['ple_w_gate'],
                                      extras=(P['ple_b_gate'][i][None], lay['e'], x2), epilogue=gate,
                                      out_dtypes=(F32, BF16, F32))
        saved.append(lay)

    dy, sq = _loss_grad(xc, target.reshape(T, D), name="loss")

    Gs = {n: [None] * DEPTH for n in ('ln1_g', 'ln1_b', 'ln2_g', 'ln2_b', 'ple_b_gate')}
    Gs.update({n: [None] * (DEPTH // 2) for n in ('ev_q_norm', 'ev_kv_norm', 'ev_sinks', 'od_b_f')})
    dbias_total = None
    for i in reversed(range(DEPTH)):
        j = i // 2
        tag = f"l{i}b"
        lay = saved[i]
        W = lay['W']
        de, dzg, dbg = _ple_bwd_elem(dy, lay['g'], lay['e'], name=f"{tag}_ple_elem")
        Gs['ple_b_gate'][i] = dbg[0]
        g_mlp = {('ple_w_proj', i): _mm_tn(lay['p'], de, name=f"{tag}_dwproj"),
                 ('ple_w_gate', i): _mm_tn(lay['x2b'], dzg, name=f"{tag}_dwgate")}
        dz2, dz2b, dg2, db2 = _mm_ln_bwd(dzg, W['ple_w_gate'], dy, 1.0, lay['xh2'], lay['r2'], P['ln2_g'][i][None],
                                         name=f"{tag}_dx2_ln2")
        Gs['ln2_g'][i], Gs['ln2_b'][i] = dg2[0], db2[0]
        g_mlp[('w_down', i)] = _mm_tn(lay['a'], dz2b, name=f"{tag}_dwdown")
        du = _mm(dz2b, W['w_down'], trans_b=True, extras=(lay['u'],), out_dtypes=(BF16,),
                 epilogue=lambda acc, u: (acc * (2.0 * jnp.maximum(u, 0.0)),), name=f"{tag}_du")
        g_mlp[('w_up', i)] = _mm_tn(lay['x1b'], du, name=f"{tag}_dwup")
        xchg.push_grads(g_mlp)
        dz1, dz1b, dg1, db1 = _mm_ln_bwd(du, W['w_up'], dz2, DN_ALPHA, lay['xh1'], lay['r1'], P['ln1_g'][i][None],
                                         name=f"{tag}_dx1_ln1")
        Gs['ln1_g'][i], Gs['ln1_b'][i] = dg1[0], db1[0]
        if i % 2 == 0:
            dy, small = _even_bwd(dz1b, dz1, lay['xb'], W, P, j, B, S, lay['tabs'], lay['mix'], xchg, tag)
            dbias_total = small['dbias'] if dbias_total is None else dbias_total + small['dbias']
            for n in ('ev_q_norm', 'ev_kv_norm', 'ev_sinks'):
                Gs[n][j] = small[n]
        else:
            dy, small = _odd_bwd(dz1b, dz1, lay['xb'], W, P, j, B, S, lay['mix'], xchg, tag)
            Gs['od_b_f'][j] = small['od_b_f']

    grads_small = {n: jnp.stack(v) for n, v in Gs.items()}
    drel = _bias_bucket_sum(dbias_total, bucket, name="rel_bias_grad")
    grads_small['rel_bias'] = drel[:, :REL_BUCKETS].T
    return sq, dy.reshape(B, S, D), grads_small


def kernel(x, p, rel_bias, ev_w_in, ev_q_norm, ev_w_uq, ev_kv_norm, ev_w_ukv, ev_sinks, ev_w_out, od_w_in, od_b_f, od_w_out, ln1_g, ln1_b, w_up, w_down, ln2_g, ln2_b, ple_w_proj, ple_w_gate, ple_b_gate, loss_target, m_rel_bias, m_ev_w_in, m_ev_q_norm, m_ev_w_uq, m_ev_kv_norm, m_ev_w_ukv, m_ev_sinks, m_ev_w_out, m_od_w_in, m_od_b_f, m_od_w_out, m_ln1_g, m_ln1_b, m_w_up, m_w_down, m_ln2_g, m_ln2_b, m_ple_w_proj, m_ple_w_gate, m_ple_b_gate, v_rel_bias, v_ev_w_in, v_ev_q_norm, v_ev_w_uq, v_ev_kv_norm, v_ev_w_ukv, v_ev_sinks, v_ev_w_out, v_od_w_in, v_od_b_f, v_od_w_out, v_ln1_g, v_ln1_b, v_w_up, v_w_down, v_ln2_g, v_ln2_b, v_ple_w_proj, v_ple_w_gate, v_ple_b_gate):
    given = dict(locals())
    w = {n: given[n] for n in WEIGHTS}
    mom = {n: given["m_" + n] for n in WEIGHTS}
    var = {n: given["v_" + n] for n in WEIGHTS}
    small_shapes = {n: w[n].shape for n in SMALL}

    xchg = _MeshExchange({n: w[n].astype(BF16) for n in BIG})
    P = {n: w[n] for n in SMALL}

    sq, grad_x, grads_small = _local_step(x, p, loss_target, P, xchg)
    loss = lax.psum(0.5 * jnp.sum(sq) / D_MODEL, ("x", "y", "c"))

    received = xchg.finish()
    g_small_packed = _all_reduce_small(_pack_small(grads_small), name="reduce_small_grads")
    g_small = _unpack_small(g_small_packed, small_shapes)

    grad, delta, new_m, new_v = {}, {}, {}, {}
    for n in BIG:
        parts = [received[(n, idx)] for idx in range(w[n].shape[0])]
        grad[n], delta[n], new_m[n], new_v[n] = _adamw_slots(w[n], parts, mom[n], var[n], name=f"adamw_{n}")
    d, nm, nv = _adamw(_pack_small(w), g_small_packed, _pack_small(mom), _pack_small(var), name="adamw_small")
    d, nm, nv = (_unpack_small(t, small_shapes) for t in (d, nm, nv))
    for n in SMALL:
        grad[n], delta[n], new_m[n], new_v[n] = g_small[n], d[n], nm[n], nv[n]

    return (loss, grad_x, *[grad[n] for n in WEIGHTS], *[delta[n] for n in WEIGHTS],
            *[new_m[n] for n in WEIGHTS], *[new_v[n] for n in WEIGHTS])
```

```python
import math

import jax
import jax.numpy as jnp
from jax import lax
from jax.experimental import pallas as pl
from jax.experimental.pallas import tpu as pltpu

F32, BF16 = jnp.float32, jnp.bfloat16

D_MODEL = 1024
DEPTH = 4
HEAD_DIM = 64
MLA_HEADS, MLA_NOPE, MLA_ROPE, MLA_V = 8, 64, 32, 64
MLA_Q_LORA, MLA_KV_LORA = 384, 256
MLA_QK = MLA_NOPE + MLA_ROPE
ROPE_THETA = 10000.0
SWA_HEADS, SWA_KV_HEADS, SWA_WINDOW = 8, 2, 128
SWA_GROUP = SWA_HEADS // SWA_KV_HEADS
REL_BUCKETS, REL_MAX_DIST = 32, 128
FOX_HEADS = 16
D_FF = 4 * D_MODEL
D_PLE = 256
BLOCK_Q = 128
DN_ALPHA = (2 * DEPTH) ** 0.25
NORM_EPS = 1e-5
NEG_INF = -1e30
EVEN_IN = 1440
ODD_QKV = 3 * FOX_HEADS * HEAD_DIM
LANES = 128

EV_QS = (0, 512)
EV_CQ = (512, 896)
EV_CKV = (896, 1152)
EV_KS = (1152, 1280)
EV_VS = (1280, 1408)
EV_KR = (1408, 1536)
EVEN_IN_PAD = 1536
KR_LANE0 = MLA_NOPE

ADAM_LR, ADAM_B1, ADAM_B2, ADAM_EPS, ADAM_WD, ADAM_STEP = 0.001, 0.9, 0.999, 1e-08, 0.01, 10

N_DEV = 8
VMEM_LIMIT_BYTES = 48 * 1024 * 1024
ATT_TILE = 512
ATT_TILE_BWD = 512
PAIRS_PER_STEP_FWD = 4
PAIRS_PER_STEP_BWD = 2

NN = (((1,), (0,)), ((), ()))
NT = (((1,), (1,)), ((), ()))
TN = (((0,), (0,)), ((), ()))

BIG = ['ev_w_in', 'ev_w_uq', 'ev_w_ukv', 'ev_w_out', 'od_w_in', 'od_w_out', 'w_up', 'w_down',
       'ple_w_proj', 'ple_w_gate']
BIG_AXIS = {'ev_w_in': 2, 'ev_w_uq': 2, 'ev_w_ukv': 2, 'ev_w_out': 1, 'od_w_in': 2, 'od_w_out': 1,
            'w_up': 2, 'w_down': 1, 'ple_w_proj': 2, 'ple_w_gate': 1}
SMALL = ['rel_bias', 'ev_q_norm', 'ev_kv_norm', 'ev_sinks', 'od_b_f', 'ln1_g', 'ln1_b', 'ln2_g', 'ln2_b',
         'ple_b_gate']
WEIGHTS = ['rel_bias', 'ev_w_in', 'ev_q_norm', 'ev_w_uq', 'ev_kv_norm', 'ev_w_ukv', 'ev_sinks', 'ev_w_out',
           'od_w_in', 'od_b_f', 'od_w_out', 'ln1_g', 'ln1_b', 'w_up', 'w_down', 'ln2_g', 'ln2_b',
           'ple_w_proj', 'ple_w_gate', 'ple_b_gate']


def _cparams(*sem):
    return pltpu.CompilerParams(dimension_semantics=sem, vmem_limit_bytes=VMEM_LIMIT_BYTES)


def _pick(n, cands):
    for c in cands:
        if n % c == 0:
            return c
    return n


MM_STEP_BYTES = 10 * 1024 * 1024
MM_OUT_BYTES = 8 * 1024 * 1024
MM_CHUNK = 512


def _mm(a, b, *, trans_b=False, extras=(), epilogue=None, row_epilogue=None, out_dtypes=(F32,), out_widths=None,
        n_sums=0, comm=(), name):
    a_parts = tuple(a) if isinstance(a, (tuple, list)) else (a,)
    n_a = len(a_parts)
    M = a_parts[0].shape[0]
    k_offs = [sum(p.shape[1] for p in a_parts[:i]) for i in range(n_a + 1)]
    N = b.shape[0] if trans_b else b.shape[1]
    n_ex, n_out = len(extras), len(out_dtypes)
    n_rows_out = n_out - n_sums
    out_widths = (N,) * n_out if out_widths is None else out_widths
    row_bytes = sum(p.shape[1] * p.dtype.itemsize for p in a_parts) + (sum(w * jnp.dtype(d).itemsize
                                            for w, d in zip(out_widths[:n_rows_out], out_dtypes))
                                        + sum(e.shape[1] * e.dtype.itemsize for e in extras if e.shape[0] == M)
                                        + (4 * N if row_epilogue is not None else 0))
    tm = next((c for c in (1024, 512, 256) if M % c == 0 and c * row_bytes <= MM_STEP_BYTES), 128)
    nc = _pick(N, (MM_CHUNK, 384, 256, 128))
    n_c, kinds = len(comm), [k for k, _ in comm]
    n_scr = 1 if row_epilogue is not None else 0

    def body(*refs):
        a_refs, refs = refs[:n_a], refs[n_a - 1:]
        c_in = refs[2 + n_ex:2 + n_ex + n_c]
        c_out = refs[2 + n_ex + n_c + n_out:2 + n_ex + 2 * n_c + n_out]
        sems = refs[2 + n_ex + 2 * n_c + n_out + n_scr:]
        refs = refs[:2 + n_ex] + refs[2 + n_ex + n_c:2 + n_ex + n_c + n_out] \
            + refs[2 + n_ex + 2 * n_c + n_out:2 + n_ex + 2 * n_c + n_out + n_scr]
        if n_c:
            place = _mesh_place()
            step = pl.program_id(0)

            @pl.when(step == 0)
            def _():
                _comm_start(kinds, c_in, c_out, sems, place)

        b_ref = refs[1]
        ex = refs[2:2 + n_ex]
        outs = refs[2 + n_ex:2 + n_ex + n_out]
        avs = [r[...].astype(BF16) for r in a_refs]
        for n0 in range(0, N, nc):
            cols = slice(n0, n0 + nc)
            acc = None
            for av, k0, k1 in zip(avs, k_offs[:-1], k_offs[1:]):
                bv = (b_ref[cols, k0:k1] if trans_b else b_ref[k0:k1, cols]).astype(BF16)
                part = lax.dot_general(av, bv, NT if trans_b else NN, preferred_element_type=F32)
                acc = part if acc is None else acc + part
            if row_epilogue is not None:
                refs[-1][:, cols] = acc
                continue
            res = epilogue(acc, *[e[:, cols] for e in ex]) if epilogue is not None else (acc,)
            for o, r in zip(outs, res):
                o[:, cols] = r.astype(o.dtype)
        if row_epilogue is not None:
            res = row_epilogue(refs[-1][...], *[e[...] for e in ex])
            for o, r in zip(outs[:n_rows_out], res):
                o[...] = r.astype(o.dtype)
            if n_sums:
                @pl.when(pl.program_id(0) == 0)
                def _():
                    for o in outs[n_rows_out:]:
                        o[...] = jnp.zeros_like(o)

                for o, r in zip(outs[n_rows_out:], res[n_rows_out:]):
                    o[...] += r
        if n_c:
            @pl.when(step == M // tm - 1)
            def _():
                _comm_wait(kinds, c_in, c_out, sems, place)

    in_specs = [pl.BlockSpec((tm, p.shape[1]), lambda i: (i, 0)) for p in a_parts]
    in_specs.append(pl.BlockSpec(b.shape, lambda i: (0, 0)))
    for e in extras:
        if e.shape[0] == M:
            in_specs.append(pl.BlockSpec((tm, e.shape[1]), lambda i: (i, 0)))
        elif e.shape == (1, N):
            in_specs.append(pl.BlockSpec((1, N), lambda i: (0, 0)))
        else:
            raise ValueError(f"extra operand of shape {e.shape} for a ({M}, {N}) result")
    res = pl.pallas_call(
        body, name=name, grid=(M // tm,), in_specs=in_specs + [HBM_SPEC] * n_c,
        out_specs=[pl.BlockSpec((tm, w), lambda i: (i, 0)) for w in out_widths[:n_rows_out]]
        + [pl.BlockSpec((1, w), lambda i: (0, 0)) for w in out_widths[n_rows_out:]] + [HBM_SPEC] * n_c,
        out_shape=[jax.ShapeDtypeStruct((M, w), d) for w, d in zip(out_widths[:n_rows_out], out_dtypes)]
        + [jax.ShapeDtypeStruct((1, w), d) for w, d in zip(out_widths[n_rows_out:], out_dtypes[n_rows_out:])]
        + _comm_out_shapes(comm),
        scratch_shapes=([pltpu.VMEM((tm, N), F32)] if row_epilogue is not None else [])
        + (_comm_scratch(comm) if n_c else []),
        compiler_params=_cparams("arbitrary" if n_sums or n_c else "parallel"),
    )(*a_parts, b, *extras, *[c for _, c in comm])
    main = res[0] if n_out == 1 else tuple(res[:n_out])
    return (main, list(res[n_out:])) if n_c else main


def _mm_tn(a, b, *, name):
    T, K = a.shape
    N = b.shape[1]
    bk, bn = K, N
    while bk * bn * 4 > MM_OUT_BYTES:
        if bn >= bk and bn % (2 * LANES) == 0:
            bn //= 2
        else:
            bk //= 2
    tt = _pick(T, (1024, 512, 256))
    ck, cn = _pick(bk, (MM_CHUNK, 384, 256, 128)), _pick(bn, (MM_CHUNK, 384, 256, 128))

    def body(a_ref, b_ref, o_ref, acc_ref):
        t = pl.program_id(2)

        @pl.when(t == 0)
        def _():
            acc_ref[...] = jnp.zeros_like(acc_ref)

        for r0 in range(0, bk, ck):
            av = a_ref[:, r0:r0 + ck].astype(BF16)
            for c0 in range(0, bn, cn):
                acc_ref[r0:r0 + ck, c0:c0 + cn] += lax.dot_general(
                    av, b_ref[:, c0:c0 + cn].astype(BF16), TN, preferred_element_type=F32)

        @pl.when(t == T // tt - 1)
        def _():
            o_ref[...] = acc_ref[...].astype(o_ref.dtype)

    return pl.pallas_call(
        body, name=name, grid=(K // bk, N // bn, T // tt),
        in_specs=[pl.BlockSpec((tt, bk), lambda i, j, t: (t, i)), pl.BlockSpec((tt, bn), lambda i, j, t: (t, j))],
        out_specs=pl.BlockSpec((bk, bn), lambda i, j, t: (i, j)),
        out_shape=jax.ShapeDtypeStruct((K, N), BF16), scratch_shapes=[pltpu.VMEM((bk, bn), F32)],
        compiler_params=_cparams("parallel", "parallel", "arbitrary"),
    )(a, b)


ROW_TILE = 256


def _row_spec(cols, col_block=0):
    return pl.BlockSpec((ROW_TILE, cols), lambda i: (i, col_block))


def _tab_spec(cols, period):
    return pl.BlockSpec((ROW_TILE, cols), lambda i: (i % period, 0))


def _full_spec(shape):
    return pl.BlockSpec(shape, lambda i: (0,) * len(shape))


def _mm_ln(a, w, x, g, b, *, comm=(), name):
    def ln_rows(m, xv, gv, bv):
        z = DN_ALPHA * xv + m
        mu = jnp.mean(z, -1, keepdims=True)
        zc = z - mu
        r = lax.rsqrt(jnp.mean(zc * zc, -1, keepdims=True) + NORM_EPS)
        xh = zc * r
        y = xh * gv + bv
        return y, y, xh, jnp.broadcast_to(r, (r.shape[0], LANES))

    D = w.shape[1]
    return _mm(a, w, extras=(x, g, b), row_epilogue=ln_rows, out_dtypes=(F32, BF16, F32, F32),
               out_widths=(D, D, D, LANES), comm=comm, name=name)


def _mm_ln_bwd(a, w, resid, resid_scale, xh, r, g, *, name):
    def ln_bwd_rows(acc, rv, xhv, rstd, gv):
        dyv = acc + resid_scale * rv
        dyg = dyv * gv
        c1 = jnp.mean(dyg, -1, keepdims=True)
        c2 = jnp.mean(dyg * xhv, -1, keepdims=True)
        dz = _widen(rstd, dyv.shape[-1]) * (dyg - c1 - xhv * c2)
        return dz, dz, jnp.sum(dyv * xhv, 0, keepdims=True), jnp.sum(dyv, 0, keepdims=True)

    D = w.shape[0]
    return _mm(a, w, trans_b=True, extras=(resid, xh, r, g), row_epilogue=ln_bwd_rows,
               out_dtypes=(F32, BF16, F32, F32), out_widths=(D, D, D, D), n_sums=2, name=name)


def _loss_grad(y, target, *, name):
    T, D = y.shape

    def body(y_ref, t_ref, dy_ref, sq_ref):
        err = y_ref[...] - t_ref[...]
        dy_ref[...] = err / D

        @pl.when(pl.program_id(0) == 0)
        def _():
            sq_ref[...] = jnp.zeros_like(sq_ref)

        sq_ref[...] += jnp.sum(err * err, 0, keepdims=True)

    return pl.pallas_call(
        body, name=name, grid=(T // ROW_TILE,),
        in_specs=[_row_spec(D), _row_spec(D)],
        out_specs=[_row_spec(D), _full_spec((1, D))],
        out_shape=[jax.ShapeDtypeStruct((T, D), F32), jax.ShapeDtypeStruct((1, D), F32)],
        compiler_params=_cparams("arbitrary"),
    )(y, target)


def _ple_bwd_elem(dx3, g, e, *, name):
    T, D = dx3.shape

    def body(dx_ref, g_ref, e_ref, de_ref, dz_ref, db_ref):
        dx, gv = dx_ref[...], g_ref[...]
        de_ref[...] = (dx * gv).astype(BF16)
        dz = dx * e_ref[...] * gv * (1.0 - gv)
        dz_ref[...] = dz.astype(BF16)

        @pl.when(pl.program_id(0) == 0)
        def _():
            db_ref[...] = jnp.zeros_like(db_ref)

        db_ref[...] += jnp.sum(dz, 0, keepdims=True)

    return pl.pallas_call(
        body, name=name, grid=(T // ROW_TILE,),
        in_specs=[_row_spec(D), _row_spec(D), _row_spec(D)],
        out_specs=[_row_spec(D), _row_spec(D), _full_spec((1, D))],
        out_shape=[jax.ShapeDtypeStruct((T, D), BF16), jax.ShapeDtypeStruct((T, D), BF16),
                   jax.ShapeDtypeStruct((1, D), F32)],
        compiler_params=_cparams("arbitrary"),
    )(dx3, g, e)


def _rotate(xv, a, bm, bp, sign):
    half = MLA_ROPE // 2
    width = xv.shape[-1]
    a, bm, bp = (_widen(t, width) for t in (a, bm, bp))
    return xv * a + sign * (pltpu.roll(xv, width - half, 1) * bm + pltpu.roll(xv, half, 1) * bp)


def _rope(x, tabs, seq, *, sign, name):
    T, width = x.shape

    def body(x_ref, a_ref, bm_ref, bp_ref, o_ref):
        o_ref[...] = _rotate(x_ref[...], a_ref[...], bm_ref[...], bp_ref[...], sign).astype(BF16)

    return pl.pallas_call(
        body, name=name, grid=(T // ROW_TILE,),
        in_specs=[_row_spec(width)] + [_tab_spec(LANES, seq // ROW_TILE)] * 3,
        out_specs=_row_spec(width),
        out_shape=jax.ShapeDtypeStruct((T, width), BF16),
        compiler_params=_cparams("parallel"),
    )(x, *tabs)


def _mla_keys(knp, h, k_tabs, seq, *, name):
    T = knp.shape[0]

    def body(k_ref, h_ref, a_ref, bm_ref, bp_ref, o_ref):
        kr = _rotate(h_ref[...], a_ref[...], bm_ref[...], bp_ref[...], 1.0)
        for hd in range(MLA_HEADS):
            cols = slice(hd * LANES, (hd + 1) * LANES)
            o_ref[:, cols] = (k_ref[:, cols].astype(F32) + kr).astype(BF16)

    return pl.pallas_call(
        body, name=name, grid=(T // ROW_TILE,),
        in_specs=[_row_spec(MLA_HEADS * LANES), _row_spec(LANES, EV_KR[0] // LANES)]
        + [_tab_spec(LANES, seq // ROW_TILE)] * 3,
        out_specs=_row_spec(MLA_HEADS * LANES),
        out_shape=jax.ShapeDtypeStruct((T, MLA_HEADS * LANES), BF16),
        compiler_params=_cparams("parallel"),
    )(knp, h, *k_tabs)


def _mla_rope_key_grad(dk, k_tabs, seq, *, name):
    T = dk.shape[0]

    def body(dk_ref, a_ref, bm_ref, bp_ref, o_ref):
        tot = dk_ref[:, 0:LANES]
        for hd in range(1, MLA_HEADS):
            tot = tot + dk_ref[:, hd * LANES:(hd + 1) * LANES]
        o_ref[...] = _rotate(tot, a_ref[...], bm_ref[...], bp_ref[...], -1.0).astype(BF16)

    return pl.pallas_call(
        body, name=name, grid=(T // ROW_TILE,),
        in_specs=[_row_spec(MLA_HEADS * LANES)] + [_tab_spec(LANES, seq // ROW_TILE)] * 3,
        out_specs=_row_spec(LANES),
        out_shape=jax.ShapeDtypeStruct((T, LANES), BF16),
        compiler_params=_cparams("parallel"),
    )(dk, *k_tabs)


def _even_norms(h, gq, gkv, *, name):
    T = h.shape[0]

    def body(h_ref, gq_ref, gkv_ref, cq_ref, ckv_ref, rq_ref, rkv_ref):
        cq = h_ref[:, EV_CQ[0]:EV_CQ[1]]
        rq = lax.rsqrt(jnp.mean(cq * cq, -1, keepdims=True) + NORM_EPS)
        cq_ref[...] = (cq * rq * gq_ref[...]).astype(BF16)
        rq_ref[...] = jnp.broadcast_to(rq, rq_ref.shape)
        ckv = h_ref[:, EV_CKV[0]:EV_CKV[1]]
        rkv = lax.rsqrt(jnp.mean(ckv * ckv, -1, keepdims=True) + NORM_EPS)
        ckv_ref[...] = (ckv * rkv * gkv_ref[...]).astype(BF16)
        rkv_ref[...] = jnp.broadcast_to(rkv, rkv_ref.shape)

    return pl.pallas_call(
        body, name=name, grid=(T // ROW_TILE,),
        in_specs=[_row_spec(EVEN_IN_PAD), _full_spec((1, MLA_Q_LORA)), _full_spec((1, MLA_KV_LORA))],
        out_specs=[_row_spec(MLA_Q_LORA), _row_spec(MLA_KV_LORA), _row_spec(LANES), _row_spec(LANES)],
        out_shape=[jax.ShapeDtypeStruct((T, MLA_Q_LORA), BF16), jax.ShapeDtypeStruct((T, MLA_KV_LORA), BF16),
                   jax.ShapeDtypeStruct((T, LANES), F32), jax.ShapeDtypeStruct((T, LANES), F32)],
        compiler_params=_cparams("parallel"),
    )(h, gq, gkv)


def _even_in_bwd(h, rq, rkv, gq, gkv, dcqn, dckvn, dqs, dks, dvs, dkr, *, name):
    T = h.shape[0]

    def rms_bwd(c, r, g, dy):
        r = _widen(r, c.shape[-1])
        xr = c * r
        dyg = dy * g
        return r * (dyg - xr * jnp.mean(dyg * xr, -1, keepdims=True)), jnp.sum(dy * xr, 0, keepdims=True)

    def body(h_ref, rq_ref, rkv_ref, gq_ref, gkv_ref, dcq_ref, dckv_ref, dqs_ref, dks_ref, dvs_ref, dkr_ref,
             dh_ref, dgq_ref, dgkv_ref):
        @pl.when(pl.program_id(0) == 0)
        def _():
            dgq_ref[...] = jnp.zeros_like(dgq_ref)
            dgkv_ref[...] = jnp.zeros_like(dgkv_ref)

        dcq, dgq = rms_bwd(h_ref[:, EV_CQ[0]:EV_CQ[1]], rq_ref[...], gq_ref[...], dcq_ref[...])
        dckv, dgkv = rms_bwd(h_ref[:, EV_CKV[0]:EV_CKV[1]], rkv_ref[...], gkv_ref[...], dckv_ref[...])
        dgq_ref[...] += dgq
        dgkv_ref[...] += dgkv
        dh_ref[:, EV_QS[0]:EV_QS[1]] = dqs_ref[...]
        dh_ref[:, EV_CQ[0]:EV_CQ[1]] = dcq.astype(BF16)
        dh_ref[:, EV_CKV[0]:EV_CKV[1]] = dckv.astype(BF16)
        dh_ref[:, EV_KS[0]:EV_KS[1]] = dks_ref[...]
        dh_ref[:, EV_VS[0]:EV_VS[1]] = dvs_ref[...]
        dh_ref[:, EV_KR[0]:EV_KR[1]] = dkr_ref[...]

    return pl.pallas_call(
        body, name=name, grid=(T // ROW_TILE,),
        in_specs=[_row_spec(EVEN_IN_PAD), _row_spec(LANES), _row_spec(LANES), _full_spec((1, MLA_Q_LORA)),
                  _full_spec((1, MLA_KV_LORA)), _row_spec(MLA_Q_LORA), _row_spec(MLA_KV_LORA),
                  _row_spec(SWA_HEADS * HEAD_DIM), _row_spec(LANES), _row_spec(LANES), _row_spec(LANES)],
        out_specs=[_row_spec(EVEN_IN_PAD), _full_spec((1, MLA_Q_LORA)), _full_spec((1, MLA_KV_LORA))],
        out_shape=[jax.ShapeDtypeStruct((T, EVEN_IN_PAD), BF16), jax.ShapeDtypeStruct((1, MLA_Q_LORA), F32),
                   jax.ShapeDtypeStruct((1, MLA_KV_LORA), F32)],
        compiler_params=_cparams("arbitrary"),
    )(h, rq, rkv, gq, gkv, dcqn, dckvn, dqs, dks, dvs, dkr)


def _fox_decay_fwd(f3, bf, *, name):
    B, S, _ = f3.shape

    def body(f_ref, b_ref, csh_ref, chs_ref):
        x = f_ref[...] + b_ref[...]
        c = jnp.minimum(x, 0.0) - jnp.log1p(jnp.exp(-jnp.abs(x)))
        row = lax.broadcasted_iota(jnp.int32, (S, LANES), 0)
        k = 1
        while k < S:
            c = c + jnp.where(row >= k, pltpu.roll(c, k, 0), 0.0)
            k *= 2
        csh_ref[...] = c
        chs_ref[...] = c.T

    return pl.pallas_call(
        body, name=name, grid=(B,),
        in_specs=[pl.BlockSpec((None, S, LANES), lambda b: (b, 0, 0)), pl.BlockSpec((1, LANES), lambda b: (0, 0))],
        out_specs=[pl.BlockSpec((None, S, LANES), lambda b: (b, 0, 0)),
                   pl.BlockSpec((None, LANES, S), lambda b: (b, 0, 0))],
        out_shape=[jax.ShapeDtypeStruct((B, S, LANES), F32), jax.ShapeDtypeStruct((B, LANES, S), F32)],
        compiler_params=_cparams("parallel"),
    )(f3, bf)


def _fox_decay_bwd(dc_hs, f3, bf, *, name):
    B, S, _ = f3.shape

    def body(dc_ref, f_ref, b_ref, df_ref, db_ref):
        g = dc_ref[...].T
        row = lax.broadcasted_iota(jnp.int32, (S, LANES), 0)
        k = 1
        while k < S:
            g = g + jnp.where(row < S - k, pltpu.roll(g, S - k, 0), 0.0)
            k *= 2
        x = f_ref[...] + b_ref[...]
        df = g * (1.0 / (1.0 + jnp.exp(x)))
        df_ref[...] = df.astype(BF16)

        @pl.when(pl.program_id(0) == 0)
        def _():
            db_ref[...] = jnp.zeros_like(db_ref)

        db_ref[...] += jnp.sum(df, 0, keepdims=True)

    return pl.pallas_call(
        body, name=name, grid=(B,),
        in_specs=[pl.BlockSpec((None, LANES, S), lambda b: (b, 0, 0)),
                  pl.BlockSpec((None, S, LANES), lambda b: (b, 0, 0)), pl.BlockSpec((1, LANES), lambda b: (0, 0))],
        out_specs=[pl.BlockSpec((None, S, LANES), lambda b: (b, 0, 0)), pl.BlockSpec((1, LANES), lambda b: (0, 0))],
        out_shape=[jax.ShapeDtypeStruct((B, S, LANES), BF16), jax.ShapeDtypeStruct((1, LANES), F32)],
        compiler_params=_cparams("arbitrary"),
    )(dc_hs, f3, bf)


def _head_column(block, h):
    lane = lax.broadcasted_iota(jnp.int32, block.shape, 1)
    return jnp.sum(jnp.where(lane == h, block, 0.0), axis=-1, keepdims=True)


def _causal_mask(s):
    r = lax.broadcasted_iota(jnp.int32, s.shape, 0)
    c = lax.broadcasted_iota(jnp.int32, s.shape, 1)
    return jnp.where(c <= r, s, NEG_INF)


def _low_half(shape):
    return (lax.broadcasted_iota(jnp.int32, shape, 1) % LANES) < HEAD_DIM


def _widen(x, cols):
    return jnp.concatenate([x] * (cols // LANES), axis=1)


def _both_halves(x, lo):
    r = pltpu.roll(x, HEAD_DIM, 1)
    return jnp.where(lo, x, r), jnp.where(lo, r, x)


MESH_ID = pl.DeviceIdType.MESH
HBM_SPEC = pl.BlockSpec(memory_space=pltpu.HBM)
VMEM_SPEC = pl.BlockSpec(memory_space=pltpu.VMEM)


def _mesh_place():
    x, y, c = lax.axis_index("x"), lax.axis_index("y"), lax.axis_index("c")
    return x, y, c, 4 * x + 2 * y + c


def _peers(x, y, c):
    out = []
    for mask in range(1, N_DEV):
        dx, dy, dc = (mask >> 2) & 1, (mask >> 1) & 1, mask & 1
        px, py, pc = (1 - x if dx else x), (1 - y if dy else y), (1 - c if dc else c)
        out.append(((px, py, pc), 4 * px + 2 * py + pc))
    return out


def _comm_out_shapes(comm):
    return [jax.ShapeDtypeStruct(a.shape if kind == "scatter" else (N_DEV,) + a.shape[1:], a.dtype) for kind, a in comm]


def _comm_scratch(comm):
    n = len(comm)
    return [pltpu.SemaphoreType.DMA((n, 7)), pltpu.SemaphoreType.DMA((n, 7)), pltpu.SemaphoreType.DMA((n,))]


def _comm_copies(kinds, in_refs, out_refs, sems, place):
    send_sems, recv_sems, local_sems = sems
    x, y, c, me = place
    local, remote = [], []
    for w, kind in enumerate(kinds):
        mine = in_refs[w].at[me] if kind == "scatter" else in_refs[w].at[kind[1]]
        local.append(pltpu.make_async_copy(mine, out_refs[w].at[me], local_sems.at[w]))
        for k, (peer, peer_idx) in enumerate(_peers(x, y, c)):
            remote.append(pltpu.make_async_remote_copy(
                src_ref=in_refs[w].at[peer_idx] if kind == "scatter" else mine, dst_ref=out_refs[w].at[me],
                send_sem=send_sems.at[w, k], recv_sem=recv_sems.at[w, k], device_id=peer, device_id_type=MESH_ID))
    return local, remote


def _comm_start(kinds, in_refs, out_refs, sems, place):
    local, remote = _comm_copies(kinds, in_refs, out_refs, sems, place)
    for cp in local + remote:
        cp.start()


def _comm_wait(kinds, in_refs, out_refs, sems, place):
    local, remote = _comm_copies(kinds, in_refs, out_refs, sems, place)
    for cp in remote:
        cp.wait_recv()
    for cp in remote:
        cp.wait_send()
    for cp in local:
        cp.wait()


def _exchange(comm, *, name):
    n = len(comm)
    kinds = [k for k, _ in comm]

    def body(*refs):
        place = _mesh_place()
        _comm_start(kinds, refs[:n], refs[n:2 * n], refs[2 * n:], place)
        _comm_wait(kinds, refs[:n], refs[n:2 * n], refs[2 * n:], place)

    return pl.pallas_call(
        body, name=name, out_shape=_comm_out_shapes(comm), in_specs=[HBM_SPEC] * n, out_specs=[HBM_SPEC] * n,
        scratch_shapes=_comm_scratch(comm),
    )(*[a for _, a in comm])


def _flash_fwd(qa, ka, va, *, q_blk0, k_blk0, v_blk0, W, n_pairs, B, S, scale, csh=None, crow=None, comm=(), name):
    t = ATT_TILE
    nq = S // t
    P = PAIRS_PER_STEP_FWD
    decay = csh is not None
    split = W == LANES
    assert n_pairs % P == 0 and q_blk0 % P == 0 and k_blk0 % P == 0 and v_blk0 % P == 0
    n_c, kinds = len(comm), [k for k, _ in comm]
    n_in = 5 if decay else 3
    fold_scale = math.log2(scale).is_integer()
    n_steps = (B, n_pairs // P, nq)

    def body(*refs):
        c_in, c_out = refs[n_in:n_in + n_c], refs[n_in + n_c + 2:n_in + 2 * n_c + 2]
        sems = refs[n_in + 2 * n_c + 4:]
        refs = refs[:n_in] + refs[n_in + n_c:n_in + n_c + 2] + refs[n_in + 2 * n_c + 2:n_in + 2 * n_c + 4]
        if decay:
            q_ref, k_ref, v_ref, csh_ref, crow_ref, o_ref, lse_ref, m_s, acc_s = refs
        else:
            q_ref, k_ref, v_ref, o_ref, lse_ref, m_s, acc_s = refs
        g, i = pl.program_id(1), pl.program_id(2)
        if n_c:
            place = _mesh_place()
            ids = [pl.program_id(ax) for ax in range(3)]

            @pl.when((ids[0] == 0) & (ids[1] == 0) & (ids[2] == 0))
            def _():
                _comm_start(kinds, c_in, c_out, sems, place)

        lo = _low_half((t, LANES))
        qv = q_ref[...]
        qh = []
        for pr in range(P):
            qp = qv[:, pr * W:(pr + 1) * W]
            qh += [jnp.where(lo, qp, jnp.zeros_like(qp)), jnp.where(lo, jnp.zeros_like(qp), qp)] if split \
                else [qp[:, :LANES], qp[:, LANES:]]
        if fold_scale:
            qh = [x * scale for x in qh]
        if decay:
            cq = [jnp.broadcast_to(_head_column(csh_ref[...], 2 * P * g + hd), (t, LANES)) for hd in range(2 * P)]
        m_s[...] = jnp.full(m_s.shape, NEG_INF, F32)
        acc_s[...] = jnp.zeros(acc_s.shape, F32)

        def step(j, masked):
            rows = pl.ds(pl.multiple_of(j * t, t), t)
            kb, vb = k_ref[rows, :], v_ref[rows, :]
            for pr in range(P):
                kp, vp = kb[:, pr * W:(pr + 1) * W], vb[:, pr * LANES:(pr + 1) * LANES]
                ones = jnp.ones_like(vp)
                vaug = [jnp.where(lo, vp, ones), jnp.where(lo, ones, vp)]
                for half in range(2):
                    hd = 2 * pr + half
                    kh = kp if split else kp[:, half * LANES:(half + 1) * LANES]
                    s = lax.dot_general(qh[hd], kh, NT, preferred_element_type=F32)
                    if not fold_scale:
                        s = s * scale
                    if decay:
                        s = s + _widen(cq[hd], t) - crow_ref[hd, j]
                    if masked:
                        s = _causal_mask(s)
                    m_prev = m_s[hd]
                    m_new = jnp.maximum(m_prev, jnp.max(s, -1, keepdims=True))
                    p = jnp.exp(s - _widen(m_new, t))
                    acc_s[hd] = jnp.exp(m_prev - m_new) * acc_s[hd] + lax.dot_general(
                        p.astype(BF16), vaug[half], NN, preferred_element_type=F32)
                    m_s[hd] = m_new

        def loop_body(j, carry):
            step(j, False)
            return carry

        lax.fori_loop(0, i, loop_body, 0)
        step(i, True)
        for pr in range(P):
            acc0, acc1 = acc_s[2 * pr], acc_s[2 * pr + 1]
            _, l0 = _both_halves(acc0, lo)
            l1, _ = _both_halves(acc1, lo)
            cols = slice(pr * LANES, (pr + 1) * LANES)
            o_ref[:, cols] = jnp.where(lo, acc0 / l0, acc1 / l1).astype(BF16)
            lse_ref[:, cols] = jnp.where(lo, m_s[2 * pr] + jnp.log(l0), m_s[2 * pr + 1] + jnp.log(l1))
        if n_c:
            @pl.when((ids[0] == n_steps[0] - 1) & (ids[1] == n_steps[1] - 1) & (ids[2] == n_steps[2] - 1))
            def _():
                _comm_wait(kinds, c_in, c_out, sems, place)

    in_specs = [pl.BlockSpec((t, P * W), lambda b, g, i: (b * nq + i, q_blk0 // P + g)),
                pl.BlockSpec((S, P * W), lambda b, g, i: (b, k_blk0 // P + g)),
                pl.BlockSpec((S, P * LANES), lambda b, g, i: (b, v_blk0 // P + g))]
    args = [qa, ka, va]
    if decay:
        in_specs += [pl.BlockSpec((None, t, LANES), lambda b, g, i: (b, i, 0)),
                     pl.BlockSpec((None, 2 * P, nq, 1, t), lambda b, g, i: (b, g, 0, 0, 0))]
        args += [csh, crow]
    out_spec = pl.BlockSpec((t, P * LANES), lambda b, g, i: (b * nq + i, g))
    res = pl.pallas_call(
        body, name=name, grid=n_steps, in_specs=in_specs + [HBM_SPEC] * n_c,
        out_specs=[out_spec, out_spec] + [HBM_SPEC] * n_c,
        out_shape=[jax.ShapeDtypeStruct((B * S, n_pairs * LANES), BF16),
                   jax.ShapeDtypeStruct((B * S, n_pairs * LANES), F32)] + _comm_out_shapes(comm),
        scratch_shapes=[pltpu.VMEM((2 * P, t, LANES), F32), pltpu.VMEM((2 * P, t, LANES), F32)]
        + (_comm_scratch(comm) if n_c else []),
        compiler_params=_cparams(*(("arbitrary",) * 3 if n_c else ("parallel",) * 3)),
    )(*args, *[a for _, a in comm])
    return res[0], res[1], list(res[2:])


def _flash_bwd(qa, ka, va, oa, doa, lsea, *, q_blk0, k_blk0, v_blk0, do_blk0, W, n_pairs, B, S, scale, qk_dtype,
               csh=None, crow=None, comm=(), name):
    t = ATT_TILE_BWD
    nq = S // t
    P = PAIRS_PER_STEP_BWD
    decay = csh is not None
    if decay:
        crow = crow.reshape(B, 2 * n_pairs, nq, 1, t)
    split = W == LANES
    assert n_pairs % P == 0 and q_blk0 % P == 0 and k_blk0 % P == 0 and v_blk0 % P == 0 and do_blk0 % P == 0
    n_c, kinds = len(comm), [k for k, _ in comm]
    n_in, n_out, n_scr = (8, 5, 8) if decay else (6, 3, 5)
    n_steps = (B, n_pairs // P, nq)

    def body(*refs):
        c_in = refs[n_in:n_in + n_c]
        c_out = refs[n_in + n_c + n_out:n_in + 2 * n_c + n_out]
        sems = refs[n_in + 2 * n_c + n_out + n_scr:]
        refs = (refs[:n_in] + refs[n_in + n_c:n_in + n_c + n_out]
                + refs[n_in + 2 * n_c + n_out:n_in + 2 * n_c + n_out + n_scr])
        if n_c:
            place = _mesh_place()
            ids = [pl.program_id(ax) for ax in range(3)]

            @pl.when((ids[0] == 0) & (ids[1] == 0) & (ids[2] == 0))
            def _():
                _comm_start(kinds, c_in, c_out, sems, place)

        if decay:
            (q_ref, k_ref, v_ref, o_ref, do_ref, lse_ref, csh_ref, crow_ref, dq_ref, dk_ref, dv_ref, dck_ref, dcq_ref,
             dq_s, lse_s, delta_s, dk_s, dv_s, cq_s, dcq_s, dck_s) = refs
        else:
            (q_ref, k_ref, v_ref, o_ref, do_ref, lse_ref, dq_ref, dk_ref, dv_ref,
             dq_s, lse_s, delta_s, dk_s, dv_s) = refs
        g, j = pl.program_id(1), pl.program_id(2)
        lo = _low_half((t, LANES))

        @pl.when(j == 0)
        def _():
            lo_s = _low_half((S, LANES))
            dq_s[...] = jnp.zeros(dq_s.shape, F32)
            for pr in range(P):
                cols = slice(pr * LANES, (pr + 1) * LANES)
                lse_s[2 * pr], lse_s[2 * pr + 1] = _both_halves(lse_ref[:, cols], lo_s)
                dd = do_ref[:, cols].astype(F32) * o_ref[:, cols].astype(F32)
                delta_s[2 * pr] = jnp.broadcast_to(jnp.sum(jnp.where(lo_s, dd, 0.0), -1, keepdims=True), (S, LANES))
                delta_s[2 * pr + 1] = jnp.broadcast_to(jnp.sum(jnp.where(lo_s, 0.0, dd), -1, keepdims=True),
                                                       (S, LANES))
            if decay:
                for hd in range(2 * P):
                    cq_s[hd] = jnp.broadcast_to(_head_column(csh_ref[...], 2 * P * g + hd), (S, LANES))
                dcq_s[...] = jnp.zeros(dcq_s.shape, F32)

        kb, vb = k_ref[...], v_ref[...]
        kh, vh = [], []
        for pr in range(P):
            kp, vp = kb[:, pr * W:(pr + 1) * W], vb[:, pr * LANES:(pr + 1) * LANES]
            zk, zv = jnp.zeros_like(kp), jnp.zeros_like(vp)
            kh += [jnp.where(lo, kp, zk), jnp.where(lo, zk, kp)] if split else [kp[:, :LANES], kp[:, LANES:]]
            vh += [jnp.where(lo, vp, zv), jnp.where(lo, zv, vp)]
        dk_s[...] = jnp.zeros(dk_s.shape, F32)
        dv_s[...] = jnp.zeros(dv_s.shape, F32)
        if decay:
            dck_s[...] = jnp.zeros(dck_s.shape, F32)

        def step(i, masked):
            rows = pl.ds(pl.multiple_of(i * t, t), t)
            qi, doi = q_ref[rows, :], do_ref[rows, :]
            for pr in range(P):
                qp, dop = qi[:, pr * W:(pr + 1) * W], doi[:, pr * LANES:(pr + 1) * LANES]
                for half in range(2):
                    hd = 2 * pr + half
                    qx = qp if split else qp[:, half * LANES:(half + 1) * LANES]
                    s = lax.dot_general(qx, kh[hd], NT, preferred_element_type=F32) * scale
                    if decay:
                        s = s + _widen(cq_s[hd, rows, :], t) - crow_ref[hd, j]
                    if masked:
                        s = _causal_mask(s)
                    p = jnp.exp(s - _widen(lse_s[hd, rows, :], t))
                    dv_s[hd] += lax.dot_general(p.astype(BF16), dop, TN, preferred_element_type=F32)
                    dp = lax.dot_general(dop, vh[hd], NT, preferred_element_type=F32)
                    ds = p * (dp - _widen(delta_s[hd, rows, :], t))
                    dss = (ds * scale).astype(BF16)
                    dk_s[hd] += lax.dot_general(dss, qx, TN, preferred_element_type=F32)
                    dqc = lax.dot_general(dss, kh[hd], NN, preferred_element_type=F32)
                    if split:
                        dq_s[rows, pr * W:(pr + 1) * W] += dqc
                    else:
                        dq_s[rows, hd * LANES:(hd + 1) * LANES] += dqc
                    if decay:
                        dck_s[hd] -= jnp.sum(ds, 0, keepdims=True)
                        part = ds[:, :LANES]
                        for c in range(1, t // LANES):
                            part = part + ds[:, c * LANES:(c + 1) * LANES]
                        dcq_s[hd, rows, :] += part

        def loop_body(i, carry):
            step(i, False)
            return carry

        step(j, True)
        lax.fori_loop(j + 1, nq, loop_body, 0)
        for pr in range(P):
            if split:
                dk_ref[:, pr * W:(pr + 1) * W] = jnp.where(lo, dk_s[2 * pr], dk_s[2 * pr + 1]).astype(dk_ref.dtype)
            else:
                for half in range(2):
                    hd = 2 * pr + half
                    dk_ref[:, hd * LANES:(hd + 1) * LANES] = dk_s[hd].astype(dk_ref.dtype)
            dv_ref[:, pr * LANES:(pr + 1) * LANES] = jnp.where(lo, dv_s[2 * pr], dv_s[2 * pr + 1]).astype(BF16)
        if decay:
            dck_ref[...] = dck_s[...]

        @pl.when(j == nq - 1)
        def _():
            dq_ref[...] = dq_s[...].astype(dq_ref.dtype)
            if decay:
                for hd in range(2 * P):
                    dcq_ref[hd] = jnp.sum(dcq_s[hd].T, 0, keepdims=True)

        if n_c:
            @pl.when((ids[0] == n_steps[0] - 1) & (ids[1] == n_steps[1] - 1) & (ids[2] == n_steps[2] - 1))
            def _():
                _comm_wait(kinds, c_in, c_out, sems, place)

    full = lambda w, blk0: pl.BlockSpec((S, P * w), lambda b, g, j: (b, blk0 // P + g))
    blk = lambda w, blk0: pl.BlockSpec((t, P * w), lambda b, g, j: (b * nq + j, blk0 // P + g))
    in_specs = [full(W, q_blk0), blk(W, k_blk0), blk(LANES, v_blk0), full(LANES, 0), full(LANES, do_blk0),
                full(LANES, 0)]
    args = [qa, ka, va, oa, doa, lsea]
    T = B * S
    out_specs = [full(W, 0), blk(W, 0), blk(LANES, 0)]
    out_shape = [jax.ShapeDtypeStruct((T, n_pairs * W), qk_dtype), jax.ShapeDtypeStruct((T, n_pairs * W), qk_dtype),
                 jax.ShapeDtypeStruct((T, n_pairs * LANES), BF16)]
    per_head = lambda rows: pltpu.VMEM((2 * P, rows, LANES), F32)
    scratch = [pltpu.VMEM((S, P * W), F32), per_head(S), per_head(S), per_head(t), per_head(t)]
    if decay:
        in_specs += [pl.BlockSpec((None, S, LANES), lambda b, g, j: (b, 0, 0)),
                     pl.BlockSpec((None, 2 * P, nq, 1, t), lambda b, g, j: (b, g, 0, 0, 0))]
        args += [csh, crow]
        out_specs += [pl.BlockSpec((None, 2 * P, None, 1, t), lambda b, g, j: (b, g, j, 0, 0)),
                      pl.BlockSpec((None, 2 * P, 1, S), lambda b, g, j: (b, g, 0, 0))]
        out_shape += [jax.ShapeDtypeStruct((B, 2 * n_pairs, nq, 1, t), F32),
                      jax.ShapeDtypeStruct((B, 2 * n_pairs, 1, S), F32)]
        scratch += [per_head(S), per_head(S), pltpu.VMEM((2 * P, 1, t), F32)]
    res = pl.pallas_call(
        body, name=name, grid=n_steps, in_specs=in_specs + [HBM_SPEC] * n_c,
        out_specs=out_specs + [HBM_SPEC] * n_c, out_shape=out_shape + _comm_out_shapes(comm),
        scratch_shapes=scratch + (_comm_scratch(comm) if n_c else []),
        compiler_params=_cparams(*(("arbitrary",) * 3 if n_c else ("parallel", "parallel", "arbitrary"))),
    )(*args, *[a for _, a in comm])
    return tuple(res[:n_out]) + (list(res[n_out:]),)


def _swa_common(q_ref, kp_ref, ko_ref, vp_ref, vo_ref, n):
    Q = BLOCK_Q
    lo = _low_half((Q, LANES))
    lo2 = _low_half((2 * Q, LANES))
    kk = jnp.concatenate([kp_ref[...], ko_ref[...]], axis=0)
    vv = jnp.concatenate([vp_ref[...], vo_ref[...]], axis=0)
    kdup = [x.astype(BF16) for x in _both_halves(kk, lo2)]
    vdup = [x.astype(BF16) for x in _both_halves(vv, lo2)]
    a = lax.broadcasted_iota(jnp.int32, (SWA_GROUP * Q, 2 * Q), 0) % Q
    col = lax.broadcasted_iota(jnp.int32, (SWA_GROUP * Q, 2 * Q), 1)
    dist = a + Q - col
    valid = (dist >= 0) & (dist < SWA_WINDOW) & ((col >= Q) | (n > 0))
    qv = q_ref[...]
    qm = []
    for a_head in range(SWA_HEADS):
        qp = qv[:, (a_head // 2) * LANES:(a_head // 2 + 1) * LANES]
        keep = lo if a_head % 2 == 0 else jnp.logical_not(lo)
        qm.append(jnp.where(keep, qp, 0.0).astype(BF16))
    qs = [jnp.concatenate(qm[g * SWA_GROUP:(g + 1) * SWA_GROUP], axis=0) for g in range(SWA_KV_HEADS)]
    return lo, lo2, kdup, vdup, valid, qs


def _swa_group_logits(g, qs, kdup, valid, bias_ref):
    heads = slice(g * SWA_GROUP, (g + 1) * SWA_GROUP)
    s = lax.dot_general(qs[g], kdup[g], NT, preferred_element_type=F32) * (HEAD_DIM ** -0.5)
    s = s + bias_ref[heads].reshape(SWA_GROUP * BLOCK_Q, 2 * BLOCK_Q)
    return heads, jnp.where(valid, s, NEG_INF)


def _pair_halves(x, lo):
    Q = BLOCK_Q
    return [jnp.where(lo, x[2 * pr * Q:(2 * pr + 1) * Q], x[(2 * pr + 1) * Q:(2 * pr + 2) * Q])
            for pr in range(SWA_GROUP // 2)]


def _swa_in_specs(nb):
    Q = BLOCK_Q
    own = lambda blk: (lambda b, n: (b * nb + n, blk))
    prev = lambda blk: (lambda b, n: (b * nb + jnp.maximum(n - 1, 0), blk))
    kb, vb = EV_KS[0] // LANES, EV_VS[0] // LANES
    return [pl.BlockSpec((Q, SWA_HEADS * HEAD_DIM), own(0)), pl.BlockSpec((Q, LANES), prev(kb)),
            pl.BlockSpec((Q, LANES), own(kb)), pl.BlockSpec((Q, LANES), prev(vb)), pl.BlockSpec((Q, LANES), own(vb))]


def _swa_fwd(h, bias, sinkcol, *, B, S, comm=(), name):
    Q = BLOCK_Q
    nb = S // Q
    n_c, kinds = len(comm), [k for k, _ in comm]

    def body(*refs):
        c_in, c_out, sems = refs[7:7 + n_c], refs[9 + n_c:9 + 2 * n_c], refs[9 + 2 * n_c:]
        q_ref, kp_ref, ko_ref, vp_ref, vo_ref, bias_ref, sink_ref = refs[:7]
        o_ref, lse_ref = refs[7 + n_c:9 + n_c]
        if n_c:
            place = _mesh_place()
            ids = [pl.program_id(0), pl.program_id(1)]

            @pl.when((ids[0] == 0) & (ids[1] == 0))
            def _():
                _comm_start(kinds, c_in, c_out, sems, place)

        lo, lo2, kdup, vdup, valid, qs = _swa_common(q_ref, kp_ref, ko_ref, vp_ref, vo_ref, pl.program_id(1))
        pairs = []
        lo4 = _low_half((SWA_GROUP * Q, LANES))
        for g in range(SWA_KV_HEADS):
            heads, s = _swa_group_logits(g, qs, kdup, valid, bias_ref)
            sink = jnp.broadcast_to(sink_ref[heads].reshape(SWA_GROUP * Q, 1), (SWA_GROUP * Q, LANES))
            m = jnp.maximum(jnp.max(s, -1, keepdims=True), sink)
            p = jnp.exp(s - _widen(m, 2 * Q))
            vaug = jnp.where(lo2, vdup[g], jnp.ones_like(vdup[g]))
            pv = lax.dot_general(p.astype(BF16), vaug, NN, preferred_element_type=F32)
            rolled = pltpu.roll(pv, HEAD_DIM, 1)
            l = jnp.where(lo4, rolled, pv) + jnp.exp(sink - m)
            out = pv / l
            lse_g = m + jnp.log(l)
            for i in range(SWA_GROUP):
                a = g * SWA_GROUP + i
                lse_ref[:, a * LANES:(a + 1) * LANES] = lse_g[i * Q:(i + 1) * Q]
            shifted = pltpu.roll(out, HEAD_DIM, 1)
            pairs += [jnp.where(lo, out[2 * pr * Q:(2 * pr + 1) * Q], shifted[(2 * pr + 1) * Q:(2 * pr + 2) * Q])
                      for pr in range(SWA_GROUP // 2)]
        o_ref[...] = jnp.concatenate(pairs, axis=1).astype(BF16)
        if n_c:
            @pl.when((ids[0] == B - 1) & (ids[1] == nb - 1))
            def _():
                _comm_wait(kinds, c_in, c_out, sems, place)

    whole = lambda shape: pl.BlockSpec(shape, lambda b, n: (0,) * len(shape))
    res = pl.pallas_call(
        body, name=name, grid=(B, nb),
        in_specs=_swa_in_specs(nb) + [whole((SWA_HEADS, Q, 2 * Q)), whole((SWA_HEADS, Q, 1))] + [HBM_SPEC] * n_c,
        out_specs=[pl.BlockSpec((Q, SWA_HEADS * HEAD_DIM), lambda b, n: (b * nb + n, 0)),
                   pl.BlockSpec((Q, SWA_HEADS * LANES), lambda b, n: (b * nb + n, 0))] + [HBM_SPEC] * n_c,
        out_shape=[jax.ShapeDtypeStruct((B * S, SWA_HEADS * HEAD_DIM), BF16),
                   jax.ShapeDtypeStruct((B * S, SWA_HEADS * LANES), F32)] + _comm_out_shapes(comm),
        scratch_shapes=_comm_scratch(comm) if n_c else [],
        compiler_params=_cparams(*(("arbitrary",) * 2 if n_c else ("parallel",) * 2)),
    )(h, h, h, h, h, bias, sinkcol, *[a for _, a in comm])
    return res[0], res[1], list(res[2:])


def _swa_bwd(h, o, do, lse, bias, sinkcol, *, do_blk0, B, S, comm=(), name):
    Q = BLOCK_Q
    nb = S // Q
    scale = HEAD_DIM ** -0.5
    n_c, kinds = len(comm), [k for k, _ in comm]

    def body(*refs):
        c_in, c_out, sems = refs[10:10 + n_c], refs[17 + n_c:17 + 2 * n_c], refs[17 + 2 * n_c:]
        q_ref, kp_ref, ko_ref, vp_ref, vo_ref, o_ref, do_ref, lse_ref, bias_ref, sink_ref = refs[:10]
        dq_ref, dko_ref, dkp_ref, dvo_ref, dvp_ref, dbias_ref, dsink_ref = refs[10 + n_c:17 + n_c]
        ids = [pl.program_id(0), pl.program_id(1)]
        if n_c:
            place = _mesh_place()

        @pl.when((ids[0] == 0) & (ids[1] == 0))
        def _():
            dbias_ref[...] = jnp.zeros_like(dbias_ref)
            dsink_ref[...] = jnp.zeros_like(dsink_ref)
            if n_c:
                _comm_start(kinds, c_in, c_out, sems, place)

        lo, lo2, kdup, vdup, valid, qs = _swa_common(q_ref, kp_ref, ko_ref, vp_ref, vo_ref, pl.program_id(1))
        dkk, dvv, dq_pairs = [], [], []
        for g in range(SWA_KV_HEADS):
            heads, s = _swa_group_logits(g, qs, kdup, valid, bias_ref)
            lse_g = jnp.concatenate([lse_ref[:, a * LANES:(a + 1) * LANES]
                                     for a in range(g * SWA_GROUP, (g + 1) * SWA_GROUP)], axis=0)
            p = jnp.exp(s - _widen(lse_g, 2 * Q))
            do_g, o_g = [], []
            for i in range(SWA_GROUP):
                cols = slice((g * SWA_GROUP + i) // 2 * LANES, ((g * SWA_GROUP + i) // 2 + 1) * LANES)
                do_p = do_ref[:, cols]
                do_g.append(jnp.where(lo if i % 2 == 0 else jnp.logical_not(lo), do_p, jnp.zeros_like(do_p)))
                o_g.append(o_ref[:, cols])
            doh, oh = jnp.concatenate(do_g, axis=0), jnp.concatenate(o_g, axis=0)
            delta = jnp.sum(doh.astype(F32) * oh.astype(F32), -1, keepdims=True)
            dp = lax.dot_general(doh, vdup[g], NT, preferred_element_type=F32)
            ds = p * (dp - delta)
            dbias_ref[heads] += ds.reshape(SWA_GROUP, Q, 2 * Q)
            dsink_ref[heads] -= (jnp.exp(sink_ref[heads].reshape(SWA_GROUP * Q, 1) - lse_g[:, :1])
                                 * delta).reshape(SWA_GROUP, Q, 1)
            dss = (ds * scale).astype(BF16)
            dq_pairs += _pair_halves(lax.dot_general(dss, kdup[g], NN, preferred_element_type=F32), lo)
            dkk.append(lax.dot_general(dss, qs[g], TN, preferred_element_type=F32))
            dvv.append(lax.dot_general(p.astype(BF16), doh, TN, preferred_element_type=F32))
        dq_ref[...] = jnp.concatenate(dq_pairs, axis=1).astype(BF16)
        fold = lambda x: x + pltpu.roll(x, HEAD_DIM, 1)
        dk_blk = jnp.where(lo2, fold(dkk[0]), fold(dkk[1]))
        dv_blk = jnp.where(lo2, fold(dvv[0]), fold(dvv[1]))
        dkp_ref[...] = dk_blk[:Q]
        dko_ref[...] = dk_blk[Q:]
        dvp_ref[...] = dv_blk[:Q]
        dvo_ref[...] = dv_blk[Q:]
        if n_c:
            @pl.when((ids[0] == B - 1) & (ids[1] == nb - 1))
            def _():
                _comm_wait(kinds, c_in, c_out, sems, place)

    whole = lambda shape: pl.BlockSpec(shape, lambda b, n: (0,) * len(shape))
    wide = lambda blk: pl.BlockSpec((Q, SWA_HEADS * HEAD_DIM), lambda b, n: (b * nb + n, blk))
    narrow = pl.BlockSpec((Q, LANES), lambda b, n: (b * nb + n, 0))
    kv_shape = jax.ShapeDtypeStruct((B * S, LANES), F32)
    res = pl.pallas_call(
        body, name=name, grid=(B, nb),
        in_specs=_swa_in_specs(nb) + [wide(0), wide(do_blk0),
                                      pl.BlockSpec((Q, SWA_HEADS * LANES), lambda b, n: (b * nb + n, 0)),
                                      whole((SWA_HEADS, Q, 2 * Q)),
                                      whole((SWA_HEADS, Q, 1))] + [HBM_SPEC] * n_c,
        out_specs=[wide(0), narrow, narrow, narrow, narrow, whole((SWA_HEADS, Q, 2 * Q)), whole((SWA_HEADS, Q, 1))]
        + [HBM_SPEC] * n_c,
        out_shape=[jax.ShapeDtypeStruct((B * S, SWA_HEADS * HEAD_DIM), BF16), kv_shape, kv_shape, kv_shape, kv_shape,
                   jax.ShapeDtypeStruct((SWA_HEADS, Q, 2 * Q), F32), jax.ShapeDtypeStruct((SWA_HEADS, Q, 1), F32)]
        + _comm_out_shapes(comm),
        scratch_shapes=_comm_scratch(comm) if n_c else [],
        compiler_params=_cparams("arbitrary", "arbitrary"),
    )(h, h, h, h, h, o, do, lse, bias, sinkcol, *[a for _, a in comm])
    return tuple(res[:7]) + (list(res[7:]),)


def _bias_bucket_sum(dbias, bucket, *, name):
    def body(d_ref, b_ref, o_ref):
        dbv, bk = d_ref[...], b_ref[...]
        lane = lax.broadcasted_iota(jnp.int32, (SWA_HEADS, LANES), 1)
        out = jnp.zeros((SWA_HEADS, LANES), F32)
        for b in range(REL_BUCKETS):
            part = jnp.sum(jnp.where(bk == b, dbv, 0.0), axis=1)
            tot = jnp.sum(part, axis=-1, keepdims=True)
            out = out + jnp.where(lane == b, tot, 0.0)
        o_ref[...] = out

    return pl.pallas_call(
        body, name=name, out_shape=jax.ShapeDtypeStruct((SWA_HEADS, LANES), F32),
        compiler_params=pltpu.CompilerParams(vmem_limit_bytes=VMEM_LIMIT_BYTES),
    )(dbias, bucket)


def _adamw_update(w, g, m, v):
    m_new = ADAM_B1 * m + (1.0 - ADAM_B1) * g
    v_new = ADAM_B2 * v + (1.0 - ADAM_B2) * jnp.square(g)
    m_hat = m_new / (1.0 - ADAM_B1 ** ADAM_STEP)
    v_hat = v_new / (1.0 - ADAM_B2 ** ADAM_STEP)
    return -ADAM_LR * (m_hat / (jnp.sqrt(v_hat) + ADAM_EPS) + ADAM_WD * w), m_new, v_new


def _adamw(w, g, m, v, *, name):
    def body(w_ref, g_ref, m_ref, v_ref, d_ref, nm_ref, nv_ref):
        d_ref[...], nm_ref[...], nv_ref[...] = _adamw_update(w_ref[...], g_ref[...], m_ref[...], v_ref[...])

    return pl.pallas_call(
        body, name=name, out_shape=[jax.ShapeDtypeStruct(w.shape, F32)] * 3,
        compiler_params=pltpu.CompilerParams(vmem_limit_bytes=VMEM_LIMIT_BYTES),
    )(w, g, m, v)


ADAMW_PARTS_BYTES = 8 * 1024 * 1024


def _adamw_slots(w, parts, m, v, *, name):
    n0, R, C = w.shape
    tr = next((c for c in (512, 256, 128, 64, 32, 16, 8) if R % c == 0 and 4 * n0 * N_DEV * c * C <= ADAMW_PARTS_BYTES), R)

    def body(*refs):
        w_ref, p_refs, (m_ref, v_ref, g_ref, d_ref, nm_ref, nv_ref) = refs[0], refs[1:1 + n0], refs[1 + n0:]
        layer = pl.program_id(0)
        for l in range(n0):
            @pl.when(layer == l)
            def _(p_ref=p_refs[l]):
                g = p_ref[0].astype(F32)
                for j in range(1, N_DEV):
                    g = g + p_ref[j].astype(F32)
                g_ref[...] = g
                d_ref[...], nm_ref[...], nv_ref[...] = _adamw_update(w_ref[...], g, m_ref[...], v_ref[...])

    spec = pl.BlockSpec((None, tr, C), lambda l, i: (l, i, 0))
    part_spec = lambda own: pl.BlockSpec((N_DEV, tr, C), lambda l, i: (0, jnp.where(l == own, i, 0), 0))
    return pl.pallas_call(
        body, name=name, grid=(n0, R // tr),
        in_specs=[spec] + [part_spec(l) for l in range(n0)] + [spec, spec], out_specs=[spec] * 4,
        out_shape=[jax.ShapeDtypeStruct((n0, R, C), F32)] * 4, compiler_params=_cparams("arbitrary", "arbitrary"),
    )(w, *parts, m, v)


def _all_gather_hbm(blocks, *, name):
    n = len(blocks)

    def body(*refs):
        x_refs, out_refs = refs[:n], refs[n:2 * n]
        send_sems, recv_sems, local_sems = refs[2 * n:]
        x, y, c, _ = _mesh_place()
        me, sibling = (x, y, c), (x, y, 1 - c)
        chips = [(1 - x, y), (x, 1 - y), (1 - x, 1 - y)]

        def copy(w, k, blk, to, src=None):
            px, py, pc = blk
            slot = out_refs[w].at[4 * px + 2 * py + pc]
            return pltpu.make_async_remote_copy(
                src_ref=slot if src is None else src, dst_ref=slot,
                send_sem=send_sems.at[w, k], recv_sem=recv_sems.at[w, k], device_id=to, device_id_type=MESH_ID)

        mine = [pltpu.make_async_copy(x_refs[w], out_refs[w].at[4 * x + 2 * y + c], local_sems.at[w])
                for w in range(n)]
        for cp in mine:
            cp.start()
        first = []
        for w in range(n):
            first.append(copy(w, 0, me, sibling, src=x_refs[w]))
            first += [copy(w, 1 + j, me, (*chip, c), src=x_refs[w]) for j, chip in enumerate(chips)]
        for cp in first:
            cp.start()
        passed = []
        for j, chip in enumerate(chips):
            for w in range(n):
                copy(w, 1 + j, (*chip, c), me).wait_recv()
                fwd = copy(w, 4 + j, (*chip, c), sibling)
                fwd.start()
                passed.append(fwd)
        for w in range(n):
            copy(w, 0, sibling, me).wait_recv()
            for j, chip in enumerate(chips):
                copy(w, 4 + j, (*chip, 1 - c), me).wait_recv()
        for cp in first + passed:
            cp.wait_send()
        for cp in mine:
            cp.wait()

    return pl.pallas_call(
        body, name=name, out_shape=[jax.ShapeDtypeStruct((N_DEV,) + b.shape, b.dtype) for b in blocks],
        in_specs=[HBM_SPEC] * n, out_specs=[HBM_SPEC] * n,
        scratch_shapes=[pltpu.SemaphoreType.DMA((n, 7)), pltpu.SemaphoreType.DMA((n, 7)),
                        pltpu.SemaphoreType.DMA((n,))],
    )(*blocks)


def _all_reduce_small(block, *, name):
    R, W = block.shape

    def body(x_ref, out_ref, buf, send_sems, recv_sems):
        x, y, c, me = _mesh_place()
        copies = []
        for k, (peer, _) in enumerate(_peers(x, y, c)):
            copies.append(pltpu.make_async_remote_copy(
                src_ref=x_ref, dst_ref=buf.at[me], send_sem=send_sems.at[k], recv_sem=recv_sems.at[k],
                device_id=peer, device_id_type=MESH_ID))
        for cp in copies:
            cp.start()
        buf[me] = x_ref[...]
        for cp in copies:
            cp.wait_recv()
        for cp in copies:
            cp.wait_send()
        acc = buf[0]
        for j in range(1, N_DEV):
            acc = acc + buf[j]
        out_ref[...] = acc

    return pl.pallas_call(
        body, name=name, out_shape=jax.ShapeDtypeStruct((R, W), F32),
        in_specs=[VMEM_SPEC], out_specs=VMEM_SPEC,
        scratch_shapes=[pltpu.VMEM((N_DEV, R, W), F32), pltpu.SemaphoreType.DMA((7,)), pltpu.SemaphoreType.DMA((7,))],
    )(block)


def _assemble(name, g):
    if BIG_AXIS[name] == 2:
        return jnp.concatenate([g[j] for j in range(N_DEV)], axis=1)
    return g.reshape(N_DEV * g.shape[1], g.shape[2])


def _split_for_devices(name, g):
    if BIG_AXIS[name] == 2:
        b = g.shape[1] // N_DEV
        return jnp.stack([g[:, j * b:(j + 1) * b] for j in range(N_DEV)]).astype(BF16)
    return g.reshape(N_DEV, g.shape[0] // N_DEV, g.shape[1]).astype(BF16)


def _layer_weight_keys(i):
    j = i // 2
    mixer = [('ev_w_in', j), ('ev_w_uq', j), ('ev_w_ukv', j), ('ev_w_out', j)] if i % 2 == 0 \
        else [('od_w_in', j), ('od_w_out', j)]
    return mixer + [('w_up', i), ('w_down', i), ('ple_w_proj', i), ('ple_w_gate', i)]


def _weight_layer(key):
    name, idx = key
    return 2 * idx if name.startswith('ev_') else 2 * idx + 1 if name.startswith('od_') else idx


FIRST_GATHER = [('ev_w_in', 0), ('ev_w_uq', 0), ('ev_w_ukv', 0), ('ev_w_out', 0)]
FWD_CARRIERS = {
    'l0_mla': [('w_up', 0), ('ple_w_proj', 0), ('ple_w_gate', 0)],
    'l0_swa': [('w_down', 0)],
    'l0_out_ln1': [('od_w_out', 0)],
    'l0_up': [('od_w_in', 0)],
    'l0_down_ln2': [('w_up', 1)],
    'l0_ple_gate': [('ple_w_proj', 1), ('ple_w_gate', 1)],
    'l1_fox': [('w_down', 1), ('ev_w_in', 1), ('ev_w_uq', 1), ('ev_w_ukv', 1), ('ev_w_out', 1), ('w_up', 2)],
    'l1_up': [('w_down', 2)],
    'l1_down_ln2': [('ple_w_proj', 2), ('ple_w_gate', 2)],
    'l2_mla': [('od_w_in', 1), ('od_w_out', 1)],
    'l2_swa': [('w_up', 3)],
    'l2_up': [('w_down', 3)],
    'l2_down_ln2': [('ple_w_proj', 3), ('ple_w_gate', 3)],
}


class _MeshExchange:
    def __init__(self, shards):
        self.shards = shards
        self.weights = {i: {} for i in range(DEPTH)}
        self.pending = []
        self.in_flight = []
        self.received = {}
        got = _all_gather_hbm([self.shards[n][idx] for n, idx in FIRST_GATHER], name="gather_first")
        self._landed(FIRST_GATHER, got)

    def _landed(self, keys, gathered):
        for k, g in zip(keys, gathered):
            self.weights[_weight_layer(k)][k[0]] = _assemble(k[0], g)

    def layer_weights(self, i):
        return self.weights[i]

    def carry(self, kernel_name):
        return [(("gather", idx), self.shards[n]) for n, idx in FWD_CARRIERS.get(kernel_name, [])]

    def carried(self, kernel_name, outs):
        self._landed(FWD_CARRIERS.get(kernel_name, []), outs)

    def push_grads(self, grads):
        self.pending += [(k, _split_for_devices(k[0], g)) for k, g in grads.items()]

    def bwd_items(self):
        self.in_flight, self.pending = self.pending, []
        return [("scatter", parts) for _, parts in self.in_flight]

    def bwd_done(self, outs):
        for (k, _), got in zip(self.in_flight, outs):
            self.received[k] = got
        self.in_flight = []

    def finish(self):
        if self.pending:
            outs = _exchange(self.bwd_items(), name="scatter_rest")
            self.bwd_done(outs)
        return self.received


PACK_ROWS = 8


def _pack_small(vals):
    flat = jnp.concatenate([vals[n].reshape(-1).astype(F32) for n in SMALL])
    pad = (-flat.shape[0]) % (PACK_ROWS * LANES)
    return jnp.pad(flat, (0, pad)).reshape(-1, LANES)


def _unpack_small(block, shapes):
    flat = block.reshape(-1)
    out, off = {}, 0
    for n in SMALL:
        sz = math.prod(shapes[n])
        out[n] = flat[off:off + sz].reshape(shapes[n])
        off += sz
    return out


def _rope_tables(S):
    half = MLA_ROPE // 2
    inv = 1.0 / (ROPE_THETA ** (jnp.arange(0, MLA_ROPE, 2, dtype=F32) / MLA_ROPE))
    ang = jnp.arange(S, dtype=F32)[:, None] * inv[None, :]
    cos, sin = jnp.cos(ang), jnp.sin(ang)
    zeros = jnp.zeros((S, half), F32)
    tail = jnp.zeros((S, LANES - MLA_QK), F32)

    def block(rope_part, nope_val):
        return jnp.concatenate([jnp.full((S, MLA_NOPE), nope_val, F32), rope_part, tail], -1)

    a_r = jnp.concatenate([cos, cos], -1)
    bm_r = jnp.concatenate([-sin, zeros], -1)
    bp_r = jnp.concatenate([zeros, sin], -1)
    q_tabs = tuple(block(r, v) for r, v in ((a_r, 1.0), (bm_r, 0.0), (bp_r, 0.0)))
    k_tabs = tuple(block(r, 0.0) for r in (a_r, bm_r, bp_r))
    return q_tabs, k_tabs


def _t5_bucket(dist):
    exact = REL_BUCKETS // 2
    d = jnp.maximum(dist, 1).astype(F32)
    large = exact + (jnp.log(d / exact) / math.log(REL_MAX_DIST / exact) * (REL_BUCKETS - exact)).astype(jnp.int32)
    large = jnp.minimum(large, REL_BUCKETS - 1)
    return jnp.where(dist < exact, dist, large)


def _swa_bucket_table():
    a = jnp.arange(BLOCK_Q)[:, None]
    col = jnp.arange(2 * BLOCK_Q)[None, :]
    return _t5_bucket(jnp.maximum(a + BLOCK_Q - col, 0)).astype(jnp.int32)


def _even_weights(W):
    w = W['ev_w_in']
    c_kv1 = MLA_Q_LORA + MLA_KV_LORA
    c_kr1 = c_kv1 + MLA_ROPE
    c_qs1 = c_kr1 + SWA_HEADS * HEAD_DIM
    zeros = lambda n: jnp.zeros((D_MODEL, n), w.dtype)
    w_in = jnp.concatenate([w[:, c_kr1:c_qs1], w[:, :c_kv1], w[:, c_qs1:], zeros(KR_LANE0), w[:, c_kv1:c_kr1],
                            zeros(LANES - KR_LANE0 - MLA_ROPE)], axis=1)
    uq = W['ev_w_uq'].reshape(MLA_Q_LORA, MLA_HEADS, MLA_QK)
    w_uq = jnp.pad(uq, ((0, 0), (0, 0), (0, LANES - MLA_QK))).reshape(MLA_Q_LORA, MLA_HEADS * LANES)
    ukv = W['ev_w_ukv'].reshape(MLA_KV_LORA, MLA_HEADS, MLA_NOPE + MLA_V)
    w_k = jnp.pad(ukv[..., :MLA_NOPE], ((0, 0), (0, 0), (0, LANES - MLA_NOPE))).reshape(MLA_KV_LORA, -1)
    w_v = ukv[..., MLA_NOPE:].reshape(MLA_KV_LORA, MLA_HEADS * MLA_V)
    return w_in, w_uq, w_k, w_v, W['ev_w_out']


def _even_in_grad_unpad(dw):
    kr0 = EV_KR[0] + KR_LANE0
    return jnp.concatenate([dw[:, EV_CQ[0]:EV_CKV[1]], dw[:, kr0:kr0 + MLA_ROPE], dw[:, EV_QS[0]:EV_QS[1]],
                            dw[:, EV_KS[0]:EV_VS[1]]], axis=1)


def _even_fwd(xb, W, P, i, B, S, tabs, xchg, tag):
    j = i // 2
    q_tabs, k_tabs, bias, sinkcol = tabs
    w_in, w_uq, w_k, w_v, w_out = _even_weights(W)
    h = _mm(xb, w_in, name=f"{tag}_in")
    cqn, ckvn, rq, rkv = _even_norms(h, P['ev_q_norm'][j][None], P['ev_kv_norm'][j][None], name=f"{tag}_norms")
    q = _rope(_mm(cqn, w_uq, name=f"{tag}_uq"), q_tabs, S, sign=1.0, name=f"{tag}_ropeq")
    knp = _mm(ckvn, w_k, out_dtypes=(BF16,), name=f"{tag}_uk")
    v = _mm(ckvn, w_v, out_dtypes=(BF16,), name=f"{tag}_uv")
    k = _mla_keys(knp, h, k_tabs, S, name=f"{tag}_keys")
    o_mla, lse_mla, got = _flash_fwd(q, k, v, q_blk0=0, k_blk0=0, v_blk0=0, W=2 * LANES, n_pairs=MLA_HEADS // 2,
                                     B=B, S=S, scale=MLA_QK ** -0.5, comm=xchg.carry(f"{tag}_mla"), name=f"{tag}_mla")
    xchg.carried(f"{tag}_mla", got)
    o_swa, lse_swa, got = _swa_fwd(h, bias, sinkcol, B=B, S=S, comm=xchg.carry(f"{tag}_swa"), name=f"{tag}_swa")
    xchg.carried(f"{tag}_swa", got)
    res = dict(h=h, cqn=cqn, ckvn=ckvn, rq=rq, rkv=rkv, q=q, k=k, v=v, o_mla=o_mla, lse_mla=lse_mla,
               o_swa=o_swa, lse_swa=lse_swa)
    return ((o_mla, o_swa), w_out), res


def _shift_prev(own, prev, B, S):
    prev = prev.reshape(B, S, LANES)
    shifted = jnp.concatenate([prev[:, BLOCK_Q:], jnp.zeros_like(prev[:, :BLOCK_Q])], axis=1)
    return (own + shifted.reshape(B * S, LANES)).astype(BF16)


def _even_bwd(dmb, dz1, xb, W, P, j, B, S, tabs, res, xchg, tag):
    q_tabs, k_tabs, bias, sinkcol = tabs
    w_in, w_uq, w_k, w_v, w_out = _even_weights(W)
    g = {}
    g['ev_w_out'] = jnp.concatenate([_mm_tn(res['o_mla'], dmb, name=f"{tag}_dwout_mla"),
                                     _mm_tn(res['o_swa'], dmb, name=f"{tag}_dwout_swa")], axis=0)
    do = _mm(dmb, w_out, trans_b=True, out_dtypes=(BF16,), name=f"{tag}_do")
    dq, dk, dv, got = _flash_bwd(res['q'], res['k'], res['v'], res['o_mla'], do, res['lse_mla'], q_blk0=0, k_blk0=0,
                                 v_blk0=0, do_blk0=0, W=2 * LANES, n_pairs=MLA_HEADS // 2, B=B, S=S,
                                 scale=MLA_QK ** -0.5, qk_dtype=F32, comm=xchg.bwd_items(), name=f"{tag}_mla_bwd")
    xchg.bwd_done(got)
    dq_pre = _rope(dq, q_tabs, S, sign=-1.0, name=f"{tag}_ropeq_bwd")
    dw_uq = _mm_tn(res['cqn'], dq_pre, name=f"{tag}_dwuq")
    g['ev_w_uq'] = dw_uq.reshape(MLA_Q_LORA, MLA_HEADS, LANES)[..., :MLA_QK].reshape(MLA_Q_LORA, MLA_HEADS * MLA_QK)
    dcqn = _mm(dq_pre, w_uq, trans_b=True, name=f"{tag}_dcqn")
    dw_k = _mm_tn(res['ckvn'], dk, name=f"{tag}_dwuk").reshape(MLA_KV_LORA, MLA_HEADS, LANES)[..., :MLA_NOPE]
    dw_v = _mm_tn(res['ckvn'], dv, name=f"{tag}_dwuv").reshape(MLA_KV_LORA, MLA_HEADS, MLA_V)
    g['ev_w_ukv'] = jnp.concatenate([dw_k, dw_v], axis=-1).reshape(MLA_KV_LORA, MLA_HEADS * (MLA_NOPE + MLA_V))
    dckvn_v = _mm(dv, w_v, trans_b=True, name=f"{tag}_dckvn_v")
    dckvn = _mm(dk, w_k, trans_b=True, extras=(dckvn_v,), epilogue=lambda acc, r: (acc + r,), name=f"{tag}_dckvn")
    dkr_pre = _mla_rope_key_grad(dk, k_tabs, S, name=f"{tag}_ropek_bwd")
    xchg.push_grads({(n, j): g.pop(n) for n in list(g)})
    dqs, dko, dkp, dvo, dvp, dbias, dsink, got = _swa_bwd(res['h'], res['o_swa'], do, res['lse_swa'], bias, sinkcol,
                                                          do_blk0=1, B=B, S=S, comm=xchg.bwd_items(),
                                                          name=f"{tag}_swa_bwd")
    xchg.bwd_done(got)
    dh, dgq, dgkv = _even_in_bwd(res['h'], res['rq'], res['rkv'], P['ev_q_norm'][j][None], P['ev_kv_norm'][j][None],
                                 dcqn, dckvn, dqs, _shift_prev(dko, dkp, B, S), _shift_prev(dvo, dvp, B, S), dkr_pre,
                                 name=f"{tag}_in_bwd")
    g['ev_w_in'] = _even_in_grad_unpad(_mm_tn(xb, dh, name=f"{tag}_dwin"))
    xchg.push_grads({(n, j): val for n, val in g.items()})
    dx_kwargs = dict(trans_b=True, extras=(dz1,), epilogue=lambda acc, r: (acc + DN_ALPHA * r,), name=f"{tag}_dx")
    dx = _scattering(xchg, _mm, dh, w_in, **dx_kwargs) if j == 0 else _mm(dh, w_in, **dx_kwargs)
    small = dict(ev_q_norm=dgq[0], ev_kv_norm=dgkv[0], dbias=dbias, ev_sinks=jnp.sum(dsink, axis=(1, 2)))
    return dx, small


def _odd_fwd(xb, W, P, i, B, S, xchg, tag):
    j = i // 2
    w = W['od_w_in']
    w_qkv = w[:, :ODD_QKV]
    w_f = jnp.pad(w[:, ODD_QKV:], ((0, 0), (0, LANES - FOX_HEADS)))
    bf = jnp.pad(P['od_b_f'][j], (0, LANES - FOX_HEADS))[None]
    qkv = _mm(xb, w_qkv, out_dtypes=(BF16,), name=f"{tag}_qkv")
    f = _mm(xb, w_f, name=f"{tag}_f").reshape(B, S, LANES)
    csh, chs = _fox_decay_fwd(f, bf, name=f"{tag}_decay")
    crow = chs[:, :FOX_HEADS].reshape(B, FOX_HEADS, S // ATT_TILE, 1, ATT_TILE)
    n_blk = FOX_HEADS * HEAD_DIM // LANES
    o, lse, got = _flash_fwd(qkv, qkv, qkv, q_blk0=0, k_blk0=n_blk, v_blk0=2 * n_blk, W=LANES,
                             n_pairs=FOX_HEADS // 2, B=B, S=S, scale=HEAD_DIM ** -0.5, csh=csh, crow=crow,
                             comm=xchg.carry(f"{tag}_fox"), name=f"{tag}_fox")
    xchg.carried(f"{tag}_fox", got)
    res = dict(f=f, bf=bf, csh=csh, crow=crow, qkv=qkv, o=o, lse=lse, w_qkv=w_qkv, w_f=w_f)
    return (o, W['od_w_out']), res


def _odd_bwd(dmb, dz1, xb, W, P, j, B, S, res, xchg, tag):
    g = {}
    w_out = W['od_w_out']
    g['od_w_out'] = _mm_tn(res['o'], dmb, name=f"{tag}_dwout")
    do = _mm(dmb, w_out, trans_b=True, out_dtypes=(BF16,), name=f"{tag}_do")
    qkv = res['qkv']
    n_blk = FOX_HEADS * HEAD_DIM // LANES
    dq, dk, dv, dck, dcq, got = _flash_bwd(qkv, qkv, qkv, res['o'], do, res['lse'], q_blk0=0, k_blk0=n_blk,
                                           v_blk0=2 * n_blk, do_blk0=0, W=LANES, n_pairs=FOX_HEADS // 2, B=B, S=S,
                                           scale=HEAD_DIM ** -0.5, qk_dtype=BF16, csh=res['csh'], crow=res['crow'],
                                           comm=xchg.bwd_items(), name=f"{tag}_fox_bwd")
    xchg.bwd_done(got)
    dc = dck.reshape(B, FOX_HEADS, S) + dcq.reshape(B, FOX_HEADS, S)
    dc_hs = jnp.pad(dc, ((0, 0), (0, LANES - FOX_HEADS), (0, 0)))
    df, dbf = _fox_decay_bwd(dc_hs, res['f'], res['bf'], name=f"{tag}_decay_bwd")
    df = df.reshape(B * S, LANES)
    dw_qkv = [_mm_tn(xb, t, name=f"{tag}_dw{n}") for n, t in (("q", dq), ("k", dk), ("v", dv))]
    dw_f = _mm_tn(xb, df, name=f"{tag}_dwf")
    g['od_w_in'] = jnp.concatenate(dw_qkv + [dw_f[:, :FOX_HEADS]], axis=1)
    dxf = _mm(df, res['w_f'], trans_b=True, extras=(dz1,), epilogue=lambda acc, r: (acc + DN_ALPHA * r,),
              name=f"{tag}_dxf")
    xchg.push_grads({(n, j): val for n, val in g.items()})
    dx = _mm((dq, dk, dv), res['w_qkv'], trans_b=True, extras=(dxf,), epilogue=lambda acc, r: (acc + r,),
             name=f"{tag}_dx")
    small = dict(od_b_f=dbf[0, :FOX_HEADS])
    return dx, small


def _carrying(xchg, name, call, *args, **kwargs):
    comm = xchg.carry(name)
    out = call(*args, comm=comm, name=name, **kwargs)
    if comm:
        out, got = out
        xchg.carried(name, got)
    return out


def _scattering(xchg, call, *args, **kwargs):
    comm = xchg.bwd_items()
    out = call(*args, comm=comm, **kwargs)
    if comm:
        out, got = out
        xchg.bwd_done(got)
    return out


def _local_step(x, p, target, P, xchg):
    B, S, D = x.shape
    T = B * S
    q_tabs, k_tabs = _rope_tables(S)
    bucket = _swa_bucket_table()
    in_bucket = (bucket[..., None] == jnp.arange(REL_BUCKETS)).astype(F32)
    bias = jnp.einsum('acb,bh->hac', in_bucket, P['rel_bias'], precision=lax.Precision.HIGHEST)

    xc = x.reshape(T, D)
    xcb = xc.astype(BF16)
    saved = []
    for i in range(DEPTH):
        j = i // 2
        tag = f"l{i}"
        W = xchg.layer_weights(i)
        lay = dict(xb=xcb, W=W)
        if i % 2 == 0:
            sinkcol = jnp.broadcast_to(P['ev_sinks'][j][:, None, None], (SWA_HEADS, BLOCK_Q, 1)).astype(F32)
            lay['tabs'] = (q_tabs, k_tabs, bias, sinkcol)
            (o, w_out), lay['mix'] = _even_fwd(xcb, W, P, i, B, S, lay['tabs'], xchg, tag)
        else:
            (o, w_out), lay['mix'] = _odd_fwd(xcb, W, P, i, B, S, xchg, tag)
        x1, x1b, lay['xh1'], lay['r1'] = _carrying(xchg, f"{tag}_out_ln1", _mm_ln, o, w_out, xc,
                                                   P['ln1_g'][i][None], P['ln1_b'][i][None])
        lay['x1b'] = x1b
        lay['u'], lay['a'] = _carrying(xchg, f"{tag}_up", _mm, x1b, W['w_up'], out_dtypes=(F32, BF16),
                                       epilogue=lambda acc: (acc, jnp.square(jnp.maximum(acc, 0.0))))
        x2, x2b, lay['xh2'], lay['r2'] = _carrying(xchg, f"{tag}_down_ln2", _mm_ln, lay['a'], W['w_down'], x1,
                                                   P['ln2_g'][i][None], P['ln2_b'][i][None])
        lay['x2b'] = x2b
        lay['p'] = p[i].reshape(T, D_PLE)
        lay['e'] = _mm(lay['p'], W['ple_w_proj'], name=f"{tag}_ple_proj")

        def gate(acc, bg, e, x2v):
            gv = 1.0 / (1.0 + jnp.exp(-(acc + bg)))
            y = x2v + gv * e
            return y, y, gv

        xc, xcb, lay['g'] = _carrying(xchg, f"{tag}_ple_gate", _mm, x2b, W['ple_w_gate'],
                                      extras=(P['ple_b_gate'][i][None], lay['e'], x2), epilogue=gate,
                                      out_dtypes=(F32, BF16, F32))
        saved.append(lay)

    dy, sq = _loss_grad(xc, target.reshape(T, D), name="loss")

    Gs = {n: [None] * DEPTH for n in ('ln1_g', 'ln1_b', 'ln2_g', 'ln2_b', 'ple_b_gate')}
    Gs.update({n: [None] * (DEPTH // 2) for n in ('ev_q_norm', 'ev_kv_norm', 'ev_sinks', 'od_b_f')})
    dbias_total = None
    for i in reversed(range(DEPTH)):
        j = i // 2
        tag = f"l{i}b"
        lay = saved[i]
        W = lay['W']
        de, dzg, dbg = _ple_bwd_elem(dy, lay['g'], lay['e'], name=f"{tag}_ple_elem")
        Gs['ple_b_gate'][i] = dbg[0]
        g_mlp = {('ple_w_proj', i): _mm_tn(lay['p'], de, name=f"{tag}_dwproj"),
                 ('ple_w_gate', i): _mm_tn(lay['x2b'], dzg, name=f"{tag}_dwgate")}
        dz2, dz2b, dg2, db2 = _mm_ln_bwd(dzg, W['ple_w_gate'], dy, 1.0, lay['xh2'], lay['r2'], P['ln2_g'][i][None],
                                         name=f"{tag}_dx2_ln2")
        Gs['ln2_g'][i], Gs['ln2_b'][i] = dg2[0], db2[0]
        g_mlp[('w_down', i)] = _mm_tn(lay['a'], dz2b, name=f"{tag}_dwdown")
        du = _mm(dz2b, W['w_down'], trans_b=True, extras=(lay['u'],), out_dtypes=(BF16,),
                 epilogue=lambda acc, u: (acc * (2.0 * jnp.maximum(u, 0.0)),), name=f"{tag}_du")
        g_mlp[('w_up', i)] = _mm_tn(lay['x1b'], du, name=f"{tag}_dwup")
        xchg.push_grads(g_mlp)
        dz1, dz1b, dg1, db1 = _mm_ln_bwd(du, W['w_up'], dz2, DN_ALPHA, lay['xh1'], lay['r1'], P['ln1_g'][i][None],
                                         name=f"{tag}_dx1_ln1")
        Gs['ln1_g'][i], Gs['ln1_b'][i] = dg1[0], db1[0]
        if i % 2 == 0:
            dy, small = _even_bwd(dz1b, dz1, lay['xb'], W, P, j, B, S, lay['tabs'], lay['mix'], xchg, tag)
            dbias_total = small['dbias'] if dbias_total is None else dbias_total + small['dbias']
            for n in ('ev_q_norm', 'ev_kv_norm', 'ev_sinks'):
                Gs[n][j] = small[n]
        else:
            dy, small = _odd_bwd(dz1b, dz1, lay['xb'], W, P, j, B, S, lay['mix'], xchg, tag)
            Gs['od_b_f'][j] = small['od_b_f']

    grads_small = {n: jnp.stack(v) for n, v in Gs.items()}
    drel = _bias_bucket_sum(dbias_total, bucket, name="rel_bias_grad")
    grads_small['rel_bias'] = drel[:, :REL_BUCKETS].T
    return sq, dy.reshape(B, S, D), grads_small


def kernel(x, p, rel_bias, ev_w_in, ev_q_norm, ev_w_uq, ev_kv_norm, ev_w_ukv, ev_sinks, ev_w_out, od_w_in, od_b_f, od_w_out, ln1_g, ln1_b, w_up, w_down, ln2_g, ln2_b, ple_w_proj, ple_w_gate, ple_b_gate, loss_target, m_rel_bias, m_ev_w_in, m_ev_q_norm, m_ev_w_uq, m_ev_kv_norm, m_ev_w_ukv, m_ev_sinks, m_ev_w_out, m_od_w_in, m_od_b_f, m_od_w_out, m_ln1_g, m_ln1_b, m_w_up, m_w_down, m_ln2_g, m_ln2_b, m_ple_w_proj, m_ple_w_gate, m_ple_b_gate, v_rel_bias, v_ev_w_in, v_ev_q_norm, v_ev_w_uq, v_ev_kv_norm, v_ev_w_ukv, v_ev_sinks, v_ev_w_out, v_od_w_in, v_od_b_f, v_od_w_out, v_ln1_g, v_ln1_b, v_w_up, v_w_down, v_ln2_g, v_ln2_b, v_ple_w_proj, v_ple_w_gate, v_ple_b_gate):
    given = dict(locals())
    w = {n: given[n] for n in WEIGHTS}
    mom = {n: given["m_" + n] for n in WEIGHTS}
    var = {n: given["v_" + n] for n in WEIGHTS}
    small_shapes = {n: w[n].shape for n in SMALL}

    xchg = _MeshExchange({n: w[n].astype(BF16) for n in BIG})
    P = {n: w[n] for n in SMALL}

    sq, grad_x, grads_small = _local_step(x, p, loss_target, P, xchg)
    loss = lax.psum(0.5 * jnp.sum(sq) / D_MODEL, ("x", "y", "c"))

    received = xchg.finish()
    g_small_packed = _all_reduce_small(_pack_small(grads_small), name="reduce_small_grads")
    g_small = _unpack_small(g_small_packed, small_shapes)

    grad, delta, new_m, new_v = {}, {}, {}, {}
    for n in BIG:
        parts = [received[(n, idx)] for idx in range(w[n].shape[0])]
        grad[n], delta[n], new_m[n], new_v[n] = _adamw_slots(w[n], parts, mom[n], var[n], name=f"adamw_{n}")
    d, nm, nv = _adamw(_pack_small(w), g_small_packed, _pack_small(mom), _pack_small(var), name="adamw_small")
    d, nm, nv = (_unpack_small(t, small_shapes) for t in (d, nm, nv))
    for n in SMALL:
        grad[n], delta[n], new_m[n], new_v[n] = g_small[n], d[n], nm[n], nv[n]

    return (loss, grad_x, *[grad[n] for n in WEIGHTS], *[delta[n] for n in WEIGHTS],
            *[new_m[n] for n in WEIGHTS], *[new_v[n] for n in WEIGHTS])
```

```python
import math

import jax
import jax.numpy as jnp
from jax import lax
from jax.experimental import pallas as pl
from jax.experimental.pallas import tpu as pltpu

F32, BF16 = jnp.float32, jnp.bfloat16

D_MODEL = 1024
DEPTH = 4
HEAD_DIM = 64
MLA_HEADS, MLA_NOPE, MLA_ROPE, MLA_V = 8, 64, 32, 64
MLA_Q_LORA, MLA_KV_LORA = 384, 256
MLA_QK = MLA_NOPE + MLA_ROPE
ROPE_THETA = 10000.0
SWA_HEADS, SWA_KV_HEADS, SWA_WINDOW = 8, 2, 128
SWA_GROUP = SWA_HEADS // SWA_KV_HEADS
REL_BUCKETS, REL_MAX_DIST = 32, 128
FOX_HEADS = 16
D_FF = 4 * D_MODEL
D_PLE = 256
BLOCK_Q = 128
DN_ALPHA = (2 * DEPTH) ** 0.25
NORM_EPS = 1e-5
NEG_INF = -1e30
EVEN_IN = 1440
ODD_QKV = 3 * FOX_HEADS * HEAD_DIM
LANES = 128

EV_QS = (0, 512)
EV_CQ = (512, 896)
EV_CKV = (896, 1152)
EV_KS = (1152, 1280)
EV_VS = (1280, 1408)
EV_KR = (1408, 1536)
EVEN_IN_PAD = 1536
KR_LANE0 = MLA_NOPE

ADAM_LR, ADAM_B1, ADAM_B2, ADAM_EPS, ADAM_WD, ADAM_STEP = 0.001, 0.9, 0.999, 1e-08, 0.01, 10

N_DEV = 8
VMEM_LIMIT_BYTES = 48 * 1024 * 1024
ATT_TILE = 512
ATT_TILE_BWD = 512
PAIRS_PER_STEP_FWD = 4
PAIRS_PER_STEP_BWD = 2

NN = (((1,), (0,)), ((), ()))
NT = (((1,), (1,)), ((), ()))
TN = (((0,), (0,)), ((), ()))

BIG = ['ev_w_in', 'ev_w_uq', 'ev_w_ukv', 'ev_w_out', 'od_w_in', 'od_w_out', 'w_up', 'w_down',
       'ple_w_proj', 'ple_w_gate']
BIG_AXIS = {'ev_w_in': 2, 'ev_w_uq': 2, 'ev_w_ukv': 2, 'ev_w_out': 1, 'od_w_in': 2, 'od_w_out': 1,
            'w_up': 2, 'w_down': 1, 'ple_w_proj': 2, 'ple_w_gate': 1}
SMALL = ['rel_bias', 'ev_q_norm', 'ev_kv_norm', 'ev_sinks', 'od_b_f', 'ln1_g', 'ln1_b', 'ln2_g', 'ln2_b',
         'ple_b_gate']
WEIGHTS = ['rel_bias', 'ev_w_in', 'ev_q_norm', 'ev_w_uq', 'ev_kv_norm', 'ev_w_ukv', 'ev_sinks', 'ev_w_out',
           'od_w_in', 'od_b_f', 'od_w_out', 'ln1_g', 'ln1_b', 'w_up', 'w_down', 'ln2_g', 'ln2_b',
           'ple_w_proj', 'ple_w_gate', 'ple_b_gate']


def _cparams(*sem):
    return pltpu.CompilerParams(dimension_semantics=sem, vmem_limit_bytes=VMEM_LIMIT_BYTES)


def _pick(n, cands):
    for c in cands:
        if n % c == 0:
            return c
    return n


MM_STEP_BYTES = 10 * 1024 * 1024
MM_OUT_BYTES = 8 * 1024 * 1024
MM_CHUNK = 512


def _mm(a, b, *, trans_b=False, extras=(), epilogue=None, row_epilogue=None, out_dtypes=(F32,), out_widths=None,
        n_sums=0, comm=(), name):
    a_parts = tuple(a) if isinstance(a, (tuple, list)) else (a,)
    n_a = len(a_parts)
    M = a_parts[0].shape[0]
    k_offs = [sum(p.shape[1] for p in a_parts[:i]) for i in range(n_a + 1)]
    N = b.shape[0] if trans_b else b.shape[1]
    n_ex, n_out = len(extras), len(out_dtypes)
    n_rows_out = n_out - n_sums
    out_widths = (N,) * n_out if out_widths is None else out_widths
    row_bytes = sum(p.shape[1] * p.dtype.itemsize for p in a_parts) + (sum(w * jnp.dtype(d).itemsize
                                            for w, d in zip(out_widths[:n_rows_out], out_dtypes))
                                        + sum(e.shape[1] * e.dtype.itemsize for e in extras if e.shape[0] == M)
                                        + (4 * N if row_epilogue is not None else 0))
    tm = next((c for c in (1024, 512, 256) if M % c == 0 and c * row_bytes <= MM_STEP_BYTES), 128)
    nc = _pick(N, (MM_CHUNK, 384, 256, 128))
    n_c, kinds = len(comm), [k for k, _ in comm]
    n_scr = 1 if row_epilogue is not None else 0

    def body(*refs):
        a_refs, refs = refs[:n_a], refs[n_a - 1:]
        c_in = refs[2 + n_ex:2 + n_ex + n_c]
        c_out = refs[2 + n_ex + n_c + n_out:2 + n_ex + 2 * n_c + n_out]
        sems = refs[2 + n_ex + 2 * n_c + n_out + n_scr:]
        refs = refs[:2 + n_ex] + refs[2 + n_ex + n_c:2 + n_ex + n_c + n_out] \
            + refs[2 + n_ex + 2 * n_c + n_out:2 + n_ex + 2 * n_c + n_out + n_scr]
        if n_c:
            place = _mesh_place()
            step = pl.program_id(0)

            @pl.when(step == 0)
            def _():
                _comm_start(kinds, c_in, c_out, sems, place)

        b_ref = refs[1]
        ex = refs[2:2 + n_ex]
        outs = refs[2 + n_ex:2 + n_ex + n_out]
        avs = [r[...].astype(BF16) for r in a_refs]
        for n0 in range(0, N, nc):
            cols = slice(n0, n0 + nc)
            acc = None
            for av, k0, k1 in zip(avs, k_offs[:-1], k_offs[1:]):
                bv = (b_ref[cols, k0:k1] if trans_b else b_ref[k0:k1, cols]).astype(BF16)
                part = lax.dot_general(av, bv, NT if trans_b else NN, preferred_element_type=F32)
                acc = part if acc is None else acc + part
            if row_epilogue is not None:
                refs[-1][:, cols] = acc
                continue
            res = epilogue(acc, *[e[:, cols] for e in ex]) if epilogue is not None else (acc,)
            for o, r in zip(outs, res):
                o[:, cols] = r.astype(o.dtype)
        if row_epilogue is not None:
            res = row_epilogue(refs[-1][...], *[e[...] for e in ex])
            for o, r in zip(outs[:n_rows_out], res):
                o[...] = r.astype(o.dtype)
            if n_sums:
                @pl.when(pl.program_id(0) == 0)
                def _():
                    for o in outs[n_rows_out:]:
                        o[...] = jnp.zeros_like(o)

                for o, r in zip(outs[n_rows_out:], res[n_rows_out:]):
                    o[...] += r
        if n_c:
            @pl.when(step == M // tm - 1)
            def _():
                _comm_wait(kinds, c_in, c_out, sems, place)

    in_specs = [pl.BlockSpec((tm, p.shape[1]), lambda i: (i, 0)) for p in a_parts]
    in_specs.append(pl.BlockSpec(b.shape, lambda i: (0, 0)))
    for e in extras:
        if e.shape[0] == M:
            in_specs.append(pl.BlockSpec((tm, e.shape[1]), lambda i: (i, 0)))
        elif e.shape == (1, N):
            in_specs.append(pl.BlockSpec((1, N), lambda i: (0, 0)))
        else:
            raise ValueError(f"extra operand of shape {e.shape} for a ({M}, {N}) result")
    res = pl.pallas_call(
        body, name=name, grid=(M // tm,), in_specs=in_specs + [HBM_SPEC] * n_c,
        out_specs=[pl.BlockSpec((tm, w), lambda i: (i, 0)) for w in out_widths[:n_rows_out]]
        + [pl.BlockSpec((1, w), lambda i: (0, 0)) for w in out_widths[n_rows_out:]] + [HBM_SPEC] * n_c,
        out_shape=[jax.ShapeDtypeStruct((M, w), d) for w, d in zip(out_widths[:n_rows_out], out_dtypes)]
        + [jax.ShapeDtypeStruct((1, w), d) for w, d in zip(out_widths[n_rows_out:], out_dtypes[n_rows_out:])]
        + _comm_out_shapes(comm),
        scratch_shapes=([pltpu.VMEM((tm, N), F32)] if row_epilogue is not None else [])
        + (_comm_scratch(comm) if n_c else []),
        compiler_params=_cparams("arbitrary" if n_sums or n_c else "parallel"),
    )(*a_parts, b, *extras, *[c for _, c in comm])
    main = res[0] if n_out == 1 else tuple(res[:n_out])
    return (main, list(res[n_out:])) if n_c else main


def _mm_tn(a, b, *, slot_width=None, name):
    T, K = a.shape
    N = b.shape[1]
    bk, bn = K, N
    while bk * bn * 4 > MM_OUT_BYTES:
        if bn >= bk and bn % (2 * LANES) == 0:
            bn //= 2
        else:
            bk //= 2
    tt = _pick(T, (1024, 512, 256))
    ck, cn = _pick(bk, (MM_CHUNK, 384, 256, 128)), _pick(bn, (MM_CHUNK, 384, 256, 128))

    def body(a_ref, b_ref, o_ref, acc_ref):
        t = pl.program_id(2)

        @pl.when(t == 0)
        def _():
            acc_ref[...] = jnp.zeros_like(acc_ref)

        for r0 in range(0, bk, ck):
            av = a_ref[:, r0:r0 + ck].astype(BF16)
            for c0 in range(0, bn, cn):
                acc_ref[r0:r0 + ck, c0:c0 + cn] += lax.dot_general(
                    av, b_ref[:, c0:c0 + cn].astype(BF16), TN, preferred_element_type=F32)

        @pl.when(t == T // tt - 1)
        def _():
            if slot_width is None:
                o_ref[...] = acc_ref[...].astype(o_ref.dtype)
            else:
                for slot in range(bn // slot_width):
                    o_ref[slot] = acc_ref[:, slot * slot_width:(slot + 1) * slot_width].astype(o_ref.dtype)

    if slot_width is None:
        out_spec, out_shape = pl.BlockSpec((bk, bn), lambda i, j, t: (i, j)), (K, N)
    else:
        assert bn % slot_width == 0 and slot_width % LANES == 0
        out_spec = pl.BlockSpec((bn // slot_width, bk, slot_width), lambda i, j, t: (j, i, 0))
        out_shape = (N // slot_width, K, slot_width)
    return pl.pallas_call(
        body, name=name, grid=(K // bk, N // bn, T // tt),
        in_specs=[pl.BlockSpec((tt, bk), lambda i, j, t: (t, i)), pl.BlockSpec((tt, bn), lambda i, j, t: (t, j))],
        out_specs=out_spec, out_shape=jax.ShapeDtypeStruct(out_shape, BF16),
        scratch_shapes=[pltpu.VMEM((bk, bn), F32)],
        compiler_params=_cparams("parallel", "parallel", "arbitrary"),
    )(a, b)


ROW_TILE = 256


def _row_spec(cols, col_block=0):
    return pl.BlockSpec((ROW_TILE, cols), lambda i: (i, col_block))


def _tab_spec(cols, period):
    return pl.BlockSpec((ROW_TILE, cols), lambda i: (i % period, 0))


def _full_spec(shape):
    return pl.BlockSpec(shape, lambda i: (0,) * len(shape))


def _mm_ln(a, w, x, g, b, *, comm=(), name):
    def ln_rows(m, xv, gv, bv):
        z = DN_ALPHA * xv + m
        mu = jnp.mean(z, -1, keepdims=True)
        zc = z - mu
        r = lax.rsqrt(jnp.mean(zc * zc, -1, keepdims=True) + NORM_EPS)
        xh = zc * r
        y = xh * gv + bv
        return y, y, xh, jnp.broadcast_to(r, (r.shape[0], LANES))

    D = w.shape[1]
    return _mm(a, w, extras=(x, g, b), row_epilogue=ln_rows, out_dtypes=(F32, BF16, F32, F32),
               out_widths=(D, D, D, LANES), comm=comm, name=name)


def _mm_ln_bwd(a, w, resid, resid_scale, xh, r, g, *, name):
    def ln_bwd_rows(acc, rv, xhv, rstd, gv):
        dyv = acc + resid_scale * rv
        dyg = dyv * gv
        c1 = jnp.mean(dyg, -1, keepdims=True)
        c2 = jnp.mean(dyg * xhv, -1, keepdims=True)
        dz = _widen(rstd, dyv.shape[-1]) * (dyg - c1 - xhv * c2)
        return dz, dz, jnp.sum(dyv * xhv, 0, keepdims=True), jnp.sum(dyv, 0, keepdims=True)

    D = w.shape[0]
    return _mm(a, w, trans_b=True, extras=(resid, xh, r, g), row_epilogue=ln_bwd_rows,
               out_dtypes=(F32, BF16, F32, F32), out_widths=(D, D, D, D), n_sums=2, name=name)


def _loss_grad(y, target, *, name):
    T, D = y.shape

    def body(y_ref, t_ref, dy_ref, sq_ref):
        err = y_ref[...] - t_ref[...]
        dy_ref[...] = err / D

        @pl.when(pl.program_id(0) == 0)
        def _():
            sq_ref[...] = jnp.zeros_like(sq_ref)

        sq_ref[...] += jnp.sum(err * err, 0, keepdims=True)

    return pl.pallas_call(
        body, name=name, grid=(T // ROW_TILE,),
        in_specs=[_row_spec(D), _row_spec(D)],
        out_specs=[_row_spec(D), _full_spec((1, D))],
        out_shape=[jax.ShapeDtypeStruct((T, D), F32), jax.ShapeDtypeStruct((1, D), F32)],
        compiler_params=_cparams("arbitrary"),
    )(y, target)


def _ple_bwd_elem(dx3, g, e, *, name):
    T, D = dx3.shape

    def body(dx_ref, g_ref, e_ref, de_ref, dz_ref, db_ref):
        dx, gv = dx_ref[...], g_ref[...]
        de_ref[...] = (dx * gv).astype(BF16)
        dz = dx * e_ref[...] * gv * (1.0 - gv)
        dz_ref[...] = dz.astype(BF16)

        @pl.when(pl.program_id(0) == 0)
        def _():
            db_ref[...] = jnp.zeros_like(db_ref)

        db_ref[...] += jnp.sum(dz, 0, keepdims=True)

    return pl.pallas_call(
        body, name=name, grid=(T // ROW_TILE,),
        in_specs=[_row_spec(D), _row_spec(D), _row_spec(D)],
        out_specs=[_row_spec(D), _row_spec(D), _full_spec((1, D))],
        out_shape=[jax.ShapeDtypeStruct((T, D), BF16), jax.ShapeDtypeStruct((T, D), BF16),
                   jax.ShapeDtypeStruct((1, D), F32)],
        compiler_params=_cparams("arbitrary"),
    )(dx3, g, e)


def _rotate(xv, a, bm, bp, sign):
    half = MLA_ROPE // 2
    width = xv.shape[-1]
    a, bm, bp = (_widen(t, width) for t in (a, bm, bp))
    return xv * a + sign * (pltpu.roll(xv, width - half, 1) * bm + pltpu.roll(xv, half, 1) * bp)


def _rope(x, tabs, seq, *, sign, name):
    T, width = x.shape

    def body(x_ref, a_ref, bm_ref, bp_ref, o_ref):
        o_ref[...] = _rotate(x_ref[...], a_ref[...], bm_ref[...], bp_ref[...], sign).astype(BF16)

    return pl.pallas_call(
        body, name=name, grid=(T // ROW_TILE,),
        in_specs=[_row_spec(width)] + [_tab_spec(LANES, seq // ROW_TILE)] * 3,
        out_specs=_row_spec(width),
        out_shape=jax.ShapeDtypeStruct((T, width), BF16),
        compiler_params=_cparams("parallel"),
    )(x, *tabs)


def _mla_keys(knp, h, k_tabs, seq, *, name):
    T = knp.shape[0]

    def body(k_ref, h_ref, a_ref, bm_ref, bp_ref, o_ref):
        kr = _rotate(h_ref[...], a_ref[...], bm_ref[...], bp_ref[...], 1.0)
        for hd in range(MLA_HEADS):
            cols = slice(hd * LANES, (hd + 1) * LANES)
            o_ref[:, cols] = (k_ref[:, cols].astype(F32) + kr).astype(BF16)

    return pl.pallas_call(
        body, name=name, grid=(T // ROW_TILE,),
        in_specs=[_row_spec(MLA_HEADS * LANES), _row_spec(LANES, EV_KR[0] // LANES)]
        + [_tab_spec(LANES, seq // ROW_TILE)] * 3,
        out_specs=_row_spec(MLA_HEADS * LANES),
        out_shape=jax.ShapeDtypeStruct((T, MLA_HEADS * LANES), BF16),
        compiler_params=_cparams("parallel"),
    )(knp, h, *k_tabs)


def _mla_rope_key_grad(dk, k_tabs, seq, *, name):
    T = dk.shape[0]

    def body(dk_ref, a_ref, bm_ref, bp_ref, o_ref):
        tot = dk_ref[:, 0:LANES]
        for hd in range(1, MLA_HEADS):
            tot = tot + dk_ref[:, hd * LANES:(hd + 1) * LANES]
        o_ref[...] = _rotate(tot, a_ref[...], bm_ref[...], bp_ref[...], -1.0).astype(BF16)

    return pl.pallas_call(
        body, name=name, grid=(T // ROW_TILE,),
        in_specs=[_row_spec(MLA_HEADS * LANES)] + [_tab_spec(LANES, seq // ROW_TILE)] * 3,
        out_specs=_row_spec(LANES),
        out_shape=jax.ShapeDtypeStruct((T, LANES), BF16),
        compiler_params=_cparams("parallel"),
    )(dk, *k_tabs)


def _even_norms(h, gq, gkv, *, name):
    T = h.shape[0]

    def body(h_ref, gq_ref, gkv_ref, cq_ref, ckv_ref, rq_ref, rkv_ref):
        cq = h_ref[:, EV_CQ[0]:EV_CQ[1]]
        rq = lax.rsqrt(jnp.mean(cq * cq, -1, keepdims=True) + NORM_EPS)
        cq_ref[...] = (cq * rq * gq_ref[...]).astype(BF16)
        rq_ref[...] = jnp.broadcast_to(rq, rq_ref.shape)
        ckv = h_ref[:, EV_CKV[0]:EV_CKV[1]]
        rkv = lax.rsqrt(jnp.mean(ckv * ckv, -1, keepdims=True) + NORM_EPS)
        ckv_ref[...] = (ckv * rkv * gkv_ref[...]).astype(BF16)
        rkv_ref[...] = jnp.broadcast_to(rkv, rkv_ref.shape)

    return pl.pallas_call(
        body, name=name, grid=(T // ROW_TILE,),
        in_specs=[_row_spec(EVEN_IN_PAD), _full_spec((1, MLA_Q_LORA)), _full_spec((1, MLA_KV_LORA))],
        out_specs=[_row_spec(MLA_Q_LORA), _row_spec(MLA_KV_LORA), _row_spec(LANES), _row_spec(LANES)],
        out_shape=[jax.ShapeDtypeStruct((T, MLA_Q_LORA), BF16), jax.ShapeDtypeStruct((T, MLA_KV_LORA), BF16),
                   jax.ShapeDtypeStruct((T, LANES), F32), jax.ShapeDtypeStruct((T, LANES), F32)],
        compiler_params=_cparams("parallel"),
    )(h, gq, gkv)


def _even_in_bwd(h, rq, rkv, gq, gkv, dcqn, dckvn, dqs, dks, dvs, dkr, *, name):
    T = h.shape[0]

    def rms_bwd(c, r, g, dy):
        r = _widen(r, c.shape[-1])
        xr = c * r
        dyg = dy * g
        return r * (dyg - xr * jnp.mean(dyg * xr, -1, keepdims=True)), jnp.sum(dy * xr, 0, keepdims=True)

    def body(h_ref, rq_ref, rkv_ref, gq_ref, gkv_ref, dcq_ref, dckv_ref, dqs_ref, dks_ref, dvs_ref, dkr_ref,
             dh_ref, dgq_ref, dgkv_ref):
        @pl.when(pl.program_id(0) == 0)
        def _():
            dgq_ref[...] = jnp.zeros_like(dgq_ref)
            dgkv_ref[...] = jnp.zeros_like(dgkv_ref)

        dcq, dgq = rms_bwd(h_ref[:, EV_CQ[0]:EV_CQ[1]], rq_ref[...], gq_ref[...], dcq_ref[...])
        dckv, dgkv = rms_bwd(h_ref[:, EV_CKV[0]:EV_CKV[1]], rkv_ref[...], gkv_ref[...], dckv_ref[...])
        dgq_ref[...] += dgq
        dgkv_ref[...] += dgkv
        dh_ref[:, EV_QS[0]:EV_QS[1]] = dqs_ref[...]
        dh_ref[:, EV_CQ[0]:EV_CQ[1]] = dcq.astype(BF16)
        dh_ref[:, EV_CKV[0]:EV_CKV[1]] = dckv.astype(BF16)
        dh_ref[:, EV_KS[0]:EV_KS[1]] = dks_ref[...]
        dh_ref[:, EV_VS[0]:EV_VS[1]] = dvs_ref[...]
        dh_ref[:, EV_KR[0]:EV_KR[1]] = dkr_ref[...]

    return pl.pallas_call(
        body, name=name, grid=(T // ROW_TILE,),
        in_specs=[_row_spec(EVEN_IN_PAD), _row_spec(LANES), _row_spec(LANES), _full_spec((1, MLA_Q_LORA)),
                  _full_spec((1, MLA_KV_LORA)), _row_spec(MLA_Q_LORA), _row_spec(MLA_KV_LORA),
                  _row_spec(SWA_HEADS * HEAD_DIM), _row_spec(LANES), _row_spec(LANES), _row_spec(LANES)],
        out_specs=[_row_spec(EVEN_IN_PAD), _full_spec((1, MLA_Q_LORA)), _full_spec((1, MLA_KV_LORA))],
        out_shape=[jax.ShapeDtypeStruct((T, EVEN_IN_PAD), BF16), jax.ShapeDtypeStruct((1, MLA_Q_LORA), F32),
                   jax.ShapeDtypeStruct((1, MLA_KV_LORA), F32)],
        compiler_params=_cparams("arbitrary"),
    )(h, rq, rkv, gq, gkv, dcqn, dckvn, dqs, dks, dvs, dkr)


def _fox_decay_fwd(f3, bf, *, name):
    B, S, _ = f3.shape

    def body(f_ref, b_ref, csh_ref, chs_ref):
        x = f_ref[...] + b_ref[...]
        c = jnp.minimum(x, 0.0) - jnp.log1p(jnp.exp(-jnp.abs(x)))
        row = lax.broadcasted_iota(jnp.int32, (S, LANES), 0)
        k = 1
        while k < S:
            c = c + jnp.where(row >= k, pltpu.roll(c, k, 0), 0.0)
            k *= 2
        csh_ref[...] = c
        chs_ref[...] = c.T

    return pl.pallas_call(
        body, name=name, grid=(B,),
        in_specs=[pl.BlockSpec((None, S, LANES), lambda b: (b, 0, 0)), pl.BlockSpec((1, LANES), lambda b: (0, 0))],
        out_specs=[pl.BlockSpec((None, S, LANES), lambda b: (b, 0, 0)),
                   pl.BlockSpec((None, LANES, S), lambda b: (b, 0, 0))],
        out_shape=[jax.ShapeDtypeStruct((B, S, LANES), F32), jax.ShapeDtypeStruct((B, LANES, S), F32)],
        compiler_params=_cparams("parallel"),
    )(f3, bf)


def _fox_decay_bwd(dc_hs, f3, bf, *, name):
    B, S, _ = f3.shape

    def body(dc_ref, f_ref, b_ref, df_ref, db_ref):
        g = dc_ref[...].T
        row = lax.broadcasted_iota(jnp.int32, (S, LANES), 0)
        k = 1
        while k < S:
            g = g + jnp.where(row < S - k, pltpu.roll(g, S - k, 0), 0.0)
            k *= 2
        x = f_ref[...] + b_ref[...]
        df = g * (1.0 / (1.0 + jnp.exp(x)))
        df_ref[...] = df.astype(BF16)

        @pl.when(pl.program_id(0) == 0)
        def _():
            db_ref[...] = jnp.zeros_like(db_ref)

        db_ref[...] += jnp.sum(df, 0, keepdims=True)

    return pl.pallas_call(
        body, name=name, grid=(B,),
        in_specs=[pl.BlockSpec((None, LANES, S), lambda b: (b, 0, 0)),
                  pl.BlockSpec((None, S, LANES), lambda b: (b, 0, 0)), pl.BlockSpec((1, LANES), lambda b: (0, 0))],
        out_specs=[pl.BlockSpec((None, S, LANES), lambda b: (b, 0, 0)), pl.BlockSpec((1, LANES), lambda b: (0, 0))],
        out_shape=[jax.ShapeDtypeStruct((B, S, LANES), BF16), jax.ShapeDtypeStruct((1, LANES), F32)],
        compiler_params=_cparams("arbitrary"),
    )(dc_hs, f3, bf)


def _head_column(block, h):
    lane = lax.broadcasted_iota(jnp.int32, block.shape, 1)
    return jnp.sum(jnp.where(lane == h, block, 0.0), axis=-1, keepdims=True)


def _causal_mask(s):
    r = lax.broadcasted_iota(jnp.int32, s.shape, 0)
    c = lax.broadcasted_iota(jnp.int32, s.shape, 1)
    return jnp.where(c <= r, s, NEG_INF)


def _low_half(shape):
    return (lax.broadcasted_iota(jnp.int32, shape, 1) % LANES) < HEAD_DIM


def _widen(x, cols):
    return jnp.concatenate([x] * (cols // LANES), axis=1)


def _both_halves(x, lo):
    r = pltpu.roll(x, HEAD_DIM, 1)
    return jnp.where(lo, x, r), jnp.where(lo, r, x)


MESH_ID = pl.DeviceIdType.MESH
HBM_SPEC = pl.BlockSpec(memory_space=pltpu.HBM)
VMEM_SPEC = pl.BlockSpec(memory_space=pltpu.VMEM)


def _mesh_place():
    x, y, c = lax.axis_index("x"), lax.axis_index("y"), lax.axis_index("c")
    return x, y, c, 4 * x + 2 * y + c


def _peers(x, y, c):
    out = []
    for mask in range(1, N_DEV):
        dx, dy, dc = (mask >> 2) & 1, (mask >> 1) & 1, mask & 1
        px, py, pc = (1 - x if dx else x), (1 - y if dy else y), (1 - c if dc else c)
        out.append(((px, py, pc), 4 * px + 2 * py + pc))
    return out


def _comm_out_shapes(comm):
    return [jax.ShapeDtypeStruct(a.shape if kind == "scatter" else (N_DEV,) + a.shape[1:], a.dtype) for kind, a in comm]


def _comm_scratch(comm):
    n = len(comm)
    return [pltpu.SemaphoreType.DMA((n, 7)), pltpu.SemaphoreType.DMA((n, 7)), pltpu.SemaphoreType.DMA((n,))]


def _comm_copies(kinds, in_refs, out_refs, sems, place):
    send_sems, recv_sems, local_sems = sems
    x, y, c, me = place
    local, remote = [], []
    for w, kind in enumerate(kinds):
        mine = in_refs[w].at[me] if kind == "scatter" else in_refs[w].at[kind[1]]
        local.append(pltpu.make_async_copy(mine, out_refs[w].at[me], local_sems.at[w]))
        for k, (peer, peer_idx) in enumerate(_peers(x, y, c)):
            remote.append(pltpu.make_async_remote_copy(
                src_ref=in_refs[w].at[peer_idx] if kind == "scatter" else mine, dst_ref=out_refs[w].at[me],
                send_sem=send_sems.at[w, k], recv_sem=recv_sems.at[w, k], device_id=peer, device_id_type=MESH_ID))
    return local, remote


def _comm_start(kinds, in_refs, out_refs, sems, place):
    local, remote = _comm_copies(kinds, in_refs, out_refs, sems, place)
    for cp in local + remote:
        cp.start()


def _comm_wait(kinds, in_refs, out_refs, sems, place):
    local, remote = _comm_copies(kinds, in_refs, out_refs, sems, place)
    for cp in remote:
        cp.wait_recv()
    for cp in remote:
        cp.wait_send()
    for cp in local:
        cp.wait()


def _exchange(comm, *, name):
    n = len(comm)
    kinds = [k for k, _ in comm]

    def body(*refs):
        place = _mesh_place()
        _comm_start(kinds, refs[:n], refs[n:2 * n], refs[2 * n:], place)
        _comm_wait(kinds, refs[:n], refs[n:2 * n], refs[2 * n:], place)

    return pl.pallas_call(
        body, name=name, out_shape=_comm_out_shapes(comm), in_specs=[HBM_SPEC] * n, out_specs=[HBM_SPEC] * n,
        scratch_shapes=_comm_scratch(comm),
    )(*[a for _, a in comm])


def _flash_fwd(qa, ka, va, *, q_blk0, k_blk0, v_blk0, W, n_pairs, B, S, scale, csh=None, crow=None, comm=(), name):
    t = ATT_TILE
    nq = S // t
    P = PAIRS_PER_STEP_FWD
    decay = csh is not None
    split = W == LANES
    assert n_pairs % P == 0 and q_blk0 % P == 0 and k_blk0 % P == 0 and v_blk0 % P == 0
    n_c, kinds = len(comm), [k for k, _ in comm]
    n_in = 5 if decay else 3
    fold_scale = math.log2(scale).is_integer()
    n_steps = (B, n_pairs // P, nq)

    def body(*refs):
        c_in, c_out = refs[n_in:n_in + n_c], refs[n_in + n_c + 2:n_in + 2 * n_c + 2]
        sems = refs[n_in + 2 * n_c + 4:]
        refs = refs[:n_in] + refs[n_in + n_c:n_in + n_c + 2] + refs[n_in + 2 * n_c + 2:n_in + 2 * n_c + 4]
        if decay:
            q_ref, k_ref, v_ref, csh_ref, crow_ref, o_ref, lse_ref, m_s, acc_s = refs
        else:
            q_ref, k_ref, v_ref, o_ref, lse_ref, m_s, acc_s = refs
        g, i = pl.program_id(1), pl.program_id(2)
        if n_c:
            place = _mesh_place()
            ids = [pl.program_id(ax) for ax in range(3)]

            @pl.when((ids[0] == 0) & (ids[1] == 0) & (ids[2] == 0))
            def _():
                _comm_start(kinds, c_in, c_out, sems, place)

        lo = _low_half((t, LANES))
        qv = q_ref[...]
        qh = []
        for pr in range(P):
            qp = qv[:, pr * W:(pr + 1) * W]
            qh += [jnp.where(lo, qp, jnp.zeros_like(qp)), jnp.where(lo, jnp.zeros_like(qp), qp)] if split \
                else [qp[:, :LANES], qp[:, LANES:]]
        if fold_scale:
            qh = [x * scale for x in qh]
        if decay:
            cq = [jnp.broadcast_to(_head_column(csh_ref[...], 2 * P * g + hd), (t, LANES)) for hd in range(2 * P)]
        m_s[...] = jnp.full(m_s.shape, NEG_INF, F32)
        acc_s[...] = jnp.zeros(acc_s.shape, F32)

        def step(j, masked):
            rows = pl.ds(pl.multiple_of(j * t, t), t)
            kb, vb = k_ref[rows, :], v_ref[rows, :]
            for pr in range(P):
                kp, vp = kb[:, pr * W:(pr + 1) * W], vb[:, pr * LANES:(pr + 1) * LANES]
                ones = jnp.ones_like(vp)
                vaug = [jnp.where(lo, vp, ones), jnp.where(lo, ones, vp)]
                for half in range(2):
                    hd = 2 * pr + half
                    kh = kp if split else kp[:, half * LANES:(half + 1) * LANES]
                    s = lax.dot_general(qh[hd], kh, NT, preferred_element_type=F32)
                    if not fold_scale:
                        s = s * scale
                    if decay:
                        s = s + _widen(cq[hd], t) - crow_ref[hd, j]
                    if masked:
                        s = _causal_mask(s)
                    m_prev = m_s[hd]
                    m_new = jnp.maximum(m_prev, jnp.max(s, -1, keepdims=True))
                    p = jnp.exp(s - _widen(m_new, t))
                    acc_s[hd] = jnp.exp(m_prev - m_new) * acc_s[hd] + lax.dot_general(
                        p.astype(BF16), vaug[half], NN, preferred_element_type=F32)
                    m_s[hd] = m_new

        def loop_body(j, carry):
            step(j, False)
            return carry

        lax.fori_loop(0, i, loop_body, 0)
        step(i, True)
        for pr in range(P):
            acc0, acc1 = acc_s[2 * pr], acc_s[2 * pr + 1]
            _, l0 = _both_halves(acc0, lo)
            l1, _ = _both_halves(acc1, lo)
            cols = slice(pr * LANES, (pr + 1) * LANES)
            o_ref[:, cols] = jnp.where(lo, acc0 / l0, acc1 / l1).astype(BF16)
            lse_ref[:, cols] = jnp.where(lo, m_s[2 * pr] + jnp.log(l0), m_s[2 * pr + 1] + jnp.log(l1))
        if n_c:
            @pl.when((ids[0] == n_steps[0] - 1) & (ids[1] == n_steps[1] - 1) & (ids[2] == n_steps[2] - 1))
            def _():
                _comm_wait(kinds, c_in, c_out, sems, place)

    in_specs = [pl.BlockSpec((t, P * W), lambda b, g, i: (b * nq + i, q_blk0 // P + g)),
                pl.BlockSpec((S, P * W), lambda b, g, i: (b, k_blk0 // P + g)),
                pl.BlockSpec((S, P * LANES), lambda b, g, i: (b, v_blk0 // P + g))]
    args = [qa, ka, va]
    if decay:
        in_specs += [pl.BlockSpec((None, t, LANES), lambda b, g, i: (b, i, 0)),
                     pl.BlockSpec((None, 2 * P, nq, 1, t), lambda b, g, i: (b, g, 0, 0, 0))]
        args += [csh, crow]
    out_spec = pl.BlockSpec((t, P * LANES), lambda b, g, i: (b * nq + i, g))
    res = pl.pallas_call(
        body, name=name, grid=n_steps, in_specs=in_specs + [HBM_SPEC] * n_c,
        out_specs=[out_spec, out_spec] + [HBM_SPEC] * n_c,
        out_shape=[jax.ShapeDtypeStruct((B * S, n_pairs * LANES), BF16),
                   jax.ShapeDtypeStruct((B * S, n_pairs * LANES), F32)] + _comm_out_shapes(comm),
        scratch_shapes=[pltpu.VMEM((2 * P, t, LANES), F32), pltpu.VMEM((2 * P, t, LANES), F32)]
        + (_comm_scratch(comm) if n_c else []),
        compiler_params=_cparams(*(("arbitrary",) * 3 if n_c else ("parallel",) * 3)),
    )(*args, *[a for _, a in comm])
    return res[0], res[1], list(res[2:])


def _flash_bwd(qa, ka, va, oa, doa, lsea, *, q_blk0, k_blk0, v_blk0, do_blk0, W, n_pairs, B, S, scale, qk_dtype,
               csh=None, crow=None, comm=(), name):
    t = ATT_TILE_BWD
    nq = S // t
    P = PAIRS_PER_STEP_BWD
    decay = csh is not None
    if decay:
        crow = crow.reshape(B, 2 * n_pairs, nq, 1, t)
    split = W == LANES
    assert n_pairs % P == 0 and q_blk0 % P == 0 and k_blk0 % P == 0 and v_blk0 % P == 0 and do_blk0 % P == 0
    n_c, kinds = len(comm), [k for k, _ in comm]
    n_in, n_out, n_scr = (8, 5, 8) if decay else (6, 3, 5)
    n_steps = (B, n_pairs // P, nq)

    def body(*refs):
        c_in = refs[n_in:n_in + n_c]
        c_out = refs[n_in + n_c + n_out:n_in + 2 * n_c + n_out]
        sems = refs[n_in + 2 * n_c + n_out + n_scr:]
        refs = (refs[:n_in] + refs[n_in + n_c:n_in + n_c + n_out]
                + refs[n_in + 2 * n_c + n_out:n_in + 2 * n_c + n_out + n_scr])
        if n_c:
            place = _mesh_place()
            ids = [pl.program_id(ax) for ax in range(3)]

            @pl.when((ids[0] == 0) & (ids[1] == 0) & (ids[2] == 0))
            def _():
                _comm_start(kinds, c_in, c_out, sems, place)

        if decay:
            (q_ref, k_ref, v_ref, o_ref, do_ref, lse_ref, csh_ref, crow_ref, dq_ref, dk_ref, dv_ref, dck_ref, dcq_ref,
             dq_s, lse_s, delta_s, dk_s, dv_s, cq_s, dcq_s, dck_s) = refs
        else:
            (q_ref, k_ref, v_ref, o_ref, do_ref, lse_ref, dq_ref, dk_ref, dv_ref,
             dq_s, lse_s, delta_s, dk_s, dv_s) = refs
        g, j = pl.program_id(1), pl.program_id(2)
        lo = _low_half((t, LANES))

        @pl.when(j == 0)
        def _():
            lo_s = _low_half((S, LANES))
            dq_s[...] = jnp.zeros(dq_s.shape, F32)
            for pr in range(P):
                cols = slice(pr * LANES, (pr + 1) * LANES)
                lse_s[2 * pr], lse_s[2 * pr + 1] = _both_halves(lse_ref[:, cols], lo_s)
                dd = do_ref[:, cols].astype(F32) * o_ref[:, cols].astype(F32)
                delta_s[2 * pr] = jnp.broadcast_to(jnp.sum(jnp.where(lo_s, dd, 0.0), -1, keepdims=True), (S, LANES))
                delta_s[2 * pr + 1] = jnp.broadcast_to(jnp.sum(jnp.where(lo_s, 0.0, dd), -1, keepdims=True),
                                                       (S, LANES))
            if decay:
                for hd in range(2 * P):
                    cq_s[hd] = jnp.broadcast_to(_head_column(csh_ref[...], 2 * P * g + hd), (S, LANES))
                dcq_s[...] = jnp.zeros(dcq_s.shape, F32)

        kb, vb = k_ref[...], v_ref[...]
        kh, vh = [], []
        for pr in range(P):
            kp, vp = kb[:, pr * W:(pr + 1) * W], vb[:, pr * LANES:(pr + 1) * LANES]
            zk, zv = jnp.zeros_like(kp), jnp.zeros_like(vp)
            kh += [jnp.where(lo, kp, zk), jnp.where(lo, zk, kp)] if split else [kp[:, :LANES], kp[:, LANES:]]
            vh += [jnp.where(lo, vp, zv), jnp.where(lo, zv, vp)]
        dk_s[...] = jnp.zeros(dk_s.shape, F32)
        dv_s[...] = jnp.zeros(dv_s.shape, F32)
        if decay:
            dck_s[...] = jnp.zeros(dck_s.shape, F32)

        def step(i, masked):
            rows = pl.ds(pl.multiple_of(i * t, t), t)
            qi, doi = q_ref[rows, :], do_ref[rows, :]
            for pr in range(P):
                qp, dop = qi[:, pr * W:(pr + 1) * W], doi[:, pr * LANES:(pr + 1) * LANES]
                for half in range(2):
                    hd = 2 * pr + half
                    qx = qp if split else qp[:, half * LANES:(half + 1) * LANES]
                    s = lax.dot_general(qx, kh[hd], NT, preferred_element_type=F32) * scale
                    if decay:
                        s = s + _widen(cq_s[hd, rows, :], t) - crow_ref[hd, j]
                    if masked:
                        s = _causal_mask(s)
                    p = jnp.exp(s - _widen(lse_s[hd, rows, :], t))
                    dv_s[hd] += lax.dot_general(p.astype(BF16), dop, TN, preferred_element_type=F32)
                    dp = lax.dot_general(dop, vh[hd], NT, preferred_element_type=F32)
                    ds = p * (dp - _widen(delta_s[hd, rows, :], t))
                    dss = (ds * scale).astype(BF16)
                    dk_s[hd] += lax.dot_general(dss, qx, TN, preferred_element_type=F32)
                    dqc = lax.dot_general(dss, kh[hd], NN, preferred_element_type=F32)
                    if split:
                        dq_s[rows, pr * W:(pr + 1) * W] += dqc
                    else:
                        dq_s[rows, hd * LANES:(hd + 1) * LANES] += dqc
                    if decay:
                        dck_s[hd] -= jnp.sum(ds, 0, keepdims=True)
                        part = ds[:, :LANES]
                        for c in range(1, t // LANES):
                            part = part + ds[:, c * LANES:(c + 1) * LANES]
                        dcq_s[hd, rows, :] += part

        def loop_body(i, carry):
            step(i, False)
            return carry

        step(j, True)
        lax.fori_loop(j + 1, nq, loop_body, 0)
        for pr in range(P):
            if split:
                dk_ref[:, pr * W:(pr + 1) * W] = jnp.where(lo, dk_s[2 * pr], dk_s[2 * pr + 1]).astype(dk_ref.dtype)
            else:
                for half in range(2):
                    hd = 2 * pr + half
                    dk_ref[:, hd * LANES:(hd + 1) * LANES] = dk_s[hd].astype(dk_ref.dtype)
            dv_ref[:, pr * LANES:(pr + 1) * LANES] = jnp.where(lo, dv_s[2 * pr], dv_s[2 * pr + 1]).astype(BF16)
        if decay:
            dck_ref[...] = dck_s[...]

        @pl.when(j == nq - 1)
        def _():
            dq_ref[...] = dq_s[...].astype(dq_ref.dtype)
            if decay:
                for hd in range(2 * P):
                    dcq_ref[hd] = jnp.sum(dcq_s[hd].T, 0, keepdims=True)

        if n_c:
            @pl.when((ids[0] == n_steps[0] - 1) & (ids[1] == n_steps[1] - 1) & (ids[2] == n_steps[2] - 1))
            def _():
                _comm_wait(kinds, c_in, c_out, sems, place)

    full = lambda w, blk0: pl.BlockSpec((S, P * w), lambda b, g, j: (b, blk0 // P + g))
    blk = lambda w, blk0: pl.BlockSpec((t, P * w), lambda b, g, j: (b * nq + j, blk0 // P + g))
    in_specs = [full(W, q_blk0), blk(W, k_blk0), blk(LANES, v_blk0), full(LANES, 0), full(LANES, do_blk0),
                full(LANES, 0)]
    args = [qa, ka, va, oa, doa, lsea]
    T = B * S
    out_specs = [full(W, 0), blk(W, 0), blk(LANES, 0)]
    out_shape = [jax.ShapeDtypeStruct((T, n_pairs * W), qk_dtype), jax.ShapeDtypeStruct((T, n_pairs * W), qk_dtype),
                 jax.ShapeDtypeStruct((T, n_pairs * LANES), BF16)]
    per_head = lambda rows: pltpu.VMEM((2 * P, rows, LANES), F32)
    scratch = [pltpu.VMEM((S, P * W), F32), per_head(S), per_head(S), per_head(t), per_head(t)]
    if decay:
        in_specs += [pl.BlockSpec((None, S, LANES), lambda b, g, j: (b, 0, 0)),
                     pl.BlockSpec((None, 2 * P, nq, 1, t), lambda b, g, j: (b, g, 0, 0, 0))]
        args += [csh, crow]
        out_specs += [pl.BlockSpec((None, 2 * P, None, 1, t), lambda b, g, j: (b, g, j, 0, 0)),
                      pl.BlockSpec((None, 2 * P, 1, S), lambda b, g, j: (b, g, 0, 0))]
        out_shape += [jax.ShapeDtypeStruct((B, 2 * n_pairs, nq, 1, t), F32),
                      jax.ShapeDtypeStruct((B, 2 * n_pairs, 1, S), F32)]
        scratch += [per_head(S), per_head(S), pltpu.VMEM((2 * P, 1, t), F32)]
    res = pl.pallas_call(
        body, name=name, grid=n_steps, in_specs=in_specs + [HBM_SPEC] * n_c,
        out_specs=out_specs + [HBM_SPEC] * n_c, out_shape=out_shape + _comm_out_shapes(comm),
        scratch_shapes=scratch + (_comm_scratch(comm) if n_c else []),
        compiler_params=_cparams(*(("arbitrary",) * 3 if n_c else ("parallel", "parallel", "arbitrary"))),
    )(*args, *[a for _, a in comm])
    return tuple(res[:n_out]) + (list(res[n_out:]),)


def _swa_common(q_ref, kp_ref, ko_ref, vp_ref, vo_ref, n):
    Q = BLOCK_Q
    lo = _low_half((Q, LANES))
    lo2 = _low_half((2 * Q, LANES))
    kk = jnp.concatenate([kp_ref[...], ko_ref[...]], axis=0)
    vv = jnp.concatenate([vp_ref[...], vo_ref[...]], axis=0)
    kdup = [x.astype(BF16) for x in _both_halves(kk, lo2)]
    vdup = [x.astype(BF16) for x in _both_halves(vv, lo2)]
    a = lax.broadcasted_iota(jnp.int32, (SWA_GROUP * Q, 2 * Q), 0) % Q
    col = lax.broadcasted_iota(jnp.int32, (SWA_GROUP * Q, 2 * Q), 1)
    dist = a + Q - col
    valid = (dist >= 0) & (dist < SWA_WINDOW) & ((col >= Q) | (n > 0))
    qv = q_ref[...]
    qm = []
    for a_head in range(SWA_HEADS):
        qp = qv[:, (a_head // 2) * LANES:(a_head // 2 + 1) * LANES]
        keep = lo if a_head % 2 == 0 else jnp.logical_not(lo)
        qm.append(jnp.where(keep, qp, 0.0).astype(BF16))
    qs = [jnp.concatenate(qm[g * SWA_GROUP:(g + 1) * SWA_GROUP], axis=0) for g in range(SWA_KV_HEADS)]
    return lo, lo2, kdup, vdup, valid, qs


def _swa_group_logits(g, qs, kdup, valid, bias_ref):
    heads = slice(g * SWA_GROUP, (g + 1) * SWA_GROUP)
    s = lax.dot_general(qs[g], kdup[g], NT, preferred_element_type=F32) * (HEAD_DIM ** -0.5)
    s = s + bias_ref[heads].reshape(SWA_GROUP * BLOCK_Q, 2 * BLOCK_Q)
    return heads, jnp.where(valid, s, NEG_INF)


def _pair_halves(x, lo):
    Q = BLOCK_Q
    return [jnp.where(lo, x[2 * pr * Q:(2 * pr + 1) * Q], x[(2 * pr + 1) * Q:(2 * pr + 2) * Q])
            for pr in range(SWA_GROUP // 2)]


def _swa_in_specs(nb):
    Q = BLOCK_Q
    own = lambda blk: (lambda b, n: (b * nb + n, blk))
    prev = lambda blk: (lambda b, n: (b * nb + jnp.maximum(n - 1, 0), blk))
    kb, vb = EV_KS[0] // LANES, EV_VS[0] // LANES
    return [pl.BlockSpec((Q, SWA_HEADS * HEAD_DIM), own(0)), pl.BlockSpec((Q, LANES), prev(kb)),
            pl.BlockSpec((Q, LANES), own(kb)), pl.BlockSpec((Q, LANES), prev(vb)), pl.BlockSpec((Q, LANES), own(vb))]


def _swa_fwd(h, bias, sinkcol, *, B, S, comm=(), name):
    Q = BLOCK_Q
    nb = S // Q
    n_c, kinds = len(comm), [k for k, _ in comm]

    def body(*refs):
        c_in, c_out, sems = refs[7:7 + n_c], refs[9 + n_c:9 + 2 * n_c], refs[9 + 2 * n_c:]
        q_ref, kp_ref, ko_ref, vp_ref, vo_ref, bias_ref, sink_ref = refs[:7]
        o_ref, lse_ref = refs[7 + n_c:9 + n_c]
        if n_c:
            place = _mesh_place()
            ids = [pl.program_id(0), pl.program_id(1)]

            @pl.when((ids[0] == 0) & (ids[1] == 0))
            def _():
                _comm_start(kinds, c_in, c_out, sems, place)

        lo, lo2, kdup, vdup, valid, qs = _swa_common(q_ref, kp_ref, ko_ref, vp_ref, vo_ref, pl.program_id(1))
        pairs = []
        lo4 = _low_half((SWA_GROUP * Q, LANES))
        for g in range(SWA_KV_HEADS):
            heads, s = _swa_group_logits(g, qs, kdup, valid, bias_ref)
            sink = jnp.broadcast_to(sink_ref[heads].reshape(SWA_GROUP * Q, 1), (SWA_GROUP * Q, LANES))
            m = jnp.maximum(jnp.max(s, -1, keepdims=True), sink)
            p = jnp.exp(s - _widen(m, 2 * Q))
            vaug = jnp.where(lo2, vdup[g], jnp.ones_like(vdup[g]))
            pv = lax.dot_general(p.astype(BF16), vaug, NN, preferred_element_type=F32)
            rolled = pltpu.roll(pv, HEAD_DIM, 1)
            l = jnp.where(lo4, rolled, pv) + jnp.exp(sink - m)
            out = pv / l
            lse_g = m + jnp.log(l)
            for i in range(SWA_GROUP):
                a = g * SWA_GROUP + i
                lse_ref[:, a * LANES:(a + 1) * LANES] = lse_g[i * Q:(i + 1) * Q]
            shifted = pltpu.roll(out, HEAD_DIM, 1)
            pairs += [jnp.where(lo, out[2 * pr * Q:(2 * pr + 1) * Q], shifted[(2 * pr + 1) * Q:(2 * pr + 2) * Q])
                      for pr in range(SWA_GROUP // 2)]
        o_ref[...] = jnp.concatenate(pairs, axis=1).astype(BF16)
        if n_c:
            @pl.when((ids[0] == B - 1) & (ids[1] == nb - 1))
            def _():
                _comm_wait(kinds, c_in, c_out, sems, place)

    whole = lambda shape: pl.BlockSpec(shape, lambda b, n: (0,) * len(shape))
    res = pl.pallas_call(
        body, name=name, grid=(B, nb),
        in_specs=_swa_in_specs(nb) + [whole((SWA_HEADS, Q, 2 * Q)), whole((SWA_HEADS, Q, 1))] + [HBM_SPEC] * n_c,
        out_specs=[pl.BlockSpec((Q, SWA_HEADS * HEAD_DIM), lambda b, n: (b * nb + n, 0)),
                   pl.BlockSpec((Q, SWA_HEADS * LANES), lambda b, n: (b * nb + n, 0))] + [HBM_SPEC] * n_c,
        out_shape=[jax.ShapeDtypeStruct((B * S, SWA_HEADS * HEAD_DIM), BF16),
                   jax.ShapeDtypeStruct((B * S, SWA_HEADS * LANES), F32)] + _comm_out_shapes(comm),
        scratch_shapes=_comm_scratch(comm) if n_c else [],
        compiler_params=_cparams(*(("arbitrary",) * 2 if n_c else ("parallel",) * 2)),
    )(h, h, h, h, h, bias, sinkcol, *[a for _, a in comm])
    return res[0], res[1], list(res[2:])


def _swa_bwd(h, o, do, lse, bias, sinkcol, *, do_blk0, B, S, comm=(), name):
    Q = BLOCK_Q
    nb = S // Q
    scale = HEAD_DIM ** -0.5
    n_c, kinds = len(comm), [k for k, _ in comm]

    def body(*refs):
        c_in, c_out, sems = refs[10:10 + n_c], refs[17 + n_c:17 + 2 * n_c], refs[17 + 2 * n_c:]
        q_ref, kp_ref, ko_ref, vp_ref, vo_ref, o_ref, do_ref, lse_ref, bias_ref, sink_ref = refs[:10]
        dq_ref, dko_ref, dkp_ref, dvo_ref, dvp_ref, dbias_ref, dsink_ref = refs[10 + n_c:17 + n_c]
        ids = [pl.program_id(0), pl.program_id(1)]
        if n_c:
            place = _mesh_place()

        @pl.when((ids[0] == 0) & (ids[1] == 0))
        def _():
            dbias_ref[...] = jnp.zeros_like(dbias_ref)
            dsink_ref[...] = jnp.zeros_like(dsink_ref)
            if n_c:
                _comm_start(kinds, c_in, c_out, sems, place)

        lo, lo2, kdup, vdup, valid, qs = _swa_common(q_ref, kp_ref, ko_ref, vp_ref, vo_ref, pl.program_id(1))
        dkk, dvv, dq_pairs = [], [], []
        for g in range(SWA_KV_HEADS):
            heads, s = _swa_group_logits(g, qs, kdup, valid, bias_ref)
            lse_g = jnp.concatenate([lse_ref[:, a * LANES:(a + 1) * LANES]
                                     for a in range(g * SWA_GROUP, (g + 1) * SWA_GROUP)], axis=0)
            p = jnp.exp(s - _widen(lse_g, 2 * Q))
            do_g, o_g = [], []
            for i in range(SWA_GROUP):
                cols = slice((g * SWA_GROUP + i) // 2 * LANES, ((g * SWA_GROUP + i) // 2 + 1) * LANES)
                do_p = do_ref[:, cols]
                do_g.append(jnp.where(lo if i % 2 == 0 else jnp.logical_not(lo), do_p, jnp.zeros_like(do_p)))
                o_g.append(o_ref[:, cols])
            doh, oh = jnp.concatenate(do_g, axis=0), jnp.concatenate(o_g, axis=0)
            delta = jnp.sum(doh.astype(F32) * oh.astype(F32), -1, keepdims=True)
            dp = lax.dot_general(doh, vdup[g], NT, preferred_element_type=F32)
            ds = p * (dp - delta)
            dbias_ref[heads] += ds.reshape(SWA_GROUP, Q, 2 * Q)
            dsink_ref[heads] -= (jnp.exp(sink_ref[heads].reshape(SWA_GROUP * Q, 1) - lse_g[:, :1])
                                 * delta).reshape(SWA_GROUP, Q, 1)
            dss = (ds * scale).astype(BF16)
            dq_pairs += _pair_halves(lax.dot_general(dss, kdup[g], NN, preferred_element_type=F32), lo)
            dkk.append(lax.dot_general(dss, qs[g], TN, preferred_element_type=F32))
            dvv.append(lax.dot_general(p.astype(BF16), doh, TN, preferred_element_type=F32))
        dq_ref[...] = jnp.concatenate(dq_pairs, axis=1).astype(BF16)
        fold = lambda x: x + pltpu.roll(x, HEAD_DIM, 1)
        dk_blk = jnp.where(lo2, fold(dkk[0]), fold(dkk[1]))
        dv_blk = jnp.where(lo2, fold(dvv[0]), fold(dvv[1]))
        dkp_ref[...] = dk_blk[:Q]
        dko_ref[...] = dk_blk[Q:]
        dvp_ref[...] = dv_blk[:Q]
        dvo_ref[...] = dv_blk[Q:]
        if n_c:
            @pl.when((ids[0] == B - 1) & (ids[1] == nb - 1))
            def _():
                _comm_wait(kinds, c_in, c_out, sems, place)

    whole = lambda shape: pl.BlockSpec(shape, lambda b, n: (0,) * len(shape))
    wide = lambda blk: pl.BlockSpec((Q, SWA_HEADS * HEAD_DIM), lambda b, n: (b * nb + n, blk))
    narrow = pl.BlockSpec((Q, LANES), lambda b, n: (b * nb + n, 0))
    kv_shape = jax.ShapeDtypeStruct((B * S, LANES), F32)
    res = pl.pallas_call(
        body, name=name, grid=(B, nb),
        in_specs=_swa_in_specs(nb) + [wide(0), wide(do_blk0),
                                      pl.BlockSpec((Q, SWA_HEADS * LANES), lambda b, n: (b * nb + n, 0)),
                                      whole((SWA_HEADS, Q, 2 * Q)),
                                      whole((SWA_HEADS, Q, 1))] + [HBM_SPEC] * n_c,
        out_specs=[wide(0), narrow, narrow, narrow, narrow, whole((SWA_HEADS, Q, 2 * Q)), whole((SWA_HEADS, Q, 1))]
        + [HBM_SPEC] * n_c,
        out_shape=[jax.ShapeDtypeStruct((B * S, SWA_HEADS * HEAD_DIM), BF16), kv_shape, kv_shape, kv_shape, kv_shape,
                   jax.ShapeDtypeStruct((SWA_HEADS, Q, 2 * Q), F32), jax.ShapeDtypeStruct((SWA_HEADS, Q, 1), F32)]
        + _comm_out_shapes(comm),
        scratch_shapes=_comm_scratch(comm) if n_c else [],
        compiler_params=_cparams("arbitrary", "arbitrary"),
    )(h, h, h, h, h, o, do, lse, bias, sinkcol, *[a for _, a in comm])
    return tuple(res[:7]) + (list(res[7:]),)


def _bias_bucket_sum(dbias, bucket, *, name):
    def body(d_ref, b_ref, o_ref):
        dbv, bk = d_ref[...], b_ref[...]
        lane = lax.broadcasted_iota(jnp.int32, (SWA_HEADS, LANES), 1)
        out = jnp.zeros((SWA_HEADS, LANES), F32)
        for b in range(REL_BUCKETS):
            part = jnp.sum(jnp.where(bk == b, dbv, 0.0), axis=1)
            tot = jnp.sum(part, axis=-1, keepdims=True)
            out = out + jnp.where(lane == b, tot, 0.0)
        o_ref[...] = out

    return pl.pallas_call(
        body, name=name, out_shape=jax.ShapeDtypeStruct((SWA_HEADS, LANES), F32),
        compiler_params=pltpu.CompilerParams(vmem_limit_bytes=VMEM_LIMIT_BYTES),
    )(dbias, bucket)


def _adamw_update(w, g, m, v):
    m_new = ADAM_B1 * m + (1.0 - ADAM_B1) * g
    v_new = ADAM_B2 * v + (1.0 - ADAM_B2) * jnp.square(g)
    m_hat = m_new / (1.0 - ADAM_B1 ** ADAM_STEP)
    v_hat = v_new / (1.0 - ADAM_B2 ** ADAM_STEP)
    return -ADAM_LR * (m_hat / (jnp.sqrt(v_hat) + ADAM_EPS) + ADAM_WD * w), m_new, v_new


def _adamw(w, g, m, v, *, name):
    def body(w_ref, g_ref, m_ref, v_ref, d_ref, nm_ref, nv_ref):
        d_ref[...], nm_ref[...], nv_ref[...] = _adamw_update(w_ref[...], g_ref[...], m_ref[...], v_ref[...])

    return pl.pallas_call(
        body, name=name, out_shape=[jax.ShapeDtypeStruct(w.shape, F32)] * 3,
        compiler_params=pltpu.CompilerParams(vmem_limit_bytes=VMEM_LIMIT_BYTES),
    )(w, g, m, v)


ADAMW_PARTS_BYTES = 8 * 1024 * 1024


def _adamw_slots(w, parts, m, v, *, name):
    n0, R, C = w.shape
    tr = next((c for c in (512, 256, 128, 64, 32, 16, 8) if R % c == 0 and 4 * n0 * N_DEV * c * C <= ADAMW_PARTS_BYTES), R)

    def body(*refs):
        w_ref, p_refs, (m_ref, v_ref, g_ref, d_ref, nm_ref, nv_ref) = refs[0], refs[1:1 + n0], refs[1 + n0:]
        layer = pl.program_id(0)
        for l in range(n0):
            @pl.when(layer == l)
            def _(p_ref=p_refs[l]):
                g = p_ref[0].astype(F32)
                for j in range(1, N_DEV):
                    g = g + p_ref[j].astype(F32)
                g_ref[...] = g
                d_ref[...], nm_ref[...], nv_ref[...] = _adamw_update(w_ref[...], g, m_ref[...], v_ref[...])

    spec = pl.BlockSpec((None, tr, C), lambda l, i: (l, i, 0))
    part_spec = lambda own: pl.BlockSpec((N_DEV, tr, C), lambda l, i: (0, jnp.where(l == own, i, 0), 0))
    return pl.pallas_call(
        body, name=name, grid=(n0, R // tr),
        in_specs=[spec] + [part_spec(l) for l in range(n0)] + [spec, spec], out_specs=[spec] * 4,
        out_shape=[jax.ShapeDtypeStruct((n0, R, C), F32)] * 4, compiler_params=_cparams("arbitrary", "arbitrary"),
    )(w, *parts, m, v)


def _all_gather_hbm(blocks, *, name):
    n = len(blocks)

    def body(*refs):
        x_refs, out_refs = refs[:n], refs[n:2 * n]
        send_sems, recv_sems, local_sems = refs[2 * n:]
        x, y, c, _ = _mesh_place()
        me, sibling = (x, y, c), (x, y, 1 - c)
        chips = [(1 - x, y), (x, 1 - y), (1 - x, 1 - y)]

        def copy(w, k, blk, to, src=None):
            px, py, pc = blk
            slot = out_refs[w].at[4 * px + 2 * py + pc]
            return pltpu.make_async_remote_copy(
                src_ref=slot if src is None else src, dst_ref=slot,
                send_sem=send_sems.at[w, k], recv_sem=recv_sems.at[w, k], device_id=to, device_id_type=MESH_ID)

        mine = [pltpu.make_async_copy(x_refs[w], out_refs[w].at[4 * x + 2 * y + c], local_sems.at[w])
                for w in range(n)]
        for cp in mine:
            cp.start()
        first = []
        for w in range(n):
            first.append(copy(w, 0, me, sibling, src=x_refs[w]))
            first += [copy(w, 1 + j, me, (*chip, c), src=x_refs[w]) for j, chip in enumerate(chips)]
        for cp in first:
            cp.start()
        passed = []
        for j, chip in enumerate(chips):
            for w in range(n):
                copy(w, 1 + j, (*chip, c), me).wait_recv()
                fwd = copy(w, 4 + j, (*chip, c), sibling)
                fwd.start()
                passed.append(fwd)
        for w in range(n):
            copy(w, 0, sibling, me).wait_recv()
            for j, chip in enumerate(chips):
                copy(w, 4 + j, (*chip, 1 - c), me).wait_recv()
        for cp in first + passed:
            cp.wait_send()
        for cp in mine:
            cp.wait()

    return pl.pallas_call(
        body, name=name, out_shape=[jax.ShapeDtypeStruct((N_DEV,) + b.shape, b.dtype) for b in blocks],
        in_specs=[HBM_SPEC] * n, out_specs=[HBM_SPEC] * n,
        scratch_shapes=[pltpu.SemaphoreType.DMA((n, 7)), pltpu.SemaphoreType.DMA((n, 7)),
                        pltpu.SemaphoreType.DMA((n,))],
    )(*blocks)


def _all_reduce_small(block, *, name):
    R, W = block.shape

    def body(x_ref, out_ref, buf, send_sems, recv_sems):
        x, y, c, me = _mesh_place()
        copies = []
        for k, (peer, _) in enumerate(_peers(x, y, c)):
            copies.append(pltpu.make_async_remote_copy(
                src_ref=x_ref, dst_ref=buf.at[me], send_sem=send_sems.at[k], recv_sem=recv_sems.at[k],
                device_id=peer, device_id_type=MESH_ID))
        for cp in copies:
            cp.start()
        buf[me] = x_ref[...]
        for cp in copies:
            cp.wait_recv()
        for cp in copies:
            cp.wait_send()
        acc = buf[0]
        for j in range(1, N_DEV):
            acc = acc + buf[j]
        out_ref[...] = acc

    return pl.pallas_call(
        body, name=name, out_shape=jax.ShapeDtypeStruct((R, W), F32),
        in_specs=[VMEM_SPEC], out_specs=VMEM_SPEC,
        scratch_shapes=[pltpu.VMEM((N_DEV, R, W), F32), pltpu.SemaphoreType.DMA((7,)), pltpu.SemaphoreType.DMA((7,))],
    )(block)


def _assemble(name, g):
    if BIG_AXIS[name] == 2:
        return jnp.concatenate([g[j] for j in range(N_DEV)], axis=1)
    return g.reshape(N_DEV * g.shape[1], g.shape[2])


def _split_for_devices(name, g):
    if g.ndim == 3:
        return g
    if BIG_AXIS[name] == 2:
        b = g.shape[1] // N_DEV
        return jnp.stack([g[:, j * b:(j + 1) * b] for j in range(N_DEV)]).astype(BF16)
    return g.reshape(N_DEV, g.shape[0] // N_DEV, g.shape[1]).astype(BF16)


def _layer_weight_keys(i):
    j = i // 2
    mixer = [('ev_w_in', j), ('ev_w_uq', j), ('ev_w_ukv', j), ('ev_w_out', j)] if i % 2 == 0 \
        else [('od_w_in', j), ('od_w_out', j)]
    return mixer + [('w_up', i), ('w_down', i), ('ple_w_proj', i), ('ple_w_gate', i)]


def _weight_layer(key):
    name, idx = key
    return 2 * idx if name.startswith('ev_') else 2 * idx + 1 if name.startswith('od_') else idx


FIRST_GATHER = [('ev_w_in', 0), ('ev_w_uq', 0), ('ev_w_ukv', 0), ('ev_w_out', 0)]
FWD_CARRIERS = {
    'l0_mla': [('w_up', 0), ('ple_w_proj', 0), ('ple_w_gate', 0)],
    'l0_swa': [('w_down', 0)],
    'l0_out_ln1': [('od_w_out', 0)],
    'l0_up': [('od_w_in', 0)],
    'l0_down_ln2': [('w_up', 1)],
    'l0_ple_gate': [('ple_w_proj', 1), ('ple_w_gate', 1)],
    'l1_fox': [('w_down', 1), ('ev_w_in', 1), ('ev_w_uq', 1), ('ev_w_ukv', 1), ('ev_w_out', 1), ('w_up', 2)],
    'l1_up': [('w_down', 2)],
    'l1_down_ln2': [('ple_w_proj', 2), ('ple_w_gate', 2)],
    'l2_mla': [('od_w_in', 1), ('od_w_out', 1)],
    'l2_swa': [('w_up', 3)],
    'l2_up': [('w_down', 3)],
    'l2_down_ln2': [('ple_w_proj', 3), ('ple_w_gate', 3)],
}


class _MeshExchange:
    def __init__(self, shards):
        self.shards = shards
        self.weights = {i: {} for i in range(DEPTH)}
        self.pending = []
        self.in_flight = []
        self.received = {}
        got = _all_gather_hbm([self.shards[n][idx] for n, idx in FIRST_GATHER], name="gather_first")
        self._landed(FIRST_GATHER, got)

    def _landed(self, keys, gathered):
        for k, g in zip(keys, gathered):
            self.weights[_weight_layer(k)][k[0]] = _assemble(k[0], g)

    def layer_weights(self, i):
        return self.weights[i]

    def carry(self, kernel_name):
        return [(("gather", idx), self.shards[n]) for n, idx in FWD_CARRIERS.get(kernel_name, [])]

    def carried(self, kernel_name, outs):
        self._landed(FWD_CARRIERS.get(kernel_name, []), outs)

    def push_grads(self, grads):
        self.pending += [(k, _split_for_devices(k[0], g)) for k, g in grads.items()]

    def bwd_items(self):
        self.in_flight, self.pending = self.pending, []
        return [("scatter", parts) for _, parts in self.in_flight]

    def bwd_done(self, outs):
        for (k, _), got in zip(self.in_flight, outs):
            self.received[k] = got
        self.in_flight = []

    def finish(self):
        if self.pending:
            outs = _exchange(self.bwd_items(), name="scatter_rest")
            self.bwd_done(outs)
        return self.received


PACK_ROWS = 8


def _pack_small(vals):
    flat = jnp.concatenate([vals[n].reshape(-1).astype(F32) for n in SMALL])
    pad = (-flat.shape[0]) % (PACK_ROWS * LANES)
    return jnp.pad(flat, (0, pad)).reshape(-1, LANES)


def _unpack_small(block, shapes):
    flat = block.reshape(-1)
    out, off = {}, 0
    for n in SMALL:
        sz = math.prod(shapes[n])
        out[n] = flat[off:off + sz].reshape(shapes[n])
        off += sz
    return out


def _rope_tables(S):
    half = MLA_ROPE // 2
    inv = 1.0 / (ROPE_THETA ** (jnp.arange(0, MLA_ROPE, 2, dtype=F32) / MLA_ROPE))
    ang = jnp.arange(S, dtype=F32)[:, None] * inv[None, :]
    cos, sin = jnp.cos(ang), jnp.sin(ang)
    zeros = jnp.zeros((S, half), F32)
    tail = jnp.zeros((S, LANES - MLA_QK), F32)

    def block(rope_part, nope_val):
        return jnp.concatenate([jnp.full((S, MLA_NOPE), nope_val, F32), rope_part, tail], -1)

    a_r = jnp.concatenate([cos, cos], -1)
    bm_r = jnp.concatenate([-sin, zeros], -1)
    bp_r = jnp.concatenate([zeros, sin], -1)
    q_tabs = tuple(block(r, v) for r, v in ((a_r, 1.0), (bm_r, 0.0), (bp_r, 0.0)))
    k_tabs = tuple(block(r, 0.0) for r in (a_r, bm_r, bp_r))
    return q_tabs, k_tabs


def _t5_bucket(dist):
    exact = REL_BUCKETS // 2
    d = jnp.maximum(dist, 1).astype(F32)
    large = exact + (jnp.log(d / exact) / math.log(REL_MAX_DIST / exact) * (REL_BUCKETS - exact)).astype(jnp.int32)
    large = jnp.minimum(large, REL_BUCKETS - 1)
    return jnp.where(dist < exact, dist, large)


def _swa_bucket_table():
    a = jnp.arange(BLOCK_Q)[:, None]
    col = jnp.arange(2 * BLOCK_Q)[None, :]
    return _t5_bucket(jnp.maximum(a + BLOCK_Q - col, 0)).astype(jnp.int32)


def _even_weights(W):
    w = W['ev_w_in']
    c_kv1 = MLA_Q_LORA + MLA_KV_LORA
    c_kr1 = c_kv1 + MLA_ROPE
    c_qs1 = c_kr1 + SWA_HEADS * HEAD_DIM
    zeros = lambda n: jnp.zeros((D_MODEL, n), w.dtype)
    w_in = jnp.concatenate([w[:, c_kr1:c_qs1], w[:, :c_kv1], w[:, c_qs1:], zeros(KR_LANE0), w[:, c_kv1:c_kr1],
                            zeros(LANES - KR_LANE0 - MLA_ROPE)], axis=1)
    uq = W['ev_w_uq'].reshape(MLA_Q_LORA, MLA_HEADS, MLA_QK)
    w_uq = jnp.pad(uq, ((0, 0), (0, 0), (0, LANES - MLA_QK))).reshape(MLA_Q_LORA, MLA_HEADS * LANES)
    ukv = W['ev_w_ukv'].reshape(MLA_KV_LORA, MLA_HEADS, MLA_NOPE + MLA_V)
    w_k = jnp.pad(ukv[..., :MLA_NOPE], ((0, 0), (0, 0), (0, LANES - MLA_NOPE))).reshape(MLA_KV_LORA, -1)
    w_v = ukv[..., MLA_NOPE:].reshape(MLA_KV_LORA, MLA_HEADS * MLA_V)
    return w_in, w_uq, w_k, w_v, W['ev_w_out']


def _even_in_grad_unpad(dw):
    kr0 = EV_KR[0] + KR_LANE0
    return jnp.concatenate([dw[:, EV_CQ[0]:EV_CKV[1]], dw[:, kr0:kr0 + MLA_ROPE], dw[:, EV_QS[0]:EV_QS[1]],
                            dw[:, EV_KS[0]:EV_VS[1]]], axis=1)


def _even_fwd(xb, W, P, i, B, S, tabs, xchg, tag):
    j = i // 2
    q_tabs, k_tabs, bias, sinkcol = tabs
    w_in, w_uq, w_k, w_v, w_out = _even_weights(W)
    h = _mm(xb, w_in, name=f"{tag}_in")
    cqn, ckvn, rq, rkv = _even_norms(h, P['ev_q_norm'][j][None], P['ev_kv_norm'][j][None], name=f"{tag}_norms")
    q = _rope(_mm(cqn, w_uq, name=f"{tag}_uq"), q_tabs, S, sign=1.0, name=f"{tag}_ropeq")
    knp = _mm(ckvn, w_k, out_dtypes=(BF16,), name=f"{tag}_uk")
    v = _mm(ckvn, w_v, out_dtypes=(BF16,), name=f"{tag}_uv")
    k = _mla_keys(knp, h, k_tabs, S, name=f"{tag}_keys")
    o_mla, lse_mla, got = _flash_fwd(q, k, v, q_blk0=0, k_blk0=0, v_blk0=0, W=2 * LANES, n_pairs=MLA_HEADS // 2,
                                     B=B, S=S, scale=MLA_QK ** -0.5, comm=xchg.carry(f"{tag}_mla"), name=f"{tag}_mla")
    xchg.carried(f"{tag}_mla", got)
    o_swa, lse_swa, got = _swa_fwd(h, bias, sinkcol, B=B, S=S, comm=xchg.carry(f"{tag}_swa"), name=f"{tag}_swa")
    xchg.carried(f"{tag}_swa", got)
    res = dict(h=h, cqn=cqn, ckvn=ckvn, rq=rq, rkv=rkv, q=q, k=k, v=v, o_mla=o_mla, lse_mla=lse_mla,
               o_swa=o_swa, lse_swa=lse_swa)
    return ((o_mla, o_swa), w_out), res


def _shift_prev(own, prev, B, S):
    prev = prev.reshape(B, S, LANES)
    shifted = jnp.concatenate([prev[:, BLOCK_Q:], jnp.zeros_like(prev[:, :BLOCK_Q])], axis=1)
    return (own + shifted.reshape(B * S, LANES)).astype(BF16)


def _even_bwd(dmb, dz1, xb, W, P, j, B, S, tabs, res, xchg, tag):
    q_tabs, k_tabs, bias, sinkcol = tabs
    w_in, w_uq, w_k, w_v, w_out = _even_weights(W)
    g = {}
    g['ev_w_out'] = jnp.concatenate([_mm_tn(res['o_mla'], dmb, name=f"{tag}_dwout_mla"),
                                     _mm_tn(res['o_swa'], dmb, name=f"{tag}_dwout_swa")], axis=0)
    do = _mm(dmb, w_out, trans_b=True, out_dtypes=(BF16,), name=f"{tag}_do")
    dq, dk, dv, got = _flash_bwd(res['q'], res['k'], res['v'], res['o_mla'], do, res['lse_mla'], q_blk0=0, k_blk0=0,
                                 v_blk0=0, do_blk0=0, W=2 * LANES, n_pairs=MLA_HEADS // 2, B=B, S=S,
                                 scale=MLA_QK ** -0.5, qk_dtype=F32, comm=xchg.bwd_items(), name=f"{tag}_mla_bwd")
    xchg.bwd_done(got)
    dq_pre = _rope(dq, q_tabs, S, sign=-1.0, name=f"{tag}_ropeq_bwd")
    dw_uq = _mm_tn(res['cqn'], dq_pre, name=f"{tag}_dwuq")
    g['ev_w_uq'] = dw_uq.reshape(MLA_Q_LORA, MLA_HEADS, LANES)[..., :MLA_QK].reshape(MLA_Q_LORA, MLA_HEADS * MLA_QK)
    dcqn = _mm(dq_pre, w_uq, trans_b=True, name=f"{tag}_dcqn")
    dw_k = _mm_tn(res['ckvn'], dk, name=f"{tag}_dwuk").reshape(MLA_KV_LORA, MLA_HEADS, LANES)[..., :MLA_NOPE]
    dw_v = _mm_tn(res['ckvn'], dv, name=f"{tag}_dwuv").reshape(MLA_KV_LORA, MLA_HEADS, MLA_V)
    g['ev_w_ukv'] = jnp.concatenate([dw_k, dw_v], axis=-1).reshape(MLA_KV_LORA, MLA_HEADS * (MLA_NOPE + MLA_V))
    dckvn_v = _mm(dv, w_v, trans_b=True, name=f"{tag}_dckvn_v")
    dckvn = _mm(dk, w_k, trans_b=True, extras=(dckvn_v,), epilogue=lambda acc, r: (acc + r,), name=f"{tag}_dckvn")
    dkr_pre = _mla_rope_key_grad(dk, k_tabs, S, name=f"{tag}_ropek_bwd")
    xchg.push_grads({(n, j): g.pop(n) for n in list(g)})
    dqs, dko, dkp, dvo, dvp, dbias, dsink, got = _swa_bwd(res['h'], res['o_swa'], do, res['lse_swa'], bias, sinkcol,
                                                          do_blk0=1, B=B, S=S, comm=xchg.bwd_items(),
                                                          name=f"{tag}_swa_bwd")
    xchg.bwd_done(got)
    dh, dgq, dgkv = _even_in_bwd(res['h'], res['rq'], res['rkv'], P['ev_q_norm'][j][None], P['ev_kv_norm'][j][None],
                                 dcqn, dckvn, dqs, _shift_prev(dko, dkp, B, S), _shift_prev(dvo, dvp, B, S), dkr_pre,
                                 name=f"{tag}_in_bwd")
    g['ev_w_in'] = _even_in_grad_unpad(_mm_tn(xb, dh, name=f"{tag}_dwin"))
    xchg.push_grads({(n, j): val for n, val in g.items()})
    dx_kwargs = dict(trans_b=True, extras=(dz1,), epilogue=lambda acc, r: (acc + DN_ALPHA * r,), name=f"{tag}_dx")
    dx = _scattering(xchg, _mm, dh, w_in, **dx_kwargs) if j == 0 else _mm(dh, w_in, **dx_kwargs)
    small = dict(ev_q_norm=dgq[0], ev_kv_norm=dgkv[0], dbias=dbias, ev_sinks=jnp.sum(dsink, axis=(1, 2)))
    return dx, small


def _odd_fwd(xb, W, P, i, B, S, xchg, tag):
    j = i // 2
    w = W['od_w_in']
    w_qkv = w[:, :ODD_QKV]
    w_f = jnp.pad(w[:, ODD_QKV:], ((0, 0), (0, LANES - FOX_HEADS)))
    bf = jnp.pad(P['od_b_f'][j], (0, LANES - FOX_HEADS))[None]
    qkv = _mm(xb, w_qkv, out_dtypes=(BF16,), name=f"{tag}_qkv")
    f = _mm(xb, w_f, name=f"{tag}_f").reshape(B, S, LANES)
    csh, chs = _fox_decay_fwd(f, bf, name=f"{tag}_decay")
    crow = chs[:, :FOX_HEADS].reshape(B, FOX_HEADS, S // ATT_TILE, 1, ATT_TILE)
    n_blk = FOX_HEADS * HEAD_DIM // LANES
    o, lse, got = _flash_fwd(qkv, qkv, qkv, q_blk0=0, k_blk0=n_blk, v_blk0=2 * n_blk, W=LANES,
                             n_pairs=FOX_HEADS // 2, B=B, S=S, scale=HEAD_DIM ** -0.5, csh=csh, crow=crow,
                             comm=xchg.carry(f"{tag}_fox"), name=f"{tag}_fox")
    xchg.carried(f"{tag}_fox", got)
    res = dict(f=f, bf=bf, csh=csh, crow=crow, qkv=qkv, o=o, lse=lse, w_qkv=w_qkv, w_f=w_f)
    return (o, W['od_w_out']), res


def _odd_bwd(dmb, dz1, xb, W, P, j, B, S, res, xchg, tag):
    g = {}
    w_out = W['od_w_out']
    g['od_w_out'] = _mm_tn(res['o'], dmb, name=f"{tag}_dwout")
    do = _mm(dmb, w_out, trans_b=True, out_dtypes=(BF16,), name=f"{tag}_do")
    qkv = res['qkv']
    n_blk = FOX_HEADS * HEAD_DIM // LANES
    dq, dk, dv, dck, dcq, got = _flash_bwd(qkv, qkv, qkv, res['o'], do, res['lse'], q_blk0=0, k_blk0=n_blk,
                                           v_blk0=2 * n_blk, do_blk0=0, W=LANES, n_pairs=FOX_HEADS // 2, B=B, S=S,
                                           scale=HEAD_DIM ** -0.5, qk_dtype=BF16, csh=res['csh'], crow=res['crow'],
                                           comm=xchg.bwd_items(), name=f"{tag}_fox_bwd")
    xchg.bwd_done(got)
    dc = dck.reshape(B, FOX_HEADS, S) + dcq.reshape(B, FOX_HEADS, S)
    dc_hs = jnp.pad(dc, ((0, 0), (0, LANES - FOX_HEADS), (0, 0)))
    df, dbf = _fox_decay_bwd(dc_hs, res['f'], res['bf'], name=f"{tag}_decay_bwd")
    df = df.reshape(B * S, LANES)
    dw_qkv = [_mm_tn(xb, t, name=f"{tag}_dw{n}") for n, t in (("q", dq), ("k", dk), ("v", dv))]
    dw_f = _mm_tn(xb, df, name=f"{tag}_dwf")
    g['od_w_in'] = jnp.concatenate(dw_qkv + [dw_f[:, :FOX_HEADS]], axis=1)
    dxf = _mm(df, res['w_f'], trans_b=True, extras=(dz1,), epilogue=lambda acc, r: (acc + DN_ALPHA * r,),
              name=f"{tag}_dxf")
    xchg.push_grads({(n, j): val for n, val in g.items()})
    dx = _mm((dq, dk, dv), res['w_qkv'], trans_b=True, extras=(dxf,), epilogue=lambda acc, r: (acc + r,),
             name=f"{tag}_dx")
    small = dict(od_b_f=dbf[0, :FOX_HEADS])
    return dx, small


def _carrying(xchg, name, call, *args, **kwargs):
    comm = xchg.carry(name)
    out = call(*args, comm=comm, name=name, **kwargs)
    if comm:
        out, got = out
        xchg.carried(name, got)
    return out


def _scattering(xchg, call, *args, **kwargs):
    comm = xchg.bwd_items()
    out = call(*args, comm=comm, **kwargs)
    if comm:
        out, got = out
        xchg.bwd_done(got)
    return out


def _local_step(x, p, target, P, xchg):
    B, S, D = x.shape
    T = B * S
    q_tabs, k_tabs = _rope_tables(S)
    bucket = _swa_bucket_table()
    in_bucket = (bucket[..., None] == jnp.arange(REL_BUCKETS)).astype(F32)
    bias = jnp.einsum('acb,bh->hac', in_bucket, P['rel_bias'], precision=lax.Precision.HIGHEST)

    xc = x.reshape(T, D)
    xcb = xc.astype(BF16)
    saved = []
    for i in range(DEPTH):
        j = i // 2
        tag = f"l{i}"
        W = xchg.layer_weights(i)
        lay = dict(xb=xcb, W=W)
        if i % 2 == 0:
            sinkcol = jnp.broadcast_to(P['ev_sinks'][j][:, None, None], (SWA_HEADS, BLOCK_Q, 1)).astype(F32)
            lay['tabs'] = (q_tabs, k_tabs, bias, sinkcol)
            (o, w_out), lay['mix'] = _even_fwd(xcb, W, P, i, B, S, lay['tabs'], xchg, tag)
        else:
            (o, w_out), lay['mix'] = _odd_fwd(xcb, W, P, i, B, S, xchg, tag)
        x1, x1b, lay['xh1'], lay['r1'] = _carrying(xchg, f"{tag}_out_ln1", _mm_ln, o, w_out, xc,
                                                   P['ln1_g'][i][None], P['ln1_b'][i][None])
        lay['x1b'] = x1b
        lay['u'], lay['a'] = _carrying(xchg, f"{tag}_up", _mm, x1b, W['w_up'], out_dtypes=(F32, BF16),
                                       epilogue=lambda acc: (acc, jnp.square(jnp.maximum(acc, 0.0))))
        x2, x2b, lay['xh2'], lay['r2'] = _carrying(xchg, f"{tag}_down_ln2", _mm_ln, lay['a'], W['w_down'], x1,
                                                   P['ln2_g'][i][None], P['ln2_b'][i][None])
        lay['x2b'] = x2b
        lay['p'] = p[i].reshape(T, D_PLE)
        lay['e'] = _mm(lay['p'], W['ple_w_proj'], name=f"{tag}_ple_proj")

        def gate(acc, bg, e, x2v):
            gv = 1.0 / (1.0 + jnp.exp(-(acc + bg)))
            y = x2v + gv * e
            return y, y, gv

        xc, xcb, lay['g'] = _carrying(xchg, f"{tag}_ple_gate", _mm, x2b, W['ple_w_gate'],
                                      extras=(P['ple_b_gate'][i][None], lay['e'], x2), epilogue=gate,
                                      out_dtypes=(F32, BF16, F32))
        saved.append(lay)

    dy, sq = _loss_grad(xc, target.reshape(T, D), name="loss")

    Gs = {n: [None] * DEPTH for n in ('ln1_g', 'ln1_b', 'ln2_g', 'ln2_b', 'ple_b_gate')}
    Gs.update({n: [None] * (DEPTH // 2) for n in ('ev_q_norm', 'ev_kv_norm', 'ev_sinks', 'od_b_f')})
    dbias_total = None
    for i in reversed(range(DEPTH)):
        j = i // 2
        tag = f"l{i}b"
        lay = saved[i]
        W = lay['W']
        de, dzg, dbg = _ple_bwd_elem(dy, lay['g'], lay['e'], name=f"{tag}_ple_elem")
        Gs['ple_b_gate'][i] = dbg[0]
        g_mlp = {('ple_w_proj', i): _mm_tn(lay['p'], de, slot_width=D_MODEL // N_DEV, name=f"{tag}_dwproj"),
                 ('ple_w_gate', i): _mm_tn(lay['x2b'], dzg, name=f"{tag}_dwgate")}
        dz2, dz2b, dg2, db2 = _mm_ln_bwd(dzg, W['ple_w_gate'], dy, 1.0, lay['xh2'], lay['r2'], P['ln2_g'][i][None],
                                         name=f"{tag}_dx2_ln2")
        Gs['ln2_g'][i], Gs['ln2_b'][i] = dg2[0], db2[0]
        g_mlp[('w_down', i)] = _mm_tn(lay['a'], dz2b, name=f"{tag}_dwdown")
        du = _mm(dz2b, W['w_down'], trans_b=True, extras=(lay['u'],), out_dtypes=(BF16,),
                 epilogue=lambda acc, u: (acc * (2.0 * jnp.maximum(u, 0.0)),), name=f"{tag}_du")
        g_mlp[('w_up', i)] = _mm_tn(lay['x1b'], du, slot_width=D_FF // N_DEV, name=f"{tag}_dwup")
        xchg.push_grads(g_mlp)
        dz1, dz1b, dg1, db1 = _mm_ln_bwd(du, W['w_up'], dz2, DN_ALPHA, lay['xh1'], lay['r1'], P['ln1_g'][i][None],
                                         name=f"{tag}_dx1_ln1")
        Gs['ln1_g'][i], Gs['ln1_b'][i] = dg1[0], db1[0]
        if i % 2 == 0:
            dy, small = _even_bwd(dz1b, dz1, lay['xb'], W, P, j, B, S, lay['tabs'], lay['mix'], xchg, tag)
            dbias_total = small['dbias'] if dbias_total is None else dbias_total + small['dbias']
            for n in ('ev_q_norm', 'ev_kv_norm', 'ev_sinks'):
                Gs[n][j] = small[n]
        else:
            dy, small = _odd_bwd(dz1b, dz1, lay['xb'], W, P, j, B, S, lay['mix'], xchg, tag)
            Gs['od_b_f'][j] = small['od_b_f']

    grads_small = {n: jnp.stack(v) for n, v in Gs.items()}
    drel = _bias_bucket_sum(dbias_total, bucket, name="rel_bias_grad")
    grads_small['rel_bias'] = drel[:, :REL_BUCKETS].T
    return sq, dy.reshape(B, S, D), grads_small


def kernel(x, p, rel_bias, ev_w_in, ev_q_norm, ev_w_uq, ev_kv_norm, ev_w_ukv, ev_sinks, ev_w_out, od_w_in, od_b_f, od_w_out, ln1_g, ln1_b, w_up, w_down, ln2_g, ln2_b, ple_w_proj, ple_w_gate, ple_b_gate, loss_target, m_rel_bias, m_ev_w_in, m_ev_q_norm, m_ev_w_uq, m_ev_kv_norm, m_ev_w_ukv, m_ev_sinks, m_ev_w_out, m_od_w_in, m_od_b_f, m_od_w_out, m_ln1_g, m_ln1_b, m_w_up, m_w_down, m_ln2_g, m_ln2_b, m_ple_w_proj, m_ple_w_gate, m_ple_b_gate, v_rel_bias, v_ev_w_in, v_ev_q_norm, v_ev_w_uq, v_ev_kv_norm, v_ev_w_ukv, v_ev_sinks, v_ev_w_out, v_od_w_in, v_od_b_f, v_od_w_out, v_ln1_g, v_ln1_b, v_w_up, v_w_down, v_ln2_g, v_ln2_b, v_ple_w_proj, v_ple_w_gate, v_ple_b_gate):
    given = dict(locals())
    w = {n: given[n] for n in WEIGHTS}
    mom = {n: given["m_" + n] for n in WEIGHTS}
    var = {n: given["v_" + n] for n in WEIGHTS}
    small_shapes = {n: w[n].shape for n in SMALL}

    xchg = _MeshExchange({n: w[n].astype(BF16) for n in BIG})
    P = {n: w[n] for n in SMALL}

    sq, grad_x, grads_small = _local_step(x, p, loss_target, P, xchg)
    loss = lax.psum(0.5 * jnp.sum(sq) / D_MODEL, ("x", "y", "c"))

    received = xchg.finish()
    g_small_packed = _all_reduce_small(_pack_small(grads_small), name="reduce_small_grads")
    g_small = _unpack_small(g_small_packed, small_shapes)

    grad, delta, new_m, new_v = {}, {}, {}, {}
    for n in BIG:
        parts = [received[(n, idx)] for idx in range(w[n].shape[0])]
        grad[n], delta[n], new_m[n], new_v[n] = _adamw_slots(w[n], parts, mom[n], var[n], name=f"adamw_{n}")
    d, nm, nv = _adamw(_pack_small(w), g_small_packed, _pack_small(mom), _pack_small(var), name="adamw_small")
    d, nm, nv = (_unpack_small(t, small_shapes) for t in (d, nm, nv))
    for n in SMALL:
        grad[n], delta[n], new_m[n], new_v[n] = g_small[n], d[n], nm[n], nv[n]

    return (loss, grad_x, *[grad[n] for n in WEIGHTS], *[delta[n] for n in WEIGHTS],
            *[new_m[n] for n in WEIGHTS], *[new_v[n] for n in WEIGHTS])
```

```python
import math

import jax
import jax.numpy as jnp
from jax import lax
from jax.experimental import pallas as pl
from jax.experimental.pallas import tpu as pltpu

F32, BF16 = jnp.float32, jnp.bfloat16

D_MODEL = 1024
DEPTH = 4
HEAD_DIM = 64
MLA_HEADS, MLA_NOPE, MLA_ROPE, MLA_V = 8, 64, 32, 64
MLA_Q_LORA, MLA_KV_LORA = 384, 256
MLA_QK = MLA_NOPE + MLA_ROPE
ROPE_THETA = 10000.0
SWA_HEADS, SWA_KV_HEADS, SWA_WINDOW = 8, 2, 128
SWA_GROUP = SWA_HEADS // SWA_KV_HEADS
REL_BUCKETS, REL_MAX_DIST = 32, 128
FOX_HEADS = 16
D_FF = 4 * D_MODEL
D_PLE = 256
BLOCK_Q = 128
DN_ALPHA = (2 * DEPTH) ** 0.25
NORM_EPS = 1e-5
NEG_INF = -1e30
EVEN_IN = 1440
ODD_QKV = 3 * FOX_HEADS * HEAD_DIM
LANES = 128

EV_QS = (0, 512)
EV_CQ = (512, 896)
EV_CKV = (896, 1152)
EV_KS = (1152, 1280)
EV_VS = (1280, 1408)
EV_KR = (1408, 1536)
EVEN_IN_PAD = 1536
KR_LANE0 = MLA_NOPE

ADAM_LR, ADAM_B1, ADAM_B2, ADAM_EPS, ADAM_WD, ADAM_STEP = 0.001, 0.9, 0.999, 1e-08, 0.01, 10

N_DEV = 8
VMEM_LIMIT_BYTES = 48 * 1024 * 1024
ATT_TILE = 512
ATT_TILE_BWD = 512
PAIRS_PER_STEP_FWD = 4
PAIRS_PER_STEP_BWD = 2

NN = (((1,), (0,)), ((), ()))
NT = (((1,), (1,)), ((), ()))
TN = (((0,), (0,)), ((), ()))

BIG = ['ev_w_in', 'ev_w_uq', 'ev_w_ukv', 'ev_w_out', 'od_w_in', 'od_w_out', 'w_up', 'w_down',
       'ple_w_proj', 'ple_w_gate']
BIG_AXIS = {'ev_w_in': 2, 'ev_w_uq': 2, 'ev_w_ukv': 2, 'ev_w_out': 1, 'od_w_in': 2, 'od_w_out': 1,
            'w_up': 2, 'w_down': 1, 'ple_w_proj': 2, 'ple_w_gate': 1}
SMALL = ['rel_bias', 'ev_q_norm', 'ev_kv_norm', 'ev_sinks', 'od_b_f', 'ln1_g', 'ln1_b', 'ln2_g', 'ln2_b',
         'ple_b_gate']
WEIGHTS = ['rel_bias', 'ev_w_in', 'ev_q_norm', 'ev_w_uq', 'ev_kv_norm', 'ev_w_ukv', 'ev_sinks', 'ev_w_out',
           'od_w_in', 'od_b_f', 'od_w_out', 'ln1_g', 'ln1_b', 'w_up', 'w_down', 'ln2_g', 'ln2_b',
           'ple_w_proj', 'ple_w_gate', 'ple_b_gate']


def _cparams(*sem):
    return pltpu.CompilerParams(dimension_semantics=sem, vmem_limit_bytes=VMEM_LIMIT_BYTES)


def _pick(n, cands):
    for c in cands:
        if n % c == 0:
            return c
    return n


MM_STEP_BYTES = 10 * 1024 * 1024
MM_OUT_BYTES = 8 * 1024 * 1024
MM_CHUNK = 512


def _mm(a, b, *, trans_b=False, extras=(), epilogue=None, row_epilogue=None, out_dtypes=(F32,), out_widths=None,
        n_sums=0, comm=(), name):
    a_parts = tuple(a) if isinstance(a, (tuple, list)) else (a,)
    n_a = len(a_parts)
    M = a_parts[0].shape[0]
    k_offs = [sum(p.shape[1] for p in a_parts[:i]) for i in range(n_a + 1)]
    slot_w = b.shape[2] if b.ndim == 3 else None
    if slot_w is None:
        N = b.shape[0] if trans_b else b.shape[1]
    else:
        assert n_a == 1 and slot_w % LANES == 0
        N = b.shape[1] if trans_b else b.shape[0] * slot_w
    n_ex, n_out = len(extras), len(out_dtypes)
    n_rows_out = n_out - n_sums
    out_widths = (N,) * n_out if out_widths is None else out_widths
    row_bytes = sum(p.shape[1] * p.dtype.itemsize for p in a_parts) + (sum(w * jnp.dtype(d).itemsize
                                            for w, d in zip(out_widths[:n_rows_out], out_dtypes))
                                        + sum(e.shape[1] * e.dtype.itemsize for e in extras if e.shape[0] == M)
                                        + (4 * N if row_epilogue is not None else 0))
    tm = next((c for c in (1024, 512, 256) if M % c == 0 and c * row_bytes <= MM_STEP_BYTES), 128)
    nc = _pick(N, (MM_CHUNK, 384, 256, 128)) if slot_w is None or trans_b else slot_w
    n_c, kinds = len(comm), [k for k, _ in comm]
    n_scr = 1 if row_epilogue is not None else 0

    def body(*refs):
        a_refs, refs = refs[:n_a], refs[n_a - 1:]
        c_in = refs[2 + n_ex:2 + n_ex + n_c]
        c_out = refs[2 + n_ex + n_c + n_out:2 + n_ex + 2 * n_c + n_out]
        sems = refs[2 + n_ex + 2 * n_c + n_out + n_scr:]
        refs = refs[:2 + n_ex] + refs[2 + n_ex + n_c:2 + n_ex + n_c + n_out] \
            + refs[2 + n_ex + 2 * n_c + n_out:2 + n_ex + 2 * n_c + n_out + n_scr]
        if n_c:
            place = _mesh_place()
            step = pl.program_id(0)

            @pl.when(step == 0)
            def _():
                _comm_start(kinds, c_in, c_out, sems, place)

        b_ref = refs[1]
        ex = refs[2:2 + n_ex]
        outs = refs[2 + n_ex:2 + n_ex + n_out]
        avs = [r[...].astype(BF16) for r in a_refs]
        for n0 in range(0, N, nc):
            cols = slice(n0, n0 + nc)
            acc = None
            if slot_w is None:
                terms = [(av, b_ref[cols, k0:k1] if trans_b else b_ref[k0:k1, cols])
                         for av, k0, k1 in zip(avs, k_offs[:-1], k_offs[1:])]
            elif trans_b:
                terms = [(avs[0][:, sl * slot_w:(sl + 1) * slot_w], b_ref[sl, cols, :]) for sl in range(b.shape[0])]
            else:
                terms = [(avs[0], b_ref[n0 // slot_w])]
            for av, bv in terms:
                part = lax.dot_general(av, bv.astype(BF16), NT if trans_b else NN, preferred_element_type=F32)
                acc = part if acc is None else acc + part
            if row_epilogue is not None:
                refs[-1][:, cols] = acc
                continue
            res = epilogue(acc, *[e[:, cols] for e in ex]) if epilogue is not None else (acc,)
            for o, r in zip(outs, res):
                o[:, cols] = r.astype(o.dtype)
        if row_epilogue is not None:
            res = row_epilogue(refs[-1][...], *[e[...] for e in ex])
            for o, r in zip(outs[:n_rows_out], res):
                o[...] = r.astype(o.dtype)
            if n_sums:
                @pl.when(pl.program_id(0) == 0)
                def _():
                    for o in outs[n_rows_out:]:
                        o[...] = jnp.zeros_like(o)

                for o, r in zip(outs[n_rows_out:], res[n_rows_out:]):
                    o[...] += r
        if n_c:
            @pl.when(step == M // tm - 1)
            def _():
                _comm_wait(kinds, c_in, c_out, sems, place)

    in_specs = [pl.BlockSpec((tm, p.shape[1]), lambda i: (i, 0)) for p in a_parts]
    in_specs.append(pl.BlockSpec(b.shape, lambda i: (0,) * b.ndim))
    for e in extras:
        if e.shape[0] == M:
            in_specs.append(pl.BlockSpec((tm, e.shape[1]), lambda i: (i, 0)))
        elif e.shape == (1, N):
            in_specs.append(pl.BlockSpec((1, N), lambda i: (0, 0)))
        else:
            raise ValueError(f"extra operand of shape {e.shape} for a ({M}, {N}) result")
    res = pl.pallas_call(
        body, name=name, grid=(M // tm,), in_specs=in_specs + [HBM_SPEC] * n_c,
        out_specs=[pl.BlockSpec((tm, w), lambda i: (i, 0)) for w in out_widths[:n_rows_out]]
        + [pl.BlockSpec((1, w), lambda i: (0, 0)) for w in out_widths[n_rows_out:]] + [HBM_SPEC] * n_c,
        out_shape=[jax.ShapeDtypeStruct((M, w), d) for w, d in zip(out_widths[:n_rows_out], out_dtypes)]
        + [jax.ShapeDtypeStruct((1, w), d) for w, d in zip(out_widths[n_rows_out:], out_dtypes[n_rows_out:])]
        + _comm_out_shapes(comm),
        scratch_shapes=([pltpu.VMEM((tm, N), F32)] if row_epilogue is not None else [])
        + (_comm_scratch(comm) if n_c else []),
        compiler_params=_cparams("arbitrary" if n_sums or n_c else "parallel"),
    )(*a_parts, b, *extras, *[c for _, c in comm])
    main = res[0] if n_out == 1 else tuple(res[:n_out])
    return (main, list(res[n_out:])) if n_c else main


def _mm_tn(a, b, *, slot_width=None, name):
    T, K = a.shape
    N = b.shape[1]
    bk, bn = K, N
    while bk * bn * 4 > MM_OUT_BYTES:
        if bn >= bk and bn % (2 * LANES) == 0:
            bn //= 2
        else:
            bk //= 2
    tt = _pick(T, (1024, 512, 256))
    ck, cn = _pick(bk, (MM_CHUNK, 384, 256, 128)), _pick(bn, (MM_CHUNK, 384, 256, 128))

    def body(a_ref, b_ref, o_ref, acc_ref):
        t = pl.program_id(2)

        @pl.when(t == 0)
        def _():
            acc_ref[...] = jnp.zeros_like(acc_ref)

        for r0 in range(0, bk, ck):
            av = a_ref[:, r0:r0 + ck].astype(BF16)
            for c0 in range(0, bn, cn):
                acc_ref[r0:r0 + ck, c0:c0 + cn] += lax.dot_general(
                    av, b_ref[:, c0:c0 + cn].astype(BF16), TN, preferred_element_type=F32)

        @pl.when(t == T // tt - 1)
        def _():
            if slot_width is None:
                o_ref[...] = acc_ref[...].astype(o_ref.dtype)
            else:
                for slot in range(bn // slot_width):
                    o_ref[slot] = acc_ref[:, slot * slot_width:(slot + 1) * slot_width].astype(o_ref.dtype)

    if slot_width is None:
        out_spec, out_shape = pl.BlockSpec((bk, bn), lambda i, j, t: (i, j)), (K, N)
    else:
        assert bn % slot_width == 0 and slot_width % LANES == 0
        out_spec = pl.BlockSpec((bn // slot_width, bk, slot_width), lambda i, j, t: (j, i, 0))
        out_shape = (N // slot_width, K, slot_width)
    return pl.pallas_call(
        body, name=name, grid=(K // bk, N // bn, T // tt),
        in_specs=[pl.BlockSpec((tt, bk), lambda i, j, t: (t, i)), pl.BlockSpec((tt, bn), lambda i, j, t: (t, j))],
        out_specs=out_spec, out_shape=jax.ShapeDtypeStruct(out_shape, BF16),
        scratch_shapes=[pltpu.VMEM((bk, bn), F32)],
        compiler_params=_cparams("parallel", "parallel", "arbitrary"),
    )(a, b)


ROW_TILE = 256


def _row_spec(cols, col_block=0):
    return pl.BlockSpec((ROW_TILE, cols), lambda i: (i, col_block))


def _tab_spec(cols, period):
    return pl.BlockSpec((ROW_TILE, cols), lambda i: (i % period, 0))


def _full_spec(shape):
    return pl.BlockSpec(shape, lambda i: (0,) * len(shape))


def _mm_ln(a, w, x, g, b, *, comm=(), name):
    def ln_rows(m, xv, gv, bv):
        z = DN_ALPHA * xv + m
        mu = jnp.mean(z, -1, keepdims=True)
        zc = z - mu
        r = lax.rsqrt(jnp.mean(zc * zc, -1, keepdims=True) + NORM_EPS)
        xh = zc * r
        y = xh * gv + bv
        return y, y, xh, jnp.broadcast_to(r, (r.shape[0], LANES))

    D = w.shape[1]
    return _mm(a, w, extras=(x, g, b), row_epilogue=ln_rows, out_dtypes=(F32, BF16, F32, F32),
               out_widths=(D, D, D, LANES), comm=comm, name=name)


def _mm_ln_bwd(a, w, resid, resid_scale, xh, r, g, *, name):
    def ln_bwd_rows(acc, rv, xhv, rstd, gv):
        dyv = acc + resid_scale * rv
        dyg = dyv * gv
        c1 = jnp.mean(dyg, -1, keepdims=True)
        c2 = jnp.mean(dyg * xhv, -1, keepdims=True)
        dz = _widen(rstd, dyv.shape[-1]) * (dyg - c1 - xhv * c2)
        return dz, dz, jnp.sum(dyv * xhv, 0, keepdims=True), jnp.sum(dyv, 0, keepdims=True)

    D = resid.shape[1]
    return _mm(a, w, trans_b=True, extras=(resid, xh, r, g), row_epilogue=ln_bwd_rows,
               out_dtypes=(F32, BF16, F32, F32), out_widths=(D, D, D, D), n_sums=2, name=name)


def _loss_grad(y, target, *, name):
    T, D = y.shape

    def body(y_ref, t_ref, dy_ref, sq_ref):
        err = y_ref[...] - t_ref[...]
        dy_ref[...] = err / D

        @pl.when(pl.program_id(0) == 0)
        def _():
            sq_ref[...] = jnp.zeros_like(sq_ref)

        sq_ref[...] += jnp.sum(err * err, 0, keepdims=True)

    return pl.pallas_call(
        body, name=name, grid=(T // ROW_TILE,),
        in_specs=[_row_spec(D), _row_spec(D)],
        out_specs=[_row_spec(D), _full_spec((1, D))],
        out_shape=[jax.ShapeDtypeStruct((T, D), F32), jax.ShapeDtypeStruct((1, D), F32)],
        compiler_params=_cparams("arbitrary"),
    )(y, target)


def _ple_bwd_elem(dx3, g, e, *, name):
    T, D = dx3.shape

    def body(dx_ref, g_ref, e_ref, de_ref, dz_ref, db_ref):
        dx, gv = dx_ref[...], g_ref[...]
        de_ref[...] = (dx * gv).astype(BF16)
        dz = dx * e_ref[...] * gv * (1.0 - gv)
        dz_ref[...] = dz.astype(BF16)

        @pl.when(pl.program_id(0) == 0)
        def _():
            db_ref[...] = jnp.zeros_like(db_ref)

        db_ref[...] += jnp.sum(dz, 0, keepdims=True)

    return pl.pallas_call(
        body, name=name, grid=(T // ROW_TILE,),
        in_specs=[_row_spec(D), _row_spec(D), _row_spec(D)],
        out_specs=[_row_spec(D), _row_spec(D), _full_spec((1, D))],
        out_shape=[jax.ShapeDtypeStruct((T, D), BF16), jax.ShapeDtypeStruct((T, D), BF16),
                   jax.ShapeDtypeStruct((1, D), F32)],
        compiler_params=_cparams("arbitrary"),
    )(dx3, g, e)


def _rotate(xv, a, bm, bp, sign):
    half = MLA_ROPE // 2
    width = xv.shape[-1]
    a, bm, bp = (_widen(t, width) for t in (a, bm, bp))
    return xv * a + sign * (pltpu.roll(xv, width - half, 1) * bm + pltpu.roll(xv, half, 1) * bp)


def _rope(x, tabs, seq, *, sign, name):
    T, width = x.shape

    def body(x_ref, a_ref, bm_ref, bp_ref, o_ref):
        o_ref[...] = _rotate(x_ref[...], a_ref[...], bm_ref[...], bp_ref[...], sign).astype(BF16)

    return pl.pallas_call(
        body, name=name, grid=(T // ROW_TILE,),
        in_specs=[_row_spec(width)] + [_tab_spec(LANES, seq // ROW_TILE)] * 3,
        out_specs=_row_spec(width),
        out_shape=jax.ShapeDtypeStruct((T, width), BF16),
        compiler_params=_cparams("parallel"),
    )(x, *tabs)


def _mla_keys(knp, h, k_tabs, seq, *, name):
    T = knp.shape[0]

    def body(k_ref, h_ref, a_ref, bm_ref, bp_ref, o_ref):
        kr = _rotate(h_ref[...], a_ref[...], bm_ref[...], bp_ref[...], 1.0)
        for hd in range(MLA_HEADS):
            cols = slice(hd * LANES, (hd + 1) * LANES)
            o_ref[:, cols] = (k_ref[:, cols].astype(F32) + kr).astype(BF16)

    return pl.pallas_call(
        body, name=name, grid=(T // ROW_TILE,),
        in_specs=[_row_spec(MLA_HEADS * LANES), _row_spec(LANES, EV_KR[0] // LANES)]
        + [_tab_spec(LANES, seq // ROW_TILE)] * 3,
        out_specs=_row_spec(MLA_HEADS * LANES),
        out_shape=jax.ShapeDtypeStruct((T, MLA_HEADS * LANES), BF16),
        compiler_params=_cparams("parallel"),
    )(knp, h, *k_tabs)


def _mla_rope_key_grad(dk, k_tabs, seq, *, name):
    T = dk.shape[0]

    def body(dk_ref, a_ref, bm_ref, bp_ref, o_ref):
        tot = dk_ref[:, 0:LANES]
        for hd in range(1, MLA_HEADS):
            tot = tot + dk_ref[:, hd * LANES:(hd + 1) * LANES]
        o_ref[...] = _rotate(tot, a_ref[...], bm_ref[...], bp_ref[...], -1.0).astype(BF16)

    return pl.pallas_call(
        body, name=name, grid=(T // ROW_TILE,),
        in_specs=[_row_spec(MLA_HEADS * LANES)] + [_tab_spec(LANES, seq // ROW_TILE)] * 3,
        out_specs=_row_spec(LANES),
        out_shape=jax.ShapeDtypeStruct((T, LANES), BF16),
        compiler_params=_cparams("parallel"),
    )(dk, *k_tabs)


def _even_norms(h, gq, gkv, *, name):
    T = h.shape[0]

    def body(h_ref, gq_ref, gkv_ref, cq_ref, ckv_ref, rq_ref, rkv_ref):
        cq = h_ref[:, EV_CQ[0]:EV_CQ[1]]
        rq = lax.rsqrt(jnp.mean(cq * cq, -1, keepdims=True) + NORM_EPS)
        cq_ref[...] = (cq * rq * gq_ref[...]).astype(BF16)
        rq_ref[...] = jnp.broadcast_to(rq, rq_ref.shape)
        ckv = h_ref[:, EV_CKV[0]:EV_CKV[1]]
        rkv = lax.rsqrt(jnp.mean(ckv * ckv, -1, keepdims=True) + NORM_EPS)
        ckv_ref[...] = (ckv * rkv * gkv_ref[...]).astype(BF16)
        rkv_ref[...] = jnp.broadcast_to(rkv, rkv_ref.shape)

    return pl.pallas_call(
        body, name=name, grid=(T // ROW_TILE,),
        in_specs=[_row_spec(EVEN_IN_PAD), _full_spec((1, MLA_Q_LORA)), _full_spec((1, MLA_KV_LORA))],
        out_specs=[_row_spec(MLA_Q_LORA), _row_spec(MLA_KV_LORA), _row_spec(LANES), _row_spec(LANES)],
        out_shape=[jax.ShapeDtypeStruct((T, MLA_Q_LORA), BF16), jax.ShapeDtypeStruct((T, MLA_KV_LORA), BF16),
                   jax.ShapeDtypeStruct((T, LANES), F32), jax.ShapeDtypeStruct((T, LANES), F32)],
        compiler_params=_cparams("parallel"),
    )(h, gq, gkv)


def _even_in_bwd(h, rq, rkv, gq, gkv, dcqn, dckvn, dqs, dks, dvs, dkr, *, name):
    T = h.shape[0]

    def rms_bwd(c, r, g, dy):
        r = _widen(r, c.shape[-1])
        xr = c * r
        dyg = dy * g
        return r * (dyg - xr * jnp.mean(dyg * xr, -1, keepdims=True)), jnp.sum(dy * xr, 0, keepdims=True)

    def body(h_ref, rq_ref, rkv_ref, gq_ref, gkv_ref, dcq_ref, dckv_ref, dqs_ref, dks_ref, dvs_ref, dkr_ref,
             dh_ref, dgq_ref, dgkv_ref):
        @pl.when(pl.program_id(0) == 0)
        def _():
            dgq_ref[...] = jnp.zeros_like(dgq_ref)
            dgkv_ref[...] = jnp.zeros_like(dgkv_ref)

        dcq, dgq = rms_bwd(h_ref[:, EV_CQ[0]:EV_CQ[1]], rq_ref[...], gq_ref[...], dcq_ref[...])
        dckv, dgkv = rms_bwd(h_ref[:, EV_CKV[0]:EV_CKV[1]], rkv_ref[...], gkv_ref[...], dckv_ref[...])
        dgq_ref[...] += dgq
        dgkv_ref[...] += dgkv
        dh_ref[:, EV_QS[0]:EV_QS[1]] = dqs_ref[...]
        dh_ref[:, EV_CQ[0]:EV_CQ[1]] = dcq.astype(BF16)
        dh_ref[:, EV_CKV[0]:EV_CKV[1]] = dckv.astype(BF16)
        dh_ref[:, EV_KS[0]:EV_KS[1]] = dks_ref[...]
        dh_ref[:, EV_VS[0]:EV_VS[1]] = dvs_ref[...]
        dh_ref[:, EV_KR[0]:EV_KR[1]] = dkr_ref[...]

    return pl.pallas_call(
        body, name=name, grid=(T // ROW_TILE,),
        in_specs=[_row_spec(EVEN_IN_PAD), _row_spec(LANES), _row_spec(LANES), _full_spec((1, MLA_Q_LORA)),
                  _full_spec((1, MLA_KV_LORA)), _row_spec(MLA_Q_LORA), _row_spec(MLA_KV_LORA),
                  _row_spec(SWA_HEADS * HEAD_DIM), _row_spec(LANES), _row_spec(LANES), _row_spec(LANES)],
        out_specs=[_row_spec(EVEN_IN_PAD), _full_spec((1, MLA_Q_LORA)), _full_spec((1, MLA_KV_LORA))],
        out_shape=[jax.ShapeDtypeStruct((T, EVEN_IN_PAD), BF16), jax.ShapeDtypeStruct((1, MLA_Q_LORA), F32),
                   jax.ShapeDtypeStruct((1, MLA_KV_LORA), F32)],
        compiler_params=_cparams("arbitrary"),
    )(h, rq, rkv, gq, gkv, dcqn, dckvn, dqs, dks, dvs, dkr)


def _fox_decay_fwd(f3, bf, *, name):
    B, S, _ = f3.shape

    def body(f_ref, b_ref, csh_ref, chs_ref):
        x = f_ref[...] + b_ref[...]
        c = jnp.minimum(x, 0.0) - jnp.log1p(jnp.exp(-jnp.abs(x)))
        row = lax.broadcasted_iota(jnp.int32, (S, LANES), 0)
        k = 1
        while k < S:
            c = c + jnp.where(row >= k, pltpu.roll(c, k, 0), 0.0)
            k *= 2
        csh_ref[...] = c
        chs_ref[...] = c.T

    return pl.pallas_call(
        body, name=name, grid=(B,),
        in_specs=[pl.BlockSpec((None, S, LANES), lambda b: (b, 0, 0)), pl.BlockSpec((1, LANES), lambda b: (0, 0))],
        out_specs=[pl.BlockSpec((None, S, LANES), lambda b: (b, 0, 0)),
                   pl.BlockSpec((None, LANES, S), lambda b: (b, 0, 0))],
        out_shape=[jax.ShapeDtypeStruct((B, S, LANES), F32), jax.ShapeDtypeStruct((B, LANES, S), F32)],
        compiler_params=_cparams("parallel"),
    )(f3, bf)


def _fox_decay_bwd(dc_hs, f3, bf, *, name):
    B, S, _ = f3.shape

    def body(dc_ref, f_ref, b_ref, df_ref, db_ref):
        g = dc_ref[...].T
        row = lax.broadcasted_iota(jnp.int32, (S, LANES), 0)
        k = 1
        while k < S:
            g = g + jnp.where(row < S - k, pltpu.roll(g, S - k, 0), 0.0)
            k *= 2
        x = f_ref[...] + b_ref[...]
        df = g * (1.0 / (1.0 + jnp.exp(x)))
        df_ref[...] = df.astype(BF16)

        @pl.when(pl.program_id(0) == 0)
        def _():
            db_ref[...] = jnp.zeros_like(db_ref)

        db_ref[...] += jnp.sum(df, 0, keepdims=True)

    return pl.pallas_call(
        body, name=name, grid=(B,),
        in_specs=[pl.BlockSpec((None, LANES, S), lambda b: (b, 0, 0)),
                  pl.BlockSpec((None, S, LANES), lambda b: (b, 0, 0)), pl.BlockSpec((1, LANES), lambda b: (0, 0))],
        out_specs=[pl.BlockSpec((None, S, LANES), lambda b: (b, 0, 0)), pl.BlockSpec((1, LANES), lambda b: (0, 0))],
        out_shape=[jax.ShapeDtypeStruct((B, S, LANES), BF16), jax.ShapeDtypeStruct((1, LANES), F32)],
        compiler_params=_cparams("arbitrary"),
    )(dc_hs, f3, bf)


def _head_column(block, h):
    lane = lax.broadcasted_iota(jnp.int32, block.shape, 1)
    return jnp.sum(jnp.where(lane == h, block, 0.0), axis=-1, keepdims=True)


def _causal_mask(s):
    r = lax.broadcasted_iota(jnp.int32, s.shape, 0)
    c = lax.broadcasted_iota(jnp.int32, s.shape, 1)
    return jnp.where(c <= r, s, NEG_INF)


def _low_half(shape):
    return (lax.broadcasted_iota(jnp.int32, shape, 1) % LANES) < HEAD_DIM


def _widen(x, cols):
    return jnp.concatenate([x] * (cols // LANES), axis=1)


def _both_halves(x, lo):
    r = pltpu.roll(x, HEAD_DIM, 1)
    return jnp.where(lo, x, r), jnp.where(lo, r, x)


MESH_ID = pl.DeviceIdType.MESH
HBM_SPEC = pl.BlockSpec(memory_space=pltpu.HBM)
VMEM_SPEC = pl.BlockSpec(memory_space=pltpu.VMEM)


def _mesh_place():
    x, y, c = lax.axis_index("x"), lax.axis_index("y"), lax.axis_index("c")
    return x, y, c, 4 * x + 2 * y + c


def _peers(x, y, c):
    out = []
    for mask in range(1, N_DEV):
        dx, dy, dc = (mask >> 2) & 1, (mask >> 1) & 1, mask & 1
        px, py, pc = (1 - x if dx else x), (1 - y if dy else y), (1 - c if dc else c)
        out.append(((px, py, pc), 4 * px + 2 * py + pc))
    return out


def _comm_out_shapes(comm):
    return [jax.ShapeDtypeStruct(a.shape if kind == "scatter" else (N_DEV,) + a.shape[1:], a.dtype) for kind, a in comm]


def _comm_scratch(comm):
    n = len(comm)
    return [pltpu.SemaphoreType.DMA((n, 7)), pltpu.SemaphoreType.DMA((n, 7)), pltpu.SemaphoreType.DMA((n,))]


def _comm_copies(kinds, in_refs, out_refs, sems, place):
    send_sems, recv_sems, local_sems = sems
    x, y, c, me = place
    local, remote = [], []
    for w, kind in enumerate(kinds):
        mine = in_refs[w].at[me] if kind == "scatter" else in_refs[w].at[kind[1]]
        local.append(pltpu.make_async_copy(mine, out_refs[w].at[me], local_sems.at[w]))
        for k, (peer, peer_idx) in enumerate(_peers(x, y, c)):
            remote.append(pltpu.make_async_remote_copy(
                src_ref=in_refs[w].at[peer_idx] if kind == "scatter" else mine, dst_ref=out_refs[w].at[me],
                send_sem=send_sems.at[w, k], recv_sem=recv_sems.at[w, k], device_id=peer, device_id_type=MESH_ID))
    return local, remote


def _comm_start(kinds, in_refs, out_refs, sems, place):
    local, remote = _comm_copies(kinds, in_refs, out_refs, sems, place)
    for cp in local + remote:
        cp.start()


def _comm_wait(kinds, in_refs, out_refs, sems, place):
    local, remote = _comm_copies(kinds, in_refs, out_refs, sems, place)
    for cp in remote:
        cp.wait_recv()
    for cp in remote:
        cp.wait_send()
    for cp in local:
        cp.wait()


def _exchange(comm, *, name):
    n = len(comm)
    kinds = [k for k, _ in comm]

    def body(*refs):
        place = _mesh_place()
        _comm_start(kinds, refs[:n], refs[n:2 * n], refs[2 * n:], place)
        _comm_wait(kinds, refs[:n], refs[n:2 * n], refs[2 * n:], place)

    return pl.pallas_call(
        body, name=name, out_shape=_comm_out_shapes(comm), in_specs=[HBM_SPEC] * n, out_specs=[HBM_SPEC] * n,
        scratch_shapes=_comm_scratch(comm),
    )(*[a for _, a in comm])


def _flash_fwd(qa, ka, va, *, q_blk0, k_blk0, v_blk0, W, n_pairs, B, S, scale, csh=None, crow=None, comm=(), name):
    t = ATT_TILE
    nq = S // t
    P = PAIRS_PER_STEP_FWD
    decay = csh is not None
    split = W == LANES
    assert n_pairs % P == 0 and q_blk0 % P == 0 and k_blk0 % P == 0 and v_blk0 % P == 0
    n_c, kinds = len(comm), [k for k, _ in comm]
    n_in = 5 if decay else 3
    fold_scale = math.log2(scale).is_integer()
    n_steps = (B, n_pairs // P, nq)

    def body(*refs):
        c_in, c_out = refs[n_in:n_in + n_c], refs[n_in + n_c + 2:n_in + 2 * n_c + 2]
        sems = refs[n_in + 2 * n_c + 4:]
        refs = refs[:n_in] + refs[n_in + n_c:n_in + n_c + 2] + refs[n_in + 2 * n_c + 2:n_in + 2 * n_c + 4]
        if decay:
            q_ref, k_ref, v_ref, csh_ref, crow_ref, o_ref, lse_ref, m_s, acc_s = refs
        else:
            q_ref, k_ref, v_ref, o_ref, lse_ref, m_s, acc_s = refs
        g, i = pl.program_id(1), pl.program_id(2)
        if n_c:
            place = _mesh_place()
            ids = [pl.program_id(ax) for ax in range(3)]

            @pl.when((ids[0] == 0) & (ids[1] == 0) & (ids[2] == 0))
            def _():
                _comm_start(kinds, c_in, c_out, sems, place)

        lo = _low_half((t, LANES))
        qv = q_ref[...]
        qh = []
        for pr in range(P):
            qp = qv[:, pr * W:(pr + 1) * W]
            qh += [jnp.where(lo, qp, jnp.zeros_like(qp)), jnp.where(lo, jnp.zeros_like(qp), qp)] if split \
                else [qp[:, :LANES], qp[:, LANES:]]
        if fold_scale:
            qh = [x * scale for x in qh]
        if decay:
            cq = [jnp.broadcast_to(_head_column(csh_ref[...], 2 * P * g + hd), (t, LANES)) for hd in range(2 * P)]
        m_s[...] = jnp.full(m_s.shape, NEG_INF, F32)
        acc_s[...] = jnp.zeros(acc_s.shape, F32)

        def step(j, masked):
            rows = pl.ds(pl.multiple_of(j * t, t), t)
            kb, vb = k_ref[rows, :], v_ref[rows, :]
            for pr in range(P):
                kp, vp = kb[:, pr * W:(pr + 1) * W], vb[:, pr * LANES:(pr + 1) * LANES]
                ones = jnp.ones_like(vp)
                vaug = [jnp.where(lo, vp, ones), jnp.where(lo, ones, vp)]
                for half in range(2):
                    hd = 2 * pr + half
                    kh = kp if split else kp[:, half * LANES:(half + 1) * LANES]
                    s = lax.dot_general(qh[hd], kh, NT, preferred_element_type=F32)
                    if not fold_scale:
                        s = s * scale
                    if decay:
                        s = s + _widen(cq[hd], t) - crow_ref[hd, j]
                    if masked:
                        s = _causal_mask(s)
                    m_prev = m_s[hd]
                    m_new = jnp.maximum(m_prev, jnp.max(s, -1, keepdims=True))
                    p = jnp.exp(s - _widen(m_new, t))
                    acc_s[hd] = jnp.exp(m_prev - m_new) * acc_s[hd] + lax.dot_general(
                        p.astype(BF16), vaug[half], NN, preferred_element_type=F32)
                    m_s[hd] = m_new

        def loop_body(j, carry):
            step(j, False)
            return carry

        lax.fori_loop(0, i, loop_body, 0)
        step(i, True)
        for pr in range(P):
            acc0, acc1 = acc_s[2 * pr], acc_s[2 * pr + 1]
            _, l0 = _both_halves(acc0, lo)
            l1, _ = _both_halves(acc1, lo)
            cols = slice(pr * LANES, (pr + 1) * LANES)
            o_ref[:, cols] = jnp.where(lo, acc0 / l0, acc1 / l1).astype(BF16)
            lse_ref[:, cols] = jnp.where(lo, m_s[2 * pr] + jnp.log(l0), m_s[2 * pr + 1] + jnp.log(l1))
        if n_c:
            @pl.when((ids[0] == n_steps[0] - 1) & (ids[1] == n_steps[1] - 1) & (ids[2] == n_steps[2] - 1))
            def _():
                _comm_wait(kinds, c_in, c_out, sems, place)

    in_specs = [pl.BlockSpec((t, P * W), lambda b, g, i: (b * nq + i, q_blk0 // P + g)),
                pl.BlockSpec((S, P * W), lambda b, g, i: (b, k_blk0 // P + g)),
                pl.BlockSpec((S, P * LANES), lambda b, g, i: (b, v_blk0 // P + g))]
    args = [qa, ka, va]
    if decay:
        in_specs += [pl.BlockSpec((None, t, LANES), lambda b, g, i: (b, i, 0)),
                     pl.BlockSpec((None, 2 * P, nq, 1, t), lambda b, g, i: (b, g, 0, 0, 0))]
        args += [csh, crow]
    out_spec = pl.BlockSpec((t, P * LANES), lambda b, g, i: (b * nq + i, g))
    res = pl.pallas_call(
        body, name=name, grid=n_steps, in_specs=in_specs + [HBM_SPEC] * n_c,
        out_specs=[out_spec, out_spec] + [HBM_SPEC] * n_c,
        out_shape=[jax.ShapeDtypeStruct((B * S, n_pairs * LANES), BF16),
                   jax.ShapeDtypeStruct((B * S, n_pairs * LANES), F32)] + _comm_out_shapes(comm),
        scratch_shapes=[pltpu.VMEM((2 * P, t, LANES), F32), pltpu.VMEM((2 * P, t, LANES), F32)]
        + (_comm_scratch(comm) if n_c else []),
        compiler_params=_cparams(*(("arbitrary",) * 3 if n_c else ("parallel",) * 3)),
    )(*args, *[a for _, a in comm])
    return res[0], res[1], list(res[2:])


def _flash_bwd(qa, ka, va, oa, doa, lsea, *, q_blk0, k_blk0, v_blk0, do_blk0, W, n_pairs, B, S, scale, qk_dtype,
               csh=None, crow=None, comm=(), name):
    t = ATT_TILE_BWD
    nq = S // t
    P = PAIRS_PER_STEP_BWD
    decay = csh is not None
    if decay:
        crow = crow.reshape(B, 2 * n_pairs, nq, 1, t)
    split = W == LANES
    assert n_pairs % P == 0 and q_blk0 % P == 0 and k_blk0 % P == 0 and v_blk0 % P == 0 and do_blk0 % P == 0
    n_c, kinds = len(comm), [k for k, _ in comm]
    n_in, n_out, n_scr = (8, 5, 8) if decay else (6, 3, 5)
    n_steps = (B, n_pairs // P, nq)

    def body(*refs):
        c_in = refs[n_in:n_in + n_c]
        c_out = refs[n_in + n_c + n_out:n_in + 2 * n_c + n_out]
        sems = refs[n_in + 2 * n_c + n_out + n_scr:]
        refs = (refs[:n_in] + refs[n_in + n_c:n_in + n_c + n_out]
                + refs[n_in + 2 * n_c + n_out:n_in + 2 * n_c + n_out + n_scr])
        if n_c:
            place = _mesh_place()
            ids = [pl.program_id(ax) for ax in range(3)]

            @pl.when((ids[0] == 0) & (ids[1] == 0) & (ids[2] == 0))
            def _():
                _comm_start(kinds, c_in, c_out, sems, place)

        if decay:
            (q_ref, k_ref, v_ref, o_ref, do_ref, lse_ref, csh_ref, crow_ref, dq_ref, dk_ref, dv_ref, dck_ref, dcq_ref,
             dq_s, lse_s, delta_s, dk_s, dv_s, cq_s, dcq_s, dck_s) = refs
        else:
            (q_ref, k_ref, v_ref, o_ref, do_ref, lse_ref, dq_ref, dk_ref, dv_ref,
             dq_s, lse_s, delta_s, dk_s, dv_s) = refs
        g, j = pl.program_id(1), pl.program_id(2)
        lo = _low_half((t, LANES))

        @pl.when(j == 0)
        def _():
            lo_s = _low_half((S, LANES))
            dq_s[...] = jnp.zeros(dq_s.shape, F32)
            for pr in range(P):
                cols = slice(pr * LANES, (pr + 1) * LANES)
                lse_s[2 * pr], lse_s[2 * pr + 1] = _both_halves(lse_ref[:, cols], lo_s)
                dd = do_ref[:, cols].astype(F32) * o_ref[:, cols].astype(F32)
                delta_s[2 * pr] = jnp.broadcast_to(jnp.sum(jnp.where(lo_s, dd, 0.0), -1, keepdims=True), (S, LANES))
                delta_s[2 * pr + 1] = jnp.broadcast_to(jnp.sum(jnp.where(lo_s, 0.0, dd), -1, keepdims=True),
                                                       (S, LANES))
            if decay:
                for hd in range(2 * P):
                    cq_s[hd] = jnp.broadcast_to(_head_column(csh_ref[...], 2 * P * g + hd), (S, LANES))
                dcq_s[...] = jnp.zeros(dcq_s.shape, F32)

        kb, vb = k_ref[...], v_ref[...]
        kh, vh = [], []
        for pr in range(P):
            kp, vp = kb[:, pr * W:(pr + 1) * W], vb[:, pr * LANES:(pr + 1) * LANES]
            zk, zv = jnp.zeros_like(kp), jnp.zeros_like(vp)
            kh += [jnp.where(lo, kp, zk), jnp.where(lo, zk, kp)] if split else [kp[:, :LANES], kp[:, LANES:]]
            vh += [jnp.where(lo, vp, zv), jnp.where(lo, zv, vp)]
        dk_s[...] = jnp.zeros(dk_s.shape, F32)
        dv_s[...] = jnp.zeros(dv_s.shape, F32)
        if decay:
            dck_s[...] = jnp.zeros(dck_s.shape, F32)

        def step(i, masked):
            rows = pl.ds(pl.multiple_of(i * t, t), t)
            qi, doi = q_ref[rows, :], do_ref[rows, :]
            for pr in range(P):
                qp, dop = qi[:, pr * W:(pr + 1) * W], doi[:, pr * LANES:(pr + 1) * LANES]
                for half in range(2):
                    hd = 2 * pr + half
                    qx = qp if split else qp[:, half * LANES:(half + 1) * LANES]
                    s = lax.dot_general(qx, kh[hd], NT, preferred_element_type=F32) * scale
                    if decay:
                        s = s + _widen(cq_s[hd, rows, :], t) - crow_ref[hd, j]
                    if masked:
                        s = _causal_mask(s)
                    p = jnp.exp(s - _widen(lse_s[hd, rows, :], t))
                    dv_s[hd] += lax.dot_general(p.astype(BF16), dop, TN, preferred_element_type=F32)
                    dp = lax.dot_general(dop, vh[hd], NT, preferred_element_type=F32)
                    ds = p * (dp - _widen(delta_s[hd, rows, :], t))
                    dss = (ds * scale).astype(BF16)
                    dk_s[hd] += lax.dot_general(dss, qx, TN, preferred_element_type=F32)
                    dqc = lax.dot_general(dss, kh[hd], NN, preferred_element_type=F32)
                    if split:
                        dq_s[rows, pr * W:(pr + 1) * W] += dqc
                    else:
                        dq_s[rows, hd * LANES:(hd + 1) * LANES] += dqc
                    if decay:
                        dck_s[hd] -= jnp.sum(ds, 0, keepdims=True)
                        part = ds[:, :LANES]
                        for c in range(1, t // LANES):
                            part = part + ds[:, c * LANES:(c + 1) * LANES]
                        dcq_s[hd, rows, :] += part

        def loop_body(i, carry):
            step(i, False)
            return carry

        step(j, True)
        lax.fori_loop(j + 1, nq, loop_body, 0)
        for pr in range(P):
            if split:
                dk_ref[:, pr * W:(pr + 1) * W] = jnp.where(lo, dk_s[2 * pr], dk_s[2 * pr + 1]).astype(dk_ref.dtype)
            else:
                for half in range(2):
                    hd = 2 * pr + half
                    dk_ref[:, hd * LANES:(hd + 1) * LANES] = dk_s[hd].astype(dk_ref.dtype)
            dv_ref[:, pr * LANES:(pr + 1) * LANES] = jnp.where(lo, dv_s[2 * pr], dv_s[2 * pr + 1]).astype(BF16)
        if decay:
            dck_ref[...] = dck_s[...]

        @pl.when(j == nq - 1)
        def _():
            dq_ref[...] = dq_s[...].astype(dq_ref.dtype)
            if decay:
                for hd in range(2 * P):
                    dcq_ref[hd] = jnp.sum(dcq_s[hd].T, 0, keepdims=True)

        if n_c:
            @pl.when((ids[0] == n_steps[0] - 1) & (ids[1] == n_steps[1] - 1) & (ids[2] == n_steps[2] - 1))
            def _():
                _comm_wait(kinds, c_in, c_out, sems, place)

    full = lambda w, blk0: pl.BlockSpec((S, P * w), lambda b, g, j: (b, blk0 // P + g))
    blk = lambda w, blk0: pl.BlockSpec((t, P * w), lambda b, g, j: (b * nq + j, blk0 // P + g))
    in_specs = [full(W, q_blk0), blk(W, k_blk0), blk(LANES, v_blk0), full(LANES, 0), full(LANES, do_blk0),
                full(LANES, 0)]
    args = [qa, ka, va, oa, doa, lsea]
    T = B * S
    out_specs = [full(W, 0), blk(W, 0), blk(LANES, 0)]
    out_shape = [jax.ShapeDtypeStruct((T, n_pairs * W), qk_dtype), jax.ShapeDtypeStruct((T, n_pairs * W), qk_dtype),
                 jax.ShapeDtypeStruct((T, n_pairs * LANES), BF16)]
    per_head = lambda rows: pltpu.VMEM((2 * P, rows, LANES), F32)
    scratch = [pltpu.VMEM((S, P * W), F32), per_head(S), per_head(S), per_head(t), per_head(t)]
    if decay:
        in_specs += [pl.BlockSpec((None, S, LANES), lambda b, g, j: (b, 0, 0)),
                     pl.BlockSpec((None, 2 * P, nq, 1, t), lambda b, g, j: (b, g, 0, 0, 0))]
        args += [csh, crow]
        out_specs += [pl.BlockSpec((None, 2 * P, None, 1, t), lambda b, g, j: (b, g, j, 0, 0)),
                      pl.BlockSpec((None, 2 * P, 1, S), lambda b, g, j: (b, g, 0, 0))]
        out_shape += [jax.ShapeDtypeStruct((B, 2 * n_pairs, nq, 1, t), F32),
                      jax.ShapeDtypeStruct((B, 2 * n_pairs, 1, S), F32)]
        scratch += [per_head(S), per_head(S), pltpu.VMEM((2 * P, 1, t), F32)]
    res = pl.pallas_call(
        body, name=name, grid=n_steps, in_specs=in_specs + [HBM_SPEC] * n_c,
        out_specs=out_specs + [HBM_SPEC] * n_c, out_shape=out_shape + _comm_out_shapes(comm),
        scratch_shapes=scratch + (_comm_scratch(comm) if n_c else []),
        compiler_params=_cparams(*(("arbitrary",) * 3 if n_c else ("parallel", "parallel", "arbitrary"))),
    )(*args, *[a for _, a in comm])
    return tuple(res[:n_out]) + (list(res[n_out:]),)


def _swa_common(q_ref, kp_ref, ko_ref, vp_ref, vo_ref, n):
    Q = BLOCK_Q
    lo = _low_half((Q, LANES))
    lo2 = _low_half((2 * Q, LANES))
    kk = jnp.concatenate([kp_ref[...], ko_ref[...]], axis=0)
    vv = jnp.concatenate([vp_ref[...], vo_ref[...]], axis=0)
    kdup = [x.astype(BF16) for x in _both_halves(kk, lo2)]
    vdup = [x.astype(BF16) for x in _both_halves(vv, lo2)]
    a = lax.broadcasted_iota(jnp.int32, (SWA_GROUP * Q, 2 * Q), 0) % Q
    col = lax.broadcasted_iota(jnp.int32, (SWA_GROUP * Q, 2 * Q), 1)
    dist = a + Q - col
    valid = (dist >= 0) & (dist < SWA_WINDOW) & ((col >= Q) | (n > 0))
    qv = q_ref[...]
    qm = []
    for a_head in range(SWA_HEADS):
        qp = qv[:, (a_head // 2) * LANES:(a_head // 2 + 1) * LANES]
        keep = lo if a_head % 2 == 0 else jnp.logical_not(lo)
        qm.append(jnp.where(keep, qp, 0.0).astype(BF16))
    qs = [jnp.concatenate(qm[g * SWA_GROUP:(g + 1) * SWA_GROUP], axis=0) for g in range(SWA_KV_HEADS)]
    return lo, lo2, kdup, vdup, valid, qs


def _swa_group_logits(g, qs, kdup, valid, bias_ref):
    heads = slice(g * SWA_GROUP, (g + 1) * SWA_GROUP)
    s = lax.dot_general(qs[g], kdup[g], NT, preferred_element_type=F32) * (HEAD_DIM ** -0.5)
    s = s + bias_ref[heads].reshape(SWA_GROUP * BLOCK_Q, 2 * BLOCK_Q)
    return heads, jnp.where(valid, s, NEG_INF)


def _pair_halves(x, lo):
    Q = BLOCK_Q
    return [jnp.where(lo, x[2 * pr * Q:(2 * pr + 1) * Q], x[(2 * pr + 1) * Q:(2 * pr + 2) * Q])
            for pr in range(SWA_GROUP // 2)]


def _swa_in_specs(nb):
    Q = BLOCK_Q
    own = lambda blk: (lambda b, n: (b * nb + n, blk))
    prev = lambda blk: (lambda b, n: (b * nb + jnp.maximum(n - 1, 0), blk))
    kb, vb = EV_KS[0] // LANES, EV_VS[0] // LANES
    return [pl.BlockSpec((Q, SWA_HEADS * HEAD_DIM), own(0)), pl.BlockSpec((Q, LANES), prev(kb)),
            pl.BlockSpec((Q, LANES), own(kb)), pl.BlockSpec((Q, LANES), prev(vb)), pl.BlockSpec((Q, LANES), own(vb))]


def _swa_fwd(h, bias, sinkcol, *, B, S, comm=(), name):
    Q = BLOCK_Q
    nb = S // Q
    n_c, kinds = len(comm), [k for k, _ in comm]

    def body(*refs):
        c_in, c_out, sems = refs[7:7 + n_c], refs[9 + n_c:9 + 2 * n_c], refs[9 + 2 * n_c:]
        q_ref, kp_ref, ko_ref, vp_ref, vo_ref, bias_ref, sink_ref = refs[:7]
        o_ref, lse_ref = refs[7 + n_c:9 + n_c]
        if n_c:
            place = _mesh_place()
            ids = [pl.program_id(0), pl.program_id(1)]

            @pl.when((ids[0] == 0) & (ids[1] == 0))
            def _():
                _comm_start(kinds, c_in, c_out, sems, place)

        lo, lo2, kdup, vdup, valid, qs = _swa_common(q_ref, kp_ref, ko_ref, vp_ref, vo_ref, pl.program_id(1))
        pairs = []
        lo4 = _low_half((SWA_GROUP * Q, LANES))
        for g in range(SWA_KV_HEADS):
            heads, s = _swa_group_logits(g, qs, kdup, valid, bias_ref)
            sink = jnp.broadcast_to(sink_ref[heads].reshape(SWA_GROUP * Q, 1), (SWA_GROUP * Q, LANES))
            m = jnp.maximum(jnp.max(s, -1, keepdims=True), sink)
            p = jnp.exp(s - _widen(m, 2 * Q))
            vaug = jnp.where(lo2, vdup[g], jnp.ones_like(vdup[g]))
            pv = lax.dot_general(p.astype(BF16), vaug, NN, preferred_element_type=F32)
            rolled = pltpu.roll(pv, HEAD_DIM, 1)
            l = jnp.where(lo4, rolled, pv) + jnp.exp(sink - m)
            out = pv / l
            lse_g = m + jnp.log(l)
            for i in range(SWA_GROUP):
                a = g * SWA_GROUP + i
                lse_ref[:, a * LANES:(a + 1) * LANES] = lse_g[i * Q:(i + 1) * Q]
            shifted = pltpu.roll(out, HEAD_DIM, 1)
            pairs += [jnp.where(lo, out[2 * pr * Q:(2 * pr + 1) * Q], shifted[(2 * pr + 1) * Q:(2 * pr + 2) * Q])
                      for pr in range(SWA_GROUP // 2)]
        o_ref[...] = jnp.concatenate(pairs, axis=1).astype(BF16)
        if n_c:
            @pl.when((ids[0] == B - 1) & (ids[1] == nb - 1))
            def _():
                _comm_wait(kinds, c_in, c_out, sems, place)

    whole = lambda shape: pl.BlockSpec(shape, lambda b, n: (0,) * len(shape))
    res = pl.pallas_call(
        body, name=name, grid=(B, nb),
        in_specs=_swa_in_specs(nb) + [whole((SWA_HEADS, Q, 2 * Q)), whole((SWA_HEADS, Q, 1))] + [HBM_SPEC] * n_c,
        out_specs=[pl.BlockSpec((Q, SWA_HEADS * HEAD_DIM), lambda b, n: (b * nb + n, 0)),
                   pl.BlockSpec((Q, SWA_HEADS * LANES), lambda b, n: (b * nb + n, 0))] + [HBM_SPEC] * n_c,
        out_shape=[jax.ShapeDtypeStruct((B * S, SWA_HEADS * HEAD_DIM), BF16),
                   jax.ShapeDtypeStruct((B * S, SWA_HEADS * LANES), F32)] + _comm_out_shapes(comm),
        scratch_shapes=_comm_scratch(comm) if n_c else [],
        compiler_params=_cparams(*(("arbitrary",) * 2 if n_c else ("parallel",) * 2)),
    )(h, h, h, h, h, bias, sinkcol, *[a for _, a in comm])
    return res[0], res[1], list(res[2:])


def _swa_bwd(h, o, do, lse, bias, sinkcol, *, do_blk0, B, S, comm=(), name):
    Q = BLOCK_Q
    nb = S // Q
    scale = HEAD_DIM ** -0.5
    n_c, kinds = len(comm), [k for k, _ in comm]

    def body(*refs):
        c_in, c_out, sems = refs[10:10 + n_c], refs[17 + n_c:17 + 2 * n_c], refs[17 + 2 * n_c:]
        q_ref, kp_ref, ko_ref, vp_ref, vo_ref, o_ref, do_ref, lse_ref, bias_ref, sink_ref = refs[:10]
        dq_ref, dko_ref, dkp_ref, dvo_ref, dvp_ref, dbias_ref, dsink_ref = refs[10 + n_c:17 + n_c]
        ids = [pl.program_id(0), pl.program_id(1)]
        if n_c:
            place = _mesh_place()

        @pl.when((ids[0] == 0) & (ids[1] == 0))
        def _():
            dbias_ref[...] = jnp.zeros_like(dbias_ref)
            dsink_ref[...] = jnp.zeros_like(dsink_ref)
            if n_c:
                _comm_start(kinds, c_in, c_out, sems, place)

        lo, lo2, kdup, vdup, valid, qs = _swa_common(q_ref, kp_ref, ko_ref, vp_ref, vo_ref, pl.program_id(1))
        dkk, dvv, dq_pairs = [], [], []
        for g in range(SWA_KV_HEADS):
            heads, s = _swa_group_logits(g, qs, kdup, valid, bias_ref)
            lse_g = jnp.concatenate([lse_ref[:, a * LANES:(a + 1) * LANES]
                                     for a in range(g * SWA_GROUP, (g + 1) * SWA_GROUP)], axis=0)
            p = jnp.exp(s - _widen(lse_g, 2 * Q))
            do_g, o_g = [], []
            for i in range(SWA_GROUP):
                cols = slice((g * SWA_GROUP + i) // 2 * LANES, ((g * SWA_GROUP + i) // 2 + 1) * LANES)
                do_p = do_ref[:, cols]
                do_g.append(jnp.where(lo if i % 2 == 0 else jnp.logical_not(lo), do_p, jnp.zeros_like(do_p)))
                o_g.append(o_ref[:, cols])
            doh, oh = jnp.concatenate(do_g, axis=0), jnp.concatenate(o_g, axis=0)
            delta = jnp.sum(doh.astype(F32) * oh.astype(F32), -1, keepdims=True)
            dp = lax.dot_general(doh, vdup[g], NT, preferred_element_type=F32)
            ds = p * (dp - delta)
            dbias_ref[heads] += ds.reshape(SWA_GROUP, Q, 2 * Q)
            dsink_ref[heads] -= (jnp.exp(sink_ref[heads].reshape(SWA_GROUP * Q, 1) - lse_g[:, :1])
                                 * delta).reshape(SWA_GROUP, Q, 1)
            dss = (ds * scale).astype(BF16)
            dq_pairs += _pair_halves(lax.dot_general(dss, kdup[g], NN, preferred_element_type=F32), lo)
            dkk.append(lax.dot_general(dss, qs[g], TN, preferred_element_type=F32))
            dvv.append(lax.dot_general(p.astype(BF16), doh, TN, preferred_element_type=F32))
        dq_ref[...] = jnp.concatenate(dq_pairs, axis=1).astype(BF16)
        fold = lambda x: x + pltpu.roll(x, HEAD_DIM, 1)
        dk_blk = jnp.where(lo2, fold(dkk[0]), fold(dkk[1]))
        dv_blk = jnp.where(lo2, fold(dvv[0]), fold(dvv[1]))
        dkp_ref[...] = dk_blk[:Q]
        dko_ref[...] = dk_blk[Q:]
        dvp_ref[...] = dv_blk[:Q]
        dvo_ref[...] = dv_blk[Q:]
        if n_c:
            @pl.when((ids[0] == B - 1) & (ids[1] == nb - 1))
            def _():
                _comm_wait(kinds, c_in, c_out, sems, place)

    whole = lambda shape: pl.BlockSpec(shape, lambda b, n: (0,) * len(shape))
    wide = lambda blk: pl.BlockSpec((Q, SWA_HEADS * HEAD_DIM), lambda b, n: (b * nb + n, blk))
    narrow = pl.BlockSpec((Q, LANES), lambda b, n: (b * nb + n, 0))
    kv_shape = jax.ShapeDtypeStruct((B * S, LANES), F32)
    res = pl.pallas_call(
        body, name=name, grid=(B, nb),
        in_specs=_swa_in_specs(nb) + [wide(0), wide(do_blk0),
                                      pl.BlockSpec((Q, SWA_HEADS * LANES), lambda b, n: (b * nb + n, 0)),
                                      whole((SWA_HEADS, Q, 2 * Q)),
                                      whole((SWA_HEADS, Q, 1))] + [HBM_SPEC] * n_c,
        out_specs=[wide(0), narrow, narrow, narrow, narrow, whole((SWA_HEADS, Q, 2 * Q)), whole((SWA_HEADS, Q, 1))]
        + [HBM_SPEC] * n_c,
        out_shape=[jax.ShapeDtypeStruct((B * S, SWA_HEADS * HEAD_DIM), BF16), kv_shape, kv_shape, kv_shape, kv_shape,
                   jax.ShapeDtypeStruct((SWA_HEADS, Q, 2 * Q), F32), jax.ShapeDtypeStruct((SWA_HEADS, Q, 1), F32)]
        + _comm_out_shapes(comm),
        scratch_shapes=_comm_scratch(comm) if n_c else [],
        compiler_params=_cparams("arbitrary", "arbitrary"),
    )(h, h, h, h, h, o, do, lse, bias, sinkcol, *[a for _, a in comm])
    return tuple(res[:7]) + (list(res[7:]),)


def _bias_bucket_sum(dbias, bucket, *, name):
    def body(d_ref, b_ref, o_ref):
        dbv, bk = d_ref[...], b_ref[...]
        lane = lax.broadcasted_iota(jnp.int32, (SWA_HEADS, LANES), 1)
        out = jnp.zeros((SWA_HEADS, LANES), F32)
        for b in range(REL_BUCKETS):
            part = jnp.sum(jnp.where(bk == b, dbv, 0.0), axis=1)
            tot = jnp.sum(part, axis=-1, keepdims=True)
            out = out + jnp.where(lane == b, tot, 0.0)
        o_ref[...] = out

    return pl.pallas_call(
        body, name=name, out_shape=jax.ShapeDtypeStruct((SWA_HEADS, LANES), F32),
        compiler_params=pltpu.CompilerParams(vmem_limit_bytes=VMEM_LIMIT_BYTES),
    )(dbias, bucket)


def _adamw_update(w, g, m, v):
    m_new = ADAM_B1 * m + (1.0 - ADAM_B1) * g
    v_new = ADAM_B2 * v + (1.0 - ADAM_B2) * jnp.square(g)
    m_hat = m_new / (1.0 - ADAM_B1 ** ADAM_STEP)
    v_hat = v_new / (1.0 - ADAM_B2 ** ADAM_STEP)
    return -ADAM_LR * (m_hat / (jnp.sqrt(v_hat) + ADAM_EPS) + ADAM_WD * w), m_new, v_new


def _adamw(w, g, m, v, *, name):
    def body(w_ref, g_ref, m_ref, v_ref, d_ref, nm_ref, nv_ref):
        d_ref[...], nm_ref[...], nv_ref[...] = _adamw_update(w_ref[...], g_ref[...], m_ref[...], v_ref[...])

    return pl.pallas_call(
        body, name=name, out_shape=[jax.ShapeDtypeStruct(w.shape, F32)] * 3,
        compiler_params=pltpu.CompilerParams(vmem_limit_bytes=VMEM_LIMIT_BYTES),
    )(w, g, m, v)


ADAMW_PARTS_BYTES = 8 * 1024 * 1024


def _adamw_slots(w, parts, m, v, *, name):
    n0, R, C = w.shape
    tr = next((c for c in (512, 256, 128, 64, 32, 16, 8) if R % c == 0 and 4 * n0 * N_DEV * c * C <= ADAMW_PARTS_BYTES), R)

    def body(*refs):
        w_ref, p_refs, (m_ref, v_ref, g_ref, d_ref, nm_ref, nv_ref) = refs[0], refs[1:1 + n0], refs[1 + n0:]
        layer = pl.program_id(0)
        for l in range(n0):
            @pl.when(layer == l)
            def _(p_ref=p_refs[l]):
                g = p_ref[0].astype(F32)
                for j in range(1, N_DEV):
                    g = g + p_ref[j].astype(F32)
                g_ref[...] = g
                d_ref[...], nm_ref[...], nv_ref[...] = _adamw_update(w_ref[...], g, m_ref[...], v_ref[...])

    spec = pl.BlockSpec((None, tr, C), lambda l, i: (l, i, 0))
    part_spec = lambda own: pl.BlockSpec((N_DEV, tr, C), lambda l, i: (0, jnp.where(l == own, i, 0), 0))
    return pl.pallas_call(
        body, name=name, grid=(n0, R // tr),
        in_specs=[spec] + [part_spec(l) for l in range(n0)] + [spec, spec], out_specs=[spec] * 4,
        out_shape=[jax.ShapeDtypeStruct((n0, R, C), F32)] * 4, compiler_params=_cparams("arbitrary", "arbitrary"),
    )(w, *parts, m, v)


def _all_gather_hbm(blocks, *, name):
    n = len(blocks)

    def body(*refs):
        x_refs, out_refs = refs[:n], refs[n:2 * n]
        send_sems, recv_sems, local_sems = refs[2 * n:]
        x, y, c, _ = _mesh_place()
        me, sibling = (x, y, c), (x, y, 1 - c)
        chips = [(1 - x, y), (x, 1 - y), (1 - x, 1 - y)]

        def copy(w, k, blk, to, src=None):
            px, py, pc = blk
            slot = out_refs[w].at[4 * px + 2 * py + pc]
            return pltpu.make_async_remote_copy(
                src_ref=slot if src is None else src, dst_ref=slot,
                send_sem=send_sems.at[w, k], recv_sem=recv_sems.at[w, k], device_id=to, device_id_type=MESH_ID)

        mine = [pltpu.make_async_copy(x_refs[w], out_refs[w].at[4 * x + 2 * y + c], local_sems.at[w])
                for w in range(n)]
        for cp in mine:
            cp.start()
        first = []
        for w in range(n):
            first.append(copy(w, 0, me, sibling, src=x_refs[w]))
            first += [copy(w, 1 + j, me, (*chip, c), src=x_refs[w]) for j, chip in enumerate(chips)]
        for cp in first:
            cp.start()
        passed = []
        for j, chip in enumerate(chips):
            for w in range(n):
                copy(w, 1 + j, (*chip, c), me).wait_recv()
                fwd = copy(w, 4 + j, (*chip, c), sibling)
                fwd.start()
                passed.append(fwd)
        for w in range(n):
            copy(w, 0, sibling, me).wait_recv()
            for j, chip in enumerate(chips):
                copy(w, 4 + j, (*chip, 1 - c), me).wait_recv()
        for cp in first + passed:
            cp.wait_send()
        for cp in mine:
            cp.wait()

    return pl.pallas_call(
        body, name=name, out_shape=[jax.ShapeDtypeStruct((N_DEV,) + b.shape, b.dtype) for b in blocks],
        in_specs=[HBM_SPEC] * n, out_specs=[HBM_SPEC] * n,
        scratch_shapes=[pltpu.SemaphoreType.DMA((n, 7)), pltpu.SemaphoreType.DMA((n, 7)),
                        pltpu.SemaphoreType.DMA((n,))],
    )(*blocks)


def _all_reduce_small(block, *, name):
    R, W = block.shape

    def body(x_ref, out_ref, buf, send_sems, recv_sems):
        x, y, c, me = _mesh_place()
        copies = []
        for k, (peer, _) in enumerate(_peers(x, y, c)):
            copies.append(pltpu.make_async_remote_copy(
                src_ref=x_ref, dst_ref=buf.at[me], send_sem=send_sems.at[k], recv_sem=recv_sems.at[k],
                device_id=peer, device_id_type=MESH_ID))
        for cp in copies:
            cp.start()
        buf[me] = x_ref[...]
        for cp in copies:
            cp.wait_recv()
        for cp in copies:
            cp.wait_send()
        acc = buf[0]
        for j in range(1, N_DEV):
            acc = acc + buf[j]
        out_ref[...] = acc

    return pl.pallas_call(
        body, name=name, out_shape=jax.ShapeDtypeStruct((R, W), F32),
        in_specs=[VMEM_SPEC], out_specs=VMEM_SPEC,
        scratch_shapes=[pltpu.VMEM((N_DEV, R, W), F32), pltpu.SemaphoreType.DMA((7,)), pltpu.SemaphoreType.DMA((7,))],
    )(block)


def _assemble(name, g):
    if name == 'w_up':
        return g
    if BIG_AXIS[name] == 2:
        return jnp.concatenate([g[j] for j in range(N_DEV)], axis=1)
    return g.reshape(N_DEV * g.shape[1], g.shape[2])


def _split_for_devices(name, g):
    if g.ndim == 3:
        return g
    if BIG_AXIS[name] == 2:
        b = g.shape[1] // N_DEV
        return jnp.stack([g[:, j * b:(j + 1) * b] for j in range(N_DEV)]).astype(BF16)
    return g.reshape(N_DEV, g.shape[0] // N_DEV, g.shape[1]).astype(BF16)


def _layer_weight_keys(i):
    j = i // 2
    mixer = [('ev_w_in', j), ('ev_w_uq', j), ('ev_w_ukv', j), ('ev_w_out', j)] if i % 2 == 0 \
        else [('od_w_in', j), ('od_w_out', j)]
    return mixer + [('w_up', i), ('w_down', i), ('ple_w_proj', i), ('ple_w_gate', i)]


def _weight_layer(key):
    name, idx = key
    return 2 * idx if name.startswith('ev_') else 2 * idx + 1 if name.startswith('od_') else idx


FIRST_GATHER = [('ev_w_in', 0), ('ev_w_uq', 0), ('ev_w_ukv', 0), ('ev_w_out', 0)]
FWD_CARRIERS = {
    'l0_mla': [('w_up', 0), ('ple_w_proj', 0), ('ple_w_gate', 0)],
    'l0_swa': [('w_down', 0)],
    'l0_out_ln1': [('od_w_out', 0)],
    'l0_up': [('od_w_in', 0)],
    'l0_down_ln2': [('w_up', 1)],
    'l0_ple_gate': [('ple_w_proj', 1), ('ple_w_gate', 1)],
    'l1_fox': [('w_down', 1), ('ev_w_in', 1), ('ev_w_uq', 1), ('ev_w_ukv', 1), ('ev_w_out', 1), ('w_up', 2)],
    'l1_up': [('w_down', 2)],
    'l1_down_ln2': [('ple_w_proj', 2), ('ple_w_gate', 2)],
    'l2_mla': [('od_w_in', 1), ('od_w_out', 1)],
    'l2_swa': [('w_up', 3)],
    'l2_up': [('w_down', 3)],
    'l2_down_ln2': [('ple_w_proj', 3), ('ple_w_gate', 3)],
}


class _MeshExchange:
    def __init__(self, shards):
        self.shards = shards
        self.weights = {i: {} for i in range(DEPTH)}
        self.pending = []
        self.in_flight = []
        self.received = {}
        got = _all_gather_hbm([self.shards[n][idx] for n, idx in FIRST_GATHER], name="gather_first")
        self._landed(FIRST_GATHER, got)

    def _landed(self, keys, gathered):
        for k, g in zip(keys, gathered):
            self.weights[_weight_layer(k)][k[0]] = _assemble(k[0], g)

    def layer_weights(self, i):
        return self.weights[i]

    def carry(self, kernel_name):
        return [(("gather", idx), self.shards[n]) for n, idx in FWD_CARRIERS.get(kernel_name, [])]

    def carried(self, kernel_name, outs):
        self._landed(FWD_CARRIERS.get(kernel_name, []), outs)

    def push_grads(self, grads):
        self.pending += [(k, _split_for_devices(k[0], g)) for k, g in grads.items()]

    def bwd_items(self):
        self.in_flight, self.pending = self.pending, []
        return [("scatter", parts) for _, parts in self.in_flight]

    def bwd_done(self, outs):
        for (k, _), got in zip(self.in_flight, outs):
            self.received[k] = got
        self.in_flight = []

    def finish(self):
        if self.pending:
            outs = _exchange(self.bwd_items(), name="scatter_rest")
            self.bwd_done(outs)
        return self.received


PACK_ROWS = 8


def _pack_small(vals):
    flat = jnp.concatenate([vals[n].reshape(-1).astype(F32) for n in SMALL])
    pad = (-flat.shape[0]) % (PACK_ROWS * LANES)
    return jnp.pad(flat, (0, pad)).reshape(-1, LANES)


def _unpack_small(block, shapes):
    flat = block.reshape(-1)
    out, off = {}, 0
    for n in SMALL:
        sz = math.prod(shapes[n])
        out[n] = flat[off:off + sz].reshape(shapes[n])
        off += sz
    return out


def _rope_tables(S):
    half = MLA_ROPE // 2
    inv = 1.0 / (ROPE_THETA ** (jnp.arange(0, MLA_ROPE, 2, dtype=F32) / MLA_ROPE))
    ang = jnp.arange(S, dtype=F32)[:, None] * inv[None, :]
    cos, sin = jnp.cos(ang), jnp.sin(ang)
    zeros = jnp.zeros((S, half), F32)
    tail = jnp.zeros((S, LANES - MLA_QK), F32)

    def block(rope_part, nope_val):
        return jnp.concatenate([jnp.full((S, MLA_NOPE), nope_val, F32), rope_part, tail], -1)

    a_r = jnp.concatenate([cos, cos], -1)
    bm_r = jnp.concatenate([-sin, zeros], -1)
    bp_r = jnp.concatenate([zeros, sin], -1)
    q_tabs = tuple(block(r, v) for r, v in ((a_r, 1.0), (bm_r, 0.0), (bp_r, 0.0)))
    k_tabs = tuple(block(r, 0.0) for r in (a_r, bm_r, bp_r))
    return q_tabs, k_tabs


def _t5_bucket(dist):
    exact = REL_BUCKETS // 2
    d = jnp.maximum(dist, 1).astype(F32)
    large = exact + (jnp.log(d / exact) / math.log(REL_MAX_DIST / exact) * (REL_BUCKETS - exact)).astype(jnp.int32)
    large = jnp.minimum(large, REL_BUCKETS - 1)
    return jnp.where(dist < exact, dist, large)


def _swa_bucket_table():
    a = jnp.arange(BLOCK_Q)[:, None]
    col = jnp.arange(2 * BLOCK_Q)[None, :]
    return _t5_bucket(jnp.maximum(a + BLOCK_Q - col, 0)).astype(jnp.int32)


def _even_weights(W):
    w = W['ev_w_in']
    c_kv1 = MLA_Q_LORA + MLA_KV_LORA
    c_kr1 = c_kv1 + MLA_ROPE
    c_qs1 = c_kr1 + SWA_HEADS * HEAD_DIM
    zeros = lambda n: jnp.zeros((D_MODEL, n), w.dtype)
    w_in = jnp.concatenate([w[:, c_kr1:c_qs1], w[:, :c_kv1], w[:, c_qs1:], zeros(KR_LANE0), w[:, c_kv1:c_kr1],
                            zeros(LANES - KR_LANE0 - MLA_ROPE)], axis=1)
    uq = W['ev_w_uq'].reshape(MLA_Q_LORA, MLA_HEADS, MLA_QK)
    w_uq = jnp.pad(uq, ((0, 0), (0, 0), (0, LANES - MLA_QK))).reshape(MLA_Q_LORA, MLA_HEADS * LANES)
    ukv = W['ev_w_ukv'].reshape(MLA_KV_LORA, MLA_HEADS, MLA_NOPE + MLA_V)
    w_k = jnp.pad(ukv[..., :MLA_NOPE], ((0, 0), (0, 0), (0, LANES - MLA_NOPE))).reshape(MLA_KV_LORA, -1)
    w_v = ukv[..., MLA_NOPE:].reshape(MLA_KV_LORA, MLA_HEADS * MLA_V)
    return w_in, w_uq, w_k, w_v, W['ev_w_out']


def _even_in_grad_unpad(dw):
    kr0 = EV_KR[0] + KR_LANE0
    return jnp.concatenate([dw[:, EV_CQ[0]:EV_CKV[1]], dw[:, kr0:kr0 + MLA_ROPE], dw[:, EV_QS[0]:EV_QS[1]],
                            dw[:, EV_KS[0]:EV_VS[1]]], axis=1)


def _even_fwd(xb, W, P, i, B, S, tabs, xchg, tag):
    j = i // 2
    q_tabs, k_tabs, bias, sinkcol = tabs
    w_in, w_uq, w_k, w_v, w_out = _even_weights(W)
    h = _mm(xb, w_in, name=f"{tag}_in")
    cqn, ckvn, rq, rkv = _even_norms(h, P['ev_q_norm'][j][None], P['ev_kv_norm'][j][None], name=f"{tag}_norms")
    q = _rope(_mm(cqn, w_uq, name=f"{tag}_uq"), q_tabs, S, sign=1.0, name=f"{tag}_ropeq")
    knp = _mm(ckvn, w_k, out_dtypes=(BF16,), name=f"{tag}_uk")
    v = _mm(ckvn, w_v, out_dtypes=(BF16,), name=f"{tag}_uv")
    k = _mla_keys(knp, h, k_tabs, S, name=f"{tag}_keys")
    o_mla, lse_mla, got = _flash_fwd(q, k, v, q_blk0=0, k_blk0=0, v_blk0=0, W=2 * LANES, n_pairs=MLA_HEADS // 2,
                                     B=B, S=S, scale=MLA_QK ** -0.5, comm=xchg.carry(f"{tag}_mla"), name=f"{tag}_mla")
    xchg.carried(f"{tag}_mla", got)
    o_swa, lse_swa, got = _swa_fwd(h, bias, sinkcol, B=B, S=S, comm=xchg.carry(f"{tag}_swa"), name=f"{tag}_swa")
    xchg.carried(f"{tag}_swa", got)
    res = dict(h=h, cqn=cqn, ckvn=ckvn, rq=rq, rkv=rkv, q=q, k=k, v=v, o_mla=o_mla, lse_mla=lse_mla,
               o_swa=o_swa, lse_swa=lse_swa)
    return ((o_mla, o_swa), w_out), res


def _shift_prev(own, prev, B, S):
    prev = prev.reshape(B, S, LANES)
    shifted = jnp.concatenate([prev[:, BLOCK_Q:], jnp.zeros_like(prev[:, :BLOCK_Q])], axis=1)
    return (own + shifted.reshape(B * S, LANES)).astype(BF16)


def _even_bwd(dmb, dz1, xb, W, P, j, B, S, tabs, res, xchg, tag):
    q_tabs, k_tabs, bias, sinkcol = tabs
    w_in, w_uq, w_k, w_v, w_out = _even_weights(W)
    g = {}
    g['ev_w_out'] = jnp.concatenate([_mm_tn(res['o_mla'], dmb, name=f"{tag}_dwout_mla"),
                                     _mm_tn(res['o_swa'], dmb, name=f"{tag}_dwout_swa")], axis=0)
    do = _mm(dmb, w_out, trans_b=True, out_dtypes=(BF16,), name=f"{tag}_do")
    dq, dk, dv, got = _flash_bwd(res['q'], res['k'], res['v'], res['o_mla'], do, res['lse_mla'], q_blk0=0, k_blk0=0,
                                 v_blk0=0, do_blk0=0, W=2 * LANES, n_pairs=MLA_HEADS // 2, B=B, S=S,
                                 scale=MLA_QK ** -0.5, qk_dtype=F32, comm=xchg.bwd_items(), name=f"{tag}_mla_bwd")
    xchg.bwd_done(got)
    dq_pre = _rope(dq, q_tabs, S, sign=-1.0, name=f"{tag}_ropeq_bwd")
    dw_uq = _mm_tn(res['cqn'], dq_pre, name=f"{tag}_dwuq")
    g['ev_w_uq'] = dw_uq.reshape(MLA_Q_LORA, MLA_HEADS, LANES)[..., :MLA_QK].reshape(MLA_Q_LORA, MLA_HEADS * MLA_QK)
    dcqn = _mm(dq_pre, w_uq, trans_b=True, name=f"{tag}_dcqn")
    dw_k = _mm_tn(res['ckvn'], dk, name=f"{tag}_dwuk").reshape(MLA_KV_LORA, MLA_HEADS, LANES)[..., :MLA_NOPE]
    dw_v = _mm_tn(res['ckvn'], dv, name=f"{tag}_dwuv").reshape(MLA_KV_LORA, MLA_HEADS, MLA_V)
    g['ev_w_ukv'] = jnp.concatenate([dw_k, dw_v], axis=-1).reshape(MLA_KV_LORA, MLA_HEADS * (MLA_NOPE + MLA_V))
    dckvn_v = _mm(dv, w_v, trans_b=True, name=f"{tag}_dckvn_v")
    dckvn = _mm(dk, w_k, trans_b=True, extras=(dckvn_v,), epilogue=lambda acc, r: (acc + r,), name=f"{tag}_dckvn")
    dkr_pre = _mla_rope_key_grad(dk, k_tabs, S, name=f"{tag}_ropek_bwd")
    xchg.push_grads({(n, j): g.pop(n) for n in list(g)})
    dqs, dko, dkp, dvo, dvp, dbias, dsink, got = _swa_bwd(res['h'], res['o_swa'], do, res['lse_swa'], bias, sinkcol,
                                                          do_blk0=1, B=B, S=S, comm=xchg.bwd_items(),
                                                          name=f"{tag}_swa_bwd")
    xchg.bwd_done(got)
    dh, dgq, dgkv = _even_in_bwd(res['h'], res['rq'], res['rkv'], P['ev_q_norm'][j][None], P['ev_kv_norm'][j][None],
                                 dcqn, dckvn, dqs, _shift_prev(dko, dkp, B, S), _shift_prev(dvo, dvp, B, S), dkr_pre,
                                 name=f"{tag}_in_bwd")
    g['ev_w_in'] = _even_in_grad_unpad(_mm_tn(xb, dh, name=f"{tag}_dwin"))
    xchg.push_grads({(n, j): val for n, val in g.items()})
    dx_kwargs = dict(trans_b=True, extras=(dz1,), epilogue=lambda acc, r: (acc + DN_ALPHA * r,), name=f"{tag}_dx")
    dx = _scattering(xchg, _mm, dh, w_in, **dx_kwargs) if j == 0 else _mm(dh, w_in, **dx_kwargs)
    small = dict(ev_q_norm=dgq[0], ev_kv_norm=dgkv[0], dbias=dbias, ev_sinks=jnp.sum(dsink, axis=(1, 2)))
    return dx, small


def _odd_fwd(xb, W, P, i, B, S, xchg, tag):
    j = i // 2
    w = W['od_w_in']
    w_qkv = w[:, :ODD_QKV]
    w_f = jnp.pad(w[:, ODD_QKV:], ((0, 0), (0, LANES - FOX_HEADS)))
    bf = jnp.pad(P['od_b_f'][j], (0, LANES - FOX_HEADS))[None]
    qkv = _mm(xb, w_qkv, out_dtypes=(BF16,), name=f"{tag}_qkv")
    f = _mm(xb, w_f, name=f"{tag}_f").reshape(B, S, LANES)
    csh, chs = _fox_decay_fwd(f, bf, name=f"{tag}_decay")
    crow = chs[:, :FOX_HEADS].reshape(B, FOX_HEADS, S // ATT_TILE, 1, ATT_TILE)
    n_blk = FOX_HEADS * HEAD_DIM // LANES
    o, lse, got = _flash_fwd(qkv, qkv, qkv, q_blk0=0, k_blk0=n_blk, v_blk0=2 * n_blk, W=LANES,
                             n_pairs=FOX_HEADS // 2, B=B, S=S, scale=HEAD_DIM ** -0.5, csh=csh, crow=crow,
                             comm=xchg.carry(f"{tag}_fox"), name=f"{tag}_fox")
    xchg.carried(f"{tag}_fox", got)
    res = dict(f=f, bf=bf, csh=csh, crow=crow, qkv=qkv, o=o, lse=lse, w_qkv=w_qkv, w_f=w_f)
    return (o, W['od_w_out']), res


def _odd_bwd(dmb, dz1, xb, W, P, j, B, S, res, xchg, tag):
    g = {}
    w_out = W['od_w_out']
    g['od_w_out'] = _mm_tn(res['o'], dmb, name=f"{tag}_dwout")
    do = _mm(dmb, w_out, trans_b=True, out_dtypes=(BF16,), name=f"{tag}_do")
    qkv = res['qkv']
    n_blk = FOX_HEADS * HEAD_DIM // LANES
    dq, dk, dv, dck, dcq, got = _flash_bwd(qkv, qkv, qkv, res['o'], do, res['lse'], q_blk0=0, k_blk0=n_blk,
                                           v_blk0=2 * n_blk, do_blk0=0, W=LANES, n_pairs=FOX_HEADS // 2, B=B, S=S,
                                           scale=HEAD_DIM ** -0.5, qk_dtype=BF16, csh=res['csh'], crow=res['crow'],
                                           comm=xchg.bwd_items(), name=f"{tag}_fox_bwd")
    xchg.bwd_done(got)
    dc = dck.reshape(B, FOX_HEADS, S) + dcq.reshape(B, FOX_HEADS, S)
    dc_hs = jnp.pad(dc, ((0, 0), (0, LANES - FOX_HEADS), (0, 0)))
    df, dbf = _fox_decay_bwd(dc_hs, res['f'], res['bf'], name=f"{tag}_decay_bwd")
    df = df.reshape(B * S, LANES)
    dw_qkv = [_mm_tn(xb, t, name=f"{tag}_dw{n}") for n, t in (("q", dq), ("k", dk), ("v", dv))]
    dw_f = _mm_tn(xb, df, name=f"{tag}_dwf")
    g['od_w_in'] = jnp.concatenate(dw_qkv + [dw_f[:, :FOX_HEADS]], axis=1)
    dxf = _mm(df, res['w_f'], trans_b=True, extras=(dz1,), epilogue=lambda acc, r: (acc + DN_ALPHA * r,),
              name=f"{tag}_dxf")
    xchg.push_grads({(n, j): val for n, val in g.items()})
    dx = _mm((dq, dk, dv), res['w_qkv'], trans_b=True, extras=(dxf,), epilogue=lambda acc, r: (acc + r,),
             name=f"{tag}_dx")
    small = dict(od_b_f=dbf[0, :FOX_HEADS])
    return dx, small


def _carrying(xchg, name, call, *args, **kwargs):
    comm = xchg.carry(name)
    out = call(*args, comm=comm, name=name, **kwargs)
    if comm:
        out, got = out
        xchg.carried(name, got)
    return out


def _scattering(xchg, call, *args, **kwargs):
    comm = xchg.bwd_items()
    out = call(*args, comm=comm, **kwargs)
    if comm:
        out, got = out
        xchg.bwd_done(got)
    return out


def _local_step(x, p, target, P, xchg):
    B, S, D = x.shape
    T = B * S
    q_tabs, k_tabs = _rope_tables(S)
    bucket = _swa_bucket_table()
    in_bucket = (bucket[..., None] == jnp.arange(REL_BUCKETS)).astype(F32)
    bias = jnp.einsum('acb,bh->hac', in_bucket, P['rel_bias'], precision=lax.Precision.HIGHEST)

    xc = x.reshape(T, D)
    xcb = xc.astype(BF16)
    saved = []
    for i in range(DEPTH):
        j = i // 2
        tag = f"l{i}"
        W = xchg.layer_weights(i)
        lay = dict(xb=xcb, W=W)
        if i % 2 == 0:
            sinkcol = jnp.broadcast_to(P['ev_sinks'][j][:, None, None], (SWA_HEADS, BLOCK_Q, 1)).astype(F32)
            lay['tabs'] = (q_tabs, k_tabs, bias, sinkcol)
            (o, w_out), lay['mix'] = _even_fwd(xcb, W, P, i, B, S, lay['tabs'], xchg, tag)
        else:
            (o, w_out), lay['mix'] = _odd_fwd(xcb, W, P, i, B, S, xchg, tag)
        x1, x1b, lay['xh1'], lay['r1'] = _carrying(xchg, f"{tag}_out_ln1", _mm_ln, o, w_out, xc,
                                                   P['ln1_g'][i][None], P['ln1_b'][i][None])
        lay['x1b'] = x1b
        lay['u'], lay['a'] = _carrying(xchg, f"{tag}_up", _mm, x1b, W['w_up'], out_dtypes=(F32, BF16),
                                       epilogue=lambda acc: (acc, jnp.square(jnp.maximum(acc, 0.0))))
        x2, x2b, lay['xh2'], lay['r2'] = _carrying(xchg, f"{tag}_down_ln2", _mm_ln, lay['a'], W['w_down'], x1,
                                                   P['ln2_g'][i][None], P['ln2_b'][i][None])
        lay['x2b'] = x2b
        lay['p'] = p[i].reshape(T, D_PLE)
        lay['e'] = _mm(lay['p'], W['ple_w_proj'], name=f"{tag}_ple_proj")

        def gate(acc, bg, e, x2v):
            gv = 1.0 / (1.0 + jnp.exp(-(acc + bg)))
            y = x2v + gv * e
            return y, y, gv

        xc, xcb, lay['g'] = _carrying(xchg, f"{tag}_ple_gate", _mm, x2b, W['ple_w_gate'],
                                      extras=(P['ple_b_gate'][i][None], lay['e'], x2), epilogue=gate,
                                      out_dtypes=(F32, BF16, F32))
        saved.append(lay)

    dy, sq = _loss_grad(xc, target.reshape(T, D), name="loss")

    Gs = {n: [None] * DEPTH for n in ('ln1_g', 'ln1_b', 'ln2_g', 'ln2_b', 'ple_b_gate')}
    Gs.update({n: [None] * (DEPTH // 2) for n in ('ev_q_norm', 'ev_kv_norm', 'ev_sinks', 'od_b_f')})
    dbias_total = None
    for i in reversed(range(DEPTH)):
        j = i // 2
        tag = f"l{i}b"
        lay = saved[i]
        W = lay['W']
        de, dzg, dbg = _ple_bwd_elem(dy, lay['g'], lay['e'], name=f"{tag}_ple_elem")
        Gs['ple_b_gate'][i] = dbg[0]
        g_mlp = {('ple_w_proj', i): _mm_tn(lay['p'], de, slot_width=D_MODEL // N_DEV, name=f"{tag}_dwproj"),
                 ('ple_w_gate', i): _mm_tn(lay['x2b'], dzg, name=f"{tag}_dwgate")}
        dz2, dz2b, dg2, db2 = _mm_ln_bwd(dzg, W['ple_w_gate'], dy, 1.0, lay['xh2'], lay['r2'], P['ln2_g'][i][None],
                                         name=f"{tag}_dx2_ln2")
        Gs['ln2_g'][i], Gs['ln2_b'][i] = dg2[0], db2[0]
        g_mlp[('w_down', i)] = _mm_tn(lay['a'], dz2b, name=f"{tag}_dwdown")
        du = _mm(dz2b, W['w_down'], trans_b=True, extras=(lay['u'],), out_dtypes=(BF16,),
                 epilogue=lambda acc, u: (acc * (2.0 * jnp.maximum(u, 0.0)),), name=f"{tag}_du")
        g_mlp[('w_up', i)] = _mm_tn(lay['x1b'], du, slot_width=D_FF // N_DEV, name=f"{tag}_dwup")
        xchg.push_grads(g_mlp)
        dz1, dz1b, dg1, db1 = _mm_ln_bwd(du, W['w_up'], dz2, DN_ALPHA, lay['xh1'], lay['r1'], P['ln1_g'][i][None],
                                         name=f"{tag}_dx1_ln1")
        Gs['ln1_g'][i], Gs['ln1_b'][i] = dg1[0], db1[0]
        if i % 2 == 0:
            dy, small = _even_bwd(dz1b, dz1, lay['xb'], W, P, j, B, S, lay['tabs'], lay['mix'], xchg, tag)
            dbias_total = small['dbias'] if dbias_total is None else dbias_total + small['dbias']
            for n in ('ev_q_norm', 'ev_kv_norm', 'ev_sinks'):
                Gs[n][j] = small[n]
        else:
            dy, small = _odd_bwd(dz1b, dz1, lay['xb'], W, P, j, B, S, lay['mix'], xchg, tag)
            Gs['od_b_f'][j] = small['od_b_f']

    grads_small = {n: jnp.stack(v) for n, v in Gs.items()}
    drel = _bias_bucket_sum(dbias_total, bucket, name="rel_bias_grad")
    grads_small['rel_bias'] = drel[:, :REL_BUCKETS].T
    return sq, dy.reshape(B, S, D), grads_small


def kernel(x, p, rel_bias, ev_w_in, ev_q_norm, ev_w_uq, ev_kv_norm, ev_w_ukv, ev_sinks, ev_w_out, od_w_in, od_b_f, od_w_out, ln1_g, ln1_b, w_up, w_down, ln2_g, ln2_b, ple_w_proj, ple_w_gate, ple_b_gate, loss_target, m_rel_bias, m_ev_w_in, m_ev_q_norm, m_ev_w_uq, m_ev_kv_norm, m_ev_w_ukv, m_ev_sinks, m_ev_w_out, m_od_w_in, m_od_b_f, m_od_w_out, m_ln1_g, m_ln1_b, m_w_up, m_w_down, m_ln2_g, m_ln2_b, m_ple_w_proj, m_ple_w_gate, m_ple_b_gate, v_rel_bias, v_ev_w_in, v_ev_q_norm, v_ev_w_uq, v_ev_kv_norm, v_ev_w_ukv, v_ev_sinks, v_ev_w_out, v_od_w_in, v_od_b_f, v_od_w_out, v_ln1_g, v_ln1_b, v_w_up, v_w_down, v_ln2_g, v_ln2_b, v_ple_w_proj, v_ple_w_gate, v_ple_b_gate):
    given = dict(locals())
    w = {n: given[n] for n in WEIGHTS}
    mom = {n: given["m_" + n] for n in WEIGHTS}
    var = {n: given["v_" + n] for n in WEIGHTS}
    small_shapes = {n: w[n].shape for n in SMALL}

    xchg = _MeshExchange({n: w[n].astype(BF16) for n in BIG})
    P = {n: w[n] for n in SMALL}

    sq, grad_x, grads_small = _local_step(x, p, loss_target, P, xchg)
    loss = lax.psum(0.5 * jnp.sum(sq) / D_MODEL, ("x", "y", "c"))

    received = xchg.finish()
    g_small_packed = _all_reduce_small(_pack_small(grads_small), name="reduce_small_grads")
    g_small = _unpack_small(g_small_packed, small_shapes)

    grad, delta, new_m, new_v = {}, {}, {}, {}
    for n in BIG:
        parts = [received[(n, idx)] for idx in range(w[n].shape[0])]
        grad[n], delta[n], new_m[n], new_v[n] = _adamw_slots(w[n], parts, mom[n], var[n], name=f"adamw_{n}")
    d, nm, nv = _adamw(_pack_small(w), g_small_packed, _pack_small(mom), _pack_small(var), name="adamw_small")
    d, nm, nv = (_unpack_small(t, small_shapes) for t in (d, nm, nv))
    for n in SMALL:
        grad[n], delta[n], new_m[n], new_v[n] = g_small[n], d[n], nm[n], nv[n]

    return (loss, grad_x, *[grad[n] for n in WEIGHTS], *[delta[n] for n in WEIGHTS],
            *[new_m[n] for n in WEIGHTS], *[new_v[n] for n in WEIGHTS])
```

```python
import math

import jax
import jax.numpy as jnp
from jax import lax
from jax.experimental import pallas as pl
from jax.experimental.pallas import tpu as pltpu

F32, BF16 = jnp.float32, jnp.bfloat16

D_MODEL = 1024
DEPTH = 4
HEAD_DIM = 64
MLA_HEADS, MLA_NOPE, MLA_ROPE, MLA_V = 8, 64, 32, 64
MLA_Q_LORA, MLA_KV_LORA = 384, 256
MLA_QK = MLA_NOPE + MLA_ROPE
ROPE_THETA = 10000.0
SWA_HEADS, SWA_KV_HEADS, SWA_WINDOW = 8, 2, 128
SWA_GROUP = SWA_HEADS // SWA_KV_HEADS
REL_BUCKETS, REL_MAX_DIST = 32, 128
FOX_HEADS = 16
D_FF = 4 * D_MODEL
D_PLE = 256
BLOCK_Q = 128
DN_ALPHA = (2 * DEPTH) ** 0.25
NORM_EPS = 1e-5
NEG_INF = -1e30
EVEN_IN = 1440
ODD_QKV = 3 * FOX_HEADS * HEAD_DIM
LANES = 128

EV_QS = (0, 512)
EV_CQ = (512, 896)
EV_CKV = (896, 1152)
EV_KS = (1152, 1280)
EV_VS = (1280, 1408)
EV_KR = (1408, 1536)
EVEN_IN_PAD = 1536
KR_LANE0 = MLA_NOPE

ADAM_LR, ADAM_B1, ADAM_B2, ADAM_EPS, ADAM_WD, ADAM_STEP = 0.001, 0.9, 0.999, 1e-08, 0.01, 10

N_DEV = 8
VMEM_LIMIT_BYTES = 48 * 1024 * 1024
ATT_TILE = 512
ATT_TILE_BWD = 512
PAIRS_PER_STEP_FWD = 4
PAIRS_PER_STEP_BWD = 2

NN = (((1,), (0,)), ((), ()))
NT = (((1,), (1,)), ((), ()))
TN = (((0,), (0,)), ((), ()))

BIG = ['ev_w_in', 'ev_w_uq', 'ev_w_ukv', 'ev_w_out', 'od_w_in', 'od_w_out', 'w_up', 'w_down',
       'ple_w_proj', 'ple_w_gate']
BIG_AXIS = {'ev_w_in': 2, 'ev_w_uq': 2, 'ev_w_ukv': 2, 'ev_w_out': 1, 'od_w_in': 2, 'od_w_out': 1,
            'w_up': 2, 'w_down': 1, 'ple_w_proj': 2, 'ple_w_gate': 1}
SMALL = ['rel_bias', 'ev_q_norm', 'ev_kv_norm', 'ev_sinks', 'od_b_f', 'ln1_g', 'ln1_b', 'ln2_g', 'ln2_b',
         'ple_b_gate']
WEIGHTS = ['rel_bias', 'ev_w_in', 'ev_q_norm', 'ev_w_uq', 'ev_kv_norm', 'ev_w_ukv', 'ev_sinks', 'ev_w_out',
           'od_w_in', 'od_b_f', 'od_w_out', 'ln1_g', 'ln1_b', 'w_up', 'w_down', 'ln2_g', 'ln2_b',
           'ple_w_proj', 'ple_w_gate', 'ple_b_gate']


def _cparams(*sem):
    return pltpu.CompilerParams(dimension_semantics=sem, vmem_limit_bytes=VMEM_LIMIT_BYTES)


def _pick(n, cands):
    for c in cands:
        if n % c == 0:
            return c
    return n


MM_STEP_BYTES = 10 * 1024 * 1024
MM_OUT_BYTES = 8 * 1024 * 1024
MM_CHUNK = 512


def _mm(a, b, *, trans_b=False, extras=(), epilogue=None, row_epilogue=None, out_dtypes=(F32,), out_widths=None,
        n_sums=0, comm=(), name):
    a_parts = tuple(a) if isinstance(a, (tuple, list)) else (a,)
    n_a = len(a_parts)
    M = a_parts[0].shape[0]
    k_offs = [sum(p.shape[1] for p in a_parts[:i]) for i in range(n_a + 1)]
    slot_w = b.shape[2] if b.ndim == 3 else None
    if slot_w is None:
        N = b.shape[0] if trans_b else b.shape[1]
    else:
        assert n_a == 1 and slot_w % LANES == 0
        N = b.shape[1] if trans_b else b.shape[0] * slot_w
    n_ex, n_out = len(extras), len(out_dtypes)
    n_rows_out = n_out - n_sums
    out_widths = (N,) * n_out if out_widths is None else out_widths
    row_bytes = sum(p.shape[1] * p.dtype.itemsize for p in a_parts) + (sum(w * jnp.dtype(d).itemsize
                                            for w, d in zip(out_widths[:n_rows_out], out_dtypes))
                                        + sum(e.shape[1] * e.dtype.itemsize for e in extras if e.shape[0] == M)
                                        + (4 * N if row_epilogue is not None else 0))
    tm = next((c for c in (1024, 512, 256) if M % c == 0 and c * row_bytes <= MM_STEP_BYTES), 128)
    nc = _pick(N, (MM_CHUNK, 384, 256, 128)) if slot_w is None or trans_b else slot_w
    n_c, kinds = len(comm), [k for k, _ in comm]
    n_scr = 1 if row_epilogue is not None else 0

    def body(*refs):
        a_refs, refs = refs[:n_a], refs[n_a - 1:]
        c_in = refs[2 + n_ex:2 + n_ex + n_c]
        c_out = refs[2 + n_ex + n_c + n_out:2 + n_ex + 2 * n_c + n_out]
        sems = refs[2 + n_ex + 2 * n_c + n_out + n_scr:]
        refs = refs[:2 + n_ex] + refs[2 + n_ex + n_c:2 + n_ex + n_c + n_out] \
            + refs[2 + n_ex + 2 * n_c + n_out:2 + n_ex + 2 * n_c + n_out + n_scr]
        if n_c:
            place = _mesh_place()
            step = pl.program_id(0)

            @pl.when(step == 0)
            def _():
                _comm_start(kinds, c_in, c_out, sems, place)

        b_ref = refs[1]
        ex = refs[2:2 + n_ex]
        outs = refs[2 + n_ex:2 + n_ex + n_out]
        avs = [r[...].astype(BF16) for r in a_refs]
        for n0 in range(0, N, nc):
            cols = slice(n0, n0 + nc)
            acc = None
            if slot_w is None:
                terms = [(av, b_ref[cols, k0:k1] if trans_b else b_ref[k0:k1, cols])
                         for av, k0, k1 in zip(avs, k_offs[:-1], k_offs[1:])]
            elif trans_b:
                terms = [(avs[0][:, sl * slot_w:(sl + 1) * slot_w], b_ref[sl, cols, :]) for sl in range(b.shape[0])]
            else:
                terms = [(avs[0], b_ref[n0 // slot_w])]
            for av, bv in terms:
                part = lax.dot_general(av, bv.astype(BF16), NT if trans_b else NN, preferred_element_type=F32)
                acc = part if acc is None else acc + part
            if row_epilogue is not None:
                refs[-1][:, cols] = acc
                continue
            res = epilogue(acc, *[e[:, cols] for e in ex]) if epilogue is not None else (acc,)
            for o, r in zip(outs, res):
                o[:, cols] = r.astype(o.dtype)
        if row_epilogue is not None:
            res = row_epilogue(refs[-1][...], *[e[...] for e in ex])
            for o, r in zip(outs[:n_rows_out], res):
                o[...] = r.astype(o.dtype)
            if n_sums:
                @pl.when(pl.program_id(0) == 0)
                def _():
                    for o in outs[n_rows_out:]:
                        o[...] = jnp.zeros_like(o)

                for o, r in zip(outs[n_rows_out:], res[n_rows_out:]):
                    o[...] += r
        if n_c:
            @pl.when(step == M // tm - 1)
            def _():
                _comm_wait(kinds, c_in, c_out, sems, place)

    in_specs = [pl.BlockSpec((tm, p.shape[1]), lambda i: (i, 0)) for p in a_parts]
    in_specs.append(pl.BlockSpec(b.shape, lambda i: (0,) * b.ndim))
    for e in extras:
        if e.shape[0] == M:
            in_specs.append(pl.BlockSpec((tm, e.shape[1]), lambda i: (i, 0)))
        elif e.shape == (1, N):
            in_specs.append(pl.BlockSpec((1, N), lambda i: (0, 0)))
        else:
            raise ValueError(f"extra operand of shape {e.shape} for a ({M}, {N}) result")
    res = pl.pallas_call(
        body, name=name, grid=(M // tm,), in_specs=in_specs + [HBM_SPEC] * n_c,
        out_specs=[pl.BlockSpec((tm, w), lambda i: (i, 0)) for w in out_widths[:n_rows_out]]
        + [pl.BlockSpec((1, w), lambda i: (0, 0)) for w in out_widths[n_rows_out:]] + [HBM_SPEC] * n_c,
        out_shape=[jax.ShapeDtypeStruct((M, w), d) for w, d in zip(out_widths[:n_rows_out], out_dtypes)]
        + [jax.ShapeDtypeStruct((1, w), d) for w, d in zip(out_widths[n_rows_out:], out_dtypes[n_rows_out:])]
        + _comm_out_shapes(comm),
        scratch_shapes=([pltpu.VMEM((tm, N), F32)] if row_epilogue is not None else [])
        + (_comm_scratch(comm) if n_c else []),
        compiler_params=_cparams("arbitrary" if n_sums or n_c else "parallel"),
    )(*a_parts, b, *extras, *[c for _, c in comm])
    main = res[0] if n_out == 1 else tuple(res[:n_out])
    return (main, list(res[n_out:])) if n_c else main


def _mm_tn(a, b, *, slot_width=None, name):
    T, K = a.shape
    N = b.shape[1]
    bk, bn = K, N
    while bk * bn * 4 > MM_OUT_BYTES:
        if bn >= bk and bn % (2 * LANES) == 0:
            bn //= 2
        else:
            bk //= 2
    tt = _pick(T, (1024, 512, 256))
    ck, cn = _pick(bk, (MM_CHUNK, 384, 256, 128)), _pick(bn, (MM_CHUNK, 384, 256, 128))

    def body(a_ref, b_ref, o_ref, acc_ref):
        t = pl.program_id(2)

        @pl.when(t == 0)
        def _():
            acc_ref[...] = jnp.zeros_like(acc_ref)

        for r0 in range(0, bk, ck):
            av = a_ref[:, r0:r0 + ck].astype(BF16)
            for c0 in range(0, bn, cn):
                acc_ref[r0:r0 + ck, c0:c0 + cn] += lax.dot_general(
                    av, b_ref[:, c0:c0 + cn].astype(BF16), TN, preferred_element_type=F32)

        @pl.when(t == T // tt - 1)
        def _():
            if slot_width is None:
                o_ref[...] = acc_ref[...].astype(o_ref.dtype)
            else:
                for slot in range(bn // slot_width):
                    o_ref[slot] = acc_ref[:, slot * slot_width:(slot + 1) * slot_width].astype(o_ref.dtype)

    if slot_width is None:
        out_spec, out_shape = pl.BlockSpec((bk, bn), lambda i, j, t: (i, j)), (K, N)
    else:
        assert bn % slot_width == 0 and slot_width % LANES == 0
        out_spec = pl.BlockSpec((bn // slot_width, bk, slot_width), lambda i, j, t: (j, i, 0))
        out_shape = (N // slot_width, K, slot_width)
    return pl.pallas_call(
        body, name=name, grid=(K // bk, N // bn, T // tt),
        in_specs=[pl.BlockSpec((tt, bk), lambda i, j, t: (t, i)), pl.BlockSpec((tt, bn), lambda i, j, t: (t, j))],
        out_specs=out_spec, out_shape=jax.ShapeDtypeStruct(out_shape, BF16),
        scratch_shapes=[pltpu.VMEM((bk, bn), F32)],
        compiler_params=_cparams("parallel", "parallel", "arbitrary"),
    )(a, b)


ROW_TILE = 512


def _row_spec(cols, col_block=0):
    return pl.BlockSpec((ROW_TILE, cols), lambda i: (i, col_block))


def _tab_spec(cols, period):
    return pl.BlockSpec((ROW_TILE, cols), lambda i: (i % period, 0))


def _full_spec(shape):
    return pl.BlockSpec(shape, lambda i: (0,) * len(shape))


def _mm_ln(a, w, x, g, b, *, comm=(), name):
    def ln_rows(m, xv, gv, bv):
        z = DN_ALPHA * xv + m
        mu = jnp.mean(z, -1, keepdims=True)
        zc = z - mu
        r = lax.rsqrt(jnp.mean(zc * zc, -1, keepdims=True) + NORM_EPS)
        xh = zc * r
        y = xh * gv + bv
        return y, y, xh, jnp.broadcast_to(r, (r.shape[0], LANES))

    D = w.shape[1]
    return _mm(a, w, extras=(x, g, b), row_epilogue=ln_rows, out_dtypes=(F32, BF16, F32, F32),
               out_widths=(D, D, D, LANES), comm=comm, name=name)


def _mm_ln_bwd(a, w, resid, resid_scale, xh, r, g, *, name):
    def ln_bwd_rows(acc, rv, xhv, rstd, gv):
        dyv = acc + resid_scale * rv
        dyg = dyv * gv
        c1 = jnp.mean(dyg, -1, keepdims=True)
        c2 = jnp.mean(dyg * xhv, -1, keepdims=True)
        dz = _widen(rstd, dyv.shape[-1]) * (dyg - c1 - xhv * c2)
        return dz, dz, jnp.sum(dyv * xhv, 0, keepdims=True), jnp.sum(dyv, 0, keepdims=True)

    D = resid.shape[1]
    return _mm(a, w, trans_b=True, extras=(resid, xh, r, g), row_epilogue=ln_bwd_rows,
               out_dtypes=(F32, BF16, F32, F32), out_widths=(D, D, D, D), n_sums=2, name=name)


def _loss_grad(y, target, *, name):
    T, D = y.shape

    def body(y_ref, t_ref, dy_ref, sq_ref):
        err = y_ref[...] - t_ref[...]
        dy_ref[...] = err / D

        @pl.when(pl.program_id(0) == 0)
        def _():
            sq_ref[...] = jnp.zeros_like(sq_ref)

        sq_ref[...] += jnp.sum(err * err, 0, keepdims=True)

    return pl.pallas_call(
        body, name=name, grid=(T // ROW_TILE,),
        in_specs=[_row_spec(D), _row_spec(D)],
        out_specs=[_row_spec(D), _full_spec((1, D))],
        out_shape=[jax.ShapeDtypeStruct((T, D), F32), jax.ShapeDtypeStruct((1, D), F32)],
        compiler_params=_cparams("arbitrary"),
    )(y, target)


def _ple_bwd_elem(dx3, g, e, *, name):
    T, D = dx3.shape

    def body(dx_ref, g_ref, e_ref, de_ref, dz_ref, db_ref):
        dx, gv = dx_ref[...], g_ref[...]
        de_ref[...] = (dx * gv).astype(BF16)
        dz = dx * e_ref[...] * gv * (1.0 - gv)
        dz_ref[...] = dz.astype(BF16)

        @pl.when(pl.program_id(0) == 0)
        def _():
            db_ref[...] = jnp.zeros_like(db_ref)

        db_ref[...] += jnp.sum(dz, 0, keepdims=True)

    return pl.pallas_call(
        body, name=name, grid=(T // ROW_TILE,),
        in_specs=[_row_spec(D), _row_spec(D), _row_spec(D)],
        out_specs=[_row_spec(D), _row_spec(D), _full_spec((1, D))],
        out_shape=[jax.ShapeDtypeStruct((T, D), BF16), jax.ShapeDtypeStruct((T, D), BF16),
                   jax.ShapeDtypeStruct((1, D), F32)],
        compiler_params=_cparams("arbitrary"),
    )(dx3, g, e)


def _rotate(xv, a, bm, bp, sign):
    half = MLA_ROPE // 2
    width = xv.shape[-1]
    a, bm, bp = (_widen(t, width) for t in (a, bm, bp))
    return xv * a + sign * (pltpu.roll(xv, width - half, 1) * bm + pltpu.roll(xv, half, 1) * bp)


def _rope(x, tabs, seq, *, sign, name):
    T, width = x.shape

    def body(x_ref, a_ref, bm_ref, bp_ref, o_ref):
        o_ref[...] = _rotate(x_ref[...], a_ref[...], bm_ref[...], bp_ref[...], sign).astype(BF16)

    return pl.pallas_call(
        body, name=name, grid=(T // ROW_TILE,),
        in_specs=[_row_spec(width)] + [_tab_spec(LANES, seq // ROW_TILE)] * 3,
        out_specs=_row_spec(width),
        out_shape=jax.ShapeDtypeStruct((T, width), BF16),
        compiler_params=_cparams("parallel"),
    )(x, *tabs)


def _mla_keys(knp, h, k_tabs, seq, *, name):
    T = knp.shape[0]

    def body(k_ref, h_ref, a_ref, bm_ref, bp_ref, o_ref):
        kr = _rotate(h_ref[...], a_ref[...], bm_ref[...], bp_ref[...], 1.0)
        for hd in range(MLA_HEADS):
            cols = slice(hd * LANES, (hd + 1) * LANES)
            o_ref[:, cols] = (k_ref[:, cols].astype(F32) + kr).astype(BF16)

    return pl.pallas_call(
        body, name=name, grid=(T // ROW_TILE,),
        in_specs=[_row_spec(MLA_HEADS * LANES), _row_spec(LANES, EV_KR[0] // LANES)]
        + [_tab_spec(LANES, seq // ROW_TILE)] * 3,
        out_specs=_row_spec(MLA_HEADS * LANES),
        out_shape=jax.ShapeDtypeStruct((T, MLA_HEADS * LANES), BF16),
        compiler_params=_cparams("parallel"),
    )(knp, h, *k_tabs)


def _mla_rope_key_grad(dk, k_tabs, seq, *, name):
    T = dk.shape[0]

    def body(dk_ref, a_ref, bm_ref, bp_ref, o_ref):
        tot = dk_ref[:, 0:LANES]
        for hd in range(1, MLA_HEADS):
            tot = tot + dk_ref[:, hd * LANES:(hd + 1) * LANES]
        o_ref[...] = _rotate(tot, a_ref[...], bm_ref[...], bp_ref[...], -1.0).astype(BF16)

    return pl.pallas_call(
        body, name=name, grid=(T // ROW_TILE,),
        in_specs=[_row_spec(MLA_HEADS * LANES)] + [_tab_spec(LANES, seq // ROW_TILE)] * 3,
        out_specs=_row_spec(LANES),
        out_shape=jax.ShapeDtypeStruct((T, LANES), BF16),
        compiler_params=_cparams("parallel"),
    )(dk, *k_tabs)


def _even_norms(h, gq, gkv, *, name):
    T = h.shape[0]

    def body(h_ref, gq_ref, gkv_ref, cq_ref, ckv_ref, rq_ref, rkv_ref):
        cq = h_ref[:, EV_CQ[0]:EV_CQ[1]]
        rq = lax.rsqrt(jnp.mean(cq * cq, -1, keepdims=True) + NORM_EPS)
        cq_ref[...] = (cq * rq * gq_ref[...]).astype(BF16)
        rq_ref[...] = jnp.broadcast_to(rq, rq_ref.shape)
        ckv = h_ref[:, EV_CKV[0]:EV_CKV[1]]
        rkv = lax.rsqrt(jnp.mean(ckv * ckv, -1, keepdims=True) + NORM_EPS)
        ckv_ref[...] = (ckv * rkv * gkv_ref[...]).astype(BF16)
        rkv_ref[...] = jnp.broadcast_to(rkv, rkv_ref.shape)

    return pl.pallas_call(
        body, name=name, grid=(T // ROW_TILE,),
        in_specs=[_row_spec(EVEN_IN_PAD), _full_spec((1, MLA_Q_LORA)), _full_spec((1, MLA_KV_LORA))],
        out_specs=[_row_spec(MLA_Q_LORA), _row_spec(MLA_KV_LORA), _row_spec(LANES), _row_spec(LANES)],
        out_shape=[jax.ShapeDtypeStruct((T, MLA_Q_LORA), BF16), jax.ShapeDtypeStruct((T, MLA_KV_LORA), BF16),
                   jax.ShapeDtypeStruct((T, LANES), F32), jax.ShapeDtypeStruct((T, LANES), F32)],
        compiler_params=_cparams("parallel"),
    )(h, gq, gkv)


def _even_in_bwd(h, rq, rkv, gq, gkv, dcqn, dckvn, dqs, dks, dvs, dkr, *, name):
    T = h.shape[0]

    def rms_bwd(c, r, g, dy):
        r = _widen(r, c.shape[-1])
        xr = c * r
        dyg = dy * g
        return r * (dyg - xr * jnp.mean(dyg * xr, -1, keepdims=True)), jnp.sum(dy * xr, 0, keepdims=True)

    def body(h_ref, rq_ref, rkv_ref, gq_ref, gkv_ref, dcq_ref, dckv_ref, dqs_ref, dks_ref, dvs_ref, dkr_ref,
             dh_ref, dgq_ref, dgkv_ref):
        @pl.when(pl.program_id(0) == 0)
        def _():
            dgq_ref[...] = jnp.zeros_like(dgq_ref)
            dgkv_ref[...] = jnp.zeros_like(dgkv_ref)

        dcq, dgq = rms_bwd(h_ref[:, EV_CQ[0]:EV_CQ[1]], rq_ref[...], gq_ref[...], dcq_ref[...])
        dckv, dgkv = rms_bwd(h_ref[:, EV_CKV[0]:EV_CKV[1]], rkv_ref[...], gkv_ref[...], dckv_ref[...])
        dgq_ref[...] += dgq
        dgkv_ref[...] += dgkv
        dh_ref[:, EV_QS[0]:EV_QS[1]] = dqs_ref[...]
        dh_ref[:, EV_CQ[0]:EV_CQ[1]] = dcq.astype(BF16)
        dh_ref[:, EV_CKV[0]:EV_CKV[1]] = dckv.astype(BF16)
        dh_ref[:, EV_KS[0]:EV_KS[1]] = dks_ref[...]
        dh_ref[:, EV_VS[0]:EV_VS[1]] = dvs_ref[...]
        dh_ref[:, EV_KR[0]:EV_KR[1]] = dkr_ref[...]

    return pl.pallas_call(
        body, name=name, grid=(T // ROW_TILE,),
        in_specs=[_row_spec(EVEN_IN_PAD), _row_spec(LANES), _row_spec(LANES), _full_spec((1, MLA_Q_LORA)),
                  _full_spec((1, MLA_KV_LORA)), _row_spec(MLA_Q_LORA), _row_spec(MLA_KV_LORA),
                  _row_spec(SWA_HEADS * HEAD_DIM), _row_spec(LANES), _row_spec(LANES), _row_spec(LANES)],
        out_specs=[_row_spec(EVEN_IN_PAD), _full_spec((1, MLA_Q_LORA)), _full_spec((1, MLA_KV_LORA))],
        out_shape=[jax.ShapeDtypeStruct((T, EVEN_IN_PAD), BF16), jax.ShapeDtypeStruct((1, MLA_Q_LORA), F32),
                   jax.ShapeDtypeStruct((1, MLA_KV_LORA), F32)],
        compiler_params=_cparams("arbitrary"),
    )(h, rq, rkv, gq, gkv, dcqn, dckvn, dqs, dks, dvs, dkr)


def _fox_decay_fwd(f3, bf, *, name):
    B, S, _ = f3.shape

    def body(f_ref, b_ref, csh_ref, chs_ref):
        x = f_ref[...] + b_ref[...]
        c = jnp.minimum(x, 0.0) - jnp.log1p(jnp.exp(-jnp.abs(x)))
        row = lax.broadcasted_iota(jnp.int32, (S, LANES), 0)
        k = 1
        while k < S:
            c = c + jnp.where(row >= k, pltpu.roll(c, k, 0), 0.0)
            k *= 2
        csh_ref[...] = c
        chs_ref[...] = c.T

    return pl.pallas_call(
        body, name=name, grid=(B,),
        in_specs=[pl.BlockSpec((None, S, LANES), lambda b: (b, 0, 0)), pl.BlockSpec((1, LANES), lambda b: (0, 0))],
        out_specs=[pl.BlockSpec((None, S, LANES), lambda b: (b, 0, 0)),
                   pl.BlockSpec((None, LANES, S), lambda b: (b, 0, 0))],
        out_shape=[jax.ShapeDtypeStruct((B, S, LANES), F32), jax.ShapeDtypeStruct((B, LANES, S), F32)],
        compiler_params=_cparams("parallel"),
    )(f3, bf)


def _fox_decay_bwd(dc_hs, f3, bf, *, name):
    B, S, _ = f3.shape

    def body(dc_ref, f_ref, b_ref, df_ref, db_ref):
        g = dc_ref[...].T
        row = lax.broadcasted_iota(jnp.int32, (S, LANES), 0)
        k = 1
        while k < S:
            g = g + jnp.where(row < S - k, pltpu.roll(g, S - k, 0), 0.0)
            k *= 2
        x = f_ref[...] + b_ref[...]
        df = g * (1.0 / (1.0 + jnp.exp(x)))
        df_ref[...] = df.astype(BF16)

        @pl.when(pl.program_id(0) == 0)
        def _():
            db_ref[...] = jnp.zeros_like(db_ref)

        db_ref[...] += jnp.sum(df, 0, keepdims=True)

    return pl.pallas_call(
        body, name=name, grid=(B,),
        in_specs=[pl.BlockSpec((None, LANES, S), lambda b: (b, 0, 0)),
                  pl.BlockSpec((None, S, LANES), lambda b: (b, 0, 0)), pl.BlockSpec((1, LANES), lambda b: (0, 0))],
        out_specs=[pl.BlockSpec((None, S, LANES), lambda b: (b, 0, 0)), pl.BlockSpec((1, LANES), lambda b: (0, 0))],
        out_shape=[jax.ShapeDtypeStruct((B, S, LANES), BF16), jax.ShapeDtypeStruct((1, LANES), F32)],
        compiler_params=_cparams("arbitrary"),
    )(dc_hs, f3, bf)


def _head_column(block, h):
    lane = lax.broadcasted_iota(jnp.int32, block.shape, 1)
    return jnp.sum(jnp.where(lane == h, block, 0.0), axis=-1, keepdims=True)


def _causal_mask(s):
    r = lax.broadcasted_iota(jnp.int32, s.shape, 0)
    c = lax.broadcasted_iota(jnp.int32, s.shape, 1)
    return jnp.where(c <= r, s, NEG_INF)


def _low_half(shape):
    return (lax.broadcasted_iota(jnp.int32, shape, 1) % LANES) < HEAD_DIM


def _widen(x, cols):
    return jnp.concatenate([x] * (cols // LANES), axis=1)


def _both_halves(x, lo):
    r = pltpu.roll(x, HEAD_DIM, 1)
    return jnp.where(lo, x, r), jnp.where(lo, r, x)


MESH_ID = pl.DeviceIdType.MESH
HBM_SPEC = pl.BlockSpec(memory_space=pltpu.HBM)
VMEM_SPEC = pl.BlockSpec(memory_space=pltpu.VMEM)


def _mesh_place():
    x, y, c = lax.axis_index("x"), lax.axis_index("y"), lax.axis_index("c")
    return x, y, c, 4 * x + 2 * y + c


def _peers(x, y, c):
    out = []
    for mask in range(1, N_DEV):
        dx, dy, dc = (mask >> 2) & 1, (mask >> 1) & 1, mask & 1
        px, py, pc = (1 - x if dx else x), (1 - y if dy else y), (1 - c if dc else c)
        out.append(((px, py, pc), 4 * px + 2 * py + pc))
    return out


def _comm_out_shapes(comm):
    return [jax.ShapeDtypeStruct(a.shape if kind == "scatter" else (N_DEV,) + a.shape[1:], a.dtype) for kind, a in comm]


def _comm_scratch(comm):
    n = len(comm)
    return [pltpu.SemaphoreType.DMA((n, 7)), pltpu.SemaphoreType.DMA((n, 7)), pltpu.SemaphoreType.DMA((n,))]


def _comm_copies(kinds, in_refs, out_refs, sems, place):
    send_sems, recv_sems, local_sems = sems
    x, y, c, me = place
    local, remote = [], []
    for w, kind in enumerate(kinds):
        mine = in_refs[w].at[me] if kind == "scatter" else in_refs[w].at[kind[1]]
        local.append(pltpu.make_async_copy(mine, out_refs[w].at[me], local_sems.at[w]))
        for k, (peer, peer_idx) in enumerate(_peers(x, y, c)):
            remote.append(pltpu.make_async_remote_copy(
                src_ref=in_refs[w].at[peer_idx] if kind == "scatter" else mine, dst_ref=out_refs[w].at[me],
                send_sem=send_sems.at[w, k], recv_sem=recv_sems.at[w, k], device_id=peer, device_id_type=MESH_ID))
    return local, remote


def _comm_start(kinds, in_refs, out_refs, sems, place):
    local, remote = _comm_copies(kinds, in_refs, out_refs, sems, place)
    for cp in local + remote:
        cp.start()


def _comm_wait(kinds, in_refs, out_refs, sems, place):
    local, remote = _comm_copies(kinds, in_refs, out_refs, sems, place)
    for cp in remote:
        cp.wait_recv()
    for cp in remote:
        cp.wait_send()
    for cp in local:
        cp.wait()


def _exchange(comm, *, name):
    n = len(comm)
    kinds = [k for k, _ in comm]

    def body(*refs):
        place = _mesh_place()
        _comm_start(kinds, refs[:n], refs[n:2 * n], refs[2 * n:], place)
        _comm_wait(kinds, refs[:n], refs[n:2 * n], refs[2 * n:], place)

    return pl.pallas_call(
        body, name=name, out_shape=_comm_out_shapes(comm), in_specs=[HBM_SPEC] * n, out_specs=[HBM_SPEC] * n,
        scratch_shapes=_comm_scratch(comm),
    )(*[a for _, a in comm])


def _flash_fwd(qa, ka, va, *, q_blk0, k_blk0, v_blk0, W, n_pairs, B, S, scale, csh=None, crow=None, comm=(), name):
    t = ATT_TILE
    nq = S // t
    P = PAIRS_PER_STEP_FWD
    decay = csh is not None
    split = W == LANES
    assert n_pairs % P == 0 and q_blk0 % P == 0 and k_blk0 % P == 0 and v_blk0 % P == 0
    n_c, kinds = len(comm), [k for k, _ in comm]
    n_in = 5 if decay else 3
    fold_scale = math.log2(scale).is_integer()
    n_steps = (B, n_pairs // P, nq)

    def body(*refs):
        c_in, c_out = refs[n_in:n_in + n_c], refs[n_in + n_c + 2:n_in + 2 * n_c + 2]
        sems = refs[n_in + 2 * n_c + 4:]
        refs = refs[:n_in] + refs[n_in + n_c:n_in + n_c + 2] + refs[n_in + 2 * n_c + 2:n_in + 2 * n_c + 4]
        if decay:
            q_ref, k_ref, v_ref, csh_ref, crow_ref, o_ref, lse_ref, m_s, acc_s = refs
        else:
            q_ref, k_ref, v_ref, o_ref, lse_ref, m_s, acc_s = refs
        g, i = pl.program_id(1), pl.program_id(2)
        if n_c:
            place = _mesh_place()
            ids = [pl.program_id(ax) for ax in range(3)]

            @pl.when((ids[0] == 0) & (ids[1] == 0) & (ids[2] == 0))
            def _():
                _comm_start(kinds, c_in, c_out, sems, place)

        lo = _low_half((t, LANES))
        qv = q_ref[...]
        qh = []
        for pr in range(P):
            qp = qv[:, pr * W:(pr + 1) * W]
            qh += [jnp.where(lo, qp, jnp.zeros_like(qp)), jnp.where(lo, jnp.zeros_like(qp), qp)] if split \
                else [qp[:, :LANES], qp[:, LANES:]]
        if fold_scale:
            qh = [x * scale for x in qh]
        if decay:
            cq = [jnp.broadcast_to(_head_column(csh_ref[...], 2 * P * g + hd), (t, LANES)) for hd in range(2 * P)]
        m_s[...] = jnp.full(m_s.shape, NEG_INF, F32)
        acc_s[...] = jnp.zeros(acc_s.shape, F32)

        def step(j, masked):
            rows = pl.ds(pl.multiple_of(j * t, t), t)
            kb, vb = k_ref[rows, :], v_ref[rows, :]
            for pr in range(P):
                kp, vp = kb[:, pr * W:(pr + 1) * W], vb[:, pr * LANES:(pr + 1) * LANES]
                ones = jnp.ones_like(vp)
                vaug = [jnp.where(lo, vp, ones), jnp.where(lo, ones, vp)]
                for half in range(2):
                    hd = 2 * pr + half
                    kh = kp if split else kp[:, half * LANES:(half + 1) * LANES]
                    s = lax.dot_general(qh[hd], kh, NT, preferred_element_type=F32)
                    if not fold_scale:
                        s = s * scale
                    if decay:
                        s = s + _widen(cq[hd], t) - crow_ref[hd, j]
                    if masked:
                        s = _causal_mask(s)
                    m_prev = m_s[hd]
                    m_new = jnp.maximum(m_prev, jnp.max(s, -1, keepdims=True))
                    p = jnp.exp(s - _widen(m_new, t))
                    acc_s[hd] = jnp.exp(m_prev - m_new) * acc_s[hd] + lax.dot_general(
                        p.astype(BF16), vaug[half], NN, preferred_element_type=F32)
                    m_s[hd] = m_new

        def loop_body(j, carry):
            step(j, False)
            return carry

        lax.fori_loop(0, i, loop_body, 0)
        step(i, True)
        for pr in range(P):
            acc0, acc1 = acc_s[2 * pr], acc_s[2 * pr + 1]
            _, l0 = _both_halves(acc0, lo)
            l1, _ = _both_halves(acc1, lo)
            cols = slice(pr * LANES, (pr + 1) * LANES)
            o_ref[:, cols] = jnp.where(lo, acc0 / l0, acc1 / l1).astype(BF16)
            lse_ref[:, cols] = jnp.where(lo, m_s[2 * pr] + jnp.log(l0), m_s[2 * pr + 1] + jnp.log(l1))
        if n_c:
            @pl.when((ids[0] == n_steps[0] - 1) & (ids[1] == n_steps[1] - 1) & (ids[2] == n_steps[2] - 1))
            def _():
                _comm_wait(kinds, c_in, c_out, sems, place)

    in_specs = [pl.BlockSpec((t, P * W), lambda b, g, i: (b * nq + i, q_blk0 // P + g)),
                pl.BlockSpec((S, P * W), lambda b, g, i: (b, k_blk0 // P + g)),
                pl.BlockSpec((S, P * LANES), lambda b, g, i: (b, v_blk0 // P + g))]
    args = [qa, ka, va]
    if decay:
        in_specs += [pl.BlockSpec((None, t, LANES), lambda b, g, i: (b, i, 0)),
                     pl.BlockSpec((None, 2 * P, nq, 1, t), lambda b, g, i: (b, g, 0, 0, 0))]
        args += [csh, crow]
    out_spec = pl.BlockSpec((t, P * LANES), lambda b, g, i: (b * nq + i, g))
    res = pl.pallas_call(
        body, name=name, grid=n_steps, in_specs=in_specs + [HBM_SPEC] * n_c,
        out_specs=[out_spec, out_spec] + [HBM_SPEC] * n_c,
        out_shape=[jax.ShapeDtypeStruct((B * S, n_pairs * LANES), BF16),
                   jax.ShapeDtypeStruct((B * S, n_pairs * LANES), F32)] + _comm_out_shapes(comm),
        scratch_shapes=[pltpu.VMEM((2 * P, t, LANES), F32), pltpu.VMEM((2 * P, t, LANES), F32)]
        + (_comm_scratch(comm) if n_c else []),
        compiler_params=_cparams(*(("arbitrary",) * 3 if n_c else ("parallel",) * 3)),
    )(*args, *[a for _, a in comm])
    return res[0], res[1], list(res[2:])


def _flash_bwd(qa, ka, va, oa, doa, lsea, *, q_blk0, k_blk0, v_blk0, do_blk0, W, n_pairs, B, S, scale, qk_dtype,
               csh=None, crow=None, comm=(), name):
    t = ATT_TILE_BWD
    nq = S // t
    P = PAIRS_PER_STEP_BWD
    decay = csh is not None
    if decay:
        crow = crow.reshape(B, 2 * n_pairs, nq, 1, t)
    split = W == LANES
    assert n_pairs % P == 0 and q_blk0 % P == 0 and k_blk0 % P == 0 and v_blk0 % P == 0 and do_blk0 % P == 0
    n_c, kinds = len(comm), [k for k, _ in comm]
    n_in, n_out, n_scr = (8, 5, 8) if decay else (6, 3, 5)
    n_steps = (B, n_pairs // P, nq)

    def body(*refs):
        c_in = refs[n_in:n_in + n_c]
        c_out = refs[n_in + n_c + n_out:n_in + 2 * n_c + n_out]
        sems = refs[n_in + 2 * n_c + n_out + n_scr:]
        refs = (refs[:n_in] + refs[n_in + n_c:n_in + n_c + n_out]
                + refs[n_in + 2 * n_c + n_out:n_in + 2 * n_c + n_out + n_scr])
        if n_c:
            place = _mesh_place()
            ids = [pl.program_id(ax) for ax in range(3)]

            @pl.when((ids[0] == 0) & (ids[1] == 0) & (ids[2] == 0))
            def _():
                _comm_start(kinds, c_in, c_out, sems, place)

        if decay:
            (q_ref, k_ref, v_ref, o_ref, do_ref, lse_ref, csh_ref, crow_ref, dq_ref, dk_ref, dv_ref, dck_ref, dcq_ref,
             dq_s, lse_s, delta_s, dk_s, dv_s, cq_s, dcq_s, dck_s) = refs
        else:
            (q_ref, k_ref, v_ref, o_ref, do_ref, lse_ref, dq_ref, dk_ref, dv_ref,
             dq_s, lse_s, delta_s, dk_s, dv_s) = refs
        g, j = pl.program_id(1), pl.program_id(2)
        lo = _low_half((t, LANES))

        @pl.when(j == 0)
        def _():
            lo_s = _low_half((S, LANES))
            dq_s[...] = jnp.zeros(dq_s.shape, F32)
            for pr in range(P):
                cols = slice(pr * LANES, (pr + 1) * LANES)
                lse_s[2 * pr], lse_s[2 * pr + 1] = _both_halves(lse_ref[:, cols], lo_s)
                dd = do_ref[:, cols].astype(F32) * o_ref[:, cols].astype(F32)
                delta_s[2 * pr] = jnp.broadcast_to(jnp.sum(jnp.where(lo_s, dd, 0.0), -1, keepdims=True), (S, LANES))
                delta_s[2 * pr + 1] = jnp.broadcast_to(jnp.sum(jnp.where(lo_s, 0.0, dd), -1, keepdims=True),
                                                       (S, LANES))
            if decay:
                for hd in range(2 * P):
                    cq_s[hd] = jnp.broadcast_to(_head_column(csh_ref[...], 2 * P * g + hd), (S, LANES))
                dcq_s[...] = jnp.zeros(dcq_s.shape, F32)

        kb, vb = k_ref[...], v_ref[...]
        kh, vh = [], []
        for pr in range(P):
            kp, vp = kb[:, pr * W:(pr + 1) * W], vb[:, pr * LANES:(pr + 1) * LANES]
            zk, zv = jnp.zeros_like(kp), jnp.zeros_like(vp)
            kh += [jnp.where(lo, kp, zk), jnp.where(lo, zk, kp)] if split else [kp[:, :LANES], kp[:, LANES:]]
            vh += [jnp.where(lo, vp, zv), jnp.where(lo, zv, vp)]
        dk_s[...] = jnp.zeros(dk_s.shape, F32)
        dv_s[...] = jnp.zeros(dv_s.shape, F32)
        if decay:
            dck_s[...] = jnp.zeros(dck_s.shape, F32)

        def step(i, masked):
            rows = pl.ds(pl.multiple_of(i * t, t), t)
            qi, doi = q_ref[rows, :], do_ref[rows, :]
            for pr in range(P):
                qp, dop = qi[:, pr * W:(pr + 1) * W], doi[:, pr * LANES:(pr + 1) * LANES]
                for half in range(2):
                    hd = 2 * pr + half
                    qx = qp if split else qp[:, half * LANES:(half + 1) * LANES]
                    s = lax.dot_general(qx, kh[hd], NT, preferred_element_type=F32) * scale
                    if decay:
                        s = s + _widen(cq_s[hd, rows, :], t) - crow_ref[hd, j]
                    if masked:
                        s = _causal_mask(s)
                    p = jnp.exp(s - _widen(lse_s[hd, rows, :], t))
                    dv_s[hd] += lax.dot_general(p.astype(BF16), dop, TN, preferred_element_type=F32)
                    dp = lax.dot_general(dop, vh[hd], NT, preferred_element_type=F32)
                    ds = p * (dp - _widen(delta_s[hd, rows, :], t))
                    dss = (ds * scale).astype(BF16)
                    dk_s[hd] += lax.dot_general(dss, qx, TN, preferred_element_type=F32)
                    dqc = lax.dot_general(dss, kh[hd], NN, preferred_element_type=F32)
                    if split:
                        dq_s[rows, pr * W:(pr + 1) * W] += dqc
                    else:
                        dq_s[rows, hd * LANES:(hd + 1) * LANES] += dqc
                    if decay:
                        dck_s[hd] -= jnp.sum(ds, 0, keepdims=True)
                        part = ds[:, :LANES]
                        for c in range(1, t // LANES):
                            part = part + ds[:, c * LANES:(c + 1) * LANES]
                        dcq_s[hd, rows, :] += part

        def loop_body(i, carry):
            step(i, False)
            return carry

        step(j, True)
        lax.fori_loop(j + 1, nq, loop_body, 0)
        for pr in range(P):
            if split:
                dk_ref[:, pr * W:(pr + 1) * W] = jnp.where(lo, dk_s[2 * pr], dk_s[2 * pr + 1]).astype(dk_ref.dtype)
            else:
                for half in range(2):
                    hd = 2 * pr + half
                    dk_ref[:, hd * LANES:(hd + 1) * LANES] = dk_s[hd].astype(dk_ref.dtype)
            dv_ref[:, pr * LANES:(pr + 1) * LANES] = jnp.where(lo, dv_s[2 * pr], dv_s[2 * pr + 1]).astype(BF16)
        if decay:
            dck_ref[...] = dck_s[...]

        @pl.when(j == nq - 1)
        def _():
            dq_ref[...] = dq_s[...].astype(dq_ref.dtype)
            if decay:
                for hd in range(2 * P):
                    dcq_ref[hd] = jnp.sum(dcq_s[hd].T, 0, keepdims=True)

        if n_c:
            @pl.when((ids[0] == n_steps[0] - 1) & (ids[1] == n_steps[1] - 1) & (ids[2] == n_steps[2] - 1))
            def _():
                _comm_wait(kinds, c_in, c_out, sems, place)

    full = lambda w, blk0: pl.BlockSpec((S, P * w), lambda b, g, j: (b, blk0 // P + g))
    blk = lambda w, blk0: pl.BlockSpec((t, P * w), lambda b, g, j: (b * nq + j, blk0 // P + g))
    in_specs = [full(W, q_blk0), blk(W, k_blk0), blk(LANES, v_blk0), full(LANES, 0), full(LANES, do_blk0),
                full(LANES, 0)]
    args = [qa, ka, va, oa, doa, lsea]
    T = B * S
    out_specs = [full(W, 0), blk(W, 0), blk(LANES, 0)]
    out_shape = [jax.ShapeDtypeStruct((T, n_pairs * W), qk_dtype), jax.ShapeDtypeStruct((T, n_pairs * W), qk_dtype),
                 jax.ShapeDtypeStruct((T, n_pairs * LANES), BF16)]
    per_head = lambda rows: pltpu.VMEM((2 * P, rows, LANES), F32)
    scratch = [pltpu.VMEM((S, P * W), F32), per_head(S), per_head(S), per_head(t), per_head(t)]
    if decay:
        in_specs += [pl.BlockSpec((None, S, LANES), lambda b, g, j: (b, 0, 0)),
                     pl.BlockSpec((None, 2 * P, nq, 1, t), lambda b, g, j: (b, g, 0, 0, 0))]
        args += [csh, crow]
        out_specs += [pl.BlockSpec((None, 2 * P, None, 1, t), lambda b, g, j: (b, g, j, 0, 0)),
                      pl.BlockSpec((None, 2 * P, 1, S), lambda b, g, j: (b, g, 0, 0))]
        out_shape += [jax.ShapeDtypeStruct((B, 2 * n_pairs, nq, 1, t), F32),
                      jax.ShapeDtypeStruct((B, 2 * n_pairs, 1, S), F32)]
        scratch += [per_head(S), per_head(S), pltpu.VMEM((2 * P, 1, t), F32)]
    res = pl.pallas_call(
        body, name=name, grid=n_steps, in_specs=in_specs + [HBM_SPEC] * n_c,
        out_specs=out_specs + [HBM_SPEC] * n_c, out_shape=out_shape + _comm_out_shapes(comm),
        scratch_shapes=scratch + (_comm_scratch(comm) if n_c else []),
        compiler_params=_cparams(*(("arbitrary",) * 3 if n_c else ("parallel", "parallel", "arbitrary"))),
    )(*args, *[a for _, a in comm])
    return tuple(res[:n_out]) + (list(res[n_out:]),)


def _swa_common(q_ref, kp_ref, ko_ref, vp_ref, vo_ref, n):
    Q = BLOCK_Q
    lo = _low_half((Q, LANES))
    lo2 = _low_half((2 * Q, LANES))
    kk = jnp.concatenate([kp_ref[...], ko_ref[...]], axis=0)
    vv = jnp.concatenate([vp_ref[...], vo_ref[...]], axis=0)
    kdup = [x.astype(BF16) for x in _both_halves(kk, lo2)]
    vdup = [x.astype(BF16) for x in _both_halves(vv, lo2)]
    a = lax.broadcasted_iota(jnp.int32, (SWA_GROUP * Q, 2 * Q), 0) % Q
    col = lax.broadcasted_iota(jnp.int32, (SWA_GROUP * Q, 2 * Q), 1)
    dist = a + Q - col
    valid = (dist >= 0) & (dist < SWA_WINDOW) & ((col >= Q) | (n > 0))
    qv = q_ref[...]
    qm = []
    for a_head in range(SWA_HEADS):
        qp = qv[:, (a_head // 2) * LANES:(a_head // 2 + 1) * LANES]
        keep = lo if a_head % 2 == 0 else jnp.logical_not(lo)
        qm.append(jnp.where(keep, qp, 0.0).astype(BF16))
    qs = [jnp.concatenate(qm[g * SWA_GROUP:(g + 1) * SWA_GROUP], axis=0) for g in range(SWA_KV_HEADS)]
    return lo, lo2, kdup, vdup, valid, qs


def _swa_group_logits(g, qs, kdup, valid, bias_ref):
    heads = slice(g * SWA_GROUP, (g + 1) * SWA_GROUP)
    s = lax.dot_general(qs[g], kdup[g], NT, preferred_element_type=F32) * (HEAD_DIM ** -0.5)
    s = s + bias_ref[heads].reshape(SWA_GROUP * BLOCK_Q, 2 * BLOCK_Q)
    return heads, jnp.where(valid, s, NEG_INF)


def _pair_halves(x, lo):
    Q = BLOCK_Q
    return [jnp.where(lo, x[2 * pr * Q:(2 * pr + 1) * Q], x[(2 * pr + 1) * Q:(2 * pr + 2) * Q])
            for pr in range(SWA_GROUP // 2)]


def _swa_in_specs(nb):
    Q = BLOCK_Q
    own = lambda blk: (lambda b, n: (b * nb + n, blk))
    prev = lambda blk: (lambda b, n: (b * nb + jnp.maximum(n - 1, 0), blk))
    kb, vb = EV_KS[0] // LANES, EV_VS[0] // LANES
    return [pl.BlockSpec((Q, SWA_HEADS * HEAD_DIM), own(0)), pl.BlockSpec((Q, LANES), prev(kb)),
            pl.BlockSpec((Q, LANES), own(kb)), pl.BlockSpec((Q, LANES), prev(vb)), pl.BlockSpec((Q, LANES), own(vb))]


def _swa_fwd(h, bias, sinkcol, *, B, S, comm=(), name):
    Q = BLOCK_Q
    nb = S // Q
    n_c, kinds = len(comm), [k for k, _ in comm]

    def body(*refs):
        c_in, c_out, sems = refs[7:7 + n_c], refs[9 + n_c:9 + 2 * n_c], refs[9 + 2 * n_c:]
        q_ref, kp_ref, ko_ref, vp_ref, vo_ref, bias_ref, sink_ref = refs[:7]
        o_ref, lse_ref = refs[7 + n_c:9 + n_c]
        if n_c:
            place = _mesh_place()
            ids = [pl.program_id(0), pl.program_id(1)]

            @pl.when((ids[0] == 0) & (ids[1] == 0))
            def _():
                _comm_start(kinds, c_in, c_out, sems, place)

        lo, lo2, kdup, vdup, valid, qs = _swa_common(q_ref, kp_ref, ko_ref, vp_ref, vo_ref, pl.program_id(1))
        pairs = []
        lo4 = _low_half((SWA_GROUP * Q, LANES))
        for g in range(SWA_KV_HEADS):
            heads, s = _swa_group_logits(g, qs, kdup, valid, bias_ref)
            sink = jnp.broadcast_to(sink_ref[heads].reshape(SWA_GROUP * Q, 1), (SWA_GROUP * Q, LANES))
            m = jnp.maximum(jnp.max(s, -1, keepdims=True), sink)
            p = jnp.exp(s - _widen(m, 2 * Q))
            vaug = jnp.where(lo2, vdup[g], jnp.ones_like(vdup[g]))
            pv = lax.dot_general(p.astype(BF16), vaug, NN, preferred_element_type=F32)
            rolled = pltpu.roll(pv, HEAD_DIM, 1)
            l = jnp.where(lo4, rolled, pv) + jnp.exp(sink - m)
            out = pv / l
            lse_g = m + jnp.log(l)
            for i in range(SWA_GROUP):
                a = g * SWA_GROUP + i
                lse_ref[:, a * LANES:(a + 1) * LANES] = lse_g[i * Q:(i + 1) * Q]
            shifted = pltpu.roll(out, HEAD_DIM, 1)
            pairs += [jnp.where(lo, out[2 * pr * Q:(2 * pr + 1) * Q], shifted[(2 * pr + 1) * Q:(2 * pr + 2) * Q])
                      for pr in range(SWA_GROUP // 2)]
        o_ref[...] = jnp.concatenate(pairs, axis=1).astype(BF16)
        if n_c:
            @pl.when((ids[0] == B - 1) & (ids[1] == nb - 1))
            def _():
                _comm_wait(kinds, c_in, c_out, sems, place)

    whole = lambda shape: pl.BlockSpec(shape, lambda b, n: (0,) * len(shape))
    res = pl.pallas_call(
        body, name=name, grid=(B, nb),
        in_specs=_swa_in_specs(nb) + [whole((SWA_HEADS, Q, 2 * Q)), whole((SWA_HEADS, Q, 1))] + [HBM_SPEC] * n_c,
        out_specs=[pl.BlockSpec((Q, SWA_HEADS * HEAD_DIM), lambda b, n: (b * nb + n, 0)),
                   pl.BlockSpec((Q, SWA_HEADS * LANES), lambda b, n: (b * nb + n, 0))] + [HBM_SPEC] * n_c,
        out_shape=[jax.ShapeDtypeStruct((B * S, SWA_HEADS * HEAD_DIM), BF16),
                   jax.ShapeDtypeStruct((B * S, SWA_HEADS * LANES), F32)] + _comm_out_shapes(comm),
        scratch_shapes=_comm_scratch(comm) if n_c else [],
        compiler_params=_cparams(*(("arbitrary",) * 2 if n_c else ("parallel",) * 2)),
    )(h, h, h, h, h, bias, sinkcol, *[a for _, a in comm])
    return res[0], res[1], list(res[2:])


def _swa_bwd(h, o, do, lse, bias, sinkcol, *, do_blk0, B, S, comm=(), name):
    Q = BLOCK_Q
    nb = S // Q
    scale = HEAD_DIM ** -0.5
    n_c, kinds = len(comm), [k for k, _ in comm]

    def body(*refs):
        c_in, c_out, sems = refs[10:10 + n_c], refs[17 + n_c:17 + 2 * n_c], refs[17 + 2 * n_c:]
        q_ref, kp_ref, ko_ref, vp_ref, vo_ref, o_ref, do_ref, lse_ref, bias_ref, sink_ref = refs[:10]
        dq_ref, dko_ref, dkp_ref, dvo_ref, dvp_ref, dbias_ref, dsink_ref = refs[10 + n_c:17 + n_c]
        ids = [pl.program_id(0), pl.program_id(1)]
        if n_c:
            place = _mesh_place()

        @pl.when((ids[0] == 0) & (ids[1] == 0))
        def _():
            dbias_ref[...] = jnp.zeros_like(dbias_ref)
            dsink_ref[...] = jnp.zeros_like(dsink_ref)
            if n_c:
                _comm_start(kinds, c_in, c_out, sems, place)

        lo, lo2, kdup, vdup, valid, qs = _swa_common(q_ref, kp_ref, ko_ref, vp_ref, vo_ref, pl.program_id(1))
        dkk, dvv, dq_pairs = [], [], []
        for g in range(SWA_KV_HEADS):
            heads, s = _swa_group_logits(g, qs, kdup, valid, bias_ref)
            lse_g = jnp.concatenate([lse_ref[:, a * LANES:(a + 1) * LANES]
                                     for a in range(g * SWA_GROUP, (g + 1) * SWA_GROUP)], axis=0)
            p = jnp.exp(s - _widen(lse_g, 2 * Q))
            do_g, o_g = [], []
            for i in range(SWA_GROUP):
                cols = slice((g * SWA_GROUP + i) // 2 * LANES, ((g * SWA_GROUP + i) // 2 + 1) * LANES)
                do_p = do_ref[:, cols]
                do_g.append(jnp.where(lo if i % 2 == 0 else jnp.logical_not(lo), do_p, jnp.zeros_like(do_p)))
                o_g.append(o_ref[:, cols])
            doh, oh = jnp.concatenate(do_g, axis=0), jnp.concatenate(o_g, axis=0)
            delta = jnp.sum(doh.astype(F32) * oh.astype(F32), -1, keepdims=True)
            dp = lax.dot_general(doh, vdup[g], NT, preferred_element_type=F32)
            ds = p * (dp - delta)
            dbias_ref[heads] += ds.reshape(SWA_GROUP, Q, 2 * Q)
            dsink_ref[heads] -= (jnp.exp(sink_ref[heads].reshape(SWA_GROUP * Q, 1) - lse_g[:, :1])
                                 * delta).reshape(SWA_GROUP, Q, 1)
            dss = (ds * scale).astype(BF16)
            dq_pairs += _pair_halves(lax.dot_general(dss, kdup[g], NN, preferred_element_type=F32), lo)
            dkk.append(lax.dot_general(dss, qs[g], TN, preferred_element_type=F32))
            dvv.append(lax.dot_general(p.astype(BF16), doh, TN, preferred_element_type=F32))
        dq_ref[...] = jnp.concatenate(dq_pairs, axis=1).astype(BF16)
        fold = lambda x: x + pltpu.roll(x, HEAD_DIM, 1)
        dk_blk = jnp.where(lo2, fold(dkk[0]), fold(dkk[1]))
        dv_blk = jnp.where(lo2, fold(dvv[0]), fold(dvv[1]))
        dkp_ref[...] = dk_blk[:Q]
        dko_ref[...] = dk_blk[Q:]
        dvp_ref[...] = dv_blk[:Q]
        dvo_ref[...] = dv_blk[Q:]
        if n_c:
            @pl.when((ids[0] == B - 1) & (ids[1] == nb - 1))
            def _():
                _comm_wait(kinds, c_in, c_out, sems, place)

    whole = lambda shape: pl.BlockSpec(shape, lambda b, n: (0,) * len(shape))
    wide = lambda blk: pl.BlockSpec((Q, SWA_HEADS * HEAD_DIM), lambda b, n: (b * nb + n, blk))
    narrow = pl.BlockSpec((Q, LANES), lambda b, n: (b * nb + n, 0))
    kv_shape = jax.ShapeDtypeStruct((B * S, LANES), F32)
    res = pl.pallas_call(
        body, name=name, grid=(B, nb),
        in_specs=_swa_in_specs(nb) + [wide(0), wide(do_blk0),
                                      pl.BlockSpec((Q, SWA_HEADS * LANES), lambda b, n: (b * nb + n, 0)),
                                      whole((SWA_HEADS, Q, 2 * Q)),
                                      whole((SWA_HEADS, Q, 1))] + [HBM_SPEC] * n_c,
        out_specs=[wide(0), narrow, narrow, narrow, narrow, whole((SWA_HEADS, Q, 2 * Q)), whole((SWA_HEADS, Q, 1))]
        + [HBM_SPEC] * n_c,
        out_shape=[jax.ShapeDtypeStruct((B * S, SWA_HEADS * HEAD_DIM), BF16), kv_shape, kv_shape, kv_shape, kv_shape,
                   jax.ShapeDtypeStruct((SWA_HEADS, Q, 2 * Q), F32), jax.ShapeDtypeStruct((SWA_HEADS, Q, 1), F32)]
        + _comm_out_shapes(comm),
        scratch_shapes=_comm_scratch(comm) if n_c else [],
        compiler_params=_cparams("arbitrary", "arbitrary"),
    )(h, h, h, h, h, o, do, lse, bias, sinkcol, *[a for _, a in comm])
    return tuple(res[:7]) + (list(res[7:]),)


def _bias_bucket_sum(dbias, bucket, *, name):
    def body(d_ref, b_ref, o_ref):
        dbv, bk = d_ref[...], b_ref[...]
        lane = lax.broadcasted_iota(jnp.int32, (SWA_HEADS, LANES), 1)
        out = jnp.zeros((SWA_HEADS, LANES), F32)
        for b in range(REL_BUCKETS):
            part = jnp.sum(jnp.where(bk == b, dbv, 0.0), axis=1)
            tot = jnp.sum(part, axis=-1, keepdims=True)
            out = out + jnp.where(lane == b, tot, 0.0)
        o_ref[...] = out

    return pl.pallas_call(
        body, name=name, out_shape=jax.ShapeDtypeStruct((SWA_HEADS, LANES), F32),
        compiler_params=pltpu.CompilerParams(vmem_limit_bytes=VMEM_LIMIT_BYTES),
    )(dbias, bucket)


def _adamw_update(w, g, m, v):
    m_new = ADAM_B1 * m + (1.0 - ADAM_B1) * g
    v_new = ADAM_B2 * v + (1.0 - ADAM_B2) * jnp.square(g)
    m_hat = m_new / (1.0 - ADAM_B1 ** ADAM_STEP)
    v_hat = v_new / (1.0 - ADAM_B2 ** ADAM_STEP)
    return -ADAM_LR * (m_hat / (jnp.sqrt(v_hat) + ADAM_EPS) + ADAM_WD * w), m_new, v_new


def _adamw(w, g, m, v, *, name):
    def body(w_ref, g_ref, m_ref, v_ref, d_ref, nm_ref, nv_ref):
        d_ref[...], nm_ref[...], nv_ref[...] = _adamw_update(w_ref[...], g_ref[...], m_ref[...], v_ref[...])

    return pl.pallas_call(
        body, name=name, out_shape=[jax.ShapeDtypeStruct(w.shape, F32)] * 3,
        compiler_params=pltpu.CompilerParams(vmem_limit_bytes=VMEM_LIMIT_BYTES),
    )(w, g, m, v)


ADAMW_PARTS_BYTES = 8 * 1024 * 1024


def _adamw_slots(w, parts, m, v, *, name):
    n0, R, C = w.shape
    tr = next((c for c in (512, 256, 128, 64, 32, 16, 8) if R % c == 0 and 4 * n0 * N_DEV * c * C <= ADAMW_PARTS_BYTES), R)

    def body(*refs):
        w_ref, p_refs, (m_ref, v_ref, g_ref, d_ref, nm_ref, nv_ref) = refs[0], refs[1:1 + n0], refs[1 + n0:]
        layer = pl.program_id(0)
        for l in range(n0):
            @pl.when(layer == l)
            def _(p_ref=p_refs[l]):
                g = p_ref[0].astype(F32)
                for j in range(1, N_DEV):
                    g = g + p_ref[j].astype(F32)
                g_ref[...] = g
                d_ref[...], nm_ref[...], nv_ref[...] = _adamw_update(w_ref[...], g, m_ref[...], v_ref[...])

    spec = pl.BlockSpec((None, tr, C), lambda l, i: (l, i, 0))
    part_spec = lambda own: pl.BlockSpec((N_DEV, tr, C), lambda l, i: (0, jnp.where(l == own, i, 0), 0))
    return pl.pallas_call(
        body, name=name, grid=(n0, R // tr),
        in_specs=[spec] + [part_spec(l) for l in range(n0)] + [spec, spec], out_specs=[spec] * 4,
        out_shape=[jax.ShapeDtypeStruct((n0, R, C), F32)] * 4, compiler_params=_cparams("arbitrary", "arbitrary"),
    )(w, *parts, m, v)


def _all_gather_hbm(blocks, *, name):
    n = len(blocks)

    def body(*refs):
        x_refs, out_refs = refs[:n], refs[n:2 * n]
        send_sems, recv_sems, local_sems = refs[2 * n:]
        x, y, c, _ = _mesh_place()
        me, sibling = (x, y, c), (x, y, 1 - c)
        chips = [(1 - x, y), (x, 1 - y), (1 - x, 1 - y)]

        def copy(w, k, blk, to, src=None):
            px, py, pc = blk
            slot = out_refs[w].at[4 * px + 2 * py + pc]
            return pltpu.make_async_remote_copy(
                src_ref=slot if src is None else src, dst_ref=slot,
                send_sem=send_sems.at[w, k], recv_sem=recv_sems.at[w, k], device_id=to, device_id_type=MESH_ID)

        mine = [pltpu.make_async_copy(x_refs[w], out_refs[w].at[4 * x + 2 * y + c], local_sems.at[w])
                for w in range(n)]
        for cp in mine:
            cp.start()
        first = []
        for w in range(n):
            first.append(copy(w, 0, me, sibling, src=x_refs[w]))
            first += [copy(w, 1 + j, me, (*chip, c), src=x_refs[w]) for j, chip in enumerate(chips)]
        for cp in first:
            cp.start()
        passed = []
        for j, chip in enumerate(chips):
            for w in range(n):
                copy(w, 1 + j, (*chip, c), me).wait_recv()
                fwd = copy(w, 4 + j, (*chip, c), sibling)
                fwd.start()
                passed.append(fwd)
        for w in range(n):
            copy(w, 0, sibling, me).wait_recv()
            for j, chip in enumerate(chips):
                copy(w, 4 + j, (*chip, 1 - c), me).wait_recv()
        for cp in first + passed:
            cp.wait_send()
        for cp in mine:
            cp.wait()

    return pl.pallas_call(
        body, name=name, out_shape=[jax.ShapeDtypeStruct((N_DEV,) + b.shape, b.dtype) for b in blocks],
        in_specs=[HBM_SPEC] * n, out_specs=[HBM_SPEC] * n,
        scratch_shapes=[pltpu.SemaphoreType.DMA((n, 7)), pltpu.SemaphoreType.DMA((n, 7)),
                        pltpu.SemaphoreType.DMA((n,))],
    )(*blocks)


def _all_reduce_small(block, *, name):
    R, W = block.shape

    def body(x_ref, out_ref, buf, send_sems, recv_sems):
        x, y, c, me = _mesh_place()
        copies = []
        for k, (peer, _) in enumerate(_peers(x, y, c)):
            copies.append(pltpu.make_async_remote_copy(
                src_ref=x_ref, dst_ref=buf.at[me], send_sem=send_sems.at[k], recv_sem=recv_sems.at[k],
                device_id=peer, device_id_type=MESH_ID))
        for cp in copies:
            cp.start()
        buf[me] = x_ref[...]
        for cp in copies:
            cp.wait_recv()
        for cp in copies:
            cp.wait_send()
        acc = buf[0]
        for j in range(1, N_DEV):
            acc = acc + buf[j]
        out_ref[...] = acc

    return pl.pallas_call(
        body, name=name, out_shape=jax.ShapeDtypeStruct((R, W), F32),
        in_specs=[VMEM_SPEC], out_specs=VMEM_SPEC,
        scratch_shapes=[pltpu.VMEM((N_DEV, R, W), F32), pltpu.SemaphoreType.DMA((7,)), pltpu.SemaphoreType.DMA((7,))],
    )(block)


def _assemble(name, g):
    if name == 'w_up':
        return g
    if BIG_AXIS[name] == 2:
        return jnp.concatenate([g[j] for j in range(N_DEV)], axis=1)
    return g.reshape(N_DEV * g.shape[1], g.shape[2])


def _split_for_devices(name, g):
    if g.ndim == 3:
        return g
    if BIG_AXIS[name] == 2:
        b = g.shape[1] // N_DEV
        return jnp.stack([g[:, j * b:(j + 1) * b] for j in range(N_DEV)]).astype(BF16)
    return g.reshape(N_DEV, g.shape[0] // N_DEV, g.shape[1]).astype(BF16)


def _layer_weight_keys(i):
    j = i // 2
    mixer = [('ev_w_in', j), ('ev_w_uq', j), ('ev_w_ukv', j), ('ev_w_out', j)] if i % 2 == 0 \
        else [('od_w_in', j), ('od_w_out', j)]
    return mixer + [('w_up', i), ('w_down', i), ('ple_w_proj', i), ('ple_w_gate', i)]


def _weight_layer(key):
    name, idx = key
    return 2 * idx if name.startswith('ev_') else 2 * idx + 1 if name.startswith('od_') else idx


FIRST_GATHER = [('ev_w_in', 0), ('ev_w_uq', 0), ('ev_w_ukv', 0)]
FWD_CARRIERS = {
    'l0_mla': [('ev_w_out', 0), ('w_up', 0), ('ple_w_proj', 0), ('ple_w_gate', 0)],
    'l0_swa': [('w_down', 0)],
    'l0_out_ln1': [('od_w_out', 0)],
    'l0_up': [('od_w_in', 0)],
    'l0_down_ln2': [('w_up', 1)],
    'l0_ple_gate': [('ple_w_proj', 1), ('ple_w_gate', 1)],
    'l1_fox': [('w_down', 1), ('ev_w_in', 1), ('ev_w_uq', 1), ('ev_w_ukv', 1), ('ev_w_out', 1), ('w_up', 2)],
    'l1_up': [('w_down', 2)],
    'l1_down_ln2': [('ple_w_proj', 2), ('ple_w_gate', 2)],
    'l2_mla': [('od_w_in', 1), ('od_w_out', 1)],
    'l2_swa': [('w_up', 3)],
    'l2_up': [('w_down', 3)],
    'l2_down_ln2': [('ple_w_proj', 3), ('ple_w_gate', 3)],
}


class _MeshExchange:
    def __init__(self, shards):
        self.shards = shards
        self.weights = {i: {} for i in range(DEPTH)}
        self.pending = []
        self.in_flight = []
        self.received = {}
        got = _all_gather_hbm([self.shards[n][idx] for n, idx in FIRST_GATHER], name="gather_first")
        self._landed(FIRST_GATHER, got)

    def _landed(self, keys, gathered):
        for k, g in zip(keys, gathered):
            self.weights[_weight_layer(k)][k[0]] = _assemble(k[0], g)

    def layer_weights(self, i):
        return self.weights[i]

    def carry(self, kernel_name):
        return [(("gather", idx), self.shards[n]) for n, idx in FWD_CARRIERS.get(kernel_name, [])]

    def carried(self, kernel_name, outs):
        self._landed(FWD_CARRIERS.get(kernel_name, []), outs)

    def push_grads(self, grads):
        self.pending += [(k, _split_for_devices(k[0], g)) for k, g in grads.items()]

    def bwd_items(self):
        self.in_flight, self.pending = self.pending, []
        return [("scatter", parts) for _, parts in self.in_flight]

    def bwd_done(self, outs):
        for (k, _), got in zip(self.in_flight, outs):
            self.received[k] = got
        self.in_flight = []

    def finish(self):
        if self.pending:
            outs = _exchange(self.bwd_items(), name="scatter_rest")
            self.bwd_done(outs)
        return self.received


PACK_ROWS = 8


def _pack_small(vals):
    flat = jnp.concatenate([vals[n].reshape(-1).astype(F32) for n in SMALL])
    pad = (-flat.shape[0]) % (PACK_ROWS * LANES)
    return jnp.pad(flat, (0, pad)).reshape(-1, LANES)


def _unpack_small(block, shapes):
    flat = block.reshape(-1)
    out, off = {}, 0
    for n in SMALL:
        sz = math.prod(shapes[n])
        out[n] = flat[off:off + sz].reshape(shapes[n])
        off += sz
    return out


def _rope_tables(S):
    half = MLA_ROPE // 2
    inv = 1.0 / (ROPE_THETA ** (jnp.arange(0, MLA_ROPE, 2, dtype=F32) / MLA_ROPE))
    ang = jnp.arange(S, dtype=F32)[:, None] * inv[None, :]
    cos, sin = jnp.cos(ang), jnp.sin(ang)
    zeros = jnp.zeros((S, half), F32)
    tail = jnp.zeros((S, LANES - MLA_QK), F32)

    def block(rope_part, nope_val):
        return jnp.concatenate([jnp.full((S, MLA_NOPE), nope_val, F32), rope_part, tail], -1)

    a_r = jnp.concatenate([cos, cos], -1)
    bm_r = jnp.concatenate([-sin, zeros], -1)
    bp_r = jnp.concatenate([zeros, sin], -1)
    q_tabs = tuple(block(r, v) for r, v in ((a_r, 1.0), (bm_r, 0.0), (bp_r, 0.0)))
    k_tabs = tuple(block(r, 0.0) for r in (a_r, bm_r, bp_r))
    return q_tabs, k_tabs


def _t5_bucket(dist):
    exact = REL_BUCKETS // 2
    d = jnp.maximum(dist, 1).astype(F32)
    large = exact + (jnp.log(d / exact) / math.log(REL_MAX_DIST / exact) * (REL_BUCKETS - exact)).astype(jnp.int32)
    large = jnp.minimum(large, REL_BUCKETS - 1)
    return jnp.where(dist < exact, dist, large)


def _swa_bucket_table():
    a = jnp.arange(BLOCK_Q)[:, None]
    col = jnp.arange(2 * BLOCK_Q)[None, :]
    return _t5_bucket(jnp.maximum(a + BLOCK_Q - col, 0)).astype(jnp.int32)


def _even_weights(W):
    w = W['ev_w_in']
    c_kv1 = MLA_Q_LORA + MLA_KV_LORA
    c_kr1 = c_kv1 + MLA_ROPE
    c_qs1 = c_kr1 + SWA_HEADS * HEAD_DIM
    zeros = lambda n: jnp.zeros((D_MODEL, n), w.dtype)
    w_in = jnp.concatenate([w[:, c_kr1:c_qs1], w[:, :c_kv1], w[:, c_qs1:], zeros(KR_LANE0), w[:, c_kv1:c_kr1],
                            zeros(LANES - KR_LANE0 - MLA_ROPE)], axis=1)
    uq = W['ev_w_uq'].reshape(MLA_Q_LORA, MLA_HEADS, MLA_QK)
    w_uq = jnp.pad(uq, ((0, 0), (0, 0), (0, LANES - MLA_QK))).reshape(MLA_Q_LORA, MLA_HEADS * LANES)
    ukv = W['ev_w_ukv'].reshape(MLA_KV_LORA, MLA_HEADS, MLA_NOPE + MLA_V)
    w_k = jnp.pad(ukv[..., :MLA_NOPE], ((0, 0), (0, 0), (0, LANES - MLA_NOPE))).reshape(MLA_KV_LORA, -1)
    w_v = ukv[..., MLA_NOPE:].reshape(MLA_KV_LORA, MLA_HEADS * MLA_V)
    return w_in, w_uq, w_k, w_v


def _even_in_grad_unpad(dw):
    kr0 = EV_KR[0] + KR_LANE0
    return jnp.concatenate([dw[:, EV_CQ[0]:EV_CKV[1]], dw[:, kr0:kr0 + MLA_ROPE], dw[:, EV_QS[0]:EV_QS[1]],
                            dw[:, EV_KS[0]:EV_VS[1]]], axis=1)


def _even_fwd(xb, W, P, i, B, S, tabs, xchg, tag):
    j = i // 2
    q_tabs, k_tabs, bias, sinkcol = tabs
    w_in, w_uq, w_k, w_v = _even_weights(W)
    h = _mm(xb, w_in, name=f"{tag}_in")
    cqn, ckvn, rq, rkv = _even_norms(h, P['ev_q_norm'][j][None], P['ev_kv_norm'][j][None], name=f"{tag}_norms")
    q = _rope(_mm(cqn, w_uq, name=f"{tag}_uq"), q_tabs, S, sign=1.0, name=f"{tag}_ropeq")
    knp = _mm(ckvn, w_k, out_dtypes=(BF16,), name=f"{tag}_uk")
    v = _mm(ckvn, w_v, out_dtypes=(BF16,), name=f"{tag}_uv")
    k = _mla_keys(knp, h, k_tabs, S, name=f"{tag}_keys")
    o_mla, lse_mla, got = _flash_fwd(q, k, v, q_blk0=0, k_blk0=0, v_blk0=0, W=2 * LANES, n_pairs=MLA_HEADS // 2,
                                     B=B, S=S, scale=MLA_QK ** -0.5, comm=xchg.carry(f"{tag}_mla"), name=f"{tag}_mla")
    xchg.carried(f"{tag}_mla", got)
    o_swa, lse_swa, got = _swa_fwd(h, bias, sinkcol, B=B, S=S, comm=xchg.carry(f"{tag}_swa"), name=f"{tag}_swa")
    xchg.carried(f"{tag}_swa", got)
    res = dict(h=h, cqn=cqn, ckvn=ckvn, rq=rq, rkv=rkv, q=q, k=k, v=v, o_mla=o_mla, lse_mla=lse_mla,
               o_swa=o_swa, lse_swa=lse_swa)
    return ((o_mla, o_swa), W['ev_w_out']), res


def _shift_prev(own, prev, B, S):
    prev = prev.reshape(B, S, LANES)
    shifted = jnp.concatenate([prev[:, BLOCK_Q:], jnp.zeros_like(prev[:, :BLOCK_Q])], axis=1)
    return (own + shifted.reshape(B * S, LANES)).astype(BF16)


def _even_bwd(dmb, dz1, xb, W, P, j, B, S, tabs, res, xchg, tag):
    q_tabs, k_tabs, bias, sinkcol = tabs
    (w_in, w_uq, w_k, w_v), w_out = _even_weights(W), W['ev_w_out']
    g = {}
    g['ev_w_out'] = jnp.concatenate([_mm_tn(res['o_mla'], dmb, name=f"{tag}_dwout_mla"),
                                     _mm_tn(res['o_swa'], dmb, name=f"{tag}_dwout_swa")], axis=0)
    do = _mm(dmb, w_out, trans_b=True, out_dtypes=(BF16,), name=f"{tag}_do")
    dq, dk, dv, got = _flash_bwd(res['q'], res['k'], res['v'], res['o_mla'], do, res['lse_mla'], q_blk0=0, k_blk0=0,
                                 v_blk0=0, do_blk0=0, W=2 * LANES, n_pairs=MLA_HEADS // 2, B=B, S=S,
                                 scale=MLA_QK ** -0.5, qk_dtype=F32, comm=xchg.bwd_items(), name=f"{tag}_mla_bwd")
    xchg.bwd_done(got)
    dq_pre = _rope(dq, q_tabs, S, sign=-1.0, name=f"{tag}_ropeq_bwd")
    dw_uq = _mm_tn(res['cqn'], dq_pre, name=f"{tag}_dwuq")
    g['ev_w_uq'] = dw_uq.reshape(MLA_Q_LORA, MLA_HEADS, LANES)[..., :MLA_QK].reshape(MLA_Q_LORA, MLA_HEADS * MLA_QK)
    dcqn = _mm(dq_pre, w_uq, trans_b=True, name=f"{tag}_dcqn")
    dw_k = _mm_tn(res['ckvn'], dk, name=f"{tag}_dwuk").reshape(MLA_KV_LORA, MLA_HEADS, LANES)[..., :MLA_NOPE]
    dw_v = _mm_tn(res['ckvn'], dv, name=f"{tag}_dwuv").reshape(MLA_KV_LORA, MLA_HEADS, MLA_V)
    g['ev_w_ukv'] = jnp.concatenate([dw_k, dw_v], axis=-1).reshape(MLA_KV_LORA, MLA_HEADS * (MLA_NOPE + MLA_V))
    dckvn_v = _mm(dv, w_v, trans_b=True, name=f"{tag}_dckvn_v")
    dckvn = _mm(dk, w_k, trans_b=True, extras=(dckvn_v,), epilogue=lambda acc, r: (acc + r,), name=f"{tag}_dckvn")
    dkr_pre = _mla_rope_key_grad(dk, k_tabs, S, name=f"{tag}_ropek_bwd")
    xchg.push_grads({(n, j): g.pop(n) for n in list(g)})
    dqs, dko, dkp, dvo, dvp, dbias, dsink, got = _swa_bwd(res['h'], res['o_swa'], do, res['lse_swa'], bias, sinkcol,
                                                          do_blk0=1, B=B, S=S, comm=xchg.bwd_items(),
                                                          name=f"{tag}_swa_bwd")
    xchg.bwd_done(got)
    dh, dgq, dgkv = _even_in_bwd(res['h'], res['rq'], res['rkv'], P['ev_q_norm'][j][None], P['ev_kv_norm'][j][None],
                                 dcqn, dckvn, dqs, _shift_prev(dko, dkp, B, S), _shift_prev(dvo, dvp, B, S), dkr_pre,
                                 name=f"{tag}_in_bwd")
    g['ev_w_in'] = _even_in_grad_unpad(_mm_tn(xb, dh, name=f"{tag}_dwin"))
    xchg.push_grads({(n, j): val for n, val in g.items()})
    dx_kwargs = dict(trans_b=True, extras=(dz1,), epilogue=lambda acc, r: (acc + DN_ALPHA * r,), name=f"{tag}_dx")
    dx = _scattering(xchg, _mm, dh, w_in, **dx_kwargs) if j == 0 else _mm(dh, w_in, **dx_kwargs)
    small = dict(ev_q_norm=dgq[0], ev_kv_norm=dgkv[0], dbias=dbias, ev_sinks=jnp.sum(dsink, axis=(1, 2)))
    return dx, small


def _odd_fwd(xb, W, P, i, B, S, xchg, tag):
    j = i // 2
    w = W['od_w_in']
    w_qkv = w[:, :ODD_QKV]
    w_f = jnp.pad(w[:, ODD_QKV:], ((0, 0), (0, LANES - FOX_HEADS)))
    bf = jnp.pad(P['od_b_f'][j], (0, LANES - FOX_HEADS))[None]
    qkv = _mm(xb, w_qkv, out_dtypes=(BF16,), name=f"{tag}_qkv")
    f = _mm(xb, w_f, name=f"{tag}_f").reshape(B, S, LANES)
    csh, chs = _fox_decay_fwd(f, bf, name=f"{tag}_decay")
    crow = chs[:, :FOX_HEADS].reshape(B, FOX_HEADS, S // ATT_TILE, 1, ATT_TILE)
    n_blk = FOX_HEADS * HEAD_DIM // LANES
    o, lse, got = _flash_fwd(qkv, qkv, qkv, q_blk0=0, k_blk0=n_blk, v_blk0=2 * n_blk, W=LANES,
                             n_pairs=FOX_HEADS // 2, B=B, S=S, scale=HEAD_DIM ** -0.5, csh=csh, crow=crow,
                             comm=xchg.carry(f"{tag}_fox"), name=f"{tag}_fox")
    xchg.carried(f"{tag}_fox", got)
    res = dict(f=f, bf=bf, csh=csh, crow=crow, qkv=qkv, o=o, lse=lse, w_qkv=w_qkv, w_f=w_f)
    return (o, W['od_w_out']), res


def _odd_bwd(dmb, dz1, xb, W, P, j, B, S, res, xchg, tag):
    g = {}
    w_out = W['od_w_out']
    g['od_w_out'] = _mm_tn(res['o'], dmb, name=f"{tag}_dwout")
    do = _mm(dmb, w_out, trans_b=True, out_dtypes=(BF16,), name=f"{tag}_do")
    qkv = res['qkv']
    n_blk = FOX_HEADS * HEAD_DIM // LANES
    dq, dk, dv, dck, dcq, got = _flash_bwd(qkv, qkv, qkv, res['o'], do, res['lse'], q_blk0=0, k_blk0=n_blk,
                                           v_blk0=2 * n_blk, do_blk0=0, W=LANES, n_pairs=FOX_HEADS // 2, B=B, S=S,
                                           scale=HEAD_DIM ** -0.5, qk_dtype=BF16, csh=res['csh'], crow=res['crow'],
                                           comm=xchg.bwd_items(), name=f"{tag}_fox_bwd")
    xchg.bwd_done(got)
    dc = dck.reshape(B, FOX_HEADS, S) + dcq.reshape(B, FOX_HEADS, S)
    dc_hs = jnp.pad(dc, ((0, 0), (0, LANES - FOX_HEADS), (0, 0)))
    df, dbf = _fox_decay_bwd(dc_hs, res['f'], res['bf'], name=f"{tag}_decay_bwd")
    df = df.reshape(B * S, LANES)
    dw_qkv = [_mm_tn(xb, t, name=f"{tag}_dw{n}") for n, t in (("q", dq), ("k", dk), ("v", dv))]
    dw_f = _mm_tn(xb, df, name=f"{tag}_dwf")
    g['od_w_in'] = jnp.concatenate(dw_qkv + [dw_f[:, :FOX_HEADS]], axis=1)
    dxf = _mm(df, res['w_f'], trans_b=True, extras=(dz1,), epilogue=lambda acc, r: (acc + DN_ALPHA * r,),
              name=f"{tag}_dxf")
    xchg.push_grads({(n, j): val for n, val in g.items()})
    dx = _mm((dq, dk, dv), res['w_qkv'], trans_b=True, extras=(dxf,), epilogue=lambda acc, r: (acc + r,),
             name=f"{tag}_dx")
    small = dict(od_b_f=dbf[0, :FOX_HEADS])
    return dx, small


def _carrying(xchg, name, call, *args, **kwargs):
    comm = xchg.carry(name)
    out = call(*args, comm=comm, name=name, **kwargs)
    if comm:
        out, got = out
        xchg.carried(name, got)
    return out


def _scattering(xchg, call, *args, **kwargs):
    comm = xchg.bwd_items()
    out = call(*args, comm=comm, **kwargs)
    if comm:
        out, got = out
        xchg.bwd_done(got)
    return out


def _local_step(x, p, target, P, xchg):
    B, S, D = x.shape
    T = B * S
    q_tabs, k_tabs = _rope_tables(S)
    bucket = _swa_bucket_table()
    in_bucket = (bucket[..., None] == jnp.arange(REL_BUCKETS)).astype(F32)
    bias = jnp.einsum('acb,bh->hac', in_bucket, P['rel_bias'], precision=lax.Precision.HIGHEST)

    xc = x.reshape(T, D)
    xcb = xc.astype(BF16)
    saved = []
    for i in range(DEPTH):
        j = i // 2
        tag = f"l{i}"
        W = xchg.layer_weights(i)
        lay = dict(xb=xcb, W=W)
        if i % 2 == 0:
            sinkcol = jnp.broadcast_to(P['ev_sinks'][j][:, None, None], (SWA_HEADS, BLOCK_Q, 1)).astype(F32)
            lay['tabs'] = (q_tabs, k_tabs, bias, sinkcol)
            (o, w_out), lay['mix'] = _even_fwd(xcb, W, P, i, B, S, lay['tabs'], xchg, tag)
        else:
            (o, w_out), lay['mix'] = _odd_fwd(xcb, W, P, i, B, S, xchg, tag)
        x1, x1b, lay['xh1'], lay['r1'] = _carrying(xchg, f"{tag}_out_ln1", _mm_ln, o, w_out, xc,
                                                   P['ln1_g'][i][None], P['ln1_b'][i][None])
        lay['x1b'] = x1b
        lay['u'], lay['a'] = _carrying(xchg, f"{tag}_up", _mm, x1b, W['w_up'], out_dtypes=(F32, BF16),
                                       epilogue=lambda acc: (acc, jnp.square(jnp.maximum(acc, 0.0))))
        x2, x2b, lay['xh2'], lay['r2'] = _carrying(xchg, f"{tag}_down_ln2", _mm_ln, lay['a'], W['w_down'], x1,
                                                   P['ln2_g'][i][None], P['ln2_b'][i][None])
        lay['x2b'] = x2b
        lay['p'] = p[i].reshape(T, D_PLE)
        lay['e'] = _mm(lay['p'], W['ple_w_proj'], name=f"{tag}_ple_proj")

        def gate(acc, bg, e, x2v):
            gv = 1.0 / (1.0 + jnp.exp(-(acc + bg)))
            y = x2v + gv * e
            return y, y, gv

        xc, xcb, lay['g'] = _carrying(xchg, f"{tag}_ple_gate", _mm, x2b, W['ple_w_gate'],
                                      extras=(P['ple_b_gate'][i][None], lay['e'], x2), epilogue=gate,
                                      out_dtypes=(F32, BF16, F32))
        saved.append(lay)

    dy, sq = _loss_grad(xc, target.reshape(T, D), name="loss")

    Gs = {n: [None] * DEPTH for n in ('ln1_g', 'ln1_b', 'ln2_g', 'ln2_b', 'ple_b_gate')}
    Gs.update({n: [None] * (DEPTH // 2) for n in ('ev_q_norm', 'ev_kv_norm', 'ev_sinks', 'od_b_f')})
    dbias_total = None
    for i in reversed(range(DEPTH)):
        j = i // 2
        tag = f"l{i}b"
        lay = saved[i]
        W = lay['W']
        de, dzg, dbg = _ple_bwd_elem(dy, lay['g'], lay['e'], name=f"{tag}_ple_elem")
        Gs['ple_b_gate'][i] = dbg[0]
        g_mlp = {('ple_w_proj', i): _mm_tn(lay['p'], de, slot_width=D_MODEL // N_DEV, name=f"{tag}_dwproj"),
                 ('ple_w_gate', i): _mm_tn(lay['x2b'], dzg, name=f"{tag}_dwgate")}
        dz2, dz2b, dg2, db2 = _mm_ln_bwd(dzg, W['ple_w_gate'], dy, 1.0, lay['xh2'], lay['r2'], P['ln2_g'][i][None],
                                         name=f"{tag}_dx2_ln2")
        Gs['ln2_g'][i], Gs['ln2_b'][i] = dg2[0], db2[0]
        g_mlp[('w_down', i)] = _mm_tn(lay['a'], dz2b, name=f"{tag}_dwdown")
        du = _mm(dz2b, W['w_down'], trans_b=True, extras=(lay['u'],), out_dtypes=(BF16,),
                 epilogue=lambda acc, u: (acc * (2.0 * jnp.maximum(u, 0.0)),), name=f"{tag}_du")
        g_mlp[('w_up', i)] = _mm_tn(lay['x1b'], du, slot_width=D_FF // N_DEV, name=f"{tag}_dwup")
        xchg.push_grads(g_mlp)
        dz1, dz1b, dg1, db1 = _mm_ln_bwd(du, W['w_up'], dz2, DN_ALPHA, lay['xh1'], lay['r1'], P['ln1_g'][i][None],
                                         name=f"{tag}_dx1_ln1")
        Gs['ln1_g'][i], Gs['ln1_b'][i] = dg1[0], db1[0]
        if i % 2 == 0:
            dy, small = _even_bwd(dz1b, dz1, lay['xb'], W, P, j, B, S, lay['tabs'], lay['mix'], xchg, tag)
            dbias_total = small['dbias'] if dbias_total is None else dbias_total + small['dbias']
            for n in ('ev_q_norm', 'ev_kv_norm', 'ev_sinks'):
                Gs[n][j] = small[n]
        else:
            dy, small = _odd_bwd(dz1b, dz1, lay['xb'], W, P, j, B, S, lay['mix'], xchg, tag)
            Gs['od_b_f'][j] = small['od_b_f']

    grads_small = {n: jnp.stack(v) for n, v in Gs.items()}
    drel = _bias_bucket_sum(dbias_total, bucket, name="rel_bias_grad")
    grads_small['rel_bias'] = drel[:, :REL_BUCKETS].T
    return sq, dy.reshape(B, S, D), grads_small


def kernel(x, p, rel_bias, ev_w_in, ev_q_norm, ev_w_uq, ev_kv_norm, ev_w_ukv, ev_sinks, ev_w_out, od_w_in, od_b_f, od_w_out, ln1_g, ln1_b, w_up, w_down, ln2_g, ln2_b, ple_w_proj, ple_w_gate, ple_b_gate, loss_target, m_rel_bias, m_ev_w_in, m_ev_q_norm, m_ev_w_uq, m_ev_kv_norm, m_ev_w_ukv, m_ev_sinks, m_ev_w_out, m_od_w_in, m_od_b_f, m_od_w_out, m_ln1_g, m_ln1_b, m_w_up, m_w_down, m_ln2_g, m_ln2_b, m_ple_w_proj, m_ple_w_gate, m_ple_b_gate, v_rel_bias, v_ev_w_in, v_ev_q_norm, v_ev_w_uq, v_ev_kv_norm, v_ev_w_ukv, v_ev_sinks, v_ev_w_out, v_od_w_in, v_od_b_f, v_od_w_out, v_ln1_g, v_ln1_b, v_w_up, v_w_down, v_ln2_g, v_ln2_b, v_ple_w_proj, v_ple_w_gate, v_ple_b_gate):
    given = dict(locals())
    w = {n: given[n] for n in WEIGHTS}
    mom = {n: given["m_" + n] for n in WEIGHTS}
    var = {n: given["v_" + n] for n in WEIGHTS}
    small_shapes = {n: w[n].shape for n in SMALL}

    xchg = _MeshExchange({n: w[n].astype(BF16) for n in BIG})
    P = {n: w[n] for n in SMALL}

    sq, grad_x, grads_small = _local_step(x, p, loss_target, P, xchg)
    loss = lax.psum(0.5 * jnp.sum(sq) / D_MODEL, ("x", "y", "c"))

    received = xchg.finish()
    g_small_packed = _all_reduce_small(_pack_small(grads_small), name="reduce_small_grads")
    g_small = _unpack_small(g_small_packed, small_shapes)

    grad, delta, new_m, new_v = {}, {}, {}, {}
    for n in BIG:
        parts = [received[(n, idx)] for idx in range(w[n].shape[0])]
        grad[n], delta[n], new_m[n], new_v[n] = _adamw_slots(w[n], parts, mom[n], var[n], name=f"adamw_{n}")
    d, nm, nv = _adamw(_pack_small(w), g_small_packed, _pack_small(mom), _pack_small(var), name="adamw_small")
    d, nm, nv = (_unpack_small(t, small_shapes) for t in (d, nm, nv))
    for n in SMALL:
        grad[n], delta[n], new_m[n], new_v[n] = g_small[n], d[n], nm[n], nv[n]

    return (loss, grad_x, *[grad[n] for n in WEIGHTS], *[delta[n] for n in WEIGHTS],
            *[new_m[n] for n in WEIGHTS], *[new_v[n] for n in WEIGHTS])
```

```python
import math

import jax
import jax.numpy as jnp
from jax import lax
from jax.experimental import pallas as pl
from jax.experimental.pallas import tpu as pltpu

F32, BF16 = jnp.float32, jnp.bfloat16

D_MODEL = 1024
DEPTH = 4
HEAD_DIM = 64
MLA_HEADS, MLA_NOPE, MLA_ROPE, MLA_V = 8, 64, 32, 64
MLA_Q_LORA, MLA_KV_LORA = 384, 256
MLA_QK = MLA_NOPE + MLA_ROPE
ROPE_THETA = 10000.0
SWA_HEADS, SWA_KV_HEADS, SWA_WINDOW = 8, 2, 128
SWA_GROUP = SWA_HEADS // SWA_KV_HEADS
REL_BUCKETS, REL_MAX_DIST = 32, 128
FOX_HEADS = 16
D_FF = 4 * D_MODEL
D_PLE = 256
BLOCK_Q = 128
DN_ALPHA = (2 * DEPTH) ** 0.25
NORM_EPS = 1e-5
NEG_INF = -1e30
EVEN_IN = 1440
ODD_QKV = 3 * FOX_HEADS * HEAD_DIM
LANES = 128

EV_QS = (0, 512)
EV_CQ = (512, 896)
EV_CKV = (896, 1152)
EV_KS = (1152, 1280)
EV_VS = (1280, 1408)
EV_KR = (1408, 1536)
EVEN_IN_PAD = 1536
KR_LANE0 = MLA_NOPE

ADAM_LR, ADAM_B1, ADAM_B2, ADAM_EPS, ADAM_WD, ADAM_STEP = 0.001, 0.9, 0.999, 1e-08, 0.01, 10

N_DEV = 8
VMEM_LIMIT_BYTES = 48 * 1024 * 1024
ATT_TILE = 512
ATT_TILE_BWD = 512
PAIRS_PER_STEP_FWD = 4
PAIRS_PER_STEP_BWD = 2

NN = (((1,), (0,)), ((), ()))
NT = (((1,), (1,)), ((), ()))
TN = (((0,), (0,)), ((), ()))

BIG = ['ev_w_in', 'ev_w_uq', 'ev_w_ukv', 'ev_w_out', 'od_w_in', 'od_w_out', 'w_up', 'w_down',
       'ple_w_proj', 'ple_w_gate']
BIG_AXIS = {'ev_w_in': 2, 'ev_w_uq': 2, 'ev_w_ukv': 2, 'ev_w_out': 1, 'od_w_in': 2, 'od_w_out': 1,
            'w_up': 2, 'w_down': 1, 'ple_w_proj': 2, 'ple_w_gate': 1}
SMALL = ['rel_bias', 'ev_q_norm', 'ev_kv_norm', 'ev_sinks', 'od_b_f', 'ln1_g', 'ln1_b', 'ln2_g', 'ln2_b',
         'ple_b_gate']
WEIGHTS = ['rel_bias', 'ev_w_in', 'ev_q_norm', 'ev_w_uq', 'ev_kv_norm', 'ev_w_ukv', 'ev_sinks', 'ev_w_out',
           'od_w_in', 'od_b_f', 'od_w_out', 'ln1_g', 'ln1_b', 'w_up', 'w_down', 'ln2_g', 'ln2_b',
           'ple_w_proj', 'ple_w_gate', 'ple_b_gate']


def _cparams(*sem):
    return pltpu.CompilerParams(dimension_semantics=sem, vmem_limit_bytes=VMEM_LIMIT_BYTES)


def _pick(n, cands):
    for c in cands:
        if n % c == 0:
            return c
    return n


MM_STEP_BYTES = 10 * 1024 * 1024
MM_OUT_BYTES = 8 * 1024 * 1024
MM_CHUNK = 512


def _mm(a, b, *, trans_b=False, extras=(), epilogue=None, row_epilogue=None, out_dtypes=(F32,), out_widths=None,
        n_sums=0, comm=(), name):
    a_parts = tuple(a) if isinstance(a, (tuple, list)) else (a,)
    n_a = len(a_parts)
    M = a_parts[0].shape[0]
    k_offs = [sum(p.shape[1] for p in a_parts[:i]) for i in range(n_a + 1)]
    slot_w = b.shape[2] if b.ndim == 3 else None
    if slot_w is None:
        N = b.shape[0] if trans_b else b.shape[1]
    else:
        assert n_a == 1 and slot_w % LANES == 0
        N = b.shape[1] if trans_b else b.shape[0] * slot_w
    n_ex, n_out = len(extras), len(out_dtypes)
    n_rows_out = n_out - n_sums
    out_widths = (N,) * n_out if out_widths is None else out_widths
    row_bytes = sum(p.shape[1] * p.dtype.itemsize for p in a_parts) + (sum(w * jnp.dtype(d).itemsize
                                            for w, d in zip(out_widths[:n_rows_out], out_dtypes))
                                        + sum(e.shape[1] * e.dtype.itemsize for e in extras if e.shape[0] == M)
                                        + (4 * N if row_epilogue is not None else 0))
    tm = next((c for c in (1024, 512, 256) if M % c == 0 and c * row_bytes <= MM_STEP_BYTES), 128)
    nc = _pick(N, (MM_CHUNK, 384, 256, 128)) if slot_w is None or trans_b else slot_w
    n_c, kinds = len(comm), [k for k, _ in comm]
    n_scr = 1 if row_epilogue is not None else 0

    def body(*refs):
        a_refs, refs = refs[:n_a], refs[n_a - 1:]
        c_in = refs[2 + n_ex:2 + n_ex + n_c]
        c_out = refs[2 + n_ex + n_c + n_out:2 + n_ex + 2 * n_c + n_out]
        sems = refs[2 + n_ex + 2 * n_c + n_out + n_scr:]
        refs = refs[:2 + n_ex] + refs[2 + n_ex + n_c:2 + n_ex + n_c + n_out] \
            + refs[2 + n_ex + 2 * n_c + n_out:2 + n_ex + 2 * n_c + n_out + n_scr]
        if n_c:
            place = _mesh_place()
            step = pl.program_id(0)

            @pl.when(step == 0)
            def _():
                _comm_start(kinds, c_in, c_out, sems, place)

        b_ref = refs[1]
        ex = refs[2:2 + n_ex]
        outs = refs[2 + n_ex:2 + n_ex + n_out]
        avs = [r[...].astype(BF16) for r in a_refs]
        for n0 in range(0, N, nc):
            cols = slice(n0, n0 + nc)
            acc = None
            if slot_w is None:
                terms = [(av, b_ref[cols, k0:k1] if trans_b else b_ref[k0:k1, cols])
                         for av, k0, k1 in zip(avs, k_offs[:-1], k_offs[1:])]
            elif trans_b:
                terms = [(avs[0][:, sl * slot_w:(sl + 1) * slot_w], b_ref[sl, cols, :]) for sl in range(b.shape[0])]
            else:
                terms = [(avs[0], b_ref[n0 // slot_w])]
            for av, bv in terms:
                part = lax.dot_general(av, bv.astype(BF16), NT if trans_b else NN, preferred_element_type=F32)
                acc = part if acc is None else acc + part
            if row_epilogue is not None:
                refs[-1][:, cols] = acc
                continue
            res = epilogue(acc, *[e[:, cols] for e in ex]) if epilogue is not None else (acc,)
            for o, r in zip(outs, res):
                o[:, cols] = r.astype(o.dtype)
        if row_epilogue is not None:
            res = row_epilogue(refs[-1][...], *[e[...] for e in ex])
            for o, r in zip(outs[:n_rows_out], res):
                o[...] = r.astype(o.dtype)
            if n_sums:
                @pl.when(pl.program_id(0) == 0)
                def _():
                    for o in outs[n_rows_out:]:
                        o[...] = jnp.zeros_like(o)

                for o, r in zip(outs[n_rows_out:], res[n_rows_out:]):
                    o[...] += r
        if n_c:
            @pl.when(step == M // tm - 1)
            def _():
                _comm_wait(kinds, c_in, c_out, sems, place)

    in_specs = [pl.BlockSpec((tm, p.shape[1]), lambda i: (i, 0)) for p in a_parts]
    in_specs.append(pl.BlockSpec(b.shape, lambda i: (0,) * b.ndim))
    for e in extras:
        if e.shape[0] == M:
            in_specs.append(pl.BlockSpec((tm, e.shape[1]), lambda i: (i, 0)))
        elif e.shape == (1, N):
            in_specs.append(pl.BlockSpec((1, N), lambda i: (0, 0)))
        else:
            raise ValueError(f"extra operand of shape {e.shape} for a ({M}, {N}) result")
    res = pl.pallas_call(
        body, name=name, grid=(M // tm,), in_specs=in_specs + [HBM_SPEC] * n_c,
        out_specs=[pl.BlockSpec((tm, w), lambda i: (i, 0)) for w in out_widths[:n_rows_out]]
        + [pl.BlockSpec((1, w), lambda i: (0, 0)) for w in out_widths[n_rows_out:]] + [HBM_SPEC] * n_c,
        out_shape=[jax.ShapeDtypeStruct((M, w), d) for w, d in zip(out_widths[:n_rows_out], out_dtypes)]
        + [jax.ShapeDtypeStruct((1, w), d) for w, d in zip(out_widths[n_rows_out:], out_dtypes[n_rows_out:])]
        + _comm_out_shapes(comm),
        scratch_shapes=([pltpu.VMEM((tm, N), F32)] if row_epilogue is not None else [])
        + (_comm_scratch(comm) if n_c else []),
        compiler_params=_cparams("arbitrary" if n_sums or n_c else "parallel"),
    )(*a_parts, b, *extras, *[c for _, c in comm])
    main = res[0] if n_out == 1 else tuple(res[:n_out])
    return (main, list(res[n_out:])) if n_c else main


def _mm_tn(a, b, *, slot_width=None, name):
    T, K = a.shape
    N = b.shape[1]
    bk, bn = K, N
    while bk * bn * 4 > MM_OUT_BYTES:
        if bn >= bk and bn % (2 * LANES) == 0:
            bn //= 2
        else:
            bk //= 2
    tt = _pick(T, (1024, 512, 256))
    ck, cn = _pick(bk, (MM_CHUNK, 384, 256, 128)), _pick(bn, (MM_CHUNK, 384, 256, 128))

    def body(a_ref, b_ref, o_ref, acc_ref):
        t = pl.program_id(2)

        @pl.when(t == 0)
        def _():
            acc_ref[...] = jnp.zeros_like(acc_ref)

        for r0 in range(0, bk, ck):
            av = a_ref[:, r0:r0 + ck].astype(BF16)
            for c0 in range(0, bn, cn):
                acc_ref[r0:r0 + ck, c0:c0 + cn] += lax.dot_general(
                    av, b_ref[:, c0:c0 + cn].astype(BF16), TN, preferred_element_type=F32)

        @pl.when(t == T // tt - 1)
        def _():
            if slot_width is None:
                o_ref[...] = acc_ref[...].astype(o_ref.dtype)
            else:
                for slot in range(bn // slot_width):
                    o_ref[slot] = acc_ref[:, slot * slot_width:(slot + 1) * slot_width].astype(o_ref.dtype)

    if slot_width is None:
        out_spec, out_shape = pl.BlockSpec((bk, bn), lambda i, j, t: (i, j)), (K, N)
    else:
        assert bn % slot_width == 0 and slot_width % LANES == 0
        out_spec = pl.BlockSpec((bn // slot_width, bk, slot_width), lambda i, j, t: (j, i, 0))
        out_shape = (N // slot_width, K, slot_width)
    return pl.pallas_call(
        body, name=name, grid=(K // bk, N // bn, T // tt),
        in_specs=[pl.BlockSpec((tt, bk), lambda i, j, t: (t, i)), pl.BlockSpec((tt, bn), lambda i, j, t: (t, j))],
        out_specs=out_spec, out_shape=jax.ShapeDtypeStruct(out_shape, BF16),
        scratch_shapes=[pltpu.VMEM((bk, bn), F32)],
        compiler_params=_cparams("parallel", "parallel", "arbitrary"),
    )(a, b)


ROW_TILE = 1024


def _row_spec(cols, col_block=0):
    return pl.BlockSpec((ROW_TILE, cols), lambda i: (i, col_block))


def _tab_spec(cols, period):
    return pl.BlockSpec((ROW_TILE, cols), lambda i: (i % period, 0))


def _full_spec(shape):
    return pl.BlockSpec(shape, lambda i: (0,) * len(shape))


def _mm_ln(a, w, x, g, b, *, comm=(), name):
    def ln_rows(m, xv, gv, bv):
        z = DN_ALPHA * xv + m
        mu = jnp.mean(z, -1, keepdims=True)
        zc = z - mu
        r = lax.rsqrt(jnp.mean(zc * zc, -1, keepdims=True) + NORM_EPS)
        xh = zc * r
        y = xh * gv + bv
        return y, y, xh, jnp.broadcast_to(r, (r.shape[0], LANES))

    D = w.shape[1]
    return _mm(a, w, extras=(x, g, b), row_epilogue=ln_rows, out_dtypes=(F32, BF16, F32, F32),
               out_widths=(D, D, D, LANES), comm=comm, name=name)


def _mm_ln_bwd(a, w, resid, resid_scale, xh, r, g, *, name):
    def ln_bwd_rows(acc, rv, xhv, rstd, gv):
        dyv = acc + resid_scale * rv
        dyg = dyv * gv
        c1 = jnp.mean(dyg, -1, keepdims=True)
        c2 = jnp.mean(dyg * xhv, -1, keepdims=True)
        dz = _widen(rstd, dyv.shape[-1]) * (dyg - c1 - xhv * c2)
        return dz, dz, jnp.sum(dyv * xhv, 0, keepdims=True), jnp.sum(dyv, 0, keepdims=True)

    D = resid.shape[1]
    return _mm(a, w, trans_b=True, extras=(resid, xh, r, g), row_epilogue=ln_bwd_rows,
               out_dtypes=(F32, BF16, F32, F32), out_widths=(D, D, D, D), n_sums=2, name=name)


def _loss_grad(y, target, *, name):
    T, D = y.shape

    def body(y_ref, t_ref, dy_ref, sq_ref):
        err = y_ref[...] - t_ref[...]
        dy_ref[...] = err / D

        @pl.when(pl.program_id(0) == 0)
        def _():
            sq_ref[...] = jnp.zeros_like(sq_ref)

        sq_ref[...] += jnp.sum(err * err, 0, keepdims=True)

    return pl.pallas_call(
        body, name=name, grid=(T // ROW_TILE,),
        in_specs=[_row_spec(D), _row_spec(D)],
        out_specs=[_row_spec(D), _full_spec((1, D))],
        out_shape=[jax.ShapeDtypeStruct((T, D), F32), jax.ShapeDtypeStruct((1, D), F32)],
        compiler_params=_cparams("arbitrary"),
    )(y, target)


def _ple_bwd_elem(dx3, g, e, *, name):
    T, D = dx3.shape

    def body(dx_ref, g_ref, e_ref, de_ref, dz_ref, db_ref):
        dx, gv = dx_ref[...], g_ref[...]
        de_ref[...] = (dx * gv).astype(BF16)
        dz = dx * e_ref[...] * gv * (1.0 - gv)
        dz_ref[...] = dz.astype(BF16)

        @pl.when(pl.program_id(0) == 0)
        def _():
            db_ref[...] = jnp.zeros_like(db_ref)

        db_ref[...] += jnp.sum(dz, 0, keepdims=True)

    return pl.pallas_call(
        body, name=name, grid=(T // ROW_TILE,),
        in_specs=[_row_spec(D), _row_spec(D), _row_spec(D)],
        out_specs=[_row_spec(D), _row_spec(D), _full_spec((1, D))],
        out_shape=[jax.ShapeDtypeStruct((T, D), BF16), jax.ShapeDtypeStruct((T, D), BF16),
                   jax.ShapeDtypeStruct((1, D), F32)],
        compiler_params=_cparams("arbitrary"),
    )(dx3, g, e)


def _rotate(xv, a, bm, bp, sign):
    half = MLA_ROPE // 2
    width = xv.shape[-1]
    a, bm, bp = (_widen(t, width) for t in (a, bm, bp))
    return xv * a + sign * (pltpu.roll(xv, width - half, 1) * bm + pltpu.roll(xv, half, 1) * bp)


def _rope(x, tabs, seq, *, sign, name):
    T, width = x.shape

    def body(x_ref, a_ref, bm_ref, bp_ref, o_ref):
        o_ref[...] = _rotate(x_ref[...], a_ref[...], bm_ref[...], bp_ref[...], sign).astype(BF16)

    return pl.pallas_call(
        body, name=name, grid=(T // ROW_TILE,),
        in_specs=[_row_spec(width)] + [_tab_spec(LANES, seq // ROW_TILE)] * 3,
        out_specs=_row_spec(width),
        out_shape=jax.ShapeDtypeStruct((T, width), BF16),
        compiler_params=_cparams("parallel"),
    )(x, *tabs)


def _mla_keys(knp, h, k_tabs, seq, *, name):
    T = knp.shape[0]

    def body(k_ref, h_ref, a_ref, bm_ref, bp_ref, o_ref):
        kr = _rotate(h_ref[...], a_ref[...], bm_ref[...], bp_ref[...], 1.0)
        for hd in range(MLA_HEADS):
            cols = slice(hd * LANES, (hd + 1) * LANES)
            o_ref[:, cols] = (k_ref[:, cols].astype(F32) + kr).astype(BF16)

    return pl.pallas_call(
        body, name=name, grid=(T // ROW_TILE,),
        in_specs=[_row_spec(MLA_HEADS * LANES), _row_spec(LANES, EV_KR[0] // LANES)]
        + [_tab_spec(LANES, seq // ROW_TILE)] * 3,
        out_specs=_row_spec(MLA_HEADS * LANES),
        out_shape=jax.ShapeDtypeStruct((T, MLA_HEADS * LANES), BF16),
        compiler_params=_cparams("parallel"),
    )(knp, h, *k_tabs)


def _mla_rope_key_grad(dk, k_tabs, seq, *, name):
    T = dk.shape[0]

    def body(dk_ref, a_ref, bm_ref, bp_ref, o_ref):
        tot = dk_ref[:, 0:LANES]
        for hd in range(1, MLA_HEADS):
            tot = tot + dk_ref[:, hd * LANES:(hd + 1) * LANES]
        o_ref[...] = _rotate(tot, a_ref[...], bm_ref[...], bp_ref[...], -1.0).astype(BF16)

    return pl.pallas_call(
        body, name=name, grid=(T // ROW_TILE,),
        in_specs=[_row_spec(MLA_HEADS * LANES)] + [_tab_spec(LANES, seq // ROW_TILE)] * 3,
        out_specs=_row_spec(LANES),
        out_shape=jax.ShapeDtypeStruct((T, LANES), BF16),
        compiler_params=_cparams("parallel"),
    )(dk, *k_tabs)


def _even_norms(h, gq, gkv, *, name):
    T = h.shape[0]

    def body(h_ref, gq_ref, gkv_ref, cq_ref, ckv_ref, rq_ref, rkv_ref):
        cq = h_ref[:, EV_CQ[0]:EV_CQ[1]]
        rq = lax.rsqrt(jnp.mean(cq * cq, -1, keepdims=True) + NORM_EPS)
        cq_ref[...] = (cq * rq * gq_ref[...]).astype(BF16)
        rq_ref[...] = jnp.broadcast_to(rq, rq_ref.shape)
        ckv = h_ref[:, EV_CKV[0]:EV_CKV[1]]
        rkv = lax.rsqrt(jnp.mean(ckv * ckv, -1, keepdims=True) + NORM_EPS)
        ckv_ref[...] = (ckv * rkv * gkv_ref[...]).astype(BF16)
        rkv_ref[...] = jnp.broadcast_to(rkv, rkv_ref.shape)

    return pl.pallas_call(
        body, name=name, grid=(T // ROW_TILE,),
        in_specs=[_row_spec(EVEN_IN_PAD), _full_spec((1, MLA_Q_LORA)), _full_spec((1, MLA_KV_LORA))],
        out_specs=[_row_spec(MLA_Q_LORA), _row_spec(MLA_KV_LORA), _row_spec(LANES), _row_spec(LANES)],
        out_shape=[jax.ShapeDtypeStruct((T, MLA_Q_LORA), BF16), jax.ShapeDtypeStruct((T, MLA_KV_LORA), BF16),
                   jax.ShapeDtypeStruct((T, LANES), F32), jax.ShapeDtypeStruct((T, LANES), F32)],
        compiler_params=_cparams("parallel"),
    )(h, gq, gkv)


def _even_in_bwd(h, rq, rkv, gq, gkv, dcqn, dckvn, dqs, dks, dvs, dkr, *, name):
    T = h.shape[0]

    def rms_bwd(c, r, g, dy):
        r = _widen(r, c.shape[-1])
        xr = c * r
        dyg = dy * g
        return r * (dyg - xr * jnp.mean(dyg * xr, -1, keepdims=True)), jnp.sum(dy * xr, 0, keepdims=True)

    def body(h_ref, rq_ref, rkv_ref, gq_ref, gkv_ref, dcq_ref, dckv_ref, dqs_ref, dks_ref, dvs_ref, dkr_ref,
             dh_ref, dgq_ref, dgkv_ref):
        @pl.when(pl.program_id(0) == 0)
        def _():
            dgq_ref[...] = jnp.zeros_like(dgq_ref)
            dgkv_ref[...] = jnp.zeros_like(dgkv_ref)

        dcq, dgq = rms_bwd(h_ref[:, EV_CQ[0]:EV_CQ[1]], rq_ref[...], gq_ref[...], dcq_ref[...])
        dckv, dgkv = rms_bwd(h_ref[:, EV_CKV[0]:EV_CKV[1]], rkv_ref[...], gkv_ref[...], dckv_ref[...])
        dgq_ref[...] += dgq
        dgkv_ref[...] += dgkv
        dh_ref[:, EV_QS[0]:EV_QS[1]] = dqs_ref[...]
        dh_ref[:, EV_CQ[0]:EV_CQ[1]] = dcq.astype(BF16)
        dh_ref[:, EV_CKV[0]:EV_CKV[1]] = dckv.astype(BF16)
        dh_ref[:, EV_KS[0]:EV_KS[1]] = dks_ref[...]
        dh_ref[:, EV_VS[0]:EV_VS[1]] = dvs_ref[...]
        dh_ref[:, EV_KR[0]:EV_KR[1]] = dkr_ref[...]

    return pl.pallas_call(
        body, name=name, grid=(T // ROW_TILE,),
        in_specs=[_row_spec(EVEN_IN_PAD), _row_spec(LANES), _row_spec(LANES), _full_spec((1, MLA_Q_LORA)),
                  _full_spec((1, MLA_KV_LORA)), _row_spec(MLA_Q_LORA), _row_spec(MLA_KV_LORA),
                  _row_spec(SWA_HEADS * HEAD_DIM), _row_spec(LANES), _row_spec(LANES), _row_spec(LANES)],
        out_specs=[_row_spec(EVEN_IN_PAD), _full_spec((1, MLA_Q_LORA)), _full_spec((1, MLA_KV_LORA))],
        out_shape=[jax.ShapeDtypeStruct((T, EVEN_IN_PAD), BF16), jax.ShapeDtypeStruct((1, MLA_Q_LORA), F32),
                   jax.ShapeDtypeStruct((1, MLA_KV_LORA), F32)],
        compiler_params=_cparams("arbitrary"),
    )(h, rq, rkv, gq, gkv, dcqn, dckvn, dqs, dks, dvs, dkr)


def _fox_decay_fwd(f3, bf, *, name):
    B, S, _ = f3.shape

    def body(f_ref, b_ref, csh_ref, chs_ref):
        x = f_ref[...] + b_ref[...]
        c = jnp.minimum(x, 0.0) - jnp.log1p(jnp.exp(-jnp.abs(x)))
        row = lax.broadcasted_iota(jnp.int32, (S, LANES), 0)
        k = 1
        while k < S:
            c = c + jnp.where(row >= k, pltpu.roll(c, k, 0), 0.0)
            k *= 2
        csh_ref[...] = c
        chs_ref[...] = c.T

    return pl.pallas_call(
        body, name=name, grid=(B,),
        in_specs=[pl.BlockSpec((None, S, LANES), lambda b: (b, 0, 0)), pl.BlockSpec((1, LANES), lambda b: (0, 0))],
        out_specs=[pl.BlockSpec((None, S, LANES), lambda b: (b, 0, 0)),
                   pl.BlockSpec((None, LANES, S), lambda b: (b, 0, 0))],
        out_shape=[jax.ShapeDtypeStruct((B, S, LANES), F32), jax.ShapeDtypeStruct((B, LANES, S), F32)],
        compiler_params=_cparams("parallel"),
    )(f3, bf)


def _fox_decay_bwd(dc_hs, f3, bf, *, name):
    B, S, _ = f3.shape

    def body(dc_ref, f_ref, b_ref, df_ref, db_ref):
        g = dc_ref[...].T
        row = lax.broadcasted_iota(jnp.int32, (S, LANES), 0)
        k = 1
        while k < S:
            g = g + jnp.where(row < S - k, pltpu.roll(g, S - k, 0), 0.0)
            k *= 2
        x = f_ref[...] + b_ref[...]
        df = g * (1.0 / (1.0 + jnp.exp(x)))
        df_ref[...] = df.astype(BF16)

        @pl.when(pl.program_id(0) == 0)
        def _():
            db_ref[...] = jnp.zeros_like(db_ref)

        db_ref[...] += jnp.sum(df, 0, keepdims=True)

    return pl.pallas_call(
        body, name=name, grid=(B,),
        in_specs=[pl.BlockSpec((None, LANES, S), lambda b: (b, 0, 0)),
                  pl.BlockSpec((None, S, LANES), lambda b: (b, 0, 0)), pl.BlockSpec((1, LANES), lambda b: (0, 0))],
        out_specs=[pl.BlockSpec((None, S, LANES), lambda b: (b, 0, 0)), pl.BlockSpec((1, LANES), lambda b: (0, 0))],
        out_shape=[jax.ShapeDtypeStruct((B, S, LANES), BF16), jax.ShapeDtypeStruct((1, LANES), F32)],
        compiler_params=_cparams("arbitrary"),
    )(dc_hs, f3, bf)


def _head_column(block, h):
    lane = lax.broadcasted_iota(jnp.int32, block.shape, 1)
    return jnp.sum(jnp.where(lane == h, block, 0.0), axis=-1, keepdims=True)


def _causal_mask(s):
    r = lax.broadcasted_iota(jnp.int32, s.shape, 0)
    c = lax.broadcasted_iota(jnp.int32, s.shape, 1)
    return jnp.where(c <= r, s, NEG_INF)


def _low_half(shape):
    return (lax.broadcasted_iota(jnp.int32, shape, 1) % LANES) < HEAD_DIM


def _widen(x, cols):
    return jnp.concatenate([x] * (cols // LANES), axis=1)


def _both_halves(x, lo):
    r = pltpu.roll(x, HEAD_DIM, 1)
    return jnp.where(lo, x, r), jnp.where(lo, r, x)


MESH_ID = pl.DeviceIdType.MESH
HBM_SPEC = pl.BlockSpec(memory_space=pltpu.HBM)
VMEM_SPEC = pl.BlockSpec(memory_space=pltpu.VMEM)


def _mesh_place():
    x, y, c = lax.axis_index("x"), lax.axis_index("y"), lax.axis_index("c")
    return x, y, c, 4 * x + 2 * y + c


def _peers(x, y, c):
    out = []
    for mask in range(1, N_DEV):
        dx, dy, dc = (mask >> 2) & 1, (mask >> 1) & 1, mask & 1
        px, py, pc = (1 - x if dx else x), (1 - y if dy else y), (1 - c if dc else c)
        out.append(((px, py, pc), 4 * px + 2 * py + pc))
    return out


def _comm_out_shapes(comm):
    return [jax.ShapeDtypeStruct(a.shape if kind == "scatter" else (N_DEV,) + a.shape[1:], a.dtype) for kind, a in comm]


def _comm_scratch(comm):
    n = len(comm)
    return [pltpu.SemaphoreType.DMA((n, 7)), pltpu.SemaphoreType.DMA((n, 7)), pltpu.SemaphoreType.DMA((n,))]


def _comm_copies(kinds, in_refs, out_refs, sems, place):
    send_sems, recv_sems, local_sems = sems
    x, y, c, me = place
    local, remote = [], []
    for w, kind in enumerate(kinds):
        mine = in_refs[w].at[me] if kind == "scatter" else in_refs[w].at[kind[1]]
        local.append(pltpu.make_async_copy(mine, out_refs[w].at[me], local_sems.at[w]))
        for k, (peer, peer_idx) in enumerate(_peers(x, y, c)):
            remote.append(pltpu.make_async_remote_copy(
                src_ref=in_refs[w].at[peer_idx] if kind == "scatter" else mine, dst_ref=out_refs[w].at[me],
                send_sem=send_sems.at[w, k], recv_sem=recv_sems.at[w, k], device_id=peer, device_id_type=MESH_ID))
    return local, remote


def _comm_start(kinds, in_refs, out_refs, sems, place):
    local, remote = _comm_copies(kinds, in_refs, out_refs, sems, place)
    for cp in local + remote:
        cp.start()


def _comm_wait(kinds, in_refs, out_refs, sems, place):
    local, remote = _comm_copies(kinds, in_refs, out_refs, sems, place)
    for cp in remote:
        cp.wait_recv()
    for cp in remote:
        cp.wait_send()
    for cp in local:
        cp.wait()


def _exchange(comm, *, name):
    n = len(comm)
    kinds = [k for k, _ in comm]

    def body(*refs):
        place = _mesh_place()
        _comm_start(kinds, refs[:n], refs[n:2 * n], refs[2 * n:], place)
        _comm_wait(kinds, refs[:n], refs[n:2 * n], refs[2 * n:], place)

    return pl.pallas_call(
        body, name=name, out_shape=_comm_out_shapes(comm), in_specs=[HBM_SPEC] * n, out_specs=[HBM_SPEC] * n,
        scratch_shapes=_comm_scratch(comm),
    )(*[a for _, a in comm])


def _flash_fwd(qa, ka, va, *, q_blk0, k_blk0, v_blk0, W, n_pairs, B, S, scale, csh=None, crow=None, comm=(), name):
    t = ATT_TILE
    nq = S // t
    P = PAIRS_PER_STEP_FWD
    decay = csh is not None
    split = W == LANES
    assert n_pairs % P == 0 and q_blk0 % P == 0 and k_blk0 % P == 0 and v_blk0 % P == 0
    n_c, kinds = len(comm), [k for k, _ in comm]
    n_in = 5 if decay else 3
    fold_scale = math.log2(scale).is_integer()
    n_steps = (B, n_pairs // P, nq)

    def body(*refs):
        c_in, c_out = refs[n_in:n_in + n_c], refs[n_in + n_c + 2:n_in + 2 * n_c + 2]
        sems = refs[n_in + 2 * n_c + 4:]
        refs = refs[:n_in] + refs[n_in + n_c:n_in + n_c + 2] + refs[n_in + 2 * n_c + 2:n_in + 2 * n_c + 4]
        if decay:
            q_ref, k_ref, v_ref, csh_ref, crow_ref, o_ref, lse_ref, m_s, acc_s = refs
        else:
            q_ref, k_ref, v_ref, o_ref, lse_ref, m_s, acc_s = refs
        g, i = pl.program_id(1), pl.program_id(2)
        if n_c:
            place = _mesh_place()
            ids = [pl.program_id(ax) for ax in range(3)]

            @pl.when((ids[0] == 0) & (ids[1] == 0) & (ids[2] == 0))
            def _():
                _comm_start(kinds, c_in, c_out, sems, place)

        lo = _low_half((t, LANES))
        qv = q_ref[...]
        qh = []
        for pr in range(P):
            qp = qv[:, pr * W:(pr + 1) * W]
            qh += [jnp.where(lo, qp, jnp.zeros_like(qp)), jnp.where(lo, jnp.zeros_like(qp), qp)] if split \
                else [qp[:, :LANES], qp[:, LANES:]]
        if fold_scale:
            qh = [x * scale for x in qh]
        if decay:
            cq = [jnp.broadcast_to(_head_column(csh_ref[...], 2 * P * g + hd), (t, LANES)) for hd in range(2 * P)]
        m_s[...] = jnp.full(m_s.shape, NEG_INF, F32)
        acc_s[...] = jnp.zeros(acc_s.shape, F32)

        def step(j, masked):
            rows = pl.ds(pl.multiple_of(j * t, t), t)
            kb, vb = k_ref[rows, :], v_ref[rows, :]
            for pr in range(P):
                kp, vp = kb[:, pr * W:(pr + 1) * W], vb[:, pr * LANES:(pr + 1) * LANES]
                ones = jnp.ones_like(vp)
                vaug = [jnp.where(lo, vp, ones), jnp.where(lo, ones, vp)]
                for half in range(2):
                    hd = 2 * pr + half
                    kh = kp if split else kp[:, half * LANES:(half + 1) * LANES]
                    s = lax.dot_general(qh[hd], kh, NT, preferred_element_type=F32)
                    if not fold_scale:
                        s = s * scale
                    if decay:
                        s = s + _widen(cq[hd], t) - crow_ref[hd, j]
                    if masked:
                        s = _causal_mask(s)
                    m_prev = m_s[hd]
                    m_new = jnp.maximum(m_prev, jnp.max(s, -1, keepdims=True))
                    p = jnp.exp(s - _widen(m_new, t))
                    acc_s[hd] = jnp.exp(m_prev - m_new) * acc_s[hd] + lax.dot_general(
                        p.astype(BF16), vaug[half], NN, preferred_element_type=F32)
                    m_s[hd] = m_new

        def loop_body(j, carry):
            step(j, False)
            return carry

        lax.fori_loop(0, i, loop_body, 0)
        step(i, True)
        for pr in range(P):
            acc0, acc1 = acc_s[2 * pr], acc_s[2 * pr + 1]
            _, l0 = _both_halves(acc0, lo)
            l1, _ = _both_halves(acc1, lo)
            cols = slice(pr * LANES, (pr + 1) * LANES)
            o_ref[:, cols] = jnp.where(lo, acc0 / l0, acc1 / l1).astype(BF16)
            lse_ref[:, cols] = jnp.where(lo, m_s[2 * pr] + jnp.log(l0), m_s[2 * pr + 1] + jnp.log(l1))
        if n_c:
            @pl.when((ids[0] == n_steps[0] - 1) & (ids[1] == n_steps[1] - 1) & (ids[2] == n_steps[2] - 1))
            def _():
                _comm_wait(kinds, c_in, c_out, sems, place)

    in_specs = [pl.BlockSpec((t, P * W), lambda b, g, i: (b * nq + i, q_blk0 // P + g)),
                pl.BlockSpec((S, P * W), lambda b, g, i: (b, k_blk0 // P + g)),
                pl.BlockSpec((S, P * LANES), lambda b, g, i: (b, v_blk0 // P + g))]
    args = [qa, ka, va]
    if decay:
        in_specs += [pl.BlockSpec((None, t, LANES), lambda b, g, i: (b, i, 0)),
                     pl.BlockSpec((None, 2 * P, nq, 1, t), lambda b, g, i: (b, g, 0, 0, 0))]
        args += [csh, crow]
    out_spec = pl.BlockSpec((t, P * LANES), lambda b, g, i: (b * nq + i, g))
    res = pl.pallas_call(
        body, name=name, grid=n_steps, in_specs=in_specs + [HBM_SPEC] * n_c,
        out_specs=[out_spec, out_spec] + [HBM_SPEC] * n_c,
        out_shape=[jax.ShapeDtypeStruct((B * S, n_pairs * LANES), BF16),
                   jax.ShapeDtypeStruct((B * S, n_pairs * LANES), F32)] + _comm_out_shapes(comm),
        scratch_shapes=[pltpu.VMEM((2 * P, t, LANES), F32), pltpu.VMEM((2 * P, t, LANES), F32)]
        + (_comm_scratch(comm) if n_c else []),
        compiler_params=_cparams(*(("arbitrary",) * 3 if n_c else ("parallel",) * 3)),
    )(*args, *[a for _, a in comm])
    return res[0], res[1], list(res[2:])


def _flash_bwd(qa, ka, va, oa, doa, lsea, *, q_blk0, k_blk0, v_blk0, do_blk0, W, n_pairs, B, S, scale, qk_dtype,
               csh=None, crow=None, comm=(), name):
    t = ATT_TILE_BWD
    nq = S // t
    P = PAIRS_PER_STEP_BWD
    decay = csh is not None
    if decay:
        crow = crow.reshape(B, 2 * n_pairs, nq, 1, t)
    split = W == LANES
    assert n_pairs % P == 0 and q_blk0 % P == 0 and k_blk0 % P == 0 and v_blk0 % P == 0 and do_blk0 % P == 0
    n_c, kinds = len(comm), [k for k, _ in comm]
    n_in, n_out, n_scr = (8, 5, 8) if decay else (6, 3, 5)
    n_steps = (B, n_pairs // P, nq)

    def body(*refs):
        c_in = refs[n_in:n_in + n_c]
        c_out = refs[n_in + n_c + n_out:n_in + 2 * n_c + n_out]
        sems = refs[n_in + 2 * n_c + n_out + n_scr:]
        refs = (refs[:n_in] + refs[n_in + n_c:n_in + n_c + n_out]
                + refs[n_in + 2 * n_c + n_out:n_in + 2 * n_c + n_out + n_scr])
        if n_c:
            place = _mesh_place()
            ids = [pl.program_id(ax) for ax in range(3)]

            @pl.when((ids[0] == 0) & (ids[1] == 0) & (ids[2] == 0))
            def _():
                _comm_start(kinds, c_in, c_out, sems, place)

        if decay:
            (q_ref, k_ref, v_ref, o_ref, do_ref, lse_ref, csh_ref, crow_ref, dq_ref, dk_ref, dv_ref, dck_ref, dcq_ref,
             dq_s, lse_s, delta_s, dk_s, dv_s, cq_s, dcq_s, dck_s) = refs
        else:
            (q_ref, k_ref, v_ref, o_ref, do_ref, lse_ref, dq_ref, dk_ref, dv_ref,
             dq_s, lse_s, delta_s, dk_s, dv_s) = refs
        g, j = pl.program_id(1), pl.program_id(2)
        lo = _low_half((t, LANES))

        @pl.when(j == 0)
        def _():
            lo_s = _low_half((S, LANES))
            dq_s[...] = jnp.zeros(dq_s.shape, F32)
            for pr in range(P):
                cols = slice(pr * LANES, (pr + 1) * LANES)
                lse_s[2 * pr], lse_s[2 * pr + 1] = _both_halves(lse_ref[:, cols], lo_s)
                dd = do_ref[:, cols].astype(F32) * o_ref[:, cols].astype(F32)
                delta_s[2 * pr] = jnp.broadcast_to(jnp.sum(jnp.where(lo_s, dd, 0.0), -1, keepdims=True), (S, LANES))
                delta_s[2 * pr + 1] = jnp.broadcast_to(jnp.sum(jnp.where(lo_s, 0.0, dd), -1, keepdims=True),
                                                       (S, LANES))
            if decay:
                for hd in range(2 * P):
                    cq_s[hd] = jnp.broadcast_to(_head_column(csh_ref[...], 2 * P * g + hd), (S, LANES))
                dcq_s[...] = jnp.zeros(dcq_s.shape, F32)

        kb, vb = k_ref[...], v_ref[...]
        kh, vh = [], []
        for pr in range(P):
            kp, vp = kb[:, pr * W:(pr + 1) * W], vb[:, pr * LANES:(pr + 1) * LANES]
            zk, zv = jnp.zeros_like(kp), jnp.zeros_like(vp)
            kh += [jnp.where(lo, kp, zk), jnp.where(lo, zk, kp)] if split else [kp[:, :LANES], kp[:, LANES:]]
            vh += [jnp.where(lo, vp, zv), jnp.where(lo, zv, vp)]
        dk_s[...] = jnp.zeros(dk_s.shape, F32)
        dv_s[...] = jnp.zeros(dv_s.shape, F32)
        if decay:
            dck_s[...] = jnp.zeros(dck_s.shape, F32)

        def step(i, masked):
            rows = pl.ds(pl.multiple_of(i * t, t), t)
            qi, doi = q_ref[rows, :], do_ref[rows, :]
            for pr in range(P):
                qp, dop = qi[:, pr * W:(pr + 1) * W], doi[:, pr * LANES:(pr + 1) * LANES]
                for half in range(2):
                    hd = 2 * pr + half
                    qx = qp if split else qp[:, half * LANES:(half + 1) * LANES]
                    s = lax.dot_general(qx, kh[hd], NT, preferred_element_type=F32) * scale
                    if decay:
                        s = s + _widen(cq_s[hd, rows, :], t) - crow_ref[hd, j]
                    if masked:
                        s = _causal_mask(s)
                    p = jnp.exp(s - _widen(lse_s[hd, rows, :], t))
                    dv_s[hd] += lax.dot_general(p.astype(BF16), dop, TN, preferred_element_type=F32)
                    dp = lax.dot_general(dop, vh[hd], NT, preferred_element_type=F32)
                    ds = p * (dp - _widen(delta_s[hd, rows, :], t))
                    dss = (ds * scale).astype(BF16)
                    dk_s[hd] += lax.dot_general(dss, qx, TN, preferred_element_type=F32)
                    dqc = lax.dot_general(dss, kh[hd], NN, preferred_element_type=F32)
                    if split:
                        dq_s[rows, pr * W:(pr + 1) * W] += dqc
                    else:
                        dq_s[rows, hd * LANES:(hd + 1) * LANES] += dqc
                    if decay:
                        dck_s[hd] -= jnp.sum(ds, 0, keepdims=True)
                        part = ds[:, :LANES]
                        for c in range(1, t // LANES):
                            part = part + ds[:, c * LANES:(c + 1) * LANES]
                        dcq_s[hd, rows, :] += part

        def loop_body(i, carry):
            step(i, False)
            return carry

        step(j, True)
        lax.fori_loop(j + 1, nq, loop_body, 0)
        for pr in range(P):
            if split:
                dk_ref[:, pr * W:(pr + 1) * W] = jnp.where(lo, dk_s[2 * pr], dk_s[2 * pr + 1]).astype(dk_ref.dtype)
            else:
                for half in range(2):
                    hd = 2 * pr + half
                    dk_ref[:, hd * LANES:(hd + 1) * LANES] = dk_s[hd].astype(dk_ref.dtype)
            dv_ref[:, pr * LANES:(pr + 1) * LANES] = jnp.where(lo, dv_s[2 * pr], dv_s[2 * pr + 1]).astype(BF16)
        if decay:
            dck_ref[...] = dck_s[...]

        @pl.when(j == nq - 1)
        def _():
            dq_ref[...] = dq_s[...].astype(dq_ref.dtype)
            if decay:
                for hd in range(2 * P):
                    dcq_ref[hd] = jnp.sum(dcq_s[hd].T, 0, keepdims=True)

        if n_c:
            @pl.when((ids[0] == n_steps[0] - 1) & (ids[1] == n_steps[1] - 1) & (ids[2] == n_steps[2] - 1))
            def _():
                _comm_wait(kinds, c_in, c_out, sems, place)

    full = lambda w, blk0: pl.BlockSpec((S, P * w), lambda b, g, j: (b, blk0 // P + g))
    blk = lambda w, blk0: pl.BlockSpec((t, P * w), lambda b, g, j: (b * nq + j, blk0 // P + g))
    in_specs = [full(W, q_blk0), blk(W, k_blk0), blk(LANES, v_blk0), full(LANES, 0), full(LANES, do_blk0),
                full(LANES, 0)]
    args = [qa, ka, va, oa, doa, lsea]
    T = B * S
    out_specs = [full(W, 0), blk(W, 0), blk(LANES, 0)]
    out_shape = [jax.ShapeDtypeStruct((T, n_pairs * W), qk_dtype), jax.ShapeDtypeStruct((T, n_pairs * W), qk_dtype),
                 jax.ShapeDtypeStruct((T, n_pairs * LANES), BF16)]
    per_head = lambda rows: pltpu.VMEM((2 * P, rows, LANES), F32)
    scratch = [pltpu.VMEM((S, P * W), F32), per_head(S), per_head(S), per_head(t), per_head(t)]
    if decay:
        in_specs += [pl.BlockSpec((None, S, LANES), lambda b, g, j: (b, 0, 0)),
                     pl.BlockSpec((None, 2 * P, nq, 1, t), lambda b, g, j: (b, g, 0, 0, 0))]
        args += [csh, crow]
        out_specs += [pl.BlockSpec((None, 2 * P, None, 1, t), lambda b, g, j: (b, g, j, 0, 0)),
                      pl.BlockSpec((None, 2 * P, 1, S), lambda b, g, j: (b, g, 0, 0))]
        out_shape += [jax.ShapeDtypeStruct((B, 2 * n_pairs, nq, 1, t), F32),
                      jax.ShapeDtypeStruct((B, 2 * n_pairs, 1, S), F32)]
        scratch += [per_head(S), per_head(S), pltpu.VMEM((2 * P, 1, t), F32)]
    res = pl.pallas_call(
        body, name=name, grid=n_steps, in_specs=in_specs + [HBM_SPEC] * n_c,
        out_specs=out_specs + [HBM_SPEC] * n_c, out_shape=out_shape + _comm_out_shapes(comm),
        scratch_shapes=scratch + (_comm_scratch(comm) if n_c else []),
        compiler_params=_cparams(*(("arbitrary",) * 3 if n_c else ("parallel", "parallel", "arbitrary"))),
    )(*args, *[a for _, a in comm])
    return tuple(res[:n_out]) + (list(res[n_out:]),)


def _swa_common(q_ref, kp_ref, ko_ref, vp_ref, vo_ref, n):
    Q = BLOCK_Q
    lo = _low_half((Q, LANES))
    lo2 = _low_half((2 * Q, LANES))
    kk = jnp.concatenate([kp_ref[...], ko_ref[...]], axis=0)
    vv = jnp.concatenate([vp_ref[...], vo_ref[...]], axis=0)
    kdup = [x.astype(BF16) for x in _both_halves(kk, lo2)]
    vdup = [x.astype(BF16) for x in _both_halves(vv, lo2)]
    a = lax.broadcasted_iota(jnp.int32, (SWA_GROUP * Q, 2 * Q), 0) % Q
    col = lax.broadcasted_iota(jnp.int32, (SWA_GROUP * Q, 2 * Q), 1)
    dist = a + Q - col
    valid = (dist >= 0) & (dist < SWA_WINDOW) & ((col >= Q) | (n > 0))
    qv = q_ref[...]
    qm = []
    for a_head in range(SWA_HEADS):
        qp = qv[:, (a_head // 2) * LANES:(a_head // 2 + 1) * LANES]
        keep = lo if a_head % 2 == 0 else jnp.logical_not(lo)
        qm.append(jnp.where(keep, qp, 0.0).astype(BF16))
    qs = [jnp.concatenate(qm[g * SWA_GROUP:(g + 1) * SWA_GROUP], axis=0) for g in range(SWA_KV_HEADS)]
    return lo, lo2, kdup, vdup, valid, qs


def _swa_group_logits(g, qs, kdup, valid, bias_ref):
    heads = slice(g * SWA_GROUP, (g + 1) * SWA_GROUP)
    s = lax.dot_general(qs[g], kdup[g], NT, preferred_element_type=F32) * (HEAD_DIM ** -0.5)
    s = s + bias_ref[heads].reshape(SWA_GROUP * BLOCK_Q, 2 * BLOCK_Q)
    return heads, jnp.where(valid, s, NEG_INF)


def _pair_halves(x, lo):
    Q = BLOCK_Q
    return [jnp.where(lo, x[2 * pr * Q:(2 * pr + 1) * Q], x[(2 * pr + 1) * Q:(2 * pr + 2) * Q])
            for pr in range(SWA_GROUP // 2)]


def _swa_in_specs(nb):
    Q = BLOCK_Q
    own = lambda blk: (lambda b, n: (b * nb + n, blk))
    prev = lambda blk: (lambda b, n: (b * nb + jnp.maximum(n - 1, 0), blk))
    kb, vb = EV_KS[0] // LANES, EV_VS[0] // LANES
    return [pl.BlockSpec((Q, SWA_HEADS * HEAD_DIM), own(0)), pl.BlockSpec((Q, LANES), prev(kb)),
            pl.BlockSpec((Q, LANES), own(kb)), pl.BlockSpec((Q, LANES), prev(vb)), pl.BlockSpec((Q, LANES), own(vb))]


def _swa_fwd(h, bias, sinkcol, *, B, S, comm=(), name):
    Q = BLOCK_Q
    nb = S // Q
    n_c, kinds = len(comm), [k for k, _ in comm]

    def body(*refs):
        c_in, c_out, sems = refs[7:7 + n_c], refs[9 + n_c:9 + 2 * n_c], refs[9 + 2 * n_c:]
        q_ref, kp_ref, ko_ref, vp_ref, vo_ref, bias_ref, sink_ref = refs[:7]
        o_ref, lse_ref = refs[7 + n_c:9 + n_c]
        if n_c:
            place = _mesh_place()
            ids = [pl.program_id(0), pl.program_id(1)]

            @pl.when((ids[0] == 0) & (ids[1] == 0))
            def _():
                _comm_start(kinds, c_in, c_out, sems, place)

        lo, lo2, kdup, vdup, valid, qs = _swa_common(q_ref, kp_ref, ko_ref, vp_ref, vo_ref, pl.program_id(1))
        pairs = []
        lo4 = _low_half((SWA_GROUP * Q, LANES))
        for g in range(SWA_KV_HEADS):
            heads, s = _swa_group_logits(g, qs, kdup, valid, bias_ref)
            sink = jnp.broadcast_to(sink_ref[heads].reshape(SWA_GROUP * Q, 1), (SWA_GROUP * Q, LANES))
            m = jnp.maximum(jnp.max(s, -1, keepdims=True), sink)
            p = jnp.exp(s - _widen(m, 2 * Q))
            vaug = jnp.where(lo2, vdup[g], jnp.ones_like(vdup[g]))
            pv = lax.dot_general(p.astype(BF16), vaug, NN, preferred_element_type=F32)
            rolled = pltpu.roll(pv, HEAD_DIM, 1)
            l = jnp.where(lo4, rolled, pv) + jnp.exp(sink - m)
            out = pv / l
            lse_g = m + jnp.log(l)
            for i in range(SWA_GROUP):
                a = g * SWA_GROUP + i
                lse_ref[:, a * LANES:(a + 1) * LANES] = lse_g[i * Q:(i + 1) * Q]
            shifted = pltpu.roll(out, HEAD_DIM, 1)
            pairs += [jnp.where(lo, out[2 * pr * Q:(2 * pr + 1) * Q], shifted[(2 * pr + 1) * Q:(2 * pr + 2) * Q])
                      for pr in range(SWA_GROUP // 2)]
        o_ref[...] = jnp.concatenate(pairs, axis=1).astype(BF16)
        if n_c:
            @pl.when((ids[0] == B - 1) & (ids[1] == nb - 1))
            def _():
                _comm_wait(kinds, c_in, c_out, sems, place)

    whole = lambda shape: pl.BlockSpec(shape, lambda b, n: (0,) * len(shape))
    res = pl.pallas_call(
        body, name=name, grid=(B, nb),
        in_specs=_swa_in_specs(nb) + [whole((SWA_HEADS, Q, 2 * Q)), whole((SWA_HEADS, Q, 1))] + [HBM_SPEC] * n_c,
        out_specs=[pl.BlockSpec((Q, SWA_HEADS * HEAD_DIM), lambda b, n: (b * nb + n, 0)),
                   pl.BlockSpec((Q, SWA_HEADS * LANES), lambda b, n: (b * nb + n, 0))] + [HBM_SPEC] * n_c,
        out_shape=[jax.ShapeDtypeStruct((B * S, SWA_HEADS * HEAD_DIM), BF16),
                   jax.ShapeDtypeStruct((B * S, SWA_HEADS * LANES), F32)] + _comm_out_shapes(comm),
        scratch_shapes=_comm_scratch(comm) if n_c else [],
        compiler_params=_cparams(*(("arbitrary",) * 2 if n_c else ("parallel",) * 2)),
    )(h, h, h, h, h, bias, sinkcol, *[a for _, a in comm])
    return res[0], res[1], list(res[2:])


def _swa_bwd(h, o, do, lse, bias, sinkcol, *, do_blk0, B, S, comm=(), name):
    Q = BLOCK_Q
    nb = S // Q
    scale = HEAD_DIM ** -0.5
    n_c, kinds = len(comm), [k for k, _ in comm]

    def body(*refs):
        c_in, c_out, sems = refs[10:10 + n_c], refs[17 + n_c:17 + 2 * n_c], refs[17 + 2 * n_c:]
        q_ref, kp_ref, ko_ref, vp_ref, vo_ref, o_ref, do_ref, lse_ref, bias_ref, sink_ref = refs[:10]
        dq_ref, dko_ref, dkp_ref, dvo_ref, dvp_ref, dbias_ref, dsink_ref = refs[10 + n_c:17 + n_c]
        ids = [pl.program_id(0), pl.program_id(1)]
        if n_c:
            place = _mesh_place()

        @pl.when((ids[0] == 0) & (ids[1] == 0))
        def _():
            dbias_ref[...] = jnp.zeros_like(dbias_ref)
            dsink_ref[...] = jnp.zeros_like(dsink_ref)
            if n_c:
                _comm_start(kinds, c_in, c_out, sems, place)

        lo, lo2, kdup, vdup, valid, qs = _swa_common(q_ref, kp_ref, ko_ref, vp_ref, vo_ref, pl.program_id(1))
        dkk, dvv, dq_pairs = [], [], []
        for g in range(SWA_KV_HEADS):
            heads, s = _swa_group_logits(g, qs, kdup, valid, bias_ref)
            lse_g = jnp.concatenate([lse_ref[:, a * LANES:(a + 1) * LANES]
                                     for a in range(g * SWA_GROUP, (g + 1) * SWA_GROUP)], axis=0)
            p = jnp.exp(s - _widen(lse_g, 2 * Q))
            do_g, o_g = [], []
            for i in range(SWA_GROUP):
                cols = slice((g * SWA_GROUP + i) // 2 * LANES, ((g * SWA_GROUP + i) // 2 + 1) * LANES)
                do_p = do_ref[:, cols]
                do_g.append(jnp.where(lo if i % 2 == 0 else jnp.logical_not(lo), do_p, jnp.zeros_like(do_p)))
                o_g.append(o_ref[:, cols])
            doh, oh = jnp.concatenate(do_g, axis=0), jnp.concatenate(o_g, axis=0)
            delta = jnp.sum(doh.astype(F32) * oh.astype(F32), -1, keepdims=True)
            dp = lax.dot_general(doh, vdup[g], NT, preferred_element_type=F32)
            ds = p * (dp - delta)
            dbias_ref[heads] += ds.reshape(SWA_GROUP, Q, 2 * Q)
            dsink_ref[heads] -= (jnp.exp(sink_ref[heads].reshape(SWA_GROUP * Q, 1) - lse_g[:, :1])
                                 * delta).reshape(SWA_GROUP, Q, 1)
            dss = (ds * scale).astype(BF16)
            dq_pairs += _pair_halves(lax.dot_general(dss, kdup[g], NN, preferred_element_type=F32), lo)
            dkk.append(lax.dot_general(dss, qs[g], TN, preferred_element_type=F32))
            dvv.append(lax.dot_general(p.astype(BF16), doh, TN, preferred_element_type=F32))
        dq_ref[...] = jnp.concatenate(dq_pairs, axis=1).astype(BF16)
        fold = lambda x: x + pltpu.roll(x, HEAD_DIM, 1)
        dk_blk = jnp.where(lo2, fold(dkk[0]), fold(dkk[1]))
        dv_blk = jnp.where(lo2, fold(dvv[0]), fold(dvv[1]))
        dkp_ref[...] = dk_blk[:Q]
        dko_ref[...] = dk_blk[Q:]
        dvp_ref[...] = dv_blk[:Q]
        dvo_ref[...] = dv_blk[Q:]
        if n_c:
            @pl.when((ids[0] == B - 1) & (ids[1] == nb - 1))
            def _():
                _comm_wait(kinds, c_in, c_out, sems, place)

    whole = lambda shape: pl.BlockSpec(shape, lambda b, n: (0,) * len(shape))
    wide = lambda blk: pl.BlockSpec((Q, SWA_HEADS * HEAD_DIM), lambda b, n: (b * nb + n, blk))
    narrow = pl.BlockSpec((Q, LANES), lambda b, n: (b * nb + n, 0))
    kv_shape = jax.ShapeDtypeStruct((B * S, LANES), F32)
    res = pl.pallas_call(
        body, name=name, grid=(B, nb),
        in_specs=_swa_in_specs(nb) + [wide(0), wide(do_blk0),
                                      pl.BlockSpec((Q, SWA_HEADS * LANES), lambda b, n: (b * nb + n, 0)),
                                      whole((SWA_HEADS, Q, 2 * Q)),
                                      whole((SWA_HEADS, Q, 1))] + [HBM_SPEC] * n_c,
        out_specs=[wide(0), narrow, narrow, narrow, narrow, whole((SWA_HEADS, Q, 2 * Q)), whole((SWA_HEADS, Q, 1))]
        + [HBM_SPEC] * n_c,
        out_shape=[jax.ShapeDtypeStruct((B * S, SWA_HEADS * HEAD_DIM), BF16), kv_shape, kv_shape, kv_shape, kv_shape,
                   jax.ShapeDtypeStruct((SWA_HEADS, Q, 2 * Q), F32), jax.ShapeDtypeStruct((SWA_HEADS, Q, 1), F32)]
        + _comm_out_shapes(comm),
        scratch_shapes=_comm_scratch(comm) if n_c else [],
        compiler_params=_cparams("arbitrary", "arbitrary"),
    )(h, h, h, h, h, o, do, lse, bias, sinkcol, *[a for _, a in comm])
    return tuple(res[:7]) + (list(res[7:]),)


def _bias_bucket_sum(dbias, bucket, *, name):
    def body(d_ref, b_ref, o_ref):
        dbv, bk = d_ref[...], b_ref[...]
        lane = lax.broadcasted_iota(jnp.int32, (SWA_HEADS, LANES), 1)
        out = jnp.zeros((SWA_HEADS, LANES), F32)
        for b in range(REL_BUCKETS):
            part = jnp.sum(jnp.where(bk == b, dbv, 0.0), axis=1)
            tot = jnp.sum(part, axis=-1, keepdims=True)
            out = out + jnp.where(lane == b, tot, 0.0)
        o_ref[...] = out

    return pl.pallas_call(
        body, name=name, out_shape=jax.ShapeDtypeStruct((SWA_HEADS, LANES), F32),
        compiler_params=pltpu.CompilerParams(vmem_limit_bytes=VMEM_LIMIT_BYTES),
    )(dbias, bucket)


def _adamw_update(w, g, m, v):
    m_new = ADAM_B1 * m + (1.0 - ADAM_B1) * g
    v_new = ADAM_B2 * v + (1.0 - ADAM_B2) * jnp.square(g)
    m_hat = m_new / (1.0 - ADAM_B1 ** ADAM_STEP)
    v_hat = v_new / (1.0 - ADAM_B2 ** ADAM_STEP)
    return -ADAM_LR * (m_hat / (jnp.sqrt(v_hat) + ADAM_EPS) + ADAM_WD * w), m_new, v_new


def _adamw(w, g, m, v, *, name):
    def body(w_ref, g_ref, m_ref, v_ref, d_ref, nm_ref, nv_ref):
        d_ref[...], nm_ref[...], nv_ref[...] = _adamw_update(w_ref[...], g_ref[...], m_ref[...], v_ref[...])

    return pl.pallas_call(
        body, name=name, out_shape=[jax.ShapeDtypeStruct(w.shape, F32)] * 3,
        compiler_params=pltpu.CompilerParams(vmem_limit_bytes=VMEM_LIMIT_BYTES),
    )(w, g, m, v)


ADAMW_PARTS_BYTES = 8 * 1024 * 1024


def _adamw_slots(w, parts, m, v, *, name):
    n0, R, C = w.shape
    tr = next((c for c in (512, 256, 128, 64, 32, 16, 8) if R % c == 0 and 4 * n0 * N_DEV * c * C <= ADAMW_PARTS_BYTES), R)

    def body(*refs):
        w_ref, p_refs, (m_ref, v_ref, g_ref, d_ref, nm_ref, nv_ref) = refs[0], refs[1:1 + n0], refs[1 + n0:]
        layer = pl.program_id(0)
        for l in range(n0):
            @pl.when(layer == l)
            def _(p_ref=p_refs[l]):
                g = p_ref[0].astype(F32)
                for j in range(1, N_DEV):
                    g = g + p_ref[j].astype(F32)
                g_ref[...] = g
                d_ref[...], nm_ref[...], nv_ref[...] = _adamw_update(w_ref[...], g, m_ref[...], v_ref[...])

    spec = pl.BlockSpec((None, tr, C), lambda l, i: (l, i, 0))
    part_spec = lambda own: pl.BlockSpec((N_DEV, tr, C), lambda l, i: (0, jnp.where(l == own, i, 0), 0))
    return pl.pallas_call(
        body, name=name, grid=(n0, R // tr),
        in_specs=[spec] + [part_spec(l) for l in range(n0)] + [spec, spec], out_specs=[spec] * 4,
        out_shape=[jax.ShapeDtypeStruct((n0, R, C), F32)] * 4, compiler_params=_cparams("arbitrary", "arbitrary"),
    )(w, *parts, m, v)


def _all_gather_hbm(blocks, *, name):
    n = len(blocks)

    def body(*refs):
        x_refs, out_refs = refs[:n], refs[n:2 * n]
        send_sems, recv_sems, local_sems = refs[2 * n:]
        x, y, c, _ = _mesh_place()
        me, sibling = (x, y, c), (x, y, 1 - c)
        chips = [(1 - x, y), (x, 1 - y), (1 - x, 1 - y)]

        def copy(w, k, blk, to, src=None):
            px, py, pc = blk
            slot = out_refs[w].at[4 * px + 2 * py + pc]
            return pltpu.make_async_remote_copy(
                src_ref=slot if src is None else src, dst_ref=slot,
                send_sem=send_sems.at[w, k], recv_sem=recv_sems.at[w, k], device_id=to, device_id_type=MESH_ID)

        mine = [pltpu.make_async_copy(x_refs[w], out_refs[w].at[4 * x + 2 * y + c], local_sems.at[w])
                for w in range(n)]
        for cp in mine:
            cp.start()
        first = []
        for w in range(n):
            first.append(copy(w, 0, me, sibling, src=x_refs[w]))
            first += [copy(w, 1 + j, me, (*chip, c), src=x_refs[w]) for j, chip in enumerate(chips)]
        for cp in first:
            cp.start()
        passed = []
        for j, chip in enumerate(chips):
            for w in range(n):
                copy(w, 1 + j, (*chip, c), me).wait_recv()
                fwd = copy(w, 4 + j, (*chip, c), sibling)
                fwd.start()
                passed.append(fwd)
        for w in range(n):
            copy(w, 0, sibling, me).wait_recv()
            for j, chip in enumerate(chips):
                copy(w, 4 + j, (*chip, 1 - c), me).wait_recv()
        for cp in first + passed:
            cp.wait_send()
        for cp in mine:
            cp.wait()

    return pl.pallas_call(
        body, name=name, out_shape=[jax.ShapeDtypeStruct((N_DEV,) + b.shape, b.dtype) for b in blocks],
        in_specs=[HBM_SPEC] * n, out_specs=[HBM_SPEC] * n,
        scratch_shapes=[pltpu.SemaphoreType.DMA((n, 7)), pltpu.SemaphoreType.DMA((n, 7)),
                        pltpu.SemaphoreType.DMA((n,))],
    )(*blocks)


def _all_reduce_small(block, *, name):
    R, W = block.shape

    def body(x_ref, out_ref, buf, send_sems, recv_sems):
        x, y, c, me = _mesh_place()
        copies = []
        for k, (peer, _) in enumerate(_peers(x, y, c)):
            copies.append(pltpu.make_async_remote_copy(
                src_ref=x_ref, dst_ref=buf.at[me], send_sem=send_sems.at[k], recv_sem=recv_sems.at[k],
                device_id=peer, device_id_type=MESH_ID))
        for cp in copies:
            cp.start()
        buf[me] = x_ref[...]
        for cp in copies:
            cp.wait_recv()
        for cp in copies:
            cp.wait_send()
        acc = buf[0]
        for j in range(1, N_DEV):
            acc = acc + buf[j]
        out_ref[...] = acc

    return pl.pallas_call(
        body, name=name, out_shape=jax.ShapeDtypeStruct((R, W), F32),
        in_specs=[VMEM_SPEC], out_specs=VMEM_SPEC,
        scratch_shapes=[pltpu.VMEM((N_DEV, R, W), F32), pltpu.SemaphoreType.DMA((7,)), pltpu.SemaphoreType.DMA((7,))],
    )(block)


def _assemble(name, g):
    if name == 'w_up':
        return g
    if BIG_AXIS[name] == 2:
        return jnp.concatenate([g[j] for j in range(N_DEV)], axis=1)
    return g.reshape(N_DEV * g.shape[1], g.shape[2])


def _split_for_devices(name, g):
    if g.ndim == 3:
        return g
    if BIG_AXIS[name] == 2:
        b = g.shape[1] // N_DEV
        return jnp.stack([g[:, j * b:(j + 1) * b] for j in range(N_DEV)]).astype(BF16)
    return g.reshape(N_DEV, g.shape[0] // N_DEV, g.shape[1]).astype(BF16)


def _layer_weight_keys(i):
    j = i // 2
    mixer = [('ev_w_in', j), ('ev_w_uq', j), ('ev_w_ukv', j), ('ev_w_out', j)] if i % 2 == 0 \
        else [('od_w_in', j), ('od_w_out', j)]
    return mixer + [('w_up', i), ('w_down', i), ('ple_w_proj', i), ('ple_w_gate', i)]


def _weight_layer(key):
    name, idx = key
    return 2 * idx if name.startswith('ev_') else 2 * idx + 1 if name.startswith('od_') else idx


FIRST_GATHER = [('ev_w_in', 0), ('ev_w_uq', 0), ('ev_w_ukv', 0)]
FWD_CARRIERS = {
    'l0_mla': [('ev_w_out', 0), ('w_up', 0), ('ple_w_proj', 0), ('ple_w_gate', 0)],
    'l0_swa': [('w_down', 0)],
    'l0_out_ln1': [('od_w_out', 0)],
    'l0_up': [('od_w_in', 0)],
    'l0_down_ln2': [('w_up', 1)],
    'l0_ple_gate': [('ple_w_proj', 1), ('ple_w_gate', 1)],
    'l1_fox': [('w_down', 1), ('ev_w_in', 1), ('ev_w_uq', 1), ('ev_w_ukv', 1), ('ev_w_out', 1), ('w_up', 2)],
    'l1_up': [('w_down', 2)],
    'l1_down_ln2': [('ple_w_proj', 2), ('ple_w_gate', 2)],
    'l2_mla': [('od_w_in', 1), ('od_w_out', 1)],
    'l2_swa': [('w_up', 3)],
    'l2_up': [('w_down', 3)],
    'l2_down_ln2': [('ple_w_proj', 3), ('ple_w_gate', 3)],
}


class _MeshExchange:
    def __init__(self, shards):
        self.shards = shards
        self.weights = {i: {} for i in range(DEPTH)}
        self.pending = []
        self.in_flight = []
        self.received = {}
        got = _all_gather_hbm([self.shards[n][idx] for n, idx in FIRST_GATHER], name="gather_first")
        self._landed(FIRST_GATHER, got)

    def _landed(self, keys, gathered):
        for k, g in zip(keys, gathered):
            self.weights[_weight_layer(k)][k[0]] = _assemble(k[0], g)

    def layer_weights(self, i):
        return self.weights[i]

    def carry(self, kernel_name):
        return [(("gather", idx), self.shards[n]) for n, idx in FWD_CARRIERS.get(kernel_name, [])]

    def carried(self, kernel_name, outs):
        self._landed(FWD_CARRIERS.get(kernel_name, []), outs)

    def push_grads(self, grads):
        self.pending += [(k, _split_for_devices(k[0], g)) for k, g in grads.items()]

    def bwd_items(self):
        self.in_flight, self.pending = self.pending, []
        return [("scatter", parts) for _, parts in self.in_flight]

    def bwd_done(self, outs):
        for (k, _), got in zip(self.in_flight, outs):
            self.received[k] = got
        self.in_flight = []

    def finish(self):
        if self.pending:
            outs = _exchange(self.bwd_items(), name="scatter_rest")
            self.bwd_done(outs)
        return self.received


PACK_ROWS = 8


def _pack_small(vals):
    flat = jnp.concatenate([vals[n].reshape(-1).astype(F32) for n in SMALL])
    pad = (-flat.shape[0]) % (PACK_ROWS * LANES)
    return jnp.pad(flat, (0, pad)).reshape(-1, LANES)


def _unpack_small(block, shapes):
    flat = block.reshape(-1)
    out, off = {}, 0
    for n in SMALL:
        sz = math.prod(shapes[n])
        out[n] = flat[off:off + sz].reshape(shapes[n])
        off += sz
    return out


def _rope_tables(S):
    half = MLA_ROPE // 2
    inv = 1.0 / (ROPE_THETA ** (jnp.arange(0, MLA_ROPE, 2, dtype=F32) / MLA_ROPE))
    ang = jnp.arange(S, dtype=F32)[:, None] * inv[None, :]
    cos, sin = jnp.cos(ang), jnp.sin(ang)
    zeros = jnp.zeros((S, half), F32)
    tail = jnp.zeros((S, LANES - MLA_QK), F32)

    def block(rope_part, nope_val):
        return jnp.concatenate([jnp.full((S, MLA_NOPE), nope_val, F32), rope_part, tail], -1)

    a_r = jnp.concatenate([cos, cos], -1)
    bm_r = jnp.concatenate([-sin, zeros], -1)
    bp_r = jnp.concatenate([zeros, sin], -1)
    q_tabs = tuple(block(r, v) for r, v in ((a_r, 1.0), (bm_r, 0.0), (bp_r, 0.0)))
    k_tabs = tuple(block(r, 0.0) for r in (a_r, bm_r, bp_r))
    return q_tabs, k_tabs


def _t5_bucket(dist):
    exact = REL_BUCKETS // 2
    d = jnp.maximum(dist, 1).astype(F32)
    large = exact + (jnp.log(d / exact) / math.log(REL_MAX_DIST / exact) * (REL_BUCKETS - exact)).astype(jnp.int32)
    large = jnp.minimum(large, REL_BUCKETS - 1)
    return jnp.where(dist < exact, dist, large)


def _swa_bucket_table():
    a = jnp.arange(BLOCK_Q)[:, None]
    col = jnp.arange(2 * BLOCK_Q)[None, :]
    return _t5_bucket(jnp.maximum(a + BLOCK_Q - col, 0)).astype(jnp.int32)


def _even_weights(W):
    w = W['ev_w_in']
    c_kv1 = MLA_Q_LORA + MLA_KV_LORA
    c_kr1 = c_kv1 + MLA_ROPE
    c_qs1 = c_kr1 + SWA_HEADS * HEAD_DIM
    zeros = lambda n: jnp.zeros((D_MODEL, n), w.dtype)
    w_in = jnp.concatenate([w[:, c_kr1:c_qs1], w[:, :c_kv1], w[:, c_qs1:], zeros(KR_LANE0), w[:, c_kv1:c_kr1],
                            zeros(LANES - KR_LANE0 - MLA_ROPE)], axis=1)
    uq = W['ev_w_uq'].reshape(MLA_Q_LORA, MLA_HEADS, MLA_QK)
    w_uq = jnp.pad(uq, ((0, 0), (0, 0), (0, LANES - MLA_QK))).reshape(MLA_Q_LORA, MLA_HEADS * LANES)
    ukv = W['ev_w_ukv'].reshape(MLA_KV_LORA, MLA_HEADS, MLA_NOPE + MLA_V)
    w_k = jnp.pad(ukv[..., :MLA_NOPE], ((0, 0), (0, 0), (0, LANES - MLA_NOPE))).reshape(MLA_KV_LORA, -1)
    w_v = ukv[..., MLA_NOPE:].reshape(MLA_KV_LORA, MLA_HEADS * MLA_V)
    return w_in, w_uq, w_k, w_v


def _even_in_grad_unpad(dw):
    kr0 = EV_KR[0] + KR_LANE0
    return jnp.concatenate([dw[:, EV_CQ[0]:EV_CKV[1]], dw[:, kr0:kr0 + MLA_ROPE], dw[:, EV_QS[0]:EV_QS[1]],
                            dw[:, EV_KS[0]:EV_VS[1]]], axis=1)


def _even_fwd(xb, W, P, i, B, S, tabs, xchg, tag):
    j = i // 2
    q_tabs, k_tabs, bias, sinkcol = tabs
    w_in, w_uq, w_k, w_v = _even_weights(W)
    h = _mm(xb, w_in, name=f"{tag}_in")
    cqn, ckvn, rq, rkv = _even_norms(h, P['ev_q_norm'][j][None], P['ev_kv_norm'][j][None], name=f"{tag}_norms")
    q = _rope(_mm(cqn, w_uq, name=f"{tag}_uq"), q_tabs, S, sign=1.0, name=f"{tag}_ropeq")
    knp = _mm(ckvn, w_k, out_dtypes=(BF16,), name=f"{tag}_uk")
    v = _mm(ckvn, w_v, out_dtypes=(BF16,), name=f"{tag}_uv")
    k = _mla_keys(knp, h, k_tabs, S, name=f"{tag}_keys")
    o_mla, lse_mla, got = _flash_fwd(q, k, v, q_blk0=0, k_blk0=0, v_blk0=0, W=2 * LANES, n_pairs=MLA_HEADS // 2,
                                     B=B, S=S, scale=MLA_QK ** -0.5, comm=xchg.carry(f"{tag}_mla"), name=f"{tag}_mla")
    xchg.carried(f"{tag}_mla", got)
    o_swa, lse_swa, got = _swa_fwd(h, bias, sinkcol, B=B, S=S, comm=xchg.carry(f"{tag}_swa"), name=f"{tag}_swa")
    xchg.carried(f"{tag}_swa", got)
    res = dict(h=h, cqn=cqn, ckvn=ckvn, rq=rq, rkv=rkv, q=q, k=k, v=v, o_mla=o_mla, lse_mla=lse_mla,
               o_swa=o_swa, lse_swa=lse_swa)
    return ((o_mla, o_swa), W['ev_w_out']), res


def _shift_prev(own, prev, B, S):
    prev = prev.reshape(B, S, LANES)
    shifted = jnp.concatenate([prev[:, BLOCK_Q:], jnp.zeros_like(prev[:, :BLOCK_Q])], axis=1)
    return (own + shifted.reshape(B * S, LANES)).astype(BF16)


def _even_bwd(dmb, dz1, xb, W, P, j, B, S, tabs, res, xchg, tag):
    q_tabs, k_tabs, bias, sinkcol = tabs
    (w_in, w_uq, w_k, w_v), w_out = _even_weights(W), W['ev_w_out']
    g = {}
    g['ev_w_out'] = jnp.concatenate([_mm_tn(res['o_mla'], dmb, name=f"{tag}_dwout_mla"),
                                     _mm_tn(res['o_swa'], dmb, name=f"{tag}_dwout_swa")], axis=0)
    do = _mm(dmb, w_out, trans_b=True, out_dtypes=(BF16,), name=f"{tag}_do")
    dq, dk, dv, got = _flash_bwd(res['q'], res['k'], res['v'], res['o_mla'], do, res['lse_mla'], q_blk0=0, k_blk0=0,
                                 v_blk0=0, do_blk0=0, W=2 * LANES, n_pairs=MLA_HEADS // 2, B=B, S=S,
                                 scale=MLA_QK ** -0.5, qk_dtype=F32, comm=xchg.bwd_items(), name=f"{tag}_mla_bwd")
    xchg.bwd_done(got)
    dq_pre = _rope(dq, q_tabs, S, sign=-1.0, name=f"{tag}_ropeq_bwd")
    dw_uq = _mm_tn(res['cqn'], dq_pre, name=f"{tag}_dwuq")
    g['ev_w_uq'] = dw_uq.reshape(MLA_Q_LORA, MLA_HEADS, LANES)[..., :MLA_QK].reshape(MLA_Q_LORA, MLA_HEADS * MLA_QK)
    dcqn = _mm(dq_pre, w_uq, trans_b=True, name=f"{tag}_dcqn")
    dw_k = _mm_tn(res['ckvn'], dk, name=f"{tag}_dwuk").reshape(MLA_KV_LORA, MLA_HEADS, LANES)[..., :MLA_NOPE]
    dw_v = _mm_tn(res['ckvn'], dv, name=f"{tag}_dwuv").reshape(MLA_KV_LORA, MLA_HEADS, MLA_V)
    g['ev_w_ukv'] = jnp.concatenate([dw_k, dw_v], axis=-1).reshape(MLA_KV_LORA, MLA_HEADS * (MLA_NOPE + MLA_V))
    dckvn_v = _mm(dv, w_v, trans_b=True, name=f"{tag}_dckvn_v")
    dckvn = _mm(dk, w_k, trans_b=True, extras=(dckvn_v,), epilogue=lambda acc, r: (acc + r,), name=f"{tag}_dckvn")
    dkr_pre = _mla_rope_key_grad(dk, k_tabs, S, name=f"{tag}_ropek_bwd")
    xchg.push_grads({(n, j): g.pop(n) for n in list(g)})
    dqs, dko, dkp, dvo, dvp, dbias, dsink, got = _swa_bwd(res['h'], res['o_swa'], do, res['lse_swa'], bias, sinkcol,
                                                          do_blk0=1, B=B, S=S, comm=xchg.bwd_items(),
                                                          name=f"{tag}_swa_bwd")
    xchg.bwd_done(got)
    dh, dgq, dgkv = _even_in_bwd(res['h'], res['rq'], res['rkv'], P['ev_q_norm'][j][None], P['ev_kv_norm'][j][None],
                                 dcqn, dckvn, dqs, _shift_prev(dko, dkp, B, S), _shift_prev(dvo, dvp, B, S), dkr_pre,
                                 name=f"{tag}_in_bwd")
    g['ev_w_in'] = _even_in_grad_unpad(_mm_tn(xb, dh, name=f"{tag}_dwin"))
    xchg.push_grads({(n, j): val for n, val in g.items()})
    dx_kwargs = dict(trans_b=True, extras=(dz1,), epilogue=lambda acc, r: (acc + DN_ALPHA * r,), name=f"{tag}_dx")
    dx = _scattering(xchg, _mm, dh, w_in, **dx_kwargs) if j == 0 else _mm(dh, w_in, **dx_kwargs)
    small = dict(ev_q_norm=dgq[0], ev_kv_norm=dgkv[0], dbias=dbias, ev_sinks=jnp.sum(dsink, axis=(1, 2)))
    return dx, small


def _odd_fwd(xb, W, P, i, B, S, xchg, tag):
    j = i // 2
    w = W['od_w_in']
    w_qkv = w[:, :ODD_QKV]
    w_f = jnp.pad(w[:, ODD_QKV:], ((0, 0), (0, LANES - FOX_HEADS)))
    bf = jnp.pad(P['od_b_f'][j], (0, LANES - FOX_HEADS))[None]
    qkv = _mm(xb, w_qkv, out_dtypes=(BF16,), name=f"{tag}_qkv")
    f = _mm(xb, w_f, name=f"{tag}_f").reshape(B, S, LANES)
    csh, chs = _fox_decay_fwd(f, bf, name=f"{tag}_decay")
    crow = chs[:, :FOX_HEADS].reshape(B, FOX_HEADS, S // ATT_TILE, 1, ATT_TILE)
    n_blk = FOX_HEADS * HEAD_DIM // LANES
    o, lse, got = _flash_fwd(qkv, qkv, qkv, q_blk0=0, k_blk0=n_blk, v_blk0=2 * n_blk, W=LANES,
                             n_pairs=FOX_HEADS // 2, B=B, S=S, scale=HEAD_DIM ** -0.5, csh=csh, crow=crow,
                             comm=xchg.carry(f"{tag}_fox"), name=f"{tag}_fox")
    xchg.carried(f"{tag}_fox", got)
    res = dict(f=f, bf=bf, csh=csh, crow=crow, qkv=qkv, o=o, lse=lse, w_qkv=w_qkv, w_f=w_f)
    return (o, W['od_w_out']), res


def _odd_bwd(dmb, dz1, xb, W, P, j, B, S, res, xchg, tag):
    g = {}
    w_out = W['od_w_out']
    g['od_w_out'] = _mm_tn(res['o'], dmb, name=f"{tag}_dwout")
    do = _mm(dmb, w_out, trans_b=True, out_dtypes=(BF16,), name=f"{tag}_do")
    qkv = res['qkv']
    n_blk = FOX_HEADS * HEAD_DIM // LANES
    dq, dk, dv, dck, dcq, got = _flash_bwd(qkv, qkv, qkv, res['o'], do, res['lse'], q_blk0=0, k_blk0=n_blk,
                                           v_blk0=2 * n_blk, do_blk0=0, W=LANES, n_pairs=FOX_HEADS // 2, B=B, S=S,
                                           scale=HEAD_DIM ** -0.5, qk_dtype=BF16, csh=res['csh'], crow=res['crow'],
                                           comm=xchg.bwd_items(), name=f"{tag}_fox_bwd")
    xchg.bwd_done(got)
    dc = dck.reshape(B, FOX_HEADS, S) + dcq.reshape(B, FOX_HEADS, S)
    dc_hs = jnp.pad(dc, ((0, 0), (0, LANES - FOX_HEADS), (0, 0)))
    df, dbf = _fox_decay_bwd(dc_hs, res['f'], res['bf'], name=f"{tag}_decay_bwd")
    df = df.reshape(B * S, LANES)
    dw_qkv = [_mm_tn(xb, t, name=f"{tag}_dw{n}") for n, t in (("q", dq), ("k", dk), ("v", dv))]
    dw_f = _mm_tn(xb, df, name=f"{tag}_dwf")
    g['od_w_in'] = jnp.concatenate(dw_qkv + [dw_f[:, :FOX_HEADS]], axis=1)
    dxf = _mm(df, res['w_f'], trans_b=True, extras=(dz1,), epilogue=lambda acc, r: (acc + DN_ALPHA * r,),
              name=f"{tag}_dxf")
    xchg.push_grads({(n, j): val for n, val in g.items()})
    dx = _mm((dq, dk, dv), res['w_qkv'], trans_b=True, extras=(dxf,), epilogue=lambda acc, r: (acc + r,),
             name=f"{tag}_dx")
    small = dict(od_b_f=dbf[0, :FOX_HEADS])
    return dx, small


def _carrying(xchg, name, call, *args, **kwargs):
    comm = xchg.carry(name)
    out = call(*args, comm=comm, name=name, **kwargs)
    if comm:
        out, got = out
        xchg.carried(name, got)
    return out


def _scattering(xchg, call, *args, **kwargs):
    comm = xchg.bwd_items()
    out = call(*args, comm=comm, **kwargs)
    if comm:
        out, got = out
        xchg.bwd_done(got)
    return out


def _local_step(x, p, target, P, xchg):
    B, S, D = x.shape
    T = B * S
    q_tabs, k_tabs = _rope_tables(S)
    bucket = _swa_bucket_table()
    in_bucket = (bucket[..., None] == jnp.arange(REL_BUCKETS)).astype(F32)
    bias = jnp.einsum('acb,bh->hac', in_bucket, P['rel_bias'], precision=lax.Precision.HIGHEST)

    xc = x.reshape(T, D)
    xcb = xc.astype(BF16)
    saved = []
    for i in range(DEPTH):
        j = i // 2
        tag = f"l{i}"
        W = xchg.layer_weights(i)
        lay = dict(xb=xcb, W=W)
        if i % 2 == 0:
            sinkcol = jnp.broadcast_to(P['ev_sinks'][j][:, None, None], (SWA_HEADS, BLOCK_Q, 1)).astype(F32)
            lay['tabs'] = (q_tabs, k_tabs, bias, sinkcol)
            (o, w_out), lay['mix'] = _even_fwd(xcb, W, P, i, B, S, lay['tabs'], xchg, tag)
        else:
            (o, w_out), lay['mix'] = _odd_fwd(xcb, W, P, i, B, S, xchg, tag)
        x1, x1b, lay['xh1'], lay['r1'] = _carrying(xchg, f"{tag}_out_ln1", _mm_ln, o, w_out, xc,
                                                   P['ln1_g'][i][None], P['ln1_b'][i][None])
        lay['x1b'] = x1b
        lay['u'], lay['a'] = _carrying(xchg, f"{tag}_up", _mm, x1b, W['w_up'], out_dtypes=(F32, BF16),
                                       epilogue=lambda acc: (acc, jnp.square(jnp.maximum(acc, 0.0))))
        x2, x2b, lay['xh2'], lay['r2'] = _carrying(xchg, f"{tag}_down_ln2", _mm_ln, lay['a'], W['w_down'], x1,
                                                   P['ln2_g'][i][None], P['ln2_b'][i][None])
        lay['x2b'] = x2b
        lay['p'] = p[i].reshape(T, D_PLE)
        lay['e'] = _mm(lay['p'], W['ple_w_proj'], name=f"{tag}_ple_proj")

        def gate(acc, bg, e, x2v):
            gv = 1.0 / (1.0 + jnp.exp(-(acc + bg)))
            y = x2v + gv * e
            return y, y, gv

        xc, xcb, lay['g'] = _carrying(xchg, f"{tag}_ple_gate", _mm, x2b, W['ple_w_gate'],
                                      extras=(P['ple_b_gate'][i][None], lay['e'], x2), epilogue=gate,
                                      out_dtypes=(F32, BF16, F32))
        saved.append(lay)

    dy, sq = _loss_grad(xc, target.reshape(T, D), name="loss")

    Gs = {n: [None] * DEPTH for n in ('ln1_g', 'ln1_b', 'ln2_g', 'ln2_b', 'ple_b_gate')}
    Gs.update({n: [None] * (DEPTH // 2) for n in ('ev_q_norm', 'ev_kv_norm', 'ev_sinks', 'od_b_f')})
    dbias_total = None
    for i in reversed(range(DEPTH)):
        j = i // 2
        tag = f"l{i}b"
        lay = saved[i]
        W = lay['W']
        de, dzg, dbg = _ple_bwd_elem(dy, lay['g'], lay['e'], name=f"{tag}_ple_elem")
        Gs['ple_b_gate'][i] = dbg[0]
        g_mlp = {('ple_w_proj', i): _mm_tn(lay['p'], de, slot_width=D_MODEL // N_DEV, name=f"{tag}_dwproj"),
                 ('ple_w_gate', i): _mm_tn(lay['x2b'], dzg, name=f"{tag}_dwgate")}
        dz2, dz2b, dg2, db2 = _mm_ln_bwd(dzg, W['ple_w_gate'], dy, 1.0, lay['xh2'], lay['r2'], P['ln2_g'][i][None],
                                         name=f"{tag}_dx2_ln2")
        Gs['ln2_g'][i], Gs['ln2_b'][i] = dg2[0], db2[0]
        g_mlp[('w_down', i)] = _mm_tn(lay['a'], dz2b, name=f"{tag}_dwdown")
        du = _mm(dz2b, W['w_down'], trans_b=True, extras=(lay['u'],), out_dtypes=(BF16,),
                 epilogue=lambda acc, u: (acc * (2.0 * jnp.maximum(u, 0.0)),), name=f"{tag}_du")
        g_mlp[('w_up', i)] = _mm_tn(lay['x1b'], du, slot_width=D_FF // N_DEV, name=f"{tag}_dwup")
        xchg.push_grads(g_mlp)
        dz1, dz1b, dg1, db1 = _mm_ln_bwd(du, W['w_up'], dz2, DN_ALPHA, lay['xh1'], lay['r1'], P['ln1_g'][i][None],
                                         name=f"{tag}_dx1_ln1")
        Gs['ln1_g'][i], Gs['ln1_b'][i] = dg1[0], db1[0]
        if i % 2 == 0:
            dy, small = _even_bwd(dz1b, dz1, lay['xb'], W, P, j, B, S, lay['tabs'], lay['mix'], xchg, tag)
            dbias_total = small['dbias'] if dbias_total is None else dbias_total + small['dbias']
            for n in ('ev_q_norm', 'ev_kv_norm', 'ev_sinks'):
                Gs[n][j] = small[n]
        else:
            dy, small = _odd_bwd(dz1b, dz1, lay['xb'], W, P, j, B, S, lay['mix'], xchg, tag)
            Gs['od_b_f'][j] = small['od_b_f']

    grads_small = {n: jnp.stack(v) for n, v in Gs.items()}
    drel = _bias_bucket_sum(dbias_total, bucket, name="rel_bias_grad")
    grads_small['rel_bias'] = drel[:, :REL_BUCKETS].T
    return sq, dy.reshape(B, S, D), grads_small


def kernel(x, p, rel_bias, ev_w_in, ev_q_norm, ev_w_uq, ev_kv_norm, ev_w_ukv, ev_sinks, ev_w_out, od_w_in, od_b_f, od_w_out, ln1_g, ln1_b, w_up, w_down, ln2_g, ln2_b, ple_w_proj, ple_w_gate, ple_b_gate, loss_target, m_rel_bias, m_ev_w_in, m_ev_q_norm, m_ev_w_uq, m_ev_kv_norm, m_ev_w_ukv, m_ev_sinks, m_ev_w_out, m_od_w_in, m_od_b_f, m_od_w_out, m_ln1_g, m_ln1_b, m_w_up, m_w_down, m_ln2_g, m_ln2_b, m_ple_w_proj, m_ple_w_gate, m_ple_b_gate, v_rel_bias, v_ev_w_in, v_ev_q_norm, v_ev_w_uq, v_ev_kv_norm, v_ev_w_ukv, v_ev_sinks, v_ev_w_out, v_od_w_in, v_od_b_f, v_od_w_out, v_ln1_g, v_ln1_b, v_w_up, v_w_down, v_ln2_g, v_ln2_b, v_ple_w_proj, v_ple_w_gate, v_ple_b_gate):
    given = dict(locals())
    w = {n: given[n] for n in WEIGHTS}
    mom = {n: given["m_" + n] for n in WEIGHTS}
    var = {n: given["v_" + n] for n in WEIGHTS}
    small_shapes = {n: w[n].shape for n in SMALL}

    xchg = _MeshExchange({n: w[n].astype(BF16) for n in BIG})
    P = {n: w[n] for n in SMALL}

    sq, grad_x, grads_small = _local_step(x, p, loss_target, P, xchg)
    loss = lax.psum(0.5 * jnp.sum(sq) / D_MODEL, ("x", "y", "c"))

    received = xchg.finish()
    g_small_packed = _all_reduce_small(_pack_small(grads_small), name="reduce_small_grads")
    g_small = _unpack_small(g_small_packed, small_shapes)

    grad, delta, new_m, new_v = {}, {}, {}, {}
    for n in BIG:
        parts = [received[(n, idx)] for idx in range(w[n].shape[0])]
        grad[n], delta[n], new_m[n], new_v[n] = _adamw_slots(w[n], parts, mom[n], var[n], name=f"adamw_{n}")
    d, nm, nv = _adamw(_pack_small(w), g_small_packed, _pack_small(mom), _pack_small(var), name="adamw_small")
    d, nm, nv = (_unpack_small(t, small_shapes) for t in (d, nm, nv))
    for n in SMALL:
        grad[n], delta[n], new_m[n], new_v[n] = g_small[n], d[n], nm[n], nv[n]

    return (loss, grad_x, *[grad[n] for n in WEIGHTS], *[delta[n] for n in WEIGHTS],
            *[new_m[n] for n in WEIGHTS], *[new_v[n] for n in WEIGHTS])
```

```python
import math

import jax
import jax.numpy as jnp
from jax import lax
from jax.experimental import pallas as pl
from jax.experimental.pallas import tpu as pltpu

F32, BF16 = jnp.float32, jnp.bfloat16

D_MODEL = 1024
DEPTH = 4
HEAD_DIM = 64
MLA_HEADS, MLA_NOPE, MLA_ROPE, MLA_V = 8, 64, 32, 64
MLA_Q_LORA, MLA_KV_LORA = 384, 256
MLA_QK = MLA_NOPE + MLA_ROPE
ROPE_THETA = 10000.0
SWA_HEADS, SWA_KV_HEADS, SWA_WINDOW = 8, 2, 128
SWA_GROUP = SWA_HEADS // SWA_KV_HEADS
REL_BUCKETS, REL_MAX_DIST = 32, 128
FOX_HEADS = 16
D_FF = 4 * D_MODEL
D_PLE = 256
BLOCK_Q = 128
DN_ALPHA = (2 * DEPTH) ** 0.25
NORM_EPS = 1e-5
NEG_INF = -1e30
EVEN_IN = 1440
ODD_QKV = 3 * FOX_HEADS * HEAD_DIM
LANES = 128

EV_QS = (0, 512)
EV_CQ = (512, 896)
EV_CKV = (896, 1152)
EV_KS = (1152, 1280)
EV_VS = (1280, 1408)
EV_KR = (1408, 1536)
EVEN_IN_PAD = 1536
KR_LANE0 = MLA_NOPE

ADAM_LR, ADAM_B1, ADAM_B2, ADAM_EPS, ADAM_WD, ADAM_STEP = 0.001, 0.9, 0.999, 1e-08, 0.01, 10

N_DEV = 8
VMEM_LIMIT_BYTES = 48 * 1024 * 1024
ATT_TILE = 512
ATT_TILE_BWD = 512
PAIRS_PER_STEP_FWD = 4
PAIRS_PER_STEP_BWD = 2

NN = (((1,), (0,)), ((), ()))
NT = (((1,), (1,)), ((), ()))
TN = (((0,), (0,)), ((), ()))

BIG = ['ev_w_in', 'ev_w_uq', 'ev_w_ukv', 'ev_w_out', 'od_w_in', 'od_w_out', 'w_up', 'w_down',
       'ple_w_proj', 'ple_w_gate']
BIG_AXIS = {'ev_w_in': 2, 'ev_w_uq': 2, 'ev_w_ukv': 2, 'ev_w_out': 1, 'od_w_in': 2, 'od_w_out': 1,
            'w_up': 2, 'w_down': 1, 'ple_w_proj': 2, 'ple_w_gate': 1}
SMALL = ['rel_bias', 'ev_q_norm', 'ev_kv_norm', 'ev_sinks', 'od_b_f', 'ln1_g', 'ln1_b', 'ln2_g', 'ln2_b',
         'ple_b_gate']
WEIGHTS = ['rel_bias', 'ev_w_in', 'ev_q_norm', 'ev_w_uq', 'ev_kv_norm', 'ev_w_ukv', 'ev_sinks', 'ev_w_out',
           'od_w_in', 'od_b_f', 'od_w_out', 'ln1_g', 'ln1_b', 'w_up', 'w_down', 'ln2_g', 'ln2_b',
           'ple_w_proj', 'ple_w_gate', 'ple_b_gate']


def _cparams(*sem):
    return pltpu.CompilerParams(dimension_semantics=sem, vmem_limit_bytes=VMEM_LIMIT_BYTES)


def _pick(n, cands):
    for c in cands:
        if n % c == 0:
            return c
    return n


MM_STEP_BYTES = 10 * 1024 * 1024
MM_OUT_BYTES = 8 * 1024 * 1024
MM_CHUNK = 512


def _mm(a, b, *, trans_b=False, extras=(), epilogue=None, row_epilogue=None, out_dtypes=(F32,), out_widths=None,
        n_sums=0, comm=(), name):
    a_parts = tuple(a) if isinstance(a, (tuple, list)) else (a,)
    n_a = len(a_parts)
    M = a_parts[0].shape[0]
    k_offs = [sum(p.shape[1] for p in a_parts[:i]) for i in range(n_a + 1)]
    slot_w = b.shape[2] if b.ndim == 3 else None
    if slot_w is None:
        N = b.shape[0] if trans_b else b.shape[1]
    else:
        assert n_a == 1 and slot_w % LANES == 0
        N = b.shape[1] if trans_b else b.shape[0] * slot_w
    n_ex, n_out = len(extras), len(out_dtypes)
    n_rows_out = n_out - n_sums
    out_widths = (N,) * n_out if out_widths is None else out_widths
    row_bytes = sum(p.shape[1] * p.dtype.itemsize for p in a_parts) + (sum(w * jnp.dtype(d).itemsize
                                            for w, d in zip(out_widths[:n_rows_out], out_dtypes))
                                        + sum(e.shape[1] * e.dtype.itemsize for e in extras if e.shape[0] == M)
                                        + (4 * N if row_epilogue is not None else 0))
    tm = next((c for c in (1024, 512, 256) if M % c == 0 and c * row_bytes <= MM_STEP_BYTES), 128)
    nc = _pick(N, (MM_CHUNK, 384, 256, 128)) if slot_w is None or trans_b else slot_w
    n_c, kinds = len(comm), [k for k, _ in comm]
    n_scr = 1 if row_epilogue is not None else 0

    def body(*refs):
        a_refs, refs = refs[:n_a], refs[n_a - 1:]
        c_in = refs[2 + n_ex:2 + n_ex + n_c]
        c_out = refs[2 + n_ex + n_c + n_out:2 + n_ex + 2 * n_c + n_out]
        sems = refs[2 + n_ex + 2 * n_c + n_out + n_scr:]
        refs = refs[:2 + n_ex] + refs[2 + n_ex + n_c:2 + n_ex + n_c + n_out] \
            + refs[2 + n_ex + 2 * n_c + n_out:2 + n_ex + 2 * n_c + n_out + n_scr]
        if n_c:
            place = _mesh_place()
            step = pl.program_id(0)

            @pl.when(step == 0)
            def _():
                _comm_start(kinds, c_in, c_out, sems, place)

        b_ref = refs[1]
        ex = refs[2:2 + n_ex]
        outs = refs[2 + n_ex:2 + n_ex + n_out]
        for n0 in range(0, N, nc):
            cols = slice(n0, n0 + nc)
            avs = [r[...].astype(BF16) for r in a_refs]
            acc = None
            if slot_w is None:
                terms = [(av, b_ref[cols, k0:k1] if trans_b else b_ref[k0:k1, cols])
                         for av, k0, k1 in zip(avs, k_offs[:-1], k_offs[1:])]
            elif trans_b:
                terms = [(avs[0][:, sl * slot_w:(sl + 1) * slot_w], b_ref[sl, cols, :]) for sl in range(b.shape[0])]
            else:
                terms = [(avs[0], b_ref[n0 // slot_w])]
            for av, bv in terms:
                part = lax.dot_general(av, bv.astype(BF16), NT if trans_b else NN, preferred_element_type=F32)
                acc = part if acc is None else acc + part
            if row_epilogue is not None:
                refs[-1][:, cols] = acc
                continue
            res = epilogue(acc, *[e[:, cols] for e in ex]) if epilogue is not None else (acc,)
            for o, r in zip(outs, res):
                o[:, cols] = r.astype(o.dtype)
        if row_epilogue is not None:
            res = row_epilogue(refs[-1][...], *[e[...] for e in ex])
            for o, r in zip(outs[:n_rows_out], res):
                o[...] = r.astype(o.dtype)
            if n_sums:
                @pl.when(pl.program_id(0) == 0)
                def _():
                    for o in outs[n_rows_out:]:
                        o[...] = jnp.zeros_like(o)

                for o, r in zip(outs[n_rows_out:], res[n_rows_out:]):
                    o[...] += r
        if n_c:
            @pl.when(step == M // tm - 1)
            def _():
                _comm_wait(kinds, c_in, c_out, sems, place)

    in_specs = [pl.BlockSpec((tm, p.shape[1]), lambda i: (i, 0)) for p in a_parts]
    in_specs.append(pl.BlockSpec(b.shape, lambda i: (0,) * b.ndim))
    for e in extras:
        if e.shape[0] == M:
            in_specs.append(pl.BlockSpec((tm, e.shape[1]), lambda i: (i, 0)))
        elif e.shape == (1, N):
            in_specs.append(pl.BlockSpec((1, N), lambda i: (0, 0)))
        else:
            raise ValueError(f"extra operand of shape {e.shape} for a ({M}, {N}) result")
    res = pl.pallas_call(
        body, name=name, grid=(M // tm,), in_specs=in_specs + [HBM_SPEC] * n_c,
        out_specs=[pl.BlockSpec((tm, w), lambda i: (i, 0)) for w in out_widths[:n_rows_out]]
        + [pl.BlockSpec((1, w), lambda i: (0, 0)) for w in out_widths[n_rows_out:]] + [HBM_SPEC] * n_c,
        out_shape=[jax.ShapeDtypeStruct((M, w), d) for w, d in zip(out_widths[:n_rows_out], out_dtypes)]
        + [jax.ShapeDtypeStruct((1, w), d) for w, d in zip(out_widths[n_rows_out:], out_dtypes[n_rows_out:])]
        + _comm_out_shapes(comm),
        scratch_shapes=([pltpu.VMEM((tm, N), F32)] if row_epilogue is not None else [])
        + (_comm_scratch(comm) if n_c else []),
        compiler_params=_cparams("arbitrary" if n_sums or n_c else "parallel"),
    )(*a_parts, b, *extras, *[c for _, c in comm])
    main = res[0] if n_out == 1 else tuple(res[:n_out])
    return (main, list(res[n_out:])) if n_c else main


def _mm_tn(a, b, *, slot_width=None, name):
    T, K = a.shape
    N = b.shape[1]
    bk, bn = K, N
    while bk * bn * 4 > MM_OUT_BYTES:
        if bn >= bk and bn % (2 * LANES) == 0:
            bn //= 2
        else:
            bk //= 2
    tt = _pick(T, (1024, 512, 256))
    ck, cn = _pick(bk, (MM_CHUNK, 384, 256, 128)), _pick(bn, (MM_CHUNK, 384, 256, 128))

    def body(a_ref, b_ref, o_ref, acc_ref):
        t = pl.program_id(2)

        @pl.when(t == 0)
        def _():
            acc_ref[...] = jnp.zeros_like(acc_ref)

        for r0 in range(0, bk, ck):
            av = a_ref[:, r0:r0 + ck].astype(BF16)
            for c0 in range(0, bn, cn):
                acc_ref[r0:r0 + ck, c0:c0 + cn] += lax.dot_general(
                    av, b_ref[:, c0:c0 + cn].astype(BF16), TN, preferred_element_type=F32)

        @pl.when(t == T // tt - 1)
        def _():
            if slot_width is None:
                o_ref[...] = acc_ref[...].astype(o_ref.dtype)
            else:
                for slot in range(bn // slot_width):
                    o_ref[slot] = acc_ref[:, slot * slot_width:(slot + 1) * slot_width].astype(o_ref.dtype)

    if slot_width is None:
        out_spec, out_shape = pl.BlockSpec((bk, bn), lambda i, j, t: (i, j)), (K, N)
    else:
        assert bn % slot_width == 0 and slot_width % LANES == 0
        out_spec = pl.BlockSpec((bn // slot_width, bk, slot_width), lambda i, j, t: (j, i, 0))
        out_shape = (N // slot_width, K, slot_width)
    return pl.pallas_call(
        body, name=name, grid=(K // bk, N // bn, T // tt),
        in_specs=[pl.BlockSpec((tt, bk), lambda i, j, t: (t, i)), pl.BlockSpec((tt, bn), lambda i, j, t: (t, j))],
        out_specs=out_spec, out_shape=jax.ShapeDtypeStruct(out_shape, BF16),
        scratch_shapes=[pltpu.VMEM((bk, bn), F32)],
        compiler_params=_cparams("parallel", "parallel", "arbitrary"),
    )(a, b)


ROW_TILE = 1024


def _row_spec(cols, col_block=0):
    return pl.BlockSpec((ROW_TILE, cols), lambda i: (i, col_block))


def _tab_spec(cols, period):
    return pl.BlockSpec((ROW_TILE, cols), lambda i: (i % period, 0))


def _full_spec(shape):
    return pl.BlockSpec(shape, lambda i: (0,) * len(shape))


def _mm_ln(a, w, x, g, b, *, comm=(), name):
    def ln_rows(m, xv, gv, bv):
        z = DN_ALPHA * xv + m
        mu = jnp.mean(z, -1, keepdims=True)
        zc = z - mu
        r = lax.rsqrt(jnp.mean(zc * zc, -1, keepdims=True) + NORM_EPS)
        xh = zc * r
        y = xh * gv + bv
        return y, y, xh, jnp.broadcast_to(r, (r.shape[0], LANES))

    D = w.shape[1]
    return _mm(a, w, extras=(x, g, b), row_epilogue=ln_rows, out_dtypes=(F32, BF16, F32, F32),
               out_widths=(D, D, D, LANES), comm=comm, name=name)


def _mm_ln_bwd(a, w, resid, resid_scale, xh, r, g, *, name):
    def ln_bwd_rows(acc, rv, xhv, rstd, gv):
        dyv = acc + resid_scale * rv
        dyg = dyv * gv
        c1 = jnp.mean(dyg, -1, keepdims=True)
        c2 = jnp.mean(dyg * xhv, -1, keepdims=True)
        dz = _widen(rstd, dyv.shape[-1]) * (dyg - c1 - xhv * c2)
        return dz, dz, jnp.sum(dyv * xhv, 0, keepdims=True), jnp.sum(dyv, 0, keepdims=True)

    D = resid.shape[1]
    return _mm(a, w, trans_b=True, extras=(resid, xh, r, g), row_epilogue=ln_bwd_rows,
               out_dtypes=(F32, BF16, F32, F32), out_widths=(D, D, D, D), n_sums=2, name=name)


def _loss_grad(y, target, *, name):
    T, D = y.shape

    def body(y_ref, t_ref, dy_ref, sq_ref):
        err = y_ref[...] - t_ref[...]
        dy_ref[...] = err / D

        @pl.when(pl.program_id(0) == 0)
        def _():
            sq_ref[...] = jnp.zeros_like(sq_ref)

        sq_ref[...] += jnp.sum(err * err, 0, keepdims=True)

    return pl.pallas_call(
        body, name=name, grid=(T // ROW_TILE,),
        in_specs=[_row_spec(D), _row_spec(D)],
        out_specs=[_row_spec(D), _full_spec((1, D))],
        out_shape=[jax.ShapeDtypeStruct((T, D), F32), jax.ShapeDtypeStruct((1, D), F32)],
        compiler_params=_cparams("arbitrary"),
    )(y, target)


def _ple_bwd_elem(dx3, g, e, *, name):
    T, D = dx3.shape

    def body(dx_ref, g_ref, e_ref, de_ref, dz_ref, db_ref):
        dx, gv = dx_ref[...], g_ref[...]
        de_ref[...] = (dx * gv).astype(BF16)
        dz = dx * e_ref[...] * gv * (1.0 - gv)
        dz_ref[...] = dz.astype(BF16)

        @pl.when(pl.program_id(0) == 0)
        def _():
            db_ref[...] = jnp.zeros_like(db_ref)

        db_ref[...] += jnp.sum(dz, 0, keepdims=True)

    return pl.pallas_call(
        body, name=name, grid=(T // ROW_TILE,),
        in_specs=[_row_spec(D), _row_spec(D), _row_spec(D)],
        out_specs=[_row_spec(D), _row_spec(D), _full_spec((1, D))],
        out_shape=[jax.ShapeDtypeStruct((T, D), BF16), jax.ShapeDtypeStruct((T, D), BF16),
                   jax.ShapeDtypeStruct((1, D), F32)],
        compiler_params=_cparams("arbitrary"),
    )(dx3, g, e)


def _rotate(xv, a, bm, bp, sign):
    half = MLA_ROPE // 2
    width = xv.shape[-1]
    a, bm, bp = (_widen(t, width) for t in (a, bm, bp))
    return xv * a + sign * (pltpu.roll(xv, width - half, 1) * bm + pltpu.roll(xv, half, 1) * bp)


def _rope(x, tabs, seq, *, sign, name):
    T, width = x.shape

    def body(x_ref, a_ref, bm_ref, bp_ref, o_ref):
        o_ref[...] = _rotate(x_ref[...], a_ref[...], bm_ref[...], bp_ref[...], sign).astype(BF16)

    return pl.pallas_call(
        body, name=name, grid=(T // ROW_TILE,),
        in_specs=[_row_spec(width)] + [_tab_spec(LANES, seq // ROW_TILE)] * 3,
        out_specs=_row_spec(width),
        out_shape=jax.ShapeDtypeStruct((T, width), BF16),
        compiler_params=_cparams("parallel"),
    )(x, *tabs)


def _mla_keys(knp, h, k_tabs, seq, *, name):
    T = knp.shape[0]

    def body(k_ref, h_ref, a_ref, bm_ref, bp_ref, o_ref):
        kr = _rotate(h_ref[...], a_ref[...], bm_ref[...], bp_ref[...], 1.0)
        for hd in range(MLA_HEADS):
            cols = slice(hd * LANES, (hd + 1) * LANES)
            o_ref[:, cols] = (k_ref[:, cols].astype(F32) + kr).astype(BF16)

    return pl.pallas_call(
        body, name=name, grid=(T // ROW_TILE,),
        in_specs=[_row_spec(MLA_HEADS * LANES), _row_spec(LANES, EV_KR[0] // LANES)]
        + [_tab_spec(LANES, seq // ROW_TILE)] * 3,
        out_specs=_row_spec(MLA_HEADS * LANES),
        out_shape=jax.ShapeDtypeStruct((T, MLA_HEADS * LANES), BF16),
        compiler_params=_cparams("parallel"),
    )(knp, h, *k_tabs)


def _mla_rope_key_grad(dk, k_tabs, seq, *, name):
    T = dk.shape[0]

    def body(dk_ref, a_ref, bm_ref, bp_ref, o_ref):
        tot = dk_ref[:, 0:LANES]
        for hd in range(1, MLA_HEADS):
            tot = tot + dk_ref[:, hd * LANES:(hd + 1) * LANES]
        o_ref[...] = _rotate(tot, a_ref[...], bm_ref[...], bp_ref[...], -1.0).astype(BF16)

    return pl.pallas_call(
        body, name=name, grid=(T // ROW_TILE,),
        in_specs=[_row_spec(MLA_HEADS * LANES)] + [_tab_spec(LANES, seq // ROW_TILE)] * 3,
        out_specs=_row_spec(LANES),
        out_shape=jax.ShapeDtypeStruct((T, LANES), BF16),
        compiler_params=_cparams("parallel"),
    )(dk, *k_tabs)


def _even_norms(h, gq, gkv, *, name):
    T = h.shape[0]

    def body(h_ref, gq_ref, gkv_ref, cq_ref, ckv_ref, rq_ref, rkv_ref):
        cq = h_ref[:, EV_CQ[0]:EV_CQ[1]]
        rq = lax.rsqrt(jnp.mean(cq * cq, -1, keepdims=True) + NORM_EPS)
        cq_ref[...] = (cq * rq * gq_ref[...]).astype(BF16)
        rq_ref[...] = jnp.broadcast_to(rq, rq_ref.shape)
        ckv = h_ref[:, EV_CKV[0]:EV_CKV[1]]
        rkv = lax.rsqrt(jnp.mean(ckv * ckv, -1, keepdims=True) + NORM_EPS)
        ckv_ref[...] = (ckv * rkv * gkv_ref[...]).astype(BF16)
        rkv_ref[...] = jnp.broadcast_to(rkv, rkv_ref.shape)

    return pl.pallas_call(
        body, name=name, grid=(T // ROW_TILE,),
        in_specs=[_row_spec(EVEN_IN_PAD), _full_spec((1, MLA_Q_LORA)), _full_spec((1, MLA_KV_LORA))],
        out_specs=[_row_spec(MLA_Q_LORA), _row_spec(MLA_KV_LORA), _row_spec(LANES), _row_spec(LANES)],
        out_shape=[jax.ShapeDtypeStruct((T, MLA_Q_LORA), BF16), jax.ShapeDtypeStruct((T, MLA_KV_LORA), BF16),
                   jax.ShapeDtypeStruct((T, LANES), F32), jax.ShapeDtypeStruct((T, LANES), F32)],
        compiler_params=_cparams("parallel"),
    )(h, gq, gkv)


def _even_in_bwd(h, rq, rkv, gq, gkv, dcqn, dckvn, dqs, dks, dvs, dkr, *, name):
    T = h.shape[0]

    def rms_bwd(c, r, g, dy):
        r = _widen(r, c.shape[-1])
        xr = c * r
        dyg = dy * g
        return r * (dyg - xr * jnp.mean(dyg * xr, -1, keepdims=True)), jnp.sum(dy * xr, 0, keepdims=True)

    def body(h_ref, rq_ref, rkv_ref, gq_ref, gkv_ref, dcq_ref, dckv_ref, dqs_ref, dks_ref, dvs_ref, dkr_ref,
             dh_ref, dgq_ref, dgkv_ref):
        @pl.when(pl.program_id(0) == 0)
        def _():
            dgq_ref[...] = jnp.zeros_like(dgq_ref)
            dgkv_ref[...] = jnp.zeros_like(dgkv_ref)

        dcq, dgq = rms_bwd(h_ref[:, EV_CQ[0]:EV_CQ[1]], rq_ref[...], gq_ref[...], dcq_ref[...])
        dckv, dgkv = rms_bwd(h_ref[:, EV_CKV[0]:EV_CKV[1]], rkv_ref[...], gkv_ref[...], dckv_ref[...])
        dgq_ref[...] += dgq
        dgkv_ref[...] += dgkv
        dh_ref[:, EV_QS[0]:EV_QS[1]] = dqs_ref[...]
        dh_ref[:, EV_CQ[0]:EV_CQ[1]] = dcq.astype(BF16)
        dh_ref[:, EV_CKV[0]:EV_CKV[1]] = dckv.astype(BF16)
        dh_ref[:, EV_KS[0]:EV_KS[1]] = dks_ref[...]
        dh_ref[:, EV_VS[0]:EV_VS[1]] = dvs_ref[...]
        dh_ref[:, EV_KR[0]:EV_KR[1]] = dkr_ref[...]

    return pl.pallas_call(
        body, name=name, grid=(T // ROW_TILE,),
        in_specs=[_row_spec(EVEN_IN_PAD), _row_spec(LANES), _row_spec(LANES), _full_spec((1, MLA_Q_LORA)),
                  _full_spec((1, MLA_KV_LORA)), _row_spec(MLA_Q_LORA), _row_spec(MLA_KV_LORA),
                  _row_spec(SWA_HEADS * HEAD_DIM), _row_spec(LANES), _row_spec(LANES), _row_spec(LANES)],
        out_specs=[_row_spec(EVEN_IN_PAD), _full_spec((1, MLA_Q_LORA)), _full_spec((1, MLA_KV_LORA))],
        out_shape=[jax.ShapeDtypeStruct((T, EVEN_IN_PAD), BF16), jax.ShapeDtypeStruct((1, MLA_Q_LORA), F32),
                   jax.ShapeDtypeStruct((1, MLA_KV_LORA), F32)],
        compiler_params=_cparams("arbitrary"),
    )(h, rq, rkv, gq, gkv, dcqn, dckvn, dqs, dks, dvs, dkr)


def _fox_decay_fwd(f3, bf, *, name):
    B, S, _ = f3.shape

    def body(f_ref, b_ref, csh_ref, chs_ref):
        x = f_ref[...] + b_ref[...]
        c = jnp.minimum(x, 0.0) - jnp.log1p(jnp.exp(-jnp.abs(x)))
        row = lax.broadcasted_iota(jnp.int32, (S, LANES), 0)
        k = 1
        while k < S:
            c = c + jnp.where(row >= k, pltpu.roll(c, k, 0), 0.0)
            k *= 2
        csh_ref[...] = c
        chs_ref[...] = c.T

    return pl.pallas_call(
        body, name=name, grid=(B,),
        in_specs=[pl.BlockSpec((None, S, LANES), lambda b: (b, 0, 0)), pl.BlockSpec((1, LANES), lambda b: (0, 0))],
        out_specs=[pl.BlockSpec((None, S, LANES), lambda b: (b, 0, 0)),
                   pl.BlockSpec((None, LANES, S), lambda b: (b, 0, 0))],
        out_shape=[jax.ShapeDtypeStruct((B, S, LANES), F32), jax.ShapeDtypeStruct((B, LANES, S), F32)],
        compiler_params=_cparams("parallel"),
    )(f3, bf)


def _fox_decay_bwd(dc_hs, f3, bf, *, name):
    B, S, _ = f3.shape

    def body(dc_ref, f_ref, b_ref, df_ref, db_ref):
        g = dc_ref[...].T
        row = lax.broadcasted_iota(jnp.int32, (S, LANES), 0)
        k = 1
        while k < S:
            g = g + jnp.where(row < S - k, pltpu.roll(g, S - k, 0), 0.0)
            k *= 2
        x = f_ref[...] + b_ref[...]
        df = g * (1.0 / (1.0 + jnp.exp(x)))
        df_ref[...] = df.astype(BF16)

        @pl.when(pl.program_id(0) == 0)
        def _():
            db_ref[...] = jnp.zeros_like(db_ref)

        db_ref[...] += jnp.sum(df, 0, keepdims=True)

    return pl.pallas_call(
        body, name=name, grid=(B,),
        in_specs=[pl.BlockSpec((None, LANES, S), lambda b: (b, 0, 0)),
                  pl.BlockSpec((None, S, LANES), lambda b: (b, 0, 0)), pl.BlockSpec((1, LANES), lambda b: (0, 0))],
        out_specs=[pl.BlockSpec((None, S, LANES), lambda b: (b, 0, 0)), pl.BlockSpec((1, LANES), lambda b: (0, 0))],
        out_shape=[jax.ShapeDtypeStruct((B, S, LANES), BF16), jax.ShapeDtypeStruct((1, LANES), F32)],
        compiler_params=_cparams("arbitrary"),
    )(dc_hs, f3, bf)


def _head_column(block, h):
    lane = lax.broadcasted_iota(jnp.int32, block.shape, 1)
    return jnp.sum(jnp.where(lane == h, block, 0.0), axis=-1, keepdims=True)


def _causal_mask(s):
    r = lax.broadcasted_iota(jnp.int32, s.shape, 0)
    c = lax.broadcasted_iota(jnp.int32, s.shape, 1)
    return jnp.where(c <= r, s, NEG_INF)


def _low_half(shape):
    return (lax.broadcasted_iota(jnp.int32, shape, 1) % LANES) < HEAD_DIM


def _widen(x, cols):
    return jnp.concatenate([x] * (cols // LANES), axis=1)


def _both_halves(x, lo):
    r = pltpu.roll(x, HEAD_DIM, 1)
    return jnp.where(lo, x, r), jnp.where(lo, r, x)


MESH_ID = pl.DeviceIdType.MESH
HBM_SPEC = pl.BlockSpec(memory_space=pltpu.HBM)
VMEM_SPEC = pl.BlockSpec(memory_space=pltpu.VMEM)


def _mesh_place():
    x, y, c = lax.axis_index("x"), lax.axis_index("y"), lax.axis_index("c")
    return x, y, c, 4 * x + 2 * y + c


def _peers(x, y, c):
    out = []
    for mask in range(1, N_DEV):
        dx, dy, dc = (mask >> 2) & 1, (mask >> 1) & 1, mask & 1
        px, py, pc = (1 - x if dx else x), (1 - y if dy else y), (1 - c if dc else c)
        out.append(((px, py, pc), 4 * px + 2 * py + pc))
    return out


def _comm_out_shapes(comm):
    return [jax.ShapeDtypeStruct(a.shape if kind == "scatter" else (N_DEV,) + a.shape[1:], a.dtype) for kind, a in comm]


def _comm_scratch(comm):
    n = len(comm)
    return [pltpu.SemaphoreType.DMA((n, 7)), pltpu.SemaphoreType.DMA((n, 7)), pltpu.SemaphoreType.DMA((n,))]


def _comm_copies(kinds, in_refs, out_refs, sems, place):
    send_sems, recv_sems, local_sems = sems
    x, y, c, me = place
    local, remote = [], []
    for w, kind in enumerate(kinds):
        mine = in_refs[w].at[me] if kind == "scatter" else in_refs[w].at[kind[1]]
        local.append(pltpu.make_async_copy(mine, out_refs[w].at[me], local_sems.at[w]))
        for k, (peer, peer_idx) in enumerate(_peers(x, y, c)):
            remote.append(pltpu.make_async_remote_copy(
                src_ref=in_refs[w].at[peer_idx] if kind == "scatter" else mine, dst_ref=out_refs[w].at[me],
                send_sem=send_sems.at[w, k], recv_sem=recv_sems.at[w, k], device_id=peer, device_id_type=MESH_ID))
    return local, remote


def _comm_start(kinds, in_refs, out_refs, sems, place):
    local, remote = _comm_copies(kinds, in_refs, out_refs, sems, place)
    for cp in local + remote:
        cp.start()


def _comm_wait(kinds, in_refs, out_refs, sems, place):
    local, remote = _comm_copies(kinds, in_refs, out_refs, sems, place)
    for cp in remote:
        cp.wait_recv()
    for cp in remote:
        cp.wait_send()
    for cp in local:
        cp.wait()


def _exchange(comm, *, name):
    n = len(comm)
    kinds = [k for k, _ in comm]

    def body(*refs):
        place = _mesh_place()
        _comm_start(kinds, refs[:n], refs[n:2 * n], refs[2 * n:], place)
        _comm_wait(kinds, refs[:n], refs[n:2 * n], refs[2 * n:], place)

    return pl.pallas_call(
        body, name=name, out_shape=_comm_out_shapes(comm), in_specs=[HBM_SPEC] * n, out_specs=[HBM_SPEC] * n,
        scratch_shapes=_comm_scratch(comm),
    )(*[a for _, a in comm])


def _flash_fwd(qa, ka, va, *, q_blk0, k_blk0, v_blk0, W, n_pairs, B, S, scale, csh=None, crow=None, comm=(), name):
    t = ATT_TILE
    nq = S // t
    P = PAIRS_PER_STEP_FWD
    decay = csh is not None
    split = W == LANES
    assert n_pairs % P == 0 and q_blk0 % P == 0 and k_blk0 % P == 0 and v_blk0 % P == 0
    n_c, kinds = len(comm), [k for k, _ in comm]
    n_in = 5 if decay else 3
    fold_scale = math.log2(scale).is_integer()
    n_steps = (B, n_pairs // P, nq)

    def body(*refs):
        c_in, c_out = refs[n_in:n_in + n_c], refs[n_in + n_c + 2:n_in + 2 * n_c + 2]
        sems = refs[n_in + 2 * n_c + 4:]
        refs = refs[:n_in] + refs[n_in + n_c:n_in + n_c + 2] + refs[n_in + 2 * n_c + 2:n_in + 2 * n_c + 4]
        if decay:
            q_ref, k_ref, v_ref, csh_ref, crow_ref, o_ref, lse_ref, m_s, acc_s = refs
        else:
            q_ref, k_ref, v_ref, o_ref, lse_ref, m_s, acc_s = refs
        g, i = pl.program_id(1), pl.program_id(2)
        if n_c:
            place = _mesh_place()
            ids = [pl.program_id(ax) for ax in range(3)]

            @pl.when((ids[0] == 0) & (ids[1] == 0) & (ids[2] == 0))
            def _():
                _comm_start(kinds, c_in, c_out, sems, place)

        lo = _low_half((t, LANES))
        qv = q_ref[...]
        qh = []
        for pr in range(P):
            qp = qv[:, pr * W:(pr + 1) * W]
            qh += [jnp.where(lo, qp, jnp.zeros_like(qp)), jnp.where(lo, jnp.zeros_like(qp), qp)] if split \
                else [qp[:, :LANES], qp[:, LANES:]]
        if fold_scale:
            qh = [x * scale for x in qh]
        if decay:
            cq = [jnp.broadcast_to(_head_column(csh_ref[...], 2 * P * g + hd), (t, LANES)) for hd in range(2 * P)]
        m_s[...] = jnp.full(m_s.shape, NEG_INF, F32)
        acc_s[...] = jnp.zeros(acc_s.shape, F32)

        def step(j, masked):
            rows = pl.ds(pl.multiple_of(j * t, t), t)
            kb, vb = k_ref[rows, :], v_ref[rows, :]
            for pr in range(P):
                kp, vp = kb[:, pr * W:(pr + 1) * W], vb[:, pr * LANES:(pr + 1) * LANES]
                ones = jnp.ones_like(vp)
                vaug = [jnp.where(lo, vp, ones), jnp.where(lo, ones, vp)]
                for half in range(2):
                    hd = 2 * pr + half
                    kh = kp if split else kp[:, half * LANES:(half + 1) * LANES]
                    s = lax.dot_general(qh[hd], kh, NT, preferred_element_type=F32)
                    if not fold_scale:
                        s = s * scale
                    if decay:
                        s = s + _widen(cq[hd], t) - crow_ref[hd, j]
                    if masked:
                        s = _causal_mask(s)
                    m_prev = m_s[hd]
                    m_new = jnp.maximum(m_prev, jnp.max(s, -1, keepdims=True))
                    p = jnp.exp(s - _widen(m_new, t))
                    acc_s[hd] = jnp.exp(m_prev - m_new) * acc_s[hd] + lax.dot_general(
                        p.astype(BF16), vaug[half], NN, preferred_element_type=F32)
                    m_s[hd] = m_new

        def loop_body(j, carry):
            step(j, False)
            return carry

        lax.fori_loop(0, i, loop_body, 0)
        step(i, True)
        for pr in range(P):
            acc0, acc1 = acc_s[2 * pr], acc_s[2 * pr + 1]
            _, l0 = _both_halves(acc0, lo)
            l1, _ = _both_halves(acc1, lo)
            cols = slice(pr * LANES, (pr + 1) * LANES)
            o_ref[:, cols] = jnp.where(lo, acc0 / l0, acc1 / l1).astype(BF16)
            lse_ref[:, cols] = jnp.where(lo, m_s[2 * pr] + jnp.log(l0), m_s[2 * pr + 1] + jnp.log(l1))
        if n_c:
            @pl.when((ids[0] == n_steps[0] - 1) & (ids[1] == n_steps[1] - 1) & (ids[2] == n_steps[2] - 1))
            def _():
                _comm_wait(kinds, c_in, c_out, sems, place)

    in_specs = [pl.BlockSpec((t, P * W), lambda b, g, i: (b * nq + i, q_blk0 // P + g)),
                pl.BlockSpec((S, P * W), lambda b, g, i: (b, k_blk0 // P + g)),
                pl.BlockSpec((S, P * LANES), lambda b, g, i: (b, v_blk0 // P + g))]
    args = [qa, ka, va]
    if decay:
        in_specs += [pl.BlockSpec((None, t, LANES), lambda b, g, i: (b, i, 0)),
                     pl.BlockSpec((None, 2 * P, nq, 1, t), lambda b, g, i: (b, g, 0, 0, 0))]
        args += [csh, crow]
    out_spec = pl.BlockSpec((t, P * LANES), lambda b, g, i: (b * nq + i, g))
    res = pl.pallas_call(
        body, name=name, grid=n_steps, in_specs=in_specs + [HBM_SPEC] * n_c,
        out_specs=[out_spec, out_spec] + [HBM_SPEC] * n_c,
        out_shape=[jax.ShapeDtypeStruct((B * S, n_pairs * LANES), BF16),
                   jax.ShapeDtypeStruct((B * S, n_pairs * LANES), F32)] + _comm_out_shapes(comm),
        scratch_shapes=[pltpu.VMEM((2 * P, t, LANES), F32), pltpu.VMEM((2 * P, t, LANES), F32)]
        + (_comm_scratch(comm) if n_c else []),
        compiler_params=_cparams(*(("arbitrary",) * 3 if n_c else ("parallel",) * 3)),
    )(*args, *[a for _, a in comm])
    return res[0], res[1], list(res[2:])


def _flash_bwd(qa, ka, va, oa, doa, lsea, *, q_blk0, k_blk0, v_blk0, do_blk0, W, n_pairs, B, S, scale, qk_dtype,
               csh=None, crow=None, comm=(), name):
    t = ATT_TILE_BWD
    nq = S // t
    P = PAIRS_PER_STEP_BWD
    decay = csh is not None
    if decay:
        crow = crow.reshape(B, 2 * n_pairs, nq, 1, t)
    split = W == LANES
    assert n_pairs % P == 0 and q_blk0 % P == 0 and k_blk0 % P == 0 and v_blk0 % P == 0 and do_blk0 % P == 0
    n_c, kinds = len(comm), [k for k, _ in comm]
    n_in, n_out, n_scr = (8, 5, 8) if decay else (6, 3, 5)
    n_steps = (B, n_pairs // P, nq)

    def body(*refs):
        c_in = refs[n_in:n_in + n_c]
        c_out = refs[n_in + n_c + n_out:n_in + 2 * n_c + n_out]
        sems = refs[n_in + 2 * n_c + n_out + n_scr:]
        refs = (refs[:n_in] + refs[n_in + n_c:n_in + n_c + n_out]
                + refs[n_in + 2 * n_c + n_out:n_in + 2 * n_c + n_out + n_scr])
        if n_c:
            place = _mesh_place()
            ids = [pl.program_id(ax) for ax in range(3)]

            @pl.when((ids[0] == 0) & (ids[1] == 0) & (ids[2] == 0))
            def _():
                _comm_start(kinds, c_in, c_out, sems, place)

        if decay:
            (q_ref, k_ref, v_ref, o_ref, do_ref, lse_ref, csh_ref, crow_ref, dq_ref, dk_ref, dv_ref, dck_ref, dcq_ref,
             dq_s, lse_s, delta_s, dk_s, dv_s, cq_s, dcq_s, dck_s) = refs
        else:
            (q_ref, k_ref, v_ref, o_ref, do_ref, lse_ref, dq_ref, dk_ref, dv_ref,
             dq_s, lse_s, delta_s, dk_s, dv_s) = refs
        g, j = pl.program_id(1), pl.program_id(2)
        lo = _low_half((t, LANES))

        @pl.when(j == 0)
        def _():
            lo_s = _low_half((S, LANES))
            dq_s[...] = jnp.zeros(dq_s.shape, F32)
            for pr in range(P):
                cols = slice(pr * LANES, (pr + 1) * LANES)
                lse_s[2 * pr], lse_s[2 * pr + 1] = _both_halves(lse_ref[:, cols], lo_s)
                dd = do_ref[:, cols].astype(F32) * o_ref[:, cols].astype(F32)
                delta_s[2 * pr] = jnp.broadcast_to(jnp.sum(jnp.where(lo_s, dd, 0.0), -1, keepdims=True), (S, LANES))
                delta_s[2 * pr + 1] = jnp.broadcast_to(jnp.sum(jnp.where(lo_s, 0.0, dd), -1, keepdims=True),
                                                       (S, LANES))
            if decay:
                for hd in range(2 * P):
                    cq_s[hd] = jnp.broadcast_to(_head_column(csh_ref[...], 2 * P * g + hd), (S, LANES))
                dcq_s[...] = jnp.zeros(dcq_s.shape, F32)

        kb, vb = k_ref[...], v_ref[...]
        kh, vh = [], []
        for pr in range(P):
            kp, vp = kb[:, pr * W:(pr + 1) * W], vb[:, pr * LANES:(pr + 1) * LANES]
            zk, zv = jnp.zeros_like(kp), jnp.zeros_like(vp)
            kh += [jnp.where(lo, kp, zk), jnp.where(lo, zk, kp)] if split else [kp[:, :LANES], kp[:, LANES:]]
            vh += [jnp.where(lo, vp, zv), jnp.where(lo, zv, vp)]
        dk_s[...] = jnp.zeros(dk_s.shape, F32)
        dv_s[...] = jnp.zeros(dv_s.shape, F32)
        if decay:
            dck_s[...] = jnp.zeros(dck_s.shape, F32)

        def step(i, masked):
            rows = pl.ds(pl.multiple_of(i * t, t), t)
            qi, doi = q_ref[rows, :], do_ref[rows, :]
            for pr in range(P):
                qp, dop = qi[:, pr * W:(pr + 1) * W], doi[:, pr * LANES:(pr + 1) * LANES]
                for half in range(2):
                    hd = 2 * pr + half
                    qx = qp if split else qp[:, half * LANES:(half + 1) * LANES]
                    s = lax.dot_general(qx, kh[hd], NT, preferred_element_type=F32) * scale
                    if decay:
                        s = s + _widen(cq_s[hd, rows, :], t) - crow_ref[hd, j]
                    if masked:
                        s = _causal_mask(s)
                    p = jnp.exp(s - _widen(lse_s[hd, rows, :], t))
                    dv_s[hd] += lax.dot_general(p.astype(BF16), dop, TN, preferred_element_type=F32)
                    dp = lax.dot_general(dop, vh[hd], NT, preferred_element_type=F32)
                    ds = p * (dp - _widen(delta_s[hd, rows, :], t))
                    dss = (ds * scale).astype(BF16)
                    dk_s[hd] += lax.dot_general(dss, qx, TN, preferred_element_type=F32)
                    dqc = lax.dot_general(dss, kh[hd], NN, preferred_element_type=F32)
                    if split:
                        dq_s[rows, pr * W:(pr + 1) * W] += dqc
                    else:
                        dq_s[rows, hd * LANES:(hd + 1) * LANES] += dqc
                    if decay:
                        dck_s[hd] -= jnp.sum(ds, 0, keepdims=True)
                        part = ds[:, :LANES]
                        for c in range(1, t // LANES):
                            part = part + ds[:, c * LANES:(c + 1) * LANES]
                        dcq_s[hd, rows, :] += part

        def loop_body(i, carry):
            step(i, False)
            return carry

        step(j, True)
        lax.fori_loop(j + 1, nq, loop_body, 0)
        for pr in range(P):
            if split:
                dk_ref[:, pr * W:(pr + 1) * W] = jnp.where(lo, dk_s[2 * pr], dk_s[2 * pr + 1]).astype(dk_ref.dtype)
            else:
                for half in range(2):
                    hd = 2 * pr + half
                    dk_ref[:, hd * LANES:(hd + 1) * LANES] = dk_s[hd].astype(dk_ref.dtype)
            dv_ref[:, pr * LANES:(pr + 1) * LANES] = jnp.where(lo, dv_s[2 * pr], dv_s[2 * pr + 1]).astype(BF16)
        if decay:
            dck_ref[...] = dck_s[...]

        @pl.when(j == nq - 1)
        def _():
            dq_ref[...] = dq_s[...].astype(dq_ref.dtype)
            if decay:
                for hd in range(2 * P):
                    dcq_ref[hd] = jnp.sum(dcq_s[hd].T, 0, keepdims=True)

        if n_c:
            @pl.when((ids[0] == n_steps[0] - 1) & (ids[1] == n_steps[1] - 1) & (ids[2] == n_steps[2] - 1))
            def _():
                _comm_wait(kinds, c_in, c_out, sems, place)

    full = lambda w, blk0: pl.BlockSpec((S, P * w), lambda b, g, j: (b, blk0 // P + g))
    blk = lambda w, blk0: pl.BlockSpec((t, P * w), lambda b, g, j: (b * nq + j, blk0 // P + g))
    in_specs = [full(W, q_blk0), blk(W, k_blk0), blk(LANES, v_blk0), full(LANES, 0), full(LANES, do_blk0),
                full(LANES, 0)]
    args = [qa, ka, va, oa, doa, lsea]
    T = B * S
    out_specs = [full(W, 0), blk(W, 0), blk(LANES, 0)]
    out_shape = [jax.ShapeDtypeStruct((T, n_pairs * W), qk_dtype), jax.ShapeDtypeStruct((T, n_pairs * W), qk_dtype),
                 jax.ShapeDtypeStruct((T, n_pairs * LANES), BF16)]
    per_head = lambda rows: pltpu.VMEM((2 * P, rows, LANES), F32)
    scratch = [pltpu.VMEM((S, P * W), F32), per_head(S), per_head(S), per_head(t), per_head(t)]
    if decay:
        in_specs += [pl.BlockSpec((None, S, LANES), lambda b, g, j: (b, 0, 0)),
                     pl.BlockSpec((None, 2 * P, nq, 1, t), lambda b, g, j: (b, g, 0, 0, 0))]
        args += [csh, crow]
        out_specs += [pl.BlockSpec((None, 2 * P, None, 1, t), lambda b, g, j: (b, g, j, 0, 0)),
                      pl.BlockSpec((None, 2 * P, 1, S), lambda b, g, j: (b, g, 0, 0))]
        out_shape += [jax.ShapeDtypeStruct((B, 2 * n_pairs, nq, 1, t), F32),
                      jax.ShapeDtypeStruct((B, 2 * n_pairs, 1, S), F32)]
        scratch += [per_head(S), per_head(S), pltpu.VMEM((2 * P, 1, t), F32)]
    res = pl.pallas_call(
        body, name=name, grid=n_steps, in_specs=in_specs + [HBM_SPEC] * n_c,
        out_specs=out_specs + [HBM_SPEC] * n_c, out_shape=out_shape + _comm_out_shapes(comm),
        scratch_shapes=scratch + (_comm_scratch(comm) if n_c else []),
        compiler_params=_cparams(*(("arbitrary",) * 3 if n_c else ("parallel", "parallel", "arbitrary"))),
    )(*args, *[a for _, a in comm])
    return tuple(res[:n_out]) + (list(res[n_out:]),)


def _swa_common(q_ref, kp_ref, ko_ref, vp_ref, vo_ref, n):
    Q = BLOCK_Q
    lo = _low_half((Q, LANES))
    lo2 = _low_half((2 * Q, LANES))
    kk = jnp.concatenate([kp_ref[...], ko_ref[...]], axis=0)
    vv = jnp.concatenate([vp_ref[...], vo_ref[...]], axis=0)
    kdup = [x.astype(BF16) for x in _both_halves(kk, lo2)]
    vdup = [x.astype(BF16) for x in _both_halves(vv, lo2)]
    a = lax.broadcasted_iota(jnp.int32, (SWA_GROUP * Q, 2 * Q), 0) % Q
    col = lax.broadcasted_iota(jnp.int32, (SWA_GROUP * Q, 2 * Q), 1)
    dist = a + Q - col
    valid = (dist >= 0) & (dist < SWA_WINDOW) & ((col >= Q) | (n > 0))
    qv = q_ref[...]
    qm = []
    for a_head in range(SWA_HEADS):
        qp = qv[:, (a_head // 2) * LANES:(a_head // 2 + 1) * LANES]
        keep = lo if a_head % 2 == 0 else jnp.logical_not(lo)
        qm.append(jnp.where(keep, qp, 0.0).astype(BF16))
    qs = [jnp.concatenate(qm[g * SWA_GROUP:(g + 1) * SWA_GROUP], axis=0) for g in range(SWA_KV_HEADS)]
    return lo, lo2, kdup, vdup, valid, qs


def _swa_group_logits(g, qs, kdup, valid, bias_ref):
    heads = slice(g * SWA_GROUP, (g + 1) * SWA_GROUP)
    s = lax.dot_general(qs[g], kdup[g], NT, preferred_element_type=F32) * (HEAD_DIM ** -0.5)
    s = s + bias_ref[heads].reshape(SWA_GROUP * BLOCK_Q, 2 * BLOCK_Q)
    return heads, jnp.where(valid, s, NEG_INF)


def _pair_halves(x, lo):
    Q = BLOCK_Q
    return [jnp.where(lo, x[2 * pr * Q:(2 * pr + 1) * Q], x[(2 * pr + 1) * Q:(2 * pr + 2) * Q])
            for pr in range(SWA_GROUP // 2)]


def _swa_in_specs(nb):
    Q = BLOCK_Q
    own = lambda blk: (lambda b, n: (b * nb + n, blk))
    prev = lambda blk: (lambda b, n: (b * nb + jnp.maximum(n - 1, 0), blk))
    kb, vb = EV_KS[0] // LANES, EV_VS[0] // LANES
    return [pl.BlockSpec((Q, SWA_HEADS * HEAD_DIM), own(0)), pl.BlockSpec((Q, LANES), prev(kb)),
            pl.BlockSpec((Q, LANES), own(kb)), pl.BlockSpec((Q, LANES), prev(vb)), pl.BlockSpec((Q, LANES), own(vb))]


def _swa_fwd(h, bias, sinkcol, *, B, S, comm=(), name):
    Q = BLOCK_Q
    nb = S // Q
    n_c, kinds = len(comm), [k for k, _ in comm]

    def body(*refs):
        c_in, c_out, sems = refs[7:7 + n_c], refs[9 + n_c:9 + 2 * n_c], refs[9 + 2 * n_c:]
        q_ref, kp_ref, ko_ref, vp_ref, vo_ref, bias_ref, sink_ref = refs[:7]
        o_ref, lse_ref = refs[7 + n_c:9 + n_c]
        if n_c:
            place = _mesh_place()
            ids = [pl.program_id(0), pl.program_id(1)]

            @pl.when((ids[0] == 0) & (ids[1] == 0))
            def _():
                _comm_start(kinds, c_in, c_out, sems, place)

        lo, lo2, kdup, vdup, valid, qs = _swa_common(q_ref, kp_ref, ko_ref, vp_ref, vo_ref, pl.program_id(1))
        pairs = []
        lo4 = _low_half((SWA_GROUP * Q, LANES))
        for g in range(SWA_KV_HEADS):
            heads, s = _swa_group_logits(g, qs, kdup, valid, bias_ref)
            sink = jnp.broadcast_to(sink_ref[heads].reshape(SWA_GROUP * Q, 1), (SWA_GROUP * Q, LANES))
            m = jnp.maximum(jnp.max(s, -1, keepdims=True), sink)
            p = jnp.exp(s - _widen(m, 2 * Q))
            vaug = jnp.where(lo2, vdup[g], jnp.ones_like(vdup[g]))
            pv = lax.dot_general(p.astype(BF16), vaug, NN, preferred_element_type=F32)
            rolled = pltpu.roll(pv, HEAD_DIM, 1)
            l = jnp.where(lo4, rolled, pv) + jnp.exp(sink - m)
            out = pv / l
            lse_g = m + jnp.log(l)
            for i in range(SWA_GROUP):
                a = g * SWA_GROUP + i
                lse_ref[:, a * LANES:(a + 1) * LANES] = lse_g[i * Q:(i + 1) * Q]
            shifted = pltpu.roll(out, HEAD_DIM, 1)
            pairs += [jnp.where(lo, out[2 * pr * Q:(2 * pr + 1) * Q], shifted[(2 * pr + 1) * Q:(2 * pr + 2) * Q])
                      for pr in range(SWA_GROUP // 2)]
        o_ref[...] = jnp.concatenate(pairs, axis=1).astype(BF16)
        if n_c:
            @pl.when((ids[0] == B - 1) & (ids[1] == nb - 1))
            def _():
                _comm_wait(kinds, c_in, c_out, sems, place)

    whole = lambda shape: pl.BlockSpec(shape, lambda b, n: (0,) * len(shape))
    res = pl.pallas_call(
        body, name=name, grid=(B, nb),
        in_specs=_swa_in_specs(nb) + [whole((SWA_HEADS, Q, 2 * Q)), whole((SWA_HEADS, Q, 1))] + [HBM_SPEC] * n_c,
        out_specs=[pl.BlockSpec((Q, SWA_HEADS * HEAD_DIM), lambda b, n: (b * nb + n, 0)),
                   pl.BlockSpec((Q, SWA_HEADS * LANES), lambda b, n: (b * nb + n, 0))] + [HBM_SPEC] * n_c,
        out_shape=[jax.ShapeDtypeStruct((B * S, SWA_HEADS * HEAD_DIM), BF16),
                   jax.ShapeDtypeStruct((B * S, SWA_HEADS * LANES), F32)] + _comm_out_shapes(comm),
        scratch_shapes=_comm_scratch(comm) if n_c else [],
        compiler_params=_cparams(*(("arbitrary",) * 2 if n_c else ("parallel",) * 2)),
    )(h, h, h, h, h, bias, sinkcol, *[a for _, a in comm])
    return res[0], res[1], list(res[2:])


def _swa_bwd(h, o, do, lse, bias, sinkcol, *, do_blk0, B, S, comm=(), name):
    Q = BLOCK_Q
    nb = S // Q
    scale = HEAD_DIM ** -0.5
    n_c, kinds = len(comm), [k for k, _ in comm]

    def body(*refs):
        c_in, c_out, sems = refs[10:10 + n_c], refs[17 + n_c:17 + 2 * n_c], refs[17 + 2 * n_c:]
        q_ref, kp_ref, ko_ref, vp_ref, vo_ref, o_ref, do_ref, lse_ref, bias_ref, sink_ref = refs[:10]
        dq_ref, dko_ref, dkp_ref, dvo_ref, dvp_ref, dbias_ref, dsink_ref = refs[10 + n_c:17 + n_c]
        ids = [pl.program_id(0), pl.program_id(1)]
        if n_c:
            place = _mesh_place()

        @pl.when((ids[0] == 0) & (ids[1] == 0))
        def _():
            dbias_ref[...] = jnp.zeros_like(dbias_ref)
            dsink_ref[...] = jnp.zeros_like(dsink_ref)
            if n_c:
                _comm_start(kinds, c_in, c_out, sems, place)

        lo, lo2, kdup, vdup, valid, qs = _swa_common(q_ref, kp_ref, ko_ref, vp_ref, vo_ref, pl.program_id(1))
        dkk, dvv, dq_pairs = [], [], []
        for g in range(SWA_KV_HEADS):
            heads, s = _swa_group_logits(g, qs, kdup, valid, bias_ref)
            lse_g = jnp.concatenate([lse_ref[:, a * LANES:(a + 1) * LANES]
                                     for a in range(g * SWA_GROUP, (g + 1) * SWA_GROUP)], axis=0)
            p = jnp.exp(s - _widen(lse_g, 2 * Q))
            do_g, o_g = [], []
            for i in range(SWA_GROUP):
                cols = slice((g * SWA_GROUP + i) // 2 * LANES, ((g * SWA_GROUP + i) // 2 + 1) * LANES)
                do_p = do_ref[:, cols]
                do_g.append(jnp.where(lo if i % 2 == 0 else jnp.logical_not(lo), do_p, jnp.zeros_like(do_p)))
                o_g.append(o_ref[:, cols])
            doh, oh = jnp.concatenate(do_g, axis=0), jnp.concatenate(o_g, axis=0)
            delta = jnp.sum(doh.astype(F32) * oh.astype(F32), -1, keepdims=True)
            dp = lax.dot_general(doh, vdup[g], NT, preferred_element_type=F32)
            ds = p * (dp - delta)
            dbias_ref[heads] += ds.reshape(SWA_GROUP, Q, 2 * Q)
            dsink_ref[heads] -= (jnp.exp(sink_ref[heads].reshape(SWA_GROUP * Q, 1) - lse_g[:, :1])
                                 * delta).reshape(SWA_GROUP, Q, 1)
            dss = (ds * scale).astype(BF16)
            dq_pairs += _pair_halves(lax.dot_general(dss, kdup[g], NN, preferred_element_type=F32), lo)
            dkk.append(lax.dot_general(dss, qs[g], TN, preferred_element_type=F32))
            dvv.append(lax.dot_general(p.astype(BF16), doh, TN, preferred_element_type=F32))
        dq_ref[...] = jnp.concatenate(dq_pairs, axis=1).astype(BF16)
        fold = lambda x: x + pltpu.roll(x, HEAD_DIM, 1)
        dk_blk = jnp.where(lo2, fold(dkk[0]), fold(dkk[1]))
        dv_blk = jnp.where(lo2, fold(dvv[0]), fold(dvv[1]))
        dkp_ref[...] = dk_blk[:Q]
        dko_ref[...] = dk_blk[Q:]
        dvp_ref[...] = dv_blk[:Q]
        dvo_ref[...] = dv_blk[Q:]
        if n_c:
            @pl.when((ids[0] == B - 1) & (ids[1] == nb - 1))
            def _():
                _comm_wait(kinds, c_in, c_out, sems, place)

    whole = lambda shape: pl.BlockSpec(shape, lambda b, n: (0,) * len(shape))
    wide = lambda blk: pl.BlockSpec((Q, SWA_HEADS * HEAD_DIM), lambda b, n: (b * nb + n, blk))
    narrow = pl.BlockSpec((Q, LANES), lambda b, n: (b * nb + n, 0))
    kv_shape = jax.ShapeDtypeStruct((B * S, LANES), F32)
    res = pl.pallas_call(
        body, name=name, grid=(B, nb),
        in_specs=_swa_in_specs(nb) + [wide(0), wide(do_blk0),
                                      pl.BlockSpec((Q, SWA_HEADS * LANES), lambda b, n: (b * nb + n, 0)),
                                      whole((SWA_HEADS, Q, 2 * Q)),
                                      whole((SWA_HEADS, Q, 1))] + [HBM_SPEC] * n_c,
        out_specs=[wide(0), narrow, narrow, narrow, narrow, whole((SWA_HEADS, Q, 2 * Q)), whole((SWA_HEADS, Q, 1))]
        + [HBM_SPEC] * n_c,
        out_shape=[jax.ShapeDtypeStruct((B * S, SWA_HEADS * HEAD_DIM), BF16), kv_shape, kv_shape, kv_shape, kv_shape,
                   jax.ShapeDtypeStruct((SWA_HEADS, Q, 2 * Q), F32), jax.ShapeDtypeStruct((SWA_HEADS, Q, 1), F32)]
        + _comm_out_shapes(comm),
        scratch_shapes=_comm_scratch(comm) if n_c else [],
        compiler_params=_cparams("arbitrary", "arbitrary"),
    )(h, h, h, h, h, o, do, lse, bias, sinkcol, *[a for _, a in comm])
    return tuple(res[:7]) + (list(res[7:]),)


def _bias_bucket_sum(dbias, bucket, *, name):
    def body(d_ref, b_ref, o_ref):
        dbv, bk = d_ref[...], b_ref[...]
        lane = lax.broadcasted_iota(jnp.int32, (SWA_HEADS, LANES), 1)
        out = jnp.zeros((SWA_HEADS, LANES), F32)
        for b in range(REL_BUCKETS):
            part = jnp.sum(jnp.where(bk == b, dbv, 0.0), axis=1)
            tot = jnp.sum(part, axis=-1, keepdims=True)
            out = out + jnp.where(lane == b, tot, 0.0)
        o_ref[...] = out

    return pl.pallas_call(
        body, name=name, out_shape=jax.ShapeDtypeStruct((SWA_HEADS, LANES), F32),
        compiler_params=pltpu.CompilerParams(vmem_limit_bytes=VMEM_LIMIT_BYTES),
    )(dbias, bucket)


def _adamw_update(w, g, m, v):
    m_new = ADAM_B1 * m + (1.0 - ADAM_B1) * g
    v_new = ADAM_B2 * v + (1.0 - ADAM_B2) * jnp.square(g)
    m_hat = m_new / (1.0 - ADAM_B1 ** ADAM_STEP)
    v_hat = v_new / (1.0 - ADAM_B2 ** ADAM_STEP)
    return -ADAM_LR * (m_hat / (jnp.sqrt(v_hat) + ADAM_EPS) + ADAM_WD * w), m_new, v_new


def _adamw(w, g, m, v, *, name):
    def body(w_ref, g_ref, m_ref, v_ref, d_ref, nm_ref, nv_ref):
        d_ref[...], nm_ref[...], nv_ref[...] = _adamw_update(w_ref[...], g_ref[...], m_ref[...], v_ref[...])

    return pl.pallas_call(
        body, name=name, out_shape=[jax.ShapeDtypeStruct(w.shape, F32)] * 3,
        compiler_params=pltpu.CompilerParams(vmem_limit_bytes=VMEM_LIMIT_BYTES),
    )(w, g, m, v)


ADAMW_PARTS_BYTES = 8 * 1024 * 1024


def _adamw_slots(w, parts, m, v, *, name):
    n0, R, C = w.shape
    tr = next((c for c in (512, 256, 128, 64, 32, 16, 8) if R % c == 0 and 4 * n0 * N_DEV * c * C <= ADAMW_PARTS_BYTES), R)

    def body(*refs):
        w_ref, p_refs, (m_ref, v_ref, g_ref, d_ref, nm_ref, nv_ref) = refs[0], refs[1:1 + n0], refs[1 + n0:]
        layer = pl.program_id(0)
        for l in range(n0):
            @pl.when(layer == l)
            def _(p_ref=p_refs[l]):
                g = p_ref[0].astype(F32)
                for j in range(1, N_DEV):
                    g = g + p_ref[j].astype(F32)
                g_ref[...] = g
                d_ref[...], nm_ref[...], nv_ref[...] = _adamw_update(w_ref[...], g, m_ref[...], v_ref[...])

    spec = pl.BlockSpec((None, tr, C), lambda l, i: (l, i, 0))
    part_spec = lambda own: pl.BlockSpec((N_DEV, tr, C), lambda l, i: (0, jnp.where(l == own, i, 0), 0))
    return pl.pallas_call(
        body, name=name, grid=(n0, R // tr),
        in_specs=[spec] + [part_spec(l) for l in range(n0)] + [spec, spec], out_specs=[spec] * 4,
        out_shape=[jax.ShapeDtypeStruct((n0, R, C), F32)] * 4, compiler_params=_cparams("arbitrary", "arbitrary"),
    )(w, *parts, m, v)


def _all_gather_hbm(blocks, *, name):
    n = len(blocks)

    def body(*refs):
        x_refs, out_refs = refs[:n], refs[n:2 * n]
        send_sems, recv_sems, local_sems = refs[2 * n:]
        x, y, c, _ = _mesh_place()
        me, sibling = (x, y, c), (x, y, 1 - c)
        chips = [(1 - x, y), (x, 1 - y), (1 - x, 1 - y)]

        def copy(w, k, blk, to, src=None):
            px, py, pc = blk
            slot = out_refs[w].at[4 * px + 2 * py + pc]
            return pltpu.make_async_remote_copy(
                src_ref=slot if src is None else src, dst_ref=slot,
                send_sem=send_sems.at[w, k], recv_sem=recv_sems.at[w, k], device_id=to, device_id_type=MESH_ID)

        mine = [pltpu.make_async_copy(x_refs[w], out_refs[w].at[4 * x + 2 * y + c], local_sems.at[w])
                for w in range(n)]
        for cp in mine:
            cp.start()
        first = []
        for w in range(n):
            first.append(copy(w, 0, me, sibling, src=x_refs[w]))
            first += [copy(w, 1 + j, me, (*chip, c), src=x_refs[w]) for j, chip in enumerate(chips)]
        for cp in first:
            cp.start()
        passed = []
        for j, chip in enumerate(chips):
            for w in range(n):
                copy(w, 1 + j, (*chip, c), me).wait_recv()
                fwd = copy(w, 4 + j, (*chip, c), sibling)
                fwd.start()
                passed.append(fwd)
        for w in range(n):
            copy(w, 0, sibling, me).wait_recv()
            for j, chip in enumerate(chips):
                copy(w, 4 + j, (*chip, 1 - c), me).wait_recv()
        for cp in first + passed:
            cp.wait_send()
        for cp in mine:
            cp.wait()

    return pl.pallas_call(
        body, name=name, out_shape=[jax.ShapeDtypeStruct((N_DEV,) + b.shape, b.dtype) for b in blocks],
        in_specs=[HBM_SPEC] * n, out_specs=[HBM_SPEC] * n,
        scratch_shapes=[pltpu.SemaphoreType.DMA((n, 7)), pltpu.SemaphoreType.DMA((n, 7)),
                        pltpu.SemaphoreType.DMA((n,))],
    )(*blocks)


def _all_reduce_small(block, *, name):
    R, W = block.shape

    def body(x_ref, out_ref, buf, send_sems, recv_sems):
        x, y, c, me = _mesh_place()
        copies = []
        for k, (peer, _) in enumerate(_peers(x, y, c)):
            copies.append(pltpu.make_async_remote_copy(
                src_ref=x_ref, dst_ref=buf.at[me], send_sem=send_sems.at[k], recv_sem=recv_sems.at[k],
                device_id=peer, device_id_type=MESH_ID))
        for cp in copies:
            cp.start()
        buf[me] = x_ref[...]
        for cp in copies:
            cp.wait_recv()
        for cp in copies:
            cp.wait_send()
        acc = buf[0]
        for j in range(1, N_DEV):
            acc = acc + buf[j]
        out_ref[...] = acc

    return pl.pallas_call(
        body, name=name, out_shape=jax.ShapeDtypeStruct((R, W), F32),
        in_specs=[VMEM_SPEC], out_specs=VMEM_SPEC,
        scratch_shapes=[pltpu.VMEM((N_DEV, R, W), F32), pltpu.SemaphoreType.DMA((7,)), pltpu.SemaphoreType.DMA((7,))],
    )(block)


def _assemble(name, g):
    if name == 'w_up':
        return g
    if BIG_AXIS[name] == 2:
        return jnp.concatenate([g[j] for j in range(N_DEV)], axis=1)
    return g.reshape(N_DEV * g.shape[1], g.shape[2])


def _split_for_devices(name, g):
    if g.ndim == 3:
        return g
    if BIG_AXIS[name] == 2:
        b = g.shape[1] // N_DEV
        return jnp.stack([g[:, j * b:(j + 1) * b] for j in range(N_DEV)]).astype(BF16)
    return g.reshape(N_DEV, g.shape[0] // N_DEV, g.shape[1]).astype(BF16)


def _layer_weight_keys(i):
    j = i // 2
    mixer = [('ev_w_in', j), ('ev_w_uq', j), ('ev_w_ukv', j), ('ev_w_out', j)] if i % 2 == 0 \
        else [('od_w_in', j), ('od_w_out', j)]
    return mixer + [('w_up', i), ('w_down', i), ('ple_w_proj', i), ('ple_w_gate', i)]


def _weight_layer(key):
    name, idx = key
    return 2 * idx if name.startswith('ev_') else 2 * idx + 1 if name.startswith('od_') else idx


FIRST_GATHER = [('ev_w_in', 0), ('ev_w_uq', 0), ('ev_w_ukv', 0)]
FWD_CARRIERS = {
    'l0_mla': [('ev_w_out', 0), ('w_up', 0), ('ple_w_proj', 0), ('ple_w_gate', 0)],
    'l0_swa': [('w_down', 0)],
    'l0_out_ln1': [('od_w_out', 0)],
    'l0_up': [('od_w_in', 0)],
    'l0_down_ln2': [('w_up', 1)],
    'l0_ple_gate': [('ple_w_proj', 1), ('ple_w_gate', 1)],
    'l1_fox': [('w_down', 1), ('ev_w_in', 1), ('ev_w_uq', 1), ('ev_w_ukv', 1), ('ev_w_out', 1), ('w_up', 2)],
    'l1_up': [('w_down', 2)],
    'l1_down_ln2': [('ple_w_proj', 2), ('ple_w_gate', 2)],
    'l2_mla': [('od_w_in', 1), ('od_w_out', 1)],
    'l2_swa': [('w_up', 3)],
    'l2_up': [('w_down', 3)],
    'l2_down_ln2': [('ple_w_proj', 3), ('ple_w_gate', 3)],
}


class _MeshExchange:
    def __init__(self, shards):
        self.shards = shards
        self.weights = {i: {} for i in range(DEPTH)}
        self.pending = []
        self.in_flight = []
        self.received = {}
        got = _all_gather_hbm([self.shards[n][idx] for n, idx in FIRST_GATHER], name="gather_first")
        self._landed(FIRST_GATHER, got)

    def _landed(self, keys, gathered):
        for k, g in zip(keys, gathered):
            self.weights[_weight_layer(k)][k[0]] = _assemble(k[0], g)

    def layer_weights(self, i):
        return self.weights[i]

    def carry(self, kernel_name):
        return [(("gather", idx), self.shards[n]) for n, idx in FWD_CARRIERS.get(kernel_name, [])]

    def carried(self, kernel_name, outs):
        self._landed(FWD_CARRIERS.get(kernel_name, []), outs)

    def push_grads(self, grads):
        self.pending += [(k, _split_for_devices(k[0], g)) for k, g in grads.items()]

    def bwd_items(self):
        self.in_flight, self.pending = self.pending, []
        return [("scatter", parts) for _, parts in self.in_flight]

    def bwd_done(self, outs):
        for (k, _), got in zip(self.in_flight, outs):
            self.received[k] = got
        self.in_flight = []

    def finish(self):
        if self.pending:
            outs = _exchange(self.bwd_items(), name="scatter_rest")
            self.bwd_done(outs)
        return self.received


PACK_ROWS = 8


def _pack_small(vals):
    flat = jnp.concatenate([vals[n].reshape(-1).astype(F32) for n in SMALL])
    pad = (-flat.shape[0]) % (PACK_ROWS * LANES)
    return jnp.pad(flat, (0, pad)).reshape(-1, LANES)


def _unpack_small(block, shapes):
    flat = block.reshape(-1)
    out, off = {}, 0
    for n in SMALL:
        sz = math.prod(shapes[n])
        out[n] = flat[off:off + sz].reshape(shapes[n])
        off += sz
    return out


def _rope_tables(S):
    half = MLA_ROPE // 2
    inv = 1.0 / (ROPE_THETA ** (jnp.arange(0, MLA_ROPE, 2, dtype=F32) / MLA_ROPE))
    ang = jnp.arange(S, dtype=F32)[:, None] * inv[None, :]
    cos, sin = jnp.cos(ang), jnp.sin(ang)
    zeros = jnp.zeros((S, half), F32)
    tail = jnp.zeros((S, LANES - MLA_QK), F32)

    def block(rope_part, nope_val):
        return jnp.concatenate([jnp.full((S, MLA_NOPE), nope_val, F32), rope_part, tail], -1)

    a_r = jnp.concatenate([cos, cos], -1)
    bm_r = jnp.concatenate([-sin, zeros], -1)
    bp_r = jnp.concatenate([zeros, sin], -1)
    q_tabs = tuple(block(r, v) for r, v in ((a_r, 1.0), (bm_r, 0.0), (bp_r, 0.0)))
    k_tabs = tuple(block(r, 0.0) for r in (a_r, bm_r, bp_r))
    return q_tabs, k_tabs


def _t5_bucket(dist):
    exact = REL_BUCKETS // 2
    d = jnp.maximum(dist, 1).astype(F32)
    large = exact + (jnp.log(d / exact) / math.log(REL_MAX_DIST / exact) * (REL_BUCKETS - exact)).astype(jnp.int32)
    large = jnp.minimum(large, REL_BUCKETS - 1)
    return jnp.where(dist < exact, dist, large)


def _swa_bucket_table():
    a = jnp.arange(BLOCK_Q)[:, None]
    col = jnp.arange(2 * BLOCK_Q)[None, :]
    return _t5_bucket(jnp.maximum(a + BLOCK_Q - col, 0)).astype(jnp.int32)


def _even_weights(W):
    w = W['ev_w_in']
    c_kv1 = MLA_Q_LORA + MLA_KV_LORA
    c_kr1 = c_kv1 + MLA_ROPE
    c_qs1 = c_kr1 + SWA_HEADS * HEAD_DIM
    zeros = lambda n: jnp.zeros((D_MODEL, n), w.dtype)
    w_in = jnp.concatenate([w[:, c_kr1:c_qs1], w[:, :c_kv1], w[:, c_qs1:], zeros(KR_LANE0), w[:, c_kv1:c_kr1],
                            zeros(LANES - KR_LANE0 - MLA_ROPE)], axis=1)
    uq = W['ev_w_uq'].reshape(MLA_Q_LORA, MLA_HEADS, MLA_QK)
    w_uq = jnp.pad(uq, ((0, 0), (0, 0), (0, LANES - MLA_QK))).reshape(MLA_Q_LORA, MLA_HEADS * LANES)
    ukv = W['ev_w_ukv'].reshape(MLA_KV_LORA, MLA_HEADS, MLA_NOPE + MLA_V)
    w_k = jnp.pad(ukv[..., :MLA_NOPE], ((0, 0), (0, 0), (0, LANES - MLA_NOPE))).reshape(MLA_KV_LORA, -1)
    w_v = ukv[..., MLA_NOPE:].reshape(MLA_KV_LORA, MLA_HEADS * MLA_V)
    return w_in, w_uq, w_k, w_v


def _even_in_grad_unpad(dw):
    kr0 = EV_KR[0] + KR_LANE0
    return jnp.concatenate([dw[:, EV_CQ[0]:EV_CKV[1]], dw[:, kr0:kr0 + MLA_ROPE], dw[:, EV_QS[0]:EV_QS[1]],
                            dw[:, EV_KS[0]:EV_VS[1]]], axis=1)


def _even_fwd(xb, W, P, i, B, S, tabs, xchg, tag):
    j = i // 2
    q_tabs, k_tabs, bias, sinkcol = tabs
    w_in, w_uq, w_k, w_v = _even_weights(W)
    h = _mm(xb, w_in, name=f"{tag}_in")
    cqn, ckvn, rq, rkv = _even_norms(h, P['ev_q_norm'][j][None], P['ev_kv_norm'][j][None], name=f"{tag}_norms")
    q = _rope(_mm(cqn, w_uq, name=f"{tag}_uq"), q_tabs, S, sign=1.0, name=f"{tag}_ropeq")
    knp = _mm(ckvn, w_k, out_dtypes=(BF16,), name=f"{tag}_uk")
    v = _mm(ckvn, w_v, out_dtypes=(BF16,), name=f"{tag}_uv")
    k = _mla_keys(knp, h, k_tabs, S, name=f"{tag}_keys")
    o_mla, lse_mla, got = _flash_fwd(q, k, v, q_blk0=0, k_blk0=0, v_blk0=0, W=2 * LANES, n_pairs=MLA_HEADS // 2,
                                     B=B, S=S, scale=MLA_QK ** -0.5, comm=xchg.carry(f"{tag}_mla"), name=f"{tag}_mla")
    xchg.carried(f"{tag}_mla", got)
    o_swa, lse_swa, got = _swa_fwd(h, bias, sinkcol, B=B, S=S, comm=xchg.carry(f"{tag}_swa"), name=f"{tag}_swa")
    xchg.carried(f"{tag}_swa", got)
    res = dict(h=h, cqn=cqn, ckvn=ckvn, rq=rq, rkv=rkv, q=q, k=k, v=v, o_mla=o_mla, lse_mla=lse_mla,
               o_swa=o_swa, lse_swa=lse_swa)
    return ((o_mla, o_swa), W['ev_w_out']), res


def _shift_prev(own, prev, B, S):
    prev = prev.reshape(B, S, LANES)
    shifted = jnp.concatenate([prev[:, BLOCK_Q:], jnp.zeros_like(prev[:, :BLOCK_Q])], axis=1)
    return (own + shifted.reshape(B * S, LANES)).astype(BF16)


def _even_bwd(dmb, dz1, xb, W, P, j, B, S, tabs, res, xchg, tag):
    q_tabs, k_tabs, bias, sinkcol = tabs
    (w_in, w_uq, w_k, w_v), w_out = _even_weights(W), W['ev_w_out']
    g = {}
    g['ev_w_out'] = jnp.concatenate([_mm_tn(res['o_mla'], dmb, name=f"{tag}_dwout_mla"),
                                     _mm_tn(res['o_swa'], dmb, name=f"{tag}_dwout_swa")], axis=0)
    do = _mm(dmb, w_out, trans_b=True, out_dtypes=(BF16,), name=f"{tag}_do")
    dq, dk, dv, got = _flash_bwd(res['q'], res['k'], res['v'], res['o_mla'], do, res['lse_mla'], q_blk0=0, k_blk0=0,
                                 v_blk0=0, do_blk0=0, W=2 * LANES, n_pairs=MLA_HEADS // 2, B=B, S=S,
                                 scale=MLA_QK ** -0.5, qk_dtype=F32, comm=xchg.bwd_items(), name=f"{tag}_mla_bwd")
    xchg.bwd_done(got)
    dq_pre = _rope(dq, q_tabs, S, sign=-1.0, name=f"{tag}_ropeq_bwd")
    dw_uq = _mm_tn(res['cqn'], dq_pre, name=f"{tag}_dwuq")
    g['ev_w_uq'] = dw_uq.reshape(MLA_Q_LORA, MLA_HEADS, LANES)[..., :MLA_QK].reshape(MLA_Q_LORA, MLA_HEADS * MLA_QK)
    dcqn = _mm(dq_pre, w_uq, trans_b=True, name=f"{tag}_dcqn")
    dw_k = _mm_tn(res['ckvn'], dk, name=f"{tag}_dwuk").reshape(MLA_KV_LORA, MLA_HEADS, LANES)[..., :MLA_NOPE]
    dw_v = _mm_tn(res['ckvn'], dv, name=f"{tag}_dwuv").reshape(MLA_KV_LORA, MLA_HEADS, MLA_V)
    g['ev_w_ukv'] = jnp.concatenate([dw_k, dw_v], axis=-1).reshape(MLA_KV_LORA, MLA_HEADS * (MLA_NOPE + MLA_V))
    dckvn_v = _mm(dv, w_v, trans_b=True, name=f"{tag}_dckvn_v")
    dckvn = _mm(dk, w_k, trans_b=True, extras=(dckvn_v,), epilogue=lambda acc, r: (acc + r,), name=f"{tag}_dckvn")
    dkr_pre = _mla_rope_key_grad(dk, k_tabs, S, name=f"{tag}_ropek_bwd")
    xchg.push_grads({(n, j): g.pop(n) for n in list(g)})
    dqs, dko, dkp, dvo, dvp, dbias, dsink, got = _swa_bwd(res['h'], res['o_swa'], do, res['lse_swa'], bias, sinkcol,
                                                          do_blk0=1, B=B, S=S, comm=xchg.bwd_items(),
                                                          name=f"{tag}_swa_bwd")
    xchg.bwd_done(got)
    dh, dgq, dgkv = _even_in_bwd(res['h'], res['rq'], res['rkv'], P['ev_q_norm'][j][None], P['ev_kv_norm'][j][None],
                                 dcqn, dckvn, dqs, _shift_prev(dko, dkp, B, S), _shift_prev(dvo, dvp, B, S), dkr_pre,
                                 name=f"{tag}_in_bwd")
    g['ev_w_in'] = _even_in_grad_unpad(_mm_tn(xb, dh, name=f"{tag}_dwin"))
    xchg.push_grads({(n, j): val for n, val in g.items()})
    dx_kwargs = dict(trans_b=True, extras=(dz1,), epilogue=lambda acc, r: (acc + DN_ALPHA * r,), name=f"{tag}_dx")
    dx = _scattering(xchg, _mm, dh, w_in, **dx_kwargs) if j == 0 else _mm(dh, w_in, **dx_kwargs)
    small = dict(ev_q_norm=dgq[0], ev_kv_norm=dgkv[0], dbias=dbias, ev_sinks=jnp.sum(dsink, axis=(1, 2)))
    return dx, small


def _odd_fwd(xb, W, P, i, B, S, xchg, tag):
    j = i // 2
    w = W['od_w_in']
    w_qkv = w[:, :ODD_QKV]
    w_f = jnp.pad(w[:, ODD_QKV:], ((0, 0), (0, LANES - FOX_HEADS)))
    bf = jnp.pad(P['od_b_f'][j], (0, LANES - FOX_HEADS))[None]
    qkv = _mm(xb, w_qkv, out_dtypes=(BF16,), name=f"{tag}_qkv")
    f = _mm(xb, w_f, name=f"{tag}_f").reshape(B, S, LANES)
    csh, chs = _fox_decay_fwd(f, bf, name=f"{tag}_decay")
    crow = chs[:, :FOX_HEADS].reshape(B, FOX_HEADS, S // ATT_TILE, 1, ATT_TILE)
    n_blk = FOX_HEADS * HEAD_DIM // LANES
    o, lse, got = _flash_fwd(qkv, qkv, qkv, q_blk0=0, k_blk0=n_blk, v_blk0=2 * n_blk, W=LANES,
                             n_pairs=FOX_HEADS // 2, B=B, S=S, scale=HEAD_DIM ** -0.5, csh=csh, crow=crow,
                             comm=xchg.carry(f"{tag}_fox"), name=f"{tag}_fox")
    xchg.carried(f"{tag}_fox", got)
    res = dict(f=f, bf=bf, csh=csh, crow=crow, qkv=qkv, o=o, lse=lse, w_qkv=w_qkv, w_f=w_f)
    return (o, W['od_w_out']), res


def _odd_bwd(dmb, dz1, xb, W, P, j, B, S, res, xchg, tag):
    g = {}
    w_out = W['od_w_out']
    g['od_w_out'] = _mm_tn(res['o'], dmb, name=f"{tag}_dwout")
    do = _mm(dmb, w_out, trans_b=True, out_dtypes=(BF16,), name=f"{tag}_do")
    qkv = res['qkv']
    n_blk = FOX_HEADS * HEAD_DIM // LANES
    dq, dk, dv, dck, dcq, got = _flash_bwd(qkv, qkv, qkv, res['o'], do, res['lse'], q_blk0=0, k_blk0=n_blk,
                                           v_blk0=2 * n_blk, do_blk0=0, W=LANES, n_pairs=FOX_HEADS // 2, B=B, S=S,
                                           scale=HEAD_DIM ** -0.5, qk_dtype=BF16, csh=res['csh'], crow=res['crow'],
                                           comm=xchg.bwd_items(), name=f"{tag}_fox_bwd")
    xchg.bwd_done(got)
    dc = dck.reshape(B, FOX_HEADS, S) + dcq.reshape(B, FOX_HEADS, S)
    dc_hs = jnp.pad(dc, ((0, 0), (0, LANES - FOX_HEADS), (0, 0)))
    df, dbf = _fox_decay_bwd(dc_hs, res['f'], res['bf'], name=f"{tag}_decay_bwd")
    df = df.reshape(B * S, LANES)
    dw_qkv = [_mm_tn(xb, t, name=f"{tag}_dw{n}") for n, t in (("q", dq), ("k", dk), ("v", dv))]
    dw_f = _mm_tn(xb, df, name=f"{tag}_dwf")
    g['od_w_in'] = jnp.concatenate(dw_qkv + [dw_f[:, :FOX_HEADS]], axis=1)
    dxf = _mm(df, res['w_f'], trans_b=True, extras=(dz1,), epilogue=lambda acc, r: (acc + DN_ALPHA * r,),
              name=f"{tag}_dxf")
    xchg.push_grads({(n, j): val for n, val in g.items()})
    dx = _mm((dq, dk, dv), res['w_qkv'], trans_b=True, extras=(dxf,), epilogue=lambda acc, r: (acc + r,),
             name=f"{tag}_dx")
    small = dict(od_b_f=dbf[0, :FOX_HEADS])
    return dx, small


def _carrying(xchg, name, call, *args, **kwargs):
    comm = xchg.carry(name)
    out = call(*args, comm=comm, name=name, **kwargs)
    if comm:
        out, got = out
        xchg.carried(name, got)
    return out


def _scattering(xchg, call, *args, **kwargs):
    comm = xchg.bwd_items()
    out = call(*args, comm=comm, **kwargs)
    if comm:
        out, got = out
        xchg.bwd_done(got)
    return out


def _local_step(x, p, target, P, xchg):
    B, S, D = x.shape
    T = B * S
    q_tabs, k_tabs = _rope_tables(S)
    bucket = _swa_bucket_table()
    in_bucket = (bucket[..., None] == jnp.arange(REL_BUCKETS)).astype(F32)
    bias = jnp.einsum('acb,bh->hac', in_bucket, P['rel_bias'], precision=lax.Precision.HIGHEST)

    xc = x.reshape(T, D)
    xcb = xc.astype(BF16)
    saved = []
    for i in range(DEPTH):
        j = i // 2
        tag = f"l{i}"
        W = xchg.layer_weights(i)
        lay = dict(xb=xcb, W=W)
        if i % 2 == 0:
            sinkcol = jnp.broadcast_to(P['ev_sinks'][j][:, None, None], (SWA_HEADS, BLOCK_Q, 1)).astype(F32)
            lay['tabs'] = (q_tabs, k_tabs, bias, sinkcol)
            (o, w_out), lay['mix'] = _even_fwd(xcb, W, P, i, B, S, lay['tabs'], xchg, tag)
        else:
            (o, w_out), lay['mix'] = _odd_fwd(xcb, W, P, i, B, S, xchg, tag)
        x1, x1b, lay['xh1'], lay['r1'] = _carrying(xchg, f"{tag}_out_ln1", _mm_ln, o, w_out, xc,
                                                   P['ln1_g'][i][None], P['ln1_b'][i][None])
        lay['x1b'] = x1b
        lay['u'], lay['a'] = _carrying(xchg, f"{tag}_up", _mm, x1b, W['w_up'], out_dtypes=(F32, BF16),
                                       epilogue=lambda acc: (acc, jnp.square(jnp.maximum(acc, 0.0))))
        x2, x2b, lay['xh2'], lay['r2'] = _carrying(xchg, f"{tag}_down_ln2", _mm_ln, lay['a'], W['w_down'], x1,
                                                   P['ln2_g'][i][None], P['ln2_b'][i][None])
        lay['x2b'] = x2b
        lay['p'] = p[i].reshape(T, D_PLE)
        lay['e'] = _mm(lay['p'], W['ple_w_proj'], name=f"{tag}_ple_proj")

        def gate(acc, bg, e, x2v):
            gv = 1.0 / (1.0 + jnp.exp(-(acc + bg)))
            y = x2v + gv * e
            return y, y, gv

        xc, xcb, lay['g'] = _carrying(xchg, f"{tag}_ple_gate", _mm, x2b, W['ple_w_gate'],
                                      extras=(P['ple_b_gate'][i][None], lay['e'], x2), epilogue=gate,
                                      out_dtypes=(F32, BF16, F32))
        saved.append(lay)

    dy, sq = _loss_grad(xc, target.reshape(T, D), name="loss")

    Gs = {n: [None] * DEPTH for n in ('ln1_g', 'ln1_b', 'ln2_g', 'ln2_b', 'ple_b_gate')}
    Gs.update({n: [None] * (DEPTH // 2) for n in ('ev_q_norm', 'ev_kv_norm', 'ev_sinks', 'od_b_f')})
    dbias_total = None
    for i in reversed(range(DEPTH)):
        j = i // 2
        tag = f"l{i}b"
        lay = saved[i]
        W = lay['W']
        de, dzg, dbg = _ple_bwd_elem(dy, lay['g'], lay['e'], name=f"{tag}_ple_elem")
        Gs['ple_b_gate'][i] = dbg[0]
        g_mlp = {('ple_w_proj', i): _mm_tn(lay['p'], de, slot_width=D_MODEL // N_DEV, name=f"{tag}_dwproj"),
                 ('ple_w_gate', i): _mm_tn(lay['x2b'], dzg, name=f"{tag}_dwgate")}
        dz2, dz2b, dg2, db2 = _mm_ln_bwd(dzg, W['ple_w_gate'], dy, 1.0, lay['xh2'], lay['r2'], P['ln2_g'][i][None],
                                         name=f"{tag}_dx2_ln2")
        Gs['ln2_g'][i], Gs['ln2_b'][i] = dg2[0], db2[0]
        g_mlp[('w_down', i)] = _mm_tn(lay['a'], dz2b, name=f"{tag}_dwdown")
        du = _mm(dz2b, W['w_down'], trans_b=True, extras=(lay['u'],), out_dtypes=(BF16,),
                 epilogue=lambda acc, u: (acc * (2.0 * jnp.maximum(u, 0.0)),), name=f"{tag}_du")
        g_mlp[('w_up', i)] = _mm_tn(lay['x1b'], du, slot_width=D_FF // N_DEV, name=f"{tag}_dwup")
        xchg.push_grads(g_mlp)
        dz1, dz1b, dg1, db1 = _mm_ln_bwd(du, W['w_up'], dz2, DN_ALPHA, lay['xh1'], lay['r1'], P['ln1_g'][i][None],
                                         name=f"{tag}_dx1_ln1")
        Gs['ln1_g'][i], Gs['ln1_b'][i] = dg1[0], db1[0]
        if i % 2 == 0:
            dy, small = _even_bwd(dz1b, dz1, lay['xb'], W, P, j, B, S, lay['tabs'], lay['mix'], xchg, tag)
            dbias_total = small['dbias'] if dbias_total is None else dbias_total + small['dbias']
            for n in ('ev_q_norm', 'ev_kv_norm', 'ev_sinks'):
                Gs[n][j] = small[n]
        else:
            dy, small = _odd_bwd(dz1b, dz1, lay['xb'], W, P, j, B, S, lay['mix'], xchg, tag)
            Gs['od_b_f'][j] = small['od_b_f']

    grads_small = {n: jnp.stack(v) for n, v in Gs.items()}
    drel = _bias_bucket_sum(dbias_total, bucket, name="rel_bias_grad")
    grads_small['rel_bias'] = drel[:, :REL_BUCKETS].T
    return sq, dy.reshape(B, S, D), grads_small


def kernel(x, p, rel_bias, ev_w_in, ev_q_norm, ev_w_uq, ev_kv_norm, ev_w_ukv, ev_sinks, ev_w_out, od_w_in, od_b_f, od_w_out, ln1_g, ln1_b, w_up, w_down, ln2_g, ln2_b, ple_w_proj, ple_w_gate, ple_b_gate, loss_target, m_rel_bias, m_ev_w_in, m_ev_q_norm, m_ev_w_uq, m_ev_kv_norm, m_ev_w_ukv, m_ev_sinks, m_ev_w_out, m_od_w_in, m_od_b_f, m_od_w_out, m_ln1_g, m_ln1_b, m_w_up, m_w_down, m_ln2_g, m_ln2_b, m_ple_w_proj, m_ple_w_gate, m_ple_b_gate, v_rel_bias, v_ev_w_in, v_ev_q_norm, v_ev_w_uq, v_ev_kv_norm, v_ev_w_ukv, v_ev_sinks, v_ev_w_out, v_od_w_in, v_od_b_f, v_od_w_out, v_ln1_g, v_ln1_b, v_w_up, v_w_down, v_ln2_g, v_ln2_b, v_ple_w_proj, v_ple_w_gate, v_ple_b_gate):
    given = dict(locals())
    w = {n: given[n] for n in WEIGHTS}
    mom = {n: given["m_" + n] for n in WEIGHTS}
    var = {n: given["v_" + n] for n in WEIGHTS}
    small_shapes = {n: w[n].shape for n in SMALL}

    xchg = _MeshExchange({n: w[n].astype(BF16) for n in BIG})
    P = {n: w[n] for n in SMALL}

    sq, grad_x, grads_small = _local_step(x, p, loss_target, P, xchg)
    loss = lax.psum(0.5 * jnp.sum(sq) / D_MODEL, ("x", "y", "c"))

    received = xchg.finish()
    g_small_packed = _all_reduce_small(_pack_small(grads_small), name="reduce_small_grads")
    g_small = _unpack_small(g_small_packed, small_shapes)

    grad, delta, new_m, new_v = {}, {}, {}, {}
    for n in BIG:
        parts = [received[(n, idx)] for idx in range(w[n].shape[0])]
        grad[n], delta[n], new_m[n], new_v[n] = _adamw_slots(w[n], parts, mom[n], var[n], name=f"adamw_{n}")
    d, nm, nv = _adamw(_pack_small(w), g_small_packed, _pack_small(mom), _pack_small(var), name="adamw_small")
    d, nm, nv = (_unpack_small(t, small_shapes) for t in (d, nm, nv))
    for n in SMALL:
        grad[n], delta[n], new_m[n], new_v[n] = g_small[n], d[n], nm[n], nv[n]

    return (loss, grad_x, *[grad[n] for n in WEIGHTS], *[delta[n] for n in WEIGHTS],
            *[new_m[n] for n in WEIGHTS], *[new_v[n] for n in WEIGHTS])
```
